```python
import jax, jax.numpy as jnp
from jax import lax
import numpy as np

D_MODEL = 2048
BATCH = 8
SEQ = 4096
DEPTH = 2

N_HEADS_MLA = 8
QK_NOPE_DIM = 128
QK_ROPE_DIM = 64
V_HEAD_DIM = 128
Q_LORA_RANK = 512
KV_LORA_RANK = 256
D_MLA = N_HEADS_MLA * V_HEAD_DIM
ROPE_THETA = 10000.0
Q_BLOCK = 128
POOL_WINDOWS = (2, 4, 8, 16)
N_POOL_GROUPS = 4
POOL_GROUP_DIM = 128
D_POOL = N_POOL_GROUPS * POOL_GROUP_DIM
N_CONV_HEADS = 4
CONV_HEAD_DIM = 128
D_CONV = N_CONV_HEADS * CONV_HEAD_DIM
CONV_WIDTH = 3
D_MIX = D_MLA + D_POOL + D_CONV
SPLIT_SIZES = (Q_LORA_RANK, KV_LORA_RANK, QK_ROPE_DIM, D_MLA, D_POOL, D_POOL, D_CONV, D_CONV, D_CONV, D_CONV)
D_IN_PROJ = sum(SPLIT_SIZES)
LN_EPS = 1e-5
RMS_EPS = 1e-6
DEEPNORM_ALPHA = (2 * DEPTH) ** 0.25
DEEPNORM_BETA = (8 * DEPTH) ** -0.25

kernel_name = "hybrid_mla_pool_shortconv_deepnorm"


def layernorm(x, g, b):
    xf = x.astype(jnp.float32)
    mu = jnp.mean(xf, axis=-1, keepdims=True)
    var = jnp.mean(jnp.square(xf - mu), axis=-1, keepdims=True)
    y = (xf - mu) * lax.rsqrt(var + LN_EPS) * g.astype(jnp.float32) + b.astype(jnp.float32)
    return y.astype(x.dtype)


def rmsnorm(x, g):
    xf = x.astype(jnp.float32)
    y = xf * lax.rsqrt(jnp.mean(jnp.square(xf), axis=-1, keepdims=True) + RMS_EPS) * g.astype(jnp.float32)
    return y.astype(x.dtype)


def apply_rope(x, cos, sin):
    half = QK_ROPE_DIM // 2
    xf = x.astype(jnp.float32)
    x1, x2 = xf[..., :half], xf[..., half:]
    return jnp.concatenate([x1 * cos - x2 * sin, x2 * cos + x1 * sin], axis=-1).astype(x.dtype)


def mla_mixer(q_lat, kv_lat, k_rope, positions, q_norm_g, kv_norm_g, w_uq, w_ukv):
    Bn, S, _ = q_lat.shape
    H = N_HEADS_MLA
    q = (rmsnorm(q_lat, q_norm_g) @ w_uq).reshape(Bn, S, H, QK_NOPE_DIM + QK_ROPE_DIM)
    q_nope, q_rope = q[..., :QK_NOPE_DIM], q[..., QK_NOPE_DIM:]
    kv = (rmsnorm(kv_lat, kv_norm_g) @ w_ukv).reshape(Bn, S, H, QK_NOPE_DIM + V_HEAD_DIM)
    k_nope, v = kv[..., :QK_NOPE_DIM], kv[..., QK_NOPE_DIM:]
    half = QK_ROPE_DIM // 2
    inv_freq = ROPE_THETA ** (-jnp.arange(half, dtype=jnp.float32) / half)
    ang = positions.astype(jnp.float32)[..., None] * inv_freq
    cos, sin = jnp.cos(ang), jnp.sin(ang)
    q_rope = apply_rope(q_rope, cos[:, :, None, :], sin[:, :, None, :])
    k_rope = apply_rope(k_rope, cos, sin)
    scale = (QK_NOPE_DIM + QK_ROPE_DIM) ** -0.5
    nb = S // Q_BLOCK
    qn_blocks = q_nope.reshape(Bn, nb, Q_BLOCK, H, QK_NOPE_DIM).transpose(1, 0, 2, 3, 4)
    qr_blocks = q_rope.reshape(Bn, nb, Q_BLOCK, H, QK_ROPE_DIM).transpose(1, 0, 2, 3, 4)
    key_idx = jnp.arange(S)

    def attend(args):
        qn, qr, blk = args
        s = (jnp.einsum('bqhd,bkhd->bhqk', qn, k_nope).astype(jnp.float32)
             + jnp.einsum('bqhr,bkr->bhqk', qr, k_rope).astype(jnp.float32)) * scale
        q_idx = blk * Q_BLOCK + jnp.arange(Q_BLOCK)
        causal = key_idx[None, :] <= q_idx[:, None]
        s = jnp.where(causal[None, None], s, -jnp.inf)
        p = jax.nn.softmax(s, axis=-1).astype(v.dtype)
        return jnp.einsum('bhqk,bkhd->bqhd', p, v)

    out = lax.map(attend, (qn_blocks, qr_blocks, jnp.arange(nb)))
    return out.transpose(1, 0, 2, 3, 4).reshape(Bn, S, D_MLA)


def pool_mixer(h, w_pool, pool_scale):
    Bn, S, _ = h.shape
    hg = h.reshape(Bn, S, N_POOL_GROUPS, POOL_GROUP_DIM).astype(jnp.float32)
    cs = jnp.cumsum(hg, axis=1)
    t1 = jnp.arange(1, S + 1, dtype=jnp.float32)
    means = []
    for g, w in enumerate(POOL_WINDOWS):
        c = cs[:, :, g]
        lag = jnp.pad(c, ((0, 0), (w, 0), (0, 0)))[:, :S]
        means.append((c - lag) / jnp.minimum(t1, float(w))[None, :, None])
    pooled = (jnp.stack(means, axis=2) - hg).astype(h.dtype)
    y = jnp.einsum('bsgc,gcd->bsgd', pooled, w_pool).reshape(Bn, S, D_POOL)
    return y * pool_scale


def conv_mixer(h, b_gate, c_gate, conv_w):
    u = c_gate * h
    y = lax.conv_general_dilated(u, conv_w[:, None, :], window_strides=(1,),
                                 padding=((CONV_WIDTH - 1, 0),),
                                 dimension_numbers=('NWC', 'WIO', 'NWC'),
                                 feature_group_count=D_CONV)
    return b_gate * y


def hybrid_layer(x, positions, w_in, q_norm_g, kv_norm_g, w_uq, w_ukv, w_pool, pool_scale,
                 conv_w, w_out, b_out, ln_g, ln_b):
    split_idx = np.cumsum(SPLIT_SIZES)[:-1].tolist()
    proj = x @ w_in
    (q_lat, kv_lat, k_rope, g_mla, p_in, g_pool, c_h, c_b, c_c, g_conv) = jnp.split(proj, split_idx, axis=-1)
    y_mla = mla_mixer(q_lat, kv_lat, k_rope, positions, q_norm_g, kv_norm_g, w_uq, w_ukv) * jax.nn.silu(g_mla)
    y_pool = pool_mixer(p_in, w_pool, pool_scale) * jax.nn.silu(g_pool)
    y_conv = conv_mixer(c_h, c_b, c_c, conv_w) * jax.nn.silu(g_conv)
    mix = jnp.concatenate([y_mla, y_pool, y_conv], axis=-1)
    out = mix @ w_out + b_out
    return layernorm(DEEPNORM_ALPHA * x + out, ln_g, ln_b)


def _fwd_setup_inputs(seed: int = 0) -> dict:
    key = jax.random.key(seed)
    ks = jax.random.split(key, 16)
    f32 = jnp.float32
    nrm = lambda k, shape, s: jax.random.normal(k, shape, f32) * s
    x = jax.random.normal(ks[0], (BATCH, SEQ, D_MODEL), f32)
    positions = jnp.broadcast_to(jnp.arange(SEQ, dtype=jnp.int32), (BATCH, SEQ))
    return {
        "x": x,
        "positions": positions,
        "emb_ln_g": 1.0 + nrm(ks[1], (D_MODEL,), 0.01),
        "emb_ln_b": nrm(ks[2], (D_MODEL,), 0.01),
        "w_in": nrm(ks[3], (DEPTH, D_MODEL, D_IN_PROJ), D_MODEL ** -0.5),
        "q_norm_g": 1.0 + nrm(ks[4], (DEPTH, Q_LORA_RANK), 0.01),
        "kv_norm_g": 1.0 + nrm(ks[5], (DEPTH, KV_LORA_RANK), 0.01),
        "w_uq": nrm(ks[6], (DEPTH, Q_LORA_RANK, N_HEADS_MLA * (QK_NOPE_DIM + QK_ROPE_DIM)), Q_LORA_RANK ** -0.5),
        "w_ukv": nrm(ks[7], (DEPTH, KV_LORA_RANK, N_HEADS_MLA * (QK_NOPE_DIM + V_HEAD_DIM)), KV_LORA_RANK ** -0.5),
        "w_pool": nrm(ks[8], (DEPTH, N_POOL_GROUPS, POOL_GROUP_DIM, POOL_GROUP_DIM), POOL_GROUP_DIM ** -0.5),
        "pool_scale": 1.0 + nrm(ks[9], (DEPTH, D_POOL), 0.1),
        "conv_w": nrm(ks[10], (DEPTH, CONV_WIDTH, D_CONV), CONV_WIDTH ** -0.5),
        "w_out": nrm(ks[11], (DEPTH, D_MIX, D_MODEL), DEEPNORM_BETA * D_MIX ** -0.5),
        "b_out": nrm(ks[12], (DEPTH, D_MODEL), 0.01),
        "ln_g": 1.0 + nrm(ks[13], (DEPTH, D_MODEL), 0.01),
        "ln_b": nrm(ks[14], (DEPTH, D_MODEL), 0.01),
    }


def _fwd_reference(x, positions, emb_ln_g, emb_ln_b, w_in, q_norm_g, kv_norm_g, w_uq, w_ukv, w_pool,
              pool_scale, conv_w, w_out, b_out, ln_g, ln_b):
    h = layernorm(x, emb_ln_g, emb_ln_b)
    for l in range(DEPTH):
        h = hybrid_layer(h, positions, w_in[l], q_norm_g[l], kv_norm_g[l], w_uq[l], w_ukv[l], w_pool[l],
                         pool_scale[l], conv_w[l], w_out[l], b_out[l], ln_g[l], ln_b[l])
    return h


import jax as _jax
import jax.numpy as _jnp

TWIN_FORMAT = 'train_step'
FWD_PARAMS = ['x', 'positions', 'emb_ln_g', 'emb_ln_b', 'w_in', 'q_norm_g', 'kv_norm_g', 'w_uq', 'w_ukv', 'w_pool', 'pool_scale', 'conv_w', 'w_out', 'b_out', 'ln_g', 'ln_b']
TWIN_WEIGHTS = ['emb_ln_g', 'emb_ln_b', 'w_in', 'q_norm_g', 'kv_norm_g', 'w_uq', 'w_ukv', 'w_pool', 'pool_scale', 'conv_w', 'w_out', 'b_out', 'ln_g', 'ln_b']
TWIN_DIFF_INPUT = 'x'
TWIN_INPUTS = ['x', 'positions', 'emb_ln_g', 'emb_ln_b', 'w_in', 'q_norm_g', 'kv_norm_g', 'w_uq', 'w_ukv', 'w_pool', 'pool_scale', 'conv_w', 'w_out', 'b_out', 'ln_g', 'ln_b', 'loss_target', 'm_emb_ln_g', 'm_emb_ln_b', 'm_w_in', 'm_q_norm_g', 'm_kv_norm_g', 'm_w_uq', 'm_w_ukv', 'm_w_pool', 'm_pool_scale', 'm_conv_w', 'm_w_out', 'm_b_out', 'm_ln_g', 'm_ln_b', 'v_emb_ln_g', 'v_emb_ln_b', 'v_w_in', 'v_q_norm_g', 'v_kv_norm_g', 'v_w_uq', 'v_w_ukv', 'v_w_pool', 'v_pool_scale', 'v_conv_w', 'v_w_out', 'v_b_out', 'v_ln_g', 'v_ln_b']
TWIN_OUTPUTS = ['loss', 'grad_x', 'grad_emb_ln_g', 'grad_emb_ln_b', 'grad_w_in', 'grad_q_norm_g', 'grad_kv_norm_g', 'grad_w_uq', 'grad_w_ukv', 'grad_w_pool', 'grad_pool_scale', 'grad_conv_w', 'grad_w_out', 'grad_b_out', 'grad_ln_g', 'grad_ln_b', 'delta_emb_ln_g', 'delta_emb_ln_b', 'delta_w_in', 'delta_q_norm_g', 'delta_kv_norm_g', 'delta_w_uq', 'delta_w_ukv', 'delta_w_pool', 'delta_pool_scale', 'delta_conv_w', 'delta_w_out', 'delta_b_out', 'delta_ln_g', 'delta_ln_b', 'new_m_emb_ln_g', 'new_m_emb_ln_b', 'new_m_w_in', 'new_m_q_norm_g', 'new_m_kv_norm_g', 'new_m_w_uq', 'new_m_w_ukv', 'new_m_w_pool', 'new_m_pool_scale', 'new_m_conv_w', 'new_m_w_out', 'new_m_b_out', 'new_m_ln_g', 'new_m_ln_b', 'new_v_emb_ln_g', 'new_v_emb_ln_b', 'new_v_w_in', 'new_v_q_norm_g', 'new_v_kv_norm_g', 'new_v_w_uq', 'new_v_w_ukv', 'new_v_w_pool', 'new_v_pool_scale', 'new_v_conv_w', 'new_v_w_out', 'new_v_b_out', 'new_v_ln_g', 'new_v_ln_b']
TWIN_LEAF_KINDS = {'loss': 'loss', 'grad_x': 'grad_x', 'grad_emb_ln_g': 'grad_w', 'grad_emb_ln_b': 'grad_w', 'grad_w_in': 'grad_w', 'grad_q_norm_g': 'grad_w', 'grad_kv_norm_g': 'grad_w', 'grad_w_uq': 'grad_w', 'grad_w_ukv': 'grad_w', 'grad_w_pool': 'grad_w', 'grad_pool_scale': 'grad_w', 'grad_conv_w': 'grad_w', 'grad_w_out': 'grad_w', 'grad_b_out': 'grad_w', 'grad_ln_g': 'grad_w', 'grad_ln_b': 'grad_w', 'delta_emb_ln_g': 'delta_w', 'delta_emb_ln_b': 'delta_w', 'delta_w_in': 'delta_w', 'delta_q_norm_g': 'delta_w', 'delta_kv_norm_g': 'delta_w', 'delta_w_uq': 'delta_w', 'delta_w_ukv': 'delta_w', 'delta_w_pool': 'delta_w', 'delta_pool_scale': 'delta_w', 'delta_conv_w': 'delta_w', 'delta_w_out': 'delta_w', 'delta_b_out': 'delta_w', 'delta_ln_g': 'delta_w', 'delta_ln_b': 'delta_w', 'new_m_emb_ln_g': 'new_m', 'new_m_emb_ln_b': 'new_m', 'new_m_w_in': 'new_m', 'new_m_q_norm_g': 'new_m', 'new_m_kv_norm_g': 'new_m', 'new_m_w_uq': 'new_m', 'new_m_w_ukv': 'new_m', 'new_m_w_pool': 'new_m', 'new_m_pool_scale': 'new_m', 'new_m_conv_w': 'new_m', 'new_m_w_out': 'new_m', 'new_m_b_out': 'new_m', 'new_m_ln_g': 'new_m', 'new_m_ln_b': 'new_m', 'new_v_emb_ln_g': 'new_v', 'new_v_emb_ln_b': 'new_v', 'new_v_w_in': 'new_v', 'new_v_q_norm_g': 'new_v', 'new_v_kv_norm_g': 'new_v', 'new_v_w_uq': 'new_v', 'new_v_w_ukv': 'new_v', 'new_v_w_pool': 'new_v', 'new_v_pool_scale': 'new_v', 'new_v_conv_w': 'new_v', 'new_v_w_out': 'new_v', 'new_v_b_out': 'new_v', 'new_v_ln_g': 'new_v', 'new_v_ln_b': 'new_v'}


def _forward(args):
    return _fwd_reference(*[args[k] for k in FWD_PARAMS])


def _output_shape():
    def fwd():
        inp = _fwd_setup_inputs(0)
        return _fwd_reference(*[inp[k] for k in FWD_PARAMS])
    out = _jax.eval_shape(fwd)
    return out.shape, out.dtype

N_MICROBATCH = 1
ADAM_LR = 0.001
ADAM_B1 = 0.9
ADAM_B2 = 0.999
ADAM_EPS = 1e-08
ADAM_WD = 0.01
ADAM_STEP = 10
PER_EXAMPLE_BATCH_AXIS = {'x': 0, 'positions': 0, 'loss_target': 0}
SHARED_INPUTS = []
_WEIGHT_DTYPES = {'emb_ln_g': _jnp.float32, 'emb_ln_b': _jnp.float32, 'w_in': _jnp.float32, 'q_norm_g': _jnp.float32, 'kv_norm_g': _jnp.float32, 'w_uq': _jnp.float32, 'w_ukv': _jnp.float32, 'w_pool': _jnp.float32, 'pool_scale': _jnp.float32, 'conv_w': _jnp.float32, 'w_out': _jnp.float32, 'b_out': _jnp.float32, 'ln_g': _jnp.float32, 'ln_b': _jnp.float32}
MOMENT_SCALE = {'emb_ln_g': 2.622708e-01, 'emb_ln_b': 1.482202e-01, 'w_in': 1.455915e-02, 'q_norm_g': 4.815075e-03, 'kv_norm_g': 9.864978e-03, 'w_uq': 2.778057e-03, 'w_ukv': 3.499290e-03, 'w_pool': 1.661451e-02, 'pool_scale': 1.637547e-02, 'conv_w': 1.988607e-02, 'w_out': 2.533438e-02, 'b_out': 1.022201e-01, 'ln_g': 1.130577e+01, 'ln_b': 2.262855e-01}


def _to_microbatches(a, axis):
    t = _jnp.moveaxis(a, axis, 0)
    t = t.reshape((N_MICROBATCH, t.shape[0] // N_MICROBATCH) + t.shape[1:])
    return _jnp.moveaxis(t, 1, axis + 1)


def setup_inputs(seed: int = 0) -> dict:
    inp = _fwd_setup_inputs(seed)
    key = _jax.random.fold_in(_jax.random.key(seed), 7919)
    shape, _ = _output_shape()
    out = dict(inp)
    out["loss_target"] = _jax.random.normal(_jax.random.fold_in(key, 0), shape, _jnp.float32)
    for i, name in enumerate(TWIN_WEIGHTS):
        w = inp[name].astype(_jnp.float32)
        if MOMENT_SCALE is None:
            s = _jnp.sqrt(_jnp.mean(_jnp.square(w)) + 1e-30)
        else:
            s = MOMENT_SCALE[name]
        km, kv = _jax.random.split(_jax.random.fold_in(key, i + 1))
        out[name] = w
        out["m_" + name] = s * _jax.random.normal(km, w.shape, _jnp.float32)
        out["v_" + name] = (s * s) * _jax.random.uniform(kv, w.shape, _jnp.float32, 0.5, 1.5)
    if N_MICROBATCH > 1:
        for name, axis in PER_EXAMPLE_BATCH_AXIS.items():
            out[name] = _to_microbatches(out[name], axis)
    return {'x': out['x'], 'positions': out['positions'], 'emb_ln_g': out['emb_ln_g'], 'emb_ln_b': out['emb_ln_b'], 'w_in': out['w_in'], 'q_norm_g': out['q_norm_g'], 'kv_norm_g': out['kv_norm_g'], 'w_uq': out['w_uq'], 'w_ukv': out['w_ukv'], 'w_pool': out['w_pool'], 'pool_scale': out['pool_scale'], 'conv_w': out['conv_w'], 'w_out': out['w_out'], 'b_out': out['b_out'], 'ln_g': out['ln_g'], 'ln_b': out['ln_b'], 'loss_target': out['loss_target'], 'm_emb_ln_g': out['m_emb_ln_g'], 'm_emb_ln_b': out['m_emb_ln_b'], 'm_w_in': out['m_w_in'], 'm_q_norm_g': out['m_q_norm_g'], 'm_kv_norm_g': out['m_kv_norm_g'], 'm_w_uq': out['m_w_uq'], 'm_w_ukv': out['m_w_ukv'], 'm_w_pool': out['m_w_pool'], 'm_pool_scale': out['m_pool_scale'], 'm_conv_w': out['m_conv_w'], 'm_w_out': out['m_w_out'], 'm_b_out': out['m_b_out'], 'm_ln_g': out['m_ln_g'], 'm_ln_b': out['m_ln_b'], 'v_emb_ln_g': out['v_emb_ln_g'], 'v_emb_ln_b': out['v_emb_ln_b'], 'v_w_in': out['v_w_in'], 'v_q_norm_g': out['v_q_norm_g'], 'v_kv_norm_g': out['v_kv_norm_g'], 'v_w_uq': out['v_w_uq'], 'v_w_ukv': out['v_w_ukv'], 'v_w_pool': out['v_w_pool'], 'v_pool_scale': out['v_pool_scale'], 'v_conv_w': out['v_conv_w'], 'v_w_out': out['v_w_out'], 'v_b_out': out['v_b_out'], 'v_ln_g': out['v_ln_g'], 'v_ln_b': out['v_ln_b']}


def _loss(weights, diff, rest, loss_target):
    with _jax.named_scope("forward"):
        args = {**rest, TWIN_DIFF_INPUT: diff, **{k: w.astype(_WEIGHT_DTYPES[k]) for k, w in weights.items()}}
        y = _forward(args)
    with _jax.named_scope("loss_head"):
        err = _jnp.square(y.astype(_jnp.float32) - loss_target)
        return 0.5 * _jnp.sum(_jnp.mean(err, axis=-1)) if err.ndim else 0.5 * err


def _adamw(w, g, m, v):
    m = ADAM_B1 * m + (1.0 - ADAM_B1) * g
    v = ADAM_B2 * v + (1.0 - ADAM_B2) * _jnp.square(g)
    m_hat = m / (1.0 - ADAM_B1 ** ADAM_STEP)
    v_hat = v / (1.0 - ADAM_B2 ** ADAM_STEP)
    delta = -ADAM_LR * (m_hat / (_jnp.sqrt(v_hat) + ADAM_EPS) + ADAM_WD * w)
    return delta, m, v


def reference(x, positions, emb_ln_g, emb_ln_b, w_in, q_norm_g, kv_norm_g, w_uq, w_ukv, w_pool, pool_scale, conv_w, w_out, b_out, ln_g, ln_b, loss_target, m_emb_ln_g, m_emb_ln_b, m_w_in, m_q_norm_g, m_kv_norm_g, m_w_uq, m_w_ukv, m_w_pool, m_pool_scale, m_conv_w, m_w_out, m_b_out, m_ln_g, m_ln_b, v_emb_ln_g, v_emb_ln_b, v_w_in, v_q_norm_g, v_kv_norm_g, v_w_uq, v_w_ukv, v_w_pool, v_pool_scale, v_conv_w, v_w_out, v_b_out, v_ln_g, v_ln_b):
    given = dict(x=x, positions=positions, emb_ln_g=emb_ln_g, emb_ln_b=emb_ln_b, w_in=w_in, q_norm_g=q_norm_g, kv_norm_g=kv_norm_g, w_uq=w_uq, w_ukv=w_ukv, w_pool=w_pool, pool_scale=pool_scale, conv_w=conv_w, w_out=w_out, b_out=b_out, ln_g=ln_g, ln_b=ln_b, loss_target=loss_target, m_emb_ln_g=m_emb_ln_g, m_emb_ln_b=m_emb_ln_b, m_w_in=m_w_in, m_q_norm_g=m_q_norm_g, m_kv_norm_g=m_kv_norm_g, m_w_uq=m_w_uq, m_w_ukv=m_w_ukv, m_w_pool=m_w_pool, m_pool_scale=m_pool_scale, m_conv_w=m_conv_w, m_w_out=m_w_out, m_b_out=m_b_out, m_ln_g=m_ln_g, m_ln_b=m_ln_b, v_emb_ln_g=v_emb_ln_g, v_emb_ln_b=v_emb_ln_b, v_w_in=v_w_in, v_q_norm_g=v_q_norm_g, v_kv_norm_g=v_kv_norm_g, v_w_uq=v_w_uq, v_w_ukv=v_w_ukv, v_w_pool=v_w_pool, v_pool_scale=v_pool_scale, v_conv_w=v_conv_w, v_w_out=v_w_out, v_b_out=v_b_out, v_ln_g=v_ln_g, v_ln_b=v_ln_b)
    weights = {n: given[n] for n in TWIN_WEIGHTS}
    shared = {n: given[n] for n in SHARED_INPUTS}
    per_example = {n: given[n] for n in ['x', 'positions']}
    grad_fn = _jax.value_and_grad(_loss, argnums=(0, 1))

    def one_microbatch(ex, loss_target):
        ex = dict(ex)
        diff = ex.pop(TWIN_DIFF_INPUT)
        return grad_fn(weights, diff, {**shared, **ex}, loss_target)

    if N_MICROBATCH == 1:
        loss, (grad_w, grad_x) = one_microbatch(per_example, given["loss_target"])
    else:
        def body(carry, xs):
            loss_sum, grad_sum = carry
            l_k, (gw_k, gx_k) = one_microbatch(xs[0], xs[1])
            with _jax.named_scope("update"):
                return (loss_sum + l_k, _jax.tree.map(_jnp.add, grad_sum, gw_k)), gx_k

        init = (_jnp.zeros((), _jnp.float32), _jax.tree.map(_jnp.zeros_like, weights))
        (loss, grad_w), grad_x = _jax.lax.scan(body, init, (per_example, given["loss_target"]))
    with _jax.named_scope("update"):
        delta_w, new_m, new_v = {}, {}, {}
        for n in TWIN_WEIGHTS:
            delta_w[n], new_m[n], new_v[n] = _adamw(weights[n], grad_w[n], given["m_" + n], given["v_" + n])
    return (loss, grad_x, *[grad_w[n] for n in TWIN_WEIGHTS], *[delta_w[n] for n in TWIN_WEIGHTS],
            *[new_m[n] for n in TWIN_WEIGHTS], *[new_v[n] for n in TWIN_WEIGHTS])
```

```python
import functools

import jax
import jax.numpy as jnp
from jax import lax
from jax.experimental import pallas as pl
from jax.experimental.pallas import tpu as pltpu

F32 = jnp.float32
BF16 = jnp.bfloat16
MESH = pl.DeviceIdType.MESH

D_MODEL = 2048
DEPTH = 2
N_HEADS = 8
NOPE = 128
ROPE = 64
Q_LORA = 512
KV_LORA = 256
D_MLA = 1024
D_POOL = 512
D_CONV = 512
POOL_WINDOWS = (2, 4, 8, 16)
D_IN_PROJ = 4928
LN_EPS = 1e-5
RMS_EPS = 1e-6
ROPE_THETA = 10000.0
ALPHA = (2 * DEPTH) ** 0.25
SCALE = (NOPE + ROPE) ** -0.5
ADAM_LR = 0.001
ADAM_B1 = 0.9
ADAM_B2 = 0.999
ADAM_EPS = 1e-08
ADAM_WD = 0.01
ADAM_STEP = 10

NP = 5120
HALO = 16
LANE = 128
N_CHIPS = 4
N_DEV = 8

NN = (((1,), (0,)), ((), ()))
NT = (((1,), (1,)), ((), ()))
TN = (((0,), (0,)), ((), ()))


def _pcall(kern, *, name, out_shape, grid=None, in_specs=None, out_specs=None, scratch=(), dims=None,
           vmem_mb=None, **kw):
    cp = {}
    if dims is not None:
        cp["dimension_semantics"] = dims
    if vmem_mb is not None:
        cp["vmem_limit_bytes"] = vmem_mb << 20
    args = dict(name=name, out_shape=out_shape, scratch_shapes=list(scratch),
                compiler_params=pltpu.CompilerParams(**cp))
    if grid is not None:
        args["grid"] = grid
    if in_specs is not None:
        args["in_specs"] = in_specs
    if out_specs is not None:
        args["out_specs"] = out_specs
    args.update(kw)
    return pl.pallas_call(kern, **args)


def _sigmoid(g):
    return 1.0 / (1.0 + jnp.exp(-g))


def _silu_and_grad(g):
    sig = _sigmoid(g)
    return g * sig, sig * (1.0 + g * (1.0 - sig))


def _matmul(a, b, mode, *, name, tm, tn, tk, out_dtype=F32, add=None, add_scale=1.0, vmem_mb=48):
    if mode == "nn":
        (M, K), N = a.shape, b.shape[1]
    elif mode == "nt":
        (M, K), N = a.shape, b.shape[0]
    else:
        (K, M), N = a.shape, b.shape[1]
    tm, tn, tk = min(tm, M), min(tn, N), min(tk, K)
    assert M % tm == 0 and N % tn == 0 and K % tk == 0, (name, M, N, K)
    nk = K // tk
    dn = {"nn": NN, "nt": NT, "tn": TN}[mode]
    if mode == "tn":
        a_spec = pl.BlockSpec((tk, tm), lambda i, j, k: (k, i))
    else:
        a_spec = pl.BlockSpec((tm, tk), lambda i, j, k: (i, k))
    if mode == "nt":
        b_spec = pl.BlockSpec((tn, tk), lambda i, j, k: (j, k))
    else:
        b_spec = pl.BlockSpec((tk, tn), lambda i, j, k: (k, j))
    o_spec = pl.BlockSpec((tm, tn), lambda i, j, k: (i, j))
    in_specs = [a_spec, b_spec] + ([o_spec] if add is not None else [])
    has_add = add is not None

    def kern(*refs):
        if has_add:
            a_ref, b_ref, add_ref, o_ref = refs[:4]
            rest = refs[4:]
        else:
            a_ref, b_ref, o_ref = refs[:3]
            add_ref = None
            rest = refs[3:]
        part = lax.dot_general(a_ref[...].astype(BF16), b_ref[...].astype(BF16), dn,
                               preferred_element_type=F32)

        def finish(acc):
            if has_add:
                acc = add_scale * add_ref[...] + acc
            o_ref[...] = acc.astype(out_dtype)

        if nk == 1:
            finish(part)
        else:
            acc_ref = rest[0]
            k = pl.program_id(2)

            @pl.when(k == 0)
            def _():
                acc_ref[...] = part

            @pl.when(k > 0)
            def _():
                acc_ref[...] += part

            @pl.when(k == nk - 1)
            def _():
                finish(acc_ref[...])

    scratch = [pltpu.VMEM((tm, tn), F32)] if nk > 1 else []
    args = (a, b) + ((add,) if has_add else ())
    return _pcall(kern, name=name, out_shape=jax.ShapeDtypeStruct((M, N), out_dtype),
                  grid=(M // tm, N // tn, nk), in_specs=in_specs, out_specs=o_spec, scratch=scratch,
                  dims=("parallel", "parallel", "arbitrary"), vmem_mb=vmem_mb)(*args)


def _ln_fwd(x, g, b, *, name):
    S, Dm = x.shape
    tm = min(512, S)

    def kern(x_ref, g_ref, b_ref, y_ref):
        xf = x_ref[...]
        mu = jnp.mean(xf, axis=-1, keepdims=True)
        xc = xf - mu
        var = jnp.mean(xc * xc, axis=-1, keepdims=True)
        y_ref[...] = xc * lax.rsqrt(var + LN_EPS) * g_ref[...] + b_ref[...]

    row = pl.BlockSpec((tm, Dm), lambda i: (i, 0))
    vec = pl.BlockSpec((1, Dm), lambda i: (0, 0))
    return _pcall(kern, name=name, out_shape=jax.ShapeDtypeStruct((S, Dm), F32), grid=(S // tm,),
                  in_specs=[row, vec, vec], out_specs=row, dims=("parallel",), vmem_mb=48)(
                      x, g.reshape(1, Dm), b.reshape(1, Dm))


def _ln_bwd(dy, r, g, *, name):
    S, Dm = r.shape
    tm = min(512, S)

    def kern(dy_ref, r_ref, g_ref, dr_ref, dg_ref, db_ref, ds_ref):
        @pl.when(pl.program_id(0) == 0)
        def _():
            dg_ref[...] = jnp.zeros_like(dg_ref)
            db_ref[...] = jnp.zeros_like(db_ref)
            ds_ref[...] = jnp.zeros_like(ds_ref)

        rf = r_ref[...]
        dyf = dy_ref[...]
        mu = jnp.mean(rf, axis=-1, keepdims=True)
        xc = rf - mu
        var = jnp.mean(xc * xc, axis=-1, keepdims=True)
        rstd = lax.rsqrt(var + LN_EPS)
        xhat = xc * rstd
        dxh = dyf * g_ref[...]
        c1 = jnp.mean(dxh, axis=-1, keepdims=True)
        c2 = jnp.mean(dxh * xhat, axis=-1, keepdims=True)
        dr = rstd * (dxh - c1 - xhat * c2)
        dr_ref[...] = dr
        dg_ref[...] += jnp.sum(dyf * xhat, axis=0, keepdims=True)
        db_ref[...] += jnp.sum(dyf, axis=0, keepdims=True)
        ds_ref[...] += jnp.sum(dr, axis=0, keepdims=True)

    row = pl.BlockSpec((tm, Dm), lambda i: (i, 0))
    vec = pl.BlockSpec((1, Dm), lambda i: (0, 0))
    vshape = jax.ShapeDtypeStruct((1, Dm), F32)
    return _pcall(kern, name=name, out_shape=(jax.ShapeDtypeStruct((S, Dm), F32), vshape, vshape, vshape),
                  grid=(S // tm,), in_specs=[row, row, vec], out_specs=(row, vec, vec, vec),
                  dims=("arbitrary",), vmem_mb=48)(dy, r, g.reshape(1, Dm))


def _loss_and_dy(y, target, *, name):
    S, Dm = y.shape
    tm = min(512, S)

    def kern(y_ref, t_ref, dy_ref, l_ref):
        @pl.when(pl.program_id(0) == 0)
        def _():
            l_ref[...] = jnp.zeros_like(l_ref)

        e = y_ref[...] - t_ref[...]
        dy_ref[...] = e / float(Dm)
        per_row = jnp.mean(e * e, axis=-1, keepdims=True)
        l_ref[...] += 0.5 * jnp.sum(per_row, axis=0, keepdims=True)

    row = pl.BlockSpec((tm, Dm), lambda i: (i, 0))
    acc = pl.BlockSpec((8, LANE), lambda i: (0, 0))
    return _pcall(kern, name=name,
                  out_shape=(jax.ShapeDtypeStruct((S, Dm), F32), jax.ShapeDtypeStruct((8, LANE), F32)),
                  grid=(S // tm,), in_specs=[row, row], out_specs=(row, acc), dims=("arbitrary",), vmem_mb=48)(
                      y, target)


def _rot_sum(t):
    return pltpu.roll(t, 32, 1) + pltpu.roll(t, 96, 1)


def _mla_qkv(proj, cos_t, sin_t, qg, kvg, wuq, wukv, *, name):
    S = proj.shape[0]
    tm = min(256, S)

    def kern(ql_ref, kvl_ref, kr_ref, cos_ref, sin_ref, qg_ref, kvg_ref, wuq_ref, wukv_ref,
             qc_ref, kc_ref, v_ref, qn_ref, kvn_ref):
        cosv = cos_ref[...]
        sinv = sin_ref[...]

        def rope(t):
            return t * cosv + _rot_sum(t) * sinv

        ql = ql_ref[...]
        qn = (ql * lax.rsqrt(jnp.mean(ql * ql, axis=-1, keepdims=True) + RMS_EPS) * qg_ref[...]).astype(BF16)
        kvl = kvl_ref[...]
        kvn = (kvl * lax.rsqrt(jnp.mean(kvl * kvl, axis=-1, keepdims=True) + RMS_EPS) * kvg_ref[...]).astype(BF16)
        qn_ref[...] = qn
        kvn_ref[...] = kvn
        q = jnp.dot(qn, wuq_ref[...], preferred_element_type=F32)
        kv = jnp.dot(kvn, wukv_ref[...], preferred_element_type=F32)
        kr = rope(kr_ref[...]).astype(BF16)
        for h in range(N_HEADS):
            c0 = 256 * h
            qc_ref[:, c0:c0 + 128] = q[:, c0:c0 + 128].astype(BF16)
            qc_ref[:, c0 + 128:c0 + 256] = rope(q[:, c0 + 128:c0 + 256]).astype(BF16)
            kc_ref[:, c0:c0 + 128] = kv[:, c0:c0 + 128].astype(BF16)
            kc_ref[:, c0 + 128:c0 + 256] = kr
            v_ref[:, 128 * h:128 * h + 128] = kv[:, c0 + 128:c0 + 256].astype(BF16)

    def row(w, blk):
        return pl.BlockSpec((tm, w), lambda i: (i, blk))

    def full(shape):
        return pl.BlockSpec(shape, lambda i: (0,) * len(shape))

    outs = (jax.ShapeDtypeStruct((S, 2048), BF16), jax.ShapeDtypeStruct((S, 2048), BF16),
            jax.ShapeDtypeStruct((S, 1024), BF16), jax.ShapeDtypeStruct((S, Q_LORA), BF16),
            jax.ShapeDtypeStruct((S, KV_LORA), BF16))
    return _pcall(kern, name=name, out_shape=outs, grid=(S // tm,),
                  in_specs=[row(512, 2), row(256, 18), row(128, 38), row(128, 0), row(128, 0),
                            full((1, Q_LORA)), full((1, KV_LORA)), full((Q_LORA, 2048)), full((KV_LORA, 2048))],
                  out_specs=(row(2048, 0), row(2048, 0), row(1024, 0), row(512, 0), row(256, 0)),
                  dims=("parallel",), vmem_mb=48)(
                      proj, proj, proj, cos_t, sin_t, qg.reshape(1, -1), kvg.reshape(1, -1), wuq, wukv)


def _mla_qkv_bwd(dqc, dkc, dv, proj, cos_t, sin_t, qg, kvg, wuq, wukv, *, name):
    S = proj.shape[0]
    tm = min(256, S)

    def kern(dq_ref, dk_ref, dv_ref, ql_ref, kvl_ref, cos_ref, sin_ref, qg_ref, kvg_ref, wuq_ref, wukv_ref,
             dqb_ref, dkvb_ref, dql_ref, dkvl_ref, dkr_ref, dqg_ref, dkvg_ref):
        @pl.when(pl.program_id(0) == 0)
        def _():
            dqg_ref[...] = jnp.zeros_like(dqg_ref)
            dkvg_ref[...] = jnp.zeros_like(dkvg_ref)

        cosv = cos_ref[...]
        sinv = sin_ref[...]

        def unrope(t):
            return t * cosv - _rot_sum(t) * sinv

        dkr = jnp.zeros((tm, 128), F32)
        for h in range(N_HEADS):
            c0 = 256 * h
            dqb_ref[:, c0:c0 + 128] = dq_ref[:, c0:c0 + 128].astype(BF16)
            dqb_ref[:, c0 + 128:c0 + 256] = unrope(dq_ref[:, c0 + 128:c0 + 256]).astype(BF16)
            dkvb_ref[:, c0:c0 + 128] = dk_ref[:, c0:c0 + 128].astype(BF16)
            dkvb_ref[:, c0 + 128:c0 + 256] = dv_ref[:, 128 * h:128 * h + 128].astype(BF16)
            dkr = dkr + dk_ref[:, c0 + 128:c0 + 256]
        dkr_ref[...] = unrope(dkr).astype(BF16)

        def rms_bwd(x, g, dy):
            n = x.shape[-1]
            rs = lax.rsqrt(jnp.mean(x * x, axis=-1, keepdims=True) + RMS_EPS)
            dyg = dy * g
            dx = rs * dyg - x * (rs * rs * rs) * (jnp.sum(dyg * x, axis=-1, keepdims=True) / n)
            return dx, jnp.sum(dy * (x * rs), axis=0, keepdims=True)

        dqn = lax.dot_general(dqb_ref[...], wuq_ref[...], NT, preferred_element_type=F32)
        dql, dqg = rms_bwd(ql_ref[...], qg_ref[...], dqn)
        dql_ref[...] = dql.astype(BF16)
        dqg_ref[...] += dqg
        dkvn = lax.dot_general(dkvb_ref[...], wukv_ref[...], NT, preferred_element_type=F32)
        dkvl, dkvg = rms_bwd(kvl_ref[...], kvg_ref[...], dkvn)
        dkvl_ref[...] = dkvl.astype(BF16)
        dkvg_ref[...] += dkvg

    def row(w, blk):
        return pl.BlockSpec((tm, w), lambda i: (i, blk))

    def full(shape):
        return pl.BlockSpec(shape, lambda i: (0,) * len(shape))

    outs = (jax.ShapeDtypeStruct((S, 2048), BF16), jax.ShapeDtypeStruct((S, 2048), BF16),
            jax.ShapeDtypeStruct((S, Q_LORA), BF16), jax.ShapeDtypeStruct((S, KV_LORA), BF16),
            jax.ShapeDtypeStruct((S, 128), BF16), jax.ShapeDtypeStruct((1, Q_LORA), F32),
            jax.ShapeDtypeStruct((1, KV_LORA), F32))
    return _pcall(kern, name=name, out_shape=outs, grid=(S // tm,),
                  in_specs=[row(2048, 0), row(2048, 0), row(1024, 0), row(512, 2), row(256, 18),
                            row(128, 0), row(128, 0), full((1, Q_LORA)), full((1, KV_LORA)),
                            full((Q_LORA, 2048)), full((KV_LORA, 2048))],
                  out_specs=(row(2048, 0), row(2048, 0), row(512, 0), row(256, 0), row(128, 0),
                             full((1, Q_LORA)), full((1, KV_LORA))),
                  dims=("arbitrary",), vmem_mb=56)(
                      dqc, dkc, dv, proj, proj, cos_t, sin_t, qg.reshape(1, -1), kvg.reshape(1, -1), wuq, wukv)


def _causal_mask(t):
    row = lax.broadcasted_iota(jnp.int32, (t, t), 0)
    col = lax.broadcasted_iota(jnp.int32, (t, t), 1)
    return col <= row


def _flash_fwd(qc, kc, v, *, name):
    S = qc.shape[0]
    t = min(512, S)
    n = S // t

    def kern(q_ref, k_ref, v_ref, o_ref, lse_ref, m_s, l_s, acc_s):
        qi = pl.program_id(1)
        ki = pl.program_id(2)

        @pl.when(ki == 0)
        def _():
            m_s[...] = jnp.full_like(m_s, -jnp.inf)
            l_s[...] = jnp.zeros_like(l_s)
            acc_s[...] = jnp.zeros_like(acc_s)

        def step(masked):
            s = lax.dot_general(q_ref[...], k_ref[...], NT, preferred_element_type=F32) * SCALE
            if masked:
                s = jnp.where(_causal_mask(t), s, -jnp.inf)
            m_prev = m_s[...]
            m_new = jnp.maximum(m_prev, jnp.max(s, axis=1, keepdims=True))
            a = jnp.exp(m_prev - m_new)
            p = jnp.exp(s - m_new)
            l_s[...] = a * l_s[...] + jnp.sum(p, axis=1, keepdims=True)
            acc_s[...] = a * acc_s[...] + jnp.dot(p.astype(BF16), v_ref[...], preferred_element_type=F32)
            m_s[...] = m_new

        @pl.when(ki < qi)
        def _():
            step(False)

        @pl.when(ki == qi)
        def _():
            step(True)
            o_ref[...] = acc_s[...] / l_s[...]
            lse_ref[...] = jnp.broadcast_to(m_s[...] + jnp.log(l_s[...]), (t, LANE))

    q_spec = pl.BlockSpec((t, 256), lambda h, qi, ki: (qi, h))
    k_spec = pl.BlockSpec((t, 256), lambda h, qi, ki: (jnp.minimum(ki, qi), h))
    v_spec = pl.BlockSpec((t, 128), lambda h, qi, ki: (jnp.minimum(ki, qi), h))
    o_spec = pl.BlockSpec((t, 128), lambda h, qi, ki: (qi, h))
    lse_spec = pl.BlockSpec((None, t, LANE), lambda h, qi, ki: (h, qi, 0))
    return _pcall(kern, name=name,
                  out_shape=(jax.ShapeDtypeStruct((S, D_MLA), F32), jax.ShapeDtypeStruct((N_HEADS, S, LANE), F32)),
                  grid=(N_HEADS, n, n), in_specs=[q_spec, k_spec, v_spec], out_specs=(o_spec, lse_spec),
                  scratch=[pltpu.VMEM((t, 1), F32), pltpu.VMEM((t, 1), F32), pltpu.VMEM((t, 128), F32)],
                  dims=("parallel", "parallel", "arbitrary"), vmem_mb=48)(qc, kc, v)


def _flash_bwd(qc, kc, v, o, do, lse, *, name):
    S = qc.shape[0]
    t = min(512, S)
    n = S // t

    def kern(q_ref, k_ref, v_ref, o_ref, do_ref, lse_ref, dq_ref, dk_ref, dv_ref):
        ki = pl.program_id(1)
        qi = pl.program_id(2)

        @pl.when(jnp.logical_and(ki == 0, qi == 0))
        def _():
            dq_ref[...] = jnp.zeros_like(dq_ref)

        @pl.when(qi == 0)
        def _():
            dk_ref[...] = jnp.zeros_like(dk_ref)
            dv_ref[...] = jnp.zeros_like(dv_ref)

        def step(masked):
            q = q_ref[...]
            k = k_ref[...]
            dof = do_ref[...]
            s = lax.dot_general(q, k, NT, preferred_element_type=F32) * SCALE
            if masked:
                s = jnp.where(_causal_mask(t), s, -jnp.inf)
            p = jnp.exp(s - lse_ref[:, 0:1])
            dob = dof.astype(BF16)
            dv_ref[...] += lax.dot_general(p.astype(BF16), dob, TN, preferred_element_type=F32)
            dp = lax.dot_general(dob, v_ref[...], NT, preferred_element_type=F32)
            delta = jnp.sum(dof * o_ref[...], axis=1, keepdims=True)
            ds = (p * (dp - delta) * SCALE).astype(BF16)
            dk_ref[...] += lax.dot_general(ds, q, TN, preferred_element_type=F32)
            r0 = pl.multiple_of(qi * t, t)
            dq_ref[pl.ds(r0, t), :] += jnp.dot(ds, k, preferred_element_type=F32)

        @pl.when(qi > ki)
        def _():
            step(False)

        @pl.when(qi == ki)
        def _():
            step(True)

    def qrow(w):
        return pl.BlockSpec((t, w), lambda h, ki, qi: (jnp.maximum(qi, ki), h))

    def krow(w):
        return pl.BlockSpec((t, w), lambda h, ki, qi: (ki, h))

    lse_spec = pl.BlockSpec((None, t, LANE), lambda h, ki, qi: (h, jnp.maximum(qi, ki), 0))
    dq_spec = pl.BlockSpec((S, 256), lambda h, ki, qi: (0, h))
    return _pcall(kern, name=name,
                  out_shape=(jax.ShapeDtypeStruct((S, 2048), F32), jax.ShapeDtypeStruct((S, 2048), F32),
                             jax.ShapeDtypeStruct((S, D_MLA), F32)),
                  grid=(N_HEADS, n, n),
                  in_specs=[qrow(256), krow(256), krow(128), qrow(128), qrow(128), lse_spec],
                  out_specs=(dq_spec, krow(256), krow(128)),
                  dims=("parallel", "arbitrary", "arbitrary"), vmem_mb=56)(qc, kc, v, o, do, lse)


def _mixer_specs(S, tm):
    hb = tm // HALO
    last_hb = S // HALO - 1

    def main(w, blk):
        return pl.BlockSpec((tm, w), lambda i: (i, blk))

    def prev(w, blk):
        return pl.BlockSpec((HALO, w), lambda i: (jnp.maximum(i * hb - 1, 0), blk))

    def nxt(w, blk):
        return pl.BlockSpec((HALO, w), lambda i: (jnp.minimum((i + 1) * hb, last_hb), blk))

    def full(shape):
        return pl.BlockSpec(shape, lambda i: (0,) * len(shape))

    return main, prev, nxt, full


def _fill_halo(i, xp, xu, hp_ref, hch_ref, hcc_ref, pin_ref, ch_ref, cc_ref, tm):
    first = i == 0
    xp[0:HALO, :] = jnp.where(first, 0.0, hp_ref[...])
    xp[HALO:HALO + tm, :] = pin_ref[...]
    xu[0:HALO, :] = jnp.where(first, 0.0, hch_ref[...] * hcc_ref[...])
    xu[HALO:HALO + tm, :] = cc_ref[...] * ch_ref[...]


def _pooled(xp, g, t1, tm):
    w = POOL_WINDOWS[g]
    lanes = slice(128 * g, 128 * g + 128)
    x0 = xp[HALO:HALO + tm, lanes]
    acc = x0
    for k in range(1, w):
        acc = acc + xp[HALO - k:HALO - k + tm, lanes]
    return acc / jnp.minimum(t1, float(w)) - x0


def _conv_fwd(xu, cw_ref, tm):
    return (cw_ref[0:1, :] * xu[HALO - 2:HALO - 2 + tm, :] + cw_ref[1:2, :] * xu[HALO - 1:HALO - 1 + tm, :]
            + cw_ref[2:3, :] * xu[HALO:HALO + tm, :])


def _mixer_fwd(proj, o, wpool, ps, convw, *, name):
    S = proj.shape[0]
    tm = min(256, S)
    main, prev, _, full = _mixer_specs(S, tm)

    def kern(gm_ref, pin_ref, gp_ref, ch_ref, cb_ref, cc_ref, gc_ref, hp_ref, hch_ref, hcc_ref,
             o_ref, wp_ref, ps_ref, cw_ref, mix_ref, xp, xu):
        i = pl.program_id(0)
        _fill_halo(i, xp, xu, hp_ref, hch_ref, hcc_ref, pin_ref, ch_ref, cc_ref, tm)
        t1 = (i * tm + lax.broadcasted_iota(jnp.int32, (tm, 1), 0) + 1).astype(F32)
        for g in range(4):
            lanes = slice(128 * g, 128 * g + 128)
            pooled = _pooled(xp, g, t1, tm)
            z = jnp.dot(pooled.astype(BF16), wp_ref[g].astype(BF16), preferred_element_type=F32)
            gp = gp_ref[:, lanes]
            y = z * ps_ref[:, lanes] * (gp * _sigmoid(gp))
            mix_ref[:, 1024 + 128 * g:1024 + 128 * g + 128] = y.astype(BF16)
        gc = gc_ref[...]
        mix_ref[:, 1536:2048] = (cb_ref[...] * _conv_fwd(xu, cw_ref, tm) * (gc * _sigmoid(gc))).astype(BF16)
        gm = gm_ref[...]
        mix_ref[:, 0:1024] = (o_ref[...] * (gm * _sigmoid(gm))).astype(BF16)

    return _pcall(kern, name=name, out_shape=jax.ShapeDtypeStruct((S, 2048), BF16), grid=(S // tm,),
                  in_specs=[main(1024, 0), main(512, 3), main(512, 4), main(512, 5), main(512, 6), main(512, 7),
                            main(512, 8), prev(512, 3), prev(512, 5), prev(512, 7),
                            main(1024, 0), full((4, 128, 128)), full((1, 512)), full((3, 512))],
                  out_specs=main(2048, 0),
                  scratch=[pltpu.VMEM((tm + HALO, 512), F32), pltpu.VMEM((tm + HALO, 512), F32)],
                  dims=("parallel",), vmem_mb=48)(
                      proj, proj, proj, proj, proj, proj, proj, proj, proj, proj, o, wpool, ps.reshape(1, 512), convw)


def _mixer_bwd(dmix, proj, o, wpool, ps, convw, *, name):
    S = proj.shape[0]
    tm = min(256, S)
    n = S // tm
    main, prev, nxt, full = _mixer_specs(S, tm)

    def kern(dm_ref, dmn_ref, gm_ref, pin_ref, gp_ref, ch_ref, cb_ref, cc_ref, gc_ref,
             hp_ref, hch_ref, hcc_ref, gpn_ref, cbn_ref, gcn_ref, o_ref, wp_ref, ps_ref, cw_ref,
             d3_ref, dgm_ref, do_ref, dwp_ref, dps_ref, dcw_ref, xp, xu, ee, ed):
        i = pl.program_id(0)
        last = i == n - 1

        @pl.when(i == 0)
        def _():
            dwp_ref[...] = jnp.zeros_like(dwp_ref)
            dps_ref[...] = jnp.zeros_like(dps_ref)
            dcw_ref[...] = jnp.zeros_like(dcw_ref)

        _fill_halo(i, xp, xu, hp_ref, hch_ref, hcc_ref, pin_ref, ch_ref, cc_ref, tm)
        t1 = (i * tm + lax.broadcasted_iota(jnp.int32, (tm, 1), 0) + 1).astype(F32)
        t1n = ((i + 1) * tm + lax.broadcasted_iota(jnp.int32, (HALO, 1), 0) + 1).astype(F32)

        for g in range(4):
            w = float(POOL_WINDOWS[g])
            lanes = slice(128 * g, 128 * g + 128)
            pooled = _pooled(xp, g, t1, tm)
            pb = pooled.astype(BF16)
            wp = wp_ref[g].astype(BF16)
            z = jnp.dot(pb, wp, preferred_element_type=F32)
            psl = ps_ref[:, lanes]
            sg, dsg = _silu_and_grad(gp_ref[:, lanes])
            dmp = dm_ref[:, 1024 + 128 * g:1024 + 128 * g + 128]
            dyp = dmp * sg
            d3_ref[:, 512 + 128 * g:512 + 128 * g + 128] = (dmp * (z * psl) * dsg).astype(BF16)
            dps_ref[:, lanes] += jnp.sum(dyp * z, axis=0, keepdims=True)
            dz = (dyp * psl).astype(BF16)
            dwp_ref[g] += lax.dot_general(pb, dz, TN, preferred_element_type=F32)
            dpl = lax.dot_general(dz, wp, NT, preferred_element_type=F32)
            ee[0:tm, lanes] = dpl / jnp.minimum(t1, w)
            gpn = gpn_ref[:, lanes]
            dzn = (dmn_ref[:, lanes] * (gpn * _sigmoid(gpn)) * psl).astype(BF16)
            dpn = lax.dot_general(dzn, wp, NT, preferred_element_type=F32)
            ee[tm:tm + HALO, lanes] = jnp.where(last, 0.0, dpn / jnp.minimum(t1n, w))
            acc = ee[0:tm, lanes]
            for k in range(1, POOL_WINDOWS[g]):
                acc = acc + ee[k:k + tm, lanes]
            d3_ref[:, lanes] = (acc - dpl).astype(BF16)

        yc = _conv_fwd(xu, cw_ref, tm)
        sgc, dsgc = _silu_and_grad(gc_ref[...])
        cb = cb_ref[...]
        dmc = dm_ref[:, 1536:2048]
        d3_ref[:, 2560:3072] = (dmc * cb * yc * dsgc).astype(BF16)
        d3_ref[:, 1536:2048] = (dmc * yc * sgc).astype(BF16)
        dyc = dmc * cb * sgc
        ed[0:tm, :] = dyc
        gcn = gcn_ref[...]
        ed[tm:tm + HALO, :] = jnp.where(last, 0.0, dmn_ref[:, 512:1024] * cbn_ref[...] * (gcn * _sigmoid(gcn)))
        dcw_ref[0:1, :] += jnp.sum(dyc * xu[HALO - 2:HALO - 2 + tm, :], axis=0, keepdims=True)
        dcw_ref[1:2, :] += jnp.sum(dyc * xu[HALO - 1:HALO - 1 + tm, :], axis=0, keepdims=True)
        dcw_ref[2:3, :] += jnp.sum(dyc * xu[HALO:HALO + tm, :], axis=0, keepdims=True)
        du = cw_ref[2:3, :] * dyc + cw_ref[1:2, :] * ed[1:1 + tm, :] + cw_ref[0:1, :] * ed[2:2 + tm, :]
        d3_ref[:, 2048:2560] = (du * ch_ref[...]).astype(BF16)
        d3_ref[:, 1024:1536] = (du * cc_ref[...]).astype(BF16)

        sgm, dsgm = _silu_and_grad(gm_ref[...])
        dmm = dm_ref[:, 0:1024]
        do_ref[...] = dmm * sgm
        dgm_ref[...] = (dmm * o_ref[...] * dsgm).astype(BF16)

    outs = (jax.ShapeDtypeStruct((S, 3072), BF16), jax.ShapeDtypeStruct((S, 1024), BF16),
            jax.ShapeDtypeStruct((S, 1024), F32), jax.ShapeDtypeStruct((4, 128, 128), F32),
            jax.ShapeDtypeStruct((1, 512), F32), jax.ShapeDtypeStruct((3, 512), F32))
    scr = [pltpu.VMEM((tm + HALO, 512), F32) for _ in range(4)]
    return _pcall(kern, name=name, out_shape=outs, grid=(n,),
                  in_specs=[main(2048, 0), nxt(1024, 1),
                            main(1024, 0), main(512, 3), main(512, 4), main(512, 5), main(512, 6), main(512, 7),
                            main(512, 8), prev(512, 3), prev(512, 5), prev(512, 7),
                            nxt(512, 4), nxt(512, 6), nxt(512, 8),
                            main(1024, 0), full((4, 128, 128)), full((1, 512)), full((3, 512))],
                  out_specs=(main(3072, 0), main(1024, 0), main(1024, 0), full((4, 128, 128)), full((1, 512)),
                             full((3, 512))),
                  scratch=scr, dims=("arbitrary",), vmem_mb=56)(
                      dmix, dmix, proj, proj, proj, proj, proj, proj, proj, proj, proj, proj, proj, proj, proj,
                      o, wpool, ps.reshape(1, 512), convw)


def _outproj_ln(mix, wout, h, bout, g, b, *, name):
    S, Dm = h.shape
    tm = min(256, S)

    def kern(mix_ref, w_ref, h_ref, bo_ref, g_ref, b_ref, y_ref, r_ref):
        out = jnp.dot(mix_ref[...], w_ref[...], preferred_element_type=F32) + bo_ref[...]
        r = ALPHA * h_ref[...] + out
        r_ref[...] = r
        mu = jnp.mean(r, axis=-1, keepdims=True)
        xc = r - mu
        var = jnp.mean(xc * xc, axis=-1, keepdims=True)
        y_ref[...] = xc * lax.rsqrt(var + LN_EPS) * g_ref[...] + b_ref[...]

    row = pl.BlockSpec((tm, Dm), lambda i: (i, 0))
    vec = pl.BlockSpec((1, Dm), lambda i: (0, 0))
    wsp = pl.BlockSpec((Dm, Dm), lambda i: (0, 0))
    sds = jax.ShapeDtypeStruct((S, Dm), F32)
    return _pcall(kern, name=name, out_shape=(sds, sds), grid=(S // tm,),
                  in_specs=[row, wsp, row, vec, vec, vec], out_specs=(row, row), dims=("parallel",), vmem_mb=56)(
                      mix, wout, h, bout.reshape(1, Dm), g.reshape(1, Dm), b.reshape(1, Dm))


def _adamw_math(w, g, m, v):
    m = ADAM_B1 * m + (1.0 - ADAM_B1) * g
    v = ADAM_B2 * v + (1.0 - ADAM_B2) * (g * g)
    m_hat = m / (1.0 - ADAM_B1 ** ADAM_STEP)
    v_hat = v / (1.0 - ADAM_B2 ** ADAM_STEP)
    delta = -ADAM_LR * (m_hat / (jnp.sqrt(v_hat) + ADAM_EPS) + ADAM_WD * w)
    return delta, m, v


def _adamw(w, g, m, v, *, name):
    shape = w.shape
    C = shape[-1]
    R = 1
    for s in shape[:-1]:
        R *= s
    tr = R
    for cand in (512, 256, 128, 64, 32, 16, 8):
        if R % cand == 0 and R > cand and cand * C <= 256 * 1024:
            tr = cand
            break

    def kern(w_ref, g_ref, m_ref, v_ref, d_ref, mo_ref, vo_ref):
        d, mn, vn = _adamw_math(w_ref[...], g_ref[...], m_ref[...], v_ref[...])
        d_ref[...] = d
        mo_ref[...] = mn
        vo_ref[...] = vn

    blk = pl.BlockSpec((tr, C), lambda i: (i, 0))
    sds = jax.ShapeDtypeStruct((R, C), F32)
    outs = _pcall(kern, name=name, out_shape=(sds, sds, sds), grid=(R // tr,), in_specs=[blk] * 4,
                  out_specs=(blk, blk, blk), dims=("parallel",), vmem_mb=48)(
                      w.reshape(R, C), g.reshape(R, C), m.reshape(R, C), v.reshape(R, C))
    return tuple(t.reshape(shape) for t in outs)


def _small_sum_adamw(gathered, w, m, v, *, name):
    R = w.shape[0]

    def kern(ga_ref, w_ref, m_ref, v_ref, g_ref, d_ref, mo_ref, vo_ref):
        g = ga_ref[0]
        for k in range(1, N_DEV):
            g = g + ga_ref[k]
        g_ref[...] = g
        d, mn, vn = _adamw_math(w_ref[...], g, m_ref[...], v_ref[...])
        d_ref[...] = d
        mo_ref[...] = mn
        vo_ref[...] = vn

    sds = jax.ShapeDtypeStruct((R, LANE), F32)
    return _pcall(kern, name=name, out_shape=(sds, sds, sds, sds), vmem_mb=48)(gathered, w, m, v)


def _add2(a, b, c_idx, *, name):
    _, R, C = a.shape
    tr = 256 if R % 256 == 0 else R

    def kern(c_ref, a_ref, b_ref, o_ref):
        o_ref[...] = a_ref[...] + b_ref[...]

    gs = pltpu.PrefetchScalarGridSpec(
        num_scalar_prefetch=1, grid=(R // tr,),
        in_specs=[pl.BlockSpec((None, tr, C), lambda i, c: (c[0], i, 0)), pl.BlockSpec((tr, C), lambda i, c: (i, 0))],
        out_specs=pl.BlockSpec((tr, C), lambda i, c: (i, 0)))
    return pl.pallas_call(kern, name=name, out_shape=jax.ShapeDtypeStruct((R, C), F32), grid_spec=gs,
                          compiler_params=pltpu.CompilerParams(dimension_semantics=("parallel",),
                                                               vmem_limit_bytes=48 << 20))(c_idx, a, b)


def _add4(p, r2, j_idx, *, name):
    _, R, C = p.shape
    tr = 256 if R % 256 == 0 else R

    def kern(j_ref, p_ref, r_ref, o_ref):
        o_ref[...] = ((p_ref[...] + r_ref[0]) + r_ref[1]) + r_ref[2]

    gs = pltpu.PrefetchScalarGridSpec(
        num_scalar_prefetch=1, grid=(R // tr,),
        in_specs=[pl.BlockSpec((None, tr, C), lambda i, j: (j[0], i, 0)),
                  pl.BlockSpec((3, tr, C), lambda i, j: (0, i, 0))],
        out_specs=pl.BlockSpec((tr, C), lambda i, j: (i, 0)))
    return pl.pallas_call(kern, name=name, out_shape=jax.ShapeDtypeStruct((R, C), F32), grid_spec=gs,
                          compiler_params=pltpu.CompilerParams(dimension_semantics=("parallel",),
                                                               vmem_limit_bytes=48 << 20))(j_idx, p, r2)


HBM_SPEC = pl.BlockSpec(memory_space=pl.ANY)


def _mesh_pos():
    x, y, c = lax.axis_index("x"), lax.axis_index("y"), lax.axis_index("c")
    return x, y, c


def _other_chips(x, y):
    return [(1 - x, y), (x, 1 - y), (1 - x, 1 - y)]


def _allgather_weights(shards, *, name):
    na = len(shards)

    def body(*refs):
        ins, outs = refs[:na], refs[na:2 * na]
        send_sems, recv_sems, local_sems = refs[2 * na:]
        x, y, c = _mesh_pos()
        j = 2 * x + y
        chips = _other_chips(x, y)
        sibling = (x, y, 1 - c)
        local = [pltpu.make_async_copy(ins[a], outs[a].at[j], local_sems.at[a]) for a in range(na)]
        for cp in local:
            cp.start()

        def first(a, k):
            return pltpu.make_async_remote_copy(
                src_ref=ins[a].at[c], dst_ref=outs[a].at[j, c], send_sem=send_sems.at[3 * a + k],
                recv_sem=recv_sems.at[3 * a + k], device_id=(*chips[k], c), device_id_type=MESH)

        def passed(a, k, layer):
            pk = 2 * chips[k][0] + chips[k][1]
            return pltpu.make_async_remote_copy(
                src_ref=outs[a].at[pk, layer], dst_ref=outs[a].at[pk, layer],
                send_sem=send_sems.at[3 * na + 3 * a + k], recv_sem=recv_sems.at[3 * na + 3 * a + k],
                device_id=sibling, device_id_type=MESH)

        for a in range(na):
            for k in range(3):
                first(a, k).start()
        for a in range(na):
            for k in range(3):
                pk = 2 * chips[k][0] + chips[k][1]
                pltpu.make_async_remote_copy(
                    src_ref=ins[a].at[c], dst_ref=outs[a].at[pk, c], send_sem=send_sems.at[3 * a + k],
                    recv_sem=recv_sems.at[3 * a + k], device_id=(*chips[k], c), device_id_type=MESH).wait_recv()
                passed(a, k, c).start()
        for a in range(na):
            for k in range(3):
                passed(a, k, 1 - c).wait_recv()
        for a in range(na):
            for k in range(3):
                first(a, k).wait_send()
                passed(a, k, c).wait_send()
        for cp in local:
            cp.wait()

    out_shape = tuple(jax.ShapeDtypeStruct((N_CHIPS,) + s.shape, s.dtype) for s in shards)
    return _pcall(body, name=name, out_shape=out_shape, in_specs=[HBM_SPEC] * na, out_specs=(HBM_SPEC,) * na,
                  scratch=[pltpu.SemaphoreType.DMA((6 * na,)), pltpu.SemaphoreType.DMA((6 * na,)),
                           pltpu.SemaphoreType.DMA((na,))])(*shards)


def _exchange_layers(grads, *, name):
    na = len(grads)

    def body(*refs):
        ins, outs = refs[:na], refs[na:2 * na]
        send_sems, recv_sems = refs[2 * na:]
        x, y, c = _mesh_pos()
        copies = [pltpu.make_async_remote_copy(
            src_ref=ins[a].at[1 - c], dst_ref=outs[a], send_sem=send_sems.at[a], recv_sem=recv_sems.at[a],
            device_id=(x, y, 1 - c), device_id_type=MESH) for a in range(na)]
        for cp in copies:
            cp.start()
        for cp in copies:
            cp.wait()

    out_shape = tuple(jax.ShapeDtypeStruct(g.shape[1:], g.dtype) for g in grads)
    return _pcall(body, name=name, out_shape=out_shape, in_specs=[HBM_SPEC] * na, out_specs=(HBM_SPEC,) * na,
                  scratch=[pltpu.SemaphoreType.DMA((na,)), pltpu.SemaphoreType.DMA((na,))])(*grads)


def _scatter_chips(parts, *, name):
    na = len(parts)

    def body(*refs):
        ins, outs = refs[:na], refs[na:2 * na]
        send_sems, recv_sems = refs[2 * na:]
        x, y, c = _mesh_pos()
        chips = _other_chips(x, y)
        copies = []
        for a in range(na):
            for k in range(3):
                pk = 2 * chips[k][0] + chips[k][1]
                copies.append(pltpu.make_async_remote_copy(
                    src_ref=ins[a].at[pk], dst_ref=outs[a].at[k], send_sem=send_sems.at[3 * a + k],
                    recv_sem=recv_sems.at[3 * a + k], device_id=(*chips[k], c), device_id_type=MESH))
        for cp in copies:
            cp.start()
        for cp in copies:
            cp.wait()

    out_shape = tuple(jax.ShapeDtypeStruct((3,) + p.shape[1:], p.dtype) for p in parts)
    return _pcall(body, name=name, out_shape=out_shape, in_specs=[HBM_SPEC] * na, out_specs=(HBM_SPEC,) * na,
                  scratch=[pltpu.SemaphoreType.DMA((3 * na,)), pltpu.SemaphoreType.DMA((3 * na,))])(*parts)


def _swap_layers(sums, *, name):
    na = len(sums)

    def body(*refs):
        ins, outs = refs[:na], refs[na:2 * na]
        send_sems, recv_sems, local_sems = refs[2 * na:]
        x, y, c = _mesh_pos()
        local = [pltpu.make_async_copy(ins[a], outs[a].at[c], local_sems.at[a]) for a in range(na)]
        copies = [pltpu.make_async_remote_copy(
            src_ref=ins[a], dst_ref=outs[a].at[c], send_sem=send_sems.at[a], recv_sem=recv_sems.at[a],
            device_id=(x, y, 1 - c), device_id_type=MESH) for a in range(na)]
        for cp in local + copies:
            cp.start()
        for cp in copies:
            cp.wait()
        for cp in local:
            cp.wait()

    out_shape = tuple(jax.ShapeDtypeStruct((2,) + s.shape, s.dtype) for s in sums)
    return _pcall(body, name=name, out_shape=out_shape, in_specs=[HBM_SPEC] * na, out_specs=(HBM_SPEC,) * na,
                  scratch=[pltpu.SemaphoreType.DMA((na,)), pltpu.SemaphoreType.DMA((na,)),
                           pltpu.SemaphoreType.DMA((na,))])(*sums)


def _allgather_small(block, *, name):
    m_per, n = block.shape

    def body(x_ref, out_ref, send_sems, recv_sems, local_sem):
        x, y, c = _mesh_pos()
        me, sibling = (x, y, c), (x, y, 1 - c)
        chips = _other_chips(x, y)

        def rows(px, py, pc):
            return out_ref.at[4 * px + 2 * py + pc]

        def copy(k, blk, to, src=None):
            return pltpu.make_async_remote_copy(
                src_ref=rows(*blk) if src is None else src, dst_ref=rows(*blk), send_sem=send_sems.at[k],
                recv_sem=recv_sems.at[k], device_id=to, device_id_type=MESH)

        mine = pltpu.make_async_copy(x_ref, rows(*me), local_sem)
        mine.start()
        first = [copy(0, me, sibling, src=x_ref)]
        first += [copy(1 + k, me, (*chip, c), src=x_ref) for k, chip in enumerate(chips)]
        for cp in first:
            cp.start()
        passed = [copy(4 + k, (*chip, c), sibling) for k, chip in enumerate(chips)]
        for k, chip in enumerate(chips):
            copy(1 + k, (*chip, c), me).wait_recv()
            passed[k].start()
        copy(0, sibling, me).wait_recv()
        for k, chip in enumerate(chips):
            copy(4 + k, (*chip, 1 - c), me).wait_recv()
        for cp in first + passed:
            cp.wait_send()
        mine.wait()

    vm = pl.BlockSpec(memory_space=pltpu.VMEM)
    return _pcall(body, name=name, out_shape=jax.ShapeDtypeStruct((N_DEV, m_per, n), block.dtype),
                  in_specs=[vm], out_specs=vm,
                  scratch=[pltpu.SemaphoreType.DMA((7,)), pltpu.SemaphoreType.DMA((7,)), pltpu.SemaphoreType.DMA],
                  vmem_mb=48)(block)


def _w_in_to_p(w):
    pad = jnp.zeros(w.shape[:-1] + (NP - D_IN_PROJ,), w.dtype)
    return jnp.concatenate([w[..., 832:1856], w[..., 0:512], w[..., 1856:4928], w[..., 512:768], w[..., 768:832], pad],
                           axis=-1)


def _w_in_from_p(g):
    return jnp.concatenate([g[..., 1024:1536], g[..., 4608:4864], g[..., 4864:4928], g[..., 0:1024], g[..., 1536:4608]],
                           axis=-1)


def _w_uq_to_p(w):
    k = w.shape[0]
    return jnp.pad(w.reshape(k, N_HEADS, NOPE + ROPE), ((0, 0), (0, 0), (0, 64))).reshape(k, N_HEADS * 256)


def _w_uq_from_p(g):
    k = g.shape[0]
    return g.reshape(k, N_HEADS, 256)[:, :, :NOPE + ROPE].reshape(k, N_HEADS * (NOPE + ROPE))


def _rope_tables(positions):
    half = ROPE // 2
    inv_freq = ROPE_THETA ** (-jnp.arange(half, dtype=F32) / half)
    ang = positions.astype(F32)[:, None] * inv_freq
    cos, sin = jnp.cos(ang), jnp.sin(ang)
    S = positions.shape[0]
    cos_t = jnp.concatenate([cos, cos, jnp.ones((S, 64), F32)], axis=1)
    sin_t = jnp.concatenate([-sin, sin, jnp.zeros((S, 64), F32)], axis=1)
    return cos_t, sin_t


def _local_step(x, positions, target, emb_g, emb_b, w_in_p, q_g, kv_g, w_uq_p, w_ukv, w_pool, pool_scale, conv_w,
                w_out, b_out, ln_g, ln_b):
    cos_t, sin_t = _rope_tables(positions)
    h = _ln_fwd(x, emb_g, emb_b, name="emb_ln")
    saved = []
    for l in range(DEPTH):
        proj = _matmul(h, w_in_p[l], "nn", name=f"in_proj{l}", tm=512, tn=1024, tk=2048)
        qc, kc, v, qn, kvn = _mla_qkv(proj, cos_t, sin_t, q_g[l], kv_g[l], w_uq_p[l], w_ukv[l], name=f"mla_qkv{l}")
        o, lse = _flash_fwd(qc, kc, v, name=f"flash_fwd{l}")
        mix = _mixer_fwd(proj, o, w_pool[l], pool_scale[l], conv_w[l], name=f"mixer_fwd{l}")
        h_next, r = _outproj_ln(mix, w_out[l], h, b_out[l], ln_g[l], ln_b[l], name=f"out_proj_ln{l}")
        saved.append((h, proj, qc, kc, v, qn, kvn, o, lse, mix, r))
        h = h_next

    dh, loss_acc = _loss_and_dy(h, target, name="loss")
    grads = [None] * DEPTH
    for l in reversed(range(DEPTH)):
        h_in, proj, qc, kc, v, qn, kvn, o, lse, mix, r = saved[l]
        dr, d_ln_g, d_ln_b, d_b_out = _ln_bwd(dh, r, ln_g[l], name=f"ln_bwd{l}")
        dmix = _matmul(dr, w_out[l], "nt", name=f"dmix{l}", tm=512, tn=1024, tk=2048)
        d_w_out = _matmul(mix, dr, "tn", name=f"dw_out{l}", tm=1024, tn=1024, tk=512)
        d3, dgm, do, d_w_pool, d_ps, d_conv = _mixer_bwd(dmix, proj, o, w_pool[l], pool_scale[l], conv_w[l],
                                                         name=f"mixer_bwd{l}")
        dqc, dkc, dv = _flash_bwd(qc, kc, v, o, do, lse, name=f"flash_bwd{l}")
        dqb, dkvb, dql, dkvl, dkr, d_qg, d_kvg = _mla_qkv_bwd(dqc, dkc, dv, proj, cos_t, sin_t, q_g[l], kv_g[l],
                                                               w_uq_p[l], w_ukv[l], name=f"mla_qkv_bwd{l}")
        d_w_uq_p = _matmul(qn, dqb, "tn", name=f"dw_uq{l}", tm=512, tn=2048, tk=512)
        d_w_ukv = _matmul(kvn, dkvb, "tn", name=f"dw_ukv{l}", tm=256, tn=2048, tk=512)
        S = x.shape[0]
        dproj = jnp.concatenate([dgm, dql, d3, dkvl, dkr, jnp.zeros((S, 128), BF16)], axis=1)
        d_w_in_p = _matmul(h_in, dproj, "tn", name=f"dw_in{l}", tm=1024, tn=1024, tk=512)
        dh = _matmul(dproj, w_in_p[l], "nt", name=f"dh{l}", tm=512, tn=1024, tk=1280, add=dr, add_scale=ALPHA)
        grads[l] = dict(w_in_p=d_w_in_p, q_g=d_qg[0], kv_g=d_kvg[0], w_uq_p=d_w_uq_p, w_ukv=d_w_ukv,
                        w_pool=d_w_pool, pool_scale=d_ps[0], conv_w=d_conv, w_out=d_w_out, b_out=d_b_out[0],
                        ln_g=d_ln_g[0], ln_b=d_ln_b[0])
    grad_x, d_emb_g, d_emb_b, _ = _ln_bwd(dh, x, emb_g, name="emb_ln_bwd")
    return loss_acc[0, 0], grad_x, d_emb_g[0], d_emb_b[0], grads


SMALL_ORDER = ("emb_ln_g", "emb_ln_b", "q_norm_g", "kv_norm_g", "w_pool", "pool_scale", "b_out", "ln_g", "ln_b")


def _pack_small(arrs, extra_rows):
    flat = jnp.concatenate([a.reshape(-1) for a in arrs])
    rows = flat.shape[0] // LANE
    total = -(-(rows + extra_rows) // 8) * 8
    return jnp.pad(flat, (0, total * LANE - flat.shape[0])).reshape(total, LANE)


def _unpack_small(packed, shapes):
    flat = packed.reshape(-1)
    out, off = [], 0
    for shp in shapes:
        n = 1
        for s in shp:
            n *= s
        out.append(flat[off:off + n].reshape(shp))
        off += n
    return out, off


def kernel(x, positions, emb_ln_g, emb_ln_b, w_in, q_norm_g, kv_norm_g, w_uq, w_ukv, w_pool, pool_scale, conv_w, w_out, b_out, ln_g, ln_b, loss_target, m_emb_ln_g, m_emb_ln_b, m_w_in, m_q_norm_g, m_kv_norm_g, m_w_uq, m_w_ukv, m_w_pool, m_pool_scale, m_conv_w, m_w_out, m_b_out, m_ln_g, m_ln_b, v_emb_ln_g, v_emb_ln_b, v_w_in, v_q_norm_g, v_kv_norm_g, v_w_uq, v_w_ukv, v_w_pool, v_pool_scale, v_conv_w, v_w_out, v_b_out, v_ln_g, v_ln_b):
    xi, yi, ci = lax.axis_index("x"), lax.axis_index("y"), lax.axis_index("c")
    chip = 2 * xi + yi

    conv_bits = lax.bitcast_convert_type(conv_w.reshape(DEPTH, 3 * 128), BF16).reshape(DEPTH, 3, 256)
    conv_bits = jnp.pad(conv_bits, ((0, 0), (0, 13), (0, 0)))
    shards = (w_in.astype(BF16), w_uq.astype(BF16), w_ukv.astype(BF16), w_out.astype(BF16), conv_bits)
    a_in, a_uq, a_ukv, a_out, a_conv = _allgather_weights(shards, name="allgather_weights")
    w_in_full = jnp.concatenate([a_in[k] for k in range(N_CHIPS)], axis=-1)
    w_in_p = _w_in_to_p(w_in_full)
    w_uq_full = jnp.concatenate([a_uq[k] for k in range(N_CHIPS)], axis=-1)
    w_uq_p = jnp.stack([_w_uq_to_p(w_uq_full[l]) for l in range(DEPTH)])
    w_ukv_full = jnp.concatenate([a_ukv[k] for k in range(N_CHIPS)], axis=-1)
    w_out_full = jnp.concatenate([a_out[k] for k in range(N_CHIPS)], axis=1)
    conv_parts = [lax.bitcast_convert_type(a_conv[k][:, :3, :].reshape(DEPTH, 3, 128, 2), F32)
                  for k in range(N_CHIPS)]
    conv_full = jnp.concatenate(conv_parts, axis=-1)

    loss_part, grad_x, d_emb_g, d_emb_b, grads = _local_step(
        x[0], positions[0], loss_target[0], emb_ln_g, emb_ln_b, w_in_p, q_norm_g, kv_norm_g, w_uq_p, w_ukv_full,
        w_pool, pool_scale, conv_full, w_out_full, b_out, ln_g, ln_b)

    def by_chip_cols(g_nat, width):
        return jnp.stack([g_nat[:, k * width:(k + 1) * width] for k in range(N_CHIPS)])

    g_in = jnp.stack([by_chip_cols(_w_in_from_p(grads[l]["w_in_p"]), 1232) for l in range(DEPTH)])
    g_uq = jnp.stack([by_chip_cols(_w_uq_from_p(grads[l]["w_uq_p"]), 384) for l in range(DEPTH)])
    g_ukv = jnp.stack([by_chip_cols(grads[l]["w_ukv"], 512) for l in range(DEPTH)])
    g_out = jnp.stack([grads[l]["w_out"].reshape(N_CHIPS, 512, D_MODEL) for l in range(DEPTH)])
    big = (g_in, g_uq, g_ukv, g_out)
    theirs = _exchange_layers(big, name="exchange_layers")
    c_idx = ci.reshape(1).astype(jnp.int32)
    j_idx = chip.reshape(1).astype(jnp.int32)
    parts = []
    for a, (g, t) in enumerate(zip(big, theirs)):
        _, _, R, C = g.shape
        parts.append(_add2(g.reshape(2, 4 * R, C), t.reshape(4 * R, C), c_idx, name=f"pair_sum{a}").reshape(4, R, C))
    recv = _scatter_chips(tuple(parts), name="scatter_chips")
    sums = tuple(_add4(p, r, j_idx, name=f"chip_sum{a}") for a, (p, r) in enumerate(zip(parts, recv)))
    gw_in, gw_uq, gw_ukv, gw_out = _swap_layers(sums, name="swap_layers")

    small_g = [d_emb_g, d_emb_b,
               jnp.stack([grads[l]["q_g"] for l in range(DEPTH)]), jnp.stack([grads[l]["kv_g"] for l in range(DEPTH)]),
               jnp.stack([grads[l]["w_pool"] for l in range(DEPTH)]),
               jnp.stack([grads[l]["pool_scale"] for l in range(DEPTH)]),
               jnp.stack([grads[l]["b_out"] for l in range(DEPTH)]), jnp.stack([grads[l]["ln_g"] for l in range(DEPTH)]),
               jnp.stack([grads[l]["ln_b"] for l in range(DEPTH)]),
               jnp.stack([grads[l]["conv_w"] for l in range(DEPTH)]),
               jnp.pad(loss_part.reshape(1), (0, LANE - 1))]
    small_w = [emb_ln_g, emb_ln_b, q_norm_g, kv_norm_g, w_pool, pool_scale, b_out, ln_g, ln_b]
    small_m = [m_emb_ln_g, m_emb_ln_b, m_q_norm_g, m_kv_norm_g, m_w_pool, m_pool_scale, m_b_out, m_ln_g, m_ln_b]
    small_v = [v_emb_ln_g, v_emb_ln_b, v_q_norm_g, v_kv_norm_g, v_w_pool, v_pool_scale, v_b_out, v_ln_g, v_ln_b]
    extra = (DEPTH * 3 * 512 + LANE) // LANE
    packed_g = _pack_small(small_g, 0)
    gathered = _allgather_small(packed_g, name="allgather_small")
    g_tot, d_small, m_small, v_small = _small_sum_adamw(
        gathered, _pack_small(small_w, extra), _pack_small(small_m, extra), _pack_small(small_v, extra),
        name="small_sum_adamw")
    shapes = [w.shape for w in small_w]
    g_list, off = _unpack_small(g_tot, shapes)
    d_list, _ = _unpack_small(d_small, shapes)
    m_list, _ = _unpack_small(m_small, shapes)
    v_list, _ = _unpack_small(v_small, shapes)
    flat_tot = g_tot.reshape(-1)
    conv_tot = flat_tot[off:off + DEPTH * 3 * 512].reshape(DEPTH, 3, 512)
    loss = flat_tot[off + DEPTH * 3 * 512]
    g_conv = lax.dynamic_slice_in_dim(conv_tot, chip * 128, 128, axis=2)

    upd = {}
    for nm, w, g, m, v in (("w_in", w_in, gw_in, m_w_in, v_w_in), ("w_uq", w_uq, gw_uq, m_w_uq, v_w_uq),
                           ("w_ukv", w_ukv, gw_ukv, m_w_ukv, v_w_ukv), ("w_out", w_out, gw_out, m_w_out, v_w_out),
                           ("conv_w", conv_w, g_conv, m_conv_w, v_conv_w)):
        upd[nm] = (g,) + _adamw(w, g, m, v, name=f"adamw_{nm}")
    for i, nm in enumerate(SMALL_ORDER):
        upd[nm] = (g_list[i], d_list[i], m_list[i], v_list[i])

    order = ("emb_ln_g", "emb_ln_b", "w_in", "q_norm_g", "kv_norm_g", "w_uq", "w_ukv", "w_pool", "pool_scale",
             "conv_w", "w_out", "b_out", "ln_g", "ln_b")
    outs = [loss, grad_x[None]]
    for field in range(4):
        outs += [upd[nm][field] for nm in order]
    return tuple(outs)
```

```python
import functools

import jax
import jax.numpy as jnp
from jax import lax
from jax.experimental import pallas as pl
from jax.experimental.pallas import tpu as pltpu

F32 = jnp.float32
BF16 = jnp.bfloat16
MESH = pl.DeviceIdType.MESH

D_MODEL = 2048
DEPTH = 2
N_HEADS = 8
NOPE = 128
ROPE = 64
Q_LORA = 512
KV_LORA = 256
D_MLA = 1024
D_POOL = 512
D_CONV = 512
POOL_WINDOWS = (2, 4, 8, 16)
D_IN_PROJ = 4928
LN_EPS = 1e-5
RMS_EPS = 1e-6
ROPE_THETA = 10000.0
ALPHA = (2 * DEPTH) ** 0.25
SCALE = (NOPE + ROPE) ** -0.5
ADAM_LR = 0.001
ADAM_B1 = 0.9
ADAM_B2 = 0.999
ADAM_EPS = 1e-08
ADAM_WD = 0.01
ADAM_STEP = 10

NP = 5120
HALO = 16
LANE = 128
N_CHIPS = 4
N_DEV = 8

NN = (((1,), (0,)), ((), ()))
NT = (((1,), (1,)), ((), ()))
TN = (((0,), (0,)), ((), ()))


def _pcall(kern, *, name, out_shape, grid=None, in_specs=None, out_specs=None, scratch=(), dims=None,
           vmem_mb=None, **kw):
    cp = {}
    if dims is not None:
        cp["dimension_semantics"] = dims
    if vmem_mb is not None:
        cp["vmem_limit_bytes"] = vmem_mb << 20
    args = dict(name=name, out_shape=out_shape, scratch_shapes=list(scratch),
                compiler_params=pltpu.CompilerParams(**cp))
    if grid is not None:
        args["grid"] = grid
    if in_specs is not None:
        args["in_specs"] = in_specs
    if out_specs is not None:
        args["out_specs"] = out_specs
    args.update(kw)
    return pl.pallas_call(kern, **args)


def _sigmoid(g):
    return 1.0 / (1.0 + jnp.exp(-g))


def _silu_and_grad(g):
    sig = _sigmoid(g)
    return g * sig, sig * (1.0 + g * (1.0 - sig))


def _matmul(a, b, mode, *, name, tm, tn, tk, out_dtype=F32, add=None, add_scale=1.0, vmem_mb=48):
    if mode == "nn":
        (M, K), N = a.shape, b.shape[1]
    elif mode == "nt":
        (M, K), N = a.shape, b.shape[0]
    else:
        (K, M), N = a.shape, b.shape[1]
    tm, tn, tk = min(tm, M), min(tn, N), min(tk, K)
    assert M % tm == 0 and N % tn == 0 and K % tk == 0, (name, M, N, K)
    nk = K // tk
    dn = {"nn": NN, "nt": NT, "tn": TN}[mode]
    if mode == "tn":
        a_spec = pl.BlockSpec((tk, tm), lambda i, j, k: (k, i))
    else:
        a_spec = pl.BlockSpec((tm, tk), lambda i, j, k: (i, k))
    if mode == "nt":
        b_spec = pl.BlockSpec((tn, tk), lambda i, j, k: (j, k))
    else:
        b_spec = pl.BlockSpec((tk, tn), lambda i, j, k: (k, j))
    o_spec = pl.BlockSpec((tm, tn), lambda i, j, k: (i, j))
    in_specs = [a_spec, b_spec] + ([o_spec] if add is not None else [])
    has_add = add is not None

    def kern(*refs):
        if has_add:
            a_ref, b_ref, add_ref, o_ref = refs[:4]
            rest = refs[4:]
        else:
            a_ref, b_ref, o_ref = refs[:3]
            add_ref = None
            rest = refs[3:]
        part = lax.dot_general(a_ref[...].astype(BF16), b_ref[...].astype(BF16), dn,
                               preferred_element_type=F32)

        def finish(acc):
            if has_add:
                acc = add_scale * add_ref[...] + acc
            o_ref[...] = acc.astype(out_dtype)

        if nk == 1:
            finish(part)
        else:
            acc_ref = rest[0]
            k = pl.program_id(2)

            @pl.when(k == 0)
            def _():
                acc_ref[...] = part

            @pl.when(k > 0)
            def _():
                acc_ref[...] += part

            @pl.when(k == nk - 1)
            def _():
                finish(acc_ref[...])

    scratch = [pltpu.VMEM((tm, tn), F32)] if nk > 1 else []
    args = (a, b) + ((add,) if has_add else ())
    return _pcall(kern, name=name, out_shape=jax.ShapeDtypeStruct((M, N), out_dtype),
                  grid=(M // tm, N // tn, nk), in_specs=in_specs, out_specs=o_spec, scratch=scratch,
                  dims=("parallel", "parallel", "arbitrary"), vmem_mb=vmem_mb)(*args)


def _ln_fwd(x, g, b, *, name):
    S, Dm = x.shape
    tm = min(512, S)

    def kern(x_ref, g_ref, b_ref, y_ref):
        xf = x_ref[...]
        mu = jnp.mean(xf, axis=-1, keepdims=True)
        xc = xf - mu
        var = jnp.mean(xc * xc, axis=-1, keepdims=True)
        y_ref[...] = xc * lax.rsqrt(var + LN_EPS) * g_ref[...] + b_ref[...]

    row = pl.BlockSpec((tm, Dm), lambda i: (i, 0))
    vec = pl.BlockSpec((1, Dm), lambda i: (0, 0))
    return _pcall(kern, name=name, out_shape=jax.ShapeDtypeStruct((S, Dm), F32), grid=(S // tm,),
                  in_specs=[row, vec, vec], out_specs=row, dims=("parallel",), vmem_mb=48)(
                      x, g.reshape(1, Dm), b.reshape(1, Dm))


def _ln_bwd(dy, r, g, *, name):
    S, Dm = r.shape
    tm = min(512, S)

    def kern(dy_ref, r_ref, g_ref, dr_ref, dg_ref, db_ref, ds_ref):
        @pl.when(pl.program_id(0) == 0)
        def _():
            dg_ref[...] = jnp.zeros_like(dg_ref)
            db_ref[...] = jnp.zeros_like(db_ref)
            ds_ref[...] = jnp.zeros_like(ds_ref)

        rf = r_ref[...]
        dyf = dy_ref[...]
        mu = jnp.mean(rf, axis=-1, keepdims=True)
        xc = rf - mu
        var = jnp.mean(xc * xc, axis=-1, keepdims=True)
        rstd = lax.rsqrt(var + LN_EPS)
        xhat = xc * rstd
        dxh = dyf * g_ref[...]
        c1 = jnp.mean(dxh, axis=-1, keepdims=True)
        c2 = jnp.mean(dxh * xhat, axis=-1, keepdims=True)
        dr = rstd * (dxh - c1 - xhat * c2)
        dr_ref[...] = dr
        dg_ref[...] += jnp.sum(dyf * xhat, axis=0, keepdims=True)
        db_ref[...] += jnp.sum(dyf, axis=0, keepdims=True)
        ds_ref[...] += jnp.sum(dr, axis=0, keepdims=True)

    row = pl.BlockSpec((tm, Dm), lambda i: (i, 0))
    vec = pl.BlockSpec((1, Dm), lambda i: (0, 0))
    vshape = jax.ShapeDtypeStruct((1, Dm), F32)
    return _pcall(kern, name=name, out_shape=(jax.ShapeDtypeStruct((S, Dm), F32), vshape, vshape, vshape),
                  grid=(S // tm,), in_specs=[row, row, vec], out_specs=(row, vec, vec, vec),
                  dims=("arbitrary",), vmem_mb=48)(dy, r, g.reshape(1, Dm))


def _loss_and_dy(y, target, *, name):
    S, Dm = y.shape
    tm = min(512, S)

    def kern(y_ref, t_ref, dy_ref, l_ref):
        @pl.when(pl.program_id(0) == 0)
        def _():
            l_ref[...] = jnp.zeros_like(l_ref)

        e = y_ref[...] - t_ref[...]
        dy_ref[...] = e / float(Dm)
        per_row = jnp.mean(e * e, axis=-1, keepdims=True)
        l_ref[...] += 0.5 * jnp.sum(per_row, axis=0, keepdims=True)

    row = pl.BlockSpec((tm, Dm), lambda i: (i, 0))
    acc = pl.BlockSpec((8, LANE), lambda i: (0, 0))
    return _pcall(kern, name=name,
                  out_shape=(jax.ShapeDtypeStruct((S, Dm), F32), jax.ShapeDtypeStruct((8, LANE), F32)),
                  grid=(S // tm,), in_specs=[row, row], out_specs=(row, acc), dims=("arbitrary",), vmem_mb=48)(
                      y, target)


def _rot_sum(t):
    return pltpu.roll(t, 32, 1) + pltpu.roll(t, 96, 1)


def _mla_qkv(proj, cos_t, sin_t, qg, kvg, wuq, wukv, *, name):
    S = proj.shape[0]
    tm = min(256, S)

    def kern(ql_ref, kvl_ref, kr_ref, cos_ref, sin_ref, qg_ref, kvg_ref, wuq_ref, wukv_ref,
             qc_ref, kc_ref, v_ref, qn_ref, kvn_ref):
        cosv = cos_ref[...]
        sinv = sin_ref[...]

        def rope(t):
            return t * cosv + _rot_sum(t) * sinv

        ql = ql_ref[...]
        qn = (ql * lax.rsqrt(jnp.mean(ql * ql, axis=-1, keepdims=True) + RMS_EPS) * qg_ref[...]).astype(BF16)
        kvl = kvl_ref[...]
        kvn = (kvl * lax.rsqrt(jnp.mean(kvl * kvl, axis=-1, keepdims=True) + RMS_EPS) * kvg_ref[...]).astype(BF16)
        qn_ref[...] = qn
        kvn_ref[...] = kvn
        q = jnp.dot(qn, wuq_ref[...], preferred_element_type=F32)
        kv = jnp.dot(kvn, wukv_ref[...], preferred_element_type=F32)
        kr = rope(kr_ref[...]).astype(BF16)
        for h in range(N_HEADS):
            c0 = 256 * h
            qc_ref[:, c0:c0 + 128] = q[:, c0:c0 + 128].astype(BF16)
            qc_ref[:, c0 + 128:c0 + 256] = rope(q[:, c0 + 128:c0 + 256]).astype(BF16)
            kc_ref[:, c0:c0 + 128] = kv[:, c0:c0 + 128].astype(BF16)
            kc_ref[:, c0 + 128:c0 + 256] = kr
            v_ref[:, 128 * h:128 * h + 128] = kv[:, c0 + 128:c0 + 256].astype(BF16)

    def row(w, blk):
        return pl.BlockSpec((tm, w), lambda i: (i, blk))

    def full(shape):
        return pl.BlockSpec(shape, lambda i: (0,) * len(shape))

    outs = (jax.ShapeDtypeStruct((S, 2048), BF16), jax.ShapeDtypeStruct((S, 2048), BF16),
            jax.ShapeDtypeStruct((S, 1024), BF16), jax.ShapeDtypeStruct((S, Q_LORA), BF16),
            jax.ShapeDtypeStruct((S, KV_LORA), BF16))
    return _pcall(kern, name=name, out_shape=outs, grid=(S // tm,),
                  in_specs=[row(512, 2), row(256, 18), row(128, 38), row(128, 0), row(128, 0),
                            full((1, Q_LORA)), full((1, KV_LORA)), full((Q_LORA, 2048)), full((KV_LORA, 2048))],
                  out_specs=(row(2048, 0), row(2048, 0), row(1024, 0), row(512, 0), row(256, 0)),
                  dims=("parallel",), vmem_mb=48)(
                      proj, proj, proj, cos_t, sin_t, qg.reshape(1, -1), kvg.reshape(1, -1), wuq, wukv)


def _mla_qkv_bwd(dqc, dkc, dv, proj, cos_t, sin_t, qg, kvg, wuq, wukv, *, name):
    S = proj.shape[0]
    tm = min(256, S)

    def kern(dq_ref, dk_ref, dv_ref, ql_ref, kvl_ref, cos_ref, sin_ref, qg_ref, kvg_ref, wuq_ref, wukv_ref,
             dqb_ref, dkvb_ref, dql_ref, dkvl_ref, dkr_ref, dqg_ref, dkvg_ref):
        @pl.when(pl.program_id(0) == 0)
        def _():
            dqg_ref[...] = jnp.zeros_like(dqg_ref)
            dkvg_ref[...] = jnp.zeros_like(dkvg_ref)

        cosv = cos_ref[...]
        sinv = sin_ref[...]

        def unrope(t):
            return t * cosv - _rot_sum(t) * sinv

        dkr = jnp.zeros((tm, 128), F32)
        for h in range(N_HEADS):
            c0 = 256 * h
            dqb_ref[:, c0:c0 + 128] = dq_ref[:, c0:c0 + 128].astype(BF16)
            dqb_ref[:, c0 + 128:c0 + 256] = unrope(dq_ref[:, c0 + 128:c0 + 256]).astype(BF16)
            dkvb_ref[:, c0:c0 + 128] = dk_ref[:, c0:c0 + 128].astype(BF16)
            dkvb_ref[:, c0 + 128:c0 + 256] = dv_ref[:, 128 * h:128 * h + 128].astype(BF16)
            dkr = dkr + dk_ref[:, c0 + 128:c0 + 256]
        dkr_ref[...] = unrope(dkr).astype(BF16)

        def rms_bwd(x, g, dy):
            n = x.shape[-1]
            rs = lax.rsqrt(jnp.mean(x * x, axis=-1, keepdims=True) + RMS_EPS)
            dyg = dy * g
            dx = rs * dyg - x * (rs * rs * rs) * (jnp.sum(dyg * x, axis=-1, keepdims=True) / n)
            return dx, jnp.sum(dy * (x * rs), axis=0, keepdims=True)

        dqn = lax.dot_general(dqb_ref[...], wuq_ref[...], NT, preferred_element_type=F32)
        dql, dqg = rms_bwd(ql_ref[...], qg_ref[...], dqn)
        dql_ref[...] = dql.astype(BF16)
        dqg_ref[...] += dqg
        dkvn = lax.dot_general(dkvb_ref[...], wukv_ref[...], NT, preferred_element_type=F32)
        dkvl, dkvg = rms_bwd(kvl_ref[...], kvg_ref[...], dkvn)
        dkvl_ref[...] = dkvl.astype(BF16)
        dkvg_ref[...] += dkvg

    def row(w, blk):
        return pl.BlockSpec((tm, w), lambda i: (i, blk))

    def full(shape):
        return pl.BlockSpec(shape, lambda i: (0,) * len(shape))

    outs = (jax.ShapeDtypeStruct((S, 2048), BF16), jax.ShapeDtypeStruct((S, 2048), BF16),
            jax.ShapeDtypeStruct((S, Q_LORA), BF16), jax.ShapeDtypeStruct((S, KV_LORA), BF16),
            jax.ShapeDtypeStruct((S, 128), BF16), jax.ShapeDtypeStruct((1, Q_LORA), F32),
            jax.ShapeDtypeStruct((1, KV_LORA), F32))
    return _pcall(kern, name=name, out_shape=outs, grid=(S // tm,),
                  in_specs=[row(2048, 0), row(2048, 0), row(1024, 0), row(512, 2), row(256, 18),
                            row(128, 0), row(128, 0), full((1, Q_LORA)), full((1, KV_LORA)),
                            full((Q_LORA, 2048)), full((KV_LORA, 2048))],
                  out_specs=(row(2048, 0), row(2048, 0), row(512, 0), row(256, 0), row(128, 0),
                             full((1, Q_LORA)), full((1, KV_LORA))),
                  dims=("arbitrary",), vmem_mb=56)(
                      dqc, dkc, dv, proj, proj, cos_t, sin_t, qg.reshape(1, -1), kvg.reshape(1, -1), wuq, wukv)


LOG2E = 1.4426950408889634
SCALE_LOG2E = SCALE * LOG2E
ROWS = 128


def _row_to_lanes(col):
    n = col.shape[0]
    return jnp.transpose(jnp.broadcast_to(col, (n, LANE)))[0:1, :]


def _flash_fwd(qc, kc, v, *, name):
    S = qc.shape[0]
    t = min(512, S)
    n = S // t
    nr = t // ROWS

    def kern(q_ref, k_ref, v_ref, o_ref, lse_ref, m_s, l_s, acc_s):
        qi = pl.program_id(1)
        m_s[...] = jnp.full_like(m_s, -jnp.inf)
        l_s[...] = jnp.zeros_like(l_s)
        acc_s[...] = jnp.zeros_like(acc_s)

        def update(r, kblk, vblk, mask):
            rows = slice(r * ROWS, (r + 1) * ROWS)
            s = lax.dot_general(q_ref[rows, :], kblk, NT, preferred_element_type=F32)
            if mask is not None:
                s = jnp.where(mask, s, -jnp.inf)
            m_prev = m_s[rows, :]
            m_new = jnp.maximum(m_prev, jnp.max(s, axis=1, keepdims=True))
            a = jnp.exp2((m_prev - m_new) * SCALE_LOG2E)
            p = jnp.exp2((s - m_new) * SCALE_LOG2E)
            l_s[rows, :] = a * l_s[rows, :] + jnp.sum(p, axis=1, keepdims=True)
            acc_s[rows, :] = a * acc_s[rows, :] + jnp.dot(p.astype(BF16), vblk, preferred_element_type=F32)
            m_s[rows, :] = m_new

        def body(kb, carry):
            k0 = pl.multiple_of(kb * t, t)
            kblk = k_ref[pl.ds(k0, t), :]
            vblk = v_ref[pl.ds(k0, t), :]
            for r in range(nr):
                update(r, kblk, vblk, None)
            return carry

        lax.fori_loop(0, qi, body, 0)
        d0 = pl.multiple_of(qi * t, t)
        for r in range(nr):
            w = (r + 1) * ROWS
            row = lax.broadcasted_iota(jnp.int32, (ROWS, w), 0) + r * ROWS
            col = lax.broadcasted_iota(jnp.int32, (ROWS, w), 1)
            update(r, k_ref[pl.ds(d0, w), :], v_ref[pl.ds(d0, w), :], col <= row)
        o_ref[...] = acc_s[...] / l_s[...]
        for r in range(nr):
            rows = slice(r * ROWS, (r + 1) * ROWS)
            lse2 = m_s[rows, :] * SCALE_LOG2E + jnp.log2(l_s[rows, :])
            lse_ref[pl.ds(qi * nr + r, 1), :] = _row_to_lanes(lse2)

    q_spec = pl.BlockSpec((t, 256), lambda h, qi: (qi, h))
    k_spec = pl.BlockSpec((S, 256), lambda h, qi: (0, h))
    v_spec = pl.BlockSpec((S, 128), lambda h, qi: (0, h))
    o_spec = pl.BlockSpec((t, 128), lambda h, qi: (qi, h))
    lse_spec = pl.BlockSpec((None, S // ROWS, ROWS), lambda h, qi: (h, 0, 0))
    return _pcall(kern, name=name,
                  out_shape=(jax.ShapeDtypeStruct((S, D_MLA), F32),
                             jax.ShapeDtypeStruct((N_HEADS, S // ROWS, ROWS), F32)),
                  grid=(N_HEADS, n), in_specs=[q_spec, k_spec, v_spec], out_specs=(o_spec, lse_spec),
                  scratch=[pltpu.VMEM((t, 1), F32), pltpu.VMEM((t, 1), F32), pltpu.VMEM((t, 128), F32)],
                  dims=("parallel", "arbitrary"), vmem_mb=48)(qc, kc, v)


def _flash_bwd(qc, kc, v, do, lse2, delta, *, name):
    S = qc.shape[0]
    t = min(512, S)
    n = S // t
    nr = t // ROWS

    def kern(q_ref, k_ref, v_ref, do_ref, lse_ref, dl_ref, dq_ref, dk_ref, dv_ref):
        ki = pl.program_id(1)

        @pl.when(ki == 0)
        def _():
            dq_ref[...] = jnp.zeros_like(dq_ref)

        dk_ref[...] = jnp.zeros_like(dk_ref)
        dv_ref[...] = jnp.zeros_like(dv_ref)

        def update(chunk, kw, mask):
            q0 = pl.multiple_of(chunk * ROWS, ROWS)
            kt = k_ref[0:kw, :]
            qc_ = q_ref[pl.ds(q0, ROWS), :]
            dob = do_ref[pl.ds(q0, ROWS), :].astype(BF16)
            st = lax.dot_general(kt, qc_, NT, preferred_element_type=F32)
            pt = jnp.exp2(st * SCALE_LOG2E - lse_ref[pl.ds(chunk, 1), :])
            if mask is not None:
                pt = jnp.where(mask, pt, 0.0)
            dv_ref[0:kw, :] += jnp.dot(pt.astype(BF16), dob, preferred_element_type=F32)
            dpt = lax.dot_general(v_ref[0:kw, :], dob, NT, preferred_element_type=F32)
            dst = (pt * (dpt - dl_ref[pl.ds(chunk, 1), :]) * SCALE).astype(BF16)
            dk_ref[0:kw, :] += jnp.dot(dst, qc_, preferred_element_type=F32)
            dq_ref[pl.ds(q0, ROWS), :] += lax.dot_general(dst, kt, TN, preferred_element_type=F32)

        for r in range(nr):
            kw = (r + 1) * ROWS
            krow_i = lax.broadcasted_iota(jnp.int32, (kw, ROWS), 0)
            qcol_i = lax.broadcasted_iota(jnp.int32, (kw, ROWS), 1) + r * ROWS
            update(ki * nr + r, kw, krow_i <= qcol_i)

        def body(qb, carry):
            for r in range(nr):
                update(qb * nr + r, t, None)
            return carry

        lax.fori_loop(ki + 1, n, body, 0)

    def whole(w):
        return pl.BlockSpec((S, w), lambda h, ki: (0, h))

    def krow(w):
        return pl.BlockSpec((t, w), lambda h, ki: (ki, h))

    stat = pl.BlockSpec((None, S // ROWS, ROWS), lambda h, ki: (h, 0, 0))
    return _pcall(kern, name=name,
                  out_shape=(jax.ShapeDtypeStruct((S, 2048), F32), jax.ShapeDtypeStruct((S, 2048), F32),
                             jax.ShapeDtypeStruct((S, D_MLA), F32)),
                  grid=(N_HEADS, n),
                  in_specs=[whole(256), krow(256), krow(128), whole(128), stat, stat],
                  out_specs=(whole(256), krow(256), krow(128)),
                  dims=("parallel", "arbitrary"), vmem_mb=56)(qc, kc, v, do, lse2, delta)


def _mixer_specs(S, tm):
    hb = tm // HALO
    last_hb = S // HALO - 1

    def main(w, blk):
        return pl.BlockSpec((tm, w), lambda i: (i, blk))

    def prev(w, blk):
        return pl.BlockSpec((HALO, w), lambda i: (jnp.maximum(i * hb - 1, 0), blk))

    def nxt(w, blk):
        return pl.BlockSpec((HALO, w), lambda i: (jnp.minimum((i + 1) * hb, last_hb), blk))

    def full(shape):
        return pl.BlockSpec(shape, lambda i: (0,) * len(shape))

    return main, prev, nxt, full


def _fill_halo(i, xp, xu, hp_ref, hch_ref, hcc_ref, pin_ref, ch_ref, cc_ref, tm):
    first = i == 0
    xp[0:HALO, :] = jnp.where(first, 0.0, hp_ref[...])
    xp[HALO:HALO + tm, :] = pin_ref[...]
    xu[0:HALO, :] = jnp.where(first, 0.0, hch_ref[...] * hcc_ref[...])
    xu[HALO:HALO + tm, :] = cc_ref[...] * ch_ref[...]


def _pooled(xp, g, t1, tm):
    w = POOL_WINDOWS[g]
    lanes = slice(128 * g, 128 * g + 128)
    x0 = xp[HALO:HALO + tm, lanes]
    acc = x0
    for k in range(1, w):
        acc = acc + xp[HALO - k:HALO - k + tm, lanes]
    return acc / jnp.minimum(t1, float(w)) - x0


def _conv_fwd(xu, cw_ref, tm):
    return (cw_ref[0:1, :] * xu[HALO - 2:HALO - 2 + tm, :] + cw_ref[1:2, :] * xu[HALO - 1:HALO - 1 + tm, :]
            + cw_ref[2:3, :] * xu[HALO:HALO + tm, :])


def _mixer_fwd(proj, o, wpool, ps, convw, *, name):
    S = proj.shape[0]
    tm = min(256, S)
    main, prev, _, full = _mixer_specs(S, tm)

    def kern(gm_ref, pin_ref, gp_ref, ch_ref, cb_ref, cc_ref, gc_ref, hp_ref, hch_ref, hcc_ref,
             o_ref, wp_ref, ps_ref, cw_ref, mix_ref, xp, xu):
        i = pl.program_id(0)
        _fill_halo(i, xp, xu, hp_ref, hch_ref, hcc_ref, pin_ref, ch_ref, cc_ref, tm)
        t1 = (i * tm + lax.broadcasted_iota(jnp.int32, (tm, 1), 0) + 1).astype(F32)
        for g in range(4):
            lanes = slice(128 * g, 128 * g + 128)
            pooled = _pooled(xp, g, t1, tm)
            z = jnp.dot(pooled.astype(BF16), wp_ref[g].astype(BF16), preferred_element_type=F32)
            gp = gp_ref[:, lanes]
            y = z * ps_ref[:, lanes] * (gp * _sigmoid(gp))
            mix_ref[:, 1024 + 128 * g:1024 + 128 * g + 128] = y.astype(BF16)
        gc = gc_ref[...]
        mix_ref[:, 1536:2048] = (cb_ref[...] * _conv_fwd(xu, cw_ref, tm) * (gc * _sigmoid(gc))).astype(BF16)
        gm = gm_ref[...]
        mix_ref[:, 0:1024] = (o_ref[...] * (gm * _sigmoid(gm))).astype(BF16)

    return _pcall(kern, name=name, out_shape=jax.ShapeDtypeStruct((S, 2048), BF16), grid=(S // tm,),
                  in_specs=[main(1024, 0), main(512, 3), main(512, 4), main(512, 5), main(512, 6), main(512, 7),
                            main(512, 8), prev(512, 3), prev(512, 5), prev(512, 7),
                            main(1024, 0), full((4, 128, 128)), full((1, 512)), full((3, 512))],
                  out_specs=main(2048, 0),
                  scratch=[pltpu.VMEM((tm + HALO, 512), F32), pltpu.VMEM((tm + HALO, 512), F32)],
                  dims=("parallel",), vmem_mb=48)(
                      proj, proj, proj, proj, proj, proj, proj, proj, proj, proj, o, wpool, ps.reshape(1, 512), convw)


def _mixer_bwd(dmix, proj, o, wpool, ps, convw, *, name):
    S = proj.shape[0]
    tm = min(256, S)
    n = S // tm
    main, prev, nxt, full = _mixer_specs(S, tm)

    def kern(dm_ref, dmn_ref, gm_ref, pin_ref, gp_ref, ch_ref, cb_ref, cc_ref, gc_ref,
             hp_ref, hch_ref, hcc_ref, gpn_ref, cbn_ref, gcn_ref, o_ref, wp_ref, ps_ref, cw_ref,
             d3_ref, dgm_ref, do_ref, dl_ref, dwp_ref, dps_ref, dcw_ref, xp, xu, ee, ed):
        i = pl.program_id(0)
        last = i == n - 1

        @pl.when(i == 0)
        def _():
            dwp_ref[...] = jnp.zeros_like(dwp_ref)
            dps_ref[...] = jnp.zeros_like(dps_ref)
            dcw_ref[...] = jnp.zeros_like(dcw_ref)

        _fill_halo(i, xp, xu, hp_ref, hch_ref, hcc_ref, pin_ref, ch_ref, cc_ref, tm)
        t1 = (i * tm + lax.broadcasted_iota(jnp.int32, (tm, 1), 0) + 1).astype(F32)
        t1n = ((i + 1) * tm + lax.broadcasted_iota(jnp.int32, (HALO, 1), 0) + 1).astype(F32)

        for g in range(4):
            w = float(POOL_WINDOWS[g])
            lanes = slice(128 * g, 128 * g + 128)
            pooled = _pooled(xp, g, t1, tm)
            pb = pooled.astype(BF16)
            wp = wp_ref[g].astype(BF16)
            z = jnp.dot(pb, wp, preferred_element_type=F32)
            psl = ps_ref[:, lanes]
            sg, dsg = _silu_and_grad(gp_ref[:, lanes])
            dmp = dm_ref[:, 1024 + 128 * g:1024 + 128 * g + 128]
            dyp = dmp * sg
            d3_ref[:, 512 + 128 * g:512 + 128 * g + 128] = (dmp * (z * psl) * dsg).astype(BF16)
            dps_ref[:, lanes] += jnp.sum(dyp * z, axis=0, keepdims=True)
            dz = (dyp * psl).astype(BF16)
            dwp_ref[g] += lax.dot_general(pb, dz, TN, preferred_element_type=F32)
            dpl = lax.dot_general(dz, wp, NT, preferred_element_type=F32)
            ee[0:tm, lanes] = dpl / jnp.minimum(t1, w)
            gpn = gpn_ref[:, lanes]
            dzn = (dmn_ref[:, lanes] * (gpn * _sigmoid(gpn)) * psl).astype(BF16)
            dpn = lax.dot_general(dzn, wp, NT, preferred_element_type=F32)
            ee[tm:tm + HALO, lanes] = jnp.where(last, 0.0, dpn / jnp.minimum(t1n, w))
            acc = ee[0:tm, lanes]
            for k in range(1, POOL_WINDOWS[g]):
                acc = acc + ee[k:k + tm, lanes]
            d3_ref[:, lanes] = (acc - dpl).astype(BF16)

        yc = _conv_fwd(xu, cw_ref, tm)
        sgc, dsgc = _silu_and_grad(gc_ref[...])
        cb = cb_ref[...]
        dmc = dm_ref[:, 1536:2048]
        d3_ref[:, 2560:3072] = (dmc * cb * yc * dsgc).astype(BF16)
        d3_ref[:, 1536:2048] = (dmc * yc * sgc).astype(BF16)
        dyc = dmc * cb * sgc
        ed[0:tm, :] = dyc
        gcn = gcn_ref[...]
        ed[tm:tm + HALO, :] = jnp.where(last, 0.0, dmn_ref[:, 512:1024] * cbn_ref[...] * (gcn * _sigmoid(gcn)))
        dcw_ref[0:1, :] += jnp.sum(dyc * xu[HALO - 2:HALO - 2 + tm, :], axis=0, keepdims=True)
        dcw_ref[1:2, :] += jnp.sum(dyc * xu[HALO - 1:HALO - 1 + tm, :], axis=0, keepdims=True)
        dcw_ref[2:3, :] += jnp.sum(dyc * xu[HALO:HALO + tm, :], axis=0, keepdims=True)
        du = cw_ref[2:3, :] * dyc + cw_ref[1:2, :] * ed[1:1 + tm, :] + cw_ref[0:1, :] * ed[2:2 + tm, :]
        d3_ref[:, 2048:2560] = (du * ch_ref[...]).astype(BF16)
        d3_ref[:, 1024:1536] = (du * cc_ref[...]).astype(BF16)

        sgm, dsgm = _silu_and_grad(gm_ref[...])
        dmm = dm_ref[:, 0:1024]
        ov = o_ref[...]
        dov = dmm * sgm
        do_ref[...] = dov
        dgm_ref[...] = (dmm * ov * dsgm).astype(BF16)
        lane = lax.broadcasted_iota(jnp.int32, (tm, LANE), 1)
        dmat = jnp.zeros((tm, LANE), F32)
        for h in range(N_HEADS):
            hs = slice(128 * h, 128 * h + 128)
            dmat = jnp.where(lane == h, jnp.sum(dov[:, hs] * ov[:, hs], axis=1, keepdims=True), dmat)
        dmat_t = jnp.transpose(dmat)
        for h in range(N_HEADS):
            for r in range(tm // ROWS):
                dl_ref[h, pl.ds(i * (tm // ROWS) + r, 1), :] = dmat_t[h:h + 1, r * ROWS:(r + 1) * ROWS]

    outs = (jax.ShapeDtypeStruct((S, 3072), BF16), jax.ShapeDtypeStruct((S, 1024), BF16),
            jax.ShapeDtypeStruct((S, 1024), F32), jax.ShapeDtypeStruct((N_HEADS, S // ROWS, ROWS), F32),
            jax.ShapeDtypeStruct((4, 128, 128), F32),
            jax.ShapeDtypeStruct((1, 512), F32), jax.ShapeDtypeStruct((3, 512), F32))
    scr = [pltpu.VMEM((tm + HALO, 512), F32) for _ in range(4)]
    return _pcall(kern, name=name, out_shape=outs, grid=(n,),
                  in_specs=[main(2048, 0), nxt(1024, 1),
                            main(1024, 0), main(512, 3), main(512, 4), main(512, 5), main(512, 6), main(512, 7),
                            main(512, 8), prev(512, 3), prev(512, 5), prev(512, 7),
                            nxt(512, 4), nxt(512, 6), nxt(512, 8),
                            main(1024, 0), full((4, 128, 128)), full((1, 512)), full((3, 512))],
                  out_specs=(main(3072, 0), main(1024, 0), main(1024, 0), full((N_HEADS, S // ROWS, ROWS)),
                             full((4, 128, 128)), full((1, 512)), full((3, 512))),
                  scratch=scr, dims=("arbitrary",), vmem_mb=56)(
                      dmix, dmix, proj, proj, proj, proj, proj, proj, proj, proj, proj, proj, proj, proj, proj,
                      o, wpool, ps.reshape(1, 512), convw)


def _outproj_ln(mix, wout, h, bout, g, b, *, name):
    S, Dm = h.shape
    tm = min(256, S)

    def kern(mix_ref, w_ref, h_ref, bo_ref, g_ref, b_ref, y_ref, r_ref):
        out = jnp.dot(mix_ref[...], w_ref[...], preferred_element_type=F32) + bo_ref[...]
        r = ALPHA * h_ref[...] + out
        r_ref[...] = r
        mu = jnp.mean(r, axis=-1, keepdims=True)
        xc = r - mu
        var = jnp.mean(xc * xc, axis=-1, keepdims=True)
        y_ref[...] = xc * lax.rsqrt(var + LN_EPS) * g_ref[...] + b_ref[...]

    row = pl.BlockSpec((tm, Dm), lambda i: (i, 0))
    vec = pl.BlockSpec((1, Dm), lambda i: (0, 0))
    wsp = pl.BlockSpec((Dm, Dm), lambda i: (0, 0))
    sds = jax.ShapeDtypeStruct((S, Dm), F32)
    return _pcall(kern, name=name, out_shape=(sds, sds), grid=(S // tm,),
                  in_specs=[row, wsp, row, vec, vec, vec], out_specs=(row, row), dims=("parallel",), vmem_mb=56)(
                      mix, wout, h, bout.reshape(1, Dm), g.reshape(1, Dm), b.reshape(1, Dm))


def _adamw_math(w, g, m, v):
    m = ADAM_B1 * m + (1.0 - ADAM_B1) * g
    v = ADAM_B2 * v + (1.0 - ADAM_B2) * (g * g)
    m_hat = m / (1.0 - ADAM_B1 ** ADAM_STEP)
    v_hat = v / (1.0 - ADAM_B2 ** ADAM_STEP)
    delta = -ADAM_LR * (m_hat / (jnp.sqrt(v_hat) + ADAM_EPS) + ADAM_WD * w)
    return delta, m, v


def _adamw(w, g, m, v, *, name):
    shape = w.shape
    C = shape[-1]
    R = 1
    for s in shape[:-1]:
        R *= s
    tr = R
    for cand in (512, 256, 128, 64, 32, 16, 8):
        if R % cand == 0 and R > cand and cand * C <= 256 * 1024:
            tr = cand
            break

    def kern(w_ref, g_ref, m_ref, v_ref, d_ref, mo_ref, vo_ref):
        d, mn, vn = _adamw_math(w_ref[...], g_ref[...], m_ref[...], v_ref[...])
        d_ref[...] = d
        mo_ref[...] = mn
        vo_ref[...] = vn

    blk = pl.BlockSpec((tr, C), lambda i: (i, 0))
    sds = jax.ShapeDtypeStruct((R, C), F32)
    outs = _pcall(kern, name=name, out_shape=(sds, sds, sds), grid=(R // tr,), in_specs=[blk] * 4,
                  out_specs=(blk, blk, blk), dims=("parallel",), vmem_mb=48)(
                      w.reshape(R, C), g.reshape(R, C), m.reshape(R, C), v.reshape(R, C))
    return tuple(t.reshape(shape) for t in outs)


def _adamw_pair(w, mine, theirs, m, v, c_idx, *, name):
    _, R, C = w.shape
    tr = R
    for cand in (512, 256, 128, 64, 32, 16, 8):
        if R % cand == 0 and cand * C <= 256 * 1024:
            tr = cand
            break
    nb = R // tr

    def kern(c_ref, w_ref, a_ref, b_ref, m_ref, v_ref, g_ref, d_ref, mo_ref, vo_ref):
        layer = pl.program_id(0) // nb
        g = jnp.where(layer == c_ref[0], a_ref[...], b_ref[...])
        g_ref[...] = g
        d, mn, vn = _adamw_math(w_ref[...], g, m_ref[...], v_ref[...])
        d_ref[...] = d
        mo_ref[...] = mn
        vo_ref[...] = vn

    full = pl.BlockSpec((tr, C), lambda i, c: (i, 0))
    half = pl.BlockSpec((tr, C), lambda i, c: (i % nb, 0))
    gs = pltpu.PrefetchScalarGridSpec(num_scalar_prefetch=1, grid=(2 * nb,),
                                      in_specs=[full, half, half, full, full], out_specs=(full,) * 4)
    sds = jax.ShapeDtypeStruct((2 * R, C), F32)
    outs = pl.pallas_call(kern, name=name, out_shape=(sds,) * 4, grid_spec=gs,
                          compiler_params=pltpu.CompilerParams(dimension_semantics=("parallel",),
                                                               vmem_limit_bytes=48 << 20))(
                              c_idx, w.reshape(2 * R, C), mine, theirs, m.reshape(2 * R, C), v.reshape(2 * R, C))
    return tuple(t.reshape(2, R, C) for t in outs)


def _small_sum_adamw(gathered, w, m, v, *, name):
    R = w.shape[0]

    def kern(ga_ref, w_ref, m_ref, v_ref, g_ref, d_ref, mo_ref, vo_ref):
        g = ga_ref[0]
        for k in range(1, N_DEV):
            g = g + ga_ref[k]
        g_ref[...] = g
        d, mn, vn = _adamw_math(w_ref[...], g, m_ref[...], v_ref[...])
        d_ref[...] = d
        mo_ref[...] = mn
        vo_ref[...] = vn

    sds = jax.ShapeDtypeStruct((R, LANE), F32)
    return _pcall(kern, name=name, out_shape=(sds, sds, sds, sds), vmem_mb=48)(gathered, w, m, v)


def _add2(a, b, c_idx, *, name):
    _, R, C = a.shape
    tr = 256 if R % 256 == 0 else R

    def kern(c_ref, a_ref, b_ref, o_ref):
        o_ref[...] = (a_ref[...] + b_ref[...]).astype(BF16)

    gs = pltpu.PrefetchScalarGridSpec(
        num_scalar_prefetch=1, grid=(R // tr,),
        in_specs=[pl.BlockSpec((None, tr, C), lambda i, c: (c[0], i, 0)), pl.BlockSpec((tr, C), lambda i, c: (i, 0))],
        out_specs=pl.BlockSpec((tr, C), lambda i, c: (i, 0)))
    return pl.pallas_call(kern, name=name, out_shape=jax.ShapeDtypeStruct((R, C), BF16), grid_spec=gs,
                          compiler_params=pltpu.CompilerParams(dimension_semantics=("parallel",),
                                                               vmem_limit_bytes=48 << 20))(c_idx, a, b)


def _add4(p, r2, j_idx, *, name):
    _, R, C = p.shape
    tr = 256 if R % 256 == 0 else R

    def kern(j_ref, p_ref, r_ref, o_ref):
        o_ref[...] = ((p_ref[...].astype(F32) + r_ref[0].astype(F32)) + r_ref[1].astype(F32)) + r_ref[2].astype(F32)

    gs = pltpu.PrefetchScalarGridSpec(
        num_scalar_prefetch=1, grid=(R // tr,),
        in_specs=[pl.BlockSpec((None, tr, C), lambda i, j: (j[0], i, 0)),
                  pl.BlockSpec((3, tr, C), lambda i, j: (0, i, 0))],
        out_specs=pl.BlockSpec((tr, C), lambda i, j: (i, 0)))
    return pl.pallas_call(kern, name=name, out_shape=jax.ShapeDtypeStruct((R, C), F32), grid_spec=gs,
                          compiler_params=pltpu.CompilerParams(dimension_semantics=("parallel",),
                                                               vmem_limit_bytes=48 << 20))(j_idx, p, r2)


HBM_SPEC = pl.BlockSpec(memory_space=pl.ANY)


def _mesh_pos():
    x, y, c = lax.axis_index("x"), lax.axis_index("y"), lax.axis_index("c")
    return x, y, c


def _other_chips(x, y):
    return [(1 - x, y), (x, 1 - y), (1 - x, 1 - y)]


def _allgather_weights(shards, *, name):
    na = len(shards)

    def body(*refs):
        ins, outs = refs[:na], refs[na:2 * na]
        send_sems, recv_sems = refs[2 * na:]
        x, y, c = _mesh_pos()
        j = 2 * x + y
        chips = _other_chips(x, y)
        sibling = (x, y, 1 - c)
        own = [pltpu.make_async_remote_copy(
            src_ref=ins[a], dst_ref=outs[a].at[j], send_sem=send_sems.at[6 * na + a],
            recv_sem=recv_sems.at[6 * na + a], device_id=sibling, device_id_type=MESH) for a in range(na)]
        for cp in own:
            cp.start()

        def first(a, k):
            return pltpu.make_async_remote_copy(
                src_ref=ins[a].at[c], dst_ref=outs[a].at[j, c], send_sem=send_sems.at[3 * a + k],
                recv_sem=recv_sems.at[3 * a + k], device_id=(*chips[k], c), device_id_type=MESH)

        def passed(a, k, layer):
            pk = 2 * chips[k][0] + chips[k][1]
            return pltpu.make_async_remote_copy(
                src_ref=outs[a].at[pk, layer], dst_ref=outs[a].at[pk, layer],
                send_sem=send_sems.at[3 * na + 3 * a + k], recv_sem=recv_sems.at[3 * na + 3 * a + k],
                device_id=sibling, device_id_type=MESH)

        for a in range(na):
            for k in range(3):
                first(a, k).start()
        for a in range(na):
            for k in range(3):
                pk = 2 * chips[k][0] + chips[k][1]
                pltpu.make_async_remote_copy(
                    src_ref=ins[a].at[c], dst_ref=outs[a].at[pk, c], send_sem=send_sems.at[3 * a + k],
                    recv_sem=recv_sems.at[3 * a + k], device_id=(*chips[k], c), device_id_type=MESH).wait_recv()
                passed(a, k, c).start()
        for a in range(na):
            for k in range(3):
                passed(a, k, 1 - c).wait_recv()
        for a in range(na):
            for k in range(3):
                first(a, k).wait_send()
                passed(a, k, c).wait_send()
        for cp in own:
            cp.wait()

    out_shape = tuple(jax.ShapeDtypeStruct((N_CHIPS,) + s.shape, s.dtype) for s in shards)
    return _pcall(body, name=name, out_shape=out_shape, in_specs=[HBM_SPEC] * na, out_specs=(HBM_SPEC,) * na,
                  scratch=[pltpu.SemaphoreType.DMA((7 * na,)), pltpu.SemaphoreType.DMA((7 * na,))])(*shards)


def _exchange_layers(grads, *, name):
    na = len(grads)

    def body(*refs):
        ins, outs = refs[:na], refs[na:2 * na]
        send_sems, recv_sems = refs[2 * na:]
        x, y, c = _mesh_pos()
        copies = [pltpu.make_async_remote_copy(
            src_ref=ins[a].at[1 - c], dst_ref=outs[a], send_sem=send_sems.at[a], recv_sem=recv_sems.at[a],
            device_id=(x, y, 1 - c), device_id_type=MESH) for a in range(na)]
        for cp in copies:
            cp.start()
        for cp in copies:
            cp.wait()

    out_shape = tuple(jax.ShapeDtypeStruct(g.shape[1:], g.dtype) for g in grads)
    return _pcall(body, name=name, out_shape=out_shape, in_specs=[HBM_SPEC] * na, out_specs=(HBM_SPEC,) * na,
                  scratch=[pltpu.SemaphoreType.DMA((na,)), pltpu.SemaphoreType.DMA((na,))])(*grads)


def _scatter_chips(parts, *, name):
    na = len(parts)

    def body(*refs):
        ins, outs = refs[:na], refs[na:2 * na]
        send_sems, recv_sems = refs[2 * na:]
        x, y, c = _mesh_pos()
        chips = _other_chips(x, y)
        copies = []
        for a in range(na):
            for k in range(3):
                pk = 2 * chips[k][0] + chips[k][1]
                copies.append(pltpu.make_async_remote_copy(
                    src_ref=ins[a].at[pk], dst_ref=outs[a].at[k], send_sem=send_sems.at[3 * a + k],
                    recv_sem=recv_sems.at[3 * a + k], device_id=(*chips[k], c), device_id_type=MESH))
        for cp in copies:
            cp.start()
        for cp in copies:
            cp.wait()

    out_shape = tuple(jax.ShapeDtypeStruct((3,) + p.shape[1:], p.dtype) for p in parts)
    return _pcall(body, name=name, out_shape=out_shape, in_specs=[HBM_SPEC] * na, out_specs=(HBM_SPEC,) * na,
                  scratch=[pltpu.SemaphoreType.DMA((3 * na,)), pltpu.SemaphoreType.DMA((3 * na,))])(*parts)


def _send_to_sibling(sums, *, name):
    na = len(sums)

    def body(*refs):
        ins, outs = refs[:na], refs[na:2 * na]
        send_sems, recv_sems = refs[2 * na:]
        x, y, c = _mesh_pos()
        copies = [pltpu.make_async_remote_copy(
            src_ref=ins[a], dst_ref=outs[a], send_sem=send_sems.at[a], recv_sem=recv_sems.at[a],
            device_id=(x, y, 1 - c), device_id_type=MESH) for a in range(na)]
        for cp in copies:
            cp.start()
        for cp in copies:
            cp.wait()

    out_shape = tuple(jax.ShapeDtypeStruct(t.shape, t.dtype) for t in sums)
    return _pcall(body, name=name, out_shape=out_shape, in_specs=[HBM_SPEC] * na, out_specs=(HBM_SPEC,) * na,
                  scratch=[pltpu.SemaphoreType.DMA((na,)), pltpu.SemaphoreType.DMA((na,))])(*sums)


def _allgather_small(block, *, name):
    m_per, n = block.shape

    def body(x_ref, out_ref, send_sems, recv_sems, local_sem):
        x, y, c = _mesh_pos()
        me, sibling = (x, y, c), (x, y, 1 - c)
        chips = _other_chips(x, y)

        def rows(px, py, pc):
            return out_ref.at[4 * px + 2 * py + pc]

        def copy(k, blk, to, src=None):
            return pltpu.make_async_remote_copy(
                src_ref=rows(*blk) if src is None else src, dst_ref=rows(*blk), send_sem=send_sems.at[k],
                recv_sem=recv_sems.at[k], device_id=to, device_id_type=MESH)

        mine = pltpu.make_async_copy(x_ref, rows(*me), local_sem)
        mine.start()
        first = [copy(0, me, sibling, src=x_ref)]
        first += [copy(1 + k, me, (*chip, c), src=x_ref) for k, chip in enumerate(chips)]
        for cp in first:
            cp.start()
        passed = [copy(4 + k, (*chip, c), sibling) for k, chip in enumerate(chips)]
        for k, chip in enumerate(chips):
            copy(1 + k, (*chip, c), me).wait_recv()
            passed[k].start()
        copy(0, sibling, me).wait_recv()
        for k, chip in enumerate(chips):
            copy(4 + k, (*chip, 1 - c), me).wait_recv()
        for cp in first + passed:
            cp.wait_send()
        mine.wait()

    vm = pl.BlockSpec(memory_space=pltpu.VMEM)
    return _pcall(body, name=name, out_shape=jax.ShapeDtypeStruct((N_DEV, m_per, n), block.dtype),
                  in_specs=[vm], out_specs=vm,
                  scratch=[pltpu.SemaphoreType.DMA((7,)), pltpu.SemaphoreType.DMA((7,)), pltpu.SemaphoreType.DMA],
                  vmem_mb=48)(block)


def _w_in_to_p(w):
    pad = jnp.zeros(w.shape[:-1] + (NP - D_IN_PROJ,), w.dtype)
    return jnp.concatenate([w[..., 832:1856], w[..., 0:512], w[..., 1856:4928], w[..., 512:768], w[..., 768:832], pad],
                           axis=-1)


def _w_in_from_p(g):
    return jnp.concatenate([g[..., 1024:1536], g[..., 4608:4864], g[..., 4864:4928], g[..., 0:1024], g[..., 1536:4608]],
                           axis=-1)


def _w_uq_to_p(w):
    k = w.shape[0]
    return jnp.pad(w.reshape(k, N_HEADS, NOPE + ROPE), ((0, 0), (0, 0), (0, 64))).reshape(k, N_HEADS * 256)


def _w_uq_from_p(g):
    k = g.shape[0]
    return g.reshape(k, N_HEADS, 256)[:, :, :NOPE + ROPE].reshape(k, N_HEADS * (NOPE + ROPE))


def _rope_tables(positions):
    half = ROPE // 2
    inv_freq = ROPE_THETA ** (-jnp.arange(half, dtype=F32) / half)
    ang = positions.astype(F32)[:, None] * inv_freq
    cos, sin = jnp.cos(ang), jnp.sin(ang)
    S = positions.shape[0]
    cos_t = jnp.concatenate([cos, cos, jnp.ones((S, 64), F32)], axis=1)
    sin_t = jnp.concatenate([-sin, sin, jnp.zeros((S, 64), F32)], axis=1)
    return cos_t, sin_t


def _local_step(x, positions, target, emb_g, emb_b, w_in_p, q_g, kv_g, w_uq_p, w_ukv, w_pool, pool_scale, conv_w,
                w_out, b_out, ln_g, ln_b):
    cos_t, sin_t = _rope_tables(positions)
    h = _ln_fwd(x, emb_g, emb_b, name="emb_ln")
    saved = []
    for l in range(DEPTH):
        proj = _matmul(h, w_in_p[l], "nn", name=f"in_proj{l}", tm=512, tn=1024, tk=2048)
        qc, kc, v, qn, kvn = _mla_qkv(proj, cos_t, sin_t, q_g[l], kv_g[l], w_uq_p[l], w_ukv[l], name=f"mla_qkv{l}")
        o, lse2 = _flash_fwd(qc, kc, v, name=f"flash_fwd{l}")
        mix = _mixer_fwd(proj, o, w_pool[l], pool_scale[l], conv_w[l], name=f"mixer_fwd{l}")
        h_next, r = _outproj_ln(mix, w_out[l], h, b_out[l], ln_g[l], ln_b[l], name=f"out_proj_ln{l}")
        saved.append((h, proj, qc, kc, v, qn, kvn, o, lse2, mix, r))
        h = h_next

    dh, loss_acc = _loss_and_dy(h, target, name="loss")
    grads = [None] * DEPTH
    for l in reversed(range(DEPTH)):
        h_in, proj, qc, kc, v, qn, kvn, o, lse2, mix, r = saved[l]
        dr, d_ln_g, d_ln_b, d_b_out = _ln_bwd(dh, r, ln_g[l], name=f"ln_bwd{l}")
        dmix = _matmul(dr, w_out[l], "nt", name=f"dmix{l}", tm=512, tn=1024, tk=2048)
        d_w_out = _matmul(mix, dr, "tn", name=f"dw_out{l}", tm=1024, tn=1024, tk=512)
        d3, dgm, do, delta, d_w_pool, d_ps, d_conv = _mixer_bwd(dmix, proj, o, w_pool[l], pool_scale[l], conv_w[l],
                                                                name=f"mixer_bwd{l}")
        dqc, dkc, dv = _flash_bwd(qc, kc, v, do, lse2, delta, name=f"flash_bwd{l}")
        dqb, dkvb, dql, dkvl, dkr, d_qg, d_kvg = _mla_qkv_bwd(dqc, dkc, dv, proj, cos_t, sin_t, q_g[l], kv_g[l],
                                                               w_uq_p[l], w_ukv[l], name=f"mla_qkv_bwd{l}")
        d_w_uq_p = _matmul(qn, dqb, "tn", name=f"dw_uq{l}", tm=512, tn=2048, tk=512)
        d_w_ukv = _matmul(kvn, dkvb, "tn", name=f"dw_ukv{l}", tm=256, tn=2048, tk=512)
        S = x.shape[0]
        dproj = jnp.concatenate([dgm, dql, d3, dkvl, dkr, jnp.zeros((S, 128), BF16)], axis=1)
        d_w_in_p = _matmul(h_in, dproj, "tn", name=f"dw_in{l}", tm=1024, tn=1024, tk=512)
        dh = _matmul(dproj, w_in_p[l], "nt", name=f"dh{l}", tm=512, tn=1024, tk=1280, add=dr, add_scale=ALPHA)
        grads[l] = dict(w_in_p=d_w_in_p, q_g=d_qg[0], kv_g=d_kvg[0], w_uq_p=d_w_uq_p, w_ukv=d_w_ukv,
                        w_pool=d_w_pool, pool_scale=d_ps[0], conv_w=d_conv, w_out=d_w_out, b_out=d_b_out[0],
                        ln_g=d_ln_g[0], ln_b=d_ln_b[0])
    grad_x, d_emb_g, d_emb_b, _ = _ln_bwd(dh, x, emb_g, name="emb_ln_bwd")
    return loss_acc[0, 0], grad_x, d_emb_g[0], d_emb_b[0], grads


SMALL_ORDER = ("emb_ln_g", "emb_ln_b", "q_norm_g", "kv_norm_g", "w_pool", "pool_scale", "b_out", "ln_g", "ln_b")


def _pack_small(arrs, extra_rows):
    flat = jnp.concatenate([a.reshape(-1) for a in arrs])
    rows = flat.shape[0] // LANE
    total = -(-(rows + extra_rows) // 8) * 8
    return jnp.pad(flat, (0, total * LANE - flat.shape[0])).reshape(total, LANE)


def _unpack_small(packed, shapes):
    flat = packed.reshape(-1)
    out, off = [], 0
    for shp in shapes:
        n = 1
        for s in shp:
            n *= s
        out.append(flat[off:off + n].reshape(shp))
        off += n
    return out, off


def kernel(x, positions, emb_ln_g, emb_ln_b, w_in, q_norm_g, kv_norm_g, w_uq, w_ukv, w_pool, pool_scale, conv_w, w_out, b_out, ln_g, ln_b, loss_target, m_emb_ln_g, m_emb_ln_b, m_w_in, m_q_norm_g, m_kv_norm_g, m_w_uq, m_w_ukv, m_w_pool, m_pool_scale, m_conv_w, m_w_out, m_b_out, m_ln_g, m_ln_b, v_emb_ln_g, v_emb_ln_b, v_w_in, v_q_norm_g, v_kv_norm_g, v_w_uq, v_w_ukv, v_w_pool, v_pool_scale, v_conv_w, v_w_out, v_b_out, v_ln_g, v_ln_b):
    xi, yi, ci = lax.axis_index("x"), lax.axis_index("y"), lax.axis_index("c")
    chip = 2 * xi + yi

    conv_bits = lax.bitcast_convert_type(conv_w.reshape(DEPTH, 3 * 128), BF16).reshape(DEPTH, 3, 256)
    conv_bits = jnp.pad(conv_bits, ((0, 0), (0, 13), (0, 0)))
    shards = (w_in.astype(BF16), w_uq.astype(BF16), w_ukv.astype(BF16), w_out.astype(BF16), conv_bits)
    a_in, a_uq, a_ukv, a_out, a_conv = _allgather_weights(shards, name="allgather_weights")
    w_in_full = jnp.concatenate([a_in[k] for k in range(N_CHIPS)], axis=-1)
    w_in_p = _w_in_to_p(w_in_full)
    w_uq_full = jnp.concatenate([a_uq[k] for k in range(N_CHIPS)], axis=-1)
    w_uq_p = jnp.stack([_w_uq_to_p(w_uq_full[l]) for l in range(DEPTH)])
    w_ukv_full = jnp.concatenate([a_ukv[k] for k in range(N_CHIPS)], axis=-1)
    w_out_full = jnp.concatenate([a_out[k] for k in range(N_CHIPS)], axis=1)
    conv_parts = [lax.bitcast_convert_type(a_conv[k][:, :3, :].reshape(DEPTH, 3, 128, 2), F32)
                  for k in range(N_CHIPS)]
    conv_full = jnp.concatenate(conv_parts, axis=-1)

    loss_part, grad_x, d_emb_g, d_emb_b, grads = _local_step(
        x[0], positions[0], loss_target[0], emb_ln_g, emb_ln_b, w_in_p, q_norm_g, kv_norm_g, w_uq_p, w_ukv_full,
        w_pool, pool_scale, conv_full, w_out_full, b_out, ln_g, ln_b)

    def by_chip_cols(g_nat, width):
        return jnp.stack([g_nat[:, k * width:(k + 1) * width] for k in range(N_CHIPS)])

    g_in = jnp.stack([by_chip_cols(_w_in_from_p(grads[l]["w_in_p"]), 1232) for l in range(DEPTH)])
    g_uq = jnp.stack([by_chip_cols(_w_uq_from_p(grads[l]["w_uq_p"]), 384) for l in range(DEPTH)])
    g_ukv = jnp.stack([by_chip_cols(grads[l]["w_ukv"], 512) for l in range(DEPTH)])
    g_out = jnp.stack([grads[l]["w_out"].reshape(N_CHIPS, 512, D_MODEL) for l in range(DEPTH)])
    big = (g_in, g_uq, g_ukv, g_out)
    theirs = _exchange_layers(big, name="exchange_layers")
    c_idx = ci.reshape(1).astype(jnp.int32)
    j_idx = chip.reshape(1).astype(jnp.int32)
    parts = []
    for a, (g, t) in enumerate(zip(big, theirs)):
        _, _, R, C = g.shape
        parts.append(_add2(g.reshape(2, 4 * R, C), t.reshape(4 * R, C), c_idx, name=f"pair_sum{a}").reshape(4, R, C))
    recv = _scatter_chips(tuple(parts), name="scatter_chips")
    sums = tuple(_add4(p, r, j_idx, name=f"chip_sum{a}") for a, (p, r) in enumerate(zip(parts, recv)))
    others = _send_to_sibling(sums, name="send_to_sibling")

    small_g = [d_emb_g, d_emb_b,
               jnp.stack([grads[l]["q_g"] for l in range(DEPTH)]), jnp.stack([grads[l]["kv_g"] for l in range(DEPTH)]),
               jnp.stack([grads[l]["w_pool"] for l in range(DEPTH)]),
               jnp.stack([grads[l]["pool_scale"] for l in range(DEPTH)]),
               jnp.stack([grads[l]["b_out"] for l in range(DEPTH)]), jnp.stack([grads[l]["ln_g"] for l in range(DEPTH)]),
               jnp.stack([grads[l]["ln_b"] for l in range(DEPTH)]),
               jnp.stack([grads[l]["conv_w"] for l in range(DEPTH)]),
               jnp.pad(loss_part.reshape(1), (0, LANE - 1))]
    small_w = [emb_ln_g, emb_ln_b, q_norm_g, kv_norm_g, w_pool, pool_scale, b_out, ln_g, ln_b]
    small_m = [m_emb_ln_g, m_emb_ln_b, m_q_norm_g, m_kv_norm_g, m_w_pool, m_pool_scale, m_b_out, m_ln_g, m_ln_b]
    small_v = [v_emb_ln_g, v_emb_ln_b, v_q_norm_g, v_kv_norm_g, v_w_pool, v_pool_scale, v_b_out, v_ln_g, v_ln_b]
    extra = (DEPTH * 3 * 512 + LANE) // LANE
    packed_g = _pack_small(small_g, 0)
    gathered = _allgather_small(packed_g, name="allgather_small")
    g_tot, d_small, m_small, v_small = _small_sum_adamw(
        gathered, _pack_small(small_w, extra), _pack_small(small_m, extra), _pack_small(small_v, extra),
        name="small_sum_adamw")
    shapes = [w.shape for w in small_w]
    g_list, off = _unpack_small(g_tot, shapes)
    d_list, _ = _unpack_small(d_small, shapes)
    m_list, _ = _unpack_small(m_small, shapes)
    v_list, _ = _unpack_small(v_small, shapes)
    flat_tot = g_tot.reshape(-1)
    conv_tot = flat_tot[off:off + DEPTH * 3 * 512].reshape(DEPTH, 3, 512)
    loss = flat_tot[off + DEPTH * 3 * 512]
    g_conv = lax.dynamic_slice_in_dim(conv_tot, chip * 128, 128, axis=2)

    upd = {}
    for a, (nm, w, m, v) in enumerate((("w_in", w_in, m_w_in, v_w_in), ("w_uq", w_uq, m_w_uq, v_w_uq),
                                       ("w_ukv", w_ukv, m_w_ukv, v_w_ukv), ("w_out", w_out, m_w_out, v_w_out))):
        upd[nm] = _adamw_pair(w, sums[a], others[a], m, v, c_idx, name=f"adamw_{nm}")
    upd["conv_w"] = (g_conv,) + _adamw(conv_w, g_conv, m_conv_w, v_conv_w, name="adamw_conv_w")
    for i, nm in enumerate(SMALL_ORDER):
        upd[nm] = (g_list[i], d_list[i], m_list[i], v_list[i])

    order = ("emb_ln_g", "emb_ln_b", "w_in", "q_norm_g", "kv_norm_g", "w_uq", "w_ukv", "w_pool", "pool_scale",
             "conv_w", "w_out", "b_out", "ln_g", "ln_b")
    outs = [loss, grad_x[None]]
    for field in range(4):
        outs += [upd[nm][field] for nm in order]
    return tuple(outs)
```

```python
import functools

import jax
import jax.numpy as jnp
from jax import lax
from jax.experimental import pallas as pl
from jax.experimental.pallas import tpu as pltpu

F32 = jnp.float32
BF16 = jnp.bfloat16
MESH = pl.DeviceIdType.MESH

D_MODEL = 2048
DEPTH = 2
N_HEADS = 8
NOPE = 128
ROPE = 64
Q_LORA = 512
KV_LORA = 256
D_MLA = 1024
D_POOL = 512
D_CONV = 512
POOL_WINDOWS = (2, 4, 8, 16)
D_IN_PROJ = 4928
LN_EPS = 1e-5
RMS_EPS = 1e-6
ROPE_THETA = 10000.0
ALPHA = (2 * DEPTH) ** 0.25
SCALE = (NOPE + ROPE) ** -0.5
ADAM_LR = 0.001
ADAM_B1 = 0.9
ADAM_B2 = 0.999
ADAM_EPS = 1e-08
ADAM_WD = 0.01
ADAM_STEP = 10

NP = 5120
HALO = 16
LANE = 128
N_CHIPS = 4
N_DEV = 8

NN = (((1,), (0,)), ((), ()))
NT = (((1,), (1,)), ((), ()))
TN = (((0,), (0,)), ((), ()))


def _pcall(kern, *, name, out_shape, grid=None, in_specs=None, out_specs=None, scratch=(), dims=None,
           vmem_mb=None, **kw):
    cp = {}
    if dims is not None:
        cp["dimension_semantics"] = dims
    if vmem_mb is not None:
        cp["vmem_limit_bytes"] = vmem_mb << 20
    args = dict(name=name, out_shape=out_shape, scratch_shapes=list(scratch),
                compiler_params=pltpu.CompilerParams(**cp))
    if grid is not None:
        args["grid"] = grid
    if in_specs is not None:
        args["in_specs"] = in_specs
    if out_specs is not None:
        args["out_specs"] = out_specs
    args.update(kw)
    return pl.pallas_call(kern, **args)


def _sigmoid(g):
    return 1.0 / (1.0 + jnp.exp(-g))


def _silu_and_grad(g):
    sig = _sigmoid(g)
    return g * sig, sig * (1.0 + g * (1.0 - sig))


def _matmul(a, b, mode, *, name, tm, tn, tk, out_dtype=F32, add=None, add_scale=1.0, vmem_mb=48):
    if mode == "nn":
        (M, K), N = a.shape, b.shape[1]
    elif mode == "nt":
        (M, K), N = a.shape, b.shape[0]
    else:
        (K, M), N = a.shape, b.shape[1]
    tm, tn, tk = min(tm, M), min(tn, N), min(tk, K)
    assert M % tm == 0 and N % tn == 0 and K % tk == 0, (name, M, N, K)
    nk = K // tk
    dn = {"nn": NN, "nt": NT, "tn": TN}[mode]
    if mode == "tn":
        a_spec = pl.BlockSpec((tk, tm), lambda i, j, k: (k, i))
    else:
        a_spec = pl.BlockSpec((tm, tk), lambda i, j, k: (i, k))
    if mode == "nt":
        b_spec = pl.BlockSpec((tn, tk), lambda i, j, k: (j, k))
    else:
        b_spec = pl.BlockSpec((tk, tn), lambda i, j, k: (k, j))
    o_spec = pl.BlockSpec((tm, tn), lambda i, j, k: (i, j))
    in_specs = [a_spec, b_spec] + ([o_spec] if add is not None else [])
    has_add = add is not None

    def kern(*refs):
        if has_add:
            a_ref, b_ref, add_ref, o_ref = refs[:4]
            rest = refs[4:]
        else:
            a_ref, b_ref, o_ref = refs[:3]
            add_ref = None
            rest = refs[3:]
        part = lax.dot_general(a_ref[...].astype(BF16), b_ref[...].astype(BF16), dn,
                               preferred_element_type=F32)

        def finish(acc):
            if has_add:
                acc = add_scale * add_ref[...] + acc
            o_ref[...] = acc.astype(out_dtype)

        if nk == 1:
            finish(part)
        else:
            acc_ref = rest[0]
            k = pl.program_id(2)

            @pl.when(k == 0)
            def _():
                acc_ref[...] = part

            @pl.when(k > 0)
            def _():
                acc_ref[...] += part

            @pl.when(k == nk - 1)
            def _():
                finish(acc_ref[...])

    scratch = [pltpu.VMEM((tm, tn), F32)] if nk > 1 else []
    args = (a, b) + ((add,) if has_add else ())
    return _pcall(kern, name=name, out_shape=jax.ShapeDtypeStruct((M, N), out_dtype),
                  grid=(M // tm, N // tn, nk), in_specs=in_specs, out_specs=o_spec, scratch=scratch,
                  dims=("parallel", "parallel", "arbitrary"), vmem_mb=vmem_mb)(*args)


def _ln_fwd(x, g, b, *, name):
    S, Dm = x.shape
    tm = min(512, S)

    def kern(x_ref, g_ref, b_ref, y_ref):
        xf = x_ref[...]
        mu = jnp.mean(xf, axis=-1, keepdims=True)
        xc = xf - mu
        var = jnp.mean(xc * xc, axis=-1, keepdims=True)
        y_ref[...] = xc * lax.rsqrt(var + LN_EPS) * g_ref[...] + b_ref[...]

    row = pl.BlockSpec((tm, Dm), lambda i: (i, 0))
    vec = pl.BlockSpec((1, Dm), lambda i: (0, 0))
    return _pcall(kern, name=name, out_shape=jax.ShapeDtypeStruct((S, Dm), F32), grid=(S // tm,),
                  in_specs=[row, vec, vec], out_specs=row, dims=("parallel",), vmem_mb=48)(
                      x, g.reshape(1, Dm), b.reshape(1, Dm))


def _ln_bwd(dy, r, g, *, name):
    S, Dm = r.shape
    tm = min(512, S)

    def kern(dy_ref, r_ref, g_ref, dr_ref, dg_ref, db_ref, ds_ref):
        @pl.when(pl.program_id(0) == 0)
        def _():
            dg_ref[...] = jnp.zeros_like(dg_ref)
            db_ref[...] = jnp.zeros_like(db_ref)
            ds_ref[...] = jnp.zeros_like(ds_ref)

        rf = r_ref[...]
        dyf = dy_ref[...]
        mu = jnp.mean(rf, axis=-1, keepdims=True)
        xc = rf - mu
        var = jnp.mean(xc * xc, axis=-1, keepdims=True)
        rstd = lax.rsqrt(var + LN_EPS)
        xhat = xc * rstd
        dxh = dyf * g_ref[...]
        c1 = jnp.mean(dxh, axis=-1, keepdims=True)
        c2 = jnp.mean(dxh * xhat, axis=-1, keepdims=True)
        dr = rstd * (dxh - c1 - xhat * c2)
        dr_ref[...] = dr
        dg_ref[...] += jnp.sum(dyf * xhat, axis=0, keepdims=True)
        db_ref[...] += jnp.sum(dyf, axis=0, keepdims=True)
        ds_ref[...] += jnp.sum(dr, axis=0, keepdims=True)

    row = pl.BlockSpec((tm, Dm), lambda i: (i, 0))
    vec = pl.BlockSpec((1, Dm), lambda i: (0, 0))
    vshape = jax.ShapeDtypeStruct((1, Dm), F32)
    return _pcall(kern, name=name, out_shape=(jax.ShapeDtypeStruct((S, Dm), F32), vshape, vshape, vshape),
                  grid=(S // tm,), in_specs=[row, row, vec], out_specs=(row, vec, vec, vec),
                  dims=("arbitrary",), vmem_mb=48)(dy, r, g.reshape(1, Dm))


def _loss_and_dy(y, target, *, name):
    S, Dm = y.shape
    tm = min(512, S)

    def kern(y_ref, t_ref, dy_ref, l_ref):
        @pl.when(pl.program_id(0) == 0)
        def _():
            l_ref[...] = jnp.zeros_like(l_ref)

        e = y_ref[...] - t_ref[...]
        dy_ref[...] = e / float(Dm)
        per_row = jnp.mean(e * e, axis=-1, keepdims=True)
        l_ref[...] += 0.5 * jnp.sum(per_row, axis=0, keepdims=True)

    row = pl.BlockSpec((tm, Dm), lambda i: (i, 0))
    acc = pl.BlockSpec((8, LANE), lambda i: (0, 0))
    return _pcall(kern, name=name,
                  out_shape=(jax.ShapeDtypeStruct((S, Dm), F32), jax.ShapeDtypeStruct((8, LANE), F32)),
                  grid=(S // tm,), in_specs=[row, row], out_specs=(row, acc), dims=("arbitrary",), vmem_mb=48)(
                      y, target)


def _rot_sum(t):
    return pltpu.roll(t, 32, 1) + pltpu.roll(t, 96, 1)


def _mla_qkv(proj, cos_t, sin_t, qg, kvg, wuq, wukv, *, name):
    S = proj.shape[0]
    tm = min(256, S)

    def kern(ql_ref, kvl_ref, kr_ref, cos_ref, sin_ref, qg_ref, kvg_ref, wuq_ref, wukv_ref,
             qc_ref, kc_ref, v_ref, vt_ref, qn_ref, kvn_ref):
        cosv = cos_ref[...]
        sinv = sin_ref[...]

        def rope(t):
            return t * cosv + _rot_sum(t) * sinv

        ql = ql_ref[...]
        qn = (ql * lax.rsqrt(jnp.mean(ql * ql, axis=-1, keepdims=True) + RMS_EPS) * qg_ref[...]).astype(BF16)
        kvl = kvl_ref[...]
        kvn = (kvl * lax.rsqrt(jnp.mean(kvl * kvl, axis=-1, keepdims=True) + RMS_EPS) * kvg_ref[...]).astype(BF16)
        qn_ref[...] = qn
        kvn_ref[...] = kvn
        q = jnp.dot(qn, wuq_ref[...], preferred_element_type=F32)
        kv = jnp.dot(kvn, wukv_ref[...], preferred_element_type=F32)
        kr = rope(kr_ref[...]).astype(BF16)
        for h in range(N_HEADS):
            c0 = 256 * h
            qc_ref[:, c0:c0 + 128] = q[:, c0:c0 + 128].astype(BF16)
            qc_ref[:, c0 + 128:c0 + 256] = rope(q[:, c0 + 128:c0 + 256]).astype(BF16)
            kc_ref[:, c0:c0 + 128] = kv[:, c0:c0 + 128].astype(BF16)
            kc_ref[:, c0 + 128:c0 + 256] = kr
            vh = kv[:, c0 + 128:c0 + 256]
            v_ref[:, 128 * h:128 * h + 128] = vh.astype(BF16)
            vt_ref[h] = jnp.transpose(vh).astype(BF16)

    def row(w, blk):
        return pl.BlockSpec((tm, w), lambda i: (i, blk))

    def full(shape):
        return pl.BlockSpec(shape, lambda i: (0,) * len(shape))

    t = min(512, S)
    per = t // tm
    vt_spec = pl.BlockSpec((N_HEADS, None, 128, tm), lambda i: (0, i // per, 0, i % per))
    outs = (jax.ShapeDtypeStruct((S, 2048), BF16), jax.ShapeDtypeStruct((S, 2048), BF16),
            jax.ShapeDtypeStruct((S, 1024), BF16), jax.ShapeDtypeStruct((N_HEADS, S // t, 128, t), BF16),
            jax.ShapeDtypeStruct((S, Q_LORA), BF16), jax.ShapeDtypeStruct((S, KV_LORA), BF16))
    return _pcall(kern, name=name, out_shape=outs, grid=(S // tm,),
                  in_specs=[row(512, 2), row(256, 18), row(128, 38), row(128, 0), row(128, 0),
                            full((1, Q_LORA)), full((1, KV_LORA)), full((Q_LORA, 2048)), full((KV_LORA, 2048))],
                  out_specs=(row(2048, 0), row(2048, 0), row(1024, 0), vt_spec, row(512, 0), row(256, 0)),
                  dims=("parallel",), vmem_mb=48)(
                      proj, proj, proj, cos_t, sin_t, qg.reshape(1, -1), kvg.reshape(1, -1), wuq, wukv)


def _mla_qkv_bwd(dqc, dkc, dv, proj, cos_t, sin_t, qg, kvg, wuq, wukv, *, name):
    S = proj.shape[0]
    tm = min(256, S)

    def kern(dq_ref, dk_ref, dv_ref, ql_ref, kvl_ref, cos_ref, sin_ref, qg_ref, kvg_ref, wuq_ref, wukv_ref,
             dqb_ref, dkvb_ref, dql_ref, dkvl_ref, dkr_ref, dqg_ref, dkvg_ref):
        @pl.when(pl.program_id(0) == 0)
        def _():
            dqg_ref[...] = jnp.zeros_like(dqg_ref)
            dkvg_ref[...] = jnp.zeros_like(dkvg_ref)

        cosv = cos_ref[...]
        sinv = sin_ref[...]

        def unrope(t):
            return t * cosv - _rot_sum(t) * sinv

        dkr = jnp.zeros((tm, 128), F32)
        for h in range(N_HEADS):
            c0 = 256 * h
            dqb_ref[:, c0:c0 + 128] = dq_ref[:, c0:c0 + 128].astype(BF16)
            dqb_ref[:, c0 + 128:c0 + 256] = unrope(dq_ref[:, c0 + 128:c0 + 256]).astype(BF16)
            dkvb_ref[:, c0:c0 + 128] = dk_ref[:, c0:c0 + 128].astype(BF16)
            dkvb_ref[:, c0 + 128:c0 + 256] = dv_ref[:, 128 * h:128 * h + 128].astype(BF16)
            dkr = dkr + dk_ref[:, c0 + 128:c0 + 256]
        dkr_ref[...] = unrope(dkr).astype(BF16)

        def rms_bwd(x, g, dy):
            n = x.shape[-1]
            rs = lax.rsqrt(jnp.mean(x * x, axis=-1, keepdims=True) + RMS_EPS)
            dyg = dy * g
            dx = rs * dyg - x * (rs * rs * rs) * (jnp.sum(dyg * x, axis=-1, keepdims=True) / n)
            return dx, jnp.sum(dy * (x * rs), axis=0, keepdims=True)

        dqn = lax.dot_general(dqb_ref[...], wuq_ref[...], NT, preferred_element_type=F32)
        dql, dqg = rms_bwd(ql_ref[...], qg_ref[...], dqn)
        dql_ref[...] = dql.astype(BF16)
        dqg_ref[...] += dqg
        dkvn = lax.dot_general(dkvb_ref[...], wukv_ref[...], NT, preferred_element_type=F32)
        dkvl, dkvg = rms_bwd(kvl_ref[...], kvg_ref[...], dkvn)
        dkvl_ref[...] = dkvl.astype(BF16)
        dkvg_ref[...] += dkvg

    def row(w, blk):
        return pl.BlockSpec((tm, w), lambda i: (i, blk))

    def full(shape):
        return pl.BlockSpec(shape, lambda i: (0,) * len(shape))

    outs = (jax.ShapeDtypeStruct((S, 2048), BF16), jax.ShapeDtypeStruct((S, 2048), BF16),
            jax.ShapeDtypeStruct((S, Q_LORA), BF16), jax.ShapeDtypeStruct((S, KV_LORA), BF16),
            jax.ShapeDtypeStruct((S, 128), BF16), jax.ShapeDtypeStruct((1, Q_LORA), F32),
            jax.ShapeDtypeStruct((1, KV_LORA), F32))
    return _pcall(kern, name=name, out_shape=outs, grid=(S // tm,),
                  in_specs=[row(2048, 0), row(2048, 0), row(1024, 0), row(512, 2), row(256, 18),
                            row(128, 0), row(128, 0), full((1, Q_LORA)), full((1, KV_LORA)),
                            full((Q_LORA, 2048)), full((KV_LORA, 2048))],
                  out_specs=(row(2048, 0), row(2048, 0), row(512, 0), row(256, 0), row(128, 0),
                             full((1, Q_LORA)), full((1, KV_LORA))),
                  dims=("arbitrary",), vmem_mb=56)(
                      dqc, dkc, dv, proj, proj, cos_t, sin_t, qg.reshape(1, -1), kvg.reshape(1, -1), wuq, wukv)


LOG2E = 1.4426950408889634
SCALE_LOG2E = SCALE * LOG2E
TQ = 512


def _kq_mask(t):
    krow = lax.broadcasted_iota(jnp.int32, (t, t), 0)
    qcol = lax.broadcasted_iota(jnp.int32, (t, t), 1)
    return krow <= qcol


def _flash_fwd(qc, kc, vt, *, name):
    S = qc.shape[0]
    t = min(TQ, S)
    n = S // t

    def kern(q_ref, k_ref, vt_ref, o_ref, lse_ref, m_s, l_s, acc_s):
        qi = pl.program_id(1)
        m_s[...] = jnp.full_like(m_s, -jnp.inf)
        l_s[...] = jnp.zeros_like(l_s)
        acc_s[...] = jnp.zeros_like(acc_s)

        def step(kb, masked):
            k0 = pl.multiple_of(kb * t, t)
            st = lax.dot_general(k_ref[pl.ds(k0, t), :], q_ref[...], NT, preferred_element_type=F32)
            if masked:
                st = jnp.where(_kq_mask(t), st, -jnp.inf)
            m_prev = m_s[...]
            m_new = jnp.maximum(m_prev, jnp.max(st, axis=0, keepdims=True))
            a = jnp.exp2((m_prev - m_new) * SCALE_LOG2E)
            pt = jnp.exp2((st - m_new) * SCALE_LOG2E)
            l_s[...] = a * l_s[...] + jnp.sum(pt, axis=0, keepdims=True)
            acc_s[...] = a * acc_s[...] + jnp.dot(vt_ref[kb], pt.astype(BF16), preferred_element_type=F32)
            m_s[...] = m_new

        def body(kb, carry):
            step(kb, False)
            return carry

        lax.fori_loop(0, qi, body, 0)
        step(qi, True)
        o_ref[...] = jnp.transpose(acc_s[...] / l_s[...])
        lse_ref[pl.ds(qi, 1), :] = m_s[...] * SCALE_LOG2E + jnp.log2(l_s[...])

    q_spec = pl.BlockSpec((t, 256), lambda h, qi: (qi, h))
    k_spec = pl.BlockSpec((S, 256), lambda h, qi: (0, h))
    vt_spec = pl.BlockSpec((None, n, 128, t), lambda h, qi: (h, 0, 0, 0))
    o_spec = pl.BlockSpec((t, 128), lambda h, qi: (qi, h))
    lse_spec = pl.BlockSpec((None, n, t), lambda h, qi: (h, 0, 0))
    return _pcall(kern, name=name,
                  out_shape=(jax.ShapeDtypeStruct((S, D_MLA), F32), jax.ShapeDtypeStruct((N_HEADS, n, t), F32)),
                  grid=(N_HEADS, n), in_specs=[q_spec, k_spec, vt_spec], out_specs=(o_spec, lse_spec),
                  scratch=[pltpu.VMEM((1, t), F32), pltpu.VMEM((1, t), F32), pltpu.VMEM((128, t), F32)],
                  dims=("parallel", "arbitrary"), vmem_mb=48)(qc, kc, vt)


def _attn_delta(o, do, *, name):
    S = o.shape[0]
    t = min(TQ, S)
    n = S // t

    def kern(o_ref, do_ref, dl_ref):
        i = pl.program_id(0)
        prod = o_ref[...] * do_ref[...]
        lane = lax.broadcasted_iota(jnp.int32, (t, LANE), 1)
        dmat = jnp.zeros((t, LANE), F32)
        for h in range(N_HEADS):
            dmat = jnp.where(lane == h, jnp.sum(prod[:, 128 * h:128 * h + 128], axis=1, keepdims=True), dmat)
        dmat_t = jnp.transpose(dmat)
        for h in range(N_HEADS):
            dl_ref[h, pl.ds(i, 1), :] = dmat_t[h:h + 1, :]

    row = pl.BlockSpec((t, D_MLA), lambda i: (i, 0))
    return _pcall(kern, name=name, out_shape=jax.ShapeDtypeStruct((N_HEADS, n, t), F32), grid=(n,),
                  in_specs=[row, row], out_specs=pl.BlockSpec((N_HEADS, n, t), lambda i: (0, 0, 0)),
                  dims=("arbitrary",), vmem_mb=48)(o, do)


def _flash_bwd(qc, kc, v, do, lse2, delta, *, name):
    S = qc.shape[0]
    t = min(TQ, S)
    n = S // t

    def kern(q_ref, k_ref, v_ref, do_ref, lse_ref, dl_ref, dq_ref, dk_ref, dv_ref):
        ki = pl.program_id(1)

        @pl.when(ki == 0)
        def _():
            dq_ref[...] = jnp.zeros_like(dq_ref)

        dk_ref[...] = jnp.zeros_like(dk_ref)
        dv_ref[...] = jnp.zeros_like(dv_ref)

        def step(qb, masked):
            q0 = pl.multiple_of(qb * t, t)
            kt = k_ref[...]
            qblk = q_ref[pl.ds(q0, t), :]
            dob = do_ref[pl.ds(q0, t), :].astype(BF16)
            st = lax.dot_general(kt, qblk, NT, preferred_element_type=F32)
            pt = jnp.exp2(st * SCALE_LOG2E - lse_ref[pl.ds(qb, 1), :])
            if masked:
                pt = jnp.where(_kq_mask(t), pt, 0.0)
            dv_ref[...] += jnp.dot(pt.astype(BF16), dob, preferred_element_type=F32)
            dpt = lax.dot_general(v_ref[...], dob, NT, preferred_element_type=F32)
            dst = (pt * (dpt - dl_ref[pl.ds(qb, 1), :]) * SCALE).astype(BF16)
            dk_ref[...] += jnp.dot(dst, qblk, preferred_element_type=F32)
            dq_ref[pl.ds(q0, t), :] += lax.dot_general(dst, kt, TN, preferred_element_type=F32)

        step(ki, True)

        def body(qb, carry):
            step(qb, False)
            return carry

        lax.fori_loop(ki + 1, n, body, 0)

    def whole(w):
        return pl.BlockSpec((S, w), lambda h, ki: (0, h))

    def krow(w):
        return pl.BlockSpec((t, w), lambda h, ki: (ki, h))

    stat = pl.BlockSpec((None, n, t), lambda h, ki: (h, 0, 0))
    return _pcall(kern, name=name,
                  out_shape=(jax.ShapeDtypeStruct((S, 2048), F32), jax.ShapeDtypeStruct((S, 2048), F32),
                             jax.ShapeDtypeStruct((S, D_MLA), F32)),
                  grid=(N_HEADS, n),
                  in_specs=[whole(256), krow(256), krow(128), whole(128), stat, stat],
                  out_specs=(whole(256), krow(256), krow(128)),
                  dims=("parallel", "arbitrary"), vmem_mb=56)(qc, kc, v, do, lse2, delta)


def _mixer_specs(S, tm):
    hb = tm // HALO
    last_hb = S // HALO - 1

    def main(w, blk):
        return pl.BlockSpec((tm, w), lambda i: (i, blk))

    def prev(w, blk):
        return pl.BlockSpec((HALO, w), lambda i: (jnp.maximum(i * hb - 1, 0), blk))

    def nxt(w, blk):
        return pl.BlockSpec((HALO, w), lambda i: (jnp.minimum((i + 1) * hb, last_hb), blk))

    def full(shape):
        return pl.BlockSpec(shape, lambda i: (0,) * len(shape))

    return main, prev, nxt, full


def _fill_halo(i, xp, xu, hp_ref, hch_ref, hcc_ref, pin_ref, ch_ref, cc_ref, tm):
    first = i == 0
    xp[0:HALO, :] = jnp.where(first, 0.0, hp_ref[...])
    xp[HALO:HALO + tm, :] = pin_ref[...]
    xu[0:HALO, :] = jnp.where(first, 0.0, hch_ref[...] * hcc_ref[...])
    xu[HALO:HALO + tm, :] = cc_ref[...] * ch_ref[...]


def _pooled(xp, g, t1, tm):
    w = POOL_WINDOWS[g]
    lanes = slice(128 * g, 128 * g + 128)
    x0 = xp[HALO:HALO + tm, lanes]
    acc = x0
    for k in range(1, w):
        acc = acc + xp[HALO - k:HALO - k + tm, lanes]
    return acc / jnp.minimum(t1, float(w)) - x0


def _conv_fwd(xu, cw_ref, tm):
    return (cw_ref[0:1, :] * xu[HALO - 2:HALO - 2 + tm, :] + cw_ref[1:2, :] * xu[HALO - 1:HALO - 1 + tm, :]
            + cw_ref[2:3, :] * xu[HALO:HALO + tm, :])


def _mixer_fwd(proj, o, wpool, ps, convw, *, name):
    S = proj.shape[0]
    tm = min(256, S)
    main, prev, _, full = _mixer_specs(S, tm)

    def kern(gm_ref, pin_ref, gp_ref, ch_ref, cb_ref, cc_ref, gc_ref, hp_ref, hch_ref, hcc_ref,
             o_ref, wp_ref, ps_ref, cw_ref, mix_ref, xp, xu):
        i = pl.program_id(0)
        _fill_halo(i, xp, xu, hp_ref, hch_ref, hcc_ref, pin_ref, ch_ref, cc_ref, tm)
        t1 = (i * tm + lax.broadcasted_iota(jnp.int32, (tm, 1), 0) + 1).astype(F32)
        for g in range(4):
            lanes = slice(128 * g, 128 * g + 128)
            pooled = _pooled(xp, g, t1, tm)
            z = jnp.dot(pooled.astype(BF16), wp_ref[g].astype(BF16), preferred_element_type=F32)
            gp = gp_ref[:, lanes]
            y = z * ps_ref[:, lanes] * (gp * _sigmoid(gp))
            mix_ref[:, 1024 + 128 * g:1024 + 128 * g + 128] = y.astype(BF16)
        gc = gc_ref[...]
        mix_ref[:, 1536:2048] = (cb_ref[...] * _conv_fwd(xu, cw_ref, tm) * (gc * _sigmoid(gc))).astype(BF16)
        gm = gm_ref[...]
        mix_ref[:, 0:1024] = (o_ref[...] * (gm * _sigmoid(gm))).astype(BF16)

    return _pcall(kern, name=name, out_shape=jax.ShapeDtypeStruct((S, 2048), BF16), grid=(S // tm,),
                  in_specs=[main(1024, 0), main(512, 3), main(512, 4), main(512, 5), main(512, 6), main(512, 7),
                            main(512, 8), prev(512, 3), prev(512, 5), prev(512, 7),
                            main(1024, 0), full((4, 128, 128)), full((1, 512)), full((3, 512))],
                  out_specs=main(2048, 0),
                  scratch=[pltpu.VMEM((tm + HALO, 512), F32), pltpu.VMEM((tm + HALO, 512), F32)],
                  dims=("parallel",), vmem_mb=48)(
                      proj, proj, proj, proj, proj, proj, proj, proj, proj, proj, o, wpool, ps.reshape(1, 512), convw)


def _mixer_bwd(dmix, proj, o, wpool, ps, convw, *, name):
    S = proj.shape[0]
    tm = min(256, S)
    n = S // tm
    main, prev, nxt, full = _mixer_specs(S, tm)

    def kern(dm_ref, dmn_ref, gm_ref, pin_ref, gp_ref, ch_ref, cb_ref, cc_ref, gc_ref,
             hp_ref, hch_ref, hcc_ref, gpn_ref, cbn_ref, gcn_ref, o_ref, wp_ref, ps_ref, cw_ref,
             d3_ref, dgm_ref, do_ref, dwp_ref, dps_ref, dcw_ref, xp, xu, ee, ed):
        i = pl.program_id(0)
        last = i == n - 1

        @pl.when(i == 0)
        def _():
            dwp_ref[...] = jnp.zeros_like(dwp_ref)
            dps_ref[...] = jnp.zeros_like(dps_ref)
            dcw_ref[...] = jnp.zeros_like(dcw_ref)

        _fill_halo(i, xp, xu, hp_ref, hch_ref, hcc_ref, pin_ref, ch_ref, cc_ref, tm)
        t1 = (i * tm + lax.broadcasted_iota(jnp.int32, (tm, 1), 0) + 1).astype(F32)
        t1n = ((i + 1) * tm + lax.broadcasted_iota(jnp.int32, (HALO, 1), 0) + 1).astype(F32)

        for g in range(4):
            w = float(POOL_WINDOWS[g])
            lanes = slice(128 * g, 128 * g + 128)
            pooled = _pooled(xp, g, t1, tm)
            pb = pooled.astype(BF16)
            wp = wp_ref[g].astype(BF16)
            z = jnp.dot(pb, wp, preferred_element_type=F32)
            psl = ps_ref[:, lanes]
            sg, dsg = _silu_and_grad(gp_ref[:, lanes])
            dmp = dm_ref[:, 1024 + 128 * g:1024 + 128 * g + 128]
            dyp = dmp * sg
            d3_ref[:, 512 + 128 * g:512 + 128 * g + 128] = (dmp * (z * psl) * dsg).astype(BF16)
            dps_ref[:, lanes] += jnp.sum(dyp * z, axis=0, keepdims=True)
            dz = (dyp * psl).astype(BF16)
            dwp_ref[g] += lax.dot_general(pb, dz, TN, preferred_element_type=F32)
            dpl = lax.dot_general(dz, wp, NT, preferred_element_type=F32)
            ee[0:tm, lanes] = dpl / jnp.minimum(t1, w)
            gpn = gpn_ref[:, lanes]
            dzn = (dmn_ref[:, lanes] * (gpn * _sigmoid(gpn)) * psl).astype(BF16)
            dpn = lax.dot_general(dzn, wp, NT, preferred_element_type=F32)
            ee[tm:tm + HALO, lanes] = jnp.where(last, 0.0, dpn / jnp.minimum(t1n, w))
            acc = ee[0:tm, lanes]
            for k in range(1, POOL_WINDOWS[g]):
                acc = acc + ee[k:k + tm, lanes]
            d3_ref[:, lanes] = (acc - dpl).astype(BF16)

        yc = _conv_fwd(xu, cw_ref, tm)
        sgc, dsgc = _silu_and_grad(gc_ref[...])
        cb = cb_ref[...]
        dmc = dm_ref[:, 1536:2048]
        d3_ref[:, 2560:3072] = (dmc * cb * yc * dsgc).astype(BF16)
        d3_ref[:, 1536:2048] = (dmc * yc * sgc).astype(BF16)
        dyc = dmc * cb * sgc
        ed[0:tm, :] = dyc
        gcn = gcn_ref[...]
        ed[tm:tm + HALO, :] = jnp.where(last, 0.0, dmn_ref[:, 512:1024] * cbn_ref[...] * (gcn * _sigmoid(gcn)))
        dcw_ref[0:1, :] += jnp.sum(dyc * xu[HALO - 2:HALO - 2 + tm, :], axis=0, keepdims=True)
        dcw_ref[1:2, :] += jnp.sum(dyc * xu[HALO - 1:HALO - 1 + tm, :], axis=0, keepdims=True)
        dcw_ref[2:3, :] += jnp.sum(dyc * xu[HALO:HALO + tm, :], axis=0, keepdims=True)
        du = cw_ref[2:3, :] * dyc + cw_ref[1:2, :] * ed[1:1 + tm, :] + cw_ref[0:1, :] * ed[2:2 + tm, :]
        d3_ref[:, 2048:2560] = (du * ch_ref[...]).astype(BF16)
        d3_ref[:, 1024:1536] = (du * cc_ref[...]).astype(BF16)

        sgm, dsgm = _silu_and_grad(gm_ref[...])
        dmm = dm_ref[:, 0:1024]
        do_ref[...] = dmm * sgm
        dgm_ref[...] = (dmm * o_ref[...] * dsgm).astype(BF16)

    outs = (jax.ShapeDtypeStruct((S, 3072), BF16), jax.ShapeDtypeStruct((S, 1024), BF16),
            jax.ShapeDtypeStruct((S, 1024), F32), jax.ShapeDtypeStruct((4, 128, 128), F32),
            jax.ShapeDtypeStruct((1, 512), F32), jax.ShapeDtypeStruct((3, 512), F32))
    scr = [pltpu.VMEM((tm + HALO, 512), F32) for _ in range(4)]
    return _pcall(kern, name=name, out_shape=outs, grid=(n,),
                  in_specs=[main(2048, 0), nxt(1024, 1),
                            main(1024, 0), main(512, 3), main(512, 4), main(512, 5), main(512, 6), main(512, 7),
                            main(512, 8), prev(512, 3), prev(512, 5), prev(512, 7),
                            nxt(512, 4), nxt(512, 6), nxt(512, 8),
                            main(1024, 0), full((4, 128, 128)), full((1, 512)), full((3, 512))],
                  out_specs=(main(3072, 0), main(1024, 0), main(1024, 0), full((4, 128, 128)), full((1, 512)),
                             full((3, 512))),
                  scratch=scr, dims=("arbitrary",), vmem_mb=56)(
                      dmix, dmix, proj, proj, proj, proj, proj, proj, proj, proj, proj, proj, proj, proj, proj,
                      o, wpool, ps.reshape(1, 512), convw)


def _outproj_ln(mix, wout, h, bout, g, b, *, name):
    S, Dm = h.shape
    tm = min(256, S)

    def kern(mix_ref, w_ref, h_ref, bo_ref, g_ref, b_ref, y_ref, r_ref):
        out = jnp.dot(mix_ref[...], w_ref[...], preferred_element_type=F32) + bo_ref[...]
        r = ALPHA * h_ref[...] + out
        r_ref[...] = r
        mu = jnp.mean(r, axis=-1, keepdims=True)
        xc = r - mu
        var = jnp.mean(xc * xc, axis=-1, keepdims=True)
        y_ref[...] = xc * lax.rsqrt(var + LN_EPS) * g_ref[...] + b_ref[...]

    row = pl.BlockSpec((tm, Dm), lambda i: (i, 0))
    vec = pl.BlockSpec((1, Dm), lambda i: (0, 0))
    wsp = pl.BlockSpec((Dm, Dm), lambda i: (0, 0))
    sds = jax.ShapeDtypeStruct((S, Dm), F32)
    return _pcall(kern, name=name, out_shape=(sds, sds), grid=(S // tm,),
                  in_specs=[row, wsp, row, vec, vec, vec], out_specs=(row, row), dims=("parallel",), vmem_mb=56)(
                      mix, wout, h, bout.reshape(1, Dm), g.reshape(1, Dm), b.reshape(1, Dm))


def _adamw_math(w, g, m, v):
    m = ADAM_B1 * m + (1.0 - ADAM_B1) * g
    v = ADAM_B2 * v + (1.0 - ADAM_B2) * (g * g)
    m_hat = m / (1.0 - ADAM_B1 ** ADAM_STEP)
    v_hat = v / (1.0 - ADAM_B2 ** ADAM_STEP)
    delta = -ADAM_LR * (m_hat / (jnp.sqrt(v_hat) + ADAM_EPS) + ADAM_WD * w)
    return delta, m, v


def _adamw(w, g, m, v, *, name):
    shape = w.shape
    C = shape[-1]
    R = 1
    for s in shape[:-1]:
        R *= s
    tr = R
    for cand in (512, 256, 128, 64, 32, 16, 8):
        if R % cand == 0 and R > cand and cand * C <= 256 * 1024:
            tr = cand
            break

    def kern(w_ref, g_ref, m_ref, v_ref, d_ref, mo_ref, vo_ref):
        d, mn, vn = _adamw_math(w_ref[...], g_ref[...], m_ref[...], v_ref[...])
        d_ref[...] = d
        mo_ref[...] = mn
        vo_ref[...] = vn

    blk = pl.BlockSpec((tr, C), lambda i: (i, 0))
    sds = jax.ShapeDtypeStruct((R, C), F32)
    outs = _pcall(kern, name=name, out_shape=(sds, sds, sds), grid=(R // tr,), in_specs=[blk] * 4,
                  out_specs=(blk, blk, blk), dims=("parallel",), vmem_mb=48)(
                      w.reshape(R, C), g.reshape(R, C), m.reshape(R, C), v.reshape(R, C))
    return tuple(t.reshape(shape) for t in outs)


def _adamw_pair(w, mine, theirs, m, v, c_idx, *, name):
    _, R, C = w.shape
    tr = R
    for cand in (512, 256, 128, 64, 32, 16, 8):
        if R % cand == 0 and cand * C <= 256 * 1024:
            tr = cand
            break
    nb = R // tr

    def kern(c_ref, w_ref, a_ref, b_ref, m_ref, v_ref, g_ref, d_ref, mo_ref, vo_ref):
        layer = pl.program_id(0) // nb
        g = jnp.where(layer == c_ref[0], a_ref[...], b_ref[...])
        g_ref[...] = g
        d, mn, vn = _adamw_math(w_ref[...], g, m_ref[...], v_ref[...])
        d_ref[...] = d
        mo_ref[...] = mn
        vo_ref[...] = vn

    full = pl.BlockSpec((tr, C), lambda i, c: (i, 0))
    half = pl.BlockSpec((tr, C), lambda i, c: (i % nb, 0))
    gs = pltpu.PrefetchScalarGridSpec(num_scalar_prefetch=1, grid=(2 * nb,),
                                      in_specs=[full, half, half, full, full], out_specs=(full,) * 4)
    sds = jax.ShapeDtypeStruct((2 * R, C), F32)
    outs = pl.pallas_call(kern, name=name, out_shape=(sds,) * 4, grid_spec=gs,
                          compiler_params=pltpu.CompilerParams(dimension_semantics=("parallel",),
                                                               vmem_limit_bytes=48 << 20))(
                              c_idx, w.reshape(2 * R, C), mine, theirs, m.reshape(2 * R, C), v.reshape(2 * R, C))
    return tuple(t.reshape(2, R, C) for t in outs)


def _small_sum_adamw(gathered, w, m, v, *, name):
    R = w.shape[0]

    def kern(ga_ref, w_ref, m_ref, v_ref, g_ref, d_ref, mo_ref, vo_ref):
        g = ga_ref[0]
        for k in range(1, N_DEV):
            g = g + ga_ref[k]
        g_ref[...] = g
        d, mn, vn = _adamw_math(w_ref[...], g, m_ref[...], v_ref[...])
        d_ref[...] = d
        mo_ref[...] = mn
        vo_ref[...] = vn

    sds = jax.ShapeDtypeStruct((R, LANE), F32)
    return _pcall(kern, name=name, out_shape=(sds, sds, sds, sds), vmem_mb=48)(gathered, w, m, v)


def _add2(a, b, c_idx, *, name):
    _, R, C = a.shape
    tr = 256 if R % 256 == 0 else R

    def kern(c_ref, a_ref, b_ref, o_ref):
        o_ref[...] = (a_ref[...] + b_ref[...]).astype(BF16)

    gs = pltpu.PrefetchScalarGridSpec(
        num_scalar_prefetch=1, grid=(R // tr,),
        in_specs=[pl.BlockSpec((None, tr, C), lambda i, c: (c[0], i, 0)), pl.BlockSpec((tr, C), lambda i, c: (i, 0))],
        out_specs=pl.BlockSpec((tr, C), lambda i, c: (i, 0)))
    return pl.pallas_call(kern, name=name, out_shape=jax.ShapeDtypeStruct((R, C), BF16), grid_spec=gs,
                          compiler_params=pltpu.CompilerParams(dimension_semantics=("parallel",),
                                                               vmem_limit_bytes=48 << 20))(c_idx, a, b)


def _add4(p, r2, j_idx, *, name):
    _, R, C = p.shape
    tr = 256 if R % 256 == 0 else R

    def kern(j_ref, p_ref, r_ref, o_ref):
        o_ref[...] = ((p_ref[...].astype(F32) + r_ref[0].astype(F32)) + r_ref[1].astype(F32)) + r_ref[2].astype(F32)

    gs = pltpu.PrefetchScalarGridSpec(
        num_scalar_prefetch=1, grid=(R // tr,),
        in_specs=[pl.BlockSpec((None, tr, C), lambda i, j: (j[0], i, 0)),
                  pl.BlockSpec((3, tr, C), lambda i, j: (0, i, 0))],
        out_specs=pl.BlockSpec((tr, C), lambda i, j: (i, 0)))
    return pl.pallas_call(kern, name=name, out_shape=jax.ShapeDtypeStruct((R, C), F32), grid_spec=gs,
                          compiler_params=pltpu.CompilerParams(dimension_semantics=("parallel",),
                                                               vmem_limit_bytes=48 << 20))(j_idx, p, r2)


HBM_SPEC = pl.BlockSpec(memory_space=pl.ANY)


def _mesh_pos():
    x, y, c = lax.axis_index("x"), lax.axis_index("y"), lax.axis_index("c")
    return x, y, c


def _other_chips(x, y):
    return [(1 - x, y), (x, 1 - y), (1 - x, 1 - y)]


def _allgather_weights(shards, *, name):
    na = len(shards)

    def body(*refs):
        ins, outs = refs[:na], refs[na:2 * na]
        send_sems, recv_sems = refs[2 * na:]
        x, y, c = _mesh_pos()
        j = 2 * x + y
        chips = _other_chips(x, y)
        sibling = (x, y, 1 - c)
        own = [pltpu.make_async_remote_copy(
            src_ref=ins[a], dst_ref=outs[a].at[j], send_sem=send_sems.at[6 * na + a],
            recv_sem=recv_sems.at[6 * na + a], device_id=sibling, device_id_type=MESH) for a in range(na)]
        for cp in own:
            cp.start()

        def first(a, k):
            return pltpu.make_async_remote_copy(
                src_ref=ins[a].at[c], dst_ref=outs[a].at[j, c], send_sem=send_sems.at[3 * a + k],
                recv_sem=recv_sems.at[3 * a + k], device_id=(*chips[k], c), device_id_type=MESH)

        def passed(a, k, layer):
            pk = 2 * chips[k][0] + chips[k][1]
            return pltpu.make_async_remote_copy(
                src_ref=outs[a].at[pk, layer], dst_ref=outs[a].at[pk, layer],
                send_sem=send_sems.at[3 * na + 3 * a + k], recv_sem=recv_sems.at[3 * na + 3 * a + k],
                device_id=sibling, device_id_type=MESH)

        for a in range(na):
            for k in range(3):
                first(a, k).start()
        for a in range(na):
            for k in range(3):
                pk = 2 * chips[k][0] + chips[k][1]
                pltpu.make_async_remote_copy(
                    src_ref=ins[a].at[c], dst_ref=outs[a].at[pk, c], send_sem=send_sems.at[3 * a + k],
                    recv_sem=recv_sems.at[3 * a + k], device_id=(*chips[k], c), device_id_type=MESH).wait_recv()
                passed(a, k, c).start()
        for a in range(na):
            for k in range(3):
                passed(a, k, 1 - c).wait_recv()
        for a in range(na):
            for k in range(3):
                first(a, k).wait_send()
                passed(a, k, c).wait_send()
        for cp in own:
            cp.wait()

    out_shape = tuple(jax.ShapeDtypeStruct((N_CHIPS,) + s.shape, s.dtype) for s in shards)
    return _pcall(body, name=name, out_shape=out_shape, in_specs=[HBM_SPEC] * na, out_specs=(HBM_SPEC,) * na,
                  scratch=[pltpu.SemaphoreType.DMA((7 * na,)), pltpu.SemaphoreType.DMA((7 * na,))])(*shards)


def _exchange_layers(grads, *, name):
    na = len(grads)

    def body(*refs):
        ins, outs = refs[:na], refs[na:2 * na]
        send_sems, recv_sems = refs[2 * na:]
        x, y, c = _mesh_pos()
        copies = [pltpu.make_async_remote_copy(
            src_ref=ins[a].at[1 - c], dst_ref=outs[a], send_sem=send_sems.at[a], recv_sem=recv_sems.at[a],
            device_id=(x, y, 1 - c), device_id_type=MESH) for a in range(na)]
        for cp in copies:
            cp.start()
        for cp in copies:
            cp.wait()

    out_shape = tuple(jax.ShapeDtypeStruct(g.shape[1:], g.dtype) for g in grads)
    return _pcall(body, name=name, out_shape=out_shape, in_specs=[HBM_SPEC] * na, out_specs=(HBM_SPEC,) * na,
                  scratch=[pltpu.SemaphoreType.DMA((na,)), pltpu.SemaphoreType.DMA((na,))])(*grads)


def _scatter_chips(parts, *, name):
    na = len(parts)

    def body(*refs):
        ins, outs = refs[:na], refs[na:2 * na]
        send_sems, recv_sems = refs[2 * na:]
        x, y, c = _mesh_pos()
        chips = _other_chips(x, y)
        copies = []
        for a in range(na):
            for k in range(3):
                pk = 2 * chips[k][0] + chips[k][1]
                copies.append(pltpu.make_async_remote_copy(
                    src_ref=ins[a].at[pk], dst_ref=outs[a].at[k], send_sem=send_sems.at[3 * a + k],
                    recv_sem=recv_sems.at[3 * a + k], device_id=(*chips[k], c), device_id_type=MESH))
        for cp in copies:
            cp.start()
        for cp in copies:
            cp.wait()

    out_shape = tuple(jax.ShapeDtypeStruct((3,) + p.shape[1:], p.dtype) for p in parts)
    return _pcall(body, name=name, out_shape=out_shape, in_specs=[HBM_SPEC] * na, out_specs=(HBM_SPEC,) * na,
                  scratch=[pltpu.SemaphoreType.DMA((3 * na,)), pltpu.SemaphoreType.DMA((3 * na,))])(*parts)


def _send_to_sibling(sums, *, name):
    na = len(sums)

    def body(*refs):
        ins, outs = refs[:na], refs[na:2 * na]
        send_sems, recv_sems = refs[2 * na:]
        x, y, c = _mesh_pos()
        copies = [pltpu.make_async_remote_copy(
            src_ref=ins[a], dst_ref=outs[a], send_sem=send_sems.at[a], recv_sem=recv_sems.at[a],
            device_id=(x, y, 1 - c), device_id_type=MESH) for a in range(na)]
        for cp in copies:
            cp.start()
        for cp in copies:
            cp.wait()

    out_shape = tuple(jax.ShapeDtypeStruct(t.shape, t.dtype) for t in sums)
    return _pcall(body, name=name, out_shape=out_shape, in_specs=[HBM_SPEC] * na, out_specs=(HBM_SPEC,) * na,
                  scratch=[pltpu.SemaphoreType.DMA((na,)), pltpu.SemaphoreType.DMA((na,))])(*sums)


def _allgather_small(block, *, name):
    m_per, n = block.shape

    def body(x_ref, out_ref, send_sems, recv_sems, local_sem):
        x, y, c = _mesh_pos()
        me, sibling = (x, y, c), (x, y, 1 - c)
        chips = _other_chips(x, y)

        def rows(px, py, pc):
            return out_ref.at[4 * px + 2 * py + pc]

        def copy(k, blk, to, src=None):
            return pltpu.make_async_remote_copy(
                src_ref=rows(*blk) if src is None else src, dst_ref=rows(*blk), send_sem=send_sems.at[k],
                recv_sem=recv_sems.at[k], device_id=to, device_id_type=MESH)

        mine = pltpu.make_async_copy(x_ref, rows(*me), local_sem)
        mine.start()
        first = [copy(0, me, sibling, src=x_ref)]
        first += [copy(1 + k, me, (*chip, c), src=x_ref) for k, chip in enumerate(chips)]
        for cp in first:
            cp.start()
        passed = [copy(4 + k, (*chip, c), sibling) for k, chip in enumerate(chips)]
        for k, chip in enumerate(chips):
            copy(1 + k, (*chip, c), me).wait_recv()
            passed[k].start()
        copy(0, sibling, me).wait_recv()
        for k, chip in enumerate(chips):
            copy(4 + k, (*chip, 1 - c), me).wait_recv()
        for cp in first + passed:
            cp.wait_send()
        mine.wait()

    vm = pl.BlockSpec(memory_space=pltpu.VMEM)
    return _pcall(body, name=name, out_shape=jax.ShapeDtypeStruct((N_DEV, m_per, n), block.dtype),
                  in_specs=[vm], out_specs=vm,
                  scratch=[pltpu.SemaphoreType.DMA((7,)), pltpu.SemaphoreType.DMA((7,)), pltpu.SemaphoreType.DMA],
                  vmem_mb=48)(block)


def _w_in_to_p(w):
    pad = jnp.zeros(w.shape[:-1] + (NP - D_IN_PROJ,), w.dtype)
    return jnp.concatenate([w[..., 832:1856], w[..., 0:512], w[..., 1856:4928], w[..., 512:768], w[..., 768:832], pad],
                           axis=-1)


def _w_in_from_p(g):
    return jnp.concatenate([g[..., 1024:1536], g[..., 4608:4864], g[..., 4864:4928], g[..., 0:1024], g[..., 1536:4608]],
                           axis=-1)


def _w_uq_to_p(w):
    k = w.shape[0]
    return jnp.pad(w.reshape(k, N_HEADS, NOPE + ROPE), ((0, 0), (0, 0), (0, 64))).reshape(k, N_HEADS * 256)


def _w_uq_from_p(g):
    k = g.shape[0]
    return g.reshape(k, N_HEADS, 256)[:, :, :NOPE + ROPE].reshape(k, N_HEADS * (NOPE + ROPE))


def _rope_tables(positions):
    half = ROPE // 2
    inv_freq = ROPE_THETA ** (-jnp.arange(half, dtype=F32) / half)
    ang = positions.astype(F32)[:, None] * inv_freq
    cos, sin = jnp.cos(ang), jnp.sin(ang)
    S = positions.shape[0]
    cos_t = jnp.concatenate([cos, cos, jnp.ones((S, 64), F32)], axis=1)
    sin_t = jnp.concatenate([-sin, sin, jnp.zeros((S, 64), F32)], axis=1)
    return cos_t, sin_t


def _local_step(x, positions, target, emb_g, emb_b, w_in_p, q_g, kv_g, w_uq_p, w_ukv, w_pool, pool_scale, conv_w,
                w_out, b_out, ln_g, ln_b):
    cos_t, sin_t = _rope_tables(positions)
    h = _ln_fwd(x, emb_g, emb_b, name="emb_ln")
    saved = []
    for l in range(DEPTH):
        proj = _matmul(h, w_in_p[l], "nn", name=f"in_proj{l}", tm=512, tn=1024, tk=2048)
        qc, kc, v, vt, qn, kvn = _mla_qkv(proj, cos_t, sin_t, q_g[l], kv_g[l], w_uq_p[l], w_ukv[l],
                                          name=f"mla_qkv{l}")
        o, lse2 = _flash_fwd(qc, kc, vt, name=f"flash_fwd{l}")
        mix = _mixer_fwd(proj, o, w_pool[l], pool_scale[l], conv_w[l], name=f"mixer_fwd{l}")
        h_next, r = _outproj_ln(mix, w_out[l], h, b_out[l], ln_g[l], ln_b[l], name=f"out_proj_ln{l}")
        saved.append((h, proj, qc, kc, v, qn, kvn, o, lse2, mix, r))
        h = h_next

    dh, loss_acc = _loss_and_dy(h, target, name="loss")
    grads = [None] * DEPTH
    for l in reversed(range(DEPTH)):
        h_in, proj, qc, kc, v, qn, kvn, o, lse2, mix, r = saved[l]
        dr, d_ln_g, d_ln_b, d_b_out = _ln_bwd(dh, r, ln_g[l], name=f"ln_bwd{l}")
        dmix = _matmul(dr, w_out[l], "nt", name=f"dmix{l}", tm=512, tn=1024, tk=2048)
        d_w_out = _matmul(mix, dr, "tn", name=f"dw_out{l}", tm=1024, tn=1024, tk=512)
        d3, dgm, do, d_w_pool, d_ps, d_conv = _mixer_bwd(dmix, proj, o, w_pool[l], pool_scale[l], conv_w[l],
                                                         name=f"mixer_bwd{l}")
        delta = _attn_delta(o, do, name=f"attn_delta{l}")
        dqc, dkc, dv = _flash_bwd(qc, kc, v, do, lse2, delta, name=f"flash_bwd{l}")
        dqb, dkvb, dql, dkvl, dkr, d_qg, d_kvg = _mla_qkv_bwd(dqc, dkc, dv, proj, cos_t, sin_t, q_g[l], kv_g[l],
                                                               w_uq_p[l], w_ukv[l], name=f"mla_qkv_bwd{l}")
        d_w_uq_p = _matmul(qn, dqb, "tn", name=f"dw_uq{l}", tm=512, tn=2048, tk=512)
        d_w_ukv = _matmul(kvn, dkvb, "tn", name=f"dw_ukv{l}", tm=256, tn=2048, tk=512)
        S = x.shape[0]
        dproj = jnp.concatenate([dgm, dql, d3, dkvl, dkr, jnp.zeros((S, 128), BF16)], axis=1)
        d_w_in_p = _matmul(h_in, dproj, "tn", name=f"dw_in{l}", tm=1024, tn=1024, tk=512)
        dh = _matmul(dproj, w_in_p[l], "nt", name=f"dh{l}", tm=512, tn=1024, tk=1280, add=dr, add_scale=ALPHA)
        grads[l] = dict(w_in_p=d_w_in_p, q_g=d_qg[0], kv_g=d_kvg[0], w_uq_p=d_w_uq_p, w_ukv=d_w_ukv,
                        w_pool=d_w_pool, pool_scale=d_ps[0], conv_w=d_conv, w_out=d_w_out, b_out=d_b_out[0],
                        ln_g=d_ln_g[0], ln_b=d_ln_b[0])
    grad_x, d_emb_g, d_emb_b, _ = _ln_bwd(dh, x, emb_g, name="emb_ln_bwd")
    return loss_acc[0, 0], grad_x, d_emb_g[0], d_emb_b[0], grads


SMALL_ORDER = ("emb_ln_g", "emb_ln_b", "q_norm_g", "kv_norm_g", "w_pool", "pool_scale", "b_out", "ln_g", "ln_b")


def _pack_small(arrs, extra_rows):
    flat = jnp.concatenate([a.reshape(-1) for a in arrs])
    rows = flat.shape[0] // LANE
    total = -(-(rows + extra_rows) // 8) * 8
    return jnp.pad(flat, (0, total * LANE - flat.shape[0])).reshape(total, LANE)


def _unpack_small(packed, shapes):
    flat = packed.reshape(-1)
    out, off = [], 0
    for shp in shapes:
        n = 1
        for s in shp:
            n *= s
        out.append(flat[off:off + n].reshape(shp))
        off += n
    return out, off


def kernel(x, positions, emb_ln_g, emb_ln_b, w_in, q_norm_g, kv_norm_g, w_uq, w_ukv, w_pool, pool_scale, conv_w, w_out, b_out, ln_g, ln_b, loss_target, m_emb_ln_g, m_emb_ln_b, m_w_in, m_q_norm_g, m_kv_norm_g, m_w_uq, m_w_ukv, m_w_pool, m_pool_scale, m_conv_w, m_w_out, m_b_out, m_ln_g, m_ln_b, v_emb_ln_g, v_emb_ln_b, v_w_in, v_q_norm_g, v_kv_norm_g, v_w_uq, v_w_ukv, v_w_pool, v_pool_scale, v_conv_w, v_w_out, v_b_out, v_ln_g, v_ln_b):
    xi, yi, ci = lax.axis_index("x"), lax.axis_index("y"), lax.axis_index("c")
    chip = 2 * xi + yi

    conv_bits = lax.bitcast_convert_type(conv_w.reshape(DEPTH, 3 * 128), BF16).reshape(DEPTH, 3, 256)
    conv_bits = jnp.pad(conv_bits, ((0, 0), (0, 13), (0, 0)))
    shards = (w_in.astype(BF16), w_uq.astype(BF16), w_ukv.astype(BF16), w_out.astype(BF16), conv_bits)
    a_in, a_uq, a_ukv, a_out, a_conv = _allgather_weights(shards, name="allgather_weights")
    w_in_full = jnp.concatenate([a_in[k] for k in range(N_CHIPS)], axis=-1)
    w_in_p = _w_in_to_p(w_in_full)
    w_uq_full = jnp.concatenate([a_uq[k] for k in range(N_CHIPS)], axis=-1)
    w_uq_p = jnp.stack([_w_uq_to_p(w_uq_full[l]) for l in range(DEPTH)])
    w_ukv_full = jnp.concatenate([a_ukv[k] for k in range(N_CHIPS)], axis=-1)
    w_out_full = jnp.concatenate([a_out[k] for k in range(N_CHIPS)], axis=1)
    conv_parts = [lax.bitcast_convert_type(a_conv[k][:, :3, :].reshape(DEPTH, 3, 128, 2), F32)
                  for k in range(N_CHIPS)]
    conv_full = jnp.concatenate(conv_parts, axis=-1)

    loss_part, grad_x, d_emb_g, d_emb_b, grads = _local_step(
        x[0], positions[0], loss_target[0], emb_ln_g, emb_ln_b, w_in_p, q_norm_g, kv_norm_g, w_uq_p, w_ukv_full,
        w_pool, pool_scale, conv_full, w_out_full, b_out, ln_g, ln_b)

    def by_chip_cols(g_nat, width):
        return jnp.stack([g_nat[:, k * width:(k + 1) * width] for k in range(N_CHIPS)])

    g_in = jnp.stack([by_chip_cols(_w_in_from_p(grads[l]["w_in_p"]), 1232) for l in range(DEPTH)])
    g_uq = jnp.stack([by_chip_cols(_w_uq_from_p(grads[l]["w_uq_p"]), 384) for l in range(DEPTH)])
    g_ukv = jnp.stack([by_chip_cols(grads[l]["w_ukv"], 512) for l in range(DEPTH)])
    g_out = jnp.stack([grads[l]["w_out"].reshape(N_CHIPS, 512, D_MODEL) for l in range(DEPTH)])
    big = (g_in, g_uq, g_ukv, g_out)
    theirs = _exchange_layers(big, name="exchange_layers")
    c_idx = ci.reshape(1).astype(jnp.int32)
    j_idx = chip.reshape(1).astype(jnp.int32)
    parts = []
    for a, (g, t) in enumerate(zip(big, theirs)):
        _, _, R, C = g.shape
        parts.append(_add2(g.reshape(2, 4 * R, C), t.reshape(4 * R, C), c_idx, name=f"pair_sum{a}").reshape(4, R, C))
    recv = _scatter_chips(tuple(parts), name="scatter_chips")
    sums = tuple(_add4(p, r, j_idx, name=f"chip_sum{a}") for a, (p, r) in enumerate(zip(parts, recv)))
    others = _send_to_sibling(sums, name="send_to_sibling")

    small_g = [d_emb_g, d_emb_b,
               jnp.stack([grads[l]["q_g"] for l in range(DEPTH)]), jnp.stack([grads[l]["kv_g"] for l in range(DEPTH)]),
               jnp.stack([grads[l]["w_pool"] for l in range(DEPTH)]),
               jnp.stack([grads[l]["pool_scale"] for l in range(DEPTH)]),
               jnp.stack([grads[l]["b_out"] for l in range(DEPTH)]), jnp.stack([grads[l]["ln_g"] for l in range(DEPTH)]),
               jnp.stack([grads[l]["ln_b"] for l in range(DEPTH)]),
               jnp.stack([grads[l]["conv_w"] for l in range(DEPTH)]),
               jnp.pad(loss_part.reshape(1), (0, LANE - 1))]
    small_w = [emb_ln_g, emb_ln_b, q_norm_g, kv_norm_g, w_pool, pool_scale, b_out, ln_g, ln_b]
    small_m = [m_emb_ln_g, m_emb_ln_b, m_q_norm_g, m_kv_norm_g, m_w_pool, m_pool_scale, m_b_out, m_ln_g, m_ln_b]
    small_v = [v_emb_ln_g, v_emb_ln_b, v_q_norm_g, v_kv_norm_g, v_w_pool, v_pool_scale, v_b_out, v_ln_g, v_ln_b]
    extra = (DEPTH * 3 * 512 + LANE) // LANE
    packed_g = _pack_small(small_g, 0)
    gathered = _allgather_small(packed_g, name="allgather_small")
    g_tot, d_small, m_small, v_small = _small_sum_adamw(
        gathered, _pack_small(small_w, extra), _pack_small(small_m, extra), _pack_small(small_v, extra),
        name="small_sum_adamw")
    shapes = [w.shape for w in small_w]
    g_list, off = _unpack_small(g_tot, shapes)
    d_list, _ = _unpack_small(d_small, shapes)
    m_list, _ = _unpack_small(m_small, shapes)
    v_list, _ = _unpack_small(v_small, shapes)
    flat_tot = g_tot.reshape(-1)
    conv_tot = flat_tot[off:off + DEPTH * 3 * 512].reshape(DEPTH, 3, 512)
    loss = flat_tot[off + DEPTH * 3 * 512]
    g_conv = lax.dynamic_slice_in_dim(conv_tot, chip * 128, 128, axis=2)

    upd = {}
    for a, (nm, w, m, v) in enumerate((("w_in", w_in, m_w_in, v_w_in), ("w_uq", w_uq, m_w_uq, v_w_uq),
                                       ("w_ukv", w_ukv, m_w_ukv, v_w_ukv), ("w_out", w_out, m_w_out, v_w_out))):
        upd[nm] = _adamw_pair(w, sums[a], others[a], m, v, c_idx, name=f"adamw_{nm}")
    upd["conv_w"] = (g_conv,) + _adamw(conv_w, g_conv, m_conv_w, v_conv_w, name="adamw_conv_w")
    for i, nm in enumerate(SMALL_ORDER):
        upd[nm] = (g_list[i], d_list[i], m_list[i], v_list[i])

    order = ("emb_ln_g", "emb_ln_b", "w_in", "q_norm_g", "kv_norm_g", "w_uq", "w_ukv", "w_pool", "pool_scale",
             "conv_w", "w_out", "b_out", "ln_g", "ln_b")
    outs = [loss, grad_x[None]]
    for field in range(4):
        outs += [upd[nm][field] for nm in order]
    return tuple(outs)
```

```python
import collections

import jax
import jax.numpy as jnp
from jax import lax
from jax.experimental import pallas as pl
from jax.experimental.pallas import tpu as pltpu

F32 = jnp.float32
BF16 = jnp.bfloat16
MESH = pl.DeviceIdType.MESH

D_MODEL = 2048
DEPTH = 2
N_HEADS = 8
NOPE = 128
ROPE = 64
Q_LORA = 512
KV_LORA = 256
D_MLA = 1024
POOL_WINDOWS = (2, 4, 8, 16)
D_IN_PROJ = 4928
LN_EPS = 1e-5
RMS_EPS = 1e-6
ROPE_THETA = 10000.0
ALPHA = (2 * DEPTH) ** 0.25
SCALE = (NOPE + ROPE) ** -0.5
LOG2E = 1.4426950408889634
SCALE_LOG2E = SCALE * LOG2E
ADAM_LR = 0.001
ADAM_B1 = 0.9
ADAM_B2 = 0.999
ADAM_EPS = 1e-08
ADAM_WD = 0.01
ADAM_STEP = 10

NP = 5120
GAP_AT = 832
GAP = NP - D_IN_PROJ
W_MLA = 1024
W_MIX = NP - W_MLA
HALO = 16
LANE = 128
N_CHIPS = 4
N_DEV = 8
TQ = 512

NN = (((1,), (0,)), ((), ()))
NT = (((1,), (1,)), ((), ()))
TN = (((0,), (0,)), ((), ()))


def _pcall(kern, *, name, out_shape, grid=None, in_specs=None, out_specs=None, scratch=(), dims=None,
           vmem_mb=None, **kw):
    cp = {}
    if dims is not None:
        cp["dimension_semantics"] = dims
    if vmem_mb is not None:
        cp["vmem_limit_bytes"] = vmem_mb << 20
    args = dict(name=name, out_shape=out_shape, scratch_shapes=list(scratch),
                compiler_params=pltpu.CompilerParams(**cp))
    if grid is not None:
        args["grid"] = grid
    if in_specs is not None:
        args["in_specs"] = in_specs
    if out_specs is not None:
        args["out_specs"] = out_specs
    args.update(kw)
    return pl.pallas_call(kern, **args)


def _sigmoid(g):
    return 1.0 / (1.0 + jnp.exp(-g))


def _silu_and_grad(g):
    sig = _sigmoid(g)
    return g * sig, sig * (1.0 + g * (1.0 - sig))


def _matmul(a, b, mode, *, name, tm, tn, tk, out_dtype=F32, vmem_mb=48):
    if mode == "nn":
        (M, K), N = a.shape, b.shape[1]
    elif mode == "nt":
        (M, K), N = a.shape, b.shape[0]
    else:
        (K, M), N = a.shape, b.shape[1]
    tm, tn, tk = min(tm, M), min(tn, N), min(tk, K)
    assert M % tm == 0 and N % tn == 0 and K % tk == 0, (name, M, N, K)
    nk = K // tk
    dn = {"nn": NN, "nt": NT, "tn": TN}[mode]
    if mode == "tn":
        a_spec = pl.BlockSpec((tk, tm), lambda i, j, k: (k, i))
    else:
        a_spec = pl.BlockSpec((tm, tk), lambda i, j, k: (i, k))
    if mode == "nt":
        b_spec = pl.BlockSpec((tn, tk), lambda i, j, k: (j, k))
    else:
        b_spec = pl.BlockSpec((tk, tn), lambda i, j, k: (k, j))
    o_spec = pl.BlockSpec((tm, tn), lambda i, j, k: (i, j))

    def kern(a_ref, b_ref, o_ref, *rest):
        part = lax.dot_general(a_ref[...].astype(BF16), b_ref[...].astype(BF16), dn,
                               preferred_element_type=F32)
        if nk == 1:
            o_ref[...] = part.astype(out_dtype)
        else:
            acc_ref = rest[0]
            k = pl.program_id(2)

            @pl.when(k == 0)
            def _():
                acc_ref[...] = part

            @pl.when(k > 0)
            def _():
                acc_ref[...] += part

            @pl.when(k == nk - 1)
            def _():
                o_ref[...] = acc_ref[...].astype(out_dtype)

    scratch = [pltpu.VMEM((tm, tn), F32)] if nk > 1 else []
    return _pcall(kern, name=name, out_shape=jax.ShapeDtypeStruct((M, N), out_dtype),
                  grid=(M // tm, N // tn, nk), in_specs=[a_spec, b_spec], out_specs=o_spec, scratch=scratch,
                  dims=("parallel", "parallel", "arbitrary"), vmem_mb=vmem_mb)(a, b)


def _dproj_times_w(d_mla, d_mix, wt, add, add_scale, *, name):
    S = d_mla.shape[0]
    Dm = wt.shape[1]
    tm, tn, tk = min(512, S), 1024, W_MLA
    nk = NP // tk

    def kern(a1_ref, a2_ref, b_ref, add_ref, o_ref, acc_ref):
        k = pl.program_id(2)

        @pl.when(k == 0)
        def _():
            acc_ref[...] = jnp.dot(a1_ref[...], b_ref[...], preferred_element_type=F32)

        @pl.when(k > 0)
        def _():
            acc_ref[...] += jnp.dot(a2_ref[...], b_ref[...], preferred_element_type=F32)

        @pl.when(k == nk - 1)
        def _():
            o_ref[...] = add_scale * add_ref[...] + acc_ref[...]

    o_spec = pl.BlockSpec((tm, tn), lambda i, j, k: (i, j))
    return _pcall(kern, name=name, out_shape=jax.ShapeDtypeStruct((S, Dm), F32), grid=(S // tm, Dm // tn, nk),
                  in_specs=[pl.BlockSpec((tm, tk), lambda i, j, k: (i, 0)),
                            pl.BlockSpec((tm, tk), lambda i, j, k: (i, jnp.maximum(k - 1, 0))),
                            pl.BlockSpec((tk, tn), lambda i, j, k: (k, j)), o_spec],
                  out_specs=o_spec, scratch=[pltpu.VMEM((tm, tn), F32)],
                  dims=("parallel", "parallel", "arbitrary"), vmem_mb=48)(d_mla, d_mix, wt, add)


def _dproj_t_times_h(d_mla, d_mix, h, *, name):
    S, Dm = h.shape
    tm, tn, tk = W_MLA, 1024, min(512, S)
    nk = S // tk

    def kern(a1_ref, a2_ref, b_ref, o_ref, acc_ref):
        i = pl.program_id(0)
        k = pl.program_id(2)
        b = b_ref[...].astype(BF16)

        def accumulate(part):
            @pl.when(k == 0)
            def _():
                acc_ref[...] = part

            @pl.when(k > 0)
            def _():
                acc_ref[...] += part

        @pl.when(i == 0)
        def _():
            accumulate(lax.dot_general(a1_ref[...], b, TN, preferred_element_type=F32))

        @pl.when(i > 0)
        def _():
            accumulate(lax.dot_general(a2_ref[...], b, TN, preferred_element_type=F32))

        @pl.when(k == nk - 1)
        def _():
            o_ref[...] = acc_ref[...]

    return _pcall(kern, name=name, out_shape=jax.ShapeDtypeStruct((NP, Dm), F32), grid=(NP // tm, Dm // tn, nk),
                  in_specs=[pl.BlockSpec((tk, tm), lambda i, j, k: (jnp.where(i == 0, k, nk - 1), 0)),
                            pl.BlockSpec((tk, tm), lambda i, j, k: (jnp.where(i == 0, 0, k), jnp.maximum(i - 1, 0))),
                            pl.BlockSpec((tk, tn), lambda i, j, k: (k, j))],
                  out_specs=pl.BlockSpec((tm, tn), lambda i, j, k: (i, j)), scratch=[pltpu.VMEM((tm, tn), F32)],
                  dims=("parallel", "parallel", "arbitrary"), vmem_mb=48)(d_mla, d_mix, h)


def _ln_fwd(x, g, b, *, name):
    S, Dm = x.shape
    tm = min(512, S)

    def kern(x_ref, g_ref, b_ref, y_ref):
        xf = x_ref[...]
        mu = jnp.mean(xf, axis=-1, keepdims=True)
        xc = xf - mu
        var = jnp.mean(xc * xc, axis=-1, keepdims=True)
        y_ref[...] = xc * lax.rsqrt(var + LN_EPS) * g_ref[...] + b_ref[...]

    row = pl.BlockSpec((tm, Dm), lambda i: (i, 0))
    vec = pl.BlockSpec((1, Dm), lambda i: (0, 0))
    return _pcall(kern, name=name, out_shape=jax.ShapeDtypeStruct((S, Dm), F32), grid=(S // tm,),
                  in_specs=[row, vec, vec], out_specs=row, dims=("parallel",), vmem_mb=48)(
                      x, g.reshape(1, Dm), b.reshape(1, Dm))


def _ln_bwd(dy, r, g, *, name):
    S, Dm = r.shape
    tm = min(512, S)

    def kern(dy_ref, r_ref, g_ref, dr_ref, dg_ref, db_ref, ds_ref):
        @pl.when(pl.program_id(0) == 0)
        def _():
            dg_ref[...] = jnp.zeros_like(dg_ref)
            db_ref[...] = jnp.zeros_like(db_ref)
            ds_ref[...] = jnp.zeros_like(ds_ref)

        rf = r_ref[...]
        dyf = dy_ref[...]
        mu = jnp.mean(rf, axis=-1, keepdims=True)
        xc = rf - mu
        var = jnp.mean(xc * xc, axis=-1, keepdims=True)
        rstd = lax.rsqrt(var + LN_EPS)
        xhat = xc * rstd
        dxh = dyf * g_ref[...]
        c1 = jnp.mean(dxh, axis=-1, keepdims=True)
        c2 = jnp.mean(dxh * xhat, axis=-1, keepdims=True)
        dr = rstd * (dxh - c1 - xhat * c2)
        dr_ref[...] = dr
        dg_ref[...] += jnp.sum(dyf * xhat, axis=0, keepdims=True)
        db_ref[...] += jnp.sum(dyf, axis=0, keepdims=True)
        ds_ref[...] += jnp.sum(dr, axis=0, keepdims=True)

    row = pl.BlockSpec((tm, Dm), lambda i: (i, 0))
    vec = pl.BlockSpec((1, Dm), lambda i: (0, 0))
    vshape = jax.ShapeDtypeStruct((1, Dm), F32)
    return _pcall(kern, name=name, out_shape=(jax.ShapeDtypeStruct((S, Dm), F32), vshape, vshape, vshape),
                  grid=(S // tm,), in_specs=[row, row, vec], out_specs=(row, vec, vec, vec),
                  dims=("arbitrary",), vmem_mb=48)(dy, r, g.reshape(1, Dm))


def _loss_and_dy(y, target, *, name):
    S, Dm = y.shape
    tm = min(512, S)

    def kern(y_ref, t_ref, dy_ref, l_ref):
        @pl.when(pl.program_id(0) == 0)
        def _():
            l_ref[...] = jnp.zeros_like(l_ref)

        e = y_ref[...] - t_ref[...]
        dy_ref[...] = e / float(Dm)
        per_row = jnp.mean(e * e, axis=-1, keepdims=True)
        l_ref[...] += 0.5 * jnp.sum(per_row, axis=0, keepdims=True)

    row = pl.BlockSpec((tm, Dm), lambda i: (i, 0))
    acc = pl.BlockSpec((8, LANE), lambda i: (0, 0))
    return _pcall(kern, name=name,
                  out_shape=(jax.ShapeDtypeStruct((S, Dm), F32), jax.ShapeDtypeStruct((8, LANE), F32)),
                  grid=(S // tm,), in_specs=[row, row], out_specs=(row, acc), dims=("arbitrary",), vmem_mb=48)(
                      y, target)


def _rot_sum(t):
    return pltpu.roll(t, 32, 1) + pltpu.roll(t, 96, 1)


def _mla_qkv(proj, cos_t, sin_t, qg, kvg, wuq_t, wukv_t, *, name):
    S = proj.shape[0]
    tm = min(256, S)

    def kern(ql_ref, kvl_ref, kr_ref, cos_ref, sin_ref, qg_ref, kvg_ref, wuq_ref, wukv_ref,
             qc_ref, kc_ref, v_ref, vt_ref, qn_ref, kvn_ref):
        cosv = cos_ref[...]
        sinv = sin_ref[...]

        def rope(t):
            return t * cosv + _rot_sum(t) * sinv

        ql = ql_ref[...]
        qn = (ql * lax.rsqrt(jnp.mean(ql * ql, axis=-1, keepdims=True) + RMS_EPS) * qg_ref[...]).astype(BF16)
        kvl = kvl_ref[...]
        kvn = (kvl * lax.rsqrt(jnp.mean(kvl * kvl, axis=-1, keepdims=True) + RMS_EPS) * kvg_ref[...]).astype(BF16)
        qn_ref[...] = qn
        kvn_ref[...] = kvn
        q = lax.dot_general(qn, wuq_ref[...], NT, preferred_element_type=F32)
        kv = lax.dot_general(kvn, wukv_ref[...], NT, preferred_element_type=F32)
        kr = rope(kr_ref[...]).astype(BF16)
        for h in range(N_HEADS):
            c0 = 256 * h
            qc_ref[:, c0:c0 + 128] = q[:, c0:c0 + 128].astype(BF16)
            qc_ref[:, c0 + 128:c0 + 256] = rope(q[:, c0 + 128:c0 + 256]).astype(BF16)
            kc_ref[:, c0:c0 + 128] = kv[:, c0:c0 + 128].astype(BF16)
            kc_ref[:, c0 + 128:c0 + 256] = kr
            vh = kv[:, c0 + 128:c0 + 256]
            v_ref[:, 128 * h:128 * h + 128] = vh.astype(BF16)
            vt_ref[h] = jnp.transpose(vh).astype(BF16)

    def row(w, blk):
        return pl.BlockSpec((tm, w), lambda i: (i, blk))

    def full(shape):
        return pl.BlockSpec(shape, lambda i: (0,) * len(shape))

    t = min(TQ, S)
    per = t // tm
    vt_spec = pl.BlockSpec((N_HEADS, None, 128, tm), lambda i: (0, i // per, 0, i % per))
    outs = (jax.ShapeDtypeStruct((S, 2048), BF16), jax.ShapeDtypeStruct((S, 2048), BF16),
            jax.ShapeDtypeStruct((S, 1024), BF16), jax.ShapeDtypeStruct((N_HEADS, S // t, 128, t), BF16),
            jax.ShapeDtypeStruct((S, Q_LORA), BF16), jax.ShapeDtypeStruct((S, KV_LORA), BF16))
    return _pcall(kern, name=name, out_shape=outs, grid=(S // tm,),
                  in_specs=[row(512, 0), row(256, 2), row(128, 6), row(128, 0), row(128, 0),
                            full((1, Q_LORA)), full((1, KV_LORA)), full((2048, Q_LORA)), full((2048, KV_LORA))],
                  out_specs=(row(2048, 0), row(2048, 0), row(1024, 0), vt_spec, row(512, 0), row(256, 0)),
                  dims=("parallel",), vmem_mb=48)(
                      proj, proj, proj, cos_t, sin_t, qg.reshape(1, -1), kvg.reshape(1, -1), wuq_t, wukv_t)


def _mla_qkv_bwd(dqc, dkc, dv, proj, cos_t, sin_t, qg, kvg, wuq_t, wukv_t, *, name):
    S = proj.shape[0]
    tm = min(256, S)

    def kern(dq_ref, dk_ref, dv_ref, ql_ref, kvl_ref, cos_ref, sin_ref, qg_ref, kvg_ref, wuq_ref, wukv_ref,
             dqb_ref, dkvb_ref, dml_ref, dqg_ref, dkvg_ref):
        @pl.when(pl.program_id(0) == 0)
        def _():
            dqg_ref[...] = jnp.zeros_like(dqg_ref)
            dkvg_ref[...] = jnp.zeros_like(dkvg_ref)

        cosv = cos_ref[...]
        sinv = sin_ref[...]

        def unrope(t):
            return t * cosv - _rot_sum(t) * sinv

        dkr = jnp.zeros((tm, 128), F32)
        for h in range(N_HEADS):
            c0 = 256 * h
            dqb_ref[:, c0:c0 + 128] = dq_ref[:, c0:c0 + 128].astype(BF16)
            dqb_ref[:, c0 + 128:c0 + 256] = unrope(dq_ref[:, c0 + 128:c0 + 256]).astype(BF16)
            dkvb_ref[:, c0:c0 + 128] = dk_ref[:, c0:c0 + 128].astype(BF16)
            dkvb_ref[:, c0 + 128:c0 + 256] = dv_ref[:, 128 * h:128 * h + 128].astype(BF16)
            dkr = dkr + dk_ref[:, c0 + 128:c0 + 256]

        def rms_bwd(x, g, dy):
            n = x.shape[-1]
            rs = lax.rsqrt(jnp.mean(x * x, axis=-1, keepdims=True) + RMS_EPS)
            dyg = dy * g
            dx = rs * dyg - x * (rs * rs * rs) * (jnp.sum(dyg * x, axis=-1, keepdims=True) / n)
            return dx, jnp.sum(dy * (x * rs), axis=0, keepdims=True)

        dqn = jnp.dot(dqb_ref[...], wuq_ref[...], preferred_element_type=F32)
        dql, dqg = rms_bwd(ql_ref[...], qg_ref[...], dqn)
        dqg_ref[...] += dqg
        dkvn = jnp.dot(dkvb_ref[...], wukv_ref[...], preferred_element_type=F32)
        dkvl, dkvg = rms_bwd(kvl_ref[...], kvg_ref[...], dkvn)
        dkvg_ref[...] += dkvg
        dml_ref[:, 0:512] = dql.astype(BF16)
        dml_ref[:, 512:768] = dkvl.astype(BF16)
        dml_ref[:, 768:896] = unrope(dkr).astype(BF16)
        dml_ref[:, 896:1024] = jnp.zeros((tm, 128), BF16)

    def row(w, blk):
        return pl.BlockSpec((tm, w), lambda i: (i, blk))

    def full(shape):
        return pl.BlockSpec(shape, lambda i: (0,) * len(shape))

    outs = (jax.ShapeDtypeStruct((S, 2048), BF16), jax.ShapeDtypeStruct((S, 2048), BF16),
            jax.ShapeDtypeStruct((S, W_MLA), BF16), jax.ShapeDtypeStruct((1, Q_LORA), F32),
            jax.ShapeDtypeStruct((1, KV_LORA), F32))
    return _pcall(kern, name=name, out_shape=outs, grid=(S // tm,),
                  in_specs=[row(2048, 0), row(2048, 0), row(1024, 0), row(512, 0), row(256, 2),
                            row(128, 0), row(128, 0), full((1, Q_LORA)), full((1, KV_LORA)),
                            full((2048, Q_LORA)), full((2048, KV_LORA))],
                  out_specs=(row(2048, 0), row(2048, 0), row(W_MLA, 0), full((1, Q_LORA)), full((1, KV_LORA))),
                  dims=("arbitrary",), vmem_mb=56)(
                      dqc, dkc, dv, proj, proj, cos_t, sin_t, qg.reshape(1, -1), kvg.reshape(1, -1), wuq_t, wukv_t)


def _kq_mask(t):
    krow = lax.broadcasted_iota(jnp.int32, (t, t), 0)
    qcol = lax.broadcasted_iota(jnp.int32, (t, t), 1)
    return krow <= qcol


def _flash_fwd(qc, kc, vt, *, name):
    S = qc.shape[0]
    t = min(TQ, S)
    n = S // t

    def kern(q_ref, k_ref, vt_ref, o_ref, lse_ref, m_s, l_s, acc_s):
        qi = pl.program_id(1)
        m_s[...] = jnp.full_like(m_s, -jnp.inf)
        l_s[...] = jnp.zeros_like(l_s)
        acc_s[...] = jnp.zeros_like(acc_s)

        def step(kb, masked):
            k0 = pl.multiple_of(kb * t, t)
            st = lax.dot_general(k_ref[pl.ds(k0, t), :], q_ref[...], NT, preferred_element_type=F32)
            if masked:
                st = jnp.where(_kq_mask(t), st, -jnp.inf)
            m_prev = m_s[...]
            m_new = jnp.maximum(m_prev, jnp.max(st, axis=0, keepdims=True))
            a = jnp.exp2((m_prev - m_new) * SCALE_LOG2E)
            pt = jnp.exp2((st - m_new) * SCALE_LOG2E)
            l_s[...] = a * l_s[...] + jnp.sum(pt, axis=0, keepdims=True)
            acc_s[...] = a * acc_s[...] + jnp.dot(vt_ref[kb], pt.astype(BF16), preferred_element_type=F32)
            m_s[...] = m_new

        def body(kb, carry):
            step(kb, False)
            return carry

        lax.fori_loop(0, qi, body, 0)
        step(qi, True)
        o_ref[...] = jnp.transpose(acc_s[...] / l_s[...])
        lse_ref[pl.ds(qi, 1), :] = m_s[...] * SCALE_LOG2E + jnp.log2(l_s[...])

    q_spec = pl.BlockSpec((t, 256), lambda h, qi: (qi, h))
    k_spec = pl.BlockSpec((S, 256), lambda h, qi: (0, h))
    vt_spec = pl.BlockSpec((None, n, 128, t), lambda h, qi: (h, 0, 0, 0))
    o_spec = pl.BlockSpec((t, 128), lambda h, qi: (qi, h))
    lse_spec = pl.BlockSpec((None, n, t), lambda h, qi: (h, 0, 0))
    return _pcall(kern, name=name,
                  out_shape=(jax.ShapeDtypeStruct((S, D_MLA), F32), jax.ShapeDtypeStruct((N_HEADS, n, t), F32)),
                  grid=(N_HEADS, n), in_specs=[q_spec, k_spec, vt_spec], out_specs=(o_spec, lse_spec),
                  scratch=[pltpu.VMEM((1, t), F32), pltpu.VMEM((1, t), F32), pltpu.VMEM((128, t), F32)],
                  dims=("parallel", "arbitrary"), vmem_mb=48)(qc, kc, vt)


def _attn_delta(o, do, *, name):
    S = o.shape[0]
    t = min(TQ, S)
    n = S // t

    def kern(o_ref, do_ref, dl_ref):
        i = pl.program_id(0)
        prod = o_ref[...] * do_ref[...]
        lane = lax.broadcasted_iota(jnp.int32, (t, LANE), 1)
        dmat = jnp.zeros((t, LANE), F32)
        for h in range(N_HEADS):
            dmat = jnp.where(lane == h, jnp.sum(prod[:, 128 * h:128 * h + 128], axis=1, keepdims=True), dmat)
        dmat_t = jnp.transpose(dmat)
        for h in range(N_HEADS):
            dl_ref[h, pl.ds(i, 1), :] = dmat_t[h:h + 1, :]

    row = pl.BlockSpec((t, D_MLA), lambda i: (i, 0))
    return _pcall(kern, name=name, out_shape=jax.ShapeDtypeStruct((N_HEADS, n, t), F32), grid=(n,),
                  in_specs=[row, row], out_specs=pl.BlockSpec((N_HEADS, n, t), lambda i: (0, 0, 0)),
                  dims=("arbitrary",), vmem_mb=48)(o, do)


def _flash_bwd(qc, kc, v, do, lse2, delta, *, name):
    S = qc.shape[0]
    t = min(TQ, S)
    n = S // t

    def kern(q_ref, k_ref, v_ref, do_ref, lse_ref, dl_ref, dq_ref, dk_ref, dv_ref):
        ki = pl.program_id(1)

        @pl.when(ki == 0)
        def _():
            dq_ref[...] = jnp.zeros_like(dq_ref)

        dk_ref[...] = jnp.zeros_like(dk_ref)
        dv_ref[...] = jnp.zeros_like(dv_ref)

        def step(qb, masked):
            q0 = pl.multiple_of(qb * t, t)
            kt = k_ref[...]
            qblk = q_ref[pl.ds(q0, t), :]
            dob = do_ref[pl.ds(q0, t), :].astype(BF16)
            st = lax.dot_general(kt, qblk, NT, preferred_element_type=F32)
            pt = jnp.exp2(st * SCALE_LOG2E - lse_ref[pl.ds(qb, 1), :])
            if masked:
                pt = jnp.where(_kq_mask(t), pt, 0.0)
            dv_ref[...] += jnp.dot(pt.astype(BF16), dob, preferred_element_type=F32)
            dpt = lax.dot_general(v_ref[...], dob, NT, preferred_element_type=F32)
            dst = (pt * (dpt - dl_ref[pl.ds(qb, 1), :]) * SCALE).astype(BF16)
            dk_ref[...] += jnp.dot(dst, qblk, preferred_element_type=F32)
            dq_ref[pl.ds(q0, t), :] += lax.dot_general(dst, kt, TN, preferred_element_type=F32)

        step(ki, True)

        def body(qb, carry):
            step(qb, False)
            return carry

        lax.fori_loop(ki + 1, n, body, 0)

    def whole(w):
        return pl.BlockSpec((S, w), lambda h, ki: (0, h))

    def krow(w):
        return pl.BlockSpec((t, w), lambda h, ki: (ki, h))

    stat = pl.BlockSpec((None, n, t), lambda h, ki: (h, 0, 0))
    return _pcall(kern, name=name,
                  out_shape=(jax.ShapeDtypeStruct((S, 2048), F32), jax.ShapeDtypeStruct((S, 2048), F32),
                             jax.ShapeDtypeStruct((S, D_MLA), F32)),
                  grid=(N_HEADS, n),
                  in_specs=[whole(256), krow(256), krow(128), whole(128), stat, stat],
                  out_specs=(whole(256), krow(256), krow(128)),
                  dims=("parallel", "arbitrary"), vmem_mb=56)(qc, kc, v, do, lse2, delta)


def _mixer_specs(S, tm):
    hb = tm // HALO
    last_hb = S // HALO - 1

    def main(w, blk):
        return pl.BlockSpec((tm, w), lambda i: (i, blk))

    def prev(w, blk):
        return pl.BlockSpec((HALO, w), lambda i: (jnp.maximum(i * hb - 1, 0), blk))

    def nxt(w, blk):
        return pl.BlockSpec((HALO, w), lambda i: (jnp.minimum((i + 1) * hb, last_hb), blk))

    def full(shape):
        return pl.BlockSpec(shape, lambda i: (0,) * len(shape))

    return main, prev, nxt, full


def _fill_halo(i, xp, xu, hp_ref, hch_ref, hcc_ref, pin_ref, ch_ref, cc_ref, tm):
    first = i == 0
    xp[0:HALO, :] = jnp.where(first, 0.0, hp_ref[...])
    xp[HALO:HALO + tm, :] = pin_ref[...]
    xu[0:HALO, :] = jnp.where(first, 0.0, hch_ref[...] * hcc_ref[...])
    xu[HALO:HALO + tm, :] = cc_ref[...] * ch_ref[...]


def _pooled(xp, g, t1, tm):
    w = POOL_WINDOWS[g]
    lanes = slice(128 * g, 128 * g + 128)
    x0 = xp[HALO:HALO + tm, lanes]
    acc = x0
    for k in range(1, w):
        acc = acc + xp[HALO - k:HALO - k + tm, lanes]
    return acc / jnp.minimum(t1, float(w)) - x0


def _conv_fwd(xu, cw_ref, tm):
    return (cw_ref[0:1, :] * xu[HALO - 2:HALO - 2 + tm, :] + cw_ref[1:2, :] * xu[HALO - 1:HALO - 1 + tm, :]
            + cw_ref[2:3, :] * xu[HALO:HALO + tm, :])


def _mixer_fwd(proj, o, wpool, ps, convw, *, name):
    S = proj.shape[0]
    tm = min(256, S)
    main, prev, _, full = _mixer_specs(S, tm)

    def kern(gm_ref, pin_ref, gp_ref, ch_ref, cb_ref, cc_ref, gc_ref, hp_ref, hch_ref, hcc_ref,
             o_ref, wp_ref, ps_ref, cw_ref, mix_ref, xp, xu):
        i = pl.program_id(0)
        _fill_halo(i, xp, xu, hp_ref, hch_ref, hcc_ref, pin_ref, ch_ref, cc_ref, tm)
        t1 = (i * tm + lax.broadcasted_iota(jnp.int32, (tm, 1), 0) + 1).astype(F32)
        for g in range(4):
            lanes = slice(128 * g, 128 * g + 128)
            pooled = _pooled(xp, g, t1, tm)
            z = jnp.dot(pooled.astype(BF16), wp_ref[g].astype(BF16), preferred_element_type=F32)
            gp = gp_ref[:, lanes]
            y = z * ps_ref[:, lanes] * (gp * _sigmoid(gp))
            mix_ref[:, 1024 + 128 * g:1024 + 128 * g + 128] = y.astype(BF16)
        gc = gc_ref[...]
        mix_ref[:, 1536:2048] = (cb_ref[...] * _conv_fwd(xu, cw_ref, tm) * (gc * _sigmoid(gc))).astype(BF16)
        gm = gm_ref[...]
        mix_ref[:, 0:1024] = (o_ref[...] * (gm * _sigmoid(gm))).astype(BF16)

    return _pcall(kern, name=name, out_shape=jax.ShapeDtypeStruct((S, 2048), BF16), grid=(S // tm,),
                  in_specs=[main(1024, 1), main(512, 4), main(512, 5), main(512, 6), main(512, 7), main(512, 8),
                            main(512, 9), prev(512, 4), prev(512, 6), prev(512, 8),
                            main(1024, 0), full((4, 128, 128)), full((1, 512)), full((3, 512))],
                  out_specs=main(2048, 0),
                  scratch=[pltpu.VMEM((tm + HALO, 512), F32), pltpu.VMEM((tm + HALO, 512), F32)],
                  dims=("parallel",), vmem_mb=48)(
                      proj, proj, proj, proj, proj, proj, proj, proj, proj, proj, o, wpool, ps.reshape(1, 512), convw)


def _mixer_bwd(dmix, proj, o, wpool, ps, convw, *, name):
    S = proj.shape[0]
    tm = min(256, S)
    n = S // tm
    main, prev, nxt, full = _mixer_specs(S, tm)

    def kern(dm_ref, dmn_ref, gm_ref, pin_ref, gp_ref, ch_ref, cb_ref, cc_ref, gc_ref,
             hp_ref, hch_ref, hcc_ref, gpn_ref, cbn_ref, gcn_ref, o_ref, wp_ref, ps_ref, cw_ref,
             d_ref, do_ref, dwp_ref, dps_ref, dcw_ref, xp, xu, ee, ed):
        i = pl.program_id(0)
        last = i == n - 1

        @pl.when(i == 0)
        def _():
            dwp_ref[...] = jnp.zeros_like(dwp_ref)
            dps_ref[...] = jnp.zeros_like(dps_ref)
            dcw_ref[...] = jnp.zeros_like(dcw_ref)

        _fill_halo(i, xp, xu, hp_ref, hch_ref, hcc_ref, pin_ref, ch_ref, cc_ref, tm)
        t1 = (i * tm + lax.broadcasted_iota(jnp.int32, (tm, 1), 0) + 1).astype(F32)
        t1n = ((i + 1) * tm + lax.broadcasted_iota(jnp.int32, (HALO, 1), 0) + 1).astype(F32)
        c_pin, c_gp, c_ch, c_cb, c_cc, c_gc = 1024, 1536, 2048, 2560, 3072, 3584

        for g in range(4):
            w = float(POOL_WINDOWS[g])
            lanes = slice(128 * g, 128 * g + 128)
            pooled = _pooled(xp, g, t1, tm)
            pb = pooled.astype(BF16)
            wp = wp_ref[g].astype(BF16)
            z = jnp.dot(pb, wp, preferred_element_type=F32)
            psl = ps_ref[:, lanes]
            sg, dsg = _silu_and_grad(gp_ref[:, lanes])
            dmp = dm_ref[:, 1024 + 128 * g:1024 + 128 * g + 128]
            dyp = dmp * sg
            d_ref[:, c_gp + 128 * g:c_gp + 128 * g + 128] = (dmp * (z * psl) * dsg).astype(BF16)
            dps_ref[:, lanes] += jnp.sum(dyp * z, axis=0, keepdims=True)
            dz = (dyp * psl).astype(BF16)
            dwp_ref[g] += lax.dot_general(pb, dz, TN, preferred_element_type=F32)
            dpl = lax.dot_general(dz, wp, NT, preferred_element_type=F32)
            ee[0:tm, lanes] = dpl / jnp.minimum(t1, w)
            gpn = gpn_ref[:, lanes]
            dzn = (dmn_ref[:, lanes] * (gpn * _sigmoid(gpn)) * psl).astype(BF16)
            dpn = lax.dot_general(dzn, wp, NT, preferred_element_type=F32)
            ee[tm:tm + HALO, lanes] = jnp.where(last, 0.0, dpn / jnp.minimum(t1n, w))
            acc = ee[0:tm, lanes]
            for k in range(1, POOL_WINDOWS[g]):
                acc = acc + ee[k:k + tm, lanes]
            d_ref[:, c_pin + 128 * g:c_pin + 128 * g + 128] = (acc - dpl).astype(BF16)

        yc = _conv_fwd(xu, cw_ref, tm)
        sgc, dsgc = _silu_and_grad(gc_ref[...])
        cb = cb_ref[...]
        dmc = dm_ref[:, 1536:2048]
        d_ref[:, c_gc:c_gc + 512] = (dmc * cb * yc * dsgc).astype(BF16)
        d_ref[:, c_cb:c_cb + 512] = (dmc * yc * sgc).astype(BF16)
        dyc = dmc * cb * sgc
        ed[0:tm, :] = dyc
        gcn = gcn_ref[...]
        ed[tm:tm + HALO, :] = jnp.where(last, 0.0, dmn_ref[:, 512:1024] * cbn_ref[...] * (gcn * _sigmoid(gcn)))
        dcw_ref[0:1, :] += jnp.sum(dyc * xu[HALO - 2:HALO - 2 + tm, :], axis=0, keepdims=True)
        dcw_ref[1:2, :] += jnp.sum(dyc * xu[HALO - 1:HALO - 1 + tm, :], axis=0, keepdims=True)
        dcw_ref[2:3, :] += jnp.sum(dyc * xu[HALO:HALO + tm, :], axis=0, keepdims=True)
        du = cw_ref[2:3, :] * dyc + cw_ref[1:2, :] * ed[1:1 + tm, :] + cw_ref[0:1, :] * ed[2:2 + tm, :]
        d_ref[:, c_cc:c_cc + 512] = (du * ch_ref[...]).astype(BF16)
        d_ref[:, c_ch:c_ch + 512] = (du * cc_ref[...]).astype(BF16)

        sgm, dsgm = _silu_and_grad(gm_ref[...])
        dmm = dm_ref[:, 0:1024]
        do_ref[...] = dmm * sgm
        d_ref[:, 0:1024] = (dmm * o_ref[...] * dsgm).astype(BF16)

    outs = (jax.ShapeDtypeStruct((S, W_MIX), BF16), jax.ShapeDtypeStruct((S, 1024), F32),
            jax.ShapeDtypeStruct((4, 128, 128), F32), jax.ShapeDtypeStruct((1, 512), F32),
            jax.ShapeDtypeStruct((3, 512), F32))
    scr = [pltpu.VMEM((tm + HALO, 512), F32) for _ in range(4)]
    return _pcall(kern, name=name, out_shape=outs, grid=(n,),
                  in_specs=[main(2048, 0), nxt(1024, 1),
                            main(1024, 1), main(512, 4), main(512, 5), main(512, 6), main(512, 7), main(512, 8),
                            main(512, 9), prev(512, 4), prev(512, 6), prev(512, 8),
                            nxt(512, 5), nxt(512, 7), nxt(512, 9),
                            main(1024, 0), full((4, 128, 128)), full((1, 512)), full((3, 512))],
                  out_specs=(main(W_MIX, 0), main(1024, 0), full((4, 128, 128)), full((1, 512)), full((3, 512))),
                  scratch=scr, dims=("arbitrary",), vmem_mb=56)(
                      dmix, dmix, proj, proj, proj, proj, proj, proj, proj, proj, proj, proj, proj, proj, proj,
                      o, wpool, ps.reshape(1, 512), convw)


def _outproj_ln(mix, wout, h, bout, g, b, *, name):
    S, Dm = h.shape
    tm = min(256, S)

    def kern(mix_ref, w_ref, h_ref, bo_ref, g_ref, b_ref, y_ref, r_ref):
        out = jnp.dot(mix_ref[...], w_ref[...], preferred_element_type=F32) + bo_ref[...]
        r = ALPHA * h_ref[...] + out
        r_ref[...] = r
        mu = jnp.mean(r, axis=-1, keepdims=True)
        xc = r - mu
        var = jnp.mean(xc * xc, axis=-1, keepdims=True)
        y_ref[...] = xc * lax.rsqrt(var + LN_EPS) * g_ref[...] + b_ref[...]

    row = pl.BlockSpec((tm, Dm), lambda i: (i, 0))
    vec = pl.BlockSpec((1, Dm), lambda i: (0, 0))
    wsp = pl.BlockSpec((Dm, Dm), lambda i: (0, 0))
    sds = jax.ShapeDtypeStruct((S, Dm), F32)
    return _pcall(kern, name=name, out_shape=(sds, sds), grid=(S // tm,),
                  in_specs=[row, wsp, row, vec, vec, vec], out_specs=(row, row), dims=("parallel",), vmem_mb=56)(
                      mix, wout, h, bout.reshape(1, Dm), g.reshape(1, Dm), b.reshape(1, Dm))


def _adamw_math(w, g, m, v):
    m = ADAM_B1 * m + (1.0 - ADAM_B1) * g
    v = ADAM_B2 * v + (1.0 - ADAM_B2) * (g * g)
    m_hat = m / (1.0 - ADAM_B1 ** ADAM_STEP)
    v_hat = v / (1.0 - ADAM_B2 ** ADAM_STEP)
    delta = -ADAM_LR * (m_hat / (jnp.sqrt(v_hat) + ADAM_EPS) + ADAM_WD * w)
    return delta, m, v


def _row_tile(R, C):
    best = None
    for cand in range(8, R, 8):
        if R % cand == 0 and cand * C <= 256 * 1024:
            best = cand
    return best if best is not None else R


def _adamw(w, g, m, v, *, name):
    shape = w.shape
    C = shape[-1]
    R = 1
    for s in shape[:-1]:
        R *= s
    tr = _row_tile(R, C)

    def kern(w_ref, g_ref, m_ref, v_ref, d_ref, mo_ref, vo_ref):
        d, mn, vn = _adamw_math(w_ref[...], g_ref[...], m_ref[...], v_ref[...])
        d_ref[...] = d
        mo_ref[...] = mn
        vo_ref[...] = vn

    blk = pl.BlockSpec((tr, C), lambda i: (i, 0))
    sds = jax.ShapeDtypeStruct((R, C), F32)
    outs = _pcall(kern, name=name, out_shape=(sds, sds, sds), grid=(R // tr,), in_specs=[blk] * 4,
                  out_specs=(blk, blk, blk), dims=("parallel",), vmem_mb=48)(
                      w.reshape(R, C), g.reshape(R, C), m.reshape(R, C), v.reshape(R, C))
    return tuple(t.reshape(shape) for t in outs)


def _adamw_halves(w, m, v, halves, c_idx, *, name):
    _, R, C = w.shape
    ch = C // 2
    tr = _row_tile(R, ch)
    nb = R // tr

    def kern(c_ref, w_ref, a0_ref, b0_ref, a1_ref, b1_ref, m_ref, v_ref, g_ref, d_ref, mo_ref, vo_ref):
        layer = pl.program_id(0) // nb
        mine = pl.program_id(1) == c_ref[0]
        g = jnp.where(layer == 0, jnp.where(mine, a0_ref[...], b0_ref[...]),
                      jnp.where(mine, a1_ref[...], b1_ref[...]))
        g_ref[...] = g
        d, mn, vn = _adamw_math(w_ref[...], g, m_ref[...], v_ref[...])
        d_ref[...] = d
        mo_ref[...] = mn
        vo_ref[...] = vn

    full = pl.BlockSpec((tr, ch), lambda i, hc, c: (i, hc))
    half = pl.BlockSpec((tr, ch), lambda i, hc, c: (i % nb, 0))
    gs = pltpu.PrefetchScalarGridSpec(num_scalar_prefetch=1, grid=(2 * nb, 2),
                                      in_specs=[full, half, half, half, half, full, full], out_specs=(full,) * 4)
    sds = jax.ShapeDtypeStruct((2 * R, C), F32)
    (a0, b0), (a1, b1) = halves
    outs = pl.pallas_call(kern, name=name, out_shape=(sds,) * 4, grid_spec=gs,
                          compiler_params=pltpu.CompilerParams(dimension_semantics=("parallel", "parallel"),
                                                               vmem_limit_bytes=48 << 20))(
                              c_idx, w.reshape(2 * R, C), a0, b0, a1, b1, m.reshape(2 * R, C), v.reshape(2 * R, C))
    return tuple(t.reshape(2, R, C) for t in outs)


def _small_sum_adamw(gathered, w, m, v, *, name):
    R = w.shape[0]

    def kern(ga_ref, w_ref, m_ref, v_ref, g_ref, d_ref, mo_ref, vo_ref):
        g = ga_ref[0]
        for k in range(1, N_DEV):
            g = g + ga_ref[k]
        g_ref[...] = g
        d, mn, vn = _adamw_math(w_ref[...], g, m_ref[...], v_ref[...])
        d_ref[...] = d
        mo_ref[...] = mn
        vo_ref[...] = vn

    sds = jax.ShapeDtypeStruct((R, LANE), F32)
    return _pcall(kern, name=name, out_shape=(sds, sds, sds, sds), vmem_mb=48)(gathered, w, m, v)


def _pair_sum(g, theirs, c_idx, *, name):
    R, C = g.shape
    ch = C // 2
    tr = _row_tile(R, ch)

    def kern(c_ref, a_ref, b_ref, o_ref):
        o_ref[...] = (a_ref[...] + b_ref[...]).astype(BF16)

    gs = pltpu.PrefetchScalarGridSpec(
        num_scalar_prefetch=1, grid=(R // tr,),
        in_specs=[pl.BlockSpec((tr, ch), lambda i, c: (i, c[0])), pl.BlockSpec((tr, ch), lambda i, c: (i, 0))],
        out_specs=pl.BlockSpec((tr, ch), lambda i, c: (i, 0)))
    return pl.pallas_call(kern, name=name, out_shape=jax.ShapeDtypeStruct((R, ch), BF16), grid_spec=gs,
                          compiler_params=pltpu.CompilerParams(dimension_semantics=("parallel",),
                                                               vmem_limit_bytes=48 << 20))(c_idx, g, theirs)


HBM_SPEC = pl.BlockSpec(memory_space=pl.ANY)

WeightRows = collections.namedtuple("WeightRows", "full_rows own_rows cols pieces zero_rows")


def _w_in_piece_a(j):
    return jnp.where(j == 0, 0, 1232 * j + GAP)


def _w_in_piece_b(j):
    return jnp.where(j == 0, GAP_AT + GAP, 1232 * j + GAP_AT + GAP)


W_IN = WeightRows(NP, 1232, D_MODEL, ((0, GAP_AT, _w_in_piece_a), (GAP_AT, 1232 - GAP_AT, _w_in_piece_b)),
                  ((GAP_AT, GAP),))
W_OUT = WeightRows(2048, 512, D_MODEL, ((0, 512, lambda j: 512 * j),), ())
W_UQ = WeightRows(2048, 384, Q_LORA, ((0, 192, lambda j: 512 * j), (192, 192, lambda j: 512 * j + 256)),
                  tuple((256 * h + 192, 64) for h in range(N_HEADS)))
W_UKV = WeightRows(2048, 512, KV_LORA, ((0, 512, lambda j: 512 * j),), ())
W_CONV = WeightRows(64, 16, 256, ((0, 16, lambda j: 16 * j),), ())
SHARDED = (W_IN, W_OUT, W_UQ, W_UKV)


def _mesh_pos():
    x, y, c = lax.axis_index("x"), lax.axis_index("y"), lax.axis_index("c")
    return x, y, c


def _other_chips(x, y):
    return [(1 - x, y), (x, 1 - y), (1 - x, 1 - y)]


def _rows(start, n):
    return pl.ds(pl.multiple_of(start, 16), n)


def _half_cols(spec, c):
    ch = spec.cols // 2
    return pl.ds(pl.multiple_of(c * ch, LANE), ch)


def _allgather_weights(specs, shards, zeros, *, name):
    na = len(specs)
    zlist = [a for a in range(na) if zeros[a] is not None]
    L = shards[0].shape[0]
    plan_first, plan_own, plan_zero = [], [], []
    for a, spec in enumerate(specs):
        for p in range(len(spec.pieces)):
            plan_own.append((a, p))
            for k in range(3):
                plan_first.append((a, p, k))
        for z in range(len(spec.zero_rows)):
            for l in range(L):
                plan_zero.append((a, z, l))
    nf = len(plan_first)
    n_sems = 2 * nf + len(plan_own) + len(plan_zero)

    def body(*refs):
        ins = refs[:na]
        zrefs = dict(zip(zlist, refs[na:na + len(zlist)]))
        outs = refs[na + len(zlist):2 * na + len(zlist)]
        send_sems, recv_sems = refs[2 * na + len(zlist):]
        x, y, c = _mesh_pos()
        j = 2 * x + y
        chips = _other_chips(x, y)
        sibling = (x, y, 1 - c)

        def remote(src, dst, sem, to):
            return pltpu.make_async_remote_copy(src_ref=src, dst_ref=dst, send_sem=send_sems.at[sem],
                                                recv_sem=recv_sems.at[sem], device_id=to, device_id_type=MESH)

        def block(a, p, chip, cols):
            _, n, dst = specs[a].pieces[p]
            return outs[a].at[:, _rows(dst(chip), n), cols]

        def first(i):
            a, p, k = plan_first[i]
            src0, n, _ = specs[a].pieces[p]
            cols = _half_cols(specs[a], c)
            return remote(ins[a].at[:, pl.ds(src0, n), cols], block(a, p, j, cols), i, (*chips[k], c))

        def landed(i, half):
            a, p, k = plan_first[i]
            return block(a, p, 2 * chips[k][0] + chips[k][1], _half_cols(specs[a], half))

        def passed(i):
            return remote(landed(i, c), landed(i, c), nf + i, sibling)

        def own(i):
            a, p = plan_own[i]
            src0, n, _ = specs[a].pieces[p]
            return remote(ins[a].at[:, pl.ds(src0, n), :], block(a, p, j, slice(None)), 2 * nf + i, sibling)

        def zero(i):
            a, z, l = plan_zero[i]
            r0, n = specs[a].zero_rows[z]
            return remote(zrefs[a].at[pl.ds(0, n), :], outs[a].at[l, pl.ds(r0, n), :],
                          2 * nf + len(plan_own) + i, sibling)

        fixed = [own(i) for i in range(len(plan_own))] + [zero(i) for i in range(len(plan_zero))]
        for cp in fixed:
            cp.start()
        for i in range(nf):
            first(i).start()
        for i in range(nf):
            remote(landed(i, c), landed(i, c), i, sibling).wait_recv()
            passed(i).start()
        for i in range(nf):
            remote(landed(i, 1 - c), landed(i, 1 - c), nf + i, sibling).wait_recv()
        for cp in fixed:
            cp.wait()
        for i in range(nf):
            first(i).wait_send()
            passed(i).wait_send()

    out_shape = tuple(jax.ShapeDtypeStruct((L, spec.full_rows, spec.cols), BF16) for spec in specs)
    args = tuple(shards) + tuple(zeros[a] for a in zlist)
    return _pcall(body, name=name, out_shape=out_shape, in_specs=[HBM_SPEC] * len(args),
                  out_specs=(HBM_SPEC,) * na,
                  scratch=[pltpu.SemaphoreType.DMA((n_sems,)), pltpu.SemaphoreType.DMA((n_sems,))])(*args)


def _exchange_halves(specs, grads, *, name):
    na = len(grads)

    def body(*refs):
        ins, outs = refs[:na], refs[na:2 * na]
        send_sems, recv_sems = refs[2 * na:]
        x, y, c = _mesh_pos()
        copies = [pltpu.make_async_remote_copy(
            src_ref=ins[a].at[:, _half_cols(specs[a], 1 - c)], dst_ref=outs[a], send_sem=send_sems.at[a],
            recv_sem=recv_sems.at[a], device_id=(x, y, 1 - c), device_id_type=MESH) for a in range(na)]
        for cp in copies:
            cp.start()
        for cp in copies:
            cp.wait()

    out_shape = tuple(jax.ShapeDtypeStruct((s.full_rows, s.cols // 2), F32) for s in specs)
    return _pcall(body, name=name, out_shape=out_shape, in_specs=[HBM_SPEC] * na, out_specs=(HBM_SPEC,) * na,
                  scratch=[pltpu.SemaphoreType.DMA((na,)), pltpu.SemaphoreType.DMA((na,))])(*grads)


def _scatter_chips(specs, parts, *, name):
    na = len(parts)
    plan = [(a, p, k) for a in range(na) for p in range(len(specs[a].pieces)) for k in range(3)]

    def body(*refs):
        ins, outs = refs[:na], refs[na:2 * na]
        send_sems, recv_sems = refs[2 * na:]
        x, y, c = _mesh_pos()
        chips = _other_chips(x, y)
        copies = []
        for i, (a, p, k) in enumerate(plan):
            src0, n, dst = specs[a].pieces[p]
            pk = 2 * chips[k][0] + chips[k][1]
            copies.append(pltpu.make_async_remote_copy(
                src_ref=ins[a].at[_rows(dst(pk), n), :], dst_ref=outs[a].at[k, pl.ds(src0, n), :],
                send_sem=send_sems.at[i], recv_sem=recv_sems.at[i], device_id=(*chips[k], c), device_id_type=MESH))
        for cp in copies:
            cp.start()
        for cp in copies:
            cp.wait()

    out_shape = tuple(jax.ShapeDtypeStruct((3, s.own_rows, s.cols // 2), BF16) for s in specs)
    return _pcall(body, name=name, out_shape=out_shape, in_specs=[HBM_SPEC] * na, out_specs=(HBM_SPEC,) * na,
                  scratch=[pltpu.SemaphoreType.DMA((len(plan),)), pltpu.SemaphoreType.DMA((len(plan),))])(*parts)


def _chip_sum(spec, part, recv, *, name):
    ch = spec.cols // 2
    npieces = len(spec.pieces)

    def kern(recv_ref, part_ref, o_ref, own_ref, sems):
        j = 2 * lax.axis_index("x") + lax.axis_index("y")
        copies = []
        for p, (src0, n, dst) in enumerate(spec.pieces):
            copies.append(pltpu.make_async_copy(part_ref.at[_rows(dst(j), n), :], own_ref.at[pl.ds(src0, n), :],
                                                sems.at[p]))
        for cp in copies:
            cp.start()
        for cp in copies:
            cp.wait()
        o_ref[...] = ((own_ref[...].astype(F32) + recv_ref[0].astype(F32)) + recv_ref[1].astype(F32)) \
            + recv_ref[2].astype(F32)

    vm = pl.BlockSpec(memory_space=pltpu.VMEM)
    return _pcall(kern, name=name, out_shape=jax.ShapeDtypeStruct((spec.own_rows, ch), F32),
                  in_specs=[vm, HBM_SPEC], out_specs=vm,
                  scratch=[pltpu.VMEM((spec.own_rows, ch), BF16), pltpu.SemaphoreType.DMA((npieces,))],
                  vmem_mb=48)(recv, part)


def _send_to_sibling(sums, *, name):
    na = len(sums)

    def body(*refs):
        ins, outs = refs[:na], refs[na:2 * na]
        send_sems, recv_sems = refs[2 * na:]
        x, y, c = _mesh_pos()
        copies = [pltpu.make_async_remote_copy(
            src_ref=ins[a], dst_ref=outs[a], send_sem=send_sems.at[a], recv_sem=recv_sems.at[a],
            device_id=(x, y, 1 - c), device_id_type=MESH) for a in range(na)]
        for cp in copies:
            cp.start()
        for cp in copies:
            cp.wait()

    out_shape = tuple(jax.ShapeDtypeStruct(t.shape, t.dtype) for t in sums)
    return _pcall(body, name=name, out_shape=out_shape, in_specs=[HBM_SPEC] * na, out_specs=(HBM_SPEC,) * na,
                  scratch=[pltpu.SemaphoreType.DMA((na,)), pltpu.SemaphoreType.DMA((na,))])(*sums)


def _allgather_small(block, *, name):
    m_per, n = block.shape

    def body(x_ref, out_ref, send_sems, recv_sems, local_sem):
        x, y, c = _mesh_pos()
        me, sibling = (x, y, c), (x, y, 1 - c)
        chips = _other_chips(x, y)

        def rows(px, py, pc):
            return out_ref.at[4 * px + 2 * py + pc]

        def copy(k, blk, to, src=None):
            return pltpu.make_async_remote_copy(
                src_ref=rows(*blk) if src is None else src, dst_ref=rows(*blk), send_sem=send_sems.at[k],
                recv_sem=recv_sems.at[k], device_id=to, device_id_type=MESH)

        mine = pltpu.make_async_copy(x_ref, rows(*me), local_sem)
        mine.start()
        first = [copy(0, me, sibling, src=x_ref)]
        first += [copy(1 + k, me, (*chip, c), src=x_ref) for k, chip in enumerate(chips)]
        for cp in first:
            cp.start()
        passed = [copy(4 + k, (*chip, c), sibling) for k, chip in enumerate(chips)]
        for k, chip in enumerate(chips):
            copy(1 + k, (*chip, c), me).wait_recv()
            passed[k].start()
        copy(0, sibling, me).wait_recv()
        for k, chip in enumerate(chips):
            copy(4 + k, (*chip, 1 - c), me).wait_recv()
        for cp in first + passed:
            cp.wait_send()
        mine.wait()

    vm = pl.BlockSpec(memory_space=pltpu.VMEM)
    return _pcall(body, name=name, out_shape=jax.ShapeDtypeStruct((N_DEV, m_per, n), block.dtype),
                  in_specs=[vm], out_specs=vm,
                  scratch=[pltpu.SemaphoreType.DMA((7,)), pltpu.SemaphoreType.DMA((7,)), pltpu.SemaphoreType.DMA],
                  vmem_mb=48)(block)


def _rope_tables(positions):
    half = ROPE // 2
    inv_freq = ROPE_THETA ** (-jnp.arange(half, dtype=F32) / half)
    ang = positions.astype(F32)[:, None] * inv_freq
    cos, sin = jnp.cos(ang), jnp.sin(ang)
    S = positions.shape[0]
    cos_t = jnp.concatenate([cos, cos, jnp.ones((S, 64), F32)], axis=1)
    sin_t = jnp.concatenate([-sin, sin, jnp.zeros((S, 64), F32)], axis=1)
    return cos_t, sin_t


def _local_step(x, positions, target, emb_g, emb_b, w_in_t, q_g, kv_g, w_uq_t, w_ukv_t, w_pool, pool_scale, conv_w,
                w_out, b_out, ln_g, ln_b):
    cos_t, sin_t = _rope_tables(positions)
    h = _ln_fwd(x, emb_g, emb_b, name="emb_ln")
    saved = []
    for l in range(DEPTH):
        proj = _matmul(h, w_in_t[l], "nt", name=f"in_proj{l}", tm=512, tn=1024, tk=2048)
        qc, kc, v, vt, qn, kvn = _mla_qkv(proj, cos_t, sin_t, q_g[l], kv_g[l], w_uq_t[l], w_ukv_t[l],
                                          name=f"mla_qkv{l}")
        o, lse2 = _flash_fwd(qc, kc, vt, name=f"flash_fwd{l}")
        mix = _mixer_fwd(proj, o, w_pool[l], pool_scale[l], conv_w[l], name=f"mixer_fwd{l}")
        h_next, r = _outproj_ln(mix, w_out[l], h, b_out[l], ln_g[l], ln_b[l], name=f"out_proj_ln{l}")
        saved.append((h, proj, qc, kc, v, qn, kvn, o, lse2, mix, r))
        h = h_next

    dh, loss_acc = _loss_and_dy(h, target, name="loss")
    grads = [None] * DEPTH
    for l in reversed(range(DEPTH)):
        h_in, proj, qc, kc, v, qn, kvn, o, lse2, mix, r = saved[l]
        dr, d_ln_g, d_ln_b, d_b_out = _ln_bwd(dh, r, ln_g[l], name=f"ln_bwd{l}")
        dmix = _matmul(dr, w_out[l], "nt", name=f"dmix{l}", tm=512, tn=1024, tk=2048)
        d_w_out = _matmul(mix, dr, "tn", name=f"dw_out{l}", tm=1024, tn=1024, tk=512)
        d_mix, do, d_w_pool, d_ps, d_conv = _mixer_bwd(dmix, proj, o, w_pool[l], pool_scale[l], conv_w[l],
                                                       name=f"mixer_bwd{l}")
        delta = _attn_delta(o, do, name=f"attn_delta{l}")
        dqc, dkc, dv = _flash_bwd(qc, kc, v, do, lse2, delta, name=f"flash_bwd{l}")
        dqb, dkvb, d_mla, d_qg, d_kvg = _mla_qkv_bwd(dqc, dkc, dv, proj, cos_t, sin_t, q_g[l], kv_g[l],
                                                     w_uq_t[l], w_ukv_t[l], name=f"mla_qkv_bwd{l}")
        d_w_uq_t = _matmul(dqb, qn, "tn", name=f"dw_uq{l}", tm=1024, tn=512, tk=512)
        d_w_ukv_t = _matmul(dkvb, kvn, "tn", name=f"dw_ukv{l}", tm=1024, tn=256, tk=512)
        d_w_in_t = _dproj_t_times_h(d_mla, d_mix, h_in, name=f"dw_in{l}")
        dh = _dproj_times_w(d_mla, d_mix, w_in_t[l], dr, ALPHA, name=f"dh{l}")
        grads[l] = dict(w_in_t=d_w_in_t, q_g=d_qg[0], kv_g=d_kvg[0], w_uq_t=d_w_uq_t, w_ukv_t=d_w_ukv_t,
                        w_pool=d_w_pool, pool_scale=d_ps[0], conv_w=d_conv, w_out=d_w_out, b_out=d_b_out[0],
                        ln_g=d_ln_g[0], ln_b=d_ln_b[0])
    grad_x, d_emb_g, d_emb_b, _ = _ln_bwd(dh, x, emb_g, name="emb_ln_bwd")
    return loss_acc[0, 0], grad_x, d_emb_g[0], d_emb_b[0], grads


SMALL_ORDER = ("emb_ln_g", "emb_ln_b", "q_norm_g", "kv_norm_g", "w_pool", "pool_scale", "b_out", "ln_g", "ln_b")


def _pack_small(arrs, extra_rows):
    flat = jnp.concatenate([a.reshape(-1) for a in arrs])
    rows = flat.shape[0] // LANE
    total = -(-(rows + extra_rows) // 8) * 8
    return jnp.pad(flat, (0, total * LANE - flat.shape[0])).reshape(total, LANE)


def _unpack_small(packed, shapes):
    flat = packed.reshape(-1)
    out, off = [], 0
    for shp in shapes:
        n = 1
        for s in shp:
            n *= s
        out.append(flat[off:off + n].reshape(shp))
        off += n
    return out, off


def kernel(x, positions, emb_ln_g, emb_ln_b, w_in, q_norm_g, kv_norm_g, w_uq, w_ukv, w_pool, pool_scale, conv_w, w_out, b_out, ln_g, ln_b, loss_target, m_emb_ln_g, m_emb_ln_b, m_w_in, m_q_norm_g, m_kv_norm_g, m_w_uq, m_w_ukv, m_w_pool, m_pool_scale, m_conv_w, m_w_out, m_b_out, m_ln_g, m_ln_b, v_emb_ln_g, v_emb_ln_b, v_w_in, v_q_norm_g, v_kv_norm_g, v_w_uq, v_w_ukv, v_w_pool, v_pool_scale, v_conv_w, v_w_out, v_b_out, v_ln_g, v_ln_b):
    xi, yi, ci = lax.axis_index("x"), lax.axis_index("y"), lax.axis_index("c")
    chip = 2 * xi + yi
    c_idx = ci.reshape(1).astype(jnp.int32)

    def t(a):
        return jnp.swapaxes(a, 1, 2)

    conv_bits = lax.bitcast_convert_type(conv_w.reshape(DEPTH, 3 * 128), BF16).reshape(DEPTH, 3, 256)
    conv_bits = jnp.pad(conv_bits, ((0, 0), (0, 13), (0, 0)))
    ag_specs = SHARDED + (W_CONV,)
    shards = (t(w_in).astype(BF16), w_out.astype(BF16), t(w_uq).astype(BF16), t(w_ukv).astype(BF16), conv_bits)
    zeros = (jnp.zeros((GAP, D_MODEL), BF16), None, jnp.zeros((64, Q_LORA), BF16), None, None)
    w_in_t, w_out_full, w_uq_t, w_ukv_t, a_conv = _allgather_weights(ag_specs, shards, zeros, name="allgather_weights")
    conv_rows = a_conv.reshape(DEPTH, N_CHIPS, 16, 256)[:, :, :3, :]
    conv_full = lax.bitcast_convert_type(conv_rows.reshape(DEPTH, N_CHIPS, 3, 128, 2), F32)
    conv_full = jnp.transpose(conv_full, (0, 2, 1, 3)).reshape(DEPTH, 3, 512)

    loss_part, grad_x, d_emb_g, d_emb_b, grads = _local_step(
        x[0], positions[0], loss_target[0], emb_ln_g, emb_ln_b, w_in_t, q_norm_g, kv_norm_g, w_uq_t, w_ukv_t,
        w_pool, pool_scale, conv_full, w_out_full, b_out, ln_g, ln_b)

    rs_specs = tuple(s for s in SHARDED for _ in range(DEPTH))
    rs_names = tuple(f"{nm}{l}" for nm in ("w_in", "w_out", "w_uq", "w_ukv") for l in range(DEPTH))
    big = tuple(grads[l][key] for key in ("w_in_t", "w_out", "w_uq_t", "w_ukv_t") for l in range(DEPTH))
    theirs = _exchange_halves(rs_specs, big, name="exchange_halves")
    parts = tuple(_pair_sum(g, th, c_idx, name=f"pair_sum_{nm}") for g, th, nm in zip(big, theirs, rs_names))
    recv = _scatter_chips(rs_specs, parts, name="scatter_chips")
    sums = tuple(_chip_sum(s, p, r, name=f"chip_sum_{nm}") for s, p, r, nm in zip(rs_specs, parts, recv, rs_names))
    others = _send_to_sibling(sums, name="send_to_sibling")

    small_g = [d_emb_g, d_emb_b,
               jnp.stack([grads[l]["q_g"] for l in range(DEPTH)]), jnp.stack([grads[l]["kv_g"] for l in range(DEPTH)]),
               jnp.stack([grads[l]["w_pool"] for l in range(DEPTH)]),
               jnp.stack([grads[l]["pool_scale"] for l in range(DEPTH)]),
               jnp.stack([grads[l]["b_out"] for l in range(DEPTH)]), jnp.stack([grads[l]["ln_g"] for l in range(DEPTH)]),
               jnp.stack([grads[l]["ln_b"] for l in range(DEPTH)]),
               jnp.stack([grads[l]["conv_w"] for l in range(DEPTH)]),
               jnp.pad(loss_part.reshape(1), (0, LANE - 1))]
    small_w = [emb_ln_g, emb_ln_b, q_norm_g, kv_norm_g, w_pool, pool_scale, b_out, ln_g, ln_b]
    small_m = [m_emb_ln_g, m_emb_ln_b, m_q_norm_g, m_kv_norm_g, m_w_pool, m_pool_scale, m_b_out, m_ln_g, m_ln_b]
    small_v = [v_emb_ln_g, v_emb_ln_b, v_q_norm_g, v_kv_norm_g, v_w_pool, v_pool_scale, v_b_out, v_ln_g, v_ln_b]
    extra = (DEPTH * 3 * 512 + LANE) // LANE
    packed_g = _pack_small(small_g, 0)
    gathered = _allgather_small(packed_g, name="allgather_small")
    g_tot, d_small, m_small, v_small = _small_sum_adamw(
        gathered, _pack_small(small_w, extra), _pack_small(small_m, extra), _pack_small(small_v, extra),
        name="small_sum_adamw")
    shapes = [w.shape for w in small_w]
    g_list, off = _unpack_small(g_tot, shapes)
    d_list, _ = _unpack_small(d_small, shapes)
    m_list, _ = _unpack_small(m_small, shapes)
    v_list, _ = _unpack_small(v_small, shapes)
    flat_tot = g_tot.reshape(-1)
    conv_tot = flat_tot[off:off + DEPTH * 3 * 512].reshape(DEPTH, 3, 512)
    loss = flat_tot[off + DEPTH * 3 * 512]
    g_conv = lax.dynamic_slice_in_dim(conv_tot, chip * 128, 128, axis=2)

    def halves(a):
        return [(sums[DEPTH * a + l], others[DEPTH * a + l]) for l in range(DEPTH)]

    def whole(a):
        return jnp.stack([jnp.where(ci == 0, jnp.concatenate([mine, oth], axis=1),
                                    jnp.concatenate([oth, mine], axis=1)) for mine, oth in halves(a)])

    upd = {}
    upd["w_in"] = tuple(t(o) for o in _adamw_halves(t(w_in), t(m_w_in), t(v_w_in), halves(0), c_idx,
                                                    name="adamw_w_in"))
    upd["w_out"] = _adamw_halves(w_out, m_w_out, v_w_out, halves(1), c_idx, name="adamw_w_out")
    g_uq, g_ukv = t(whole(2)), t(whole(3))
    upd["w_uq"] = (g_uq,) + _adamw(w_uq, g_uq, m_w_uq, v_w_uq, name="adamw_w_uq")
    upd["w_ukv"] = (g_ukv,) + _adamw(w_ukv, g_ukv, m_w_ukv, v_w_ukv, name="adamw_w_ukv")
    upd["conv_w"] = (g_conv,) + _adamw(conv_w, g_conv, m_conv_w, v_conv_w, name="adamw_conv_w")
    for i, nm in enumerate(SMALL_ORDER):
        upd[nm] = (g_list[i], d_list[i], m_list[i], v_list[i])

    order = ("emb_ln_g", "emb_ln_b", "w_in", "q_norm_g", "kv_norm_g", "w_uq", "w_ukv", "w_pool", "pool_scale",
             "conv_w", "w_out", "b_out", "ln_g", "ln_b")
    outs = [loss, grad_x[None]]
    for field in range(4):
        outs += [upd[nm][field] for nm in order]
    return tuple(outs)
```

```python
import collections

import jax
import jax.numpy as jnp
from jax import lax
from jax.experimental import pallas as pl
from jax.experimental.pallas import tpu as pltpu

F32 = jnp.float32
BF16 = jnp.bfloat16
MESH = pl.DeviceIdType.MESH

D_MODEL = 2048
DEPTH = 2
N_HEADS = 8
NOPE = 128
ROPE = 64
Q_LORA = 512
KV_LORA = 256
D_MLA = 1024
POOL_WINDOWS = (2, 4, 8, 16)
D_IN_PROJ = 4928
LN_EPS = 1e-5
RMS_EPS = 1e-6
ROPE_THETA = 10000.0
ALPHA = (2 * DEPTH) ** 0.25
SCALE = (NOPE + ROPE) ** -0.5
LOG2E = 1.4426950408889634
SCALE_LOG2E = SCALE * LOG2E
ADAM_LR = 0.001
ADAM_B1 = 0.9
ADAM_B2 = 0.999
ADAM_EPS = 1e-08
ADAM_WD = 0.01
ADAM_STEP = 10

NP = 5120
GAP_AT = 832
GAP = NP - D_IN_PROJ
W_MLA = 1024
W_MIX = NP - W_MLA
HALO = 16
LANE = 128
N_CHIPS = 4
N_DEV = 8
TQ = 512

NN = (((1,), (0,)), ((), ()))
NT = (((1,), (1,)), ((), ()))
TN = (((0,), (0,)), ((), ()))


CommScript = collections.namedtuple("CommScript", "args out_shape n_sems start finish")
HBM_SPEC = pl.BlockSpec(memory_space=pl.ANY)


def _pcall(kern, *, name, out_shape, grid=None, in_specs=None, out_specs=None, scratch=(), dims=None,
           vmem_mb=None, comm=None):
    cp = {}
    if dims is not None:
        cp["dimension_semantics"] = dims if comm is None else ("arbitrary",) * len(dims)
    if vmem_mb is not None:
        cp["vmem_limit_bytes"] = vmem_mb << 20
    if comm is None:
        args = dict(name=name, out_shape=out_shape, scratch_shapes=list(scratch),
                    compiler_params=pltpu.CompilerParams(**cp))
        if grid is not None:
            args["grid"] = grid
        if in_specs is not None:
            args["in_specs"] = in_specs
        if out_specs is not None:
            args["out_specs"] = out_specs
        return pl.pallas_call(kern, **args)

    single = not isinstance(out_shape, (tuple, list))
    own_out = (out_shape,) if single else tuple(out_shape)
    own_out_specs = (out_specs,) if single else tuple(out_specs)
    n_in, n_out, n_scr = len(in_specs), len(own_out), len(scratch)
    na, no = len(comm.args), len(comm.out_shape)

    def at(end):
        cond = None
        for d, n in enumerate(grid):
            here = pl.program_id(d) == (n - 1 if end else 0)
            cond = here if cond is None else jnp.logical_and(cond, here)
        return cond

    def wrapped(*refs):
        own_in, c_in = refs[:n_in], refs[n_in:n_in + na]
        o0 = n_in + na
        own_o, c_out = refs[o0:o0 + n_out], refs[o0 + n_out:o0 + n_out + no]
        s0 = o0 + n_out + no
        own_s, (send_sems, recv_sems) = refs[s0:s0 + n_scr], refs[s0 + n_scr:]

        @pl.when(at(False))
        def _():
            comm.start(c_in, c_out, send_sems, recv_sems)

        kern(*own_in, *own_o, *own_s)

        @pl.when(at(True))
        def _():
            comm.finish(c_in, c_out, send_sems, recv_sems)

    call = pl.pallas_call(
        wrapped, name=name, out_shape=own_out + tuple(comm.out_shape), grid=grid,
        in_specs=list(in_specs) + [HBM_SPEC] * na, out_specs=own_out_specs + (HBM_SPEC,) * no,
        scratch_shapes=list(scratch) + [pltpu.SemaphoreType.DMA((comm.n_sems,)),
                                        pltpu.SemaphoreType.DMA((comm.n_sems,))],
        compiler_params=pltpu.CompilerParams(**cp))

    def run(*args):
        res = call(*args, *comm.args)
        own = res[0] if single else tuple(res[:n_out])
        return own, tuple(res[n_out:])

    return run


def _run_comm(script, *, name):
    na, no = len(script.args), len(script.out_shape)

    def body(*refs):
        ins, outs = refs[:na], refs[na:na + no]
        send_sems, recv_sems = refs[na + no:]
        script.start(ins, outs, send_sems, recv_sems)
        script.finish(ins, outs, send_sems, recv_sems)

    return pl.pallas_call(
        body, name=name, out_shape=tuple(script.out_shape), in_specs=[HBM_SPEC] * na, out_specs=(HBM_SPEC,) * no,
        scratch_shapes=[pltpu.SemaphoreType.DMA((script.n_sems,)), pltpu.SemaphoreType.DMA((script.n_sems,))])(
            *script.args)


def _sigmoid(g):
    return 1.0 / (1.0 + jnp.exp(-g))


def _silu_and_grad(g):
    sig = _sigmoid(g)
    return g * sig, sig * (1.0 + g * (1.0 - sig))


def _matmul(a, b, mode, *, name, tm, tn, tk, out_dtype=F32, vmem_mb=48, comm=None):
    if mode == "nn":
        (M, K), N = a.shape, b.shape[1]
    elif mode == "nt":
        (M, K), N = a.shape, b.shape[0]
    else:
        (K, M), N = a.shape, b.shape[1]
    tm, tn, tk = min(tm, M), min(tn, N), min(tk, K)
    assert M % tm == 0 and N % tn == 0 and K % tk == 0, (name, M, N, K)
    nk = K // tk
    dn = {"nn": NN, "nt": NT, "tn": TN}[mode]
    if mode == "tn":
        a_spec = pl.BlockSpec((tk, tm), lambda i, j, k: (k, i))
    else:
        a_spec = pl.BlockSpec((tm, tk), lambda i, j, k: (i, k))
    if mode == "nt":
        b_spec = pl.BlockSpec((tn, tk), lambda i, j, k: (j, k))
    else:
        b_spec = pl.BlockSpec((tk, tn), lambda i, j, k: (k, j))
    o_spec = pl.BlockSpec((tm, tn), lambda i, j, k: (i, j))

    def kern(a_ref, b_ref, o_ref, *rest):
        part = lax.dot_general(a_ref[...].astype(BF16), b_ref[...].astype(BF16), dn,
                               preferred_element_type=F32)
        if nk == 1:
            o_ref[...] = part.astype(out_dtype)
        else:
            acc_ref = rest[0]
            k = pl.program_id(2)

            @pl.when(k == 0)
            def _():
                acc_ref[...] = part

            @pl.when(k > 0)
            def _():
                acc_ref[...] += part

            @pl.when(k == nk - 1)
            def _():
                o_ref[...] = acc_ref[...].astype(out_dtype)

    scratch = [pltpu.VMEM((tm, tn), F32)] if nk > 1 else []
    return _pcall(kern, name=name, out_shape=jax.ShapeDtypeStruct((M, N), out_dtype),
                  grid=(M // tm, N // tn, nk), in_specs=[a_spec, b_spec], out_specs=o_spec, scratch=scratch,
                  dims=("parallel", "parallel", "arbitrary"), vmem_mb=vmem_mb, comm=comm)(a, b)


def _dproj_times_w(d_mla, d_mix, wt, add, add_scale, *, name, comm=None):
    S = d_mla.shape[0]
    Dm = wt.shape[1]
    tm, tn, tk = min(512, S), 1024, W_MLA
    nk = NP // tk

    def kern(a1_ref, a2_ref, b_ref, add_ref, o_ref, acc_ref):
        k = pl.program_id(2)

        @pl.when(k == 0)
        def _():
            acc_ref[...] = jnp.dot(a1_ref[...], b_ref[...], preferred_element_type=F32)

        @pl.when(k > 0)
        def _():
            acc_ref[...] += jnp.dot(a2_ref[...], b_ref[...], preferred_element_type=F32)

        @pl.when(k == nk - 1)
        def _():
            o_ref[...] = add_scale * add_ref[...] + acc_ref[...]

    o_spec = pl.BlockSpec((tm, tn), lambda i, j, k: (i, j))
    return _pcall(kern, name=name, out_shape=jax.ShapeDtypeStruct((S, Dm), F32), grid=(S // tm, Dm // tn, nk),
                  in_specs=[pl.BlockSpec((tm, tk), lambda i, j, k: (i, 0)),
                            pl.BlockSpec((tm, tk), lambda i, j, k: (i, jnp.maximum(k - 1, 0))),
                            pl.BlockSpec((tk, tn), lambda i, j, k: (k, j)), o_spec],
                  out_specs=o_spec, scratch=[pltpu.VMEM((tm, tn), F32)],
                  dims=("parallel", "parallel", "arbitrary"), vmem_mb=48, comm=comm)(d_mla, d_mix, wt, add)


def _dproj_t_times_h(d_mla, d_mix, h, *, name, comm=None):
    S, Dm = h.shape
    tm, tn, tk = W_MLA, 1024, min(512, S)
    nk = S // tk

    def kern(a1_ref, a2_ref, b_ref, o_ref, acc_ref):
        i = pl.program_id(0)
        k = pl.program_id(2)
        b = b_ref[...].astype(BF16)

        def accumulate(part):
            @pl.when(k == 0)
            def _():
                acc_ref[...] = part

            @pl.when(k > 0)
            def _():
                acc_ref[...] += part

        @pl.when(i == 0)
        def _():
            accumulate(lax.dot_general(a1_ref[...], b, TN, preferred_element_type=F32))

        @pl.when(i > 0)
        def _():
            accumulate(lax.dot_general(a2_ref[...], b, TN, preferred_element_type=F32))

        @pl.when(k == nk - 1)
        def _():
            o_ref[...] = acc_ref[...]

    return _pcall(kern, name=name, out_shape=jax.ShapeDtypeStruct((NP, Dm), F32), grid=(NP // tm, Dm // tn, nk),
                  in_specs=[pl.BlockSpec((tk, tm), lambda i, j, k: (jnp.where(i == 0, k, nk - 1), 0)),
                            pl.BlockSpec((tk, tm), lambda i, j, k: (jnp.where(i == 0, 0, k), jnp.maximum(i - 1, 0))),
                            pl.BlockSpec((tk, tn), lambda i, j, k: (k, j))],
                  out_specs=pl.BlockSpec((tm, tn), lambda i, j, k: (i, j)), scratch=[pltpu.VMEM((tm, tn), F32)],
                  dims=("parallel", "parallel", "arbitrary"), vmem_mb=48, comm=comm)(d_mla, d_mix, h)


def _ln_fwd(x, g, b, *, name):
    S, Dm = x.shape
    tm = min(512, S)

    def kern(x_ref, g_ref, b_ref, y_ref):
        xf = x_ref[...]
        mu = jnp.mean(xf, axis=-1, keepdims=True)
        xc = xf - mu
        var = jnp.mean(xc * xc, axis=-1, keepdims=True)
        y_ref[...] = xc * lax.rsqrt(var + LN_EPS) * g_ref[...] + b_ref[...]

    row = pl.BlockSpec((tm, Dm), lambda i: (i, 0))
    vec = pl.BlockSpec((1, Dm), lambda i: (0, 0))
    return _pcall(kern, name=name, out_shape=jax.ShapeDtypeStruct((S, Dm), F32), grid=(S // tm,),
                  in_specs=[row, vec, vec], out_specs=row, dims=("parallel",), vmem_mb=48)(
                      x, g.reshape(1, Dm), b.reshape(1, Dm))


def _ln_bwd(dy, r, g, *, name):
    S, Dm = r.shape
    tm = min(512, S)

    def kern(dy_ref, r_ref, g_ref, dr_ref, dg_ref, db_ref, ds_ref):
        @pl.when(pl.program_id(0) == 0)
        def _():
            dg_ref[...] = jnp.zeros_like(dg_ref)
            db_ref[...] = jnp.zeros_like(db_ref)
            ds_ref[...] = jnp.zeros_like(ds_ref)

        rf = r_ref[...]
        dyf = dy_ref[...]
        mu = jnp.mean(rf, axis=-1, keepdims=True)
        xc = rf - mu
        var = jnp.mean(xc * xc, axis=-1, keepdims=True)
        rstd = lax.rsqrt(var + LN_EPS)
        xhat = xc * rstd
        dxh = dyf * g_ref[...]
        c1 = jnp.mean(dxh, axis=-1, keepdims=True)
        c2 = jnp.mean(dxh * xhat, axis=-1, keepdims=True)
        dr = rstd * (dxh - c1 - xhat * c2)
        dr_ref[...] = dr
        dg_ref[...] += jnp.sum(dyf * xhat, axis=0, keepdims=True)
        db_ref[...] += jnp.sum(dyf, axis=0, keepdims=True)
        ds_ref[...] += jnp.sum(dr, axis=0, keepdims=True)

    row = pl.BlockSpec((tm, Dm), lambda i: (i, 0))
    vec = pl.BlockSpec((1, Dm), lambda i: (0, 0))
    vshape = jax.ShapeDtypeStruct((1, Dm), F32)
    return _pcall(kern, name=name, out_shape=(jax.ShapeDtypeStruct((S, Dm), F32), vshape, vshape, vshape),
                  grid=(S // tm,), in_specs=[row, row, vec], out_specs=(row, vec, vec, vec),
                  dims=("arbitrary",), vmem_mb=48)(dy, r, g.reshape(1, Dm))


def _loss_and_dy(y, target, *, name):
    S, Dm = y.shape
    tm = min(512, S)

    def kern(y_ref, t_ref, dy_ref, l_ref):
        @pl.when(pl.program_id(0) == 0)
        def _():
            l_ref[...] = jnp.zeros_like(l_ref)

        e = y_ref[...] - t_ref[...]
        dy_ref[...] = e / float(Dm)
        per_row = jnp.mean(e * e, axis=-1, keepdims=True)
        l_ref[...] += 0.5 * jnp.sum(per_row, axis=0, keepdims=True)

    row = pl.BlockSpec((tm, Dm), lambda i: (i, 0))
    acc = pl.BlockSpec((8, LANE), lambda i: (0, 0))
    return _pcall(kern, name=name,
                  out_shape=(jax.ShapeDtypeStruct((S, Dm), F32), jax.ShapeDtypeStruct((8, LANE), F32)),
                  grid=(S // tm,), in_specs=[row, row], out_specs=(row, acc), dims=("arbitrary",), vmem_mb=48)(
                      y, target)


def _rot_sum(t):
    return pltpu.roll(t, 32, 1) + pltpu.roll(t, 96, 1)


def _mla_qkv(proj, cos_t, sin_t, qg, kvg, wuq_t, wukv_t, *, name):
    S = proj.shape[0]
    tm = min(256, S)

    def kern(ql_ref, kvl_ref, kr_ref, cos_ref, sin_ref, qg_ref, kvg_ref, wuq_ref, wukv_ref,
             qc_ref, kc_ref, v_ref, vt_ref, qn_ref, kvn_ref):
        cosv = cos_ref[...]
        sinv = sin_ref[...]

        def rope(t):
            return t * cosv + _rot_sum(t) * sinv

        ql = ql_ref[...]
        qn = (ql * lax.rsqrt(jnp.mean(ql * ql, axis=-1, keepdims=True) + RMS_EPS) * qg_ref[...]).astype(BF16)
        kvl = kvl_ref[...]
        kvn = (kvl * lax.rsqrt(jnp.mean(kvl * kvl, axis=-1, keepdims=True) + RMS_EPS) * kvg_ref[...]).astype(BF16)
        qn_ref[...] = qn
        kvn_ref[...] = kvn
        q = lax.dot_general(qn, wuq_ref[...], NT, preferred_element_type=F32)
        kv = lax.dot_general(kvn, wukv_ref[...], NT, preferred_element_type=F32)
        kr = rope(kr_ref[...]).astype(BF16)
        for h in range(N_HEADS):
            c0 = 256 * h
            qc_ref[:, c0:c0 + 128] = q[:, c0:c0 + 128].astype(BF16)
            qc_ref[:, c0 + 128:c0 + 256] = rope(q[:, c0 + 128:c0 + 256]).astype(BF16)
            kc_ref[:, c0:c0 + 128] = kv[:, c0:c0 + 128].astype(BF16)
            kc_ref[:, c0 + 128:c0 + 256] = kr
            vh = kv[:, c0 + 128:c0 + 256]
            v_ref[:, 128 * h:128 * h + 128] = vh.astype(BF16)
            vt_ref[h] = jnp.transpose(vh).astype(BF16)

    def row(w, blk):
        return pl.BlockSpec((tm, w), lambda i: (i, blk))

    def full(shape):
        return pl.BlockSpec(shape, lambda i: (0,) * len(shape))

    t = min(TQ, S)
    per = t // tm
    vt_spec = pl.BlockSpec((N_HEADS, None, 128, tm), lambda i: (0, i // per, 0, i % per))
    outs = (jax.ShapeDtypeStruct((S, 2048), BF16), jax.ShapeDtypeStruct((S, 2048), BF16),
            jax.ShapeDtypeStruct((S, 1024), BF16), jax.ShapeDtypeStruct((N_HEADS, S // t, 128, t), BF16),
            jax.ShapeDtypeStruct((S, Q_LORA), BF16), jax.ShapeDtypeStruct((S, KV_LORA), BF16))
    return _pcall(kern, name=name, out_shape=outs, grid=(S // tm,),
                  in_specs=[row(512, 0), row(256, 2), row(128, 6), row(128, 0), row(128, 0),
                            full((1, Q_LORA)), full((1, KV_LORA)), full((2048, Q_LORA)), full((2048, KV_LORA))],
                  out_specs=(row(2048, 0), row(2048, 0), row(1024, 0), vt_spec, row(512, 0), row(256, 0)),
                  dims=("parallel",), vmem_mb=48)(
                      proj, proj, proj, cos_t, sin_t, qg.reshape(1, -1), kvg.reshape(1, -1), wuq_t, wukv_t)


def _mla_qkv_bwd(dqc, dkc, dv, proj, cos_t, sin_t, qg, kvg, wuq_t, wukv_t, *, name):
    S = proj.shape[0]
    tm = min(256, S)

    def kern(dq_ref, dk_ref, dv_ref, ql_ref, kvl_ref, cos_ref, sin_ref, qg_ref, kvg_ref, wuq_ref, wukv_ref,
             dqb_ref, dkvb_ref, dml_ref, dqg_ref, dkvg_ref):
        @pl.when(pl.program_id(0) == 0)
        def _():
            dqg_ref[...] = jnp.zeros_like(dqg_ref)
            dkvg_ref[...] = jnp.zeros_like(dkvg_ref)

        cosv = cos_ref[...]
        sinv = sin_ref[...]

        def unrope(t):
            return t * cosv - _rot_sum(t) * sinv

        dkr = jnp.zeros((tm, 128), F32)
        for h in range(N_HEADS):
            c0 = 256 * h
            dqb_ref[:, c0:c0 + 128] = dq_ref[:, c0:c0 + 128].astype(BF16)
            dqb_ref[:, c0 + 128:c0 + 256] = unrope(dq_ref[:, c0 + 128:c0 + 256]).astype(BF16)
            dkvb_ref[:, c0:c0 + 128] = dk_ref[:, c0:c0 + 128].astype(BF16)
            dkvb_ref[:, c0 + 128:c0 + 256] = dv_ref[:, 128 * h:128 * h + 128].astype(BF16)
            dkr = dkr + dk_ref[:, c0 + 128:c0 + 256]

        def rms_bwd(x, g, dy):
            n = x.shape[-1]
            rs = lax.rsqrt(jnp.mean(x * x, axis=-1, keepdims=True) + RMS_EPS)
            dyg = dy * g
            dx = rs * dyg - x * (rs * rs * rs) * (jnp.sum(dyg * x, axis=-1, keepdims=True) / n)
            return dx, jnp.sum(dy * (x * rs), axis=0, keepdims=True)

        dqn = jnp.dot(dqb_ref[...], wuq_ref[...], preferred_element_type=F32)
        dql, dqg = rms_bwd(ql_ref[...], qg_ref[...], dqn)
        dqg_ref[...] += dqg
        dkvn = jnp.dot(dkvb_ref[...], wukv_ref[...], preferred_element_type=F32)
        dkvl, dkvg = rms_bwd(kvl_ref[...], kvg_ref[...], dkvn)
        dkvg_ref[...] += dkvg
        dml_ref[:, 0:512] = dql.astype(BF16)
        dml_ref[:, 512:768] = dkvl.astype(BF16)
        dml_ref[:, 768:896] = unrope(dkr).astype(BF16)
        dml_ref[:, 896:1024] = jnp.zeros((tm, 128), BF16)

    def row(w, blk):
        return pl.BlockSpec((tm, w), lambda i: (i, blk))

    def full(shape):
        return pl.BlockSpec(shape, lambda i: (0,) * len(shape))

    outs = (jax.ShapeDtypeStruct((S, 2048), BF16), jax.ShapeDtypeStruct((S, 2048), BF16),
            jax.ShapeDtypeStruct((S, W_MLA), BF16), jax.ShapeDtypeStruct((1, Q_LORA), F32),
            jax.ShapeDtypeStruct((1, KV_LORA), F32))
    return _pcall(kern, name=name, out_shape=outs, grid=(S // tm,),
                  in_specs=[row(2048, 0), row(2048, 0), row(1024, 0), row(512, 0), row(256, 2),
                            row(128, 0), row(128, 0), full((1, Q_LORA)), full((1, KV_LORA)),
                            full((2048, Q_LORA)), full((2048, KV_LORA))],
                  out_specs=(row(2048, 0), row(2048, 0), row(W_MLA, 0), full((1, Q_LORA)), full((1, KV_LORA))),
                  dims=("arbitrary",), vmem_mb=56)(
                      dqc, dkc, dv, proj, proj, cos_t, sin_t, qg.reshape(1, -1), kvg.reshape(1, -1), wuq_t, wukv_t)


def _kq_mask(t):
    krow = lax.broadcasted_iota(jnp.int32, (t, t), 0)
    qcol = lax.broadcasted_iota(jnp.int32, (t, t), 1)
    return krow <= qcol


def _flash_fwd(qc, kc, vt, *, name, comm=None):
    S = qc.shape[0]
    t = min(TQ, S)
    n = S // t

    def kern(q_ref, k_ref, vt_ref, o_ref, lse_ref, m_s, l_s, acc_s):
        qi = pl.program_id(1)
        m_s[...] = jnp.full_like(m_s, -jnp.inf)
        l_s[...] = jnp.zeros_like(l_s)
        acc_s[...] = jnp.zeros_like(acc_s)

        def step(kb, masked):
            k0 = pl.multiple_of(kb * t, t)
            st = lax.dot_general(k_ref[pl.ds(k0, t), :], q_ref[...], NT, preferred_element_type=F32)
            if masked:
                st = jnp.where(_kq_mask(t), st, -jnp.inf)
            m_prev = m_s[...]
            m_new = jnp.maximum(m_prev, jnp.max(st, axis=0, keepdims=True))
            a = jnp.exp2((m_prev - m_new) * SCALE_LOG2E)
            pt = jnp.exp2((st - m_new) * SCALE_LOG2E)
            l_s[...] = a * l_s[...] + jnp.sum(pt, axis=0, keepdims=True)
            acc_s[...] = a * acc_s[...] + jnp.dot(vt_ref[kb], pt.astype(BF16), preferred_element_type=F32)
            m_s[...] = m_new

        def body(kb, carry):
            step(kb, False)
            return carry

        lax.fori_loop(0, qi, body, 0)
        step(qi, True)
        o_ref[...] = jnp.transpose(acc_s[...] / l_s[...])
        lse_ref[pl.ds(qi, 1), :] = m_s[...] * SCALE_LOG2E + jnp.log2(l_s[...])

    q_spec = pl.BlockSpec((t, 256), lambda h, qi: (qi, h))
    k_spec = pl.BlockSpec((S, 256), lambda h, qi: (0, h))
    vt_spec = pl.BlockSpec((None, n, 128, t), lambda h, qi: (h, 0, 0, 0))
    o_spec = pl.BlockSpec((t, 128), lambda h, qi: (qi, h))
    lse_spec = pl.BlockSpec((None, n, t), lambda h, qi: (h, 0, 0))
    return _pcall(kern, name=name,
                  out_shape=(jax.ShapeDtypeStruct((S, D_MLA), F32), jax.ShapeDtypeStruct((N_HEADS, n, t), F32)),
                  grid=(N_HEADS, n), in_specs=[q_spec, k_spec, vt_spec], out_specs=(o_spec, lse_spec),
                  scratch=[pltpu.VMEM((1, t), F32), pltpu.VMEM((1, t), F32), pltpu.VMEM((128, t), F32)],
                  dims=("parallel", "arbitrary"), vmem_mb=48, comm=comm)(qc, kc, vt)


def _attn_delta(o, do, *, name):
    S = o.shape[0]
    t = min(TQ, S)
    n = S // t

    def kern(o_ref, do_ref, dl_ref):
        i = pl.program_id(0)
        prod = o_ref[...] * do_ref[...]
        lane = lax.broadcasted_iota(jnp.int32, (t, LANE), 1)
        dmat = jnp.zeros((t, LANE), F32)
        for h in range(N_HEADS):
            dmat = jnp.where(lane == h, jnp.sum(prod[:, 128 * h:128 * h + 128], axis=1, keepdims=True), dmat)
        dmat_t = jnp.transpose(dmat)
        for h in range(N_HEADS):
            dl_ref[h, pl.ds(i, 1), :] = dmat_t[h:h + 1, :]

    row = pl.BlockSpec((t, D_MLA), lambda i: (i, 0))
    return _pcall(kern, name=name, out_shape=jax.ShapeDtypeStruct((N_HEADS, n, t), F32), grid=(n,),
                  in_specs=[row, row], out_specs=pl.BlockSpec((N_HEADS, n, t), lambda i: (0, 0, 0)),
                  dims=("arbitrary",), vmem_mb=48)(o, do)


def _flash_bwd(qc, kc, v, do, lse2, delta, *, name, comm=None):
    S = qc.shape[0]
    t = min(TQ, S)
    n = S // t

    def kern(q_ref, k_ref, v_ref, do_ref, lse_ref, dl_ref, dq_ref, dk_ref, dv_ref):
        ki = pl.program_id(1)

        @pl.when(ki == 0)
        def _():
            dq_ref[...] = jnp.zeros_like(dq_ref)

        dk_ref[...] = jnp.zeros_like(dk_ref)
        dv_ref[...] = jnp.zeros_like(dv_ref)

        def step(qb, masked):
            q0 = pl.multiple_of(qb * t, t)
            kt = k_ref[...]
            qblk = q_ref[pl.ds(q0, t), :]
            dob = do_ref[pl.ds(q0, t), :].astype(BF16)
            st = lax.dot_general(kt, qblk, NT, preferred_element_type=F32)
            pt = jnp.exp2(st * SCALE_LOG2E - lse_ref[pl.ds(qb, 1), :])
            if masked:
                pt = jnp.where(_kq_mask(t), pt, 0.0)
            dv_ref[...] += jnp.dot(pt.astype(BF16), dob, preferred_element_type=F32)
            dpt = lax.dot_general(v_ref[...], dob, NT, preferred_element_type=F32)
            dst = (pt * (dpt - dl_ref[pl.ds(qb, 1), :]) * SCALE).astype(BF16)
            dk_ref[...] += jnp.dot(dst, qblk, preferred_element_type=F32)
            dq_ref[pl.ds(q0, t), :] += lax.dot_general(dst, kt, TN, preferred_element_type=F32)

        step(ki, True)

        def body(qb, carry):
            step(qb, False)
            return carry

        lax.fori_loop(ki + 1, n, body, 0)

    def whole(w):
        return pl.BlockSpec((S, w), lambda h, ki: (0, h))

    def krow(w):
        return pl.BlockSpec((t, w), lambda h, ki: (ki, h))

    stat = pl.BlockSpec((None, n, t), lambda h, ki: (h, 0, 0))
    return _pcall(kern, name=name,
                  out_shape=(jax.ShapeDtypeStruct((S, 2048), F32), jax.ShapeDtypeStruct((S, 2048), F32),
                             jax.ShapeDtypeStruct((S, D_MLA), F32)),
                  grid=(N_HEADS, n),
                  in_specs=[whole(256), krow(256), krow(128), whole(128), stat, stat],
                  out_specs=(whole(256), krow(256), krow(128)),
                  dims=("parallel", "arbitrary"), vmem_mb=56, comm=comm)(qc, kc, v, do, lse2, delta)


def _mixer_specs(S, tm):
    hb = tm // HALO
    last_hb = S // HALO - 1

    def main(w, blk):
        return pl.BlockSpec((tm, w), lambda i: (i, blk))

    def prev(w, blk):
        return pl.BlockSpec((HALO, w), lambda i: (jnp.maximum(i * hb - 1, 0), blk))

    def nxt(w, blk):
        return pl.BlockSpec((HALO, w), lambda i: (jnp.minimum((i + 1) * hb, last_hb), blk))

    def full(shape):
        return pl.BlockSpec(shape, lambda i: (0,) * len(shape))

    return main, prev, nxt, full


def _fill_halo(i, xp, xu, hp_ref, hch_ref, hcc_ref, pin_ref, ch_ref, cc_ref, tm):
    first = i == 0
    xp[0:HALO, :] = jnp.where(first, 0.0, hp_ref[...])
    xp[HALO:HALO + tm, :] = pin_ref[...]
    xu[0:HALO, :] = jnp.where(first, 0.0, hch_ref[...] * hcc_ref[...])
    xu[HALO:HALO + tm, :] = cc_ref[...] * ch_ref[...]


def _pooled(xp, g, t1, tm):
    w = POOL_WINDOWS[g]
    lanes = slice(128 * g, 128 * g + 128)
    x0 = xp[HALO:HALO + tm, lanes]
    acc = x0
    for k in range(1, w):
        acc = acc + xp[HALO - k:HALO - k + tm, lanes]
    return acc / jnp.minimum(t1, float(w)) - x0


def _conv_fwd(xu, cw_ref, tm):
    return (cw_ref[0:1, :] * xu[HALO - 2:HALO - 2 + tm, :] + cw_ref[1:2, :] * xu[HALO - 1:HALO - 1 + tm, :]
            + cw_ref[2:3, :] * xu[HALO:HALO + tm, :])


def _mixer_fwd(proj, o, wpool, ps, convw, *, name):
    S = proj.shape[0]
    tm = min(256, S)
    main, prev, _, full = _mixer_specs(S, tm)

    def kern(gm_ref, pin_ref, gp_ref, ch_ref, cb_ref, cc_ref, gc_ref, hp_ref, hch_ref, hcc_ref,
             o_ref, wp_ref, ps_ref, cw_ref, mix_ref, xp, xu):
        i = pl.program_id(0)
        _fill_halo(i, xp, xu, hp_ref, hch_ref, hcc_ref, pin_ref, ch_ref, cc_ref, tm)
        t1 = (i * tm + lax.broadcasted_iota(jnp.int32, (tm, 1), 0) + 1).astype(F32)
        for g in range(4):
            lanes = slice(128 * g, 128 * g + 128)
            pooled = _pooled(xp, g, t1, tm)
            z = jnp.dot(pooled.astype(BF16), wp_ref[g].astype(BF16), preferred_element_type=F32)
            gp = gp_ref[:, lanes]
            y = z * ps_ref[:, lanes] * (gp * _sigmoid(gp))
            mix_ref[:, 1024 + 128 * g:1024 + 128 * g + 128] = y.astype(BF16)
        gc = gc_ref[...]
        mix_ref[:, 1536:2048] = (cb_ref[...] * _conv_fwd(xu, cw_ref, tm) * (gc * _sigmoid(gc))).astype(BF16)
        gm = gm_ref[...]
        mix_ref[:, 0:1024] = (o_ref[...] * (gm * _sigmoid(gm))).astype(BF16)

    return _pcall(kern, name=name, out_shape=jax.ShapeDtypeStruct((S, 2048), BF16), grid=(S // tm,),
                  in_specs=[main(1024, 1), main(512, 4), main(512, 5), main(512, 6), main(512, 7), main(512, 8),
                            main(512, 9), prev(512, 4), prev(512, 6), prev(512, 8),
                            main(1024, 0), full((4, 128, 128)), full((1, 512)), full((3, 512))],
                  out_specs=main(2048, 0),
                  scratch=[pltpu.VMEM((tm + HALO, 512), F32), pltpu.VMEM((tm + HALO, 512), F32)],
                  dims=("parallel",), vmem_mb=48)(
                      proj, proj, proj, proj, proj, proj, proj, proj, proj, proj, o, wpool, ps.reshape(1, 512), convw)


def _mixer_bwd(dmix, proj, o, wpool, ps, convw, *, name):
    S = proj.shape[0]
    tm = min(256, S)
    n = S // tm
    main, prev, nxt, full = _mixer_specs(S, tm)

    def kern(dm_ref, dmn_ref, gm_ref, pin_ref, gp_ref, ch_ref, cb_ref, cc_ref, gc_ref,
             hp_ref, hch_ref, hcc_ref, gpn_ref, cbn_ref, gcn_ref, o_ref, wp_ref, ps_ref, cw_ref,
             d_ref, do_ref, dwp_ref, dps_ref, dcw_ref, xp, xu, ee, ed):
        i = pl.program_id(0)
        last = i == n - 1

        @pl.when(i == 0)
        def _():
            dwp_ref[...] = jnp.zeros_like(dwp_ref)
            dps_ref[...] = jnp.zeros_like(dps_ref)
            dcw_ref[...] = jnp.zeros_like(dcw_ref)

        _fill_halo(i, xp, xu, hp_ref, hch_ref, hcc_ref, pin_ref, ch_ref, cc_ref, tm)
        t1 = (i * tm + lax.broadcasted_iota(jnp.int32, (tm, 1), 0) + 1).astype(F32)
        t1n = ((i + 1) * tm + lax.broadcasted_iota(jnp.int32, (HALO, 1), 0) + 1).astype(F32)
        c_pin, c_gp, c_ch, c_cb, c_cc, c_gc = 1024, 1536, 2048, 2560, 3072, 3584

        for g in range(4):
            w = float(POOL_WINDOWS[g])
            lanes = slice(128 * g, 128 * g + 128)
            pooled = _pooled(xp, g, t1, tm)
            pb = pooled.astype(BF16)
            wp = wp_ref[g].astype(BF16)
            z = jnp.dot(pb, wp, preferred_element_type=F32)
            psl = ps_ref[:, lanes]
            sg, dsg = _silu_and_grad(gp_ref[:, lanes])
            dmp = dm_ref[:, 1024 + 128 * g:1024 + 128 * g + 128]
            dyp = dmp * sg
            d_ref[:, c_gp + 128 * g:c_gp + 128 * g + 128] = (dmp * (z * psl) * dsg).astype(BF16)
            dps_ref[:, lanes] += jnp.sum(dyp * z, axis=0, keepdims=True)
            dz = (dyp * psl).astype(BF16)
            dwp_ref[g] += lax.dot_general(pb, dz, TN, preferred_element_type=F32)
            dpl = lax.dot_general(dz, wp, NT, preferred_element_type=F32)
            ee[0:tm, lanes] = dpl / jnp.minimum(t1, w)
            gpn = gpn_ref[:, lanes]
            dzn = (dmn_ref[:, lanes] * (gpn * _sigmoid(gpn)) * psl).astype(BF16)
            dpn = lax.dot_general(dzn, wp, NT, preferred_element_type=F32)
            ee[tm:tm + HALO, lanes] = jnp.where(last, 0.0, dpn / jnp.minimum(t1n, w))
            acc = ee[0:tm, lanes]
            for k in range(1, POOL_WINDOWS[g]):
                acc = acc + ee[k:k + tm, lanes]
            d_ref[:, c_pin + 128 * g:c_pin + 128 * g + 128] = (acc - dpl).astype(BF16)

        yc = _conv_fwd(xu, cw_ref, tm)
        sgc, dsgc = _silu_and_grad(gc_ref[...])
        cb = cb_ref[...]
        dmc = dm_ref[:, 1536:2048]
        d_ref[:, c_gc:c_gc + 512] = (dmc * cb * yc * dsgc).astype(BF16)
        d_ref[:, c_cb:c_cb + 512] = (dmc * yc * sgc).astype(BF16)
        dyc = dmc * cb * sgc
        ed[0:tm, :] = dyc
        gcn = gcn_ref[...]
        ed[tm:tm + HALO, :] = jnp.where(last, 0.0, dmn_ref[:, 512:1024] * cbn_ref[...] * (gcn * _sigmoid(gcn)))
        dcw_ref[0:1, :] += jnp.sum(dyc * xu[HALO - 2:HALO - 2 + tm, :], axis=0, keepdims=True)
        dcw_ref[1:2, :] += jnp.sum(dyc * xu[HALO - 1:HALO - 1 + tm, :], axis=0, keepdims=True)
        dcw_ref[2:3, :] += jnp.sum(dyc * xu[HALO:HALO + tm, :], axis=0, keepdims=True)
        du = cw_ref[2:3, :] * dyc + cw_ref[1:2, :] * ed[1:1 + tm, :] + cw_ref[0:1, :] * ed[2:2 + tm, :]
        d_ref[:, c_cc:c_cc + 512] = (du * ch_ref[...]).astype(BF16)
        d_ref[:, c_ch:c_ch + 512] = (du * cc_ref[...]).astype(BF16)

        sgm, dsgm = _silu_and_grad(gm_ref[...])
        dmm = dm_ref[:, 0:1024]
        do_ref[...] = dmm * sgm
        d_ref[:, 0:1024] = (dmm * o_ref[...] * dsgm).astype(BF16)

    outs = (jax.ShapeDtypeStruct((S, W_MIX), BF16), jax.ShapeDtypeStruct((S, 1024), F32),
            jax.ShapeDtypeStruct((4, 128, 128), F32), jax.ShapeDtypeStruct((1, 512), F32),
            jax.ShapeDtypeStruct((3, 512), F32))
    scr = [pltpu.VMEM((tm + HALO, 512), F32) for _ in range(4)]
    return _pcall(kern, name=name, out_shape=outs, grid=(n,),
                  in_specs=[main(2048, 0), nxt(1024, 1),
                            main(1024, 1), main(512, 4), main(512, 5), main(512, 6), main(512, 7), main(512, 8),
                            main(512, 9), prev(512, 4), prev(512, 6), prev(512, 8),
                            nxt(512, 5), nxt(512, 7), nxt(512, 9),
                            main(1024, 0), full((4, 128, 128)), full((1, 512)), full((3, 512))],
                  out_specs=(main(W_MIX, 0), main(1024, 0), full((4, 128, 128)), full((1, 512)), full((3, 512))),
                  scratch=scr, dims=("arbitrary",), vmem_mb=56)(
                      dmix, dmix, proj, proj, proj, proj, proj, proj, proj, proj, proj, proj, proj, proj, proj,
                      o, wpool, ps.reshape(1, 512), convw)


def _outproj_ln(mix, wout, h, bout, g, b, *, name):
    S, Dm = h.shape
    tm = min(256, S)

    def kern(mix_ref, w_ref, h_ref, bo_ref, g_ref, b_ref, y_ref, r_ref):
        out = jnp.dot(mix_ref[...], w_ref[...], preferred_element_type=F32) + bo_ref[...]
        r = ALPHA * h_ref[...] + out
        r_ref[...] = r
        mu = jnp.mean(r, axis=-1, keepdims=True)
        xc = r - mu
        var = jnp.mean(xc * xc, axis=-1, keepdims=True)
        y_ref[...] = xc * lax.rsqrt(var + LN_EPS) * g_ref[...] + b_ref[...]

    row = pl.BlockSpec((tm, Dm), lambda i: (i, 0))
    vec = pl.BlockSpec((1, Dm), lambda i: (0, 0))
    wsp = pl.BlockSpec((Dm, Dm), lambda i: (0, 0))
    sds = jax.ShapeDtypeStruct((S, Dm), F32)
    return _pcall(kern, name=name, out_shape=(sds, sds), grid=(S // tm,),
                  in_specs=[row, wsp, row, vec, vec, vec], out_specs=(row, row), dims=("parallel",), vmem_mb=56)(
                      mix, wout, h, bout.reshape(1, Dm), g.reshape(1, Dm), b.reshape(1, Dm))


def _adamw_math(w, g, m, v):
    m = ADAM_B1 * m + (1.0 - ADAM_B1) * g
    v = ADAM_B2 * v + (1.0 - ADAM_B2) * (g * g)
    m_hat = m / (1.0 - ADAM_B1 ** ADAM_STEP)
    v_hat = v / (1.0 - ADAM_B2 ** ADAM_STEP)
    delta = -ADAM_LR * (m_hat / (jnp.sqrt(v_hat) + ADAM_EPS) + ADAM_WD * w)
    return delta, m, v


def _row_tile(R, C):
    best = None
    for cand in range(8, R, 8):
        if R % cand == 0 and cand * C <= 256 * 1024:
            best = cand
    return best if best is not None else R


def _adamw(w, g, m, v, *, name):
    shape = w.shape
    C = shape[-1]
    R = 1
    for s in shape[:-1]:
        R *= s
    tr = _row_tile(R, C)

    def kern(w_ref, g_ref, m_ref, v_ref, d_ref, mo_ref, vo_ref):
        d, mn, vn = _adamw_math(w_ref[...], g_ref[...], m_ref[...], v_ref[...])
        d_ref[...] = d
        mo_ref[...] = mn
        vo_ref[...] = vn

    blk = pl.BlockSpec((tr, C), lambda i: (i, 0))
    sds = jax.ShapeDtypeStruct((R, C), F32)
    outs = _pcall(kern, name=name, out_shape=(sds, sds, sds), grid=(R // tr,), in_specs=[blk] * 4,
                  out_specs=(blk, blk, blk), dims=("parallel",), vmem_mb=48)(
                      w.reshape(R, C), g.reshape(R, C), m.reshape(R, C), v.reshape(R, C))
    return tuple(t.reshape(shape) for t in outs)


def _adamw_halves(w, m, v, halves, c_idx, *, name):
    _, R, C = w.shape
    ch = C // 2
    tr = _row_tile(R, ch)
    nb = R // tr

    def kern(c_ref, w_ref, a0_ref, b0_ref, a1_ref, b1_ref, m_ref, v_ref, g_ref, d_ref, mo_ref, vo_ref):
        layer = pl.program_id(0) // nb
        mine = pl.program_id(1) == c_ref[0]
        g = jnp.where(layer == 0, jnp.where(mine, a0_ref[...], b0_ref[...]),
                      jnp.where(mine, a1_ref[...], b1_ref[...]))
        g_ref[...] = g
        d, mn, vn = _adamw_math(w_ref[...], g, m_ref[...], v_ref[...])
        d_ref[...] = d
        mo_ref[...] = mn
        vo_ref[...] = vn

    full = pl.BlockSpec((tr, ch), lambda i, hc, c: (i, hc))
    half = pl.BlockSpec((tr, ch), lambda i, hc, c: (i % nb, 0))
    gs = pltpu.PrefetchScalarGridSpec(num_scalar_prefetch=1, grid=(2 * nb, 2),
                                      in_specs=[full, half, half, half, half, full, full], out_specs=(full,) * 4)
    sds = jax.ShapeDtypeStruct((2 * R, C), F32)
    (a0, b0), (a1, b1) = halves
    outs = pl.pallas_call(kern, name=name, out_shape=(sds,) * 4, grid_spec=gs,
                          compiler_params=pltpu.CompilerParams(dimension_semantics=("parallel", "parallel"),
                                                               vmem_limit_bytes=48 << 20))(
                              c_idx, w.reshape(2 * R, C), a0, b0, a1, b1, m.reshape(2 * R, C), v.reshape(2 * R, C))
    return tuple(t.reshape(2, R, C) for t in outs)


def _small_sum_adamw(gathered, w, m, v, *, name):
    R = w.shape[0]

    def kern(ga_ref, w_ref, m_ref, v_ref, g_ref, d_ref, mo_ref, vo_ref):
        g = ga_ref[0]
        for k in range(1, N_DEV):
            g = g + ga_ref[k]
        g_ref[...] = g
        d, mn, vn = _adamw_math(w_ref[...], g, m_ref[...], v_ref[...])
        d_ref[...] = d
        mo_ref[...] = mn
        vo_ref[...] = vn

    sds = jax.ShapeDtypeStruct((R, LANE), F32)
    return _pcall(kern, name=name, out_shape=(sds, sds, sds, sds), vmem_mb=48)(gathered, w, m, v)


def _pair_sum(g, theirs, c_idx, *, name):
    R, C = g.shape
    ch = C // 2
    tr = _row_tile(R, ch)

    def kern(c_ref, a_ref, b_ref, o_ref):
        o_ref[...] = (a_ref[...] + b_ref[...]).astype(BF16)

    gs = pltpu.PrefetchScalarGridSpec(
        num_scalar_prefetch=1, grid=(R // tr,),
        in_specs=[pl.BlockSpec((tr, ch), lambda i, c: (i, c[0])), pl.BlockSpec((tr, ch), lambda i, c: (i, 0))],
        out_specs=pl.BlockSpec((tr, ch), lambda i, c: (i, 0)))
    return pl.pallas_call(kern, name=name, out_shape=jax.ShapeDtypeStruct((R, ch), BF16), grid_spec=gs,
                          compiler_params=pltpu.CompilerParams(dimension_semantics=("parallel",),
                                                               vmem_limit_bytes=48 << 20))(c_idx, g, theirs)


WeightRows = collections.namedtuple("WeightRows", "full_rows own_rows cols pieces zero_rows")


def _w_in_piece_a(j):
    return jnp.where(j == 0, 0, 1232 * j + GAP)


def _w_in_piece_b(j):
    return jnp.where(j == 0, GAP_AT + GAP, 1232 * j + GAP_AT + GAP)


W_IN = WeightRows(NP, 1232, D_MODEL, ((0, GAP_AT, _w_in_piece_a), (GAP_AT, 1232 - GAP_AT, _w_in_piece_b)),
                  ((GAP_AT, GAP),))
W_OUT = WeightRows(2048, 512, D_MODEL, ((0, 512, lambda j: 512 * j),), ())
W_UQ = WeightRows(2048, 384, Q_LORA, ((0, 192, lambda j: 512 * j), (192, 192, lambda j: 512 * j + 256)),
                  tuple((256 * h + 192, 64) for h in range(N_HEADS)))
W_UKV = WeightRows(2048, 512, KV_LORA, ((0, 512, lambda j: 512 * j),), ())
W_CONV = WeightRows(64, 16, 256, ((0, 16, lambda j: 16 * j),), ())
SHARDED = (W_IN, W_OUT, W_UQ, W_UKV)


def _mesh_pos():
    x, y, c = lax.axis_index("x"), lax.axis_index("y"), lax.axis_index("c")
    return x, y, c


def _other_chips(x, y):
    return [(1 - x, y), (x, 1 - y), (1 - x, 1 - y)]


def _rows(start, n):
    return pl.ds(pl.multiple_of(start, 16), n)


def _half_cols(spec, c):
    ch = spec.cols // 2
    return pl.ds(pl.multiple_of(c * ch, LANE), ch)


def _allgather_script(specs, shards, zeros):
    na = len(specs)
    zlist = [a for a in range(na) if zeros[a] is not None]
    plan_first, plan_own, plan_zero = [], [], []
    for a, spec in enumerate(specs):
        for p in range(len(spec.pieces)):
            plan_own.append((a, p))
            for k in range(3):
                plan_first.append((a, p, k))
        for z in range(len(spec.zero_rows)):
            for l in range(shards[a].shape[0]):
                plan_zero.append((a, z, l))
    nf = len(plan_first)
    n_sems = 2 * nf + len(plan_own) + len(plan_zero)

    def copies(ins_all, outs, send_sems, recv_sems):
        ins = ins_all[:na]
        zrefs = dict(zip(zlist, ins_all[na:]))
        x, y, c = _mesh_pos()
        j = 2 * x + y
        chips = _other_chips(x, y)
        sibling = (x, y, 1 - c)

        def remote(src, dst, sem, to):
            return pltpu.make_async_remote_copy(src_ref=src, dst_ref=dst, send_sem=send_sems.at[sem],
                                                recv_sem=recv_sems.at[sem], device_id=to, device_id_type=MESH)

        def block(a, p, chip, cols):
            _, n, dst = specs[a].pieces[p]
            return outs[a].at[:, _rows(dst(chip), n), cols]

        def first(i):
            a, p, k = plan_first[i]
            src0, n, _ = specs[a].pieces[p]
            cols = _half_cols(specs[a], c)
            return remote(ins[a].at[:, pl.ds(src0, n), cols], block(a, p, j, cols), i, (*chips[k], c))

        def landed(i, half):
            a, p, k = plan_first[i]
            return block(a, p, 2 * chips[k][0] + chips[k][1], _half_cols(specs[a], half))

        def arrival(i, half, sem):
            return remote(landed(i, half), landed(i, half), sem, sibling)

        def passed(i):
            return remote(landed(i, c), landed(i, c), nf + i, sibling)

        def own(i):
            a, p = plan_own[i]
            src0, n, _ = specs[a].pieces[p]
            return remote(ins[a].at[:, pl.ds(src0, n), :], block(a, p, j, slice(None)), 2 * nf + i, sibling)

        def zero(i):
            a, z, l = plan_zero[i]
            r0, n = specs[a].zero_rows[z]
            return remote(zrefs[a].at[pl.ds(0, n), :], outs[a].at[l, pl.ds(r0, n), :],
                          2 * nf + len(plan_own) + i, sibling)

        fixed = [own(i) for i in range(len(plan_own))] + [zero(i) for i in range(len(plan_zero))]
        return c, fixed, first, arrival, passed

    def start(ins, outs, send_sems, recv_sems):
        _, fixed, first, _, _ = copies(ins, outs, send_sems, recv_sems)
        for cp in fixed:
            cp.start()
        for i in range(nf):
            first(i).start()

    def finish(ins, outs, send_sems, recv_sems):
        c, fixed, first, arrival, passed = copies(ins, outs, send_sems, recv_sems)
        for i in range(nf):
            arrival(i, c, i).wait_recv()
            passed(i).start()
        for i in range(nf):
            arrival(i, 1 - c, nf + i).wait_recv()
        for cp in fixed:
            cp.wait()
        for i in range(nf):
            first(i).wait_send()
            passed(i).wait_send()

    out_shape = tuple(jax.ShapeDtypeStruct((shards[a].shape[0], spec.full_rows, spec.cols), BF16)
                      for a, spec in enumerate(specs))
    args = tuple(shards) + tuple(zeros[a] for a in zlist)
    return CommScript(args, out_shape, n_sems, start, finish)


def _start_all_wait_all(args, out_shape, n_sems, make_copies):
    def start(ins, outs, send_sems, recv_sems):
        for cp in make_copies(ins, outs, send_sems, recv_sems):
            cp.start()

    def finish(ins, outs, send_sems, recv_sems):
        for cp in make_copies(ins, outs, send_sems, recv_sems):
            cp.wait()

    return CommScript(tuple(args), tuple(out_shape), n_sems, start, finish)


def _exchange_script(specs, grads):
    na = len(grads)

    def make_copies(ins, outs, send_sems, recv_sems):
        x, y, c = _mesh_pos()
        return [pltpu.make_async_remote_copy(
            src_ref=ins[a].at[:, _half_cols(specs[a], 1 - c)], dst_ref=outs[a], send_sem=send_sems.at[a],
            recv_sem=recv_sems.at[a], device_id=(x, y, 1 - c), device_id_type=MESH) for a in range(na)]

    out_shape = [jax.ShapeDtypeStruct((s.full_rows, s.cols // 2), F32) for s in specs]
    return _start_all_wait_all(grads, out_shape, na, make_copies)


def _scatter_script(specs, parts):
    na = len(parts)
    plan = [(a, p, k) for a in range(na) for p in range(len(specs[a].pieces)) for k in range(3)]

    def make_copies(ins, outs, send_sems, recv_sems):
        x, y, c = _mesh_pos()
        chips = _other_chips(x, y)
        copies = []
        for i, (a, p, k) in enumerate(plan):
            src0, n, dst = specs[a].pieces[p]
            pk = 2 * chips[k][0] + chips[k][1]
            copies.append(pltpu.make_async_remote_copy(
                src_ref=ins[a].at[_rows(dst(pk), n), :], dst_ref=outs[a].at[k, pl.ds(src0, n), :],
                send_sem=send_sems.at[i], recv_sem=recv_sems.at[i], device_id=(*chips[k], c), device_id_type=MESH))
        return copies

    out_shape = [jax.ShapeDtypeStruct((3, s.own_rows, s.cols // 2), BF16) for s in specs]
    return _start_all_wait_all(parts, out_shape, len(plan), make_copies)


def _chip_sum(spec, part, recv, *, name):
    ch = spec.cols // 2
    npieces = len(spec.pieces)

    def kern(recv_ref, part_ref, o_ref, own_ref, sems):
        j = 2 * lax.axis_index("x") + lax.axis_index("y")
        copies = []
        for p, (src0, n, dst) in enumerate(spec.pieces):
            copies.append(pltpu.make_async_copy(part_ref.at[_rows(dst(j), n), :], own_ref.at[pl.ds(src0, n), :],
                                                sems.at[p]))
        for cp in copies:
            cp.start()
        for cp in copies:
            cp.wait()
        o_ref[...] = ((own_ref[...].astype(F32) + recv_ref[0].astype(F32)) + recv_ref[1].astype(F32)) \
            + recv_ref[2].astype(F32)

    vm = pl.BlockSpec(memory_space=pltpu.VMEM)
    return _pcall(kern, name=name, out_shape=jax.ShapeDtypeStruct((spec.own_rows, ch), F32),
                  in_specs=[vm, HBM_SPEC], out_specs=vm,
                  scratch=[pltpu.VMEM((spec.own_rows, ch), BF16), pltpu.SemaphoreType.DMA((npieces,))],
                  vmem_mb=48)(recv, part)


def _sibling_script(sums):
    na = len(sums)

    def make_copies(ins, outs, send_sems, recv_sems):
        x, y, c = _mesh_pos()
        return [pltpu.make_async_remote_copy(
            src_ref=ins[a], dst_ref=outs[a], send_sem=send_sems.at[a], recv_sem=recv_sems.at[a],
            device_id=(x, y, 1 - c), device_id_type=MESH) for a in range(na)]

    out_shape = [jax.ShapeDtypeStruct(t.shape, t.dtype) for t in sums]
    return _start_all_wait_all(sums, out_shape, na, make_copies)


class _GradReducer:
    def __init__(self, layer, grads, c_idx):
        self.layer, self.grads, self.c_idx = layer, tuple(grads), c_idx
        self.names = [f"{nm}{layer}" for nm in ("w_in", "w_out", "w_uq", "w_ukv")]

    def exchange(self):
        return _exchange_script(SHARDED, self.grads)

    def scatter(self, theirs):
        self.parts = tuple(_pair_sum(g, th, self.c_idx, name=f"pair_sum_{nm}")
                           for g, th, nm in zip(self.grads, theirs, self.names))
        return _scatter_script(SHARDED, self.parts)

    def sibling(self, recv):
        self.sums = tuple(_chip_sum(s, p, r, name=f"chip_sum_{nm}")
                          for s, p, r, nm in zip(SHARDED, self.parts, recv, self.names))
        return _sibling_script(self.sums)

    def done(self, others):
        return list(zip(self.sums, others))


def _allgather_small(block, *, name):
    m_per, n = block.shape

    def body(x_ref, out_ref, send_sems, recv_sems, local_sem):
        x, y, c = _mesh_pos()
        me, sibling = (x, y, c), (x, y, 1 - c)
        chips = _other_chips(x, y)

        def rows(px, py, pc):
            return out_ref.at[4 * px + 2 * py + pc]

        def copy(k, blk, to, src=None):
            return pltpu.make_async_remote_copy(
                src_ref=rows(*blk) if src is None else src, dst_ref=rows(*blk), send_sem=send_sems.at[k],
                recv_sem=recv_sems.at[k], device_id=to, device_id_type=MESH)

        mine = pltpu.make_async_copy(x_ref, rows(*me), local_sem)
        mine.start()
        first = [copy(0, me, sibling, src=x_ref)]
        first += [copy(1 + k, me, (*chip, c), src=x_ref) for k, chip in enumerate(chips)]
        for cp in first:
            cp.start()
        passed = [copy(4 + k, (*chip, c), sibling) for k, chip in enumerate(chips)]
        for k, chip in enumerate(chips):
            copy(1 + k, (*chip, c), me).wait_recv()
            passed[k].start()
        copy(0, sibling, me).wait_recv()
        for k, chip in enumerate(chips):
            copy(4 + k, (*chip, 1 - c), me).wait_recv()
        for cp in first + passed:
            cp.wait_send()
        mine.wait()

    vm = pl.BlockSpec(memory_space=pltpu.VMEM)
    return _pcall(body, name=name, out_shape=jax.ShapeDtypeStruct((N_DEV, m_per, n), block.dtype),
                  in_specs=[vm], out_specs=vm,
                  scratch=[pltpu.SemaphoreType.DMA((7,)), pltpu.SemaphoreType.DMA((7,)), pltpu.SemaphoreType.DMA],
                  vmem_mb=48)(block)


def _rope_tables(positions):
    half = ROPE // 2
    inv_freq = ROPE_THETA ** (-jnp.arange(half, dtype=F32) / half)
    ang = positions.astype(F32)[:, None] * inv_freq
    cos, sin = jnp.cos(ang), jnp.sin(ang)
    S = positions.shape[0]
    cos_t = jnp.concatenate([cos, cos, jnp.ones((S, 64), F32)], axis=1)
    sin_t = jnp.concatenate([-sin, sin, jnp.zeros((S, 64), F32)], axis=1)
    return cos_t, sin_t


def _local_step(x, positions, target, emb_g, emb_b, weights0, weights1, q_g, kv_g, w_pool, pool_scale, conv_w,
                b_out, ln_g, ln_b, c_idx=None):
    cos_t, sin_t = _rope_tables(positions)
    h = _ln_fwd(x, emb_g, emb_b, name="emb_ln")
    weights = [weights0, weights1]
    saved = []
    for l in range(DEPTH):
        w_in_t, w_out, w_uq_t, w_ukv_t = weights[l]
        proj = _matmul(h, w_in_t, "nt", name=f"in_proj{l}", tm=512, tn=1024, tk=2048)
        qc, kc, v, vt, qn, kvn = _mla_qkv(proj, cos_t, sin_t, q_g[l], kv_g[l], w_uq_t, w_ukv_t, name=f"mla_qkv{l}")
        nxt = weights[l + 1] if l + 1 < DEPTH else None
        if isinstance(nxt, CommScript):
            (o, lse2), landed = _flash_fwd(qc, kc, vt, name=f"flash_fwd{l}", comm=nxt)
            weights[l + 1] = tuple(a[0] for a in landed)
        else:
            o, lse2 = _flash_fwd(qc, kc, vt, name=f"flash_fwd{l}")
        mix = _mixer_fwd(proj, o, w_pool[l], pool_scale[l], conv_w[l], name=f"mixer_fwd{l}")
        h_next, r = _outproj_ln(mix, w_out, h, b_out[l], ln_g[l], ln_b[l], name=f"out_proj_ln{l}")
        saved.append((h, proj, qc, kc, v, qn, kvn, o, lse2, mix, r))
        h = h_next

    dh, loss_acc = _loss_and_dy(h, target, name="loss")
    small = [None] * DEPTH
    big = [None] * DEPTH
    above = scatter_above = None
    for l in reversed(range(DEPTH)):
        w_in_t, w_out, w_uq_t, w_ukv_t = weights[l]
        h_in, proj, qc, kc, v, qn, kvn, o, lse2, mix, r = saved[l]
        dr, d_ln_g, d_ln_b, d_b_out = _ln_bwd(dh, r, ln_g[l], name=f"ln_bwd{l}")
        dmix = _matmul(dr, w_out, "nt", name=f"dmix{l}", tm=512, tn=1024, tk=2048)
        d_w_out = _matmul(mix, dr, "tn", name=f"dw_out{l}", tm=1024, tn=1024, tk=512)
        d_mix, do, d_w_pool, d_ps, d_conv = _mixer_bwd(dmix, proj, o, w_pool[l], pool_scale[l], conv_w[l],
                                                       name=f"mixer_bwd{l}")
        delta = _attn_delta(o, do, name=f"attn_delta{l}")
        if above is not None:
            (dqc, dkc, dv), recv = _flash_bwd(qc, kc, v, do, lse2, delta, name=f"flash_bwd{l}", comm=scatter_above)
            sibling_above = above.sibling(recv)
        else:
            dqc, dkc, dv = _flash_bwd(qc, kc, v, do, lse2, delta, name=f"flash_bwd{l}")
        dqb, dkvb, d_mla, d_qg, d_kvg = _mla_qkv_bwd(dqc, dkc, dv, proj, cos_t, sin_t, q_g[l], kv_g[l],
                                                     w_uq_t, w_ukv_t, name=f"mla_qkv_bwd{l}")
        d_w_uq_t = _matmul(dqb, qn, "tn", name=f"dw_uq{l}", tm=1024, tn=512, tk=512)
        d_w_ukv_t = _matmul(dkvb, kvn, "tn", name=f"dw_ukv{l}", tm=1024, tn=256, tk=512)
        if above is not None:
            d_w_in_t, others = _dproj_t_times_h(d_mla, d_mix, h_in, name=f"dw_in{l}", comm=sibling_above)
            big[l + 1] = above.done(others)
            above = None
        else:
            d_w_in_t = _dproj_t_times_h(d_mla, d_mix, h_in, name=f"dw_in{l}")
        big[l] = (d_w_in_t, d_w_out, d_w_uq_t, d_w_ukv_t)
        small[l] = dict(q_g=d_qg[0], kv_g=d_kvg[0], w_pool=d_w_pool, pool_scale=d_ps[0], conv_w=d_conv,
                        b_out=d_b_out[0], ln_g=d_ln_g[0], ln_b=d_ln_b[0])
        if c_idx is None:
            dh = _dproj_times_w(d_mla, d_mix, w_in_t, dr, ALPHA, name=f"dh{l}")
        elif l > 0:
            above = _GradReducer(l, big[l], c_idx)
            dh, theirs = _dproj_times_w(d_mla, d_mix, w_in_t, dr, ALPHA, name=f"dh{l}", comm=above.exchange())
            scatter_above = above.scatter(theirs)
        else:
            last = _GradReducer(l, big[l], c_idx)
            theirs = _run_comm(last.exchange(), name="exchange_halves0")
            dh, recv = _dproj_times_w(d_mla, d_mix, w_in_t, dr, ALPHA, name=f"dh{l}", comm=last.scatter(theirs))
    grad_x, d_emb_g, d_emb_b, _ = _ln_bwd(dh, x, emb_g, name="emb_ln_bwd")
    if c_idx is not None:
        big[0] = last.done(_run_comm(last.sibling(recv), name="send_to_sibling0"))
    return loss_acc[0, 0], grad_x, d_emb_g[0], d_emb_b[0], small, big


SMALL_ORDER = ("emb_ln_g", "emb_ln_b", "q_norm_g", "kv_norm_g", "w_pool", "pool_scale", "b_out", "ln_g", "ln_b")


def _pack_small(arrs, extra_rows):
    flat = jnp.concatenate([a.reshape(-1) for a in arrs])
    rows = flat.shape[0] // LANE
    total = -(-(rows + extra_rows) // 8) * 8
    return jnp.pad(flat, (0, total * LANE - flat.shape[0])).reshape(total, LANE)


def _unpack_small(packed, shapes):
    flat = packed.reshape(-1)
    out, off = [], 0
    for shp in shapes:
        n = 1
        for s in shp:
            n *= s
        out.append(flat[off:off + n].reshape(shp))
        off += n
    return out, off


def kernel(x, positions, emb_ln_g, emb_ln_b, w_in, q_norm_g, kv_norm_g, w_uq, w_ukv, w_pool, pool_scale, conv_w, w_out, b_out, ln_g, ln_b, loss_target, m_emb_ln_g, m_emb_ln_b, m_w_in, m_q_norm_g, m_kv_norm_g, m_w_uq, m_w_ukv, m_w_pool, m_pool_scale, m_conv_w, m_w_out, m_b_out, m_ln_g, m_ln_b, v_emb_ln_g, v_emb_ln_b, v_w_in, v_q_norm_g, v_kv_norm_g, v_w_uq, v_w_ukv, v_w_pool, v_pool_scale, v_conv_w, v_w_out, v_b_out, v_ln_g, v_ln_b):
    xi, yi, ci = lax.axis_index("x"), lax.axis_index("y"), lax.axis_index("c")
    chip = 2 * xi + yi
    c_idx = ci.reshape(1).astype(jnp.int32)

    def t(a):
        return jnp.swapaxes(a, 1, 2)

    conv_bits = lax.bitcast_convert_type(conv_w.reshape(DEPTH, 3 * 128), BF16).reshape(DEPTH, 3, 256)
    conv_bits = jnp.pad(conv_bits, ((0, 0), (0, 13), (0, 0)))
    own = (t(w_in).astype(BF16), w_out.astype(BF16), t(w_uq).astype(BF16), t(w_ukv).astype(BF16))
    zeros = (jnp.zeros((GAP, D_MODEL), BF16), None, jnp.zeros((64, Q_LORA), BF16), None)
    gather0 = _allgather_script(SHARDED + (W_CONV,), tuple(a[0:1] for a in own) + (conv_bits,), zeros + (None,))
    *weights0, a_conv = _run_comm(gather0, name="allgather_weights0")
    weights0 = tuple(a[0] for a in weights0)
    gather1 = _allgather_script(SHARDED, tuple(a[1:2] for a in own), zeros)
    conv_rows = a_conv.reshape(DEPTH, N_CHIPS, 16, 256)[:, :, :3, :]
    conv_full = lax.bitcast_convert_type(conv_rows.reshape(DEPTH, N_CHIPS, 3, 128, 2), F32)
    conv_full = jnp.transpose(conv_full, (0, 2, 1, 3)).reshape(DEPTH, 3, 512)

    loss_part, grad_x, d_emb_g, d_emb_b, grads, reduced = _local_step(
        x[0], positions[0], loss_target[0], emb_ln_g, emb_ln_b, weights0, gather1, q_norm_g, kv_norm_g,
        w_pool, pool_scale, conv_full, b_out, ln_g, ln_b, c_idx)

    small_g = [d_emb_g, d_emb_b,
               jnp.stack([grads[l]["q_g"] for l in range(DEPTH)]), jnp.stack([grads[l]["kv_g"] for l in range(DEPTH)]),
               jnp.stack([grads[l]["w_pool"] for l in range(DEPTH)]),
               jnp.stack([grads[l]["pool_scale"] for l in range(DEPTH)]),
               jnp.stack([grads[l]["b_out"] for l in range(DEPTH)]), jnp.stack([grads[l]["ln_g"] for l in range(DEPTH)]),
               jnp.stack([grads[l]["ln_b"] for l in range(DEPTH)]),
               jnp.stack([grads[l]["conv_w"] for l in range(DEPTH)]),
               jnp.pad(loss_part.reshape(1), (0, LANE - 1))]
    small_w = [emb_ln_g, emb_ln_b, q_norm_g, kv_norm_g, w_pool, pool_scale, b_out, ln_g, ln_b]
    small_m = [m_emb_ln_g, m_emb_ln_b, m_q_norm_g, m_kv_norm_g, m_w_pool, m_pool_scale, m_b_out, m_ln_g, m_ln_b]
    small_v = [v_emb_ln_g, v_emb_ln_b, v_q_norm_g, v_kv_norm_g, v_w_pool, v_pool_scale, v_b_out, v_ln_g, v_ln_b]
    extra = (DEPTH * 3 * 512 + LANE) // LANE
    packed_g = _pack_small(small_g, 0)
    gathered = _allgather_small(packed_g, name="allgather_small")
    g_tot, d_small, m_small, v_small = _small_sum_adamw(
        gathered, _pack_small(small_w, extra), _pack_small(small_m, extra), _pack_small(small_v, extra),
        name="small_sum_adamw")
    shapes = [w.shape for w in small_w]
    g_list, off = _unpack_small(g_tot, shapes)
    d_list, _ = _unpack_small(d_small, shapes)
    m_list, _ = _unpack_small(m_small, shapes)
    v_list, _ = _unpack_small(v_small, shapes)
    flat_tot = g_tot.reshape(-1)
    conv_tot = flat_tot[off:off + DEPTH * 3 * 512].reshape(DEPTH, 3, 512)
    loss = flat_tot[off + DEPTH * 3 * 512]
    g_conv = lax.dynamic_slice_in_dim(conv_tot, chip * 128, 128, axis=2)

    def halves(a):
        return [reduced[l][a] for l in range(DEPTH)]

    def whole(a):
        return jnp.stack([jnp.where(ci == 0, jnp.concatenate([mine, oth], axis=1),
                                    jnp.concatenate([oth, mine], axis=1)) for mine, oth in halves(a)])

    upd = {}
    upd["w_in"] = tuple(t(o) for o in _adamw_halves(t(w_in), t(m_w_in), t(v_w_in), halves(0), c_idx,
                                                    name="adamw_w_in"))
    upd["w_out"] = _adamw_halves(w_out, m_w_out, v_w_out, halves(1), c_idx, name="adamw_w_out")
    g_uq, g_ukv = t(whole(2)), t(whole(3))
    upd["w_uq"] = (g_uq,) + _adamw(w_uq, g_uq, m_w_uq, v_w_uq, name="adamw_w_uq")
    upd["w_ukv"] = (g_ukv,) + _adamw(w_ukv, g_ukv, m_w_ukv, v_w_ukv, name="adamw_w_ukv")
    upd["conv_w"] = (g_conv,) + _adamw(conv_w, g_conv, m_conv_w, v_conv_w, name="adamw_conv_w")
    for i, nm in enumerate(SMALL_ORDER):
        upd[nm] = (g_list[i], d_list[i], m_list[i], v_list[i])

    order = ("emb_ln_g", "emb_ln_b", "w_in", "q_norm_g", "kv_norm_g", "w_uq", "w_ukv", "w_pool", "pool_scale",
             "conv_w", "w_out", "b_out", "ln_g", "ln_b")
    outs = [loss, grad_x[None]]
    for field in range(4):
        outs += [upd[nm][field] for nm in order]
    return tuple(outs)
```

```python
import collections

import jax
import jax.numpy as jnp
from jax import lax
from jax.experimental import pallas as pl
from jax.experimental.pallas import tpu as pltpu

F32 = jnp.float32
BF16 = jnp.bfloat16
MESH = pl.DeviceIdType.MESH

D_MODEL = 2048
DEPTH = 2
N_HEADS = 8
NOPE = 128
ROPE = 64
Q_LORA = 512
KV_LORA = 256
D_MLA = 1024
POOL_WINDOWS = (2, 4, 8, 16)
D_IN_PROJ = 4928
LN_EPS = 1e-5
RMS_EPS = 1e-6
ROPE_THETA = 10000.0
ALPHA = (2 * DEPTH) ** 0.25
SCALE = (NOPE + ROPE) ** -0.5
LOG2E = 1.4426950408889634
SCALE_LOG2E = SCALE * LOG2E
ADAM_LR = 0.001
ADAM_B1 = 0.9
ADAM_B2 = 0.999
ADAM_EPS = 1e-08
ADAM_WD = 0.01
ADAM_STEP = 10

NP = 5120
GAP_AT = 832
GAP = NP - D_IN_PROJ
W_MLA = 1024
W_MIX = NP - W_MLA
HALO = 16
LANE = 128
N_CHIPS = 4
N_DEV = 8
TQ = 512

NN = (((1,), (0,)), ((), ()))
NT = (((1,), (1,)), ((), ()))
TN = (((0,), (0,)), ((), ()))


CommScript = collections.namedtuple("CommScript", "args out_shape n_sems start finish")
HBM_SPEC = pl.BlockSpec(memory_space=pl.ANY)


def _pcall(kern, *, name, out_shape, grid=None, in_specs=None, out_specs=None, scratch=(), dims=None,
           vmem_mb=None, comm=None):
    cp = {}
    if dims is not None:
        cp["dimension_semantics"] = dims if comm is None else ("arbitrary",) * len(dims)
    if vmem_mb is not None:
        cp["vmem_limit_bytes"] = vmem_mb << 20
    if comm is None:
        args = dict(name=name, out_shape=out_shape, scratch_shapes=list(scratch),
                    compiler_params=pltpu.CompilerParams(**cp))
        if grid is not None:
            args["grid"] = grid
        if in_specs is not None:
            args["in_specs"] = in_specs
        if out_specs is not None:
            args["out_specs"] = out_specs
        return pl.pallas_call(kern, **args)

    single = not isinstance(out_shape, (tuple, list))
    own_out = (out_shape,) if single else tuple(out_shape)
    own_out_specs = (out_specs,) if single else tuple(out_specs)
    n_in, n_out, n_scr = len(in_specs), len(own_out), len(scratch)
    na, no = len(comm.args), len(comm.out_shape)

    def at(end):
        cond = None
        for d, n in enumerate(grid):
            here = pl.program_id(d) == (n - 1 if end else 0)
            cond = here if cond is None else jnp.logical_and(cond, here)
        return cond

    def wrapped(*refs):
        own_in, c_in = refs[:n_in], refs[n_in:n_in + na]
        o0 = n_in + na
        own_o, c_out = refs[o0:o0 + n_out], refs[o0 + n_out:o0 + n_out + no]
        s0 = o0 + n_out + no
        own_s, (send_sems, recv_sems) = refs[s0:s0 + n_scr], refs[s0 + n_scr:]

        @pl.when(at(False))
        def _():
            comm.start(c_in, c_out, send_sems, recv_sems)

        kern(*own_in, *own_o, *own_s)

        @pl.when(at(True))
        def _():
            comm.finish(c_in, c_out, send_sems, recv_sems)

    call = pl.pallas_call(
        wrapped, name=name, out_shape=own_out + tuple(comm.out_shape), grid=grid,
        in_specs=list(in_specs) + [HBM_SPEC] * na, out_specs=own_out_specs + (HBM_SPEC,) * no,
        scratch_shapes=list(scratch) + [pltpu.SemaphoreType.DMA((comm.n_sems,)),
                                        pltpu.SemaphoreType.DMA((comm.n_sems,))],
        compiler_params=pltpu.CompilerParams(**cp))

    def run(*args):
        res = call(*args, *comm.args)
        own = res[0] if single else tuple(res[:n_out])
        return own, tuple(res[n_out:])

    return run


def _run_comm(script, *, name):
    na, no = len(script.args), len(script.out_shape)

    def body(*refs):
        ins, outs = refs[:na], refs[na:na + no]
        send_sems, recv_sems = refs[na + no:]
        script.start(ins, outs, send_sems, recv_sems)
        script.finish(ins, outs, send_sems, recv_sems)

    return pl.pallas_call(
        body, name=name, out_shape=tuple(script.out_shape), in_specs=[HBM_SPEC] * na, out_specs=(HBM_SPEC,) * no,
        scratch_shapes=[pltpu.SemaphoreType.DMA((script.n_sems,)), pltpu.SemaphoreType.DMA((script.n_sems,))])(
            *script.args)


def _sigmoid(g):
    return 1.0 / (1.0 + jnp.exp(-g))


def _silu_and_grad(g):
    sig = _sigmoid(g)
    return g * sig, sig * (1.0 + g * (1.0 - sig))


def _matmul(a, b, mode, *, name, tm, tn, tk, out_dtype=F32, vmem_mb=48, comm=None):
    if mode == "nn":
        (M, K), N = a.shape, b.shape[1]
    elif mode == "nt":
        (M, K), N = a.shape, b.shape[0]
    else:
        (K, M), N = a.shape, b.shape[1]
    tm, tn, tk = min(tm, M), min(tn, N), min(tk, K)
    assert M % tm == 0 and N % tn == 0 and K % tk == 0, (name, M, N, K)
    nk = K // tk
    dn = {"nn": NN, "nt": NT, "tn": TN}[mode]
    if mode == "tn":
        a_spec = pl.BlockSpec((tk, tm), lambda i, j, k: (k, i))
    else:
        a_spec = pl.BlockSpec((tm, tk), lambda i, j, k: (i, k))
    if mode == "nt":
        b_spec = pl.BlockSpec((tn, tk), lambda i, j, k: (j, k))
    else:
        b_spec = pl.BlockSpec((tk, tn), lambda i, j, k: (k, j))
    o_spec = pl.BlockSpec((tm, tn), lambda i, j, k: (i, j))

    def kern(a_ref, b_ref, o_ref, *rest):
        part = lax.dot_general(a_ref[...].astype(BF16), b_ref[...].astype(BF16), dn,
                               preferred_element_type=F32)
        if nk == 1:
            o_ref[...] = part.astype(out_dtype)
        else:
            acc_ref = rest[0]
            k = pl.program_id(2)

            @pl.when(k == 0)
            def _():
                acc_ref[...] = part

            @pl.when(k > 0)
            def _():
                acc_ref[...] += part

            @pl.when(k == nk - 1)
            def _():
                o_ref[...] = acc_ref[...].astype(out_dtype)

    scratch = [pltpu.VMEM((tm, tn), F32)] if nk > 1 else []
    return _pcall(kern, name=name, out_shape=jax.ShapeDtypeStruct((M, N), out_dtype),
                  grid=(M // tm, N // tn, nk), in_specs=[a_spec, b_spec], out_specs=o_spec, scratch=scratch,
                  dims=("parallel", "parallel", "arbitrary"), vmem_mb=vmem_mb, comm=comm)(a, b)


def _dproj_times_w(d_mla, d_mix, wt, add, add_scale, *, name, comm=None):
    S = d_mla.shape[0]
    Dm = wt.shape[1]
    tm, tn, tk = min(1024, S), 1024, W_MLA
    nk = NP // tk

    def kern(a1_ref, a2_ref, b_ref, add_ref, o_ref, acc_ref):
        k = pl.program_id(2)

        @pl.when(k == 0)
        def _():
            acc_ref[...] = jnp.dot(a1_ref[...], b_ref[...], preferred_element_type=F32)

        @pl.when(k > 0)
        def _():
            acc_ref[...] += jnp.dot(a2_ref[...], b_ref[...], preferred_element_type=F32)

        @pl.when(k == nk - 1)
        def _():
            o_ref[...] = add_scale * add_ref[...] + acc_ref[...]

    o_spec = pl.BlockSpec((tm, tn), lambda i, j, k: (i, j))
    return _pcall(kern, name=name, out_shape=jax.ShapeDtypeStruct((S, Dm), F32), grid=(S // tm, Dm // tn, nk),
                  in_specs=[pl.BlockSpec((tm, tk), lambda i, j, k: (i, 0)),
                            pl.BlockSpec((tm, tk), lambda i, j, k: (i, jnp.maximum(k - 1, 0))),
                            pl.BlockSpec((tk, tn), lambda i, j, k: (k, j)), o_spec],
                  out_specs=o_spec, scratch=[pltpu.VMEM((tm, tn), F32)],
                  dims=("parallel", "parallel", "arbitrary"), vmem_mb=48, comm=comm)(d_mla, d_mix, wt, add)


def _dproj_t_times_h(d_mla, d_mix, h, *, name, comm=None):
    S, Dm = h.shape
    tm, tn, tk = W_MLA, Dm, min(512, S)
    nk = S // tk

    def kern(a1_ref, a2_ref, b_ref, o_ref, acc_ref):
        i = pl.program_id(0)
        k = pl.program_id(2)
        b = b_ref[...].astype(BF16)

        def accumulate(part):
            @pl.when(k == 0)
            def _():
                acc_ref[...] = part

            @pl.when(k > 0)
            def _():
                acc_ref[...] += part

        @pl.when(i == 0)
        def _():
            accumulate(lax.dot_general(a1_ref[...], b, TN, preferred_element_type=F32))

        @pl.when(i > 0)
        def _():
            accumulate(lax.dot_general(a2_ref[...], b, TN, preferred_element_type=F32))

        @pl.when(k == nk - 1)
        def _():
            o_ref[...] = acc_ref[...]

    return _pcall(kern, name=name, out_shape=jax.ShapeDtypeStruct((NP, Dm), F32), grid=(NP // tm, Dm // tn, nk),
                  in_specs=[pl.BlockSpec((tk, tm), lambda i, j, k: (jnp.where(i == 0, k, nk - 1), 0)),
                            pl.BlockSpec((tk, tm), lambda i, j, k: (jnp.where(i == 0, 0, k), jnp.maximum(i - 1, 0))),
                            pl.BlockSpec((tk, tn), lambda i, j, k: (k, j))],
                  out_specs=pl.BlockSpec((tm, tn), lambda i, j, k: (i, j)), scratch=[pltpu.VMEM((tm, tn), F32)],
                  dims=("parallel", "parallel", "arbitrary"), vmem_mb=48, comm=comm)(d_mla, d_mix, h)


def _ln_fwd(x, g, b, *, name):
    S, Dm = x.shape
    tm = min(512, S)

    def kern(x_ref, g_ref, b_ref, y_ref, yb_ref):
        xf = x_ref[...]
        mu = jnp.mean(xf, axis=-1, keepdims=True)
        xc = xf - mu
        var = jnp.mean(xc * xc, axis=-1, keepdims=True)
        y = xc * lax.rsqrt(var + LN_EPS) * g_ref[...] + b_ref[...]
        y_ref[...] = y
        yb_ref[...] = y.astype(BF16)

    row = pl.BlockSpec((tm, Dm), lambda i: (i, 0))
    vec = pl.BlockSpec((1, Dm), lambda i: (0, 0))
    return _pcall(kern, name=name,
                  out_shape=(jax.ShapeDtypeStruct((S, Dm), F32), jax.ShapeDtypeStruct((S, Dm), BF16)),
                  grid=(S // tm,), in_specs=[row, vec, vec], out_specs=(row, row), dims=("parallel",), vmem_mb=48)(
                      x, g.reshape(1, Dm), b.reshape(1, Dm))


def _ln_bwd(dy, r, g, *, name):
    S, Dm = r.shape
    tm = min(512, S)

    def kern(dy_ref, r_ref, g_ref, dr_ref, drb_ref, dg_ref, db_ref, ds_ref):
        @pl.when(pl.program_id(0) == 0)
        def _():
            dg_ref[...] = jnp.zeros_like(dg_ref)
            db_ref[...] = jnp.zeros_like(db_ref)
            ds_ref[...] = jnp.zeros_like(ds_ref)

        rf = r_ref[...]
        dyf = dy_ref[...]
        mu = jnp.mean(rf, axis=-1, keepdims=True)
        xc = rf - mu
        var = jnp.mean(xc * xc, axis=-1, keepdims=True)
        rstd = lax.rsqrt(var + LN_EPS)
        xhat = xc * rstd
        dxh = dyf * g_ref[...]
        c1 = jnp.mean(dxh, axis=-1, keepdims=True)
        c2 = jnp.mean(dxh * xhat, axis=-1, keepdims=True)
        dr = rstd * (dxh - c1 - xhat * c2)
        dr_ref[...] = dr
        drb_ref[...] = dr.astype(BF16)
        dg_ref[...] += jnp.sum(dyf * xhat, axis=0, keepdims=True)
        db_ref[...] += jnp.sum(dyf, axis=0, keepdims=True)
        ds_ref[...] += jnp.sum(dr, axis=0, keepdims=True)

    row = pl.BlockSpec((tm, Dm), lambda i: (i, 0))
    vec = pl.BlockSpec((1, Dm), lambda i: (0, 0))
    vshape = jax.ShapeDtypeStruct((1, Dm), F32)
    return _pcall(kern, name=name,
                  out_shape=(jax.ShapeDtypeStruct((S, Dm), F32), jax.ShapeDtypeStruct((S, Dm), BF16),
                             vshape, vshape, vshape),
                  grid=(S // tm,), in_specs=[row, row, vec], out_specs=(row, row, vec, vec, vec),
                  dims=("arbitrary",), vmem_mb=48)(dy, r, g.reshape(1, Dm))


def _loss_and_dy(y, target, *, name):
    S, Dm = y.shape
    tm = min(512, S)

    def kern(y_ref, t_ref, dy_ref, l_ref):
        @pl.when(pl.program_id(0) == 0)
        def _():
            l_ref[...] = jnp.zeros_like(l_ref)

        e = y_ref[...] - t_ref[...]
        dy_ref[...] = e / float(Dm)
        per_row = jnp.mean(e * e, axis=-1, keepdims=True)
        l_ref[...] += 0.5 * jnp.sum(per_row, axis=0, keepdims=True)

    row = pl.BlockSpec((tm, Dm), lambda i: (i, 0))
    acc = pl.BlockSpec((8, LANE), lambda i: (0, 0))
    return _pcall(kern, name=name,
                  out_shape=(jax.ShapeDtypeStruct((S, Dm), F32), jax.ShapeDtypeStruct((8, LANE), F32)),
                  grid=(S // tm,), in_specs=[row, row], out_specs=(row, acc), dims=("arbitrary",), vmem_mb=48)(
                      y, target)


def _rot_sum(t):
    return pltpu.roll(t, 32, 1) + pltpu.roll(t, 96, 1)


def _mla_qkv(proj, cos_t, sin_t, qg, kvg, wuq_t, wukv_t, *, name):
    S = proj.shape[0]
    tm = min(256, S)

    def kern(ql_ref, kvl_ref, kr_ref, cos_ref, sin_ref, qg_ref, kvg_ref, wuq_ref, wukv_ref,
             qc_ref, kc_ref, v_ref, vt_ref, qn_ref, kvn_ref):
        cosv = cos_ref[...]
        sinv = sin_ref[...]

        def rope(t):
            return t * cosv + _rot_sum(t) * sinv

        ql = ql_ref[...]
        qn = (ql * lax.rsqrt(jnp.mean(ql * ql, axis=-1, keepdims=True) + RMS_EPS) * qg_ref[...]).astype(BF16)
        kvl = kvl_ref[...]
        kvn = (kvl * lax.rsqrt(jnp.mean(kvl * kvl, axis=-1, keepdims=True) + RMS_EPS) * kvg_ref[...]).astype(BF16)
        qn_ref[...] = qn
        kvn_ref[...] = kvn
        q = lax.dot_general(qn, wuq_ref[...], NT, preferred_element_type=F32)
        kv = lax.dot_general(kvn, wukv_ref[...], NT, preferred_element_type=F32)
        kr = rope(kr_ref[...]).astype(BF16)
        for h in range(N_HEADS):
            c0 = 256 * h
            qc_ref[:, c0:c0 + 128] = q[:, c0:c0 + 128].astype(BF16)
            qc_ref[:, c0 + 128:c0 + 256] = rope(q[:, c0 + 128:c0 + 256]).astype(BF16)
            kc_ref[:, c0:c0 + 128] = kv[:, c0:c0 + 128].astype(BF16)
            kc_ref[:, c0 + 128:c0 + 256] = kr
            vh = kv[:, c0 + 128:c0 + 256]
            v_ref[:, 128 * h:128 * h + 128] = vh.astype(BF16)
            vt_ref[h] = jnp.transpose(vh).astype(BF16)

    def row(w, blk):
        return pl.BlockSpec((tm, w), lambda i: (i, blk))

    def full(shape):
        return pl.BlockSpec(shape, lambda i: (0,) * len(shape))

    t = min(TQ, S)
    per = t // tm
    vt_spec = pl.BlockSpec((N_HEADS, None, 128, tm), lambda i: (0, i // per, 0, i % per))
    outs = (jax.ShapeDtypeStruct((S, 2048), BF16), jax.ShapeDtypeStruct((S, 2048), BF16),
            jax.ShapeDtypeStruct((S, 1024), BF16), jax.ShapeDtypeStruct((N_HEADS, S // t, 128, t), BF16),
            jax.ShapeDtypeStruct((S, Q_LORA), BF16), jax.ShapeDtypeStruct((S, KV_LORA), BF16))
    return _pcall(kern, name=name, out_shape=outs, grid=(S // tm,),
                  in_specs=[row(512, 0), row(256, 2), row(128, 6), row(128, 0), row(128, 0),
                            full((1, Q_LORA)), full((1, KV_LORA)), full((2048, Q_LORA)), full((2048, KV_LORA))],
                  out_specs=(row(2048, 0), row(2048, 0), row(1024, 0), vt_spec, row(512, 0), row(256, 0)),
                  dims=("parallel",), vmem_mb=48)(
                      proj, proj, proj, cos_t, sin_t, qg.reshape(1, -1), kvg.reshape(1, -1), wuq_t, wukv_t)


def _mla_qkv_bwd(dqc, dkc, dv, proj, cos_t, sin_t, qg, kvg, wuq_t, wukv_t, *, name):
    S = proj.shape[0]
    tm = min(256, S)

    def kern(dq_ref, dk_ref, dv_ref, ql_ref, kvl_ref, cos_ref, sin_ref, qg_ref, kvg_ref, wuq_ref, wukv_ref,
             dqb_ref, dkvb_ref, dml_ref, dqg_ref, dkvg_ref):
        @pl.when(pl.program_id(0) == 0)
        def _():
            dqg_ref[...] = jnp.zeros_like(dqg_ref)
            dkvg_ref[...] = jnp.zeros_like(dkvg_ref)

        cosv = cos_ref[...]
        sinv = sin_ref[...]

        def unrope(t):
            return t * cosv - _rot_sum(t) * sinv

        dkr = jnp.zeros((tm, 128), F32)
        for h in range(N_HEADS):
            c0 = 256 * h
            dqb_ref[:, c0:c0 + 128] = dq_ref[:, c0:c0 + 128].astype(BF16)
            dqb_ref[:, c0 + 128:c0 + 256] = unrope(dq_ref[:, c0 + 128:c0 + 256]).astype(BF16)
            dkvb_ref[:, c0:c0 + 128] = dk_ref[:, c0:c0 + 128].astype(BF16)
            dkvb_ref[:, c0 + 128:c0 + 256] = dv_ref[:, 128 * h:128 * h + 128].astype(BF16)
            dkr = dkr + dk_ref[:, c0 + 128:c0 + 256]

        def rms_bwd(x, g, dy):
            n = x.shape[-1]
            rs = lax.rsqrt(jnp.mean(x * x, axis=-1, keepdims=True) + RMS_EPS)
            dyg = dy * g
            dx = rs * dyg - x * (rs * rs * rs) * (jnp.sum(dyg * x, axis=-1, keepdims=True) / n)
            return dx, jnp.sum(dy * (x * rs), axis=0, keepdims=True)

        dqn = jnp.dot(dqb_ref[...], wuq_ref[...], preferred_element_type=F32)
        dql, dqg = rms_bwd(ql_ref[...], qg_ref[...], dqn)
        dqg_ref[...] += dqg
        dkvn = jnp.dot(dkvb_ref[...], wukv_ref[...], preferred_element_type=F32)
        dkvl, dkvg = rms_bwd(kvl_ref[...], kvg_ref[...], dkvn)
        dkvg_ref[...] += dkvg
        dml_ref[:, 0:512] = dql.astype(BF16)
        dml_ref[:, 512:768] = dkvl.astype(BF16)
        dml_ref[:, 768:896] = unrope(dkr).astype(BF16)
        dml_ref[:, 896:1024] = jnp.zeros((tm, 128), BF16)

    def row(w, blk):
        return pl.BlockSpec((tm, w), lambda i: (i, blk))

    def full(shape):
        return pl.BlockSpec(shape, lambda i: (0,) * len(shape))

    outs = (jax.ShapeDtypeStruct((S, 2048), BF16), jax.ShapeDtypeStruct((S, 2048), BF16),
            jax.ShapeDtypeStruct((S, W_MLA), BF16), jax.ShapeDtypeStruct((1, Q_LORA), F32),
            jax.ShapeDtypeStruct((1, KV_LORA), F32))
    return _pcall(kern, name=name, out_shape=outs, grid=(S // tm,),
                  in_specs=[row(2048, 0), row(2048, 0), row(1024, 0), row(512, 0), row(256, 2),
                            row(128, 0), row(128, 0), full((1, Q_LORA)), full((1, KV_LORA)),
                            full((2048, Q_LORA)), full((2048, KV_LORA))],
                  out_specs=(row(2048, 0), row(2048, 0), row(W_MLA, 0), full((1, Q_LORA)), full((1, KV_LORA))),
                  dims=("arbitrary",), vmem_mb=56)(
                      dqc, dkc, dv, proj, proj, cos_t, sin_t, qg.reshape(1, -1), kvg.reshape(1, -1), wuq_t, wukv_t)


def _kq_mask(t):
    krow = lax.broadcasted_iota(jnp.int32, (t, t), 0)
    qcol = lax.broadcasted_iota(jnp.int32, (t, t), 1)
    return krow <= qcol


def _flash_fwd(qc, kc, vt, *, name, comm=None):
    S = qc.shape[0]
    t = min(TQ, S)
    n = S // t

    def kern(q_ref, k_ref, vt_ref, o_ref, lse_ref, m_s, l_s, acc_s):
        qi = pl.program_id(1)
        m_s[...] = jnp.full_like(m_s, -jnp.inf)
        l_s[...] = jnp.zeros_like(l_s)
        acc_s[...] = jnp.zeros_like(acc_s)

        def scores(kb):
            k0 = pl.multiple_of(kb * t, t)
            return lax.dot_general(k_ref[pl.ds(k0, t), :], q_ref[...], NT, preferred_element_type=F32)

        def update(kb, st, masked):
            if masked:
                st = jnp.where(_kq_mask(t), st, -jnp.inf)
            m_prev = m_s[...]
            m_new = jnp.maximum(m_prev, jnp.max(st, axis=0, keepdims=True))
            a = jnp.exp2((m_prev - m_new) * SCALE_LOG2E)
            pt = jnp.exp2((st - m_new) * SCALE_LOG2E)
            l_s[...] = a * l_s[...] + jnp.sum(pt, axis=0, keepdims=True)
            acc_s[...] = a * acc_s[...] + jnp.dot(vt_ref[kb], pt.astype(BF16), preferred_element_type=F32)
            m_s[...] = m_new

        def pair(kb, second_masked):
            s0, s1 = scores(kb), scores(kb + 1)
            update(kb, s0, False)
            update(kb + 1, s1, second_masked)

        def body(i, carry):
            pair(2 * i, False)
            return carry

        lax.fori_loop(0, qi // 2, body, 0)

        @pl.when(qi % 2 == 1)
        def _():
            pair(qi - 1, True)

        @pl.when(qi % 2 == 0)
        def _():
            update(qi, scores(qi), True)
        o_ref[...] = jnp.transpose(acc_s[...] / l_s[...])
        lse_ref[pl.ds(qi, 1), :] = m_s[...] * SCALE_LOG2E + jnp.log2(l_s[...])

    q_spec = pl.BlockSpec((t, 256), lambda h, qi: (qi, h))
    k_spec = pl.BlockSpec((S, 256), lambda h, qi: (0, h))
    vt_spec = pl.BlockSpec((None, n, 128, t), lambda h, qi: (h, 0, 0, 0))
    o_spec = pl.BlockSpec((t, 128), lambda h, qi: (qi, h))
    lse_spec = pl.BlockSpec((None, n, t), lambda h, qi: (h, 0, 0))
    return _pcall(kern, name=name,
                  out_shape=(jax.ShapeDtypeStruct((S, D_MLA), F32), jax.ShapeDtypeStruct((N_HEADS, n, t), F32)),
                  grid=(N_HEADS, n), in_specs=[q_spec, k_spec, vt_spec], out_specs=(o_spec, lse_spec),
                  scratch=[pltpu.VMEM((1, t), F32), pltpu.VMEM((1, t), F32), pltpu.VMEM((128, t), F32)],
                  dims=("parallel", "arbitrary"), vmem_mb=48, comm=comm)(qc, kc, vt)


def _attn_delta(o, do, *, name):
    S = o.shape[0]
    t = min(TQ, S)
    n = S // t

    def kern(o_ref, do_ref, dl_ref):
        i = pl.program_id(0)
        prod = o_ref[...] * do_ref[...]
        lane = lax.broadcasted_iota(jnp.int32, (t, LANE), 1)
        dmat = jnp.zeros((t, LANE), F32)
        for h in range(N_HEADS):
            dmat = jnp.where(lane == h, jnp.sum(prod[:, 128 * h:128 * h + 128], axis=1, keepdims=True), dmat)
        dmat_t = jnp.transpose(dmat)
        for h in range(N_HEADS):
            dl_ref[h, pl.ds(i, 1), :] = dmat_t[h:h + 1, :]

    row = pl.BlockSpec((t, D_MLA), lambda i: (i, 0))
    return _pcall(kern, name=name, out_shape=jax.ShapeDtypeStruct((N_HEADS, n, t), F32), grid=(n,),
                  in_specs=[row, row], out_specs=pl.BlockSpec((N_HEADS, n, t), lambda i: (0, 0, 0)),
                  dims=("arbitrary",), vmem_mb=48)(o, do)


def _flash_bwd(qc, kc, v, do, lse2, delta, *, name, comm=None):
    S = qc.shape[0]
    t = min(TQ, S)
    n = S // t

    def kern(q_ref, k_ref, v_ref, do_ref, lse_ref, dl_ref, dq_ref, dk_ref, dv_ref):
        ki = pl.program_id(1)

        @pl.when(ki == 0)
        def _():
            dq_ref[...] = jnp.zeros_like(dq_ref)

        dk_ref[...] = jnp.zeros_like(dk_ref)
        dv_ref[...] = jnp.zeros_like(dv_ref)

        def step(qb, masked):
            q0 = pl.multiple_of(qb * t, t)
            kt = k_ref[...]
            qblk = q_ref[pl.ds(q0, t), :]
            dob = do_ref[pl.ds(q0, t), :].astype(BF16)
            st = lax.dot_general(kt, qblk, NT, preferred_element_type=F32)
            pt = jnp.exp2(st * SCALE_LOG2E - lse_ref[pl.ds(qb, 1), :])
            if masked:
                pt = jnp.where(_kq_mask(t), pt, 0.0)
            dv_ref[...] += jnp.dot(pt.astype(BF16), dob, preferred_element_type=F32)
            dpt = lax.dot_general(v_ref[...], dob, NT, preferred_element_type=F32)
            dst = (pt * (dpt - dl_ref[pl.ds(qb, 1), :]) * SCALE).astype(BF16)
            dk_ref[...] += jnp.dot(dst, qblk, preferred_element_type=F32)
            dq_ref[pl.ds(q0, t), :] += lax.dot_general(dst, kt, TN, preferred_element_type=F32)

        step(ki, True)

        def body(qb, carry):
            step(qb, False)
            return carry

        lax.fori_loop(ki + 1, n, body, 0)

    def whole(w):
        return pl.BlockSpec((S, w), lambda h, ki: (0, h))

    def krow(w):
        return pl.BlockSpec((t, w), lambda h, ki: (ki, h))

    stat = pl.BlockSpec((None, n, t), lambda h, ki: (h, 0, 0))
    return _pcall(kern, name=name,
                  out_shape=(jax.ShapeDtypeStruct((S, 2048), F32), jax.ShapeDtypeStruct((S, 2048), F32),
                             jax.ShapeDtypeStruct((S, D_MLA), F32)),
                  grid=(N_HEADS, n),
                  in_specs=[whole(256), krow(256), krow(128), whole(128), stat, stat],
                  out_specs=(whole(256), krow(256), krow(128)),
                  dims=("parallel", "arbitrary"), vmem_mb=56, comm=comm)(qc, kc, v, do, lse2, delta)


def _mixer_specs(S, tm):
    hb = tm // HALO
    last_hb = S // HALO - 1

    def main(w, blk):
        return pl.BlockSpec((tm, w), lambda i: (i, blk))

    def prev(w, blk):
        return pl.BlockSpec((HALO, w), lambda i: (jnp.maximum(i * hb - 1, 0), blk))

    def nxt(w, blk):
        return pl.BlockSpec((HALO, w), lambda i: (jnp.minimum((i + 1) * hb, last_hb), blk))

    def full(shape):
        return pl.BlockSpec(shape, lambda i: (0,) * len(shape))

    return main, prev, nxt, full


def _fill_halo(i, xp, xu, hp_ref, hch_ref, hcc_ref, pin_ref, ch_ref, cc_ref, tm):
    first = i == 0
    xp[0:HALO, :] = jnp.where(first, 0.0, hp_ref[...])
    xp[HALO:HALO + tm, :] = pin_ref[...]
    xu[0:HALO, :] = jnp.where(first, 0.0, hch_ref[...] * hcc_ref[...])
    xu[HALO:HALO + tm, :] = cc_ref[...] * ch_ref[...]


def _pooled(xp, g, t1, tm):
    w = POOL_WINDOWS[g]
    lanes = slice(128 * g, 128 * g + 128)
    x0 = xp[HALO:HALO + tm, lanes]
    acc = x0
    for k in range(1, w):
        acc = acc + xp[HALO - k:HALO - k + tm, lanes]
    return acc / jnp.minimum(t1, float(w)) - x0


def _conv_fwd(xu, cw_ref, tm):
    return (cw_ref[0:1, :] * xu[HALO - 2:HALO - 2 + tm, :] + cw_ref[1:2, :] * xu[HALO - 1:HALO - 1 + tm, :]
            + cw_ref[2:3, :] * xu[HALO:HALO + tm, :])


def _mixer_fwd(proj, o, wpool, ps, convw, *, name):
    S = proj.shape[0]
    tm = min(256, S)
    main, prev, _, full = _mixer_specs(S, tm)

    def kern(gm_ref, pin_ref, gp_ref, ch_ref, cb_ref, cc_ref, gc_ref, hp_ref, hch_ref, hcc_ref,
             o_ref, wp_ref, ps_ref, cw_ref, mix_ref, xp, xu):
        i = pl.program_id(0)
        _fill_halo(i, xp, xu, hp_ref, hch_ref, hcc_ref, pin_ref, ch_ref, cc_ref, tm)
        t1 = (i * tm + lax.broadcasted_iota(jnp.int32, (tm, 1), 0) + 1).astype(F32)
        for g in range(4):
            lanes = slice(128 * g, 128 * g + 128)
            pooled = _pooled(xp, g, t1, tm)
            z = jnp.dot(pooled.astype(BF16), wp_ref[g].astype(BF16), preferred_element_type=F32)
            gp = gp_ref[:, lanes]
            y = z * ps_ref[:, lanes] * (gp * _sigmoid(gp))
            mix_ref[:, 1024 + 128 * g:1024 + 128 * g + 128] = y.astype(BF16)
        gc = gc_ref[...]
        mix_ref[:, 1536:2048] = (cb_ref[...] * _conv_fwd(xu, cw_ref, tm) * (gc * _sigmoid(gc))).astype(BF16)
        gm = gm_ref[...]
        mix_ref[:, 0:1024] = (o_ref[...] * (gm * _sigmoid(gm))).astype(BF16)

    return _pcall(kern, name=name, out_shape=jax.ShapeDtypeStruct((S, 2048), BF16), grid=(S // tm,),
                  in_specs=[main(1024, 1), main(512, 4), main(512, 5), main(512, 6), main(512, 7), main(512, 8),
                            main(512, 9), prev(512, 4), prev(512, 6), prev(512, 8),
                            main(1024, 0), full((4, 128, 128)), full((1, 512)), full((3, 512))],
                  out_specs=main(2048, 0),
                  scratch=[pltpu.VMEM((tm + HALO, 512), F32), pltpu.VMEM((tm + HALO, 512), F32)],
                  dims=("parallel",), vmem_mb=48)(
                      proj, proj, proj, proj, proj, proj, proj, proj, proj, proj, o, wpool, ps.reshape(1, 512), convw)


def _mixer_bwd(dmix, proj, o, wpool, ps, convw, *, name):
    S = proj.shape[0]
    tm = min(256, S)
    n = S // tm
    main, prev, nxt, full = _mixer_specs(S, tm)

    def kern(dm_ref, dmn_ref, gm_ref, pin_ref, gp_ref, ch_ref, cb_ref, cc_ref, gc_ref,
             hp_ref, hch_ref, hcc_ref, gpn_ref, cbn_ref, gcn_ref, o_ref, wp_ref, ps_ref, cw_ref,
             d_ref, do_ref, dwp_ref, dps_ref, dcw_ref, xp, xu, ee, ed):
        i = pl.program_id(0)
        last = i == n - 1

        @pl.when(i == 0)
        def _():
            dwp_ref[...] = jnp.zeros_like(dwp_ref)
            dps_ref[...] = jnp.zeros_like(dps_ref)
            dcw_ref[...] = jnp.zeros_like(dcw_ref)

        _fill_halo(i, xp, xu, hp_ref, hch_ref, hcc_ref, pin_ref, ch_ref, cc_ref, tm)
        t1 = (i * tm + lax.broadcasted_iota(jnp.int32, (tm, 1), 0) + 1).astype(F32)
        t1n = ((i + 1) * tm + lax.broadcasted_iota(jnp.int32, (HALO, 1), 0) + 1).astype(F32)
        c_pin, c_gp, c_ch, c_cb, c_cc, c_gc = 1024, 1536, 2048, 2560, 3072, 3584

        for g in range(4):
            w = float(POOL_WINDOWS[g])
            lanes = slice(128 * g, 128 * g + 128)
            pooled = _pooled(xp, g, t1, tm)
            pb = pooled.astype(BF16)
            wp = wp_ref[g].astype(BF16)
            z = jnp.dot(pb, wp, preferred_element_type=F32)
            psl = ps_ref[:, lanes]
            sg, dsg = _silu_and_grad(gp_ref[:, lanes])
            dmp = dm_ref[:, 1024 + 128 * g:1024 + 128 * g + 128]
            dyp = dmp * sg
            d_ref[:, c_gp + 128 * g:c_gp + 128 * g + 128] = (dmp * (z * psl) * dsg).astype(BF16)
            dps_ref[:, lanes] += jnp.sum(dyp * z, axis=0, keepdims=True)
            dz = (dyp * psl).astype(BF16)
            dwp_ref[g] += lax.dot_general(pb, dz, TN, preferred_element_type=F32)
            dpl = lax.dot_general(dz, wp, NT, preferred_element_type=F32)
            ee[0:tm, lanes] = dpl / jnp.minimum(t1, w)
            gpn = gpn_ref[:, lanes]
            dzn = (dmn_ref[:, lanes] * (gpn * _sigmoid(gpn)) * psl).astype(BF16)
            dpn = lax.dot_general(dzn, wp, NT, preferred_element_type=F32)
            ee[tm:tm + HALO, lanes] = jnp.where(last, 0.0, dpn / jnp.minimum(t1n, w))
            acc = ee[0:tm, lanes]
            for k in range(1, POOL_WINDOWS[g]):
                acc = acc + ee[k:k + tm, lanes]
            d_ref[:, c_pin + 128 * g:c_pin + 128 * g + 128] = (acc - dpl).astype(BF16)

        yc = _conv_fwd(xu, cw_ref, tm)
        sgc, dsgc = _silu_and_grad(gc_ref[...])
        cb = cb_ref[...]
        dmc = dm_ref[:, 1536:2048]
        d_ref[:, c_gc:c_gc + 512] = (dmc * cb * yc * dsgc).astype(BF16)
        d_ref[:, c_cb:c_cb + 512] = (dmc * yc * sgc).astype(BF16)
        dyc = dmc * cb * sgc
        ed[0:tm, :] = dyc
        gcn = gcn_ref[...]
        ed[tm:tm + HALO, :] = jnp.where(last, 0.0, dmn_ref[:, 512:1024] * cbn_ref[...] * (gcn * _sigmoid(gcn)))
        dcw_ref[0:1, :] += jnp.sum(dyc * xu[HALO - 2:HALO - 2 + tm, :], axis=0, keepdims=True)
        dcw_ref[1:2, :] += jnp.sum(dyc * xu[HALO - 1:HALO - 1 + tm, :], axis=0, keepdims=True)
        dcw_ref[2:3, :] += jnp.sum(dyc * xu[HALO:HALO + tm, :], axis=0, keepdims=True)
        du = cw_ref[2:3, :] * dyc + cw_ref[1:2, :] * ed[1:1 + tm, :] + cw_ref[0:1, :] * ed[2:2 + tm, :]
        d_ref[:, c_cc:c_cc + 512] = (du * ch_ref[...]).astype(BF16)
        d_ref[:, c_ch:c_ch + 512] = (du * cc_ref[...]).astype(BF16)

        sgm, dsgm = _silu_and_grad(gm_ref[...])
        dmm = dm_ref[:, 0:1024]
        do_ref[...] = dmm * sgm
        d_ref[:, 0:1024] = (dmm * o_ref[...] * dsgm).astype(BF16)

    outs = (jax.ShapeDtypeStruct((S, W_MIX), BF16), jax.ShapeDtypeStruct((S, 1024), F32),
            jax.ShapeDtypeStruct((4, 128, 128), F32), jax.ShapeDtypeStruct((1, 512), F32),
            jax.ShapeDtypeStruct((3, 512), F32))
    scr = [pltpu.VMEM((tm + HALO, 512), F32) for _ in range(4)]
    return _pcall(kern, name=name, out_shape=outs, grid=(n,),
                  in_specs=[main(2048, 0), nxt(1024, 1),
                            main(1024, 1), main(512, 4), main(512, 5), main(512, 6), main(512, 7), main(512, 8),
                            main(512, 9), prev(512, 4), prev(512, 6), prev(512, 8),
                            nxt(512, 5), nxt(512, 7), nxt(512, 9),
                            main(1024, 0), full((4, 128, 128)), full((1, 512)), full((3, 512))],
                  out_specs=(main(W_MIX, 0), main(1024, 0), full((4, 128, 128)), full((1, 512)), full((3, 512))),
                  scratch=scr, dims=("arbitrary",), vmem_mb=56)(
                      dmix, dmix, proj, proj, proj, proj, proj, proj, proj, proj, proj, proj, proj, proj, proj,
                      o, wpool, ps.reshape(1, 512), convw)


def _outproj_ln(mix, wout, h, bout, g, b, *, name):
    S, Dm = h.shape
    tm = min(256, S)

    def kern(mix_ref, w_ref, h_ref, bo_ref, g_ref, b_ref, y_ref, yb_ref, r_ref):
        out = jnp.dot(mix_ref[...], w_ref[...], preferred_element_type=F32) + bo_ref[...]
        r = ALPHA * h_ref[...] + out
        r_ref[...] = r
        mu = jnp.mean(r, axis=-1, keepdims=True)
        xc = r - mu
        var = jnp.mean(xc * xc, axis=-1, keepdims=True)
        y = xc * lax.rsqrt(var + LN_EPS) * g_ref[...] + b_ref[...]
        y_ref[...] = y
        yb_ref[...] = y.astype(BF16)

    row = pl.BlockSpec((tm, Dm), lambda i: (i, 0))
    vec = pl.BlockSpec((1, Dm), lambda i: (0, 0))
    wsp = pl.BlockSpec((Dm, Dm), lambda i: (0, 0))
    sds = jax.ShapeDtypeStruct((S, Dm), F32)
    return _pcall(kern, name=name, out_shape=(sds, jax.ShapeDtypeStruct((S, Dm), BF16), sds), grid=(S // tm,),
                  in_specs=[row, wsp, row, vec, vec, vec], out_specs=(row, row, row), dims=("parallel",),
                  vmem_mb=56)(
                      mix, wout, h, bout.reshape(1, Dm), g.reshape(1, Dm), b.reshape(1, Dm))


def _adamw_math(w, g, m, v):
    m = ADAM_B1 * m + (1.0 - ADAM_B1) * g
    v = ADAM_B2 * v + (1.0 - ADAM_B2) * (g * g)
    m_hat = m / (1.0 - ADAM_B1 ** ADAM_STEP)
    v_hat = v / (1.0 - ADAM_B2 ** ADAM_STEP)
    delta = -ADAM_LR * (m_hat / (jnp.sqrt(v_hat) + ADAM_EPS) + ADAM_WD * w)
    return delta, m, v


def _row_tile(R, C):
    best = None
    for cand in range(8, R, 8):
        if R % cand == 0 and cand * C <= 256 * 1024:
            best = cand
    return best if best is not None else R


def _adamw(w, g, m, v, *, name):
    shape = w.shape
    C = shape[-1]
    R = 1
    for s in shape[:-1]:
        R *= s
    tr = _row_tile(R, C)

    def kern(w_ref, g_ref, m_ref, v_ref, d_ref, mo_ref, vo_ref):
        d, mn, vn = _adamw_math(w_ref[...], g_ref[...], m_ref[...], v_ref[...])
        d_ref[...] = d
        mo_ref[...] = mn
        vo_ref[...] = vn

    blk = pl.BlockSpec((tr, C), lambda i: (i, 0))
    sds = jax.ShapeDtypeStruct((R, C), F32)
    outs = _pcall(kern, name=name, out_shape=(sds, sds, sds), grid=(R // tr,), in_specs=[blk] * 4,
                  out_specs=(blk, blk, blk), dims=("parallel",), vmem_mb=48)(
                      w.reshape(R, C), g.reshape(R, C), m.reshape(R, C), v.reshape(R, C))
    return tuple(t.reshape(shape) for t in outs)


def _adamw_halves(w, m, v, halves, c_idx, *, name):
    _, R, C = w.shape
    ch = C // 2
    tr = _row_tile(R, ch)
    nb = R // tr

    def kern(c_ref, w_ref, a0_ref, b0_ref, a1_ref, b1_ref, m_ref, v_ref, g_ref, d_ref, mo_ref, vo_ref):
        layer = pl.program_id(0) // nb
        mine = pl.program_id(1) == c_ref[0]
        g = jnp.where(layer == 0, jnp.where(mine, a0_ref[...], b0_ref[...]),
                      jnp.where(mine, a1_ref[...], b1_ref[...]))
        g_ref[...] = g
        d, mn, vn = _adamw_math(w_ref[...], g, m_ref[...], v_ref[...])
        d_ref[...] = d
        mo_ref[...] = mn
        vo_ref[...] = vn

    full = pl.BlockSpec((tr, ch), lambda i, hc, c: (i, hc))
    half = pl.BlockSpec((tr, ch), lambda i, hc, c: (i % nb, 0))
    gs = pltpu.PrefetchScalarGridSpec(num_scalar_prefetch=1, grid=(2 * nb, 2),
                                      in_specs=[full, half, half, half, half, full, full], out_specs=(full,) * 4)
    sds = jax.ShapeDtypeStruct((2 * R, C), F32)
    (a0, b0), (a1, b1) = halves
    outs = pl.pallas_call(kern, name=name, out_shape=(sds,) * 4, grid_spec=gs,
                          compiler_params=pltpu.CompilerParams(dimension_semantics=("parallel", "parallel"),
                                                               vmem_limit_bytes=48 << 20))(
                              c_idx, w.reshape(2 * R, C), a0, b0, a1, b1, m.reshape(2 * R, C), v.reshape(2 * R, C))
    return tuple(t.reshape(2, R, C) for t in outs)


def _small_sum_adamw(gathered, w, m, v, *, name):
    R = w.shape[0]

    def kern(ga_ref, w_ref, m_ref, v_ref, g_ref, d_ref, mo_ref, vo_ref):
        g = ga_ref[0]
        for k in range(1, N_DEV):
            g = g + ga_ref[k]
        g_ref[...] = g
        d, mn, vn = _adamw_math(w_ref[...], g, m_ref[...], v_ref[...])
        d_ref[...] = d
        mo_ref[...] = mn
        vo_ref[...] = vn

    sds = jax.ShapeDtypeStruct((R, LANE), F32)
    return _pcall(kern, name=name, out_shape=(sds, sds, sds, sds), vmem_mb=48)(gathered, w, m, v)


def _pair_sum(g, theirs, c_idx, *, name):
    R, C = g.shape
    ch = C // 2
    tr = _row_tile(R, ch)

    def kern(c_ref, a_ref, b_ref, o_ref):
        o_ref[...] = (a_ref[...] + b_ref[...]).astype(BF16)

    gs = pltpu.PrefetchScalarGridSpec(
        num_scalar_prefetch=1, grid=(R // tr,),
        in_specs=[pl.BlockSpec((tr, ch), lambda i, c: (i, c[0])), pl.BlockSpec((tr, ch), lambda i, c: (i, 0))],
        out_specs=pl.BlockSpec((tr, ch), lambda i, c: (i, 0)))
    return pl.pallas_call(kern, name=name, out_shape=jax.ShapeDtypeStruct((R, ch), BF16), grid_spec=gs,
                          compiler_params=pltpu.CompilerParams(dimension_semantics=("parallel",),
                                                               vmem_limit_bytes=48 << 20))(c_idx, g, theirs)


WeightRows = collections.namedtuple("WeightRows", "full_rows own_rows cols pieces zero_rows")


def _w_in_piece_a(j):
    return jnp.where(j == 0, 0, 1232 * j + GAP)


def _w_in_piece_b(j):
    return jnp.where(j == 0, GAP_AT + GAP, 1232 * j + GAP_AT + GAP)


W_IN = WeightRows(NP, 1232, D_MODEL, ((0, GAP_AT, _w_in_piece_a), (GAP_AT, 1232 - GAP_AT, _w_in_piece_b)),
                  ((GAP_AT, GAP),))
W_OUT = WeightRows(2048, 512, D_MODEL, ((0, 512, lambda j: 512 * j),), ())
W_UQ = WeightRows(2048, 384, Q_LORA, ((0, 192, lambda j: 512 * j), (192, 192, lambda j: 512 * j + 256)),
                  tuple((256 * h + 192, 64) for h in range(N_HEADS)))
W_UKV = WeightRows(2048, 512, KV_LORA, ((0, 512, lambda j: 512 * j),), ())
W_CONV = WeightRows(64, 16, 256, ((0, 16, lambda j: 16 * j),), ())
SHARDED = (W_IN, W_OUT, W_UQ, W_UKV)


def _mesh_pos():
    x, y, c = lax.axis_index("x"), lax.axis_index("y"), lax.axis_index("c")
    return x, y, c


def _other_chips(x, y):
    return [(1 - x, y), (x, 1 - y), (1 - x, 1 - y)]


def _rows(start, n):
    return pl.ds(pl.multiple_of(start, 16), n)


def _half_cols(spec, c):
    ch = spec.cols // 2
    return pl.ds(pl.multiple_of(c * ch, LANE), ch)


def _allgather_script(specs, shards, zeros):
    na = len(specs)
    zlist = [a for a in range(na) if zeros[a] is not None]
    plan_first, plan_own, plan_zero = [], [], []
    for a, spec in enumerate(specs):
        for p in range(len(spec.pieces)):
            plan_own.append((a, p))
            for k in range(3):
                plan_first.append((a, p, k))
        for z in range(len(spec.zero_rows)):
            for l in range(shards[a].shape[0]):
                plan_zero.append((a, z, l))
    nf = len(plan_first)
    n_sems = 2 * nf + len(plan_own) + len(plan_zero)

    def copies(ins_all, outs, send_sems, recv_sems):
        ins = ins_all[:na]
        zrefs = dict(zip(zlist, ins_all[na:]))
        x, y, c = _mesh_pos()
        j = 2 * x + y
        chips = _other_chips(x, y)
        sibling = (x, y, 1 - c)

        def remote(src, dst, sem, to):
            return pltpu.make_async_remote_copy(src_ref=src, dst_ref=dst, send_sem=send_sems.at[sem],
                                                recv_sem=recv_sems.at[sem], device_id=to, device_id_type=MESH)

        def block(a, p, chip, cols):
            _, n, dst = specs[a].pieces[p]
            return outs[a].at[:, _rows(dst(chip), n), cols]

        def first(i):
            a, p, k = plan_first[i]
            src0, n, _ = specs[a].pieces[p]
            cols = _half_cols(specs[a], c)
            return remote(ins[a].at[:, pl.ds(src0, n), cols], block(a, p, j, cols), i, (*chips[k], c))

        def landed(i, half):
            a, p, k = plan_first[i]
            return block(a, p, 2 * chips[k][0] + chips[k][1], _half_cols(specs[a], half))

        def arrival(i, half, sem):
            return remote(landed(i, half), landed(i, half), sem, sibling)

        def passed(i):
            return remote(landed(i, c), landed(i, c), nf + i, sibling)

        def own(i):
            a, p = plan_own[i]
            src0, n, _ = specs[a].pieces[p]
            return remote(ins[a].at[:, pl.ds(src0, n), :], block(a, p, j, slice(None)), 2 * nf + i, sibling)

        def zero(i):
            a, z, l = plan_zero[i]
            r0, n = specs[a].zero_rows[z]
            return remote(zrefs[a].at[pl.ds(0, n), :], outs[a].at[l, pl.ds(r0, n), :],
                          2 * nf + len(plan_own) + i, sibling)

        fixed = [own(i) for i in range(len(plan_own))] + [zero(i) for i in range(len(plan_zero))]
        return c, fixed, first, arrival, passed

    def start(ins, outs, send_sems, recv_sems):
        _, fixed, first, _, _ = copies(ins, outs, send_sems, recv_sems)
        for cp in fixed:
            cp.start()
        for i in range(nf):
            first(i).start()

    def finish(ins, outs, send_sems, recv_sems):
        c, fixed, first, arrival, passed = copies(ins, outs, send_sems, recv_sems)
        for i in range(nf):
            arrival(i, c, i).wait_recv()
            passed(i).start()
        for i in range(nf):
            arrival(i, 1 - c, nf + i).wait_recv()
        for cp in fixed:
            cp.wait()
        for i in range(nf):
            first(i).wait_send()
            passed(i).wait_send()

    out_shape = tuple(jax.ShapeDtypeStruct((shards[a].shape[0], spec.full_rows, spec.cols), BF16)
                      for a, spec in enumerate(specs))
    args = tuple(shards) + tuple(zeros[a] for a in zlist)
    return CommScript(args, out_shape, n_sems, start, finish)


def _start_all_wait_all(args, out_shape, n_sems, make_copies):
    def start(ins, outs, send_sems, recv_sems):
        for cp in make_copies(ins, outs, send_sems, recv_sems):
            cp.start()

    def finish(ins, outs, send_sems, recv_sems):
        for cp in make_copies(ins, outs, send_sems, recv_sems):
            cp.wait()

    return CommScript(tuple(args), tuple(out_shape), n_sems, start, finish)


def _exchange_script(specs, grads):
    na = len(grads)

    def make_copies(ins, outs, send_sems, recv_sems):
        x, y, c = _mesh_pos()
        return [pltpu.make_async_remote_copy(
            src_ref=ins[a].at[:, _half_cols(specs[a], 1 - c)], dst_ref=outs[a], send_sem=send_sems.at[a],
            recv_sem=recv_sems.at[a], device_id=(x, y, 1 - c), device_id_type=MESH) for a in range(na)]

    out_shape = [jax.ShapeDtypeStruct((s.full_rows, s.cols // 2), F32) for s in specs]
    return _start_all_wait_all(grads, out_shape, na, make_copies)


def _scatter_script(specs, parts):
    na = len(parts)
    plan = [(a, p, k) for a in range(na) for p in range(len(specs[a].pieces)) for k in range(3)]

    def make_copies(ins, outs, send_sems, recv_sems):
        x, y, c = _mesh_pos()
        chips = _other_chips(x, y)
        copies = []
        for i, (a, p, k) in enumerate(plan):
            src0, n, dst = specs[a].pieces[p]
            pk = 2 * chips[k][0] + chips[k][1]
            copies.append(pltpu.make_async_remote_copy(
                src_ref=ins[a].at[_rows(dst(pk), n), :], dst_ref=outs[a].at[k, pl.ds(src0, n), :],
                send_sem=send_sems.at[i], recv_sem=recv_sems.at[i], device_id=(*chips[k], c), device_id_type=MESH))
        return copies

    out_shape = [jax.ShapeDtypeStruct((3, s.own_rows, s.cols // 2), BF16) for s in specs]
    return _start_all_wait_all(parts, out_shape, len(plan), make_copies)


def _chip_sum(spec, part, recv, *, name):
    ch = spec.cols // 2
    npieces = len(spec.pieces)

    def kern(recv_ref, part_ref, o_ref, own_ref, sems):
        j = 2 * lax.axis_index("x") + lax.axis_index("y")
        copies = []
        for p, (src0, n, dst) in enumerate(spec.pieces):
            copies.append(pltpu.make_async_copy(part_ref.at[_rows(dst(j), n), :], own_ref.at[pl.ds(src0, n), :],
                                                sems.at[p]))
        for cp in copies:
            cp.start()
        for cp in copies:
            cp.wait()
        o_ref[...] = ((own_ref[...].astype(F32) + recv_ref[0].astype(F32)) + recv_ref[1].astype(F32)) \
            + recv_ref[2].astype(F32)

    vm = pl.BlockSpec(memory_space=pltpu.VMEM)
    return _pcall(kern, name=name, out_shape=jax.ShapeDtypeStruct((spec.own_rows, ch), F32),
                  in_specs=[vm, HBM_SPEC], out_specs=vm,
                  scratch=[pltpu.VMEM((spec.own_rows, ch), BF16), pltpu.SemaphoreType.DMA((npieces,))],
                  vmem_mb=48)(recv, part)


def _sibling_script(sums):
    na = len(sums)

    def make_copies(ins, outs, send_sems, recv_sems):
        x, y, c = _mesh_pos()
        return [pltpu.make_async_remote_copy(
            src_ref=ins[a], dst_ref=outs[a], send_sem=send_sems.at[a], recv_sem=recv_sems.at[a],
            device_id=(x, y, 1 - c), device_id_type=MESH) for a in range(na)]

    out_shape = [jax.ShapeDtypeStruct(t.shape, t.dtype) for t in sums]
    return _start_all_wait_all(sums, out_shape, na, make_copies)


class _GradReducer:
    def __init__(self, layer, grads, c_idx):
        self.layer, self.grads, self.c_idx = layer, tuple(grads), c_idx
        self.names = [f"{nm}{layer}" for nm in ("w_in", "w_out", "w_uq", "w_ukv")]

    def exchange(self):
        return _exchange_script(SHARDED, self.grads)

    def scatter(self, theirs):
        self.parts = tuple(_pair_sum(g, th, self.c_idx, name=f"pair_sum_{nm}")
                           for g, th, nm in zip(self.grads, theirs, self.names))
        return _scatter_script(SHARDED, self.parts)

    def sibling(self, recv):
        self.sums = tuple(_chip_sum(s, p, r, name=f"chip_sum_{nm}")
                          for s, p, r, nm in zip(SHARDED, self.parts, recv, self.names))
        return _sibling_script(self.sums)

    def done(self, others):
        return list(zip(self.sums, others))


def _allgather_small(block, *, name):
    m_per, n = block.shape

    def body(x_ref, out_ref, send_sems, recv_sems, local_sem):
        x, y, c = _mesh_pos()
        me, sibling = (x, y, c), (x, y, 1 - c)
        chips = _other_chips(x, y)

        def rows(px, py, pc):
            return out_ref.at[4 * px + 2 * py + pc]

        def copy(k, blk, to, src=None):
            return pltpu.make_async_remote_copy(
                src_ref=rows(*blk) if src is None else src, dst_ref=rows(*blk), send_sem=send_sems.at[k],
                recv_sem=recv_sems.at[k], device_id=to, device_id_type=MESH)

        mine = pltpu.make_async_copy(x_ref, rows(*me), local_sem)
        mine.start()
        first = [copy(0, me, sibling, src=x_ref)]
        first += [copy(1 + k, me, (*chip, c), src=x_ref) for k, chip in enumerate(chips)]
        for cp in first:
            cp.start()
        passed = [copy(4 + k, (*chip, c), sibling) for k, chip in enumerate(chips)]
        for k, chip in enumerate(chips):
            copy(1 + k, (*chip, c), me).wait_recv()
            passed[k].start()
        copy(0, sibling, me).wait_recv()
        for k, chip in enumerate(chips):
            copy(4 + k, (*chip, 1 - c), me).wait_recv()
        for cp in first + passed:
            cp.wait_send()
        mine.wait()

    vm = pl.BlockSpec(memory_space=pltpu.VMEM)
    return _pcall(body, name=name, out_shape=jax.ShapeDtypeStruct((N_DEV, m_per, n), block.dtype),
                  in_specs=[vm], out_specs=vm,
                  scratch=[pltpu.SemaphoreType.DMA((7,)), pltpu.SemaphoreType.DMA((7,)), pltpu.SemaphoreType.DMA],
                  vmem_mb=48)(block)


def _rope_tables(positions):
    half = ROPE // 2
    inv_freq = ROPE_THETA ** (-jnp.arange(half, dtype=F32) / half)
    ang = positions.astype(F32)[:, None] * inv_freq
    cos, sin = jnp.cos(ang), jnp.sin(ang)
    S = positions.shape[0]
    cos_t = jnp.concatenate([cos, cos, jnp.ones((S, 64), F32)], axis=1)
    sin_t = jnp.concatenate([-sin, sin, jnp.zeros((S, 64), F32)], axis=1)
    return cos_t, sin_t


def _local_step(x, positions, target, emb_g, emb_b, weights0, weights1, q_g, kv_g, w_pool, pool_scale, conv_w,
                b_out, ln_g, ln_b, c_idx=None):
    cos_t, sin_t = _rope_tables(positions)
    h, hb = _ln_fwd(x, emb_g, emb_b, name="emb_ln")
    weights = [weights0, weights1]
    saved = []
    for l in range(DEPTH):
        w_in_t, w_out, w_uq_t, w_ukv_t = weights[l]
        proj = _matmul(hb, w_in_t, "nt", name=f"in_proj{l}", tm=1024, tn=1024, tk=2048, vmem_mb=56)
        qc, kc, v, vt, qn, kvn = _mla_qkv(proj, cos_t, sin_t, q_g[l], kv_g[l], w_uq_t, w_ukv_t, name=f"mla_qkv{l}")
        nxt = weights[l + 1] if l + 1 < DEPTH else None
        if isinstance(nxt, CommScript):
            (o, lse2), landed = _flash_fwd(qc, kc, vt, name=f"flash_fwd{l}", comm=nxt)
            weights[l + 1] = tuple(a[0] for a in landed)
        else:
            o, lse2 = _flash_fwd(qc, kc, vt, name=f"flash_fwd{l}")
        mix = _mixer_fwd(proj, o, w_pool[l], pool_scale[l], conv_w[l], name=f"mixer_fwd{l}")
        h_next, hb_next, r = _outproj_ln(mix, w_out, h, b_out[l], ln_g[l], ln_b[l], name=f"out_proj_ln{l}")
        saved.append((hb, proj, qc, kc, v, qn, kvn, o, lse2, mix, r))
        h, hb = h_next, hb_next

    dh, loss_acc = _loss_and_dy(h, target, name="loss")
    small = [None] * DEPTH
    big = [None] * DEPTH
    above = scatter_above = None
    for l in reversed(range(DEPTH)):
        w_in_t, w_out, w_uq_t, w_ukv_t = weights[l]
        hb_in, proj, qc, kc, v, qn, kvn, o, lse2, mix, r = saved[l]
        dr, drb, d_ln_g, d_ln_b, d_b_out = _ln_bwd(dh, r, ln_g[l], name=f"ln_bwd{l}")
        dmix = _matmul(drb, w_out, "nt", name=f"dmix{l}", tm=1024, tn=1024, tk=2048, vmem_mb=56)
        d_w_out = _matmul(mix, drb, "tn", name=f"dw_out{l}", tm=1024, tn=2048, tk=512, vmem_mb=56)
        d_mix, do, d_w_pool, d_ps, d_conv = _mixer_bwd(dmix, proj, o, w_pool[l], pool_scale[l], conv_w[l],
                                                       name=f"mixer_bwd{l}")
        delta = _attn_delta(o, do, name=f"attn_delta{l}")
        if above is not None:
            (dqc, dkc, dv), recv = _flash_bwd(qc, kc, v, do, lse2, delta, name=f"flash_bwd{l}", comm=scatter_above)
            sibling_above = above.sibling(recv)
        else:
            dqc, dkc, dv = _flash_bwd(qc, kc, v, do, lse2, delta, name=f"flash_bwd{l}")
        dqb, dkvb, d_mla, d_qg, d_kvg = _mla_qkv_bwd(dqc, dkc, dv, proj, cos_t, sin_t, q_g[l], kv_g[l],
                                                     w_uq_t, w_ukv_t, name=f"mla_qkv_bwd{l}")
        d_w_uq_t = _matmul(dqb, qn, "tn", name=f"dw_uq{l}", tm=2048, tn=512, tk=512)
        d_w_ukv_t = _matmul(dkvb, kvn, "tn", name=f"dw_ukv{l}", tm=2048, tn=256, tk=512)
        if above is not None:
            d_w_in_t, others = _dproj_t_times_h(d_mla, d_mix, hb_in, name=f"dw_in{l}", comm=sibling_above)
            big[l + 1] = above.done(others)
            above = None
        else:
            d_w_in_t = _dproj_t_times_h(d_mla, d_mix, hb_in, name=f"dw_in{l}")
        big[l] = (d_w_in_t, d_w_out, d_w_uq_t, d_w_ukv_t)
        small[l] = dict(q_g=d_qg[0], kv_g=d_kvg[0], w_pool=d_w_pool, pool_scale=d_ps[0], conv_w=d_conv,
                        b_out=d_b_out[0], ln_g=d_ln_g[0], ln_b=d_ln_b[0])
        if c_idx is None:
            dh = _dproj_times_w(d_mla, d_mix, w_in_t, dr, ALPHA, name=f"dh{l}")
        elif l > 0:
            above = _GradReducer(l, big[l], c_idx)
            dh, theirs = _dproj_times_w(d_mla, d_mix, w_in_t, dr, ALPHA, name=f"dh{l}", comm=above.exchange())
            scatter_above = above.scatter(theirs)
        else:
            last = _GradReducer(l, big[l], c_idx)
            theirs = _run_comm(last.exchange(), name="exchange_halves0")
            dh, recv = _dproj_times_w(d_mla, d_mix, w_in_t, dr, ALPHA, name=f"dh{l}", comm=last.scatter(theirs))
    grad_x, _, d_emb_g, d_emb_b, _ = _ln_bwd(dh, x, emb_g, name="emb_ln_bwd")
    if c_idx is not None:
        big[0] = last.done(_run_comm(last.sibling(recv), name="send_to_sibling0"))
    return loss_acc[0, 0], grad_x, d_emb_g[0], d_emb_b[0], small, big


SMALL_ORDER = ("emb_ln_g", "emb_ln_b", "q_norm_g", "kv_norm_g", "w_pool", "pool_scale", "b_out", "ln_g", "ln_b")


def _pack_small(arrs, extra_rows):
    flat = jnp.concatenate([a.reshape(-1) for a in arrs])
    rows = flat.shape[0] // LANE
    total = -(-(rows + extra_rows) // 8) * 8
    return jnp.pad(flat, (0, total * LANE - flat.shape[0])).reshape(total, LANE)


def _unpack_small(packed, shapes):
    flat = packed.reshape(-1)
    out, off = [], 0
    for shp in shapes:
        n = 1
        for s in shp:
            n *= s
        out.append(flat[off:off + n].reshape(shp))
        off += n
    return out, off


def kernel(x, positions, emb_ln_g, emb_ln_b, w_in, q_norm_g, kv_norm_g, w_uq, w_ukv, w_pool, pool_scale, conv_w, w_out, b_out, ln_g, ln_b, loss_target, m_emb_ln_g, m_emb_ln_b, m_w_in, m_q_norm_g, m_kv_norm_g, m_w_uq, m_w_ukv, m_w_pool, m_pool_scale, m_conv_w, m_w_out, m_b_out, m_ln_g, m_ln_b, v_emb_ln_g, v_emb_ln_b, v_w_in, v_q_norm_g, v_kv_norm_g, v_w_uq, v_w_ukv, v_w_pool, v_pool_scale, v_conv_w, v_w_out, v_b_out, v_ln_g, v_ln_b):
    xi, yi, ci = lax.axis_index("x"), lax.axis_index("y"), lax.axis_index("c")
    chip = 2 * xi + yi
    c_idx = ci.reshape(1).astype(jnp.int32)

    def t(a):
        return jnp.swapaxes(a, 1, 2)

    conv_bits = lax.bitcast_convert_type(conv_w.reshape(DEPTH, 3 * 128), BF16).reshape(DEPTH, 3, 256)
    conv_bits = jnp.pad(conv_bits, ((0, 0), (0, 13), (0, 0)))
    own = (t(w_in).astype(BF16), w_out.astype(BF16), t(w_uq).astype(BF16), t(w_ukv).astype(BF16))
    zeros = (jnp.zeros((GAP, D_MODEL), BF16), None, jnp.zeros((64, Q_LORA), BF16), None)
    gather0 = _allgather_script(SHARDED + (W_CONV,), tuple(a[0:1] for a in own) + (conv_bits,), zeros + (None,))
    *weights0, a_conv = _run_comm(gather0, name="allgather_weights0")
    weights0 = tuple(a[0] for a in weights0)
    gather1 = _allgather_script(SHARDED, tuple(a[1:2] for a in own), zeros)
    conv_rows = a_conv.reshape(DEPTH, N_CHIPS, 16, 256)[:, :, :3, :]
    conv_full = lax.bitcast_convert_type(conv_rows.reshape(DEPTH, N_CHIPS, 3, 128, 2), F32)
    conv_full = jnp.transpose(conv_full, (0, 2, 1, 3)).reshape(DEPTH, 3, 512)

    loss_part, grad_x, d_emb_g, d_emb_b, grads, reduced = _local_step(
        x[0], positions[0], loss_target[0], emb_ln_g, emb_ln_b, weights0, gather1, q_norm_g, kv_norm_g,
        w_pool, pool_scale, conv_full, b_out, ln_g, ln_b, c_idx)

    small_g = [d_emb_g, d_emb_b,
               jnp.stack([grads[l]["q_g"] for l in range(DEPTH)]), jnp.stack([grads[l]["kv_g"] for l in range(DEPTH)]),
               jnp.stack([grads[l]["w_pool"] for l in range(DEPTH)]),
               jnp.stack([grads[l]["pool_scale"] for l in range(DEPTH)]),
               jnp.stack([grads[l]["b_out"] for l in range(DEPTH)]), jnp.stack([grads[l]["ln_g"] for l in range(DEPTH)]),
               jnp.stack([grads[l]["ln_b"] for l in range(DEPTH)]),
               jnp.stack([grads[l]["conv_w"] for l in range(DEPTH)]),
               jnp.pad(loss_part.reshape(1), (0, LANE - 1))]
    small_w = [emb_ln_g, emb_ln_b, q_norm_g, kv_norm_g, w_pool, pool_scale, b_out, ln_g, ln_b]
    small_m = [m_emb_ln_g, m_emb_ln_b, m_q_norm_g, m_kv_norm_g, m_w_pool, m_pool_scale, m_b_out, m_ln_g, m_ln_b]
    small_v = [v_emb_ln_g, v_emb_ln_b, v_q_norm_g, v_kv_norm_g, v_w_pool, v_pool_scale, v_b_out, v_ln_g, v_ln_b]
    extra = (DEPTH * 3 * 512 + LANE) // LANE
    packed_g = _pack_small(small_g, 0)
    gathered = _allgather_small(packed_g, name="allgather_small")
    g_tot, d_small, m_small, v_small = _small_sum_adamw(
        gathered, _pack_small(small_w, extra), _pack_small(small_m, extra), _pack_small(small_v, extra),
        name="small_sum_adamw")
    shapes = [w.shape for w in small_w]
    g_list, off = _unpack_small(g_tot, shapes)
    d_list, _ = _unpack_small(d_small, shapes)
    m_list, _ = _unpack_small(m_small, shapes)
    v_list, _ = _unpack_small(v_small, shapes)
    flat_tot = g_tot.reshape(-1)
    conv_tot = flat_tot[off:off + DEPTH * 3 * 512].reshape(DEPTH, 3, 512)
    loss = flat_tot[off + DEPTH * 3 * 512]
    g_conv = lax.dynamic_slice_in_dim(conv_tot, chip * 128, 128, axis=2)

    def halves(a):
        return [reduced[l][a] for l in range(DEPTH)]

    def whole(a):
        return jnp.stack([jnp.where(ci == 0, jnp.concatenate([mine, oth], axis=1),
                                    jnp.concatenate([oth, mine], axis=1)) for mine, oth in halves(a)])

    upd = {}
    upd["w_in"] = tuple(t(o) for o in _adamw_halves(t(w_in), t(m_w_in), t(v_w_in), halves(0), c_idx,
                                                    name="adamw_w_in"))
    upd["w_out"] = _adamw_halves(w_out, m_w_out, v_w_out, halves(1), c_idx, name="adamw_w_out")
    g_uq, g_ukv = t(whole(2)), t(whole(3))
    upd["w_uq"] = (g_uq,) + _adamw(w_uq, g_uq, m_w_uq, v_w_uq, name="adamw_w_uq")
    upd["w_ukv"] = (g_ukv,) + _adamw(w_ukv, g_ukv, m_w_ukv, v_w_ukv, name="adamw_w_ukv")
    upd["conv_w"] = (g_conv,) + _adamw(conv_w, g_conv, m_conv_w, v_conv_w, name="adamw_conv_w")
    for i, nm in enumerate(SMALL_ORDER):
        upd[nm] = (g_list[i], d_list[i], m_list[i], v_list[i])

    order = ("emb_ln_g", "emb_ln_b", "w_in", "q_norm_g", "kv_norm_g", "w_uq", "w_ukv", "w_pool", "pool_scale",
             "conv_w", "w_out", "b_out", "ln_g", "ln_b")
    outs = [loss, grad_x[None]]
    for field in range(4):
        outs += [upd[nm][field] for nm in order]
    return tuple(outs)
```

```python
import collections

import jax
import jax.numpy as jnp
from jax import lax
from jax.experimental import pallas as pl
from jax.experimental.pallas import tpu as pltpu

F32 = jnp.float32
BF16 = jnp.bfloat16
MESH = pl.DeviceIdType.MESH

D_MODEL = 2048
DEPTH = 2
N_HEADS = 8
NOPE = 128
ROPE = 64
Q_LORA = 512
KV_LORA = 256
D_MLA = 1024
POOL_WINDOWS = (2, 4, 8, 16)
D_IN_PROJ = 4928
LN_EPS = 1e-5
RMS_EPS = 1e-6
ROPE_THETA = 10000.0
ALPHA = (2 * DEPTH) ** 0.25
SCALE = (NOPE + ROPE) ** -0.5
LOG2E = 1.4426950408889634
SCALE_LOG2E = SCALE * LOG2E
ADAM_LR = 0.001
ADAM_B1 = 0.9
ADAM_B2 = 0.999
ADAM_EPS = 1e-08
ADAM_WD = 0.01
ADAM_STEP = 10

NP = 5120
GAP_AT = 832
GAP = NP - D_IN_PROJ
W_MLA = 1024
W_MIX = NP - W_MLA
HALO = 16
LANE = 128
N_CHIPS = 4
N_DEV = 8
TQ = 512

NN = (((1,), (0,)), ((), ()))
NT = (((1,), (1,)), ((), ()))
TN = (((0,), (0,)), ((), ()))


CommScript = collections.namedtuple("CommScript", "args out_shape n_sems start finish")
HBM_SPEC = pl.BlockSpec(memory_space=pl.ANY)


def _pcall(kern, *, name, out_shape, grid=None, in_specs=None, out_specs=None, scratch=(), dims=None,
           vmem_mb=None, comm=None):
    cp = {}
    if dims is not None:
        cp["dimension_semantics"] = dims if comm is None else ("arbitrary",) * len(dims)
    if vmem_mb is not None:
        cp["vmem_limit_bytes"] = vmem_mb << 20
    if comm is None:
        args = dict(name=name, out_shape=out_shape, scratch_shapes=list(scratch),
                    compiler_params=pltpu.CompilerParams(**cp))
        if grid is not None:
            args["grid"] = grid
        if in_specs is not None:
            args["in_specs"] = in_specs
        if out_specs is not None:
            args["out_specs"] = out_specs
        return pl.pallas_call(kern, **args)

    single = not isinstance(out_shape, (tuple, list))
    own_out = (out_shape,) if single else tuple(out_shape)
    own_out_specs = (out_specs,) if single else tuple(out_specs)
    n_in, n_out, n_scr = len(in_specs), len(own_out), len(scratch)
    na, no = len(comm.args), len(comm.out_shape)

    def at(end):
        cond = None
        for d, n in enumerate(grid):
            here = pl.program_id(d) == (n - 1 if end else 0)
            cond = here if cond is None else jnp.logical_and(cond, here)
        return cond

    def wrapped(*refs):
        own_in, c_in = refs[:n_in], refs[n_in:n_in + na]
        o0 = n_in + na
        own_o, c_out = refs[o0:o0 + n_out], refs[o0 + n_out:o0 + n_out + no]
        s0 = o0 + n_out + no
        own_s, (send_sems, recv_sems) = refs[s0:s0 + n_scr], refs[s0 + n_scr:]

        @pl.when(at(False))
        def _():
            comm.start(c_in, c_out, send_sems, recv_sems)

        kern(*own_in, *own_o, *own_s)

        @pl.when(at(True))
        def _():
            comm.finish(c_in, c_out, send_sems, recv_sems)

    call = pl.pallas_call(
        wrapped, name=name, out_shape=own_out + tuple(comm.out_shape), grid=grid,
        in_specs=list(in_specs) + [HBM_SPEC] * na, out_specs=own_out_specs + (HBM_SPEC,) * no,
        scratch_shapes=list(scratch) + [pltpu.SemaphoreType.DMA((comm.n_sems,)),
                                        pltpu.SemaphoreType.DMA((comm.n_sems,))],
        compiler_params=pltpu.CompilerParams(**cp))

    def run(*args):
        res = call(*args, *comm.args)
        own = res[0] if single else tuple(res[:n_out])
        return own, tuple(res[n_out:])

    return run


def _run_comm(script, *, name):
    na, no = len(script.args), len(script.out_shape)

    def body(*refs):
        ins, outs = refs[:na], refs[na:na + no]
        send_sems, recv_sems = refs[na + no:]
        script.start(ins, outs, send_sems, recv_sems)
        script.finish(ins, outs, send_sems, recv_sems)

    return pl.pallas_call(
        body, name=name, out_shape=tuple(script.out_shape), in_specs=[HBM_SPEC] * na, out_specs=(HBM_SPEC,) * no,
        scratch_shapes=[pltpu.SemaphoreType.DMA((script.n_sems,)), pltpu.SemaphoreType.DMA((script.n_sems,))])(
            *script.args)


def _sigmoid(g):
    return 1.0 / (1.0 + jnp.exp(-g))


def _silu_and_grad(g):
    sig = _sigmoid(g)
    return g * sig, sig * (1.0 + g * (1.0 - sig))


def _matmul(a, b, mode, *, name, tm, tn, tk, out_dtype=F32, vmem_mb=48, comm=None):
    if mode == "nn":
        (M, K), N = a.shape, b.shape[1]
    elif mode == "nt":
        (M, K), N = a.shape, b.shape[0]
    else:
        (K, M), N = a.shape, b.shape[1]
    tm, tn, tk = min(tm, M), min(tn, N), min(tk, K)
    assert M % tm == 0 and N % tn == 0 and K % tk == 0, (name, M, N, K)
    nk = K // tk
    dn = {"nn": NN, "nt": NT, "tn": TN}[mode]
    if mode == "tn":
        a_spec = pl.BlockSpec((tk, tm), lambda i, j, k: (k, i))
    else:
        a_spec = pl.BlockSpec((tm, tk), lambda i, j, k: (i, k))
    if mode == "nt":
        b_spec = pl.BlockSpec((tn, tk), lambda i, j, k: (j, k))
    else:
        b_spec = pl.BlockSpec((tk, tn), lambda i, j, k: (k, j))
    o_spec = pl.BlockSpec((tm, tn), lambda i, j, k: (i, j))

    def kern(a_ref, b_ref, o_ref, *rest):
        part = lax.dot_general(a_ref[...].astype(BF16), b_ref[...].astype(BF16), dn,
                               preferred_element_type=F32)
        if nk == 1:
            o_ref[...] = part.astype(out_dtype)
        else:
            acc_ref = rest[0]
            k = pl.program_id(2)

            @pl.when(k == 0)
            def _():
                acc_ref[...] = part

            @pl.when(k > 0)
            def _():
                acc_ref[...] += part

            @pl.when(k == nk - 1)
            def _():
                o_ref[...] = acc_ref[...].astype(out_dtype)

    scratch = [pltpu.VMEM((tm, tn), F32)] if nk > 1 else []
    return _pcall(kern, name=name, out_shape=jax.ShapeDtypeStruct((M, N), out_dtype),
                  grid=(M // tm, N // tn, nk), in_specs=[a_spec, b_spec], out_specs=o_spec, scratch=scratch,
                  dims=("parallel", "parallel", "arbitrary"), vmem_mb=vmem_mb, comm=comm)(a, b)


def _dproj_times_w(d_mla, d_mix, wt, add, add_scale, *, name, comm=None):
    S = d_mla.shape[0]
    Dm = wt.shape[1]
    tm, tn, tk = min(1024, S), 1024, W_MLA
    nk = NP // tk

    def kern(a1_ref, a2_ref, b_ref, add_ref, o_ref, acc_ref):
        k = pl.program_id(2)

        @pl.when(k == 0)
        def _():
            acc_ref[...] = jnp.dot(a1_ref[...], b_ref[...], preferred_element_type=F32)

        @pl.when(k > 0)
        def _():
            acc_ref[...] += jnp.dot(a2_ref[...], b_ref[...], preferred_element_type=F32)

        @pl.when(k == nk - 1)
        def _():
            o_ref[...] = add_scale * add_ref[...] + acc_ref[...]

    o_spec = pl.BlockSpec((tm, tn), lambda i, j, k: (i, j))
    return _pcall(kern, name=name, out_shape=jax.ShapeDtypeStruct((S, Dm), F32), grid=(S // tm, Dm // tn, nk),
                  in_specs=[pl.BlockSpec((tm, tk), lambda i, j, k: (i, 0)),
                            pl.BlockSpec((tm, tk), lambda i, j, k: (i, jnp.maximum(k - 1, 0))),
                            pl.BlockSpec((tk, tn), lambda i, j, k: (k, j)), o_spec],
                  out_specs=o_spec, scratch=[pltpu.VMEM((tm, tn), F32)],
                  dims=("parallel", "parallel", "arbitrary"), vmem_mb=48, comm=comm)(d_mla, d_mix, wt, add)


def _dproj_t_times_h(d_mla, d_mix, h, *, name, comm=None):
    S, Dm = h.shape
    tm, tn, tk = W_MLA, 1024, min(2048, S)
    nk = S // tk

    def kern(a1_ref, a2_ref, b_ref, o_ref, acc_ref):
        i = pl.program_id(0)
        k = pl.program_id(2)
        b = b_ref[...].astype(BF16)

        def accumulate(part):
            @pl.when(k == 0)
            def _():
                acc_ref[...] = part

            @pl.when(k > 0)
            def _():
                acc_ref[...] += part

        @pl.when(i == 0)
        def _():
            accumulate(lax.dot_general(a1_ref[...], b, TN, preferred_element_type=F32))

        @pl.when(i > 0)
        def _():
            accumulate(lax.dot_general(a2_ref[...], b, TN, preferred_element_type=F32))

        @pl.when(k == nk - 1)
        def _():
            o_ref[...] = acc_ref[...]

    return _pcall(kern, name=name, out_shape=jax.ShapeDtypeStruct((NP, Dm), F32), grid=(NP // tm, Dm // tn, nk),
                  in_specs=[pl.BlockSpec((tk, tm), lambda i, j, k: (jnp.where(i == 0, k, nk - 1), 0)),
                            pl.BlockSpec((tk, tm), lambda i, j, k: (jnp.where(i == 0, 0, k), jnp.maximum(i - 1, 0))),
                            pl.BlockSpec((tk, tn), lambda i, j, k: (k, j))],
                  out_specs=pl.BlockSpec((tm, tn), lambda i, j, k: (i, j)), scratch=[pltpu.VMEM((tm, tn), F32)],
                  dims=("parallel", "parallel", "arbitrary"), vmem_mb=48, comm=comm)(d_mla, d_mix, h)


def _ln_fwd(x, g, b, *, name):
    S, Dm = x.shape
    tm = min(512, S)

    def kern(x_ref, g_ref, b_ref, y_ref, yb_ref):
        xf = x_ref[...]
        mu = jnp.mean(xf, axis=-1, keepdims=True)
        xc = xf - mu
        var = jnp.mean(xc * xc, axis=-1, keepdims=True)
        y = xc * lax.rsqrt(var + LN_EPS) * g_ref[...] + b_ref[...]
        y_ref[...] = y
        yb_ref[...] = y.astype(BF16)

    row = pl.BlockSpec((tm, Dm), lambda i: (i, 0))
    vec = pl.BlockSpec((1, Dm), lambda i: (0, 0))
    return _pcall(kern, name=name,
                  out_shape=(jax.ShapeDtypeStruct((S, Dm), F32), jax.ShapeDtypeStruct((S, Dm), BF16)),
                  grid=(S // tm,), in_specs=[row, vec, vec], out_specs=(row, row), dims=("parallel",), vmem_mb=48)(
                      x, g.reshape(1, Dm), b.reshape(1, Dm))


def _ln_bwd(dy, r, g, *, name):
    S, Dm = r.shape
    tm = min(512, S)

    def kern(dy_ref, r_ref, g_ref, dr_ref, drb_ref, dg_ref, db_ref, ds_ref):
        @pl.when(pl.program_id(0) == 0)
        def _():
            dg_ref[...] = jnp.zeros_like(dg_ref)
            db_ref[...] = jnp.zeros_like(db_ref)
            ds_ref[...] = jnp.zeros_like(ds_ref)

        rf = r_ref[...]
        dyf = dy_ref[...]
        mu = jnp.mean(rf, axis=-1, keepdims=True)
        xc = rf - mu
        var = jnp.mean(xc * xc, axis=-1, keepdims=True)
        rstd = lax.rsqrt(var + LN_EPS)
        xhat = xc * rstd
        dxh = dyf * g_ref[...]
        c1 = jnp.mean(dxh, axis=-1, keepdims=True)
        c2 = jnp.mean(dxh * xhat, axis=-1, keepdims=True)
        dr = rstd * (dxh - c1 - xhat * c2)
        dr_ref[...] = dr
        drb_ref[...] = dr.astype(BF16)
        dg_ref[...] += jnp.sum(dyf * xhat, axis=0, keepdims=True)
        db_ref[...] += jnp.sum(dyf, axis=0, keepdims=True)
        ds_ref[...] += jnp.sum(dr, axis=0, keepdims=True)

    row = pl.BlockSpec((tm, Dm), lambda i: (i, 0))
    vec = pl.BlockSpec((1, Dm), lambda i: (0, 0))
    vshape = jax.ShapeDtypeStruct((1, Dm), F32)
    return _pcall(kern, name=name,
                  out_shape=(jax.ShapeDtypeStruct((S, Dm), F32), jax.ShapeDtypeStruct((S, Dm), BF16),
                             vshape, vshape, vshape),
                  grid=(S // tm,), in_specs=[row, row, vec], out_specs=(row, row, vec, vec, vec),
                  dims=("arbitrary",), vmem_mb=48)(dy, r, g.reshape(1, Dm))


def _loss_ln_bwd(y, target, r, g, *, name):
    S, Dm = r.shape
    tm = min(512, S)

    def kern(y_ref, t_ref, r_ref, g_ref, l_ref, dr_ref, drb_ref, dg_ref, db_ref, ds_ref):
        @pl.when(pl.program_id(0) == 0)
        def _():
            l_ref[...] = jnp.zeros_like(l_ref)
            dg_ref[...] = jnp.zeros_like(dg_ref)
            db_ref[...] = jnp.zeros_like(db_ref)
            ds_ref[...] = jnp.zeros_like(ds_ref)

        e = y_ref[...] - t_ref[...]
        dyf = e / float(Dm)
        per_row = jnp.mean(e * e, axis=-1, keepdims=True)
        l_ref[...] += 0.5 * jnp.sum(per_row, axis=0, keepdims=True)
        rf = r_ref[...]
        mu = jnp.mean(rf, axis=-1, keepdims=True)
        xc = rf - mu
        var = jnp.mean(xc * xc, axis=-1, keepdims=True)
        rstd = lax.rsqrt(var + LN_EPS)
        xhat = xc * rstd
        dxh = dyf * g_ref[...]
        c1 = jnp.mean(dxh, axis=-1, keepdims=True)
        c2 = jnp.mean(dxh * xhat, axis=-1, keepdims=True)
        dr = rstd * (dxh - c1 - xhat * c2)
        dr_ref[...] = dr
        drb_ref[...] = dr.astype(BF16)
        dg_ref[...] += jnp.sum(dyf * xhat, axis=0, keepdims=True)
        db_ref[...] += jnp.sum(dyf, axis=0, keepdims=True)
        ds_ref[...] += jnp.sum(dr, axis=0, keepdims=True)

    row = pl.BlockSpec((tm, Dm), lambda i: (i, 0))
    vec = pl.BlockSpec((1, Dm), lambda i: (0, 0))
    acc = pl.BlockSpec((8, LANE), lambda i: (0, 0))
    vshape = jax.ShapeDtypeStruct((1, Dm), F32)
    return _pcall(kern, name=name,
                  out_shape=(jax.ShapeDtypeStruct((8, LANE), F32), jax.ShapeDtypeStruct((S, Dm), F32),
                             jax.ShapeDtypeStruct((S, Dm), BF16), vshape, vshape, vshape),
                  grid=(S // tm,), in_specs=[row, row, row, vec], out_specs=(acc, row, row, vec, vec, vec),
                  dims=("arbitrary",), vmem_mb=56)(y, target, r, g.reshape(1, Dm))


def _rot_sum(t):
    return pltpu.roll(t, 32, 1) + pltpu.roll(t, 96, 1)


def _mla_qkv(proj, cos_t, sin_t, qg, kvg, wuq_t, wukv_t, *, name):
    S = proj.shape[0]
    tm = min(256, S)

    def kern(ql_ref, kvl_ref, kr_ref, cos_ref, sin_ref, qg_ref, kvg_ref, wuq_ref, wukv_ref,
             qc_ref, kc_ref, v_ref, vt_ref, qn_ref, kvn_ref):
        cosv = cos_ref[...]
        sinv = sin_ref[...]

        def rope(t):
            return t * cosv + _rot_sum(t) * sinv

        ql = ql_ref[...]
        qn = (ql * lax.rsqrt(jnp.mean(ql * ql, axis=-1, keepdims=True) + RMS_EPS) * qg_ref[...]).astype(BF16)
        kvl = kvl_ref[...]
        kvn = (kvl * lax.rsqrt(jnp.mean(kvl * kvl, axis=-1, keepdims=True) + RMS_EPS) * kvg_ref[...]).astype(BF16)
        qn_ref[...] = qn
        kvn_ref[...] = kvn
        q = lax.dot_general(qn, wuq_ref[...], NT, preferred_element_type=F32)
        kv = lax.dot_general(kvn, wukv_ref[...], NT, preferred_element_type=F32)
        kr = rope(kr_ref[...]).astype(BF16)
        for h in range(N_HEADS):
            c0 = 256 * h
            qc_ref[:, c0:c0 + 128] = q[:, c0:c0 + 128].astype(BF16)
            qc_ref[:, c0 + 128:c0 + 256] = rope(q[:, c0 + 128:c0 + 256]).astype(BF16)
            kc_ref[:, c0:c0 + 128] = kv[:, c0:c0 + 128].astype(BF16)
            kc_ref[:, c0 + 128:c0 + 256] = kr
            vh = kv[:, c0 + 128:c0 + 256]
            v_ref[:, 128 * h:128 * h + 128] = vh.astype(BF16)
            vt_ref[h] = jnp.transpose(vh).astype(BF16)

    def row(w, blk):
        return pl.BlockSpec((tm, w), lambda i: (i, blk))

    def full(shape):
        return pl.BlockSpec(shape, lambda i: (0,) * len(shape))

    t = min(TQ, S)
    per = t // tm
    vt_spec = pl.BlockSpec((N_HEADS, None, 128, tm), lambda i: (0, i // per, 0, i % per))
    outs = (jax.ShapeDtypeStruct((S, 2048), BF16), jax.ShapeDtypeStruct((S, 2048), BF16),
            jax.ShapeDtypeStruct((S, 1024), BF16), jax.ShapeDtypeStruct((N_HEADS, S // t, 128, t), BF16),
            jax.ShapeDtypeStruct((S, Q_LORA), BF16), jax.ShapeDtypeStruct((S, KV_LORA), BF16))
    return _pcall(kern, name=name, out_shape=outs, grid=(S // tm,),
                  in_specs=[row(512, 0), row(256, 2), row(128, 6), row(128, 0), row(128, 0),
                            full((1, Q_LORA)), full((1, KV_LORA)), full((2048, Q_LORA)), full((2048, KV_LORA))],
                  out_specs=(row(2048, 0), row(2048, 0), row(1024, 0), vt_spec, row(512, 0), row(256, 0)),
                  dims=("parallel",), vmem_mb=48)(
                      proj, proj, proj, cos_t, sin_t, qg.reshape(1, -1), kvg.reshape(1, -1), wuq_t, wukv_t)


def _mla_qkv_bwd(dqc, dkc, dv, proj, cos_t, sin_t, qg, kvg, wuq_t, wukv_t, *, name):
    S = proj.shape[0]
    tm = min(256, S)

    def kern(dq_ref, dk_ref, dv_ref, ql_ref, kvl_ref, cos_ref, sin_ref, qg_ref, kvg_ref, wuq_ref, wukv_ref,
             dqb_ref, dkvb_ref, dml_ref, dqg_ref, dkvg_ref):
        @pl.when(pl.program_id(0) == 0)
        def _():
            dqg_ref[...] = jnp.zeros_like(dqg_ref)
            dkvg_ref[...] = jnp.zeros_like(dkvg_ref)

        cosv = cos_ref[...]
        sinv = sin_ref[...]

        def unrope(t):
            return t * cosv - _rot_sum(t) * sinv

        dkr = jnp.zeros((tm, 128), F32)
        for h in range(N_HEADS):
            c0 = 256 * h
            dqb_ref[:, c0:c0 + 128] = dq_ref[:, c0:c0 + 128].astype(BF16)
            dqb_ref[:, c0 + 128:c0 + 256] = unrope(dq_ref[:, c0 + 128:c0 + 256]).astype(BF16)
            dkvb_ref[:, c0:c0 + 128] = dk_ref[:, c0:c0 + 128].astype(BF16)
            dkvb_ref[:, c0 + 128:c0 + 256] = dv_ref[:, 128 * h:128 * h + 128].astype(BF16)
            dkr = dkr + dk_ref[:, c0 + 128:c0 + 256]

        def rms_bwd(x, g, dy):
            n = x.shape[-1]
            rs = lax.rsqrt(jnp.mean(x * x, axis=-1, keepdims=True) + RMS_EPS)
            dyg = dy * g
            dx = rs * dyg - x * (rs * rs * rs) * (jnp.sum(dyg * x, axis=-1, keepdims=True) / n)
            return dx, jnp.sum(dy * (x * rs), axis=0, keepdims=True)

        dqn = jnp.dot(dqb_ref[...], wuq_ref[...], preferred_element_type=F32)
        dql, dqg = rms_bwd(ql_ref[...], qg_ref[...], dqn)
        dqg_ref[...] += dqg
        dkvn = jnp.dot(dkvb_ref[...], wukv_ref[...], preferred_element_type=F32)
        dkvl, dkvg = rms_bwd(kvl_ref[...], kvg_ref[...], dkvn)
        dkvg_ref[...] += dkvg
        dml_ref[:, 0:512] = dql.astype(BF16)
        dml_ref[:, 512:768] = dkvl.astype(BF16)
        dml_ref[:, 768:896] = unrope(dkr).astype(BF16)
        dml_ref[:, 896:1024] = jnp.zeros((tm, 128), BF16)

    def row(w, blk):
        return pl.BlockSpec((tm, w), lambda i: (i, blk))

    def full(shape):
        return pl.BlockSpec(shape, lambda i: (0,) * len(shape))

    outs = (jax.ShapeDtypeStruct((S, 2048), BF16), jax.ShapeDtypeStruct((S, 2048), BF16),
            jax.ShapeDtypeStruct((S, W_MLA), BF16), jax.ShapeDtypeStruct((1, Q_LORA), F32),
            jax.ShapeDtypeStruct((1, KV_LORA), F32))
    return _pcall(kern, name=name, out_shape=outs, grid=(S // tm,),
                  in_specs=[row(2048, 0), row(2048, 0), row(1024, 0), row(512, 0), row(256, 2),
                            row(128, 0), row(128, 0), full((1, Q_LORA)), full((1, KV_LORA)),
                            full((2048, Q_LORA)), full((2048, KV_LORA))],
                  out_specs=(row(2048, 0), row(2048, 0), row(W_MLA, 0), full((1, Q_LORA)), full((1, KV_LORA))),
                  dims=("arbitrary",), vmem_mb=56)(
                      dqc, dkc, dv, proj, proj, cos_t, sin_t, qg.reshape(1, -1), kvg.reshape(1, -1), wuq_t, wukv_t)


def _kq_mask(t):
    krow = lax.broadcasted_iota(jnp.int32, (t, t), 0)
    qcol = lax.broadcasted_iota(jnp.int32, (t, t), 1)
    return krow <= qcol


def _flash_fwd(qc, kc, vt, *, name, comm=None):
    S = qc.shape[0]
    t = min(TQ, S)
    n = S // t

    def kern(q_ref, k_ref, vt_ref, o_ref, lse_ref, m_s, l_s, acc_s):
        qi = pl.program_id(1)
        m_s[...] = jnp.full_like(m_s, -jnp.inf)
        l_s[...] = jnp.zeros_like(l_s)
        acc_s[...] = jnp.zeros_like(acc_s)

        def scores(kb):
            k0 = pl.multiple_of(kb * t, t)
            return lax.dot_general(k_ref[pl.ds(k0, t), :], q_ref[...], NT, preferred_element_type=F32)

        def update(kb, st, masked):
            if masked:
                st = jnp.where(_kq_mask(t), st, -jnp.inf)
            m_prev = m_s[...]
            m_new = jnp.maximum(m_prev, jnp.max(st, axis=0, keepdims=True))
            a = jnp.exp2((m_prev - m_new) * SCALE_LOG2E)
            pt = jnp.exp2((st - m_new) * SCALE_LOG2E)
            l_s[...] = a * l_s[...] + jnp.sum(pt, axis=0, keepdims=True)
            acc_s[...] = a * acc_s[...] + jnp.dot(vt_ref[kb], pt.astype(BF16), preferred_element_type=F32)
            m_s[...] = m_new

        def pair(kb, second_masked):
            s0, s1 = scores(kb), scores(kb + 1)
            update(kb, s0, False)
            update(kb + 1, s1, second_masked)

        def body(i, carry):
            pair(2 * i, False)
            return carry

        lax.fori_loop(0, qi // 2, body, 0)

        @pl.when(qi % 2 == 1)
        def _():
            pair(qi - 1, True)

        @pl.when(qi % 2 == 0)
        def _():
            update(qi, scores(qi), True)
        o_ref[...] = jnp.transpose(acc_s[...] / l_s[...])
        lse_ref[pl.ds(qi, 1), :] = m_s[...] * SCALE_LOG2E + jnp.log2(l_s[...])

    q_spec = pl.BlockSpec((t, 256), lambda h, qi: (qi, h))
    k_spec = pl.BlockSpec((S, 256), lambda h, qi: (0, h))
    vt_spec = pl.BlockSpec((None, n, 128, t), lambda h, qi: (h, 0, 0, 0))
    o_spec = pl.BlockSpec((t, 128), lambda h, qi: (qi, h))
    lse_spec = pl.BlockSpec((None, n, t), lambda h, qi: (h, 0, 0))
    return _pcall(kern, name=name,
                  out_shape=(jax.ShapeDtypeStruct((S, D_MLA), F32), jax.ShapeDtypeStruct((N_HEADS, n, t), F32)),
                  grid=(N_HEADS, n), in_specs=[q_spec, k_spec, vt_spec], out_specs=(o_spec, lse_spec),
                  scratch=[pltpu.VMEM((1, t), F32), pltpu.VMEM((1, t), F32), pltpu.VMEM((128, t), F32)],
                  dims=("parallel", "arbitrary"), vmem_mb=48, comm=comm)(qc, kc, vt)


def _attn_delta(o, do, *, name):
    S = o.shape[0]
    t = min(TQ, S)
    n = S // t

    def kern(o_ref, do_ref, dl_ref):
        i = pl.program_id(0)
        prod = o_ref[...] * do_ref[...]
        lane = lax.broadcasted_iota(jnp.int32, (t, LANE), 1)
        dmat = jnp.zeros((t, LANE), F32)
        for h in range(N_HEADS):
            dmat = jnp.where(lane == h, jnp.sum(prod[:, 128 * h:128 * h + 128], axis=1, keepdims=True), dmat)
        dmat_t = jnp.transpose(dmat)
        for h in range(N_HEADS):
            dl_ref[h, pl.ds(i, 1), :] = dmat_t[h:h + 1, :]

    row = pl.BlockSpec((t, D_MLA), lambda i: (i, 0))
    return _pcall(kern, name=name, out_shape=jax.ShapeDtypeStruct((N_HEADS, n, t), F32), grid=(n,),
                  in_specs=[row, row], out_specs=pl.BlockSpec((N_HEADS, n, t), lambda i: (0, 0, 0)),
                  dims=("arbitrary",), vmem_mb=48)(o, do)


def _flash_bwd(qc, kc, v, do, lse2, delta, *, name, comm=None):
    S = qc.shape[0]
    t = min(TQ, S)
    n = S // t

    def kern(q_ref, k_ref, v_ref, do_ref, lse_ref, dl_ref, dq_ref, dk_ref, dv_ref):
        ki = pl.program_id(1)

        @pl.when(ki == 0)
        def _():
            dq_ref[...] = jnp.zeros_like(dq_ref)

        dk_ref[...] = jnp.zeros_like(dk_ref)
        dv_ref[...] = jnp.zeros_like(dv_ref)

        def step(qb, masked):
            q0 = pl.multiple_of(qb * t, t)
            kt = k_ref[...]
            qblk = q_ref[pl.ds(q0, t), :]
            dob = do_ref[pl.ds(q0, t), :].astype(BF16)
            st = lax.dot_general(kt, qblk, NT, preferred_element_type=F32)
            pt = jnp.exp2(st * SCALE_LOG2E - lse_ref[pl.ds(qb, 1), :])
            if masked:
                pt = jnp.where(_kq_mask(t), pt, 0.0)
            dv_ref[...] += jnp.dot(pt.astype(BF16), dob, preferred_element_type=F32)
            dpt = lax.dot_general(v_ref[...], dob, NT, preferred_element_type=F32)
            dst = (pt * (dpt - dl_ref[pl.ds(qb, 1), :]) * SCALE).astype(BF16)
            dk_ref[...] += jnp.dot(dst, qblk, preferred_element_type=F32)
            dq_ref[pl.ds(q0, t), :] += lax.dot_general(dst, kt, TN, preferred_element_type=F32)

        step(ki, True)
        rest = n - 1 - ki

        def body(i, carry):
            step(ki + 1 + 2 * i, False)
            step(ki + 2 + 2 * i, False)
            return carry

        lax.fori_loop(0, rest // 2, body, 0)

        @pl.when(rest % 2 == 1)
        def _():
            step(n - 1, False)

    def whole(w):
        return pl.BlockSpec((S, w), lambda h, ki: (0, h))

    def krow(w):
        return pl.BlockSpec((t, w), lambda h, ki: (ki, h))

    stat = pl.BlockSpec((None, n, t), lambda h, ki: (h, 0, 0))
    return _pcall(kern, name=name,
                  out_shape=(jax.ShapeDtypeStruct((S, 2048), F32), jax.ShapeDtypeStruct((S, 2048), F32),
                             jax.ShapeDtypeStruct((S, D_MLA), F32)),
                  grid=(N_HEADS, n),
                  in_specs=[whole(256), krow(256), krow(128), whole(128), stat, stat],
                  out_specs=(whole(256), krow(256), krow(128)),
                  dims=("parallel", "arbitrary"), vmem_mb=56, comm=comm)(qc, kc, v, do, lse2, delta)


def _mixer_specs(S, tm):
    hb = tm // HALO
    last_hb = S // HALO - 1

    def main(w, blk):
        return pl.BlockSpec((tm, w), lambda i: (i, blk))

    def prev(w, blk):
        return pl.BlockSpec((HALO, w), lambda i: (jnp.maximum(i * hb - 1, 0), blk))

    def nxt(w, blk):
        return pl.BlockSpec((HALO, w), lambda i: (jnp.minimum((i + 1) * hb, last_hb), blk))

    def full(shape):
        return pl.BlockSpec(shape, lambda i: (0,) * len(shape))

    return main, prev, nxt, full


def _fill_halo(i, xp, xu, hp_ref, hch_ref, hcc_ref, pin_ref, ch_ref, cc_ref, tm):
    first = i == 0
    xp[0:HALO, :] = jnp.where(first, 0.0, hp_ref[...])
    xp[HALO:HALO + tm, :] = pin_ref[...]
    xu[0:HALO, :] = jnp.where(first, 0.0, hch_ref[...] * hcc_ref[...])
    xu[HALO:HALO + tm, :] = cc_ref[...] * ch_ref[...]


def _pooled(xp, g, t1, tm):
    w = POOL_WINDOWS[g]
    lanes = slice(128 * g, 128 * g + 128)
    x0 = xp[HALO:HALO + tm, lanes]
    acc = x0
    for k in range(1, w):
        acc = acc + xp[HALO - k:HALO - k + tm, lanes]
    return acc / jnp.minimum(t1, float(w)) - x0


def _conv_fwd(xu, cw_ref, tm):
    return (cw_ref[0:1, :] * xu[HALO - 2:HALO - 2 + tm, :] + cw_ref[1:2, :] * xu[HALO - 1:HALO - 1 + tm, :]
            + cw_ref[2:3, :] * xu[HALO:HALO + tm, :])


def _mixer_fwd(proj, o, wpool, ps, convw, *, name):
    S = proj.shape[0]
    tm = min(256, S)
    main, prev, _, full = _mixer_specs(S, tm)

    def kern(gm_ref, pin_ref, gp_ref, ch_ref, cb_ref, cc_ref, gc_ref, hp_ref, hch_ref, hcc_ref,
             o_ref, wp_ref, ps_ref, cw_ref, mix_ref, xp, xu):
        i = pl.program_id(0)
        _fill_halo(i, xp, xu, hp_ref, hch_ref, hcc_ref, pin_ref, ch_ref, cc_ref, tm)
        t1 = (i * tm + lax.broadcasted_iota(jnp.int32, (tm, 1), 0) + 1).astype(F32)
        for g in range(4):
            lanes = slice(128 * g, 128 * g + 128)
            pooled = _pooled(xp, g, t1, tm)
            z = jnp.dot(pooled.astype(BF16), wp_ref[g].astype(BF16), preferred_element_type=F32)
            gp = gp_ref[:, lanes]
            y = z * ps_ref[:, lanes] * (gp * _sigmoid(gp))
            mix_ref[:, 1024 + 128 * g:1024 + 128 * g + 128] = y.astype(BF16)
        gc = gc_ref[...]
        mix_ref[:, 1536:2048] = (cb_ref[...] * _conv_fwd(xu, cw_ref, tm) * (gc * _sigmoid(gc))).astype(BF16)
        gm = gm_ref[...]
        mix_ref[:, 0:1024] = (o_ref[...] * (gm * _sigmoid(gm))).astype(BF16)

    return _pcall(kern, name=name, out_shape=jax.ShapeDtypeStruct((S, 2048), BF16), grid=(S // tm,),
                  in_specs=[main(1024, 1), main(512, 4), main(512, 5), main(512, 6), main(512, 7), main(512, 8),
                            main(512, 9), prev(512, 4), prev(512, 6), prev(512, 8),
                            main(1024, 0), full((4, 128, 128)), full((1, 512)), full((3, 512))],
                  out_specs=main(2048, 0),
                  scratch=[pltpu.VMEM((tm + HALO, 512), F32), pltpu.VMEM((tm + HALO, 512), F32)],
                  dims=("parallel",), vmem_mb=48)(
                      proj, proj, proj, proj, proj, proj, proj, proj, proj, proj, o, wpool, ps.reshape(1, 512), convw)


def _mixer_bwd(dmix, proj, o, wpool, ps, convw, *, name):
    S = proj.shape[0]
    tm = min(256, S)
    n = S // tm
    main, prev, nxt, full = _mixer_specs(S, tm)

    def kern(dm_ref, dmn_ref, gm_ref, pin_ref, gp_ref, ch_ref, cb_ref, cc_ref, gc_ref,
             hp_ref, hch_ref, hcc_ref, gpn_ref, cbn_ref, gcn_ref, o_ref, wp_ref, ps_ref, cw_ref,
             d_ref, do_ref, dwp_ref, dps_ref, dcw_ref, xp, xu, ee, ed):
        i = pl.program_id(0)
        last = i == n - 1

        @pl.when(i == 0)
        def _():
            dwp_ref[...] = jnp.zeros_like(dwp_ref)
            dps_ref[...] = jnp.zeros_like(dps_ref)
            dcw_ref[...] = jnp.zeros_like(dcw_ref)

        _fill_halo(i, xp, xu, hp_ref, hch_ref, hcc_ref, pin_ref, ch_ref, cc_ref, tm)
        t1 = (i * tm + lax.broadcasted_iota(jnp.int32, (tm, 1), 0) + 1).astype(F32)
        t1n = ((i + 1) * tm + lax.broadcasted_iota(jnp.int32, (HALO, 1), 0) + 1).astype(F32)
        c_pin, c_gp, c_ch, c_cb, c_cc, c_gc = 1024, 1536, 2048, 2560, 3072, 3584

        for g in range(4):
            w = float(POOL_WINDOWS[g])
            lanes = slice(128 * g, 128 * g + 128)
            pooled = _pooled(xp, g, t1, tm)
            pb = pooled.astype(BF16)
            wp = wp_ref[g].astype(BF16)
            z = jnp.dot(pb, wp, preferred_element_type=F32)
            psl = ps_ref[:, lanes]
            sg, dsg = _silu_and_grad(gp_ref[:, lanes])
            dmp = dm_ref[:, 1024 + 128 * g:1024 + 128 * g + 128]
            dyp = dmp * sg
            d_ref[:, c_gp + 128 * g:c_gp + 128 * g + 128] = (dmp * (z * psl) * dsg).astype(BF16)
            dps_ref[:, lanes] += jnp.sum(dyp * z, axis=0, keepdims=True)
            dz = (dyp * psl).astype(BF16)
            dwp_ref[g] += lax.dot_general(pb, dz, TN, preferred_element_type=F32)
            dpl = lax.dot_general(dz, wp, NT, preferred_element_type=F32)
            ee[0:tm, lanes] = dpl / jnp.minimum(t1, w)
            gpn = gpn_ref[:, lanes]
            dzn = (dmn_ref[:, lanes] * (gpn * _sigmoid(gpn)) * psl).astype(BF16)
            dpn = lax.dot_general(dzn, wp, NT, preferred_element_type=F32)
            ee[tm:tm + HALO, lanes] = jnp.where(last, 0.0, dpn / jnp.minimum(t1n, w))
            acc = ee[0:tm, lanes]
            for k in range(1, POOL_WINDOWS[g]):
                acc = acc + ee[k:k + tm, lanes]
            d_ref[:, c_pin + 128 * g:c_pin + 128 * g + 128] = (acc - dpl).astype(BF16)

        yc = _conv_fwd(xu, cw_ref, tm)
        sgc, dsgc = _silu_and_grad(gc_ref[...])
        cb = cb_ref[...]
        dmc = dm_ref[:, 1536:2048]
        d_ref[:, c_gc:c_gc + 512] = (dmc * cb * yc * dsgc).astype(BF16)
        d_ref[:, c_cb:c_cb + 512] = (dmc * yc * sgc).astype(BF16)
        dyc = dmc * cb * sgc
        ed[0:tm, :] = dyc
        gcn = gcn_ref[...]
        ed[tm:tm + HALO, :] = jnp.where(last, 0.0, dmn_ref[:, 512:1024] * cbn_ref[...] * (gcn * _sigmoid(gcn)))
        dcw_ref[0:1, :] += jnp.sum(dyc * xu[HALO - 2:HALO - 2 + tm, :], axis=0, keepdims=True)
        dcw_ref[1:2, :] += jnp.sum(dyc * xu[HALO - 1:HALO - 1 + tm, :], axis=0, keepdims=True)
        dcw_ref[2:3, :] += jnp.sum(dyc * xu[HALO:HALO + tm, :], axis=0, keepdims=True)
        du = cw_ref[2:3, :] * dyc + cw_ref[1:2, :] * ed[1:1 + tm, :] + cw_ref[0:1, :] * ed[2:2 + tm, :]
        d_ref[:, c_cc:c_cc + 512] = (du * ch_ref[...]).astype(BF16)
        d_ref[:, c_ch:c_ch + 512] = (du * cc_ref[...]).astype(BF16)

        sgm, dsgm = _silu_and_grad(gm_ref[...])
        dmm = dm_ref[:, 0:1024]
        do_ref[...] = dmm * sgm
        d_ref[:, 0:1024] = (dmm * o_ref[...] * dsgm).astype(BF16)

    outs = (jax.ShapeDtypeStruct((S, W_MIX), BF16), jax.ShapeDtypeStruct((S, 1024), F32),
            jax.ShapeDtypeStruct((4, 128, 128), F32), jax.ShapeDtypeStruct((1, 512), F32),
            jax.ShapeDtypeStruct((3, 512), F32))
    scr = [pltpu.VMEM((tm + HALO, 512), F32) for _ in range(4)]
    return _pcall(kern, name=name, out_shape=outs, grid=(n,),
                  in_specs=[main(2048, 0), nxt(1024, 1),
                            main(1024, 1), main(512, 4), main(512, 5), main(512, 6), main(512, 7), main(512, 8),
                            main(512, 9), prev(512, 4), prev(512, 6), prev(512, 8),
                            nxt(512, 5), nxt(512, 7), nxt(512, 9),
                            main(1024, 0), full((4, 128, 128)), full((1, 512)), full((3, 512))],
                  out_specs=(main(W_MIX, 0), main(1024, 0), full((4, 128, 128)), full((1, 512)), full((3, 512))),
                  scratch=scr, dims=("arbitrary",), vmem_mb=56)(
                      dmix, dmix, proj, proj, proj, proj, proj, proj, proj, proj, proj, proj, proj, proj, proj,
                      o, wpool, ps.reshape(1, 512), convw)


def _outproj_ln(mix, wout, h, bout, g, b, *, name):
    S, Dm = h.shape
    tm = min(256, S)

    def kern(mix_ref, w_ref, h_ref, bo_ref, g_ref, b_ref, y_ref, yb_ref, r_ref):
        out = jnp.dot(mix_ref[...], w_ref[...], preferred_element_type=F32) + bo_ref[...]
        r = ALPHA * h_ref[...] + out
        r_ref[...] = r
        mu = jnp.mean(r, axis=-1, keepdims=True)
        xc = r - mu
        var = jnp.mean(xc * xc, axis=-1, keepdims=True)
        y = xc * lax.rsqrt(var + LN_EPS) * g_ref[...] + b_ref[...]
        y_ref[...] = y
        yb_ref[...] = y.astype(BF16)

    row = pl.BlockSpec((tm, Dm), lambda i: (i, 0))
    vec = pl.BlockSpec((1, Dm), lambda i: (0, 0))
    wsp = pl.BlockSpec((Dm, Dm), lambda i: (0, 0))
    sds = jax.ShapeDtypeStruct((S, Dm), F32)
    return _pcall(kern, name=name, out_shape=(sds, jax.ShapeDtypeStruct((S, Dm), BF16), sds), grid=(S // tm,),
                  in_specs=[row, wsp, row, vec, vec, vec], out_specs=(row, row, row), dims=("parallel",),
                  vmem_mb=56)(
                      mix, wout, h, bout.reshape(1, Dm), g.reshape(1, Dm), b.reshape(1, Dm))


def _adamw_math(w, g, m, v):
    m = ADAM_B1 * m + (1.0 - ADAM_B1) * g
    v = ADAM_B2 * v + (1.0 - ADAM_B2) * (g * g)
    m_hat = m / (1.0 - ADAM_B1 ** ADAM_STEP)
    v_hat = v / (1.0 - ADAM_B2 ** ADAM_STEP)
    delta = -ADAM_LR * (m_hat / (jnp.sqrt(v_hat) + ADAM_EPS) + ADAM_WD * w)
    return delta, m, v


def _row_tile(R, C):
    best = None
    for cand in range(8, R, 8):
        if R % cand == 0 and cand * C <= 256 * 1024:
            best = cand
    return best if best is not None else R


def _adamw(w, g, m, v, *, name):
    shape = w.shape
    C = shape[-1]
    R = 1
    for s in shape[:-1]:
        R *= s
    tr = _row_tile(R, C)

    def kern(w_ref, g_ref, m_ref, v_ref, d_ref, mo_ref, vo_ref):
        d, mn, vn = _adamw_math(w_ref[...], g_ref[...], m_ref[...], v_ref[...])
        d_ref[...] = d
        mo_ref[...] = mn
        vo_ref[...] = vn

    blk = pl.BlockSpec((tr, C), lambda i: (i, 0))
    sds = jax.ShapeDtypeStruct((R, C), F32)
    outs = _pcall(kern, name=name, out_shape=(sds, sds, sds), grid=(R // tr,), in_specs=[blk] * 4,
                  out_specs=(blk, blk, blk), dims=("parallel",), vmem_mb=48)(
                      w.reshape(R, C), g.reshape(R, C), m.reshape(R, C), v.reshape(R, C))
    return tuple(t.reshape(shape) for t in outs)


def _adamw_halves(w, m, v, halves, c_idx, *, name):
    _, R, C = w.shape
    ch = C // 2
    tr = _row_tile(R, ch)
    nb = R // tr

    def kern(c_ref, w_ref, a0_ref, b0_ref, a1_ref, b1_ref, m_ref, v_ref, g_ref, d_ref, mo_ref, vo_ref):
        layer = pl.program_id(0) // nb
        mine = pl.program_id(1) == c_ref[0]
        g = jnp.where(layer == 0, jnp.where(mine, a0_ref[...], b0_ref[...]),
                      jnp.where(mine, a1_ref[...], b1_ref[...]))
        g_ref[...] = g
        d, mn, vn = _adamw_math(w_ref[...], g, m_ref[...], v_ref[...])
        d_ref[...] = d
        mo_ref[...] = mn
        vo_ref[...] = vn

    full = pl.BlockSpec((tr, ch), lambda i, hc, c: (i, hc))
    half = pl.BlockSpec((tr, ch), lambda i, hc, c: (i % nb, 0))
    gs = pltpu.PrefetchScalarGridSpec(num_scalar_prefetch=1, grid=(2 * nb, 2),
                                      in_specs=[full, half, half, half, half, full, full], out_specs=(full,) * 4)
    sds = jax.ShapeDtypeStruct((2 * R, C), F32)
    (a0, b0), (a1, b1) = halves
    outs = pl.pallas_call(kern, name=name, out_shape=(sds,) * 4, grid_spec=gs,
                          compiler_params=pltpu.CompilerParams(dimension_semantics=("parallel", "parallel"),
                                                               vmem_limit_bytes=48 << 20))(
                              c_idx, w.reshape(2 * R, C), a0, b0, a1, b1, m.reshape(2 * R, C), v.reshape(2 * R, C))
    return tuple(t.reshape(2, R, C) for t in outs)


def _small_sum_adamw(gathered, w, m, v, *, name):
    R = w.shape[0]

    def kern(ga_ref, w_ref, m_ref, v_ref, g_ref, d_ref, mo_ref, vo_ref):
        g = ga_ref[0]
        for k in range(1, N_DEV):
            g = g + ga_ref[k]
        g_ref[...] = g
        d, mn, vn = _adamw_math(w_ref[...], g, m_ref[...], v_ref[...])
        d_ref[...] = d
        mo_ref[...] = mn
        vo_ref[...] = vn

    sds = jax.ShapeDtypeStruct((R, LANE), F32)
    return _pcall(kern, name=name, out_shape=(sds, sds, sds, sds), vmem_mb=48)(gathered, w, m, v)


def _pair_sum(g, theirs, c_idx, *, name):
    R, C = g.shape
    ch = C // 2
    tr = _row_tile(R, ch)

    def kern(c_ref, a_ref, b_ref, o_ref):
        o_ref[...] = (a_ref[...] + b_ref[...]).astype(BF16)

    gs = pltpu.PrefetchScalarGridSpec(
        num_scalar_prefetch=1, grid=(R // tr,),
        in_specs=[pl.BlockSpec((tr, ch), lambda i, c: (i, c[0])), pl.BlockSpec((tr, ch), lambda i, c: (i, 0))],
        out_specs=pl.BlockSpec((tr, ch), lambda i, c: (i, 0)))
    return pl.pallas_call(kern, name=name, out_shape=jax.ShapeDtypeStruct((R, ch), BF16), grid_spec=gs,
                          compiler_params=pltpu.CompilerParams(dimension_semantics=("parallel",),
                                                               vmem_limit_bytes=48 << 20))(c_idx, g, theirs)


WeightRows = collections.namedtuple("WeightRows", "full_rows own_rows cols pieces zero_rows")


def _w_in_piece_a(j):
    return jnp.where(j == 0, 0, 1232 * j + GAP)


def _w_in_piece_b(j):
    return jnp.where(j == 0, GAP_AT + GAP, 1232 * j + GAP_AT + GAP)


W_IN = WeightRows(NP, 1232, D_MODEL, ((0, GAP_AT, _w_in_piece_a), (GAP_AT, 1232 - GAP_AT, _w_in_piece_b)),
                  ((GAP_AT, GAP),))
W_OUT = WeightRows(2048, 512, D_MODEL, ((0, 512, lambda j: 512 * j),), ())
W_UQ = WeightRows(2048, 384, Q_LORA, ((0, 192, lambda j: 512 * j), (192, 192, lambda j: 512 * j + 256)),
                  tuple((256 * h + 192, 64) for h in range(N_HEADS)))
W_UKV = WeightRows(2048, 512, KV_LORA, ((0, 512, lambda j: 512 * j),), ())
W_CONV = WeightRows(64, 16, 256, ((0, 16, lambda j: 16 * j),), ())
SHARDED = (W_IN, W_OUT, W_UQ, W_UKV)


def _mesh_pos():
    x, y, c = lax.axis_index("x"), lax.axis_index("y"), lax.axis_index("c")
    return x, y, c


def _other_chips(x, y):
    return [(1 - x, y), (x, 1 - y), (1 - x, 1 - y)]


def _rows(start, n):
    return pl.ds(pl.multiple_of(start, 16), n)


def _half_cols(spec, c):
    ch = spec.cols // 2
    return pl.ds(pl.multiple_of(c * ch, LANE), ch)


def _allgather_script(specs, shards, zeros):
    na = len(specs)
    zlist = [a for a in range(na) if zeros[a] is not None]
    plan_first, plan_own, plan_zero = [], [], []
    for a, spec in enumerate(specs):
        for p in range(len(spec.pieces)):
            plan_own.append((a, p))
            for k in range(3):
                plan_first.append((a, p, k))
        for z in range(len(spec.zero_rows)):
            for l in range(shards[a].shape[0]):
                plan_zero.append((a, z, l))
    nf = len(plan_first)
    n_sems = 2 * nf + len(plan_own) + len(plan_zero)

    def copies(ins_all, outs, send_sems, recv_sems):
        ins = ins_all[:na]
        zrefs = dict(zip(zlist, ins_all[na:]))
        x, y, c = _mesh_pos()
        j = 2 * x + y
        chips = _other_chips(x, y)
        sibling = (x, y, 1 - c)

        def remote(src, dst, sem, to):
            return pltpu.make_async_remote_copy(src_ref=src, dst_ref=dst, send_sem=send_sems.at[sem],
                                                recv_sem=recv_sems.at[sem], device_id=to, device_id_type=MESH)

        def block(a, p, chip, cols):
            _, n, dst = specs[a].pieces[p]
            return outs[a].at[:, _rows(dst(chip), n), cols]

        def first(i):
            a, p, k = plan_first[i]
            src0, n, _ = specs[a].pieces[p]
            cols = _half_cols(specs[a], c)
            return remote(ins[a].at[:, pl.ds(src0, n), cols], block(a, p, j, cols), i, (*chips[k], c))

        def landed(i, half):
            a, p, k = plan_first[i]
            return block(a, p, 2 * chips[k][0] + chips[k][1], _half_cols(specs[a], half))

        def arrival(i, half, sem):
            return remote(landed(i, half), landed(i, half), sem, sibling)

        def passed(i):
            return remote(landed(i, c), landed(i, c), nf + i, sibling)

        def own(i):
            a, p = plan_own[i]
            src0, n, _ = specs[a].pieces[p]
            return remote(ins[a].at[:, pl.ds(src0, n), :], block(a, p, j, slice(None)), 2 * nf + i, sibling)

        def zero(i):
            a, z, l = plan_zero[i]
            r0, n = specs[a].zero_rows[z]
            return remote(zrefs[a].at[pl.ds(0, n), :], outs[a].at[l, pl.ds(r0, n), :],
                          2 * nf + len(plan_own) + i, sibling)

        fixed = [own(i) for i in range(len(plan_own))] + [zero(i) for i in range(len(plan_zero))]
        return c, fixed, first, arrival, passed

    def start(ins, outs, send_sems, recv_sems):
        _, fixed, first, _, _ = copies(ins, outs, send_sems, recv_sems)
        for cp in fixed:
            cp.start()
        for i in range(nf):
            first(i).start()

    def finish(ins, outs, send_sems, recv_sems):
        c, fixed, first, arrival, passed = copies(ins, outs, send_sems, recv_sems)
        for i in range(nf):
            arrival(i, c, i).wait_recv()
            passed(i).start()
        for i in range(nf):
            arrival(i, 1 - c, nf + i).wait_recv()
        for cp in fixed:
            cp.wait()
        for i in range(nf):
            first(i).wait_send()
            passed(i).wait_send()

    out_shape = tuple(jax.ShapeDtypeStruct((shards[a].shape[0], spec.full_rows, spec.cols), BF16)
                      for a, spec in enumerate(specs))
    args = tuple(shards) + tuple(zeros[a] for a in zlist)
    return CommScript(args, out_shape, n_sems, start, finish)


def _start_all_wait_all(args, out_shape, n_sems, make_copies):
    def start(ins, outs, send_sems, recv_sems):
        for cp in make_copies(ins, outs, send_sems, recv_sems):
            cp.start()

    def finish(ins, outs, send_sems, recv_sems):
        for cp in make_copies(ins, outs, send_sems, recv_sems):
            cp.wait()

    return CommScript(tuple(args), tuple(out_shape), n_sems, start, finish)


def _exchange_script(specs, grads):
    na = len(grads)

    def make_copies(ins, outs, send_sems, recv_sems):
        x, y, c = _mesh_pos()
        return [pltpu.make_async_remote_copy(
            src_ref=ins[a].at[:, _half_cols(specs[a], 1 - c)], dst_ref=outs[a], send_sem=send_sems.at[a],
            recv_sem=recv_sems.at[a], device_id=(x, y, 1 - c), device_id_type=MESH) for a in range(na)]

    out_shape = [jax.ShapeDtypeStruct((s.full_rows, s.cols // 2), F32) for s in specs]
    return _start_all_wait_all(grads, out_shape, na, make_copies)


def _scatter_script(specs, parts):
    na = len(parts)
    plan = [(a, p, k) for a in range(na) for p in range(len(specs[a].pieces)) for k in range(3)]

    def make_copies(ins, outs, send_sems, recv_sems):
        x, y, c = _mesh_pos()
        chips = _other_chips(x, y)
        copies = []
        for i, (a, p, k) in enumerate(plan):
            src0, n, dst = specs[a].pieces[p]
            pk = 2 * chips[k][0] + chips[k][1]
            copies.append(pltpu.make_async_remote_copy(
                src_ref=ins[a].at[_rows(dst(pk), n), :], dst_ref=outs[a].at[k, pl.ds(src0, n), :],
                send_sem=send_sems.at[i], recv_sem=recv_sems.at[i], device_id=(*chips[k], c), device_id_type=MESH))
        return copies

    out_shape = [jax.ShapeDtypeStruct((3, s.own_rows, s.cols // 2), BF16) for s in specs]
    return _start_all_wait_all(parts, out_shape, len(plan), make_copies)


def _chip_sum(spec, part, recv, *, name):
    ch = spec.cols // 2
    npieces = len(spec.pieces)

    def kern(recv_ref, part_ref, o_ref, own_ref, sems):
        j = 2 * lax.axis_index("x") + lax.axis_index("y")
        copies = []
        for p, (src0, n, dst) in enumerate(spec.pieces):
            copies.append(pltpu.make_async_copy(part_ref.at[_rows(dst(j), n), :], own_ref.at[pl.ds(src0, n), :],
                                                sems.at[p]))
        for cp in copies:
            cp.start()
        for cp in copies:
            cp.wait()
        o_ref[...] = ((own_ref[...].astype(F32) + recv_ref[0].astype(F32)) + recv_ref[1].astype(F32)) \
            + recv_ref[2].astype(F32)

    vm = pl.BlockSpec(memory_space=pltpu.VMEM)
    return _pcall(kern, name=name, out_shape=jax.ShapeDtypeStruct((spec.own_rows, ch), F32),
                  in_specs=[vm, HBM_SPEC], out_specs=vm,
                  scratch=[pltpu.VMEM((spec.own_rows, ch), BF16), pltpu.SemaphoreType.DMA((npieces,))],
                  vmem_mb=48)(recv, part)


def _sibling_script(sums):
    na = len(sums)

    def make_copies(ins, outs, send_sems, recv_sems):
        x, y, c = _mesh_pos()
        return [pltpu.make_async_remote_copy(
            src_ref=ins[a], dst_ref=outs[a], send_sem=send_sems.at[a], recv_sem=recv_sems.at[a],
            device_id=(x, y, 1 - c), device_id_type=MESH) for a in range(na)]

    out_shape = [jax.ShapeDtypeStruct(t.shape, t.dtype) for t in sums]
    return _start_all_wait_all(sums, out_shape, na, make_copies)


class _GradReducer:
    def __init__(self, layer, grads, c_idx):
        self.layer, self.grads, self.c_idx = layer, tuple(grads), c_idx
        self.names = [f"{nm}{layer}" for nm in ("w_in", "w_out", "w_uq", "w_ukv")]

    def exchange(self):
        return _exchange_script(SHARDED, self.grads)

    def scatter(self, theirs):
        self.parts = tuple(_pair_sum(g, th, self.c_idx, name=f"pair_sum_{nm}")
                           for g, th, nm in zip(self.grads, theirs, self.names))
        return _scatter_script(SHARDED, self.parts)

    def sibling(self, recv):
        self.sums = tuple(_chip_sum(s, p, r, name=f"chip_sum_{nm}")
                          for s, p, r, nm in zip(SHARDED, self.parts, recv, self.names))
        return _sibling_script(self.sums)

    def done(self, others):
        return list(zip(self.sums, others))


def _allgather_small(block, *, name):
    m_per, n = block.shape

    def body(x_ref, out_ref, send_sems, recv_sems, local_sem):
        x, y, c = _mesh_pos()
        me, sibling = (x, y, c), (x, y, 1 - c)
        chips = _other_chips(x, y)

        def rows(px, py, pc):
            return out_ref.at[4 * px + 2 * py + pc]

        def copy(k, blk, to, src=None):
            return pltpu.make_async_remote_copy(
                src_ref=rows(*blk) if src is None else src, dst_ref=rows(*blk), send_sem=send_sems.at[k],
                recv_sem=recv_sems.at[k], device_id=to, device_id_type=MESH)

        mine = pltpu.make_async_copy(x_ref, rows(*me), local_sem)
        mine.start()
        first = [copy(0, me, sibling, src=x_ref)]
        first += [copy(1 + k, me, (*chip, c), src=x_ref) for k, chip in enumerate(chips)]
        for cp in first:
            cp.start()
        passed = [copy(4 + k, (*chip, c), sibling) for k, chip in enumerate(chips)]
        for k, chip in enumerate(chips):
            copy(1 + k, (*chip, c), me).wait_recv()
            passed[k].start()
        copy(0, sibling, me).wait_recv()
        for k, chip in enumerate(chips):
            copy(4 + k, (*chip, 1 - c), me).wait_recv()
        for cp in first + passed:
            cp.wait_send()
        mine.wait()

    vm = pl.BlockSpec(memory_space=pltpu.VMEM)
    return _pcall(body, name=name, out_shape=jax.ShapeDtypeStruct((N_DEV, m_per, n), block.dtype),
                  in_specs=[vm], out_specs=vm,
                  scratch=[pltpu.SemaphoreType.DMA((7,)), pltpu.SemaphoreType.DMA((7,)), pltpu.SemaphoreType.DMA],
                  vmem_mb=48)(block)


def _rope_tables(positions):
    half = ROPE // 2
    inv_freq = ROPE_THETA ** (-jnp.arange(half, dtype=F32) / half)
    ang = positions.astype(F32)[:, None] * inv_freq
    cos, sin = jnp.cos(ang), jnp.sin(ang)
    S = positions.shape[0]
    cos_t = jnp.concatenate([cos, cos, jnp.ones((S, 64), F32)], axis=1)
    sin_t = jnp.concatenate([-sin, sin, jnp.zeros((S, 64), F32)], axis=1)
    return cos_t, sin_t


def _decode_conv(bits):
    rows = bits.reshape(DEPTH, N_CHIPS, 16, 256)[:, :, :3, :]
    conv = lax.bitcast_convert_type(rows.reshape(DEPTH, N_CHIPS, 3, 128, 2), F32)
    return jnp.transpose(conv, (0, 2, 1, 3)).reshape(DEPTH, 3, 512)


def _local_step(x, positions, target, emb_g, emb_b, w_in_t0, rest0, weights1, q_g, kv_g, w_pool, pool_scale,
                b_out, ln_g, ln_b, c_idx=None):
    cos_t, sin_t = _rope_tables(positions)
    h, hb = _ln_fwd(x, emb_g, emb_b, name="emb_ln")
    weights = [None, weights1]
    saved = []
    for l in range(DEPTH):
        if l == 0 and isinstance(rest0, CommScript):
            proj, landed = _matmul(hb, w_in_t0, "nt", name="in_proj0", tm=1024, tn=1024, tk=2048, vmem_mb=56,
                                   comm=rest0)
            weights[0] = (w_in_t0,) + tuple(a[0] for a in landed[:3])
            conv_w = _decode_conv(landed[3])
        else:
            if l == 0:
                weights[0] = (w_in_t0,) + tuple(rest0[:3])
                conv_w = rest0[3]
            proj = _matmul(hb, weights[l][0], "nt", name=f"in_proj{l}", tm=1024, tn=1024, tk=2048, vmem_mb=56)
        w_in_t, w_out, w_uq_t, w_ukv_t = weights[l]
        qc, kc, v, vt, qn, kvn = _mla_qkv(proj, cos_t, sin_t, q_g[l], kv_g[l], w_uq_t, w_ukv_t, name=f"mla_qkv{l}")
        nxt = weights[l + 1] if l + 1 < DEPTH else None
        if isinstance(nxt, CommScript):
            (o, lse2), landed = _flash_fwd(qc, kc, vt, name=f"flash_fwd{l}", comm=nxt)
            weights[l + 1] = tuple(a[0] for a in landed)
        else:
            o, lse2 = _flash_fwd(qc, kc, vt, name=f"flash_fwd{l}")
        mix = _mixer_fwd(proj, o, w_pool[l], pool_scale[l], conv_w[l], name=f"mixer_fwd{l}")
        h_next, hb_next, r = _outproj_ln(mix, w_out, h, b_out[l], ln_g[l], ln_b[l], name=f"out_proj_ln{l}")
        saved.append((hb, proj, qc, kc, v, qn, kvn, o, lse2, mix, r))
        h, hb = h_next, hb_next

    y_final = h
    small = [None] * DEPTH
    big = [None] * DEPTH
    above = scatter_above = None
    for l in reversed(range(DEPTH)):
        w_in_t, w_out, w_uq_t, w_ukv_t = weights[l]
        hb_in, proj, qc, kc, v, qn, kvn, o, lse2, mix, r = saved[l]
        if l == DEPTH - 1:
            loss_acc, dr, drb, d_ln_g, d_ln_b, d_b_out = _loss_ln_bwd(y_final, target, r, ln_g[l], name="loss_ln_bwd")
        else:
            dr, drb, d_ln_g, d_ln_b, d_b_out = _ln_bwd(dh, r, ln_g[l], name=f"ln_bwd{l}")
        dmix = _matmul(drb, w_out, "nt", name=f"dmix{l}", tm=1024, tn=1024, tk=2048, vmem_mb=56)
        d_w_out = _matmul(mix, drb, "tn", name=f"dw_out{l}", tm=1024, tn=1024, tk=2048, vmem_mb=56)
        d_mix, do, d_w_pool, d_ps, d_conv = _mixer_bwd(dmix, proj, o, w_pool[l], pool_scale[l], conv_w[l],
                                                       name=f"mixer_bwd{l}")
        delta = _attn_delta(o, do, name=f"attn_delta{l}")
        if above is not None:
            (dqc, dkc, dv), recv = _flash_bwd(qc, kc, v, do, lse2, delta, name=f"flash_bwd{l}", comm=scatter_above)
            sibling_above = above.sibling(recv)
        else:
            dqc, dkc, dv = _flash_bwd(qc, kc, v, do, lse2, delta, name=f"flash_bwd{l}")
        dqb, dkvb, d_mla, d_qg, d_kvg = _mla_qkv_bwd(dqc, dkc, dv, proj, cos_t, sin_t, q_g[l], kv_g[l],
                                                     w_uq_t, w_ukv_t, name=f"mla_qkv_bwd{l}")
        d_w_uq_t = _matmul(dqb, qn, "tn", name=f"dw_uq{l}", tm=2048, tn=512, tk=2048, vmem_mb=56)
        d_w_ukv_t = _matmul(dkvb, kvn, "tn", name=f"dw_ukv{l}", tm=2048, tn=256, tk=2048, vmem_mb=56)
        if above is not None:
            d_w_in_t, others = _dproj_t_times_h(d_mla, d_mix, hb_in, name=f"dw_in{l}", comm=sibling_above)
            big[l + 1] = above.done(others)
            above = None
        else:
            d_w_in_t = _dproj_t_times_h(d_mla, d_mix, hb_in, name=f"dw_in{l}")
        big[l] = (d_w_in_t, d_w_out, d_w_uq_t, d_w_ukv_t)
        small[l] = dict(q_g=d_qg[0], kv_g=d_kvg[0], w_pool=d_w_pool, pool_scale=d_ps[0], conv_w=d_conv,
                        b_out=d_b_out[0], ln_g=d_ln_g[0], ln_b=d_ln_b[0])
        if c_idx is None:
            dh = _dproj_times_w(d_mla, d_mix, w_in_t, dr, ALPHA, name=f"dh{l}")
        elif l > 0:
            above = _GradReducer(l, big[l], c_idx)
            dh, theirs = _dproj_times_w(d_mla, d_mix, w_in_t, dr, ALPHA, name=f"dh{l}", comm=above.exchange())
            scatter_above = above.scatter(theirs)
        else:
            last = _GradReducer(l, big[l], c_idx)
            theirs = _run_comm(last.exchange(), name="exchange_halves0")
            dh, recv = _dproj_times_w(d_mla, d_mix, w_in_t, dr, ALPHA, name=f"dh{l}", comm=last.scatter(theirs))
    grad_x, _, d_emb_g, d_emb_b, _ = _ln_bwd(dh, x, emb_g, name="emb_ln_bwd")
    if c_idx is not None:
        big[0] = last.done(_run_comm(last.sibling(recv), name="send_to_sibling0"))
    return loss_acc[0, 0], grad_x, d_emb_g[0], d_emb_b[0], small, big


SMALL_ORDER = ("emb_ln_g", "emb_ln_b", "q_norm_g", "kv_norm_g", "w_pool", "pool_scale", "b_out", "ln_g", "ln_b")


def _pack_small(arrs, extra_rows):
    flat = jnp.concatenate([a.reshape(-1) for a in arrs])
    rows = flat.shape[0] // LANE
    total = -(-(rows + extra_rows) // 8) * 8
    return jnp.pad(flat, (0, total * LANE - flat.shape[0])).reshape(total, LANE)


def _unpack_small(packed, shapes):
    flat = packed.reshape(-1)
    out, off = [], 0
    for shp in shapes:
        n = 1
        for s in shp:
            n *= s
        out.append(flat[off:off + n].reshape(shp))
        off += n
    return out, off


def kernel(x, positions, emb_ln_g, emb_ln_b, w_in, q_norm_g, kv_norm_g, w_uq, w_ukv, w_pool, pool_scale, conv_w, w_out, b_out, ln_g, ln_b, loss_target, m_emb_ln_g, m_emb_ln_b, m_w_in, m_q_norm_g, m_kv_norm_g, m_w_uq, m_w_ukv, m_w_pool, m_pool_scale, m_conv_w, m_w_out, m_b_out, m_ln_g, m_ln_b, v_emb_ln_g, v_emb_ln_b, v_w_in, v_q_norm_g, v_kv_norm_g, v_w_uq, v_w_ukv, v_w_pool, v_pool_scale, v_conv_w, v_w_out, v_b_out, v_ln_g, v_ln_b):
    xi, yi, ci = lax.axis_index("x"), lax.axis_index("y"), lax.axis_index("c")
    chip = 2 * xi + yi
    c_idx = ci.reshape(1).astype(jnp.int32)

    def t(a):
        return jnp.swapaxes(a, 1, 2)

    conv_bits = lax.bitcast_convert_type(conv_w.reshape(DEPTH, 3 * 128), BF16).reshape(DEPTH, 3, 256)
    conv_bits = jnp.pad(conv_bits, ((0, 0), (0, 13), (0, 0)))
    own = (t(w_in).astype(BF16), w_out.astype(BF16), t(w_uq).astype(BF16), t(w_ukv).astype(BF16))
    zeros = (jnp.zeros((GAP, D_MODEL), BF16), None, jnp.zeros((64, Q_LORA), BF16), None)
    (w_in_t0,) = _run_comm(_allgather_script((W_IN,), (own[0][0:1],), zeros[:1]), name="allgather_w_in0")
    gather0 = _allgather_script(SHARDED[1:] + (W_CONV,), tuple(a[0:1] for a in own[1:]) + (conv_bits,),
                                zeros[1:] + (None,))
    gather1 = _allgather_script(SHARDED, tuple(a[1:2] for a in own), zeros)

    loss_part, grad_x, d_emb_g, d_emb_b, grads, reduced = _local_step(
        x[0], positions[0], loss_target[0], emb_ln_g, emb_ln_b, w_in_t0[0], gather0, gather1, q_norm_g, kv_norm_g,
        w_pool, pool_scale, b_out, ln_g, ln_b, c_idx)

    small_g = [d_emb_g, d_emb_b,
               jnp.stack([grads[l]["q_g"] for l in range(DEPTH)]), jnp.stack([grads[l]["kv_g"] for l in range(DEPTH)]),
               jnp.stack([grads[l]["w_pool"] for l in range(DEPTH)]),
               jnp.stack([grads[l]["pool_scale"] for l in range(DEPTH)]),
               jnp.stack([grads[l]["b_out"] for l in range(DEPTH)]), jnp.stack([grads[l]["ln_g"] for l in range(DEPTH)]),
               jnp.stack([grads[l]["ln_b"] for l in range(DEPTH)]),
               jnp.stack([grads[l]["conv_w"] for l in range(DEPTH)]),
               jnp.pad(loss_part.reshape(1), (0, LANE - 1))]
    small_w = [emb_ln_g, emb_ln_b, q_norm_g, kv_norm_g, w_pool, pool_scale, b_out, ln_g, ln_b]
    small_m = [m_emb_ln_g, m_emb_ln_b, m_q_norm_g, m_kv_norm_g, m_w_pool, m_pool_scale, m_b_out, m_ln_g, m_ln_b]
    small_v = [v_emb_ln_g, v_emb_ln_b, v_q_norm_g, v_kv_norm_g, v_w_pool, v_pool_scale, v_b_out, v_ln_g, v_ln_b]
    extra = (DEPTH * 3 * 512 + LANE) // LANE
    packed_g = _pack_small(small_g, 0)
    gathered = _allgather_small(packed_g, name="allgather_small")
    g_tot, d_small, m_small, v_small = _small_sum_adamw(
        gathered, _pack_small(small_w, extra), _pack_small(small_m, extra), _pack_small(small_v, extra),
        name="small_sum_adamw")
    shapes = [w.shape for w in small_w]
    g_list, off = _unpack_small(g_tot, shapes)
    d_list, _ = _unpack_small(d_small, shapes)
    m_list, _ = _unpack_small(m_small, shapes)
    v_list, _ = _unpack_small(v_small, shapes)
    flat_tot = g_tot.reshape(-1)
    conv_tot = flat_tot[off:off + DEPTH * 3 * 512].reshape(DEPTH, 3, 512)
    loss = flat_tot[off + DEPTH * 3 * 512]
    g_conv = lax.dynamic_slice_in_dim(conv_tot, chip * 128, 128, axis=2)

    def halves(a):
        return [reduced[l][a] for l in range(DEPTH)]

    def whole(a):
        return jnp.stack([jnp.where(ci == 0, jnp.concatenate([mine, oth], axis=1),
                                    jnp.concatenate([oth, mine], axis=1)) for mine, oth in halves(a)])

    upd = {}
    upd["w_in"] = tuple(t(o) for o in _adamw_halves(t(w_in), t(m_w_in), t(v_w_in), halves(0), c_idx,
                                                    name="adamw_w_in"))
    upd["w_out"] = _adamw_halves(w_out, m_w_out, v_w_out, halves(1), c_idx, name="adamw_w_out")
    g_uq, g_ukv = t(whole(2)), t(whole(3))
    upd["w_uq"] = (g_uq,) + _adamw(w_uq, g_uq, m_w_uq, v_w_uq, name="adamw_w_uq")
    upd["w_ukv"] = (g_ukv,) + _adamw(w_ukv, g_ukv, m_w_ukv, v_w_ukv, name="adamw_w_ukv")
    upd["conv_w"] = (g_conv,) + _adamw(conv_w, g_conv, m_conv_w, v_conv_w, name="adamw_conv_w")
    for i, nm in enumerate(SMALL_ORDER):
        upd[nm] = (g_list[i], d_list[i], m_list[i], v_list[i])

    order = ("emb_ln_g", "emb_ln_b", "w_in", "q_norm_g", "kv_norm_g", "w_uq", "w_ukv", "w_pool", "pool_scale",
             "conv_w", "w_out", "b_out", "ln_g", "ln_b")
    outs = [loss, grad_x[None]]
    for field in range(4):
        outs += [upd[nm][field] for nm in order]
    return tuple(outs)
```

```python
import collections

import jax
import jax.numpy as jnp
from jax import lax
from jax.experimental import pallas as pl
from jax.experimental.pallas import tpu as pltpu

F32 = jnp.float32
BF16 = jnp.bfloat16
MESH = pl.DeviceIdType.MESH

D_MODEL = 2048
DEPTH = 2
N_HEADS = 8
NOPE = 128
ROPE = 64
Q_LORA = 512
KV_LORA = 256
D_MLA = 1024
POOL_WINDOWS = (2, 4, 8, 16)
D_IN_PROJ = 4928
LN_EPS = 1e-5
RMS_EPS = 1e-6
ROPE_THETA = 10000.0
ALPHA = (2 * DEPTH) ** 0.25
SCALE = (NOPE + ROPE) ** -0.5
LOG2E = 1.4426950408889634
SCALE_LOG2E = SCALE * LOG2E
ADAM_LR = 0.001
ADAM_B1 = 0.9
ADAM_B2 = 0.999
ADAM_EPS = 1e-08
ADAM_WD = 0.01
ADAM_STEP = 10

NP = 5120
GAP_AT = 832
GAP = NP - D_IN_PROJ
W_MLA = 1024
W_MIX = NP - W_MLA
HALO = 16
LANE = 128
N_CHIPS = 4
N_DEV = 8
TQ = 512

NN = (((1,), (0,)), ((), ()))
NT = (((1,), (1,)), ((), ()))
TN = (((0,), (0,)), ((), ()))


CommScript = collections.namedtuple("CommScript", "args out_shape n_sems start finish")
HBM_SPEC = pl.BlockSpec(memory_space=pl.ANY)


def _pcall(kern, *, name, out_shape, grid=None, in_specs=None, out_specs=None, scratch=(), dims=None,
           vmem_mb=None, comm=None):
    cp = {}
    if dims is not None:
        cp["dimension_semantics"] = dims if comm is None else ("arbitrary",) * len(dims)
    if vmem_mb is not None:
        cp["vmem_limit_bytes"] = vmem_mb << 20
    if comm is None:
        args = dict(name=name, out_shape=out_shape, scratch_shapes=list(scratch),
                    compiler_params=pltpu.CompilerParams(**cp))
        if grid is not None:
            args["grid"] = grid
        if in_specs is not None:
            args["in_specs"] = in_specs
        if out_specs is not None:
            args["out_specs"] = out_specs
        return pl.pallas_call(kern, **args)

    single = not isinstance(out_shape, (tuple, list))
    own_out = (out_shape,) if single else tuple(out_shape)
    own_out_specs = (out_specs,) if single else tuple(out_specs)
    n_in, n_out, n_scr = len(in_specs), len(own_out), len(scratch)
    na, no = len(comm.args), len(comm.out_shape)

    def at(end):
        cond = None
        for d, n in enumerate(grid):
            here = pl.program_id(d) == (n - 1 if end else 0)
            cond = here if cond is None else jnp.logical_and(cond, here)
        return cond

    def wrapped(*refs):
        own_in, c_in = refs[:n_in], refs[n_in:n_in + na]
        o0 = n_in + na
        own_o, c_out = refs[o0:o0 + n_out], refs[o0 + n_out:o0 + n_out + no]
        s0 = o0 + n_out + no
        own_s, (send_sems, recv_sems) = refs[s0:s0 + n_scr], refs[s0 + n_scr:]

        @pl.when(at(False))
        def _():
            comm.start(c_in, c_out, send_sems, recv_sems)

        kern(*own_in, *own_o, *own_s)

        @pl.when(at(True))
        def _():
            comm.finish(c_in, c_out, send_sems, recv_sems)

    call = pl.pallas_call(
        wrapped, name=name, out_shape=own_out + tuple(comm.out_shape), grid=grid,
        in_specs=list(in_specs) + [HBM_SPEC] * na, out_specs=own_out_specs + (HBM_SPEC,) * no,
        scratch_shapes=list(scratch) + [pltpu.SemaphoreType.DMA((comm.n_sems,)),
                                        pltpu.SemaphoreType.DMA((comm.n_sems,))],
        compiler_params=pltpu.CompilerParams(**cp))

    def run(*args):
        res = call(*args, *comm.args)
        own = res[0] if single else tuple(res[:n_out])
        return own, tuple(res[n_out:])

    return run


def _run_comm(script, *, name):
    na, no = len(script.args), len(script.out_shape)

    def body(*refs):
        ins, outs = refs[:na], refs[na:na + no]
        send_sems, recv_sems = refs[na + no:]
        script.start(ins, outs, send_sems, recv_sems)
        script.finish(ins, outs, send_sems, recv_sems)

    return pl.pallas_call(
        body, name=name, out_shape=tuple(script.out_shape), in_specs=[HBM_SPEC] * na, out_specs=(HBM_SPEC,) * no,
        scratch_shapes=[pltpu.SemaphoreType.DMA((script.n_sems,)), pltpu.SemaphoreType.DMA((script.n_sems,))])(
            *script.args)


def _sigmoid(g):
    return 1.0 / (1.0 + jnp.exp(-g))


def _silu_and_grad(g):
    sig = _sigmoid(g)
    return g * sig, sig * (1.0 + g * (1.0 - sig))


def _matmul(a, b, mode, *, name, tm, tn, tk, out_dtype=F32, vmem_mb=48, comm=None):
    if mode == "nn":
        (M, K), N = a.shape, b.shape[1]
    elif mode == "nt":
        (M, K), N = a.shape, b.shape[0]
    else:
        (K, M), N = a.shape, b.shape[1]
    tm, tn, tk = min(tm, M), min(tn, N), min(tk, K)
    assert M % tm == 0 and N % tn == 0 and K % tk == 0, (name, M, N, K)
    nk = K // tk
    dn = {"nn": NN, "nt": NT, "tn": TN}[mode]
    if mode == "tn":
        a_spec = pl.BlockSpec((tk, tm), lambda i, j, k: (k, i))
    else:
        a_spec = pl.BlockSpec((tm, tk), lambda i, j, k: (i, k))
    if mode == "nt":
        b_spec = pl.BlockSpec((tn, tk), lambda i, j, k: (j, k))
    else:
        b_spec = pl.BlockSpec((tk, tn), lambda i, j, k: (k, j))
    o_spec = pl.BlockSpec((tm, tn), lambda i, j, k: (i, j))

    def kern(a_ref, b_ref, o_ref, *rest):
        part = lax.dot_general(a_ref[...].astype(BF16), b_ref[...].astype(BF16), dn,
                               preferred_element_type=F32)
        if nk == 1:
            o_ref[...] = part.astype(out_dtype)
        else:
            acc_ref = rest[0]
            k = pl.program_id(2)

            @pl.when(k == 0)
            def _():
                acc_ref[...] = part

            @pl.when(k > 0)
            def _():
                acc_ref[...] += part

            @pl.when(k == nk - 1)
            def _():
                o_ref[...] = acc_ref[...].astype(out_dtype)

    scratch = [pltpu.VMEM((tm, tn), F32)] if nk > 1 else []
    return _pcall(kern, name=name, out_shape=jax.ShapeDtypeStruct((M, N), out_dtype),
                  grid=(M // tm, N // tn, nk), in_specs=[a_spec, b_spec], out_specs=o_spec, scratch=scratch,
                  dims=("parallel", "parallel", "arbitrary"), vmem_mb=vmem_mb, comm=comm)(a, b)


def _dproj_times_w(d_mla, d_mix, wt, add, add_scale, *, name, comm=None):
    S = d_mla.shape[0]
    Dm = wt.shape[1]
    tm, tn, tk = min(1024, S), 1024, W_MLA
    nk = NP // tk

    def kern(a1_ref, a2_ref, b_ref, add_ref, o_ref, acc_ref):
        k = pl.program_id(2)

        @pl.when(k == 0)
        def _():
            acc_ref[...] = jnp.dot(a1_ref[...], b_ref[...], preferred_element_type=F32)

        @pl.when(k > 0)
        def _():
            acc_ref[...] += jnp.dot(a2_ref[...], b_ref[...], preferred_element_type=F32)

        @pl.when(k == nk - 1)
        def _():
            o_ref[...] = add_scale * add_ref[...] + acc_ref[...]

    o_spec = pl.BlockSpec((tm, tn), lambda i, j, k: (i, j))
    return _pcall(kern, name=name, out_shape=jax.ShapeDtypeStruct((S, Dm), F32), grid=(S // tm, Dm // tn, nk),
                  in_specs=[pl.BlockSpec((tm, tk), lambda i, j, k: (i, 0)),
                            pl.BlockSpec((tm, tk), lambda i, j, k: (i, jnp.maximum(k - 1, 0))),
                            pl.BlockSpec((tk, tn), lambda i, j, k: (k, j)), o_spec],
                  out_specs=o_spec, scratch=[pltpu.VMEM((tm, tn), F32)],
                  dims=("parallel", "parallel", "arbitrary"), vmem_mb=48, comm=comm)(d_mla, d_mix, wt, add)


def _dproj_t_times_h(d_mla, d_mix, h, *, name, comm=None):
    S, Dm = h.shape
    tm, tn, tk = W_MLA, 1024, min(2048, S)
    nk = S // tk

    def kern(a1_ref, a2_ref, b_ref, o_ref, acc_ref):
        i = pl.program_id(0)
        k = pl.program_id(2)
        b = b_ref[...].astype(BF16)

        def accumulate(part):
            @pl.when(k == 0)
            def _():
                acc_ref[...] = part

            @pl.when(k > 0)
            def _():
                acc_ref[...] += part

        @pl.when(i == 0)
        def _():
            accumulate(lax.dot_general(a1_ref[...], b, TN, preferred_element_type=F32))

        @pl.when(i > 0)
        def _():
            accumulate(lax.dot_general(a2_ref[...], b, TN, preferred_element_type=F32))

        @pl.when(k == nk - 1)
        def _():
            o_ref[...] = acc_ref[...]

    return _pcall(kern, name=name, out_shape=jax.ShapeDtypeStruct((NP, Dm), F32), grid=(NP // tm, Dm // tn, nk),
                  in_specs=[pl.BlockSpec((tk, tm), lambda i, j, k: (jnp.where(i == 0, k, nk - 1), 0)),
                            pl.BlockSpec((tk, tm), lambda i, j, k: (jnp.where(i == 0, 0, k), jnp.maximum(i - 1, 0))),
                            pl.BlockSpec((tk, tn), lambda i, j, k: (k, j))],
                  out_specs=pl.BlockSpec((tm, tn), lambda i, j, k: (i, j)), scratch=[pltpu.VMEM((tm, tn), F32)],
                  dims=("parallel", "parallel", "arbitrary"), vmem_mb=48, comm=comm)(d_mla, d_mix, h)


def _ln_fwd(x, g, b, *, name, comm=None):
    S, Dm = x.shape
    tm = min(512, S)

    def kern(x_ref, g_ref, b_ref, y_ref, yb_ref):
        xf = x_ref[...]
        mu = jnp.mean(xf, axis=-1, keepdims=True)
        xc = xf - mu
        var = jnp.mean(xc * xc, axis=-1, keepdims=True)
        y = xc * lax.rsqrt(var + LN_EPS) * g_ref[...] + b_ref[...]
        y_ref[...] = y
        yb_ref[...] = y.astype(BF16)

    row = pl.BlockSpec((tm, Dm), lambda i: (i, 0))
    vec = pl.BlockSpec((1, Dm), lambda i: (0, 0))
    return _pcall(kern, name=name,
                  out_shape=(jax.ShapeDtypeStruct((S, Dm), F32), jax.ShapeDtypeStruct((S, Dm), BF16)),
                  grid=(S // tm,), in_specs=[row, vec, vec], out_specs=(row, row), dims=("parallel",), vmem_mb=48,
                  comm=comm)(
                      x, g.reshape(1, Dm), b.reshape(1, Dm))


def _ln_bwd(dy, r, g, *, name):
    S, Dm = r.shape
    tm = min(512, S)

    def kern(dy_ref, r_ref, g_ref, dr_ref, drb_ref, dg_ref, db_ref, ds_ref):
        @pl.when(pl.program_id(0) == 0)
        def _():
            dg_ref[...] = jnp.zeros_like(dg_ref)
            db_ref[...] = jnp.zeros_like(db_ref)
            ds_ref[...] = jnp.zeros_like(ds_ref)

        rf = r_ref[...]
        dyf = dy_ref[...]
        mu = jnp.mean(rf, axis=-1, keepdims=True)
        xc = rf - mu
        var = jnp.mean(xc * xc, axis=-1, keepdims=True)
        rstd = lax.rsqrt(var + LN_EPS)
        xhat = xc * rstd
        dxh = dyf * g_ref[...]
        c1 = jnp.mean(dxh, axis=-1, keepdims=True)
        c2 = jnp.mean(dxh * xhat, axis=-1, keepdims=True)
        dr = rstd * (dxh - c1 - xhat * c2)
        dr_ref[...] = dr
        drb_ref[...] = dr.astype(BF16)
        dg_ref[...] += jnp.sum(dyf * xhat, axis=0, keepdims=True)
        db_ref[...] += jnp.sum(dyf, axis=0, keepdims=True)
        ds_ref[...] += jnp.sum(dr, axis=0, keepdims=True)

    row = pl.BlockSpec((tm, Dm), lambda i: (i, 0))
    vec = pl.BlockSpec((1, Dm), lambda i: (0, 0))
    vshape = jax.ShapeDtypeStruct((1, Dm), F32)
    return _pcall(kern, name=name,
                  out_shape=(jax.ShapeDtypeStruct((S, Dm), F32), jax.ShapeDtypeStruct((S, Dm), BF16),
                             vshape, vshape, vshape),
                  grid=(S // tm,), in_specs=[row, row, vec], out_specs=(row, row, vec, vec, vec),
                  dims=("arbitrary",), vmem_mb=48)(dy, r, g.reshape(1, Dm))


def _loss_ln_bwd(y, target, r, g, *, name):
    S, Dm = r.shape
    tm = min(512, S)

    def kern(y_ref, t_ref, r_ref, g_ref, l_ref, dr_ref, drb_ref, dg_ref, db_ref, ds_ref):
        @pl.when(pl.program_id(0) == 0)
        def _():
            l_ref[...] = jnp.zeros_like(l_ref)
            dg_ref[...] = jnp.zeros_like(dg_ref)
            db_ref[...] = jnp.zeros_like(db_ref)
            ds_ref[...] = jnp.zeros_like(ds_ref)

        e = y_ref[...] - t_ref[...]
        dyf = e / float(Dm)
        per_row = jnp.mean(e * e, axis=-1, keepdims=True)
        l_ref[...] += 0.5 * jnp.sum(per_row, axis=0, keepdims=True)
        rf = r_ref[...]
        mu = jnp.mean(rf, axis=-1, keepdims=True)
        xc = rf - mu
        var = jnp.mean(xc * xc, axis=-1, keepdims=True)
        rstd = lax.rsqrt(var + LN_EPS)
        xhat = xc * rstd
        dxh = dyf * g_ref[...]
        c1 = jnp.mean(dxh, axis=-1, keepdims=True)
        c2 = jnp.mean(dxh * xhat, axis=-1, keepdims=True)
        dr = rstd * (dxh - c1 - xhat * c2)
        dr_ref[...] = dr
        drb_ref[...] = dr.astype(BF16)
        dg_ref[...] += jnp.sum(dyf * xhat, axis=0, keepdims=True)
        db_ref[...] += jnp.sum(dyf, axis=0, keepdims=True)
        ds_ref[...] += jnp.sum(dr, axis=0, keepdims=True)

    row = pl.BlockSpec((tm, Dm), lambda i: (i, 0))
    vec = pl.BlockSpec((1, Dm), lambda i: (0, 0))
    acc = pl.BlockSpec((8, LANE), lambda i: (0, 0))
    vshape = jax.ShapeDtypeStruct((1, Dm), F32)
    return _pcall(kern, name=name,
                  out_shape=(jax.ShapeDtypeStruct((8, LANE), F32), jax.ShapeDtypeStruct((S, Dm), F32),
                             jax.ShapeDtypeStruct((S, Dm), BF16), vshape, vshape, vshape),
                  grid=(S // tm,), in_specs=[row, row, row, vec], out_specs=(acc, row, row, vec, vec, vec),
                  dims=("arbitrary",), vmem_mb=56)(y, target, r, g.reshape(1, Dm))


def _rot_sum(t):
    return pltpu.roll(t, 32, 1) + pltpu.roll(t, 96, 1)


def _mla_qkv(proj, cos_t, sin_t, qg, kvg, wuq_t, wukv_t, *, name):
    S = proj.shape[0]
    tm = min(256, S)

    def kern(ql_ref, kvl_ref, kr_ref, cos_ref, sin_ref, qg_ref, kvg_ref, wuq_ref, wukv_ref,
             qc_ref, kc_ref, v_ref, vt_ref, qn_ref, kvn_ref):
        cosv = cos_ref[...]
        sinv = sin_ref[...]

        def rope(t):
            return t * cosv + _rot_sum(t) * sinv

        ql = ql_ref[...]
        qn = (ql * lax.rsqrt(jnp.mean(ql * ql, axis=-1, keepdims=True) + RMS_EPS) * qg_ref[...]).astype(BF16)
        kvl = kvl_ref[...]
        kvn = (kvl * lax.rsqrt(jnp.mean(kvl * kvl, axis=-1, keepdims=True) + RMS_EPS) * kvg_ref[...]).astype(BF16)
        qn_ref[...] = qn
        kvn_ref[...] = kvn
        q = lax.dot_general(qn, wuq_ref[...], NT, preferred_element_type=F32)
        kv = lax.dot_general(kvn, wukv_ref[...], NT, preferred_element_type=F32)
        kr = rope(kr_ref[...]).astype(BF16)
        for h in range(N_HEADS):
            c0 = 256 * h
            qc_ref[:, c0:c0 + 128] = q[:, c0:c0 + 128].astype(BF16)
            qc_ref[:, c0 + 128:c0 + 256] = rope(q[:, c0 + 128:c0 + 256]).astype(BF16)
            kc_ref[:, c0:c0 + 128] = kv[:, c0:c0 + 128].astype(BF16)
            kc_ref[:, c0 + 128:c0 + 256] = kr
            vh = kv[:, c0 + 128:c0 + 256]
            v_ref[:, 128 * h:128 * h + 128] = vh.astype(BF16)
            vt_ref[h] = jnp.transpose(vh).astype(BF16)

    def row(w, blk):
        return pl.BlockSpec((tm, w), lambda i: (i, blk))

    def full(shape):
        return pl.BlockSpec(shape, lambda i: (0,) * len(shape))

    t = min(TQ, S)
    per = t // tm
    vt_spec = pl.BlockSpec((N_HEADS, None, 128, tm), lambda i: (0, i // per, 0, i % per))
    outs = (jax.ShapeDtypeStruct((S, 2048), BF16), jax.ShapeDtypeStruct((S, 2048), BF16),
            jax.ShapeDtypeStruct((S, 1024), BF16), jax.ShapeDtypeStruct((N_HEADS, S // t, 128, t), BF16),
            jax.ShapeDtypeStruct((S, Q_LORA), BF16), jax.ShapeDtypeStruct((S, KV_LORA), BF16))
    return _pcall(kern, name=name, out_shape=outs, grid=(S // tm,),
                  in_specs=[row(512, 0), row(256, 2), row(128, 6), row(128, 0), row(128, 0),
                            full((1, Q_LORA)), full((1, KV_LORA)), full((2048, Q_LORA)), full((2048, KV_LORA))],
                  out_specs=(row(2048, 0), row(2048, 0), row(1024, 0), vt_spec, row(512, 0), row(256, 0)),
                  dims=("parallel",), vmem_mb=48)(
                      proj, proj, proj, cos_t, sin_t, qg.reshape(1, -1), kvg.reshape(1, -1), wuq_t, wukv_t)


def _mla_qkv_bwd(dqb, dkvb, dkr_heads, proj, cos_t, sin_t, qg, kvg, wuq_t, wukv_t, *, name):
    S = proj.shape[0]
    tm = min(256, S)

    def kern(dqb_ref, dkvb_ref, dkrh_ref, ql_ref, kvl_ref, cos_ref, sin_ref, qg_ref, kvg_ref, wuq_ref, wukv_ref,
             dml_ref, dqg_ref, dkvg_ref):
        @pl.when(pl.program_id(0) == 0)
        def _():
            dqg_ref[...] = jnp.zeros_like(dqg_ref)
            dkvg_ref[...] = jnp.zeros_like(dkvg_ref)

        cosv = cos_ref[...]
        sinv = sin_ref[...]

        def unrope(t):
            return t * cosv - _rot_sum(t) * sinv

        dkr = dkrh_ref[:, 0:128]
        for h in range(1, N_HEADS):
            dkr = dkr + dkrh_ref[:, 128 * h:128 * h + 128]

        def rms_bwd(x, g, dy):
            n = x.shape[-1]
            rs = lax.rsqrt(jnp.mean(x * x, axis=-1, keepdims=True) + RMS_EPS)
            dyg = dy * g
            dx = rs * dyg - x * (rs * rs * rs) * (jnp.sum(dyg * x, axis=-1, keepdims=True) / n)
            return dx, jnp.sum(dy * (x * rs), axis=0, keepdims=True)

        dqn = jnp.dot(dqb_ref[...], wuq_ref[...], preferred_element_type=F32)
        dql, dqg = rms_bwd(ql_ref[...], qg_ref[...], dqn)
        dqg_ref[...] += dqg
        dkvn = jnp.dot(dkvb_ref[...], wukv_ref[...], preferred_element_type=F32)
        dkvl, dkvg = rms_bwd(kvl_ref[...], kvg_ref[...], dkvn)
        dkvg_ref[...] += dkvg
        dml_ref[:, 0:512] = dql.astype(BF16)
        dml_ref[:, 512:768] = dkvl.astype(BF16)
        dml_ref[:, 768:896] = unrope(dkr).astype(BF16)
        dml_ref[:, 896:1024] = jnp.zeros((tm, 128), BF16)

    def row(w, blk):
        return pl.BlockSpec((tm, w), lambda i: (i, blk))

    def full(shape):
        return pl.BlockSpec(shape, lambda i: (0,) * len(shape))

    outs = (jax.ShapeDtypeStruct((S, W_MLA), BF16), jax.ShapeDtypeStruct((1, Q_LORA), F32),
            jax.ShapeDtypeStruct((1, KV_LORA), F32))
    return _pcall(kern, name=name, out_shape=outs, grid=(S // tm,),
                  in_specs=[row(2048, 0), row(2048, 0), row(1024, 0), row(512, 0), row(256, 2),
                            row(128, 0), row(128, 0), full((1, Q_LORA)), full((1, KV_LORA)),
                            full((2048, Q_LORA)), full((2048, KV_LORA))],
                  out_specs=(row(W_MLA, 0), full((1, Q_LORA)), full((1, KV_LORA))),
                  dims=("arbitrary",), vmem_mb=56)(
                      dqb, dkvb, dkr_heads, proj, proj, cos_t, sin_t, qg.reshape(1, -1), kvg.reshape(1, -1),
                      wuq_t, wukv_t)


def _kq_mask(t):
    krow = lax.broadcasted_iota(jnp.int32, (t, t), 0)
    qcol = lax.broadcasted_iota(jnp.int32, (t, t), 1)
    return krow <= qcol


def _flash_fwd(qc, kc, vt, *, name, comm=None):
    S = qc.shape[0]
    t = min(TQ, S)
    n = S // t

    def kern(q_ref, k_ref, vt_ref, o_ref, lse_ref, m_s, l_s, acc_s):
        qi = pl.program_id(1)
        m_s[...] = jnp.full_like(m_s, -jnp.inf)
        l_s[...] = jnp.zeros_like(l_s)
        acc_s[...] = jnp.zeros_like(acc_s)

        def scores(kb):
            k0 = pl.multiple_of(kb * t, t)
            return lax.dot_general(k_ref[pl.ds(k0, t), :], q_ref[...], NT, preferred_element_type=F32)

        def update(kb, st, masked):
            if masked:
                st = jnp.where(_kq_mask(t), st, -jnp.inf)
            m_prev = m_s[...]
            m_new = jnp.maximum(m_prev, jnp.max(st, axis=0, keepdims=True))
            a = jnp.exp2((m_prev - m_new) * SCALE_LOG2E)
            pt = jnp.exp2((st - m_new) * SCALE_LOG2E)
            l_s[...] = a * l_s[...] + jnp.sum(pt, axis=0, keepdims=True)
            acc_s[...] = a * acc_s[...] + jnp.dot(vt_ref[kb], pt.astype(BF16), preferred_element_type=F32)
            m_s[...] = m_new

        def pair(kb, second_masked):
            s0, s1 = scores(kb), scores(kb + 1)
            update(kb, s0, False)
            update(kb + 1, s1, second_masked)

        def body(i, carry):
            pair(2 * i, False)
            return carry

        lax.fori_loop(0, qi // 2, body, 0)

        @pl.when(qi % 2 == 1)
        def _():
            pair(qi - 1, True)

        @pl.when(qi % 2 == 0)
        def _():
            update(qi, scores(qi), True)
        o_ref[...] = jnp.transpose(acc_s[...] / l_s[...])
        lse_ref[pl.ds(qi, 1), :] = m_s[...] * SCALE_LOG2E + jnp.log2(l_s[...])

    q_spec = pl.BlockSpec((t, 256), lambda h, qi: (qi, h))
    k_spec = pl.BlockSpec((S, 256), lambda h, qi: (0, h))
    vt_spec = pl.BlockSpec((None, n, 128, t), lambda h, qi: (h, 0, 0, 0))
    o_spec = pl.BlockSpec((t, 128), lambda h, qi: (qi, h))
    lse_spec = pl.BlockSpec((None, n, t), lambda h, qi: (h, 0, 0))
    return _pcall(kern, name=name,
                  out_shape=(jax.ShapeDtypeStruct((S, D_MLA), F32), jax.ShapeDtypeStruct((N_HEADS, n, t), F32)),
                  grid=(N_HEADS, n), in_specs=[q_spec, k_spec, vt_spec], out_specs=(o_spec, lse_spec),
                  scratch=[pltpu.VMEM((1, t), F32), pltpu.VMEM((1, t), F32), pltpu.VMEM((128, t), F32)],
                  dims=("parallel", "arbitrary"), vmem_mb=48, comm=comm)(qc, kc, vt)


def _attn_delta(o, do, *, name):
    S = o.shape[0]
    t = min(TQ, S)
    n = S // t

    def kern(o_ref, do_ref, dl_ref):
        i = pl.program_id(0)
        prod = o_ref[...] * do_ref[...]
        lane = lax.broadcasted_iota(jnp.int32, (t, LANE), 1)
        dmat = jnp.zeros((t, LANE), F32)
        for h in range(N_HEADS):
            dmat = jnp.where(lane == h, jnp.sum(prod[:, 128 * h:128 * h + 128], axis=1, keepdims=True), dmat)
        dmat_t = jnp.transpose(dmat)
        for h in range(N_HEADS):
            dl_ref[h, pl.ds(i, 1), :] = dmat_t[h:h + 1, :]

    row = pl.BlockSpec((t, D_MLA), lambda i: (i, 0))
    return _pcall(kern, name=name, out_shape=jax.ShapeDtypeStruct((N_HEADS, n, t), F32), grid=(n,),
                  in_specs=[row, row], out_specs=pl.BlockSpec((N_HEADS, n, t), lambda i: (0, 0, 0)),
                  dims=("arbitrary",), vmem_mb=48)(o, do)


def _flash_bwd(qc, kc, v, do, lse2, delta, cos_t, sin_t, *, name, comm=None):
    S = qc.shape[0]
    t = min(TQ, S)
    n = S // t

    def kern(q_ref, k_ref, v_ref, do_ref, lse_ref, dl_ref, cos_ref, sin_ref, dqb_ref, dkvb_ref, dkr_ref,
             dq_ref, dk_ref, dv_ref):
        ki = pl.program_id(1)

        @pl.when(ki == 0)
        def _():
            dq_ref[...] = jnp.zeros_like(dq_ref)

        dk_ref[...] = jnp.zeros_like(dk_ref)
        dv_ref[...] = jnp.zeros_like(dv_ref)

        def step(qb, masked):
            q0 = pl.multiple_of(qb * t, t)
            kt = k_ref[...]
            qblk = q_ref[pl.ds(q0, t), :]
            dob = do_ref[pl.ds(q0, t), :].astype(BF16)
            st = lax.dot_general(kt, qblk, NT, preferred_element_type=F32)
            pt = jnp.exp2(st * SCALE_LOG2E - lse_ref[pl.ds(qb, 1), :])
            if masked:
                pt = jnp.where(_kq_mask(t), pt, 0.0)
            dv_ref[...] += jnp.dot(pt.astype(BF16), dob, preferred_element_type=F32)
            dpt = lax.dot_general(v_ref[...], dob, NT, preferred_element_type=F32)
            dst = (pt * (dpt - dl_ref[pl.ds(qb, 1), :]) * SCALE).astype(BF16)
            dk_ref[...] += jnp.dot(dst, qblk, preferred_element_type=F32)
            dq_ref[pl.ds(q0, t), :] += lax.dot_general(dst, kt, TN, preferred_element_type=F32)

        step(ki, True)
        rest = n - 1 - ki

        def body(i, carry):
            step(ki + 1 + 2 * i, False)
            step(ki + 2 + 2 * i, False)
            return carry

        lax.fori_loop(0, rest // 2, body, 0)

        @pl.when(rest % 2 == 1)
        def _():
            step(n - 1, False)

        dkvb_ref[:, 0:128] = dk_ref[:, 0:128].astype(BF16)
        dkvb_ref[:, 128:256] = dv_ref[...].astype(BF16)
        dkr_ref[...] = dk_ref[:, 128:256]

        @pl.when(ki == n - 1)
        def _():
            dqb_ref[:, 0:128] = dq_ref[:, 0:128].astype(BF16)
            dqr = dq_ref[:, 128:256]
            dqb_ref[:, 128:256] = (dqr * cos_ref[...] - _rot_sum(dqr) * sin_ref[...]).astype(BF16)

    def whole(w):
        return pl.BlockSpec((S, w), lambda h, ki: (0, h))

    def krow(w):
        return pl.BlockSpec((t, w), lambda h, ki: (ki, h))

    stat = pl.BlockSpec((None, n, t), lambda h, ki: (h, 0, 0))
    table = pl.BlockSpec((S, 128), lambda h, ki: (0, 0))
    return _pcall(kern, name=name,
                  out_shape=(jax.ShapeDtypeStruct((S, 2048), BF16), jax.ShapeDtypeStruct((S, 2048), BF16),
                             jax.ShapeDtypeStruct((S, D_MLA), F32)),
                  grid=(N_HEADS, n),
                  in_specs=[whole(256), krow(256), krow(128), whole(128), stat, stat, table, table],
                  out_specs=(whole(256), krow(256), krow(128)),
                  scratch=[pltpu.VMEM((S, 256), F32), pltpu.VMEM((t, 256), F32), pltpu.VMEM((t, 128), F32)],
                  dims=("parallel", "arbitrary"), vmem_mb=56, comm=comm)(qc, kc, v, do, lse2, delta, cos_t, sin_t)


def _mixer_specs(S, tm):
    hb = tm // HALO
    last_hb = S // HALO - 1

    def main(w, blk):
        return pl.BlockSpec((tm, w), lambda i: (i, blk))

    def prev(w, blk):
        return pl.BlockSpec((HALO, w), lambda i: (jnp.maximum(i * hb - 1, 0), blk))

    def nxt(w, blk):
        return pl.BlockSpec((HALO, w), lambda i: (jnp.minimum((i + 1) * hb, last_hb), blk))

    def full(shape):
        return pl.BlockSpec(shape, lambda i: (0,) * len(shape))

    return main, prev, nxt, full


def _fill_halo(i, xp, xu, hp_ref, hch_ref, hcc_ref, pin_ref, ch_ref, cc_ref, tm):
    first = i == 0
    xp[0:HALO, :] = jnp.where(first, 0.0, hp_ref[...])
    xp[HALO:HALO + tm, :] = pin_ref[...]
    xu[0:HALO, :] = jnp.where(first, 0.0, hch_ref[...] * hcc_ref[...])
    xu[HALO:HALO + tm, :] = cc_ref[...] * ch_ref[...]


def _pooled(xp, g, t1, tm):
    w = POOL_WINDOWS[g]
    lanes = slice(128 * g, 128 * g + 128)
    x0 = xp[HALO:HALO + tm, lanes]
    acc = x0
    for k in range(1, w):
        acc = acc + xp[HALO - k:HALO - k + tm, lanes]
    return acc / jnp.minimum(t1, float(w)) - x0


def _conv_fwd(xu, cw_ref, tm):
    return (cw_ref[0:1, :] * xu[HALO - 2:HALO - 2 + tm, :] + cw_ref[1:2, :] * xu[HALO - 1:HALO - 1 + tm, :]
            + cw_ref[2:3, :] * xu[HALO:HALO + tm, :])


def _mixer_fwd(proj, o, wpool, ps, convw, *, name):
    S = proj.shape[0]
    tm = min(256, S)
    main, prev, _, full = _mixer_specs(S, tm)

    def kern(gm_ref, pin_ref, gp_ref, ch_ref, cb_ref, cc_ref, gc_ref, hp_ref, hch_ref, hcc_ref,
             o_ref, wp_ref, ps_ref, cw_ref, mix_ref, xp, xu):
        i = pl.program_id(0)
        _fill_halo(i, xp, xu, hp_ref, hch_ref, hcc_ref, pin_ref, ch_ref, cc_ref, tm)
        t1 = (i * tm + lax.broadcasted_iota(jnp.int32, (tm, 1), 0) + 1).astype(F32)
        for g in range(4):
            lanes = slice(128 * g, 128 * g + 128)
            pooled = _pooled(xp, g, t1, tm)
            z = jnp.dot(pooled.astype(BF16), wp_ref[g].astype(BF16), preferred_element_type=F32)
            gp = gp_ref[:, lanes]
            y = z * ps_ref[:, lanes] * (gp * _sigmoid(gp))
            mix_ref[:, 1024 + 128 * g:1024 + 128 * g + 128] = y.astype(BF16)
        gc = gc_ref[...]
        mix_ref[:, 1536:2048] = (cb_ref[...] * _conv_fwd(xu, cw_ref, tm) * (gc * _sigmoid(gc))).astype(BF16)
        gm = gm_ref[...]
        mix_ref[:, 0:1024] = (o_ref[...] * (gm * _sigmoid(gm))).astype(BF16)

    return _pcall(kern, name=name, out_shape=jax.ShapeDtypeStruct((S, 2048), BF16), grid=(S // tm,),
                  in_specs=[main(1024, 1), main(512, 4), main(512, 5), main(512, 6), main(512, 7), main(512, 8),
                            main(512, 9), prev(512, 4), prev(512, 6), prev(512, 8),
                            main(1024, 0), full((4, 128, 128)), full((1, 512)), full((3, 512))],
                  out_specs=main(2048, 0),
                  scratch=[pltpu.VMEM((tm + HALO, 512), F32), pltpu.VMEM((tm + HALO, 512), F32)],
                  dims=("parallel",), vmem_mb=48)(
                      proj, proj, proj, proj, proj, proj, proj, proj, proj, proj, o, wpool, ps.reshape(1, 512), convw)


def _mixer_bwd(dmix, proj, o, wpool, ps, convw, *, name):
    S = proj.shape[0]
    tm = min(256, S)
    n = S // tm
    main, prev, nxt, full = _mixer_specs(S, tm)

    def kern(dm_ref, dmn_ref, gm_ref, pin_ref, gp_ref, ch_ref, cb_ref, cc_ref, gc_ref,
             hp_ref, hch_ref, hcc_ref, gpn_ref, cbn_ref, gcn_ref, o_ref, wp_ref, ps_ref, cw_ref,
             d_ref, do_ref, dwp_ref, dps_ref, dcw_ref, xp, xu, ee, ed):
        i = pl.program_id(0)
        last = i == n - 1

        @pl.when(i == 0)
        def _():
            dwp_ref[...] = jnp.zeros_like(dwp_ref)
            dps_ref[...] = jnp.zeros_like(dps_ref)
            dcw_ref[...] = jnp.zeros_like(dcw_ref)

        _fill_halo(i, xp, xu, hp_ref, hch_ref, hcc_ref, pin_ref, ch_ref, cc_ref, tm)
        t1 = (i * tm + lax.broadcasted_iota(jnp.int32, (tm, 1), 0) + 1).astype(F32)
        t1n = ((i + 1) * tm + lax.broadcasted_iota(jnp.int32, (HALO, 1), 0) + 1).astype(F32)
        c_pin, c_gp, c_ch, c_cb, c_cc, c_gc = 1024, 1536, 2048, 2560, 3072, 3584

        for g in range(4):
            w = float(POOL_WINDOWS[g])
            lanes = slice(128 * g, 128 * g + 128)
            pooled = _pooled(xp, g, t1, tm)
            pb = pooled.astype(BF16)
            wp = wp_ref[g].astype(BF16)
            z = jnp.dot(pb, wp, preferred_element_type=F32)
            psl = ps_ref[:, lanes]
            sg, dsg = _silu_and_grad(gp_ref[:, lanes])
            dmp = dm_ref[:, 1024 + 128 * g:1024 + 128 * g + 128]
            dyp = dmp * sg
            d_ref[:, c_gp + 128 * g:c_gp + 128 * g + 128] = (dmp * (z * psl) * dsg).astype(BF16)
            dps_ref[:, lanes] += jnp.sum(dyp * z, axis=0, keepdims=True)
            dz = (dyp * psl).astype(BF16)
            dwp_ref[g] += lax.dot_general(pb, dz, TN, preferred_element_type=F32)
            dpl = lax.dot_general(dz, wp, NT, preferred_element_type=F32)
            ee[0:tm, lanes] = dpl / jnp.minimum(t1, w)
            gpn = gpn_ref[:, lanes]
            dzn = (dmn_ref[:, lanes] * (gpn * _sigmoid(gpn)) * psl).astype(BF16)
            dpn = lax.dot_general(dzn, wp, NT, preferred_element_type=F32)
            ee[tm:tm + HALO, lanes] = jnp.where(last, 0.0, dpn / jnp.minimum(t1n, w))
            acc = ee[0:tm, lanes]
            for k in range(1, POOL_WINDOWS[g]):
                acc = acc + ee[k:k + tm, lanes]
            d_ref[:, c_pin + 128 * g:c_pin + 128 * g + 128] = (acc - dpl).astype(BF16)

        yc = _conv_fwd(xu, cw_ref, tm)
        sgc, dsgc = _silu_and_grad(gc_ref[...])
        cb = cb_ref[...]
        dmc = dm_ref[:, 1536:2048]
        d_ref[:, c_gc:c_gc + 512] = (dmc * cb * yc * dsgc).astype(BF16)
        d_ref[:, c_cb:c_cb + 512] = (dmc * yc * sgc).astype(BF16)
        dyc = dmc * cb * sgc
        ed[0:tm, :] = dyc
        gcn = gcn_ref[...]
        ed[tm:tm + HALO, :] = jnp.where(last, 0.0, dmn_ref[:, 512:1024] * cbn_ref[...] * (gcn * _sigmoid(gcn)))
        dcw_ref[0:1, :] += jnp.sum(dyc * xu[HALO - 2:HALO - 2 + tm, :], axis=0, keepdims=True)
        dcw_ref[1:2, :] += jnp.sum(dyc * xu[HALO - 1:HALO - 1 + tm, :], axis=0, keepdims=True)
        dcw_ref[2:3, :] += jnp.sum(dyc * xu[HALO:HALO + tm, :], axis=0, keepdims=True)
        du = cw_ref[2:3, :] * dyc + cw_ref[1:2, :] * ed[1:1 + tm, :] + cw_ref[0:1, :] * ed[2:2 + tm, :]
        d_ref[:, c_cc:c_cc + 512] = (du * ch_ref[...]).astype(BF16)
        d_ref[:, c_ch:c_ch + 512] = (du * cc_ref[...]).astype(BF16)

        sgm, dsgm = _silu_and_grad(gm_ref[...])
        dmm = dm_ref[:, 0:1024]
        do_ref[...] = dmm * sgm
        d_ref[:, 0:1024] = (dmm * o_ref[...] * dsgm).astype(BF16)

    outs = (jax.ShapeDtypeStruct((S, W_MIX), BF16), jax.ShapeDtypeStruct((S, 1024), F32),
            jax.ShapeDtypeStruct((4, 128, 128), F32), jax.ShapeDtypeStruct((1, 512), F32),
            jax.ShapeDtypeStruct((3, 512), F32))
    scr = [pltpu.VMEM((tm + HALO, 512), F32) for _ in range(4)]
    return _pcall(kern, name=name, out_shape=outs, grid=(n,),
                  in_specs=[main(2048, 0), nxt(1024, 1),
                            main(1024, 1), main(512, 4), main(512, 5), main(512, 6), main(512, 7), main(512, 8),
                            main(512, 9), prev(512, 4), prev(512, 6), prev(512, 8),
                            nxt(512, 5), nxt(512, 7), nxt(512, 9),
                            main(1024, 0), full((4, 128, 128)), full((1, 512)), full((3, 512))],
                  out_specs=(main(W_MIX, 0), main(1024, 0), full((4, 128, 128)), full((1, 512)), full((3, 512))),
                  scratch=scr, dims=("arbitrary",), vmem_mb=56)(
                      dmix, dmix, proj, proj, proj, proj, proj, proj, proj, proj, proj, proj, proj, proj, proj,
                      o, wpool, ps.reshape(1, 512), convw)


def _outproj_ln(mix, wout, h, bout, g, b, *, name):
    S, Dm = h.shape
    tm = min(256, S)

    def kern(mix_ref, w_ref, h_ref, bo_ref, g_ref, b_ref, y_ref, yb_ref, r_ref):
        out = jnp.dot(mix_ref[...], w_ref[...], preferred_element_type=F32) + bo_ref[...]
        r = ALPHA * h_ref[...] + out
        r_ref[...] = r
        mu = jnp.mean(r, axis=-1, keepdims=True)
        xc = r - mu
        var = jnp.mean(xc * xc, axis=-1, keepdims=True)
        y = xc * lax.rsqrt(var + LN_EPS) * g_ref[...] + b_ref[...]
        y_ref[...] = y
        yb_ref[...] = y.astype(BF16)

    row = pl.BlockSpec((tm, Dm), lambda i: (i, 0))
    vec = pl.BlockSpec((1, Dm), lambda i: (0, 0))
    wsp = pl.BlockSpec((Dm, Dm), lambda i: (0, 0))
    sds = jax.ShapeDtypeStruct((S, Dm), F32)
    return _pcall(kern, name=name, out_shape=(sds, jax.ShapeDtypeStruct((S, Dm), BF16), sds), grid=(S // tm,),
                  in_specs=[row, wsp, row, vec, vec, vec], out_specs=(row, row, row), dims=("parallel",),
                  vmem_mb=56)(
                      mix, wout, h, bout.reshape(1, Dm), g.reshape(1, Dm), b.reshape(1, Dm))


def _adamw_math(w, g, m, v):
    m = ADAM_B1 * m + (1.0 - ADAM_B1) * g
    v = ADAM_B2 * v + (1.0 - ADAM_B2) * (g * g)
    m_hat = m / (1.0 - ADAM_B1 ** ADAM_STEP)
    v_hat = v / (1.0 - ADAM_B2 ** ADAM_STEP)
    delta = -ADAM_LR * (m_hat / (jnp.sqrt(v_hat) + ADAM_EPS) + ADAM_WD * w)
    return delta, m, v


def _row_tile(R, C):
    best = None
    for cand in range(8, R, 8):
        if R % cand == 0 and cand * C <= 256 * 1024:
            best = cand
    return best if best is not None else R


def _adamw(w, g, m, v, *, name):
    shape = w.shape
    C = shape[-1]
    R = 1
    for s in shape[:-1]:
        R *= s
    tr = _row_tile(R, C)

    def kern(w_ref, g_ref, m_ref, v_ref, d_ref, mo_ref, vo_ref):
        d, mn, vn = _adamw_math(w_ref[...], g_ref[...], m_ref[...], v_ref[...])
        d_ref[...] = d
        mo_ref[...] = mn
        vo_ref[...] = vn

    blk = pl.BlockSpec((tr, C), lambda i: (i, 0))
    sds = jax.ShapeDtypeStruct((R, C), F32)
    outs = _pcall(kern, name=name, out_shape=(sds, sds, sds), grid=(R // tr,), in_specs=[blk] * 4,
                  out_specs=(blk, blk, blk), dims=("parallel",), vmem_mb=48)(
                      w.reshape(R, C), g.reshape(R, C), m.reshape(R, C), v.reshape(R, C))
    return tuple(t.reshape(shape) for t in outs)


def _adamw_halves(w, m, v, halves, c_idx, *, name):
    _, R, C = w.shape
    ch = C // 2
    tr = _row_tile(R, ch)
    nb = R // tr

    def kern(c_ref, w_ref, a0_ref, b0_ref, a1_ref, b1_ref, m_ref, v_ref, g_ref, d_ref, mo_ref, vo_ref):
        layer = pl.program_id(0) // nb
        mine = pl.program_id(1) == c_ref[0]
        g = jnp.where(layer == 0, jnp.where(mine, a0_ref[...], b0_ref[...]),
                      jnp.where(mine, a1_ref[...], b1_ref[...]))
        g_ref[...] = g
        d, mn, vn = _adamw_math(w_ref[...], g, m_ref[...], v_ref[...])
        d_ref[...] = d
        mo_ref[...] = mn
        vo_ref[...] = vn

    full = pl.BlockSpec((tr, ch), lambda i, hc, c: (i, hc))
    half = pl.BlockSpec((tr, ch), lambda i, hc, c: (i % nb, 0))
    gs = pltpu.PrefetchScalarGridSpec(num_scalar_prefetch=1, grid=(2 * nb, 2),
                                      in_specs=[full, half, half, half, half, full, full], out_specs=(full,) * 4)
    sds = jax.ShapeDtypeStruct((2 * R, C), F32)
    (a0, b0), (a1, b1) = halves
    outs = pl.pallas_call(kern, name=name, out_shape=(sds,) * 4, grid_spec=gs,
                          compiler_params=pltpu.CompilerParams(dimension_semantics=("parallel", "parallel"),
                                                               vmem_limit_bytes=48 << 20))(
                              c_idx, w.reshape(2 * R, C), a0, b0, a1, b1, m.reshape(2 * R, C), v.reshape(2 * R, C))
    return tuple(t.reshape(2, R, C) for t in outs)


def _small_sum_adamw(gathered, w, m, v, *, name):
    R = w.shape[0]

    def kern(ga_ref, w_ref, m_ref, v_ref, g_ref, d_ref, mo_ref, vo_ref):
        g = ga_ref[0]
        for k in range(1, N_DEV):
            g = g + ga_ref[k]
        g_ref[...] = g
        d, mn, vn = _adamw_math(w_ref[...], g, m_ref[...], v_ref[...])
        d_ref[...] = d
        mo_ref[...] = mn
        vo_ref[...] = vn

    sds = jax.ShapeDtypeStruct((R, LANE), F32)
    return _pcall(kern, name=name, out_shape=(sds, sds, sds, sds), vmem_mb=48)(gathered, w, m, v)


def _pair_sum(g, theirs, c_idx, *, name):
    R, C = g.shape
    ch = C // 2
    tr = _row_tile(R, ch)

    def kern(c_ref, a_ref, b_ref, o_ref):
        o_ref[...] = (a_ref[...] + b_ref[...]).astype(BF16)

    gs = pltpu.PrefetchScalarGridSpec(
        num_scalar_prefetch=1, grid=(R // tr,),
        in_specs=[pl.BlockSpec((tr, ch), lambda i, c: (i, c[0])), pl.BlockSpec((tr, ch), lambda i, c: (i, 0))],
        out_specs=pl.BlockSpec((tr, ch), lambda i, c: (i, 0)))
    return pl.pallas_call(kern, name=name, out_shape=jax.ShapeDtypeStruct((R, ch), BF16), grid_spec=gs,
                          compiler_params=pltpu.CompilerParams(dimension_semantics=("parallel",),
                                                               vmem_limit_bytes=48 << 20))(c_idx, g, theirs)


WeightRows = collections.namedtuple("WeightRows", "full_rows own_rows cols pieces zero_rows")


def _w_in_piece_a(j):
    return jnp.where(j == 0, 0, 1232 * j + GAP)


def _w_in_piece_b(j):
    return jnp.where(j == 0, GAP_AT + GAP, 1232 * j + GAP_AT + GAP)


W_IN = WeightRows(NP, 1232, D_MODEL, ((0, GAP_AT, _w_in_piece_a), (GAP_AT, 1232 - GAP_AT, _w_in_piece_b)),
                  ((GAP_AT, GAP),))
W_OUT = WeightRows(2048, 512, D_MODEL, ((0, 512, lambda j: 512 * j),), ())
W_UQ = WeightRows(2048, 384, Q_LORA, ((0, 192, lambda j: 512 * j), (192, 192, lambda j: 512 * j + 256)),
                  tuple((256 * h + 192, 64) for h in range(N_HEADS)))
W_UKV = WeightRows(2048, 512, KV_LORA, ((0, 512, lambda j: 512 * j),), ())
W_CONV = WeightRows(64, 16, 256, ((0, 16, lambda j: 16 * j),), ())
SHARDED = (W_IN, W_OUT, W_UQ, W_UKV)


def _mesh_pos():
    x, y, c = lax.axis_index("x"), lax.axis_index("y"), lax.axis_index("c")
    return x, y, c


def _other_chips(x, y):
    return [(1 - x, y), (x, 1 - y), (1 - x, 1 - y)]


def _rows(start, n):
    return pl.ds(pl.multiple_of(start, 16), n)


def _half_cols(spec, c):
    ch = spec.cols // 2
    return pl.ds(pl.multiple_of(c * ch, LANE), ch)


def _allgather_script(specs, shards, zeros):
    na = len(specs)
    zlist = [a for a in range(na) if zeros[a] is not None]
    plan_first, plan_own, plan_zero = [], [], []
    for a, spec in enumerate(specs):
        for p in range(len(spec.pieces)):
            plan_own.append((a, p))
            for k in range(3):
                plan_first.append((a, p, k))
        for z in range(len(spec.zero_rows)):
            for l in range(shards[a].shape[0]):
                plan_zero.append((a, z, l))
    nf = len(plan_first)
    n_sems = 2 * nf + len(plan_own) + len(plan_zero)

    def copies(ins_all, outs, send_sems, recv_sems):
        ins = ins_all[:na]
        zrefs = dict(zip(zlist, ins_all[na:]))
        x, y, c = _mesh_pos()
        j = 2 * x + y
        chips = _other_chips(x, y)
        sibling = (x, y, 1 - c)

        def remote(src, dst, sem, to):
            return pltpu.make_async_remote_copy(src_ref=src, dst_ref=dst, send_sem=send_sems.at[sem],
                                                recv_sem=recv_sems.at[sem], device_id=to, device_id_type=MESH)

        def block(a, p, chip, cols):
            _, n, dst = specs[a].pieces[p]
            return outs[a].at[:, _rows(dst(chip), n), cols]

        def first(i):
            a, p, k = plan_first[i]
            src0, n, _ = specs[a].pieces[p]
            cols = _half_cols(specs[a], c)
            return remote(ins[a].at[:, pl.ds(src0, n), cols], block(a, p, j, cols), i, (*chips[k], c))

        def landed(i, half):
            a, p, k = plan_first[i]
            return block(a, p, 2 * chips[k][0] + chips[k][1], _half_cols(specs[a], half))

        def arrival(i, half, sem):
            return remote(landed(i, half), landed(i, half), sem, sibling)

        def passed(i):
            return remote(landed(i, c), landed(i, c), nf + i, sibling)

        def own(i):
            a, p = plan_own[i]
            src0, n, _ = specs[a].pieces[p]
            return remote(ins[a].at[:, pl.ds(src0, n), :], block(a, p, j, slice(None)), 2 * nf + i, sibling)

        def zero(i):
            a, z, l = plan_zero[i]
            r0, n = specs[a].zero_rows[z]
            return remote(zrefs[a].at[pl.ds(0, n), :], outs[a].at[l, pl.ds(r0, n), :],
                          2 * nf + len(plan_own) + i, sibling)

        fixed = [own(i) for i in range(len(plan_own))] + [zero(i) for i in range(len(plan_zero))]
        return c, fixed, first, arrival, passed

    def start(ins, outs, send_sems, recv_sems):
        _, fixed, first, _, _ = copies(ins, outs, send_sems, recv_sems)
        for cp in fixed:
            cp.start()
        for i in range(nf):
            first(i).start()

    def finish(ins, outs, send_sems, recv_sems):
        c, fixed, first, arrival, passed = copies(ins, outs, send_sems, recv_sems)
        for i in range(nf):
            arrival(i, c, i).wait_recv()
            passed(i).start()
        for i in range(nf):
            arrival(i, 1 - c, nf + i).wait_recv()
        for cp in fixed:
            cp.wait()
        for i in range(nf):
            first(i).wait_send()
            passed(i).wait_send()

    out_shape = tuple(jax.ShapeDtypeStruct((shards[a].shape[0], spec.full_rows, spec.cols), BF16)
                      for a, spec in enumerate(specs))
    args = tuple(shards) + tuple(zeros[a] for a in zlist)
    return CommScript(args, out_shape, n_sems, start, finish)


def _start_all_wait_all(args, out_shape, n_sems, make_copies):
    def start(ins, outs, send_sems, recv_sems):
        for cp in make_copies(ins, outs, send_sems, recv_sems):
            cp.start()

    def finish(ins, outs, send_sems, recv_sems):
        for cp in make_copies(ins, outs, send_sems, recv_sems):
            cp.wait()

    return CommScript(tuple(args), tuple(out_shape), n_sems, start, finish)


def _exchange_script(specs, grads):
    na = len(grads)

    def make_copies(ins, outs, send_sems, recv_sems):
        x, y, c = _mesh_pos()
        return [pltpu.make_async_remote_copy(
            src_ref=ins[a].at[:, _half_cols(specs[a], 1 - c)], dst_ref=outs[a], send_sem=send_sems.at[a],
            recv_sem=recv_sems.at[a], device_id=(x, y, 1 - c), device_id_type=MESH) for a in range(na)]

    out_shape = [jax.ShapeDtypeStruct((s.full_rows, s.cols // 2), F32) for s in specs]
    return _start_all_wait_all(grads, out_shape, na, make_copies)


def _scatter_script(specs, parts):
    na = len(parts)
    plan = [(a, p, k) for a in range(na) for p in range(len(specs[a].pieces)) for k in range(3)]

    def make_copies(ins, outs, send_sems, recv_sems):
        x, y, c = _mesh_pos()
        chips = _other_chips(x, y)
        copies = []
        for i, (a, p, k) in enumerate(plan):
            src0, n, dst = specs[a].pieces[p]
            pk = 2 * chips[k][0] + chips[k][1]
            copies.append(pltpu.make_async_remote_copy(
                src_ref=ins[a].at[_rows(dst(pk), n), :], dst_ref=outs[a].at[k, pl.ds(src0, n), :],
                send_sem=send_sems.at[i], recv_sem=recv_sems.at[i], device_id=(*chips[k], c), device_id_type=MESH))
        return copies

    out_shape = [jax.ShapeDtypeStruct((3, s.own_rows, s.cols // 2), BF16) for s in specs]
    return _start_all_wait_all(parts, out_shape, len(plan), make_copies)


def _chip_sum(spec, part, recv, *, name):
    ch = spec.cols // 2
    npieces = len(spec.pieces)

    def kern(recv_ref, part_ref, o_ref, own_ref, sems):
        j = 2 * lax.axis_index("x") + lax.axis_index("y")
        copies = []
        for p, (src0, n, dst) in enumerate(spec.pieces):
            copies.append(pltpu.make_async_copy(part_ref.at[_rows(dst(j), n), :], own_ref.at[pl.ds(src0, n), :],
                                                sems.at[p]))
        for cp in copies:
            cp.start()
        for cp in copies:
            cp.wait()
        o_ref[...] = ((own_ref[...].astype(F32) + recv_ref[0].astype(F32)) + recv_ref[1].astype(F32)) \
            + recv_ref[2].astype(F32)

    vm = pl.BlockSpec(memory_space=pltpu.VMEM)
    return _pcall(kern, name=name, out_shape=jax.ShapeDtypeStruct((spec.own_rows, ch), F32),
                  in_specs=[vm, HBM_SPEC], out_specs=vm,
                  scratch=[pltpu.VMEM((spec.own_rows, ch), BF16), pltpu.SemaphoreType.DMA((npieces,))],
                  vmem_mb=48)(recv, part)


def _sibling_script(sums):
    na = len(sums)

    def make_copies(ins, outs, send_sems, recv_sems):
        x, y, c = _mesh_pos()
        return [pltpu.make_async_remote_copy(
            src_ref=ins[a], dst_ref=outs[a], send_sem=send_sems.at[a], recv_sem=recv_sems.at[a],
            device_id=(x, y, 1 - c), device_id_type=MESH) for a in range(na)]

    out_shape = [jax.ShapeDtypeStruct(t.shape, t.dtype) for t in sums]
    return _start_all_wait_all(sums, out_shape, na, make_copies)


class _GradReducer:
    def __init__(self, layer, grads, c_idx):
        self.layer, self.grads, self.c_idx = layer, tuple(grads), c_idx
        self.names = [f"{nm}{layer}" for nm in ("w_in", "w_out", "w_uq", "w_ukv")]

    def exchange(self):
        return _exchange_script(SHARDED, self.grads)

    def scatter(self, theirs):
        self.parts = tuple(_pair_sum(g, th, self.c_idx, name=f"pair_sum_{nm}")
                           for g, th, nm in zip(self.grads, theirs, self.names))
        return _scatter_script(SHARDED, self.parts)

    def sibling(self, recv):
        self.sums = tuple(_chip_sum(s, p, r, name=f"chip_sum_{nm}")
                          for s, p, r, nm in zip(SHARDED, self.parts, recv, self.names))
        return _sibling_script(self.sums)

    def done(self, others):
        return list(zip(self.sums, others))


def _allgather_small(block, *, name):
    m_per, n = block.shape

    def body(x_ref, out_ref, send_sems, recv_sems, local_sem):
        x, y, c = _mesh_pos()
        me, sibling = (x, y, c), (x, y, 1 - c)
        chips = _other_chips(x, y)

        def rows(px, py, pc):
            return out_ref.at[4 * px + 2 * py + pc]

        def copy(k, blk, to, src=None):
            return pltpu.make_async_remote_copy(
                src_ref=rows(*blk) if src is None else src, dst_ref=rows(*blk), send_sem=send_sems.at[k],
                recv_sem=recv_sems.at[k], device_id=to, device_id_type=MESH)

        mine = pltpu.make_async_copy(x_ref, rows(*me), local_sem)
        mine.start()
        first = [copy(0, me, sibling, src=x_ref)]
        first += [copy(1 + k, me, (*chip, c), src=x_ref) for k, chip in enumerate(chips)]
        for cp in first:
            cp.start()
        passed = [copy(4 + k, (*chip, c), sibling) for k, chip in enumerate(chips)]
        for k, chip in enumerate(chips):
            copy(1 + k, (*chip, c), me).wait_recv()
            passed[k].start()
        copy(0, sibling, me).wait_recv()
        for k, chip in enumerate(chips):
            copy(4 + k, (*chip, 1 - c), me).wait_recv()
        for cp in first + passed:
            cp.wait_send()
        mine.wait()

    vm = pl.BlockSpec(memory_space=pltpu.VMEM)
    return _pcall(body, name=name, out_shape=jax.ShapeDtypeStruct((N_DEV, m_per, n), block.dtype),
                  in_specs=[vm], out_specs=vm,
                  scratch=[pltpu.SemaphoreType.DMA((7,)), pltpu.SemaphoreType.DMA((7,)), pltpu.SemaphoreType.DMA],
                  vmem_mb=48)(block)


def _rope_tables(positions):
    half = ROPE // 2
    inv_freq = ROPE_THETA ** (-jnp.arange(half, dtype=F32) / half)
    ang = positions.astype(F32)[:, None] * inv_freq
    cos, sin = jnp.cos(ang), jnp.sin(ang)
    S = positions.shape[0]
    cos_t = jnp.concatenate([cos, cos, jnp.ones((S, 64), F32)], axis=1)
    sin_t = jnp.concatenate([-sin, sin, jnp.zeros((S, 64), F32)], axis=1)
    return cos_t, sin_t


def _decode_conv(bits):
    rows = bits.reshape(DEPTH, N_CHIPS, 16, 256)[:, :, :3, :]
    conv = lax.bitcast_convert_type(rows.reshape(DEPTH, N_CHIPS, 3, 128, 2), F32)
    return jnp.transpose(conv, (0, 2, 1, 3)).reshape(DEPTH, 3, 512)


def _local_step(x, positions, target, emb_g, emb_b, w_in_t0, rest0, weights1, q_g, kv_g, w_pool, pool_scale,
                b_out, ln_g, ln_b, c_idx=None):
    cos_t, sin_t = _rope_tables(positions)
    if isinstance(w_in_t0, CommScript):
        (h, hb), (landed,) = _ln_fwd(x, emb_g, emb_b, name="emb_ln", comm=w_in_t0)
        w_in_t0 = landed[0]
    else:
        h, hb = _ln_fwd(x, emb_g, emb_b, name="emb_ln")
    weights = [None, weights1]
    saved = []
    for l in range(DEPTH):
        if l == 0 and isinstance(rest0, CommScript):
            proj, landed = _matmul(hb, w_in_t0, "nt", name="in_proj0", tm=1024, tn=1024, tk=2048, vmem_mb=56,
                                   comm=rest0)
            weights[0] = (w_in_t0,) + tuple(a[0] for a in landed[:3])
            conv_w = _decode_conv(landed[3])
        else:
            if l == 0:
                weights[0] = (w_in_t0,) + tuple(rest0[:3])
                conv_w = rest0[3]
            proj = _matmul(hb, weights[l][0], "nt", name=f"in_proj{l}", tm=1024, tn=1024, tk=2048, vmem_mb=56)
        w_in_t, w_out, w_uq_t, w_ukv_t = weights[l]
        qc, kc, v, vt, qn, kvn = _mla_qkv(proj, cos_t, sin_t, q_g[l], kv_g[l], w_uq_t, w_ukv_t, name=f"mla_qkv{l}")
        nxt = weights[l + 1] if l + 1 < DEPTH else None
        if isinstance(nxt, CommScript):
            (o, lse2), landed = _flash_fwd(qc, kc, vt, name=f"flash_fwd{l}", comm=nxt)
            weights[l + 1] = tuple(a[0] for a in landed)
        else:
            o, lse2 = _flash_fwd(qc, kc, vt, name=f"flash_fwd{l}")
        mix = _mixer_fwd(proj, o, w_pool[l], pool_scale[l], conv_w[l], name=f"mixer_fwd{l}")
        h_next, hb_next, r = _outproj_ln(mix, w_out, h, b_out[l], ln_g[l], ln_b[l], name=f"out_proj_ln{l}")
        saved.append((hb, proj, qc, kc, v, qn, kvn, o, lse2, mix, r))
        h, hb = h_next, hb_next

    y_final = h
    small = [None] * DEPTH
    big = [None] * DEPTH
    above = scatter_above = None
    for l in reversed(range(DEPTH)):
        w_in_t, w_out, w_uq_t, w_ukv_t = weights[l]
        hb_in, proj, qc, kc, v, qn, kvn, o, lse2, mix, r = saved[l]
        if l == DEPTH - 1:
            loss_acc, dr, drb, d_ln_g, d_ln_b, d_b_out = _loss_ln_bwd(y_final, target, r, ln_g[l], name="loss_ln_bwd")
        else:
            dr, drb, d_ln_g, d_ln_b, d_b_out = _ln_bwd(dh, r, ln_g[l], name=f"ln_bwd{l}")
        dmix = _matmul(drb, w_out, "nt", name=f"dmix{l}", tm=1024, tn=1024, tk=2048, vmem_mb=56)
        d_w_out = _matmul(mix, drb, "tn", name=f"dw_out{l}", tm=1024, tn=1024, tk=2048, vmem_mb=56)
        d_mix, do, d_w_pool, d_ps, d_conv = _mixer_bwd(dmix, proj, o, w_pool[l], pool_scale[l], conv_w[l],
                                                       name=f"mixer_bwd{l}")
        delta = _attn_delta(o, do, name=f"attn_delta{l}")
        if above is not None:
            (dqb, dkvb, dkr), recv = _flash_bwd(qc, kc, v, do, lse2, delta, cos_t, sin_t, name=f"flash_bwd{l}",
                                                comm=scatter_above)
            sibling_above = above.sibling(recv)
        else:
            dqb, dkvb, dkr = _flash_bwd(qc, kc, v, do, lse2, delta, cos_t, sin_t, name=f"flash_bwd{l}")
        d_mla, d_qg, d_kvg = _mla_qkv_bwd(dqb, dkvb, dkr, proj, cos_t, sin_t, q_g[l], kv_g[l], w_uq_t, w_ukv_t,
                                          name=f"mla_qkv_bwd{l}")
        d_w_uq_t = _matmul(dqb, qn, "tn", name=f"dw_uq{l}", tm=2048, tn=512, tk=2048, vmem_mb=56)
        d_w_ukv_t = _matmul(dkvb, kvn, "tn", name=f"dw_ukv{l}", tm=2048, tn=256, tk=2048, vmem_mb=56)
        if above is not None:
            d_w_in_t, others = _dproj_t_times_h(d_mla, d_mix, hb_in, name=f"dw_in{l}", comm=sibling_above)
            big[l + 1] = above.done(others)
            above = None
        else:
            d_w_in_t = _dproj_t_times_h(d_mla, d_mix, hb_in, name=f"dw_in{l}")
        big[l] = (d_w_in_t, d_w_out, d_w_uq_t, d_w_ukv_t)
        small[l] = dict(q_g=d_qg[0], kv_g=d_kvg[0], w_pool=d_w_pool, pool_scale=d_ps[0], conv_w=d_conv,
                        b_out=d_b_out[0], ln_g=d_ln_g[0], ln_b=d_ln_b[0])
        if c_idx is None:
            dh = _dproj_times_w(d_mla, d_mix, w_in_t, dr, ALPHA, name=f"dh{l}")
        elif l > 0:
            above = _GradReducer(l, big[l], c_idx)
            dh, theirs = _dproj_times_w(d_mla, d_mix, w_in_t, dr, ALPHA, name=f"dh{l}", comm=above.exchange())
            scatter_above = above.scatter(theirs)
        else:
            last = _GradReducer(l, big[l], c_idx)
            theirs = _run_comm(last.exchange(), name="exchange_halves0")
            dh, recv = _dproj_times_w(d_mla, d_mix, w_in_t, dr, ALPHA, name=f"dh{l}", comm=last.scatter(theirs))
    grad_x, _, d_emb_g, d_emb_b, _ = _ln_bwd(dh, x, emb_g, name="emb_ln_bwd")
    if c_idx is not None:
        big[0] = last.done(_run_comm(last.sibling(recv), name="send_to_sibling0"))
    return loss_acc[0, 0], grad_x, d_emb_g[0], d_emb_b[0], small, big


SMALL_ORDER = ("emb_ln_g", "emb_ln_b", "q_norm_g", "kv_norm_g", "w_pool", "pool_scale", "b_out", "ln_g", "ln_b")


def _pack_small(arrs, extra_rows):
    flat = jnp.concatenate([a.reshape(-1) for a in arrs])
    rows = flat.shape[0] // LANE
    total = -(-(rows + extra_rows) // 8) * 8
    return jnp.pad(flat, (0, total * LANE - flat.shape[0])).reshape(total, LANE)


def _unpack_small(packed, shapes):
    flat = packed.reshape(-1)
    out, off = [], 0
    for shp in shapes:
        n = 1
        for s in shp:
            n *= s
        out.append(flat[off:off + n].reshape(shp))
        off += n
    return out, off


def kernel(x, positions, emb_ln_g, emb_ln_b, w_in, q_norm_g, kv_norm_g, w_uq, w_ukv, w_pool, pool_scale, conv_w, w_out, b_out, ln_g, ln_b, loss_target, m_emb_ln_g, m_emb_ln_b, m_w_in, m_q_norm_g, m_kv_norm_g, m_w_uq, m_w_ukv, m_w_pool, m_pool_scale, m_conv_w, m_w_out, m_b_out, m_ln_g, m_ln_b, v_emb_ln_g, v_emb_ln_b, v_w_in, v_q_norm_g, v_kv_norm_g, v_w_uq, v_w_ukv, v_w_pool, v_pool_scale, v_conv_w, v_w_out, v_b_out, v_ln_g, v_ln_b):
    xi, yi, ci = lax.axis_index("x"), lax.axis_index("y"), lax.axis_index("c")
    chip = 2 * xi + yi
    c_idx = ci.reshape(1).astype(jnp.int32)

    def t(a):
        return jnp.swapaxes(a, 1, 2)

    conv_bits = lax.bitcast_convert_type(conv_w.reshape(DEPTH, 3 * 128), BF16).reshape(DEPTH, 3, 256)
    conv_bits = jnp.pad(conv_bits, ((0, 0), (0, 13), (0, 0)))
    own = (t(w_in).astype(BF16), w_out.astype(BF16), t(w_uq).astype(BF16), t(w_ukv).astype(BF16))
    zeros = (jnp.zeros((GAP, D_MODEL), BF16), None, jnp.zeros((64, Q_LORA), BF16), None)
    gather_in0 = _allgather_script((W_IN,), (own[0][0:1],), zeros[:1])
    gather0 = _allgather_script(SHARDED[1:] + (W_CONV,), tuple(a[0:1] for a in own[1:]) + (conv_bits,),
                                zeros[1:] + (None,))
    gather1 = _allgather_script(SHARDED, tuple(a[1:2] for a in own), zeros)

    loss_part, grad_x, d_emb_g, d_emb_b, grads, reduced = _local_step(
        x[0], positions[0], loss_target[0], emb_ln_g, emb_ln_b, gather_in0, gather0, gather1, q_norm_g, kv_norm_g,
        w_pool, pool_scale, b_out, ln_g, ln_b, c_idx)

    small_g = [d_emb_g, d_emb_b,
               jnp.stack([grads[l]["q_g"] for l in range(DEPTH)]), jnp.stack([grads[l]["kv_g"] for l in range(DEPTH)]),
               jnp.stack([grads[l]["w_pool"] for l in range(DEPTH)]),
               jnp.stack([grads[l]["pool_scale"] for l in range(DEPTH)]),
               jnp.stack([grads[l]["b_out"] for l in range(DEPTH)]), jnp.stack([grads[l]["ln_g"] for l in range(DEPTH)]),
               jnp.stack([grads[l]["ln_b"] for l in range(DEPTH)]),
               jnp.stack([grads[l]["conv_w"] for l in range(DEPTH)]),
               jnp.pad(loss_part.reshape(1), (0, LANE - 1))]
    small_w = [emb_ln_g, emb_ln_b, q_norm_g, kv_norm_g, w_pool, pool_scale, b_out, ln_g, ln_b]
    small_m = [m_emb_ln_g, m_emb_ln_b, m_q_norm_g, m_kv_norm_g, m_w_pool, m_pool_scale, m_b_out, m_ln_g, m_ln_b]
    small_v = [v_emb_ln_g, v_emb_ln_b, v_q_norm_g, v_kv_norm_g, v_w_pool, v_pool_scale, v_b_out, v_ln_g, v_ln_b]
    extra = (DEPTH * 3 * 512 + LANE) // LANE
    packed_g = _pack_small(small_g, 0)
    gathered = _allgather_small(packed_g, name="allgather_small")
    g_tot, d_small, m_small, v_small = _small_sum_adamw(
        gathered, _pack_small(small_w, extra), _pack_small(small_m, extra), _pack_small(small_v, extra),
        name="small_sum_adamw")
    shapes = [w.shape for w in small_w]
    g_list, off = _unpack_small(g_tot, shapes)
    d_list, _ = _unpack_small(d_small, shapes)
    m_list, _ = _unpack_small(m_small, shapes)
    v_list, _ = _unpack_small(v_small, shapes)
    flat_tot = g_tot.reshape(-1)
    conv_tot = flat_tot[off:off + DEPTH * 3 * 512].reshape(DEPTH, 3, 512)
    loss = flat_tot[off + DEPTH * 3 * 512]
    g_conv = lax.dynamic_slice_in_dim(conv_tot, chip * 128, 128, axis=2)

    def halves(a):
        return [reduced[l][a] for l in range(DEPTH)]

    def whole(a):
        return jnp.stack([jnp.where(ci == 0, jnp.concatenate([mine, oth], axis=1),
                                    jnp.concatenate([oth, mine], axis=1)) for mine, oth in halves(a)])

    upd = {}
    upd["w_in"] = tuple(t(o) for o in _adamw_halves(t(w_in), t(m_w_in), t(v_w_in), halves(0), c_idx,
                                                    name="adamw_w_in"))
    upd["w_out"] = _adamw_halves(w_out, m_w_out, v_w_out, halves(1), c_idx, name="adamw_w_out")
    g_uq, g_ukv = t(whole(2)), t(whole(3))
    upd["w_uq"] = (g_uq,) + _adamw(w_uq, g_uq, m_w_uq, v_w_uq, name="adamw_w_uq")
    upd["w_ukv"] = (g_ukv,) + _adamw(w_ukv, g_ukv, m_w_ukv, v_w_ukv, name="adamw_w_ukv")
    upd["conv_w"] = (g_conv,) + _adamw(conv_w, g_conv, m_conv_w, v_conv_w, name="adamw_conv_w")
    for i, nm in enumerate(SMALL_ORDER):
        upd[nm] = (g_list[i], d_list[i], m_list[i], v_list[i])

    order = ("emb_ln_g", "emb_ln_b", "w_in", "q_norm_g", "kv_norm_g", "w_uq", "w_ukv", "w_pool", "pool_scale",
             "conv_w", "w_out", "b_out", "ln_g", "ln_b")
    outs = [loss, grad_x[None]]
    for field in range(4):
        outs += [upd[nm][field] for nm in order]
    return tuple(outs)
```

```python
import collections

import jax
import jax.numpy as jnp
from jax import lax
from jax.experimental import pallas as pl
from jax.experimental.pallas import tpu as pltpu

F32 = jnp.float32
BF16 = jnp.bfloat16
MESH = pl.DeviceIdType.MESH

D_MODEL = 2048
DEPTH = 2
N_HEADS = 8
NOPE = 128
ROPE = 64
Q_LORA = 512
KV_LORA = 256
D_MLA = 1024
POOL_WINDOWS = (2, 4, 8, 16)
D_IN_PROJ = 4928
LN_EPS = 1e-5
RMS_EPS = 1e-6
ROPE_THETA = 10000.0
ALPHA = (2 * DEPTH) ** 0.25
SCALE = (NOPE + ROPE) ** -0.5
LOG2E = 1.4426950408889634
SCALE_LOG2E = SCALE * LOG2E
ADAM_LR = 0.001
ADAM_B1 = 0.9
ADAM_B2 = 0.999
ADAM_EPS = 1e-08
ADAM_WD = 0.01
ADAM_STEP = 10

NP = 5120
GAP_AT = 832
GAP = NP - D_IN_PROJ
W_MLA = 1024
W_MIX = NP - W_MLA
HALO = 16
LANE = 128
N_CHIPS = 4
N_DEV = 8
TQ = 512
FWD_GROUP = 4

NN = (((1,), (0,)), ((), ()))
NT = (((1,), (1,)), ((), ()))
TN = (((0,), (0,)), ((), ()))


CommScript = collections.namedtuple("CommScript", "args out_shape n_sems start finish")
HBM_SPEC = pl.BlockSpec(memory_space=pl.ANY)


def _pcall(kern, *, name, out_shape, grid=None, in_specs=None, out_specs=None, scratch=(), dims=None,
           vmem_mb=None, comm=None):
    cp = {}
    if dims is not None:
        cp["dimension_semantics"] = dims if comm is None else ("arbitrary",) * len(dims)
    if vmem_mb is not None:
        cp["vmem_limit_bytes"] = vmem_mb << 20
    if comm is None:
        args = dict(name=name, out_shape=out_shape, scratch_shapes=list(scratch),
                    compiler_params=pltpu.CompilerParams(**cp))
        if grid is not None:
            args["grid"] = grid
        if in_specs is not None:
            args["in_specs"] = in_specs
        if out_specs is not None:
            args["out_specs"] = out_specs
        return pl.pallas_call(kern, **args)

    single = not isinstance(out_shape, (tuple, list))
    own_out = (out_shape,) if single else tuple(out_shape)
    own_out_specs = (out_specs,) if single else tuple(out_specs)
    n_in, n_out, n_scr = len(in_specs), len(own_out), len(scratch)
    na, no = len(comm.args), len(comm.out_shape)

    def at(end):
        cond = None
        for d, n in enumerate(grid):
            here = pl.program_id(d) == (n - 1 if end else 0)
            cond = here if cond is None else jnp.logical_and(cond, here)
        return cond

    def wrapped(*refs):
        own_in, c_in = refs[:n_in], refs[n_in:n_in + na]
        o0 = n_in + na
        own_o, c_out = refs[o0:o0 + n_out], refs[o0 + n_out:o0 + n_out + no]
        s0 = o0 + n_out + no
        own_s, (send_sems, recv_sems) = refs[s0:s0 + n_scr], refs[s0 + n_scr:]

        @pl.when(at(False))
        def _():
            comm.start(c_in, c_out, send_sems, recv_sems)

        kern(*own_in, *own_o, *own_s)

        @pl.when(at(True))
        def _():
            comm.finish(c_in, c_out, send_sems, recv_sems)

    call = pl.pallas_call(
        wrapped, name=name, out_shape=own_out + tuple(comm.out_shape), grid=grid,
        in_specs=list(in_specs) + [HBM_SPEC] * na, out_specs=own_out_specs + (HBM_SPEC,) * no,
        scratch_shapes=list(scratch) + [pltpu.SemaphoreType.DMA((comm.n_sems,)),
                                        pltpu.SemaphoreType.DMA((comm.n_sems,))],
        compiler_params=pltpu.CompilerParams(**cp))

    def run(*args):
        res = call(*args, *comm.args)
        own = res[0] if single else tuple(res[:n_out])
        return own, tuple(res[n_out:])

    return run


def _run_comm(script, *, name):
    na, no = len(script.args), len(script.out_shape)

    def body(*refs):
        ins, outs = refs[:na], refs[na:na + no]
        send_sems, recv_sems = refs[na + no:]
        script.start(ins, outs, send_sems, recv_sems)
        script.finish(ins, outs, send_sems, recv_sems)

    return pl.pallas_call(
        body, name=name, out_shape=tuple(script.out_shape), in_specs=[HBM_SPEC] * na, out_specs=(HBM_SPEC,) * no,
        scratch_shapes=[pltpu.SemaphoreType.DMA((script.n_sems,)), pltpu.SemaphoreType.DMA((script.n_sems,))])(
            *script.args)


def _sigmoid(g):
    return 1.0 / (1.0 + jnp.exp(-g))


def _silu_and_grad(g):
    sig = _sigmoid(g)
    return g * sig, sig * (1.0 + g * (1.0 - sig))


def _matmul(a, b, mode, *, name, tm, tn, tk, out_dtype=F32, vmem_mb=48, comm=None):
    if mode == "nn":
        (M, K), N = a.shape, b.shape[1]
    elif mode == "nt":
        (M, K), N = a.shape, b.shape[0]
    else:
        (K, M), N = a.shape, b.shape[1]
    tm, tn, tk = min(tm, M), min(tn, N), min(tk, K)
    assert M % tm == 0 and N % tn == 0 and K % tk == 0, (name, M, N, K)
    nk = K // tk
    dn = {"nn": NN, "nt": NT, "tn": TN}[mode]
    if mode == "tn":
        a_spec = pl.BlockSpec((tk, tm), lambda i, j, k: (k, i))
    else:
        a_spec = pl.BlockSpec((tm, tk), lambda i, j, k: (i, k))
    if mode == "nt":
        b_spec = pl.BlockSpec((tn, tk), lambda i, j, k: (j, k))
    else:
        b_spec = pl.BlockSpec((tk, tn), lambda i, j, k: (k, j))
    o_spec = pl.BlockSpec((tm, tn), lambda i, j, k: (i, j))

    def kern(a_ref, b_ref, o_ref, *rest):
        part = lax.dot_general(a_ref[...].astype(BF16), b_ref[...].astype(BF16), dn,
                               preferred_element_type=F32)
        if nk == 1:
            o_ref[...] = part.astype(out_dtype)
        else:
            acc_ref = rest[0]
            k = pl.program_id(2)

            @pl.when(k == 0)
            def _():
                acc_ref[...] = part

            @pl.when(k > 0)
            def _():
                acc_ref[...] += part

            @pl.when(k == nk - 1)
            def _():
                o_ref[...] = acc_ref[...].astype(out_dtype)

    scratch = [pltpu.VMEM((tm, tn), F32)] if nk > 1 else []
    return _pcall(kern, name=name, out_shape=jax.ShapeDtypeStruct((M, N), out_dtype),
                  grid=(M // tm, N // tn, nk), in_specs=[a_spec, b_spec], out_specs=o_spec, scratch=scratch,
                  dims=("parallel", "parallel", "arbitrary"), vmem_mb=vmem_mb, comm=comm)(a, b)


def _dproj_times_w(d_mla, d_mix, wt, add, add_scale, *, name, comm=None):
    S = d_mla.shape[0]
    Dm = wt.shape[1]
    tm, tn, tk = min(1024, S), 1024, 2048
    nk = 1 + W_MIX // tk

    def kern(a1_ref, a2_ref, b1_ref, b2_ref, add_ref, o_ref, acc_ref):
        k = pl.program_id(2)

        @pl.when(k == 0)
        def _():
            acc_ref[...] = jnp.dot(a1_ref[...], b1_ref[...], preferred_element_type=F32)

        @pl.when(k > 0)
        def _():
            acc_ref[...] += jnp.dot(a2_ref[...], b2_ref[...], preferred_element_type=F32)

        @pl.when(k == nk - 1)
        def _():
            o_ref[...] = add_scale * add_ref[...] + acc_ref[...]

    o_spec = pl.BlockSpec((tm, tn), lambda i, j, k: (i, j))
    b2_spec = pl.BlockSpec((pl.Element(tk), pl.Element(tn)),
                           lambda i, j, k: (pl.multiple_of(W_MLA + tk * jnp.maximum(k - 1, 0), W_MLA),
                                            pl.multiple_of(j * tn, tn)))
    return _pcall(kern, name=name, out_shape=jax.ShapeDtypeStruct((S, Dm), F32), grid=(S // tm, Dm // tn, nk),
                  in_specs=[pl.BlockSpec((tm, W_MLA), lambda i, j, k: (i, 0)),
                            pl.BlockSpec((tm, tk), lambda i, j, k: (i, jnp.maximum(k - 1, 0))),
                            pl.BlockSpec((W_MLA, tn), lambda i, j, k: (0, j)), b2_spec, o_spec],
                  out_specs=o_spec, scratch=[pltpu.VMEM((tm, tn), F32)],
                  dims=("parallel", "parallel", "arbitrary"), vmem_mb=56, comm=comm)(d_mla, d_mix, wt, wt, add)


def _dproj_t_times_h(d_mla, d_mix, h, *, name, comm=None):
    S, Dm = h.shape
    tm, tn, tk = W_MLA, 1024, min(2048, S)
    nk = S // tk

    def kern(a1_ref, a2_ref, b_ref, o_ref, acc_ref):
        i = pl.program_id(0)
        k = pl.program_id(2)
        b = b_ref[...].astype(BF16)

        def accumulate(part):
            @pl.when(k == 0)
            def _():
                acc_ref[...] = part

            @pl.when(k > 0)
            def _():
                acc_ref[...] += part

        @pl.when(i == 0)
        def _():
            accumulate(lax.dot_general(a1_ref[...], b, TN, preferred_element_type=F32))

        @pl.when(i > 0)
        def _():
            accumulate(lax.dot_general(a2_ref[...], b, TN, preferred_element_type=F32))

        @pl.when(k == nk - 1)
        def _():
            o_ref[...] = acc_ref[...]

    return _pcall(kern, name=name, out_shape=jax.ShapeDtypeStruct((NP, Dm), F32), grid=(NP // tm, Dm // tn, nk),
                  in_specs=[pl.BlockSpec((tk, tm), lambda i, j, k: (jnp.where(i == 0, k, nk - 1), 0)),
                            pl.BlockSpec((tk, tm), lambda i, j, k: (jnp.where(i == 0, 0, k), jnp.maximum(i - 1, 0))),
                            pl.BlockSpec((tk, tn), lambda i, j, k: (k, j))],
                  out_specs=pl.BlockSpec((tm, tn), lambda i, j, k: (i, j)), scratch=[pltpu.VMEM((tm, tn), F32)],
                  dims=("parallel", "parallel", "arbitrary"), vmem_mb=48, comm=comm)(d_mla, d_mix, h)


def _ln_fwd(x, g, b, *, name, comm=None):
    S, Dm = x.shape
    tm = min(512, S)

    def kern(x_ref, g_ref, b_ref, y_ref, yb_ref):
        xf = x_ref[...]
        mu = jnp.mean(xf, axis=-1, keepdims=True)
        xc = xf - mu
        var = jnp.mean(xc * xc, axis=-1, keepdims=True)
        y = xc * lax.rsqrt(var + LN_EPS) * g_ref[...] + b_ref[...]
        y_ref[...] = y
        yb_ref[...] = y.astype(BF16)

    row = pl.BlockSpec((tm, Dm), lambda i: (i, 0))
    vec = pl.BlockSpec((1, Dm), lambda i: (0, 0))
    return _pcall(kern, name=name,
                  out_shape=(jax.ShapeDtypeStruct((S, Dm), F32), jax.ShapeDtypeStruct((S, Dm), BF16)),
                  grid=(S // tm,), in_specs=[row, vec, vec], out_specs=(row, row), dims=("parallel",), vmem_mb=48,
                  comm=comm)(
                      x, g.reshape(1, Dm), b.reshape(1, Dm))


def _ln_bwd(dy, r, g, *, name):
    S, Dm = r.shape
    tm = min(512, S)

    def kern(dy_ref, r_ref, g_ref, dr_ref, drb_ref, dg_ref, db_ref, ds_ref):
        @pl.when(pl.program_id(0) == 0)
        def _():
            dg_ref[...] = jnp.zeros_like(dg_ref)
            db_ref[...] = jnp.zeros_like(db_ref)
            ds_ref[...] = jnp.zeros_like(ds_ref)

        rf = r_ref[...]
        dyf = dy_ref[...]
        mu = jnp.mean(rf, axis=-1, keepdims=True)
        xc = rf - mu
        var = jnp.mean(xc * xc, axis=-1, keepdims=True)
        rstd = lax.rsqrt(var + LN_EPS)
        xhat = xc * rstd
        dxh = dyf * g_ref[...]
        c1 = jnp.mean(dxh, axis=-1, keepdims=True)
        c2 = jnp.mean(dxh * xhat, axis=-1, keepdims=True)
        dr = rstd * (dxh - c1 - xhat * c2)
        dr_ref[...] = dr
        drb_ref[...] = dr.astype(BF16)
        dg_ref[...] += jnp.sum(dyf * xhat, axis=0, keepdims=True)
        db_ref[...] += jnp.sum(dyf, axis=0, keepdims=True)
        ds_ref[...] += jnp.sum(dr, axis=0, keepdims=True)

    row = pl.BlockSpec((tm, Dm), lambda i: (i, 0))
    vec = pl.BlockSpec((1, Dm), lambda i: (0, 0))
    vshape = jax.ShapeDtypeStruct((1, Dm), F32)
    return _pcall(kern, name=name,
                  out_shape=(jax.ShapeDtypeStruct((S, Dm), F32), jax.ShapeDtypeStruct((S, Dm), BF16),
                             vshape, vshape, vshape),
                  grid=(S // tm,), in_specs=[row, row, vec], out_specs=(row, row, vec, vec, vec),
                  dims=("arbitrary",), vmem_mb=48)(dy, r, g.reshape(1, Dm))


def _loss_ln_bwd(y, target, r, g, *, name):
    S, Dm = r.shape
    tm = min(512, S)

    def kern(y_ref, t_ref, r_ref, g_ref, l_ref, dr_ref, drb_ref, dg_ref, db_ref, ds_ref):
        @pl.when(pl.program_id(0) == 0)
        def _():
            l_ref[...] = jnp.zeros_like(l_ref)
            dg_ref[...] = jnp.zeros_like(dg_ref)
            db_ref[...] = jnp.zeros_like(db_ref)
            ds_ref[...] = jnp.zeros_like(ds_ref)

        e = y_ref[...] - t_ref[...]
        dyf = e / float(Dm)
        per_row = jnp.mean(e * e, axis=-1, keepdims=True)
        l_ref[...] += 0.5 * jnp.sum(per_row, axis=0, keepdims=True)
        rf = r_ref[...]
        mu = jnp.mean(rf, axis=-1, keepdims=True)
        xc = rf - mu
        var = jnp.mean(xc * xc, axis=-1, keepdims=True)
        rstd = lax.rsqrt(var + LN_EPS)
        xhat = xc * rstd
        dxh = dyf * g_ref[...]
        c1 = jnp.mean(dxh, axis=-1, keepdims=True)
        c2 = jnp.mean(dxh * xhat, axis=-1, keepdims=True)
        dr = rstd * (dxh - c1 - xhat * c2)
        dr_ref[...] = dr
        drb_ref[...] = dr.astype(BF16)
        dg_ref[...] += jnp.sum(dyf * xhat, axis=0, keepdims=True)
        db_ref[...] += jnp.sum(dyf, axis=0, keepdims=True)
        ds_ref[...] += jnp.sum(dr, axis=0, keepdims=True)

    row = pl.BlockSpec((tm, Dm), lambda i: (i, 0))
    vec = pl.BlockSpec((1, Dm), lambda i: (0, 0))
    acc = pl.BlockSpec((8, LANE), lambda i: (0, 0))
    vshape = jax.ShapeDtypeStruct((1, Dm), F32)
    return _pcall(kern, name=name,
                  out_shape=(jax.ShapeDtypeStruct((8, LANE), F32), jax.ShapeDtypeStruct((S, Dm), F32),
                             jax.ShapeDtypeStruct((S, Dm), BF16), vshape, vshape, vshape),
                  grid=(S // tm,), in_specs=[row, row, row, vec], out_specs=(acc, row, row, vec, vec, vec),
                  dims=("arbitrary",), vmem_mb=56)(y, target, r, g.reshape(1, Dm))


def _rot_sum(t):
    return pltpu.roll(t, 32, 1) + pltpu.roll(t, 96, 1)


def _mla_qkv(proj, cos_t, sin_t, qg, kvg, wuq_t, wukv_t, *, name):
    S = proj.shape[0]
    tm = min(256, S)

    def kern(ql_ref, kvl_ref, kr_ref, cos_ref, sin_ref, qg_ref, kvg_ref, wuq_ref, wukv_ref,
             qc_ref, kc_ref, v_ref, vt_ref, qn_ref, kvn_ref):
        cosv = cos_ref[...]
        sinv = sin_ref[...]

        def rope(t):
            return t * cosv + _rot_sum(t) * sinv

        ql = ql_ref[...]
        qn = (ql * lax.rsqrt(jnp.mean(ql * ql, axis=-1, keepdims=True) + RMS_EPS) * qg_ref[...]).astype(BF16)
        kvl = kvl_ref[...]
        kvn = (kvl * lax.rsqrt(jnp.mean(kvl * kvl, axis=-1, keepdims=True) + RMS_EPS) * kvg_ref[...]).astype(BF16)
        qn_ref[...] = qn
        kvn_ref[...] = kvn
        q = lax.dot_general(qn, wuq_ref[...], NT, preferred_element_type=F32)
        kv = lax.dot_general(kvn, wukv_ref[...], NT, preferred_element_type=F32)
        kr = rope(kr_ref[...]).astype(BF16)
        for h in range(N_HEADS):
            c0 = 256 * h
            qc_ref[:, c0:c0 + 128] = q[:, c0:c0 + 128].astype(BF16)
            qc_ref[:, c0 + 128:c0 + 256] = rope(q[:, c0 + 128:c0 + 256]).astype(BF16)
            kc_ref[:, c0:c0 + 128] = kv[:, c0:c0 + 128].astype(BF16)
            kc_ref[:, c0 + 128:c0 + 256] = kr
            vh = kv[:, c0 + 128:c0 + 256]
            v_ref[:, 128 * h:128 * h + 128] = vh.astype(BF16)
            vt_ref[h] = jnp.transpose(vh).astype(BF16)

    def row(w, blk):
        return pl.BlockSpec((tm, w), lambda i: (i, blk))

    def full(shape):
        return pl.BlockSpec(shape, lambda i: (0,) * len(shape))

    t = min(TQ, S)
    per = t // tm
    vt_spec = pl.BlockSpec((N_HEADS, None, 128, tm), lambda i: (0, i // per, 0, i % per))
    outs = (jax.ShapeDtypeStruct((S, 2048), BF16), jax.ShapeDtypeStruct((S, 2048), BF16),
            jax.ShapeDtypeStruct((S, 1024), BF16), jax.ShapeDtypeStruct((N_HEADS, S // t, 128, t), BF16),
            jax.ShapeDtypeStruct((S, Q_LORA), BF16), jax.ShapeDtypeStruct((S, KV_LORA), BF16))
    return _pcall(kern, name=name, out_shape=outs, grid=(S // tm,),
                  in_specs=[row(512, 0), row(256, 2), row(128, 6), row(128, 0), row(128, 0),
                            full((1, Q_LORA)), full((1, KV_LORA)), full((2048, Q_LORA)), full((2048, KV_LORA))],
                  out_specs=(row(2048, 0), row(2048, 0), row(1024, 0), vt_spec, row(512, 0), row(256, 0)),
                  dims=("parallel",), vmem_mb=48)(
                      proj, proj, proj, cos_t, sin_t, qg.reshape(1, -1), kvg.reshape(1, -1), wuq_t, wukv_t)


def _mla_qkv_bwd(dqb, dkvb, dkr_heads, proj, cos_t, sin_t, qg, kvg, wuq_t, wukv_t, *, name):
    S = proj.shape[0]
    tm = min(256, S)

    def kern(dqb_ref, dkvb_ref, dkrh_ref, ql_ref, kvl_ref, cos_ref, sin_ref, qg_ref, kvg_ref, wuq_ref, wukv_ref,
             dml_ref, dqg_ref, dkvg_ref):
        @pl.when(pl.program_id(0) == 0)
        def _():
            dqg_ref[...] = jnp.zeros_like(dqg_ref)
            dkvg_ref[...] = jnp.zeros_like(dkvg_ref)

        cosv = cos_ref[...]
        sinv = sin_ref[...]

        def unrope(t):
            return t * cosv - _rot_sum(t) * sinv

        dkr = dkrh_ref[:, 0:128]
        for h in range(1, N_HEADS):
            dkr = dkr + dkrh_ref[:, 128 * h:128 * h + 128]

        def rms_bwd(x, g, dy):
            n = x.shape[-1]
            rs = lax.rsqrt(jnp.mean(x * x, axis=-1, keepdims=True) + RMS_EPS)
            dyg = dy * g
            dx = rs * dyg - x * (rs * rs * rs) * (jnp.sum(dyg * x, axis=-1, keepdims=True) / n)
            return dx, jnp.sum(dy * (x * rs), axis=0, keepdims=True)

        dqn = jnp.dot(dqb_ref[...], wuq_ref[...], preferred_element_type=F32)
        dql, dqg = rms_bwd(ql_ref[...], qg_ref[...], dqn)
        dqg_ref[...] += dqg
        dkvn = jnp.dot(dkvb_ref[...], wukv_ref[...], preferred_element_type=F32)
        dkvl, dkvg = rms_bwd(kvl_ref[...], kvg_ref[...], dkvn)
        dkvg_ref[...] += dkvg
        dml_ref[:, 0:512] = dql.astype(BF16)
        dml_ref[:, 512:768] = dkvl.astype(BF16)
        dml_ref[:, 768:896] = unrope(dkr).astype(BF16)
        dml_ref[:, 896:1024] = jnp.zeros((tm, 128), BF16)

    def row(w, blk):
        return pl.BlockSpec((tm, w), lambda i: (i, blk))

    def full(shape):
        return pl.BlockSpec(shape, lambda i: (0,) * len(shape))

    outs = (jax.ShapeDtypeStruct((S, W_MLA), BF16), jax.ShapeDtypeStruct((1, Q_LORA), F32),
            jax.ShapeDtypeStruct((1, KV_LORA), F32))
    return _pcall(kern, name=name, out_shape=outs, grid=(S // tm,),
                  in_specs=[row(2048, 0), row(2048, 0), row(1024, 0), row(512, 0), row(256, 2),
                            row(128, 0), row(128, 0), full((1, Q_LORA)), full((1, KV_LORA)),
                            full((2048, Q_LORA)), full((2048, KV_LORA))],
                  out_specs=(row(W_MLA, 0), full((1, Q_LORA)), full((1, KV_LORA))),
                  dims=("arbitrary",), vmem_mb=56)(
                      dqb, dkvb, dkr_heads, proj, proj, cos_t, sin_t, qg.reshape(1, -1), kvg.reshape(1, -1),
                      wuq_t, wukv_t)


def _kq_mask(t):
    krow = lax.broadcasted_iota(jnp.int32, (t, t), 0)
    qcol = lax.broadcasted_iota(jnp.int32, (t, t), 1)
    return krow <= qcol


def _flash_fwd(qc, kc, vt, *, name, comm=None):
    S = qc.shape[0]
    t = min(TQ, S)
    n = S // t

    def kern(q_ref, k_ref, vt_ref, o_ref, lse_ref, m_s, l_s, acc_s):
        qi = pl.program_id(1)
        m_s[...] = jnp.full_like(m_s, -jnp.inf)
        l_s[...] = jnp.zeros_like(l_s)
        acc_s[...] = jnp.zeros_like(acc_s)

        def scores(kb):
            k0 = pl.multiple_of(kb * t, t)
            return lax.dot_general(k_ref[pl.ds(k0, t), :], q_ref[...], NT, preferred_element_type=F32)

        def update(kb, st, masked):
            if masked:
                st = jnp.where(_kq_mask(t), st, -jnp.inf)
            m_prev = m_s[...]
            m_new = jnp.maximum(m_prev, jnp.max(st, axis=0, keepdims=True))
            a = jnp.exp2((m_prev - m_new) * SCALE_LOG2E)
            pt = jnp.exp2((st - m_new) * SCALE_LOG2E)
            l_s[...] = a * l_s[...] + jnp.sum(pt, axis=0, keepdims=True)
            acc_s[...] = a * acc_s[...] + jnp.dot(vt_ref[kb], pt.astype(BF16), preferred_element_type=F32)
            m_s[...] = m_new

        def group(kb, count, last_masked):
            sts = [scores(kb + g) for g in range(count)]
            for g in range(count):
                update(kb + g, sts[g], last_masked and g == count - 1)

        def body(i, carry):
            group(FWD_GROUP * i, FWD_GROUP, False)
            return carry

        full = qi // FWD_GROUP
        lax.fori_loop(0, full, body, 0)
        for rem in range(FWD_GROUP):
            @pl.when(qi - FWD_GROUP * full == rem)
            def _():
                group(qi - rem, rem + 1, True)
        o_ref[...] = jnp.transpose(acc_s[...] / l_s[...])
        lse_ref[pl.ds(qi, 1), :] = m_s[...] * SCALE_LOG2E + jnp.log2(l_s[...])

    q_spec = pl.BlockSpec((t, 256), lambda h, qi: (qi, h))
    k_spec = pl.BlockSpec((S, 256), lambda h, qi: (0, h))
    vt_spec = pl.BlockSpec((None, n, 128, t), lambda h, qi: (h, 0, 0, 0))
    o_spec = pl.BlockSpec((t, 128), lambda h, qi: (qi, h))
    lse_spec = pl.BlockSpec((None, n, t), lambda h, qi: (h, 0, 0))
    return _pcall(kern, name=name,
                  out_shape=(jax.ShapeDtypeStruct((S, D_MLA), F32), jax.ShapeDtypeStruct((N_HEADS, n, t), F32)),
                  grid=(N_HEADS, n), in_specs=[q_spec, k_spec, vt_spec], out_specs=(o_spec, lse_spec),
                  scratch=[pltpu.VMEM((1, t), F32), pltpu.VMEM((1, t), F32), pltpu.VMEM((128, t), F32)],
                  dims=("parallel", "arbitrary"), vmem_mb=48, comm=comm)(qc, kc, vt)


def _attn_delta(o, do, *, name):
    S = o.shape[0]
    t = min(TQ, S)
    n = S // t

    def kern(o_ref, do_ref, dl_ref):
        i = pl.program_id(0)
        prod = o_ref[...] * do_ref[...]
        lane = lax.broadcasted_iota(jnp.int32, (t, LANE), 1)
        dmat = jnp.zeros((t, LANE), F32)
        for h in range(N_HEADS):
            dmat = jnp.where(lane == h, jnp.sum(prod[:, 128 * h:128 * h + 128], axis=1, keepdims=True), dmat)
        dmat_t = jnp.transpose(dmat)
        for h in range(N_HEADS):
            dl_ref[h, pl.ds(i, 1), :] = dmat_t[h:h + 1, :]

    row = pl.BlockSpec((t, D_MLA), lambda i: (i, 0))
    return _pcall(kern, name=name, out_shape=jax.ShapeDtypeStruct((N_HEADS, n, t), F32), grid=(n,),
                  in_specs=[row, row], out_specs=pl.BlockSpec((N_HEADS, n, t), lambda i: (0, 0, 0)),
                  dims=("arbitrary",), vmem_mb=48)(o, do)


def _flash_bwd(qc, kc, v, do, lse2, delta, cos_t, sin_t, *, name, comm=None):
    S = qc.shape[0]
    t = min(TQ, S)
    n = S // t

    def kern(q_ref, k_ref, v_ref, do_ref, lse_ref, dl_ref, cos_ref, sin_ref, dqb_ref, dkvb_ref, dkr_ref,
             dq_ref, dk_ref, dv_ref):
        ki = pl.program_id(1)

        @pl.when(ki == 0)
        def _():
            dq_ref[...] = jnp.zeros_like(dq_ref)

        dk_ref[...] = jnp.zeros_like(dk_ref)
        dv_ref[...] = jnp.zeros_like(dv_ref)

        def step(qb, masked):
            q0 = pl.multiple_of(qb * t, t)
            kt = k_ref[...]
            qblk = q_ref[pl.ds(q0, t), :]
            dob = do_ref[pl.ds(q0, t), :].astype(BF16)
            st = lax.dot_general(kt, qblk, NT, preferred_element_type=F32)
            pt = jnp.exp2(st * SCALE_LOG2E - lse_ref[pl.ds(qb, 1), :])
            if masked:
                pt = jnp.where(_kq_mask(t), pt, 0.0)
            dv_ref[...] += jnp.dot(pt.astype(BF16), dob, preferred_element_type=F32)
            dpt = lax.dot_general(v_ref[...], dob, NT, preferred_element_type=F32)
            dst = (pt * (dpt - dl_ref[pl.ds(qb, 1), :]) * SCALE).astype(BF16)
            dk_ref[...] += jnp.dot(dst, qblk, preferred_element_type=F32)
            dq_ref[pl.ds(q0, t), :] += lax.dot_general(dst, kt, TN, preferred_element_type=F32)

        step(ki, True)
        rest = n - 1 - ki

        def body(i, carry):
            step(ki + 1 + 2 * i, False)
            step(ki + 2 + 2 * i, False)
            return carry

        lax.fori_loop(0, rest // 2, body, 0)

        @pl.when(rest % 2 == 1)
        def _():
            step(n - 1, False)

        dkvb_ref[:, 0:128] = dk_ref[:, 0:128].astype(BF16)
        dkvb_ref[:, 128:256] = dv_ref[...].astype(BF16)
        dkr_ref[...] = dk_ref[:, 128:256]

        @pl.when(ki == n - 1)
        def _():
            dqb_ref[:, 0:128] = dq_ref[:, 0:128].astype(BF16)
            dqr = dq_ref[:, 128:256]
            dqb_ref[:, 128:256] = (dqr * cos_ref[...] - _rot_sum(dqr) * sin_ref[...]).astype(BF16)

    def whole(w):
        return pl.BlockSpec((S, w), lambda h, ki: (0, h))

    def krow(w):
        return pl.BlockSpec((t, w), lambda h, ki: (ki, h))

    stat = pl.BlockSpec((None, n, t), lambda h, ki: (h, 0, 0))
    table = pl.BlockSpec((S, 128), lambda h, ki: (0, 0))
    return _pcall(kern, name=name,
                  out_shape=(jax.ShapeDtypeStruct((S, 2048), BF16), jax.ShapeDtypeStruct((S, 2048), BF16),
                             jax.ShapeDtypeStruct((S, D_MLA), F32)),
                  grid=(N_HEADS, n),
                  in_specs=[whole(256), krow(256), krow(128), whole(128), stat, stat, table, table],
                  out_specs=(whole(256), krow(256), krow(128)),
                  scratch=[pltpu.VMEM((S, 256), F32), pltpu.VMEM((t, 256), F32), pltpu.VMEM((t, 128), F32)],
                  dims=("parallel", "arbitrary"), vmem_mb=56, comm=comm)(qc, kc, v, do, lse2, delta, cos_t, sin_t)


def _mixer_specs(S, tm):
    hb = tm // HALO
    last_hb = S // HALO - 1

    def main(w, blk):
        return pl.BlockSpec((tm, w), lambda i: (i, blk))

    def prev(w, blk):
        return pl.BlockSpec((HALO, w), lambda i: (jnp.maximum(i * hb - 1, 0), blk))

    def nxt(w, blk):
        return pl.BlockSpec((HALO, w), lambda i: (jnp.minimum((i + 1) * hb, last_hb), blk))

    def full(shape):
        return pl.BlockSpec(shape, lambda i: (0,) * len(shape))

    return main, prev, nxt, full


def _fill_halo(i, xp, xu, hp_ref, hch_ref, hcc_ref, pin_ref, ch_ref, cc_ref, tm):
    first = i == 0
    xp[0:HALO, :] = jnp.where(first, 0.0, hp_ref[...])
    xp[HALO:HALO + tm, :] = pin_ref[...]
    xu[0:HALO, :] = jnp.where(first, 0.0, hch_ref[...] * hcc_ref[...])
    xu[HALO:HALO + tm, :] = cc_ref[...] * ch_ref[...]


def _pooled(xp, g, t1, tm):
    w = POOL_WINDOWS[g]
    lanes = slice(128 * g, 128 * g + 128)
    x0 = xp[HALO:HALO + tm, lanes]
    acc = x0
    for k in range(1, w):
        acc = acc + xp[HALO - k:HALO - k + tm, lanes]
    return acc / jnp.minimum(t1, float(w)) - x0


def _conv_fwd(xu, cw_ref, tm):
    return (cw_ref[0:1, :] * xu[HALO - 2:HALO - 2 + tm, :] + cw_ref[1:2, :] * xu[HALO - 1:HALO - 1 + tm, :]
            + cw_ref[2:3, :] * xu[HALO:HALO + tm, :])


def _mixer_fwd(proj, o, wpool, ps, convw, *, name):
    S = proj.shape[0]
    tm = min(256, S)
    main, prev, _, full = _mixer_specs(S, tm)

    def kern(gm_ref, pin_ref, gp_ref, ch_ref, cb_ref, cc_ref, gc_ref, hp_ref, hch_ref, hcc_ref,
             o_ref, wp_ref, ps_ref, cw_ref, mix_ref, xp, xu):
        i = pl.program_id(0)
        _fill_halo(i, xp, xu, hp_ref, hch_ref, hcc_ref, pin_ref, ch_ref, cc_ref, tm)
        t1 = (i * tm + lax.broadcasted_iota(jnp.int32, (tm, 1), 0) + 1).astype(F32)
        for g in range(4):
            lanes = slice(128 * g, 128 * g + 128)
            pooled = _pooled(xp, g, t1, tm)
            z = jnp.dot(pooled.astype(BF16), wp_ref[g].astype(BF16), preferred_element_type=F32)
            gp = gp_ref[:, lanes]
            y = z * ps_ref[:, lanes] * (gp * _sigmoid(gp))
            mix_ref[:, 1024 + 128 * g:1024 + 128 * g + 128] = y.astype(BF16)
        gc = gc_ref[...]
        mix_ref[:, 1536:2048] = (cb_ref[...] * _conv_fwd(xu, cw_ref, tm) * (gc * _sigmoid(gc))).astype(BF16)
        gm = gm_ref[...]
        mix_ref[:, 0:1024] = (o_ref[...] * (gm * _sigmoid(gm))).astype(BF16)

    return _pcall(kern, name=name, out_shape=jax.ShapeDtypeStruct((S, 2048), BF16), grid=(S // tm,),
                  in_specs=[main(1024, 1), main(512, 4), main(512, 5), main(512, 6), main(512, 7), main(512, 8),
                            main(512, 9), prev(512, 4), prev(512, 6), prev(512, 8),
                            main(1024, 0), full((4, 128, 128)), full((1, 512)), full((3, 512))],
                  out_specs=main(2048, 0),
                  scratch=[pltpu.VMEM((tm + HALO, 512), F32), pltpu.VMEM((tm + HALO, 512), F32)],
                  dims=("parallel",), vmem_mb=48)(
                      proj, proj, proj, proj, proj, proj, proj, proj, proj, proj, o, wpool, ps.reshape(1, 512), convw)


def _mixer_bwd(dmix, proj, o, wpool, ps, convw, *, name):
    S = proj.shape[0]
    tm = min(256, S)
    n = S // tm
    main, prev, nxt, full = _mixer_specs(S, tm)

    def kern(dm_ref, dmn_ref, gm_ref, pin_ref, gp_ref, ch_ref, cb_ref, cc_ref, gc_ref,
             hp_ref, hch_ref, hcc_ref, gpn_ref, cbn_ref, gcn_ref, o_ref, wp_ref, ps_ref, cw_ref,
             d_ref, do_ref, dwp_ref, dps_ref, dcw_ref, xp, xu, ee, ed):
        i = pl.program_id(0)
        last = i == n - 1

        @pl.when(i == 0)
        def _():
            dwp_ref[...] = jnp.zeros_like(dwp_ref)
            dps_ref[...] = jnp.zeros_like(dps_ref)
            dcw_ref[...] = jnp.zeros_like(dcw_ref)

        _fill_halo(i, xp, xu, hp_ref, hch_ref, hcc_ref, pin_ref, ch_ref, cc_ref, tm)
        t1 = (i * tm + lax.broadcasted_iota(jnp.int32, (tm, 1), 0) + 1).astype(F32)
        t1n = ((i + 1) * tm + lax.broadcasted_iota(jnp.int32, (HALO, 1), 0) + 1).astype(F32)
        c_pin, c_gp, c_ch, c_cb, c_cc, c_gc = 1024, 1536, 2048, 2560, 3072, 3584

        for g in range(4):
            w = float(POOL_WINDOWS[g])
            lanes = slice(128 * g, 128 * g + 128)
            pooled = _pooled(xp, g, t1, tm)
            pb = pooled.astype(BF16)
            wp = wp_ref[g].astype(BF16)
            z = jnp.dot(pb, wp, preferred_element_type=F32)
            psl = ps_ref[:, lanes]
            sg, dsg = _silu_and_grad(gp_ref[:, lanes])
            dmp = dm_ref[:, 1024 + 128 * g:1024 + 128 * g + 128]
            dyp = dmp * sg
            d_ref[:, c_gp + 128 * g:c_gp + 128 * g + 128] = (dmp * (z * psl) * dsg).astype(BF16)
            dps_ref[:, lanes] += jnp.sum(dyp * z, axis=0, keepdims=True)
            dz = (dyp * psl).astype(BF16)
            dwp_ref[g] += lax.dot_general(pb, dz, TN, preferred_element_type=F32)
            dpl = lax.dot_general(dz, wp, NT, preferred_element_type=F32)
            ee[0:tm, lanes] = dpl / jnp.minimum(t1, w)
            gpn = gpn_ref[:, lanes]
            dzn = (dmn_ref[:, lanes] * (gpn * _sigmoid(gpn)) * psl).astype(BF16)
            dpn = lax.dot_general(dzn, wp, NT, preferred_element_type=F32)
            ee[tm:tm + HALO, lanes] = jnp.where(last, 0.0, dpn / jnp.minimum(t1n, w))
            acc = ee[0:tm, lanes]
            for k in range(1, POOL_WINDOWS[g]):
                acc = acc + ee[k:k + tm, lanes]
            d_ref[:, c_pin + 128 * g:c_pin + 128 * g + 128] = (acc - dpl).astype(BF16)

        yc = _conv_fwd(xu, cw_ref, tm)
        sgc, dsgc = _silu_and_grad(gc_ref[...])
        cb = cb_ref[...]
        dmc = dm_ref[:, 1536:2048]
        d_ref[:, c_gc:c_gc + 512] = (dmc * cb * yc * dsgc).astype(BF16)
        d_ref[:, c_cb:c_cb + 512] = (dmc * yc * sgc).astype(BF16)
        dyc = dmc * cb * sgc
        ed[0:tm, :] = dyc
        gcn = gcn_ref[...]
        ed[tm:tm + HALO, :] = jnp.where(last, 0.0, dmn_ref[:, 512:1024] * cbn_ref[...] * (gcn * _sigmoid(gcn)))
        dcw_ref[0:1, :] += jnp.sum(dyc * xu[HALO - 2:HALO - 2 + tm, :], axis=0, keepdims=True)
        dcw_ref[1:2, :] += jnp.sum(dyc * xu[HALO - 1:HALO - 1 + tm, :], axis=0, keepdims=True)
        dcw_ref[2:3, :] += jnp.sum(dyc * xu[HALO:HALO + tm, :], axis=0, keepdims=True)
        du = cw_ref[2:3, :] * dyc + cw_ref[1:2, :] * ed[1:1 + tm, :] + cw_ref[0:1, :] * ed[2:2 + tm, :]
        d_ref[:, c_cc:c_cc + 512] = (du * ch_ref[...]).astype(BF16)
        d_ref[:, c_ch:c_ch + 512] = (du * cc_ref[...]).astype(BF16)

        sgm, dsgm = _silu_and_grad(gm_ref[...])
        dmm = dm_ref[:, 0:1024]
        do_ref[...] = dmm * sgm
        d_ref[:, 0:1024] = (dmm * o_ref[...] * dsgm).astype(BF16)

    outs = (jax.ShapeDtypeStruct((S, W_MIX), BF16), jax.ShapeDtypeStruct((S, 1024), F32),
            jax.ShapeDtypeStruct((4, 128, 128), F32), jax.ShapeDtypeStruct((1, 512), F32),
            jax.ShapeDtypeStruct((3, 512), F32))
    scr = [pltpu.VMEM((tm + HALO, 512), F32) for _ in range(4)]
    return _pcall(kern, name=name, out_shape=outs, grid=(n,),
                  in_specs=[main(2048, 0), nxt(1024, 1),
                            main(1024, 1), main(512, 4), main(512, 5), main(512, 6), main(512, 7), main(512, 8),
                            main(512, 9), prev(512, 4), prev(512, 6), prev(512, 8),
                            nxt(512, 5), nxt(512, 7), nxt(512, 9),
                            main(1024, 0), full((4, 128, 128)), full((1, 512)), full((3, 512))],
                  out_specs=(main(W_MIX, 0), main(1024, 0), full((4, 128, 128)), full((1, 512)), full((3, 512))),
                  scratch=scr, dims=("arbitrary",), vmem_mb=56)(
                      dmix, dmix, proj, proj, proj, proj, proj, proj, proj, proj, proj, proj, proj, proj, proj,
                      o, wpool, ps.reshape(1, 512), convw)


def _outproj_ln(mix, wout, h, bout, g, b, *, name):
    S, Dm = h.shape
    tm = min(256, S)

    def kern(mix_ref, w_ref, h_ref, bo_ref, g_ref, b_ref, y_ref, yb_ref, r_ref):
        out = jnp.dot(mix_ref[...], w_ref[...], preferred_element_type=F32) + bo_ref[...]
        r = ALPHA * h_ref[...] + out
        r_ref[...] = r
        mu = jnp.mean(r, axis=-1, keepdims=True)
        xc = r - mu
        var = jnp.mean(xc * xc, axis=-1, keepdims=True)
        y = xc * lax.rsqrt(var + LN_EPS) * g_ref[...] + b_ref[...]
        y_ref[...] = y
        yb_ref[...] = y.astype(BF16)

    row = pl.BlockSpec((tm, Dm), lambda i: (i, 0))
    vec = pl.BlockSpec((1, Dm), lambda i: (0, 0))
    wsp = pl.BlockSpec((Dm, Dm), lambda i: (0, 0))
    sds = jax.ShapeDtypeStruct((S, Dm), F32)
    return _pcall(kern, name=name, out_shape=(sds, jax.ShapeDtypeStruct((S, Dm), BF16), sds), grid=(S // tm,),
                  in_specs=[row, wsp, row, vec, vec, vec], out_specs=(row, row, row), dims=("parallel",),
                  vmem_mb=56)(
                      mix, wout, h, bout.reshape(1, Dm), g.reshape(1, Dm), b.reshape(1, Dm))


def _adamw_math(w, g, m, v):
    m = ADAM_B1 * m + (1.0 - ADAM_B1) * g
    v = ADAM_B2 * v + (1.0 - ADAM_B2) * (g * g)
    m_hat = m / (1.0 - ADAM_B1 ** ADAM_STEP)
    v_hat = v / (1.0 - ADAM_B2 ** ADAM_STEP)
    delta = -ADAM_LR * (m_hat / (jnp.sqrt(v_hat) + ADAM_EPS) + ADAM_WD * w)
    return delta, m, v


def _row_tile(R, C):
    best = None
    for cand in range(8, R, 8):
        if R % cand == 0 and cand * C <= 256 * 1024:
            best = cand
    return best if best is not None else R


def _adamw(w, g, m, v, *, name):
    shape = w.shape
    C = shape[-1]
    R = 1
    for s in shape[:-1]:
        R *= s
    tr = _row_tile(R, C)

    def kern(w_ref, g_ref, m_ref, v_ref, d_ref, mo_ref, vo_ref):
        d, mn, vn = _adamw_math(w_ref[...], g_ref[...], m_ref[...], v_ref[...])
        d_ref[...] = d
        mo_ref[...] = mn
        vo_ref[...] = vn

    blk = pl.BlockSpec((tr, C), lambda i: (i, 0))
    sds = jax.ShapeDtypeStruct((R, C), F32)
    outs = _pcall(kern, name=name, out_shape=(sds, sds, sds), grid=(R // tr,), in_specs=[blk] * 4,
                  out_specs=(blk, blk, blk), dims=("parallel",), vmem_mb=48)(
                      w.reshape(R, C), g.reshape(R, C), m.reshape(R, C), v.reshape(R, C))
    return tuple(t.reshape(shape) for t in outs)


def _adamw_halves(w, m, v, halves, c_idx, *, name):
    _, R, C = w.shape
    ch = C // 2
    tr = _row_tile(R, ch)
    nb = R // tr

    def kern(c_ref, w_ref, a0_ref, b0_ref, a1_ref, b1_ref, m_ref, v_ref, g_ref, d_ref, mo_ref, vo_ref):
        layer = pl.program_id(0) // nb
        mine = pl.program_id(1) == c_ref[0]
        g = jnp.where(layer == 0, jnp.where(mine, a0_ref[...], b0_ref[...]),
                      jnp.where(mine, a1_ref[...], b1_ref[...]))
        g_ref[...] = g
        d, mn, vn = _adamw_math(w_ref[...], g, m_ref[...], v_ref[...])
        d_ref[...] = d
        mo_ref[...] = mn
        vo_ref[...] = vn

    full = pl.BlockSpec((tr, ch), lambda i, hc, c: (i, hc))
    half = pl.BlockSpec((tr, ch), lambda i, hc, c: (i % nb, 0))
    gs = pltpu.PrefetchScalarGridSpec(num_scalar_prefetch=1, grid=(2 * nb, 2),
                                      in_specs=[full, half, half, half, half, full, full], out_specs=(full,) * 4)
    sds = jax.ShapeDtypeStruct((2 * R, C), F32)
    (a0, b0), (a1, b1) = halves
    outs = pl.pallas_call(kern, name=name, out_shape=(sds,) * 4, grid_spec=gs,
                          compiler_params=pltpu.CompilerParams(dimension_semantics=("parallel", "parallel"),
                                                               vmem_limit_bytes=48 << 20))(
                              c_idx, w.reshape(2 * R, C), a0, b0, a1, b1, m.reshape(2 * R, C), v.reshape(2 * R, C))
    return tuple(t.reshape(2, R, C) for t in outs)


def _small_sum_adamw(gathered, w, m, v, *, name):
    R = w.shape[0]

    def kern(ga_ref, w_ref, m_ref, v_ref, g_ref, d_ref, mo_ref, vo_ref):
        g = ga_ref[0]
        for k in range(1, N_DEV):
            g = g + ga_ref[k]
        g_ref[...] = g
        d, mn, vn = _adamw_math(w_ref[...], g, m_ref[...], v_ref[...])
        d_ref[...] = d
        mo_ref[...] = mn
        vo_ref[...] = vn

    sds = jax.ShapeDtypeStruct((R, LANE), F32)
    return _pcall(kern, name=name, out_shape=(sds, sds, sds, sds), vmem_mb=48)(gathered, w, m, v)


def _pair_sum(g, theirs, c_idx, *, name):
    R, C = g.shape
    ch = C // 2
    tr = _row_tile(R, ch)

    def kern(c_ref, a_ref, b_ref, o_ref):
        o_ref[...] = (a_ref[...] + b_ref[...]).astype(BF16)

    gs = pltpu.PrefetchScalarGridSpec(
        num_scalar_prefetch=1, grid=(R // tr,),
        in_specs=[pl.BlockSpec((tr, ch), lambda i, c: (i, c[0])), pl.BlockSpec((tr, ch), lambda i, c: (i, 0))],
        out_specs=pl.BlockSpec((tr, ch), lambda i, c: (i, 0)))
    return pl.pallas_call(kern, name=name, out_shape=jax.ShapeDtypeStruct((R, ch), BF16), grid_spec=gs,
                          compiler_params=pltpu.CompilerParams(dimension_semantics=("parallel",),
                                                               vmem_limit_bytes=48 << 20))(c_idx, g, theirs)


WeightRows = collections.namedtuple("WeightRows", "full_rows own_rows cols pieces zero_rows")


def _w_in_piece_a(j):
    return jnp.where(j == 0, 0, 1232 * j + GAP)


def _w_in_piece_b(j):
    return jnp.where(j == 0, GAP_AT + GAP, 1232 * j + GAP_AT + GAP)


W_IN = WeightRows(NP, 1232, D_MODEL, ((0, GAP_AT, _w_in_piece_a), (GAP_AT, 1232 - GAP_AT, _w_in_piece_b)),
                  ((GAP_AT, GAP),))
W_OUT = WeightRows(2048, 512, D_MODEL, ((0, 512, lambda j: 512 * j),), ())
W_UQ = WeightRows(2048, 384, Q_LORA, ((0, 192, lambda j: 512 * j), (192, 192, lambda j: 512 * j + 256)),
                  tuple((256 * h + 192, 64) for h in range(N_HEADS)))
W_UKV = WeightRows(2048, 512, KV_LORA, ((0, 512, lambda j: 512 * j),), ())
W_CONV = WeightRows(64, 16, 256, ((0, 16, lambda j: 16 * j),), ())
SHARDED = (W_IN, W_OUT, W_UQ, W_UKV)


def _mesh_pos():
    x, y, c = lax.axis_index("x"), lax.axis_index("y"), lax.axis_index("c")
    return x, y, c


def _other_chips(x, y):
    return [(1 - x, y), (x, 1 - y), (1 - x, 1 - y)]


def _rows(start, n):
    return pl.ds(pl.multiple_of(start, 16), n)


def _half_cols(spec, c):
    ch = spec.cols // 2
    return pl.ds(pl.multiple_of(c * ch, LANE), ch)


def _allgather_script(specs, shards, zeros):
    na = len(specs)
    zlist = [a for a in range(na) if zeros[a] is not None]
    plan_first, plan_own, plan_zero = [], [], []
    for a, spec in enumerate(specs):
        for p in range(len(spec.pieces)):
            plan_own.append((a, p))
            for k in range(3):
                plan_first.append((a, p, k))
        for z in range(len(spec.zero_rows)):
            for l in range(shards[a].shape[0]):
                plan_zero.append((a, z, l))
    nf = len(plan_first)
    n_sems = 2 * nf + len(plan_own) + len(plan_zero)

    def copies(ins_all, outs, send_sems, recv_sems):
        ins = ins_all[:na]
        zrefs = dict(zip(zlist, ins_all[na:]))
        x, y, c = _mesh_pos()
        j = 2 * x + y
        chips = _other_chips(x, y)
        sibling = (x, y, 1 - c)

        def remote(src, dst, sem, to):
            return pltpu.make_async_remote_copy(src_ref=src, dst_ref=dst, send_sem=send_sems.at[sem],
                                                recv_sem=recv_sems.at[sem], device_id=to, device_id_type=MESH)

        def block(a, p, chip, cols):
            _, n, dst = specs[a].pieces[p]
            return outs[a].at[:, _rows(dst(chip), n), cols]

        def first(i):
            a, p, k = plan_first[i]
            src0, n, _ = specs[a].pieces[p]
            cols = _half_cols(specs[a], c)
            return remote(ins[a].at[:, pl.ds(src0, n), cols], block(a, p, j, cols), i, (*chips[k], c))

        def landed(i, half):
            a, p, k = plan_first[i]
            return block(a, p, 2 * chips[k][0] + chips[k][1], _half_cols(specs[a], half))

        def arrival(i, half, sem):
            return remote(landed(i, half), landed(i, half), sem, sibling)

        def passed(i):
            return remote(landed(i, c), landed(i, c), nf + i, sibling)

        def own(i):
            a, p = plan_own[i]
            src0, n, _ = specs[a].pieces[p]
            return remote(ins[a].at[:, pl.ds(src0, n), :], block(a, p, j, slice(None)), 2 * nf + i, sibling)

        def zero(i):
            a, z, l = plan_zero[i]
            r0, n = specs[a].zero_rows[z]
            return remote(zrefs[a].at[pl.ds(0, n), :], outs[a].at[l, pl.ds(r0, n), :],
                          2 * nf + len(plan_own) + i, sibling)

        fixed = [own(i) for i in range(len(plan_own))] + [zero(i) for i in range(len(plan_zero))]
        return c, fixed, first, arrival, passed

    def start(ins, outs, send_sems, recv_sems):
        _, fixed, first, _, _ = copies(ins, outs, send_sems, recv_sems)
        for cp in fixed:
            cp.start()
        for i in range(nf):
            first(i).start()

    def finish(ins, outs, send_sems, recv_sems):
        c, fixed, first, arrival, passed = copies(ins, outs, send_sems, recv_sems)
        for i in range(nf):
            arrival(i, c, i).wait_recv()
            passed(i).start()
        for i in range(nf):
            arrival(i, 1 - c, nf + i).wait_recv()
        for cp in fixed:
            cp.wait()
        for i in range(nf):
            first(i).wait_send()
            passed(i).wait_send()

    out_shape = tuple(jax.ShapeDtypeStruct((shards[a].shape[0], spec.full_rows, spec.cols), BF16)
                      for a, spec in enumerate(specs))
    args = tuple(shards) + tuple(zeros[a] for a in zlist)
    return CommScript(args, out_shape, n_sems, start, finish)


def _start_all_wait_all(args, out_shape, n_sems, make_copies):
    def start(ins, outs, send_sems, recv_sems):
        for cp in make_copies(ins, outs, send_sems, recv_sems):
            cp.start()

    def finish(ins, outs, send_sems, recv_sems):
        for cp in make_copies(ins, outs, send_sems, recv_sems):
            cp.wait()

    return CommScript(tuple(args), tuple(out_shape), n_sems, start, finish)


def _exchange_script(specs, grads):
    na = len(grads)

    def make_copies(ins, outs, send_sems, recv_sems):
        x, y, c = _mesh_pos()
        return [pltpu.make_async_remote_copy(
            src_ref=ins[a].at[:, _half_cols(specs[a], 1 - c)], dst_ref=outs[a], send_sem=send_sems.at[a],
            recv_sem=recv_sems.at[a], device_id=(x, y, 1 - c), device_id_type=MESH) for a in range(na)]

    out_shape = [jax.ShapeDtypeStruct((s.full_rows, s.cols // 2), F32) for s in specs]
    return _start_all_wait_all(grads, out_shape, na, make_copies)


def _scatter_script(specs, parts):
    na = len(parts)
    plan = [(a, p, k) for a in range(na) for p in range(len(specs[a].pieces)) for k in range(3)]

    def make_copies(ins, outs, send_sems, recv_sems):
        x, y, c = _mesh_pos()
        chips = _other_chips(x, y)
        copies = []
        for i, (a, p, k) in enumerate(plan):
            src0, n, dst = specs[a].pieces[p]
            pk = 2 * chips[k][0] + chips[k][1]
            copies.append(pltpu.make_async_remote_copy(
                src_ref=ins[a].at[_rows(dst(pk), n), :], dst_ref=outs[a].at[k, pl.ds(src0, n), :],
                send_sem=send_sems.at[i], recv_sem=recv_sems.at[i], device_id=(*chips[k], c), device_id_type=MESH))
        return copies

    out_shape = [jax.ShapeDtypeStruct((3, s.own_rows, s.cols // 2), BF16) for s in specs]
    return _start_all_wait_all(parts, out_shape, len(plan), make_copies)


def _chip_sum(spec, part, recv, *, name):
    ch = spec.cols // 2
    npieces = len(spec.pieces)

    def kern(recv_ref, part_ref, o_ref, own_ref, sems):
        j = 2 * lax.axis_index("x") + lax.axis_index("y")
        copies = []
        for p, (src0, n, dst) in enumerate(spec.pieces):
            copies.append(pltpu.make_async_copy(part_ref.at[_rows(dst(j), n), :], own_ref.at[pl.ds(src0, n), :],
                                                sems.at[p]))
        for cp in copies:
            cp.start()
        for cp in copies:
            cp.wait()
        o_ref[...] = ((own_ref[...].astype(F32) + recv_ref[0].astype(F32)) + recv_ref[1].astype(F32)) \
            + recv_ref[2].astype(F32)

    vm = pl.BlockSpec(memory_space=pltpu.VMEM)
    return _pcall(kern, name=name, out_shape=jax.ShapeDtypeStruct((spec.own_rows, ch), F32),
                  in_specs=[vm, HBM_SPEC], out_specs=vm,
                  scratch=[pltpu.VMEM((spec.own_rows, ch), BF16), pltpu.SemaphoreType.DMA((npieces,))],
                  vmem_mb=48)(recv, part)


def _sibling_script(sums):
    na = len(sums)

    def make_copies(ins, outs, send_sems, recv_sems):
        x, y, c = _mesh_pos()
        return [pltpu.make_async_remote_copy(
            src_ref=ins[a], dst_ref=outs[a], send_sem=send_sems.at[a], recv_sem=recv_sems.at[a],
            device_id=(x, y, 1 - c), device_id_type=MESH) for a in range(na)]

    out_shape = [jax.ShapeDtypeStruct(t.shape, t.dtype) for t in sums]
    return _start_all_wait_all(sums, out_shape, na, make_copies)


class _GradReducer:
    def __init__(self, layer, grads, c_idx):
        self.layer, self.grads, self.c_idx = layer, tuple(grads), c_idx
        self.names = [f"{nm}{layer}" for nm in ("w_in", "w_out", "w_uq", "w_ukv")]

    def exchange(self):
        return _exchange_script(SHARDED, self.grads)

    def scatter(self, theirs):
        self.parts = tuple(_pair_sum(g, th, self.c_idx, name=f"pair_sum_{nm}")
                           for g, th, nm in zip(self.grads, theirs, self.names))
        return _scatter_script(SHARDED, self.parts)

    def sibling(self, recv):
        self.sums = tuple(_chip_sum(s, p, r, name=f"chip_sum_{nm}")
                          for s, p, r, nm in zip(SHARDED, self.parts, recv, self.names))
        return _sibling_script(self.sums)

    def done(self, others):
        return list(zip(self.sums, others))


def _allgather_small(block, *, name):
    m_per, n = block.shape

    def body(x_ref, out_ref, send_sems, recv_sems, local_sem):
        x, y, c = _mesh_pos()
        me, sibling = (x, y, c), (x, y, 1 - c)
        chips = _other_chips(x, y)

        def rows(px, py, pc):
            return out_ref.at[4 * px + 2 * py + pc]

        def copy(k, blk, to, src=None):
            return pltpu.make_async_remote_copy(
                src_ref=rows(*blk) if src is None else src, dst_ref=rows(*blk), send_sem=send_sems.at[k],
                recv_sem=recv_sems.at[k], device_id=to, device_id_type=MESH)

        mine = pltpu.make_async_copy(x_ref, rows(*me), local_sem)
        mine.start()
        first = [copy(0, me, sibling, src=x_ref)]
        first += [copy(1 + k, me, (*chip, c), src=x_ref) for k, chip in enumerate(chips)]
        for cp in first:
            cp.start()
        passed = [copy(4 + k, (*chip, c), sibling) for k, chip in enumerate(chips)]
        for k, chip in enumerate(chips):
            copy(1 + k, (*chip, c), me).wait_recv()
            passed[k].start()
        copy(0, sibling, me).wait_recv()
        for k, chip in enumerate(chips):
            copy(4 + k, (*chip, 1 - c), me).wait_recv()
        for cp in first + passed:
            cp.wait_send()
        mine.wait()

    vm = pl.BlockSpec(memory_space=pltpu.VMEM)
    return _pcall(body, name=name, out_shape=jax.ShapeDtypeStruct((N_DEV, m_per, n), block.dtype),
                  in_specs=[vm], out_specs=vm,
                  scratch=[pltpu.SemaphoreType.DMA((7,)), pltpu.SemaphoreType.DMA((7,)), pltpu.SemaphoreType.DMA],
                  vmem_mb=48)(block)


def _rope_tables(positions):
    half = ROPE // 2
    inv_freq = ROPE_THETA ** (-jnp.arange(half, dtype=F32) / half)
    ang = positions.astype(F32)[:, None] * inv_freq
    cos, sin = jnp.cos(ang), jnp.sin(ang)
    S = positions.shape[0]
    cos_t = jnp.concatenate([cos, cos, jnp.ones((S, 64), F32)], axis=1)
    sin_t = jnp.concatenate([-sin, sin, jnp.zeros((S, 64), F32)], axis=1)
    return cos_t, sin_t


def _decode_conv(bits):
    rows = bits.reshape(DEPTH, N_CHIPS, 16, 256)[:, :, :3, :]
    conv = lax.bitcast_convert_type(rows.reshape(DEPTH, N_CHIPS, 3, 128, 2), F32)
    return jnp.transpose(conv, (0, 2, 1, 3)).reshape(DEPTH, 3, 512)


def _local_step(x, positions, target, emb_g, emb_b, w_in_t0, rest0, weights1, q_g, kv_g, w_pool, pool_scale,
                b_out, ln_g, ln_b, c_idx=None):
    cos_t, sin_t = _rope_tables(positions)
    if isinstance(w_in_t0, CommScript):
        (h, hb), (landed,) = _ln_fwd(x, emb_g, emb_b, name="emb_ln", comm=w_in_t0)
        w_in_t0 = landed[0]
    else:
        h, hb = _ln_fwd(x, emb_g, emb_b, name="emb_ln")
    weights = [None, weights1]
    saved = []
    for l in range(DEPTH):
        if l == 0 and isinstance(rest0, CommScript):
            proj, landed = _matmul(hb, w_in_t0, "nt", name="in_proj0", tm=1024, tn=1024, tk=2048, vmem_mb=56,
                                   comm=rest0)
            weights[0] = (w_in_t0,) + tuple(a[0] for a in landed[:3])
            conv_w = _decode_conv(landed[3])
        else:
            if l == 0:
                weights[0] = (w_in_t0,) + tuple(rest0[:3])
                conv_w = rest0[3]
            proj = _matmul(hb, weights[l][0], "nt", name=f"in_proj{l}", tm=1024, tn=1024, tk=2048, vmem_mb=56)
        w_in_t, w_out, w_uq_t, w_ukv_t = weights[l]
        qc, kc, v, vt, qn, kvn = _mla_qkv(proj, cos_t, sin_t, q_g[l], kv_g[l], w_uq_t, w_ukv_t, name=f"mla_qkv{l}")
        nxt = weights[l + 1] if l + 1 < DEPTH else None
        if isinstance(nxt, CommScript):
            (o, lse2), landed = _flash_fwd(qc, kc, vt, name=f"flash_fwd{l}", comm=nxt)
            weights[l + 1] = tuple(a[0] for a in landed)
        else:
            o, lse2 = _flash_fwd(qc, kc, vt, name=f"flash_fwd{l}")
        mix = _mixer_fwd(proj, o, w_pool[l], pool_scale[l], conv_w[l], name=f"mixer_fwd{l}")
        h_next, hb_next, r = _outproj_ln(mix, w_out, h, b_out[l], ln_g[l], ln_b[l], name=f"out_proj_ln{l}")
        saved.append((hb, proj, qc, kc, v, qn, kvn, o, lse2, mix, r))
        h, hb = h_next, hb_next

    y_final = h
    small = [None] * DEPTH
    big = [None] * DEPTH
    above = scatter_above = None
    for l in reversed(range(DEPTH)):
        w_in_t, w_out, w_uq_t, w_ukv_t = weights[l]
        hb_in, proj, qc, kc, v, qn, kvn, o, lse2, mix, r = saved[l]
        if l == DEPTH - 1:
            loss_acc, dr, drb, d_ln_g, d_ln_b, d_b_out = _loss_ln_bwd(y_final, target, r, ln_g[l], name="loss_ln_bwd")
        else:
            dr, drb, d_ln_g, d_ln_b, d_b_out = _ln_bwd(dh, r, ln_g[l], name=f"ln_bwd{l}")
        dmix = _matmul(drb, w_out, "nt", name=f"dmix{l}", tm=1024, tn=1024, tk=2048, vmem_mb=56)
        d_w_out = _matmul(mix, drb, "tn", name=f"dw_out{l}", tm=1024, tn=1024, tk=2048, vmem_mb=56)
        d_mix, do, d_w_pool, d_ps, d_conv = _mixer_bwd(dmix, proj, o, w_pool[l], pool_scale[l], conv_w[l],
                                                       name=f"mixer_bwd{l}")
        delta = _attn_delta(o, do, name=f"attn_delta{l}")
        if above is not None:
            (dqb, dkvb, dkr), recv = _flash_bwd(qc, kc, v, do, lse2, delta, cos_t, sin_t, name=f"flash_bwd{l}",
                                                comm=scatter_above)
            sibling_above = above.sibling(recv)
        else:
            dqb, dkvb, dkr = _flash_bwd(qc, kc, v, do, lse2, delta, cos_t, sin_t, name=f"flash_bwd{l}")
        d_mla, d_qg, d_kvg = _mla_qkv_bwd(dqb, dkvb, dkr, proj, cos_t, sin_t, q_g[l], kv_g[l], w_uq_t, w_ukv_t,
                                          name=f"mla_qkv_bwd{l}")
        d_w_uq_t = _matmul(dqb, qn, "tn", name=f"dw_uq{l}", tm=2048, tn=512, tk=2048, vmem_mb=56)
        d_w_ukv_t = _matmul(dkvb, kvn, "tn", name=f"dw_ukv{l}", tm=2048, tn=256, tk=2048, vmem_mb=56)
        if above is not None:
            d_w_in_t, others = _dproj_t_times_h(d_mla, d_mix, hb_in, name=f"dw_in{l}", comm=sibling_above)
            big[l + 1] = above.done(others)
            above = None
        else:
            d_w_in_t = _dproj_t_times_h(d_mla, d_mix, hb_in, name=f"dw_in{l}")
        big[l] = (d_w_in_t, d_w_out, d_w_uq_t, d_w_ukv_t)
        small[l] = dict(q_g=d_qg[0], kv_g=d_kvg[0], w_pool=d_w_pool, pool_scale=d_ps[0], conv_w=d_conv,
                        b_out=d_b_out[0], ln_g=d_ln_g[0], ln_b=d_ln_b[0])
        if c_idx is None:
            dh = _dproj_times_w(d_mla, d_mix, w_in_t, dr, ALPHA, name=f"dh{l}")
        elif l > 0:
            above = _GradReducer(l, big[l], c_idx)
            dh, theirs = _dproj_times_w(d_mla, d_mix, w_in_t, dr, ALPHA, name=f"dh{l}", comm=above.exchange())
            scatter_above = above.scatter(theirs)
        else:
            last = _GradReducer(l, big[l], c_idx)
            theirs = _run_comm(last.exchange(), name="exchange_halves0")
            dh, recv = _dproj_times_w(d_mla, d_mix, w_in_t, dr, ALPHA, name=f"dh{l}", comm=last.scatter(theirs))
    grad_x, _, d_emb_g, d_emb_b, _ = _ln_bwd(dh, x, emb_g, name="emb_ln_bwd")
    if c_idx is not None:
        big[0] = last.done(_run_comm(last.sibling(recv), name="send_to_sibling0"))
    return loss_acc[0, 0], grad_x, d_emb_g[0], d_emb_b[0], small, big


SMALL_ORDER = ("emb_ln_g", "emb_ln_b", "q_norm_g", "kv_norm_g", "w_pool", "pool_scale", "b_out", "ln_g", "ln_b")


def _pack_small(arrs, extra_rows):
    flat = jnp.concatenate([a.reshape(-1) for a in arrs])
    rows = flat.shape[0] // LANE
    total = -(-(rows + extra_rows) // 8) * 8
    return jnp.pad(flat, (0, total * LANE - flat.shape[0])).reshape(total, LANE)


def _unpack_small(packed, shapes):
    flat = packed.reshape(-1)
    out, off = [], 0
    for shp in shapes:
        n = 1
        for s in shp:
            n *= s
        out.append(flat[off:off + n].reshape(shp))
        off += n
    return out, off


def kernel(x, positions, emb_ln_g, emb_ln_b, w_in, q_norm_g, kv_norm_g, w_uq, w_ukv, w_pool, pool_scale, conv_w, w_out, b_out, ln_g, ln_b, loss_target, m_emb_ln_g, m_emb_ln_b, m_w_in, m_q_norm_g, m_kv_norm_g, m_w_uq, m_w_ukv, m_w_pool, m_pool_scale, m_conv_w, m_w_out, m_b_out, m_ln_g, m_ln_b, v_emb_ln_g, v_emb_ln_b, v_w_in, v_q_norm_g, v_kv_norm_g, v_w_uq, v_w_ukv, v_w_pool, v_pool_scale, v_conv_w, v_w_out, v_b_out, v_ln_g, v_ln_b):
    xi, yi, ci = lax.axis_index("x"), lax.axis_index("y"), lax.axis_index("c")
    chip = 2 * xi + yi
    c_idx = ci.reshape(1).astype(jnp.int32)

    def t(a):
        return jnp.swapaxes(a, 1, 2)

    conv_bits = lax.bitcast_convert_type(conv_w.reshape(DEPTH, 3 * 128), BF16).reshape(DEPTH, 3, 256)
    conv_bits = jnp.pad(conv_bits, ((0, 0), (0, 13), (0, 0)))
    own = (t(w_in).astype(BF16), w_out.astype(BF16), t(w_uq).astype(BF16), t(w_ukv).astype(BF16))
    zeros = (jnp.zeros((GAP, D_MODEL), BF16), None, jnp.zeros((64, Q_LORA), BF16), None)
    gather_in0 = _allgather_script((W_IN,), (own[0][0:1],), zeros[:1])
    gather0 = _allgather_script(SHARDED[1:] + (W_CONV,), tuple(a[0:1] for a in own[1:]) + (conv_bits,),
                                zeros[1:] + (None,))
    gather1 = _allgather_script(SHARDED, tuple(a[1:2] for a in own), zeros)

    loss_part, grad_x, d_emb_g, d_emb_b, grads, reduced = _local_step(
        x[0], positions[0], loss_target[0], emb_ln_g, emb_ln_b, gather_in0, gather0, gather1, q_norm_g, kv_norm_g,
        w_pool, pool_scale, b_out, ln_g, ln_b, c_idx)

    small_g = [d_emb_g, d_emb_b,
               jnp.stack([grads[l]["q_g"] for l in range(DEPTH)]), jnp.stack([grads[l]["kv_g"] for l in range(DEPTH)]),
               jnp.stack([grads[l]["w_pool"] for l in range(DEPTH)]),
               jnp.stack([grads[l]["pool_scale"] for l in range(DEPTH)]),
               jnp.stack([grads[l]["b_out"] for l in range(DEPTH)]), jnp.stack([grads[l]["ln_g"] for l in range(DEPTH)]),
               jnp.stack([grads[l]["ln_b"] for l in range(DEPTH)]),
               jnp.stack([grads[l]["conv_w"] for l in range(DEPTH)]),
               jnp.pad(loss_part.reshape(1), (0, LANE - 1))]
    small_w = [emb_ln_g, emb_ln_b, q_norm_g, kv_norm_g, w_pool, pool_scale, b_out, ln_g, ln_b]
    small_m = [m_emb_ln_g, m_emb_ln_b, m_q_norm_g, m_kv_norm_g, m_w_pool, m_pool_scale, m_b_out, m_ln_g, m_ln_b]
    small_v = [v_emb_ln_g, v_emb_ln_b, v_q_norm_g, v_kv_norm_g, v_w_pool, v_pool_scale, v_b_out, v_ln_g, v_ln_b]
    extra = (DEPTH * 3 * 512 + LANE) // LANE
    packed_g = _pack_small(small_g, 0)
    gathered = _allgather_small(packed_g, name="allgather_small")
    g_tot, d_small, m_small, v_small = _small_sum_adamw(
        gathered, _pack_small(small_w, extra), _pack_small(small_m, extra), _pack_small(small_v, extra),
        name="small_sum_adamw")
    shapes = [w.shape for w in small_w]
    g_list, off = _unpack_small(g_tot, shapes)
    d_list, _ = _unpack_small(d_small, shapes)
    m_list, _ = _unpack_small(m_small, shapes)
    v_list, _ = _unpack_small(v_small, shapes)
    flat_tot = g_tot.reshape(-1)
    conv_tot = flat_tot[off:off + DEPTH * 3 * 512].reshape(DEPTH, 3, 512)
    loss = flat_tot[off + DEPTH * 3 * 512]
    g_conv = lax.dynamic_slice_in_dim(conv_tot, chip * 128, 128, axis=2)

    def halves(a):
        return [reduced[l][a] for l in range(DEPTH)]

    def whole(a):
        return jnp.stack([jnp.where(ci == 0, jnp.concatenate([mine, oth], axis=1),
                                    jnp.concatenate([oth, mine], axis=1)) for mine, oth in halves(a)])

    upd = {}
    upd["w_in"] = tuple(t(o) for o in _adamw_halves(t(w_in), t(m_w_in), t(v_w_in), halves(0), c_idx,
                                                    name="adamw_w_in"))
    upd["w_out"] = _adamw_halves(w_out, m_w_out, v_w_out, halves(1), c_idx, name="adamw_w_out")
    g_uq, g_ukv = t(whole(2)), t(whole(3))
    upd["w_uq"] = (g_uq,) + _adamw(w_uq, g_uq, m_w_uq, v_w_uq, name="adamw_w_uq")
    upd["w_ukv"] = (g_ukv,) + _adamw(w_ukv, g_ukv, m_w_ukv, v_w_ukv, name="adamw_w_ukv")
    upd["conv_w"] = (g_conv,) + _adamw(conv_w, g_conv, m_conv_w, v_conv_w, name="adamw_conv_w")
    for i, nm in enumerate(SMALL_ORDER):
        upd[nm] = (g_list[i], d_list[i], m_list[i], v_list[i])

    order = ("emb_ln_g", "emb_ln_b", "w_in", "q_norm_g", "kv_norm_g", "w_uq", "w_ukv", "w_pool", "pool_scale",
             "conv_w", "w_out", "b_out", "ln_g", "ln_b")
    outs = [loss, grad_x[None]]
    for field in range(4):
        outs += [upd[nm][field] for nm in order]
    return tuple(outs)
```

```python
import collections

import jax
import jax.numpy as jnp
from jax import lax
from jax.experimental import pallas as pl
from jax.experimental.pallas import tpu as pltpu

F32 = jnp.float32
BF16 = jnp.bfloat16
MESH = pl.DeviceIdType.MESH

D_MODEL = 2048
DEPTH = 2
N_HEADS = 8
NOPE = 128
ROPE = 64
Q_LORA = 512
KV_LORA = 256
D_MLA = 1024
POOL_WINDOWS = (2, 4, 8, 16)
D_IN_PROJ = 4928
LN_EPS = 1e-5
RMS_EPS = 1e-6
ROPE_THETA = 10000.0
ALPHA = (2 * DEPTH) ** 0.25
SCALE = (NOPE + ROPE) ** -0.5
LOG2E = 1.4426950408889634
SCALE_LOG2E = SCALE * LOG2E
ADAM_LR = 0.001
ADAM_B1 = 0.9
ADAM_B2 = 0.999
ADAM_EPS = 1e-08
ADAM_WD = 0.01
ADAM_STEP = 10

NP = 5120
GAP_AT = 832
GAP = NP - D_IN_PROJ
W_MLA = 1024
W_MIX = NP - W_MLA
HALO = 16
LANE = 128
N_CHIPS = 4
N_DEV = 8
TQ = 512
FWD_GROUP = 4

NN = (((1,), (0,)), ((), ()))
NT = (((1,), (1,)), ((), ()))
TN = (((0,), (0,)), ((), ()))


CommScript = collections.namedtuple("CommScript", "args out_shape n_sems start finish")
HBM_SPEC = pl.BlockSpec(memory_space=pl.ANY)


def _pcall(kern, *, name, out_shape, grid=None, in_specs=None, out_specs=None, scratch=(), dims=None,
           vmem_mb=None, comm=None):
    cp = {}
    if dims is not None:
        cp["dimension_semantics"] = dims if comm is None else ("arbitrary",) * len(dims)
    if vmem_mb is not None:
        cp["vmem_limit_bytes"] = vmem_mb << 20
    if comm is None:
        args = dict(name=name, out_shape=out_shape, scratch_shapes=list(scratch),
                    compiler_params=pltpu.CompilerParams(**cp))
        if grid is not None:
            args["grid"] = grid
        if in_specs is not None:
            args["in_specs"] = in_specs
        if out_specs is not None:
            args["out_specs"] = out_specs
        return pl.pallas_call(kern, **args)

    single = not isinstance(out_shape, (tuple, list))
    own_out = (out_shape,) if single else tuple(out_shape)
    own_out_specs = (out_specs,) if single else tuple(out_specs)
    n_in, n_out, n_scr = len(in_specs), len(own_out), len(scratch)
    na, no = len(comm.args), len(comm.out_shape)

    def at(end):
        cond = None
        for d, n in enumerate(grid):
            here = pl.program_id(d) == (n - 1 if end else 0)
            cond = here if cond is None else jnp.logical_and(cond, here)
        return cond

    def wrapped(*refs):
        own_in, c_in = refs[:n_in], refs[n_in:n_in + na]
        o0 = n_in + na
        own_o, c_out = refs[o0:o0 + n_out], refs[o0 + n_out:o0 + n_out + no]
        s0 = o0 + n_out + no
        own_s, (send_sems, recv_sems) = refs[s0:s0 + n_scr], refs[s0 + n_scr:]

        @pl.when(at(False))
        def _():
            comm.start(c_in, c_out, send_sems, recv_sems)

        kern(*own_in, *own_o, *own_s)

        @pl.when(at(True))
        def _():
            comm.finish(c_in, c_out, send_sems, recv_sems)

    call = pl.pallas_call(
        wrapped, name=name, out_shape=own_out + tuple(comm.out_shape), grid=grid,
        in_specs=list(in_specs) + [HBM_SPEC] * na, out_specs=own_out_specs + (HBM_SPEC,) * no,
        scratch_shapes=list(scratch) + [pltpu.SemaphoreType.DMA((comm.n_sems,)),
                                        pltpu.SemaphoreType.DMA((comm.n_sems,))],
        compiler_params=pltpu.CompilerParams(**cp))

    def run(*args):
        res = call(*args, *comm.args)
        own = res[0] if single else tuple(res[:n_out])
        return own, tuple(res[n_out:])

    return run


def _run_comm(script, *, name):
    na, no = len(script.args), len(script.out_shape)

    def body(*refs):
        ins, outs = refs[:na], refs[na:na + no]
        send_sems, recv_sems = refs[na + no:]
        script.start(ins, outs, send_sems, recv_sems)
        script.finish(ins, outs, send_sems, recv_sems)

    return pl.pallas_call(
        body, name=name, out_shape=tuple(script.out_shape), in_specs=[HBM_SPEC] * na, out_specs=(HBM_SPEC,) * no,
        scratch_shapes=[pltpu.SemaphoreType.DMA((script.n_sems,)), pltpu.SemaphoreType.DMA((script.n_sems,))])(
            *script.args)


def _sigmoid(g):
    return 1.0 / (1.0 + jnp.exp(-g))


def _silu_and_grad(g):
    sig = _sigmoid(g)
    return g * sig, sig * (1.0 + g * (1.0 - sig))


def _matmul(a, b, mode, *, name, tm, tn, tk, out_dtype=F32, vmem_mb=48, comm=None):
    if mode == "nn":
        (M, K), N = a.shape, b.shape[1]
    elif mode == "nt":
        (M, K), N = a.shape, b.shape[0]
    else:
        (K, M), N = a.shape, b.shape[1]
    tm, tn, tk = min(tm, M), min(tn, N), min(tk, K)
    assert M % tm == 0 and N % tn == 0 and K % tk == 0, (name, M, N, K)
    nk = K // tk
    dn = {"nn": NN, "nt": NT, "tn": TN}[mode]
    if mode == "tn":
        a_spec = pl.BlockSpec((tk, tm), lambda i, j, k: (k, i))
    else:
        a_spec = pl.BlockSpec((tm, tk), lambda i, j, k: (i, k))
    if mode == "nt":
        b_spec = pl.BlockSpec((tn, tk), lambda i, j, k: (j, k))
    else:
        b_spec = pl.BlockSpec((tk, tn), lambda i, j, k: (k, j))
    o_spec = pl.BlockSpec((tm, tn), lambda i, j, k: (i, j))

    def kern(a_ref, b_ref, o_ref, *rest):
        part = lax.dot_general(a_ref[...].astype(BF16), b_ref[...].astype(BF16), dn,
                               preferred_element_type=F32)
        if nk == 1:
            o_ref[...] = part.astype(out_dtype)
        else:
            acc_ref = rest[0]
            k = pl.program_id(2)

            @pl.when(k == 0)
            def _():
                acc_ref[...] = part

            @pl.when(k > 0)
            def _():
                acc_ref[...] += part

            @pl.when(k == nk - 1)
            def _():
                o_ref[...] = acc_ref[...].astype(out_dtype)

    scratch = [pltpu.VMEM((tm, tn), F32)] if nk > 1 else []
    return _pcall(kern, name=name, out_shape=jax.ShapeDtypeStruct((M, N), out_dtype),
                  grid=(M // tm, N // tn, nk), in_specs=[a_spec, b_spec], out_specs=o_spec, scratch=scratch,
                  dims=("parallel", "parallel", "arbitrary"), vmem_mb=vmem_mb, comm=comm)(a, b)


def _dproj_times_w(d_mla, d_mix, wt, add, add_scale, *, name, comm=None):
    S = d_mla.shape[0]
    Dm = wt.shape[1]
    tm, tn, tk = min(1024, S), 1024, 2048
    nk = 1 + W_MIX // tk

    def kern(a1_ref, a2_ref, b1_ref, b2_ref, add_ref, o_ref, acc_ref):
        k = pl.program_id(2)

        @pl.when(k == 0)
        def _():
            acc_ref[...] = jnp.dot(a1_ref[...], b1_ref[...], preferred_element_type=F32)

        @pl.when(k > 0)
        def _():
            acc_ref[...] += jnp.dot(a2_ref[...], b2_ref[...], preferred_element_type=F32)

        @pl.when(k == nk - 1)
        def _():
            o_ref[...] = add_scale * add_ref[...] + acc_ref[...]

    o_spec = pl.BlockSpec((tm, tn), lambda i, j, k: (i, j))
    b2_spec = pl.BlockSpec((pl.Element(tk), pl.Element(tn)),
                           lambda i, j, k: (pl.multiple_of(W_MLA + tk * jnp.maximum(k - 1, 0), W_MLA),
                                            pl.multiple_of(j * tn, tn)))
    return _pcall(kern, name=name, out_shape=jax.ShapeDtypeStruct((S, Dm), F32), grid=(S // tm, Dm // tn, nk),
                  in_specs=[pl.BlockSpec((tm, W_MLA), lambda i, j, k: (i, 0)),
                            pl.BlockSpec((tm, tk), lambda i, j, k: (i, jnp.maximum(k - 1, 0))),
                            pl.BlockSpec((W_MLA, tn), lambda i, j, k: (0, j)), b2_spec, o_spec],
                  out_specs=o_spec, scratch=[pltpu.VMEM((tm, tn), F32)],
                  dims=("parallel", "parallel", "arbitrary"), vmem_mb=56, comm=comm)(d_mla, d_mix, wt, wt, add)


def _dproj_t_times_h(d_mla, d_mix, h, *, name, comm=None):
    S, Dm = h.shape
    tm, tn, tk = W_MLA, 1024, min(2048, S)
    nk = S // tk

    def kern(a1_ref, a2_ref, b_ref, o_ref, acc_ref):
        i = pl.program_id(0)
        k = pl.program_id(2)
        b = b_ref[...].astype(BF16)

        def accumulate(part):
            @pl.when(k == 0)
            def _():
                acc_ref[...] = part

            @pl.when(k > 0)
            def _():
                acc_ref[...] += part

        @pl.when(i == 0)
        def _():
            accumulate(lax.dot_general(a1_ref[...], b, TN, preferred_element_type=F32))

        @pl.when(i > 0)
        def _():
            accumulate(lax.dot_general(a2_ref[...], b, TN, preferred_element_type=F32))

        @pl.when(k == nk - 1)
        def _():
            o_ref[...] = acc_ref[...]

    return _pcall(kern, name=name, out_shape=jax.ShapeDtypeStruct((NP, Dm), F32), grid=(NP // tm, Dm // tn, nk),
                  in_specs=[pl.BlockSpec((tk, tm), lambda i, j, k: (jnp.where(i == 0, k, nk - 1), 0)),
                            pl.BlockSpec((tk, tm), lambda i, j, k: (jnp.where(i == 0, 0, k), jnp.maximum(i - 1, 0))),
                            pl.BlockSpec((tk, tn), lambda i, j, k: (k, j))],
                  out_specs=pl.BlockSpec((tm, tn), lambda i, j, k: (i, j)), scratch=[pltpu.VMEM((tm, tn), F32)],
                  dims=("parallel", "parallel", "arbitrary"), vmem_mb=48, comm=comm)(d_mla, d_mix, h)


def _ln_fwd(x, g, b, *, name, comm=None):
    S, Dm = x.shape
    tm = min(512, S)

    def kern(x_ref, g_ref, b_ref, y_ref, yb_ref):
        xf = x_ref[...]
        mu = jnp.mean(xf, axis=-1, keepdims=True)
        xc = xf - mu
        var = jnp.mean(xc * xc, axis=-1, keepdims=True)
        y = xc * lax.rsqrt(var + LN_EPS) * g_ref[...] + b_ref[...]
        y_ref[...] = y
        yb_ref[...] = y.astype(BF16)

    row = pl.BlockSpec((tm, Dm), lambda i: (i, 0))
    vec = pl.BlockSpec((1, Dm), lambda i: (0, 0))
    return _pcall(kern, name=name,
                  out_shape=(jax.ShapeDtypeStruct((S, Dm), F32), jax.ShapeDtypeStruct((S, Dm), BF16)),
                  grid=(S // tm,), in_specs=[row, vec, vec], out_specs=(row, row), dims=("parallel",), vmem_mb=48,
                  comm=comm)(
                      x, g.reshape(1, Dm), b.reshape(1, Dm))


def _ln_bwd(dy, r, g, *, name):
    S, Dm = r.shape
    tm = min(512, S)

    def kern(dy_ref, r_ref, g_ref, dr_ref, drb_ref, dg_ref, db_ref, ds_ref):
        @pl.when(pl.program_id(0) == 0)
        def _():
            dg_ref[...] = jnp.zeros_like(dg_ref)
            db_ref[...] = jnp.zeros_like(db_ref)
            ds_ref[...] = jnp.zeros_like(ds_ref)

        rf = r_ref[...]
        dyf = dy_ref[...]
        mu = jnp.mean(rf, axis=-1, keepdims=True)
        xc = rf - mu
        var = jnp.mean(xc * xc, axis=-1, keepdims=True)
        rstd = lax.rsqrt(var + LN_EPS)
        xhat = xc * rstd
        dxh = dyf * g_ref[...]
        c1 = jnp.mean(dxh, axis=-1, keepdims=True)
        c2 = jnp.mean(dxh * xhat, axis=-1, keepdims=True)
        dr = rstd * (dxh - c1 - xhat * c2)
        dr_ref[...] = dr
        drb_ref[...] = dr.astype(BF16)
        dg_ref[...] += jnp.sum(dyf * xhat, axis=0, keepdims=True)
        db_ref[...] += jnp.sum(dyf, axis=0, keepdims=True)
        ds_ref[...] += jnp.sum(dr, axis=0, keepdims=True)

    row = pl.BlockSpec((tm, Dm), lambda i: (i, 0))
    vec = pl.BlockSpec((1, Dm), lambda i: (0, 0))
    vshape = jax.ShapeDtypeStruct((1, Dm), F32)
    return _pcall(kern, name=name,
                  out_shape=(jax.ShapeDtypeStruct((S, Dm), F32), jax.ShapeDtypeStruct((S, Dm), BF16),
                             vshape, vshape, vshape),
                  grid=(S // tm,), in_specs=[row, row, vec], out_specs=(row, row, vec, vec, vec),
                  dims=("arbitrary",), vmem_mb=48)(dy, r, g.reshape(1, Dm))


def _loss_ln_bwd(y, target, r, g, *, name):
    S, Dm = r.shape
    tm = min(512, S)

    def kern(y_ref, t_ref, r_ref, g_ref, l_ref, dr_ref, drb_ref, dg_ref, db_ref, ds_ref):
        @pl.when(pl.program_id(0) == 0)
        def _():
            l_ref[...] = jnp.zeros_like(l_ref)
            dg_ref[...] = jnp.zeros_like(dg_ref)
            db_ref[...] = jnp.zeros_like(db_ref)
            ds_ref[...] = jnp.zeros_like(ds_ref)

        e = y_ref[...] - t_ref[...]
        dyf = e / float(Dm)
        per_row = jnp.mean(e * e, axis=-1, keepdims=True)
        l_ref[...] += 0.5 * jnp.sum(per_row, axis=0, keepdims=True)
        rf = r_ref[...]
        mu = jnp.mean(rf, axis=-1, keepdims=True)
        xc = rf - mu
        var = jnp.mean(xc * xc, axis=-1, keepdims=True)
        rstd = lax.rsqrt(var + LN_EPS)
        xhat = xc * rstd
        dxh = dyf * g_ref[...]
        c1 = jnp.mean(dxh, axis=-1, keepdims=True)
        c2 = jnp.mean(dxh * xhat, axis=-1, keepdims=True)
        dr = rstd * (dxh - c1 - xhat * c2)
        dr_ref[...] = dr
        drb_ref[...] = dr.astype(BF16)
        dg_ref[...] += jnp.sum(dyf * xhat, axis=0, keepdims=True)
        db_ref[...] += jnp.sum(dyf, axis=0, keepdims=True)
        ds_ref[...] += jnp.sum(dr, axis=0, keepdims=True)

    row = pl.BlockSpec((tm, Dm), lambda i: (i, 0))
    vec = pl.BlockSpec((1, Dm), lambda i: (0, 0))
    acc = pl.BlockSpec((8, LANE), lambda i: (0, 0))
    vshape = jax.ShapeDtypeStruct((1, Dm), F32)
    return _pcall(kern, name=name,
                  out_shape=(jax.ShapeDtypeStruct((8, LANE), F32), jax.ShapeDtypeStruct((S, Dm), F32),
                             jax.ShapeDtypeStruct((S, Dm), BF16), vshape, vshape, vshape),
                  grid=(S // tm,), in_specs=[row, row, row, vec], out_specs=(acc, row, row, vec, vec, vec),
                  dims=("arbitrary",), vmem_mb=56)(y, target, r, g.reshape(1, Dm))


def _rot_sum(t):
    return pltpu.roll(t, 32, 1) + pltpu.roll(t, 96, 1)


def _mla_qkv(proj, cos_t, sin_t, qg, kvg, wuq_t, wukv_t, *, name):
    S = proj.shape[0]
    tm = min(256, S)

    def kern(ql_ref, kvl_ref, kr_ref, cos_ref, sin_ref, qg_ref, kvg_ref, wuq_ref, wukv_ref,
             qc_ref, kc_ref, v_ref, vt_ref, qn_ref, kvn_ref):
        cosv = cos_ref[...]
        sinv = sin_ref[...]

        def rope(t):
            return t * cosv + _rot_sum(t) * sinv

        ql = ql_ref[...]
        qn = (ql * lax.rsqrt(jnp.mean(ql * ql, axis=-1, keepdims=True) + RMS_EPS) * qg_ref[...]).astype(BF16)
        kvl = kvl_ref[...]
        kvn = (kvl * lax.rsqrt(jnp.mean(kvl * kvl, axis=-1, keepdims=True) + RMS_EPS) * kvg_ref[...]).astype(BF16)
        qn_ref[...] = qn
        kvn_ref[...] = kvn
        q = lax.dot_general(qn, wuq_ref[...], NT, preferred_element_type=F32)
        kv = lax.dot_general(kvn, wukv_ref[...], NT, preferred_element_type=F32)
        kr = rope(kr_ref[...]).astype(BF16)
        for h in range(N_HEADS):
            c0 = 256 * h
            qc_ref[:, c0:c0 + 128] = q[:, c0:c0 + 128].astype(BF16)
            qc_ref[:, c0 + 128:c0 + 256] = rope(q[:, c0 + 128:c0 + 256]).astype(BF16)
            kc_ref[:, c0:c0 + 128] = kv[:, c0:c0 + 128].astype(BF16)
            kc_ref[:, c0 + 128:c0 + 256] = kr
            vh = kv[:, c0 + 128:c0 + 256]
            v_ref[:, 128 * h:128 * h + 128] = vh.astype(BF16)
            vt_ref[h] = jnp.transpose(vh).astype(BF16)

    def row(w, blk):
        return pl.BlockSpec((tm, w), lambda i: (i, blk))

    def full(shape):
        return pl.BlockSpec(shape, lambda i: (0,) * len(shape))

    t = min(TQ, S)
    per = t // tm
    vt_spec = pl.BlockSpec((N_HEADS, None, 128, tm), lambda i: (0, i // per, 0, i % per))
    outs = (jax.ShapeDtypeStruct((S, 2048), BF16), jax.ShapeDtypeStruct((S, 2048), BF16),
            jax.ShapeDtypeStruct((S, 1024), BF16), jax.ShapeDtypeStruct((N_HEADS, S // t, 128, t), BF16),
            jax.ShapeDtypeStruct((S, Q_LORA), BF16), jax.ShapeDtypeStruct((S, KV_LORA), BF16))
    return _pcall(kern, name=name, out_shape=outs, grid=(S // tm,),
                  in_specs=[row(512, 0), row(256, 2), row(128, 6), row(128, 0), row(128, 0),
                            full((1, Q_LORA)), full((1, KV_LORA)), full((2048, Q_LORA)), full((2048, KV_LORA))],
                  out_specs=(row(2048, 0), row(2048, 0), row(1024, 0), vt_spec, row(512, 0), row(256, 0)),
                  dims=("parallel",), vmem_mb=48)(
                      proj, proj, proj, cos_t, sin_t, qg.reshape(1, -1), kvg.reshape(1, -1), wuq_t, wukv_t)


def _mla_qkv_bwd(dqb, dkvb, dkr_heads, proj, cos_t, sin_t, qg, kvg, wuq_t, wukv_t, *, name):
    S = proj.shape[0]
    tm = min(256, S)

    def kern(dqb_ref, dkvb_ref, dkrh_ref, ql_ref, kvl_ref, cos_ref, sin_ref, qg_ref, kvg_ref, wuq_ref, wukv_ref,
             dml_ref, dqg_ref, dkvg_ref):
        @pl.when(pl.program_id(0) == 0)
        def _():
            dqg_ref[...] = jnp.zeros_like(dqg_ref)
            dkvg_ref[...] = jnp.zeros_like(dkvg_ref)

        cosv = cos_ref[...]
        sinv = sin_ref[...]

        def unrope(t):
            return t * cosv - _rot_sum(t) * sinv

        dkr = dkrh_ref[:, 0:128]
        for h in range(1, N_HEADS):
            dkr = dkr + dkrh_ref[:, 128 * h:128 * h + 128]

        def rms_bwd(x, g, dy):
            n = x.shape[-1]
            rs = lax.rsqrt(jnp.mean(x * x, axis=-1, keepdims=True) + RMS_EPS)
            dyg = dy * g
            dx = rs * dyg - x * (rs * rs * rs) * (jnp.sum(dyg * x, axis=-1, keepdims=True) / n)
            return dx, jnp.sum(dy * (x * rs), axis=0, keepdims=True)

        dqn = jnp.dot(dqb_ref[...], wuq_ref[...], preferred_element_type=F32)
        dql, dqg = rms_bwd(ql_ref[...], qg_ref[...], dqn)
        dqg_ref[...] += dqg
        dkvn = jnp.dot(dkvb_ref[...], wukv_ref[...], preferred_element_type=F32)
        dkvl, dkvg = rms_bwd(kvl_ref[...], kvg_ref[...], dkvn)
        dkvg_ref[...] += dkvg
        dml_ref[:, 0:512] = dql.astype(BF16)
        dml_ref[:, 512:768] = dkvl.astype(BF16)
        dml_ref[:, 768:896] = unrope(dkr).astype(BF16)
        dml_ref[:, 896:1024] = jnp.zeros((tm, 128), BF16)

    def row(w, blk):
        return pl.BlockSpec((tm, w), lambda i: (i, blk))

    def full(shape):
        return pl.BlockSpec(shape, lambda i: (0,) * len(shape))

    outs = (jax.ShapeDtypeStruct((S, W_MLA), BF16), jax.ShapeDtypeStruct((1, Q_LORA), F32),
            jax.ShapeDtypeStruct((1, KV_LORA), F32))
    return _pcall(kern, name=name, out_shape=outs, grid=(S // tm,),
                  in_specs=[row(2048, 0), row(2048, 0), row(1024, 0), row(512, 0), row(256, 2),
                            row(128, 0), row(128, 0), full((1, Q_LORA)), full((1, KV_LORA)),
                            full((2048, Q_LORA)), full((2048, KV_LORA))],
                  out_specs=(row(W_MLA, 0), full((1, Q_LORA)), full((1, KV_LORA))),
                  dims=("arbitrary",), vmem_mb=56)(
                      dqb, dkvb, dkr_heads, proj, proj, cos_t, sin_t, qg.reshape(1, -1), kvg.reshape(1, -1),
                      wuq_t, wukv_t)


def _kq_mask(t):
    krow = lax.broadcasted_iota(jnp.int32, (t, t), 0)
    qcol = lax.broadcasted_iota(jnp.int32, (t, t), 1)
    return krow <= qcol


def _flash_fwd(qc, kc, vt, *, name, comm=None):
    S = qc.shape[0]
    t = min(TQ, S)
    n = S // t

    def kern(q_ref, k_ref, vt_ref, o_ref, lse_ref, m_s, l_s, acc_s):
        qi = pl.program_id(1)
        m_s[...] = jnp.full_like(m_s, -jnp.inf)
        l_s[...] = jnp.zeros_like(l_s)
        acc_s[...] = jnp.zeros_like(acc_s)

        def scores(kb):
            k0 = pl.multiple_of(kb * t, t)
            return lax.dot_general(k_ref[pl.ds(k0, t), :], q_ref[...], NT, preferred_element_type=F32)

        def update(kb, st, masked):
            if masked:
                st = jnp.where(_kq_mask(t), st, -jnp.inf)
            m_prev = m_s[...]
            m_new = jnp.maximum(m_prev, jnp.max(st, axis=0, keepdims=True))
            a = jnp.exp2((m_prev - m_new) * SCALE_LOG2E)
            pt = jnp.exp2((st - m_new) * SCALE_LOG2E)
            l_s[...] = a * l_s[...] + jnp.sum(pt, axis=0, keepdims=True)
            acc_s[...] = a * acc_s[...] + jnp.dot(vt_ref[kb], pt.astype(BF16), preferred_element_type=F32)
            m_s[...] = m_new

        def group(kb, count, last_masked):
            sts = [scores(kb + g) for g in range(count)]
            for g in range(count):
                update(kb + g, sts[g], last_masked and g == count - 1)

        def body(i, carry):
            group(FWD_GROUP * i, FWD_GROUP, False)
            return carry

        full = qi // FWD_GROUP
        lax.fori_loop(0, full, body, 0)
        for rem in range(FWD_GROUP):
            @pl.when(qi - FWD_GROUP * full == rem)
            def _():
                group(qi - rem, rem + 1, True)
        o_ref[...] = jnp.transpose(acc_s[...] / l_s[...])
        lse_ref[pl.ds(qi, 1), :] = m_s[...] * SCALE_LOG2E + jnp.log2(l_s[...])

    q_spec = pl.BlockSpec((t, 256), lambda h, qi: (qi, h))
    k_spec = pl.BlockSpec((S, 256), lambda h, qi: (0, h))
    vt_spec = pl.BlockSpec((None, n, 128, t), lambda h, qi: (h, 0, 0, 0))
    o_spec = pl.BlockSpec((t, 128), lambda h, qi: (qi, h))
    lse_spec = pl.BlockSpec((None, n, t), lambda h, qi: (h, 0, 0))
    return _pcall(kern, name=name,
                  out_shape=(jax.ShapeDtypeStruct((S, D_MLA), F32), jax.ShapeDtypeStruct((N_HEADS, n, t), F32)),
                  grid=(N_HEADS, n), in_specs=[q_spec, k_spec, vt_spec], out_specs=(o_spec, lse_spec),
                  scratch=[pltpu.VMEM((1, t), F32), pltpu.VMEM((1, t), F32), pltpu.VMEM((128, t), F32)],
                  dims=("parallel", "arbitrary"), vmem_mb=48, comm=comm)(qc, kc, vt)


def _attn_delta(o, do, *, name):
    S = o.shape[0]
    t = min(TQ, S)
    n = S // t

    def kern(o_ref, do_ref, dl_ref):
        i = pl.program_id(0)
        prod = o_ref[...] * do_ref[...]
        lane = lax.broadcasted_iota(jnp.int32, (t, LANE), 1)
        dmat = jnp.zeros((t, LANE), F32)
        for h in range(N_HEADS):
            dmat = jnp.where(lane == h, jnp.sum(prod[:, 128 * h:128 * h + 128], axis=1, keepdims=True), dmat)
        dmat_t = jnp.transpose(dmat)
        for h in range(N_HEADS):
            dl_ref[h, pl.ds(i, 1), :] = dmat_t[h:h + 1, :]

    row = pl.BlockSpec((t, D_MLA), lambda i: (i, 0))
    return _pcall(kern, name=name, out_shape=jax.ShapeDtypeStruct((N_HEADS, n, t), F32), grid=(n,),
                  in_specs=[row, row], out_specs=pl.BlockSpec((N_HEADS, n, t), lambda i: (0, 0, 0)),
                  dims=("arbitrary",), vmem_mb=48)(o, do)


def _flash_bwd(qc, kc, v, do, lse2, delta, cos_t, sin_t, *, name, comm=None):
    S = qc.shape[0]
    t = min(TQ, S)
    n = S // t

    def kern(q_ref, k_ref, v_ref, do_ref, lse_ref, dl_ref, cos_ref, sin_ref, dqb_ref, dkvb_ref, dkr_ref,
             dq_ref, dk_ref, dv_ref):
        ki = pl.program_id(1)

        @pl.when(ki == 0)
        def _():
            dq_ref[...] = jnp.zeros_like(dq_ref)

        dk_ref[...] = jnp.zeros_like(dk_ref)
        dv_ref[...] = jnp.zeros_like(dv_ref)

        def step(qb, masked):
            q0 = pl.multiple_of(qb * t, t)
            kt = k_ref[...]
            qblk = q_ref[pl.ds(q0, t), :]
            dob = do_ref[pl.ds(q0, t), :].astype(BF16)
            st = lax.dot_general(kt, qblk, NT, preferred_element_type=F32)
            pt = jnp.exp2(st * SCALE_LOG2E - lse_ref[pl.ds(qb, 1), :])
            if masked:
                pt = jnp.where(_kq_mask(t), pt, 0.0)
            dv_ref[...] += jnp.dot(pt.astype(BF16), dob, preferred_element_type=F32)
            dpt = lax.dot_general(v_ref[...], dob, NT, preferred_element_type=F32)
            dst = (pt * (dpt - dl_ref[pl.ds(qb, 1), :]) * SCALE).astype(BF16)
            dk_ref[...] += jnp.dot(dst, qblk, preferred_element_type=F32)
            dq_ref[pl.ds(q0, t), :] += lax.dot_general(dst, kt, TN, preferred_element_type=F32)

        step(ki, True)
        rest = n - 1 - ki

        def body(i, carry):
            step(ki + 1 + 2 * i, False)
            step(ki + 2 + 2 * i, False)
            return carry

        lax.fori_loop(0, rest // 2, body, 0)

        @pl.when(rest % 2 == 1)
        def _():
            step(n - 1, False)

        dkvb_ref[:, 0:128] = dk_ref[:, 0:128].astype(BF16)
        dkvb_ref[:, 128:256] = dv_ref[...].astype(BF16)
        dkr_ref[...] = dk_ref[:, 128:256]

        @pl.when(ki == n - 1)
        def _():
            dqb_ref[:, 0:128] = dq_ref[:, 0:128].astype(BF16)
            dqr = dq_ref[:, 128:256]
            dqb_ref[:, 128:256] = (dqr * cos_ref[...] - _rot_sum(dqr) * sin_ref[...]).astype(BF16)

    def whole(w):
        return pl.BlockSpec((S, w), lambda h, ki: (0, h))

    def krow(w):
        return pl.BlockSpec((t, w), lambda h, ki: (ki, h))

    stat = pl.BlockSpec((None, n, t), lambda h, ki: (h, 0, 0))
    table = pl.BlockSpec((S, 128), lambda h, ki: (0, 0))
    return _pcall(kern, name=name,
                  out_shape=(jax.ShapeDtypeStruct((S, 2048), BF16), jax.ShapeDtypeStruct((S, 2048), BF16),
                             jax.ShapeDtypeStruct((S, D_MLA), F32)),
                  grid=(N_HEADS, n),
                  in_specs=[whole(256), krow(256), krow(128), whole(128), stat, stat, table, table],
                  out_specs=(whole(256), krow(256), krow(128)),
                  scratch=[pltpu.VMEM((S, 256), F32), pltpu.VMEM((t, 256), F32), pltpu.VMEM((t, 128), F32)],
                  dims=("parallel", "arbitrary"), vmem_mb=56, comm=comm)(qc, kc, v, do, lse2, delta, cos_t, sin_t)


def _mixer_specs(S, tm):
    hb = tm // HALO
    last_hb = S // HALO - 1

    def main(w, blk):
        return pl.BlockSpec((tm, w), lambda i: (i, blk))

    def prev(w, blk):
        return pl.BlockSpec((HALO, w), lambda i: (jnp.maximum(i * hb - 1, 0), blk))

    def nxt(w, blk):
        return pl.BlockSpec((HALO, w), lambda i: (jnp.minimum((i + 1) * hb, last_hb), blk))

    def full(shape):
        return pl.BlockSpec(shape, lambda i: (0,) * len(shape))

    return main, prev, nxt, full


def _fill_halo(i, xp, xu, hp_ref, hch_ref, hcc_ref, pin_ref, ch_ref, cc_ref, tm):
    first = i == 0
    xp[0:HALO, :] = jnp.where(first, 0.0, hp_ref[...])
    xp[HALO:HALO + tm, :] = pin_ref[...]
    xu[0:HALO, :] = jnp.where(first, 0.0, hch_ref[...] * hcc_ref[...])
    xu[HALO:HALO + tm, :] = cc_ref[...] * ch_ref[...]


def _pooled(xp, g, t1, tm):
    w = POOL_WINDOWS[g]
    lanes = slice(128 * g, 128 * g + 128)
    x0 = xp[HALO:HALO + tm, lanes]
    acc = x0
    for k in range(1, w):
        acc = acc + xp[HALO - k:HALO - k + tm, lanes]
    return acc / jnp.minimum(t1, float(w)) - x0


def _conv_fwd(xu, cw_ref, tm):
    return (cw_ref[0:1, :] * xu[HALO - 2:HALO - 2 + tm, :] + cw_ref[1:2, :] * xu[HALO - 1:HALO - 1 + tm, :]
            + cw_ref[2:3, :] * xu[HALO:HALO + tm, :])


def _mixer_fwd(proj, o, wpool, ps, convw, *, name):
    S = proj.shape[0]
    tm = min(256, S)
    main, prev, _, full = _mixer_specs(S, tm)

    def kern(gm_ref, pin_ref, gp_ref, ch_ref, cb_ref, cc_ref, gc_ref, hp_ref, hch_ref, hcc_ref,
             o_ref, wp_ref, ps_ref, cw_ref, mix_ref, xp, xu):
        i = pl.program_id(0)
        _fill_halo(i, xp, xu, hp_ref, hch_ref, hcc_ref, pin_ref, ch_ref, cc_ref, tm)
        t1 = (i * tm + lax.broadcasted_iota(jnp.int32, (tm, 1), 0) + 1).astype(F32)
        for g in range(4):
            lanes = slice(128 * g, 128 * g + 128)
            pooled = _pooled(xp, g, t1, tm)
            z = jnp.dot(pooled.astype(BF16), wp_ref[g].astype(BF16), preferred_element_type=F32)
            gp = gp_ref[:, lanes]
            y = z * ps_ref[:, lanes] * (gp * _sigmoid(gp))
            mix_ref[:, 1024 + 128 * g:1024 + 128 * g + 128] = y.astype(BF16)
        gc = gc_ref[...]
        mix_ref[:, 1536:2048] = (cb_ref[...] * _conv_fwd(xu, cw_ref, tm) * (gc * _sigmoid(gc))).astype(BF16)
        gm = gm_ref[...]
        mix_ref[:, 0:1024] = (o_ref[...] * (gm * _sigmoid(gm))).astype(BF16)

    return _pcall(kern, name=name, out_shape=jax.ShapeDtypeStruct((S, 2048), BF16), grid=(S // tm,),
                  in_specs=[main(1024, 1), main(512, 4), main(512, 5), main(512, 6), main(512, 7), main(512, 8),
                            main(512, 9), prev(512, 4), prev(512, 6), prev(512, 8),
                            main(1024, 0), full((4, 128, 128)), full((1, 512)), full((3, 512))],
                  out_specs=main(2048, 0),
                  scratch=[pltpu.VMEM((tm + HALO, 512), F32), pltpu.VMEM((tm + HALO, 512), F32)],
                  dims=("parallel",), vmem_mb=48)(
                      proj, proj, proj, proj, proj, proj, proj, proj, proj, proj, o, wpool, ps.reshape(1, 512), convw)


def _mixer_bwd(dmix, proj, o, wpool, ps, convw, *, name):
    S = proj.shape[0]
    tm = min(256, S)
    n = S // tm
    main, prev, nxt, full = _mixer_specs(S, tm)

    def kern(dm_ref, dmn_ref, gm_ref, pin_ref, gp_ref, ch_ref, cb_ref, cc_ref, gc_ref,
             hp_ref, hch_ref, hcc_ref, gpn_ref, cbn_ref, gcn_ref, o_ref, wp_ref, ps_ref, cw_ref,
             d_ref, do_ref, dwp_ref, dps_ref, dcw_ref, xp, xu, ee, ed):
        i = pl.program_id(0)
        last = i == n - 1

        @pl.when(i == 0)
        def _():
            dwp_ref[...] = jnp.zeros_like(dwp_ref)
            dps_ref[...] = jnp.zeros_like(dps_ref)
            dcw_ref[...] = jnp.zeros_like(dcw_ref)

        _fill_halo(i, xp, xu, hp_ref, hch_ref, hcc_ref, pin_ref, ch_ref, cc_ref, tm)
        t1 = (i * tm + lax.broadcasted_iota(jnp.int32, (tm, 1), 0) + 1).astype(F32)
        t1n = ((i + 1) * tm + lax.broadcasted_iota(jnp.int32, (HALO, 1), 0) + 1).astype(F32)
        c_pin, c_gp, c_ch, c_cb, c_cc, c_gc = 1024, 1536, 2048, 2560, 3072, 3584

        for g in range(4):
            w = float(POOL_WINDOWS[g])
            lanes = slice(128 * g, 128 * g + 128)
            pooled = _pooled(xp, g, t1, tm)
            pb = pooled.astype(BF16)
            wp = wp_ref[g].astype(BF16)
            z = jnp.dot(pb, wp, preferred_element_type=F32)
            psl = ps_ref[:, lanes]
            sg, dsg = _silu_and_grad(gp_ref[:, lanes])
            dmp = dm_ref[:, 1024 + 128 * g:1024 + 128 * g + 128]
            dyp = dmp * sg
            d_ref[:, c_gp + 128 * g:c_gp + 128 * g + 128] = (dmp * (z * psl) * dsg).astype(BF16)
            dps_ref[:, lanes] += jnp.sum(dyp * z, axis=0, keepdims=True)
            dz = (dyp * psl).astype(BF16)
            dwp_ref[g] += lax.dot_general(pb, dz, TN, preferred_element_type=F32)
            dpl = lax.dot_general(dz, wp, NT, preferred_element_type=F32)
            ee[0:tm, lanes] = dpl / jnp.minimum(t1, w)
            gpn = gpn_ref[:, lanes]
            dzn = (dmn_ref[:, lanes] * (gpn * _sigmoid(gpn)) * psl).astype(BF16)
            dpn = lax.dot_general(dzn, wp, NT, preferred_element_type=F32)
            ee[tm:tm + HALO, lanes] = jnp.where(last, 0.0, dpn / jnp.minimum(t1n, w))
            acc = ee[0:tm, lanes]
            for k in range(1, POOL_WINDOWS[g]):
                acc = acc + ee[k:k + tm, lanes]
            d_ref[:, c_pin + 128 * g:c_pin + 128 * g + 128] = (acc - dpl).astype(BF16)

        yc = _conv_fwd(xu, cw_ref, tm)
        sgc, dsgc = _silu_and_grad(gc_ref[...])
        cb = cb_ref[...]
        dmc = dm_ref[:, 1536:2048]
        d_ref[:, c_gc:c_gc + 512] = (dmc * cb * yc * dsgc).astype(BF16)
        d_ref[:, c_cb:c_cb + 512] = (dmc * yc * sgc).astype(BF16)
        dyc = dmc * cb * sgc
        ed[0:tm, :] = dyc
        gcn = gcn_ref[...]
        ed[tm:tm + HALO, :] = jnp.where(last, 0.0, dmn_ref[:, 512:1024] * cbn_ref[...] * (gcn * _sigmoid(gcn)))
        dcw_ref[0:1, :] += jnp.sum(dyc * xu[HALO - 2:HALO - 2 + tm, :], axis=0, keepdims=True)
        dcw_ref[1:2, :] += jnp.sum(dyc * xu[HALO - 1:HALO - 1 + tm, :], axis=0, keepdims=True)
        dcw_ref[2:3, :] += jnp.sum(dyc * xu[HALO:HALO + tm, :], axis=0, keepdims=True)
        du = cw_ref[2:3, :] * dyc + cw_ref[1:2, :] * ed[1:1 + tm, :] + cw_ref[0:1, :] * ed[2:2 + tm, :]
        d_ref[:, c_cc:c_cc + 512] = (du * ch_ref[...]).astype(BF16)
        d_ref[:, c_ch:c_ch + 512] = (du * cc_ref[...]).astype(BF16)

        sgm, dsgm = _silu_and_grad(gm_ref[...])
        dmm = dm_ref[:, 0:1024]
        do_ref[...] = dmm * sgm
        d_ref[:, 0:1024] = (dmm * o_ref[...] * dsgm).astype(BF16)

    outs = (jax.ShapeDtypeStruct((S, W_MIX), BF16), jax.ShapeDtypeStruct((S, 1024), F32),
            jax.ShapeDtypeStruct((4, 128, 128), F32), jax.ShapeDtypeStruct((1, 512), F32),
            jax.ShapeDtypeStruct((3, 512), F32))
    scr = [pltpu.VMEM((tm + HALO, 512), F32) for _ in range(4)]
    return _pcall(kern, name=name, out_shape=outs, grid=(n,),
                  in_specs=[main(2048, 0), nxt(1024, 1),
                            main(1024, 1), main(512, 4), main(512, 5), main(512, 6), main(512, 7), main(512, 8),
                            main(512, 9), prev(512, 4), prev(512, 6), prev(512, 8),
                            nxt(512, 5), nxt(512, 7), nxt(512, 9),
                            main(1024, 0), full((4, 128, 128)), full((1, 512)), full((3, 512))],
                  out_specs=(main(W_MIX, 0), main(1024, 0), full((4, 128, 128)), full((1, 512)), full((3, 512))),
                  scratch=scr, dims=("arbitrary",), vmem_mb=56)(
                      dmix, dmix, proj, proj, proj, proj, proj, proj, proj, proj, proj, proj, proj, proj, proj,
                      o, wpool, ps.reshape(1, 512), convw)


def _outproj_ln(mix, wout, h, bout, g, b, *, name):
    S, Dm = h.shape
    tm = min(256, S)

    def kern(mix_ref, w_ref, h_ref, bo_ref, g_ref, b_ref, y_ref, yb_ref, r_ref):
        out = jnp.dot(mix_ref[...], w_ref[...], preferred_element_type=F32) + bo_ref[...]
        r = ALPHA * h_ref[...] + out
        r_ref[...] = r
        mu = jnp.mean(r, axis=-1, keepdims=True)
        xc = r - mu
        var = jnp.mean(xc * xc, axis=-1, keepdims=True)
        y = xc * lax.rsqrt(var + LN_EPS) * g_ref[...] + b_ref[...]
        y_ref[...] = y
        yb_ref[...] = y.astype(BF16)

    row = pl.BlockSpec((tm, Dm), lambda i: (i, 0))
    vec = pl.BlockSpec((1, Dm), lambda i: (0, 0))
    wsp = pl.BlockSpec((Dm, Dm), lambda i: (0, 0))
    sds = jax.ShapeDtypeStruct((S, Dm), F32)
    return _pcall(kern, name=name, out_shape=(sds, jax.ShapeDtypeStruct((S, Dm), BF16), sds), grid=(S // tm,),
                  in_specs=[row, wsp, row, vec, vec, vec], out_specs=(row, row, row), dims=("parallel",),
                  vmem_mb=56)(
                      mix, wout, h, bout.reshape(1, Dm), g.reshape(1, Dm), b.reshape(1, Dm))


def _adamw_math(w, g, m, v):
    m = ADAM_B1 * m + (1.0 - ADAM_B1) * g
    v = ADAM_B2 * v + (1.0 - ADAM_B2) * (g * g)
    m_hat = m / (1.0 - ADAM_B1 ** ADAM_STEP)
    v_hat = v / (1.0 - ADAM_B2 ** ADAM_STEP)
    delta = -ADAM_LR * (m_hat / (jnp.sqrt(v_hat) + ADAM_EPS) + ADAM_WD * w)
    return delta, m, v


def _row_tile(R, C):
    best = None
    for cand in range(8, R, 8):
        if R % cand == 0 and cand * C <= 256 * 1024:
            best = cand
    return best if best is not None else R


def _adamw(w, g, m, v, *, name):
    shape = w.shape
    C = shape[-1]
    R = 1
    for s in shape[:-1]:
        R *= s
    tr = _row_tile(R, C)

    def kern(w_ref, g_ref, m_ref, v_ref, d_ref, mo_ref, vo_ref):
        d, mn, vn = _adamw_math(w_ref[...], g_ref[...], m_ref[...], v_ref[...])
        d_ref[...] = d
        mo_ref[...] = mn
        vo_ref[...] = vn

    blk = pl.BlockSpec((tr, C), lambda i: (i, 0))
    sds = jax.ShapeDtypeStruct((R, C), F32)
    outs = _pcall(kern, name=name, out_shape=(sds, sds, sds), grid=(R // tr,), in_specs=[blk] * 4,
                  out_specs=(blk, blk, blk), dims=("parallel",), vmem_mb=48)(
                      w.reshape(R, C), g.reshape(R, C), m.reshape(R, C), v.reshape(R, C))
    return tuple(t.reshape(shape) for t in outs)


def _adamw_halves(w, m, v, halves, c_idx, *, name):
    _, R, C = w.shape
    ch = C // 2
    tr = _row_tile(R, ch)
    nb = R // tr

    def kern(c_ref, w_ref, a0_ref, b0_ref, a1_ref, b1_ref, m_ref, v_ref, g_ref, d_ref, mo_ref, vo_ref):
        layer = pl.program_id(0) // nb
        mine = pl.program_id(1) == c_ref[0]
        g = jnp.where(layer == 0, jnp.where(mine, a0_ref[...], b0_ref[...]),
                      jnp.where(mine, a1_ref[...], b1_ref[...]))
        g_ref[...] = g
        d, mn, vn = _adamw_math(w_ref[...], g, m_ref[...], v_ref[...])
        d_ref[...] = d
        mo_ref[...] = mn
        vo_ref[...] = vn

    full = pl.BlockSpec((tr, ch), lambda i, hc, c: (i, hc))
    half = pl.BlockSpec((tr, ch), lambda i, hc, c: (i % nb, 0))
    gs = pltpu.PrefetchScalarGridSpec(num_scalar_prefetch=1, grid=(2 * nb, 2),
                                      in_specs=[full, half, half, half, half, full, full], out_specs=(full,) * 4)
    sds = jax.ShapeDtypeStruct((2 * R, C), F32)
    (a0, b0), (a1, b1) = halves
    outs = pl.pallas_call(kern, name=name, out_shape=(sds,) * 4, grid_spec=gs,
                          compiler_params=pltpu.CompilerParams(dimension_semantics=("parallel", "parallel"),
                                                               vmem_limit_bytes=48 << 20))(
                              c_idx, w.reshape(2 * R, C), a0, b0, a1, b1, m.reshape(2 * R, C), v.reshape(2 * R, C))
    return tuple(t.reshape(2, R, C) for t in outs)


def _small_sum_adamw(gathered, w, m, v, *, name):
    R = w.shape[0]

    def kern(ga_ref, w_ref, m_ref, v_ref, g_ref, d_ref, mo_ref, vo_ref):
        g = ga_ref[0]
        for k in range(1, N_DEV):
            g = g + ga_ref[k]
        g_ref[...] = g
        d, mn, vn = _adamw_math(w_ref[...], g, m_ref[...], v_ref[...])
        d_ref[...] = d
        mo_ref[...] = mn
        vo_ref[...] = vn

    sds = jax.ShapeDtypeStruct((R, LANE), F32)
    return _pcall(kern, name=name, out_shape=(sds, sds, sds, sds), vmem_mb=48)(gathered, w, m, v)


def _pair_sum(g, theirs, c_idx, *, name):
    R, C = g.shape
    ch = C // 2
    tr = _row_tile(R, ch)

    def kern(c_ref, a_ref, b_ref, o_ref):
        o_ref[...] = (a_ref[...] + b_ref[...]).astype(BF16)

    gs = pltpu.PrefetchScalarGridSpec(
        num_scalar_prefetch=1, grid=(R // tr,),
        in_specs=[pl.BlockSpec((tr, ch), lambda i, c: (i, c[0])), pl.BlockSpec((tr, ch), lambda i, c: (i, 0))],
        out_specs=pl.BlockSpec((tr, ch), lambda i, c: (i, 0)))
    return pl.pallas_call(kern, name=name, out_shape=jax.ShapeDtypeStruct((R, ch), BF16), grid_spec=gs,
                          compiler_params=pltpu.CompilerParams(dimension_semantics=("parallel",),
                                                               vmem_limit_bytes=48 << 20))(c_idx, g, theirs)


WeightRows = collections.namedtuple("WeightRows", "full_rows own_rows cols pieces zero_rows")


def _w_in_piece_a(j):
    return jnp.where(j == 0, 0, 1232 * j + GAP)


def _w_in_piece_b(j):
    return jnp.where(j == 0, GAP_AT + GAP, 1232 * j + GAP_AT + GAP)


W_IN = WeightRows(NP, 1232, D_MODEL, ((0, GAP_AT, _w_in_piece_a), (GAP_AT, 1232 - GAP_AT, _w_in_piece_b)),
                  ((GAP_AT, GAP),))
W_OUT = WeightRows(2048, 512, D_MODEL, ((0, 512, lambda j: 512 * j),), ())
W_UQ = WeightRows(2048, 384, Q_LORA, ((0, 192, lambda j: 512 * j), (192, 192, lambda j: 512 * j + 256)),
                  tuple((256 * h + 192, 64) for h in range(N_HEADS)))
W_UKV = WeightRows(2048, 512, KV_LORA, ((0, 512, lambda j: 512 * j),), ())
W_CONV = WeightRows(64, 16, 256, ((0, 16, lambda j: 16 * j),), ())
SHARDED = (W_IN, W_OUT, W_UQ, W_UKV)
SHARDED_NAMES = ("w_in", "w_out", "w_uq", "w_ukv")
WEIGHT_ROWS = dict(zip(SHARDED_NAMES, SHARDED))


def _mesh_pos():
    x, y, c = lax.axis_index("x"), lax.axis_index("y"), lax.axis_index("c")
    return x, y, c


def _other_chips(x, y):
    return [(1 - x, y), (x, 1 - y), (1 - x, 1 - y)]


def _rows(start, n):
    return pl.ds(pl.multiple_of(start, 16), n)


def _half_cols(spec, c):
    ch = spec.cols // 2
    return pl.ds(pl.multiple_of(c * ch, LANE), ch)


def _allgather_script(specs, shards, zeros):
    na = len(specs)
    zlist = [a for a in range(na) if zeros[a] is not None]
    plan_first, plan_own, plan_zero = [], [], []
    for a, spec in enumerate(specs):
        for p in range(len(spec.pieces)):
            plan_own.append((a, p))
            for k in range(3):
                plan_first.append((a, p, k))
        for z in range(len(spec.zero_rows)):
            for l in range(shards[a].shape[0]):
                plan_zero.append((a, z, l))
    nf = len(plan_first)
    n_sems = 2 * nf + len(plan_own) + len(plan_zero)

    def copies(ins_all, outs, send_sems, recv_sems):
        ins = ins_all[:na]
        zrefs = dict(zip(zlist, ins_all[na:]))
        x, y, c = _mesh_pos()
        j = 2 * x + y
        chips = _other_chips(x, y)
        sibling = (x, y, 1 - c)

        def remote(src, dst, sem, to):
            return pltpu.make_async_remote_copy(src_ref=src, dst_ref=dst, send_sem=send_sems.at[sem],
                                                recv_sem=recv_sems.at[sem], device_id=to, device_id_type=MESH)

        def block(a, p, chip, cols):
            _, n, dst = specs[a].pieces[p]
            return outs[a].at[:, _rows(dst(chip), n), cols]

        def first(i):
            a, p, k = plan_first[i]
            src0, n, _ = specs[a].pieces[p]
            cols = _half_cols(specs[a], c)
            return remote(ins[a].at[:, pl.ds(src0, n), cols], block(a, p, j, cols), i, (*chips[k], c))

        def landed(i, half):
            a, p, k = plan_first[i]
            return block(a, p, 2 * chips[k][0] + chips[k][1], _half_cols(specs[a], half))

        def arrival(i, half, sem):
            return remote(landed(i, half), landed(i, half), sem, sibling)

        def passed(i):
            return remote(landed(i, c), landed(i, c), nf + i, sibling)

        def own(i):
            a, p = plan_own[i]
            src0, n, _ = specs[a].pieces[p]
            return remote(ins[a].at[:, pl.ds(src0, n), :], block(a, p, j, slice(None)), 2 * nf + i, sibling)

        def zero(i):
            a, z, l = plan_zero[i]
            r0, n = specs[a].zero_rows[z]
            return remote(zrefs[a].at[pl.ds(0, n), :], outs[a].at[l, pl.ds(r0, n), :],
                          2 * nf + len(plan_own) + i, sibling)

        fixed = [own(i) for i in range(len(plan_own))] + [zero(i) for i in range(len(plan_zero))]
        return c, fixed, first, arrival, passed

    def start(ins, outs, send_sems, recv_sems):
        _, fixed, first, _, _ = copies(ins, outs, send_sems, recv_sems)
        for cp in fixed:
            cp.start()
        for i in range(nf):
            first(i).start()

    def finish(ins, outs, send_sems, recv_sems):
        c, fixed, first, arrival, passed = copies(ins, outs, send_sems, recv_sems)
        for i in range(nf):
            arrival(i, c, i).wait_recv()
            passed(i).start()
        for i in range(nf):
            arrival(i, 1 - c, nf + i).wait_recv()
        for cp in fixed:
            cp.wait()
        for i in range(nf):
            first(i).wait_send()
            passed(i).wait_send()

    out_shape = tuple(jax.ShapeDtypeStruct((shards[a].shape[0], spec.full_rows, spec.cols), BF16)
                      for a, spec in enumerate(specs))
    args = tuple(shards) + tuple(zeros[a] for a in zlist)
    return CommScript(args, out_shape, n_sems, start, finish)


def _start_all_wait_all(args, out_shape, n_sems, make_copies):
    def start(ins, outs, send_sems, recv_sems):
        for cp in make_copies(ins, outs, send_sems, recv_sems):
            cp.start()

    def finish(ins, outs, send_sems, recv_sems):
        for cp in make_copies(ins, outs, send_sems, recv_sems):
            cp.wait()

    return CommScript(tuple(args), tuple(out_shape), n_sems, start, finish)


def _exchange_script(specs, grads):
    na = len(grads)

    def make_copies(ins, outs, send_sems, recv_sems):
        x, y, c = _mesh_pos()
        return [pltpu.make_async_remote_copy(
            src_ref=ins[a].at[:, _half_cols(specs[a], 1 - c)], dst_ref=outs[a], send_sem=send_sems.at[a],
            recv_sem=recv_sems.at[a], device_id=(x, y, 1 - c), device_id_type=MESH) for a in range(na)]

    out_shape = [jax.ShapeDtypeStruct((s.full_rows, s.cols // 2), F32) for s in specs]
    return _start_all_wait_all(grads, out_shape, na, make_copies)


def _scatter_script(specs, parts):
    na = len(parts)
    plan = [(a, p, k) for a in range(na) for p in range(len(specs[a].pieces)) for k in range(3)]

    def make_copies(ins, outs, send_sems, recv_sems):
        x, y, c = _mesh_pos()
        chips = _other_chips(x, y)
        copies = []
        for i, (a, p, k) in enumerate(plan):
            src0, n, dst = specs[a].pieces[p]
            pk = 2 * chips[k][0] + chips[k][1]
            copies.append(pltpu.make_async_remote_copy(
                src_ref=ins[a].at[_rows(dst(pk), n), :], dst_ref=outs[a].at[k, pl.ds(src0, n), :],
                send_sem=send_sems.at[i], recv_sem=recv_sems.at[i], device_id=(*chips[k], c), device_id_type=MESH))
        return copies

    out_shape = [jax.ShapeDtypeStruct((3, s.own_rows, s.cols // 2), BF16) for s in specs]
    return _start_all_wait_all(parts, out_shape, len(plan), make_copies)


def _chip_sum(spec, part, recv, *, name):
    ch = spec.cols // 2
    npieces = len(spec.pieces)

    def kern(recv_ref, part_ref, o_ref, own_ref, sems):
        j = 2 * lax.axis_index("x") + lax.axis_index("y")
        copies = []
        for p, (src0, n, dst) in enumerate(spec.pieces):
            copies.append(pltpu.make_async_copy(part_ref.at[_rows(dst(j), n), :], own_ref.at[pl.ds(src0, n), :],
                                                sems.at[p]))
        for cp in copies:
            cp.start()
        for cp in copies:
            cp.wait()
        o_ref[...] = ((own_ref[...].astype(F32) + recv_ref[0].astype(F32)) + recv_ref[1].astype(F32)) \
            + recv_ref[2].astype(F32)

    vm = pl.BlockSpec(memory_space=pltpu.VMEM)
    return _pcall(kern, name=name, out_shape=jax.ShapeDtypeStruct((spec.own_rows, ch), F32),
                  in_specs=[vm, HBM_SPEC], out_specs=vm,
                  scratch=[pltpu.VMEM((spec.own_rows, ch), BF16), pltpu.SemaphoreType.DMA((npieces,))],
                  vmem_mb=48)(recv, part)


def _sibling_script(sums):
    na = len(sums)

    def make_copies(ins, outs, send_sems, recv_sems):
        x, y, c = _mesh_pos()
        return [pltpu.make_async_remote_copy(
            src_ref=ins[a], dst_ref=outs[a], send_sem=send_sems.at[a], recv_sem=recv_sems.at[a],
            device_id=(x, y, 1 - c), device_id_type=MESH) for a in range(na)]

    out_shape = [jax.ShapeDtypeStruct(t.shape, t.dtype) for t in sums]
    return _start_all_wait_all(sums, out_shape, na, make_copies)


class _SemWindow:
    def __init__(self, sems, offset):
        self._sems, self._offset = sems, offset

    @property
    def at(self):
        return self

    def __getitem__(self, i):
        return self._sems.at[i + self._offset]


def _merge_scripts(*scripts):
    a_off, o_off, s_off = [0], [0], [0]
    for s in scripts:
        a_off.append(a_off[-1] + len(s.args))
        o_off.append(o_off[-1] + len(s.out_shape))
        s_off.append(s_off[-1] + s.n_sems)

    def phase(which):
        def run(ins, outs, send_sems, recv_sems):
            for n, s in enumerate(scripts):
                getattr(s, which)(ins[a_off[n]:a_off[n + 1]], outs[o_off[n]:o_off[n + 1]],
                                  _SemWindow(send_sems, s_off[n]), _SemWindow(recv_sems, s_off[n]))
        return run

    return CommScript(sum((tuple(s.args) for s in scripts), ()), sum((tuple(s.out_shape) for s in scripts), ()),
                      s_off[-1], phase("start"), phase("finish"))


class _GradReducer:
    def __init__(self, layer, names, grads, c_idx):
        self.specs = tuple(WEIGHT_ROWS[nm] for nm in names)
        self.grads, self.c_idx = tuple(grads), c_idx
        self.names = [f"{nm}{layer}" for nm in names]

    def exchange(self):
        return _exchange_script(self.specs, self.grads)

    def scatter(self, theirs):
        self.parts = tuple(_pair_sum(g, th, self.c_idx, name=f"pair_sum_{nm}")
                           for g, th, nm in zip(self.grads, theirs, self.names))
        return _scatter_script(self.specs, self.parts)

    def sibling(self, recv):
        self.sums = tuple(_chip_sum(s, p, r, name=f"chip_sum_{nm}")
                          for s, p, r, nm in zip(self.specs, self.parts, recv, self.names))
        return _sibling_script(self.sums)

    def done(self, others):
        return list(zip(self.sums, others))


def _allgather_small(block, *, name):
    m_per, n = block.shape

    def body(x_ref, out_ref, send_sems, recv_sems, local_sem):
        x, y, c = _mesh_pos()
        me, sibling = (x, y, c), (x, y, 1 - c)
        chips = _other_chips(x, y)

        def rows(px, py, pc):
            return out_ref.at[4 * px + 2 * py + pc]

        def copy(k, blk, to, src=None):
            return pltpu.make_async_remote_copy(
                src_ref=rows(*blk) if src is None else src, dst_ref=rows(*blk), send_sem=send_sems.at[k],
                recv_sem=recv_sems.at[k], device_id=to, device_id_type=MESH)

        mine = pltpu.make_async_copy(x_ref, rows(*me), local_sem)
        mine.start()
        first = [copy(0, me, sibling, src=x_ref)]
        first += [copy(1 + k, me, (*chip, c), src=x_ref) for k, chip in enumerate(chips)]
        for cp in first:
            cp.start()
        passed = [copy(4 + k, (*chip, c), sibling) for k, chip in enumerate(chips)]
        for k, chip in enumerate(chips):
            copy(1 + k, (*chip, c), me).wait_recv()
            passed[k].start()
        copy(0, sibling, me).wait_recv()
        for k, chip in enumerate(chips):
            copy(4 + k, (*chip, 1 - c), me).wait_recv()
        for cp in first + passed:
            cp.wait_send()
        mine.wait()

    vm = pl.BlockSpec(memory_space=pltpu.VMEM)
    return _pcall(body, name=name, out_shape=jax.ShapeDtypeStruct((N_DEV, m_per, n), block.dtype),
                  in_specs=[vm], out_specs=vm,
                  scratch=[pltpu.SemaphoreType.DMA((7,)), pltpu.SemaphoreType.DMA((7,)), pltpu.SemaphoreType.DMA],
                  vmem_mb=48)(block)


def _rope_tables(positions):
    half = ROPE // 2
    inv_freq = ROPE_THETA ** (-jnp.arange(half, dtype=F32) / half)
    ang = positions.astype(F32)[:, None] * inv_freq
    cos, sin = jnp.cos(ang), jnp.sin(ang)
    S = positions.shape[0]
    cos_t = jnp.concatenate([cos, cos, jnp.ones((S, 64), F32)], axis=1)
    sin_t = jnp.concatenate([-sin, sin, jnp.zeros((S, 64), F32)], axis=1)
    return cos_t, sin_t


def _decode_conv(bits):
    rows = bits.reshape(DEPTH, N_CHIPS, 16, 256)[:, :, :3, :]
    conv = lax.bitcast_convert_type(rows.reshape(DEPTH, N_CHIPS, 3, 128, 2), F32)
    return jnp.transpose(conv, (0, 2, 1, 3)).reshape(DEPTH, 3, 512)


def _local_step(x, positions, target, emb_g, emb_b, w_in_t0, rest0, weights1, q_g, kv_g, w_pool, pool_scale,
                b_out, ln_g, ln_b, c_idx=None):
    cos_t, sin_t = _rope_tables(positions)
    if isinstance(w_in_t0, CommScript):
        (h, hb), (landed,) = _ln_fwd(x, emb_g, emb_b, name="emb_ln", comm=w_in_t0)
        w_in_t0 = landed[0]
    else:
        h, hb = _ln_fwd(x, emb_g, emb_b, name="emb_ln")
    weights = [None, weights1]
    saved = []
    for l in range(DEPTH):
        if l == 0 and isinstance(rest0, CommScript):
            proj, landed = _matmul(hb, w_in_t0, "nt", name="in_proj0", tm=1024, tn=1024, tk=2048, vmem_mb=56,
                                   comm=rest0)
            weights[0] = (w_in_t0,) + tuple(a[0] for a in landed[:3])
            conv_w = _decode_conv(landed[3])
        else:
            if l == 0:
                weights[0] = (w_in_t0,) + tuple(rest0[:3])
                conv_w = rest0[3]
            proj = _matmul(hb, weights[l][0], "nt", name=f"in_proj{l}", tm=1024, tn=1024, tk=2048, vmem_mb=56)
        w_in_t, w_out, w_uq_t, w_ukv_t = weights[l]
        qc, kc, v, vt, qn, kvn = _mla_qkv(proj, cos_t, sin_t, q_g[l], kv_g[l], w_uq_t, w_ukv_t, name=f"mla_qkv{l}")
        nxt = weights[l + 1] if l + 1 < DEPTH else None
        if isinstance(nxt, CommScript):
            (o, lse2), landed = _flash_fwd(qc, kc, vt, name=f"flash_fwd{l}", comm=nxt)
            weights[l + 1] = tuple(a[0] for a in landed)
        else:
            o, lse2 = _flash_fwd(qc, kc, vt, name=f"flash_fwd{l}")
        mix = _mixer_fwd(proj, o, w_pool[l], pool_scale[l], conv_w[l], name=f"mixer_fwd{l}")
        h_next, hb_next, r = _outproj_ln(mix, w_out, h, b_out[l], ln_g[l], ln_b[l], name=f"out_proj_ln{l}")
        saved.append((hb, proj, qc, kc, v, qn, kvn, o, lse2, mix, r))
        h, hb = h_next, hb_next

    y_final = h
    small = [None] * DEPTH
    big = [None] * DEPTH
    above = scatter_above = None
    for l in reversed(range(DEPTH)):
        w_in_t, w_out, w_uq_t, w_ukv_t = weights[l]
        hb_in, proj, qc, kc, v, qn, kvn, o, lse2, mix, r = saved[l]
        if l == DEPTH - 1:
            loss_acc, dr, drb, d_ln_g, d_ln_b, d_b_out = _loss_ln_bwd(y_final, target, r, ln_g[l], name="loss_ln_bwd")
        else:
            dr, drb, d_ln_g, d_ln_b, d_b_out = _ln_bwd(dh, r, ln_g[l], name=f"ln_bwd{l}")
        dmix = _matmul(drb, w_out, "nt", name=f"dmix{l}", tm=1024, tn=1024, tk=2048, vmem_mb=56)
        d_w_out = _matmul(mix, drb, "tn", name=f"dw_out{l}", tm=1024, tn=1024, tk=2048, vmem_mb=56)
        d_mix, do, d_w_pool, d_ps, d_conv = _mixer_bwd(dmix, proj, o, w_pool[l], pool_scale[l], conv_w[l],
                                                       name=f"mixer_bwd{l}")
        delta = _attn_delta(o, do, name=f"attn_delta{l}")
        if above is not None:
            (dqb, dkvb, dkr), recv = _flash_bwd(qc, kc, v, do, lse2, delta, cos_t, sin_t, name=f"flash_bwd{l}",
                                                comm=scatter_above)
            sibling_above = above.sibling(recv)
        else:
            dqb, dkvb, dkr = _flash_bwd(qc, kc, v, do, lse2, delta, cos_t, sin_t, name=f"flash_bwd{l}")
        d_mla, d_qg, d_kvg = _mla_qkv_bwd(dqb, dkvb, dkr, proj, cos_t, sin_t, q_g[l], kv_g[l], w_uq_t, w_ukv_t,
                                          name=f"mla_qkv_bwd{l}")
        d_w_uq_t = _matmul(dqb, qn, "tn", name=f"dw_uq{l}", tm=2048, tn=512, tk=2048, vmem_mb=56)
        d_w_ukv_t = _matmul(dkvb, kvn, "tn", name=f"dw_ukv{l}", tm=2048, tn=256, tk=2048, vmem_mb=56)
        small[l] = dict(q_g=d_qg[0], kv_g=d_kvg[0], w_pool=d_w_pool, pool_scale=d_ps[0], conv_w=d_conv,
                        b_out=d_b_out[0], ln_g=d_ln_g[0], ln_b=d_ln_b[0])
        rest = (d_w_out, d_w_uq_t, d_w_ukv_t)
        if c_idx is None:
            d_w_in_t = _dproj_t_times_h(d_mla, d_mix, hb_in, name=f"dw_in{l}")
            dh = _dproj_times_w(d_mla, d_mix, w_in_t, dr, ALPHA, name=f"dh{l}")
            big[l] = (d_w_in_t,) + rest
        elif l > 0:
            d_w_in_t = _dproj_t_times_h(d_mla, d_mix, hb_in, name=f"dw_in{l}")
            above = _GradReducer(l, SHARDED_NAMES, (d_w_in_t,) + rest, c_idx)
            dh, theirs = _dproj_times_w(d_mla, d_mix, w_in_t, dr, ALPHA, name=f"dh{l}", comm=above.exchange())
            scatter_above = above.scatter(theirs)
        else:
            red_rest = _GradReducer(l, SHARDED_NAMES[1:], rest, c_idx)
            d_w_in_t, landed = _dproj_t_times_h(d_mla, d_mix, hb_in, name=f"dw_in{l}",
                                                comm=_merge_scripts(sibling_above, red_rest.exchange()))
            big[l + 1] = above.done(landed[:len(SHARDED)])
            red_in = _GradReducer(l, SHARDED_NAMES[:1], (d_w_in_t,), c_idx)
            landed = _run_comm(_merge_scripts(red_in.exchange(), red_rest.scatter(landed[len(SHARDED):])),
                               name="exchange_w_in0")
            sibling_rest = red_rest.sibling(landed[1:])
            dh, landed = _dproj_times_w(d_mla, d_mix, w_in_t, dr, ALPHA, name=f"dh{l}",
                                        comm=_merge_scripts(red_in.scatter(landed[:1]), sibling_rest))
            recv_in, others_rest = landed[:1], landed[1:]
    grad_x, _, d_emb_g, d_emb_b, _ = _ln_bwd(dh, x, emb_g, name="emb_ln_bwd")
    if c_idx is not None:
        others_in = _run_comm(red_in.sibling(recv_in), name="send_to_sibling0")
        big[0] = red_in.done(others_in) + red_rest.done(others_rest)
    return loss_acc[0, 0], grad_x, d_emb_g[0], d_emb_b[0], small, big


SMALL_ORDER = ("emb_ln_g", "emb_ln_b", "q_norm_g", "kv_norm_g", "w_pool", "pool_scale", "b_out", "ln_g", "ln_b")


def _pack_small(arrs, extra_rows):
    flat = jnp.concatenate([a.reshape(-1) for a in arrs])
    rows = flat.shape[0] // LANE
    total = -(-(rows + extra_rows) // 8) * 8
    return jnp.pad(flat, (0, total * LANE - flat.shape[0])).reshape(total, LANE)


def _unpack_small(packed, shapes):
    flat = packed.reshape(-1)
    out, off = [], 0
    for shp in shapes:
        n = 1
        for s in shp:
            n *= s
        out.append(flat[off:off + n].reshape(shp))
        off += n
    return out, off


def kernel(x, positions, emb_ln_g, emb_ln_b, w_in, q_norm_g, kv_norm_g, w_uq, w_ukv, w_pool, pool_scale, conv_w, w_out, b_out, ln_g, ln_b, loss_target, m_emb_ln_g, m_emb_ln_b, m_w_in, m_q_norm_g, m_kv_norm_g, m_w_uq, m_w_ukv, m_w_pool, m_pool_scale, m_conv_w, m_w_out, m_b_out, m_ln_g, m_ln_b, v_emb_ln_g, v_emb_ln_b, v_w_in, v_q_norm_g, v_kv_norm_g, v_w_uq, v_w_ukv, v_w_pool, v_pool_scale, v_conv_w, v_w_out, v_b_out, v_ln_g, v_ln_b):
    xi, yi, ci = lax.axis_index("x"), lax.axis_index("y"), lax.axis_index("c")
    chip = 2 * xi + yi
    c_idx = ci.reshape(1).astype(jnp.int32)

    def t(a):
        return jnp.swapaxes(a, 1, 2)

    conv_bits = lax.bitcast_convert_type(conv_w.reshape(DEPTH, 3 * 128), BF16).reshape(DEPTH, 3, 256)
    conv_bits = jnp.pad(conv_bits, ((0, 0), (0, 13), (0, 0)))
    own = (t(w_in).astype(BF16), w_out.astype(BF16), t(w_uq).astype(BF16), t(w_ukv).astype(BF16))
    zeros = (jnp.zeros((GAP, D_MODEL), BF16), None, jnp.zeros((64, Q_LORA), BF16), None)
    gather_in0 = _allgather_script((W_IN,), (own[0][0:1],), zeros[:1])
    gather0 = _allgather_script(SHARDED[1:] + (W_CONV,), tuple(a[0:1] for a in own[1:]) + (conv_bits,),
                                zeros[1:] + (None,))
    gather1 = _allgather_script(SHARDED, tuple(a[1:2] for a in own), zeros)

    loss_part, grad_x, d_emb_g, d_emb_b, grads, reduced = _local_step(
        x[0], positions[0], loss_target[0], emb_ln_g, emb_ln_b, gather_in0, gather0, gather1, q_norm_g, kv_norm_g,
        w_pool, pool_scale, b_out, ln_g, ln_b, c_idx)

    small_g = [d_emb_g, d_emb_b,
               jnp.stack([grads[l]["q_g"] for l in range(DEPTH)]), jnp.stack([grads[l]["kv_g"] for l in range(DEPTH)]),
               jnp.stack([grads[l]["w_pool"] for l in range(DEPTH)]),
               jnp.stack([grads[l]["pool_scale"] for l in range(DEPTH)]),
               jnp.stack([grads[l]["b_out"] for l in range(DEPTH)]), jnp.stack([grads[l]["ln_g"] for l in range(DEPTH)]),
               jnp.stack([grads[l]["ln_b"] for l in range(DEPTH)]),
               jnp.stack([grads[l]["conv_w"] for l in range(DEPTH)]),
               jnp.pad(loss_part.reshape(1), (0, LANE - 1))]
    small_w = [emb_ln_g, emb_ln_b, q_norm_g, kv_norm_g, w_pool, pool_scale, b_out, ln_g, ln_b]
    small_m = [m_emb_ln_g, m_emb_ln_b, m_q_norm_g, m_kv_norm_g, m_w_pool, m_pool_scale, m_b_out, m_ln_g, m_ln_b]
    small_v = [v_emb_ln_g, v_emb_ln_b, v_q_norm_g, v_kv_norm_g, v_w_pool, v_pool_scale, v_b_out, v_ln_g, v_ln_b]
    extra = (DEPTH * 3 * 512 + LANE) // LANE
    packed_g = _pack_small(small_g, 0)
    gathered = _allgather_small(packed_g, name="allgather_small")
    g_tot, d_small, m_small, v_small = _small_sum_adamw(
        gathered, _pack_small(small_w, extra), _pack_small(small_m, extra), _pack_small(small_v, extra),
        name="small_sum_adamw")
    shapes = [w.shape for w in small_w]
    g_list, off = _unpack_small(g_tot, shapes)
    d_list, _ = _unpack_small(d_small, shapes)
    m_list, _ = _unpack_small(m_small, shapes)
    v_list, _ = _unpack_small(v_small, shapes)
    flat_tot = g_tot.reshape(-1)
    conv_tot = flat_tot[off:off + DEPTH * 3 * 512].reshape(DEPTH, 3, 512)
    loss = flat_tot[off + DEPTH * 3 * 512]
    g_conv = lax.dynamic_slice_in_dim(conv_tot, chip * 128, 128, axis=2)

    def halves(a):
        return [reduced[l][a] for l in range(DEPTH)]

    def whole(a):
        return jnp.stack([jnp.where(ci == 0, jnp.concatenate([mine, oth], axis=1),
                                    jnp.concatenate([oth, mine], axis=1)) for mine, oth in halves(a)])

    upd = {}
    upd["w_in"] = tuple(t(o) for o in _adamw_halves(t(w_in), t(m_w_in), t(v_w_in), halves(0), c_idx,
                                                    name="adamw_w_in"))
    upd["w_out"] = _adamw_halves(w_out, m_w_out, v_w_out, halves(1), c_idx, name="adamw_w_out")
    g_uq, g_ukv = t(whole(2)), t(whole(3))
    upd["w_uq"] = (g_uq,) + _adamw(w_uq, g_uq, m_w_uq, v_w_uq, name="adamw_w_uq")
    upd["w_ukv"] = (g_ukv,) + _adamw(w_ukv, g_ukv, m_w_ukv, v_w_ukv, name="adamw_w_ukv")
    upd["conv_w"] = (g_conv,) + _adamw(conv_w, g_conv, m_conv_w, v_conv_w, name="adamw_conv_w")
    for i, nm in enumerate(SMALL_ORDER):
        upd[nm] = (g_list[i], d_list[i], m_list[i], v_list[i])

    order = ("emb_ln_g", "emb_ln_b", "w_in", "q_norm_g", "kv_norm_g", "w_uq", "w_ukv", "w_pool", "pool_scale",
             "conv_w", "w_out", "b_out", "ln_g", "ln_b")
    outs = [loss, grad_x[None]]
    for field in range(4):
        outs += [upd[nm][field] for nm in order]
    return tuple(outs)
```

```python
import collections

import jax
import jax.numpy as jnp
from jax import lax
from jax.experimental import pallas as pl
from jax.experimental.pallas import tpu as pltpu

F32 = jnp.float32
BF16 = jnp.bfloat16
MESH = pl.DeviceIdType.MESH

D_MODEL = 2048
DEPTH = 2
N_HEADS = 8
NOPE = 128
ROPE = 64
Q_LORA = 512
KV_LORA = 256
D_MLA = 1024
POOL_WINDOWS = (2, 4, 8, 16)
D_IN_PROJ = 4928
LN_EPS = 1e-5
RMS_EPS = 1e-6
ROPE_THETA = 10000.0
ALPHA = (2 * DEPTH) ** 0.25
SCALE = (NOPE + ROPE) ** -0.5
LOG2E = 1.4426950408889634
SCALE_LOG2E = SCALE * LOG2E
ADAM_LR = 0.001
ADAM_B1 = 0.9
ADAM_B2 = 0.999
ADAM_EPS = 1e-08
ADAM_WD = 0.01
ADAM_STEP = 10

NP = 5120
GAP_AT = 832
GAP = NP - D_IN_PROJ
W_MLA = 1024
W_MIX = NP - W_MLA
HALO = 16
LANE = 128
N_CHIPS = 4
N_DEV = 8
TQ = 512
FWD_GROUP = 4

NN = (((1,), (0,)), ((), ()))
NT = (((1,), (1,)), ((), ()))
TN = (((0,), (0,)), ((), ()))


CommScript = collections.namedtuple("CommScript", "args out_shape n_sems start finish")
HBM_SPEC = pl.BlockSpec(memory_space=pl.ANY)


def _pcall(kern, *, name, out_shape, grid=None, in_specs=None, out_specs=None, scratch=(), dims=None,
           vmem_mb=None, comm=None):
    cp = {}
    if dims is not None:
        cp["dimension_semantics"] = dims if comm is None else ("arbitrary",) * len(dims)
    if vmem_mb is not None:
        cp["vmem_limit_bytes"] = vmem_mb << 20
    if comm is None:
        args = dict(name=name, out_shape=out_shape, scratch_shapes=list(scratch),
                    compiler_params=pltpu.CompilerParams(**cp))
        if grid is not None:
            args["grid"] = grid
        if in_specs is not None:
            args["in_specs"] = in_specs
        if out_specs is not None:
            args["out_specs"] = out_specs
        return pl.pallas_call(kern, **args)

    single = not isinstance(out_shape, (tuple, list))
    own_out = (out_shape,) if single else tuple(out_shape)
    own_out_specs = (out_specs,) if single else tuple(out_specs)
    n_in, n_out, n_scr = len(in_specs), len(own_out), len(scratch)
    na, no = len(comm.args), len(comm.out_shape)

    def at(end):
        cond = None
        for d, n in enumerate(grid):
            here = pl.program_id(d) == (n - 1 if end else 0)
            cond = here if cond is None else jnp.logical_and(cond, here)
        return cond

    def wrapped(*refs):
        own_in, c_in = refs[:n_in], refs[n_in:n_in + na]
        o0 = n_in + na
        own_o, c_out = refs[o0:o0 + n_out], refs[o0 + n_out:o0 + n_out + no]
        s0 = o0 + n_out + no
        own_s, (send_sems, recv_sems) = refs[s0:s0 + n_scr], refs[s0 + n_scr:]

        @pl.when(at(False))
        def _():
            comm.start(c_in, c_out, send_sems, recv_sems)

        kern(*own_in, *own_o, *own_s)

        @pl.when(at(True))
        def _():
            comm.finish(c_in, c_out, send_sems, recv_sems)

    call = pl.pallas_call(
        wrapped, name=name, out_shape=own_out + tuple(comm.out_shape), grid=grid,
        in_specs=list(in_specs) + [HBM_SPEC] * na, out_specs=own_out_specs + (HBM_SPEC,) * no,
        scratch_shapes=list(scratch) + [pltpu.SemaphoreType.DMA((comm.n_sems,)),
                                        pltpu.SemaphoreType.DMA((comm.n_sems,))],
        compiler_params=pltpu.CompilerParams(**cp))

    def run(*args):
        res = call(*args, *comm.args)
        own = res[0] if single else tuple(res[:n_out])
        return own, tuple(res[n_out:])

    return run


def _run_comm(script, *, name):
    na, no = len(script.args), len(script.out_shape)

    def body(*refs):
        ins, outs = refs[:na], refs[na:na + no]
        send_sems, recv_sems = refs[na + no:]
        script.start(ins, outs, send_sems, recv_sems)
        script.finish(ins, outs, send_sems, recv_sems)

    return pl.pallas_call(
        body, name=name, out_shape=tuple(script.out_shape), in_specs=[HBM_SPEC] * na, out_specs=(HBM_SPEC,) * no,
        scratch_shapes=[pltpu.SemaphoreType.DMA((script.n_sems,)), pltpu.SemaphoreType.DMA((script.n_sems,))])(
            *script.args)


def _sigmoid(g):
    return 1.0 / (1.0 + jnp.exp(-g))


def _silu_and_grad(g):
    sig = _sigmoid(g)
    return g * sig, sig * (1.0 + g * (1.0 - sig))


def _matmul(a, b, mode, *, name, tm, tn, tk, out_dtype=F32, vmem_mb=48, comm=None):
    if mode == "nn":
        (M, K), N = a.shape, b.shape[1]
    elif mode == "nt":
        (M, K), N = a.shape, b.shape[0]
    else:
        (K, M), N = a.shape, b.shape[1]
    tm, tn, tk = min(tm, M), min(tn, N), min(tk, K)
    assert M % tm == 0 and N % tn == 0 and K % tk == 0, (name, M, N, K)
    nk = K // tk
    dn = {"nn": NN, "nt": NT, "tn": TN}[mode]
    if mode == "tn":
        a_spec = pl.BlockSpec((tk, tm), lambda i, j, k: (k, i))
    else:
        a_spec = pl.BlockSpec((tm, tk), lambda i, j, k: (i, k))
    if mode == "nt":
        b_spec = pl.BlockSpec((tn, tk), lambda i, j, k: (j, k))
    else:
        b_spec = pl.BlockSpec((tk, tn), lambda i, j, k: (k, j))
    o_spec = pl.BlockSpec((tm, tn), lambda i, j, k: (i, j))

    def kern(a_ref, b_ref, o_ref, *rest):
        part = lax.dot_general(a_ref[...].astype(BF16), b_ref[...].astype(BF16), dn,
                               preferred_element_type=F32)
        if nk == 1:
            o_ref[...] = part.astype(out_dtype)
        else:
            acc_ref = rest[0]
            k = pl.program_id(2)

            @pl.when(k == 0)
            def _():
                acc_ref[...] = part

            @pl.when(k > 0)
            def _():
                acc_ref[...] += part

            @pl.when(k == nk - 1)
            def _():
                o_ref[...] = acc_ref[...].astype(out_dtype)

    scratch = [pltpu.VMEM((tm, tn), F32)] if nk > 1 else []
    return _pcall(kern, name=name, out_shape=jax.ShapeDtypeStruct((M, N), out_dtype),
                  grid=(M // tm, N // tn, nk), in_specs=[a_spec, b_spec], out_specs=o_spec, scratch=scratch,
                  dims=("parallel", "parallel", "arbitrary"), vmem_mb=vmem_mb, comm=comm)(a, b)


def _dproj_times_w(d_mla, d_mix, wt, add, add_scale, *, name, comm=None):
    S = d_mla.shape[0]
    Dm = wt.shape[1]
    tm, tn, tk = min(1024, S), 1024, 2048
    nk = 1 + W_MIX // tk

    def kern(a1_ref, a2_ref, b1_ref, b2_ref, add_ref, o_ref, acc_ref):
        k = pl.program_id(2)

        @pl.when(k == 0)
        def _():
            acc_ref[...] = jnp.dot(a1_ref[...], b1_ref[...], preferred_element_type=F32)

        @pl.when(k > 0)
        def _():
            acc_ref[...] += jnp.dot(a2_ref[...], b2_ref[...], preferred_element_type=F32)

        @pl.when(k == nk - 1)
        def _():
            o_ref[...] = add_scale * add_ref[...] + acc_ref[...]

    o_spec = pl.BlockSpec((tm, tn), lambda i, j, k: (i, j))
    b2_spec = pl.BlockSpec((pl.Element(tk), pl.Element(tn)),
                           lambda i, j, k: (pl.multiple_of(W_MLA + tk * jnp.maximum(k - 1, 0), W_MLA),
                                            pl.multiple_of(j * tn, tn)))
    return _pcall(kern, name=name, out_shape=jax.ShapeDtypeStruct((S, Dm), F32), grid=(S // tm, Dm // tn, nk),
                  in_specs=[pl.BlockSpec((tm, W_MLA), lambda i, j, k: (i, 0)),
                            pl.BlockSpec((tm, tk), lambda i, j, k: (i, jnp.maximum(k - 1, 0))),
                            pl.BlockSpec((W_MLA, tn), lambda i, j, k: (0, j)), b2_spec, o_spec],
                  out_specs=o_spec, scratch=[pltpu.VMEM((tm, tn), F32)],
                  dims=("parallel", "parallel", "arbitrary"), vmem_mb=56, comm=comm)(d_mla, d_mix, wt, wt, add)


def _dproj_t_times_h(d_mla, d_mix, h, *, name, comm=None):
    S, Dm = h.shape
    tm, tn, tk = W_MLA, 1024, min(2048, S)
    nk = S // tk

    def kern(a1_ref, a2_ref, b_ref, o_ref, acc_ref):
        i = pl.program_id(0)
        k = pl.program_id(2)
        b = b_ref[...].astype(BF16)

        def accumulate(part):
            @pl.when(k == 0)
            def _():
                acc_ref[...] = part

            @pl.when(k > 0)
            def _():
                acc_ref[...] += part

        @pl.when(i == 0)
        def _():
            accumulate(lax.dot_general(a1_ref[...], b, TN, preferred_element_type=F32))

        @pl.when(i > 0)
        def _():
            accumulate(lax.dot_general(a2_ref[...], b, TN, preferred_element_type=F32))

        @pl.when(k == nk - 1)
        def _():
            o_ref[...] = acc_ref[...]

    return _pcall(kern, name=name, out_shape=jax.ShapeDtypeStruct((NP, Dm), F32), grid=(NP // tm, Dm // tn, nk),
                  in_specs=[pl.BlockSpec((tk, tm), lambda i, j, k: (jnp.where(i == 0, k, nk - 1), 0)),
                            pl.BlockSpec((tk, tm), lambda i, j, k: (jnp.where(i == 0, 0, k), jnp.maximum(i - 1, 0))),
                            pl.BlockSpec((tk, tn), lambda i, j, k: (k, j))],
                  out_specs=pl.BlockSpec((tm, tn), lambda i, j, k: (i, j)), scratch=[pltpu.VMEM((tm, tn), F32)],
                  dims=("parallel", "parallel", "arbitrary"), vmem_mb=48, comm=comm)(d_mla, d_mix, h)


def _ln_fwd(x, g, b, *, name, comm=None):
    S, Dm = x.shape
    tm = min(512, S)

    def kern(x_ref, g_ref, b_ref, y_ref, yb_ref):
        xf = x_ref[...]
        mu = jnp.mean(xf, axis=-1, keepdims=True)
        xc = xf - mu
        var = jnp.mean(xc * xc, axis=-1, keepdims=True)
        y = xc * lax.rsqrt(var + LN_EPS) * g_ref[...] + b_ref[...]
        y_ref[...] = y
        yb_ref[...] = y.astype(BF16)

    row = pl.BlockSpec((tm, Dm), lambda i: (i, 0))
    vec = pl.BlockSpec((1, Dm), lambda i: (0, 0))
    return _pcall(kern, name=name,
                  out_shape=(jax.ShapeDtypeStruct((S, Dm), F32), jax.ShapeDtypeStruct((S, Dm), BF16)),
                  grid=(S // tm,), in_specs=[row, vec, vec], out_specs=(row, row), dims=("parallel",), vmem_mb=48,
                  comm=comm)(
                      x, g.reshape(1, Dm), b.reshape(1, Dm))


def _ln_bwd(dy, r, g, *, name):
    S, Dm = r.shape
    tm = min(512, S)

    def kern(dy_ref, r_ref, g_ref, dr_ref, drb_ref, dg_ref, db_ref, ds_ref):
        @pl.when(pl.program_id(0) == 0)
        def _():
            dg_ref[...] = jnp.zeros_like(dg_ref)
            db_ref[...] = jnp.zeros_like(db_ref)
            ds_ref[...] = jnp.zeros_like(ds_ref)

        rf = r_ref[...]
        dyf = dy_ref[...]
        mu = jnp.mean(rf, axis=-1, keepdims=True)
        xc = rf - mu
        var = jnp.mean(xc * xc, axis=-1, keepdims=True)
        rstd = lax.rsqrt(var + LN_EPS)
        xhat = xc * rstd
        dxh = dyf * g_ref[...]
        c1 = jnp.mean(dxh, axis=-1, keepdims=True)
        c2 = jnp.mean(dxh * xhat, axis=-1, keepdims=True)
        dr = rstd * (dxh - c1 - xhat * c2)
        dr_ref[...] = dr
        drb_ref[...] = dr.astype(BF16)
        dg_ref[...] += jnp.sum(dyf * xhat, axis=0, keepdims=True)
        db_ref[...] += jnp.sum(dyf, axis=0, keepdims=True)
        ds_ref[...] += jnp.sum(dr, axis=0, keepdims=True)

    row = pl.BlockSpec((tm, Dm), lambda i: (i, 0))
    vec = pl.BlockSpec((1, Dm), lambda i: (0, 0))
    vshape = jax.ShapeDtypeStruct((1, Dm), F32)
    return _pcall(kern, name=name,
                  out_shape=(jax.ShapeDtypeStruct((S, Dm), F32), jax.ShapeDtypeStruct((S, Dm), BF16),
                             vshape, vshape, vshape),
                  grid=(S // tm,), in_specs=[row, row, vec], out_specs=(row, row, vec, vec, vec),
                  dims=("arbitrary",), vmem_mb=48)(dy, r, g.reshape(1, Dm))


def _loss_ln_bwd(target, r, g, b, *, name):
    S, Dm = r.shape
    tm = min(512, S)

    def kern(t_ref, r_ref, g_ref, b_ref, l_ref, dr_ref, drb_ref, dg_ref, db_ref, ds_ref):
        @pl.when(pl.program_id(0) == 0)
        def _():
            l_ref[...] = jnp.zeros_like(l_ref)
            dg_ref[...] = jnp.zeros_like(dg_ref)
            db_ref[...] = jnp.zeros_like(db_ref)
            ds_ref[...] = jnp.zeros_like(ds_ref)

        rf = r_ref[...]
        mu = jnp.mean(rf, axis=-1, keepdims=True)
        xc = rf - mu
        var = jnp.mean(xc * xc, axis=-1, keepdims=True)
        rstd = lax.rsqrt(var + LN_EPS)
        xhat = xc * rstd
        e = (xhat * g_ref[...] + b_ref[...]) - t_ref[...]
        dyf = e / float(Dm)
        per_row = jnp.mean(e * e, axis=-1, keepdims=True)
        l_ref[...] += 0.5 * jnp.sum(per_row, axis=0, keepdims=True)
        dxh = dyf * g_ref[...]
        c1 = jnp.mean(dxh, axis=-1, keepdims=True)
        c2 = jnp.mean(dxh * xhat, axis=-1, keepdims=True)
        dr = rstd * (dxh - c1 - xhat * c2)
        dr_ref[...] = dr
        drb_ref[...] = dr.astype(BF16)
        dg_ref[...] += jnp.sum(dyf * xhat, axis=0, keepdims=True)
        db_ref[...] += jnp.sum(dyf, axis=0, keepdims=True)
        ds_ref[...] += jnp.sum(dr, axis=0, keepdims=True)

    row = pl.BlockSpec((tm, Dm), lambda i: (i, 0))
    vec = pl.BlockSpec((1, Dm), lambda i: (0, 0))
    acc = pl.BlockSpec((8, LANE), lambda i: (0, 0))
    vshape = jax.ShapeDtypeStruct((1, Dm), F32)
    return _pcall(kern, name=name,
                  out_shape=(jax.ShapeDtypeStruct((8, LANE), F32), jax.ShapeDtypeStruct((S, Dm), F32),
                             jax.ShapeDtypeStruct((S, Dm), BF16), vshape, vshape, vshape),
                  grid=(S // tm,), in_specs=[row, row, vec, vec], out_specs=(acc, row, row, vec, vec, vec),
                  dims=("arbitrary",), vmem_mb=56)(target, r, g.reshape(1, Dm), b.reshape(1, Dm))


def _rot_sum(t):
    return pltpu.roll(t, 32, 1) + pltpu.roll(t, 96, 1)


def _mla_qkv(proj, cos_t, sin_t, qg, kvg, wuq_t, wukv_t, *, name):
    S = proj.shape[0]
    tm = min(256, S)

    def kern(ql_ref, kvl_ref, kr_ref, cos_ref, sin_ref, qg_ref, kvg_ref, wuq_ref, wukv_ref,
             qc_ref, kc_ref, v_ref, vt_ref, qn_ref, kvn_ref):
        cosv = cos_ref[...]
        sinv = sin_ref[...]

        def rope(t):
            return t * cosv + _rot_sum(t) * sinv

        ql = ql_ref[...]
        qn = (ql * lax.rsqrt(jnp.mean(ql * ql, axis=-1, keepdims=True) + RMS_EPS) * qg_ref[...]).astype(BF16)
        kvl = kvl_ref[...]
        kvn = (kvl * lax.rsqrt(jnp.mean(kvl * kvl, axis=-1, keepdims=True) + RMS_EPS) * kvg_ref[...]).astype(BF16)
        qn_ref[...] = qn
        kvn_ref[...] = kvn
        q = lax.dot_general(qn, wuq_ref[...], NT, preferred_element_type=F32)
        kv = lax.dot_general(kvn, wukv_ref[...], NT, preferred_element_type=F32)
        kr = rope(kr_ref[...]).astype(BF16)
        for h in range(N_HEADS):
            c0 = 256 * h
            qc_ref[:, c0:c0 + 128] = q[:, c0:c0 + 128].astype(BF16)
            qc_ref[:, c0 + 128:c0 + 256] = rope(q[:, c0 + 128:c0 + 256]).astype(BF16)
            kc_ref[:, c0:c0 + 128] = kv[:, c0:c0 + 128].astype(BF16)
            kc_ref[:, c0 + 128:c0 + 256] = kr
            vh = kv[:, c0 + 128:c0 + 256]
            v_ref[:, 128 * h:128 * h + 128] = vh.astype(BF16)
            vt_ref[h] = jnp.transpose(vh).astype(BF16)

    def row(w, blk):
        return pl.BlockSpec((tm, w), lambda i: (i, blk))

    def full(shape):
        return pl.BlockSpec(shape, lambda i: (0,) * len(shape))

    t = min(TQ, S)
    per = t // tm
    vt_spec = pl.BlockSpec((N_HEADS, None, 128, tm), lambda i: (0, i // per, 0, i % per))
    outs = (jax.ShapeDtypeStruct((S, 2048), BF16), jax.ShapeDtypeStruct((S, 2048), BF16),
            jax.ShapeDtypeStruct((S, 1024), BF16), jax.ShapeDtypeStruct((N_HEADS, S // t, 128, t), BF16),
            jax.ShapeDtypeStruct((S, Q_LORA), BF16), jax.ShapeDtypeStruct((S, KV_LORA), BF16))
    return _pcall(kern, name=name, out_shape=outs, grid=(S // tm,),
                  in_specs=[row(512, 0), row(256, 2), row(128, 6), row(128, 0), row(128, 0),
                            full((1, Q_LORA)), full((1, KV_LORA)), full((2048, Q_LORA)), full((2048, KV_LORA))],
                  out_specs=(row(2048, 0), row(2048, 0), row(1024, 0), vt_spec, row(512, 0), row(256, 0)),
                  dims=("parallel",), vmem_mb=48)(
                      proj, proj, proj, cos_t, sin_t, qg.reshape(1, -1), kvg.reshape(1, -1), wuq_t, wukv_t)


def _mla_qkv_bwd(dqb, dkvb, dkr_heads, proj, cos_t, sin_t, qg, kvg, wuq_t, wukv_t, *, name):
    S = proj.shape[0]
    tm = min(256, S)

    def kern(dqb_ref, dkvb_ref, dkrh_ref, ql_ref, kvl_ref, cos_ref, sin_ref, qg_ref, kvg_ref, wuq_ref, wukv_ref,
             dml_ref, dqg_ref, dkvg_ref):
        @pl.when(pl.program_id(0) == 0)
        def _():
            dqg_ref[...] = jnp.zeros_like(dqg_ref)
            dkvg_ref[...] = jnp.zeros_like(dkvg_ref)

        cosv = cos_ref[...]
        sinv = sin_ref[...]

        def unrope(t):
            return t * cosv - _rot_sum(t) * sinv

        dkr = dkrh_ref[:, 0:128]
        for h in range(1, N_HEADS):
            dkr = dkr + dkrh_ref[:, 128 * h:128 * h + 128]

        def rms_bwd(x, g, dy):
            n = x.shape[-1]
            rs = lax.rsqrt(jnp.mean(x * x, axis=-1, keepdims=True) + RMS_EPS)
            dyg = dy * g
            dx = rs * dyg - x * (rs * rs * rs) * (jnp.sum(dyg * x, axis=-1, keepdims=True) / n)
            return dx, jnp.sum(dy * (x * rs), axis=0, keepdims=True)

        dqn = jnp.dot(dqb_ref[...], wuq_ref[...], preferred_element_type=F32)
        dql, dqg = rms_bwd(ql_ref[...], qg_ref[...], dqn)
        dqg_ref[...] += dqg
        dkvn = jnp.dot(dkvb_ref[...], wukv_ref[...], preferred_element_type=F32)
        dkvl, dkvg = rms_bwd(kvl_ref[...], kvg_ref[...], dkvn)
        dkvg_ref[...] += dkvg
        dml_ref[:, 0:512] = dql.astype(BF16)
        dml_ref[:, 512:768] = dkvl.astype(BF16)
        dml_ref[:, 768:896] = unrope(dkr).astype(BF16)
        dml_ref[:, 896:1024] = jnp.zeros((tm, 128), BF16)

    def row(w, blk):
        return pl.BlockSpec((tm, w), lambda i: (i, blk))

    def full(shape):
        return pl.BlockSpec(shape, lambda i: (0,) * len(shape))

    outs = (jax.ShapeDtypeStruct((S, W_MLA), BF16), jax.ShapeDtypeStruct((1, Q_LORA), F32),
            jax.ShapeDtypeStruct((1, KV_LORA), F32))
    return _pcall(kern, name=name, out_shape=outs, grid=(S // tm,),
                  in_specs=[row(2048, 0), row(2048, 0), row(1024, 0), row(512, 0), row(256, 2),
                            row(128, 0), row(128, 0), full((1, Q_LORA)), full((1, KV_LORA)),
                            full((2048, Q_LORA)), full((2048, KV_LORA))],
                  out_specs=(row(W_MLA, 0), full((1, Q_LORA)), full((1, KV_LORA))),
                  dims=("arbitrary",), vmem_mb=56)(
                      dqb, dkvb, dkr_heads, proj, proj, cos_t, sin_t, qg.reshape(1, -1), kvg.reshape(1, -1),
                      wuq_t, wukv_t)


def _flash_fwd(qc, kc, vt, *, name, comm=None):
    S = qc.shape[0]
    t = min(TQ, S)
    n = S // t

    def kern(q_ref, k_ref, vt_ref, o_ref, lse_ref, m_s, l_s, acc_s):
        qi = pl.program_id(1)
        m_s[...] = jnp.full_like(m_s, -jnp.inf)
        l_s[...] = jnp.zeros_like(l_s)
        acc_s[...] = jnp.zeros_like(acc_s)

        half = t // 2

        def scores(kb, q_lo=0, q_n=t, k_n=t):
            k0 = pl.multiple_of(kb * t, t)
            return lax.dot_general(k_ref[pl.ds(k0, k_n), :], q_ref[q_lo:q_lo + q_n, :], NT,
                                   preferred_element_type=F32)

        def update(kb, st, q_lo=0, diagonal=False):
            k_n, q_n = st.shape
            if diagonal:
                krow = lax.broadcasted_iota(jnp.int32, (k_n, q_n), 0)
                qcol = lax.broadcasted_iota(jnp.int32, (k_n, q_n), 1) + q_lo
                st = jnp.where(krow <= qcol, st, -jnp.inf)
            lanes = slice(q_lo, q_lo + q_n)
            m_prev = m_s[:, lanes]
            m_new = jnp.maximum(m_prev, jnp.max(st, axis=0, keepdims=True))
            a = jnp.exp2((m_prev - m_new) * SCALE_LOG2E)
            pt = jnp.exp2((st - m_new) * SCALE_LOG2E)
            l_s[:, lanes] = a * l_s[:, lanes] + jnp.sum(pt, axis=0, keepdims=True)
            acc_s[:, lanes] = a * acc_s[:, lanes] + jnp.dot(vt_ref[kb, :, 0:k_n], pt.astype(BF16),
                                                            preferred_element_type=F32)
            m_s[:, lanes] = m_new

        def group(kb, count, last_diagonal):
            whole = count - 1 if last_diagonal else count
            sts = [scores(kb + g) for g in range(whole)]
            if last_diagonal:
                kd = kb + count - 1
                s_lo, s_hi = scores(kd, 0, half, half), scores(kd, half, half, t)
            for g in range(whole):
                update(kb + g, sts[g])
            if last_diagonal:
                update(kd, s_lo, 0, True)
                update(kd, s_hi, half, True)

        def body(i, carry):
            group(FWD_GROUP * i, FWD_GROUP, False)
            return carry

        full = qi // FWD_GROUP
        lax.fori_loop(0, full, body, 0)
        for rem in range(FWD_GROUP):
            @pl.when(qi - FWD_GROUP * full == rem)
            def _():
                group(qi - rem, rem + 1, True)
        o_ref[...] = jnp.transpose(acc_s[...] / l_s[...])
        lse_ref[pl.ds(qi, 1), :] = m_s[...] * SCALE_LOG2E + jnp.log2(l_s[...])

    q_spec = pl.BlockSpec((t, 256), lambda h, qi: (qi, h))
    k_spec = pl.BlockSpec((S, 256), lambda h, qi: (0, h))
    vt_spec = pl.BlockSpec((None, n, 128, t), lambda h, qi: (h, 0, 0, 0))
    o_spec = pl.BlockSpec((t, 128), lambda h, qi: (qi, h))
    lse_spec = pl.BlockSpec((None, n, t), lambda h, qi: (h, 0, 0))
    return _pcall(kern, name=name,
                  out_shape=(jax.ShapeDtypeStruct((S, D_MLA), F32), jax.ShapeDtypeStruct((N_HEADS, n, t), F32)),
                  grid=(N_HEADS, n), in_specs=[q_spec, k_spec, vt_spec], out_specs=(o_spec, lse_spec),
                  scratch=[pltpu.VMEM((1, t), F32), pltpu.VMEM((1, t), F32), pltpu.VMEM((128, t), F32)],
                  dims=("parallel", "arbitrary"), vmem_mb=48, comm=comm)(qc, kc, vt)


def _attn_delta(o, do, *, name):
    S = o.shape[0]
    t = min(TQ, S)
    n = S // t

    def kern(o_ref, do_ref, dl_ref):
        i = pl.program_id(0)
        prod = o_ref[...] * do_ref[...]
        lane = lax.broadcasted_iota(jnp.int32, (t, LANE), 1)
        dmat = jnp.zeros((t, LANE), F32)
        for h in range(N_HEADS):
            dmat = jnp.where(lane == h, jnp.sum(prod[:, 128 * h:128 * h + 128], axis=1, keepdims=True), dmat)
        dmat_t = jnp.transpose(dmat)
        for h in range(N_HEADS):
            dl_ref[h, pl.ds(i, 1), :] = dmat_t[h:h + 1, :]

    row = pl.BlockSpec((t, D_MLA), lambda i: (i, 0))
    return _pcall(kern, name=name, out_shape=jax.ShapeDtypeStruct((N_HEADS, n, t), F32), grid=(n,),
                  in_specs=[row, row], out_specs=pl.BlockSpec((N_HEADS, n, t), lambda i: (0, 0, 0)),
                  dims=("arbitrary",), vmem_mb=48)(o, do)


def _flash_bwd(qc, kc, v, do, lse2, delta, cos_t, sin_t, *, name, comm=None):
    S = qc.shape[0]
    t = min(TQ, S)
    n = S // t

    def kern(q_ref, k_ref, v_ref, do_ref, lse_ref, dl_ref, cos_ref, sin_ref, dqb_ref, dkvb_ref, dkr_ref,
             dq_ref, dk_ref, dv_ref):
        ki = pl.program_id(1)

        @pl.when(ki == 0)
        def _():
            dq_ref[...] = jnp.zeros_like(dq_ref)

        dk_ref[...] = jnp.zeros_like(dk_ref)
        dv_ref[...] = jnp.zeros_like(dv_ref)

        half = t // 2

        def step(qb, q_lo=0, q_n=t, k_n=t, diagonal=False):
            q0 = pl.multiple_of(qb * t + q_lo, half)
            lanes = slice(q_lo, q_lo + q_n)
            kt = k_ref[0:k_n, :]
            qblk = q_ref[pl.ds(q0, q_n), :]
            dob = do_ref[pl.ds(q0, q_n), :].astype(BF16)
            st = lax.dot_general(kt, qblk, NT, preferred_element_type=F32)
            pt = jnp.exp2(st * SCALE_LOG2E - lse_ref[pl.ds(qb, 1), lanes])
            if diagonal:
                krow = lax.broadcasted_iota(jnp.int32, (k_n, q_n), 0)
                qcol = lax.broadcasted_iota(jnp.int32, (k_n, q_n), 1) + q_lo
                pt = jnp.where(krow <= qcol, pt, 0.0)
            dv_ref[0:k_n, :] += jnp.dot(pt.astype(BF16), dob, preferred_element_type=F32)
            dpt = lax.dot_general(v_ref[0:k_n, :], dob, NT, preferred_element_type=F32)
            dst = (pt * (dpt - dl_ref[pl.ds(qb, 1), lanes]) * SCALE).astype(BF16)
            dk_ref[0:k_n, :] += jnp.dot(dst, qblk, preferred_element_type=F32)
            dq_ref[pl.ds(q0, q_n), :] += lax.dot_general(dst, kt, TN, preferred_element_type=F32)

        step(ki, 0, half, half, True)
        step(ki, half, half, t, True)
        rest = n - 1 - ki

        def body(i, carry):
            step(ki + 1 + 2 * i)
            step(ki + 2 + 2 * i)
            return carry

        lax.fori_loop(0, rest // 2, body, 0)

        @pl.when(rest % 2 == 1)
        def _():
            step(n - 1)

        dkvb_ref[:, 0:128] = dk_ref[:, 0:128].astype(BF16)
        dkvb_ref[:, 128:256] = dv_ref[...].astype(BF16)
        dkr_ref[...] = dk_ref[:, 128:256]

        @pl.when(ki == n - 1)
        def _():
            dqb_ref[:, 0:128] = dq_ref[:, 0:128].astype(BF16)
            dqr = dq_ref[:, 128:256]
            dqb_ref[:, 128:256] = (dqr * cos_ref[...] - _rot_sum(dqr) * sin_ref[...]).astype(BF16)

    def whole(w):
        return pl.BlockSpec((S, w), lambda h, ki: (0, h))

    def krow(w):
        return pl.BlockSpec((t, w), lambda h, ki: (ki, h))

    stat = pl.BlockSpec((None, n, t), lambda h, ki: (h, 0, 0))
    table = pl.BlockSpec((S, 128), lambda h, ki: (0, 0))
    return _pcall(kern, name=name,
                  out_shape=(jax.ShapeDtypeStruct((S, 2048), BF16), jax.ShapeDtypeStruct((S, 2048), BF16),
                             jax.ShapeDtypeStruct((S, D_MLA), F32)),
                  grid=(N_HEADS, n),
                  in_specs=[whole(256), krow(256), krow(128), whole(128), stat, stat, table, table],
                  out_specs=(whole(256), krow(256), krow(128)),
                  scratch=[pltpu.VMEM((S, 256), F32), pltpu.VMEM((t, 256), F32), pltpu.VMEM((t, 128), F32)],
                  dims=("parallel", "arbitrary"), vmem_mb=56, comm=comm)(qc, kc, v, do, lse2, delta, cos_t, sin_t)


def _mixer_specs(S, tm):
    hb = tm // HALO
    last_hb = S // HALO - 1

    def main(w, blk):
        return pl.BlockSpec((tm, w), lambda i: (i, blk))

    def prev(w, blk):
        return pl.BlockSpec((HALO, w), lambda i: (jnp.maximum(i * hb - 1, 0), blk))

    def nxt(w, blk):
        return pl.BlockSpec((HALO, w), lambda i: (jnp.minimum((i + 1) * hb, last_hb), blk))

    def full(shape):
        return pl.BlockSpec(shape, lambda i: (0,) * len(shape))

    return main, prev, nxt, full


def _fill_halo(i, xp, xu, hp_ref, hch_ref, hcc_ref, pin_ref, ch_ref, cc_ref, tm):
    first = i == 0
    xp[0:HALO, :] = jnp.where(first, 0.0, hp_ref[...])
    xp[HALO:HALO + tm, :] = pin_ref[...]
    xu[0:HALO, :] = jnp.where(first, 0.0, hch_ref[...] * hcc_ref[...])
    xu[HALO:HALO + tm, :] = cc_ref[...] * ch_ref[...]


def _pooled(xp, g, t1, tm):
    w = POOL_WINDOWS[g]
    lanes = slice(128 * g, 128 * g + 128)
    x0 = xp[HALO:HALO + tm, lanes]
    acc = x0
    for k in range(1, w):
        acc = acc + xp[HALO - k:HALO - k + tm, lanes]
    return acc / jnp.minimum(t1, float(w)) - x0


def _conv_fwd(xu, cw_ref, tm):
    return (cw_ref[0:1, :] * xu[HALO - 2:HALO - 2 + tm, :] + cw_ref[1:2, :] * xu[HALO - 1:HALO - 1 + tm, :]
            + cw_ref[2:3, :] * xu[HALO:HALO + tm, :])


def _mixer_fwd(proj, o, wpool, ps, convw, *, name):
    S = proj.shape[0]
    tm = min(256, S)
    main, prev, _, full = _mixer_specs(S, tm)

    def kern(gm_ref, pin_ref, gp_ref, ch_ref, cb_ref, cc_ref, gc_ref, hp_ref, hch_ref, hcc_ref,
             o_ref, wp_ref, ps_ref, cw_ref, mix_ref, xp, xu):
        i = pl.program_id(0)
        _fill_halo(i, xp, xu, hp_ref, hch_ref, hcc_ref, pin_ref, ch_ref, cc_ref, tm)
        t1 = (i * tm + lax.broadcasted_iota(jnp.int32, (tm, 1), 0) + 1).astype(F32)
        for g in range(4):
            lanes = slice(128 * g, 128 * g + 128)
            pooled = _pooled(xp, g, t1, tm)
            z = jnp.dot(pooled.astype(BF16), wp_ref[g].astype(BF16), preferred_element_type=F32)
            gp = gp_ref[:, lanes]
            y = z * ps_ref[:, lanes] * (gp * _sigmoid(gp))
            mix_ref[:, 1024 + 128 * g:1024 + 128 * g + 128] = y.astype(BF16)
        gc = gc_ref[...]
        mix_ref[:, 1536:2048] = (cb_ref[...] * _conv_fwd(xu, cw_ref, tm) * (gc * _sigmoid(gc))).astype(BF16)
        gm = gm_ref[...]
        mix_ref[:, 0:1024] = (o_ref[...] * (gm * _sigmoid(gm))).astype(BF16)

    return _pcall(kern, name=name, out_shape=jax.ShapeDtypeStruct((S, 2048), BF16), grid=(S // tm,),
                  in_specs=[main(1024, 1), main(512, 4), main(512, 5), main(512, 6), main(512, 7), main(512, 8),
                            main(512, 9), prev(512, 4), prev(512, 6), prev(512, 8),
                            main(1024, 0), full((4, 128, 128)), full((1, 512)), full((3, 512))],
                  out_specs=main(2048, 0),
                  scratch=[pltpu.VMEM((tm + HALO, 512), F32), pltpu.VMEM((tm + HALO, 512), F32)],
                  dims=("parallel",), vmem_mb=48)(
                      proj, proj, proj, proj, proj, proj, proj, proj, proj, proj, o, wpool, ps.reshape(1, 512), convw)


def _mixer_bwd(dmix, proj, o, wpool, ps, convw, *, name):
    S = proj.shape[0]
    tm = min(256, S)
    n = S // tm
    main, prev, nxt, full = _mixer_specs(S, tm)

    def kern(dm_ref, dmn_ref, gm_ref, pin_ref, gp_ref, ch_ref, cb_ref, cc_ref, gc_ref,
             hp_ref, hch_ref, hcc_ref, gpn_ref, cbn_ref, gcn_ref, o_ref, wp_ref, ps_ref, cw_ref,
             d_ref, do_ref, dwp_ref, dps_ref, dcw_ref, xp, xu, ee, ed):
        i = pl.program_id(0)
        last = i == n - 1

        @pl.when(i == 0)
        def _():
            dwp_ref[...] = jnp.zeros_like(dwp_ref)
            dps_ref[...] = jnp.zeros_like(dps_ref)
            dcw_ref[...] = jnp.zeros_like(dcw_ref)

        _fill_halo(i, xp, xu, hp_ref, hch_ref, hcc_ref, pin_ref, ch_ref, cc_ref, tm)
        t1 = (i * tm + lax.broadcasted_iota(jnp.int32, (tm, 1), 0) + 1).astype(F32)
        t1n = ((i + 1) * tm + lax.broadcasted_iota(jnp.int32, (HALO, 1), 0) + 1).astype(F32)
        c_pin, c_gp, c_ch, c_cb, c_cc, c_gc = 1024, 1536, 2048, 2560, 3072, 3584

        for g in range(4):
            w = float(POOL_WINDOWS[g])
            lanes = slice(128 * g, 128 * g + 128)
            pooled = _pooled(xp, g, t1, tm)
            pb = pooled.astype(BF16)
            wp = wp_ref[g].astype(BF16)
            z = jnp.dot(pb, wp, preferred_element_type=F32)
            psl = ps_ref[:, lanes]
            sg, dsg = _silu_and_grad(gp_ref[:, lanes])
            dmp = dm_ref[:, 1024 + 128 * g:1024 + 128 * g + 128]
            dyp = dmp * sg
            d_ref[:, c_gp + 128 * g:c_gp + 128 * g + 128] = (dmp * (z * psl) * dsg).astype(BF16)
            dps_ref[:, lanes] += jnp.sum(dyp * z, axis=0, keepdims=True)
            dz = (dyp * psl).astype(BF16)
            dwp_ref[g] += lax.dot_general(pb, dz, TN, preferred_element_type=F32)
            dpl = lax.dot_general(dz, wp, NT, preferred_element_type=F32)
            ee[0:tm, lanes] = dpl / jnp.minimum(t1, w)
            gpn = gpn_ref[:, lanes]
            dzn = (dmn_ref[:, lanes] * (gpn * _sigmoid(gpn)) * psl).astype(BF16)
            dpn = lax.dot_general(dzn, wp, NT, preferred_element_type=F32)
            ee[tm:tm + HALO, lanes] = jnp.where(last, 0.0, dpn / jnp.minimum(t1n, w))
            acc = ee[0:tm, lanes]
            for k in range(1, POOL_WINDOWS[g]):
                acc = acc + ee[k:k + tm, lanes]
            d_ref[:, c_pin + 128 * g:c_pin + 128 * g + 128] = (acc - dpl).astype(BF16)

        yc = _conv_fwd(xu, cw_ref, tm)
        sgc, dsgc = _silu_and_grad(gc_ref[...])
        cb = cb_ref[...]
        dmc = dm_ref[:, 1536:2048]
        d_ref[:, c_gc:c_gc + 512] = (dmc * cb * yc * dsgc).astype(BF16)
        d_ref[:, c_cb:c_cb + 512] = (dmc * yc * sgc).astype(BF16)
        dyc = dmc * cb * sgc
        ed[0:tm, :] = dyc
        gcn = gcn_ref[...]
        ed[tm:tm + HALO, :] = jnp.where(last, 0.0, dmn_ref[:, 512:1024] * cbn_ref[...] * (gcn * _sigmoid(gcn)))
        dcw_ref[0:1, :] += jnp.sum(dyc * xu[HALO - 2:HALO - 2 + tm, :], axis=0, keepdims=True)
        dcw_ref[1:2, :] += jnp.sum(dyc * xu[HALO - 1:HALO - 1 + tm, :], axis=0, keepdims=True)
        dcw_ref[2:3, :] += jnp.sum(dyc * xu[HALO:HALO + tm, :], axis=0, keepdims=True)
        du = cw_ref[2:3, :] * dyc + cw_ref[1:2, :] * ed[1:1 + tm, :] + cw_ref[0:1, :] * ed[2:2 + tm, :]
        d_ref[:, c_cc:c_cc + 512] = (du * ch_ref[...]).astype(BF16)
        d_ref[:, c_ch:c_ch + 512] = (du * cc_ref[...]).astype(BF16)

        sgm, dsgm = _silu_and_grad(gm_ref[...])
        dmm = dm_ref[:, 0:1024]
        do_ref[...] = dmm * sgm
        d_ref[:, 0:1024] = (dmm * o_ref[...] * dsgm).astype(BF16)

    outs = (jax.ShapeDtypeStruct((S, W_MIX), BF16), jax.ShapeDtypeStruct((S, 1024), F32),
            jax.ShapeDtypeStruct((4, 128, 128), F32), jax.ShapeDtypeStruct((1, 512), F32),
            jax.ShapeDtypeStruct((3, 512), F32))
    scr = [pltpu.VMEM((tm + HALO, 512), F32) for _ in range(4)]
    return _pcall(kern, name=name, out_shape=outs, grid=(n,),
                  in_specs=[main(2048, 0), nxt(1024, 1),
                            main(1024, 1), main(512, 4), main(512, 5), main(512, 6), main(512, 7), main(512, 8),
                            main(512, 9), prev(512, 4), prev(512, 6), prev(512, 8),
                            nxt(512, 5), nxt(512, 7), nxt(512, 9),
                            main(1024, 0), full((4, 128, 128)), full((1, 512)), full((3, 512))],
                  out_specs=(main(W_MIX, 0), main(1024, 0), full((4, 128, 128)), full((1, 512)), full((3, 512))),
                  scratch=scr, dims=("arbitrary",), vmem_mb=56)(
                      dmix, dmix, proj, proj, proj, proj, proj, proj, proj, proj, proj, proj, proj, proj, proj,
                      o, wpool, ps.reshape(1, 512), convw)


def _outproj_residual(mix, wout, h, bout, *, name):
    S, Dm = h.shape
    tm = min(256, S)

    def kern(mix_ref, w_ref, h_ref, bo_ref, r_ref):
        out = jnp.dot(mix_ref[...], w_ref[...], preferred_element_type=F32) + bo_ref[...]
        r_ref[...] = ALPHA * h_ref[...] + out

    row = pl.BlockSpec((tm, Dm), lambda i: (i, 0))
    vec = pl.BlockSpec((1, Dm), lambda i: (0, 0))
    wsp = pl.BlockSpec((Dm, Dm), lambda i: (0, 0))
    return _pcall(kern, name=name, out_shape=jax.ShapeDtypeStruct((S, Dm), F32), grid=(S // tm,),
                  in_specs=[row, wsp, row, vec], out_specs=row, dims=("parallel",), vmem_mb=56)(
                      mix, wout, h, bout.reshape(1, Dm))


def _outproj_ln(mix, wout, h, bout, g, b, *, name):
    S, Dm = h.shape
    tm = min(256, S)

    def kern(mix_ref, w_ref, h_ref, bo_ref, g_ref, b_ref, y_ref, yb_ref, r_ref):
        out = jnp.dot(mix_ref[...], w_ref[...], preferred_element_type=F32) + bo_ref[...]
        r = ALPHA * h_ref[...] + out
        r_ref[...] = r
        mu = jnp.mean(r, axis=-1, keepdims=True)
        xc = r - mu
        var = jnp.mean(xc * xc, axis=-1, keepdims=True)
        y = xc * lax.rsqrt(var + LN_EPS) * g_ref[...] + b_ref[...]
        y_ref[...] = y
        yb_ref[...] = y.astype(BF16)

    row = pl.BlockSpec((tm, Dm), lambda i: (i, 0))
    vec = pl.BlockSpec((1, Dm), lambda i: (0, 0))
    wsp = pl.BlockSpec((Dm, Dm), lambda i: (0, 0))
    sds = jax.ShapeDtypeStruct((S, Dm), F32)
    return _pcall(kern, name=name, out_shape=(sds, jax.ShapeDtypeStruct((S, Dm), BF16), sds), grid=(S // tm,),
                  in_specs=[row, wsp, row, vec, vec, vec], out_specs=(row, row, row), dims=("parallel",),
                  vmem_mb=56)(
                      mix, wout, h, bout.reshape(1, Dm), g.reshape(1, Dm), b.reshape(1, Dm))


def _adamw_math(w, g, m, v):
    m = ADAM_B1 * m + (1.0 - ADAM_B1) * g
    v = ADAM_B2 * v + (1.0 - ADAM_B2) * (g * g)
    m_hat = m / (1.0 - ADAM_B1 ** ADAM_STEP)
    v_hat = v / (1.0 - ADAM_B2 ** ADAM_STEP)
    delta = -ADAM_LR * (m_hat / (jnp.sqrt(v_hat) + ADAM_EPS) + ADAM_WD * w)
    return delta, m, v


def _row_tile(R, C):
    best = None
    for cand in range(8, R, 8):
        if R % cand == 0 and cand * C <= 256 * 1024:
            best = cand
    return best if best is not None else R


def _adamw(w, g, m, v, *, name):
    shape = w.shape
    C = shape[-1]
    R = 1
    for s in shape[:-1]:
        R *= s
    tr = _row_tile(R, C)

    def kern(w_ref, g_ref, m_ref, v_ref, d_ref, mo_ref, vo_ref):
        d, mn, vn = _adamw_math(w_ref[...], g_ref[...], m_ref[...], v_ref[...])
        d_ref[...] = d
        mo_ref[...] = mn
        vo_ref[...] = vn

    blk = pl.BlockSpec((tr, C), lambda i: (i, 0))
    sds = jax.ShapeDtypeStruct((R, C), F32)
    outs = _pcall(kern, name=name, out_shape=(sds, sds, sds), grid=(R // tr,), in_specs=[blk] * 4,
                  out_specs=(blk, blk, blk), dims=("parallel",), vmem_mb=48)(
                      w.reshape(R, C), g.reshape(R, C), m.reshape(R, C), v.reshape(R, C))
    return tuple(t.reshape(shape) for t in outs)


def _adamw_halves(w, m, v, halves, c_idx, *, name):
    _, R, C = w.shape
    ch = C // 2
    tr = _row_tile(R, ch)
    nb = R // tr

    def kern(c_ref, w_ref, a0_ref, b0_ref, a1_ref, b1_ref, m_ref, v_ref, g_ref, d_ref, mo_ref, vo_ref):
        layer = pl.program_id(0) // nb
        mine = pl.program_id(1) == c_ref[0]
        g = jnp.where(layer == 0, jnp.where(mine, a0_ref[...], b0_ref[...]),
                      jnp.where(mine, a1_ref[...], b1_ref[...]))
        g_ref[...] = g
        d, mn, vn = _adamw_math(w_ref[...], g, m_ref[...], v_ref[...])
        d_ref[...] = d
        mo_ref[...] = mn
        vo_ref[...] = vn

    full = pl.BlockSpec((tr, ch), lambda i, hc, c: (i, hc))
    half = pl.BlockSpec((tr, ch), lambda i, hc, c: (i % nb, 0))
    gs = pltpu.PrefetchScalarGridSpec(num_scalar_prefetch=1, grid=(2 * nb, 2),
                                      in_specs=[full, half, half, half, half, full, full], out_specs=(full,) * 4)
    sds = jax.ShapeDtypeStruct((2 * R, C), F32)
    (a0, b0), (a1, b1) = halves
    outs = pl.pallas_call(kern, name=name, out_shape=(sds,) * 4, grid_spec=gs,
                          compiler_params=pltpu.CompilerParams(dimension_semantics=("parallel", "parallel"),
                                                               vmem_limit_bytes=48 << 20))(
                              c_idx, w.reshape(2 * R, C), a0, b0, a1, b1, m.reshape(2 * R, C), v.reshape(2 * R, C))
    return tuple(t.reshape(2, R, C) for t in outs)


def _small_sum_adamw(gathered, w, m, v, *, name):
    R = w.shape[0]

    def kern(ga_ref, w_ref, m_ref, v_ref, g_ref, d_ref, mo_ref, vo_ref):
        g = ga_ref[0]
        for k in range(1, N_DEV):
            g = g + ga_ref[k]
        g_ref[...] = g
        d, mn, vn = _adamw_math(w_ref[...], g, m_ref[...], v_ref[...])
        d_ref[...] = d
        mo_ref[...] = mn
        vo_ref[...] = vn

    sds = jax.ShapeDtypeStruct((R, LANE), F32)
    return _pcall(kern, name=name, out_shape=(sds, sds, sds, sds), vmem_mb=48)(gathered, w, m, v)


def _pair_sum(g, theirs, c_idx, *, name):
    R, C = g.shape
    ch = C // 2
    tr = _row_tile(R, ch)

    def kern(c_ref, a_ref, b_ref, o_ref):
        o_ref[...] = (a_ref[...] + b_ref[...]).astype(BF16)

    gs = pltpu.PrefetchScalarGridSpec(
        num_scalar_prefetch=1, grid=(R // tr,),
        in_specs=[pl.BlockSpec((tr, ch), lambda i, c: (i, c[0])), pl.BlockSpec((tr, ch), lambda i, c: (i, 0))],
        out_specs=pl.BlockSpec((tr, ch), lambda i, c: (i, 0)))
    return pl.pallas_call(kern, name=name, out_shape=jax.ShapeDtypeStruct((R, ch), BF16), grid_spec=gs,
                          compiler_params=pltpu.CompilerParams(dimension_semantics=("parallel",),
                                                               vmem_limit_bytes=48 << 20))(c_idx, g, theirs)


WeightRows = collections.namedtuple("WeightRows", "full_rows own_rows cols pieces zero_rows")


def _w_in_piece_a(j):
    return jnp.where(j == 0, 0, 1232 * j + GAP)


def _w_in_piece_b(j):
    return jnp.where(j == 0, GAP_AT + GAP, 1232 * j + GAP_AT + GAP)


W_IN = WeightRows(NP, 1232, D_MODEL, ((0, GAP_AT, _w_in_piece_a), (GAP_AT, 1232 - GAP_AT, _w_in_piece_b)),
                  ((GAP_AT, GAP),))
W_OUT = WeightRows(2048, 512, D_MODEL, ((0, 512, lambda j: 512 * j),), ())
W_UQ = WeightRows(2048, 384, Q_LORA, ((0, 192, lambda j: 512 * j), (192, 192, lambda j: 512 * j + 256)),
                  tuple((256 * h + 192, 64) for h in range(N_HEADS)))
W_UKV = WeightRows(2048, 512, KV_LORA, ((0, 512, lambda j: 512 * j),), ())
W_CONV = WeightRows(64, 16, 256, ((0, 16, lambda j: 16 * j),), ())
SHARDED = (W_IN, W_OUT, W_UQ, W_UKV)
SHARDED_NAMES = ("w_in", "w_out", "w_uq", "w_ukv")
WEIGHT_ROWS = dict(zip(SHARDED_NAMES, SHARDED))


def _mesh_pos():
    x, y, c = lax.axis_index("x"), lax.axis_index("y"), lax.axis_index("c")
    return x, y, c


def _other_chips(x, y):
    return [(1 - x, y), (x, 1 - y), (1 - x, 1 - y)]


def _rows(start, n):
    return pl.ds(pl.multiple_of(start, 16), n)


def _half_cols(spec, c):
    ch = spec.cols // 2
    return pl.ds(pl.multiple_of(c * ch, LANE), ch)


def _allgather_script(specs, shards, zeros):
    na = len(specs)
    zlist = [a for a in range(na) if zeros[a] is not None]
    plan_first, plan_own, plan_zero = [], [], []
    for a, spec in enumerate(specs):
        for p in range(len(spec.pieces)):
            plan_own.append((a, p))
            for k in range(3):
                plan_first.append((a, p, k))
        for z in range(len(spec.zero_rows)):
            for l in range(shards[a].shape[0]):
                plan_zero.append((a, z, l))
    nf = len(plan_first)
    n_sems = 2 * nf + len(plan_own) + len(plan_zero)

    def copies(ins_all, outs, send_sems, recv_sems):
        ins = ins_all[:na]
        zrefs = dict(zip(zlist, ins_all[na:]))
        x, y, c = _mesh_pos()
        j = 2 * x + y
        chips = _other_chips(x, y)
        sibling = (x, y, 1 - c)

        def remote(src, dst, sem, to):
            return pltpu.make_async_remote_copy(src_ref=src, dst_ref=dst, send_sem=send_sems.at[sem],
                                                recv_sem=recv_sems.at[sem], device_id=to, device_id_type=MESH)

        def block(a, p, chip, cols):
            _, n, dst = specs[a].pieces[p]
            return outs[a].at[:, _rows(dst(chip), n), cols]

        def first(i):
            a, p, k = plan_first[i]
            src0, n, _ = specs[a].pieces[p]
            cols = _half_cols(specs[a], c)
            return remote(ins[a].at[:, pl.ds(src0, n), cols], block(a, p, j, cols), i, (*chips[k], c))

        def landed(i, half):
            a, p, k = plan_first[i]
            return block(a, p, 2 * chips[k][0] + chips[k][1], _half_cols(specs[a], half))

        def arrival(i, half, sem):
            return remote(landed(i, half), landed(i, half), sem, sibling)

        def passed(i):
            return remote(landed(i, c), landed(i, c), nf + i, sibling)

        def own(i):
            a, p = plan_own[i]
            src0, n, _ = specs[a].pieces[p]
            return remote(ins[a].at[:, pl.ds(src0, n), :], block(a, p, j, slice(None)), 2 * nf + i, sibling)

        def zero(i):
            a, z, l = plan_zero[i]
            r0, n = specs[a].zero_rows[z]
            return remote(zrefs[a].at[pl.ds(0, n), :], outs[a].at[l, pl.ds(r0, n), :],
                          2 * nf + len(plan_own) + i, sibling)

        fixed = [own(i) for i in range(len(plan_own))] + [zero(i) for i in range(len(plan_zero))]
        return c, fixed, first, arrival, passed

    def start(ins, outs, send_sems, recv_sems):
        _, fixed, first, _, _ = copies(ins, outs, send_sems, recv_sems)
        for cp in fixed:
            cp.start()
        for i in range(nf):
            first(i).start()

    def finish(ins, outs, send_sems, recv_sems):
        c, fixed, first, arrival, passed = copies(ins, outs, send_sems, recv_sems)
        for i in range(nf):
            arrival(i, c, i).wait_recv()
            passed(i).start()
        for i in range(nf):
            arrival(i, 1 - c, nf + i).wait_recv()
        for cp in fixed:
            cp.wait()
        for i in range(nf):
            first(i).wait_send()
            passed(i).wait_send()

    out_shape = tuple(jax.ShapeDtypeStruct((shards[a].shape[0], spec.full_rows, spec.cols), BF16)
                      for a, spec in enumerate(specs))
    args = tuple(shards) + tuple(zeros[a] for a in zlist)
    return CommScript(args, out_shape, n_sems, start, finish)


def _start_all_wait_all(args, out_shape, n_sems, make_copies):
    def start(ins, outs, send_sems, recv_sems):
        for cp in make_copies(ins, outs, send_sems, recv_sems):
            cp.start()

    def finish(ins, outs, send_sems, recv_sems):
        for cp in make_copies(ins, outs, send_sems, recv_sems):
            cp.wait()

    return CommScript(tuple(args), tuple(out_shape), n_sems, start, finish)


def _exchange_script(specs, grads):
    na = len(grads)

    def make_copies(ins, outs, send_sems, recv_sems):
        x, y, c = _mesh_pos()
        return [pltpu.make_async_remote_copy(
            src_ref=ins[a].at[:, _half_cols(specs[a], 1 - c)], dst_ref=outs[a], send_sem=send_sems.at[a],
            recv_sem=recv_sems.at[a], device_id=(x, y, 1 - c), device_id_type=MESH) for a in range(na)]

    out_shape = [jax.ShapeDtypeStruct((s.full_rows, s.cols // 2), F32) for s in specs]
    return _start_all_wait_all(grads, out_shape, na, make_copies)


def _scatter_script(specs, parts):
    na = len(parts)
    plan = [(a, p, k) for a in range(na) for p in range(len(specs[a].pieces)) for k in range(3)]

    def make_copies(ins, outs, send_sems, recv_sems):
        x, y, c = _mesh_pos()
        chips = _other_chips(x, y)
        copies = []
        for i, (a, p, k) in enumerate(plan):
            src0, n, dst = specs[a].pieces[p]
            pk = 2 * chips[k][0] + chips[k][1]
            copies.append(pltpu.make_async_remote_copy(
                src_ref=ins[a].at[_rows(dst(pk), n), :], dst_ref=outs[a].at[k, pl.ds(src0, n), :],
                send_sem=send_sems.at[i], recv_sem=recv_sems.at[i], device_id=(*chips[k], c), device_id_type=MESH))
        return copies

    out_shape = [jax.ShapeDtypeStruct((3, s.own_rows, s.cols // 2), BF16) for s in specs]
    return _start_all_wait_all(parts, out_shape, len(plan), make_copies)


def _chip_sum(spec, part, recv, *, name):
    ch = spec.cols // 2
    npieces = len(spec.pieces)

    def kern(recv_ref, part_ref, o_ref, own_ref, sems):
        j = 2 * lax.axis_index("x") + lax.axis_index("y")
        copies = []
        for p, (src0, n, dst) in enumerate(spec.pieces):
            copies.append(pltpu.make_async_copy(part_ref.at[_rows(dst(j), n), :], own_ref.at[pl.ds(src0, n), :],
                                                sems.at[p]))
        for cp in copies:
            cp.start()
        for cp in copies:
            cp.wait()
        o_ref[...] = ((own_ref[...].astype(F32) + recv_ref[0].astype(F32)) + recv_ref[1].astype(F32)) \
            + recv_ref[2].astype(F32)

    vm = pl.BlockSpec(memory_space=pltpu.VMEM)
    return _pcall(kern, name=name, out_shape=jax.ShapeDtypeStruct((spec.own_rows, ch), F32),
                  in_specs=[vm, HBM_SPEC], out_specs=vm,
                  scratch=[pltpu.VMEM((spec.own_rows, ch), BF16), pltpu.SemaphoreType.DMA((npieces,))],
                  vmem_mb=48)(recv, part)


def _sibling_script(sums):
    na = len(sums)

    def make_copies(ins, outs, send_sems, recv_sems):
        x, y, c = _mesh_pos()
        return [pltpu.make_async_remote_copy(
            src_ref=ins[a], dst_ref=outs[a], send_sem=send_sems.at[a], recv_sem=recv_sems.at[a],
            device_id=(x, y, 1 - c), device_id_type=MESH) for a in range(na)]

    out_shape = [jax.ShapeDtypeStruct(t.shape, t.dtype) for t in sums]
    return _start_all_wait_all(sums, out_shape, na, make_copies)


class _SemWindow:
    def __init__(self, sems, offset):
        self._sems, self._offset = sems, offset

    @property
    def at(self):
        return self

    def __getitem__(self, i):
        return self._sems.at[i + self._offset]


def _merge_scripts(*scripts):
    a_off, o_off, s_off = [0], [0], [0]
    for s in scripts:
        a_off.append(a_off[-1] + len(s.args))
        o_off.append(o_off[-1] + len(s.out_shape))
        s_off.append(s_off[-1] + s.n_sems)

    def phase(which):
        def run(ins, outs, send_sems, recv_sems):
            for n, s in enumerate(scripts):
                getattr(s, which)(ins[a_off[n]:a_off[n + 1]], outs[o_off[n]:o_off[n + 1]],
                                  _SemWindow(send_sems, s_off[n]), _SemWindow(recv_sems, s_off[n]))
        return run

    return CommScript(sum((tuple(s.args) for s in scripts), ()), sum((tuple(s.out_shape) for s in scripts), ()),
                      s_off[-1], phase("start"), phase("finish"))


class _GradReducer:
    def __init__(self, layer, names, grads, c_idx):
        self.specs = tuple(WEIGHT_ROWS[nm] for nm in names)
        self.grads, self.c_idx = tuple(grads), c_idx
        self.names = [f"{nm}{layer}" for nm in names]

    def exchange(self):
        return _exchange_script(self.specs, self.grads)

    def scatter(self, theirs):
        self.parts = tuple(_pair_sum(g, th, self.c_idx, name=f"pair_sum_{nm}")
                           for g, th, nm in zip(self.grads, theirs, self.names))
        return _scatter_script(self.specs, self.parts)

    def sibling(self, recv):
        self.sums = tuple(_chip_sum(s, p, r, name=f"chip_sum_{nm}")
                          for s, p, r, nm in zip(self.specs, self.parts, recv, self.names))
        return _sibling_script(self.sums)

    def done(self, others):
        return list(zip(self.sums, others))


def _allgather_small(block, *, name):
    m_per, n = block.shape

    def body(x_ref, out_ref, send_sems, recv_sems, local_sem):
        x, y, c = _mesh_pos()
        me, sibling = (x, y, c), (x, y, 1 - c)
        chips = _other_chips(x, y)

        def rows(px, py, pc):
            return out_ref.at[4 * px + 2 * py + pc]

        def copy(k, blk, to, src=None):
            return pltpu.make_async_remote_copy(
                src_ref=rows(*blk) if src is None else src, dst_ref=rows(*blk), send_sem=send_sems.at[k],
                recv_sem=recv_sems.at[k], device_id=to, device_id_type=MESH)

        mine = pltpu.make_async_copy(x_ref, rows(*me), local_sem)
        mine.start()
        first = [copy(0, me, sibling, src=x_ref)]
        first += [copy(1 + k, me, (*chip, c), src=x_ref) for k, chip in enumerate(chips)]
        for cp in first:
            cp.start()
        passed = [copy(4 + k, (*chip, c), sibling) for k, chip in enumerate(chips)]
        for k, chip in enumerate(chips):
            copy(1 + k, (*chip, c), me).wait_recv()
            passed[k].start()
        copy(0, sibling, me).wait_recv()
        for k, chip in enumerate(chips):
            copy(4 + k, (*chip, 1 - c), me).wait_recv()
        for cp in first + passed:
            cp.wait_send()
        mine.wait()

    vm = pl.BlockSpec(memory_space=pltpu.VMEM)
    return _pcall(body, name=name, out_shape=jax.ShapeDtypeStruct((N_DEV, m_per, n), block.dtype),
                  in_specs=[vm], out_specs=vm,
                  scratch=[pltpu.SemaphoreType.DMA((7,)), pltpu.SemaphoreType.DMA((7,)), pltpu.SemaphoreType.DMA],
                  vmem_mb=48)(block)


def _rope_tables(positions):
    half = ROPE // 2
    inv_freq = ROPE_THETA ** (-jnp.arange(half, dtype=F32) / half)
    ang = positions.astype(F32)[:, None] * inv_freq
    cos, sin = jnp.cos(ang), jnp.sin(ang)
    S = positions.shape[0]
    cos_t = jnp.concatenate([cos, cos, jnp.ones((S, 64), F32)], axis=1)
    sin_t = jnp.concatenate([-sin, sin, jnp.zeros((S, 64), F32)], axis=1)
    return cos_t, sin_t


def _decode_conv(bits):
    rows = bits.reshape(DEPTH, N_CHIPS, 16, 256)[:, :, :3, :]
    conv = lax.bitcast_convert_type(rows.reshape(DEPTH, N_CHIPS, 3, 128, 2), F32)
    return jnp.transpose(conv, (0, 2, 1, 3)).reshape(DEPTH, 3, 512)


def _local_step(x, positions, target, emb_g, emb_b, w_in_t0, rest0, weights1, q_g, kv_g, w_pool, pool_scale,
                b_out, ln_g, ln_b, c_idx=None):
    cos_t, sin_t = _rope_tables(positions)
    if isinstance(w_in_t0, CommScript):
        (h, hb), (landed,) = _ln_fwd(x, emb_g, emb_b, name="emb_ln", comm=w_in_t0)
        w_in_t0 = landed[0]
    else:
        h, hb = _ln_fwd(x, emb_g, emb_b, name="emb_ln")
    weights = [None, weights1]
    saved = []
    for l in range(DEPTH):
        if l == 0 and isinstance(rest0, CommScript):
            proj, landed = _matmul(hb, w_in_t0, "nt", name="in_proj0", tm=1024, tn=1024, tk=2048, vmem_mb=56,
                                   comm=rest0)
            weights[0] = (w_in_t0,) + tuple(a[0] for a in landed[:3])
            conv_w = _decode_conv(landed[3])
        else:
            if l == 0:
                weights[0] = (w_in_t0,) + tuple(rest0[:3])
                conv_w = rest0[3]
            proj = _matmul(hb, weights[l][0], "nt", name=f"in_proj{l}", tm=1024, tn=1024, tk=2048, vmem_mb=56)
        w_in_t, w_out, w_uq_t, w_ukv_t = weights[l]
        qc, kc, v, vt, qn, kvn = _mla_qkv(proj, cos_t, sin_t, q_g[l], kv_g[l], w_uq_t, w_ukv_t, name=f"mla_qkv{l}")
        nxt = weights[l + 1] if l + 1 < DEPTH else None
        if isinstance(nxt, CommScript):
            (o, lse2), landed = _flash_fwd(qc, kc, vt, name=f"flash_fwd{l}", comm=nxt)
            weights[l + 1] = tuple(a[0] for a in landed)
        else:
            o, lse2 = _flash_fwd(qc, kc, vt, name=f"flash_fwd{l}")
        mix = _mixer_fwd(proj, o, w_pool[l], pool_scale[l], conv_w[l], name=f"mixer_fwd{l}")
        if l == DEPTH - 1:
            r = _outproj_residual(mix, w_out, h, b_out[l], name=f"out_proj{l}")
            saved.append((hb, proj, qc, kc, v, qn, kvn, o, lse2, mix, r))
        else:
            h_next, hb_next, r = _outproj_ln(mix, w_out, h, b_out[l], ln_g[l], ln_b[l], name=f"out_proj_ln{l}")
            saved.append((hb, proj, qc, kc, v, qn, kvn, o, lse2, mix, r))
            h, hb = h_next, hb_next

    small = [None] * DEPTH
    big = [None] * DEPTH
    above = scatter_above = None
    for l in reversed(range(DEPTH)):
        w_in_t, w_out, w_uq_t, w_ukv_t = weights[l]
        hb_in, proj, qc, kc, v, qn, kvn, o, lse2, mix, r = saved[l]
        if l == DEPTH - 1:
            loss_acc, dr, drb, d_ln_g, d_ln_b, d_b_out = _loss_ln_bwd(target, r, ln_g[l], ln_b[l], name="loss_ln_bwd")
        else:
            dr, drb, d_ln_g, d_ln_b, d_b_out = _ln_bwd(dh, r, ln_g[l], name=f"ln_bwd{l}")
        dmix = _matmul(drb, w_out, "nt", name=f"dmix{l}", tm=1024, tn=1024, tk=2048, vmem_mb=56)
        d_w_out = _matmul(mix, drb, "tn", name=f"dw_out{l}", tm=1024, tn=1024, tk=2048, vmem_mb=56)
        d_mix, do, d_w_pool, d_ps, d_conv = _mixer_bwd(dmix, proj, o, w_pool[l], pool_scale[l], conv_w[l],
                                                       name=f"mixer_bwd{l}")
        delta = _attn_delta(o, do, name=f"attn_delta{l}")
        if above is not None:
            (dqb, dkvb, dkr), recv = _flash_bwd(qc, kc, v, do, lse2, delta, cos_t, sin_t, name=f"flash_bwd{l}",
                                                comm=scatter_above)
            sibling_above = above.sibling(recv)
        else:
            dqb, dkvb, dkr = _flash_bwd(qc, kc, v, do, lse2, delta, cos_t, sin_t, name=f"flash_bwd{l}")
        d_mla, d_qg, d_kvg = _mla_qkv_bwd(dqb, dkvb, dkr, proj, cos_t, sin_t, q_g[l], kv_g[l], w_uq_t, w_ukv_t,
                                          name=f"mla_qkv_bwd{l}")
        d_w_uq_t = _matmul(dqb, qn, "tn", name=f"dw_uq{l}", tm=2048, tn=512, tk=2048, vmem_mb=56)
        d_w_ukv_t = _matmul(dkvb, kvn, "tn", name=f"dw_ukv{l}", tm=2048, tn=256, tk=2048, vmem_mb=56)
        small[l] = dict(q_g=d_qg[0], kv_g=d_kvg[0], w_pool=d_w_pool, pool_scale=d_ps[0], conv_w=d_conv,
                        b_out=d_b_out[0], ln_g=d_ln_g[0], ln_b=d_ln_b[0])
        rest = (d_w_out, d_w_uq_t, d_w_ukv_t)
        if c_idx is None:
            d_w_in_t = _dproj_t_times_h(d_mla, d_mix, hb_in, name=f"dw_in{l}")
            dh = _dproj_times_w(d_mla, d_mix, w_in_t, dr, ALPHA, name=f"dh{l}")
            big[l] = (d_w_in_t,) + rest
        elif l > 0:
            d_w_in_t = _dproj_t_times_h(d_mla, d_mix, hb_in, name=f"dw_in{l}")
            above = _GradReducer(l, SHARDED_NAMES, (d_w_in_t,) + rest, c_idx)
            dh, theirs = _dproj_times_w(d_mla, d_mix, w_in_t, dr, ALPHA, name=f"dh{l}", comm=above.exchange())
            scatter_above = above.scatter(theirs)
        else:
            red_rest = _GradReducer(l, SHARDED_NAMES[1:], rest, c_idx)
            d_w_in_t, landed = _dproj_t_times_h(d_mla, d_mix, hb_in, name=f"dw_in{l}",
                                                comm=_merge_scripts(sibling_above, red_rest.exchange()))
            big[l + 1] = above.done(landed[:len(SHARDED)])
            red_in = _GradReducer(l, SHARDED_NAMES[:1], (d_w_in_t,), c_idx)
            landed = _run_comm(_merge_scripts(red_in.exchange(), red_rest.scatter(landed[len(SHARDED):])),
                               name="exchange_w_in0")
            sibling_rest = red_rest.sibling(landed[1:])
            dh, landed = _dproj_times_w(d_mla, d_mix, w_in_t, dr, ALPHA, name=f"dh{l}",
                                        comm=_merge_scripts(red_in.scatter(landed[:1]), sibling_rest))
            recv_in, others_rest = landed[:1], landed[1:]
    grad_x, _, d_emb_g, d_emb_b, _ = _ln_bwd(dh, x, emb_g, name="emb_ln_bwd")
    if c_idx is not None:
        others_in = _run_comm(red_in.sibling(recv_in), name="send_to_sibling0")
        big[0] = red_in.done(others_in) + red_rest.done(others_rest)
    return loss_acc[0, 0], grad_x, d_emb_g[0], d_emb_b[0], small, big


SMALL_ORDER = ("emb_ln_g", "emb_ln_b", "q_norm_g", "kv_norm_g", "w_pool", "pool_scale", "b_out", "ln_g", "ln_b")


def _pack_small(arrs, extra_rows):
    flat = jnp.concatenate([a.reshape(-1) for a in arrs])
    rows = flat.shape[0] // LANE
    total = -(-(rows + extra_rows) // 8) * 8
    return jnp.pad(flat, (0, total * LANE - flat.shape[0])).reshape(total, LANE)


def _unpack_small(packed, shapes):
    flat = packed.reshape(-1)
    out, off = [], 0
    for shp in shapes:
        n = 1
        for s in shp:
            n *= s
        out.append(flat[off:off + n].reshape(shp))
        off += n
    return out, off


def kernel(x, positions, emb_ln_g, emb_ln_b, w_in, q_norm_g, kv_norm_g, w_uq, w_ukv, w_pool, pool_scale, conv_w, w_out, b_out, ln_g, ln_b, loss_target, m_emb_ln_g, m_emb_ln_b, m_w_in, m_q_norm_g, m_kv_norm_g, m_w_uq, m_w_ukv, m_w_pool, m_pool_scale, m_conv_w, m_w_out, m_b_out, m_ln_g, m_ln_b, v_emb_ln_g, v_emb_ln_b, v_w_in, v_q_norm_g, v_kv_norm_g, v_w_uq, v_w_ukv, v_w_pool, v_pool_scale, v_conv_w, v_w_out, v_b_out, v_ln_g, v_ln_b):
    xi, yi, ci = lax.axis_index("x"), lax.axis_index("y"), lax.axis_index("c")
    chip = 2 * xi + yi
    c_idx = ci.reshape(1).astype(jnp.int32)

    def t(a):
        return jnp.swapaxes(a, 1, 2)

    conv_bits = lax.bitcast_convert_type(conv_w.reshape(DEPTH, 3 * 128), BF16).reshape(DEPTH, 3, 256)
    conv_bits = jnp.pad(conv_bits, ((0, 0), (0, 13), (0, 0)))
    own = (t(w_in).astype(BF16), w_out.astype(BF16), t(w_uq).astype(BF16), t(w_ukv).astype(BF16))
    zeros = (jnp.zeros((GAP, D_MODEL), BF16), None, jnp.zeros((64, Q_LORA), BF16), None)
    gather_in0 = _allgather_script((W_IN,), (own[0][0:1],), zeros[:1])
    gather0 = _allgather_script(SHARDED[1:] + (W_CONV,), tuple(a[0:1] for a in own[1:]) + (conv_bits,),
                                zeros[1:] + (None,))
    gather1 = _allgather_script(SHARDED, tuple(a[1:2] for a in own), zeros)

    loss_part, grad_x, d_emb_g, d_emb_b, grads, reduced = _local_step(
        x[0], positions[0], loss_target[0], emb_ln_g, emb_ln_b, gather_in0, gather0, gather1, q_norm_g, kv_norm_g,
        w_pool, pool_scale, b_out, ln_g, ln_b, c_idx)

    small_g = [d_emb_g, d_emb_b,
               jnp.stack([grads[l]["q_g"] for l in range(DEPTH)]), jnp.stack([grads[l]["kv_g"] for l in range(DEPTH)]),
               jnp.stack([grads[l]["w_pool"] for l in range(DEPTH)]),
               jnp.stack([grads[l]["pool_scale"] for l in range(DEPTH)]),
               jnp.stack([grads[l]["b_out"] for l in range(DEPTH)]), jnp.stack([grads[l]["ln_g"] for l in range(DEPTH)]),
               jnp.stack([grads[l]["ln_b"] for l in range(DEPTH)]),
               jnp.stack([grads[l]["conv_w"] for l in range(DEPTH)]),
               jnp.pad(loss_part.reshape(1), (0, LANE - 1))]
    small_w = [emb_ln_g, emb_ln_b, q_norm_g, kv_norm_g, w_pool, pool_scale, b_out, ln_g, ln_b]
    small_m = [m_emb_ln_g, m_emb_ln_b, m_q_norm_g, m_kv_norm_g, m_w_pool, m_pool_scale, m_b_out, m_ln_g, m_ln_b]
    small_v = [v_emb_ln_g, v_emb_ln_b, v_q_norm_g, v_kv_norm_g, v_w_pool, v_pool_scale, v_b_out, v_ln_g, v_ln_b]
    extra = (DEPTH * 3 * 512 + LANE) // LANE
    packed_g = _pack_small(small_g, 0)
    gathered = _allgather_small(packed_g, name="allgather_small")
    g_tot, d_small, m_small, v_small = _small_sum_adamw(
        gathered, _pack_small(small_w, extra), _pack_small(small_m, extra), _pack_small(small_v, extra),
        name="small_sum_adamw")
    shapes = [w.shape for w in small_w]
    g_list, off = _unpack_small(g_tot, shapes)
    d_list, _ = _unpack_small(d_small, shapes)
    m_list, _ = _unpack_small(m_small, shapes)
    v_list, _ = _unpack_small(v_small, shapes)
    flat_tot = g_tot.reshape(-1)
    conv_tot = flat_tot[off:off + DEPTH * 3 * 512].reshape(DEPTH, 3, 512)
    loss = flat_tot[off + DEPTH * 3 * 512]
    g_conv = lax.dynamic_slice_in_dim(conv_tot, chip * 128, 128, axis=2)

    def halves(a):
        return [reduced[l][a] for l in range(DEPTH)]

    def whole(a):
        return jnp.stack([jnp.where(ci == 0, jnp.concatenate([mine, oth], axis=1),
                                    jnp.concatenate([oth, mine], axis=1)) for mine, oth in halves(a)])

    upd = {}
    upd["w_in"] = tuple(t(o) for o in _adamw_halves(t(w_in), t(m_w_in), t(v_w_in), halves(0), c_idx,
                                                    name="adamw_w_in"))
    upd["w_out"] = _adamw_halves(w_out, m_w_out, v_w_out, halves(1), c_idx, name="adamw_w_out")
    g_uq, g_ukv = t(whole(2)), t(whole(3))
    upd["w_uq"] = (g_uq,) + _adamw(w_uq, g_uq, m_w_uq, v_w_uq, name="adamw_w_uq")
    upd["w_ukv"] = (g_ukv,) + _adamw(w_ukv, g_ukv, m_w_ukv, v_w_ukv, name="adamw_w_ukv")
    upd["conv_w"] = (g_conv,) + _adamw(conv_w, g_conv, m_conv_w, v_conv_w, name="adamw_conv_w")
    for i, nm in enumerate(SMALL_ORDER):
        upd[nm] = (g_list[i], d_list[i], m_list[i], v_list[i])

    order = ("emb_ln_g", "emb_ln_b", "w_in", "q_norm_g", "kv_norm_g", "w_uq", "w_ukv", "w_pool", "pool_scale",
             "conv_w", "w_out", "b_out", "ln_g", "ln_b")
    outs = [loss, grad_x[None]]
    for field in range(4):
        outs += [upd[nm][field] for nm in order]
    return tuple(outs)
```

```python
import collections

import jax
import jax.numpy as jnp
from jax import lax
from jax.experimental import pallas as pl
from jax.experimental.pallas import tpu as pltpu

F32 = jnp.float32
BF16 = jnp.bfloat16
MESH = pl.DeviceIdType.MESH

D_MODEL = 2048
DEPTH = 2
N_HEADS = 8
NOPE = 128
ROPE = 64
Q_LORA = 512
KV_LORA = 256
D_MLA = 1024
POOL_WINDOWS = (2, 4, 8, 16)
D_IN_PROJ = 4928
LN_EPS = 1e-5
RMS_EPS = 1e-6
ROPE_THETA = 10000.0
ALPHA = (2 * DEPTH) ** 0.25
SCALE = (NOPE + ROPE) ** -0.5
LOG2E = 1.4426950408889634
SCALE_LOG2E = SCALE * LOG2E
ADAM_LR = 0.001
ADAM_B1 = 0.9
ADAM_B2 = 0.999
ADAM_EPS = 1e-08
ADAM_WD = 0.01
ADAM_STEP = 10

NP = 5120
GAP_AT = 832
GAP = NP - D_IN_PROJ
W_MLA = 1024
W_MIX = NP - W_MLA
HALO = 16
LANE = 128
N_CHIPS = 4
N_DEV = 8
TQ = 512
FWD_GROUP = 4

NN = (((1,), (0,)), ((), ()))
NT = (((1,), (1,)), ((), ()))
TN = (((0,), (0,)), ((), ()))


CommScript = collections.namedtuple("CommScript", "args out_shape n_sems start finish")
HBM_SPEC = pl.BlockSpec(memory_space=pl.ANY)


def _pcall(kern, *, name, out_shape, grid=None, in_specs=None, out_specs=None, scratch=(), dims=None,
           vmem_mb=None, comm=None):
    cp = {}
    if dims is not None:
        cp["dimension_semantics"] = dims if comm is None else ("arbitrary",) * len(dims)
    if vmem_mb is not None:
        cp["vmem_limit_bytes"] = vmem_mb << 20
    if comm is None:
        args = dict(name=name, out_shape=out_shape, scratch_shapes=list(scratch),
                    compiler_params=pltpu.CompilerParams(**cp))
        if grid is not None:
            args["grid"] = grid
        if in_specs is not None:
            args["in_specs"] = in_specs
        if out_specs is not None:
            args["out_specs"] = out_specs
        return pl.pallas_call(kern, **args)

    single = not isinstance(out_shape, (tuple, list))
    own_out = (out_shape,) if single else tuple(out_shape)
    own_out_specs = (out_specs,) if single else tuple(out_specs)
    n_in, n_out, n_scr = len(in_specs), len(own_out), len(scratch)
    na, no = len(comm.args), len(comm.out_shape)

    def at(end):
        cond = None
        for d, n in enumerate(grid):
            here = pl.program_id(d) == (n - 1 if end else 0)
            cond = here if cond is None else jnp.logical_and(cond, here)
        return cond

    def wrapped(*refs):
        own_in, c_in = refs[:n_in], refs[n_in:n_in + na]
        o0 = n_in + na
        own_o, c_out = refs[o0:o0 + n_out], refs[o0 + n_out:o0 + n_out + no]
        s0 = o0 + n_out + no
        own_s, (send_sems, recv_sems) = refs[s0:s0 + n_scr], refs[s0 + n_scr:]

        @pl.when(at(False))
        def _():
            comm.start(c_in, c_out, send_sems, recv_sems)

        kern(*own_in, *own_o, *own_s)

        @pl.when(at(True))
        def _():
            comm.finish(c_in, c_out, send_sems, recv_sems)

    call = pl.pallas_call(
        wrapped, name=name, out_shape=own_out + tuple(comm.out_shape), grid=grid,
        in_specs=list(in_specs) + [HBM_SPEC] * na, out_specs=own_out_specs + (HBM_SPEC,) * no,
        scratch_shapes=list(scratch) + [pltpu.SemaphoreType.DMA((comm.n_sems,)),
                                        pltpu.SemaphoreType.DMA((comm.n_sems,))],
        compiler_params=pltpu.CompilerParams(**cp))

    def run(*args):
        res = call(*args, *comm.args)
        own = res[0] if single else tuple(res[:n_out])
        return own, tuple(res[n_out:])

    return run


def _run_comm(script, *, name):
    na, no = len(script.args), len(script.out_shape)

    def body(*refs):
        ins, outs = refs[:na], refs[na:na + no]
        send_sems, recv_sems = refs[na + no:]
        script.start(ins, outs, send_sems, recv_sems)
        script.finish(ins, outs, send_sems, recv_sems)

    return pl.pallas_call(
        body, name=name, out_shape=tuple(script.out_shape), in_specs=[HBM_SPEC] * na, out_specs=(HBM_SPEC,) * no,
        scratch_shapes=[pltpu.SemaphoreType.DMA((script.n_sems,)), pltpu.SemaphoreType.DMA((script.n_sems,))])(
            *script.args)


def _sigmoid(g):
    return 1.0 / (1.0 + jnp.exp(-g))


def _silu_and_grad(g):
    sig = _sigmoid(g)
    return g * sig, sig * (1.0 + g * (1.0 - sig))


def _matmul(a, b, mode, *, name, tm, tn, tk, out_dtype=F32, vmem_mb=48, comm=None):
    if mode == "nn":
        (M, K), N = a.shape, b.shape[1]
    elif mode == "nt":
        (M, K), N = a.shape, b.shape[0]
    else:
        (K, M), N = a.shape, b.shape[1]
    tm, tn, tk = min(tm, M), min(tn, N), min(tk, K)
    assert M % tm == 0 and N % tn == 0 and K % tk == 0, (name, M, N, K)
    nk = K // tk
    dn = {"nn": NN, "nt": NT, "tn": TN}[mode]
    if mode == "tn":
        a_spec = pl.BlockSpec((tk, tm), lambda i, j, k: (k, i))
    else:
        a_spec = pl.BlockSpec((tm, tk), lambda i, j, k: (i, k))
    if mode == "nt":
        b_spec = pl.BlockSpec((tn, tk), lambda i, j, k: (j, k))
    else:
        b_spec = pl.BlockSpec((tk, tn), lambda i, j, k: (k, j))
    o_spec = pl.BlockSpec((tm, tn), lambda i, j, k: (i, j))

    def kern(a_ref, b_ref, o_ref, *rest):
        part = lax.dot_general(a_ref[...].astype(BF16), b_ref[...].astype(BF16), dn,
                               preferred_element_type=F32)
        if nk == 1:
            o_ref[...] = part.astype(out_dtype)
        else:
            acc_ref = rest[0]
            k = pl.program_id(2)

            @pl.when(k == 0)
            def _():
                acc_ref[...] = part

            @pl.when(k > 0)
            def _():
                acc_ref[...] += part

            @pl.when(k == nk - 1)
            def _():
                o_ref[...] = acc_ref[...].astype(out_dtype)

    scratch = [pltpu.VMEM((tm, tn), F32)] if nk > 1 else []
    return _pcall(kern, name=name, out_shape=jax.ShapeDtypeStruct((M, N), out_dtype),
                  grid=(M // tm, N // tn, nk), in_specs=[a_spec, b_spec], out_specs=o_spec, scratch=scratch,
                  dims=("parallel", "parallel", "arbitrary"), vmem_mb=vmem_mb, comm=comm)(a, b)


def _dproj_times_w(d_mla, d_mix, wt, add, add_scale, *, name, comm=None):
    S = d_mla.shape[0]
    Dm = wt.shape[1]
    tm, tn, tk = min(1024, S), 1024, 2048
    nk = 1 + W_MIX // tk

    def kern(a1_ref, a2_ref, b1_ref, b2_ref, add_ref, o_ref, acc_ref):
        k = pl.program_id(2)

        @pl.when(k == 0)
        def _():
            acc_ref[...] = jnp.dot(a1_ref[...], b1_ref[...], preferred_element_type=F32)

        @pl.when(k > 0)
        def _():
            acc_ref[...] += jnp.dot(a2_ref[...], b2_ref[...], preferred_element_type=F32)

        @pl.when(k == nk - 1)
        def _():
            o_ref[...] = add_scale * add_ref[...] + acc_ref[...]

    o_spec = pl.BlockSpec((tm, tn), lambda i, j, k: (i, j))
    b2_spec = pl.BlockSpec((pl.Element(tk), pl.Element(tn)),
                           lambda i, j, k: (pl.multiple_of(W_MLA + tk * jnp.maximum(k - 1, 0), W_MLA),
                                            pl.multiple_of(j * tn, tn)))
    return _pcall(kern, name=name, out_shape=jax.ShapeDtypeStruct((S, Dm), F32), grid=(S // tm, Dm // tn, nk),
                  in_specs=[pl.BlockSpec((tm, W_MLA), lambda i, j, k: (i, 0)),
                            pl.BlockSpec((tm, tk), lambda i, j, k: (i, jnp.maximum(k - 1, 0))),
                            pl.BlockSpec((W_MLA, tn), lambda i, j, k: (0, j)), b2_spec, o_spec],
                  out_specs=o_spec, scratch=[pltpu.VMEM((tm, tn), F32)],
                  dims=("parallel", "parallel", "arbitrary"), vmem_mb=56, comm=comm)(d_mla, d_mix, wt, wt, add)


def _dproj_t_times_h(d_mla, d_mix, h, *, name, comm=None):
    S, Dm = h.shape
    tm, tn, tk = W_MLA, 1024, min(2048, S)
    nk = S // tk

    def kern(a1_ref, a2_ref, b_ref, o_ref, acc_ref):
        i = pl.program_id(0)
        k = pl.program_id(2)
        b = b_ref[...].astype(BF16)

        def accumulate(part):
            @pl.when(k == 0)
            def _():
                acc_ref[...] = part

            @pl.when(k > 0)
            def _():
                acc_ref[...] += part

        @pl.when(i == 0)
        def _():
            accumulate(lax.dot_general(a1_ref[...], b, TN, preferred_element_type=F32))

        @pl.when(i > 0)
        def _():
            accumulate(lax.dot_general(a2_ref[...], b, TN, preferred_element_type=F32))

        @pl.when(k == nk - 1)
        def _():
            o_ref[...] = acc_ref[...]

    return _pcall(kern, name=name, out_shape=jax.ShapeDtypeStruct((NP, Dm), F32), grid=(NP // tm, Dm // tn, nk),
                  in_specs=[pl.BlockSpec((tk, tm), lambda i, j, k: (jnp.where(i == 0, k, nk - 1), 0)),
                            pl.BlockSpec((tk, tm), lambda i, j, k: (jnp.where(i == 0, 0, k), jnp.maximum(i - 1, 0))),
                            pl.BlockSpec((tk, tn), lambda i, j, k: (k, j))],
                  out_specs=pl.BlockSpec((tm, tn), lambda i, j, k: (i, j)), scratch=[pltpu.VMEM((tm, tn), F32)],
                  dims=("parallel", "parallel", "arbitrary"), vmem_mb=48, comm=comm)(d_mla, d_mix, h)


def _ln_fwd(x, g, b, *, name, comm=None):
    S, Dm = x.shape
    tm = min(512, S)

    def kern(x_ref, g_ref, b_ref, y_ref, yb_ref):
        xf = x_ref[...]
        mu = jnp.mean(xf, axis=-1, keepdims=True)
        xc = xf - mu
        var = jnp.mean(xc * xc, axis=-1, keepdims=True)
        y = xc * lax.rsqrt(var + LN_EPS) * g_ref[...] + b_ref[...]
        y_ref[...] = y
        yb_ref[...] = y.astype(BF16)

    row = pl.BlockSpec((tm, Dm), lambda i: (i, 0))
    vec = pl.BlockSpec((1, Dm), lambda i: (0, 0))
    return _pcall(kern, name=name,
                  out_shape=(jax.ShapeDtypeStruct((S, Dm), F32), jax.ShapeDtypeStruct((S, Dm), BF16)),
                  grid=(S // tm,), in_specs=[row, vec, vec], out_specs=(row, row), dims=("parallel",), vmem_mb=48,
                  comm=comm)(
                      x, g.reshape(1, Dm), b.reshape(1, Dm))


def _ln_bwd(dy, r, g, *, name):
    S, Dm = r.shape
    tm = min(512, S)

    def kern(dy_ref, r_ref, g_ref, dr_ref, drb_ref, dg_ref, db_ref, ds_ref):
        @pl.when(pl.program_id(0) == 0)
        def _():
            dg_ref[...] = jnp.zeros_like(dg_ref)
            db_ref[...] = jnp.zeros_like(db_ref)
            ds_ref[...] = jnp.zeros_like(ds_ref)

        rf = r_ref[...]
        dyf = dy_ref[...]
        mu = jnp.mean(rf, axis=-1, keepdims=True)
        xc = rf - mu
        var = jnp.mean(xc * xc, axis=-1, keepdims=True)
        rstd = lax.rsqrt(var + LN_EPS)
        xhat = xc * rstd
        dxh = dyf * g_ref[...]
        c1 = jnp.mean(dxh, axis=-1, keepdims=True)
        c2 = jnp.mean(dxh * xhat, axis=-1, keepdims=True)
        dr = rstd * (dxh - c1 - xhat * c2)
        dr_ref[...] = dr
        drb_ref[...] = dr.astype(BF16)
        dg_ref[...] += jnp.sum(dyf * xhat, axis=0, keepdims=True)
        db_ref[...] += jnp.sum(dyf, axis=0, keepdims=True)
        ds_ref[...] += jnp.sum(dr, axis=0, keepdims=True)

    row = pl.BlockSpec((tm, Dm), lambda i: (i, 0))
    vec = pl.BlockSpec((1, Dm), lambda i: (0, 0))
    vshape = jax.ShapeDtypeStruct((1, Dm), F32)
    return _pcall(kern, name=name,
                  out_shape=(jax.ShapeDtypeStruct((S, Dm), F32), jax.ShapeDtypeStruct((S, Dm), BF16),
                             vshape, vshape, vshape),
                  grid=(S // tm,), in_specs=[row, row, vec], out_specs=(row, row, vec, vec, vec),
                  dims=("arbitrary",), vmem_mb=48)(dy, r, g.reshape(1, Dm))


def _loss_ln_bwd(target, r, g, b, *, name):
    S, Dm = r.shape
    tm = min(512, S)

    def kern(t_ref, r_ref, g_ref, b_ref, l_ref, dr_ref, drb_ref, dg_ref, db_ref, ds_ref):
        @pl.when(pl.program_id(0) == 0)
        def _():
            l_ref[...] = jnp.zeros_like(l_ref)
            dg_ref[...] = jnp.zeros_like(dg_ref)
            db_ref[...] = jnp.zeros_like(db_ref)
            ds_ref[...] = jnp.zeros_like(ds_ref)

        rf = r_ref[...]
        mu = jnp.mean(rf, axis=-1, keepdims=True)
        xc = rf - mu
        var = jnp.mean(xc * xc, axis=-1, keepdims=True)
        rstd = lax.rsqrt(var + LN_EPS)
        xhat = xc * rstd
        e = (xhat * g_ref[...] + b_ref[...]) - t_ref[...]
        dyf = e / float(Dm)
        per_row = jnp.mean(e * e, axis=-1, keepdims=True)
        l_ref[...] += 0.5 * jnp.sum(per_row, axis=0, keepdims=True)
        dxh = dyf * g_ref[...]
        c1 = jnp.mean(dxh, axis=-1, keepdims=True)
        c2 = jnp.mean(dxh * xhat, axis=-1, keepdims=True)
        dr = rstd * (dxh - c1 - xhat * c2)
        dr_ref[...] = dr
        drb_ref[...] = dr.astype(BF16)
        dg_ref[...] += jnp.sum(dyf * xhat, axis=0, keepdims=True)
        db_ref[...] += jnp.sum(dyf, axis=0, keepdims=True)
        ds_ref[...] += jnp.sum(dr, axis=0, keepdims=True)

    row = pl.BlockSpec((tm, Dm), lambda i: (i, 0))
    vec = pl.BlockSpec((1, Dm), lambda i: (0, 0))
    acc = pl.BlockSpec((8, LANE), lambda i: (0, 0))
    vshape = jax.ShapeDtypeStruct((1, Dm), F32)
    return _pcall(kern, name=name,
                  out_shape=(jax.ShapeDtypeStruct((8, LANE), F32), jax.ShapeDtypeStruct((S, Dm), F32),
                             jax.ShapeDtypeStruct((S, Dm), BF16), vshape, vshape, vshape),
                  grid=(S // tm,), in_specs=[row, row, vec, vec], out_specs=(acc, row, row, vec, vec, vec),
                  dims=("arbitrary",), vmem_mb=56)(target, r, g.reshape(1, Dm), b.reshape(1, Dm))


def _rot_sum(t):
    return pltpu.roll(t, 32, 1) + pltpu.roll(t, 96, 1)


def _mla_qkv(proj, cos_t, sin_t, qg, kvg, wuq_t, wukv_t, *, name):
    S = proj.shape[0]
    tm = min(256, S)

    def kern(ql_ref, kvl_ref, kr_ref, cos_ref, sin_ref, qg_ref, kvg_ref, wuq_ref, wukv_ref,
             qc_ref, kc_ref, v_ref, vt_ref, qn_ref, kvn_ref):
        cosv = cos_ref[...]
        sinv = sin_ref[...]

        def rope(t):
            return t * cosv + _rot_sum(t) * sinv

        ql = ql_ref[...]
        qn = (ql * lax.rsqrt(jnp.mean(ql * ql, axis=-1, keepdims=True) + RMS_EPS) * qg_ref[...]).astype(BF16)
        kvl = kvl_ref[...]
        kvn = (kvl * lax.rsqrt(jnp.mean(kvl * kvl, axis=-1, keepdims=True) + RMS_EPS) * kvg_ref[...]).astype(BF16)
        qn_ref[...] = qn
        kvn_ref[...] = kvn
        q = lax.dot_general(qn, wuq_ref[...], NT, preferred_element_type=F32)
        kv = lax.dot_general(kvn, wukv_ref[...], NT, preferred_element_type=F32)
        kr = rope(kr_ref[...]).astype(BF16)
        for h in range(N_HEADS):
            c0 = 256 * h
            qc_ref[:, c0:c0 + 128] = q[:, c0:c0 + 128].astype(BF16)
            qc_ref[:, c0 + 128:c0 + 256] = rope(q[:, c0 + 128:c0 + 256]).astype(BF16)
            kc_ref[:, c0:c0 + 128] = kv[:, c0:c0 + 128].astype(BF16)
            kc_ref[:, c0 + 128:c0 + 256] = kr
            vh = kv[:, c0 + 128:c0 + 256]
            v_ref[:, 128 * h:128 * h + 128] = vh.astype(BF16)
            vt_ref[h] = jnp.transpose(vh).astype(BF16)

    def row(w, blk):
        return pl.BlockSpec((tm, w), lambda i: (i, blk))

    def full(shape):
        return pl.BlockSpec(shape, lambda i: (0,) * len(shape))

    t = min(TQ, S)
    per = t // tm
    vt_spec = pl.BlockSpec((N_HEADS, None, 128, tm), lambda i: (0, i // per, 0, i % per))
    outs = (jax.ShapeDtypeStruct((S, 2048), BF16), jax.ShapeDtypeStruct((S, 2048), BF16),
            jax.ShapeDtypeStruct((S, 1024), BF16), jax.ShapeDtypeStruct((N_HEADS, S // t, 128, t), BF16),
            jax.ShapeDtypeStruct((S, Q_LORA), BF16), jax.ShapeDtypeStruct((S, KV_LORA), BF16))
    return _pcall(kern, name=name, out_shape=outs, grid=(S // tm,),
                  in_specs=[row(512, 0), row(256, 2), row(128, 6), row(128, 0), row(128, 0),
                            full((1, Q_LORA)), full((1, KV_LORA)), full((2048, Q_LORA)), full((2048, KV_LORA))],
                  out_specs=(row(2048, 0), row(2048, 0), row(1024, 0), vt_spec, row(512, 0), row(256, 0)),
                  dims=("parallel",), vmem_mb=48)(
                      proj, proj, proj, cos_t, sin_t, qg.reshape(1, -1), kvg.reshape(1, -1), wuq_t, wukv_t)


def _mla_qkv_bwd(dqb, dkvb, dkr_heads, proj, cos_t, sin_t, qg, kvg, wuq_t, wukv_t, *, name):
    S = proj.shape[0]
    tm = min(256, S)

    def kern(dqb_ref, dkvb_ref, dkrh_ref, ql_ref, kvl_ref, cos_ref, sin_ref, qg_ref, kvg_ref, wuq_ref, wukv_ref,
             dml_ref, dqg_ref, dkvg_ref):
        @pl.when(pl.program_id(0) == 0)
        def _():
            dqg_ref[...] = jnp.zeros_like(dqg_ref)
            dkvg_ref[...] = jnp.zeros_like(dkvg_ref)

        cosv = cos_ref[...]
        sinv = sin_ref[...]

        def unrope(t):
            return t * cosv - _rot_sum(t) * sinv

        dkr = dkrh_ref[:, 0:128]
        for h in range(1, N_HEADS):
            dkr = dkr + dkrh_ref[:, 128 * h:128 * h + 128]

        def rms_bwd(x, g, dy):
            n = x.shape[-1]
            rs = lax.rsqrt(jnp.mean(x * x, axis=-1, keepdims=True) + RMS_EPS)
            dyg = dy * g
            dx = rs * dyg - x * (rs * rs * rs) * (jnp.sum(dyg * x, axis=-1, keepdims=True) / n)
            return dx, jnp.sum(dy * (x * rs), axis=0, keepdims=True)

        dqn = jnp.dot(dqb_ref[...], wuq_ref[...], preferred_element_type=F32)
        dql, dqg = rms_bwd(ql_ref[...], qg_ref[...], dqn)
        dqg_ref[...] += dqg
        dkvn = jnp.dot(dkvb_ref[...], wukv_ref[...], preferred_element_type=F32)
        dkvl, dkvg = rms_bwd(kvl_ref[...], kvg_ref[...], dkvn)
        dkvg_ref[...] += dkvg
        dml_ref[:, 0:512] = dql.astype(BF16)
        dml_ref[:, 512:768] = dkvl.astype(BF16)
        dml_ref[:, 768:896] = unrope(dkr).astype(BF16)
        dml_ref[:, 896:1024] = jnp.zeros((tm, 128), BF16)

    def row(w, blk):
        return pl.BlockSpec((tm, w), lambda i: (i, blk))

    def full(shape):
        return pl.BlockSpec(shape, lambda i: (0,) * len(shape))

    outs = (jax.ShapeDtypeStruct((S, W_MLA), BF16), jax.ShapeDtypeStruct((1, Q_LORA), F32),
            jax.ShapeDtypeStruct((1, KV_LORA), F32))
    return _pcall(kern, name=name, out_shape=outs, grid=(S // tm,),
                  in_specs=[row(2048, 0), row(2048, 0), row(1024, 0), row(512, 0), row(256, 2),
                            row(128, 0), row(128, 0), full((1, Q_LORA)), full((1, KV_LORA)),
                            full((2048, Q_LORA)), full((2048, KV_LORA))],
                  out_specs=(row(W_MLA, 0), full((1, Q_LORA)), full((1, KV_LORA))),
                  dims=("arbitrary",), vmem_mb=56)(
                      dqb, dkvb, dkr_heads, proj, proj, cos_t, sin_t, qg.reshape(1, -1), kvg.reshape(1, -1),
                      wuq_t, wukv_t)


def _flash_fwd(qc, kc, vt, *, name, comm=None):
    S = qc.shape[0]
    t = min(TQ, S)
    n = S // t

    def kern(q_ref, k_ref, vt_ref, o_ref, lse_ref, m_s, l_s, acc_s):
        qi = pl.program_id(1)
        m_s[...] = jnp.full_like(m_s, -jnp.inf)
        l_s[...] = jnp.zeros_like(l_s)
        acc_s[...] = jnp.zeros_like(acc_s)

        half = t // 2

        def scores(kb, q_lo=0, q_n=t, k_n=t):
            k0 = pl.multiple_of(kb * t, t)
            return lax.dot_general(k_ref[pl.ds(k0, k_n), :], q_ref[q_lo:q_lo + q_n, :], NT,
                                   preferred_element_type=F32)

        def update(kb, st, q_lo=0, diagonal=False):
            k_n, q_n = st.shape
            if diagonal:
                krow = lax.broadcasted_iota(jnp.int32, (k_n, q_n), 0)
                qcol = lax.broadcasted_iota(jnp.int32, (k_n, q_n), 1) + q_lo
                st = jnp.where(krow <= qcol, st, -jnp.inf)
            lanes = slice(q_lo, q_lo + q_n)
            m_prev = m_s[:, lanes]
            m_new = jnp.maximum(m_prev, jnp.max(st, axis=0, keepdims=True))
            a = jnp.exp2((m_prev - m_new) * SCALE_LOG2E)
            pt = jnp.exp2((st - m_new) * SCALE_LOG2E)
            l_s[:, lanes] = a * l_s[:, lanes] + jnp.sum(pt, axis=0, keepdims=True)
            acc_s[:, lanes] = a * acc_s[:, lanes] + jnp.dot(vt_ref[kb, :, 0:k_n], pt.astype(BF16),
                                                            preferred_element_type=F32)
            m_s[:, lanes] = m_new

        def group(kb, count, last_diagonal):
            whole = count - 1 if last_diagonal else count
            sts = [scores(kb + g) for g in range(whole)]
            if last_diagonal:
                kd = kb + count - 1
                s_lo, s_hi = scores(kd, 0, half, half), scores(kd, half, half, t)
            for g in range(whole):
                update(kb + g, sts[g])
            if last_diagonal:
                update(kd, s_lo, 0, True)
                update(kd, s_hi, half, True)

        def body(i, carry):
            group(FWD_GROUP * i, FWD_GROUP, False)
            return carry

        full = qi // FWD_GROUP
        lax.fori_loop(0, full, body, 0)
        for rem in range(FWD_GROUP):
            @pl.when(qi - FWD_GROUP * full == rem)
            def _():
                group(qi - rem, rem + 1, True)
        o_ref[...] = jnp.transpose(acc_s[...] / l_s[...])
        lse_ref[pl.ds(qi, 1), :] = m_s[...] * SCALE_LOG2E + jnp.log2(l_s[...])

    q_spec = pl.BlockSpec((t, 256), lambda h, qi: (qi, h))
    k_spec = pl.BlockSpec((S, 256), lambda h, qi: (0, h))
    vt_spec = pl.BlockSpec((None, n, 128, t), lambda h, qi: (h, 0, 0, 0))
    o_spec = pl.BlockSpec((t, 128), lambda h, qi: (qi, h))
    lse_spec = pl.BlockSpec((None, n, t), lambda h, qi: (h, 0, 0))
    return _pcall(kern, name=name,
                  out_shape=(jax.ShapeDtypeStruct((S, D_MLA), F32), jax.ShapeDtypeStruct((N_HEADS, n, t), F32)),
                  grid=(N_HEADS, n), in_specs=[q_spec, k_spec, vt_spec], out_specs=(o_spec, lse_spec),
                  scratch=[pltpu.VMEM((1, t), F32), pltpu.VMEM((1, t), F32), pltpu.VMEM((128, t), F32)],
                  dims=("parallel", "arbitrary"), vmem_mb=48, comm=comm)(qc, kc, vt)


def _attn_delta(o, do, *, name):
    S = o.shape[0]
    t = min(TQ, S)
    n = S // t

    def kern(o_ref, do_ref, dl_ref):
        i = pl.program_id(0)
        prod = o_ref[...] * do_ref[...]
        lane = lax.broadcasted_iota(jnp.int32, (t, LANE), 1)
        dmat = jnp.zeros((t, LANE), F32)
        for h in range(N_HEADS):
            dmat = jnp.where(lane == h, jnp.sum(prod[:, 128 * h:128 * h + 128], axis=1, keepdims=True), dmat)
        dmat_t = jnp.transpose(dmat)
        for h in range(N_HEADS):
            dl_ref[h, pl.ds(i, 1), :] = dmat_t[h:h + 1, :]

    row = pl.BlockSpec((t, D_MLA), lambda i: (i, 0))
    return _pcall(kern, name=name, out_shape=jax.ShapeDtypeStruct((N_HEADS, n, t), F32), grid=(n,),
                  in_specs=[row, row], out_specs=pl.BlockSpec((N_HEADS, n, t), lambda i: (0, 0, 0)),
                  dims=("arbitrary",), vmem_mb=48)(o, do)


def _flash_bwd(qc, kc, v, do, lse2, delta, cos_t, sin_t, *, name, comm=None):
    S = qc.shape[0]
    t = min(TQ, S)
    n = S // t

    def kern(q_ref, k_ref, v_ref, do_ref, lse_ref, dl_ref, cos_ref, sin_ref, dqb_ref, dkvb_ref, dkr_ref,
             dq_ref, dk_ref, dv_ref):
        ki = pl.program_id(1)

        @pl.when(ki == 0)
        def _():
            dq_ref[...] = jnp.zeros_like(dq_ref)

        dk_ref[...] = jnp.zeros_like(dk_ref)
        dv_ref[...] = jnp.zeros_like(dv_ref)

        half = t // 2

        def step(qb, q_lo=0, q_n=t, k_n=t, diagonal=False):
            q0 = pl.multiple_of(qb * t + q_lo, half)
            lanes = slice(q_lo, q_lo + q_n)
            kt = k_ref[0:k_n, :]
            qblk = q_ref[pl.ds(q0, q_n), :]
            dob = do_ref[pl.ds(q0, q_n), :].astype(BF16)
            st = lax.dot_general(kt, qblk, NT, preferred_element_type=F32)
            pt = jnp.exp2(st * SCALE_LOG2E - lse_ref[pl.ds(qb, 1), lanes])
            if diagonal:
                krow = lax.broadcasted_iota(jnp.int32, (k_n, q_n), 0)
                qcol = lax.broadcasted_iota(jnp.int32, (k_n, q_n), 1) + q_lo
                pt = jnp.where(krow <= qcol, pt, 0.0)
            dv_ref[0:k_n, :] += jnp.dot(pt.astype(BF16), dob, preferred_element_type=F32)
            dpt = lax.dot_general(v_ref[0:k_n, :], dob, NT, preferred_element_type=F32)
            dst = (pt * (dpt - dl_ref[pl.ds(qb, 1), lanes]) * SCALE).astype(BF16)
            dk_ref[0:k_n, :] += jnp.dot(dst, qblk, preferred_element_type=F32)
            dq_ref[pl.ds(q0, q_n), :] += lax.dot_general(dst, kt, TN, preferred_element_type=F32)

        step(ki, 0, half, half, True)
        step(ki, half, half, t, True)
        rest = n - 1 - ki

        def body(i, carry):
            step(ki + 1 + 2 * i)
            step(ki + 2 + 2 * i)
            return carry

        lax.fori_loop(0, rest // 2, body, 0)

        @pl.when(rest % 2 == 1)
        def _():
            step(n - 1)

        dkvb_ref[:, 0:128] = dk_ref[:, 0:128].astype(BF16)
        dkvb_ref[:, 128:256] = dv_ref[...].astype(BF16)
        dkr_ref[...] = dk_ref[:, 128:256]

        @pl.when(ki == n - 1)
        def _():
            dqb_ref[:, 0:128] = dq_ref[:, 0:128].astype(BF16)
            dqr = dq_ref[:, 128:256]
            dqb_ref[:, 128:256] = (dqr * cos_ref[...] - _rot_sum(dqr) * sin_ref[...]).astype(BF16)

    def whole(w):
        return pl.BlockSpec((S, w), lambda h, ki: (0, h))

    def krow(w):
        return pl.BlockSpec((t, w), lambda h, ki: (ki, h))

    stat = pl.BlockSpec((None, n, t), lambda h, ki: (h, 0, 0))
    table = pl.BlockSpec((S, 128), lambda h, ki: (0, 0))
    return _pcall(kern, name=name,
                  out_shape=(jax.ShapeDtypeStruct((S, 2048), BF16), jax.ShapeDtypeStruct((S, 2048), BF16),
                             jax.ShapeDtypeStruct((S, D_MLA), F32)),
                  grid=(N_HEADS, n),
                  in_specs=[whole(256), krow(256), krow(128), whole(128), stat, stat, table, table],
                  out_specs=(whole(256), krow(256), krow(128)),
                  scratch=[pltpu.VMEM((S, 256), F32), pltpu.VMEM((t, 256), F32), pltpu.VMEM((t, 128), F32)],
                  dims=("parallel", "arbitrary"), vmem_mb=56, comm=comm)(qc, kc, v, do, lse2, delta, cos_t, sin_t)


def _mixer_specs(S, tm):
    hb = tm // HALO
    last_hb = S // HALO - 1

    def main(w, blk):
        return pl.BlockSpec((tm, w), lambda i: (i, blk))

    def prev(w, blk):
        return pl.BlockSpec((HALO, w), lambda i: (jnp.maximum(i * hb - 1, 0), blk))

    def nxt(w, blk):
        return pl.BlockSpec((HALO, w), lambda i: (jnp.minimum((i + 1) * hb, last_hb), blk))

    def full(shape):
        return pl.BlockSpec(shape, lambda i: (0,) * len(shape))

    return main, prev, nxt, full


def _fill_halo(i, xp, xu, hp_ref, hch_ref, hcc_ref, pin_ref, ch_ref, cc_ref, tm):
    first = i == 0
    xp[0:HALO, :] = jnp.where(first, 0.0, hp_ref[...])
    xp[HALO:HALO + tm, :] = pin_ref[...]
    xu[0:HALO, :] = jnp.where(first, 0.0, hch_ref[...] * hcc_ref[...])
    xu[HALO:HALO + tm, :] = cc_ref[...] * ch_ref[...]


def _pooled(xp, g, t1, tm):
    w = POOL_WINDOWS[g]
    lanes = slice(128 * g, 128 * g + 128)
    x0 = xp[HALO:HALO + tm, lanes]
    acc = x0
    for k in range(1, w):
        acc = acc + xp[HALO - k:HALO - k + tm, lanes]
    return acc / jnp.minimum(t1, float(w)) - x0


def _conv_fwd(xu, cw_ref, tm):
    return (cw_ref[0:1, :] * xu[HALO - 2:HALO - 2 + tm, :] + cw_ref[1:2, :] * xu[HALO - 1:HALO - 1 + tm, :]
            + cw_ref[2:3, :] * xu[HALO:HALO + tm, :])


def _mixer_fwd(proj, o, wpool, ps, convw, *, name):
    S = proj.shape[0]
    tm = min(256, S)
    main, prev, _, full = _mixer_specs(S, tm)

    def kern(gm_ref, pin_ref, gp_ref, ch_ref, cb_ref, cc_ref, gc_ref, hp_ref, hch_ref, hcc_ref,
             o_ref, wp_ref, ps_ref, cw_ref, mix_ref, xp, xu):
        i = pl.program_id(0)
        _fill_halo(i, xp, xu, hp_ref, hch_ref, hcc_ref, pin_ref, ch_ref, cc_ref, tm)
        t1 = (i * tm + lax.broadcasted_iota(jnp.int32, (tm, 1), 0) + 1).astype(F32)
        for g in range(4):
            lanes = slice(128 * g, 128 * g + 128)
            pooled = _pooled(xp, g, t1, tm)
            z = jnp.dot(pooled.astype(BF16), wp_ref[g].astype(BF16), preferred_element_type=F32)
            gp = gp_ref[:, lanes]
            y = z * ps_ref[:, lanes] * (gp * _sigmoid(gp))
            mix_ref[:, 1024 + 128 * g:1024 + 128 * g + 128] = y.astype(BF16)
        gc = gc_ref[...]
        mix_ref[:, 1536:2048] = (cb_ref[...] * _conv_fwd(xu, cw_ref, tm) * (gc * _sigmoid(gc))).astype(BF16)
        gm = gm_ref[...]
        mix_ref[:, 0:1024] = (o_ref[...] * (gm * _sigmoid(gm))).astype(BF16)

    return _pcall(kern, name=name, out_shape=jax.ShapeDtypeStruct((S, 2048), BF16), grid=(S // tm,),
                  in_specs=[main(1024, 1), main(512, 4), main(512, 5), main(512, 6), main(512, 7), main(512, 8),
                            main(512, 9), prev(512, 4), prev(512, 6), prev(512, 8),
                            main(1024, 0), full((4, 128, 128)), full((1, 512)), full((3, 512))],
                  out_specs=main(2048, 0),
                  scratch=[pltpu.VMEM((tm + HALO, 512), F32), pltpu.VMEM((tm + HALO, 512), F32)],
                  dims=("parallel",), vmem_mb=48)(
                      proj, proj, proj, proj, proj, proj, proj, proj, proj, proj, o, wpool, ps.reshape(1, 512), convw)


def _mixer_bwd(dmix, proj, o, wpool, ps, convw, *, name):
    S = proj.shape[0]
    tm = min(256, S)
    n = S // tm
    main, prev, nxt, full = _mixer_specs(S, tm)

    def kern(dm_ref, dmn_ref, gm_ref, pin_ref, gp_ref, ch_ref, cb_ref, cc_ref, gc_ref,
             hp_ref, hch_ref, hcc_ref, gpn_ref, cbn_ref, gcn_ref, o_ref, wp_ref, ps_ref, cw_ref,
             d_ref, do_ref, dwp_ref, dps_ref, dcw_ref, xp, xu, ee, ed):
        i = pl.program_id(0)
        last = i == n - 1

        @pl.when(i == 0)
        def _():
            dwp_ref[...] = jnp.zeros_like(dwp_ref)
            dps_ref[...] = jnp.zeros_like(dps_ref)
            dcw_ref[...] = jnp.zeros_like(dcw_ref)

        _fill_halo(i, xp, xu, hp_ref, hch_ref, hcc_ref, pin_ref, ch_ref, cc_ref, tm)
        t1 = (i * tm + lax.broadcasted_iota(jnp.int32, (tm, 1), 0) + 1).astype(F32)
        t1n = ((i + 1) * tm + lax.broadcasted_iota(jnp.int32, (HALO, 1), 0) + 1).astype(F32)
        c_pin, c_gp, c_ch, c_cb, c_cc, c_gc = 1024, 1536, 2048, 2560, 3072, 3584

        for g in range(4):
            w = float(POOL_WINDOWS[g])
            lanes = slice(128 * g, 128 * g + 128)
            pooled = _pooled(xp, g, t1, tm)
            pb = pooled.astype(BF16)
            wp = wp_ref[g].astype(BF16)
            z = jnp.dot(pb, wp, preferred_element_type=F32)
            psl = ps_ref[:, lanes]
            sg, dsg = _silu_and_grad(gp_ref[:, lanes])
            dmp = dm_ref[:, 1024 + 128 * g:1024 + 128 * g + 128]
            dyp = dmp * sg
            d_ref[:, c_gp + 128 * g:c_gp + 128 * g + 128] = (dmp * (z * psl) * dsg).astype(BF16)
            dps_ref[:, lanes] += jnp.sum(dyp * z, axis=0, keepdims=True)
            dz = (dyp * psl).astype(BF16)
            dwp_ref[g] += lax.dot_general(pb, dz, TN, preferred_element_type=F32)
            dpl = lax.dot_general(dz, wp, NT, preferred_element_type=F32)
            ee[0:tm, lanes] = dpl / jnp.minimum(t1, w)
            gpn = gpn_ref[:, lanes]
            dzn = (dmn_ref[:, lanes] * (gpn * _sigmoid(gpn)) * psl).astype(BF16)
            dpn = lax.dot_general(dzn, wp, NT, preferred_element_type=F32)
            ee[tm:tm + HALO, lanes] = jnp.where(last, 0.0, dpn / jnp.minimum(t1n, w))
            acc = ee[0:tm, lanes]
            for k in range(1, POOL_WINDOWS[g]):
                acc = acc + ee[k:k + tm, lanes]
            d_ref[:, c_pin + 128 * g:c_pin + 128 * g + 128] = (acc - dpl).astype(BF16)

        yc = _conv_fwd(xu, cw_ref, tm)
        sgc, dsgc = _silu_and_grad(gc_ref[...])
        cb = cb_ref[...]
        dmc = dm_ref[:, 1536:2048]
        d_ref[:, c_gc:c_gc + 512] = (dmc * cb * yc * dsgc).astype(BF16)
        d_ref[:, c_cb:c_cb + 512] = (dmc * yc * sgc).astype(BF16)
        dyc = dmc * cb * sgc
        ed[0:tm, :] = dyc
        gcn = gcn_ref[...]
        ed[tm:tm + HALO, :] = jnp.where(last, 0.0, dmn_ref[:, 512:1024] * cbn_ref[...] * (gcn * _sigmoid(gcn)))
        dcw_ref[0:1, :] += jnp.sum(dyc * xu[HALO - 2:HALO - 2 + tm, :], axis=0, keepdims=True)
        dcw_ref[1:2, :] += jnp.sum(dyc * xu[HALO - 1:HALO - 1 + tm, :], axis=0, keepdims=True)
        dcw_ref[2:3, :] += jnp.sum(dyc * xu[HALO:HALO + tm, :], axis=0, keepdims=True)
        du = cw_ref[2:3, :] * dyc + cw_ref[1:2, :] * ed[1:1 + tm, :] + cw_ref[0:1, :] * ed[2:2 + tm, :]
        d_ref[:, c_cc:c_cc + 512] = (du * ch_ref[...]).astype(BF16)
        d_ref[:, c_ch:c_ch + 512] = (du * cc_ref[...]).astype(BF16)

        sgm, dsgm = _silu_and_grad(gm_ref[...])
        dmm = dm_ref[:, 0:1024]
        do_ref[...] = dmm * sgm
        d_ref[:, 0:1024] = (dmm * o_ref[...] * dsgm).astype(BF16)

    outs = (jax.ShapeDtypeStruct((S, W_MIX), BF16), jax.ShapeDtypeStruct((S, 1024), F32),
            jax.ShapeDtypeStruct((4, 128, 128), F32), jax.ShapeDtypeStruct((1, 512), F32),
            jax.ShapeDtypeStruct((3, 512), F32))
    scr = [pltpu.VMEM((tm + HALO, 512), F32) for _ in range(4)]
    return _pcall(kern, name=name, out_shape=outs, grid=(n,),
                  in_specs=[main(2048, 0), nxt(1024, 1),
                            main(1024, 1), main(512, 4), main(512, 5), main(512, 6), main(512, 7), main(512, 8),
                            main(512, 9), prev(512, 4), prev(512, 6), prev(512, 8),
                            nxt(512, 5), nxt(512, 7), nxt(512, 9),
                            main(1024, 0), full((4, 128, 128)), full((1, 512)), full((3, 512))],
                  out_specs=(main(W_MIX, 0), main(1024, 0), full((4, 128, 128)), full((1, 512)), full((3, 512))),
                  scratch=scr, dims=("arbitrary",), vmem_mb=56)(
                      dmix, dmix, proj, proj, proj, proj, proj, proj, proj, proj, proj, proj, proj, proj, proj,
                      o, wpool, ps.reshape(1, 512), convw)


def _outproj_residual(mix, wout, h, bout, *, name):
    S, Dm = h.shape
    tm = min(256, S)

    def kern(mix_ref, w_ref, h_ref, bo_ref, r_ref):
        out = jnp.dot(mix_ref[...], w_ref[...], preferred_element_type=F32) + bo_ref[...]
        r_ref[...] = ALPHA * h_ref[...] + out

    row = pl.BlockSpec((tm, Dm), lambda i: (i, 0))
    vec = pl.BlockSpec((1, Dm), lambda i: (0, 0))
    wsp = pl.BlockSpec((Dm, Dm), lambda i: (0, 0))
    return _pcall(kern, name=name, out_shape=jax.ShapeDtypeStruct((S, Dm), F32), grid=(S // tm,),
                  in_specs=[row, wsp, row, vec], out_specs=row, dims=("parallel",), vmem_mb=56)(
                      mix, wout, h, bout.reshape(1, Dm))


def _outproj_ln(mix, wout, h, bout, g, b, *, name):
    S, Dm = h.shape
    tm = min(256, S)

    def kern(mix_ref, w_ref, h_ref, bo_ref, g_ref, b_ref, y_ref, yb_ref, r_ref):
        out = jnp.dot(mix_ref[...], w_ref[...], preferred_element_type=F32) + bo_ref[...]
        r = ALPHA * h_ref[...] + out
        r_ref[...] = r
        mu = jnp.mean(r, axis=-1, keepdims=True)
        xc = r - mu
        var = jnp.mean(xc * xc, axis=-1, keepdims=True)
        y = xc * lax.rsqrt(var + LN_EPS) * g_ref[...] + b_ref[...]
        y_ref[...] = y
        yb_ref[...] = y.astype(BF16)

    row = pl.BlockSpec((tm, Dm), lambda i: (i, 0))
    vec = pl.BlockSpec((1, Dm), lambda i: (0, 0))
    wsp = pl.BlockSpec((Dm, Dm), lambda i: (0, 0))
    sds = jax.ShapeDtypeStruct((S, Dm), F32)
    return _pcall(kern, name=name, out_shape=(sds, jax.ShapeDtypeStruct((S, Dm), BF16), sds), grid=(S // tm,),
                  in_specs=[row, wsp, row, vec, vec, vec], out_specs=(row, row, row), dims=("parallel",),
                  vmem_mb=56)(
                      mix, wout, h, bout.reshape(1, Dm), g.reshape(1, Dm), b.reshape(1, Dm))


def _adamw_math(w, g, m, v):
    m = ADAM_B1 * m + (1.0 - ADAM_B1) * g
    v = ADAM_B2 * v + (1.0 - ADAM_B2) * (g * g)
    m_hat = m / (1.0 - ADAM_B1 ** ADAM_STEP)
    v_hat = v / (1.0 - ADAM_B2 ** ADAM_STEP)
    delta = -ADAM_LR * (m_hat / (jnp.sqrt(v_hat) + ADAM_EPS) + ADAM_WD * w)
    return delta, m, v


def _row_tile(R, C):
    best = None
    for cand in range(8, R, 8):
        if R % cand == 0 and cand * C <= 256 * 1024:
            best = cand
    return best if best is not None else R


def _adamw(w, g, m, v, *, name):
    shape = w.shape
    C = shape[-1]
    R = 1
    for s in shape[:-1]:
        R *= s
    tr = _row_tile(R, C)

    def kern(w_ref, g_ref, m_ref, v_ref, d_ref, mo_ref, vo_ref):
        d, mn, vn = _adamw_math(w_ref[...], g_ref[...], m_ref[...], v_ref[...])
        d_ref[...] = d
        mo_ref[...] = mn
        vo_ref[...] = vn

    blk = pl.BlockSpec((tr, C), lambda i: (i, 0))
    sds = jax.ShapeDtypeStruct((R, C), F32)
    outs = _pcall(kern, name=name, out_shape=(sds, sds, sds), grid=(R // tr,), in_specs=[blk] * 4,
                  out_specs=(blk, blk, blk), dims=("parallel",), vmem_mb=48)(
                      w.reshape(R, C), g.reshape(R, C), m.reshape(R, C), v.reshape(R, C))
    return tuple(t.reshape(shape) for t in outs)


def _adamw_halves(w, m, v, halves, c_idx, *, name):
    _, R, C = w.shape
    ch = C // 2
    tr = _row_tile(R, ch)
    nb = R // tr

    def kern(c_ref, w_ref, a0_ref, b0_ref, a1_ref, b1_ref, m_ref, v_ref, g_ref, d_ref, mo_ref, vo_ref):
        layer = pl.program_id(0) // nb
        mine = pl.program_id(1) == c_ref[0]
        g = jnp.where(layer == 0, jnp.where(mine, a0_ref[...], b0_ref[...]),
                      jnp.where(mine, a1_ref[...], b1_ref[...]))
        g_ref[...] = g
        d, mn, vn = _adamw_math(w_ref[...], g, m_ref[...], v_ref[...])
        d_ref[...] = d
        mo_ref[...] = mn
        vo_ref[...] = vn

    full = pl.BlockSpec((tr, ch), lambda i, hc, c: (i, hc))
    half = pl.BlockSpec((tr, ch), lambda i, hc, c: (i % nb, 0))
    gs = pltpu.PrefetchScalarGridSpec(num_scalar_prefetch=1, grid=(2 * nb, 2),
                                      in_specs=[full, half, half, half, half, full, full], out_specs=(full,) * 4)
    sds = jax.ShapeDtypeStruct((2 * R, C), F32)
    (a0, b0), (a1, b1) = halves
    outs = pl.pallas_call(kern, name=name, out_shape=(sds,) * 4, grid_spec=gs,
                          compiler_params=pltpu.CompilerParams(dimension_semantics=("parallel", "parallel"),
                                                               vmem_limit_bytes=48 << 20))(
                              c_idx, w.reshape(2 * R, C), a0, b0, a1, b1, m.reshape(2 * R, C), v.reshape(2 * R, C))
    return tuple(t.reshape(2, R, C) for t in outs)


def _packed_pieces(shape):
    if len(shape) == 4:
        return [((l * shape[1] + g) * 128, 128, (l, g)) for l in range(shape[0]) for g in range(shape[1])]
    per_row = shape[1] // LANE
    return [(a * per_row + j, 1, (slice(a, a + 1), slice(LANE * j, LANE * (j + 1))))
            for a in range(shape[0]) for j in range(per_row)]


def _small_sum_adamw(gathered, weights, *, name):
    R = gathered.shape[1]
    nw = len(weights)
    shapes = [w.shape for w, _, _ in weights]
    first_row, r0 = [], 0
    for shp in shapes:
        first_row.append(r0)
        n = 1
        for s in shp:
            n *= s
        r0 += n // LANE

    def kern(ga_ref, *refs):
        ins, gsum_ref, outs = refs[:3 * nw], refs[3 * nw], refs[3 * nw + 1:]
        g = ga_ref[0]
        for k in range(1, N_DEV):
            g = g + ga_ref[k]
        gsum_ref[...] = g
        for p, shp in enumerate(shapes):
            w_ref, m_ref, v_ref = ins[3 * p:3 * p + 3]
            g_out, d_out, m_out, v_out = outs[4 * p:4 * p + 4]
            for row, rows, idx in _packed_pieces(shp):
                gp = gsum_ref[first_row[p] + row:first_row[p] + row + rows, :]
                d, mn, vn = _adamw_math(w_ref[idx], gp, m_ref[idx], v_ref[idx])
                g_out[idx] = gp
                d_out[idx] = d
                m_out[idx] = mn
                v_out[idx] = vn

    out_shape = [jax.ShapeDtypeStruct((R, LANE), F32)]
    for shp in shapes:
        out_shape += [jax.ShapeDtypeStruct(shp, F32)] * 4
    flat = [a for wmv in weights for a in wmv]
    res = _pcall(kern, name=name, out_shape=tuple(out_shape), vmem_mb=48)(gathered, *flat)
    return res[0], [tuple(res[1 + 4 * p:5 + 4 * p]) for p in range(nw)]


def _pair_sum(g, theirs, c_idx, *, name):
    R, C = g.shape
    ch = C // 2
    tr = _row_tile(R, ch)

    def kern(c_ref, a_ref, b_ref, o_ref):
        o_ref[...] = (a_ref[...] + b_ref[...]).astype(BF16)

    gs = pltpu.PrefetchScalarGridSpec(
        num_scalar_prefetch=1, grid=(R // tr,),
        in_specs=[pl.BlockSpec((tr, ch), lambda i, c: (i, c[0])), pl.BlockSpec((tr, ch), lambda i, c: (i, 0))],
        out_specs=pl.BlockSpec((tr, ch), lambda i, c: (i, 0)))
    return pl.pallas_call(kern, name=name, out_shape=jax.ShapeDtypeStruct((R, ch), BF16), grid_spec=gs,
                          compiler_params=pltpu.CompilerParams(dimension_semantics=("parallel",),
                                                               vmem_limit_bytes=48 << 20))(c_idx, g, theirs)


WeightRows = collections.namedtuple("WeightRows", "full_rows own_rows cols pieces zero_rows")


def _w_in_piece_a(j):
    return jnp.where(j == 0, 0, 1232 * j + GAP)


def _w_in_piece_b(j):
    return jnp.where(j == 0, GAP_AT + GAP, 1232 * j + GAP_AT + GAP)


W_IN = WeightRows(NP, 1232, D_MODEL, ((0, GAP_AT, _w_in_piece_a), (GAP_AT, 1232 - GAP_AT, _w_in_piece_b)),
                  ((GAP_AT, GAP),))
W_OUT = WeightRows(2048, 512, D_MODEL, ((0, 512, lambda j: 512 * j),), ())
W_UQ = WeightRows(2048, 384, Q_LORA, ((0, 192, lambda j: 512 * j), (192, 192, lambda j: 512 * j + 256)),
                  tuple((256 * h + 192, 64) for h in range(N_HEADS)))
W_UKV = WeightRows(2048, 512, KV_LORA, ((0, 512, lambda j: 512 * j),), ())
W_CONV = WeightRows(64, 16, 256, ((0, 16, lambda j: 16 * j),), ())
SHARDED = (W_IN, W_OUT, W_UQ, W_UKV)
SHARDED_NAMES = ("w_in", "w_out", "w_uq", "w_ukv")
WEIGHT_ROWS = dict(zip(SHARDED_NAMES, SHARDED))


def _mesh_pos():
    x, y, c = lax.axis_index("x"), lax.axis_index("y"), lax.axis_index("c")
    return x, y, c


def _other_chips(x, y):
    return [(1 - x, y), (x, 1 - y), (1 - x, 1 - y)]


def _rows(start, n):
    return pl.ds(pl.multiple_of(start, 16), n)


def _half_cols(spec, c):
    ch = spec.cols // 2
    return pl.ds(pl.multiple_of(c * ch, LANE), ch)


def _allgather_script(specs, shards, zeros):
    na = len(specs)
    zlist = [a for a in range(na) if zeros[a] is not None]
    plan_first, plan_own, plan_zero = [], [], []
    for a, spec in enumerate(specs):
        for p in range(len(spec.pieces)):
            plan_own.append((a, p))
            for k in range(3):
                plan_first.append((a, p, k))
        for z in range(len(spec.zero_rows)):
            for l in range(shards[a].shape[0]):
                plan_zero.append((a, z, l))
    nf = len(plan_first)
    n_sems = 2 * nf + len(plan_own) + len(plan_zero)

    def copies(ins_all, outs, send_sems, recv_sems):
        ins = ins_all[:na]
        zrefs = dict(zip(zlist, ins_all[na:]))
        x, y, c = _mesh_pos()
        j = 2 * x + y
        chips = _other_chips(x, y)
        sibling = (x, y, 1 - c)

        def remote(src, dst, sem, to):
            return pltpu.make_async_remote_copy(src_ref=src, dst_ref=dst, send_sem=send_sems.at[sem],
                                                recv_sem=recv_sems.at[sem], device_id=to, device_id_type=MESH)

        def block(a, p, chip, cols):
            _, n, dst = specs[a].pieces[p]
            return outs[a].at[:, _rows(dst(chip), n), cols]

        def first(i):
            a, p, k = plan_first[i]
            src0, n, _ = specs[a].pieces[p]
            cols = _half_cols(specs[a], c)
            return remote(ins[a].at[:, pl.ds(src0, n), cols], block(a, p, j, cols), i, (*chips[k], c))

        def landed(i, half):
            a, p, k = plan_first[i]
            return block(a, p, 2 * chips[k][0] + chips[k][1], _half_cols(specs[a], half))

        def arrival(i, half, sem):
            return remote(landed(i, half), landed(i, half), sem, sibling)

        def passed(i):
            return remote(landed(i, c), landed(i, c), nf + i, sibling)

        def own(i):
            a, p = plan_own[i]
            src0, n, _ = specs[a].pieces[p]
            return remote(ins[a].at[:, pl.ds(src0, n), :], block(a, p, j, slice(None)), 2 * nf + i, sibling)

        def zero(i):
            a, z, l = plan_zero[i]
            r0, n = specs[a].zero_rows[z]
            return remote(zrefs[a].at[pl.ds(0, n), :], outs[a].at[l, pl.ds(r0, n), :],
                          2 * nf + len(plan_own) + i, sibling)

        fixed = [own(i) for i in range(len(plan_own))] + [zero(i) for i in range(len(plan_zero))]
        return c, fixed, first, arrival, passed

    def start(ins, outs, send_sems, recv_sems):
        _, fixed, first, _, _ = copies(ins, outs, send_sems, recv_sems)
        for cp in fixed:
            cp.start()
        for i in range(nf):
            first(i).start()

    def finish(ins, outs, send_sems, recv_sems):
        c, fixed, first, arrival, passed = copies(ins, outs, send_sems, recv_sems)
        for i in range(nf):
            arrival(i, c, i).wait_recv()
            passed(i).start()
        for i in range(nf):
            arrival(i, 1 - c, nf + i).wait_recv()
        for cp in fixed:
            cp.wait()
        for i in range(nf):
            first(i).wait_send()
            passed(i).wait_send()

    out_shape = tuple(jax.ShapeDtypeStruct((shards[a].shape[0], spec.full_rows, spec.cols), BF16)
                      for a, spec in enumerate(specs))
    args = tuple(shards) + tuple(zeros[a] for a in zlist)
    return CommScript(args, out_shape, n_sems, start, finish)


def _start_all_wait_all(args, out_shape, n_sems, make_copies):
    def start(ins, outs, send_sems, recv_sems):
        for cp in make_copies(ins, outs, send_sems, recv_sems):
            cp.start()

    def finish(ins, outs, send_sems, recv_sems):
        for cp in make_copies(ins, outs, send_sems, recv_sems):
            cp.wait()

    return CommScript(tuple(args), tuple(out_shape), n_sems, start, finish)


def _exchange_script(specs, grads):
    na = len(grads)

    def make_copies(ins, outs, send_sems, recv_sems):
        x, y, c = _mesh_pos()
        return [pltpu.make_async_remote_copy(
            src_ref=ins[a].at[:, _half_cols(specs[a], 1 - c)], dst_ref=outs[a], send_sem=send_sems.at[a],
            recv_sem=recv_sems.at[a], device_id=(x, y, 1 - c), device_id_type=MESH) for a in range(na)]

    out_shape = [jax.ShapeDtypeStruct((s.full_rows, s.cols // 2), F32) for s in specs]
    return _start_all_wait_all(grads, out_shape, na, make_copies)


def _scatter_script(specs, parts):
    na = len(parts)
    plan = [(a, p, k) for a in range(na) for p in range(len(specs[a].pieces)) for k in range(3)]

    def make_copies(ins, outs, send_sems, recv_sems):
        x, y, c = _mesh_pos()
        chips = _other_chips(x, y)
        copies = []
        for i, (a, p, k) in enumerate(plan):
            src0, n, dst = specs[a].pieces[p]
            pk = 2 * chips[k][0] + chips[k][1]
            copies.append(pltpu.make_async_remote_copy(
                src_ref=ins[a].at[_rows(dst(pk), n), :], dst_ref=outs[a].at[k, pl.ds(src0, n), :],
                send_sem=send_sems.at[i], recv_sem=recv_sems.at[i], device_id=(*chips[k], c), device_id_type=MESH))
        return copies

    out_shape = [jax.ShapeDtypeStruct((3, s.own_rows, s.cols // 2), BF16) for s in specs]
    return _start_all_wait_all(parts, out_shape, len(plan), make_copies)


def _chip_sum(spec, part, recv, *, name):
    ch = spec.cols // 2
    npieces = len(spec.pieces)

    def kern(recv_ref, part_ref, o_ref, own_ref, sems):
        j = 2 * lax.axis_index("x") + lax.axis_index("y")
        copies = []
        for p, (src0, n, dst) in enumerate(spec.pieces):
            copies.append(pltpu.make_async_copy(part_ref.at[_rows(dst(j), n), :], own_ref.at[pl.ds(src0, n), :],
                                                sems.at[p]))
        for cp in copies:
            cp.start()
        for cp in copies:
            cp.wait()
        o_ref[...] = ((own_ref[...].astype(F32) + recv_ref[0].astype(F32)) + recv_ref[1].astype(F32)) \
            + recv_ref[2].astype(F32)

    vm = pl.BlockSpec(memory_space=pltpu.VMEM)
    return _pcall(kern, name=name, out_shape=jax.ShapeDtypeStruct((spec.own_rows, ch), F32),
                  in_specs=[vm, HBM_SPEC], out_specs=vm,
                  scratch=[pltpu.VMEM((spec.own_rows, ch), BF16), pltpu.SemaphoreType.DMA((npieces,))],
                  vmem_mb=48)(recv, part)


def _sibling_script(sums):
    na = len(sums)

    def make_copies(ins, outs, send_sems, recv_sems):
        x, y, c = _mesh_pos()
        return [pltpu.make_async_remote_copy(
            src_ref=ins[a], dst_ref=outs[a], send_sem=send_sems.at[a], recv_sem=recv_sems.at[a],
            device_id=(x, y, 1 - c), device_id_type=MESH) for a in range(na)]

    out_shape = [jax.ShapeDtypeStruct(t.shape, t.dtype) for t in sums]
    return _start_all_wait_all(sums, out_shape, na, make_copies)


class _SemWindow:
    def __init__(self, sems, offset):
        self._sems, self._offset = sems, offset

    @property
    def at(self):
        return self

    def __getitem__(self, i):
        return self._sems.at[i + self._offset]


def _merge_scripts(*scripts):
    a_off, o_off, s_off = [0], [0], [0]
    for s in scripts:
        a_off.append(a_off[-1] + len(s.args))
        o_off.append(o_off[-1] + len(s.out_shape))
        s_off.append(s_off[-1] + s.n_sems)

    def phase(which):
        def run(ins, outs, send_sems, recv_sems):
            for n, s in enumerate(scripts):
                getattr(s, which)(ins[a_off[n]:a_off[n + 1]], outs[o_off[n]:o_off[n + 1]],
                                  _SemWindow(send_sems, s_off[n]), _SemWindow(recv_sems, s_off[n]))
        return run

    return CommScript(sum((tuple(s.args) for s in scripts), ()), sum((tuple(s.out_shape) for s in scripts), ()),
                      s_off[-1], phase("start"), phase("finish"))


class _GradReducer:
    def __init__(self, layer, names, grads, c_idx):
        self.specs = tuple(WEIGHT_ROWS[nm] for nm in names)
        self.grads, self.c_idx = tuple(grads), c_idx
        self.names = [f"{nm}{layer}" for nm in names]

    def exchange(self):
        return _exchange_script(self.specs, self.grads)

    def scatter(self, theirs):
        self.parts = tuple(_pair_sum(g, th, self.c_idx, name=f"pair_sum_{nm}")
                           for g, th, nm in zip(self.grads, theirs, self.names))
        return _scatter_script(self.specs, self.parts)

    def sibling(self, recv):
        self.sums = tuple(_chip_sum(s, p, r, name=f"chip_sum_{nm}")
                          for s, p, r, nm in zip(self.specs, self.parts, recv, self.names))
        return _sibling_script(self.sums)

    def done(self, others):
        return list(zip(self.sums, others))


def _allgather_small(block, *, name):
    m_per, n = block.shape

    def body(x_ref, out_ref, send_sems, recv_sems, local_sem):
        x, y, c = _mesh_pos()
        me, sibling = (x, y, c), (x, y, 1 - c)
        chips = _other_chips(x, y)

        def rows(px, py, pc):
            return out_ref.at[4 * px + 2 * py + pc]

        def copy(k, blk, to, src=None):
            return pltpu.make_async_remote_copy(
                src_ref=rows(*blk) if src is None else src, dst_ref=rows(*blk), send_sem=send_sems.at[k],
                recv_sem=recv_sems.at[k], device_id=to, device_id_type=MESH)

        mine = pltpu.make_async_copy(x_ref, rows(*me), local_sem)
        mine.start()
        first = [copy(0, me, sibling, src=x_ref)]
        first += [copy(1 + k, me, (*chip, c), src=x_ref) for k, chip in enumerate(chips)]
        for cp in first:
            cp.start()
        passed = [copy(4 + k, (*chip, c), sibling) for k, chip in enumerate(chips)]
        for k, chip in enumerate(chips):
            copy(1 + k, (*chip, c), me).wait_recv()
            passed[k].start()
        copy(0, sibling, me).wait_recv()
        for k, chip in enumerate(chips):
            copy(4 + k, (*chip, 1 - c), me).wait_recv()
        for cp in first + passed:
            cp.wait_send()
        mine.wait()

    vm = pl.BlockSpec(memory_space=pltpu.VMEM)
    return _pcall(body, name=name, out_shape=jax.ShapeDtypeStruct((N_DEV, m_per, n), block.dtype),
                  in_specs=[vm], out_specs=vm,
                  scratch=[pltpu.SemaphoreType.DMA((7,)), pltpu.SemaphoreType.DMA((7,)), pltpu.SemaphoreType.DMA],
                  vmem_mb=48)(block)


def _rope_tables(positions):
    half = ROPE // 2
    inv_freq = ROPE_THETA ** (-jnp.arange(half, dtype=F32) / half)
    ang = positions.astype(F32)[:, None] * inv_freq
    cos, sin = jnp.cos(ang), jnp.sin(ang)
    S = positions.shape[0]
    cos_t = jnp.concatenate([cos, cos, jnp.ones((S, 64), F32)], axis=1)
    sin_t = jnp.concatenate([-sin, sin, jnp.zeros((S, 64), F32)], axis=1)
    return cos_t, sin_t


def _decode_conv(bits):
    rows = bits.reshape(DEPTH, N_CHIPS, 16, 256)[:, :, :3, :]
    conv = lax.bitcast_convert_type(rows.reshape(DEPTH, N_CHIPS, 3, 128, 2), F32)
    return jnp.transpose(conv, (0, 2, 1, 3)).reshape(DEPTH, 3, 512)


def _local_step(x, positions, target, emb_g, emb_b, w_in_t0, rest0, weights1, q_g, kv_g, w_pool, pool_scale,
                b_out, ln_g, ln_b, c_idx=None):
    cos_t, sin_t = _rope_tables(positions)
    if isinstance(w_in_t0, CommScript):
        (h, hb), (landed,) = _ln_fwd(x, emb_g, emb_b, name="emb_ln", comm=w_in_t0)
        w_in_t0 = landed[0]
    else:
        h, hb = _ln_fwd(x, emb_g, emb_b, name="emb_ln")
    weights = [None, weights1]
    saved = []
    for l in range(DEPTH):
        if l == 0 and isinstance(rest0, CommScript):
            proj, landed = _matmul(hb, w_in_t0, "nt", name="in_proj0", tm=1024, tn=1024, tk=2048, vmem_mb=56,
                                   comm=rest0)
            weights[0] = (w_in_t0,) + tuple(a[0] for a in landed[:3])
            conv_w = _decode_conv(landed[3])
        else:
            if l == 0:
                weights[0] = (w_in_t0,) + tuple(rest0[:3])
                conv_w = rest0[3]
            proj = _matmul(hb, weights[l][0], "nt", name=f"in_proj{l}", tm=1024, tn=1024, tk=2048, vmem_mb=56)
        w_in_t, w_out, w_uq_t, w_ukv_t = weights[l]
        qc, kc, v, vt, qn, kvn = _mla_qkv(proj, cos_t, sin_t, q_g[l], kv_g[l], w_uq_t, w_ukv_t, name=f"mla_qkv{l}")
        nxt = weights[l + 1] if l + 1 < DEPTH else None
        if isinstance(nxt, CommScript):
            (o, lse2), landed = _flash_fwd(qc, kc, vt, name=f"flash_fwd{l}", comm=nxt)
            weights[l + 1] = tuple(a[0] for a in landed)
        else:
            o, lse2 = _flash_fwd(qc, kc, vt, name=f"flash_fwd{l}")
        mix = _mixer_fwd(proj, o, w_pool[l], pool_scale[l], conv_w[l], name=f"mixer_fwd{l}")
        if l == DEPTH - 1:
            r = _outproj_residual(mix, w_out, h, b_out[l], name=f"out_proj{l}")
            saved.append((hb, proj, qc, kc, v, qn, kvn, o, lse2, mix, r))
        else:
            h_next, hb_next, r = _outproj_ln(mix, w_out, h, b_out[l], ln_g[l], ln_b[l], name=f"out_proj_ln{l}")
            saved.append((hb, proj, qc, kc, v, qn, kvn, o, lse2, mix, r))
            h, hb = h_next, hb_next

    small = [None] * DEPTH
    big = [None] * DEPTH
    above = scatter_above = None
    for l in reversed(range(DEPTH)):
        w_in_t, w_out, w_uq_t, w_ukv_t = weights[l]
        hb_in, proj, qc, kc, v, qn, kvn, o, lse2, mix, r = saved[l]
        if l == DEPTH - 1:
            loss_acc, dr, drb, d_ln_g, d_ln_b, d_b_out = _loss_ln_bwd(target, r, ln_g[l], ln_b[l], name="loss_ln_bwd")
        else:
            dr, drb, d_ln_g, d_ln_b, d_b_out = _ln_bwd(dh, r, ln_g[l], name=f"ln_bwd{l}")
        dmix = _matmul(drb, w_out, "nt", name=f"dmix{l}", tm=1024, tn=1024, tk=2048, vmem_mb=56)
        d_w_out = _matmul(mix, drb, "tn", name=f"dw_out{l}", tm=1024, tn=1024, tk=2048, vmem_mb=56)
        d_mix, do, d_w_pool, d_ps, d_conv = _mixer_bwd(dmix, proj, o, w_pool[l], pool_scale[l], conv_w[l],
                                                       name=f"mixer_bwd{l}")
        delta = _attn_delta(o, do, name=f"attn_delta{l}")
        if above is not None:
            (dqb, dkvb, dkr), recv = _flash_bwd(qc, kc, v, do, lse2, delta, cos_t, sin_t, name=f"flash_bwd{l}",
                                                comm=scatter_above)
            sibling_above = above.sibling(recv)
        else:
            dqb, dkvb, dkr = _flash_bwd(qc, kc, v, do, lse2, delta, cos_t, sin_t, name=f"flash_bwd{l}")
        d_mla, d_qg, d_kvg = _mla_qkv_bwd(dqb, dkvb, dkr, proj, cos_t, sin_t, q_g[l], kv_g[l], w_uq_t, w_ukv_t,
                                          name=f"mla_qkv_bwd{l}")
        d_w_uq_t = _matmul(dqb, qn, "tn", name=f"dw_uq{l}", tm=2048, tn=512, tk=2048, vmem_mb=56)
        d_w_ukv_t = _matmul(dkvb, kvn, "tn", name=f"dw_ukv{l}", tm=2048, tn=256, tk=2048, vmem_mb=56)
        small[l] = dict(q_g=d_qg[0], kv_g=d_kvg[0], w_pool=d_w_pool, pool_scale=d_ps[0], conv_w=d_conv,
                        b_out=d_b_out[0], ln_g=d_ln_g[0], ln_b=d_ln_b[0])
        rest = (d_w_out, d_w_uq_t, d_w_ukv_t)
        if c_idx is None:
            d_w_in_t = _dproj_t_times_h(d_mla, d_mix, hb_in, name=f"dw_in{l}")
            dh = _dproj_times_w(d_mla, d_mix, w_in_t, dr, ALPHA, name=f"dh{l}")
            big[l] = (d_w_in_t,) + rest
        elif l > 0:
            d_w_in_t = _dproj_t_times_h(d_mla, d_mix, hb_in, name=f"dw_in{l}")
            above = _GradReducer(l, SHARDED_NAMES, (d_w_in_t,) + rest, c_idx)
            dh, theirs = _dproj_times_w(d_mla, d_mix, w_in_t, dr, ALPHA, name=f"dh{l}", comm=above.exchange())
            scatter_above = above.scatter(theirs)
        else:
            red_rest = _GradReducer(l, SHARDED_NAMES[1:], rest, c_idx)
            d_w_in_t, landed = _dproj_t_times_h(d_mla, d_mix, hb_in, name=f"dw_in{l}",
                                                comm=_merge_scripts(sibling_above, red_rest.exchange()))
            big[l + 1] = above.done(landed[:len(SHARDED)])
            red_in = _GradReducer(l, SHARDED_NAMES[:1], (d_w_in_t,), c_idx)
            landed = _run_comm(_merge_scripts(red_in.exchange(), red_rest.scatter(landed[len(SHARDED):])),
                               name="exchange_w_in0")
            sibling_rest = red_rest.sibling(landed[1:])
            dh, landed = _dproj_times_w(d_mla, d_mix, w_in_t, dr, ALPHA, name=f"dh{l}",
                                        comm=_merge_scripts(red_in.scatter(landed[:1]), sibling_rest))
            recv_in, others_rest = landed[:1], landed[1:]
    grad_x, _, d_emb_g, d_emb_b, _ = _ln_bwd(dh, x, emb_g, name="emb_ln_bwd")
    if c_idx is not None:
        others_in = _run_comm(red_in.sibling(recv_in), name="send_to_sibling0")
        big[0] = red_in.done(others_in) + red_rest.done(others_rest)
    return loss_acc[0, 0], grad_x, d_emb_g[0], d_emb_b[0], small, big


SMALL_ORDER = ("emb_ln_g", "emb_ln_b", "q_norm_g", "kv_norm_g", "w_pool", "pool_scale", "b_out", "ln_g", "ln_b")


def _pack_small(arrs, extra_rows):
    flat = jnp.concatenate([a.reshape(-1) for a in arrs])
    rows = flat.shape[0] // LANE
    total = -(-(rows + extra_rows) // 8) * 8
    return jnp.pad(flat, (0, total * LANE - flat.shape[0])).reshape(total, LANE)


def kernel(x, positions, emb_ln_g, emb_ln_b, w_in, q_norm_g, kv_norm_g, w_uq, w_ukv, w_pool, pool_scale, conv_w, w_out, b_out, ln_g, ln_b, loss_target, m_emb_ln_g, m_emb_ln_b, m_w_in, m_q_norm_g, m_kv_norm_g, m_w_uq, m_w_ukv, m_w_pool, m_pool_scale, m_conv_w, m_w_out, m_b_out, m_ln_g, m_ln_b, v_emb_ln_g, v_emb_ln_b, v_w_in, v_q_norm_g, v_kv_norm_g, v_w_uq, v_w_ukv, v_w_pool, v_pool_scale, v_conv_w, v_w_out, v_b_out, v_ln_g, v_ln_b):
    xi, yi, ci = lax.axis_index("x"), lax.axis_index("y"), lax.axis_index("c")
    chip = 2 * xi + yi
    c_idx = ci.reshape(1).astype(jnp.int32)

    def t(a):
        return jnp.swapaxes(a, 1, 2)

    conv_bits = lax.bitcast_convert_type(conv_w.reshape(DEPTH, 3 * 128), BF16).reshape(DEPTH, 3, 256)
    conv_bits = jnp.pad(conv_bits, ((0, 0), (0, 13), (0, 0)))
    own = (t(w_in).astype(BF16), w_out.astype(BF16), t(w_uq).astype(BF16), t(w_ukv).astype(BF16))
    zeros = (jnp.zeros((GAP, D_MODEL), BF16), None, jnp.zeros((64, Q_LORA), BF16), None)
    gather_in0 = _allgather_script((W_IN,), (own[0][0:1],), zeros[:1])
    gather0 = _allgather_script(SHARDED[1:] + (W_CONV,), tuple(a[0:1] for a in own[1:]) + (conv_bits,),
                                zeros[1:] + (None,))
    gather1 = _allgather_script(SHARDED, tuple(a[1:2] for a in own), zeros)

    loss_part, grad_x, d_emb_g, d_emb_b, grads, reduced = _local_step(
        x[0], positions[0], loss_target[0], emb_ln_g, emb_ln_b, gather_in0, gather0, gather1, q_norm_g, kv_norm_g,
        w_pool, pool_scale, b_out, ln_g, ln_b, c_idx)

    small_g = [d_emb_g, d_emb_b,
               jnp.stack([grads[l]["q_g"] for l in range(DEPTH)]), jnp.stack([grads[l]["kv_g"] for l in range(DEPTH)]),
               jnp.stack([grads[l]["w_pool"] for l in range(DEPTH)]),
               jnp.stack([grads[l]["pool_scale"] for l in range(DEPTH)]),
               jnp.stack([grads[l]["b_out"] for l in range(DEPTH)]), jnp.stack([grads[l]["ln_g"] for l in range(DEPTH)]),
               jnp.stack([grads[l]["ln_b"] for l in range(DEPTH)]),
               jnp.stack([grads[l]["conv_w"] for l in range(DEPTH)]),
               jnp.pad(loss_part.reshape(1), (0, LANE - 1))]
    def rows(a):
        return a.reshape(1, -1) if a.ndim == 1 else a

    small_wmv = [tuple(rows(a) for a in wmv) for wmv in (
        (emb_ln_g, m_emb_ln_g, v_emb_ln_g), (emb_ln_b, m_emb_ln_b, v_emb_ln_b),
        (q_norm_g, m_q_norm_g, v_q_norm_g), (kv_norm_g, m_kv_norm_g, v_kv_norm_g), (w_pool, m_w_pool, v_w_pool),
        (pool_scale, m_pool_scale, v_pool_scale), (b_out, m_b_out, v_b_out), (ln_g, m_ln_g, v_ln_g),
        (ln_b, m_ln_b, v_ln_b))]
    packed_g = _pack_small(small_g, 0)
    gathered = _allgather_small(packed_g, name="allgather_small")
    g_tot, small_upd = _small_sum_adamw(gathered, small_wmv, name="small_sum_adamw")
    off = sum(w.size for w, _, _ in small_wmv)
    flat_tot = g_tot.reshape(-1)
    conv_tot = flat_tot[off:off + DEPTH * 3 * 512].reshape(DEPTH, 3, 512)
    loss = flat_tot[off + DEPTH * 3 * 512]
    g_conv = lax.dynamic_slice_in_dim(conv_tot, chip * 128, 128, axis=2)

    def halves(a):
        return [reduced[l][a] for l in range(DEPTH)]

    def whole(a):
        return jnp.stack([jnp.where(ci == 0, jnp.concatenate([mine, oth], axis=1),
                                    jnp.concatenate([oth, mine], axis=1)) for mine, oth in halves(a)])

    upd = {}
    upd["w_in"] = tuple(t(o) for o in _adamw_halves(t(w_in), t(m_w_in), t(v_w_in), halves(0), c_idx,
                                                    name="adamw_w_in"))
    upd["w_out"] = _adamw_halves(w_out, m_w_out, v_w_out, halves(1), c_idx, name="adamw_w_out")
    g_uq, g_ukv = t(whole(2)), t(whole(3))
    upd["w_uq"] = (g_uq,) + _adamw(w_uq, g_uq, m_w_uq, v_w_uq, name="adamw_w_uq")
    upd["w_ukv"] = (g_ukv,) + _adamw(w_ukv, g_ukv, m_w_ukv, v_w_ukv, name="adamw_w_ukv")
    upd["conv_w"] = (g_conv,) + _adamw(conv_w, g_conv, m_conv_w, v_conv_w, name="adamw_conv_w")
    for nm, res in zip(SMALL_ORDER, small_upd):
        upd[nm] = tuple(a.reshape(-1) for a in res) if nm in ("emb_ln_g", "emb_ln_b") else res

    order = ("emb_ln_g", "emb_ln_b", "w_in", "q_norm_g", "kv_norm_g", "w_uq", "w_ukv", "w_pool", "pool_scale",
             "conv_w", "w_out", "b_out", "ln_g", "ln_b")
    outs = [loss, grad_x[None]]
    for field in range(4):
        outs += [upd[nm][field] for nm in order]
    return tuple(outs)
```

```python
import collections

import jax
import jax.numpy as jnp
from jax import lax
from jax.experimental import pallas as pl
from jax.experimental.pallas import tpu as pltpu

F32 = jnp.float32
BF16 = jnp.bfloat16
MESH = pl.DeviceIdType.MESH

D_MODEL = 2048
DEPTH = 2
N_HEADS = 8
NOPE = 128
ROPE = 64
Q_LORA = 512
KV_LORA = 256
D_MLA = 1024
POOL_WINDOWS = (2, 4, 8, 16)
D_IN_PROJ = 4928
LN_EPS = 1e-5
RMS_EPS = 1e-6
ROPE_THETA = 10000.0
ALPHA = (2 * DEPTH) ** 0.25
SCALE = (NOPE + ROPE) ** -0.5
LOG2E = 1.4426950408889634
SCALE_LOG2E = SCALE * LOG2E
ADAM_LR = 0.001
ADAM_B1 = 0.9
ADAM_B2 = 0.999
ADAM_EPS = 1e-08
ADAM_WD = 0.01
ADAM_STEP = 10

NP = 5120
GAP_AT = 832
GAP = NP - D_IN_PROJ
W_MLA = 1024
W_MIX = NP - W_MLA
HALO = 16
LANE = 128
N_CHIPS = 4
N_DEV = 8
TQ = 512
FWD_GROUP = 4

NN = (((1,), (0,)), ((), ()))
NT = (((1,), (1,)), ((), ()))
TN = (((0,), (0,)), ((), ()))


CommScript = collections.namedtuple("CommScript", "args out_shape n_sems start finish")
HBM_SPEC = pl.BlockSpec(memory_space=pl.ANY)


def _pcall(kern, *, name, out_shape, grid=None, in_specs=None, out_specs=None, scratch=(), dims=None,
           vmem_mb=None, comm=None):
    cp = {}
    if dims is not None:
        cp["dimension_semantics"] = dims if comm is None else ("arbitrary",) * len(dims)
    if vmem_mb is not None:
        cp["vmem_limit_bytes"] = vmem_mb << 20
    if comm is None:
        args = dict(name=name, out_shape=out_shape, scratch_shapes=list(scratch),
                    compiler_params=pltpu.CompilerParams(**cp))
        if grid is not None:
            args["grid"] = grid
        if in_specs is not None:
            args["in_specs"] = in_specs
        if out_specs is not None:
            args["out_specs"] = out_specs
        return pl.pallas_call(kern, **args)

    single = not isinstance(out_shape, (tuple, list))
    own_out = (out_shape,) if single else tuple(out_shape)
    own_out_specs = (out_specs,) if single else tuple(out_specs)
    n_in, n_out, n_scr = len(in_specs), len(own_out), len(scratch)
    na, no = len(comm.args), len(comm.out_shape)

    def at(end):
        cond = None
        for d, n in enumerate(grid):
            here = pl.program_id(d) == (n - 1 if end else 0)
            cond = here if cond is None else jnp.logical_and(cond, here)
        return cond

    def wrapped(*refs):
        own_in, c_in = refs[:n_in], refs[n_in:n_in + na]
        o0 = n_in + na
        own_o, c_out = refs[o0:o0 + n_out], refs[o0 + n_out:o0 + n_out + no]
        s0 = o0 + n_out + no
        own_s, (send_sems, recv_sems) = refs[s0:s0 + n_scr], refs[s0 + n_scr:]

        @pl.when(at(False))
        def _():
            comm.start(c_in, c_out, send_sems, recv_sems)

        kern(*own_in, *own_o, *own_s)

        @pl.when(at(True))
        def _():
            comm.finish(c_in, c_out, send_sems, recv_sems)

    call = pl.pallas_call(
        wrapped, name=name, out_shape=own_out + tuple(comm.out_shape), grid=grid,
        in_specs=list(in_specs) + [HBM_SPEC] * na, out_specs=own_out_specs + (HBM_SPEC,) * no,
        scratch_shapes=list(scratch) + [pltpu.SemaphoreType.DMA((comm.n_sems,)),
                                        pltpu.SemaphoreType.DMA((comm.n_sems,))],
        compiler_params=pltpu.CompilerParams(**cp))

    def run(*args):
        res = call(*args, *comm.args)
        own = res[0] if single else tuple(res[:n_out])
        return own, tuple(res[n_out:])

    return run


def _run_comm(script, *, name):
    na, no = len(script.args), len(script.out_shape)

    def body(*refs):
        ins, outs = refs[:na], refs[na:na + no]
        send_sems, recv_sems = refs[na + no:]
        script.start(ins, outs, send_sems, recv_sems)
        script.finish(ins, outs, send_sems, recv_sems)

    return pl.pallas_call(
        body, name=name, out_shape=tuple(script.out_shape), in_specs=[HBM_SPEC] * na, out_specs=(HBM_SPEC,) * no,
        scratch_shapes=[pltpu.SemaphoreType.DMA((script.n_sems,)), pltpu.SemaphoreType.DMA((script.n_sems,))])(
            *script.args)


def _sigmoid(g):
    return 1.0 / (1.0 + jnp.exp(-g))


def _silu_and_grad(g):
    sig = _sigmoid(g)
    return g * sig, sig * (1.0 + g * (1.0 - sig))


def _matmul(a, b, mode, *, name, tm, tn, tk, out_dtype=F32, vmem_mb=48, comm=None):
    if mode == "nn":
        (M, K), N = a.shape, b.shape[1]
    elif mode == "nt":
        (M, K), N = a.shape, b.shape[0]
    else:
        (K, M), N = a.shape, b.shape[1]
    tm, tn, tk = min(tm, M), min(tn, N), min(tk, K)
    assert M % tm == 0 and N % tn == 0 and K % tk == 0, (name, M, N, K)
    nk = K // tk
    dn = {"nn": NN, "nt": NT, "tn": TN}[mode]
    if mode == "tn":
        a_spec = pl.BlockSpec((tk, tm), lambda i, j, k: (k, i))
    else:
        a_spec = pl.BlockSpec((tm, tk), lambda i, j, k: (i, k))
    if mode == "nt":
        b_spec = pl.BlockSpec((tn, tk), lambda i, j, k: (j, k))
    else:
        b_spec = pl.BlockSpec((tk, tn), lambda i, j, k: (k, j))
    o_spec = pl.BlockSpec((tm, tn), lambda i, j, k: (i, j))

    def kern(a_ref, b_ref, o_ref, *rest):
        part = lax.dot_general(a_ref[...].astype(BF16), b_ref[...].astype(BF16), dn,
                               preferred_element_type=F32)
        if nk == 1:
            o_ref[...] = part.astype(out_dtype)
        else:
            acc_ref = rest[0]
            k = pl.program_id(2)

            @pl.when(k == 0)
            def _():
                acc_ref[...] = part

            @pl.when(k > 0)
            def _():
                acc_ref[...] += part

            @pl.when(k == nk - 1)
            def _():
                o_ref[...] = acc_ref[...].astype(out_dtype)

    scratch = [pltpu.VMEM((tm, tn), F32)] if nk > 1 else []
    return _pcall(kern, name=name, out_shape=jax.ShapeDtypeStruct((M, N), out_dtype),
                  grid=(M // tm, N // tn, nk), in_specs=[a_spec, b_spec], out_specs=o_spec, scratch=scratch,
                  dims=("parallel", "parallel", "arbitrary"), vmem_mb=vmem_mb, comm=comm)(a, b)


def _dproj_times_w(d_mla, d_mix, wt, add, add_scale, *, name, comm=None):
    S = d_mla.shape[0]
    Dm = wt.shape[1]
    tm, tn, tk = min(1024, S), 1024, 2048
    nk = 1 + W_MIX // tk

    def kern(a1_ref, a2_ref, b1_ref, b2_ref, add_ref, o_ref, acc_ref):
        k = pl.program_id(2)

        @pl.when(k == 0)
        def _():
            acc_ref[...] = jnp.dot(a1_ref[...], b1_ref[...], preferred_element_type=F32)

        @pl.when(k > 0)
        def _():
            acc_ref[...] += jnp.dot(a2_ref[...], b2_ref[...], preferred_element_type=F32)

        @pl.when(k == nk - 1)
        def _():
            o_ref[...] = add_scale * add_ref[...] + acc_ref[...]

    o_spec = pl.BlockSpec((tm, tn), lambda i, j, k: (i, j))
    b2_spec = pl.BlockSpec((pl.Element(tk), pl.Element(tn)),
                           lambda i, j, k: (pl.multiple_of(W_MLA + tk * jnp.maximum(k - 1, 0), W_MLA),
                                            pl.multiple_of(j * tn, tn)))
    return _pcall(kern, name=name, out_shape=jax.ShapeDtypeStruct((S, Dm), F32), grid=(S // tm, Dm // tn, nk),
                  in_specs=[pl.BlockSpec((tm, W_MLA), lambda i, j, k: (i, 0)),
                            pl.BlockSpec((tm, tk), lambda i, j, k: (i, jnp.maximum(k - 1, 0))),
                            pl.BlockSpec((W_MLA, tn), lambda i, j, k: (0, j)), b2_spec, o_spec],
                  out_specs=o_spec, scratch=[pltpu.VMEM((tm, tn), F32)],
                  dims=("parallel", "parallel", "arbitrary"), vmem_mb=56, comm=comm)(d_mla, d_mix, wt, wt, add)


def _dproj_t_times_h(d_mla, d_mix, h, *, name, comm=None):
    S, Dm = h.shape
    tm, tn, tk = W_MLA, 1024, min(2048, S)
    nk = S // tk

    def kern(a1_ref, a2_ref, b_ref, o_ref, acc_ref):
        i = pl.program_id(0)
        k = pl.program_id(2)
        b = b_ref[...].astype(BF16)

        def accumulate(part):
            @pl.when(k == 0)
            def _():
                acc_ref[...] = part

            @pl.when(k > 0)
            def _():
                acc_ref[...] += part

        @pl.when(i == 0)
        def _():
            accumulate(lax.dot_general(a1_ref[...], b, TN, preferred_element_type=F32))

        @pl.when(i > 0)
        def _():
            accumulate(lax.dot_general(a2_ref[...], b, TN, preferred_element_type=F32))

        @pl.when(k == nk - 1)
        def _():
            o_ref[...] = acc_ref[...]

    return _pcall(kern, name=name, out_shape=jax.ShapeDtypeStruct((NP, Dm), F32), grid=(NP // tm, Dm // tn, nk),
                  in_specs=[pl.BlockSpec((tk, tm), lambda i, j, k: (jnp.where(i == 0, k, nk - 1), 0)),
                            pl.BlockSpec((tk, tm), lambda i, j, k: (jnp.where(i == 0, 0, k), jnp.maximum(i - 1, 0))),
                            pl.BlockSpec((tk, tn), lambda i, j, k: (k, j))],
                  out_specs=pl.BlockSpec((tm, tn), lambda i, j, k: (i, j)), scratch=[pltpu.VMEM((tm, tn), F32)],
                  dims=("parallel", "parallel", "arbitrary"), vmem_mb=48, comm=comm)(d_mla, d_mix, h)


def _ln_fwd(x, g, b, *, name, comm=None):
    S, Dm = x.shape
    tm = min(512, S)

    def kern(x_ref, g_ref, b_ref, y_ref, yb_ref):
        xf = x_ref[...]
        mu = jnp.mean(xf, axis=-1, keepdims=True)
        xc = xf - mu
        var = jnp.mean(xc * xc, axis=-1, keepdims=True)
        y = xc * lax.rsqrt(var + LN_EPS) * g_ref[...] + b_ref[...]
        y_ref[...] = y
        yb_ref[...] = y.astype(BF16)

    row = pl.BlockSpec((tm, Dm), lambda i: (i, 0))
    vec = pl.BlockSpec((1, Dm), lambda i: (0, 0))
    return _pcall(kern, name=name,
                  out_shape=(jax.ShapeDtypeStruct((S, Dm), F32), jax.ShapeDtypeStruct((S, Dm), BF16)),
                  grid=(S // tm,), in_specs=[row, vec, vec], out_specs=(row, row), dims=("parallel",), vmem_mb=48,
                  comm=comm)(
                      x, g.reshape(1, Dm), b.reshape(1, Dm))


def _ln_bwd(dy, r, g, *, name, comm=None):
    S, Dm = r.shape
    tm = min(512, S)

    def kern(dy_ref, r_ref, g_ref, dr_ref, drb_ref, dg_ref, db_ref, ds_ref):
        @pl.when(pl.program_id(0) == 0)
        def _():
            dg_ref[...] = jnp.zeros_like(dg_ref)
            db_ref[...] = jnp.zeros_like(db_ref)
            ds_ref[...] = jnp.zeros_like(ds_ref)

        rf = r_ref[...]
        dyf = dy_ref[...]
        mu = jnp.mean(rf, axis=-1, keepdims=True)
        xc = rf - mu
        var = jnp.mean(xc * xc, axis=-1, keepdims=True)
        rstd = lax.rsqrt(var + LN_EPS)
        xhat = xc * rstd
        dxh = dyf * g_ref[...]
        c1 = jnp.mean(dxh, axis=-1, keepdims=True)
        c2 = jnp.mean(dxh * xhat, axis=-1, keepdims=True)
        dr = rstd * (dxh - c1 - xhat * c2)
        dr_ref[...] = dr
        drb_ref[...] = dr.astype(BF16)
        dg_ref[...] += jnp.sum(dyf * xhat, axis=0, keepdims=True)
        db_ref[...] += jnp.sum(dyf, axis=0, keepdims=True)
        ds_ref[...] += jnp.sum(dr, axis=0, keepdims=True)

    row = pl.BlockSpec((tm, Dm), lambda i: (i, 0))
    vec = pl.BlockSpec((1, Dm), lambda i: (0, 0))
    vshape = jax.ShapeDtypeStruct((1, Dm), F32)
    return _pcall(kern, name=name,
                  out_shape=(jax.ShapeDtypeStruct((S, Dm), F32), jax.ShapeDtypeStruct((S, Dm), BF16),
                             vshape, vshape, vshape),
                  grid=(S // tm,), in_specs=[row, row, vec], out_specs=(row, row, vec, vec, vec),
                  dims=("arbitrary",), vmem_mb=48, comm=comm)(dy, r, g.reshape(1, Dm))


def _loss_ln_bwd(target, r, g, b, *, name):
    S, Dm = r.shape
    tm = min(512, S)

    def kern(t_ref, r_ref, g_ref, b_ref, l_ref, dr_ref, drb_ref, dg_ref, db_ref, ds_ref):
        @pl.when(pl.program_id(0) == 0)
        def _():
            l_ref[...] = jnp.zeros_like(l_ref)
            dg_ref[...] = jnp.zeros_like(dg_ref)
            db_ref[...] = jnp.zeros_like(db_ref)
            ds_ref[...] = jnp.zeros_like(ds_ref)

        rf = r_ref[...]
        mu = jnp.mean(rf, axis=-1, keepdims=True)
        xc = rf - mu
        var = jnp.mean(xc * xc, axis=-1, keepdims=True)
        rstd = lax.rsqrt(var + LN_EPS)
        xhat = xc * rstd
        e = (xhat * g_ref[...] + b_ref[...]) - t_ref[...]
        dyf = e / float(Dm)
        per_row = jnp.mean(e * e, axis=-1, keepdims=True)
        l_ref[...] += 0.5 * jnp.sum(per_row, axis=0, keepdims=True)
        dxh = dyf * g_ref[...]
        c1 = jnp.mean(dxh, axis=-1, keepdims=True)
        c2 = jnp.mean(dxh * xhat, axis=-1, keepdims=True)
        dr = rstd * (dxh - c1 - xhat * c2)
        dr_ref[...] = dr
        drb_ref[...] = dr.astype(BF16)
        dg_ref[...] += jnp.sum(dyf * xhat, axis=0, keepdims=True)
        db_ref[...] += jnp.sum(dyf, axis=0, keepdims=True)
        ds_ref[...] += jnp.sum(dr, axis=0, keepdims=True)

    row = pl.BlockSpec((tm, Dm), lambda i: (i, 0))
    vec = pl.BlockSpec((1, Dm), lambda i: (0, 0))
    acc = pl.BlockSpec((8, LANE), lambda i: (0, 0))
    vshape = jax.ShapeDtypeStruct((1, Dm), F32)
    return _pcall(kern, name=name,
                  out_shape=(jax.ShapeDtypeStruct((8, LANE), F32), jax.ShapeDtypeStruct((S, Dm), F32),
                             jax.ShapeDtypeStruct((S, Dm), BF16), vshape, vshape, vshape),
                  grid=(S // tm,), in_specs=[row, row, vec, vec], out_specs=(acc, row, row, vec, vec, vec),
                  dims=("arbitrary",), vmem_mb=56)(target, r, g.reshape(1, Dm), b.reshape(1, Dm))


def _rot_sum(t):
    return pltpu.roll(t, 32, 1) + pltpu.roll(t, 96, 1)


def _mla_qkv(proj, cos_t, sin_t, qg, kvg, wuq_t, wukv_t, *, name):
    S = proj.shape[0]
    tm = min(256, S)

    def kern(ql_ref, kvl_ref, kr_ref, cos_ref, sin_ref, qg_ref, kvg_ref, wuq_ref, wukv_ref,
             qc_ref, kc_ref, v_ref, vt_ref, qn_ref, kvn_ref):
        cosv = cos_ref[...]
        sinv = sin_ref[...]

        def rope(t):
            return t * cosv + _rot_sum(t) * sinv

        ql = ql_ref[...]
        qn = (ql * lax.rsqrt(jnp.mean(ql * ql, axis=-1, keepdims=True) + RMS_EPS) * qg_ref[...]).astype(BF16)
        kvl = kvl_ref[...]
        kvn = (kvl * lax.rsqrt(jnp.mean(kvl * kvl, axis=-1, keepdims=True) + RMS_EPS) * kvg_ref[...]).astype(BF16)
        qn_ref[...] = qn
        kvn_ref[...] = kvn
        q = lax.dot_general(qn, wuq_ref[...], NT, preferred_element_type=F32)
        kv = lax.dot_general(kvn, wukv_ref[...], NT, preferred_element_type=F32)
        kr = rope(kr_ref[...]).astype(BF16)
        for h in range(N_HEADS):
            c0 = 256 * h
            qc_ref[:, c0:c0 + 128] = q[:, c0:c0 + 128].astype(BF16)
            qc_ref[:, c0 + 128:c0 + 256] = rope(q[:, c0 + 128:c0 + 256]).astype(BF16)
            kc_ref[:, c0:c0 + 128] = kv[:, c0:c0 + 128].astype(BF16)
            kc_ref[:, c0 + 128:c0 + 256] = kr
            vh = kv[:, c0 + 128:c0 + 256]
            v_ref[:, 128 * h:128 * h + 128] = vh.astype(BF16)
            vt_ref[h] = jnp.transpose(vh).astype(BF16)

    def row(w, blk):
        return pl.BlockSpec((tm, w), lambda i: (i, blk))

    def full(shape):
        return pl.BlockSpec(shape, lambda i: (0,) * len(shape))

    t = min(TQ, S)
    per = t // tm
    vt_spec = pl.BlockSpec((N_HEADS, None, 128, tm), lambda i: (0, i // per, 0, i % per))
    outs = (jax.ShapeDtypeStruct((S, 2048), BF16), jax.ShapeDtypeStruct((S, 2048), BF16),
            jax.ShapeDtypeStruct((S, 1024), BF16), jax.ShapeDtypeStruct((N_HEADS, S // t, 128, t), BF16),
            jax.ShapeDtypeStruct((S, Q_LORA), BF16), jax.ShapeDtypeStruct((S, KV_LORA), BF16))
    return _pcall(kern, name=name, out_shape=outs, grid=(S // tm,),
                  in_specs=[row(512, 0), row(256, 2), row(128, 6), row(128, 0), row(128, 0),
                            full((1, Q_LORA)), full((1, KV_LORA)), full((2048, Q_LORA)), full((2048, KV_LORA))],
                  out_specs=(row(2048, 0), row(2048, 0), row(1024, 0), vt_spec, row(512, 0), row(256, 0)),
                  dims=("parallel",), vmem_mb=48)(
                      proj, proj, proj, cos_t, sin_t, qg.reshape(1, -1), kvg.reshape(1, -1), wuq_t, wukv_t)


def _mla_qkv_bwd(dqb, dkvb, dkr_heads, proj, cos_t, sin_t, qg, kvg, wuq_t, wukv_t, *, name):
    S = proj.shape[0]
    tm = min(256, S)

    def kern(dqb_ref, dkvb_ref, dkrh_ref, ql_ref, kvl_ref, cos_ref, sin_ref, qg_ref, kvg_ref, wuq_ref, wukv_ref,
             dml_ref, dqg_ref, dkvg_ref):
        @pl.when(pl.program_id(0) == 0)
        def _():
            dqg_ref[...] = jnp.zeros_like(dqg_ref)
            dkvg_ref[...] = jnp.zeros_like(dkvg_ref)

        cosv = cos_ref[...]
        sinv = sin_ref[...]

        def unrope(t):
            return t * cosv - _rot_sum(t) * sinv

        dkr = dkrh_ref[:, 0:128]
        for h in range(1, N_HEADS):
            dkr = dkr + dkrh_ref[:, 128 * h:128 * h + 128]

        def rms_bwd(x, g, dy):
            n = x.shape[-1]
            rs = lax.rsqrt(jnp.mean(x * x, axis=-1, keepdims=True) + RMS_EPS)
            dyg = dy * g
            dx = rs * dyg - x * (rs * rs * rs) * (jnp.sum(dyg * x, axis=-1, keepdims=True) / n)
            return dx, jnp.sum(dy * (x * rs), axis=0, keepdims=True)

        dqn = jnp.dot(dqb_ref[...], wuq_ref[...], preferred_element_type=F32)
        dql, dqg = rms_bwd(ql_ref[...], qg_ref[...], dqn)
        dqg_ref[...] += dqg
        dkvn = jnp.dot(dkvb_ref[...], wukv_ref[...], preferred_element_type=F32)
        dkvl, dkvg = rms_bwd(kvl_ref[...], kvg_ref[...], dkvn)
        dkvg_ref[...] += dkvg
        dml_ref[:, 0:512] = dql.astype(BF16)
        dml_ref[:, 512:768] = dkvl.astype(BF16)
        dml_ref[:, 768:896] = unrope(dkr).astype(BF16)
        dml_ref[:, 896:1024] = jnp.zeros((tm, 128), BF16)

    def row(w, blk):
        return pl.BlockSpec((tm, w), lambda i: (i, blk))

    def full(shape):
        return pl.BlockSpec(shape, lambda i: (0,) * len(shape))

    outs = (jax.ShapeDtypeStruct((S, W_MLA), BF16), jax.ShapeDtypeStruct((1, Q_LORA), F32),
            jax.ShapeDtypeStruct((1, KV_LORA), F32))
    return _pcall(kern, name=name, out_shape=outs, grid=(S // tm,),
                  in_specs=[row(2048, 0), row(2048, 0), row(1024, 0), row(512, 0), row(256, 2),
                            row(128, 0), row(128, 0), full((1, Q_LORA)), full((1, KV_LORA)),
                            full((2048, Q_LORA)), full((2048, KV_LORA))],
                  out_specs=(row(W_MLA, 0), full((1, Q_LORA)), full((1, KV_LORA))),
                  dims=("arbitrary",), vmem_mb=56)(
                      dqb, dkvb, dkr_heads, proj, proj, cos_t, sin_t, qg.reshape(1, -1), kvg.reshape(1, -1),
                      wuq_t, wukv_t)


def _flash_fwd(qc, kc, vt, *, name, comm=None):
    S = qc.shape[0]
    t = min(TQ, S)
    n = S // t

    def kern(q_ref, k_ref, vt_ref, o_ref, lse_ref, m_s, l_s, acc_s):
        qi = pl.program_id(1)
        m_s[...] = jnp.full_like(m_s, -jnp.inf)
        l_s[...] = jnp.zeros_like(l_s)
        acc_s[...] = jnp.zeros_like(acc_s)

        half = t // 2

        def scores(kb, q_lo=0, q_n=t, k_n=t):
            k0 = pl.multiple_of(kb * t, t)
            return lax.dot_general(k_ref[pl.ds(k0, k_n), :], q_ref[q_lo:q_lo + q_n, :], NT,
                                   preferred_element_type=F32)

        def update(kb, st, q_lo=0, diagonal=False):
            k_n, q_n = st.shape
            if diagonal:
                krow = lax.broadcasted_iota(jnp.int32, (k_n, q_n), 0)
                qcol = lax.broadcasted_iota(jnp.int32, (k_n, q_n), 1) + q_lo
                st = jnp.where(krow <= qcol, st, -jnp.inf)
            lanes = slice(q_lo, q_lo + q_n)
            m_prev = m_s[:, lanes]
            m_new = jnp.maximum(m_prev, jnp.max(st, axis=0, keepdims=True))
            a = jnp.exp2((m_prev - m_new) * SCALE_LOG2E)
            pt = jnp.exp2((st - m_new) * SCALE_LOG2E)
            l_s[:, lanes] = a * l_s[:, lanes] + jnp.sum(pt, axis=0, keepdims=True)
            acc_s[:, lanes] = a * acc_s[:, lanes] + jnp.dot(vt_ref[kb, :, 0:k_n], pt.astype(BF16),
                                                            preferred_element_type=F32)
            m_s[:, lanes] = m_new

        def group(kb, count, last_diagonal):
            whole = count - 1 if last_diagonal else count
            sts = [scores(kb + g) for g in range(whole)]
            if last_diagonal:
                kd = kb + count - 1
                s_lo, s_hi = scores(kd, 0, half, half), scores(kd, half, half, t)
            for g in range(whole):
                update(kb + g, sts[g])
            if last_diagonal:
                update(kd, s_lo, 0, True)
                update(kd, s_hi, half, True)

        def body(i, carry):
            group(FWD_GROUP * i, FWD_GROUP, False)
            return carry

        full = qi // FWD_GROUP
        lax.fori_loop(0, full, body, 0)
        for rem in range(FWD_GROUP):
            @pl.when(qi - FWD_GROUP * full == rem)
            def _():
                group(qi - rem, rem + 1, True)
        o_ref[...] = jnp.transpose(acc_s[...] / l_s[...])
        lse_ref[pl.ds(qi, 1), :] = m_s[...] * SCALE_LOG2E + jnp.log2(l_s[...])

    q_spec = pl.BlockSpec((t, 256), lambda h, qi: (qi, h))
    k_spec = pl.BlockSpec((S, 256), lambda h, qi: (0, h))
    vt_spec = pl.BlockSpec((None, n, 128, t), lambda h, qi: (h, 0, 0, 0))
    o_spec = pl.BlockSpec((t, 128), lambda h, qi: (qi, h))
    lse_spec = pl.BlockSpec((None, n, t), lambda h, qi: (h, 0, 0))
    return _pcall(kern, name=name,
                  out_shape=(jax.ShapeDtypeStruct((S, D_MLA), F32), jax.ShapeDtypeStruct((N_HEADS, n, t), F32)),
                  grid=(N_HEADS, n), in_specs=[q_spec, k_spec, vt_spec], out_specs=(o_spec, lse_spec),
                  scratch=[pltpu.VMEM((1, t), F32), pltpu.VMEM((1, t), F32), pltpu.VMEM((128, t), F32)],
                  dims=("parallel", "arbitrary"), vmem_mb=48, comm=comm)(qc, kc, vt)


def _attn_delta(o, do, *, name):
    S = o.shape[0]
    t = min(TQ, S)
    n = S // t

    def kern(o_ref, do_ref, dl_ref):
        i = pl.program_id(0)
        prod = o_ref[...] * do_ref[...]
        lane = lax.broadcasted_iota(jnp.int32, (t, LANE), 1)
        dmat = jnp.zeros((t, LANE), F32)
        for h in range(N_HEADS):
            dmat = jnp.where(lane == h, jnp.sum(prod[:, 128 * h:128 * h + 128], axis=1, keepdims=True), dmat)
        dmat_t = jnp.transpose(dmat)
        for h in range(N_HEADS):
            dl_ref[h, pl.ds(i, 1), :] = dmat_t[h:h + 1, :]

    row = pl.BlockSpec((t, D_MLA), lambda i: (i, 0))
    return _pcall(kern, name=name, out_shape=jax.ShapeDtypeStruct((N_HEADS, n, t), F32), grid=(n,),
                  in_specs=[row, row], out_specs=pl.BlockSpec((N_HEADS, n, t), lambda i: (0, 0, 0)),
                  dims=("arbitrary",), vmem_mb=48)(o, do)


def _flash_bwd(qc, kc, v, do, lse2, delta, cos_t, sin_t, *, name, comm=None):
    S = qc.shape[0]
    t = min(TQ, S)
    n = S // t

    def kern(q_ref, k_ref, v_ref, do_ref, lse_ref, dl_ref, cos_ref, sin_ref, dqb_ref, dkvb_ref, dkr_ref,
             dq_ref, dk_ref, dv_ref):
        ki = pl.program_id(1)

        @pl.when(ki == 0)
        def _():
            dq_ref[...] = jnp.zeros_like(dq_ref)

        dk_ref[...] = jnp.zeros_like(dk_ref)
        dv_ref[...] = jnp.zeros_like(dv_ref)

        half = t // 2

        def step(qb, q_lo=0, q_n=t, k_n=t, diagonal=False):
            q0 = pl.multiple_of(qb * t + q_lo, half)
            lanes = slice(q_lo, q_lo + q_n)
            kt = k_ref[0:k_n, :]
            qblk = q_ref[pl.ds(q0, q_n), :]
            dob = do_ref[pl.ds(q0, q_n), :].astype(BF16)
            st = lax.dot_general(kt, qblk, NT, preferred_element_type=F32)
            pt = jnp.exp2(st * SCALE_LOG2E - lse_ref[pl.ds(qb, 1), lanes])
            if diagonal:
                krow = lax.broadcasted_iota(jnp.int32, (k_n, q_n), 0)
                qcol = lax.broadcasted_iota(jnp.int32, (k_n, q_n), 1) + q_lo
                pt = jnp.where(krow <= qcol, pt, 0.0)
            dv_ref[0:k_n, :] += jnp.dot(pt.astype(BF16), dob, preferred_element_type=F32)
            dpt = lax.dot_general(v_ref[0:k_n, :], dob, NT, preferred_element_type=F32)
            dst = (pt * (dpt - dl_ref[pl.ds(qb, 1), lanes]) * SCALE).astype(BF16)
            dk_ref[0:k_n, :] += jnp.dot(dst, qblk, preferred_element_type=F32)
            dq_ref[pl.ds(q0, q_n), :] += lax.dot_general(dst, kt, TN, preferred_element_type=F32)

        step(ki, 0, half, half, True)
        step(ki, half, half, t, True)
        rest = n - 1 - ki

        def body(i, carry):
            step(ki + 1 + 2 * i)
            step(ki + 2 + 2 * i)
            return carry

        lax.fori_loop(0, rest // 2, body, 0)

        @pl.when(rest % 2 == 1)
        def _():
            step(n - 1)

        dkvb_ref[:, 0:128] = dk_ref[:, 0:128].astype(BF16)
        dkvb_ref[:, 128:256] = dv_ref[...].astype(BF16)
        dkr_ref[...] = dk_ref[:, 128:256]

        @pl.when(ki == n - 1)
        def _():
            dqb_ref[:, 0:128] = dq_ref[:, 0:128].astype(BF16)
            dqr = dq_ref[:, 128:256]
            dqb_ref[:, 128:256] = (dqr * cos_ref[...] - _rot_sum(dqr) * sin_ref[...]).astype(BF16)

    def whole(w):
        return pl.BlockSpec((S, w), lambda h, ki: (0, h))

    def krow(w):
        return pl.BlockSpec((t, w), lambda h, ki: (ki, h))

    stat = pl.BlockSpec((None, n, t), lambda h, ki: (h, 0, 0))
    table = pl.BlockSpec((S, 128), lambda h, ki: (0, 0))
    return _pcall(kern, name=name,
                  out_shape=(jax.ShapeDtypeStruct((S, 2048), BF16), jax.ShapeDtypeStruct((S, 2048), BF16),
                             jax.ShapeDtypeStruct((S, D_MLA), F32)),
                  grid=(N_HEADS, n),
                  in_specs=[whole(256), krow(256), krow(128), whole(128), stat, stat, table, table],
                  out_specs=(whole(256), krow(256), krow(128)),
                  scratch=[pltpu.VMEM((S, 256), F32), pltpu.VMEM((t, 256), F32), pltpu.VMEM((t, 128), F32)],
                  dims=("parallel", "arbitrary"), vmem_mb=56, comm=comm)(qc, kc, v, do, lse2, delta, cos_t, sin_t)


def _mixer_specs(S, tm):
    hb = tm // HALO
    last_hb = S // HALO - 1

    def main(w, blk):
        return pl.BlockSpec((tm, w), lambda i: (i, blk))

    def prev(w, blk):
        return pl.BlockSpec((HALO, w), lambda i: (jnp.maximum(i * hb - 1, 0), blk))

    def nxt(w, blk):
        return pl.BlockSpec((HALO, w), lambda i: (jnp.minimum((i + 1) * hb, last_hb), blk))

    def full(shape):
        return pl.BlockSpec(shape, lambda i: (0,) * len(shape))

    return main, prev, nxt, full


def _fill_halo(i, xp, xu, hp_ref, hch_ref, hcc_ref, pin_ref, ch_ref, cc_ref, tm):
    first = i == 0
    xp[0:HALO, :] = jnp.where(first, 0.0, hp_ref[...])
    xp[HALO:HALO + tm, :] = pin_ref[...]
    xu[0:HALO, :] = jnp.where(first, 0.0, hch_ref[...] * hcc_ref[...])
    xu[HALO:HALO + tm, :] = cc_ref[...] * ch_ref[...]


def _pooled(xp, g, t1, tm):
    w = POOL_WINDOWS[g]
    lanes = slice(128 * g, 128 * g + 128)
    x0 = xp[HALO:HALO + tm, lanes]
    acc = x0
    for k in range(1, w):
        acc = acc + xp[HALO - k:HALO - k + tm, lanes]
    return acc / jnp.minimum(t1, float(w)) - x0


def _conv_fwd(xu, cw_ref, tm):
    return (cw_ref[0:1, :] * xu[HALO - 2:HALO - 2 + tm, :] + cw_ref[1:2, :] * xu[HALO - 1:HALO - 1 + tm, :]
            + cw_ref[2:3, :] * xu[HALO:HALO + tm, :])


def _mixer_fwd(proj, o, wpool, ps, convw, *, name):
    S = proj.shape[0]
    tm = min(256, S)
    main, prev, _, full = _mixer_specs(S, tm)

    def kern(gm_ref, pin_ref, gp_ref, ch_ref, cb_ref, cc_ref, gc_ref, hp_ref, hch_ref, hcc_ref,
             o_ref, wp_ref, ps_ref, cw_ref, mix_ref, xp, xu):
        i = pl.program_id(0)
        _fill_halo(i, xp, xu, hp_ref, hch_ref, hcc_ref, pin_ref, ch_ref, cc_ref, tm)
        t1 = (i * tm + lax.broadcasted_iota(jnp.int32, (tm, 1), 0) + 1).astype(F32)
        for g in range(4):
            lanes = slice(128 * g, 128 * g + 128)
            pooled = _pooled(xp, g, t1, tm)
            z = jnp.dot(pooled.astype(BF16), wp_ref[g].astype(BF16), preferred_element_type=F32)
            gp = gp_ref[:, lanes]
            y = z * ps_ref[:, lanes] * (gp * _sigmoid(gp))
            mix_ref[:, 1024 + 128 * g:1024 + 128 * g + 128] = y.astype(BF16)
        gc = gc_ref[...]
        mix_ref[:, 1536:2048] = (cb_ref[...] * _conv_fwd(xu, cw_ref, tm) * (gc * _sigmoid(gc))).astype(BF16)
        gm = gm_ref[...]
        mix_ref[:, 0:1024] = (o_ref[...] * (gm * _sigmoid(gm))).astype(BF16)

    return _pcall(kern, name=name, out_shape=jax.ShapeDtypeStruct((S, 2048), BF16), grid=(S // tm,),
                  in_specs=[main(1024, 1), main(512, 4), main(512, 5), main(512, 6), main(512, 7), main(512, 8),
                            main(512, 9), prev(512, 4), prev(512, 6), prev(512, 8),
                            main(1024, 0), full((4, 128, 128)), full((1, 512)), full((3, 512))],
                  out_specs=main(2048, 0),
                  scratch=[pltpu.VMEM((tm + HALO, 512), F32), pltpu.VMEM((tm + HALO, 512), F32)],
                  dims=("parallel",), vmem_mb=48)(
                      proj, proj, proj, proj, proj, proj, proj, proj, proj, proj, o, wpool, ps.reshape(1, 512), convw)


def _mixer_bwd(dmix, proj, o, wpool, ps, convw, *, name):
    S = proj.shape[0]
    tm = min(256, S)
    n = S // tm
    main, prev, nxt, full = _mixer_specs(S, tm)

    def kern(dm_ref, dmn_ref, gm_ref, pin_ref, gp_ref, ch_ref, cb_ref, cc_ref, gc_ref,
             hp_ref, hch_ref, hcc_ref, gpn_ref, cbn_ref, gcn_ref, o_ref, wp_ref, ps_ref, cw_ref,
             d_ref, do_ref, dwp_ref, dps_ref, dcw_ref, xp, xu, ee, ed):
        i = pl.program_id(0)
        last = i == n - 1

        @pl.when(i == 0)
        def _():
            dwp_ref[...] = jnp.zeros_like(dwp_ref)
            dps_ref[...] = jnp.zeros_like(dps_ref)
            dcw_ref[...] = jnp.zeros_like(dcw_ref)

        _fill_halo(i, xp, xu, hp_ref, hch_ref, hcc_ref, pin_ref, ch_ref, cc_ref, tm)
        t1 = (i * tm + lax.broadcasted_iota(jnp.int32, (tm, 1), 0) + 1).astype(F32)
        t1n = ((i + 1) * tm + lax.broadcasted_iota(jnp.int32, (HALO, 1), 0) + 1).astype(F32)
        c_pin, c_gp, c_ch, c_cb, c_cc, c_gc = 1024, 1536, 2048, 2560, 3072, 3584

        for g in range(4):
            w = float(POOL_WINDOWS[g])
            lanes = slice(128 * g, 128 * g + 128)
            pooled = _pooled(xp, g, t1, tm)
            pb = pooled.astype(BF16)
            wp = wp_ref[g].astype(BF16)
            z = jnp.dot(pb, wp, preferred_element_type=F32)
            psl = ps_ref[:, lanes]
            sg, dsg = _silu_and_grad(gp_ref[:, lanes])
            dmp = dm_ref[:, 1024 + 128 * g:1024 + 128 * g + 128]
            dyp = dmp * sg
            d_ref[:, c_gp + 128 * g:c_gp + 128 * g + 128] = (dmp * (z * psl) * dsg).astype(BF16)
            dps_ref[:, lanes] += jnp.sum(dyp * z, axis=0, keepdims=True)
            dz = (dyp * psl).astype(BF16)
            dwp_ref[g] += lax.dot_general(pb, dz, TN, preferred_element_type=F32)
            dpl = lax.dot_general(dz, wp, NT, preferred_element_type=F32)
            ee[0:tm, lanes] = dpl / jnp.minimum(t1, w)
            gpn = gpn_ref[:, lanes]
            dzn = (dmn_ref[:, lanes] * (gpn * _sigmoid(gpn)) * psl).astype(BF16)
            dpn = lax.dot_general(dzn, wp, NT, preferred_element_type=F32)
            ee[tm:tm + HALO, lanes] = jnp.where(last, 0.0, dpn / jnp.minimum(t1n, w))
            acc = ee[0:tm, lanes]
            for k in range(1, POOL_WINDOWS[g]):
                acc = acc + ee[k:k + tm, lanes]
            d_ref[:, c_pin + 128 * g:c_pin + 128 * g + 128] = (acc - dpl).astype(BF16)

        yc = _conv_fwd(xu, cw_ref, tm)
        sgc, dsgc = _silu_and_grad(gc_ref[...])
        cb = cb_ref[...]
        dmc = dm_ref[:, 1536:2048]
        d_ref[:, c_gc:c_gc + 512] = (dmc * cb * yc * dsgc).astype(BF16)
        d_ref[:, c_cb:c_cb + 512] = (dmc * yc * sgc).astype(BF16)
        dyc = dmc * cb * sgc
        ed[0:tm, :] = dyc
        gcn = gcn_ref[...]
        ed[tm:tm + HALO, :] = jnp.where(last, 0.0, dmn_ref[:, 512:1024] * cbn_ref[...] * (gcn * _sigmoid(gcn)))
        dcw_ref[0:1, :] += jnp.sum(dyc * xu[HALO - 2:HALO - 2 + tm, :], axis=0, keepdims=True)
        dcw_ref[1:2, :] += jnp.sum(dyc * xu[HALO - 1:HALO - 1 + tm, :], axis=0, keepdims=True)
        dcw_ref[2:3, :] += jnp.sum(dyc * xu[HALO:HALO + tm, :], axis=0, keepdims=True)
        du = cw_ref[2:3, :] * dyc + cw_ref[1:2, :] * ed[1:1 + tm, :] + cw_ref[0:1, :] * ed[2:2 + tm, :]
        d_ref[:, c_cc:c_cc + 512] = (du * ch_ref[...]).astype(BF16)
        d_ref[:, c_ch:c_ch + 512] = (du * cc_ref[...]).astype(BF16)

        sgm, dsgm = _silu_and_grad(gm_ref[...])
        dmm = dm_ref[:, 0:1024]
        do_ref[...] = dmm * sgm
        d_ref[:, 0:1024] = (dmm * o_ref[...] * dsgm).astype(BF16)

    outs = (jax.ShapeDtypeStruct((S, W_MIX), BF16), jax.ShapeDtypeStruct((S, 1024), F32),
            jax.ShapeDtypeStruct((4, 128, 128), F32), jax.ShapeDtypeStruct((1, 512), F32),
            jax.ShapeDtypeStruct((3, 512), F32))
    scr = [pltpu.VMEM((tm + HALO, 512), F32) for _ in range(4)]
    return _pcall(kern, name=name, out_shape=outs, grid=(n,),
                  in_specs=[main(2048, 0), nxt(1024, 1),
                            main(1024, 1), main(512, 4), main(512, 5), main(512, 6), main(512, 7), main(512, 8),
                            main(512, 9), prev(512, 4), prev(512, 6), prev(512, 8),
                            nxt(512, 5), nxt(512, 7), nxt(512, 9),
                            main(1024, 0), full((4, 128, 128)), full((1, 512)), full((3, 512))],
                  out_specs=(main(W_MIX, 0), main(1024, 0), full((4, 128, 128)), full((1, 512)), full((3, 512))),
                  scratch=scr, dims=("arbitrary",), vmem_mb=56)(
                      dmix, dmix, proj, proj, proj, proj, proj, proj, proj, proj, proj, proj, proj, proj, proj,
                      o, wpool, ps.reshape(1, 512), convw)


def _outproj_residual(mix, wout, h, bout, *, name):
    S, Dm = h.shape
    tm = min(256, S)

    def kern(mix_ref, w_ref, h_ref, bo_ref, r_ref):
        out = jnp.dot(mix_ref[...], w_ref[...], preferred_element_type=F32) + bo_ref[...]
        r_ref[...] = ALPHA * h_ref[...] + out

    row = pl.BlockSpec((tm, Dm), lambda i: (i, 0))
    vec = pl.BlockSpec((1, Dm), lambda i: (0, 0))
    wsp = pl.BlockSpec((Dm, Dm), lambda i: (0, 0))
    return _pcall(kern, name=name, out_shape=jax.ShapeDtypeStruct((S, Dm), F32), grid=(S // tm,),
                  in_specs=[row, wsp, row, vec], out_specs=row, dims=("parallel",), vmem_mb=56)(
                      mix, wout, h, bout.reshape(1, Dm))


def _outproj_ln(mix, wout, h, bout, g, b, *, name):
    S, Dm = h.shape
    tm = min(256, S)

    def kern(mix_ref, w_ref, h_ref, bo_ref, g_ref, b_ref, y_ref, yb_ref, r_ref):
        out = jnp.dot(mix_ref[...], w_ref[...], preferred_element_type=F32) + bo_ref[...]
        r = ALPHA * h_ref[...] + out
        r_ref[...] = r
        mu = jnp.mean(r, axis=-1, keepdims=True)
        xc = r - mu
        var = jnp.mean(xc * xc, axis=-1, keepdims=True)
        y = xc * lax.rsqrt(var + LN_EPS) * g_ref[...] + b_ref[...]
        y_ref[...] = y
        yb_ref[...] = y.astype(BF16)

    row = pl.BlockSpec((tm, Dm), lambda i: (i, 0))
    vec = pl.BlockSpec((1, Dm), lambda i: (0, 0))
    wsp = pl.BlockSpec((Dm, Dm), lambda i: (0, 0))
    sds = jax.ShapeDtypeStruct((S, Dm), F32)
    return _pcall(kern, name=name, out_shape=(sds, jax.ShapeDtypeStruct((S, Dm), BF16), sds), grid=(S // tm,),
                  in_specs=[row, wsp, row, vec, vec, vec], out_specs=(row, row, row), dims=("parallel",),
                  vmem_mb=56)(
                      mix, wout, h, bout.reshape(1, Dm), g.reshape(1, Dm), b.reshape(1, Dm))


def _adamw_math(w, g, m, v):
    m = ADAM_B1 * m + (1.0 - ADAM_B1) * g
    v = ADAM_B2 * v + (1.0 - ADAM_B2) * (g * g)
    m_hat = m / (1.0 - ADAM_B1 ** ADAM_STEP)
    v_hat = v / (1.0 - ADAM_B2 ** ADAM_STEP)
    delta = -ADAM_LR * (m_hat / (jnp.sqrt(v_hat) + ADAM_EPS) + ADAM_WD * w)
    return delta, m, v


def _row_tile(R, C):
    best = None
    for cand in range(8, R, 8):
        if R % cand == 0 and cand * C <= 256 * 1024:
            best = cand
    return best if best is not None else R


def _adamw(w, g, m, v, *, name):
    shape = w.shape
    C = shape[-1]
    R = 1
    for s in shape[:-1]:
        R *= s
    tr = _row_tile(R, C)

    def kern(w_ref, g_ref, m_ref, v_ref, d_ref, mo_ref, vo_ref):
        d, mn, vn = _adamw_math(w_ref[...], g_ref[...], m_ref[...], v_ref[...])
        d_ref[...] = d
        mo_ref[...] = mn
        vo_ref[...] = vn

    blk = pl.BlockSpec((tr, C), lambda i: (i, 0))
    sds = jax.ShapeDtypeStruct((R, C), F32)
    outs = _pcall(kern, name=name, out_shape=(sds, sds, sds), grid=(R // tr,), in_specs=[blk] * 4,
                  out_specs=(blk, blk, blk), dims=("parallel",), vmem_mb=48)(
                      w.reshape(R, C), g.reshape(R, C), m.reshape(R, C), v.reshape(R, C))
    return tuple(t.reshape(shape) for t in outs)


def _adamw_halves(w, m, v, halves, c_idx, *, name, comm=None):
    _, R, C = w.shape
    ch = C // 2
    tr = _row_tile(R, ch)
    nb = R // tr

    def kern(c_ref, w_ref, a0_ref, b0_ref, a1_ref, b1_ref, m_ref, v_ref, g_ref, d_ref, mo_ref, vo_ref):
        layer = pl.program_id(0) // nb
        mine = pl.program_id(1) == c_ref[0]
        g = jnp.where(layer == 0, jnp.where(mine, a0_ref[...], b0_ref[...]),
                      jnp.where(mine, a1_ref[...], b1_ref[...]))
        g_ref[...] = g
        d, mn, vn = _adamw_math(w_ref[...], g, m_ref[...], v_ref[...])
        d_ref[...] = d
        mo_ref[...] = mn
        vo_ref[...] = vn

    full = pl.BlockSpec((tr, ch), lambda i, hc: (i, hc))
    half = pl.BlockSpec((tr, ch), lambda i, hc: (i % nb, 0))
    sds = jax.ShapeDtypeStruct((2 * R, C), F32)
    (a0, b0), (a1, b1) = halves
    res = _pcall(kern, name=name, out_shape=(sds,) * 4, grid=(2 * nb, 2),
                 in_specs=[pl.BlockSpec(memory_space=pltpu.SMEM), full, half, half, half, half, full, full],
                 out_specs=(full,) * 4, dims=("parallel", "parallel"), vmem_mb=48, comm=comm)(
                     c_idx, w.reshape(2 * R, C), a0, b0, a1, b1, m.reshape(2 * R, C), v.reshape(2 * R, C))
    outs, landed = res if comm is not None else (res, None)
    outs = tuple(t.reshape(2, R, C) for t in outs)
    return outs if comm is None else (outs, landed)


def _packed_pieces(shape):
    if len(shape) == 4:
        return [((l * shape[1] + g) * 128, 128, (l, g)) for l in range(shape[0]) for g in range(shape[1])]
    per_row = shape[1] // LANE
    return [(a * per_row + j, 1, (slice(a, a + 1), slice(LANE * j, LANE * (j + 1))))
            for a in range(shape[0]) for j in range(per_row)]


def _small_sum_adamw(gathered, own, weights, *, name):
    R = gathered.shape[1]
    nw = len(weights)
    shapes = [w.shape for w, _, _ in weights]
    first_row, r0 = [], 0
    for shp in shapes:
        first_row.append(r0)
        n = 1
        for s in shp:
            n *= s
        r0 += n // LANE

    def kern(ga_ref, own_ref, *refs):
        ins, gsum_ref, outs = refs[:3 * nw], refs[3 * nw], refs[3 * nw + 1:]
        me = 4 * lax.axis_index("x") + 2 * lax.axis_index("y") + lax.axis_index("c")

        def block(k):
            other = ga_ref[jnp.where(me == k, (k + 1) % N_DEV, k)]
            return jnp.where(me == k, own_ref[...], other)

        g = block(0)
        for k in range(1, N_DEV):
            g = g + block(k)
        gsum_ref[...] = g
        for p, shp in enumerate(shapes):
            w_ref, m_ref, v_ref = ins[3 * p:3 * p + 3]
            g_out, d_out, m_out, v_out = outs[4 * p:4 * p + 4]
            for row, rows, idx in _packed_pieces(shp):
                gp = gsum_ref[first_row[p] + row:first_row[p] + row + rows, :]
                d, mn, vn = _adamw_math(w_ref[idx], gp, m_ref[idx], v_ref[idx])
                g_out[idx] = gp
                d_out[idx] = d
                m_out[idx] = mn
                v_out[idx] = vn

    out_shape = [jax.ShapeDtypeStruct((R, LANE), F32)]
    for shp in shapes:
        out_shape += [jax.ShapeDtypeStruct(shp, F32)] * 4
    flat = [a for wmv in weights for a in wmv]
    res = _pcall(kern, name=name, out_shape=tuple(out_shape), vmem_mb=48)(gathered, own, *flat)
    return res[0], [tuple(res[1 + 4 * p:5 + 4 * p]) for p in range(nw)]


def _pair_sum(g, theirs, c_idx, *, name):
    R, C = g.shape
    ch = C // 2
    tr = _row_tile(R, ch)

    def kern(c_ref, a_ref, b_ref, o_ref):
        o_ref[...] = (a_ref[...] + b_ref[...]).astype(BF16)

    gs = pltpu.PrefetchScalarGridSpec(
        num_scalar_prefetch=1, grid=(R // tr,),
        in_specs=[pl.BlockSpec((tr, ch), lambda i, c: (i, c[0])), pl.BlockSpec((tr, ch), lambda i, c: (i, 0))],
        out_specs=pl.BlockSpec((tr, ch), lambda i, c: (i, 0)))
    return pl.pallas_call(kern, name=name, out_shape=jax.ShapeDtypeStruct((R, ch), BF16), grid_spec=gs,
                          compiler_params=pltpu.CompilerParams(dimension_semantics=("parallel",),
                                                               vmem_limit_bytes=48 << 20))(c_idx, g, theirs)


WeightRows = collections.namedtuple("WeightRows", "full_rows own_rows cols pieces zero_rows")


def _w_in_piece_a(j):
    return jnp.where(j == 0, 0, 1232 * j + GAP)


def _w_in_piece_b(j):
    return jnp.where(j == 0, GAP_AT + GAP, 1232 * j + GAP_AT + GAP)


W_IN = WeightRows(NP, 1232, D_MODEL, ((0, GAP_AT, _w_in_piece_a), (GAP_AT, 1232 - GAP_AT, _w_in_piece_b)),
                  ((GAP_AT, GAP),))
W_OUT = WeightRows(2048, 512, D_MODEL, ((0, 512, lambda j: 512 * j),), ())
W_UQ = WeightRows(2048, 384, Q_LORA, ((0, 192, lambda j: 512 * j), (192, 192, lambda j: 512 * j + 256)),
                  tuple((256 * h + 192, 64) for h in range(N_HEADS)))
W_UKV = WeightRows(2048, 512, KV_LORA, ((0, 512, lambda j: 512 * j),), ())
W_CONV = WeightRows(64, 16, 256, ((0, 16, lambda j: 16 * j),), ())
SHARDED = (W_IN, W_OUT, W_UQ, W_UKV)
SHARDED_NAMES = ("w_in", "w_out", "w_uq", "w_ukv")
WEIGHT_ROWS = dict(zip(SHARDED_NAMES, SHARDED))


def _mesh_pos():
    x, y, c = lax.axis_index("x"), lax.axis_index("y"), lax.axis_index("c")
    return x, y, c


def _other_chips(x, y):
    return [(1 - x, y), (x, 1 - y), (1 - x, 1 - y)]


def _rows(start, n):
    return pl.ds(pl.multiple_of(start, 16), n)


def _half_cols(spec, c):
    ch = spec.cols // 2
    return pl.ds(pl.multiple_of(c * ch, LANE), ch)


def _allgather_script(specs, shards, zeros):
    na = len(specs)
    zlist = [a for a in range(na) if zeros[a] is not None]
    plan_first, plan_own, plan_zero = [], [], []
    for a, spec in enumerate(specs):
        for p in range(len(spec.pieces)):
            plan_own.append((a, p))
            for k in range(3):
                plan_first.append((a, p, k))
        for z in range(len(spec.zero_rows)):
            for l in range(shards[a].shape[0]):
                plan_zero.append((a, z, l))
    nf = len(plan_first)
    n_sems = 2 * nf + len(plan_own) + len(plan_zero)

    def copies(ins_all, outs, send_sems, recv_sems):
        ins = ins_all[:na]
        zrefs = dict(zip(zlist, ins_all[na:]))
        x, y, c = _mesh_pos()
        j = 2 * x + y
        chips = _other_chips(x, y)
        sibling = (x, y, 1 - c)

        def remote(src, dst, sem, to):
            return pltpu.make_async_remote_copy(src_ref=src, dst_ref=dst, send_sem=send_sems.at[sem],
                                                recv_sem=recv_sems.at[sem], device_id=to, device_id_type=MESH)

        def block(a, p, chip, cols):
            _, n, dst = specs[a].pieces[p]
            return outs[a].at[:, _rows(dst(chip), n), cols]

        def first(i):
            a, p, k = plan_first[i]
            src0, n, _ = specs[a].pieces[p]
            cols = _half_cols(specs[a], c)
            return remote(ins[a].at[:, pl.ds(src0, n), cols], block(a, p, j, cols), i, (*chips[k], c))

        def landed(i, half):
            a, p, k = plan_first[i]
            return block(a, p, 2 * chips[k][0] + chips[k][1], _half_cols(specs[a], half))

        def arrival(i, half, sem):
            return remote(landed(i, half), landed(i, half), sem, sibling)

        def passed(i):
            return remote(landed(i, c), landed(i, c), nf + i, sibling)

        def own(i):
            a, p = plan_own[i]
            src0, n, _ = specs[a].pieces[p]
            return remote(ins[a].at[:, pl.ds(src0, n), :], block(a, p, j, slice(None)), 2 * nf + i, sibling)

        def zero(i):
            a, z, l = plan_zero[i]
            r0, n = specs[a].zero_rows[z]
            return remote(zrefs[a].at[pl.ds(0, n), :], outs[a].at[l, pl.ds(r0, n), :],
                          2 * nf + len(plan_own) + i, sibling)

        fixed = [own(i) for i in range(len(plan_own))] + [zero(i) for i in range(len(plan_zero))]
        return c, fixed, first, arrival, passed

    def start(ins, outs, send_sems, recv_sems):
        _, fixed, first, _, _ = copies(ins, outs, send_sems, recv_sems)
        for cp in fixed:
            cp.start()
        for i in range(nf):
            first(i).start()

    def finish(ins, outs, send_sems, recv_sems):
        c, fixed, first, arrival, passed = copies(ins, outs, send_sems, recv_sems)
        for i in range(nf):
            arrival(i, c, i).wait_recv()
            passed(i).start()
        for i in range(nf):
            arrival(i, 1 - c, nf + i).wait_recv()
        for cp in fixed:
            cp.wait()
        for i in range(nf):
            first(i).wait_send()
            passed(i).wait_send()

    out_shape = tuple(jax.ShapeDtypeStruct((shards[a].shape[0], spec.full_rows, spec.cols), BF16)
                      for a, spec in enumerate(specs))
    args = tuple(shards) + tuple(zeros[a] for a in zlist)
    return CommScript(args, out_shape, n_sems, start, finish)


def _start_all_wait_all(args, out_shape, n_sems, make_copies):
    def start(ins, outs, send_sems, recv_sems):
        for cp in make_copies(ins, outs, send_sems, recv_sems):
            cp.start()

    def finish(ins, outs, send_sems, recv_sems):
        for cp in make_copies(ins, outs, send_sems, recv_sems):
            cp.wait()

    return CommScript(tuple(args), tuple(out_shape), n_sems, start, finish)


def _exchange_script(specs, grads):
    na = len(grads)

    def make_copies(ins, outs, send_sems, recv_sems):
        x, y, c = _mesh_pos()
        return [pltpu.make_async_remote_copy(
            src_ref=ins[a].at[:, _half_cols(specs[a], 1 - c)], dst_ref=outs[a], send_sem=send_sems.at[a],
            recv_sem=recv_sems.at[a], device_id=(x, y, 1 - c), device_id_type=MESH) for a in range(na)]

    out_shape = [jax.ShapeDtypeStruct((s.full_rows, s.cols // 2), F32) for s in specs]
    return _start_all_wait_all(grads, out_shape, na, make_copies)


def _scatter_script(specs, parts):
    na = len(parts)
    plan = [(a, p, k) for a in range(na) for p in range(len(specs[a].pieces)) for k in range(3)]

    def make_copies(ins, outs, send_sems, recv_sems):
        x, y, c = _mesh_pos()
        chips = _other_chips(x, y)
        copies = []
        for i, (a, p, k) in enumerate(plan):
            src0, n, dst = specs[a].pieces[p]
            pk = 2 * chips[k][0] + chips[k][1]
            copies.append(pltpu.make_async_remote_copy(
                src_ref=ins[a].at[_rows(dst(pk), n), :], dst_ref=outs[a].at[k, pl.ds(src0, n), :],
                send_sem=send_sems.at[i], recv_sem=recv_sems.at[i], device_id=(*chips[k], c), device_id_type=MESH))
        return copies

    out_shape = [jax.ShapeDtypeStruct((3, s.own_rows, s.cols // 2), BF16) for s in specs]
    return _start_all_wait_all(parts, out_shape, len(plan), make_copies)


def _chip_sum(spec, part, recv, *, name):
    ch = spec.cols // 2
    npieces = len(spec.pieces)

    def kern(recv_ref, part_ref, o_ref, own_ref, sems):
        j = 2 * lax.axis_index("x") + lax.axis_index("y")
        copies = []
        for p, (src0, n, dst) in enumerate(spec.pieces):
            copies.append(pltpu.make_async_copy(part_ref.at[_rows(dst(j), n), :], own_ref.at[pl.ds(src0, n), :],
                                                sems.at[p]))
        for cp in copies:
            cp.start()
        for cp in copies:
            cp.wait()
        o_ref[...] = ((own_ref[...].astype(F32) + recv_ref[0].astype(F32)) + recv_ref[1].astype(F32)) \
            + recv_ref[2].astype(F32)

    vm = pl.BlockSpec(memory_space=pltpu.VMEM)
    return _pcall(kern, name=name, out_shape=jax.ShapeDtypeStruct((spec.own_rows, ch), F32),
                  in_specs=[vm, HBM_SPEC], out_specs=vm,
                  scratch=[pltpu.VMEM((spec.own_rows, ch), BF16), pltpu.SemaphoreType.DMA((npieces,))],
                  vmem_mb=48)(recv, part)


def _sibling_script(sums):
    na = len(sums)

    def make_copies(ins, outs, send_sems, recv_sems):
        x, y, c = _mesh_pos()
        return [pltpu.make_async_remote_copy(
            src_ref=ins[a], dst_ref=outs[a], send_sem=send_sems.at[a], recv_sem=recv_sems.at[a],
            device_id=(x, y, 1 - c), device_id_type=MESH) for a in range(na)]

    out_shape = [jax.ShapeDtypeStruct(t.shape, t.dtype) for t in sums]
    return _start_all_wait_all(sums, out_shape, na, make_copies)


class _SemWindow:
    def __init__(self, sems, offset):
        self._sems, self._offset = sems, offset

    @property
    def at(self):
        return self

    def __getitem__(self, i):
        return self._sems.at[i + self._offset]


def _merge_scripts(*scripts):
    a_off, o_off, s_off = [0], [0], [0]
    for s in scripts:
        a_off.append(a_off[-1] + len(s.args))
        o_off.append(o_off[-1] + len(s.out_shape))
        s_off.append(s_off[-1] + s.n_sems)

    def phase(which):
        def run(ins, outs, send_sems, recv_sems):
            for n, s in enumerate(scripts):
                getattr(s, which)(ins[a_off[n]:a_off[n + 1]], outs[o_off[n]:o_off[n + 1]],
                                  _SemWindow(send_sems, s_off[n]), _SemWindow(recv_sems, s_off[n]))
        return run

    return CommScript(sum((tuple(s.args) for s in scripts), ()), sum((tuple(s.out_shape) for s in scripts), ()),
                      s_off[-1], phase("start"), phase("finish"))


class _GradReducer:
    def __init__(self, layer, names, grads, c_idx):
        self.specs = tuple(WEIGHT_ROWS[nm] for nm in names)
        self.grads, self.c_idx = tuple(grads), c_idx
        self.names = [f"{nm}{layer}" for nm in names]

    def exchange(self):
        return _exchange_script(self.specs, self.grads)

    def scatter(self, theirs):
        self.parts = tuple(_pair_sum(g, th, self.c_idx, name=f"pair_sum_{nm}")
                           for g, th, nm in zip(self.grads, theirs, self.names))
        return _scatter_script(self.specs, self.parts)

    def sibling(self, recv):
        self.sums = tuple(_chip_sum(s, p, r, name=f"chip_sum_{nm}")
                          for s, p, r, nm in zip(self.specs, self.parts, recv, self.names))
        return _sibling_script(self.sums)

    def done(self, others):
        return list(zip(self.sums, others))


def _allgather_small_script(block):
    m_per, n = block.shape

    def copies(ins, outs, send_sems, recv_sems):
        (x_ref,), (out_ref,) = ins, outs
        x, y, c = _mesh_pos()
        me, sibling = (x, y, c), (x, y, 1 - c)
        chips = _other_chips(x, y)

        def rows(px, py, pc):
            return out_ref.at[4 * px + 2 * py + pc]

        def copy(k, blk, to, src=None):
            return pltpu.make_async_remote_copy(
                src_ref=rows(*blk) if src is None else src, dst_ref=rows(*blk), send_sem=send_sems.at[k],
                recv_sem=recv_sems.at[k], device_id=to, device_id_type=MESH)

        first = [copy(0, me, sibling, src=x_ref)]
        first += [copy(1 + k, me, (*chip, c), src=x_ref) for k, chip in enumerate(chips)]
        passed = [copy(4 + k, (*chip, c), sibling) for k, chip in enumerate(chips)]
        landed = [copy(1 + k, (*chip, c), me) for k, chip in enumerate(chips)]
        from_sibling = [copy(0, sibling, me)] + [copy(4 + k, (*chip, 1 - c), me) for k, chip in enumerate(chips)]
        return first, passed, landed, from_sibling

    def start(ins, outs, send_sems, recv_sems):
        first, _, _, _ = copies(ins, outs, send_sems, recv_sems)
        for cp in first:
            cp.start()

    def finish(ins, outs, send_sems, recv_sems):
        first, passed, landed, from_sibling = copies(ins, outs, send_sems, recv_sems)
        for k in range(3):
            landed[k].wait_recv()
            passed[k].start()
        for cp in from_sibling:
            cp.wait_recv()
        for cp in first + passed:
            cp.wait_send()

    return CommScript((block,), (jax.ShapeDtypeStruct((N_DEV, m_per, n), block.dtype),), 7, start, finish)


def _rope_tables(positions):
    half = ROPE // 2
    inv_freq = ROPE_THETA ** (-jnp.arange(half, dtype=F32) / half)
    ang = positions.astype(F32)[:, None] * inv_freq
    cos, sin = jnp.cos(ang), jnp.sin(ang)
    S = positions.shape[0]
    cos_t = jnp.concatenate([cos, cos, jnp.ones((S, 64), F32)], axis=1)
    sin_t = jnp.concatenate([-sin, sin, jnp.zeros((S, 64), F32)], axis=1)
    return cos_t, sin_t


def _decode_conv(bits):
    rows = bits.reshape(DEPTH, N_CHIPS, 16, 256)[:, :, :3, :]
    conv = lax.bitcast_convert_type(rows.reshape(DEPTH, N_CHIPS, 3, 128, 2), F32)
    return jnp.transpose(conv, (0, 2, 1, 3)).reshape(DEPTH, 3, 512)


def _local_step(x, positions, target, emb_g, emb_b, w_in_t0, rest0, weights1, q_g, kv_g, w_pool, pool_scale,
                b_out, ln_g, ln_b, c_idx=None):
    cos_t, sin_t = _rope_tables(positions)
    if isinstance(w_in_t0, CommScript):
        (h, hb), (landed,) = _ln_fwd(x, emb_g, emb_b, name="emb_ln", comm=w_in_t0)
        w_in_t0 = landed[0]
    else:
        h, hb = _ln_fwd(x, emb_g, emb_b, name="emb_ln")
    weights = [None, weights1]
    saved = []
    for l in range(DEPTH):
        if l == 0 and isinstance(rest0, CommScript):
            proj, landed = _matmul(hb, w_in_t0, "nt", name="in_proj0", tm=1024, tn=1024, tk=2048, vmem_mb=56,
                                   comm=rest0)
            weights[0] = (w_in_t0,) + tuple(a[0] for a in landed[:3])
            conv_w = _decode_conv(landed[3])
        else:
            if l == 0:
                weights[0] = (w_in_t0,) + tuple(rest0[:3])
                conv_w = rest0[3]
            proj = _matmul(hb, weights[l][0], "nt", name=f"in_proj{l}", tm=1024, tn=1024, tk=2048, vmem_mb=56)
        w_in_t, w_out, w_uq_t, w_ukv_t = weights[l]
        qc, kc, v, vt, qn, kvn = _mla_qkv(proj, cos_t, sin_t, q_g[l], kv_g[l], w_uq_t, w_ukv_t, name=f"mla_qkv{l}")
        nxt = weights[l + 1] if l + 1 < DEPTH else None
        if isinstance(nxt, CommScript):
            (o, lse2), landed = _flash_fwd(qc, kc, vt, name=f"flash_fwd{l}", comm=nxt)
            weights[l + 1] = tuple(a[0] for a in landed)
        else:
            o, lse2 = _flash_fwd(qc, kc, vt, name=f"flash_fwd{l}")
        mix = _mixer_fwd(proj, o, w_pool[l], pool_scale[l], conv_w[l], name=f"mixer_fwd{l}")
        if l == DEPTH - 1:
            r = _outproj_residual(mix, w_out, h, b_out[l], name=f"out_proj{l}")
            saved.append((hb, proj, qc, kc, v, qn, kvn, o, lse2, mix, r))
        else:
            h_next, hb_next, r = _outproj_ln(mix, w_out, h, b_out[l], ln_g[l], ln_b[l], name=f"out_proj_ln{l}")
            saved.append((hb, proj, qc, kc, v, qn, kvn, o, lse2, mix, r))
            h, hb = h_next, hb_next

    small = [None] * DEPTH
    big = [None] * DEPTH
    above = scatter_above = None
    for l in reversed(range(DEPTH)):
        w_in_t, w_out, w_uq_t, w_ukv_t = weights[l]
        hb_in, proj, qc, kc, v, qn, kvn, o, lse2, mix, r = saved[l]
        if l == DEPTH - 1:
            loss_acc, dr, drb, d_ln_g, d_ln_b, d_b_out = _loss_ln_bwd(target, r, ln_g[l], ln_b[l], name="loss_ln_bwd")
        else:
            dr, drb, d_ln_g, d_ln_b, d_b_out = _ln_bwd(dh, r, ln_g[l], name=f"ln_bwd{l}")
        dmix = _matmul(drb, w_out, "nt", name=f"dmix{l}", tm=1024, tn=1024, tk=2048, vmem_mb=56)
        d_w_out = _matmul(mix, drb, "tn", name=f"dw_out{l}", tm=1024, tn=1024, tk=2048, vmem_mb=56)
        d_mix, do, d_w_pool, d_ps, d_conv = _mixer_bwd(dmix, proj, o, w_pool[l], pool_scale[l], conv_w[l],
                                                       name=f"mixer_bwd{l}")
        delta = _attn_delta(o, do, name=f"attn_delta{l}")
        if above is not None:
            (dqb, dkvb, dkr), recv = _flash_bwd(qc, kc, v, do, lse2, delta, cos_t, sin_t, name=f"flash_bwd{l}",
                                                comm=scatter_above)
            sibling_above = above.sibling(recv)
        else:
            dqb, dkvb, dkr = _flash_bwd(qc, kc, v, do, lse2, delta, cos_t, sin_t, name=f"flash_bwd{l}")
        d_mla, d_qg, d_kvg = _mla_qkv_bwd(dqb, dkvb, dkr, proj, cos_t, sin_t, q_g[l], kv_g[l], w_uq_t, w_ukv_t,
                                          name=f"mla_qkv_bwd{l}")
        d_w_uq_t = _matmul(dqb, qn, "tn", name=f"dw_uq{l}", tm=2048, tn=512, tk=2048, vmem_mb=56)
        d_w_ukv_t = _matmul(dkvb, kvn, "tn", name=f"dw_ukv{l}", tm=2048, tn=256, tk=2048, vmem_mb=56)
        small[l] = dict(q_g=d_qg[0], kv_g=d_kvg[0], w_pool=d_w_pool, pool_scale=d_ps[0], conv_w=d_conv,
                        b_out=d_b_out[0], ln_g=d_ln_g[0], ln_b=d_ln_b[0])
        rest = (d_w_out, d_w_uq_t, d_w_ukv_t)
        if c_idx is None:
            d_w_in_t = _dproj_t_times_h(d_mla, d_mix, hb_in, name=f"dw_in{l}")
            dh = _dproj_times_w(d_mla, d_mix, w_in_t, dr, ALPHA, name=f"dh{l}")
            big[l] = (d_w_in_t,) + rest
        elif l > 0:
            d_w_in_t = _dproj_t_times_h(d_mla, d_mix, hb_in, name=f"dw_in{l}")
            above = _GradReducer(l, SHARDED_NAMES, (d_w_in_t,) + rest, c_idx)
            dh, theirs = _dproj_times_w(d_mla, d_mix, w_in_t, dr, ALPHA, name=f"dh{l}", comm=above.exchange())
            scatter_above = above.scatter(theirs)
        else:
            red_rest = _GradReducer(l, SHARDED_NAMES[1:], rest, c_idx)
            d_w_in_t, landed = _dproj_t_times_h(d_mla, d_mix, hb_in, name=f"dw_in{l}",
                                                comm=_merge_scripts(sibling_above, red_rest.exchange()))
            big[l + 1] = above.done(landed[:len(SHARDED)])
            red_in = _GradReducer(l, SHARDED_NAMES[:1], (d_w_in_t,), c_idx)
            landed = _run_comm(_merge_scripts(red_in.exchange(), red_rest.scatter(landed[len(SHARDED):])),
                               name="exchange_w_in0")
            sibling_rest = red_rest.sibling(landed[1:])
            dh, landed = _dproj_times_w(d_mla, d_mix, w_in_t, dr, ALPHA, name=f"dh{l}",
                                        comm=_merge_scripts(red_in.scatter(landed[:1]), sibling_rest))
            recv_in, others_rest = landed[:1], landed[1:]
    if c_idx is None:
        grad_x, _, d_emb_g, d_emb_b, _ = _ln_bwd(dh, x, emb_g, name="emb_ln_bwd")
    else:
        (grad_x, _, d_emb_g, d_emb_b, _), others_in = _ln_bwd(dh, x, emb_g, name="emb_ln_bwd",
                                                              comm=red_in.sibling(recv_in))
        big[0] = red_in.done(others_in) + red_rest.done(others_rest)
    return loss_acc[0, 0], grad_x, d_emb_g[0], d_emb_b[0], small, big


SMALL_ORDER = ("emb_ln_g", "emb_ln_b", "q_norm_g", "kv_norm_g", "w_pool", "pool_scale", "b_out", "ln_g", "ln_b")


def _pack_small(arrs, extra_rows):
    flat = jnp.concatenate([a.reshape(-1) for a in arrs])
    rows = flat.shape[0] // LANE
    total = -(-(rows + extra_rows) // 8) * 8
    return jnp.pad(flat, (0, total * LANE - flat.shape[0])).reshape(total, LANE)


def kernel(x, positions, emb_ln_g, emb_ln_b, w_in, q_norm_g, kv_norm_g, w_uq, w_ukv, w_pool, pool_scale, conv_w, w_out, b_out, ln_g, ln_b, loss_target, m_emb_ln_g, m_emb_ln_b, m_w_in, m_q_norm_g, m_kv_norm_g, m_w_uq, m_w_ukv, m_w_pool, m_pool_scale, m_conv_w, m_w_out, m_b_out, m_ln_g, m_ln_b, v_emb_ln_g, v_emb_ln_b, v_w_in, v_q_norm_g, v_kv_norm_g, v_w_uq, v_w_ukv, v_w_pool, v_pool_scale, v_conv_w, v_w_out, v_b_out, v_ln_g, v_ln_b):
    xi, yi, ci = lax.axis_index("x"), lax.axis_index("y"), lax.axis_index("c")
    chip = 2 * xi + yi
    c_idx = ci.reshape(1).astype(jnp.int32)

    def t(a):
        return jnp.swapaxes(a, 1, 2)

    conv_bits = lax.bitcast_convert_type(conv_w.reshape(DEPTH, 3 * 128), BF16).reshape(DEPTH, 3, 256)
    conv_bits = jnp.pad(conv_bits, ((0, 0), (0, 13), (0, 0)))
    own = (t(w_in).astype(BF16), w_out.astype(BF16), t(w_uq).astype(BF16), t(w_ukv).astype(BF16))
    zeros = (jnp.zeros((GAP, D_MODEL), BF16), None, jnp.zeros((64, Q_LORA), BF16), None)
    gather_in0 = _allgather_script((W_IN,), (own[0][0:1],), zeros[:1])
    gather0 = _allgather_script(SHARDED[1:] + (W_CONV,), tuple(a[0:1] for a in own[1:]) + (conv_bits,),
                                zeros[1:] + (None,))
    gather1 = _allgather_script(SHARDED, tuple(a[1:2] for a in own), zeros)

    loss_part, grad_x, d_emb_g, d_emb_b, grads, reduced = _local_step(
        x[0], positions[0], loss_target[0], emb_ln_g, emb_ln_b, gather_in0, gather0, gather1, q_norm_g, kv_norm_g,
        w_pool, pool_scale, b_out, ln_g, ln_b, c_idx)

    small_g = [d_emb_g, d_emb_b,
               jnp.stack([grads[l]["q_g"] for l in range(DEPTH)]), jnp.stack([grads[l]["kv_g"] for l in range(DEPTH)]),
               jnp.stack([grads[l]["w_pool"] for l in range(DEPTH)]),
               jnp.stack([grads[l]["pool_scale"] for l in range(DEPTH)]),
               jnp.stack([grads[l]["b_out"] for l in range(DEPTH)]), jnp.stack([grads[l]["ln_g"] for l in range(DEPTH)]),
               jnp.stack([grads[l]["ln_b"] for l in range(DEPTH)]),
               jnp.stack([grads[l]["conv_w"] for l in range(DEPTH)]),
               jnp.pad(loss_part.reshape(1), (0, LANE - 1))]
    def rows(a):
        return a.reshape(1, -1) if a.ndim == 1 else a

    small_wmv = [tuple(rows(a) for a in wmv) for wmv in (
        (emb_ln_g, m_emb_ln_g, v_emb_ln_g), (emb_ln_b, m_emb_ln_b, v_emb_ln_b),
        (q_norm_g, m_q_norm_g, v_q_norm_g), (kv_norm_g, m_kv_norm_g, v_kv_norm_g), (w_pool, m_w_pool, v_w_pool),
        (pool_scale, m_pool_scale, v_pool_scale), (b_out, m_b_out, v_b_out), (ln_g, m_ln_g, v_ln_g),
        (ln_b, m_ln_b, v_ln_b))]
    packed_g = _pack_small(small_g, 0)

    def halves(a):
        return [reduced[l][a] for l in range(DEPTH)]

    upd = {}
    res_w_in, (gathered,) = _adamw_halves(t(w_in), t(m_w_in), t(v_w_in), halves(0), c_idx, name="adamw_w_in",
                                          comm=_allgather_small_script(packed_g))
    upd["w_in"] = tuple(t(o) for o in res_w_in)
    g_tot, small_upd = _small_sum_adamw(gathered, packed_g, small_wmv, name="small_sum_adamw")
    off = sum(w.size for w, _, _ in small_wmv)
    flat_tot = g_tot.reshape(-1)
    conv_tot = flat_tot[off:off + DEPTH * 3 * 512].reshape(DEPTH, 3, 512)
    loss = flat_tot[off + DEPTH * 3 * 512]
    g_conv = lax.dynamic_slice_in_dim(conv_tot, chip * 128, 128, axis=2)

    def whole(a):
        return jnp.stack([jnp.where(ci == 0, jnp.concatenate([mine, oth], axis=1),
                                    jnp.concatenate([oth, mine], axis=1)) for mine, oth in halves(a)])

    upd["w_out"] = _adamw_halves(w_out, m_w_out, v_w_out, halves(1), c_idx, name="adamw_w_out")
    g_uq, g_ukv = t(whole(2)), t(whole(3))
    upd["w_uq"] = (g_uq,) + _adamw(w_uq, g_uq, m_w_uq, v_w_uq, name="adamw_w_uq")
    upd["w_ukv"] = (g_ukv,) + _adamw(w_ukv, g_ukv, m_w_ukv, v_w_ukv, name="adamw_w_ukv")
    upd["conv_w"] = (g_conv,) + _adamw(conv_w, g_conv, m_conv_w, v_conv_w, name="adamw_conv_w")
    for nm, res in zip(SMALL_ORDER, small_upd):
        upd[nm] = tuple(a.reshape(-1) for a in res) if nm in ("emb_ln_g", "emb_ln_b") else res

    order = ("emb_ln_g", "emb_ln_b", "w_in", "q_norm_g", "kv_norm_g", "w_uq", "w_ukv", "w_pool", "pool_scale",
             "conv_w", "w_out", "b_out", "ln_g", "ln_b")
    outs = [loss, grad_x[None]]
    for field in range(4):
        outs += [upd[nm][field] for nm in order]
    return tuple(outs)
```

```python
import collections

import jax
import jax.numpy as jnp
from jax import lax
from jax.experimental import pallas as pl
from jax.experimental.pallas import tpu as pltpu

F32 = jnp.float32
BF16 = jnp.bfloat16
MESH = pl.DeviceIdType.MESH

D_MODEL = 2048
DEPTH = 2
N_HEADS = 8
NOPE = 128
ROPE = 64
Q_LORA = 512
KV_LORA = 256
D_MLA = 1024
POOL_WINDOWS = (2, 4, 8, 16)
D_IN_PROJ = 4928
LN_EPS = 1e-5
RMS_EPS = 1e-6
ROPE_THETA = 10000.0
ALPHA = (2 * DEPTH) ** 0.25
SCALE = (NOPE + ROPE) ** -0.5
LOG2E = 1.4426950408889634
SCALE_LOG2E = SCALE * LOG2E
ADAM_LR = 0.001
ADAM_B1 = 0.9
ADAM_B2 = 0.999
ADAM_EPS = 1e-08
ADAM_WD = 0.01
ADAM_STEP = 10

NP = 5120
GAP_AT = 832
GAP = NP - D_IN_PROJ
W_MLA = 1024
W_MIX = NP - W_MLA
HALO = 16
LANE = 128
N_CHIPS = 4
N_DEV = 8
TQ = 512
FWD_GROUP = 4

NN = (((1,), (0,)), ((), ()))
NT = (((1,), (1,)), ((), ()))
TN = (((0,), (0,)), ((), ()))


CommScript = collections.namedtuple("CommScript", "args out_shape n_sems start finish")
HBM_SPEC = pl.BlockSpec(memory_space=pl.ANY)


def _pcall(kern, *, name, out_shape, grid=None, in_specs=None, out_specs=None, scratch=(), dims=None,
           vmem_mb=None, comm=None):
    cp = {}
    if dims is not None:
        cp["dimension_semantics"] = dims if comm is None else ("arbitrary",) * len(dims)
    if vmem_mb is not None:
        cp["vmem_limit_bytes"] = vmem_mb << 20
    if comm is None:
        args = dict(name=name, out_shape=out_shape, scratch_shapes=list(scratch),
                    compiler_params=pltpu.CompilerParams(**cp))
        if grid is not None:
            args["grid"] = grid
        if in_specs is not None:
            args["in_specs"] = in_specs
        if out_specs is not None:
            args["out_specs"] = out_specs
        return pl.pallas_call(kern, **args)

    single = not isinstance(out_shape, (tuple, list))
    own_out = (out_shape,) if single else tuple(out_shape)
    own_out_specs = (out_specs,) if single else tuple(out_specs)
    n_in, n_out, n_scr = len(in_specs), len(own_out), len(scratch)
    na, no = len(comm.args), len(comm.out_shape)

    def at(end):
        cond = None
        for d, n in enumerate(grid):
            here = pl.program_id(d) == (n - 1 if end else 0)
            cond = here if cond is None else jnp.logical_and(cond, here)
        return cond

    def wrapped(*refs):
        own_in, c_in = refs[:n_in], refs[n_in:n_in + na]
        o0 = n_in + na
        own_o, c_out = refs[o0:o0 + n_out], refs[o0 + n_out:o0 + n_out + no]
        s0 = o0 + n_out + no
        own_s, (send_sems, recv_sems) = refs[s0:s0 + n_scr], refs[s0 + n_scr:]

        @pl.when(at(False))
        def _():
            comm.start(c_in, c_out, send_sems, recv_sems)

        kern(*own_in, *own_o, *own_s)

        @pl.when(at(True))
        def _():
            comm.finish(c_in, c_out, send_sems, recv_sems)

    call = pl.pallas_call(
        wrapped, name=name, out_shape=own_out + tuple(comm.out_shape), grid=grid,
        in_specs=list(in_specs) + [HBM_SPEC] * na, out_specs=own_out_specs + (HBM_SPEC,) * no,
        scratch_shapes=list(scratch) + [pltpu.SemaphoreType.DMA((comm.n_sems,)),
                                        pltpu.SemaphoreType.DMA((comm.n_sems,))],
        compiler_params=pltpu.CompilerParams(**cp))

    def run(*args):
        res = call(*args, *comm.args)
        own = res[0] if single else tuple(res[:n_out])
        return own, tuple(res[n_out:])

    return run


def _run_comm(script, *, name):
    na, no = len(script.args), len(script.out_shape)

    def body(*refs):
        ins, outs = refs[:na], refs[na:na + no]
        send_sems, recv_sems = refs[na + no:]
        script.start(ins, outs, send_sems, recv_sems)
        script.finish(ins, outs, send_sems, recv_sems)

    return pl.pallas_call(
        body, name=name, out_shape=tuple(script.out_shape), in_specs=[HBM_SPEC] * na, out_specs=(HBM_SPEC,) * no,
        scratch_shapes=[pltpu.SemaphoreType.DMA((script.n_sems,)), pltpu.SemaphoreType.DMA((script.n_sems,))])(
            *script.args)


def _sigmoid(g):
    return 1.0 / (1.0 + jnp.exp(-g))


def _silu_and_grad(g):
    sig = _sigmoid(g)
    return g * sig, sig * (1.0 + g * (1.0 - sig))


def _matmul(a, b, mode, *, name, tm, tn, tk, out_dtype=F32, vmem_mb=48, comm=None):
    if mode == "nn":
        (M, K), N = a.shape, b.shape[1]
    elif mode == "nt":
        (M, K), N = a.shape, b.shape[0]
    else:
        (K, M), N = a.shape, b.shape[1]
    tm, tn, tk = min(tm, M), min(tn, N), min(tk, K)
    assert M % tm == 0 and N % tn == 0 and K % tk == 0, (name, M, N, K)
    nk = K // tk
    dn = {"nn": NN, "nt": NT, "tn": TN}[mode]
    if mode == "tn":
        a_spec = pl.BlockSpec((tk, tm), lambda i, j, k: (k, i))
    else:
        a_spec = pl.BlockSpec((tm, tk), lambda i, j, k: (i, k))
    if mode == "nt":
        b_spec = pl.BlockSpec((tn, tk), lambda i, j, k: (j, k))
    else:
        b_spec = pl.BlockSpec((tk, tn), lambda i, j, k: (k, j))
    o_spec = pl.BlockSpec((tm, tn), lambda i, j, k: (i, j))

    def kern(a_ref, b_ref, o_ref, *rest):
        part = lax.dot_general(a_ref[...].astype(BF16), b_ref[...].astype(BF16), dn,
                               preferred_element_type=F32)
        if nk == 1:
            o_ref[...] = part.astype(out_dtype)
        else:
            acc_ref = rest[0]
            k = pl.program_id(2)

            @pl.when(k == 0)
            def _():
                acc_ref[...] = part

            @pl.when(k > 0)
            def _():
                acc_ref[...] += part

            @pl.when(k == nk - 1)
            def _():
                o_ref[...] = acc_ref[...].astype(out_dtype)

    scratch = [pltpu.VMEM((tm, tn), F32)] if nk > 1 else []
    return _pcall(kern, name=name, out_shape=jax.ShapeDtypeStruct((M, N), out_dtype),
                  grid=(M // tm, N // tn, nk), in_specs=[a_spec, b_spec], out_specs=o_spec, scratch=scratch,
                  dims=("parallel", "parallel", "arbitrary"), vmem_mb=vmem_mb, comm=comm)(a, b)


def _dproj_times_w(d_mla, d_mix, wt, add, add_scale, *, name, comm=None):
    S = d_mla.shape[0]
    Dm = wt.shape[1]
    tm, tn, tk = min(1024, S), 1024, 2048
    nk = 1 + W_MIX // tk

    def kern(a1_ref, a2_ref, b1_ref, b2_ref, add_ref, o_ref, acc_ref):
        k = pl.program_id(2)

        @pl.when(k == 0)
        def _():
            acc_ref[...] = jnp.dot(a1_ref[...], b1_ref[...], preferred_element_type=F32)

        @pl.when(k > 0)
        def _():
            acc_ref[...] += jnp.dot(a2_ref[...], b2_ref[...], preferred_element_type=F32)

        @pl.when(k == nk - 1)
        def _():
            o_ref[...] = add_scale * add_ref[...] + acc_ref[...]

    o_spec = pl.BlockSpec((tm, tn), lambda i, j, k: (i, j))
    b2_spec = pl.BlockSpec((pl.Element(tk), pl.Element(tn)),
                           lambda i, j, k: (pl.multiple_of(W_MLA + tk * jnp.maximum(k - 1, 0), W_MLA),
                                            pl.multiple_of(j * tn, tn)))
    return _pcall(kern, name=name, out_shape=jax.ShapeDtypeStruct((S, Dm), F32), grid=(S // tm, Dm // tn, nk),
                  in_specs=[pl.BlockSpec((tm, W_MLA), lambda i, j, k: (i, 0)),
                            pl.BlockSpec((tm, tk), lambda i, j, k: (i, jnp.maximum(k - 1, 0))),
                            pl.BlockSpec((W_MLA, tn), lambda i, j, k: (0, j)), b2_spec, o_spec],
                  out_specs=o_spec, scratch=[pltpu.VMEM((tm, tn), F32)],
                  dims=("parallel", "parallel", "arbitrary"), vmem_mb=56, comm=comm)(d_mla, d_mix, wt, wt, add)


def _dproj_t_times_h(d_mla, d_mix, h, *, name, comm=None):
    S, Dm = h.shape
    tm, tn, tk = W_MLA, 1024, min(2048, S)
    nk = S // tk

    def kern(a1_ref, a2_ref, b_ref, o_ref, acc_ref):
        i = pl.program_id(0)
        k = pl.program_id(2)
        b = b_ref[...].astype(BF16)

        def accumulate(part):
            @pl.when(k == 0)
            def _():
                acc_ref[...] = part

            @pl.when(k > 0)
            def _():
                acc_ref[...] += part

        @pl.when(i == 0)
        def _():
            accumulate(lax.dot_general(a1_ref[...], b, TN, preferred_element_type=F32))

        @pl.when(i > 0)
        def _():
            accumulate(lax.dot_general(a2_ref[...], b, TN, preferred_element_type=F32))

        @pl.when(k == nk - 1)
        def _():
            o_ref[...] = acc_ref[...]

    return _pcall(kern, name=name, out_shape=jax.ShapeDtypeStruct((NP, Dm), F32), grid=(NP // tm, Dm // tn, nk),
                  in_specs=[pl.BlockSpec((tk, tm), lambda i, j, k: (jnp.where(i == 0, k, nk - 1), 0)),
                            pl.BlockSpec((tk, tm), lambda i, j, k: (jnp.where(i == 0, 0, k), jnp.maximum(i - 1, 0))),
                            pl.BlockSpec((tk, tn), lambda i, j, k: (k, j))],
                  out_specs=pl.BlockSpec((tm, tn), lambda i, j, k: (i, j)), scratch=[pltpu.VMEM((tm, tn), F32)],
                  dims=("parallel", "parallel", "arbitrary"), vmem_mb=48, comm=comm)(d_mla, d_mix, h)


def _ln_fwd(x, g, b, *, name, comm=None):
    S, Dm = x.shape
    tm = min(512, S)

    def kern(x_ref, g_ref, b_ref, y_ref, yb_ref):
        xf = x_ref[...]
        mu = jnp.mean(xf, axis=-1, keepdims=True)
        xc = xf - mu
        var = jnp.mean(xc * xc, axis=-1, keepdims=True)
        y = xc * lax.rsqrt(var + LN_EPS) * g_ref[...] + b_ref[...]
        y_ref[...] = y
        yb_ref[...] = y.astype(BF16)

    row = pl.BlockSpec((tm, Dm), lambda i: (i, 0))
    vec = pl.BlockSpec((1, Dm), lambda i: (0, 0))
    return _pcall(kern, name=name,
                  out_shape=(jax.ShapeDtypeStruct((S, Dm), F32), jax.ShapeDtypeStruct((S, Dm), BF16)),
                  grid=(S // tm,), in_specs=[row, vec, vec], out_specs=(row, row), dims=("parallel",), vmem_mb=48,
                  comm=comm)(
                      x, g.reshape(1, Dm), b.reshape(1, Dm))


def _ln_bwd(dy, r, g, *, name, comm=None):
    S, Dm = r.shape
    tm = min(512, S)

    def kern(dy_ref, r_ref, g_ref, dr_ref, drb_ref, dg_ref, db_ref, ds_ref):
        @pl.when(pl.program_id(0) == 0)
        def _():
            dg_ref[...] = jnp.zeros_like(dg_ref)
            db_ref[...] = jnp.zeros_like(db_ref)
            ds_ref[...] = jnp.zeros_like(ds_ref)

        rf = r_ref[...]
        dyf = dy_ref[...]
        mu = jnp.mean(rf, axis=-1, keepdims=True)
        xc = rf - mu
        var = jnp.mean(xc * xc, axis=-1, keepdims=True)
        rstd = lax.rsqrt(var + LN_EPS)
        xhat = xc * rstd
        dxh = dyf * g_ref[...]
        c1 = jnp.mean(dxh, axis=-1, keepdims=True)
        c2 = jnp.mean(dxh * xhat, axis=-1, keepdims=True)
        dr = rstd * (dxh - c1 - xhat * c2)
        dr_ref[...] = dr
        drb_ref[...] = dr.astype(BF16)
        dg_ref[...] += jnp.sum(dyf * xhat, axis=0, keepdims=True)
        db_ref[...] += jnp.sum(dyf, axis=0, keepdims=True)
        ds_ref[...] += jnp.sum(dr, axis=0, keepdims=True)

    row = pl.BlockSpec((tm, Dm), lambda i: (i, 0))
    vec = pl.BlockSpec((1, Dm), lambda i: (0, 0))
    vshape = jax.ShapeDtypeStruct((1, Dm), F32)
    return _pcall(kern, name=name,
                  out_shape=(jax.ShapeDtypeStruct((S, Dm), F32), jax.ShapeDtypeStruct((S, Dm), BF16),
                             vshape, vshape, vshape),
                  grid=(S // tm,), in_specs=[row, row, vec], out_specs=(row, row, vec, vec, vec),
                  dims=("arbitrary",), vmem_mb=48, comm=comm)(dy, r, g.reshape(1, Dm))


def _loss_ln_bwd(target, r, g, b, *, name):
    S, Dm = r.shape
    tm = min(512, S)

    def kern(t_ref, r_ref, g_ref, b_ref, l_ref, dr_ref, drb_ref, dg_ref, db_ref, ds_ref):
        @pl.when(pl.program_id(0) == 0)
        def _():
            l_ref[...] = jnp.zeros_like(l_ref)
            dg_ref[...] = jnp.zeros_like(dg_ref)
            db_ref[...] = jnp.zeros_like(db_ref)
            ds_ref[...] = jnp.zeros_like(ds_ref)

        rf = r_ref[...]
        mu = jnp.mean(rf, axis=-1, keepdims=True)
        xc = rf - mu
        var = jnp.mean(xc * xc, axis=-1, keepdims=True)
        rstd = lax.rsqrt(var + LN_EPS)
        xhat = xc * rstd
        e = (xhat * g_ref[...] + b_ref[...]) - t_ref[...]
        dyf = e / float(Dm)
        per_row = jnp.mean(e * e, axis=-1, keepdims=True)
        l_ref[...] += 0.5 * jnp.sum(per_row, axis=0, keepdims=True)
        dxh = dyf * g_ref[...]
        c1 = jnp.mean(dxh, axis=-1, keepdims=True)
        c2 = jnp.mean(dxh * xhat, axis=-1, keepdims=True)
        dr = rstd * (dxh - c1 - xhat * c2)
        dr_ref[...] = dr
        drb_ref[...] = dr.astype(BF16)
        dg_ref[...] += jnp.sum(dyf * xhat, axis=0, keepdims=True)
        db_ref[...] += jnp.sum(dyf, axis=0, keepdims=True)
        ds_ref[...] += jnp.sum(dr, axis=0, keepdims=True)

    row = pl.BlockSpec((tm, Dm), lambda i: (i, 0))
    vec = pl.BlockSpec((1, Dm), lambda i: (0, 0))
    acc = pl.BlockSpec((8, LANE), lambda i: (0, 0))
    vshape = jax.ShapeDtypeStruct((1, Dm), F32)
    return _pcall(kern, name=name,
                  out_shape=(jax.ShapeDtypeStruct((8, LANE), F32), jax.ShapeDtypeStruct((S, Dm), F32),
                             jax.ShapeDtypeStruct((S, Dm), BF16), vshape, vshape, vshape),
                  grid=(S // tm,), in_specs=[row, row, vec, vec], out_specs=(acc, row, row, vec, vec, vec),
                  dims=("arbitrary",), vmem_mb=56)(target, r, g.reshape(1, Dm), b.reshape(1, Dm))


def _rot_sum(t):
    return pltpu.roll(t, 32, 1) + pltpu.roll(t, 96, 1)


def _mla_qkv(proj, cos_t, sin_t, qg, kvg, wuq_t, wukv_t, *, name):
    S = proj.shape[0]
    tm = min(256, S)

    def kern(ql_ref, kvl_ref, kr_ref, cos_ref, sin_ref, qg_ref, kvg_ref, wuq_ref, wukv_ref,
             qc_ref, kc_ref, v_ref, vt_ref, qn_ref, kvn_ref):
        cosv = cos_ref[...]
        sinv = sin_ref[...]

        def rope(t):
            return t * cosv + _rot_sum(t) * sinv

        ql = ql_ref[...]
        qn = (ql * lax.rsqrt(jnp.mean(ql * ql, axis=-1, keepdims=True) + RMS_EPS) * qg_ref[...]).astype(BF16)
        kvl = kvl_ref[...]
        kvn = (kvl * lax.rsqrt(jnp.mean(kvl * kvl, axis=-1, keepdims=True) + RMS_EPS) * kvg_ref[...]).astype(BF16)
        qn_ref[...] = qn
        kvn_ref[...] = kvn
        q = lax.dot_general(qn, wuq_ref[...], NT, preferred_element_type=F32)
        kv = lax.dot_general(kvn, wukv_ref[...], NT, preferred_element_type=F32)
        kr = rope(kr_ref[...]).astype(BF16)
        for h in range(N_HEADS):
            c0 = 256 * h
            qc_ref[:, c0:c0 + 128] = q[:, c0:c0 + 128].astype(BF16)
            qc_ref[:, c0 + 128:c0 + 256] = rope(q[:, c0 + 128:c0 + 256]).astype(BF16)
            kc_ref[:, c0:c0 + 128] = kv[:, c0:c0 + 128].astype(BF16)
            kc_ref[:, c0 + 128:c0 + 256] = kr
            vh = kv[:, c0 + 128:c0 + 256]
            v_ref[:, 128 * h:128 * h + 128] = vh.astype(BF16)
            vt_ref[h] = jnp.transpose(vh).astype(BF16)

    def row(w, blk):
        return pl.BlockSpec((tm, w), lambda i: (i, blk))

    def full(shape):
        return pl.BlockSpec(shape, lambda i: (0,) * len(shape))

    t = min(TQ, S)
    per = t // tm
    vt_spec = pl.BlockSpec((N_HEADS, None, 128, tm), lambda i: (0, i // per, 0, i % per))
    outs = (jax.ShapeDtypeStruct((S, 2048), BF16), jax.ShapeDtypeStruct((S, 2048), BF16),
            jax.ShapeDtypeStruct((S, 1024), BF16), jax.ShapeDtypeStruct((N_HEADS, S // t, 128, t), BF16),
            jax.ShapeDtypeStruct((S, Q_LORA), BF16), jax.ShapeDtypeStruct((S, KV_LORA), BF16))
    return _pcall(kern, name=name, out_shape=outs, grid=(S // tm,),
                  in_specs=[row(512, 0), row(256, 2), row(128, 6), row(128, 0), row(128, 0),
                            full((1, Q_LORA)), full((1, KV_LORA)), full((2048, Q_LORA)), full((2048, KV_LORA))],
                  out_specs=(row(2048, 0), row(2048, 0), row(1024, 0), vt_spec, row(512, 0), row(256, 0)),
                  dims=("parallel",), vmem_mb=48)(
                      proj, proj, proj, cos_t, sin_t, qg.reshape(1, -1), kvg.reshape(1, -1), wuq_t, wukv_t)


def _mla_qkv_bwd(dqb, dkvb, dkr_heads, proj, cos_t, sin_t, qg, kvg, wuq_t, wukv_t, *, name):
    S = proj.shape[0]
    tm = min(256, S)

    def kern(dqb_ref, dkvb_ref, dkrh_ref, ql_ref, kvl_ref, cos_ref, sin_ref, qg_ref, kvg_ref, wuq_ref, wukv_ref,
             dml_ref, dqg_ref, dkvg_ref):
        @pl.when(pl.program_id(0) == 0)
        def _():
            dqg_ref[...] = jnp.zeros_like(dqg_ref)
            dkvg_ref[...] = jnp.zeros_like(dkvg_ref)

        cosv = cos_ref[...]
        sinv = sin_ref[...]

        def unrope(t):
            return t * cosv - _rot_sum(t) * sinv

        dkr = dkrh_ref[:, 0:128]
        for h in range(1, N_HEADS):
            dkr = dkr + dkrh_ref[:, 128 * h:128 * h + 128]

        def rms_bwd(x, g, dy):
            n = x.shape[-1]
            rs = lax.rsqrt(jnp.mean(x * x, axis=-1, keepdims=True) + RMS_EPS)
            dyg = dy * g
            dx = rs * dyg - x * (rs * rs * rs) * (jnp.sum(dyg * x, axis=-1, keepdims=True) / n)
            return dx, jnp.sum(dy * (x * rs), axis=0, keepdims=True)

        dqn = jnp.dot(dqb_ref[...], wuq_ref[...], preferred_element_type=F32)
        dql, dqg = rms_bwd(ql_ref[...], qg_ref[...], dqn)
        dqg_ref[...] += dqg
        dkvn = jnp.dot(dkvb_ref[...], wukv_ref[...], preferred_element_type=F32)
        dkvl, dkvg = rms_bwd(kvl_ref[...], kvg_ref[...], dkvn)
        dkvg_ref[...] += dkvg
        dml_ref[:, 0:512] = dql.astype(BF16)
        dml_ref[:, 512:768] = dkvl.astype(BF16)
        dml_ref[:, 768:896] = unrope(dkr).astype(BF16)
        dml_ref[:, 896:1024] = jnp.zeros((tm, 128), BF16)

    def row(w, blk):
        return pl.BlockSpec((tm, w), lambda i: (i, blk))

    def full(shape):
        return pl.BlockSpec(shape, lambda i: (0,) * len(shape))

    outs = (jax.ShapeDtypeStruct((S, W_MLA), BF16), jax.ShapeDtypeStruct((1, Q_LORA), F32),
            jax.ShapeDtypeStruct((1, KV_LORA), F32))
    return _pcall(kern, name=name, out_shape=outs, grid=(S // tm,),
                  in_specs=[row(2048, 0), row(2048, 0), row(1024, 0), row(512, 0), row(256, 2),
                            row(128, 0), row(128, 0), full((1, Q_LORA)), full((1, KV_LORA)),
                            full((2048, Q_LORA)), full((2048, KV_LORA))],
                  out_specs=(row(W_MLA, 0), full((1, Q_LORA)), full((1, KV_LORA))),
                  dims=("arbitrary",), vmem_mb=56)(
                      dqb, dkvb, dkr_heads, proj, proj, cos_t, sin_t, qg.reshape(1, -1), kvg.reshape(1, -1),
                      wuq_t, wukv_t)


def _flash_fwd(qc, kc, vt, *, name, comm=None):
    S = qc.shape[0]
    t = min(TQ, S)
    n = S // t

    def kern(q_ref, k_ref, vt_ref, o_ref, lse_ref, m_s, l_s, acc_s):
        qi = pl.program_id(1)
        m_s[...] = jnp.full_like(m_s, -jnp.inf)
        l_s[...] = jnp.zeros_like(l_s)
        acc_s[...] = jnp.zeros_like(acc_s)

        half = t // 2

        def scores(kb, q_lo=0, q_n=t, k_n=t):
            k0 = pl.multiple_of(kb * t, t)
            return lax.dot_general(k_ref[pl.ds(k0, k_n), :], q_ref[q_lo:q_lo + q_n, :], NT,
                                   preferred_element_type=F32)

        def update(kb, st, q_lo=0, diagonal=False):
            k_n, q_n = st.shape
            if diagonal:
                krow = lax.broadcasted_iota(jnp.int32, (k_n, q_n), 0)
                qcol = lax.broadcasted_iota(jnp.int32, (k_n, q_n), 1) + q_lo
                st = jnp.where(krow <= qcol, st, -jnp.inf)
            lanes = slice(q_lo, q_lo + q_n)
            m_prev = m_s[:, lanes]
            m_new = jnp.maximum(m_prev, jnp.max(st, axis=0, keepdims=True))
            a = jnp.exp2((m_prev - m_new) * SCALE_LOG2E)
            pt = jnp.exp2((st - m_new) * SCALE_LOG2E)
            l_s[:, lanes] = a * l_s[:, lanes] + jnp.sum(pt, axis=0, keepdims=True)
            acc_s[:, lanes] = a * acc_s[:, lanes] + jnp.dot(vt_ref[kb, :, 0:k_n], pt.astype(BF16),
                                                            preferred_element_type=F32)
            m_s[:, lanes] = m_new

        def group(kb, count, last_diagonal):
            whole = count - 1 if last_diagonal else count
            sts = [scores(kb + g) for g in range(whole)]
            if last_diagonal:
                kd = kb + count - 1
                s_lo, s_hi = scores(kd, 0, half, half), scores(kd, half, half, t)
            for g in range(whole):
                update(kb + g, sts[g])
            if last_diagonal:
                update(kd, s_lo, 0, True)
                update(kd, s_hi, half, True)

        def body(i, carry):
            group(FWD_GROUP * i, FWD_GROUP, False)
            return carry

        full = qi // FWD_GROUP
        lax.fori_loop(0, full, body, 0)
        for rem in range(FWD_GROUP):
            @pl.when(qi - FWD_GROUP * full == rem)
            def _():
                group(qi - rem, rem + 1, True)
        o_ref[...] = jnp.transpose(acc_s[...] / l_s[...])
        lse_ref[pl.ds(qi, 1), :] = m_s[...] * SCALE_LOG2E + jnp.log2(l_s[...])

    q_spec = pl.BlockSpec((t, 256), lambda h, qi: (qi, h))
    k_spec = pl.BlockSpec((S, 256), lambda h, qi: (0, h))
    vt_spec = pl.BlockSpec((None, n, 128, t), lambda h, qi: (h, 0, 0, 0))
    o_spec = pl.BlockSpec((t, 128), lambda h, qi: (qi, h))
    lse_spec = pl.BlockSpec((None, n, t), lambda h, qi: (h, 0, 0))
    return _pcall(kern, name=name,
                  out_shape=(jax.ShapeDtypeStruct((S, D_MLA), F32), jax.ShapeDtypeStruct((N_HEADS, n, t), F32)),
                  grid=(N_HEADS, n), in_specs=[q_spec, k_spec, vt_spec], out_specs=(o_spec, lse_spec),
                  scratch=[pltpu.VMEM((1, t), F32), pltpu.VMEM((1, t), F32), pltpu.VMEM((128, t), F32)],
                  dims=("parallel", "arbitrary"), vmem_mb=48, comm=comm)(qc, kc, vt)


def _attn_delta(o, do, *, name):
    S = o.shape[0]
    t = min(TQ, S)
    n = S // t

    def kern(o_ref, do_ref, dl_ref):
        i = pl.program_id(0)
        prod = o_ref[...] * do_ref[...]
        lane = lax.broadcasted_iota(jnp.int32, (t, LANE), 1)
        dmat = jnp.zeros((t, LANE), F32)
        for h in range(N_HEADS):
            dmat = jnp.where(lane == h, jnp.sum(prod[:, 128 * h:128 * h + 128], axis=1, keepdims=True), dmat)
        dmat_t = jnp.transpose(dmat)
        for h in range(N_HEADS):
            dl_ref[h, pl.ds(i, 1), :] = dmat_t[h:h + 1, :]

    row = pl.BlockSpec((t, D_MLA), lambda i: (i, 0))
    return _pcall(kern, name=name, out_shape=jax.ShapeDtypeStruct((N_HEADS, n, t), F32), grid=(n,),
                  in_specs=[row, row], out_specs=pl.BlockSpec((N_HEADS, n, t), lambda i: (0, 0, 0)),
                  dims=("arbitrary",), vmem_mb=48)(o, do)


def _flash_bwd(qc, kc, v, do, lse2, delta, cos_t, sin_t, *, name, comm=None):
    S = qc.shape[0]
    t = min(TQ, S)
    n = S // t

    def kern(q_ref, k_ref, v_ref, do_ref, lse_ref, dl_ref, cos_ref, sin_ref, dqb_ref, dkvb_ref, dkr_ref,
             dq_ref, dk_ref, dv_ref):
        ki = pl.program_id(1)

        @pl.when(ki == 0)
        def _():
            dq_ref[...] = jnp.zeros_like(dq_ref)

        dk_ref[...] = jnp.zeros_like(dk_ref)
        dv_ref[...] = jnp.zeros_like(dv_ref)

        half = t // 2

        def step(qb, q_lo=0, q_n=t, k_n=t, diagonal=False):
            q0 = pl.multiple_of(qb * t + q_lo, half)
            lanes = slice(q_lo, q_lo + q_n)
            kt = k_ref[0:k_n, :]
            qblk = q_ref[pl.ds(q0, q_n), :]
            dob = do_ref[pl.ds(q0, q_n), :].astype(BF16)
            st = lax.dot_general(kt, qblk, NT, preferred_element_type=F32)
            pt = jnp.exp2(st * SCALE_LOG2E - lse_ref[pl.ds(qb, 1), lanes])
            if diagonal:
                krow = lax.broadcasted_iota(jnp.int32, (k_n, q_n), 0)
                qcol = lax.broadcasted_iota(jnp.int32, (k_n, q_n), 1) + q_lo
                pt = jnp.where(krow <= qcol, pt, 0.0)
            dv_ref[0:k_n, :] += jnp.dot(pt.astype(BF16), dob, preferred_element_type=F32)
            dpt = lax.dot_general(v_ref[0:k_n, :], dob, NT, preferred_element_type=F32)
            dst = (pt * (dpt - dl_ref[pl.ds(qb, 1), lanes]) * SCALE).astype(BF16)
            dk_ref[0:k_n, :] += jnp.dot(dst, qblk, preferred_element_type=F32)
            dq_ref[pl.ds(q0, q_n), :] += lax.dot_general(dst, kt, TN, preferred_element_type=F32)

        step(ki, 0, half, half, True)
        step(ki, half, half, t, True)
        rest = n - 1 - ki

        def body(i, carry):
            step(ki + 1 + 2 * i)
            step(ki + 2 + 2 * i)
            return carry

        lax.fori_loop(0, rest // 2, body, 0)

        @pl.when(rest % 2 == 1)
        def _():
            step(n - 1)

        dkvb_ref[:, 0:128] = dk_ref[:, 0:128].astype(BF16)
        dkvb_ref[:, 128:256] = dv_ref[...].astype(BF16)
        dkr_ref[...] = dk_ref[:, 128:256]

        @pl.when(ki == n - 1)
        def _():
            dqb_ref[:, 0:128] = dq_ref[:, 0:128].astype(BF16)
            dqr = dq_ref[:, 128:256]
            dqb_ref[:, 128:256] = (dqr * cos_ref[...] - _rot_sum(dqr) * sin_ref[...]).astype(BF16)

    def whole(w):
        return pl.BlockSpec((S, w), lambda h, ki: (0, h))

    def krow(w):
        return pl.BlockSpec((t, w), lambda h, ki: (ki, h))

    stat = pl.BlockSpec((None, n, t), lambda h, ki: (h, 0, 0))
    table = pl.BlockSpec((S, 128), lambda h, ki: (0, 0))
    return _pcall(kern, name=name,
                  out_shape=(jax.ShapeDtypeStruct((S, 2048), BF16), jax.ShapeDtypeStruct((S, 2048), BF16),
                             jax.ShapeDtypeStruct((S, D_MLA), F32)),
                  grid=(N_HEADS, n),
                  in_specs=[whole(256), krow(256), krow(128), whole(128), stat, stat, table, table],
                  out_specs=(whole(256), krow(256), krow(128)),
                  scratch=[pltpu.VMEM((S, 256), F32), pltpu.VMEM((t, 256), F32), pltpu.VMEM((t, 128), F32)],
                  dims=("parallel", "arbitrary"), vmem_mb=56, comm=comm)(qc, kc, v, do, lse2, delta, cos_t, sin_t)


def _mixer_specs(S, tm):
    hb = tm // HALO
    last_hb = S // HALO - 1

    def main(w, blk):
        return pl.BlockSpec((tm, w), lambda i: (i, blk))

    def prev(w, blk):
        return pl.BlockSpec((HALO, w), lambda i: (jnp.maximum(i * hb - 1, 0), blk))

    def nxt(w, blk):
        return pl.BlockSpec((HALO, w), lambda i: (jnp.minimum((i + 1) * hb, last_hb), blk))

    def full(shape):
        return pl.BlockSpec(shape, lambda i: (0,) * len(shape))

    return main, prev, nxt, full


def _fill_halo(i, xp, xu, hp_ref, hch_ref, hcc_ref, pin_ref, ch_ref, cc_ref, tm):
    first = i == 0
    xp[0:HALO, :] = jnp.where(first, 0.0, hp_ref[...])
    xp[HALO:HALO + tm, :] = pin_ref[...]
    xu[0:HALO, :] = jnp.where(first, 0.0, hch_ref[...] * hcc_ref[...])
    xu[HALO:HALO + tm, :] = cc_ref[...] * ch_ref[...]


def _pooled(xp, g, t1, tm):
    w = POOL_WINDOWS[g]
    lanes = slice(128 * g, 128 * g + 128)
    x0 = xp[HALO:HALO + tm, lanes]
    acc = x0
    for k in range(1, w):
        acc = acc + xp[HALO - k:HALO - k + tm, lanes]
    return acc / jnp.minimum(t1, float(w)) - x0


def _conv_fwd(xu, cw_ref, tm):
    return (cw_ref[0:1, :] * xu[HALO - 2:HALO - 2 + tm, :] + cw_ref[1:2, :] * xu[HALO - 1:HALO - 1 + tm, :]
            + cw_ref[2:3, :] * xu[HALO:HALO + tm, :])


def _mixer_fwd(proj, o, wpool, ps, convw, *, name):
    S = proj.shape[0]
    tm = min(256, S)
    main, prev, _, full = _mixer_specs(S, tm)

    def kern(gm_ref, pin_ref, gp_ref, ch_ref, cb_ref, cc_ref, gc_ref, hp_ref, hch_ref, hcc_ref,
             o_ref, wp_ref, ps_ref, cw_ref, mix_ref, xp, xu):
        i = pl.program_id(0)
        _fill_halo(i, xp, xu, hp_ref, hch_ref, hcc_ref, pin_ref, ch_ref, cc_ref, tm)
        t1 = (i * tm + lax.broadcasted_iota(jnp.int32, (tm, 1), 0) + 1).astype(F32)
        for g in range(4):
            lanes = slice(128 * g, 128 * g + 128)
            pooled = _pooled(xp, g, t1, tm)
            z = jnp.dot(pooled.astype(BF16), wp_ref[g].astype(BF16), preferred_element_type=F32)
            gp = gp_ref[:, lanes]
            y = z * ps_ref[:, lanes] * (gp * _sigmoid(gp))
            mix_ref[:, 1024 + 128 * g:1024 + 128 * g + 128] = y.astype(BF16)
        gc = gc_ref[...]
        mix_ref[:, 1536:2048] = (cb_ref[...] * _conv_fwd(xu, cw_ref, tm) * (gc * _sigmoid(gc))).astype(BF16)
        gm = gm_ref[...]
        mix_ref[:, 0:1024] = (o_ref[...] * (gm * _sigmoid(gm))).astype(BF16)

    return _pcall(kern, name=name, out_shape=jax.ShapeDtypeStruct((S, 2048), BF16), grid=(S // tm,),
                  in_specs=[main(1024, 1), main(512, 4), main(512, 5), main(512, 6), main(512, 7), main(512, 8),
                            main(512, 9), prev(512, 4), prev(512, 6), prev(512, 8),
                            main(1024, 0), full((4, 128, 128)), full((1, 512)), full((3, 512))],
                  out_specs=main(2048, 0),
                  scratch=[pltpu.VMEM((tm + HALO, 512), F32), pltpu.VMEM((tm + HALO, 512), F32)],
                  dims=("parallel",), vmem_mb=48)(
                      proj, proj, proj, proj, proj, proj, proj, proj, proj, proj, o, wpool, ps.reshape(1, 512), convw)


def _mixer_bwd(dmix, proj, o, wpool, ps, convw, *, name):
    S = proj.shape[0]
    tm = min(256, S)
    n = S // tm
    main, prev, nxt, full = _mixer_specs(S, tm)

    def kern(dm_ref, dmn_ref, gm_ref, pin_ref, gp_ref, ch_ref, cb_ref, cc_ref, gc_ref,
             hp_ref, hch_ref, hcc_ref, gpn_ref, cbn_ref, gcn_ref, o_ref, wp_ref, ps_ref, cw_ref,
             d_ref, do_ref, dwp_ref, dps_ref, dcw_ref, xp, xu, ee, ed):
        i = pl.program_id(0)
        last = i == n - 1

        @pl.when(i == 0)
        def _():
            dwp_ref[...] = jnp.zeros_like(dwp_ref)
            dps_ref[...] = jnp.zeros_like(dps_ref)
            dcw_ref[...] = jnp.zeros_like(dcw_ref)

        _fill_halo(i, xp, xu, hp_ref, hch_ref, hcc_ref, pin_ref, ch_ref, cc_ref, tm)
        t1 = (i * tm + lax.broadcasted_iota(jnp.int32, (tm, 1), 0) + 1).astype(F32)
        t1n = ((i + 1) * tm + lax.broadcasted_iota(jnp.int32, (HALO, 1), 0) + 1).astype(F32)
        c_pin, c_gp, c_ch, c_cb, c_cc, c_gc = 1024, 1536, 2048, 2560, 3072, 3584

        for g in range(4):
            w = float(POOL_WINDOWS[g])
            lanes = slice(128 * g, 128 * g + 128)
            pooled = _pooled(xp, g, t1, tm)
            pb = pooled.astype(BF16)
            wp = wp_ref[g].astype(BF16)
            z = jnp.dot(pb, wp, preferred_element_type=F32)
            psl = ps_ref[:, lanes]
            sg, dsg = _silu_and_grad(gp_ref[:, lanes])
            dmp = dm_ref[:, 1024 + 128 * g:1024 + 128 * g + 128]
            dyp = dmp * sg
            d_ref[:, c_gp + 128 * g:c_gp + 128 * g + 128] = (dmp * (z * psl) * dsg).astype(BF16)
            dps_ref[:, lanes] += jnp.sum(dyp * z, axis=0, keepdims=True)
            dz = (dyp * psl).astype(BF16)
            dwp_ref[g] += lax.dot_general(pb, dz, TN, preferred_element_type=F32)
            dpl = lax.dot_general(dz, wp, NT, preferred_element_type=F32)
            ee[0:tm, lanes] = dpl / jnp.minimum(t1, w)
            gpn = gpn_ref[:, lanes]
            dzn = (dmn_ref[:, lanes] * (gpn * _sigmoid(gpn)) * psl).astype(BF16)
            dpn = lax.dot_general(dzn, wp, NT, preferred_element_type=F32)
            ee[tm:tm + HALO, lanes] = jnp.where(last, 0.0, dpn / jnp.minimum(t1n, w))
            acc = ee[0:tm, lanes]
            for k in range(1, POOL_WINDOWS[g]):
                acc = acc + ee[k:k + tm, lanes]
            d_ref[:, c_pin + 128 * g:c_pin + 128 * g + 128] = (acc - dpl).astype(BF16)

        yc = _conv_fwd(xu, cw_ref, tm)
        sgc, dsgc = _silu_and_grad(gc_ref[...])
        cb = cb_ref[...]
        dmc = dm_ref[:, 1536:2048]
        d_ref[:, c_gc:c_gc + 512] = (dmc * cb * yc * dsgc).astype(BF16)
        d_ref[:, c_cb:c_cb + 512] = (dmc * yc * sgc).astype(BF16)
        dyc = dmc * cb * sgc
        ed[0:tm, :] = dyc
        gcn = gcn_ref[...]
        ed[tm:tm + HALO, :] = jnp.where(last, 0.0, dmn_ref[:, 512:1024] * cbn_ref[...] * (gcn * _sigmoid(gcn)))
        dcw_ref[0:1, :] += jnp.sum(dyc * xu[HALO - 2:HALO - 2 + tm, :], axis=0, keepdims=True)
        dcw_ref[1:2, :] += jnp.sum(dyc * xu[HALO - 1:HALO - 1 + tm, :], axis=0, keepdims=True)
        dcw_ref[2:3, :] += jnp.sum(dyc * xu[HALO:HALO + tm, :], axis=0, keepdims=True)
        du = cw_ref[2:3, :] * dyc + cw_ref[1:2, :] * ed[1:1 + tm, :] + cw_ref[0:1, :] * ed[2:2 + tm, :]
        d_ref[:, c_cc:c_cc + 512] = (du * ch_ref[...]).astype(BF16)
        d_ref[:, c_ch:c_ch + 512] = (du * cc_ref[...]).astype(BF16)

        sgm, dsgm = _silu_and_grad(gm_ref[...])
        dmm = dm_ref[:, 0:1024]
        do_ref[...] = dmm * sgm
        d_ref[:, 0:1024] = (dmm * o_ref[...] * dsgm).astype(BF16)

    outs = (jax.ShapeDtypeStruct((S, W_MIX), BF16), jax.ShapeDtypeStruct((S, 1024), F32),
            jax.ShapeDtypeStruct((4, 128, 128), F32), jax.ShapeDtypeStruct((1, 512), F32),
            jax.ShapeDtypeStruct((3, 512), F32))
    scr = [pltpu.VMEM((tm + HALO, 512), F32) for _ in range(4)]
    return _pcall(kern, name=name, out_shape=outs, grid=(n,),
                  in_specs=[main(2048, 0), nxt(1024, 1),
                            main(1024, 1), main(512, 4), main(512, 5), main(512, 6), main(512, 7), main(512, 8),
                            main(512, 9), prev(512, 4), prev(512, 6), prev(512, 8),
                            nxt(512, 5), nxt(512, 7), nxt(512, 9),
                            main(1024, 0), full((4, 128, 128)), full((1, 512)), full((3, 512))],
                  out_specs=(main(W_MIX, 0), main(1024, 0), full((4, 128, 128)), full((1, 512)), full((3, 512))),
                  scratch=scr, dims=("arbitrary",), vmem_mb=56)(
                      dmix, dmix, proj, proj, proj, proj, proj, proj, proj, proj, proj, proj, proj, proj, proj,
                      o, wpool, ps.reshape(1, 512), convw)


def _outproj_residual(mix, wout, h, bout, *, name):
    S, Dm = h.shape
    tm = min(256, S)

    def kern(mix_ref, w_ref, h_ref, bo_ref, r_ref):
        out = jnp.dot(mix_ref[...], w_ref[...], preferred_element_type=F32) + bo_ref[...]
        r_ref[...] = ALPHA * h_ref[...] + out

    row = pl.BlockSpec((tm, Dm), lambda i: (i, 0))
    vec = pl.BlockSpec((1, Dm), lambda i: (0, 0))
    wsp = pl.BlockSpec((Dm, Dm), lambda i: (0, 0))
    return _pcall(kern, name=name, out_shape=jax.ShapeDtypeStruct((S, Dm), F32), grid=(S // tm,),
                  in_specs=[row, wsp, row, vec], out_specs=row, dims=("parallel",), vmem_mb=56)(
                      mix, wout, h, bout.reshape(1, Dm))


def _outproj_ln(mix, wout, h, bout, g, b, *, name):
    S, Dm = h.shape
    tm = min(256, S)

    def kern(mix_ref, w_ref, h_ref, bo_ref, g_ref, b_ref, y_ref, yb_ref, r_ref):
        out = jnp.dot(mix_ref[...], w_ref[...], preferred_element_type=F32) + bo_ref[...]
        r = ALPHA * h_ref[...] + out
        r_ref[...] = r
        mu = jnp.mean(r, axis=-1, keepdims=True)
        xc = r - mu
        var = jnp.mean(xc * xc, axis=-1, keepdims=True)
        y = xc * lax.rsqrt(var + LN_EPS) * g_ref[...] + b_ref[...]
        y_ref[...] = y
        yb_ref[...] = y.astype(BF16)

    row = pl.BlockSpec((tm, Dm), lambda i: (i, 0))
    vec = pl.BlockSpec((1, Dm), lambda i: (0, 0))
    wsp = pl.BlockSpec((Dm, Dm), lambda i: (0, 0))
    sds = jax.ShapeDtypeStruct((S, Dm), F32)
    return _pcall(kern, name=name, out_shape=(sds, jax.ShapeDtypeStruct((S, Dm), BF16), sds), grid=(S // tm,),
                  in_specs=[row, wsp, row, vec, vec, vec], out_specs=(row, row, row), dims=("parallel",),
                  vmem_mb=56)(
                      mix, wout, h, bout.reshape(1, Dm), g.reshape(1, Dm), b.reshape(1, Dm))


def _adamw_math(w, g, m, v):
    m = ADAM_B1 * m + (1.0 - ADAM_B1) * g
    v = ADAM_B2 * v + (1.0 - ADAM_B2) * (g * g)
    m_hat = m / (1.0 - ADAM_B1 ** ADAM_STEP)
    v_hat = v / (1.0 - ADAM_B2 ** ADAM_STEP)
    delta = -ADAM_LR * (m_hat / (jnp.sqrt(v_hat) + ADAM_EPS) + ADAM_WD * w)
    return delta, m, v


def _row_tile(R, C):
    best = None
    for cand in range(8, R, 8):
        if R % cand == 0 and cand * C <= 256 * 1024:
            best = cand
    return best if best is not None else R


def _adamw(w, g, m, v, *, name):
    shape = w.shape
    C = shape[-1]
    R = 1
    for s in shape[:-1]:
        R *= s
    tr = _row_tile(R, C)

    def kern(w_ref, g_ref, m_ref, v_ref, d_ref, mo_ref, vo_ref):
        d, mn, vn = _adamw_math(w_ref[...], g_ref[...], m_ref[...], v_ref[...])
        d_ref[...] = d
        mo_ref[...] = mn
        vo_ref[...] = vn

    blk = pl.BlockSpec((tr, C), lambda i: (i, 0))
    sds = jax.ShapeDtypeStruct((R, C), F32)
    outs = _pcall(kern, name=name, out_shape=(sds, sds, sds), grid=(R // tr,), in_specs=[blk] * 4,
                  out_specs=(blk, blk, blk), dims=("parallel",), vmem_mb=48)(
                      w.reshape(R, C), g.reshape(R, C), m.reshape(R, C), v.reshape(R, C))
    return tuple(t.reshape(shape) for t in outs)


def _adamw_halves(w, m, v, halves, c_idx, *, name, comm=None):
    _, R, C = w.shape
    ch = C // 2
    tr = _row_tile(R, ch)
    nb = R // tr

    def kern(c_ref, w_ref, a0_ref, b0_ref, a1_ref, b1_ref, m_ref, v_ref, g_ref, d_ref, mo_ref, vo_ref):
        layer = pl.program_id(0) // nb
        mine = pl.program_id(1) == c_ref[0]
        g = jnp.where(layer == 0, jnp.where(mine, a0_ref[...], b0_ref[...]),
                      jnp.where(mine, a1_ref[...], b1_ref[...]))
        g_ref[...] = g
        d, mn, vn = _adamw_math(w_ref[...], g, m_ref[...], v_ref[...])
        d_ref[...] = d
        mo_ref[...] = mn
        vo_ref[...] = vn

    full = pl.BlockSpec((tr, ch), lambda i, hc: (i, hc))
    half = pl.BlockSpec((tr, ch), lambda i, hc: (i % nb, 0))
    sds = jax.ShapeDtypeStruct((2 * R, C), F32)
    (a0, b0), (a1, b1) = halves
    res = _pcall(kern, name=name, out_shape=(sds,) * 4, grid=(2 * nb, 2),
                 in_specs=[pl.BlockSpec(memory_space=pltpu.SMEM), full, half, half, half, half, full, full],
                 out_specs=(full,) * 4, dims=("parallel", "parallel"), vmem_mb=48, comm=comm)(
                     c_idx, w.reshape(2 * R, C), a0, b0, a1, b1, m.reshape(2 * R, C), v.reshape(2 * R, C))
    outs, landed = res if comm is not None else (res, None)
    outs = tuple(t.reshape(2, R, C) for t in outs)
    return outs if comm is None else (outs, landed)


def _packed_pieces(shape):
    if len(shape) == 4:
        return [((l * shape[1] + g) * 128, 128, (l, g)) for l in range(shape[0]) for g in range(shape[1])]
    per_row = shape[1] // LANE
    return [(a * per_row + j, 1, (slice(a, a + 1), slice(LANE * j, LANE * (j + 1))))
            for a in range(shape[0]) for j in range(per_row)]


def _small_sum_adamw(gathered, own, weights, *, name):
    R = gathered.shape[1]
    nw = len(weights)
    shapes = [w.shape for w, _, _ in weights]
    first_row, r0 = [], 0
    for shp in shapes:
        first_row.append(r0)
        n = 1
        for s in shp:
            n *= s
        r0 += n // LANE

    def kern(ga_ref, own_ref, *refs):
        ins, gsum_ref, outs = refs[:3 * nw], refs[3 * nw], refs[3 * nw + 1:]
        me = 4 * lax.axis_index("x") + 2 * lax.axis_index("y") + lax.axis_index("c")

        def block(k):
            other = ga_ref[jnp.where(me == k, (k + 1) % N_DEV, k)]
            return jnp.where(me == k, own_ref[...], other)

        g = block(0)
        for k in range(1, N_DEV):
            g = g + block(k)
        gsum_ref[...] = g
        for p, shp in enumerate(shapes):
            w_ref, m_ref, v_ref = ins[3 * p:3 * p + 3]
            g_out, d_out, m_out, v_out = outs[4 * p:4 * p + 4]
            for row, rows, idx in _packed_pieces(shp):
                gp = gsum_ref[first_row[p] + row:first_row[p] + row + rows, :]
                d, mn, vn = _adamw_math(w_ref[idx], gp, m_ref[idx], v_ref[idx])
                g_out[idx] = gp
                d_out[idx] = d
                m_out[idx] = mn
                v_out[idx] = vn

    out_shape = [jax.ShapeDtypeStruct((R, LANE), F32)]
    for shp in shapes:
        out_shape += [jax.ShapeDtypeStruct(shp, F32)] * 4
    flat = [a for wmv in weights for a in wmv]
    res = _pcall(kern, name=name, out_shape=tuple(out_shape), vmem_mb=48)(gathered, own, *flat)
    return res[0], [tuple(res[1 + 4 * p:5 + 4 * p]) for p in range(nw)]


def _pair_sum(g, theirs, c_idx, *, name):
    R, C = g.shape
    ch = C // 2
    tr = _row_tile(R, ch)

    def kern(c_ref, a_ref, b_ref, o_ref):
        o_ref[...] = (a_ref[...] + b_ref[...]).astype(BF16)

    gs = pltpu.PrefetchScalarGridSpec(
        num_scalar_prefetch=1, grid=(R // tr,),
        in_specs=[pl.BlockSpec((tr, ch), lambda i, c: (i, c[0])), pl.BlockSpec((tr, ch), lambda i, c: (i, 0))],
        out_specs=pl.BlockSpec((tr, ch), lambda i, c: (i, 0)))
    return pl.pallas_call(kern, name=name, out_shape=jax.ShapeDtypeStruct((R, ch), BF16), grid_spec=gs,
                          compiler_params=pltpu.CompilerParams(dimension_semantics=("parallel",),
                                                               vmem_limit_bytes=48 << 20))(c_idx, g, theirs)


WeightRows = collections.namedtuple("WeightRows", "full_rows own_rows cols pieces zero_rows")


def _w_in_piece_a(j):
    return jnp.where(j == 0, 0, 1232 * j + GAP)


def _w_in_piece_b(j):
    return jnp.where(j == 0, GAP_AT + GAP, 1232 * j + GAP_AT + GAP)


W_IN = WeightRows(NP, 1232, D_MODEL, ((0, GAP_AT, _w_in_piece_a), (GAP_AT, 1232 - GAP_AT, _w_in_piece_b)),
                  ((GAP_AT, GAP),))
W_OUT = WeightRows(2048, 512, D_MODEL, ((0, 512, lambda j: 512 * j),), ())
W_UQ = WeightRows(2048, 384, Q_LORA, ((0, 192, lambda j: 512 * j), (192, 192, lambda j: 512 * j + 256)),
                  tuple((256 * h + 192, 64) for h in range(N_HEADS)))
W_UKV = WeightRows(2048, 512, KV_LORA, ((0, 512, lambda j: 512 * j),), ())
W_CONV = WeightRows(64, 16, 256, ((0, 16, lambda j: 16 * j),), ())
SHARDED = (W_IN, W_OUT, W_UQ, W_UKV)
SHARDED_NAMES = ("w_in", "w_out", "w_uq", "w_ukv")
WEIGHT_ROWS = dict(zip(SHARDED_NAMES, SHARDED))


def _mesh_pos():
    x, y, c = lax.axis_index("x"), lax.axis_index("y"), lax.axis_index("c")
    return x, y, c


def _other_chips(x, y):
    return [(1 - x, y), (x, 1 - y), (1 - x, 1 - y)]


def _rows(start, n):
    return pl.ds(pl.multiple_of(start, 16), n)


def _half_cols(spec, c):
    ch = spec.cols // 2
    return pl.ds(pl.multiple_of(c * ch, LANE), ch)


def _allgather_script(specs, shards, zeros):
    na = len(specs)
    zlist = [a for a in range(na) if zeros[a] is not None]
    plan_first, plan_own, plan_zero = [], [], []
    for a, spec in enumerate(specs):
        for p in range(len(spec.pieces)):
            plan_own.append((a, p))
            for k in range(3):
                plan_first.append((a, p, k))
        for z in range(len(spec.zero_rows)):
            for l in range(shards[a].shape[0]):
                plan_zero.append((a, z, l))
    nf = len(plan_first)
    n_sems = 2 * nf + len(plan_own) + len(plan_zero)

    def copies(ins_all, outs, send_sems, recv_sems):
        ins = ins_all[:na]
        zrefs = dict(zip(zlist, ins_all[na:]))
        x, y, c = _mesh_pos()
        j = 2 * x + y
        chips = _other_chips(x, y)
        sibling = (x, y, 1 - c)

        def remote(src, dst, sem, to):
            return pltpu.make_async_remote_copy(src_ref=src, dst_ref=dst, send_sem=send_sems.at[sem],
                                                recv_sem=recv_sems.at[sem], device_id=to, device_id_type=MESH)

        def block(a, p, chip, cols):
            _, n, dst = specs[a].pieces[p]
            return outs[a].at[:, _rows(dst(chip), n), cols]

        def first(i):
            a, p, k = plan_first[i]
            src0, n, _ = specs[a].pieces[p]
            cols = _half_cols(specs[a], c)
            return remote(ins[a].at[:, pl.ds(src0, n), cols], block(a, p, j, cols), i, (*chips[k], c))

        def landed(i, half):
            a, p, k = plan_first[i]
            return block(a, p, 2 * chips[k][0] + chips[k][1], _half_cols(specs[a], half))

        def arrival(i, half, sem):
            return remote(landed(i, half), landed(i, half), sem, sibling)

        def passed(i):
            return remote(landed(i, c), landed(i, c), nf + i, sibling)

        def own(i):
            a, p = plan_own[i]
            src0, n, _ = specs[a].pieces[p]
            return remote(ins[a].at[:, pl.ds(src0, n), :], block(a, p, j, slice(None)), 2 * nf + i, sibling)

        def zero(i):
            a, z, l = plan_zero[i]
            r0, n = specs[a].zero_rows[z]
            return remote(zrefs[a].at[pl.ds(0, n), :], outs[a].at[l, pl.ds(r0, n), :],
                          2 * nf + len(plan_own) + i, sibling)

        fixed = [own(i) for i in range(len(plan_own))] + [zero(i) for i in range(len(plan_zero))]
        return c, fixed, first, arrival, passed

    def start(ins, outs, send_sems, recv_sems):
        _, fixed, first, _, _ = copies(ins, outs, send_sems, recv_sems)
        for cp in fixed:
            cp.start()
        for i in range(nf):
            first(i).start()

    def finish(ins, outs, send_sems, recv_sems):
        c, fixed, first, arrival, passed = copies(ins, outs, send_sems, recv_sems)
        for i in range(nf):
            arrival(i, c, i).wait_recv()
            passed(i).start()
        for i in range(nf):
            arrival(i, 1 - c, nf + i).wait_recv()
        for cp in fixed:
            cp.wait()
        for i in range(nf):
            first(i).wait_send()
            passed(i).wait_send()

    out_shape = tuple(jax.ShapeDtypeStruct((shards[a].shape[0], spec.full_rows, spec.cols), BF16)
                      for a, spec in enumerate(specs))
    args = tuple(shards) + tuple(zeros[a] for a in zlist)
    return CommScript(args, out_shape, n_sems, start, finish)


def _start_all_wait_all(args, out_shape, n_sems, make_copies):
    def start(ins, outs, send_sems, recv_sems):
        for cp in make_copies(ins, outs, send_sems, recv_sems):
            cp.start()

    def finish(ins, outs, send_sems, recv_sems):
        for cp in make_copies(ins, outs, send_sems, recv_sems):
            cp.wait()

    return CommScript(tuple(args), tuple(out_shape), n_sems, start, finish)


def _exchange_script(specs, grads):
    na = len(grads)

    def make_copies(ins, outs, send_sems, recv_sems):
        x, y, c = _mesh_pos()
        return [pltpu.make_async_remote_copy(
            src_ref=ins[a].at[:, _half_cols(specs[a], 1 - c)], dst_ref=outs[a], send_sem=send_sems.at[a],
            recv_sem=recv_sems.at[a], device_id=(x, y, 1 - c), device_id_type=MESH) for a in range(na)]

    out_shape = [jax.ShapeDtypeStruct((s.full_rows, s.cols // 2), F32) for s in specs]
    return _start_all_wait_all(grads, out_shape, na, make_copies)


def _scatter_script(specs, parts):
    na = len(parts)
    plan = [(a, p, k) for a in range(na) for p in range(len(specs[a].pieces)) for k in range(3)]

    def make_copies(ins, outs, send_sems, recv_sems):
        x, y, c = _mesh_pos()
        chips = _other_chips(x, y)
        copies = []
        for i, (a, p, k) in enumerate(plan):
            src0, n, dst = specs[a].pieces[p]
            pk = 2 * chips[k][0] + chips[k][1]
            copies.append(pltpu.make_async_remote_copy(
                src_ref=ins[a].at[_rows(dst(pk), n), :], dst_ref=outs[a].at[k, pl.ds(src0, n), :],
                send_sem=send_sems.at[i], recv_sem=recv_sems.at[i], device_id=(*chips[k], c), device_id_type=MESH))
        return copies

    out_shape = [jax.ShapeDtypeStruct((3, s.own_rows, s.cols // 2), BF16) for s in specs]
    return _start_all_wait_all(parts, out_shape, len(plan), make_copies)


def _chip_sum(spec, part, recv, *, name):
    ch = spec.cols // 2
    npieces = len(spec.pieces)

    def kern(recv_ref, part_ref, o_ref, own_ref, sems):
        j = 2 * lax.axis_index("x") + lax.axis_index("y")
        copies = []
        for p, (src0, n, dst) in enumerate(spec.pieces):
            copies.append(pltpu.make_async_copy(part_ref.at[_rows(dst(j), n), :], own_ref.at[pl.ds(src0, n), :],
                                                sems.at[p]))
        for cp in copies:
            cp.start()
        for cp in copies:
            cp.wait()
        o_ref[...] = ((own_ref[...].astype(F32) + recv_ref[0].astype(F32)) + recv_ref[1].astype(F32)) \
            + recv_ref[2].astype(F32)

    vm = pl.BlockSpec(memory_space=pltpu.VMEM)
    return _pcall(kern, name=name, out_shape=jax.ShapeDtypeStruct((spec.own_rows, ch), F32),
                  in_specs=[vm, HBM_SPEC], out_specs=vm,
                  scratch=[pltpu.VMEM((spec.own_rows, ch), BF16), pltpu.SemaphoreType.DMA((npieces,))],
                  vmem_mb=48)(recv, part)


def _sibling_script(sums):
    na = len(sums)

    def make_copies(ins, outs, send_sems, recv_sems):
        x, y, c = _mesh_pos()
        return [pltpu.make_async_remote_copy(
            src_ref=ins[a], dst_ref=outs[a], send_sem=send_sems.at[a], recv_sem=recv_sems.at[a],
            device_id=(x, y, 1 - c), device_id_type=MESH) for a in range(na)]

    out_shape = [jax.ShapeDtypeStruct(t.shape, t.dtype) for t in sums]
    return _start_all_wait_all(sums, out_shape, na, make_copies)


class _SemWindow:
    def __init__(self, sems, offset):
        self._sems, self._offset = sems, offset

    @property
    def at(self):
        return self

    def __getitem__(self, i):
        return self._sems.at[i + self._offset]


def _merge_scripts(*scripts):
    a_off, o_off, s_off = [0], [0], [0]
    for s in scripts:
        a_off.append(a_off[-1] + len(s.args))
        o_off.append(o_off[-1] + len(s.out_shape))
        s_off.append(s_off[-1] + s.n_sems)

    def phase(which):
        def run(ins, outs, send_sems, recv_sems):
            for n, s in enumerate(scripts):
                getattr(s, which)(ins[a_off[n]:a_off[n + 1]], outs[o_off[n]:o_off[n + 1]],
                                  _SemWindow(send_sems, s_off[n]), _SemWindow(recv_sems, s_off[n]))
        return run

    return CommScript(sum((tuple(s.args) for s in scripts), ()), sum((tuple(s.out_shape) for s in scripts), ()),
                      s_off[-1], phase("start"), phase("finish"))


class _GradReducer:
    def __init__(self, layer, names, grads, c_idx):
        self.specs = tuple(WEIGHT_ROWS[nm] for nm in names)
        self.grads, self.c_idx = tuple(grads), c_idx
        self.names = [f"{nm}{layer}" for nm in names]

    def exchange(self):
        return _exchange_script(self.specs, self.grads)

    def scatter(self, theirs):
        self.parts = tuple(_pair_sum(g, th, self.c_idx, name=f"pair_sum_{nm}")
                           for g, th, nm in zip(self.grads, theirs, self.names))
        return _scatter_script(self.specs, self.parts)

    def sibling(self, recv):
        self.sums = tuple(_chip_sum(s, p, r, name=f"chip_sum_{nm}")
                          for s, p, r, nm in zip(self.specs, self.parts, recv, self.names))
        return _sibling_script(self.sums)

    def done(self, others):
        return list(zip(self.sums, others))


def _allgather_small_script(block):
    m_per, n = block.shape

    def copies(ins, outs, send_sems, recv_sems):
        (x_ref,), (out_ref,) = ins, outs
        x, y, c = _mesh_pos()
        me, sibling = (x, y, c), (x, y, 1 - c)
        chips = _other_chips(x, y)

        def rows(px, py, pc):
            return out_ref.at[4 * px + 2 * py + pc]

        def copy(k, blk, to, src=None):
            return pltpu.make_async_remote_copy(
                src_ref=rows(*blk) if src is None else src, dst_ref=rows(*blk), send_sem=send_sems.at[k],
                recv_sem=recv_sems.at[k], device_id=to, device_id_type=MESH)

        first = [copy(0, me, sibling, src=x_ref)]
        first += [copy(1 + k, me, (*chip, c), src=x_ref) for k, chip in enumerate(chips)]
        passed = [copy(4 + k, (*chip, c), sibling) for k, chip in enumerate(chips)]
        landed = [copy(1 + k, (*chip, c), me) for k, chip in enumerate(chips)]
        from_sibling = [copy(0, sibling, me)] + [copy(4 + k, (*chip, 1 - c), me) for k, chip in enumerate(chips)]
        return first, passed, landed, from_sibling

    def start(ins, outs, send_sems, recv_sems):
        first, _, _, _ = copies(ins, outs, send_sems, recv_sems)
        for cp in first:
            cp.start()

    def finish(ins, outs, send_sems, recv_sems):
        first, passed, landed, from_sibling = copies(ins, outs, send_sems, recv_sems)
        for k in range(3):
            landed[k].wait_recv()
            passed[k].start()
        for cp in from_sibling:
            cp.wait_recv()
        for cp in first + passed:
            cp.wait_send()

    return CommScript((block,), (jax.ShapeDtypeStruct((N_DEV, m_per, n), block.dtype),), 7, start, finish)


def _rope_tables(positions):
    half = ROPE // 2
    inv_freq = ROPE_THETA ** (-jnp.arange(half, dtype=F32) / half)
    ang = positions.astype(F32)[:, None] * inv_freq
    cos, sin = jnp.cos(ang), jnp.sin(ang)
    S = positions.shape[0]
    cos_t = jnp.concatenate([cos, cos, jnp.ones((S, 64), F32)], axis=1)
    sin_t = jnp.concatenate([-sin, sin, jnp.zeros((S, 64), F32)], axis=1)
    return cos_t, sin_t


def _decode_conv(bits):
    rows = bits.reshape(DEPTH, N_CHIPS, 16, 256)[:, :, :3, :]
    conv = lax.bitcast_convert_type(rows.reshape(DEPTH, N_CHIPS, 3, 128, 2), F32)
    return jnp.transpose(conv, (0, 2, 1, 3)).reshape(DEPTH, 3, 512)


def _local_step(x, positions, target, emb_g, emb_b, w_in_t0, rest0, weights1, q_g, kv_g, w_pool, pool_scale,
                b_out, ln_g, ln_b, c_idx=None):
    cos_t, sin_t = _rope_tables(positions)
    if isinstance(w_in_t0, CommScript):
        (h, hb), (landed,) = _ln_fwd(x, emb_g, emb_b, name="emb_ln", comm=w_in_t0)
        w_in_t0 = landed[0]
    else:
        h, hb = _ln_fwd(x, emb_g, emb_b, name="emb_ln")
    weights = [None, weights1]
    saved = []
    for l in range(DEPTH):
        if l == 0 and isinstance(rest0, CommScript):
            proj, landed = _matmul(hb, w_in_t0, "nt", name="in_proj0", tm=1024, tn=1024, tk=2048, vmem_mb=56,
                                   comm=rest0)
            weights[0] = (w_in_t0,) + tuple(a[0] for a in landed[:3])
            conv_w = _decode_conv(landed[3])
        else:
            if l == 0:
                weights[0] = (w_in_t0,) + tuple(rest0[:3])
                conv_w = rest0[3]
            proj = _matmul(hb, weights[l][0], "nt", name=f"in_proj{l}", tm=1024, tn=1024, tk=2048, vmem_mb=56)
        w_in_t, w_out, w_uq_t, w_ukv_t = weights[l]
        qc, kc, v, vt, qn, kvn = _mla_qkv(proj, cos_t, sin_t, q_g[l], kv_g[l], w_uq_t, w_ukv_t, name=f"mla_qkv{l}")
        nxt = weights[l + 1] if l + 1 < DEPTH else None
        if isinstance(nxt, CommScript):
            (o, lse2), landed = _flash_fwd(qc, kc, vt, name=f"flash_fwd{l}", comm=nxt)
            weights[l + 1] = tuple(a[0] for a in landed)
        else:
            o, lse2 = _flash_fwd(qc, kc, vt, name=f"flash_fwd{l}")
        mix = _mixer_fwd(proj, o, w_pool[l], pool_scale[l], conv_w[l], name=f"mixer_fwd{l}")
        if l == DEPTH - 1:
            r = _outproj_residual(mix, w_out, h, b_out[l], name=f"out_proj{l}")
            saved.append((hb, proj, qc, kc, v, qn, kvn, o, lse2, mix, r))
        else:
            h_next, hb_next, r = _outproj_ln(mix, w_out, h, b_out[l], ln_g[l], ln_b[l], name=f"out_proj_ln{l}")
            saved.append((hb, proj, qc, kc, v, qn, kvn, o, lse2, mix, r))
            h, hb = h_next, hb_next

    small = [None] * DEPTH
    big = [None] * DEPTH
    above = scatter_above = None
    for l in reversed(range(DEPTH)):
        w_in_t, w_out, w_uq_t, w_ukv_t = weights[l]
        hb_in, proj, qc, kc, v, qn, kvn, o, lse2, mix, r = saved[l]
        if l == DEPTH - 1:
            loss_acc, dr, drb, d_ln_g, d_ln_b, d_b_out = _loss_ln_bwd(target, r, ln_g[l], ln_b[l], name="loss_ln_bwd")
        else:
            dr, drb, d_ln_g, d_ln_b, d_b_out = _ln_bwd(dh, r, ln_g[l], name=f"ln_bwd{l}")
        dmix = _matmul(drb, w_out, "nt", name=f"dmix{l}", tm=1024, tn=1024, tk=2048, vmem_mb=56)
        d_w_out = _matmul(mix, drb, "tn", name=f"dw_out{l}", tm=1024, tn=1024, tk=2048, vmem_mb=56)
        d_mix, do, d_w_pool, d_ps, d_conv = _mixer_bwd(dmix, proj, o, w_pool[l], pool_scale[l], conv_w[l],
                                                       name=f"mixer_bwd{l}")
        delta = _attn_delta(o, do, name=f"attn_delta{l}")
        if above is not None:
            (dqb, dkvb, dkr), recv = _flash_bwd(qc, kc, v, do, lse2, delta, cos_t, sin_t, name=f"flash_bwd{l}",
                                                comm=scatter_above)
            sibling_above = above.sibling(recv)
        else:
            dqb, dkvb, dkr = _flash_bwd(qc, kc, v, do, lse2, delta, cos_t, sin_t, name=f"flash_bwd{l}")
        d_mla, d_qg, d_kvg = _mla_qkv_bwd(dqb, dkvb, dkr, proj, cos_t, sin_t, q_g[l], kv_g[l], w_uq_t, w_ukv_t,
                                          name=f"mla_qkv_bwd{l}")
        d_w_uq_t = _matmul(dqb, qn, "tn", name=f"dw_uq{l}", tm=2048, tn=512, tk=2048, vmem_mb=56)
        d_w_ukv_t = _matmul(dkvb, kvn, "tn", name=f"dw_ukv{l}", tm=2048, tn=256, tk=2048, vmem_mb=56)
        small[l] = dict(q_g=d_qg[0], kv_g=d_kvg[0], w_pool=d_w_pool, pool_scale=d_ps[0], conv_w=d_conv,
                        b_out=d_b_out[0], ln_g=d_ln_g[0], ln_b=d_ln_b[0])
        rest = (d_w_out, d_w_uq_t, d_w_ukv_t)
        if c_idx is None:
            d_w_in_t = _dproj_t_times_h(d_mla, d_mix, hb_in, name=f"dw_in{l}")
            dh = _dproj_times_w(d_mla, d_mix, w_in_t, dr, ALPHA, name=f"dh{l}")
            big[l] = (d_w_in_t,) + rest
        elif l > 0:
            d_w_in_t = _dproj_t_times_h(d_mla, d_mix, hb_in, name=f"dw_in{l}")
            above = _GradReducer(l, SHARDED_NAMES, (d_w_in_t,) + rest, c_idx)
            dh, theirs = _dproj_times_w(d_mla, d_mix, w_in_t, dr, ALPHA, name=f"dh{l}", comm=above.exchange())
            scatter_above = above.scatter(theirs)
        else:
            red_rest = _GradReducer(l, SHARDED_NAMES[1:], rest, c_idx)
            d_w_in_t, landed = _dproj_t_times_h(d_mla, d_mix, hb_in, name=f"dw_in{l}",
                                                comm=_merge_scripts(sibling_above, red_rest.exchange()))
            big[l + 1] = above.done(landed[:len(SHARDED)])
            red_in = _GradReducer(l, SHARDED_NAMES[:1], (d_w_in_t,), c_idx)
            landed = _run_comm(_merge_scripts(red_in.exchange(), red_rest.scatter(landed[len(SHARDED):])),
                               name="exchange_w_in0")
            sibling_rest = red_rest.sibling(landed[1:])
            packed_small = _pack_small(
                [jnp.stack([small[i][key] for i in range(DEPTH)]) for key in SMALL_LAYER_KEYS]
                + [jnp.pad(loss_acc[0, 0].reshape(1), (0, LANE - 1))], 0)
            dh, landed = _dproj_times_w(
                d_mla, d_mix, w_in_t, dr, ALPHA, name=f"dh{l}",
                comm=_merge_scripts(red_in.scatter(landed[:1]), sibling_rest, _allgather_small_script(packed_small)))
            recv_in, others_rest, gathered_small = landed[:1], landed[1:-1], landed[-1]
    grad_x, _, d_emb_g, d_emb_b, _ = _ln_bwd(dh, x, emb_g, name="emb_ln_bwd")
    if c_idx is None:
        return loss_acc[0, 0], grad_x, d_emb_g, d_emb_b, small, big
    others_in = _run_comm(red_in.sibling(recv_in), name="send_to_sibling0")
    big[0] = red_in.done(others_in) + red_rest.done(others_rest)
    return (packed_small, gathered_small), grad_x, d_emb_g, d_emb_b, small, big


SMALL_ORDER = ("emb_ln_g", "emb_ln_b", "q_norm_g", "kv_norm_g", "w_pool", "pool_scale", "b_out", "ln_g", "ln_b")
SMALL_LAYER_KEYS = ("q_g", "kv_g", "w_pool", "pool_scale", "b_out", "ln_g", "ln_b", "conv_w")


def _pack_small(arrs, extra_rows):
    flat = jnp.concatenate([a.reshape(-1) for a in arrs])
    rows = flat.shape[0] // LANE
    total = -(-(rows + extra_rows) // 8) * 8
    return jnp.pad(flat, (0, total * LANE - flat.shape[0])).reshape(total, LANE)


def kernel(x, positions, emb_ln_g, emb_ln_b, w_in, q_norm_g, kv_norm_g, w_uq, w_ukv, w_pool, pool_scale, conv_w, w_out, b_out, ln_g, ln_b, loss_target, m_emb_ln_g, m_emb_ln_b, m_w_in, m_q_norm_g, m_kv_norm_g, m_w_uq, m_w_ukv, m_w_pool, m_pool_scale, m_conv_w, m_w_out, m_b_out, m_ln_g, m_ln_b, v_emb_ln_g, v_emb_ln_b, v_w_in, v_q_norm_g, v_kv_norm_g, v_w_uq, v_w_ukv, v_w_pool, v_pool_scale, v_conv_w, v_w_out, v_b_out, v_ln_g, v_ln_b):
    xi, yi, ci = lax.axis_index("x"), lax.axis_index("y"), lax.axis_index("c")
    chip = 2 * xi + yi
    c_idx = ci.reshape(1).astype(jnp.int32)

    def t(a):
        return jnp.swapaxes(a, 1, 2)

    conv_bits = lax.bitcast_convert_type(conv_w.reshape(DEPTH, 3 * 128), BF16).reshape(DEPTH, 3, 256)
    conv_bits = jnp.pad(conv_bits, ((0, 0), (0, 13), (0, 0)))
    own = (t(w_in).astype(BF16), w_out.astype(BF16), t(w_uq).astype(BF16), t(w_ukv).astype(BF16))
    zeros = (jnp.zeros((GAP, D_MODEL), BF16), None, jnp.zeros((64, Q_LORA), BF16), None)
    gather_in0 = _allgather_script((W_IN,), (own[0][0:1],), zeros[:1])
    gather0 = _allgather_script(SHARDED[1:] + (W_CONV,), tuple(a[0:1] for a in own[1:]) + (conv_bits,),
                                zeros[1:] + (None,))
    gather1 = _allgather_script(SHARDED, tuple(a[1:2] for a in own), zeros)

    (packed_small, gathered_small), grad_x, d_emb_g, d_emb_b, _, reduced = _local_step(
        x[0], positions[0], loss_target[0], emb_ln_g, emb_ln_b, gather_in0, gather0, gather1, q_norm_g, kv_norm_g,
        w_pool, pool_scale, b_out, ln_g, ln_b, c_idx)

    def rows(a):
        return a.reshape(1, -1) if a.ndim == 1 else a

    small_wmv = [tuple(rows(a) for a in wmv) for wmv in (
        (emb_ln_g, m_emb_ln_g, v_emb_ln_g), (emb_ln_b, m_emb_ln_b, v_emb_ln_b),
        (q_norm_g, m_q_norm_g, v_q_norm_g), (kv_norm_g, m_kv_norm_g, v_kv_norm_g), (w_pool, m_w_pool, v_w_pool),
        (pool_scale, m_pool_scale, v_pool_scale), (b_out, m_b_out, v_b_out), (ln_g, m_ln_g, v_ln_g),
        (ln_b, m_ln_b, v_ln_b))]
    packed_emb = jnp.concatenate([d_emb_g.reshape(-1, LANE), d_emb_b.reshape(-1, LANE)])
    (gathered_emb,) = _run_comm(_allgather_small_script(packed_emb), name="allgather_emb")
    _, upd_emb = _small_sum_adamw(gathered_emb, packed_emb, small_wmv[:2], name="adamw_emb")
    g_tot, upd_layers = _small_sum_adamw(gathered_small, packed_small, small_wmv[2:], name="adamw_small")
    small_upd = upd_emb + upd_layers
    off = sum(w.size for w, _, _ in small_wmv[2:])
    flat_tot = g_tot.reshape(-1)

    def halves(a):
        return [reduced[l][a] for l in range(DEPTH)]

    upd = {}
    upd["w_in"] = tuple(t(o) for o in _adamw_halves(t(w_in), t(m_w_in), t(v_w_in), halves(0), c_idx,
                                                    name="adamw_w_in"))
    conv_tot = flat_tot[off:off + DEPTH * 3 * 512].reshape(DEPTH, 3, 512)
    loss = flat_tot[off + DEPTH * 3 * 512]
    g_conv = lax.dynamic_slice_in_dim(conv_tot, chip * 128, 128, axis=2)

    def whole(a):
        return jnp.stack([jnp.where(ci == 0, jnp.concatenate([mine, oth], axis=1),
                                    jnp.concatenate([oth, mine], axis=1)) for mine, oth in halves(a)])

    upd["w_out"] = _adamw_halves(w_out, m_w_out, v_w_out, halves(1), c_idx, name="adamw_w_out")
    g_uq, g_ukv = t(whole(2)), t(whole(3))
    upd["w_uq"] = (g_uq,) + _adamw(w_uq, g_uq, m_w_uq, v_w_uq, name="adamw_w_uq")
    upd["w_ukv"] = (g_ukv,) + _adamw(w_ukv, g_ukv, m_w_ukv, v_w_ukv, name="adamw_w_ukv")
    upd["conv_w"] = (g_conv,) + _adamw(conv_w, g_conv, m_conv_w, v_conv_w, name="adamw_conv_w")
    for nm, res in zip(SMALL_ORDER, small_upd):
        upd[nm] = tuple(a.reshape(-1) for a in res) if nm in ("emb_ln_g", "emb_ln_b") else res

    order = ("emb_ln_g", "emb_ln_b", "w_in", "q_norm_g", "kv_norm_g", "w_uq", "w_ukv", "w_pool", "pool_scale",
             "conv_w", "w_out", "b_out", "ln_g", "ln_b")
    outs = [loss, grad_x[None]]
    for field in range(4):
        outs += [upd[nm][field] for nm in order]
    return tuple(outs)
```

```python
import collections

import jax
import jax.numpy as jnp
from jax import lax
from jax.experimental import pallas as pl
from jax.experimental.pallas import tpu as pltpu

F32 = jnp.float32
BF16 = jnp.bfloat16
MESH = pl.DeviceIdType.MESH

D_MODEL = 2048
DEPTH = 2
N_HEADS = 8
NOPE = 128
ROPE = 64
Q_LORA = 512
KV_LORA = 256
D_MLA = 1024
POOL_WINDOWS = (2, 4, 8, 16)
D_IN_PROJ = 4928
LN_EPS = 1e-5
RMS_EPS = 1e-6
ROPE_THETA = 10000.0
ALPHA = (2 * DEPTH) ** 0.25
SCALE = (NOPE + ROPE) ** -0.5
LOG2E = 1.4426950408889634
SCALE_LOG2E = SCALE * LOG2E
ADAM_LR = 0.001
ADAM_B1 = 0.9
ADAM_B2 = 0.999
ADAM_EPS = 1e-08
ADAM_WD = 0.01
ADAM_STEP = 10

NP = 5120
GAP_AT = 832
GAP = NP - D_IN_PROJ
W_MLA = 1024
W_MIX = NP - W_MLA
HALO = 16
LANE = 128
N_CHIPS = 4
N_DEV = 8
TQ = 512
FWD_GROUP = 4

NN = (((1,), (0,)), ((), ()))
NT = (((1,), (1,)), ((), ()))
TN = (((0,), (0,)), ((), ()))


CommScript = collections.namedtuple("CommScript", "args out_shape n_sems start finish")
HBM_SPEC = pl.BlockSpec(memory_space=pl.ANY)


def _pcall(kern, *, name, out_shape, grid=None, in_specs=None, out_specs=None, scratch=(), dims=None,
           vmem_mb=None, comm=None):
    cp = {}
    if dims is not None:
        cp["dimension_semantics"] = dims if comm is None else ("arbitrary",) * len(dims)
    if vmem_mb is not None:
        cp["vmem_limit_bytes"] = vmem_mb << 20
    if comm is None:
        args = dict(name=name, out_shape=out_shape, scratch_shapes=list(scratch),
                    compiler_params=pltpu.CompilerParams(**cp))
        if grid is not None:
            args["grid"] = grid
        if in_specs is not None:
            args["in_specs"] = in_specs
        if out_specs is not None:
            args["out_specs"] = out_specs
        return pl.pallas_call(kern, **args)

    single = not isinstance(out_shape, (tuple, list))
    own_out = (out_shape,) if single else tuple(out_shape)
    own_out_specs = (out_specs,) if single else tuple(out_specs)
    n_in, n_out, n_scr = len(in_specs), len(own_out), len(scratch)
    na, no = len(comm.args), len(comm.out_shape)

    def at(end):
        cond = None
        for d, n in enumerate(grid):
            here = pl.program_id(d) == (n - 1 if end else 0)
            cond = here if cond is None else jnp.logical_and(cond, here)
        return cond

    def wrapped(*refs):
        own_in, c_in = refs[:n_in], refs[n_in:n_in + na]
        o0 = n_in + na
        own_o, c_out = refs[o0:o0 + n_out], refs[o0 + n_out:o0 + n_out + no]
        s0 = o0 + n_out + no
        own_s, (send_sems, recv_sems) = refs[s0:s0 + n_scr], refs[s0 + n_scr:]

        @pl.when(at(False))
        def _():
            comm.start(c_in, c_out, send_sems, recv_sems)

        kern(*own_in, *own_o, *own_s)

        @pl.when(at(True))
        def _():
            comm.finish(c_in, c_out, send_sems, recv_sems)

    call = pl.pallas_call(
        wrapped, name=name, out_shape=own_out + tuple(comm.out_shape), grid=grid,
        in_specs=list(in_specs) + [HBM_SPEC] * na, out_specs=own_out_specs + (HBM_SPEC,) * no,
        scratch_shapes=list(scratch) + [pltpu.SemaphoreType.DMA((comm.n_sems,)),
                                        pltpu.SemaphoreType.DMA((comm.n_sems,))],
        compiler_params=pltpu.CompilerParams(**cp))

    def run(*args):
        res = call(*args, *comm.args)
        own = res[0] if single else tuple(res[:n_out])
        return own, tuple(res[n_out:])

    return run


def _run_comm(script, *, name):
    na, no = len(script.args), len(script.out_shape)

    def body(*refs):
        ins, outs = refs[:na], refs[na:na + no]
        send_sems, recv_sems = refs[na + no:]
        script.start(ins, outs, send_sems, recv_sems)
        script.finish(ins, outs, send_sems, recv_sems)

    return pl.pallas_call(
        body, name=name, out_shape=tuple(script.out_shape), in_specs=[HBM_SPEC] * na, out_specs=(HBM_SPEC,) * no,
        scratch_shapes=[pltpu.SemaphoreType.DMA((script.n_sems,)), pltpu.SemaphoreType.DMA((script.n_sems,))])(
            *script.args)


def _sigmoid(g):
    return 1.0 / (1.0 + jnp.exp(-g))


def _silu_and_grad(g):
    sig = _sigmoid(g)
    return g * sig, sig * (1.0 + g * (1.0 - sig))


def _matmul(a, b, mode, *, name, tm, tn, tk, out_dtype=F32, vmem_mb=48, comm=None):
    if mode == "nn":
        (M, K), N = a.shape, b.shape[1]
    elif mode == "nt":
        (M, K), N = a.shape, b.shape[0]
    else:
        (K, M), N = a.shape, b.shape[1]
    tm, tn, tk = min(tm, M), min(tn, N), min(tk, K)
    assert M % tm == 0 and N % tn == 0 and K % tk == 0, (name, M, N, K)
    nk = K // tk
    dn = {"nn": NN, "nt": NT, "tn": TN}[mode]
    if mode == "tn":
        a_spec = pl.BlockSpec((tk, tm), lambda i, j, k: (k, i))
    else:
        a_spec = pl.BlockSpec((tm, tk), lambda i, j, k: (i, k))
    if mode == "nt":
        b_spec = pl.BlockSpec((tn, tk), lambda i, j, k: (j, k))
    else:
        b_spec = pl.BlockSpec((tk, tn), lambda i, j, k: (k, j))
    o_spec = pl.BlockSpec((tm, tn), lambda i, j, k: (i, j))

    def kern(a_ref, b_ref, o_ref, *rest):
        part = lax.dot_general(a_ref[...].astype(BF16), b_ref[...].astype(BF16), dn,
                               preferred_element_type=F32)
        if nk == 1:
            o_ref[...] = part.astype(out_dtype)
        else:
            acc_ref = rest[0]
            k = pl.program_id(2)

            @pl.when(k == 0)
            def _():
                acc_ref[...] = part

            @pl.when(k > 0)
            def _():
                acc_ref[...] += part

            @pl.when(k == nk - 1)
            def _():
                o_ref[...] = acc_ref[...].astype(out_dtype)

    scratch = [pltpu.VMEM((tm, tn), F32)] if nk > 1 else []
    return _pcall(kern, name=name, out_shape=jax.ShapeDtypeStruct((M, N), out_dtype),
                  grid=(M // tm, N // tn, nk), in_specs=[a_spec, b_spec], out_specs=o_spec, scratch=scratch,
                  dims=("parallel", "parallel", "arbitrary"), vmem_mb=vmem_mb, comm=comm)(a, b)


def _dproj_times_w(d_mla, d_mix, wt, add, add_scale, *, name, comm=None):
    S = d_mla.shape[0]
    Dm = wt.shape[1]
    tm, tn, tk = min(1024, S), 1024, 2048
    nk = 1 + W_MIX // tk

    def kern(a1_ref, a2_ref, b1_ref, b2_ref, add_ref, o_ref, acc_ref):
        k = pl.program_id(2)

        @pl.when(k == 0)
        def _():
            acc_ref[...] = jnp.dot(a1_ref[...], b1_ref[...], preferred_element_type=F32)

        @pl.when(k > 0)
        def _():
            acc_ref[...] += jnp.dot(a2_ref[...], b2_ref[...], preferred_element_type=F32)

        @pl.when(k == nk - 1)
        def _():
            o_ref[...] = add_scale * add_ref[...] + acc_ref[...]

    o_spec = pl.BlockSpec((tm, tn), lambda i, j, k: (i, j))
    b2_spec = pl.BlockSpec((pl.Element(tk), pl.Element(tn)),
                           lambda i, j, k: (pl.multiple_of(W_MLA + tk * jnp.maximum(k - 1, 0), W_MLA),
                                            pl.multiple_of(j * tn, tn)))
    return _pcall(kern, name=name, out_shape=jax.ShapeDtypeStruct((S, Dm), F32), grid=(S // tm, Dm // tn, nk),
                  in_specs=[pl.BlockSpec((tm, W_MLA), lambda i, j, k: (i, 0)),
                            pl.BlockSpec((tm, tk), lambda i, j, k: (i, jnp.maximum(k - 1, 0))),
                            pl.BlockSpec((W_MLA, tn), lambda i, j, k: (0, j)), b2_spec, o_spec],
                  out_specs=o_spec, scratch=[pltpu.VMEM((tm, tn), F32)],
                  dims=("parallel", "parallel", "arbitrary"), vmem_mb=56, comm=comm)(d_mla, d_mix, wt, wt, add)


def _dproj_t_times_h(d_mla, d_mix, h, *, name, comm=None):
    S, Dm = h.shape
    tm, tn, tk = W_MLA, 1024, min(2048, S)
    nk = S // tk

    def kern(a1_ref, a2_ref, b_ref, o_ref, acc_ref):
        i = pl.program_id(0)
        k = pl.program_id(2)
        b = b_ref[...].astype(BF16)

        def accumulate(part):
            @pl.when(k == 0)
            def _():
                acc_ref[...] = part

            @pl.when(k > 0)
            def _():
                acc_ref[...] += part

        @pl.when(i == 0)
        def _():
            accumulate(lax.dot_general(a1_ref[...], b, TN, preferred_element_type=F32))

        @pl.when(i > 0)
        def _():
            accumulate(lax.dot_general(a2_ref[...], b, TN, preferred_element_type=F32))

        @pl.when(k == nk - 1)
        def _():
            o_ref[...] = acc_ref[...]

    return _pcall(kern, name=name, out_shape=jax.ShapeDtypeStruct((NP, Dm), F32), grid=(NP // tm, Dm // tn, nk),
                  in_specs=[pl.BlockSpec((tk, tm), lambda i, j, k: (jnp.where(i == 0, k, nk - 1), 0)),
                            pl.BlockSpec((tk, tm), lambda i, j, k: (jnp.where(i == 0, 0, k), jnp.maximum(i - 1, 0))),
                            pl.BlockSpec((tk, tn), lambda i, j, k: (k, j))],
                  out_specs=pl.BlockSpec((tm, tn), lambda i, j, k: (i, j)), scratch=[pltpu.VMEM((tm, tn), F32)],
                  dims=("parallel", "parallel", "arbitrary"), vmem_mb=48, comm=comm)(d_mla, d_mix, h)


def _ln_fwd(x, g, b, *, name, comm=None):
    S, Dm = x.shape
    tm = min(512, S)

    def kern(x_ref, g_ref, b_ref, y_ref, yb_ref):
        xf = x_ref[...]
        mu = jnp.mean(xf, axis=-1, keepdims=True)
        xc = xf - mu
        var = jnp.mean(xc * xc, axis=-1, keepdims=True)
        y = xc * lax.rsqrt(var + LN_EPS) * g_ref[...] + b_ref[...]
        y_ref[...] = y
        yb_ref[...] = y.astype(BF16)

    row = pl.BlockSpec((tm, Dm), lambda i: (i, 0))
    vec = pl.BlockSpec((1, Dm), lambda i: (0, 0))
    return _pcall(kern, name=name,
                  out_shape=(jax.ShapeDtypeStruct((S, Dm), F32), jax.ShapeDtypeStruct((S, Dm), BF16)),
                  grid=(S // tm,), in_specs=[row, vec, vec], out_specs=(row, row), dims=("parallel",), vmem_mb=48,
                  comm=comm)(
                      x, g.reshape(1, Dm), b.reshape(1, Dm))


def _ln_bwd(dy, r, g, *, name, bf16_copy=True):
    S, Dm = r.shape
    tm = min(512, S)

    def kern(dy_ref, r_ref, g_ref, dr_ref, *rest):
        drb_ref = rest[0] if bf16_copy else None
        dg_ref, db_ref, ds_ref = rest[-3:]

        @pl.when(pl.program_id(0) == 0)
        def _():
            dg_ref[...] = jnp.zeros_like(dg_ref)
            db_ref[...] = jnp.zeros_like(db_ref)
            ds_ref[...] = jnp.zeros_like(ds_ref)

        rf = r_ref[...]
        dyf = dy_ref[...]
        mu = jnp.mean(rf, axis=-1, keepdims=True)
        xc = rf - mu
        var = jnp.mean(xc * xc, axis=-1, keepdims=True)
        rstd = lax.rsqrt(var + LN_EPS)
        xhat = xc * rstd
        dxh = dyf * g_ref[...]
        c1 = jnp.mean(dxh, axis=-1, keepdims=True)
        c2 = jnp.mean(dxh * xhat, axis=-1, keepdims=True)
        dr = rstd * (dxh - c1 - xhat * c2)
        dr_ref[...] = dr
        if bf16_copy:
            drb_ref[...] = dr.astype(BF16)
        dg_ref[...] += jnp.sum(dyf * xhat, axis=0, keepdims=True)
        db_ref[...] += jnp.sum(dyf, axis=0, keepdims=True)
        ds_ref[...] += jnp.sum(dr, axis=0, keepdims=True)

    row = pl.BlockSpec((tm, Dm), lambda i: (i, 0))
    vec = pl.BlockSpec((1, Dm), lambda i: (0, 0))
    vshape = jax.ShapeDtypeStruct((1, Dm), F32)
    copies = ((jax.ShapeDtypeStruct((S, Dm), BF16),), (row,)) if bf16_copy else ((), ())
    res = _pcall(kern, name=name,
                 out_shape=(jax.ShapeDtypeStruct((S, Dm), F32),) + copies[0] + (vshape, vshape, vshape),
                 grid=(S // tm,), in_specs=[row, row, vec], out_specs=(row,) + copies[1] + (vec, vec, vec),
                 dims=("arbitrary",), vmem_mb=48)(dy, r, g.reshape(1, Dm))
    return res if bf16_copy else (res[0], None) + tuple(res[1:])


def _loss_ln_bwd(target, r, g, b, *, name):
    S, Dm = r.shape
    tm = min(512, S)

    def kern(t_ref, r_ref, g_ref, b_ref, l_ref, dr_ref, drb_ref, dg_ref, db_ref, ds_ref):
        @pl.when(pl.program_id(0) == 0)
        def _():
            l_ref[...] = jnp.zeros_like(l_ref)
            dg_ref[...] = jnp.zeros_like(dg_ref)
            db_ref[...] = jnp.zeros_like(db_ref)
            ds_ref[...] = jnp.zeros_like(ds_ref)

        rf = r_ref[...]
        mu = jnp.mean(rf, axis=-1, keepdims=True)
        xc = rf - mu
        var = jnp.mean(xc * xc, axis=-1, keepdims=True)
        rstd = lax.rsqrt(var + LN_EPS)
        xhat = xc * rstd
        e = (xhat * g_ref[...] + b_ref[...]) - t_ref[...]
        dyf = e / float(Dm)
        per_row = jnp.mean(e * e, axis=-1, keepdims=True)
        l_ref[...] += 0.5 * jnp.sum(per_row, axis=0, keepdims=True)
        dxh = dyf * g_ref[...]
        c1 = jnp.mean(dxh, axis=-1, keepdims=True)
        c2 = jnp.mean(dxh * xhat, axis=-1, keepdims=True)
        dr = rstd * (dxh - c1 - xhat * c2)
        dr_ref[...] = dr
        drb_ref[...] = dr.astype(BF16)
        dg_ref[...] += jnp.sum(dyf * xhat, axis=0, keepdims=True)
        db_ref[...] += jnp.sum(dyf, axis=0, keepdims=True)
        ds_ref[...] += jnp.sum(dr, axis=0, keepdims=True)

    row = pl.BlockSpec((tm, Dm), lambda i: (i, 0))
    vec = pl.BlockSpec((1, Dm), lambda i: (0, 0))
    acc = pl.BlockSpec((8, LANE), lambda i: (0, 0))
    vshape = jax.ShapeDtypeStruct((1, Dm), F32)
    return _pcall(kern, name=name,
                  out_shape=(jax.ShapeDtypeStruct((8, LANE), F32), jax.ShapeDtypeStruct((S, Dm), F32),
                             jax.ShapeDtypeStruct((S, Dm), BF16), vshape, vshape, vshape),
                  grid=(S // tm,), in_specs=[row, row, vec, vec], out_specs=(acc, row, row, vec, vec, vec),
                  dims=("arbitrary",), vmem_mb=56)(target, r, g.reshape(1, Dm), b.reshape(1, Dm))


def _rot_sum(t):
    return pltpu.roll(t, 32, 1) + pltpu.roll(t, 96, 1)


def _mla_qkv(proj, cos_t, sin_t, qg, kvg, wuq_t, wukv_t, *, name):
    S = proj.shape[0]
    tm = min(256, S)

    def kern(ql_ref, kvl_ref, kr_ref, cos_ref, sin_ref, qg_ref, kvg_ref, wuq_ref, wukv_ref,
             qc_ref, kc_ref, v_ref, vt_ref, qn_ref, kvn_ref):
        cosv = cos_ref[...]
        sinv = sin_ref[...]

        def rope(t):
            return t * cosv + _rot_sum(t) * sinv

        ql = ql_ref[...]
        qn = (ql * lax.rsqrt(jnp.mean(ql * ql, axis=-1, keepdims=True) + RMS_EPS) * qg_ref[...]).astype(BF16)
        kvl = kvl_ref[...]
        kvn = (kvl * lax.rsqrt(jnp.mean(kvl * kvl, axis=-1, keepdims=True) + RMS_EPS) * kvg_ref[...]).astype(BF16)
        qn_ref[...] = qn
        kvn_ref[...] = kvn
        q = lax.dot_general(qn, wuq_ref[...], NT, preferred_element_type=F32)
        kv = lax.dot_general(kvn, wukv_ref[...], NT, preferred_element_type=F32)
        kr = rope(kr_ref[...]).astype(BF16)
        for h in range(N_HEADS):
            c0 = 256 * h
            qc_ref[:, c0:c0 + 128] = q[:, c0:c0 + 128].astype(BF16)
            qc_ref[:, c0 + 128:c0 + 256] = rope(q[:, c0 + 128:c0 + 256]).astype(BF16)
            kc_ref[:, c0:c0 + 128] = kv[:, c0:c0 + 128].astype(BF16)
            kc_ref[:, c0 + 128:c0 + 256] = kr
            vh = kv[:, c0 + 128:c0 + 256]
            v_ref[:, 128 * h:128 * h + 128] = vh.astype(BF16)
            vt_ref[h] = jnp.transpose(vh).astype(BF16)

    def row(w, blk):
        return pl.BlockSpec((tm, w), lambda i: (i, blk))

    def full(shape):
        return pl.BlockSpec(shape, lambda i: (0,) * len(shape))

    t = min(TQ, S)
    per = t // tm
    vt_spec = pl.BlockSpec((N_HEADS, None, 128, tm), lambda i: (0, i // per, 0, i % per))
    outs = (jax.ShapeDtypeStruct((S, 2048), BF16), jax.ShapeDtypeStruct((S, 2048), BF16),
            jax.ShapeDtypeStruct((S, 1024), BF16), jax.ShapeDtypeStruct((N_HEADS, S // t, 128, t), BF16),
            jax.ShapeDtypeStruct((S, Q_LORA), BF16), jax.ShapeDtypeStruct((S, KV_LORA), BF16))
    return _pcall(kern, name=name, out_shape=outs, grid=(S // tm,),
                  in_specs=[row(512, 0), row(256, 2), row(128, 6), row(128, 0), row(128, 0),
                            full((1, Q_LORA)), full((1, KV_LORA)), full((2048, Q_LORA)), full((2048, KV_LORA))],
                  out_specs=(row(2048, 0), row(2048, 0), row(1024, 0), vt_spec, row(512, 0), row(256, 0)),
                  dims=("parallel",), vmem_mb=48)(
                      proj, proj, proj, cos_t, sin_t, qg.reshape(1, -1), kvg.reshape(1, -1), wuq_t, wukv_t)


def _mla_qkv_bwd(dqb, dkvb, dkr_heads, proj, cos_t, sin_t, qg, kvg, wuq_t, wukv_t, *, name):
    S = proj.shape[0]
    tm = min(256, S)

    def kern(dqb_ref, dkvb_ref, dkrh_ref, ql_ref, kvl_ref, cos_ref, sin_ref, qg_ref, kvg_ref, wuq_ref, wukv_ref,
             dml_ref, dqg_ref, dkvg_ref):
        @pl.when(pl.program_id(0) == 0)
        def _():
            dqg_ref[...] = jnp.zeros_like(dqg_ref)
            dkvg_ref[...] = jnp.zeros_like(dkvg_ref)

        cosv = cos_ref[...]
        sinv = sin_ref[...]

        def unrope(t):
            return t * cosv - _rot_sum(t) * sinv

        dkr = dkrh_ref[:, 0:128]
        for h in range(1, N_HEADS):
            dkr = dkr + dkrh_ref[:, 128 * h:128 * h + 128]

        def rms_bwd(x, g, dy):
            n = x.shape[-1]
            rs = lax.rsqrt(jnp.mean(x * x, axis=-1, keepdims=True) + RMS_EPS)
            dyg = dy * g
            dx = rs * dyg - x * (rs * rs * rs) * (jnp.sum(dyg * x, axis=-1, keepdims=True) / n)
            return dx, jnp.sum(dy * (x * rs), axis=0, keepdims=True)

        dqn = jnp.dot(dqb_ref[...], wuq_ref[...], preferred_element_type=F32)
        dql, dqg = rms_bwd(ql_ref[...], qg_ref[...], dqn)
        dqg_ref[...] += dqg
        dkvn = jnp.dot(dkvb_ref[...], wukv_ref[...], preferred_element_type=F32)
        dkvl, dkvg = rms_bwd(kvl_ref[...], kvg_ref[...], dkvn)
        dkvg_ref[...] += dkvg
        dml_ref[:, 0:512] = dql.astype(BF16)
        dml_ref[:, 512:768] = dkvl.astype(BF16)
        dml_ref[:, 768:896] = unrope(dkr).astype(BF16)
        dml_ref[:, 896:1024] = jnp.zeros((tm, 128), BF16)

    def row(w, blk):
        return pl.BlockSpec((tm, w), lambda i: (i, blk))

    def full(shape):
        return pl.BlockSpec(shape, lambda i: (0,) * len(shape))

    outs = (jax.ShapeDtypeStruct((S, W_MLA), BF16), jax.ShapeDtypeStruct((1, Q_LORA), F32),
            jax.ShapeDtypeStruct((1, KV_LORA), F32))
    return _pcall(kern, name=name, out_shape=outs, grid=(S // tm,),
                  in_specs=[row(2048, 0), row(2048, 0), row(1024, 0), row(512, 0), row(256, 2),
                            row(128, 0), row(128, 0), full((1, Q_LORA)), full((1, KV_LORA)),
                            full((2048, Q_LORA)), full((2048, KV_LORA))],
                  out_specs=(row(W_MLA, 0), full((1, Q_LORA)), full((1, KV_LORA))),
                  dims=("arbitrary",), vmem_mb=56)(
                      dqb, dkvb, dkr_heads, proj, proj, cos_t, sin_t, qg.reshape(1, -1), kvg.reshape(1, -1),
                      wuq_t, wukv_t)


def _flash_fwd(qc, kc, vt, *, name, comm=None):
    S = qc.shape[0]
    t = min(TQ, S)
    n = S // t

    def kern(q_ref, k_ref, vt_ref, o_ref, lse_ref, m_s, l_s, acc_s):
        qi = pl.program_id(1)
        m_s[...] = jnp.full_like(m_s, -jnp.inf)
        l_s[...] = jnp.zeros_like(l_s)
        acc_s[...] = jnp.zeros_like(acc_s)

        half = t // 2

        def scores(kb, q_lo=0, q_n=t, k_n=t):
            k0 = pl.multiple_of(kb * t, t)
            return lax.dot_general(k_ref[pl.ds(k0, k_n), :], q_ref[q_lo:q_lo + q_n, :], NT,
                                   preferred_element_type=F32)

        def update(kb, st, q_lo=0, diagonal=False):
            k_n, q_n = st.shape
            if diagonal:
                krow = lax.broadcasted_iota(jnp.int32, (k_n, q_n), 0)
                qcol = lax.broadcasted_iota(jnp.int32, (k_n, q_n), 1) + q_lo
                st = jnp.where(krow <= qcol, st, -jnp.inf)
            lanes = slice(q_lo, q_lo + q_n)
            m_prev = m_s[:, lanes]
            m_new = jnp.maximum(m_prev, jnp.max(st, axis=0, keepdims=True))
            a = jnp.exp2((m_prev - m_new) * SCALE_LOG2E)
            pt = jnp.exp2((st - m_new) * SCALE_LOG2E)
            l_s[:, lanes] = a * l_s[:, lanes] + jnp.sum(pt, axis=0, keepdims=True)
            acc_s[:, lanes] = a * acc_s[:, lanes] + jnp.dot(vt_ref[kb, :, 0:k_n], pt.astype(BF16),
                                                            preferred_element_type=F32)
            m_s[:, lanes] = m_new

        def group(kb, count, last_diagonal):
            whole = count - 1 if last_diagonal else count
            sts = [scores(kb + g) for g in range(whole)]
            if last_diagonal:
                kd = kb + count - 1
                s_lo, s_hi = scores(kd, 0, half, half), scores(kd, half, half, t)
            for g in range(whole):
                update(kb + g, sts[g])
            if last_diagonal:
                update(kd, s_lo, 0, True)
                update(kd, s_hi, half, True)

        def body(i, carry):
            group(FWD_GROUP * i, FWD_GROUP, False)
            return carry

        full = qi // FWD_GROUP
        lax.fori_loop(0, full, body, 0)
        for rem in range(FWD_GROUP):
            @pl.when(qi - FWD_GROUP * full == rem)
            def _():
                group(qi - rem, rem + 1, True)
        o_ref[...] = jnp.transpose(acc_s[...] / l_s[...])
        lse_ref[pl.ds(qi, 1), :] = m_s[...] * SCALE_LOG2E + jnp.log2(l_s[...])

    q_spec = pl.BlockSpec((t, 256), lambda h, qi: (qi, h))
    k_spec = pl.BlockSpec((S, 256), lambda h, qi: (0, h))
    vt_spec = pl.BlockSpec((None, n, 128, t), lambda h, qi: (h, 0, 0, 0))
    o_spec = pl.BlockSpec((t, 128), lambda h, qi: (qi, h))
    lse_spec = pl.BlockSpec((None, n, t), lambda h, qi: (h, 0, 0))
    return _pcall(kern, name=name,
                  out_shape=(jax.ShapeDtypeStruct((S, D_MLA), F32), jax.ShapeDtypeStruct((N_HEADS, n, t), F32)),
                  grid=(N_HEADS, n), in_specs=[q_spec, k_spec, vt_spec], out_specs=(o_spec, lse_spec),
                  scratch=[pltpu.VMEM((1, t), F32), pltpu.VMEM((1, t), F32), pltpu.VMEM((128, t), F32)],
                  dims=("parallel", "arbitrary"), vmem_mb=48, comm=comm)(qc, kc, vt)


def _attn_delta(o, do, *, name):
    S = o.shape[0]
    t = min(TQ, S)
    n = S // t

    def kern(o_ref, do_ref, dl_ref):
        i = pl.program_id(0)
        prod = o_ref[...] * do_ref[...]
        lane = lax.broadcasted_iota(jnp.int32, (t, LANE), 1)
        dmat = jnp.zeros((t, LANE), F32)
        for h in range(N_HEADS):
            dmat = jnp.where(lane == h, jnp.sum(prod[:, 128 * h:128 * h + 128], axis=1, keepdims=True), dmat)
        dmat_t = jnp.transpose(dmat)
        for h in range(N_HEADS):
            dl_ref[h, pl.ds(i, 1), :] = dmat_t[h:h + 1, :]

    row = pl.BlockSpec((t, D_MLA), lambda i: (i, 0))
    return _pcall(kern, name=name, out_shape=jax.ShapeDtypeStruct((N_HEADS, n, t), F32), grid=(n,),
                  in_specs=[row, row], out_specs=pl.BlockSpec((N_HEADS, n, t), lambda i: (0, 0, 0)),
                  dims=("arbitrary",), vmem_mb=48)(o, do)


def _flash_bwd(qc, kc, v, do, lse2, delta, cos_t, sin_t, *, name, comm=None):
    S = qc.shape[0]
    t = min(TQ, S)
    n = S // t

    def kern(q_ref, k_ref, v_ref, do_ref, lse_ref, dl_ref, cos_ref, sin_ref, dqb_ref, dkvb_ref, dkr_ref,
             dq_ref, dk_ref, dv_ref):
        ki = pl.program_id(1)

        @pl.when(ki == 0)
        def _():
            dq_ref[...] = jnp.zeros_like(dq_ref)

        dk_ref[...] = jnp.zeros_like(dk_ref)
        dv_ref[...] = jnp.zeros_like(dv_ref)

        half = t // 2

        def step(qb, q_lo=0, q_n=t, k_n=t, diagonal=False):
            q0 = pl.multiple_of(qb * t + q_lo, half)
            lanes = slice(q_lo, q_lo + q_n)
            kt = k_ref[0:k_n, :]
            qblk = q_ref[pl.ds(q0, q_n), :]
            dob = do_ref[pl.ds(q0, q_n), :].astype(BF16)
            st = lax.dot_general(kt, qblk, NT, preferred_element_type=F32)
            pt = jnp.exp2(st * SCALE_LOG2E - lse_ref[pl.ds(qb, 1), lanes])
            if diagonal:
                krow = lax.broadcasted_iota(jnp.int32, (k_n, q_n), 0)
                qcol = lax.broadcasted_iota(jnp.int32, (k_n, q_n), 1) + q_lo
                pt = jnp.where(krow <= qcol, pt, 0.0)
            dv_ref[0:k_n, :] += jnp.dot(pt.astype(BF16), dob, preferred_element_type=F32)
            dpt = lax.dot_general(v_ref[0:k_n, :], dob, NT, preferred_element_type=F32)
            dst = (pt * (dpt - dl_ref[pl.ds(qb, 1), lanes]) * SCALE).astype(BF16)
            dk_ref[0:k_n, :] += jnp.dot(dst, qblk, preferred_element_type=F32)
            dq_ref[pl.ds(q0, q_n), :] += lax.dot_general(dst, kt, TN, preferred_element_type=F32)

        step(ki, 0, half, half, True)
        step(ki, half, half, t, True)
        rest = n - 1 - ki

        def body(i, carry):
            step(ki + 1 + 2 * i)
            step(ki + 2 + 2 * i)
            return carry

        lax.fori_loop(0, rest // 2, body, 0)

        @pl.when(rest % 2 == 1)
        def _():
            step(n - 1)

        dkvb_ref[:, 0:128] = dk_ref[:, 0:128].astype(BF16)
        dkvb_ref[:, 128:256] = dv_ref[...].astype(BF16)
        dkr_ref[...] = dk_ref[:, 128:256]

        @pl.when(ki == n - 1)
        def _():
            dqb_ref[:, 0:128] = dq_ref[:, 0:128].astype(BF16)
            dqr = dq_ref[:, 128:256]
            dqb_ref[:, 128:256] = (dqr * cos_ref[...] - _rot_sum(dqr) * sin_ref[...]).astype(BF16)

    def whole(w):
        return pl.BlockSpec((S, w), lambda h, ki: (0, h))

    def krow(w):
        return pl.BlockSpec((t, w), lambda h, ki: (ki, h))

    stat = pl.BlockSpec((None, n, t), lambda h, ki: (h, 0, 0))
    table = pl.BlockSpec((S, 128), lambda h, ki: (0, 0))
    return _pcall(kern, name=name,
                  out_shape=(jax.ShapeDtypeStruct((S, 2048), BF16), jax.ShapeDtypeStruct((S, 2048), BF16),
                             jax.ShapeDtypeStruct((S, D_MLA), F32)),
                  grid=(N_HEADS, n),
                  in_specs=[whole(256), krow(256), krow(128), whole(128), stat, stat, table, table],
                  out_specs=(whole(256), krow(256), krow(128)),
                  scratch=[pltpu.VMEM((S, 256), F32), pltpu.VMEM((t, 256), F32), pltpu.VMEM((t, 128), F32)],
                  dims=("parallel", "arbitrary"), vmem_mb=56, comm=comm)(qc, kc, v, do, lse2, delta, cos_t, sin_t)


def _mixer_specs(S, tm):
    hb = tm // HALO
    last_hb = S // HALO - 1

    def main(w, blk):
        return pl.BlockSpec((tm, w), lambda i: (i, blk))

    def prev(w, blk):
        return pl.BlockSpec((HALO, w), lambda i: (jnp.maximum(i * hb - 1, 0), blk))

    def nxt(w, blk):
        return pl.BlockSpec((HALO, w), lambda i: (jnp.minimum((i + 1) * hb, last_hb), blk))

    def full(shape):
        return pl.BlockSpec(shape, lambda i: (0,) * len(shape))

    return main, prev, nxt, full


def _fill_halo(i, xp, xu, hp_ref, hch_ref, hcc_ref, pin_ref, ch_ref, cc_ref, tm):
    first = i == 0
    xp[0:HALO, :] = jnp.where(first, 0.0, hp_ref[...])
    xp[HALO:HALO + tm, :] = pin_ref[...]
    xu[0:HALO, :] = jnp.where(first, 0.0, hch_ref[...] * hcc_ref[...])
    xu[HALO:HALO + tm, :] = cc_ref[...] * ch_ref[...]


def _pooled(xp, g, t1, tm):
    w = POOL_WINDOWS[g]
    lanes = slice(128 * g, 128 * g + 128)
    x0 = xp[HALO:HALO + tm, lanes]
    acc = x0
    for k in range(1, w):
        acc = acc + xp[HALO - k:HALO - k + tm, lanes]
    return acc / jnp.minimum(t1, float(w)) - x0


def _conv_fwd(xu, cw_ref, tm):
    return (cw_ref[0:1, :] * xu[HALO - 2:HALO - 2 + tm, :] + cw_ref[1:2, :] * xu[HALO - 1:HALO - 1 + tm, :]
            + cw_ref[2:3, :] * xu[HALO:HALO + tm, :])


def _mixer_fwd(proj, o, wpool, ps, convw, *, name):
    S = proj.shape[0]
    tm = min(256, S)
    main, prev, _, full = _mixer_specs(S, tm)

    def kern(gm_ref, pin_ref, gp_ref, ch_ref, cb_ref, cc_ref, gc_ref, hp_ref, hch_ref, hcc_ref,
             o_ref, wp_ref, ps_ref, cw_ref, mix_ref, xp, xu):
        i = pl.program_id(0)
        _fill_halo(i, xp, xu, hp_ref, hch_ref, hcc_ref, pin_ref, ch_ref, cc_ref, tm)
        t1 = (i * tm + lax.broadcasted_iota(jnp.int32, (tm, 1), 0) + 1).astype(F32)
        for g in range(4):
            lanes = slice(128 * g, 128 * g + 128)
            pooled = _pooled(xp, g, t1, tm)
            z = jnp.dot(pooled.astype(BF16), wp_ref[g].astype(BF16), preferred_element_type=F32)
            gp = gp_ref[:, lanes]
            y = z * ps_ref[:, lanes] * (gp * _sigmoid(gp))
            mix_ref[:, 1024 + 128 * g:1024 + 128 * g + 128] = y.astype(BF16)
        gc = gc_ref[...]
        mix_ref[:, 1536:2048] = (cb_ref[...] * _conv_fwd(xu, cw_ref, tm) * (gc * _sigmoid(gc))).astype(BF16)
        gm = gm_ref[...]
        mix_ref[:, 0:1024] = (o_ref[...] * (gm * _sigmoid(gm))).astype(BF16)

    return _pcall(kern, name=name, out_shape=jax.ShapeDtypeStruct((S, 2048), BF16), grid=(S // tm,),
                  in_specs=[main(1024, 1), main(512, 4), main(512, 5), main(512, 6), main(512, 7), main(512, 8),
                            main(512, 9), prev(512, 4), prev(512, 6), prev(512, 8),
                            main(1024, 0), full((4, 128, 128)), full((1, 512)), full((3, 512))],
                  out_specs=main(2048, 0),
                  scratch=[pltpu.VMEM((tm + HALO, 512), F32), pltpu.VMEM((tm + HALO, 512), F32)],
                  dims=("parallel",), vmem_mb=48)(
                      proj, proj, proj, proj, proj, proj, proj, proj, proj, proj, o, wpool, ps.reshape(1, 512), convw)


def _mixer_bwd(dmix, proj, o, wpool, ps, convw, *, name):
    S = proj.shape[0]
    tm = min(256, S)
    n = S // tm
    main, prev, nxt, full = _mixer_specs(S, tm)

    def kern(dm_ref, dmn_ref, gm_ref, pin_ref, gp_ref, ch_ref, cb_ref, cc_ref, gc_ref,
             hp_ref, hch_ref, hcc_ref, gpn_ref, cbn_ref, gcn_ref, o_ref, wp_ref, ps_ref, cw_ref,
             d_ref, do_ref, dwp_ref, dps_ref, dcw_ref, xp, xu, ee, ed):
        i = pl.program_id(0)
        last = i == n - 1

        @pl.when(i == 0)
        def _():
            dwp_ref[...] = jnp.zeros_like(dwp_ref)
            dps_ref[...] = jnp.zeros_like(dps_ref)
            dcw_ref[...] = jnp.zeros_like(dcw_ref)

        _fill_halo(i, xp, xu, hp_ref, hch_ref, hcc_ref, pin_ref, ch_ref, cc_ref, tm)
        t1 = (i * tm + lax.broadcasted_iota(jnp.int32, (tm, 1), 0) + 1).astype(F32)
        t1n = ((i + 1) * tm + lax.broadcasted_iota(jnp.int32, (HALO, 1), 0) + 1).astype(F32)
        c_pin, c_gp, c_ch, c_cb, c_cc, c_gc = 1024, 1536, 2048, 2560, 3072, 3584

        for g in range(4):
            w = float(POOL_WINDOWS[g])
            lanes = slice(128 * g, 128 * g + 128)
            pooled = _pooled(xp, g, t1, tm)
            pb = pooled.astype(BF16)
            wp = wp_ref[g].astype(BF16)
            z = jnp.dot(pb, wp, preferred_element_type=F32)
            psl = ps_ref[:, lanes]
            sg, dsg = _silu_and_grad(gp_ref[:, lanes])
            dmp = dm_ref[:, 1024 + 128 * g:1024 + 128 * g + 128]
            dyp = dmp * sg
            d_ref[:, c_gp + 128 * g:c_gp + 128 * g + 128] = (dmp * (z * psl) * dsg).astype(BF16)
            dps_ref[:, lanes] += jnp.sum(dyp * z, axis=0, keepdims=True)
            dz = (dyp * psl).astype(BF16)
            dwp_ref[g] += lax.dot_general(pb, dz, TN, preferred_element_type=F32)
            dpl = lax.dot_general(dz, wp, NT, preferred_element_type=F32)
            ee[0:tm, lanes] = dpl / jnp.minimum(t1, w)
            gpn = gpn_ref[:, lanes]
            dzn = (dmn_ref[:, lanes] * (gpn * _sigmoid(gpn)) * psl).astype(BF16)
            dpn = lax.dot_general(dzn, wp, NT, preferred_element_type=F32)
            ee[tm:tm + HALO, lanes] = jnp.where(last, 0.0, dpn / jnp.minimum(t1n, w))
            acc = ee[0:tm, lanes]
            for k in range(1, POOL_WINDOWS[g]):
                acc = acc + ee[k:k + tm, lanes]
            d_ref[:, c_pin + 128 * g:c_pin + 128 * g + 128] = (acc - dpl).astype(BF16)

        yc = _conv_fwd(xu, cw_ref, tm)
        sgc, dsgc = _silu_and_grad(gc_ref[...])
        cb = cb_ref[...]
        dmc = dm_ref[:, 1536:2048]
        d_ref[:, c_gc:c_gc + 512] = (dmc * cb * yc * dsgc).astype(BF16)
        d_ref[:, c_cb:c_cb + 512] = (dmc * yc * sgc).astype(BF16)
        dyc = dmc * cb * sgc
        ed[0:tm, :] = dyc
        gcn = gcn_ref[...]
        ed[tm:tm + HALO, :] = jnp.where(last, 0.0, dmn_ref[:, 512:1024] * cbn_ref[...] * (gcn * _sigmoid(gcn)))
        dcw_ref[0:1, :] += jnp.sum(dyc * xu[HALO - 2:HALO - 2 + tm, :], axis=0, keepdims=True)
        dcw_ref[1:2, :] += jnp.sum(dyc * xu[HALO - 1:HALO - 1 + tm, :], axis=0, keepdims=True)
        dcw_ref[2:3, :] += jnp.sum(dyc * xu[HALO:HALO + tm, :], axis=0, keepdims=True)
        du = cw_ref[2:3, :] * dyc + cw_ref[1:2, :] * ed[1:1 + tm, :] + cw_ref[0:1, :] * ed[2:2 + tm, :]
        d_ref[:, c_cc:c_cc + 512] = (du * ch_ref[...]).astype(BF16)
        d_ref[:, c_ch:c_ch + 512] = (du * cc_ref[...]).astype(BF16)

        sgm, dsgm = _silu_and_grad(gm_ref[...])
        dmm = dm_ref[:, 0:1024]
        do_ref[...] = dmm * sgm
        d_ref[:, 0:1024] = (dmm * o_ref[...] * dsgm).astype(BF16)

    outs = (jax.ShapeDtypeStruct((S, W_MIX), BF16), jax.ShapeDtypeStruct((S, 1024), F32),
            jax.ShapeDtypeStruct((4, 128, 128), F32), jax.ShapeDtypeStruct((1, 512), F32),
            jax.ShapeDtypeStruct((3, 512), F32))
    scr = [pltpu.VMEM((tm + HALO, 512), F32) for _ in range(4)]
    return _pcall(kern, name=name, out_shape=outs, grid=(n,),
                  in_specs=[main(2048, 0), nxt(1024, 1),
                            main(1024, 1), main(512, 4), main(512, 5), main(512, 6), main(512, 7), main(512, 8),
                            main(512, 9), prev(512, 4), prev(512, 6), prev(512, 8),
                            nxt(512, 5), nxt(512, 7), nxt(512, 9),
                            main(1024, 0), full((4, 128, 128)), full((1, 512)), full((3, 512))],
                  out_specs=(main(W_MIX, 0), main(1024, 0), full((4, 128, 128)), full((1, 512)), full((3, 512))),
                  scratch=scr, dims=("arbitrary",), vmem_mb=56)(
                      dmix, dmix, proj, proj, proj, proj, proj, proj, proj, proj, proj, proj, proj, proj, proj,
                      o, wpool, ps.reshape(1, 512), convw)


def _outproj_residual(mix, wout, h, bout, *, name):
    S, Dm = h.shape
    tm = min(256, S)

    def kern(mix_ref, w_ref, h_ref, bo_ref, r_ref):
        out = jnp.dot(mix_ref[...], w_ref[...], preferred_element_type=F32) + bo_ref[...]
        r_ref[...] = ALPHA * h_ref[...] + out

    row = pl.BlockSpec((tm, Dm), lambda i: (i, 0))
    vec = pl.BlockSpec((1, Dm), lambda i: (0, 0))
    wsp = pl.BlockSpec((Dm, Dm), lambda i: (0, 0))
    return _pcall(kern, name=name, out_shape=jax.ShapeDtypeStruct((S, Dm), F32), grid=(S // tm,),
                  in_specs=[row, wsp, row, vec], out_specs=row, dims=("parallel",), vmem_mb=56)(
                      mix, wout, h, bout.reshape(1, Dm))


def _outproj_ln(mix, wout, h, bout, g, b, *, name):
    S, Dm = h.shape
    tm = min(256, S)

    def kern(mix_ref, w_ref, h_ref, bo_ref, g_ref, b_ref, y_ref, yb_ref, r_ref):
        out = jnp.dot(mix_ref[...], w_ref[...], preferred_element_type=F32) + bo_ref[...]
        r = ALPHA * h_ref[...] + out
        r_ref[...] = r
        mu = jnp.mean(r, axis=-1, keepdims=True)
        xc = r - mu
        var = jnp.mean(xc * xc, axis=-1, keepdims=True)
        y = xc * lax.rsqrt(var + LN_EPS) * g_ref[...] + b_ref[...]
        y_ref[...] = y
        yb_ref[...] = y.astype(BF16)

    row = pl.BlockSpec((tm, Dm), lambda i: (i, 0))
    vec = pl.BlockSpec((1, Dm), lambda i: (0, 0))
    wsp = pl.BlockSpec((Dm, Dm), lambda i: (0, 0))
    sds = jax.ShapeDtypeStruct((S, Dm), F32)
    return _pcall(kern, name=name, out_shape=(sds, jax.ShapeDtypeStruct((S, Dm), BF16), sds), grid=(S // tm,),
                  in_specs=[row, wsp, row, vec, vec, vec], out_specs=(row, row, row), dims=("parallel",),
                  vmem_mb=56)(
                      mix, wout, h, bout.reshape(1, Dm), g.reshape(1, Dm), b.reshape(1, Dm))


def _adamw_math(w, g, m, v):
    m = ADAM_B1 * m + (1.0 - ADAM_B1) * g
    v = ADAM_B2 * v + (1.0 - ADAM_B2) * (g * g)
    m_hat = m / (1.0 - ADAM_B1 ** ADAM_STEP)
    v_hat = v / (1.0 - ADAM_B2 ** ADAM_STEP)
    delta = -ADAM_LR * (m_hat / (jnp.sqrt(v_hat) + ADAM_EPS) + ADAM_WD * w)
    return delta, m, v


def _row_tile(R, C):
    best = None
    for cand in range(8, R, 8):
        if R % cand == 0 and cand * C <= 256 * 1024:
            best = cand
    return best if best is not None else R


def _adamw(w, g, m, v, *, name):
    shape = w.shape
    C = shape[-1]
    R = 1
    for s in shape[:-1]:
        R *= s
    tr = _row_tile(R, C)

    def kern(w_ref, g_ref, m_ref, v_ref, d_ref, mo_ref, vo_ref):
        d, mn, vn = _adamw_math(w_ref[...], g_ref[...], m_ref[...], v_ref[...])
        d_ref[...] = d
        mo_ref[...] = mn
        vo_ref[...] = vn

    blk = pl.BlockSpec((tr, C), lambda i: (i, 0))
    sds = jax.ShapeDtypeStruct((R, C), F32)
    outs = _pcall(kern, name=name, out_shape=(sds, sds, sds), grid=(R // tr,), in_specs=[blk] * 4,
                  out_specs=(blk, blk, blk), dims=("parallel",), vmem_mb=48)(
                      w.reshape(R, C), g.reshape(R, C), m.reshape(R, C), v.reshape(R, C))
    return tuple(t.reshape(shape) for t in outs)


def _adamw_halves(w, m, v, halves, c_idx, *, name, comm=None):
    _, R, C = w.shape
    ch = C // 2
    tr = _row_tile(R, ch)
    nb = R // tr

    def kern(c_ref, w_ref, a0_ref, b0_ref, a1_ref, b1_ref, m_ref, v_ref, g_ref, d_ref, mo_ref, vo_ref):
        layer = pl.program_id(0) // nb
        mine = pl.program_id(1) == c_ref[0]
        g = jnp.where(layer == 0, jnp.where(mine, a0_ref[...], b0_ref[...]),
                      jnp.where(mine, a1_ref[...], b1_ref[...]))
        g_ref[...] = g
        d, mn, vn = _adamw_math(w_ref[...], g, m_ref[...], v_ref[...])
        d_ref[...] = d
        mo_ref[...] = mn
        vo_ref[...] = vn

    full = pl.BlockSpec((tr, ch), lambda i, hc: (i, hc))
    half = pl.BlockSpec((tr, ch), lambda i, hc: (i % nb, 0))
    sds = jax.ShapeDtypeStruct((2 * R, C), F32)
    (a0, b0), (a1, b1) = halves
    res = _pcall(kern, name=name, out_shape=(sds,) * 4, grid=(2 * nb, 2),
                 in_specs=[pl.BlockSpec(memory_space=pltpu.SMEM), full, half, half, half, half, full, full],
                 out_specs=(full,) * 4, dims=("parallel", "parallel"), vmem_mb=48, comm=comm)(
                     c_idx, w.reshape(2 * R, C), a0, b0, a1, b1, m.reshape(2 * R, C), v.reshape(2 * R, C))
    outs, landed = res if comm is not None else (res, None)
    outs = tuple(t.reshape(2, R, C) for t in outs)
    return outs if comm is None else (outs, landed)


def _packed_pieces(shape):
    if len(shape) == 4:
        return [((l * shape[1] + g) * 128, 128, (l, g)) for l in range(shape[0]) for g in range(shape[1])]
    per_row = shape[1] // LANE
    return [(a * per_row + j, 1, (slice(a, a + 1), slice(LANE * j, LANE * (j + 1))))
            for a in range(shape[0]) for j in range(per_row)]


def _small_sum_adamw(gathered, own, weights, *, name):
    R = gathered.shape[1]
    nw = len(weights)
    shapes = [w.shape for w, _, _ in weights]
    first_row, r0 = [], 0
    for shp in shapes:
        first_row.append(r0)
        n = 1
        for s in shp:
            n *= s
        r0 += n // LANE

    def kern(ga_ref, own_ref, *refs):
        ins, gsum_ref, outs = refs[:3 * nw], refs[3 * nw], refs[3 * nw + 1:]
        me = 4 * lax.axis_index("x") + 2 * lax.axis_index("y") + lax.axis_index("c")

        def block(k):
            other = ga_ref[jnp.where(me == k, (k + 1) % N_DEV, k)]
            return jnp.where(me == k, own_ref[...], other)

        g = block(0)
        for k in range(1, N_DEV):
            g = g + block(k)
        gsum_ref[...] = g
        for p, shp in enumerate(shapes):
            w_ref, m_ref, v_ref = ins[3 * p:3 * p + 3]
            g_out, d_out, m_out, v_out = outs[4 * p:4 * p + 4]
            for row, rows, idx in _packed_pieces(shp):
                gp = gsum_ref[first_row[p] + row:first_row[p] + row + rows, :]
                d, mn, vn = _adamw_math(w_ref[idx], gp, m_ref[idx], v_ref[idx])
                g_out[idx] = gp
                d_out[idx] = d
                m_out[idx] = mn
                v_out[idx] = vn

    out_shape = [jax.ShapeDtypeStruct((R, LANE), F32)]
    for shp in shapes:
        out_shape += [jax.ShapeDtypeStruct(shp, F32)] * 4
    flat = [a for wmv in weights for a in wmv]
    res = _pcall(kern, name=name, out_shape=tuple(out_shape), vmem_mb=48)(gathered, own, *flat)
    return res[0], [tuple(res[1 + 4 * p:5 + 4 * p]) for p in range(nw)]


def _pair_sum(g, theirs, c_idx, *, name):
    R, C = g.shape
    ch = C // 2
    tr = _row_tile(R, ch)

    def kern(c_ref, a_ref, b_ref, o_ref):
        o_ref[...] = (a_ref[...] + b_ref[...]).astype(BF16)

    gs = pltpu.PrefetchScalarGridSpec(
        num_scalar_prefetch=1, grid=(R // tr,),
        in_specs=[pl.BlockSpec((tr, ch), lambda i, c: (i, c[0])), pl.BlockSpec((tr, ch), lambda i, c: (i, 0))],
        out_specs=pl.BlockSpec((tr, ch), lambda i, c: (i, 0)))
    return pl.pallas_call(kern, name=name, out_shape=jax.ShapeDtypeStruct((R, ch), BF16), grid_spec=gs,
                          compiler_params=pltpu.CompilerParams(dimension_semantics=("parallel",),
                                                               vmem_limit_bytes=48 << 20))(c_idx, g, theirs)


WeightRows = collections.namedtuple("WeightRows", "full_rows own_rows cols pieces zero_rows")


def _w_in_piece_a(j):
    return jnp.where(j == 0, 0, 1232 * j + GAP)


def _w_in_piece_b(j):
    return jnp.where(j == 0, GAP_AT + GAP, 1232 * j + GAP_AT + GAP)


W_IN = WeightRows(NP, 1232, D_MODEL, ((0, GAP_AT, _w_in_piece_a), (GAP_AT, 1232 - GAP_AT, _w_in_piece_b)),
                  ((GAP_AT, GAP),))
W_OUT = WeightRows(2048, 512, D_MODEL, ((0, 512, lambda j: 512 * j),), ())
W_UQ = WeightRows(2048, 384, Q_LORA, ((0, 192, lambda j: 512 * j), (192, 192, lambda j: 512 * j + 256)),
                  tuple((256 * h + 192, 64) for h in range(N_HEADS)))
W_UKV = WeightRows(2048, 512, KV_LORA, ((0, 512, lambda j: 512 * j),), ())
W_CONV = WeightRows(64, 16, 256, ((0, 16, lambda j: 16 * j),), ())
SHARDED = (W_IN, W_OUT, W_UQ, W_UKV)
SHARDED_NAMES = ("w_in", "w_out", "w_uq", "w_ukv")
WEIGHT_ROWS = dict(zip(SHARDED_NAMES, SHARDED))


def _mesh_pos():
    x, y, c = lax.axis_index("x"), lax.axis_index("y"), lax.axis_index("c")
    return x, y, c


def _other_chips(x, y):
    return [(1 - x, y), (x, 1 - y), (1 - x, 1 - y)]


def _rows(start, n):
    return pl.ds(pl.multiple_of(start, 16), n)


def _half_cols(spec, c):
    ch = spec.cols // 2
    return pl.ds(pl.multiple_of(c * ch, LANE), ch)


def _allgather_script(specs, shards, zeros, layers):
    na = len(specs)
    zlist = [a for a in range(na) if zeros[a] is not None]
    n_layers = [shards[a].shape[0] if layers[a] is None else 1 for a in range(na)]
    plan_first, plan_own, plan_zero = [], [], []
    for a, spec in enumerate(specs):
        for p in range(len(spec.pieces)):
            plan_own.append((a, p))
            for k in range(3):
                plan_first.append((a, p, k))
        for z in range(len(spec.zero_rows)):
            for l in range(n_layers[a]):
                plan_zero.append((a, z, l))
    nf = len(plan_first)
    n_sems = 2 * nf + len(plan_own) + len(plan_zero)

    def copies(ins_all, outs, send_sems, recv_sems):
        ins = [ins_all[a] if layers[a] is None else ins_all[a].at[pl.ds(layers[a], 1)] for a in range(na)]
        zrefs = dict(zip(zlist, ins_all[na:]))
        x, y, c = _mesh_pos()
        j = 2 * x + y
        chips = _other_chips(x, y)
        sibling = (x, y, 1 - c)

        def remote(src, dst, sem, to):
            return pltpu.make_async_remote_copy(src_ref=src, dst_ref=dst, send_sem=send_sems.at[sem],
                                                recv_sem=recv_sems.at[sem], device_id=to, device_id_type=MESH)

        def block(a, p, chip, cols):
            _, n, dst = specs[a].pieces[p]
            return outs[a].at[:, _rows(dst(chip), n), cols]

        def first(i):
            a, p, k = plan_first[i]
            src0, n, _ = specs[a].pieces[p]
            cols = _half_cols(specs[a], c)
            return remote(ins[a].at[:, pl.ds(src0, n), cols], block(a, p, j, cols), i, (*chips[k], c))

        def landed(i, half):
            a, p, k = plan_first[i]
            return block(a, p, 2 * chips[k][0] + chips[k][1], _half_cols(specs[a], half))

        def arrival(i, half, sem):
            return remote(landed(i, half), landed(i, half), sem, sibling)

        def passed(i):
            return remote(landed(i, c), landed(i, c), nf + i, sibling)

        def own(i):
            a, p = plan_own[i]
            src0, n, _ = specs[a].pieces[p]
            return remote(ins[a].at[:, pl.ds(src0, n), :], block(a, p, j, slice(None)), 2 * nf + i, sibling)

        def zero(i):
            a, z, l = plan_zero[i]
            r0, n = specs[a].zero_rows[z]
            return remote(zrefs[a].at[pl.ds(0, n), :], outs[a].at[l, pl.ds(r0, n), :],
                          2 * nf + len(plan_own) + i, sibling)

        fixed = [own(i) for i in range(len(plan_own))] + [zero(i) for i in range(len(plan_zero))]
        return c, fixed, first, arrival, passed

    def start(ins, outs, send_sems, recv_sems):
        _, fixed, first, _, _ = copies(ins, outs, send_sems, recv_sems)
        for cp in fixed:
            cp.start()
        for i in range(nf):
            first(i).start()

    def finish(ins, outs, send_sems, recv_sems):
        c, fixed, first, arrival, passed = copies(ins, outs, send_sems, recv_sems)
        for i in range(nf):
            arrival(i, c, i).wait_recv()
            passed(i).start()
        for i in range(nf):
            arrival(i, 1 - c, nf + i).wait_recv()
        for cp in fixed:
            cp.wait()
        for i in range(nf):
            first(i).wait_send()
            passed(i).wait_send()

    out_shape = tuple(jax.ShapeDtypeStruct((n_layers[a], spec.full_rows, spec.cols), BF16)
                      for a, spec in enumerate(specs))
    args = tuple(shards) + tuple(zeros[a] for a in zlist)
    return CommScript(args, out_shape, n_sems, start, finish)


def _start_all_wait_all(args, out_shape, n_sems, make_copies):
    def start(ins, outs, send_sems, recv_sems):
        for cp in make_copies(ins, outs, send_sems, recv_sems):
            cp.start()

    def finish(ins, outs, send_sems, recv_sems):
        for cp in make_copies(ins, outs, send_sems, recv_sems):
            cp.wait()

    return CommScript(tuple(args), tuple(out_shape), n_sems, start, finish)


def _exchange_script(specs, grads):
    na = len(grads)

    def make_copies(ins, outs, send_sems, recv_sems):
        x, y, c = _mesh_pos()
        return [pltpu.make_async_remote_copy(
            src_ref=ins[a].at[:, _half_cols(specs[a], 1 - c)], dst_ref=outs[a], send_sem=send_sems.at[a],
            recv_sem=recv_sems.at[a], device_id=(x, y, 1 - c), device_id_type=MESH) for a in range(na)]

    out_shape = [jax.ShapeDtypeStruct((s.full_rows, s.cols // 2), F32) for s in specs]
    return _start_all_wait_all(grads, out_shape, na, make_copies)


def _scatter_script(specs, parts):
    na = len(parts)
    plan = [(a, p, k) for a in range(na) for p in range(len(specs[a].pieces)) for k in range(3)]

    def make_copies(ins, outs, send_sems, recv_sems):
        x, y, c = _mesh_pos()
        chips = _other_chips(x, y)
        copies = []
        for i, (a, p, k) in enumerate(plan):
            src0, n, dst = specs[a].pieces[p]
            pk = 2 * chips[k][0] + chips[k][1]
            copies.append(pltpu.make_async_remote_copy(
                src_ref=ins[a].at[_rows(dst(pk), n), :], dst_ref=outs[a].at[k, pl.ds(src0, n), :],
                send_sem=send_sems.at[i], recv_sem=recv_sems.at[i], device_id=(*chips[k], c), device_id_type=MESH))
        return copies

    out_shape = [jax.ShapeDtypeStruct((3, s.own_rows, s.cols // 2), BF16) for s in specs]
    return _start_all_wait_all(parts, out_shape, len(plan), make_copies)


def _chip_sum(spec, part, recv, *, name):
    ch = spec.cols // 2
    npieces = len(spec.pieces)

    def kern(recv_ref, part_ref, o_ref, own_ref, sems):
        j = 2 * lax.axis_index("x") + lax.axis_index("y")
        copies = []
        for p, (src0, n, dst) in enumerate(spec.pieces):
            copies.append(pltpu.make_async_copy(part_ref.at[_rows(dst(j), n), :], own_ref.at[pl.ds(src0, n), :],
                                                sems.at[p]))
        for cp in copies:
            cp.start()
        for cp in copies:
            cp.wait()
        o_ref[...] = ((own_ref[...].astype(F32) + recv_ref[0].astype(F32)) + recv_ref[1].astype(F32)) \
            + recv_ref[2].astype(F32)

    vm = pl.BlockSpec(memory_space=pltpu.VMEM)
    return _pcall(kern, name=name, out_shape=jax.ShapeDtypeStruct((spec.own_rows, ch), F32),
                  in_specs=[vm, HBM_SPEC], out_specs=vm,
                  scratch=[pltpu.VMEM((spec.own_rows, ch), BF16), pltpu.SemaphoreType.DMA((npieces,))],
                  vmem_mb=48)(recv, part)


def _sibling_script(sums):
    na = len(sums)

    def make_copies(ins, outs, send_sems, recv_sems):
        x, y, c = _mesh_pos()
        return [pltpu.make_async_remote_copy(
            src_ref=ins[a], dst_ref=outs[a], send_sem=send_sems.at[a], recv_sem=recv_sems.at[a],
            device_id=(x, y, 1 - c), device_id_type=MESH) for a in range(na)]

    out_shape = [jax.ShapeDtypeStruct(t.shape, t.dtype) for t in sums]
    return _start_all_wait_all(sums, out_shape, na, make_copies)


class _SemWindow:
    def __init__(self, sems, offset):
        self._sems, self._offset = sems, offset

    @property
    def at(self):
        return self

    def __getitem__(self, i):
        return self._sems.at[i + self._offset]


def _merge_scripts(*scripts):
    a_off, o_off, s_off = [0], [0], [0]
    for s in scripts:
        a_off.append(a_off[-1] + len(s.args))
        o_off.append(o_off[-1] + len(s.out_shape))
        s_off.append(s_off[-1] + s.n_sems)

    def phase(which):
        def run(ins, outs, send_sems, recv_sems):
            for n, s in enumerate(scripts):
                getattr(s, which)(ins[a_off[n]:a_off[n + 1]], outs[o_off[n]:o_off[n + 1]],
                                  _SemWindow(send_sems, s_off[n]), _SemWindow(recv_sems, s_off[n]))
        return run

    return CommScript(sum((tuple(s.args) for s in scripts), ()), sum((tuple(s.out_shape) for s in scripts), ()),
                      s_off[-1], phase("start"), phase("finish"))


class _GradReducer:
    def __init__(self, layer, names, grads, c_idx):
        self.specs = tuple(WEIGHT_ROWS[nm] for nm in names)
        self.grads, self.c_idx = tuple(grads), c_idx
        self.names = [f"{nm}{layer}" for nm in names]

    def exchange(self):
        return _exchange_script(self.specs, self.grads)

    def scatter(self, theirs):
        self.parts = tuple(_pair_sum(g, th, self.c_idx, name=f"pair_sum_{nm}")
                           for g, th, nm in zip(self.grads, theirs, self.names))
        return _scatter_script(self.specs, self.parts)

    def sibling(self, recv):
        self.sums = tuple(_chip_sum(s, p, r, name=f"chip_sum_{nm}")
                          for s, p, r, nm in zip(self.specs, self.parts, recv, self.names))
        return _sibling_script(self.sums)

    def done(self, others):
        return list(zip(self.sums, others))


def _allgather_small_script(block):
    m_per, n = block.shape

    def copies(ins, outs, send_sems, recv_sems):
        (x_ref,), (out_ref,) = ins, outs
        x, y, c = _mesh_pos()
        me, sibling = (x, y, c), (x, y, 1 - c)
        chips = _other_chips(x, y)

        def rows(px, py, pc):
            return out_ref.at[4 * px + 2 * py + pc]

        def copy(k, blk, to, src=None):
            return pltpu.make_async_remote_copy(
                src_ref=rows(*blk) if src is None else src, dst_ref=rows(*blk), send_sem=send_sems.at[k],
                recv_sem=recv_sems.at[k], device_id=to, device_id_type=MESH)

        first = [copy(0, me, sibling, src=x_ref)]
        first += [copy(1 + k, me, (*chip, c), src=x_ref) for k, chip in enumerate(chips)]
        passed = [copy(4 + k, (*chip, c), sibling) for k, chip in enumerate(chips)]
        landed = [copy(1 + k, (*chip, c), me) for k, chip in enumerate(chips)]
        from_sibling = [copy(0, sibling, me)] + [copy(4 + k, (*chip, 1 - c), me) for k, chip in enumerate(chips)]
        return first, passed, landed, from_sibling

    def start(ins, outs, send_sems, recv_sems):
        first, _, _, _ = copies(ins, outs, send_sems, recv_sems)
        for cp in first:
            cp.start()

    def finish(ins, outs, send_sems, recv_sems):
        first, passed, landed, from_sibling = copies(ins, outs, send_sems, recv_sems)
        for k in range(3):
            landed[k].wait_recv()
            passed[k].start()
        for cp in from_sibling:
            cp.wait_recv()
        for cp in first + passed:
            cp.wait_send()

    return CommScript((block,), (jax.ShapeDtypeStruct((N_DEV, m_per, n), block.dtype),), 7, start, finish)


def _rope_tables(positions):
    half = ROPE // 2
    inv_freq = ROPE_THETA ** (-jnp.arange(half, dtype=F32) / half)
    ang = positions.astype(F32)[:, None] * inv_freq
    cos, sin = jnp.cos(ang), jnp.sin(ang)
    S = positions.shape[0]
    cos_t = jnp.concatenate([cos, cos, jnp.ones((S, 64), F32)], axis=1)
    sin_t = jnp.concatenate([-sin, sin, jnp.zeros((S, 64), F32)], axis=1)
    return cos_t, sin_t


def _decode_conv(bits):
    rows = bits.reshape(DEPTH, N_CHIPS, 16, 256)[:, :, :3, :]
    conv = lax.bitcast_convert_type(rows.reshape(DEPTH, N_CHIPS, 3, 128, 2), F32)
    return jnp.transpose(conv, (0, 2, 1, 3)).reshape(DEPTH, 3, 512)


def _local_step(x, positions, target, emb_g, emb_b, w_in_t0, rest0, weights1, q_g, kv_g, w_pool, pool_scale,
                b_out, ln_g, ln_b, c_idx=None):
    cos_t, sin_t = _rope_tables(positions)
    if isinstance(w_in_t0, CommScript):
        (h, hb), (landed,) = _ln_fwd(x, emb_g, emb_b, name="emb_ln", comm=w_in_t0)
        w_in_t0 = landed[0]
    else:
        h, hb = _ln_fwd(x, emb_g, emb_b, name="emb_ln")
    weights = [None, weights1]
    saved = []
    for l in range(DEPTH):
        if l == 0 and isinstance(rest0, CommScript):
            proj, landed = _matmul(hb, w_in_t0, "nt", name="in_proj0", tm=1024, tn=1024, tk=2048, vmem_mb=56,
                                   comm=rest0)
            weights[0] = (w_in_t0,) + tuple(a[0] for a in landed[:3])
            conv_w = _decode_conv(landed[3])
        else:
            if l == 0:
                weights[0] = (w_in_t0,) + tuple(rest0[:3])
                conv_w = rest0[3]
            proj = _matmul(hb, weights[l][0], "nt", name=f"in_proj{l}", tm=1024, tn=1024, tk=2048, vmem_mb=56)
        w_in_t, w_out, w_uq_t, w_ukv_t = weights[l]
        qc, kc, v, vt, qn, kvn = _mla_qkv(proj, cos_t, sin_t, q_g[l], kv_g[l], w_uq_t, w_ukv_t, name=f"mla_qkv{l}")
        nxt = weights[l + 1] if l + 1 < DEPTH else None
        if isinstance(nxt, CommScript):
            (o, lse2), landed = _flash_fwd(qc, kc, vt, name=f"flash_fwd{l}", comm=nxt)
            weights[l + 1] = tuple(a[0] for a in landed)
        else:
            o, lse2 = _flash_fwd(qc, kc, vt, name=f"flash_fwd{l}")
        mix = _mixer_fwd(proj, o, w_pool[l], pool_scale[l], conv_w[l], name=f"mixer_fwd{l}")
        if l == DEPTH - 1:
            r = _outproj_residual(mix, w_out, h, b_out[l], name=f"out_proj{l}")
            saved.append((hb, proj, qc, kc, v, qn, kvn, o, lse2, mix, r))
        else:
            h_next, hb_next, r = _outproj_ln(mix, w_out, h, b_out[l], ln_g[l], ln_b[l], name=f"out_proj_ln{l}")
            saved.append((hb, proj, qc, kc, v, qn, kvn, o, lse2, mix, r))
            h, hb = h_next, hb_next

    small = [None] * DEPTH
    big = [None] * DEPTH
    above = scatter_above = None
    for l in reversed(range(DEPTH)):
        w_in_t, w_out, w_uq_t, w_ukv_t = weights[l]
        hb_in, proj, qc, kc, v, qn, kvn, o, lse2, mix, r = saved[l]
        if l == DEPTH - 1:
            loss_acc, dr, drb, d_ln_g, d_ln_b, d_b_out = _loss_ln_bwd(target, r, ln_g[l], ln_b[l], name="loss_ln_bwd")
        else:
            dr, drb, d_ln_g, d_ln_b, d_b_out = _ln_bwd(dh, r, ln_g[l], name=f"ln_bwd{l}")
        dmix = _matmul(drb, w_out, "nt", name=f"dmix{l}", tm=1024, tn=1024, tk=2048, vmem_mb=56)
        d_w_out = _matmul(mix, drb, "tn", name=f"dw_out{l}", tm=1024, tn=1024, tk=2048, vmem_mb=56)
        d_mix, do, d_w_pool, d_ps, d_conv = _mixer_bwd(dmix, proj, o, w_pool[l], pool_scale[l], conv_w[l],
                                                       name=f"mixer_bwd{l}")
        delta = _attn_delta(o, do, name=f"attn_delta{l}")
        if above is not None:
            (dqb, dkvb, dkr), recv = _flash_bwd(qc, kc, v, do, lse2, delta, cos_t, sin_t, name=f"flash_bwd{l}",
                                                comm=scatter_above)
            sibling_above = above.sibling(recv)
        else:
            dqb, dkvb, dkr = _flash_bwd(qc, kc, v, do, lse2, delta, cos_t, sin_t, name=f"flash_bwd{l}")
        d_mla, d_qg, d_kvg = _mla_qkv_bwd(dqb, dkvb, dkr, proj, cos_t, sin_t, q_g[l], kv_g[l], w_uq_t, w_ukv_t,
                                          name=f"mla_qkv_bwd{l}")
        d_w_uq_t = _matmul(dqb, qn, "tn", name=f"dw_uq{l}", tm=2048, tn=512, tk=2048, vmem_mb=56)
        d_w_ukv_t = _matmul(dkvb, kvn, "tn", name=f"dw_ukv{l}", tm=2048, tn=256, tk=2048, vmem_mb=56)
        small[l] = dict(q_g=d_qg[0], kv_g=d_kvg[0], w_pool=d_w_pool, pool_scale=d_ps[0], conv_w=d_conv,
                        b_out=d_b_out[0], ln_g=d_ln_g[0], ln_b=d_ln_b[0])
        rest = (d_w_out, d_w_uq_t, d_w_ukv_t)
        if c_idx is None:
            d_w_in_t = _dproj_t_times_h(d_mla, d_mix, hb_in, name=f"dw_in{l}")
            dh = _dproj_times_w(d_mla, d_mix, w_in_t, dr, ALPHA, name=f"dh{l}")
            big[l] = (d_w_in_t,) + rest
        elif l > 0:
            d_w_in_t = _dproj_t_times_h(d_mla, d_mix, hb_in, name=f"dw_in{l}")
            above = _GradReducer(l, SHARDED_NAMES, (d_w_in_t,) + rest, c_idx)
            dh, theirs = _dproj_times_w(d_mla, d_mix, w_in_t, dr, ALPHA, name=f"dh{l}", comm=above.exchange())
            scatter_above = above.scatter(theirs)
        else:
            red_rest = _GradReducer(l, SHARDED_NAMES[1:], rest, c_idx)
            d_w_in_t, landed = _dproj_t_times_h(d_mla, d_mix, hb_in, name=f"dw_in{l}",
                                                comm=_merge_scripts(sibling_above, red_rest.exchange()))
            big[l + 1] = above.done(landed[:len(SHARDED)])
            red_in = _GradReducer(l, SHARDED_NAMES[:1], (d_w_in_t,), c_idx)
            landed = _run_comm(_merge_scripts(red_in.exchange(), red_rest.scatter(landed[len(SHARDED):])),
                               name="exchange_w_in0")
            sibling_rest = red_rest.sibling(landed[1:])
            dh, landed = _dproj_times_w(d_mla, d_mix, w_in_t, dr, ALPHA, name=f"dh{l}",
                                        comm=_merge_scripts(red_in.scatter(landed[:1]), sibling_rest))
            recv_in, others_rest = landed[:1], landed[1:]
    grad_x, _, d_emb_g, d_emb_b, _ = _ln_bwd(dh, x, emb_g, name="emb_ln_bwd", bf16_copy=False)
    if c_idx is not None:
        others_in = _run_comm(red_in.sibling(recv_in), name="send_to_sibling0")
        big[0] = red_in.done(others_in) + red_rest.done(others_rest)
    return loss_acc[0, 0], grad_x, d_emb_g, d_emb_b, small, big


SMALL_ORDER = ("emb_ln_g", "emb_ln_b", "q_norm_g", "kv_norm_g", "w_pool", "pool_scale", "b_out", "ln_g", "ln_b")
SMALL_LAYER_KEYS = ("q_g", "kv_g", "w_pool", "pool_scale", "b_out", "ln_g", "ln_b", "conv_w")


def _pack_small(arrs, extra_rows):
    flat = jnp.concatenate([a.reshape(-1) for a in arrs])
    rows = flat.shape[0] // LANE
    total = -(-(rows + extra_rows) // 8) * 8
    return jnp.pad(flat, (0, total * LANE - flat.shape[0])).reshape(total, LANE)


def kernel(x, positions, emb_ln_g, emb_ln_b, w_in, q_norm_g, kv_norm_g, w_uq, w_ukv, w_pool, pool_scale, conv_w, w_out, b_out, ln_g, ln_b, loss_target, m_emb_ln_g, m_emb_ln_b, m_w_in, m_q_norm_g, m_kv_norm_g, m_w_uq, m_w_ukv, m_w_pool, m_pool_scale, m_conv_w, m_w_out, m_b_out, m_ln_g, m_ln_b, v_emb_ln_g, v_emb_ln_b, v_w_in, v_q_norm_g, v_kv_norm_g, v_w_uq, v_w_ukv, v_w_pool, v_pool_scale, v_conv_w, v_w_out, v_b_out, v_ln_g, v_ln_b):
    xi, yi, ci = lax.axis_index("x"), lax.axis_index("y"), lax.axis_index("c")
    chip = 2 * xi + yi
    c_idx = ci.reshape(1).astype(jnp.int32)

    def t(a):
        return jnp.swapaxes(a, 1, 2)

    conv_bits = lax.bitcast_convert_type(conv_w.reshape(DEPTH, 3 * 128), BF16).reshape(DEPTH, 3, 256)
    conv_bits = jnp.pad(conv_bits, ((0, 0), (0, 13), (0, 0)))
    own = (t(w_in).astype(BF16), w_out.astype(BF16), t(w_uq).astype(BF16), t(w_ukv).astype(BF16))
    zeros = (jnp.zeros((GAP, D_MODEL), BF16), None, jnp.zeros((64, Q_LORA), BF16), None)
    gather_in0 = _allgather_script((W_IN,), own[:1], zeros[:1], (0,))
    gather0 = _allgather_script(SHARDED[1:] + (W_CONV,), own[1:] + (conv_bits,), zeros[1:] + (None,),
                                (0, 0, 0, None))
    gather1 = _allgather_script(SHARDED, own, zeros, (1, 1, 1, 1))

    loss_part, grad_x, d_emb_g, d_emb_b, grads, reduced = _local_step(
        x[0], positions[0], loss_target[0], emb_ln_g, emb_ln_b, gather_in0, gather0, gather1, q_norm_g, kv_norm_g,
        w_pool, pool_scale, b_out, ln_g, ln_b, c_idx)

    def rows(a):
        return a.reshape(1, -1) if a.ndim == 1 else a

    small_wmv = [tuple(rows(a) for a in wmv) for wmv in (
        (emb_ln_g, m_emb_ln_g, v_emb_ln_g), (emb_ln_b, m_emb_ln_b, v_emb_ln_b),
        (q_norm_g, m_q_norm_g, v_q_norm_g), (kv_norm_g, m_kv_norm_g, v_kv_norm_g), (w_pool, m_w_pool, v_w_pool),
        (pool_scale, m_pool_scale, v_pool_scale), (b_out, m_b_out, v_b_out), (ln_g, m_ln_g, v_ln_g),
        (ln_b, m_ln_b, v_ln_b))]
    packed_g = _pack_small(
        [d_emb_g, d_emb_b] + [jnp.stack([grads[l][key] for l in range(DEPTH)]) for key in SMALL_LAYER_KEYS]
        + [jnp.pad(loss_part.reshape(1), (0, LANE - 1))], 0)
    (gathered,) = _run_comm(_allgather_small_script(packed_g), name="allgather_small")
    g_tot, small_upd = _small_sum_adamw(gathered, packed_g, small_wmv, name="small_sum_adamw")
    off = sum(w.size for w, _, _ in small_wmv)
    flat_tot = g_tot.reshape(-1)

    def halves(a):
        return [reduced[l][a] for l in range(DEPTH)]

    upd = {}
    upd["w_in"] = tuple(t(o) for o in _adamw_halves(t(w_in), t(m_w_in), t(v_w_in), halves(0), c_idx,
                                                    name="adamw_w_in"))
    conv_tot = flat_tot[off:off + DEPTH * 3 * 512].reshape(DEPTH, 3, 512)
    loss = flat_tot[off + DEPTH * 3 * 512]
    g_conv = lax.dynamic_slice_in_dim(conv_tot, chip * 128, 128, axis=2)

    def whole(a):
        return jnp.stack([jnp.where(ci == 0, jnp.concatenate([mine, oth], axis=1),
                                    jnp.concatenate([oth, mine], axis=1)) for mine, oth in halves(a)])

    upd["w_out"] = _adamw_halves(w_out, m_w_out, v_w_out, halves(1), c_idx, name="adamw_w_out")
    g_uq, g_ukv = t(whole(2)), t(whole(3))
    upd["w_uq"] = (g_uq,) + _adamw(w_uq, g_uq, m_w_uq, v_w_uq, name="adamw_w_uq")
    upd["w_ukv"] = (g_ukv,) + _adamw(w_ukv, g_ukv, m_w_ukv, v_w_ukv, name="adamw_w_ukv")
    upd["conv_w"] = (g_conv,) + _adamw(conv_w, g_conv, m_conv_w, v_conv_w, name="adamw_conv_w")
    for nm, res in zip(SMALL_ORDER, small_upd):
        upd[nm] = tuple(a.reshape(-1) for a in res) if nm in ("emb_ln_g", "emb_ln_b") else res

    order = ("emb_ln_g", "emb_ln_b", "w_in", "q_norm_g", "kv_norm_g", "w_uq", "w_ukv", "w_pool", "pool_scale",
             "conv_w", "w_out", "b_out", "ln_g", "ln_b")
    outs = [loss, grad_x[None]]
    for field in range(4):
        outs += [upd[nm][field] for nm in order]
    return tuple(outs)
```

```python
import collections

import jax
import jax.numpy as jnp
from jax import lax
from jax.experimental import pallas as pl
from jax.experimental.pallas import tpu as pltpu

F32 = jnp.float32
BF16 = jnp.bfloat16
MESH = pl.DeviceIdType.MESH

D_MODEL = 2048
DEPTH = 2
N_HEADS = 8
NOPE = 128
ROPE = 64
Q_LORA = 512
KV_LORA = 256
D_MLA = 1024
POOL_WINDOWS = (2, 4, 8, 16)
D_IN_PROJ = 4928
LN_EPS = 1e-5
RMS_EPS = 1e-6
ROPE_THETA = 10000.0
ALPHA = (2 * DEPTH) ** 0.25
SCALE = (NOPE + ROPE) ** -0.5
LOG2E = 1.4426950408889634
SCALE_LOG2E = SCALE * LOG2E
ADAM_LR = 0.001
ADAM_B1 = 0.9
ADAM_B2 = 0.999
ADAM_EPS = 1e-08
ADAM_WD = 0.01
ADAM_STEP = 10

NP = 5120
GAP_AT = 832
GAP = NP - D_IN_PROJ
W_MLA = 1024
W_MIX = NP - W_MLA
HALO = 32
LANE = 128
N_CHIPS = 4
N_DEV = 8
TQ = 512
FWD_GROUP = 4

NN = (((1,), (0,)), ((), ()))
NT = (((1,), (1,)), ((), ()))
TN = (((0,), (0,)), ((), ()))


CommScript = collections.namedtuple("CommScript", "args out_shape n_sems start finish")
HBM_SPEC = pl.BlockSpec(memory_space=pl.ANY)


def _pcall(kern, *, name, out_shape, grid=None, in_specs=None, out_specs=None, scratch=(), dims=None,
           vmem_mb=None, comm=None):
    cp = {}
    if dims is not None:
        cp["dimension_semantics"] = dims if comm is None else ("arbitrary",) * len(dims)
    if vmem_mb is not None:
        cp["vmem_limit_bytes"] = vmem_mb << 20
    if comm is None:
        args = dict(name=name, out_shape=out_shape, scratch_shapes=list(scratch),
                    compiler_params=pltpu.CompilerParams(**cp))
        if grid is not None:
            args["grid"] = grid
        if in_specs is not None:
            args["in_specs"] = in_specs
        if out_specs is not None:
            args["out_specs"] = out_specs
        return pl.pallas_call(kern, **args)

    single = not isinstance(out_shape, (tuple, list))
    own_out = (out_shape,) if single else tuple(out_shape)
    own_out_specs = (out_specs,) if single else tuple(out_specs)
    n_in, n_out, n_scr = len(in_specs), len(own_out), len(scratch)
    na, no = len(comm.args), len(comm.out_shape)

    def at(end):
        cond = None
        for d, n in enumerate(grid):
            here = pl.program_id(d) == (n - 1 if end else 0)
            cond = here if cond is None else jnp.logical_and(cond, here)
        return cond

    def wrapped(*refs):
        own_in, c_in = refs[:n_in], refs[n_in:n_in + na]
        o0 = n_in + na
        own_o, c_out = refs[o0:o0 + n_out], refs[o0 + n_out:o0 + n_out + no]
        s0 = o0 + n_out + no
        own_s, (send_sems, recv_sems) = refs[s0:s0 + n_scr], refs[s0 + n_scr:]

        @pl.when(at(False))
        def _():
            comm.start(c_in, c_out, send_sems, recv_sems)

        kern(*own_in, *own_o, *own_s)

        @pl.when(at(True))
        def _():
            comm.finish(c_in, c_out, send_sems, recv_sems)

    call = pl.pallas_call(
        wrapped, name=name, out_shape=own_out + tuple(comm.out_shape), grid=grid,
        in_specs=list(in_specs) + [HBM_SPEC] * na, out_specs=own_out_specs + (HBM_SPEC,) * no,
        scratch_shapes=list(scratch) + [pltpu.SemaphoreType.DMA((comm.n_sems,)),
                                        pltpu.SemaphoreType.DMA((comm.n_sems,))],
        compiler_params=pltpu.CompilerParams(**cp))

    def run(*args):
        res = call(*args, *comm.args)
        own = res[0] if single else tuple(res[:n_out])
        return own, tuple(res[n_out:])

    return run


def _run_comm(script, *, name):
    na, no = len(script.args), len(script.out_shape)

    def body(*refs):
        ins, outs = refs[:na], refs[na:na + no]
        send_sems, recv_sems = refs[na + no:]
        script.start(ins, outs, send_sems, recv_sems)
        script.finish(ins, outs, send_sems, recv_sems)

    return pl.pallas_call(
        body, name=name, out_shape=tuple(script.out_shape), in_specs=[HBM_SPEC] * na, out_specs=(HBM_SPEC,) * no,
        scratch_shapes=[pltpu.SemaphoreType.DMA((script.n_sems,)), pltpu.SemaphoreType.DMA((script.n_sems,))])(
            *script.args)


def _sigmoid(g):
    return 1.0 / (1.0 + jnp.exp(-g))


def _silu_and_grad(g):
    sig = _sigmoid(g)
    return g * sig, sig * (1.0 + g * (1.0 - sig))


def _matmul(a, b, mode, *, name, tm, tn, tk, out_dtype=F32, vmem_mb=48, comm=None):
    if mode == "nn":
        (M, K), N = a.shape, b.shape[1]
    elif mode == "nt":
        (M, K), N = a.shape, b.shape[0]
    else:
        (K, M), N = a.shape, b.shape[1]
    tm, tn, tk = min(tm, M), min(tn, N), min(tk, K)
    assert M % tm == 0 and N % tn == 0 and K % tk == 0, (name, M, N, K)
    nk = K // tk
    dn = {"nn": NN, "nt": NT, "tn": TN}[mode]
    if mode == "tn":
        a_spec = pl.BlockSpec((tk, tm), lambda i, j, k: (k, i))
    else:
        a_spec = pl.BlockSpec((tm, tk), lambda i, j, k: (i, k))
    if mode == "nt":
        b_spec = pl.BlockSpec((tn, tk), lambda i, j, k: (j, k))
    else:
        b_spec = pl.BlockSpec((tk, tn), lambda i, j, k: (k, j))
    o_spec = pl.BlockSpec((tm, tn), lambda i, j, k: (i, j))

    def kern(a_ref, b_ref, o_ref, *rest):
        part = lax.dot_general(a_ref[...].astype(BF16), b_ref[...].astype(BF16), dn,
                               preferred_element_type=F32)
        if nk == 1:
            o_ref[...] = part.astype(out_dtype)
        else:
            acc_ref = rest[0]
            k = pl.program_id(2)

            @pl.when(k == 0)
            def _():
                acc_ref[...] = part

            @pl.when(k > 0)
            def _():
                acc_ref[...] += part

            @pl.when(k == nk - 1)
            def _():
                o_ref[...] = acc_ref[...].astype(out_dtype)

    scratch = [pltpu.VMEM((tm, tn), F32)] if nk > 1 else []
    return _pcall(kern, name=name, out_shape=jax.ShapeDtypeStruct((M, N), out_dtype),
                  grid=(M // tm, N // tn, nk), in_specs=[a_spec, b_spec], out_specs=o_spec, scratch=scratch,
                  dims=("parallel", "parallel", "arbitrary"), vmem_mb=vmem_mb, comm=comm)(a, b)


def _dproj_times_w(d_mla, d_mix, wt, add, add_scale, *, name, comm=None):
    S = d_mla.shape[0]
    Dm = wt.shape[1]
    tm, tn, tk = min(1024, S), 1024, 2048
    nk = 1 + W_MIX // tk

    def kern(a1_ref, a2_ref, b1_ref, b2_ref, add_ref, o_ref, acc_ref):
        k = pl.program_id(2)

        @pl.when(k == 0)
        def _():
            acc_ref[...] = jnp.dot(a1_ref[...], b1_ref[...], preferred_element_type=F32)

        @pl.when(k > 0)
        def _():
            acc_ref[...] += jnp.dot(a2_ref[...], b2_ref[...], preferred_element_type=F32)

        @pl.when(k == nk - 1)
        def _():
            o_ref[...] = add_scale * add_ref[...] + acc_ref[...]

    o_spec = pl.BlockSpec((tm, tn), lambda i, j, k: (i, j))
    b2_spec = pl.BlockSpec((pl.Element(tk), pl.Element(tn)),
                           lambda i, j, k: (pl.multiple_of(W_MLA + tk * jnp.maximum(k - 1, 0), W_MLA),
                                            pl.multiple_of(j * tn, tn)))
    return _pcall(kern, name=name, out_shape=jax.ShapeDtypeStruct((S, Dm), F32), grid=(S // tm, Dm // tn, nk),
                  in_specs=[pl.BlockSpec((tm, W_MLA), lambda i, j, k: (i, 0)),
                            pl.BlockSpec((tm, tk), lambda i, j, k: (i, jnp.maximum(k - 1, 0))),
                            pl.BlockSpec((W_MLA, tn), lambda i, j, k: (0, j)), b2_spec, o_spec],
                  out_specs=o_spec, scratch=[pltpu.VMEM((tm, tn), F32)],
                  dims=("parallel", "parallel", "arbitrary"), vmem_mb=56, comm=comm)(d_mla, d_mix, wt, wt, add)


def _dproj_t_times_h(d_mla, d_mix, h, *, name, comm=None):
    S, Dm = h.shape
    tm, tn, tk = W_MLA, 1024, min(2048, S)
    nk = S // tk

    def kern(a1_ref, a2_ref, b_ref, o_ref, acc_ref):
        i = pl.program_id(0)
        k = pl.program_id(2)
        b = b_ref[...].astype(BF16)

        def accumulate(part):
            @pl.when(k == 0)
            def _():
                acc_ref[...] = part

            @pl.when(k > 0)
            def _():
                acc_ref[...] += part

        @pl.when(i == 0)
        def _():
            accumulate(lax.dot_general(a1_ref[...], b, TN, preferred_element_type=F32))

        @pl.when(i > 0)
        def _():
            accumulate(lax.dot_general(a2_ref[...], b, TN, preferred_element_type=F32))

        @pl.when(k == nk - 1)
        def _():
            o_ref[...] = acc_ref[...]

    return _pcall(kern, name=name, out_shape=jax.ShapeDtypeStruct((NP, Dm), F32), grid=(NP // tm, Dm // tn, nk),
                  in_specs=[pl.BlockSpec((tk, tm), lambda i, j, k: (jnp.where(i == 0, k, nk - 1), 0)),
                            pl.BlockSpec((tk, tm), lambda i, j, k: (jnp.where(i == 0, 0, k), jnp.maximum(i - 1, 0))),
                            pl.BlockSpec((tk, tn), lambda i, j, k: (k, j))],
                  out_specs=pl.BlockSpec((tm, tn), lambda i, j, k: (i, j)), scratch=[pltpu.VMEM((tm, tn), F32)],
                  dims=("parallel", "parallel", "arbitrary"), vmem_mb=48, comm=comm)(d_mla, d_mix, h)


def _ln_fwd(x, g, b, *, name, comm=None):
    S, Dm = x.shape
    tm = min(512, S)

    def kern(x_ref, g_ref, b_ref, y_ref, yb_ref):
        xf = x_ref[...]
        mu = jnp.mean(xf, axis=-1, keepdims=True)
        xc = xf - mu
        var = jnp.mean(xc * xc, axis=-1, keepdims=True)
        y = xc * lax.rsqrt(var + LN_EPS) * g_ref[...] + b_ref[...]
        y_ref[...] = y
        yb_ref[...] = y.astype(BF16)

    row = pl.BlockSpec((tm, Dm), lambda i: (i, 0))
    vec = pl.BlockSpec((1, Dm), lambda i: (0, 0))
    return _pcall(kern, name=name,
                  out_shape=(jax.ShapeDtypeStruct((S, Dm), F32), jax.ShapeDtypeStruct((S, Dm), BF16)),
                  grid=(S // tm,), in_specs=[row, vec, vec], out_specs=(row, row), dims=("parallel",), vmem_mb=48,
                  comm=comm)(
                      x, g.reshape(1, Dm), b.reshape(1, Dm))


def _ln_bwd(dy, r, g, *, name, bf16_copy=True):
    S, Dm = r.shape
    tm = min(512, S)

    def kern(dy_ref, r_ref, g_ref, dr_ref, *rest):
        drb_ref = rest[0] if bf16_copy else None
        dg_ref, db_ref, ds_ref = rest[-3:]

        @pl.when(pl.program_id(0) == 0)
        def _():
            dg_ref[...] = jnp.zeros_like(dg_ref)
            db_ref[...] = jnp.zeros_like(db_ref)
            ds_ref[...] = jnp.zeros_like(ds_ref)

        rf = r_ref[...]
        dyf = dy_ref[...]
        mu = jnp.mean(rf, axis=-1, keepdims=True)
        xc = rf - mu
        var = jnp.mean(xc * xc, axis=-1, keepdims=True)
        rstd = lax.rsqrt(var + LN_EPS)
        xhat = xc * rstd
        dxh = dyf * g_ref[...]
        c1 = jnp.mean(dxh, axis=-1, keepdims=True)
        c2 = jnp.mean(dxh * xhat, axis=-1, keepdims=True)
        dr = rstd * (dxh - c1 - xhat * c2)
        dr_ref[...] = dr
        if bf16_copy:
            drb_ref[...] = dr.astype(BF16)
        dg_ref[...] += jnp.sum(dyf * xhat, axis=0, keepdims=True)
        db_ref[...] += jnp.sum(dyf, axis=0, keepdims=True)
        ds_ref[...] += jnp.sum(dr, axis=0, keepdims=True)

    row = pl.BlockSpec((tm, Dm), lambda i: (i, 0))
    vec = pl.BlockSpec((1, Dm), lambda i: (0, 0))
    vshape = jax.ShapeDtypeStruct((1, Dm), F32)
    copies = ((jax.ShapeDtypeStruct((S, Dm), BF16),), (row,)) if bf16_copy else ((), ())
    res = _pcall(kern, name=name,
                 out_shape=(jax.ShapeDtypeStruct((S, Dm), F32),) + copies[0] + (vshape, vshape, vshape),
                 grid=(S // tm,), in_specs=[row, row, vec], out_specs=(row,) + copies[1] + (vec, vec, vec),
                 dims=("arbitrary",), vmem_mb=48)(dy, r, g.reshape(1, Dm))
    return res if bf16_copy else (res[0], None) + tuple(res[1:])


def _loss_ln_bwd(target, r, g, b, *, name):
    S, Dm = r.shape
    tm = min(512, S)

    def kern(t_ref, r_ref, g_ref, b_ref, l_ref, dr_ref, drb_ref, dg_ref, db_ref, ds_ref):
        @pl.when(pl.program_id(0) == 0)
        def _():
            l_ref[...] = jnp.zeros_like(l_ref)
            dg_ref[...] = jnp.zeros_like(dg_ref)
            db_ref[...] = jnp.zeros_like(db_ref)
            ds_ref[...] = jnp.zeros_like(ds_ref)

        rf = r_ref[...]
        mu = jnp.mean(rf, axis=-1, keepdims=True)
        xc = rf - mu
        var = jnp.mean(xc * xc, axis=-1, keepdims=True)
        rstd = lax.rsqrt(var + LN_EPS)
        xhat = xc * rstd
        e = (xhat * g_ref[...] + b_ref[...]) - t_ref[...]
        dyf = e / float(Dm)
        per_row = jnp.mean(e * e, axis=-1, keepdims=True)
        l_ref[...] += 0.5 * jnp.sum(per_row, axis=0, keepdims=True)
        dxh = dyf * g_ref[...]
        c1 = jnp.mean(dxh, axis=-1, keepdims=True)
        c2 = jnp.mean(dxh * xhat, axis=-1, keepdims=True)
        dr = rstd * (dxh - c1 - xhat * c2)
        dr_ref[...] = dr
        drb_ref[...] = dr.astype(BF16)
        dg_ref[...] += jnp.sum(dyf * xhat, axis=0, keepdims=True)
        db_ref[...] += jnp.sum(dyf, axis=0, keepdims=True)
        ds_ref[...] += jnp.sum(dr, axis=0, keepdims=True)

    row = pl.BlockSpec((tm, Dm), lambda i: (i, 0))
    vec = pl.BlockSpec((1, Dm), lambda i: (0, 0))
    acc = pl.BlockSpec((8, LANE), lambda i: (0, 0))
    vshape = jax.ShapeDtypeStruct((1, Dm), F32)
    return _pcall(kern, name=name,
                  out_shape=(jax.ShapeDtypeStruct((8, LANE), F32), jax.ShapeDtypeStruct((S, Dm), F32),
                             jax.ShapeDtypeStruct((S, Dm), BF16), vshape, vshape, vshape),
                  grid=(S // tm,), in_specs=[row, row, vec, vec], out_specs=(acc, row, row, vec, vec, vec),
                  dims=("arbitrary",), vmem_mb=56)(target, r, g.reshape(1, Dm), b.reshape(1, Dm))


def _rot_sum(t):
    return pltpu.roll(t, 32, 1) + pltpu.roll(t, 96, 1)


def _mla_qkv(proj, cos_t, sin_t, qg, kvg, wuq_t, wukv_t, *, name):
    S = proj.shape[0]
    tm = min(256, S)

    def kern(ql_ref, kvl_ref, kr_ref, cos_ref, sin_ref, qg_ref, kvg_ref, wuq_ref, wukv_ref,
             qc_ref, kc_ref, v_ref, vt_ref, qn_ref, kvn_ref):
        cosv = cos_ref[...]
        sinv = sin_ref[...]

        def rope(t):
            return t * cosv + _rot_sum(t) * sinv

        ql = ql_ref[...]
        qn = (ql * lax.rsqrt(jnp.mean(ql * ql, axis=-1, keepdims=True) + RMS_EPS) * qg_ref[...]).astype(BF16)
        kvl = kvl_ref[...]
        kvn = (kvl * lax.rsqrt(jnp.mean(kvl * kvl, axis=-1, keepdims=True) + RMS_EPS) * kvg_ref[...]).astype(BF16)
        qn_ref[...] = qn
        kvn_ref[...] = kvn
        q = lax.dot_general(qn, wuq_ref[...], NT, preferred_element_type=F32)
        kv = lax.dot_general(kvn, wukv_ref[...], NT, preferred_element_type=F32)
        kr = rope(kr_ref[...]).astype(BF16)
        for h in range(N_HEADS):
            c0 = 256 * h
            qc_ref[:, c0:c0 + 128] = q[:, c0:c0 + 128].astype(BF16)
            qc_ref[:, c0 + 128:c0 + 256] = rope(q[:, c0 + 128:c0 + 256]).astype(BF16)
            kc_ref[:, c0:c0 + 128] = kv[:, c0:c0 + 128].astype(BF16)
            kc_ref[:, c0 + 128:c0 + 256] = kr
            vh = kv[:, c0 + 128:c0 + 256]
            v_ref[:, 128 * h:128 * h + 128] = vh.astype(BF16)
            vt_ref[h] = jnp.transpose(vh).astype(BF16)

    def row(w, blk):
        return pl.BlockSpec((tm, w), lambda i: (i, blk))

    def full(shape):
        return pl.BlockSpec(shape, lambda i: (0,) * len(shape))

    t = min(TQ, S)
    per = t // tm
    vt_spec = pl.BlockSpec((N_HEADS, None, 128, tm), lambda i: (0, i // per, 0, i % per))
    outs = (jax.ShapeDtypeStruct((S, 2048), BF16), jax.ShapeDtypeStruct((S, 2048), BF16),
            jax.ShapeDtypeStruct((S, 1024), BF16), jax.ShapeDtypeStruct((N_HEADS, S // t, 128, t), BF16),
            jax.ShapeDtypeStruct((S, Q_LORA), BF16), jax.ShapeDtypeStruct((S, KV_LORA), BF16))
    return _pcall(kern, name=name, out_shape=outs, grid=(S // tm,),
                  in_specs=[row(512, 0), row(256, 2), row(128, 6), row(128, 0), row(128, 0),
                            full((1, Q_LORA)), full((1, KV_LORA)), full((2048, Q_LORA)), full((2048, KV_LORA))],
                  out_specs=(row(2048, 0), row(2048, 0), row(1024, 0), vt_spec, row(512, 0), row(256, 0)),
                  dims=("parallel",), vmem_mb=48)(
                      proj, proj, proj, cos_t, sin_t, qg.reshape(1, -1), kvg.reshape(1, -1), wuq_t, wukv_t)


def _mla_qkv_bwd(dqb, dkvb, dkr_heads, proj, cos_t, sin_t, qg, kvg, wuq_t, wukv_t, *, name):
    S = proj.shape[0]
    tm = min(256, S)

    def kern(dqb_ref, dkvb_ref, dkrh_ref, ql_ref, kvl_ref, cos_ref, sin_ref, qg_ref, kvg_ref, wuq_ref, wukv_ref,
             dml_ref, dqg_ref, dkvg_ref):
        @pl.when(pl.program_id(0) == 0)
        def _():
            dqg_ref[...] = jnp.zeros_like(dqg_ref)
            dkvg_ref[...] = jnp.zeros_like(dkvg_ref)

        cosv = cos_ref[...]
        sinv = sin_ref[...]

        def unrope(t):
            return t * cosv - _rot_sum(t) * sinv

        dkr = dkrh_ref[:, 0:128]
        for h in range(1, N_HEADS):
            dkr = dkr + dkrh_ref[:, 128 * h:128 * h + 128]

        def rms_bwd(x, g, dy):
            n = x.shape[-1]
            rs = lax.rsqrt(jnp.mean(x * x, axis=-1, keepdims=True) + RMS_EPS)
            dyg = dy * g
            dx = rs * dyg - x * (rs * rs * rs) * (jnp.sum(dyg * x, axis=-1, keepdims=True) / n)
            return dx, jnp.sum(dy * (x * rs), axis=0, keepdims=True)

        dqn = jnp.dot(dqb_ref[...], wuq_ref[...], preferred_element_type=F32)
        dql, dqg = rms_bwd(ql_ref[...], qg_ref[...], dqn)
        dqg_ref[...] += dqg
        dkvn = jnp.dot(dkvb_ref[...], wukv_ref[...], preferred_element_type=F32)
        dkvl, dkvg = rms_bwd(kvl_ref[...], kvg_ref[...], dkvn)
        dkvg_ref[...] += dkvg
        dml_ref[:, 0:512] = dql.astype(BF16)
        dml_ref[:, 512:768] = dkvl.astype(BF16)
        dml_ref[:, 768:896] = unrope(dkr).astype(BF16)
        dml_ref[:, 896:1024] = jnp.zeros((tm, 128), BF16)

    def row(w, blk):
        return pl.BlockSpec((tm, w), lambda i: (i, blk))

    def full(shape):
        return pl.BlockSpec(shape, lambda i: (0,) * len(shape))

    outs = (jax.ShapeDtypeStruct((S, W_MLA), BF16), jax.ShapeDtypeStruct((1, Q_LORA), F32),
            jax.ShapeDtypeStruct((1, KV_LORA), F32))
    return _pcall(kern, name=name, out_shape=outs, grid=(S // tm,),
                  in_specs=[row(2048, 0), row(2048, 0), row(1024, 0), row(512, 0), row(256, 2),
                            row(128, 0), row(128, 0), full((1, Q_LORA)), full((1, KV_LORA)),
                            full((2048, Q_LORA)), full((2048, KV_LORA))],
                  out_specs=(row(W_MLA, 0), full((1, Q_LORA)), full((1, KV_LORA))),
                  dims=("arbitrary",), vmem_mb=56)(
                      dqb, dkvb, dkr_heads, proj, proj, cos_t, sin_t, qg.reshape(1, -1), kvg.reshape(1, -1),
                      wuq_t, wukv_t)


def _flash_fwd(qc, kc, vt, *, name, comm=None):
    S = qc.shape[0]
    t = min(TQ, S)
    n = S // t

    def kern(q_ref, k_ref, vt_ref, o_ref, lse_ref, m_s, l_s, acc_s):
        qi = pl.program_id(1)
        m_s[...] = jnp.full_like(m_s, -jnp.inf)
        l_s[...] = jnp.zeros_like(l_s)
        acc_s[...] = jnp.zeros_like(acc_s)

        half = t // 2

        def scores(kb, q_lo=0, q_n=t, k_n=t):
            k0 = pl.multiple_of(kb * t, t)
            return lax.dot_general(k_ref[pl.ds(k0, k_n), :], q_ref[q_lo:q_lo + q_n, :], NT,
                                   preferred_element_type=F32)

        def update(kb, st, q_lo=0, diagonal=False):
            k_n, q_n = st.shape
            if diagonal:
                krow = lax.broadcasted_iota(jnp.int32, (k_n, q_n), 0)
                qcol = lax.broadcasted_iota(jnp.int32, (k_n, q_n), 1) + q_lo
                st = jnp.where(krow <= qcol, st, -jnp.inf)
            lanes = slice(q_lo, q_lo + q_n)
            m_prev = m_s[:, lanes]
            m_new = jnp.maximum(m_prev, jnp.max(st, axis=0, keepdims=True))
            a = jnp.exp2((m_prev - m_new) * SCALE_LOG2E)
            pt = jnp.exp2((st - m_new) * SCALE_LOG2E)
            l_s[:, lanes] = a * l_s[:, lanes] + jnp.sum(pt, axis=0, keepdims=True)
            acc_s[:, lanes] = a * acc_s[:, lanes] + jnp.dot(vt_ref[kb, :, 0:k_n], pt.astype(BF16),
                                                            preferred_element_type=F32)
            m_s[:, lanes] = m_new

        def group(kb, count, last_diagonal):
            whole = count - 1 if last_diagonal else count
            sts = [scores(kb + g) for g in range(whole)]
            if last_diagonal:
                kd = kb + count - 1
                s_lo, s_hi = scores(kd, 0, half, half), scores(kd, half, half, t)
            for g in range(whole):
                update(kb + g, sts[g])
            if last_diagonal:
                update(kd, s_lo, 0, True)
                update(kd, s_hi, half, True)

        def body(i, carry):
            group(FWD_GROUP * i, FWD_GROUP, False)
            return carry

        full = qi // FWD_GROUP
        lax.fori_loop(0, full, body, 0)
        for rem in range(FWD_GROUP):
            @pl.when(qi - FWD_GROUP * full == rem)
            def _():
                group(qi - rem, rem + 1, True)
        o_ref[...] = jnp.transpose(acc_s[...] / l_s[...])
        lse_ref[pl.ds(qi, 1), :] = m_s[...] * SCALE_LOG2E + jnp.log2(l_s[...])

    q_spec = pl.BlockSpec((t, 256), lambda h, qi: (qi, h))
    k_spec = pl.BlockSpec((S, 256), lambda h, qi: (0, h))
    vt_spec = pl.BlockSpec((None, n, 128, t), lambda h, qi: (h, 0, 0, 0))
    o_spec = pl.BlockSpec((t, 128), lambda h, qi: (qi, h))
    lse_spec = pl.BlockSpec((None, n, t), lambda h, qi: (h, 0, 0))
    return _pcall(kern, name=name,
                  out_shape=(jax.ShapeDtypeStruct((S, D_MLA), F32), jax.ShapeDtypeStruct((N_HEADS, n, t), F32)),
                  grid=(N_HEADS, n), in_specs=[q_spec, k_spec, vt_spec], out_specs=(o_spec, lse_spec),
                  scratch=[pltpu.VMEM((1, t), F32), pltpu.VMEM((1, t), F32), pltpu.VMEM((128, t), F32)],
                  dims=("parallel", "arbitrary"), vmem_mb=48, comm=comm)(qc, kc, vt)


def _attn_delta(o, do, *, name):
    S = o.shape[0]
    t = min(TQ, S)
    n = S // t

    def kern(o_ref, do_ref, dl_ref):
        i = pl.program_id(0)
        prod = o_ref[...] * do_ref[...]
        lane = lax.broadcasted_iota(jnp.int32, (t, LANE), 1)
        dmat = jnp.zeros((t, LANE), F32)
        for h in range(N_HEADS):
            dmat = jnp.where(lane == h, jnp.sum(prod[:, 128 * h:128 * h + 128], axis=1, keepdims=True), dmat)
        dmat_t = jnp.transpose(dmat)
        for h in range(N_HEADS):
            dl_ref[h, pl.ds(i, 1), :] = dmat_t[h:h + 1, :]

    row = pl.BlockSpec((t, D_MLA), lambda i: (i, 0))
    return _pcall(kern, name=name, out_shape=jax.ShapeDtypeStruct((N_HEADS, n, t), F32), grid=(n,),
                  in_specs=[row, row], out_specs=pl.BlockSpec((N_HEADS, n, t), lambda i: (0, 0, 0)),
                  dims=("arbitrary",), vmem_mb=48)(o, do)


def _flash_bwd(qc, kc, v, do, lse2, delta, cos_t, sin_t, *, name, comm=None):
    S = qc.shape[0]
    t = min(TQ, S)
    n = S // t

    def kern(q_ref, k_ref, v_ref, do_ref, lse_ref, dl_ref, cos_ref, sin_ref, dqb_ref, dkvb_ref, dkr_ref,
             dq_ref, dk_ref, dv_ref):
        ki = pl.program_id(1)

        @pl.when(ki == 0)
        def _():
            dq_ref[...] = jnp.zeros_like(dq_ref)

        dk_ref[...] = jnp.zeros_like(dk_ref)
        dv_ref[...] = jnp.zeros_like(dv_ref)

        half = t // 2

        def step(qb, q_lo=0, q_n=t, k_n=t, diagonal=False):
            q0 = pl.multiple_of(qb * t + q_lo, half)
            lanes = slice(q_lo, q_lo + q_n)
            kt = k_ref[0:k_n, :]
            qblk = q_ref[pl.ds(q0, q_n), :]
            dob = do_ref[pl.ds(q0, q_n), :].astype(BF16)
            st = lax.dot_general(kt, qblk, NT, preferred_element_type=F32)
            pt = jnp.exp2(st * SCALE_LOG2E - lse_ref[pl.ds(qb, 1), lanes])
            if diagonal:
                krow = lax.broadcasted_iota(jnp.int32, (k_n, q_n), 0)
                qcol = lax.broadcasted_iota(jnp.int32, (k_n, q_n), 1) + q_lo
                pt = jnp.where(krow <= qcol, pt, 0.0)
            dv_ref[0:k_n, :] += jnp.dot(pt.astype(BF16), dob, preferred_element_type=F32)
            dpt = lax.dot_general(v_ref[0:k_n, :], dob, NT, preferred_element_type=F32)
            dst = (pt * (dpt - dl_ref[pl.ds(qb, 1), lanes]) * SCALE).astype(BF16)
            dk_ref[0:k_n, :] += jnp.dot(dst, qblk, preferred_element_type=F32)
            dq_ref[pl.ds(q0, q_n), :] += lax.dot_general(dst, kt, TN, preferred_element_type=F32)

        step(ki, 0, half, half, True)
        step(ki, half, half, t, True)
        rest = n - 1 - ki

        def body(i, carry):
            step(ki + 1 + 2 * i)
            step(ki + 2 + 2 * i)
            return carry

        lax.fori_loop(0, rest // 2, body, 0)

        @pl.when(rest % 2 == 1)
        def _():
            step(n - 1)

        dkvb_ref[:, 0:128] = dk_ref[:, 0:128].astype(BF16)
        dkvb_ref[:, 128:256] = dv_ref[...].astype(BF16)
        dkr_ref[...] = dk_ref[:, 128:256]

        @pl.when(ki == n - 1)
        def _():
            dqb_ref[:, 0:128] = dq_ref[:, 0:128].astype(BF16)
            dqr = dq_ref[:, 128:256]
            dqb_ref[:, 128:256] = (dqr * cos_ref[...] - _rot_sum(dqr) * sin_ref[...]).astype(BF16)

    def whole(w):
        return pl.BlockSpec((S, w), lambda h, ki: (0, h))

    def krow(w):
        return pl.BlockSpec((t, w), lambda h, ki: (ki, h))

    stat = pl.BlockSpec((None, n, t), lambda h, ki: (h, 0, 0))
    table = pl.BlockSpec((S, 128), lambda h, ki: (0, 0))
    return _pcall(kern, name=name,
                  out_shape=(jax.ShapeDtypeStruct((S, 2048), BF16), jax.ShapeDtypeStruct((S, 2048), BF16),
                             jax.ShapeDtypeStruct((S, D_MLA), F32)),
                  grid=(N_HEADS, n),
                  in_specs=[whole(256), krow(256), krow(128), whole(128), stat, stat, table, table],
                  out_specs=(whole(256), krow(256), krow(128)),
                  scratch=[pltpu.VMEM((S, 256), F32), pltpu.VMEM((t, 256), F32), pltpu.VMEM((t, 128), F32)],
                  dims=("parallel", "arbitrary"), vmem_mb=56, comm=comm)(qc, kc, v, do, lse2, delta, cos_t, sin_t)


def _mixer_specs(S, tm):
    hb = tm // HALO
    last_hb = S // HALO - 1

    def main(w, blk):
        return pl.BlockSpec((tm, w), lambda i: (i, blk))

    def prev(w, blk):
        return pl.BlockSpec((HALO, w), lambda i: (jnp.maximum(i * hb - 1, 0), blk))

    def nxt(w, blk):
        return pl.BlockSpec((HALO, w), lambda i: (jnp.minimum((i + 1) * hb, last_hb), blk))

    def full(shape):
        return pl.BlockSpec(shape, lambda i: (0,) * len(shape))

    return main, prev, nxt, full


def _fill_halo(i, xp, xu, hp_ref, hch_ref, hcc_ref, pin_ref, ch_ref, cc_ref, tm):
    first = i == 0
    xp[0:HALO, :] = jnp.where(first, 0.0, hp_ref[...])
    xp[HALO:HALO + tm, :] = pin_ref[...]
    xu[0:HALO, :] = jnp.where(first, 0.0, hch_ref[...] * hcc_ref[...])
    xu[HALO:HALO + tm, :] = cc_ref[...] * ch_ref[...]


def _trailing_sums(xp, la, lb, tm):
    R = HALO + tm
    la[8:R, :] = xp[8:R, :] + xp[7:R - 1, :]
    lb[16:R, 128:512] = la[16:R, 128:512] + la[14:R - 2, 128:512]
    la[24:R, 256:512] = lb[24:R, 256:512] + lb[20:R - 4, 256:512]
    lb[32:R, 384:512] = la[32:R, 384:512] + la[24:R - 8, 384:512]


def _leading_sums(ee, la, lb, tm):
    R = tm + HALO
    la[0:R - 8, :] = ee[0:R - 8, :] + ee[1:R - 7, :]
    lb[0:R - 16, 128:512] = la[0:R - 16, 128:512] + la[2:R - 14, 128:512]
    la[0:R - 24, 256:512] = lb[0:R - 24, 256:512] + lb[4:R - 20, 256:512]
    lb[0:R - 32, 384:512] = la[0:R - 32, 384:512] + la[8:R - 24, 384:512]


def _window_sums(la, lb, g):
    return (la, lb, la, lb)[g]


def _pooled(xp, la, lb, g, t1, tm):
    lanes = slice(128 * g, 128 * g + 128)
    acc = _window_sums(la, lb, g)[HALO:HALO + tm, lanes]
    return acc / jnp.minimum(t1, float(POOL_WINDOWS[g])) - xp[HALO:HALO + tm, lanes]


def _conv_fwd(xu, cw_ref, tm):
    return (cw_ref[0:1, :] * xu[HALO - 2:HALO - 2 + tm, :] + cw_ref[1:2, :] * xu[HALO - 1:HALO - 1 + tm, :]
            + cw_ref[2:3, :] * xu[HALO:HALO + tm, :])


def _mixer_fwd(proj, o, wpool, ps, convw, *, name):
    S = proj.shape[0]
    tm = min(256, S)
    main, prev, _, full = _mixer_specs(S, tm)

    def kern(gm_ref, pin_ref, gp_ref, ch_ref, cb_ref, cc_ref, gc_ref, hp_ref, hch_ref, hcc_ref,
             o_ref, wp_ref, ps_ref, cw_ref, mix_ref, xp, xu, la, lb):
        i = pl.program_id(0)
        _fill_halo(i, xp, xu, hp_ref, hch_ref, hcc_ref, pin_ref, ch_ref, cc_ref, tm)
        _trailing_sums(xp, la, lb, tm)
        t1 = (i * tm + lax.broadcasted_iota(jnp.int32, (tm, 1), 0) + 1).astype(F32)
        for g in range(4):
            lanes = slice(128 * g, 128 * g + 128)
            pooled = _pooled(xp, la, lb, g, t1, tm)
            z = jnp.dot(pooled.astype(BF16), wp_ref[g].astype(BF16), preferred_element_type=F32)
            gp = gp_ref[:, lanes]
            y = z * ps_ref[:, lanes] * (gp * _sigmoid(gp))
            mix_ref[:, 1024 + 128 * g:1024 + 128 * g + 128] = y.astype(BF16)
        gc = gc_ref[...]
        mix_ref[:, 1536:2048] = (cb_ref[...] * _conv_fwd(xu, cw_ref, tm) * (gc * _sigmoid(gc))).astype(BF16)
        gm = gm_ref[...]
        mix_ref[:, 0:1024] = (o_ref[...] * (gm * _sigmoid(gm))).astype(BF16)

    return _pcall(kern, name=name, out_shape=jax.ShapeDtypeStruct((S, 2048), BF16), grid=(S // tm,),
                  in_specs=[main(1024, 1), main(512, 4), main(512, 5), main(512, 6), main(512, 7), main(512, 8),
                            main(512, 9), prev(512, 4), prev(512, 6), prev(512, 8),
                            main(1024, 0), full((4, 128, 128)), full((1, 512)), full((3, 512))],
                  out_specs=main(2048, 0),
                  scratch=[pltpu.VMEM((tm + HALO, 512), F32) for _ in range(4)],
                  dims=("parallel",), vmem_mb=48)(
                      proj, proj, proj, proj, proj, proj, proj, proj, proj, proj, o, wpool, ps.reshape(1, 512), convw)


def _mixer_bwd(dmix, proj, o, wpool, ps, convw, *, name):
    S = proj.shape[0]
    tm = min(256, S)
    n = S // tm
    main, prev, nxt, full = _mixer_specs(S, tm)

    def kern(dm_ref, dmn_ref, gm_ref, pin_ref, gp_ref, ch_ref, cb_ref, cc_ref, gc_ref,
             hp_ref, hch_ref, hcc_ref, gpn_ref, cbn_ref, gcn_ref, o_ref, wp_ref, ps_ref, cw_ref,
             d_ref, do_ref, dwp_ref, dps_ref, dcw_ref, xp, xu, ee, ed, la, lb, dp_s):
        i = pl.program_id(0)
        last = i == n - 1

        @pl.when(i == 0)
        def _():
            dwp_ref[...] = jnp.zeros_like(dwp_ref)
            dps_ref[...] = jnp.zeros_like(dps_ref)
            dcw_ref[...] = jnp.zeros_like(dcw_ref)

        _fill_halo(i, xp, xu, hp_ref, hch_ref, hcc_ref, pin_ref, ch_ref, cc_ref, tm)
        _trailing_sums(xp, la, lb, tm)
        t1 = (i * tm + lax.broadcasted_iota(jnp.int32, (tm, 1), 0) + 1).astype(F32)
        t1n = ((i + 1) * tm + lax.broadcasted_iota(jnp.int32, (HALO, 1), 0) + 1).astype(F32)
        c_pin, c_gp, c_ch, c_cb, c_cc, c_gc = 1024, 1536, 2048, 2560, 3072, 3584

        for g in range(4):
            w = float(POOL_WINDOWS[g])
            lanes = slice(128 * g, 128 * g + 128)
            pooled = _pooled(xp, la, lb, g, t1, tm)
            pb = pooled.astype(BF16)
            wp = wp_ref[g].astype(BF16)
            z = jnp.dot(pb, wp, preferred_element_type=F32)
            psl = ps_ref[:, lanes]
            sg, dsg = _silu_and_grad(gp_ref[:, lanes])
            dmp = dm_ref[:, 1024 + 128 * g:1024 + 128 * g + 128]
            dyp = dmp * sg
            d_ref[:, c_gp + 128 * g:c_gp + 128 * g + 128] = (dmp * (z * psl) * dsg).astype(BF16)
            dps_ref[:, lanes] += jnp.sum(dyp * z, axis=0, keepdims=True)
            dz = (dyp * psl).astype(BF16)
            dwp_ref[g] += lax.dot_general(pb, dz, TN, preferred_element_type=F32)
            dpl = lax.dot_general(dz, wp, NT, preferred_element_type=F32)
            dp_s[:, lanes] = dpl
            ee[0:tm, lanes] = dpl / jnp.minimum(t1, w)
            gpn = gpn_ref[:, lanes]
            dzn = (dmn_ref[:, lanes] * (gpn * _sigmoid(gpn)) * psl).astype(BF16)
            dpn = lax.dot_general(dzn, wp, NT, preferred_element_type=F32)
            ee[tm:tm + HALO, lanes] = jnp.where(last, 0.0, dpn / jnp.minimum(t1n, w))
        _leading_sums(ee, la, lb, tm)
        for g in range(4):
            lanes = slice(128 * g, 128 * g + 128)
            acc = _window_sums(la, lb, g)[0:tm, lanes]
            d_ref[:, c_pin + 128 * g:c_pin + 128 * g + 128] = (acc - dp_s[:, lanes]).astype(BF16)

        yc = _conv_fwd(xu, cw_ref, tm)
        sgc, dsgc = _silu_and_grad(gc_ref[...])
        cb = cb_ref[...]
        dmc = dm_ref[:, 1536:2048]
        d_ref[:, c_gc:c_gc + 512] = (dmc * cb * yc * dsgc).astype(BF16)
        d_ref[:, c_cb:c_cb + 512] = (dmc * yc * sgc).astype(BF16)
        dyc = dmc * cb * sgc
        ed[0:tm, :] = dyc
        gcn = gcn_ref[...]
        ed[tm:tm + HALO, :] = jnp.where(last, 0.0, dmn_ref[:, 512:1024] * cbn_ref[...] * (gcn * _sigmoid(gcn)))
        dcw_ref[0:1, :] += jnp.sum(dyc * xu[HALO - 2:HALO - 2 + tm, :], axis=0, keepdims=True)
        dcw_ref[1:2, :] += jnp.sum(dyc * xu[HALO - 1:HALO - 1 + tm, :], axis=0, keepdims=True)
        dcw_ref[2:3, :] += jnp.sum(dyc * xu[HALO:HALO + tm, :], axis=0, keepdims=True)
        du = cw_ref[2:3, :] * dyc + cw_ref[1:2, :] * ed[1:1 + tm, :] + cw_ref[0:1, :] * ed[2:2 + tm, :]
        d_ref[:, c_cc:c_cc + 512] = (du * ch_ref[...]).astype(BF16)
        d_ref[:, c_ch:c_ch + 512] = (du * cc_ref[...]).astype(BF16)

        sgm, dsgm = _silu_and_grad(gm_ref[...])
        dmm = dm_ref[:, 0:1024]
        do_ref[...] = dmm * sgm
        d_ref[:, 0:1024] = (dmm * o_ref[...] * dsgm).astype(BF16)

    outs = (jax.ShapeDtypeStruct((S, W_MIX), BF16), jax.ShapeDtypeStruct((S, 1024), F32),
            jax.ShapeDtypeStruct((4, 128, 128), F32), jax.ShapeDtypeStruct((1, 512), F32),
            jax.ShapeDtypeStruct((3, 512), F32))
    scr = [pltpu.VMEM((tm + HALO, 512), F32) for _ in range(6)] + [pltpu.VMEM((tm, 512), F32)]
    return _pcall(kern, name=name, out_shape=outs, grid=(n,),
                  in_specs=[main(2048, 0), nxt(1024, 1),
                            main(1024, 1), main(512, 4), main(512, 5), main(512, 6), main(512, 7), main(512, 8),
                            main(512, 9), prev(512, 4), prev(512, 6), prev(512, 8),
                            nxt(512, 5), nxt(512, 7), nxt(512, 9),
                            main(1024, 0), full((4, 128, 128)), full((1, 512)), full((3, 512))],
                  out_specs=(main(W_MIX, 0), main(1024, 0), full((4, 128, 128)), full((1, 512)), full((3, 512))),
                  scratch=scr, dims=("arbitrary",), vmem_mb=56)(
                      dmix, dmix, proj, proj, proj, proj, proj, proj, proj, proj, proj, proj, proj, proj, proj,
                      o, wpool, ps.reshape(1, 512), convw)


def _outproj_residual(mix, wout, h, bout, *, name):
    S, Dm = h.shape
    tm = min(256, S)

    def kern(mix_ref, w_ref, h_ref, bo_ref, r_ref):
        out = jnp.dot(mix_ref[...], w_ref[...], preferred_element_type=F32) + bo_ref[...]
        r_ref[...] = ALPHA * h_ref[...] + out

    row = pl.BlockSpec((tm, Dm), lambda i: (i, 0))
    vec = pl.BlockSpec((1, Dm), lambda i: (0, 0))
    wsp = pl.BlockSpec((Dm, Dm), lambda i: (0, 0))
    return _pcall(kern, name=name, out_shape=jax.ShapeDtypeStruct((S, Dm), F32), grid=(S // tm,),
                  in_specs=[row, wsp, row, vec], out_specs=row, dims=("parallel",), vmem_mb=56)(
                      mix, wout, h, bout.reshape(1, Dm))


def _outproj_ln(mix, wout, h, bout, g, b, *, name):
    S, Dm = h.shape
    tm = min(256, S)

    def kern(mix_ref, w_ref, h_ref, bo_ref, g_ref, b_ref, y_ref, yb_ref, r_ref):
        out = jnp.dot(mix_ref[...], w_ref[...], preferred_element_type=F32) + bo_ref[...]
        r = ALPHA * h_ref[...] + out
        r_ref[...] = r
        mu = jnp.mean(r, axis=-1, keepdims=True)
        xc = r - mu
        var = jnp.mean(xc * xc, axis=-1, keepdims=True)
        y = xc * lax.rsqrt(var + LN_EPS) * g_ref[...] + b_ref[...]
        y_ref[...] = y
        yb_ref[...] = y.astype(BF16)

    row = pl.BlockSpec((tm, Dm), lambda i: (i, 0))
    vec = pl.BlockSpec((1, Dm), lambda i: (0, 0))
    wsp = pl.BlockSpec((Dm, Dm), lambda i: (0, 0))
    sds = jax.ShapeDtypeStruct((S, Dm), F32)
    return _pcall(kern, name=name, out_shape=(sds, jax.ShapeDtypeStruct((S, Dm), BF16), sds), grid=(S // tm,),
                  in_specs=[row, wsp, row, vec, vec, vec], out_specs=(row, row, row), dims=("parallel",),
                  vmem_mb=56)(
                      mix, wout, h, bout.reshape(1, Dm), g.reshape(1, Dm), b.reshape(1, Dm))


def _adamw_math(w, g, m, v):
    m = ADAM_B1 * m + (1.0 - ADAM_B1) * g
    v = ADAM_B2 * v + (1.0 - ADAM_B2) * (g * g)
    m_hat = m / (1.0 - ADAM_B1 ** ADAM_STEP)
    v_hat = v / (1.0 - ADAM_B2 ** ADAM_STEP)
    delta = -ADAM_LR * (m_hat / (jnp.sqrt(v_hat) + ADAM_EPS) + ADAM_WD * w)
    return delta, m, v


def _row_tile(R, C):
    best = None
    for cand in range(8, R, 8):
        if R % cand == 0 and cand * C <= 256 * 1024:
            best = cand
    return best if best is not None else R


def _adamw(w, g, m, v, *, name):
    shape = w.shape
    C = shape[-1]
    R = 1
    for s in shape[:-1]:
        R *= s
    tr = _row_tile(R, C)

    def kern(w_ref, g_ref, m_ref, v_ref, d_ref, mo_ref, vo_ref):
        d, mn, vn = _adamw_math(w_ref[...], g_ref[...], m_ref[...], v_ref[...])
        d_ref[...] = d
        mo_ref[...] = mn
        vo_ref[...] = vn

    blk = pl.BlockSpec((tr, C), lambda i: (i, 0))
    sds = jax.ShapeDtypeStruct((R, C), F32)
    outs = _pcall(kern, name=name, out_shape=(sds, sds, sds), grid=(R // tr,), in_specs=[blk] * 4,
                  out_specs=(blk, blk, blk), dims=("parallel",), vmem_mb=48)(
                      w.reshape(R, C), g.reshape(R, C), m.reshape(R, C), v.reshape(R, C))
    return tuple(t.reshape(shape) for t in outs)


def _adamw_halves(w, m, v, halves, c_idx, *, name, comm=None):
    _, R, C = w.shape
    ch = C // 2
    tr = _row_tile(R, ch)
    nb = R // tr

    def kern(c_ref, w_ref, a0_ref, b0_ref, a1_ref, b1_ref, m_ref, v_ref, g_ref, d_ref, mo_ref, vo_ref):
        layer = pl.program_id(0) // nb
        mine = pl.program_id(1) == c_ref[0]
        g = jnp.where(layer == 0, jnp.where(mine, a0_ref[...], b0_ref[...]),
                      jnp.where(mine, a1_ref[...], b1_ref[...]))
        g_ref[...] = g
        d, mn, vn = _adamw_math(w_ref[...], g, m_ref[...], v_ref[...])
        d_ref[...] = d
        mo_ref[...] = mn
        vo_ref[...] = vn

    full = pl.BlockSpec((tr, ch), lambda i, hc: (i, hc))
    half = pl.BlockSpec((tr, ch), lambda i, hc: (i % nb, 0))
    sds = jax.ShapeDtypeStruct((2 * R, C), F32)
    (a0, b0), (a1, b1) = halves
    res = _pcall(kern, name=name, out_shape=(sds,) * 4, grid=(2 * nb, 2),
                 in_specs=[pl.BlockSpec(memory_space=pltpu.SMEM), full, half, half, half, half, full, full],
                 out_specs=(full,) * 4, dims=("parallel", "parallel"), vmem_mb=48, comm=comm)(
                     c_idx, w.reshape(2 * R, C), a0, b0, a1, b1, m.reshape(2 * R, C), v.reshape(2 * R, C))
    outs, landed = res if comm is not None else (res, None)
    outs = tuple(t.reshape(2, R, C) for t in outs)
    return outs if comm is None else (outs, landed)


def _packed_pieces(shape):
    if len(shape) == 4:
        return [((l * shape[1] + g) * 128, 128, (l, g)) for l in range(shape[0]) for g in range(shape[1])]
    per_row = shape[1] // LANE
    return [(a * per_row + j, 1, (slice(a, a + 1), slice(LANE * j, LANE * (j + 1))))
            for a in range(shape[0]) for j in range(per_row)]


def _small_sum_adamw(gathered, own, weights, *, name):
    R = gathered.shape[1]
    nw = len(weights)
    shapes = [w.shape for w, _, _ in weights]
    first_row, r0 = [], 0
    for shp in shapes:
        first_row.append(r0)
        n = 1
        for s in shp:
            n *= s
        r0 += n // LANE

    def kern(ga_ref, own_ref, *refs):
        ins, gsum_ref, outs = refs[:3 * nw], refs[3 * nw], refs[3 * nw + 1:]
        me = 4 * lax.axis_index("x") + 2 * lax.axis_index("y") + lax.axis_index("c")

        def block(k):
            other = ga_ref[jnp.where(me == k, (k + 1) % N_DEV, k)]
            return jnp.where(me == k, own_ref[...], other)

        g = block(0)
        for k in range(1, N_DEV):
            g = g + block(k)
        gsum_ref[...] = g
        for p, shp in enumerate(shapes):
            w_ref, m_ref, v_ref = ins[3 * p:3 * p + 3]
            g_out, d_out, m_out, v_out = outs[4 * p:4 * p + 4]
            for row, rows, idx in _packed_pieces(shp):
                gp = gsum_ref[first_row[p] + row:first_row[p] + row + rows, :]
                d, mn, vn = _adamw_math(w_ref[idx], gp, m_ref[idx], v_ref[idx])
                g_out[idx] = gp
                d_out[idx] = d
                m_out[idx] = mn
                v_out[idx] = vn

    out_shape = [jax.ShapeDtypeStruct((R, LANE), F32)]
    for shp in shapes:
        out_shape += [jax.ShapeDtypeStruct(shp, F32)] * 4
    flat = [a for wmv in weights for a in wmv]
    res = _pcall(kern, name=name, out_shape=tuple(out_shape), vmem_mb=48)(gathered, own, *flat)
    return res[0], [tuple(res[1 + 4 * p:5 + 4 * p]) for p in range(nw)]


def _pair_sum(g, theirs, c_idx, *, name):
    R, C = g.shape
    ch = C // 2
    tr = _row_tile(R, ch)

    def kern(c_ref, a_ref, b_ref, o_ref):
        o_ref[...] = (a_ref[...] + b_ref[...]).astype(BF16)

    gs = pltpu.PrefetchScalarGridSpec(
        num_scalar_prefetch=1, grid=(R // tr,),
        in_specs=[pl.BlockSpec((tr, ch), lambda i, c: (i, c[0])), pl.BlockSpec((tr, ch), lambda i, c: (i, 0))],
        out_specs=pl.BlockSpec((tr, ch), lambda i, c: (i, 0)))
    return pl.pallas_call(kern, name=name, out_shape=jax.ShapeDtypeStruct((R, ch), BF16), grid_spec=gs,
                          compiler_params=pltpu.CompilerParams(dimension_semantics=("parallel",),
                                                               vmem_limit_bytes=48 << 20))(c_idx, g, theirs)


WeightRows = collections.namedtuple("WeightRows", "full_rows own_rows cols pieces zero_rows")


def _w_in_piece_a(j):
    return jnp.where(j == 0, 0, 1232 * j + GAP)


def _w_in_piece_b(j):
    return jnp.where(j == 0, GAP_AT + GAP, 1232 * j + GAP_AT + GAP)


W_IN = WeightRows(NP, 1232, D_MODEL, ((0, GAP_AT, _w_in_piece_a), (GAP_AT, 1232 - GAP_AT, _w_in_piece_b)),
                  ((GAP_AT, GAP),))
W_OUT = WeightRows(2048, 512, D_MODEL, ((0, 512, lambda j: 512 * j),), ())
W_UQ = WeightRows(2048, 384, Q_LORA, ((0, 192, lambda j: 512 * j), (192, 192, lambda j: 512 * j + 256)),
                  tuple((256 * h + 192, 64) for h in range(N_HEADS)))
W_UKV = WeightRows(2048, 512, KV_LORA, ((0, 512, lambda j: 512 * j),), ())
W_CONV = WeightRows(64, 16, 256, ((0, 16, lambda j: 16 * j),), ())
SHARDED = (W_IN, W_OUT, W_UQ, W_UKV)
SHARDED_NAMES = ("w_in", "w_out", "w_uq", "w_ukv")
WEIGHT_ROWS = dict(zip(SHARDED_NAMES, SHARDED))


def _mesh_pos():
    x, y, c = lax.axis_index("x"), lax.axis_index("y"), lax.axis_index("c")
    return x, y, c


def _other_chips(x, y):
    return [(1 - x, y), (x, 1 - y), (1 - x, 1 - y)]


def _rows(start, n):
    return pl.ds(pl.multiple_of(start, 16), n)


def _half_cols(spec, c):
    ch = spec.cols // 2
    return pl.ds(pl.multiple_of(c * ch, LANE), ch)


def _allgather_script(specs, shards, zeros, layers):
    na = len(specs)
    zlist = [a for a in range(na) if zeros[a] is not None]
    n_layers = [shards[a].shape[0] if layers[a] is None else 1 for a in range(na)]
    plan_first, plan_own, plan_zero = [], [], []
    for a, spec in enumerate(specs):
        for p in range(len(spec.pieces)):
            plan_own.append((a, p))
            for k in range(3):
                plan_first.append((a, p, k))
        for z in range(len(spec.zero_rows)):
            for l in range(n_layers[a]):
                plan_zero.append((a, z, l))
    nf = len(plan_first)
    n_sems = 2 * nf + len(plan_own) + len(plan_zero)

    def copies(ins_all, outs, send_sems, recv_sems):
        ins = [ins_all[a] if layers[a] is None else ins_all[a].at[pl.ds(layers[a], 1)] for a in range(na)]
        zrefs = dict(zip(zlist, ins_all[na:]))
        x, y, c = _mesh_pos()
        j = 2 * x + y
        chips = _other_chips(x, y)
        sibling = (x, y, 1 - c)

        def remote(src, dst, sem, to):
            return pltpu.make_async_remote_copy(src_ref=src, dst_ref=dst, send_sem=send_sems.at[sem],
                                                recv_sem=recv_sems.at[sem], device_id=to, device_id_type=MESH)

        def block(a, p, chip, cols):
            _, n, dst = specs[a].pieces[p]
            return outs[a].at[:, _rows(dst(chip), n), cols]

        def first(i):
            a, p, k = plan_first[i]
            src0, n, _ = specs[a].pieces[p]
            cols = _half_cols(specs[a], c)
            return remote(ins[a].at[:, pl.ds(src0, n), cols], block(a, p, j, cols), i, (*chips[k], c))

        def landed(i, half):
            a, p, k = plan_first[i]
            return block(a, p, 2 * chips[k][0] + chips[k][1], _half_cols(specs[a], half))

        def arrival(i, half, sem):
            return remote(landed(i, half), landed(i, half), sem, sibling)

        def passed(i):
            return remote(landed(i, c), landed(i, c), nf + i, sibling)

        def own(i):
            a, p = plan_own[i]
            src0, n, _ = specs[a].pieces[p]
            return remote(ins[a].at[:, pl.ds(src0, n), :], block(a, p, j, slice(None)), 2 * nf + i, sibling)

        def zero(i):
            a, z, l = plan_zero[i]
            r0, n = specs[a].zero_rows[z]
            return remote(zrefs[a].at[pl.ds(0, n), :], outs[a].at[l, pl.ds(r0, n), :],
                          2 * nf + len(plan_own) + i, sibling)

        fixed = [own(i) for i in range(len(plan_own))] + [zero(i) for i in range(len(plan_zero))]
        return c, fixed, first, arrival, passed

    def start(ins, outs, send_sems, recv_sems):
        _, fixed, first, _, _ = copies(ins, outs, send_sems, recv_sems)
        for cp in fixed:
            cp.start()
        for i in range(nf):
            first(i).start()

    def finish(ins, outs, send_sems, recv_sems):
        c, fixed, first, arrival, passed = copies(ins, outs, send_sems, recv_sems)
        for i in range(nf):
            arrival(i, c, i).wait_recv()
            passed(i).start()
        for i in range(nf):
            arrival(i, 1 - c, nf + i).wait_recv()
        for cp in fixed:
            cp.wait()
        for i in range(nf):
            first(i).wait_send()
            passed(i).wait_send()

    out_shape = tuple(jax.ShapeDtypeStruct((n_layers[a], spec.full_rows, spec.cols), BF16)
                      for a, spec in enumerate(specs))
    args = tuple(shards) + tuple(zeros[a] for a in zlist)
    return CommScript(args, out_shape, n_sems, start, finish)


def _start_all_wait_all(args, out_shape, n_sems, make_copies):
    def start(ins, outs, send_sems, recv_sems):
        for cp in make_copies(ins, outs, send_sems, recv_sems):
            cp.start()

    def finish(ins, outs, send_sems, recv_sems):
        for cp in make_copies(ins, outs, send_sems, recv_sems):
            cp.wait()

    return CommScript(tuple(args), tuple(out_shape), n_sems, start, finish)


def _exchange_script(specs, grads):
    na = len(grads)

    def make_copies(ins, outs, send_sems, recv_sems):
        x, y, c = _mesh_pos()
        return [pltpu.make_async_remote_copy(
            src_ref=ins[a].at[:, _half_cols(specs[a], 1 - c)], dst_ref=outs[a], send_sem=send_sems.at[a],
            recv_sem=recv_sems.at[a], device_id=(x, y, 1 - c), device_id_type=MESH) for a in range(na)]

    out_shape = [jax.ShapeDtypeStruct((s.full_rows, s.cols // 2), F32) for s in specs]
    return _start_all_wait_all(grads, out_shape, na, make_copies)


def _scatter_script(specs, parts):
    na = len(parts)
    plan = [(a, p, k) for a in range(na) for p in range(len(specs[a].pieces)) for k in range(3)]

    def make_copies(ins, outs, send_sems, recv_sems):
        x, y, c = _mesh_pos()
        chips = _other_chips(x, y)
        copies = []
        for i, (a, p, k) in enumerate(plan):
            src0, n, dst = specs[a].pieces[p]
            pk = 2 * chips[k][0] + chips[k][1]
            copies.append(pltpu.make_async_remote_copy(
                src_ref=ins[a].at[_rows(dst(pk), n), :], dst_ref=outs[a].at[k, pl.ds(src0, n), :],
                send_sem=send_sems.at[i], recv_sem=recv_sems.at[i], device_id=(*chips[k], c), device_id_type=MESH))
        return copies

    out_shape = [jax.ShapeDtypeStruct((3, s.own_rows, s.cols // 2), BF16) for s in specs]
    return _start_all_wait_all(parts, out_shape, len(plan), make_copies)


def _chip_sum(spec, part, recv, *, name):
    ch = spec.cols // 2
    npieces = len(spec.pieces)

    def kern(recv_ref, part_ref, o_ref, own_ref, sems):
        j = 2 * lax.axis_index("x") + lax.axis_index("y")
        copies = []
        for p, (src0, n, dst) in enumerate(spec.pieces):
            copies.append(pltpu.make_async_copy(part_ref.at[_rows(dst(j), n), :], own_ref.at[pl.ds(src0, n), :],
                                                sems.at[p]))
        for cp in copies:
            cp.start()
        for cp in copies:
            cp.wait()
        o_ref[...] = ((own_ref[...].astype(F32) + recv_ref[0].astype(F32)) + recv_ref[1].astype(F32)) \
            + recv_ref[2].astype(F32)

    vm = pl.BlockSpec(memory_space=pltpu.VMEM)
    return _pcall(kern, name=name, out_shape=jax.ShapeDtypeStruct((spec.own_rows, ch), F32),
                  in_specs=[vm, HBM_SPEC], out_specs=vm,
                  scratch=[pltpu.VMEM((spec.own_rows, ch), BF16), pltpu.SemaphoreType.DMA((npieces,))],
                  vmem_mb=48)(recv, part)


def _sibling_script(sums):
    na = len(sums)

    def make_copies(ins, outs, send_sems, recv_sems):
        x, y, c = _mesh_pos()
        return [pltpu.make_async_remote_copy(
            src_ref=ins[a], dst_ref=outs[a], send_sem=send_sems.at[a], recv_sem=recv_sems.at[a],
            device_id=(x, y, 1 - c), device_id_type=MESH) for a in range(na)]

    out_shape = [jax.ShapeDtypeStruct(t.shape, t.dtype) for t in sums]
    return _start_all_wait_all(sums, out_shape, na, make_copies)


class _SemWindow:
    def __init__(self, sems, offset):
        self._sems, self._offset = sems, offset

    @property
    def at(self):
        return self

    def __getitem__(self, i):
        return self._sems.at[i + self._offset]


def _merge_scripts(*scripts):
    a_off, o_off, s_off = [0], [0], [0]
    for s in scripts:
        a_off.append(a_off[-1] + len(s.args))
        o_off.append(o_off[-1] + len(s.out_shape))
        s_off.append(s_off[-1] + s.n_sems)

    def phase(which):
        def run(ins, outs, send_sems, recv_sems):
            for n, s in enumerate(scripts):
                getattr(s, which)(ins[a_off[n]:a_off[n + 1]], outs[o_off[n]:o_off[n + 1]],
                                  _SemWindow(send_sems, s_off[n]), _SemWindow(recv_sems, s_off[n]))
        return run

    return CommScript(sum((tuple(s.args) for s in scripts), ()), sum((tuple(s.out_shape) for s in scripts), ()),
                      s_off[-1], phase("start"), phase("finish"))


class _GradReducer:
    def __init__(self, layer, names, grads, c_idx):
        self.specs = tuple(WEIGHT_ROWS[nm] for nm in names)
        self.grads, self.c_idx = tuple(grads), c_idx
        self.names = [f"{nm}{layer}" for nm in names]

    def exchange(self):
        return _exchange_script(self.specs, self.grads)

    def scatter(self, theirs):
        self.parts = tuple(_pair_sum(g, th, self.c_idx, name=f"pair_sum_{nm}")
                           for g, th, nm in zip(self.grads, theirs, self.names))
        return _scatter_script(self.specs, self.parts)

    def sibling(self, recv):
        self.sums = tuple(_chip_sum(s, p, r, name=f"chip_sum_{nm}")
                          for s, p, r, nm in zip(self.specs, self.parts, recv, self.names))
        return _sibling_script(self.sums)

    def done(self, others):
        return list(zip(self.sums, others))


def _allgather_small_script(block):
    m_per, n = block.shape

    def copies(ins, outs, send_sems, recv_sems):
        (x_ref,), (out_ref,) = ins, outs
        x, y, c = _mesh_pos()
        me, sibling = (x, y, c), (x, y, 1 - c)
        chips = _other_chips(x, y)

        def rows(px, py, pc):
            return out_ref.at[4 * px + 2 * py + pc]

        def copy(k, blk, to, src=None):
            return pltpu.make_async_remote_copy(
                src_ref=rows(*blk) if src is None else src, dst_ref=rows(*blk), send_sem=send_sems.at[k],
                recv_sem=recv_sems.at[k], device_id=to, device_id_type=MESH)

        first = [copy(0, me, sibling, src=x_ref)]
        first += [copy(1 + k, me, (*chip, c), src=x_ref) for k, chip in enumerate(chips)]
        passed = [copy(4 + k, (*chip, c), sibling) for k, chip in enumerate(chips)]
        landed = [copy(1 + k, (*chip, c), me) for k, chip in enumerate(chips)]
        from_sibling = [copy(0, sibling, me)] + [copy(4 + k, (*chip, 1 - c), me) for k, chip in enumerate(chips)]
        return first, passed, landed, from_sibling

    def start(ins, outs, send_sems, recv_sems):
        first, _, _, _ = copies(ins, outs, send_sems, recv_sems)
        for cp in first:
            cp.start()

    def finish(ins, outs, send_sems, recv_sems):
        first, passed, landed, from_sibling = copies(ins, outs, send_sems, recv_sems)
        for k in range(3):
            landed[k].wait_recv()
            passed[k].start()
        for cp in from_sibling:
            cp.wait_recv()
        for cp in first + passed:
            cp.wait_send()

    return CommScript((block,), (jax.ShapeDtypeStruct((N_DEV, m_per, n), block.dtype),), 7, start, finish)


def _rope_tables(positions):
    half = ROPE // 2
    inv_freq = ROPE_THETA ** (-jnp.arange(half, dtype=F32) / half)
    ang = positions.astype(F32)[:, None] * inv_freq
    cos, sin = jnp.cos(ang), jnp.sin(ang)
    S = positions.shape[0]
    cos_t = jnp.concatenate([cos, cos, jnp.ones((S, 64), F32)], axis=1)
    sin_t = jnp.concatenate([-sin, sin, jnp.zeros((S, 64), F32)], axis=1)
    return cos_t, sin_t


def _decode_conv(bits):
    rows = bits.reshape(DEPTH, N_CHIPS, 16, 256)[:, :, :3, :]
    conv = lax.bitcast_convert_type(rows.reshape(DEPTH, N_CHIPS, 3, 128, 2), F32)
    return jnp.transpose(conv, (0, 2, 1, 3)).reshape(DEPTH, 3, 512)


def _local_step(x, positions, target, emb_g, emb_b, w_in_t0, rest0, weights1, q_g, kv_g, w_pool, pool_scale,
                b_out, ln_g, ln_b, c_idx=None):
    cos_t, sin_t = _rope_tables(positions)
    if isinstance(w_in_t0, CommScript):
        (h, hb), (landed,) = _ln_fwd(x, emb_g, emb_b, name="emb_ln", comm=w_in_t0)
        w_in_t0 = landed[0]
    else:
        h, hb = _ln_fwd(x, emb_g, emb_b, name="emb_ln")
    weights = [None, weights1]
    saved = []
    for l in range(DEPTH):
        if l == 0 and isinstance(rest0, CommScript):
            proj, landed = _matmul(hb, w_in_t0, "nt", name="in_proj0", tm=1024, tn=1024, tk=2048, vmem_mb=56,
                                   comm=rest0)
            weights[0] = (w_in_t0,) + tuple(a[0] for a in landed[:3])
            conv_w = _decode_conv(landed[3])
        else:
            if l == 0:
                weights[0] = (w_in_t0,) + tuple(rest0[:3])
                conv_w = rest0[3]
            proj = _matmul(hb, weights[l][0], "nt", name=f"in_proj{l}", tm=1024, tn=1024, tk=2048, vmem_mb=56)
        w_in_t, w_out, w_uq_t, w_ukv_t = weights[l]
        qc, kc, v, vt, qn, kvn = _mla_qkv(proj, cos_t, sin_t, q_g[l], kv_g[l], w_uq_t, w_ukv_t, name=f"mla_qkv{l}")
        nxt = weights[l + 1] if l + 1 < DEPTH else None
        if isinstance(nxt, CommScript):
            (o, lse2), landed = _flash_fwd(qc, kc, vt, name=f"flash_fwd{l}", comm=nxt)
            weights[l + 1] = tuple(a[0] for a in landed)
        else:
            o, lse2 = _flash_fwd(qc, kc, vt, name=f"flash_fwd{l}")
        mix = _mixer_fwd(proj, o, w_pool[l], pool_scale[l], conv_w[l], name=f"mixer_fwd{l}")
        if l == DEPTH - 1:
            r = _outproj_residual(mix, w_out, h, b_out[l], name=f"out_proj{l}")
            saved.append((hb, proj, qc, kc, v, qn, kvn, o, lse2, mix, r))
        else:
            h_next, hb_next, r = _outproj_ln(mix, w_out, h, b_out[l], ln_g[l], ln_b[l], name=f"out_proj_ln{l}")
            saved.append((hb, proj, qc, kc, v, qn, kvn, o, lse2, mix, r))
            h, hb = h_next, hb_next

    small = [None] * DEPTH
    big = [None] * DEPTH
    above = scatter_above = None
    for l in reversed(range(DEPTH)):
        w_in_t, w_out, w_uq_t, w_ukv_t = weights[l]
        hb_in, proj, qc, kc, v, qn, kvn, o, lse2, mix, r = saved[l]
        if l == DEPTH - 1:
            loss_acc, dr, drb, d_ln_g, d_ln_b, d_b_out = _loss_ln_bwd(target, r, ln_g[l], ln_b[l], name="loss_ln_bwd")
        else:
            dr, drb, d_ln_g, d_ln_b, d_b_out = _ln_bwd(dh, r, ln_g[l], name=f"ln_bwd{l}")
        dmix = _matmul(drb, w_out, "nt", name=f"dmix{l}", tm=1024, tn=1024, tk=2048, vmem_mb=56)
        d_w_out = _matmul(mix, drb, "tn", name=f"dw_out{l}", tm=1024, tn=1024, tk=2048, vmem_mb=56)
        d_mix, do, d_w_pool, d_ps, d_conv = _mixer_bwd(dmix, proj, o, w_pool[l], pool_scale[l], conv_w[l],
                                                       name=f"mixer_bwd{l}")
        delta = _attn_delta(o, do, name=f"attn_delta{l}")
        if above is not None:
            (dqb, dkvb, dkr), recv = _flash_bwd(qc, kc, v, do, lse2, delta, cos_t, sin_t, name=f"flash_bwd{l}",
                                                comm=scatter_above)
            sibling_above = above.sibling(recv)
        else:
            dqb, dkvb, dkr = _flash_bwd(qc, kc, v, do, lse2, delta, cos_t, sin_t, name=f"flash_bwd{l}")
        d_mla, d_qg, d_kvg = _mla_qkv_bwd(dqb, dkvb, dkr, proj, cos_t, sin_t, q_g[l], kv_g[l], w_uq_t, w_ukv_t,
                                          name=f"mla_qkv_bwd{l}")
        d_w_uq_t = _matmul(dqb, qn, "tn", name=f"dw_uq{l}", tm=2048, tn=512, tk=2048, vmem_mb=56)
        d_w_ukv_t = _matmul(dkvb, kvn, "tn", name=f"dw_ukv{l}", tm=2048, tn=256, tk=2048, vmem_mb=56)
        small[l] = dict(q_g=d_qg[0], kv_g=d_kvg[0], w_pool=d_w_pool, pool_scale=d_ps[0], conv_w=d_conv,
                        b_out=d_b_out[0], ln_g=d_ln_g[0], ln_b=d_ln_b[0])
        rest = (d_w_out, d_w_uq_t, d_w_ukv_t)
        if c_idx is None:
            d_w_in_t = _dproj_t_times_h(d_mla, d_mix, hb_in, name=f"dw_in{l}")
            dh = _dproj_times_w(d_mla, d_mix, w_in_t, dr, ALPHA, name=f"dh{l}")
            big[l] = (d_w_in_t,) + rest
        elif l > 0:
            d_w_in_t = _dproj_t_times_h(d_mla, d_mix, hb_in, name=f"dw_in{l}")
            above = _GradReducer(l, SHARDED_NAMES, (d_w_in_t,) + rest, c_idx)
            dh, theirs = _dproj_times_w(d_mla, d_mix, w_in_t, dr, ALPHA, name=f"dh{l}", comm=above.exchange())
            scatter_above = above.scatter(theirs)
        else:
            red_rest = _GradReducer(l, SHARDED_NAMES[1:], rest, c_idx)
            d_w_in_t, landed = _dproj_t_times_h(d_mla, d_mix, hb_in, name=f"dw_in{l}",
                                                comm=_merge_scripts(sibling_above, red_rest.exchange()))
            big[l + 1] = above.done(landed[:len(SHARDED)])
            red_in = _GradReducer(l, SHARDED_NAMES[:1], (d_w_in_t,), c_idx)
            landed = _run_comm(_merge_scripts(red_in.exchange(), red_rest.scatter(landed[len(SHARDED):])),
                               name="exchange_w_in0")
            sibling_rest = red_rest.sibling(landed[1:])
            dh, landed = _dproj_times_w(d_mla, d_mix, w_in_t, dr, ALPHA, name=f"dh{l}",
                                        comm=_merge_scripts(red_in.scatter(landed[:1]), sibling_rest))
            recv_in, others_rest = landed[:1], landed[1:]
    grad_x, _, d_emb_g, d_emb_b, _ = _ln_bwd(dh, x, emb_g, name="emb_ln_bwd", bf16_copy=False)
    if c_idx is not None:
        others_in = _run_comm(red_in.sibling(recv_in), name="send_to_sibling0")
        big[0] = red_in.done(others_in) + red_rest.done(others_rest)
    return loss_acc[0, 0], grad_x, d_emb_g, d_emb_b, small, big


SMALL_ORDER = ("emb_ln_g", "emb_ln_b", "q_norm_g", "kv_norm_g", "w_pool", "pool_scale", "b_out", "ln_g", "ln_b")
SMALL_LAYER_KEYS = ("q_g", "kv_g", "w_pool", "pool_scale", "b_out", "ln_g", "ln_b", "conv_w")


def _pack_small(arrs, extra_rows):
    flat = jnp.concatenate([a.reshape(-1) for a in arrs])
    rows = flat.shape[0] // LANE
    total = -(-(rows + extra_rows) // 8) * 8
    return jnp.pad(flat, (0, total * LANE - flat.shape[0])).reshape(total, LANE)


def kernel(x, positions, emb_ln_g, emb_ln_b, w_in, q_norm_g, kv_norm_g, w_uq, w_ukv, w_pool, pool_scale, conv_w, w_out, b_out, ln_g, ln_b, loss_target, m_emb_ln_g, m_emb_ln_b, m_w_in, m_q_norm_g, m_kv_norm_g, m_w_uq, m_w_ukv, m_w_pool, m_pool_scale, m_conv_w, m_w_out, m_b_out, m_ln_g, m_ln_b, v_emb_ln_g, v_emb_ln_b, v_w_in, v_q_norm_g, v_kv_norm_g, v_w_uq, v_w_ukv, v_w_pool, v_pool_scale, v_conv_w, v_w_out, v_b_out, v_ln_g, v_ln_b):
    xi, yi, ci = lax.axis_index("x"), lax.axis_index("y"), lax.axis_index("c")
    chip = 2 * xi + yi
    c_idx = ci.reshape(1).astype(jnp.int32)

    def t(a):
        return jnp.swapaxes(a, 1, 2)

    conv_bits = lax.bitcast_convert_type(conv_w.reshape(DEPTH, 3 * 128), BF16).reshape(DEPTH, 3, 256)
    conv_bits = jnp.pad(conv_bits, ((0, 0), (0, 13), (0, 0)))
    own = (t(w_in).astype(BF16), w_out.astype(BF16), t(w_uq).astype(BF16), t(w_ukv).astype(BF16))
    zeros = (jnp.zeros((GAP, D_MODEL), BF16), None, jnp.zeros((64, Q_LORA), BF16), None)
    gather_in0 = _allgather_script((W_IN,), own[:1], zeros[:1], (0,))
    gather0 = _allgather_script(SHARDED[1:] + (W_CONV,), own[1:] + (conv_bits,), zeros[1:] + (None,),
                                (0, 0, 0, None))
    gather1 = _allgather_script(SHARDED, own, zeros, (1, 1, 1, 1))

    loss_part, grad_x, d_emb_g, d_emb_b, grads, reduced = _local_step(
        x[0], positions[0], loss_target[0], emb_ln_g, emb_ln_b, gather_in0, gather0, gather1, q_norm_g, kv_norm_g,
        w_pool, pool_scale, b_out, ln_g, ln_b, c_idx)

    def rows(a):
        return a.reshape(1, -1) if a.ndim == 1 else a

    small_wmv = [tuple(rows(a) for a in wmv) for wmv in (
        (emb_ln_g, m_emb_ln_g, v_emb_ln_g), (emb_ln_b, m_emb_ln_b, v_emb_ln_b),
        (q_norm_g, m_q_norm_g, v_q_norm_g), (kv_norm_g, m_kv_norm_g, v_kv_norm_g), (w_pool, m_w_pool, v_w_pool),
        (pool_scale, m_pool_scale, v_pool_scale), (b_out, m_b_out, v_b_out), (ln_g, m_ln_g, v_ln_g),
        (ln_b, m_ln_b, v_ln_b))]
    packed_g = _pack_small(
        [d_emb_g, d_emb_b] + [jnp.stack([grads[l][key] for l in range(DEPTH)]) for key in SMALL_LAYER_KEYS]
        + [jnp.pad(loss_part.reshape(1), (0, LANE - 1))], 0)
    (gathered,) = _run_comm(_allgather_small_script(packed_g), name="allgather_small")
    g_tot, small_upd = _small_sum_adamw(gathered, packed_g, small_wmv, name="small_sum_adamw")
    off = sum(w.size for w, _, _ in small_wmv)
    flat_tot = g_tot.reshape(-1)

    def halves(a):
        return [reduced[l][a] for l in range(DEPTH)]

    upd = {}
    upd["w_in"] = tuple(t(o) for o in _adamw_halves(t(w_in), t(m_w_in), t(v_w_in), halves(0), c_idx,
                                                    name="adamw_w_in"))
    conv_tot = flat_tot[off:off + DEPTH * 3 * 512].reshape(DEPTH, 3, 512)
    loss = flat_tot[off + DEPTH * 3 * 512]
    g_conv = lax.dynamic_slice_in_dim(conv_tot, chip * 128, 128, axis=2)

    def whole(a):
        return jnp.stack([jnp.where(ci == 0, jnp.concatenate([mine, oth], axis=1),
                                    jnp.concatenate([oth, mine], axis=1)) for mine, oth in halves(a)])

    upd["w_out"] = _adamw_halves(w_out, m_w_out, v_w_out, halves(1), c_idx, name="adamw_w_out")
    g_uq, g_ukv = t(whole(2)), t(whole(3))
    upd["w_uq"] = (g_uq,) + _adamw(w_uq, g_uq, m_w_uq, v_w_uq, name="adamw_w_uq")
    upd["w_ukv"] = (g_ukv,) + _adamw(w_ukv, g_ukv, m_w_ukv, v_w_ukv, name="adamw_w_ukv")
    upd["conv_w"] = (g_conv,) + _adamw(conv_w, g_conv, m_conv_w, v_conv_w, name="adamw_conv_w")
    for nm, res in zip(SMALL_ORDER, small_upd):
        upd[nm] = tuple(a.reshape(-1) for a in res) if nm in ("emb_ln_g", "emb_ln_b") else res

    order = ("emb_ln_g", "emb_ln_b", "w_in", "q_norm_g", "kv_norm_g", "w_uq", "w_ukv", "w_pool", "pool_scale",
             "conv_w", "w_out", "b_out", "ln_g", "ln_b")
    outs = [loss, grad_x[None]]
    for field in range(4):
        outs += [upd[nm][field] for nm in order]
    return tuple(outs)
```

```python
import collections

import jax
import jax.numpy as jnp
from jax import lax
from jax.experimental import pallas as pl
from jax.experimental.pallas import tpu as pltpu

F32 = jnp.float32
BF16 = jnp.bfloat16
MESH = pl.DeviceIdType.MESH

D_MODEL = 2048
DEPTH = 2
N_HEADS = 8
NOPE = 128
ROPE = 64
Q_LORA = 512
KV_LORA = 256
D_MLA = 1024
POOL_WINDOWS = (2, 4, 8, 16)
D_IN_PROJ = 4928
LN_EPS = 1e-5
RMS_EPS = 1e-6
ROPE_THETA = 10000.0
ALPHA = (2 * DEPTH) ** 0.25
SCALE = (NOPE + ROPE) ** -0.5
LOG2E = 1.4426950408889634
SCALE_LOG2E = SCALE * LOG2E
ADAM_LR = 0.001
ADAM_B1 = 0.9
ADAM_B2 = 0.999
ADAM_EPS = 1e-08
ADAM_WD = 0.01
ADAM_STEP = 10

NP = 5120
GAP_AT = 832
GAP = NP - D_IN_PROJ
W_MLA = 1024
W_MIX = NP - W_MLA
HALO = 16
LANE = 128
N_CHIPS = 4
N_DEV = 8
TQ = 512
FWD_GROUP = 4

NN = (((1,), (0,)), ((), ()))
NT = (((1,), (1,)), ((), ()))
TN = (((0,), (0,)), ((), ()))


CommScript = collections.namedtuple("CommScript", "args out_shape n_sems start finish")
HBM_SPEC = pl.BlockSpec(memory_space=pl.ANY)


def _pcall(kern, *, name, out_shape, grid=None, in_specs=None, out_specs=None, scratch=(), dims=None,
           vmem_mb=None, comm=None):
    cp = {}
    if dims is not None:
        cp["dimension_semantics"] = dims if comm is None else ("arbitrary",) * len(dims)
    if vmem_mb is not None:
        cp["vmem_limit_bytes"] = vmem_mb << 20
    if comm is None:
        args = dict(name=name, out_shape=out_shape, scratch_shapes=list(scratch),
                    compiler_params=pltpu.CompilerParams(**cp))
        if grid is not None:
            args["grid"] = grid
        if in_specs is not None:
            args["in_specs"] = in_specs
        if out_specs is not None:
            args["out_specs"] = out_specs
        return pl.pallas_call(kern, **args)

    single = not isinstance(out_shape, (tuple, list))
    own_out = (out_shape,) if single else tuple(out_shape)
    own_out_specs = (out_specs,) if single else tuple(out_specs)
    n_in, n_out, n_scr = len(in_specs), len(own_out), len(scratch)
    na, no = len(comm.args), len(comm.out_shape)

    def at(end):
        cond = None
        for d, n in enumerate(grid):
            here = pl.program_id(d) == (n - 1 if end else 0)
            cond = here if cond is None else jnp.logical_and(cond, here)
        return cond

    def wrapped(*refs):
        own_in, c_in = refs[:n_in], refs[n_in:n_in + na]
        o0 = n_in + na
        own_o, c_out = refs[o0:o0 + n_out], refs[o0 + n_out:o0 + n_out + no]
        s0 = o0 + n_out + no
        own_s, (send_sems, recv_sems) = refs[s0:s0 + n_scr], refs[s0 + n_scr:]

        @pl.when(at(False))
        def _():
            comm.start(c_in, c_out, send_sems, recv_sems)

        kern(*own_in, *own_o, *own_s)

        @pl.when(at(True))
        def _():
            comm.finish(c_in, c_out, send_sems, recv_sems)

    call = pl.pallas_call(
        wrapped, name=name, out_shape=own_out + tuple(comm.out_shape), grid=grid,
        in_specs=list(in_specs) + [HBM_SPEC] * na, out_specs=own_out_specs + (HBM_SPEC,) * no,
        scratch_shapes=list(scratch) + [pltpu.SemaphoreType.DMA((comm.n_sems,)),
                                        pltpu.SemaphoreType.DMA((comm.n_sems,))],
        compiler_params=pltpu.CompilerParams(**cp))

    def run(*args):
        res = call(*args, *comm.args)
        own = res[0] if single else tuple(res[:n_out])
        return own, tuple(res[n_out:])

    return run


def _run_comm(script, *, name):
    na, no = len(script.args), len(script.out_shape)

    def body(*refs):
        ins, outs = refs[:na], refs[na:na + no]
        send_sems, recv_sems = refs[na + no:]
        script.start(ins, outs, send_sems, recv_sems)
        script.finish(ins, outs, send_sems, recv_sems)

    return pl.pallas_call(
        body, name=name, out_shape=tuple(script.out_shape), in_specs=[HBM_SPEC] * na, out_specs=(HBM_SPEC,) * no,
        scratch_shapes=[pltpu.SemaphoreType.DMA((script.n_sems,)), pltpu.SemaphoreType.DMA((script.n_sems,))])(
            *script.args)


def _sigmoid(g):
    return 1.0 / (1.0 + jnp.exp(-g))


def _silu_and_grad(g):
    sig = _sigmoid(g)
    return g * sig, sig * (1.0 + g * (1.0 - sig))


def _matmul(a, b, mode, *, name, tm, tn, tk, out_dtype=F32, vmem_mb=48, comm=None):
    if mode == "nn":
        (M, K), N = a.shape, b.shape[1]
    elif mode == "nt":
        (M, K), N = a.shape, b.shape[0]
    else:
        (K, M), N = a.shape, b.shape[1]
    tm, tn, tk = min(tm, M), min(tn, N), min(tk, K)
    assert M % tm == 0 and N % tn == 0 and K % tk == 0, (name, M, N, K)
    nk = K // tk
    dn = {"nn": NN, "nt": NT, "tn": TN}[mode]
    if mode == "tn":
        a_spec = pl.BlockSpec((tk, tm), lambda i, j, k: (k, i))
    else:
        a_spec = pl.BlockSpec((tm, tk), lambda i, j, k: (i, k))
    if mode == "nt":
        b_spec = pl.BlockSpec((tn, tk), lambda i, j, k: (j, k))
    else:
        b_spec = pl.BlockSpec((tk, tn), lambda i, j, k: (k, j))
    o_spec = pl.BlockSpec((tm, tn), lambda i, j, k: (i, j))

    def kern(a_ref, b_ref, o_ref, *rest):
        part = lax.dot_general(a_ref[...].astype(BF16), b_ref[...].astype(BF16), dn,
                               preferred_element_type=F32)
        if nk == 1:
            o_ref[...] = part.astype(out_dtype)
        else:
            acc_ref = rest[0]
            k = pl.program_id(2)

            @pl.when(k == 0)
            def _():
                acc_ref[...] = part

            @pl.when(k > 0)
            def _():
                acc_ref[...] += part

            @pl.when(k == nk - 1)
            def _():
                o_ref[...] = acc_ref[...].astype(out_dtype)

    scratch = [pltpu.VMEM((tm, tn), F32)] if nk > 1 else []
    return _pcall(kern, name=name, out_shape=jax.ShapeDtypeStruct((M, N), out_dtype),
                  grid=(M // tm, N // tn, nk), in_specs=[a_spec, b_spec], out_specs=o_spec, scratch=scratch,
                  dims=("parallel", "parallel", "arbitrary"), vmem_mb=vmem_mb, comm=comm)(a, b)


def _dproj_times_w(d_mla, d_mix, wt, add, add_scale, *, name, comm=None):
    S = d_mla.shape[0]
    Dm = wt.shape[1]
    tm, tn, tk = min(1024, S), 1024, 2048
    nk = 1 + W_MIX // tk

    def kern(a1_ref, a2_ref, b1_ref, b2_ref, add_ref, o_ref, acc_ref):
        k = pl.program_id(2)

        @pl.when(k == 0)
        def _():
            acc_ref[...] = jnp.dot(a1_ref[...], b1_ref[...], preferred_element_type=F32)

        @pl.when(k > 0)
        def _():
            acc_ref[...] += jnp.dot(a2_ref[...], b2_ref[...], preferred_element_type=F32)

        @pl.when(k == nk - 1)
        def _():
            o_ref[...] = add_scale * add_ref[...] + acc_ref[...]

    o_spec = pl.BlockSpec((tm, tn), lambda i, j, k: (i, j))
    b2_spec = pl.BlockSpec((pl.Element(tk), pl.Element(tn)),
                           lambda i, j, k: (pl.multiple_of(W_MLA + tk * jnp.maximum(k - 1, 0), W_MLA),
                                            pl.multiple_of(j * tn, tn)))
    return _pcall(kern, name=name, out_shape=jax.ShapeDtypeStruct((S, Dm), F32), grid=(S // tm, Dm // tn, nk),
                  in_specs=[pl.BlockSpec((tm, W_MLA), lambda i, j, k: (i, 0)),
                            pl.BlockSpec((tm, tk), lambda i, j, k: (i, jnp.maximum(k - 1, 0))),
                            pl.BlockSpec((W_MLA, tn), lambda i, j, k: (0, j)), b2_spec, o_spec],
                  out_specs=o_spec, scratch=[pltpu.VMEM((tm, tn), F32)],
                  dims=("parallel", "parallel", "arbitrary"), vmem_mb=56, comm=comm)(d_mla, d_mix, wt, wt, add)


def _dproj_t_times_h(d_mla, d_mix, h, *, name, comm=None):
    S, Dm = h.shape
    tm, tn, tk = W_MLA, 1024, min(2048, S)
    nk = S // tk

    def kern(a1_ref, a2_ref, b_ref, o_ref, acc_ref):
        i = pl.program_id(0)
        k = pl.program_id(2)
        b = b_ref[...].astype(BF16)

        def accumulate(part):
            @pl.when(k == 0)
            def _():
                acc_ref[...] = part

            @pl.when(k > 0)
            def _():
                acc_ref[...] += part

        @pl.when(i == 0)
        def _():
            accumulate(lax.dot_general(a1_ref[...], b, TN, preferred_element_type=F32))

        @pl.when(i > 0)
        def _():
            accumulate(lax.dot_general(a2_ref[...], b, TN, preferred_element_type=F32))

        @pl.when(k == nk - 1)
        def _():
            o_ref[...] = acc_ref[...]

    return _pcall(kern, name=name, out_shape=jax.ShapeDtypeStruct((NP, Dm), F32), grid=(NP // tm, Dm // tn, nk),
                  in_specs=[pl.BlockSpec((tk, tm), lambda i, j, k: (jnp.where(i == 0, k, nk - 1), 0)),
                            pl.BlockSpec((tk, tm), lambda i, j, k: (jnp.where(i == 0, 0, k), jnp.maximum(i - 1, 0))),
                            pl.BlockSpec((tk, tn), lambda i, j, k: (k, j))],
                  out_specs=pl.BlockSpec((tm, tn), lambda i, j, k: (i, j)), scratch=[pltpu.VMEM((tm, tn), F32)],
                  dims=("parallel", "parallel", "arbitrary"), vmem_mb=48, comm=comm)(d_mla, d_mix, h)


def _ln_fwd(x, g, b, *, name, comm=None):
    S, Dm = x.shape
    tm = min(512, S)

    def kern(x_ref, g_ref, b_ref, y_ref, yb_ref):
        xf = x_ref[...]
        mu = jnp.mean(xf, axis=-1, keepdims=True)
        xc = xf - mu
        var = jnp.mean(xc * xc, axis=-1, keepdims=True)
        y = xc * lax.rsqrt(var + LN_EPS) * g_ref[...] + b_ref[...]
        y_ref[...] = y
        yb_ref[...] = y.astype(BF16)

    row = pl.BlockSpec((tm, Dm), lambda i: (i, 0))
    vec = pl.BlockSpec((1, Dm), lambda i: (0, 0))
    return _pcall(kern, name=name,
                  out_shape=(jax.ShapeDtypeStruct((S, Dm), F32), jax.ShapeDtypeStruct((S, Dm), BF16)),
                  grid=(S // tm,), in_specs=[row, vec, vec], out_specs=(row, row), dims=("parallel",), vmem_mb=48,
                  comm=comm)(
                      x, g.reshape(1, Dm), b.reshape(1, Dm))


def _ln_bwd(dy, r, g, *, name, bf16_copy=True):
    S, Dm = r.shape
    tm = min(512, S)

    def kern(dy_ref, r_ref, g_ref, dr_ref, *rest):
        drb_ref = rest[0] if bf16_copy else None
        dg_ref, db_ref, ds_ref = rest[-3:]

        @pl.when(pl.program_id(0) == 0)
        def _():
            dg_ref[...] = jnp.zeros_like(dg_ref)
            db_ref[...] = jnp.zeros_like(db_ref)
            ds_ref[...] = jnp.zeros_like(ds_ref)

        rf = r_ref[...]
        dyf = dy_ref[...]
        mu = jnp.mean(rf, axis=-1, keepdims=True)
        xc = rf - mu
        var = jnp.mean(xc * xc, axis=-1, keepdims=True)
        rstd = lax.rsqrt(var + LN_EPS)
        xhat = xc * rstd
        dxh = dyf * g_ref[...]
        c1 = jnp.mean(dxh, axis=-1, keepdims=True)
        c2 = jnp.mean(dxh * xhat, axis=-1, keepdims=True)
        dr = rstd * (dxh - c1 - xhat * c2)
        dr_ref[...] = dr
        if bf16_copy:
            drb_ref[...] = dr.astype(BF16)
        dg_ref[...] += jnp.sum(dyf * xhat, axis=0, keepdims=True)
        db_ref[...] += jnp.sum(dyf, axis=0, keepdims=True)
        ds_ref[...] += jnp.sum(dr, axis=0, keepdims=True)

    row = pl.BlockSpec((tm, Dm), lambda i: (i, 0))
    vec = pl.BlockSpec((1, Dm), lambda i: (0, 0))
    vshape = jax.ShapeDtypeStruct((1, Dm), F32)
    copies = ((jax.ShapeDtypeStruct((S, Dm), BF16),), (row,)) if bf16_copy else ((), ())
    res = _pcall(kern, name=name,
                 out_shape=(jax.ShapeDtypeStruct((S, Dm), F32),) + copies[0] + (vshape, vshape, vshape),
                 grid=(S // tm,), in_specs=[row, row, vec], out_specs=(row,) + copies[1] + (vec, vec, vec),
                 dims=("arbitrary",), vmem_mb=48)(dy, r, g.reshape(1, Dm))
    return res if bf16_copy else (res[0], None) + tuple(res[1:])


def _loss_ln_bwd(target, r, g, b, *, name):
    S, Dm = r.shape
    tm = min(512, S)

    def kern(t_ref, r_ref, g_ref, b_ref, l_ref, dr_ref, drb_ref, dg_ref, db_ref, ds_ref):
        @pl.when(pl.program_id(0) == 0)
        def _():
            l_ref[...] = jnp.zeros_like(l_ref)
            dg_ref[...] = jnp.zeros_like(dg_ref)
            db_ref[...] = jnp.zeros_like(db_ref)
            ds_ref[...] = jnp.zeros_like(ds_ref)

        rf = r_ref[...]
        mu = jnp.mean(rf, axis=-1, keepdims=True)
        xc = rf - mu
        var = jnp.mean(xc * xc, axis=-1, keepdims=True)
        rstd = lax.rsqrt(var + LN_EPS)
        xhat = xc * rstd
        e = (xhat * g_ref[...] + b_ref[...]) - t_ref[...]
        dyf = e / float(Dm)
        per_row = jnp.mean(e * e, axis=-1, keepdims=True)
        l_ref[...] += 0.5 * jnp.sum(per_row, axis=0, keepdims=True)
        dxh = dyf * g_ref[...]
        c1 = jnp.mean(dxh, axis=-1, keepdims=True)
        c2 = jnp.mean(dxh * xhat, axis=-1, keepdims=True)
        dr = rstd * (dxh - c1 - xhat * c2)
        dr_ref[...] = dr
        drb_ref[...] = dr.astype(BF16)
        dg_ref[...] += jnp.sum(dyf * xhat, axis=0, keepdims=True)
        db_ref[...] += jnp.sum(dyf, axis=0, keepdims=True)
        ds_ref[...] += jnp.sum(dr, axis=0, keepdims=True)

    row = pl.BlockSpec((tm, Dm), lambda i: (i, 0))
    vec = pl.BlockSpec((1, Dm), lambda i: (0, 0))
    acc = pl.BlockSpec((8, LANE), lambda i: (0, 0))
    vshape = jax.ShapeDtypeStruct((1, Dm), F32)
    return _pcall(kern, name=name,
                  out_shape=(jax.ShapeDtypeStruct((8, LANE), F32), jax.ShapeDtypeStruct((S, Dm), F32),
                             jax.ShapeDtypeStruct((S, Dm), BF16), vshape, vshape, vshape),
                  grid=(S // tm,), in_specs=[row, row, vec, vec], out_specs=(acc, row, row, vec, vec, vec),
                  dims=("arbitrary",), vmem_mb=56)(target, r, g.reshape(1, Dm), b.reshape(1, Dm))


def _rot_sum(t):
    return pltpu.roll(t, 32, 1) + pltpu.roll(t, 96, 1)


def _mla_qkv(proj, cos_t, sin_t, qg, kvg, wuq_t, wukv_t, *, name):
    S = proj.shape[0]
    tm = min(256, S)

    def kern(ql_ref, kvl_ref, kr_ref, cos_ref, sin_ref, qg_ref, kvg_ref, wuq_ref, wukv_ref,
             qc_ref, kc_ref, v_ref, vt_ref, qn_ref, kvn_ref):
        cosv = cos_ref[...]
        sinv = sin_ref[...]

        def rope(t):
            return t * cosv + _rot_sum(t) * sinv

        ql = ql_ref[...]
        qn = (ql * lax.rsqrt(jnp.mean(ql * ql, axis=-1, keepdims=True) + RMS_EPS) * qg_ref[...]).astype(BF16)
        kvl = kvl_ref[...]
        kvn = (kvl * lax.rsqrt(jnp.mean(kvl * kvl, axis=-1, keepdims=True) + RMS_EPS) * kvg_ref[...]).astype(BF16)
        qn_ref[...] = qn
        kvn_ref[...] = kvn
        q = lax.dot_general(qn, wuq_ref[...], NT, preferred_element_type=F32)
        kv = lax.dot_general(kvn, wukv_ref[...], NT, preferred_element_type=F32)
        kr = rope(kr_ref[...]).astype(BF16)
        for h in range(N_HEADS):
            c0 = 256 * h
            qc_ref[:, c0:c0 + 128] = q[:, c0:c0 + 128].astype(BF16)
            qc_ref[:, c0 + 128:c0 + 256] = rope(q[:, c0 + 128:c0 + 256]).astype(BF16)
            kc_ref[:, c0:c0 + 128] = kv[:, c0:c0 + 128].astype(BF16)
            kc_ref[:, c0 + 128:c0 + 256] = kr
            vh = kv[:, c0 + 128:c0 + 256]
            v_ref[:, 128 * h:128 * h + 128] = vh.astype(BF16)
            vt_ref[h] = jnp.transpose(vh).astype(BF16)

    def row(w, blk):
        return pl.BlockSpec((tm, w), lambda i: (i, blk))

    def full(shape):
        return pl.BlockSpec(shape, lambda i: (0,) * len(shape))

    t = min(TQ, S)
    per = t // tm
    vt_spec = pl.BlockSpec((N_HEADS, None, 128, tm), lambda i: (0, i // per, 0, i % per))
    outs = (jax.ShapeDtypeStruct((S, 2048), BF16), jax.ShapeDtypeStruct((S, 2048), BF16),
            jax.ShapeDtypeStruct((S, 1024), BF16), jax.ShapeDtypeStruct((N_HEADS, S // t, 128, t), BF16),
            jax.ShapeDtypeStruct((S, Q_LORA), BF16), jax.ShapeDtypeStruct((S, KV_LORA), BF16))
    return _pcall(kern, name=name, out_shape=outs, grid=(S // tm,),
                  in_specs=[row(512, 0), row(256, 2), row(128, 6), row(128, 0), row(128, 0),
                            full((1, Q_LORA)), full((1, KV_LORA)), full((2048, Q_LORA)), full((2048, KV_LORA))],
                  out_specs=(row(2048, 0), row(2048, 0), row(1024, 0), vt_spec, row(512, 0), row(256, 0)),
                  dims=("parallel",), vmem_mb=48)(
                      proj, proj, proj, cos_t, sin_t, qg.reshape(1, -1), kvg.reshape(1, -1), wuq_t, wukv_t)


def _mla_qkv_bwd(dqb, dkvb, dkr_heads, proj, cos_t, sin_t, qg, kvg, wuq_t, wukv_t, *, name):
    S = proj.shape[0]
    tm = min(256, S)

    def kern(dqb_ref, dkvb_ref, dkrh_ref, ql_ref, kvl_ref, cos_ref, sin_ref, qg_ref, kvg_ref, wuq_ref, wukv_ref,
             dml_ref, dqg_ref, dkvg_ref):
        @pl.when(pl.program_id(0) == 0)
        def _():
            dqg_ref[...] = jnp.zeros_like(dqg_ref)
            dkvg_ref[...] = jnp.zeros_like(dkvg_ref)

        cosv = cos_ref[...]
        sinv = sin_ref[...]

        def unrope(t):
            return t * cosv - _rot_sum(t) * sinv

        dkr = dkrh_ref[:, 0:128]
        for h in range(1, N_HEADS):
            dkr = dkr + dkrh_ref[:, 128 * h:128 * h + 128]

        def rms_bwd(x, g, dy):
            n = x.shape[-1]
            rs = lax.rsqrt(jnp.mean(x * x, axis=-1, keepdims=True) + RMS_EPS)
            dyg = dy * g
            dx = rs * dyg - x * (rs * rs * rs) * (jnp.sum(dyg * x, axis=-1, keepdims=True) / n)
            return dx, jnp.sum(dy * (x * rs), axis=0, keepdims=True)

        dqn = jnp.dot(dqb_ref[...], wuq_ref[...], preferred_element_type=F32)
        dql, dqg = rms_bwd(ql_ref[...], qg_ref[...], dqn)
        dqg_ref[...] += dqg
        dkvn = jnp.dot(dkvb_ref[...], wukv_ref[...], preferred_element_type=F32)
        dkvl, dkvg = rms_bwd(kvl_ref[...], kvg_ref[...], dkvn)
        dkvg_ref[...] += dkvg
        dml_ref[:, 0:512] = dql.astype(BF16)
        dml_ref[:, 512:768] = dkvl.astype(BF16)
        dml_ref[:, 768:896] = unrope(dkr).astype(BF16)
        dml_ref[:, 896:1024] = jnp.zeros((tm, 128), BF16)

    def row(w, blk):
        return pl.BlockSpec((tm, w), lambda i: (i, blk))

    def full(shape):
        return pl.BlockSpec(shape, lambda i: (0,) * len(shape))

    outs = (jax.ShapeDtypeStruct((S, W_MLA), BF16), jax.ShapeDtypeStruct((1, Q_LORA), F32),
            jax.ShapeDtypeStruct((1, KV_LORA), F32))
    return _pcall(kern, name=name, out_shape=outs, grid=(S // tm,),
                  in_specs=[row(2048, 0), row(2048, 0), row(1024, 0), row(512, 0), row(256, 2),
                            row(128, 0), row(128, 0), full((1, Q_LORA)), full((1, KV_LORA)),
                            full((2048, Q_LORA)), full((2048, KV_LORA))],
                  out_specs=(row(W_MLA, 0), full((1, Q_LORA)), full((1, KV_LORA))),
                  dims=("arbitrary",), vmem_mb=56)(
                      dqb, dkvb, dkr_heads, proj, proj, cos_t, sin_t, qg.reshape(1, -1), kvg.reshape(1, -1),
                      wuq_t, wukv_t)


def _flash_fwd(qc, kc, vt, *, name, comm=None):
    S = qc.shape[0]
    t = min(TQ, S)
    n = S // t

    def kern(q_ref, k_ref, vt_ref, o_ref, lse_ref, m_s, l_s, acc_s):
        qi = pl.program_id(1)
        m_s[...] = jnp.full_like(m_s, -jnp.inf)
        l_s[...] = jnp.zeros_like(l_s)
        acc_s[...] = jnp.zeros_like(acc_s)

        half = t // 2

        def scores(kb, q_lo=0, q_n=t, k_n=t):
            k0 = pl.multiple_of(kb * t, t)
            return lax.dot_general(k_ref[pl.ds(k0, k_n), :], q_ref[q_lo:q_lo + q_n, :], NT,
                                   preferred_element_type=F32)

        def update(kb, st, q_lo=0, diagonal=False):
            k_n, q_n = st.shape
            if diagonal:
                krow = lax.broadcasted_iota(jnp.int32, (k_n, q_n), 0)
                qcol = lax.broadcasted_iota(jnp.int32, (k_n, q_n), 1) + q_lo
                st = jnp.where(krow <= qcol, st, -jnp.inf)
            lanes = slice(q_lo, q_lo + q_n)
            m_prev = m_s[:, lanes]
            m_new = jnp.maximum(m_prev, jnp.max(st, axis=0, keepdims=True))
            a = jnp.exp2((m_prev - m_new) * SCALE_LOG2E)
            pt = jnp.exp2((st - m_new) * SCALE_LOG2E)
            l_s[:, lanes] = a * l_s[:, lanes] + jnp.sum(pt, axis=0, keepdims=True)
            acc_s[:, lanes] = a * acc_s[:, lanes] + jnp.dot(vt_ref[kb, :, 0:k_n], pt.astype(BF16),
                                                            preferred_element_type=F32)
            m_s[:, lanes] = m_new

        def group(kb, count, last_diagonal):
            whole = count - 1 if last_diagonal else count
            sts = [scores(kb + g) for g in range(whole)]
            if last_diagonal:
                kd = kb + count - 1
                s_lo, s_hi = scores(kd, 0, half, half), scores(kd, half, half, t)
            for g in range(whole):
                update(kb + g, sts[g])
            if last_diagonal:
                update(kd, s_lo, 0, True)
                update(kd, s_hi, half, True)

        def body(i, carry):
            group(FWD_GROUP * i, FWD_GROUP, False)
            return carry

        full = qi // FWD_GROUP
        lax.fori_loop(0, full, body, 0)
        for rem in range(FWD_GROUP):
            @pl.when(qi - FWD_GROUP * full == rem)
            def _():
                group(qi - rem, rem + 1, True)
        o_ref[...] = jnp.transpose(acc_s[...] / l_s[...])
        lse_ref[pl.ds(qi, 1), :] = m_s[...] * SCALE_LOG2E + jnp.log2(l_s[...])

    q_spec = pl.BlockSpec((t, 256), lambda h, qi: (qi, h))
    k_spec = pl.BlockSpec((S, 256), lambda h, qi: (0, h))
    vt_spec = pl.BlockSpec((None, n, 128, t), lambda h, qi: (h, 0, 0, 0))
    o_spec = pl.BlockSpec((t, 128), lambda h, qi: (qi, h))
    lse_spec = pl.BlockSpec((None, n, t), lambda h, qi: (h, 0, 0))
    return _pcall(kern, name=name,
                  out_shape=(jax.ShapeDtypeStruct((S, D_MLA), F32), jax.ShapeDtypeStruct((N_HEADS, n, t), F32)),
                  grid=(N_HEADS, n), in_specs=[q_spec, k_spec, vt_spec], out_specs=(o_spec, lse_spec),
                  scratch=[pltpu.VMEM((1, t), F32), pltpu.VMEM((1, t), F32), pltpu.VMEM((128, t), F32)],
                  dims=("parallel", "arbitrary"), vmem_mb=48, comm=comm)(qc, kc, vt)


def _attn_delta(o, do, *, name):
    S = o.shape[0]
    t = min(TQ, S)
    n = S // t

    def kern(o_ref, do_ref, dl_ref):
        i = pl.program_id(0)
        prod = o_ref[...] * do_ref[...]
        lane = lax.broadcasted_iota(jnp.int32, (t, LANE), 1)
        dmat = jnp.zeros((t, LANE), F32)
        for h in range(N_HEADS):
            dmat = jnp.where(lane == h, jnp.sum(prod[:, 128 * h:128 * h + 128], axis=1, keepdims=True), dmat)
        dmat_t = jnp.transpose(dmat)
        for h in range(N_HEADS):
            dl_ref[h, pl.ds(i, 1), :] = dmat_t[h:h + 1, :]

    row = pl.BlockSpec((t, D_MLA), lambda i: (i, 0))
    return _pcall(kern, name=name, out_shape=jax.ShapeDtypeStruct((N_HEADS, n, t), F32), grid=(n,),
                  in_specs=[row, row], out_specs=pl.BlockSpec((N_HEADS, n, t), lambda i: (0, 0, 0)),
                  dims=("arbitrary",), vmem_mb=48)(o, do)


def _flash_bwd(qc, kc, v, do, lse2, delta, cos_t, sin_t, *, name, comm=None):
    S = qc.shape[0]
    t = min(TQ, S)
    n = S // t

    def kern(q_ref, k_ref, v_ref, do_ref, lse_ref, dl_ref, cos_ref, sin_ref, dqb_ref, dkvb_ref, dkr_ref,
             dq_ref, dk_ref, dv_ref):
        ki = pl.program_id(1)

        @pl.when(ki == 0)
        def _():
            dq_ref[...] = jnp.zeros_like(dq_ref)

        dk_ref[...] = jnp.zeros_like(dk_ref)
        dv_ref[...] = jnp.zeros_like(dv_ref)

        half = t // 2

        def step(qb, q_lo=0, q_n=t, k_n=t, diagonal=False):
            q0 = pl.multiple_of(qb * t + q_lo, half)
            lanes = slice(q_lo, q_lo + q_n)
            kt = k_ref[0:k_n, :]
            qblk = q_ref[pl.ds(q0, q_n), :]
            dob = do_ref[pl.ds(q0, q_n), :].astype(BF16)
            st = lax.dot_general(kt, qblk, NT, preferred_element_type=F32)
            pt = jnp.exp2(st * SCALE_LOG2E - lse_ref[pl.ds(qb, 1), lanes])
            if diagonal:
                krow = lax.broadcasted_iota(jnp.int32, (k_n, q_n), 0)
                qcol = lax.broadcasted_iota(jnp.int32, (k_n, q_n), 1) + q_lo
                pt = jnp.where(krow <= qcol, pt, 0.0)
            dv_ref[0:k_n, :] += jnp.dot(pt.astype(BF16), dob, preferred_element_type=F32)
            dpt = lax.dot_general(v_ref[0:k_n, :], dob, NT, preferred_element_type=F32)
            dst = (pt * (dpt - dl_ref[pl.ds(qb, 1), lanes]) * SCALE).astype(BF16)
            dk_ref[0:k_n, :] += jnp.dot(dst, qblk, preferred_element_type=F32)
            dq_ref[pl.ds(q0, q_n), :] += lax.dot_general(dst, kt, TN, preferred_element_type=F32)

        step(ki, 0, half, half, True)
        step(ki, half, half, t, True)
        rest = n - 1 - ki

        def body(i, carry):
            step(ki + 1 + 2 * i)
            step(ki + 2 + 2 * i)
            return carry

        lax.fori_loop(0, rest // 2, body, 0)

        @pl.when(rest % 2 == 1)
        def _():
            step(n - 1)

        dkvb_ref[:, 0:128] = dk_ref[:, 0:128].astype(BF16)
        dkvb_ref[:, 128:256] = dv_ref[...].astype(BF16)
        dkr_ref[...] = dk_ref[:, 128:256]

        @pl.when(ki == n - 1)
        def _():
            dqb_ref[:, 0:128] = dq_ref[:, 0:128].astype(BF16)
            dqr = dq_ref[:, 128:256]
            dqb_ref[:, 128:256] = (dqr * cos_ref[...] - _rot_sum(dqr) * sin_ref[...]).astype(BF16)

    def whole(w):
        return pl.BlockSpec((S, w), lambda h, ki: (0, h))

    def krow(w):
        return pl.BlockSpec((t, w), lambda h, ki: (ki, h))

    stat = pl.BlockSpec((None, n, t), lambda h, ki: (h, 0, 0))
    table = pl.BlockSpec((S, 128), lambda h, ki: (0, 0))
    return _pcall(kern, name=name,
                  out_shape=(jax.ShapeDtypeStruct((S, 2048), BF16), jax.ShapeDtypeStruct((S, 2048), BF16),
                             jax.ShapeDtypeStruct((S, D_MLA), F32)),
                  grid=(N_HEADS, n),
                  in_specs=[whole(256), krow(256), krow(128), whole(128), stat, stat, table, table],
                  out_specs=(whole(256), krow(256), krow(128)),
                  scratch=[pltpu.VMEM((S, 256), F32), pltpu.VMEM((t, 256), F32), pltpu.VMEM((t, 128), F32)],
                  dims=("parallel", "arbitrary"), vmem_mb=56, comm=comm)(qc, kc, v, do, lse2, delta, cos_t, sin_t)


def _mixer_specs(S, tm):
    hb = tm // HALO
    last_hb = S // HALO - 1

    def main(w, blk):
        return pl.BlockSpec((tm, w), lambda i: (i, blk))

    def prev(w, blk):
        return pl.BlockSpec((HALO, w), lambda i: (jnp.maximum(i * hb - 1, 0), blk))

    def nxt(w, blk):
        return pl.BlockSpec((HALO, w), lambda i: (jnp.minimum((i + 1) * hb, last_hb), blk))

    def full(shape):
        return pl.BlockSpec(shape, lambda i: (0,) * len(shape))

    return main, prev, nxt, full


def _fill_halo(i, xp, xu, hp_ref, hch_ref, hcc_ref, pin_ref, ch_ref, cc_ref, tm):
    first = i == 0
    xp[0:HALO, :] = jnp.where(first, 0.0, hp_ref[...])
    xp[HALO:HALO + tm, :] = pin_ref[...]
    xu[0:HALO, :] = jnp.where(first, 0.0, hch_ref[...] * hcc_ref[...])
    xu[HALO:HALO + tm, :] = cc_ref[...] * ch_ref[...]


def _pooled(xp, g, t1, tm):
    w = POOL_WINDOWS[g]
    lanes = slice(128 * g, 128 * g + 128)
    x0 = xp[HALO:HALO + tm, lanes]
    acc = x0
    for k in range(1, w):
        acc = acc + xp[HALO - k:HALO - k + tm, lanes]
    return acc / jnp.minimum(t1, float(w)) - x0


def _conv_fwd(xu, cw_ref, tm):
    return (cw_ref[0:1, :] * xu[HALO - 2:HALO - 2 + tm, :] + cw_ref[1:2, :] * xu[HALO - 1:HALO - 1 + tm, :]
            + cw_ref[2:3, :] * xu[HALO:HALO + tm, :])


def _mixer_fwd(proj, o, wpool, ps, convw, *, name):
    S = proj.shape[0]
    tm = min(256, S)
    main, prev, _, full = _mixer_specs(S, tm)

    def kern(gm_ref, pin_ref, gp_ref, ch_ref, cb_ref, cc_ref, gc_ref, hp_ref, hch_ref, hcc_ref,
             o_ref, wp_ref, ps_ref, cw_ref, mix_ref, xp, xu):
        i = pl.program_id(0)
        _fill_halo(i, xp, xu, hp_ref, hch_ref, hcc_ref, pin_ref, ch_ref, cc_ref, tm)
        t1 = (i * tm + lax.broadcasted_iota(jnp.int32, (tm, 1), 0) + 1).astype(F32)
        for g in range(4):
            lanes = slice(128 * g, 128 * g + 128)
            pooled = _pooled(xp, g, t1, tm)
            z = jnp.dot(pooled.astype(BF16), wp_ref[g].astype(BF16), preferred_element_type=F32)
            gp = gp_ref[:, lanes]
            y = z * ps_ref[:, lanes] * (gp * _sigmoid(gp))
            mix_ref[:, 1024 + 128 * g:1024 + 128 * g + 128] = y.astype(BF16)
        gc = gc_ref[...]
        mix_ref[:, 1536:2048] = (cb_ref[...] * _conv_fwd(xu, cw_ref, tm) * (gc * _sigmoid(gc))).astype(BF16)
        gm = gm_ref[...]
        mix_ref[:, 0:1024] = (o_ref[...] * (gm * _sigmoid(gm))).astype(BF16)

    return _pcall(kern, name=name, out_shape=jax.ShapeDtypeStruct((S, 2048), BF16), grid=(S // tm,),
                  in_specs=[main(1024, 1), main(512, 4), main(512, 5), main(512, 6), main(512, 7), main(512, 8),
                            main(512, 9), prev(512, 4), prev(512, 6), prev(512, 8),
                            main(1024, 0), full((4, 128, 128)), full((1, 512)), full((3, 512))],
                  out_specs=main(2048, 0),
                  scratch=[pltpu.VMEM((tm + HALO, 512), F32), pltpu.VMEM((tm + HALO, 512), F32)],
                  dims=("parallel",), vmem_mb=48)(
                      proj, proj, proj, proj, proj, proj, proj, proj, proj, proj, o, wpool, ps.reshape(1, 512), convw)


def _mixer_bwd(dmix, proj, o, wpool, ps, convw, *, name):
    S = proj.shape[0]
    tm = min(256, S)
    n = S // tm
    main, prev, nxt, full = _mixer_specs(S, tm)

    def kern(dm_ref, dmn_ref, gm_ref, pin_ref, gp_ref, ch_ref, cb_ref, cc_ref, gc_ref,
             hp_ref, hch_ref, hcc_ref, gpn_ref, cbn_ref, gcn_ref, o_ref, wp_ref, ps_ref, cw_ref,
             d_ref, do_ref, dwp_ref, dps_ref, dcw_ref, xp, xu, ee, ed):
        i = pl.program_id(0)
        last = i == n - 1

        @pl.when(i == 0)
        def _():
            dwp_ref[...] = jnp.zeros_like(dwp_ref)
            dps_ref[...] = jnp.zeros_like(dps_ref)
            dcw_ref[...] = jnp.zeros_like(dcw_ref)

        _fill_halo(i, xp, xu, hp_ref, hch_ref, hcc_ref, pin_ref, ch_ref, cc_ref, tm)
        t1 = (i * tm + lax.broadcasted_iota(jnp.int32, (tm, 1), 0) + 1).astype(F32)
        t1n = ((i + 1) * tm + lax.broadcasted_iota(jnp.int32, (HALO, 1), 0) + 1).astype(F32)
        c_pin, c_gp, c_ch, c_cb, c_cc, c_gc = 1024, 1536, 2048, 2560, 3072, 3584

        for g in range(4):
            w = float(POOL_WINDOWS[g])
            lanes = slice(128 * g, 128 * g + 128)
            pooled = _pooled(xp, g, t1, tm)
            pb = pooled.astype(BF16)
            wp = wp_ref[g].astype(BF16)
            z = jnp.dot(pb, wp, preferred_element_type=F32)
            psl = ps_ref[:, lanes]
            sg, dsg = _silu_and_grad(gp_ref[:, lanes])
            dmp = dm_ref[:, 1024 + 128 * g:1024 + 128 * g + 128]
            dyp = dmp * sg
            d_ref[:, c_gp + 128 * g:c_gp + 128 * g + 128] = (dmp * (z * psl) * dsg).astype(BF16)
            dps_ref[:, lanes] += jnp.sum(dyp * z, axis=0, keepdims=True)
            dz = (dyp * psl).astype(BF16)
            dwp_ref[g] += lax.dot_general(pb, dz, TN, preferred_element_type=F32)
            dpl = lax.dot_general(dz, wp, NT, preferred_element_type=F32)
            ee[0:tm, lanes] = dpl / jnp.minimum(t1, w)
            gpn = gpn_ref[:, lanes]
            dzn = (dmn_ref[:, lanes] * (gpn * _sigmoid(gpn)) * psl).astype(BF16)
            dpn = lax.dot_general(dzn, wp, NT, preferred_element_type=F32)
            ee[tm:tm + HALO, lanes] = jnp.where(last, 0.0, dpn / jnp.minimum(t1n, w))
            acc = ee[0:tm, lanes]
            for k in range(1, POOL_WINDOWS[g]):
                acc = acc + ee[k:k + tm, lanes]
            d_ref[:, c_pin + 128 * g:c_pin + 128 * g + 128] = (acc - dpl).astype(BF16)

        yc = _conv_fwd(xu, cw_ref, tm)
        sgc, dsgc = _silu_and_grad(gc_ref[...])
        cb = cb_ref[...]
        dmc = dm_ref[:, 1536:2048]
        d_ref[:, c_gc:c_gc + 512] = (dmc * cb * yc * dsgc).astype(BF16)
        d_ref[:, c_cb:c_cb + 512] = (dmc * yc * sgc).astype(BF16)
        dyc = dmc * cb * sgc
        ed[0:tm, :] = dyc
        gcn = gcn_ref[...]
        ed[tm:tm + HALO, :] = jnp.where(last, 0.0, dmn_ref[:, 512:1024] * cbn_ref[...] * (gcn * _sigmoid(gcn)))
        dcw_ref[0:1, :] += jnp.sum(dyc * xu[HALO - 2:HALO - 2 + tm, :], axis=0, keepdims=True)
        dcw_ref[1:2, :] += jnp.sum(dyc * xu[HALO - 1:HALO - 1 + tm, :], axis=0, keepdims=True)
        dcw_ref[2:3, :] += jnp.sum(dyc * xu[HALO:HALO + tm, :], axis=0, keepdims=True)
        du = cw_ref[2:3, :] * dyc + cw_ref[1:2, :] * ed[1:1 + tm, :] + cw_ref[0:1, :] * ed[2:2 + tm, :]
        d_ref[:, c_cc:c_cc + 512] = (du * ch_ref[...]).astype(BF16)
        d_ref[:, c_ch:c_ch + 512] = (du * cc_ref[...]).astype(BF16)

        sgm, dsgm = _silu_and_grad(gm_ref[...])
        dmm = dm_ref[:, 0:1024]
        do_ref[...] = dmm * sgm
        d_ref[:, 0:1024] = (dmm * o_ref[...] * dsgm).astype(BF16)

    outs = (jax.ShapeDtypeStruct((S, W_MIX), BF16), jax.ShapeDtypeStruct((S, 1024), F32),
            jax.ShapeDtypeStruct((4, 128, 128), F32), jax.ShapeDtypeStruct((1, 512), F32),
            jax.ShapeDtypeStruct((3, 512), F32))
    scr = [pltpu.VMEM((tm + HALO, 512), F32) for _ in range(4)]
    return _pcall(kern, name=name, out_shape=outs, grid=(n,),
                  in_specs=[main(2048, 0), nxt(1024, 1),
                            main(1024, 1), main(512, 4), main(512, 5), main(512, 6), main(512, 7), main(512, 8),
                            main(512, 9), prev(512, 4), prev(512, 6), prev(512, 8),
                            nxt(512, 5), nxt(512, 7), nxt(512, 9),
                            main(1024, 0), full((4, 128, 128)), full((1, 512)), full((3, 512))],
                  out_specs=(main(W_MIX, 0), main(1024, 0), full((4, 128, 128)), full((1, 512)), full((3, 512))),
                  scratch=scr, dims=("arbitrary",), vmem_mb=56)(
                      dmix, dmix, proj, proj, proj, proj, proj, proj, proj, proj, proj, proj, proj, proj, proj,
                      o, wpool, ps.reshape(1, 512), convw)


def _outproj_residual(mix, wout, h, bout, *, name):
    S, Dm = h.shape
    tm = min(512, S)

    def kern(mix_ref, w_ref, h_ref, bo_ref, r_ref):
        out = jnp.dot(mix_ref[...], w_ref[...], preferred_element_type=F32) + bo_ref[...]
        r_ref[...] = ALPHA * h_ref[...] + out

    row = pl.BlockSpec((tm, Dm), lambda i: (i, 0))
    vec = pl.BlockSpec((1, Dm), lambda i: (0, 0))
    wsp = pl.BlockSpec((Dm, Dm), lambda i: (0, 0), pipeline_mode=pl.Buffered(1))
    return _pcall(kern, name=name, out_shape=jax.ShapeDtypeStruct((S, Dm), F32), grid=(S // tm,),
                  in_specs=[row, wsp, row, vec], out_specs=row, dims=("parallel",), vmem_mb=56)(
                      mix, wout, h, bout.reshape(1, Dm))


def _outproj_ln(mix, wout, h, bout, g, b, *, name):
    S, Dm = h.shape
    tm = min(512, S)

    def kern(mix_ref, w_ref, h_ref, bo_ref, g_ref, b_ref, y_ref, yb_ref, r_ref):
        out = jnp.dot(mix_ref[...], w_ref[...], preferred_element_type=F32) + bo_ref[...]
        r = ALPHA * h_ref[...] + out
        r_ref[...] = r
        mu = jnp.mean(r, axis=-1, keepdims=True)
        xc = r - mu
        var = jnp.mean(xc * xc, axis=-1, keepdims=True)
        y = xc * lax.rsqrt(var + LN_EPS) * g_ref[...] + b_ref[...]
        y_ref[...] = y
        yb_ref[...] = y.astype(BF16)

    row = pl.BlockSpec((tm, Dm), lambda i: (i, 0))
    vec = pl.BlockSpec((1, Dm), lambda i: (0, 0))
    wsp = pl.BlockSpec((Dm, Dm), lambda i: (0, 0), pipeline_mode=pl.Buffered(1))
    sds = jax.ShapeDtypeStruct((S, Dm), F32)
    return _pcall(kern, name=name, out_shape=(sds, jax.ShapeDtypeStruct((S, Dm), BF16), sds), grid=(S // tm,),
                  in_specs=[row, wsp, row, vec, vec, vec], out_specs=(row, row, row), dims=("parallel",),
                  vmem_mb=56)(
                      mix, wout, h, bout.reshape(1, Dm), g.reshape(1, Dm), b.reshape(1, Dm))


def _adamw_math(w, g, m, v):
    m = ADAM_B1 * m + (1.0 - ADAM_B1) * g
    v = ADAM_B2 * v + (1.0 - ADAM_B2) * (g * g)
    m_hat = m / (1.0 - ADAM_B1 ** ADAM_STEP)
    v_hat = v / (1.0 - ADAM_B2 ** ADAM_STEP)
    delta = -ADAM_LR * (m_hat / (jnp.sqrt(v_hat) + ADAM_EPS) + ADAM_WD * w)
    return delta, m, v


def _row_tile(R, C):
    best = None
    for cand in range(8, R, 8):
        if R % cand == 0 and cand * C <= 256 * 1024:
            best = cand
    return best if best is not None else R


def _adamw(w, g, m, v, *, name):
    shape = w.shape
    C = shape[-1]
    R = 1
    for s in shape[:-1]:
        R *= s
    tr = _row_tile(R, C)

    def kern(w_ref, g_ref, m_ref, v_ref, d_ref, mo_ref, vo_ref):
        d, mn, vn = _adamw_math(w_ref[...], g_ref[...], m_ref[...], v_ref[...])
        d_ref[...] = d
        mo_ref[...] = mn
        vo_ref[...] = vn

    blk = pl.BlockSpec((tr, C), lambda i: (i, 0))
    sds = jax.ShapeDtypeStruct((R, C), F32)
    outs = _pcall(kern, name=name, out_shape=(sds, sds, sds), grid=(R // tr,), in_specs=[blk] * 4,
                  out_specs=(blk, blk, blk), dims=("parallel",), vmem_mb=48)(
                      w.reshape(R, C), g.reshape(R, C), m.reshape(R, C), v.reshape(R, C))
    return tuple(t.reshape(shape) for t in outs)


def _adamw_halves(w, m, v, halves, c_idx, *, name, comm=None):
    _, R, C = w.shape
    ch = C // 2
    tr = _row_tile(R, ch)
    nb = R // tr

    def kern(c_ref, w_ref, a0_ref, b0_ref, a1_ref, b1_ref, m_ref, v_ref, g_ref, d_ref, mo_ref, vo_ref):
        layer = pl.program_id(0) // nb
        mine = pl.program_id(1) == c_ref[0]
        g = jnp.where(layer == 0, jnp.where(mine, a0_ref[...], b0_ref[...]),
                      jnp.where(mine, a1_ref[...], b1_ref[...]))
        g_ref[...] = g
        d, mn, vn = _adamw_math(w_ref[...], g, m_ref[...], v_ref[...])
        d_ref[...] = d
        mo_ref[...] = mn
        vo_ref[...] = vn

    full = pl.BlockSpec((tr, ch), lambda i, hc: (i, hc))
    half = pl.BlockSpec((tr, ch), lambda i, hc: (i % nb, 0))
    sds = jax.ShapeDtypeStruct((2 * R, C), F32)
    (a0, b0), (a1, b1) = halves
    res = _pcall(kern, name=name, out_shape=(sds,) * 4, grid=(2 * nb, 2),
                 in_specs=[pl.BlockSpec(memory_space=pltpu.SMEM), full, half, half, half, half, full, full],
                 out_specs=(full,) * 4, dims=("parallel", "parallel"), vmem_mb=48, comm=comm)(
                     c_idx, w.reshape(2 * R, C), a0, b0, a1, b1, m.reshape(2 * R, C), v.reshape(2 * R, C))
    outs, landed = res if comm is not None else (res, None)
    outs = tuple(t.reshape(2, R, C) for t in outs)
    return outs if comm is None else (outs, landed)


def _packed_pieces(shape):
    if len(shape) == 4:
        return [((l * shape[1] + g) * 128, 128, (l, g)) for l in range(shape[0]) for g in range(shape[1])]
    per_row = shape[1] // LANE
    return [(a * per_row + j, 1, (slice(a, a + 1), slice(LANE * j, LANE * (j + 1))))
            for a in range(shape[0]) for j in range(per_row)]


def _small_sum_adamw(gathered, own, weights, *, name):
    R = gathered.shape[1]
    nw = len(weights)
    shapes = [w.shape for w, _, _ in weights]
    first_row, r0 = [], 0
    for shp in shapes:
        first_row.append(r0)
        n = 1
        for s in shp:
            n *= s
        r0 += n // LANE

    def kern(ga_ref, own_ref, *refs):
        ins, gsum_ref, outs = refs[:3 * nw], refs[3 * nw], refs[3 * nw + 1:]
        me = 4 * lax.axis_index("x") + 2 * lax.axis_index("y") + lax.axis_index("c")

        def block(k):
            other = ga_ref[jnp.where(me == k, (k + 1) % N_DEV, k)]
            return jnp.where(me == k, own_ref[...], other)

        g = block(0)
        for k in range(1, N_DEV):
            g = g + block(k)
        gsum_ref[...] = g
        for p, shp in enumerate(shapes):
            w_ref, m_ref, v_ref = ins[3 * p:3 * p + 3]
            g_out, d_out, m_out, v_out = outs[4 * p:4 * p + 4]
            for row, rows, idx in _packed_pieces(shp):
                gp = gsum_ref[first_row[p] + row:first_row[p] + row + rows, :]
                d, mn, vn = _adamw_math(w_ref[idx], gp, m_ref[idx], v_ref[idx])
                g_out[idx] = gp
                d_out[idx] = d
                m_out[idx] = mn
                v_out[idx] = vn

    out_shape = [jax.ShapeDtypeStruct((R, LANE), F32)]
    for shp in shapes:
        out_shape += [jax.ShapeDtypeStruct(shp, F32)] * 4
    flat = [a for wmv in weights for a in wmv]
    res = _pcall(kern, name=name, out_shape=tuple(out_shape), vmem_mb=48)(gathered, own, *flat)
    return res[0], [tuple(res[1 + 4 * p:5 + 4 * p]) for p in range(nw)]


def _pair_sum(g, theirs, c_idx, *, name):
    R, C = g.shape
    ch = C // 2
    tr = _row_tile(R, ch)

    def kern(c_ref, a_ref, b_ref, o_ref):
        o_ref[...] = (a_ref[...] + b_ref[...]).astype(BF16)

    gs = pltpu.PrefetchScalarGridSpec(
        num_scalar_prefetch=1, grid=(R // tr,),
        in_specs=[pl.BlockSpec((tr, ch), lambda i, c: (i, c[0])), pl.BlockSpec((tr, ch), lambda i, c: (i, 0))],
        out_specs=pl.BlockSpec((tr, ch), lambda i, c: (i, 0)))
    return pl.pallas_call(kern, name=name, out_shape=jax.ShapeDtypeStruct((R, ch), BF16), grid_spec=gs,
                          compiler_params=pltpu.CompilerParams(dimension_semantics=("parallel",),
                                                               vmem_limit_bytes=48 << 20))(c_idx, g, theirs)


WeightRows = collections.namedtuple("WeightRows", "full_rows own_rows cols pieces zero_rows")


def _w_in_piece_a(j):
    return jnp.where(j == 0, 0, 1232 * j + GAP)


def _w_in_piece_b(j):
    return jnp.where(j == 0, GAP_AT + GAP, 1232 * j + GAP_AT + GAP)


W_IN = WeightRows(NP, 1232, D_MODEL, ((0, GAP_AT, _w_in_piece_a), (GAP_AT, 1232 - GAP_AT, _w_in_piece_b)),
                  ((GAP_AT, GAP),))
W_OUT = WeightRows(2048, 512, D_MODEL, ((0, 512, lambda j: 512 * j),), ())
W_UQ = WeightRows(2048, 384, Q_LORA, ((0, 192, lambda j: 512 * j), (192, 192, lambda j: 512 * j + 256)),
                  tuple((256 * h + 192, 64) for h in range(N_HEADS)))
W_UKV = WeightRows(2048, 512, KV_LORA, ((0, 512, lambda j: 512 * j),), ())
W_CONV = WeightRows(64, 16, 256, ((0, 16, lambda j: 16 * j),), ())
SHARDED = (W_IN, W_OUT, W_UQ, W_UKV)
SHARDED_NAMES = ("w_in", "w_out", "w_uq", "w_ukv")
WEIGHT_ROWS = dict(zip(SHARDED_NAMES, SHARDED))


def _mesh_pos():
    x, y, c = lax.axis_index("x"), lax.axis_index("y"), lax.axis_index("c")
    return x, y, c


def _other_chips(x, y):
    return [(1 - x, y), (x, 1 - y), (1 - x, 1 - y)]


def _rows(start, n):
    return pl.ds(pl.multiple_of(start, 16), n)


def _half_cols(spec, c):
    ch = spec.cols // 2
    return pl.ds(pl.multiple_of(c * ch, LANE), ch)


def _allgather_script(specs, shards, zeros, layers):
    na = len(specs)
    zlist = [a for a in range(na) if zeros[a] is not None]
    n_layers = [shards[a].shape[0] if layers[a] is None else 1 for a in range(na)]
    plan_first, plan_own, plan_zero = [], [], []
    for a, spec in enumerate(specs):
        for p in range(len(spec.pieces)):
            plan_own.append((a, p))
            for k in range(3):
                plan_first.append((a, p, k))
        for z in range(len(spec.zero_rows)):
            for l in range(n_layers[a]):
                plan_zero.append((a, z, l))
    nf = len(plan_first)
    n_sems = 2 * nf + len(plan_own) + len(plan_zero)

    def copies(ins_all, outs, send_sems, recv_sems):
        ins = [ins_all[a] if layers[a] is None else ins_all[a].at[pl.ds(layers[a], 1)] for a in range(na)]
        zrefs = dict(zip(zlist, ins_all[na:]))
        x, y, c = _mesh_pos()
        j = 2 * x + y
        chips = _other_chips(x, y)
        sibling = (x, y, 1 - c)

        def remote(src, dst, sem, to):
            return pltpu.make_async_remote_copy(src_ref=src, dst_ref=dst, send_sem=send_sems.at[sem],
                                                recv_sem=recv_sems.at[sem], device_id=to, device_id_type=MESH)

        def block(a, p, chip, cols):
            _, n, dst = specs[a].pieces[p]
            return outs[a].at[:, _rows(dst(chip), n), cols]

        def first(i):
            a, p, k = plan_first[i]
            src0, n, _ = specs[a].pieces[p]
            cols = _half_cols(specs[a], c)
            return remote(ins[a].at[:, pl.ds(src0, n), cols], block(a, p, j, cols), i, (*chips[k], c))

        def landed(i, half):
            a, p, k = plan_first[i]
            return block(a, p, 2 * chips[k][0] + chips[k][1], _half_cols(specs[a], half))

        def arrival(i, half, sem):
            return remote(landed(i, half), landed(i, half), sem, sibling)

        def passed(i):
            return remote(landed(i, c), landed(i, c), nf + i, sibling)

        def own(i):
            a, p = plan_own[i]
            src0, n, _ = specs[a].pieces[p]
            return remote(ins[a].at[:, pl.ds(src0, n), :], block(a, p, j, slice(None)), 2 * nf + i, sibling)

        def zero(i):
            a, z, l = plan_zero[i]
            r0, n = specs[a].zero_rows[z]
            return remote(zrefs[a].at[pl.ds(0, n), :], outs[a].at[l, pl.ds(r0, n), :],
                          2 * nf + len(plan_own) + i, sibling)

        fixed = [own(i) for i in range(len(plan_own))] + [zero(i) for i in range(len(plan_zero))]
        return c, fixed, first, arrival, passed

    def start(ins, outs, send_sems, recv_sems):
        _, fixed, first, _, _ = copies(ins, outs, send_sems, recv_sems)
        for cp in fixed:
            cp.start()
        for i in range(nf):
            first(i).start()

    def finish(ins, outs, send_sems, recv_sems):
        c, fixed, first, arrival, passed = copies(ins, outs, send_sems, recv_sems)
        for i in range(nf):
            arrival(i, c, i).wait_recv()
            passed(i).start()
        for i in range(nf):
            arrival(i, 1 - c, nf + i).wait_recv()
        for cp in fixed:
            cp.wait()
        for i in range(nf):
            first(i).wait_send()
            passed(i).wait_send()

    out_shape = tuple(jax.ShapeDtypeStruct((n_layers[a], spec.full_rows, spec.cols), BF16)
                      for a, spec in enumerate(specs))
    args = tuple(shards) + tuple(zeros[a] for a in zlist)
    return CommScript(args, out_shape, n_sems, start, finish)


def _start_all_wait_all(args, out_shape, n_sems, make_copies):
    def start(ins, outs, send_sems, recv_sems):
        for cp in make_copies(ins, outs, send_sems, recv_sems):
            cp.start()

    def finish(ins, outs, send_sems, recv_sems):
        for cp in make_copies(ins, outs, send_sems, recv_sems):
            cp.wait()

    return CommScript(tuple(args), tuple(out_shape), n_sems, start, finish)


def _exchange_script(specs, grads):
    na = len(grads)

    def make_copies(ins, outs, send_sems, recv_sems):
        x, y, c = _mesh_pos()
        return [pltpu.make_async_remote_copy(
            src_ref=ins[a].at[:, _half_cols(specs[a], 1 - c)], dst_ref=outs[a], send_sem=send_sems.at[a],
            recv_sem=recv_sems.at[a], device_id=(x, y, 1 - c), device_id_type=MESH) for a in range(na)]

    out_shape = [jax.ShapeDtypeStruct((s.full_rows, s.cols // 2), F32) for s in specs]
    return _start_all_wait_all(grads, out_shape, na, make_copies)


def _scatter_script(specs, parts):
    na = len(parts)
    plan = [(a, p, k) for a in range(na) for p in range(len(specs[a].pieces)) for k in range(3)]

    def make_copies(ins, outs, send_sems, recv_sems):
        x, y, c = _mesh_pos()
        chips = _other_chips(x, y)
        copies = []
        for i, (a, p, k) in enumerate(plan):
            src0, n, dst = specs[a].pieces[p]
            pk = 2 * chips[k][0] + chips[k][1]
            copies.append(pltpu.make_async_remote_copy(
                src_ref=ins[a].at[_rows(dst(pk), n), :], dst_ref=outs[a].at[k, pl.ds(src0, n), :],
                send_sem=send_sems.at[i], recv_sem=recv_sems.at[i], device_id=(*chips[k], c), device_id_type=MESH))
        return copies

    out_shape = [jax.ShapeDtypeStruct((3, s.own_rows, s.cols // 2), BF16) for s in specs]
    return _start_all_wait_all(parts, out_shape, len(plan), make_copies)


def _chip_sum(spec, part, recv, *, name):
    ch = spec.cols // 2
    npieces = len(spec.pieces)

    def kern(recv_ref, part_ref, o_ref, own_ref, sems):
        j = 2 * lax.axis_index("x") + lax.axis_index("y")
        copies = []
        for p, (src0, n, dst) in enumerate(spec.pieces):
            copies.append(pltpu.make_async_copy(part_ref.at[_rows(dst(j), n), :], own_ref.at[pl.ds(src0, n), :],
                                                sems.at[p]))
        for cp in copies:
            cp.start()
        for cp in copies:
            cp.wait()
        o_ref[...] = ((own_ref[...].astype(F32) + recv_ref[0].astype(F32)) + recv_ref[1].astype(F32)) \
            + recv_ref[2].astype(F32)

    vm = pl.BlockSpec(memory_space=pltpu.VMEM)
    return _pcall(kern, name=name, out_shape=jax.ShapeDtypeStruct((spec.own_rows, ch), F32),
                  in_specs=[vm, HBM_SPEC], out_specs=vm,
                  scratch=[pltpu.VMEM((spec.own_rows, ch), BF16), pltpu.SemaphoreType.DMA((npieces,))],
                  vmem_mb=48)(recv, part)


def _sibling_script(sums):
    na = len(sums)

    def make_copies(ins, outs, send_sems, recv_sems):
        x, y, c = _mesh_pos()
        return [pltpu.make_async_remote_copy(
            src_ref=ins[a], dst_ref=outs[a], send_sem=send_sems.at[a], recv_sem=recv_sems.at[a],
            device_id=(x, y, 1 - c), device_id_type=MESH) for a in range(na)]

    out_shape = [jax.ShapeDtypeStruct(t.shape, t.dtype) for t in sums]
    return _start_all_wait_all(sums, out_shape, na, make_copies)


class _SemWindow:
    def __init__(self, sems, offset):
        self._sems, self._offset = sems, offset

    @property
    def at(self):
        return self

    def __getitem__(self, i):
        return self._sems.at[i + self._offset]


def _merge_scripts(*scripts):
    a_off, o_off, s_off = [0], [0], [0]
    for s in scripts:
        a_off.append(a_off[-1] + len(s.args))
        o_off.append(o_off[-1] + len(s.out_shape))
        s_off.append(s_off[-1] + s.n_sems)

    def phase(which):
        def run(ins, outs, send_sems, recv_sems):
            for n, s in enumerate(scripts):
                getattr(s, which)(ins[a_off[n]:a_off[n + 1]], outs[o_off[n]:o_off[n + 1]],
                                  _SemWindow(send_sems, s_off[n]), _SemWindow(recv_sems, s_off[n]))
        return run

    return CommScript(sum((tuple(s.args) for s in scripts), ()), sum((tuple(s.out_shape) for s in scripts), ()),
                      s_off[-1], phase("start"), phase("finish"))


class _GradReducer:
    def __init__(self, layer, names, grads, c_idx):
        self.specs = tuple(WEIGHT_ROWS[nm] for nm in names)
        self.grads, self.c_idx = tuple(grads), c_idx
        self.names = [f"{nm}{layer}" for nm in names]

    def exchange(self):
        return _exchange_script(self.specs, self.grads)

    def scatter(self, theirs):
        self.parts = tuple(_pair_sum(g, th, self.c_idx, name=f"pair_sum_{nm}")
                           for g, th, nm in zip(self.grads, theirs, self.names))
        return _scatter_script(self.specs, self.parts)

    def sibling(self, recv):
        self.sums = tuple(_chip_sum(s, p, r, name=f"chip_sum_{nm}")
                          for s, p, r, nm in zip(self.specs, self.parts, recv, self.names))
        return _sibling_script(self.sums)

    def done(self, others):
        return list(zip(self.sums, others))


def _allgather_small_script(block):
    m_per, n = block.shape

    def copies(ins, outs, send_sems, recv_sems):
        (x_ref,), (out_ref,) = ins, outs
        x, y, c = _mesh_pos()
        me, sibling = (x, y, c), (x, y, 1 - c)
        chips = _other_chips(x, y)

        def rows(px, py, pc):
            return out_ref.at[4 * px + 2 * py + pc]

        def copy(k, blk, to, src=None):
            return pltpu.make_async_remote_copy(
                src_ref=rows(*blk) if src is None else src, dst_ref=rows(*blk), send_sem=send_sems.at[k],
                recv_sem=recv_sems.at[k], device_id=to, device_id_type=MESH)

        first = [copy(0, me, sibling, src=x_ref)]
        first += [copy(1 + k, me, (*chip, c), src=x_ref) for k, chip in enumerate(chips)]
        passed = [copy(4 + k, (*chip, c), sibling) for k, chip in enumerate(chips)]
        landed = [copy(1 + k, (*chip, c), me) for k, chip in enumerate(chips)]
        from_sibling = [copy(0, sibling, me)] + [copy(4 + k, (*chip, 1 - c), me) for k, chip in enumerate(chips)]
        return first, passed, landed, from_sibling

    def start(ins, outs, send_sems, recv_sems):
        first, _, _, _ = copies(ins, outs, send_sems, recv_sems)
        for cp in first:
            cp.start()

    def finish(ins, outs, send_sems, recv_sems):
        first, passed, landed, from_sibling = copies(ins, outs, send_sems, recv_sems)
        for k in range(3):
            landed[k].wait_recv()
            passed[k].start()
        for cp in from_sibling:
            cp.wait_recv()
        for cp in first + passed:
            cp.wait_send()

    return CommScript((block,), (jax.ShapeDtypeStruct((N_DEV, m_per, n), block.dtype),), 7, start, finish)


def _rope_tables(positions):
    half = ROPE // 2
    inv_freq = ROPE_THETA ** (-jnp.arange(half, dtype=F32) / half)
    ang = positions.astype(F32)[:, None] * inv_freq
    cos, sin = jnp.cos(ang), jnp.sin(ang)
    S = positions.shape[0]
    cos_t = jnp.concatenate([cos, cos, jnp.ones((S, 64), F32)], axis=1)
    sin_t = jnp.concatenate([-sin, sin, jnp.zeros((S, 64), F32)], axis=1)
    return cos_t, sin_t


def _decode_conv(bits):
    rows = bits.reshape(DEPTH, N_CHIPS, 16, 256)[:, :, :3, :]
    conv = lax.bitcast_convert_type(rows.reshape(DEPTH, N_CHIPS, 3, 128, 2), F32)
    return jnp.transpose(conv, (0, 2, 1, 3)).reshape(DEPTH, 3, 512)


def _local_step(x, positions, target, emb_g, emb_b, w_in_t0, rest0, weights1, q_g, kv_g, w_pool, pool_scale,
                b_out, ln_g, ln_b, c_idx=None):
    cos_t, sin_t = _rope_tables(positions)
    if isinstance(w_in_t0, CommScript):
        (h, hb), (landed,) = _ln_fwd(x, emb_g, emb_b, name="emb_ln", comm=w_in_t0)
        w_in_t0 = landed[0]
    else:
        h, hb = _ln_fwd(x, emb_g, emb_b, name="emb_ln")
    weights = [None, weights1]
    saved = []
    for l in range(DEPTH):
        if l == 0 and isinstance(rest0, CommScript):
            proj, landed = _matmul(hb, w_in_t0, "nt", name="in_proj0", tm=1024, tn=1024, tk=2048, vmem_mb=56,
                                   comm=rest0)
            weights[0] = (w_in_t0,) + tuple(a[0] for a in landed[:3])
            conv_w = _decode_conv(landed[3])
        else:
            if l == 0:
                weights[0] = (w_in_t0,) + tuple(rest0[:3])
                conv_w = rest0[3]
            proj = _matmul(hb, weights[l][0], "nt", name=f"in_proj{l}", tm=1024, tn=1024, tk=2048, vmem_mb=56)
        w_in_t, w_out, w_uq_t, w_ukv_t = weights[l]
        qc, kc, v, vt, qn, kvn = _mla_qkv(proj, cos_t, sin_t, q_g[l], kv_g[l], w_uq_t, w_ukv_t, name=f"mla_qkv{l}")
        nxt = weights[l + 1] if l + 1 < DEPTH else None
        if isinstance(nxt, CommScript):
            (o, lse2), landed = _flash_fwd(qc, kc, vt, name=f"flash_fwd{l}", comm=nxt)
            weights[l + 1] = tuple(a[0] for a in landed)
        else:
            o, lse2 = _flash_fwd(qc, kc, vt, name=f"flash_fwd{l}")
        mix = _mixer_fwd(proj, o, w_pool[l], pool_scale[l], conv_w[l], name=f"mixer_fwd{l}")
        if l == DEPTH - 1:
            r = _outproj_residual(mix, w_out, h, b_out[l], name=f"out_proj{l}")
            saved.append((hb, proj, qc, kc, v, qn, kvn, o, lse2, mix, r))
        else:
            h_next, hb_next, r = _outproj_ln(mix, w_out, h, b_out[l], ln_g[l], ln_b[l], name=f"out_proj_ln{l}")
            saved.append((hb, proj, qc, kc, v, qn, kvn, o, lse2, mix, r))
            h, hb = h_next, hb_next

    small = [None] * DEPTH
    big = [None] * DEPTH
    above = scatter_above = None
    for l in reversed(range(DEPTH)):
        w_in_t, w_out, w_uq_t, w_ukv_t = weights[l]
        hb_in, proj, qc, kc, v, qn, kvn, o, lse2, mix, r = saved[l]
        if l == DEPTH - 1:
            loss_acc, dr, drb, d_ln_g, d_ln_b, d_b_out = _loss_ln_bwd(target, r, ln_g[l], ln_b[l], name="loss_ln_bwd")
        else:
            dr, drb, d_ln_g, d_ln_b, d_b_out = _ln_bwd(dh, r, ln_g[l], name=f"ln_bwd{l}")
        dmix = _matmul(drb, w_out, "nt", name=f"dmix{l}", tm=1024, tn=1024, tk=2048, vmem_mb=56)
        d_w_out = _matmul(mix, drb, "tn", name=f"dw_out{l}", tm=1024, tn=1024, tk=2048, vmem_mb=56)
        d_mix, do, d_w_pool, d_ps, d_conv = _mixer_bwd(dmix, proj, o, w_pool[l], pool_scale[l], conv_w[l],
                                                       name=f"mixer_bwd{l}")
        delta = _attn_delta(o, do, name=f"attn_delta{l}")
        if above is not None:
            (dqb, dkvb, dkr), recv = _flash_bwd(qc, kc, v, do, lse2, delta, cos_t, sin_t, name=f"flash_bwd{l}",
                                                comm=scatter_above)
            sibling_above = above.sibling(recv)
        else:
            dqb, dkvb, dkr = _flash_bwd(qc, kc, v, do, lse2, delta, cos_t, sin_t, name=f"flash_bwd{l}")
        d_mla, d_qg, d_kvg = _mla_qkv_bwd(dqb, dkvb, dkr, proj, cos_t, sin_t, q_g[l], kv_g[l], w_uq_t, w_ukv_t,
                                          name=f"mla_qkv_bwd{l}")
        d_w_uq_t = _matmul(dqb, qn, "tn", name=f"dw_uq{l}", tm=2048, tn=512, tk=2048, vmem_mb=56)
        d_w_ukv_t = _matmul(dkvb, kvn, "tn", name=f"dw_ukv{l}", tm=2048, tn=256, tk=2048, vmem_mb=56)
        small[l] = dict(q_g=d_qg[0], kv_g=d_kvg[0], w_pool=d_w_pool, pool_scale=d_ps[0], conv_w=d_conv,
                        b_out=d_b_out[0], ln_g=d_ln_g[0], ln_b=d_ln_b[0])
        rest = (d_w_out, d_w_uq_t, d_w_ukv_t)
        if c_idx is None:
            d_w_in_t = _dproj_t_times_h(d_mla, d_mix, hb_in, name=f"dw_in{l}")
            dh = _dproj_times_w(d_mla, d_mix, w_in_t, dr, ALPHA, name=f"dh{l}")
            big[l] = (d_w_in_t,) + rest
        elif l > 0:
            d_w_in_t = _dproj_t_times_h(d_mla, d_mix, hb_in, name=f"dw_in{l}")
            above = _GradReducer(l, SHARDED_NAMES, (d_w_in_t,) + rest, c_idx)
            dh, theirs = _dproj_times_w(d_mla, d_mix, w_in_t, dr, ALPHA, name=f"dh{l}", comm=above.exchange())
            scatter_above = above.scatter(theirs)
        else:
            red_rest = _GradReducer(l, SHARDED_NAMES[1:], rest, c_idx)
            d_w_in_t, landed = _dproj_t_times_h(d_mla, d_mix, hb_in, name=f"dw_in{l}",
                                                comm=_merge_scripts(sibling_above, red_rest.exchange()))
            big[l + 1] = above.done(landed[:len(SHARDED)])
            red_in = _GradReducer(l, SHARDED_NAMES[:1], (d_w_in_t,), c_idx)
            landed = _run_comm(_merge_scripts(red_in.exchange(), red_rest.scatter(landed[len(SHARDED):])),
                               name="exchange_w_in0")
            sibling_rest = red_rest.sibling(landed[1:])
            dh, landed = _dproj_times_w(d_mla, d_mix, w_in_t, dr, ALPHA, name=f"dh{l}",
                                        comm=_merge_scripts(red_in.scatter(landed[:1]), sibling_rest))
            recv_in, others_rest = landed[:1], landed[1:]
    grad_x, _, d_emb_g, d_emb_b, _ = _ln_bwd(dh, x, emb_g, name="emb_ln_bwd", bf16_copy=False)
    if c_idx is not None:
        others_in = _run_comm(red_in.sibling(recv_in), name="send_to_sibling0")
        big[0] = red_in.done(others_in) + red_rest.done(others_rest)
    return loss_acc[0, 0], grad_x, d_emb_g, d_emb_b, small, big


SMALL_ORDER = ("emb_ln_g", "emb_ln_b", "q_norm_g", "kv_norm_g", "w_pool", "pool_scale", "b_out", "ln_g", "ln_b")
SMALL_LAYER_KEYS = ("q_g", "kv_g", "w_pool", "pool_scale", "b_out", "ln_g", "ln_b", "conv_w")


def _pack_small(arrs, extra_rows):
    flat = jnp.concatenate([a.reshape(-1) for a in arrs])
    rows = flat.shape[0] // LANE
    total = -(-(rows + extra_rows) // 8) * 8
    return jnp.pad(flat, (0, total * LANE - flat.shape[0])).reshape(total, LANE)


def kernel(x, positions, emb_ln_g, emb_ln_b, w_in, q_norm_g, kv_norm_g, w_uq, w_ukv, w_pool, pool_scale, conv_w, w_out, b_out, ln_g, ln_b, loss_target, m_emb_ln_g, m_emb_ln_b, m_w_in, m_q_norm_g, m_kv_norm_g, m_w_uq, m_w_ukv, m_w_pool, m_pool_scale, m_conv_w, m_w_out, m_b_out, m_ln_g, m_ln_b, v_emb_ln_g, v_emb_ln_b, v_w_in, v_q_norm_g, v_kv_norm_g, v_w_uq, v_w_ukv, v_w_pool, v_pool_scale, v_conv_w, v_w_out, v_b_out, v_ln_g, v_ln_b):
    xi, yi, ci = lax.axis_index("x"), lax.axis_index("y"), lax.axis_index("c")
    chip = 2 * xi + yi
    c_idx = ci.reshape(1).astype(jnp.int32)

    def t(a):
        return jnp.swapaxes(a, 1, 2)

    conv_bits = lax.bitcast_convert_type(conv_w.reshape(DEPTH, 3 * 128), BF16).reshape(DEPTH, 3, 256)
    conv_bits = jnp.pad(conv_bits, ((0, 0), (0, 13), (0, 0)))
    own = (t(w_in).astype(BF16), w_out.astype(BF16), t(w_uq).astype(BF16), t(w_ukv).astype(BF16))
    zeros = (jnp.zeros((GAP, D_MODEL), BF16), None, jnp.zeros((64, Q_LORA), BF16), None)
    gather_in0 = _allgather_script((W_IN,), own[:1], zeros[:1], (0,))
    gather0 = _allgather_script(SHARDED[1:] + (W_CONV,), own[1:] + (conv_bits,), zeros[1:] + (None,),
                                (0, 0, 0, None))
    gather1 = _allgather_script(SHARDED, own, zeros, (1, 1, 1, 1))

    loss_part, grad_x, d_emb_g, d_emb_b, grads, reduced = _local_step(
        x[0], positions[0], loss_target[0], emb_ln_g, emb_ln_b, gather_in0, gather0, gather1, q_norm_g, kv_norm_g,
        w_pool, pool_scale, b_out, ln_g, ln_b, c_idx)

    def rows(a):
        return a.reshape(1, -1) if a.ndim == 1 else a

    small_wmv = [tuple(rows(a) for a in wmv) for wmv in (
        (emb_ln_g, m_emb_ln_g, v_emb_ln_g), (emb_ln_b, m_emb_ln_b, v_emb_ln_b),
        (q_norm_g, m_q_norm_g, v_q_norm_g), (kv_norm_g, m_kv_norm_g, v_kv_norm_g), (w_pool, m_w_pool, v_w_pool),
        (pool_scale, m_pool_scale, v_pool_scale), (b_out, m_b_out, v_b_out), (ln_g, m_ln_g, v_ln_g),
        (ln_b, m_ln_b, v_ln_b))]
    packed_g = _pack_small(
        [d_emb_g, d_emb_b] + [jnp.stack([grads[l][key] for l in range(DEPTH)]) for key in SMALL_LAYER_KEYS]
        + [jnp.pad(loss_part.reshape(1), (0, LANE - 1))], 0)
    (gathered,) = _run_comm(_allgather_small_script(packed_g), name="allgather_small")
    g_tot, small_upd = _small_sum_adamw(gathered, packed_g, small_wmv, name="small_sum_adamw")
    off = sum(w.size for w, _, _ in small_wmv)
    flat_tot = g_tot.reshape(-1)

    def halves(a):
        return [reduced[l][a] for l in range(DEPTH)]

    upd = {}
    upd["w_in"] = tuple(t(o) for o in _adamw_halves(t(w_in), t(m_w_in), t(v_w_in), halves(0), c_idx,
                                                    name="adamw_w_in"))
    conv_tot = flat_tot[off:off + DEPTH * 3 * 512].reshape(DEPTH, 3, 512)
    loss = flat_tot[off + DEPTH * 3 * 512]
    g_conv = lax.dynamic_slice_in_dim(conv_tot, chip * 128, 128, axis=2)

    def whole(a):
        return jnp.stack([jnp.where(ci == 0, jnp.concatenate([mine, oth], axis=1),
                                    jnp.concatenate([oth, mine], axis=1)) for mine, oth in halves(a)])

    upd["w_out"] = _adamw_halves(w_out, m_w_out, v_w_out, halves(1), c_idx, name="adamw_w_out")
    g_uq, g_ukv = t(whole(2)), t(whole(3))
    upd["w_uq"] = (g_uq,) + _adamw(w_uq, g_uq, m_w_uq, v_w_uq, name="adamw_w_uq")
    upd["w_ukv"] = (g_ukv,) + _adamw(w_ukv, g_ukv, m_w_ukv, v_w_ukv, name="adamw_w_ukv")
    upd["conv_w"] = (g_conv,) + _adamw(conv_w, g_conv, m_conv_w, v_conv_w, name="adamw_conv_w")
    for nm, res in zip(SMALL_ORDER, small_upd):
        upd[nm] = tuple(a.reshape(-1) for a in res) if nm in ("emb_ln_g", "emb_ln_b") else res

    order = ("emb_ln_g", "emb_ln_b", "w_in", "q_norm_g", "kv_norm_g", "w_uq", "w_ukv", "w_pool", "pool_scale",
             "conv_w", "w_out", "b_out", "ln_g", "ln_b")
    outs = [loss, grad_x[None]]
    for field in range(4):
        outs += [upd[nm][field] for nm in order]
    return tuple(outs)
```

```python
import collections

import jax
import jax.numpy as jnp
from jax import lax
from jax.experimental import pallas as pl
from jax.experimental.pallas import tpu as pltpu

F32 = jnp.float32
BF16 = jnp.bfloat16
MESH = pl.DeviceIdType.MESH

D_MODEL = 2048
DEPTH = 2
N_HEADS = 8
NOPE = 128
ROPE = 64
Q_LORA = 512
KV_LORA = 256
D_MLA = 1024
POOL_WINDOWS = (2, 4, 8, 16)
D_IN_PROJ = 4928
LN_EPS = 1e-5
RMS_EPS = 1e-6
ROPE_THETA = 10000.0
ALPHA = (2 * DEPTH) ** 0.25
SCALE = (NOPE + ROPE) ** -0.5
LOG2E = 1.4426950408889634
SCALE_LOG2E = SCALE * LOG2E
ADAM_LR = 0.001
ADAM_B1 = 0.9
ADAM_B2 = 0.999
ADAM_EPS = 1e-08
ADAM_WD = 0.01
ADAM_STEP = 10

NP = 5120
GAP_AT = 832
GAP = NP - D_IN_PROJ
W_MLA = 1024
W_MIX = NP - W_MLA
HALO = 16
LANE = 128
N_CHIPS = 4
N_DEV = 8
TQ = 512
FWD_GROUP = 4
BWD_GROUP = 3

NN = (((1,), (0,)), ((), ()))
NT = (((1,), (1,)), ((), ()))
TN = (((0,), (0,)), ((), ()))


CommScript = collections.namedtuple("CommScript", "args out_shape n_sems start finish")
HBM_SPEC = pl.BlockSpec(memory_space=pl.ANY)


def _pcall(kern, *, name, out_shape, grid=None, in_specs=None, out_specs=None, scratch=(), dims=None,
           vmem_mb=None, comm=None):
    cp = {}
    if dims is not None:
        cp["dimension_semantics"] = dims if comm is None else ("arbitrary",) * len(dims)
    if vmem_mb is not None:
        cp["vmem_limit_bytes"] = vmem_mb << 20
    if comm is None:
        args = dict(name=name, out_shape=out_shape, scratch_shapes=list(scratch),
                    compiler_params=pltpu.CompilerParams(**cp))
        if grid is not None:
            args["grid"] = grid
        if in_specs is not None:
            args["in_specs"] = in_specs
        if out_specs is not None:
            args["out_specs"] = out_specs
        return pl.pallas_call(kern, **args)

    single = not isinstance(out_shape, (tuple, list))
    own_out = (out_shape,) if single else tuple(out_shape)
    own_out_specs = (out_specs,) if single else tuple(out_specs)
    n_in, n_out, n_scr = len(in_specs), len(own_out), len(scratch)
    na, no = len(comm.args), len(comm.out_shape)

    def at(end):
        cond = None
        for d, n in enumerate(grid):
            here = pl.program_id(d) == (n - 1 if end else 0)
            cond = here if cond is None else jnp.logical_and(cond, here)
        return cond

    def wrapped(*refs):
        own_in, c_in = refs[:n_in], refs[n_in:n_in + na]
        o0 = n_in + na
        own_o, c_out = refs[o0:o0 + n_out], refs[o0 + n_out:o0 + n_out + no]
        s0 = o0 + n_out + no
        own_s, (send_sems, recv_sems) = refs[s0:s0 + n_scr], refs[s0 + n_scr:]

        @pl.when(at(False))
        def _():
            comm.start(c_in, c_out, send_sems, recv_sems)

        kern(*own_in, *own_o, *own_s)

        @pl.when(at(True))
        def _():
            comm.finish(c_in, c_out, send_sems, recv_sems)

    call = pl.pallas_call(
        wrapped, name=name, out_shape=own_out + tuple(comm.out_shape), grid=grid,
        in_specs=list(in_specs) + [HBM_SPEC] * na, out_specs=own_out_specs + (HBM_SPEC,) * no,
        scratch_shapes=list(scratch) + [pltpu.SemaphoreType.DMA((comm.n_sems,)),
                                        pltpu.SemaphoreType.DMA((comm.n_sems,))],
        compiler_params=pltpu.CompilerParams(**cp))

    def run(*args):
        res = call(*args, *comm.args)
        own = res[0] if single else tuple(res[:n_out])
        return own, tuple(res[n_out:])

    return run


def _run_comm(script, *, name):
    na, no = len(script.args), len(script.out_shape)

    def body(*refs):
        ins, outs = refs[:na], refs[na:na + no]
        send_sems, recv_sems = refs[na + no:]
        script.start(ins, outs, send_sems, recv_sems)
        script.finish(ins, outs, send_sems, recv_sems)

    return pl.pallas_call(
        body, name=name, out_shape=tuple(script.out_shape), in_specs=[HBM_SPEC] * na, out_specs=(HBM_SPEC,) * no,
        scratch_shapes=[pltpu.SemaphoreType.DMA((script.n_sems,)), pltpu.SemaphoreType.DMA((script.n_sems,))])(
            *script.args)


def _sigmoid(g):
    return 1.0 / (1.0 + jnp.exp(-g))


def _silu_and_grad(g):
    sig = _sigmoid(g)
    return g * sig, sig * (1.0 + g * (1.0 - sig))


def _matmul(a, b, mode, *, name, tm, tn, tk, out_dtype=F32, vmem_mb=48, comm=None):
    if mode == "nn":
        (M, K), N = a.shape, b.shape[1]
    elif mode == "nt":
        (M, K), N = a.shape, b.shape[0]
    else:
        (K, M), N = a.shape, b.shape[1]
    tm, tn, tk = min(tm, M), min(tn, N), min(tk, K)
    assert M % tm == 0 and N % tn == 0 and K % tk == 0, (name, M, N, K)
    nk = K // tk
    dn = {"nn": NN, "nt": NT, "tn": TN}[mode]
    if mode == "tn":
        a_spec = pl.BlockSpec((tk, tm), lambda i, j, k: (k, i))
    else:
        a_spec = pl.BlockSpec((tm, tk), lambda i, j, k: (i, k))
    if mode == "nt":
        b_spec = pl.BlockSpec((tn, tk), lambda i, j, k: (j, k))
    else:
        b_spec = pl.BlockSpec((tk, tn), lambda i, j, k: (k, j))
    o_spec = pl.BlockSpec((tm, tn), lambda i, j, k: (i, j))

    def kern(a_ref, b_ref, o_ref, *rest):
        part = lax.dot_general(a_ref[...].astype(BF16), b_ref[...].astype(BF16), dn,
                               preferred_element_type=F32)
        if nk == 1:
            o_ref[...] = part.astype(out_dtype)
        else:
            acc_ref = rest[0]
            k = pl.program_id(2)

            @pl.when(k == 0)
            def _():
                acc_ref[...] = part

            @pl.when(k > 0)
            def _():
                acc_ref[...] += part

            @pl.when(k == nk - 1)
            def _():
                o_ref[...] = acc_ref[...].astype(out_dtype)

    scratch = [pltpu.VMEM((tm, tn), F32)] if nk > 1 else []
    return _pcall(kern, name=name, out_shape=jax.ShapeDtypeStruct((M, N), out_dtype),
                  grid=(M // tm, N // tn, nk), in_specs=[a_spec, b_spec], out_specs=o_spec, scratch=scratch,
                  dims=("parallel", "parallel", "arbitrary"), vmem_mb=vmem_mb, comm=comm)(a, b)


def _dproj_times_w(d_mla, d_mix, wt, add, add_scale, *, name, comm=None):
    S = d_mla.shape[0]
    Dm = wt.shape[1]
    tm, tn, tk = min(1024, S), 1024, 2048
    nk = 1 + W_MIX // tk

    def kern(a1_ref, a2_ref, b1_ref, b2_ref, add_ref, o_ref, acc_ref):
        k = pl.program_id(2)

        @pl.when(k == 0)
        def _():
            acc_ref[...] = jnp.dot(a1_ref[...], b1_ref[...], preferred_element_type=F32)

        @pl.when(k > 0)
        def _():
            acc_ref[...] += jnp.dot(a2_ref[...], b2_ref[...], preferred_element_type=F32)

        @pl.when(k == nk - 1)
        def _():
            o_ref[...] = add_scale * add_ref[...] + acc_ref[...]

    o_spec = pl.BlockSpec((tm, tn), lambda i, j, k: (i, j))
    b2_spec = pl.BlockSpec((pl.Element(tk), pl.Element(tn)),
                           lambda i, j, k: (pl.multiple_of(W_MLA + tk * jnp.maximum(k - 1, 0), W_MLA),
                                            pl.multiple_of(j * tn, tn)))
    return _pcall(kern, name=name, out_shape=jax.ShapeDtypeStruct((S, Dm), F32), grid=(S // tm, Dm // tn, nk),
                  in_specs=[pl.BlockSpec((tm, W_MLA), lambda i, j, k: (i, 0)),
                            pl.BlockSpec((tm, tk), lambda i, j, k: (i, jnp.maximum(k - 1, 0))),
                            pl.BlockSpec((W_MLA, tn), lambda i, j, k: (0, j)), b2_spec, o_spec],
                  out_specs=o_spec, scratch=[pltpu.VMEM((tm, tn), F32)],
                  dims=("parallel", "parallel", "arbitrary"), vmem_mb=56, comm=comm)(d_mla, d_mix, wt, wt, add)


def _dproj_t_times_h(d_mla, d_mix, h, *, name, comm=None):
    S, Dm = h.shape
    tm, tn, tk = W_MLA, 1024, min(2048, S)
    nk = S // tk

    def kern(a1_ref, a2_ref, b_ref, o_ref, acc_ref):
        i = pl.program_id(0)
        k = pl.program_id(2)
        b = b_ref[...].astype(BF16)

        def accumulate(part):
            @pl.when(k == 0)
            def _():
                acc_ref[...] = part

            @pl.when(k > 0)
            def _():
                acc_ref[...] += part

        @pl.when(i == 0)
        def _():
            accumulate(lax.dot_general(a1_ref[...], b, TN, preferred_element_type=F32))

        @pl.when(i > 0)
        def _():
            accumulate(lax.dot_general(a2_ref[...], b, TN, preferred_element_type=F32))

        @pl.when(k == nk - 1)
        def _():
            o_ref[...] = acc_ref[...]

    return _pcall(kern, name=name, out_shape=jax.ShapeDtypeStruct((NP, Dm), F32), grid=(NP // tm, Dm // tn, nk),
                  in_specs=[pl.BlockSpec((tk, tm), lambda i, j, k: (jnp.where(i == 0, k, nk - 1), 0)),
                            pl.BlockSpec((tk, tm), lambda i, j, k: (jnp.where(i == 0, 0, k), jnp.maximum(i - 1, 0))),
                            pl.BlockSpec((tk, tn), lambda i, j, k: (k, j))],
                  out_specs=pl.BlockSpec((tm, tn), lambda i, j, k: (i, j)), scratch=[pltpu.VMEM((tm, tn), F32)],
                  dims=("parallel", "parallel", "arbitrary"), vmem_mb=48, comm=comm)(d_mla, d_mix, h)


def _ln_fwd(x, g, b, *, name, comm=None):
    S, Dm = x.shape
    tm = min(512, S)

    def kern(x_ref, g_ref, b_ref, y_ref, yb_ref):
        xf = x_ref[...]
        mu = jnp.mean(xf, axis=-1, keepdims=True)
        xc = xf - mu
        var = jnp.mean(xc * xc, axis=-1, keepdims=True)
        y = xc * lax.rsqrt(var + LN_EPS) * g_ref[...] + b_ref[...]
        y_ref[...] = y
        yb_ref[...] = y.astype(BF16)

    row = pl.BlockSpec((tm, Dm), lambda i: (i, 0))
    vec = pl.BlockSpec((1, Dm), lambda i: (0, 0))
    return _pcall(kern, name=name,
                  out_shape=(jax.ShapeDtypeStruct((S, Dm), F32), jax.ShapeDtypeStruct((S, Dm), BF16)),
                  grid=(S // tm,), in_specs=[row, vec, vec], out_specs=(row, row), dims=("parallel",), vmem_mb=48,
                  comm=comm)(
                      x, g.reshape(1, Dm), b.reshape(1, Dm))


def _ln_bwd(dy, r, g, *, name, bf16_copy=True):
    S, Dm = r.shape
    tm = min(512, S)

    def kern(dy_ref, r_ref, g_ref, dr_ref, *rest):
        drb_ref = rest[0] if bf16_copy else None
        dg_ref, db_ref, ds_ref = rest[-3:]

        @pl.when(pl.program_id(0) == 0)
        def _():
            dg_ref[...] = jnp.zeros_like(dg_ref)
            db_ref[...] = jnp.zeros_like(db_ref)
            ds_ref[...] = jnp.zeros_like(ds_ref)

        rf = r_ref[...]
        dyf = dy_ref[...]
        mu = jnp.mean(rf, axis=-1, keepdims=True)
        xc = rf - mu
        var = jnp.mean(xc * xc, axis=-1, keepdims=True)
        rstd = lax.rsqrt(var + LN_EPS)
        xhat = xc * rstd
        dxh = dyf * g_ref[...]
        c1 = jnp.mean(dxh, axis=-1, keepdims=True)
        c2 = jnp.mean(dxh * xhat, axis=-1, keepdims=True)
        dr = rstd * (dxh - c1 - xhat * c2)
        dr_ref[...] = dr
        if bf16_copy:
            drb_ref[...] = dr.astype(BF16)
        dg_ref[...] += jnp.sum(dyf * xhat, axis=0, keepdims=True)
        db_ref[...] += jnp.sum(dyf, axis=0, keepdims=True)
        ds_ref[...] += jnp.sum(dr, axis=0, keepdims=True)

    row = pl.BlockSpec((tm, Dm), lambda i: (i, 0))
    vec = pl.BlockSpec((1, Dm), lambda i: (0, 0))
    vshape = jax.ShapeDtypeStruct((1, Dm), F32)
    copies = ((jax.ShapeDtypeStruct((S, Dm), BF16),), (row,)) if bf16_copy else ((), ())
    res = _pcall(kern, name=name,
                 out_shape=(jax.ShapeDtypeStruct((S, Dm), F32),) + copies[0] + (vshape, vshape, vshape),
                 grid=(S // tm,), in_specs=[row, row, vec], out_specs=(row,) + copies[1] + (vec, vec, vec),
                 dims=("arbitrary",), vmem_mb=48)(dy, r, g.reshape(1, Dm))
    return res if bf16_copy else (res[0], None) + tuple(res[1:])


def _loss_ln_bwd(target, r, g, b, *, name):
    S, Dm = r.shape
    tm = min(512, S)

    def kern(t_ref, r_ref, g_ref, b_ref, l_ref, dr_ref, drb_ref, dg_ref, db_ref, ds_ref):
        @pl.when(pl.program_id(0) == 0)
        def _():
            l_ref[...] = jnp.zeros_like(l_ref)
            dg_ref[...] = jnp.zeros_like(dg_ref)
            db_ref[...] = jnp.zeros_like(db_ref)
            ds_ref[...] = jnp.zeros_like(ds_ref)

        rf = r_ref[...]
        mu = jnp.mean(rf, axis=-1, keepdims=True)
        xc = rf - mu
        var = jnp.mean(xc * xc, axis=-1, keepdims=True)
        rstd = lax.rsqrt(var + LN_EPS)
        xhat = xc * rstd
        e = (xhat * g_ref[...] + b_ref[...]) - t_ref[...]
        dyf = e / float(Dm)
        per_row = jnp.mean(e * e, axis=-1, keepdims=True)
        l_ref[...] += 0.5 * jnp.sum(per_row, axis=0, keepdims=True)
        dxh = dyf * g_ref[...]
        c1 = jnp.mean(dxh, axis=-1, keepdims=True)
        c2 = jnp.mean(dxh * xhat, axis=-1, keepdims=True)
        dr = rstd * (dxh - c1 - xhat * c2)
        dr_ref[...] = dr
        drb_ref[...] = dr.astype(BF16)
        dg_ref[...] += jnp.sum(dyf * xhat, axis=0, keepdims=True)
        db_ref[...] += jnp.sum(dyf, axis=0, keepdims=True)
        ds_ref[...] += jnp.sum(dr, axis=0, keepdims=True)

    row = pl.BlockSpec((tm, Dm), lambda i: (i, 0))
    vec = pl.BlockSpec((1, Dm), lambda i: (0, 0))
    acc = pl.BlockSpec((8, LANE), lambda i: (0, 0))
    vshape = jax.ShapeDtypeStruct((1, Dm), F32)
    return _pcall(kern, name=name,
                  out_shape=(jax.ShapeDtypeStruct((8, LANE), F32), jax.ShapeDtypeStruct((S, Dm), F32),
                             jax.ShapeDtypeStruct((S, Dm), BF16), vshape, vshape, vshape),
                  grid=(S // tm,), in_specs=[row, row, vec, vec], out_specs=(acc, row, row, vec, vec, vec),
                  dims=("arbitrary",), vmem_mb=56)(target, r, g.reshape(1, Dm), b.reshape(1, Dm))


def _rot_sum(t):
    return pltpu.roll(t, 32, 1) + pltpu.roll(t, 96, 1)


def _mla_qkv(proj, cos_t, sin_t, qg, kvg, wuq_t, wukv_t, *, name):
    S = proj.shape[0]
    tm = min(256, S)

    def kern(ql_ref, kvl_ref, kr_ref, cos_ref, sin_ref, qg_ref, kvg_ref, wuq_ref, wukv_ref,
             qc_ref, kc_ref, v_ref, vt_ref, qn_ref, kvn_ref):
        cosv = cos_ref[...]
        sinv = sin_ref[...]

        def rope(t):
            return t * cosv + _rot_sum(t) * sinv

        ql = ql_ref[...]
        qn = (ql * lax.rsqrt(jnp.mean(ql * ql, axis=-1, keepdims=True) + RMS_EPS) * qg_ref[...]).astype(BF16)
        kvl = kvl_ref[...]
        kvn = (kvl * lax.rsqrt(jnp.mean(kvl * kvl, axis=-1, keepdims=True) + RMS_EPS) * kvg_ref[...]).astype(BF16)
        qn_ref[...] = qn
        kvn_ref[...] = kvn
        q = lax.dot_general(qn, wuq_ref[...], NT, preferred_element_type=F32)
        kv = lax.dot_general(kvn, wukv_ref[...], NT, preferred_element_type=F32)
        kr = rope(kr_ref[...]).astype(BF16)
        for h in range(N_HEADS):
            c0 = 256 * h
            qc_ref[:, c0:c0 + 128] = q[:, c0:c0 + 128].astype(BF16)
            qc_ref[:, c0 + 128:c0 + 256] = rope(q[:, c0 + 128:c0 + 256]).astype(BF16)
            kc_ref[:, c0:c0 + 128] = kv[:, c0:c0 + 128].astype(BF16)
            kc_ref[:, c0 + 128:c0 + 256] = kr
            vh = kv[:, c0 + 128:c0 + 256]
            v_ref[:, 128 * h:128 * h + 128] = vh.astype(BF16)
            vt_ref[h] = jnp.transpose(vh).astype(BF16)

    def row(w, blk):
        return pl.BlockSpec((tm, w), lambda i: (i, blk))

    def full(shape):
        return pl.BlockSpec(shape, lambda i: (0,) * len(shape))

    t = min(TQ, S)
    per = t // tm
    vt_spec = pl.BlockSpec((N_HEADS, None, 128, tm), lambda i: (0, i // per, 0, i % per))
    outs = (jax.ShapeDtypeStruct((S, 2048), BF16), jax.ShapeDtypeStruct((S, 2048), BF16),
            jax.ShapeDtypeStruct((S, 1024), BF16), jax.ShapeDtypeStruct((N_HEADS, S // t, 128, t), BF16),
            jax.ShapeDtypeStruct((S, Q_LORA), BF16), jax.ShapeDtypeStruct((S, KV_LORA), BF16))
    return _pcall(kern, name=name, out_shape=outs, grid=(S // tm,),
                  in_specs=[row(512, 0), row(256, 2), row(128, 6), row(128, 0), row(128, 0),
                            full((1, Q_LORA)), full((1, KV_LORA)), full((2048, Q_LORA)), full((2048, KV_LORA))],
                  out_specs=(row(2048, 0), row(2048, 0), row(1024, 0), vt_spec, row(512, 0), row(256, 0)),
                  dims=("parallel",), vmem_mb=48)(
                      proj, proj, proj, cos_t, sin_t, qg.reshape(1, -1), kvg.reshape(1, -1), wuq_t, wukv_t)


def _mla_qkv_bwd(dqb, dkvb, dkr_heads, proj, cos_t, sin_t, qg, kvg, wuq_t, wukv_t, *, name):
    S = proj.shape[0]
    tm = min(256, S)

    def kern(dqb_ref, dkvb_ref, dkrh_ref, ql_ref, kvl_ref, cos_ref, sin_ref, qg_ref, kvg_ref, wuq_ref, wukv_ref,
             dml_ref, dqg_ref, dkvg_ref):
        @pl.when(pl.program_id(0) == 0)
        def _():
            dqg_ref[...] = jnp.zeros_like(dqg_ref)
            dkvg_ref[...] = jnp.zeros_like(dkvg_ref)

        cosv = cos_ref[...]
        sinv = sin_ref[...]

        def unrope(t):
            return t * cosv - _rot_sum(t) * sinv

        dkr = dkrh_ref[:, 0:128]
        for h in range(1, N_HEADS):
            dkr = dkr + dkrh_ref[:, 128 * h:128 * h + 128]

        def rms_bwd(x, g, dy):
            n = x.shape[-1]
            rs = lax.rsqrt(jnp.mean(x * x, axis=-1, keepdims=True) + RMS_EPS)
            dyg = dy * g
            dx = rs * dyg - x * (rs * rs * rs) * (jnp.sum(dyg * x, axis=-1, keepdims=True) / n)
            return dx, jnp.sum(dy * (x * rs), axis=0, keepdims=True)

        dqn = jnp.dot(dqb_ref[...], wuq_ref[...], preferred_element_type=F32)
        dql, dqg = rms_bwd(ql_ref[...], qg_ref[...], dqn)
        dqg_ref[...] += dqg
        dkvn = jnp.dot(dkvb_ref[...], wukv_ref[...], preferred_element_type=F32)
        dkvl, dkvg = rms_bwd(kvl_ref[...], kvg_ref[...], dkvn)
        dkvg_ref[...] += dkvg
        dml_ref[:, 0:512] = dql.astype(BF16)
        dml_ref[:, 512:768] = dkvl.astype(BF16)
        dml_ref[:, 768:896] = unrope(dkr).astype(BF16)
        dml_ref[:, 896:1024] = jnp.zeros((tm, 128), BF16)

    def row(w, blk):
        return pl.BlockSpec((tm, w), lambda i: (i, blk))

    def full(shape):
        return pl.BlockSpec(shape, lambda i: (0,) * len(shape))

    outs = (jax.ShapeDtypeStruct((S, W_MLA), BF16), jax.ShapeDtypeStruct((1, Q_LORA), F32),
            jax.ShapeDtypeStruct((1, KV_LORA), F32))
    return _pcall(kern, name=name, out_shape=outs, grid=(S // tm,),
                  in_specs=[row(2048, 0), row(2048, 0), row(1024, 0), row(512, 0), row(256, 2),
                            row(128, 0), row(128, 0), full((1, Q_LORA)), full((1, KV_LORA)),
                            full((2048, Q_LORA)), full((2048, KV_LORA))],
                  out_specs=(row(W_MLA, 0), full((1, Q_LORA)), full((1, KV_LORA))),
                  dims=("arbitrary",), vmem_mb=56)(
                      dqb, dkvb, dkr_heads, proj, proj, cos_t, sin_t, qg.reshape(1, -1), kvg.reshape(1, -1),
                      wuq_t, wukv_t)


def _flash_fwd(qc, kc, vt, *, name, comm=None):
    S = qc.shape[0]
    t = min(TQ, S)
    n = S // t

    def kern(q_ref, k_ref, vt_ref, o_ref, lse_ref, m_s, l_s, acc_s):
        qi = pl.program_id(1)
        m_s[...] = jnp.full_like(m_s, -jnp.inf)
        l_s[...] = jnp.zeros_like(l_s)
        acc_s[...] = jnp.zeros_like(acc_s)

        half = t // 2

        def scores(kb, q_lo=0, q_n=t, k_n=t):
            k0 = pl.multiple_of(kb * t, t)
            return lax.dot_general(k_ref[pl.ds(k0, k_n), :], q_ref[q_lo:q_lo + q_n, :], NT,
                                   preferred_element_type=F32)

        def update(kb, st, q_lo=0, diagonal=False):
            k_n, q_n = st.shape
            if diagonal:
                krow = lax.broadcasted_iota(jnp.int32, (k_n, q_n), 0)
                qcol = lax.broadcasted_iota(jnp.int32, (k_n, q_n), 1) + q_lo
                st = jnp.where(krow <= qcol, st, -jnp.inf)
            lanes = slice(q_lo, q_lo + q_n)
            m_prev = m_s[:, lanes]
            m_new = jnp.maximum(m_prev, jnp.max(st, axis=0, keepdims=True))
            a = jnp.exp2((m_prev - m_new) * SCALE_LOG2E)
            pt = jnp.exp2((st - m_new) * SCALE_LOG2E)
            l_s[:, lanes] = a * l_s[:, lanes] + jnp.sum(pt, axis=0, keepdims=True)
            acc_s[:, lanes] = a * acc_s[:, lanes] + jnp.dot(vt_ref[kb, :, 0:k_n], pt.astype(BF16),
                                                            preferred_element_type=F32)
            m_s[:, lanes] = m_new

        def group(kb, count, last_diagonal):
            whole = count - 1 if last_diagonal else count
            sts = [scores(kb + g) for g in range(whole)]
            if last_diagonal:
                kd = kb + count - 1
                s_lo, s_hi = scores(kd, 0, half, half), scores(kd, half, half, t)
            for g in range(whole):
                update(kb + g, sts[g])
            if last_diagonal:
                update(kd, s_lo, 0, True)
                update(kd, s_hi, half, True)

        def body(i, carry):
            group(FWD_GROUP * i, FWD_GROUP, False)
            return carry

        full = qi // FWD_GROUP
        lax.fori_loop(0, full, body, 0)
        for rem in range(FWD_GROUP):
            @pl.when(qi - FWD_GROUP * full == rem)
            def _():
                group(qi - rem, rem + 1, True)
        o_ref[...] = jnp.transpose(acc_s[...] / l_s[...])
        lse_ref[pl.ds(qi, 1), :] = m_s[...] * SCALE_LOG2E + jnp.log2(l_s[...])

    q_spec = pl.BlockSpec((t, 256), lambda h, qi: (qi, h))
    k_spec = pl.BlockSpec((S, 256), lambda h, qi: (0, h))
    vt_spec = pl.BlockSpec((None, n, 128, t), lambda h, qi: (h, 0, 0, 0))
    o_spec = pl.BlockSpec((t, 128), lambda h, qi: (qi, h))
    lse_spec = pl.BlockSpec((None, n, t), lambda h, qi: (h, 0, 0))
    return _pcall(kern, name=name,
                  out_shape=(jax.ShapeDtypeStruct((S, D_MLA), F32), jax.ShapeDtypeStruct((N_HEADS, n, t), F32)),
                  grid=(N_HEADS, n), in_specs=[q_spec, k_spec, vt_spec], out_specs=(o_spec, lse_spec),
                  scratch=[pltpu.VMEM((1, t), F32), pltpu.VMEM((1, t), F32), pltpu.VMEM((128, t), F32)],
                  dims=("parallel", "arbitrary"), vmem_mb=48, comm=comm)(qc, kc, vt)


def _attn_delta(o, do, *, name):
    S = o.shape[0]
    t = min(TQ, S)
    n = S // t

    def kern(o_ref, do_ref, dl_ref):
        i = pl.program_id(0)
        prod = o_ref[...] * do_ref[...]
        lane = lax.broadcasted_iota(jnp.int32, (t, LANE), 1)
        dmat = jnp.zeros((t, LANE), F32)
        for h in range(N_HEADS):
            dmat = jnp.where(lane == h, jnp.sum(prod[:, 128 * h:128 * h + 128], axis=1, keepdims=True), dmat)
        dmat_t = jnp.transpose(dmat)
        for h in range(N_HEADS):
            dl_ref[h, pl.ds(i, 1), :] = dmat_t[h:h + 1, :]

    row = pl.BlockSpec((t, D_MLA), lambda i: (i, 0))
    return _pcall(kern, name=name, out_shape=jax.ShapeDtypeStruct((N_HEADS, n, t), F32), grid=(n,),
                  in_specs=[row, row], out_specs=pl.BlockSpec((N_HEADS, n, t), lambda i: (0, 0, 0)),
                  dims=("arbitrary",), vmem_mb=48)(o, do)


def _flash_bwd(qc, kc, v, do, lse2, delta, cos_t, sin_t, *, name, comm=None):
    S = qc.shape[0]
    t = min(TQ, S)
    n = S // t

    def kern(q_ref, k_ref, v_ref, do_ref, lse_ref, dl_ref, cos_ref, sin_ref, dqb_ref, dkvb_ref, dkr_ref,
             dq_ref, dk_ref, dv_ref):
        ki = pl.program_id(1)

        @pl.when(ki == 0)
        def _():
            dq_ref[...] = jnp.zeros_like(dq_ref)

        dk_ref[...] = jnp.zeros_like(dk_ref)
        dv_ref[...] = jnp.zeros_like(dv_ref)

        half = t // 2

        def step(qb, q_lo=0, q_n=t, k_n=t, diagonal=False):
            q0 = pl.multiple_of(qb * t + q_lo, half)
            lanes = slice(q_lo, q_lo + q_n)
            kt = k_ref[0:k_n, :]
            qblk = q_ref[pl.ds(q0, q_n), :]
            dob = do_ref[pl.ds(q0, q_n), :].astype(BF16)
            st = lax.dot_general(kt, qblk, NT, preferred_element_type=F32)
            pt = jnp.exp2(st * SCALE_LOG2E - lse_ref[pl.ds(qb, 1), lanes])
            if diagonal:
                krow = lax.broadcasted_iota(jnp.int32, (k_n, q_n), 0)
                qcol = lax.broadcasted_iota(jnp.int32, (k_n, q_n), 1) + q_lo
                pt = jnp.where(krow <= qcol, pt, 0.0)
            dv_ref[0:k_n, :] += jnp.dot(pt.astype(BF16), dob, preferred_element_type=F32)
            dpt = lax.dot_general(v_ref[0:k_n, :], dob, NT, preferred_element_type=F32)
            dst = (pt * (dpt - dl_ref[pl.ds(qb, 1), lanes]) * SCALE).astype(BF16)
            dk_ref[0:k_n, :] += jnp.dot(dst, qblk, preferred_element_type=F32)
            dq_ref[pl.ds(q0, q_n), :] += lax.dot_general(dst, kt, TN, preferred_element_type=F32)

        step(ki, 0, half, half, True)
        step(ki, half, half, t, True)
        rest = n - 1 - ki
        full = rest // BWD_GROUP

        def body(i, carry):
            for g in range(BWD_GROUP):
                step(ki + 1 + BWD_GROUP * i + g)
            return carry

        lax.fori_loop(0, full, body, 0)
        for rem in range(1, BWD_GROUP):
            @pl.when(rest - BWD_GROUP * full == rem)
            def _():
                for g in range(rem):
                    step(n - rem + g)

        dkvb_ref[:, 0:128] = dk_ref[:, 0:128].astype(BF16)
        dkvb_ref[:, 128:256] = dv_ref[...].astype(BF16)
        dkr_ref[...] = dk_ref[:, 128:256]

        @pl.when(ki == n - 1)
        def _():
            dqb_ref[:, 0:128] = dq_ref[:, 0:128].astype(BF16)
            dqr = dq_ref[:, 128:256]
            dqb_ref[:, 128:256] = (dqr * cos_ref[...] - _rot_sum(dqr) * sin_ref[...]).astype(BF16)

    def whole(w):
        return pl.BlockSpec((S, w), lambda h, ki: (0, h))

    def krow(w):
        return pl.BlockSpec((t, w), lambda h, ki: (ki, h))

    stat = pl.BlockSpec((None, n, t), lambda h, ki: (h, 0, 0))
    table = pl.BlockSpec((S, 128), lambda h, ki: (0, 0))
    return _pcall(kern, name=name,
                  out_shape=(jax.ShapeDtypeStruct((S, 2048), BF16), jax.ShapeDtypeStruct((S, 2048), BF16),
                             jax.ShapeDtypeStruct((S, D_MLA), F32)),
                  grid=(N_HEADS, n),
                  in_specs=[whole(256), krow(256), krow(128), whole(128), stat, stat, table, table],
                  out_specs=(whole(256), krow(256), krow(128)),
                  scratch=[pltpu.VMEM((S, 256), F32), pltpu.VMEM((t, 256), F32), pltpu.VMEM((t, 128), F32)],
                  dims=("parallel", "arbitrary"), vmem_mb=56, comm=comm)(qc, kc, v, do, lse2, delta, cos_t, sin_t)


def _mixer_specs(S, tm):
    hb = tm // HALO
    last_hb = S // HALO - 1

    def main(w, blk):
        return pl.BlockSpec((tm, w), lambda i: (i, blk))

    def prev(w, blk):
        return pl.BlockSpec((HALO, w), lambda i: (jnp.maximum(i * hb - 1, 0), blk))

    def nxt(w, blk):
        return pl.BlockSpec((HALO, w), lambda i: (jnp.minimum((i + 1) * hb, last_hb), blk))

    def full(shape):
        return pl.BlockSpec(shape, lambda i: (0,) * len(shape))

    return main, prev, nxt, full


def _fill_halo(i, xp, xu, hp_ref, hch_ref, hcc_ref, pin_ref, ch_ref, cc_ref, tm):
    first = i == 0
    xp[0:HALO, :] = jnp.where(first, 0.0, hp_ref[...])
    xp[HALO:HALO + tm, :] = pin_ref[...]
    xu[0:HALO, :] = jnp.where(first, 0.0, hch_ref[...] * hcc_ref[...])
    xu[HALO:HALO + tm, :] = cc_ref[...] * ch_ref[...]


def _pooled(xp, g, t1, tm):
    w = POOL_WINDOWS[g]
    lanes = slice(128 * g, 128 * g + 128)
    x0 = xp[HALO:HALO + tm, lanes]
    acc = x0
    for k in range(1, w):
        acc = acc + xp[HALO - k:HALO - k + tm, lanes]
    return acc / jnp.minimum(t1, float(w)) - x0


def _conv_fwd(xu, cw_ref, tm):
    return (cw_ref[0:1, :] * xu[HALO - 2:HALO - 2 + tm, :] + cw_ref[1:2, :] * xu[HALO - 1:HALO - 1 + tm, :]
            + cw_ref[2:3, :] * xu[HALO:HALO + tm, :])


def _mixer_fwd(proj, o, wpool, ps, convw, *, name):
    S = proj.shape[0]
    tm = min(256, S)
    main, prev, _, full = _mixer_specs(S, tm)

    def kern(gm_ref, pin_ref, gp_ref, ch_ref, cb_ref, cc_ref, gc_ref, hp_ref, hch_ref, hcc_ref,
             o_ref, wp_ref, ps_ref, cw_ref, mix_ref, xp, xu):
        i = pl.program_id(0)
        _fill_halo(i, xp, xu, hp_ref, hch_ref, hcc_ref, pin_ref, ch_ref, cc_ref, tm)
        t1 = (i * tm + lax.broadcasted_iota(jnp.int32, (tm, 1), 0) + 1).astype(F32)
        for g in range(4):
            lanes = slice(128 * g, 128 * g + 128)
            pooled = _pooled(xp, g, t1, tm)
            z = jnp.dot(pooled.astype(BF16), wp_ref[g].astype(BF16), preferred_element_type=F32)
            gp = gp_ref[:, lanes]
            y = z * ps_ref[:, lanes] * (gp * _sigmoid(gp))
            mix_ref[:, 1024 + 128 * g:1024 + 128 * g + 128] = y.astype(BF16)
        gc = gc_ref[...]
        mix_ref[:, 1536:2048] = (cb_ref[...] * _conv_fwd(xu, cw_ref, tm) * (gc * _sigmoid(gc))).astype(BF16)
        gm = gm_ref[...]
        mix_ref[:, 0:1024] = (o_ref[...] * (gm * _sigmoid(gm))).astype(BF16)

    return _pcall(kern, name=name, out_shape=jax.ShapeDtypeStruct((S, 2048), BF16), grid=(S // tm,),
                  in_specs=[main(1024, 1), main(512, 4), main(512, 5), main(512, 6), main(512, 7), main(512, 8),
                            main(512, 9), prev(512, 4), prev(512, 6), prev(512, 8),
                            main(1024, 0), full((4, 128, 128)), full((1, 512)), full((3, 512))],
                  out_specs=main(2048, 0),
                  scratch=[pltpu.VMEM((tm + HALO, 512), F32), pltpu.VMEM((tm + HALO, 512), F32)],
                  dims=("parallel",), vmem_mb=48)(
                      proj, proj, proj, proj, proj, proj, proj, proj, proj, proj, o, wpool, ps.reshape(1, 512), convw)


def _mixer_bwd(dmix, proj, o, wpool, ps, convw, *, name):
    S = proj.shape[0]
    tm = min(256, S)
    n = S // tm
    main, prev, nxt, full = _mixer_specs(S, tm)

    def kern(dm_ref, dmn_ref, gm_ref, pin_ref, gp_ref, ch_ref, cb_ref, cc_ref, gc_ref,
             hp_ref, hch_ref, hcc_ref, gpn_ref, cbn_ref, gcn_ref, o_ref, wp_ref, ps_ref, cw_ref,
             d_ref, do_ref, dwp_ref, dps_ref, dcw_ref, xp, xu, ee, ed):
        i = pl.program_id(0)
        last = i == n - 1

        @pl.when(i == 0)
        def _():
            dwp_ref[...] = jnp.zeros_like(dwp_ref)
            dps_ref[...] = jnp.zeros_like(dps_ref)
            dcw_ref[...] = jnp.zeros_like(dcw_ref)

        _fill_halo(i, xp, xu, hp_ref, hch_ref, hcc_ref, pin_ref, ch_ref, cc_ref, tm)
        t1 = (i * tm + lax.broadcasted_iota(jnp.int32, (tm, 1), 0) + 1).astype(F32)
        t1n = ((i + 1) * tm + lax.broadcasted_iota(jnp.int32, (HALO, 1), 0) + 1).astype(F32)
        c_pin, c_gp, c_ch, c_cb, c_cc, c_gc = 1024, 1536, 2048, 2560, 3072, 3584

        for g in range(4):
            w = float(POOL_WINDOWS[g])
            lanes = slice(128 * g, 128 * g + 128)
            pooled = _pooled(xp, g, t1, tm)
            pb = pooled.astype(BF16)
            wp = wp_ref[g].astype(BF16)
            z = jnp.dot(pb, wp, preferred_element_type=F32)
            psl = ps_ref[:, lanes]
            sg, dsg = _silu_and_grad(gp_ref[:, lanes])
            dmp = dm_ref[:, 1024 + 128 * g:1024 + 128 * g + 128]
            dyp = dmp * sg
            d_ref[:, c_gp + 128 * g:c_gp + 128 * g + 128] = (dmp * (z * psl) * dsg).astype(BF16)
            dps_ref[:, lanes] += jnp.sum(dyp * z, axis=0, keepdims=True)
            dz = (dyp * psl).astype(BF16)
            dwp_ref[g] += lax.dot_general(pb, dz, TN, preferred_element_type=F32)
            dpl = lax.dot_general(dz, wp, NT, preferred_element_type=F32)
            ee[0:tm, lanes] = dpl / jnp.minimum(t1, w)
            gpn = gpn_ref[:, lanes]
            dzn = (dmn_ref[:, lanes] * (gpn * _sigmoid(gpn)) * psl).astype(BF16)
            dpn = lax.dot_general(dzn, wp, NT, preferred_element_type=F32)
            ee[tm:tm + HALO, lanes] = jnp.where(last, 0.0, dpn / jnp.minimum(t1n, w))
            acc = ee[0:tm, lanes]
            for k in range(1, POOL_WINDOWS[g]):
                acc = acc + ee[k:k + tm, lanes]
            d_ref[:, c_pin + 128 * g:c_pin + 128 * g + 128] = (acc - dpl).astype(BF16)

        yc = _conv_fwd(xu, cw_ref, tm)
        sgc, dsgc = _silu_and_grad(gc_ref[...])
        cb = cb_ref[...]
        dmc = dm_ref[:, 1536:2048]
        d_ref[:, c_gc:c_gc + 512] = (dmc * cb * yc * dsgc).astype(BF16)
        d_ref[:, c_cb:c_cb + 512] = (dmc * yc * sgc).astype(BF16)
        dyc = dmc * cb * sgc
        ed[0:tm, :] = dyc
        gcn = gcn_ref[...]
        ed[tm:tm + HALO, :] = jnp.where(last, 0.0, dmn_ref[:, 512:1024] * cbn_ref[...] * (gcn * _sigmoid(gcn)))
        dcw_ref[0:1, :] += jnp.sum(dyc * xu[HALO - 2:HALO - 2 + tm, :], axis=0, keepdims=True)
        dcw_ref[1:2, :] += jnp.sum(dyc * xu[HALO - 1:HALO - 1 + tm, :], axis=0, keepdims=True)
        dcw_ref[2:3, :] += jnp.sum(dyc * xu[HALO:HALO + tm, :], axis=0, keepdims=True)
        du = cw_ref[2:3, :] * dyc + cw_ref[1:2, :] * ed[1:1 + tm, :] + cw_ref[0:1, :] * ed[2:2 + tm, :]
        d_ref[:, c_cc:c_cc + 512] = (du * ch_ref[...]).astype(BF16)
        d_ref[:, c_ch:c_ch + 512] = (du * cc_ref[...]).astype(BF16)

        sgm, dsgm = _silu_and_grad(gm_ref[...])
        dmm = dm_ref[:, 0:1024]
        do_ref[...] = dmm * sgm
        d_ref[:, 0:1024] = (dmm * o_ref[...] * dsgm).astype(BF16)

    outs = (jax.ShapeDtypeStruct((S, W_MIX), BF16), jax.ShapeDtypeStruct((S, 1024), F32),
            jax.ShapeDtypeStruct((4, 128, 128), F32), jax.ShapeDtypeStruct((1, 512), F32),
            jax.ShapeDtypeStruct((3, 512), F32))
    scr = [pltpu.VMEM((tm + HALO, 512), F32) for _ in range(4)]
    return _pcall(kern, name=name, out_shape=outs, grid=(n,),
                  in_specs=[main(2048, 0), nxt(1024, 1),
                            main(1024, 1), main(512, 4), main(512, 5), main(512, 6), main(512, 7), main(512, 8),
                            main(512, 9), prev(512, 4), prev(512, 6), prev(512, 8),
                            nxt(512, 5), nxt(512, 7), nxt(512, 9),
                            main(1024, 0), full((4, 128, 128)), full((1, 512)), full((3, 512))],
                  out_specs=(main(W_MIX, 0), main(1024, 0), full((4, 128, 128)), full((1, 512)), full((3, 512))),
                  scratch=scr, dims=("arbitrary",), vmem_mb=56)(
                      dmix, dmix, proj, proj, proj, proj, proj, proj, proj, proj, proj, proj, proj, proj, proj,
                      o, wpool, ps.reshape(1, 512), convw)


def _outproj_residual(mix, wout, h, bout, *, name):
    S, Dm = h.shape
    tm = min(512, S)

    def kern(mix_ref, w_ref, h_ref, bo_ref, r_ref):
        out = jnp.dot(mix_ref[...], w_ref[...], preferred_element_type=F32) + bo_ref[...]
        r_ref[...] = ALPHA * h_ref[...] + out

    row = pl.BlockSpec((tm, Dm), lambda i: (i, 0))
    vec = pl.BlockSpec((1, Dm), lambda i: (0, 0))
    wsp = pl.BlockSpec((Dm, Dm), lambda i: (0, 0), pipeline_mode=pl.Buffered(1))
    return _pcall(kern, name=name, out_shape=jax.ShapeDtypeStruct((S, Dm), F32), grid=(S // tm,),
                  in_specs=[row, wsp, row, vec], out_specs=row, dims=("parallel",), vmem_mb=56)(
                      mix, wout, h, bout.reshape(1, Dm))


def _outproj_ln(mix, wout, h, bout, g, b, *, name):
    S, Dm = h.shape
    tm = min(512, S)

    def kern(mix_ref, w_ref, h_ref, bo_ref, g_ref, b_ref, y_ref, yb_ref, r_ref):
        out = jnp.dot(mix_ref[...], w_ref[...], preferred_element_type=F32) + bo_ref[...]
        r = ALPHA * h_ref[...] + out
        r_ref[...] = r
        mu = jnp.mean(r, axis=-1, keepdims=True)
        xc = r - mu
        var = jnp.mean(xc * xc, axis=-1, keepdims=True)
        y = xc * lax.rsqrt(var + LN_EPS) * g_ref[...] + b_ref[...]
        y_ref[...] = y
        yb_ref[...] = y.astype(BF16)

    row = pl.BlockSpec((tm, Dm), lambda i: (i, 0))
    vec = pl.BlockSpec((1, Dm), lambda i: (0, 0))
    wsp = pl.BlockSpec((Dm, Dm), lambda i: (0, 0), pipeline_mode=pl.Buffered(1))
    sds = jax.ShapeDtypeStruct((S, Dm), F32)
    return _pcall(kern, name=name, out_shape=(sds, jax.ShapeDtypeStruct((S, Dm), BF16), sds), grid=(S // tm,),
                  in_specs=[row, wsp, row, vec, vec, vec], out_specs=(row, row, row), dims=("parallel",),
                  vmem_mb=56)(
                      mix, wout, h, bout.reshape(1, Dm), g.reshape(1, Dm), b.reshape(1, Dm))


def _adamw_math(w, g, m, v):
    m = ADAM_B1 * m + (1.0 - ADAM_B1) * g
    v = ADAM_B2 * v + (1.0 - ADAM_B2) * (g * g)
    m_hat = m / (1.0 - ADAM_B1 ** ADAM_STEP)
    v_hat = v / (1.0 - ADAM_B2 ** ADAM_STEP)
    delta = -ADAM_LR * (m_hat / (jnp.sqrt(v_hat) + ADAM_EPS) + ADAM_WD * w)
    return delta, m, v


def _row_tile(R, C):
    best = None
    for cand in range(8, R, 8):
        if R % cand == 0 and cand * C <= 256 * 1024:
            best = cand
    return best if best is not None else R


def _adamw(w, g, m, v, *, name):
    shape = w.shape
    C = shape[-1]
    R = 1
    for s in shape[:-1]:
        R *= s
    tr = _row_tile(R, C)

    def kern(w_ref, g_ref, m_ref, v_ref, d_ref, mo_ref, vo_ref):
        d, mn, vn = _adamw_math(w_ref[...], g_ref[...], m_ref[...], v_ref[...])
        d_ref[...] = d
        mo_ref[...] = mn
        vo_ref[...] = vn

    blk = pl.BlockSpec((tr, C), lambda i: (i, 0))
    sds = jax.ShapeDtypeStruct((R, C), F32)
    outs = _pcall(kern, name=name, out_shape=(sds, sds, sds), grid=(R // tr,), in_specs=[blk] * 4,
                  out_specs=(blk, blk, blk), dims=("parallel",), vmem_mb=48)(
                      w.reshape(R, C), g.reshape(R, C), m.reshape(R, C), v.reshape(R, C))
    return tuple(t.reshape(shape) for t in outs)


def _adamw_halves(w, m, v, halves, c_idx, *, name, comm=None):
    _, R, C = w.shape
    ch = C // 2
    tr = _row_tile(R, ch)
    nb = R // tr

    def kern(c_ref, w_ref, a0_ref, b0_ref, a1_ref, b1_ref, m_ref, v_ref, g_ref, d_ref, mo_ref, vo_ref):
        layer = pl.program_id(0) // nb
        mine = pl.program_id(1) == c_ref[0]
        g = jnp.where(layer == 0, jnp.where(mine, a0_ref[...], b0_ref[...]),
                      jnp.where(mine, a1_ref[...], b1_ref[...]))
        g_ref[...] = g
        d, mn, vn = _adamw_math(w_ref[...], g, m_ref[...], v_ref[...])
        d_ref[...] = d
        mo_ref[...] = mn
        vo_ref[...] = vn

    full = pl.BlockSpec((tr, ch), lambda i, hc: (i, hc))
    half = pl.BlockSpec((tr, ch), lambda i, hc: (i % nb, 0))
    sds = jax.ShapeDtypeStruct((2 * R, C), F32)
    (a0, b0), (a1, b1) = halves
    res = _pcall(kern, name=name, out_shape=(sds,) * 4, grid=(2 * nb, 2),
                 in_specs=[pl.BlockSpec(memory_space=pltpu.SMEM), full, half, half, half, half, full, full],
                 out_specs=(full,) * 4, dims=("parallel", "parallel"), vmem_mb=48, comm=comm)(
                     c_idx, w.reshape(2 * R, C), a0, b0, a1, b1, m.reshape(2 * R, C), v.reshape(2 * R, C))
    outs, landed = res if comm is not None else (res, None)
    outs = tuple(t.reshape(2, R, C) for t in outs)
    return outs if comm is None else (outs, landed)


def _packed_pieces(shape):
    if len(shape) == 4:
        return [((l * shape[1] + g) * 128, 128, (l, g)) for l in range(shape[0]) for g in range(shape[1])]
    per_row = shape[1] // LANE
    return [(a * per_row + j, 1, (slice(a, a + 1), slice(LANE * j, LANE * (j + 1))))
            for a in range(shape[0]) for j in range(per_row)]


def _small_sum_adamw(gathered, own, weights, *, name):
    R = gathered.shape[1]
    nw = len(weights)
    shapes = [w.shape for w, _, _ in weights]
    first_row, r0 = [], 0
    for shp in shapes:
        first_row.append(r0)
        n = 1
        for s in shp:
            n *= s
        r0 += n // LANE

    def kern(ga_ref, own_ref, *refs):
        ins, gsum_ref, outs = refs[:3 * nw], refs[3 * nw], refs[3 * nw + 1:]
        me = 4 * lax.axis_index("x") + 2 * lax.axis_index("y") + lax.axis_index("c")

        def block(k):
            other = ga_ref[jnp.where(me == k, (k + 1) % N_DEV, k)]
            return jnp.where(me == k, own_ref[...], other)

        g = block(0)
        for k in range(1, N_DEV):
            g = g + block(k)
        gsum_ref[...] = g
        for p, shp in enumerate(shapes):
            w_ref, m_ref, v_ref = ins[3 * p:3 * p + 3]
            g_out, d_out, m_out, v_out = outs[4 * p:4 * p + 4]
            for row, rows, idx in _packed_pieces(shp):
                gp = gsum_ref[first_row[p] + row:first_row[p] + row + rows, :]
                d, mn, vn = _adamw_math(w_ref[idx], gp, m_ref[idx], v_ref[idx])
                g_out[idx] = gp
                d_out[idx] = d
                m_out[idx] = mn
                v_out[idx] = vn

    out_shape = [jax.ShapeDtypeStruct((R, LANE), F32)]
    for shp in shapes:
        out_shape += [jax.ShapeDtypeStruct(shp, F32)] * 4
    flat = [a for wmv in weights for a in wmv]
    res = _pcall(kern, name=name, out_shape=tuple(out_shape), vmem_mb=48)(gathered, own, *flat)
    return res[0], [tuple(res[1 + 4 * p:5 + 4 * p]) for p in range(nw)]


def _pair_sum(g, theirs, c_idx, *, name):
    R, C = g.shape
    ch = C // 2
    tr = _row_tile(R, ch)

    def kern(c_ref, a_ref, b_ref, o_ref):
        o_ref[...] = (a_ref[...] + b_ref[...]).astype(BF16)

    gs = pltpu.PrefetchScalarGridSpec(
        num_scalar_prefetch=1, grid=(R // tr,),
        in_specs=[pl.BlockSpec((tr, ch), lambda i, c: (i, c[0])), pl.BlockSpec((tr, ch), lambda i, c: (i, 0))],
        out_specs=pl.BlockSpec((tr, ch), lambda i, c: (i, 0)))
    return pl.pallas_call(kern, name=name, out_shape=jax.ShapeDtypeStruct((R, ch), BF16), grid_spec=gs,
                          compiler_params=pltpu.CompilerParams(dimension_semantics=("parallel",),
                                                               vmem_limit_bytes=48 << 20))(c_idx, g, theirs)


WeightRows = collections.namedtuple("WeightRows", "full_rows own_rows cols pieces zero_rows")


def _w_in_piece_a(j):
    return jnp.where(j == 0, 0, 1232 * j + GAP)


def _w_in_piece_b(j):
    return jnp.where(j == 0, GAP_AT + GAP, 1232 * j + GAP_AT + GAP)


W_IN = WeightRows(NP, 1232, D_MODEL, ((0, GAP_AT, _w_in_piece_a), (GAP_AT, 1232 - GAP_AT, _w_in_piece_b)),
                  ((GAP_AT, GAP),))
W_OUT = WeightRows(2048, 512, D_MODEL, ((0, 512, lambda j: 512 * j),), ())
W_UQ = WeightRows(2048, 384, Q_LORA, ((0, 192, lambda j: 512 * j), (192, 192, lambda j: 512 * j + 256)),
                  tuple((256 * h + 192, 64) for h in range(N_HEADS)))
W_UKV = WeightRows(2048, 512, KV_LORA, ((0, 512, lambda j: 512 * j),), ())
W_CONV = WeightRows(64, 16, 256, ((0, 16, lambda j: 16 * j),), ())
SHARDED = (W_IN, W_OUT, W_UQ, W_UKV)
SHARDED_NAMES = ("w_in", "w_out", "w_uq", "w_ukv")
WEIGHT_ROWS = dict(zip(SHARDED_NAMES, SHARDED))


def _mesh_pos():
    x, y, c = lax.axis_index("x"), lax.axis_index("y"), lax.axis_index("c")
    return x, y, c


def _other_chips(x, y):
    return [(1 - x, y), (x, 1 - y), (1 - x, 1 - y)]


def _rows(start, n):
    return pl.ds(pl.multiple_of(start, 16), n)


def _half_cols(spec, c):
    ch = spec.cols // 2
    return pl.ds(pl.multiple_of(c * ch, LANE), ch)


def _allgather_script(specs, shards, zeros, layers):
    na = len(specs)
    zlist = [a for a in range(na) if zeros[a] is not None]
    n_layers = [shards[a].shape[0] if layers[a] is None else 1 for a in range(na)]
    plan_first, plan_own, plan_zero = [], [], []
    for a, spec in enumerate(specs):
        for p in range(len(spec.pieces)):
            plan_own.append((a, p))
            for k in range(3):
                plan_first.append((a, p, k))
        for z in range(len(spec.zero_rows)):
            for l in range(n_layers[a]):
                plan_zero.append((a, z, l))
    nf = len(plan_first)
    n_sems = 2 * nf + len(plan_own) + len(plan_zero)

    def copies(ins_all, outs, send_sems, recv_sems):
        ins = [ins_all[a] if layers[a] is None else ins_all[a].at[pl.ds(layers[a], 1)] for a in range(na)]
        zrefs = dict(zip(zlist, ins_all[na:]))
        x, y, c = _mesh_pos()
        j = 2 * x + y
        chips = _other_chips(x, y)
        sibling = (x, y, 1 - c)

        def remote(src, dst, sem, to):
            return pltpu.make_async_remote_copy(src_ref=src, dst_ref=dst, send_sem=send_sems.at[sem],
                                                recv_sem=recv_sems.at[sem], device_id=to, device_id_type=MESH)

        def block(a, p, chip, cols):
            _, n, dst = specs[a].pieces[p]
            return outs[a].at[:, _rows(dst(chip), n), cols]

        def first(i):
            a, p, k = plan_first[i]
            src0, n, _ = specs[a].pieces[p]
            cols = _half_cols(specs[a], c)
            return remote(ins[a].at[:, pl.ds(src0, n), cols], block(a, p, j, cols), i, (*chips[k], c))

        def landed(i, half):
            a, p, k = plan_first[i]
            return block(a, p, 2 * chips[k][0] + chips[k][1], _half_cols(specs[a], half))

        def arrival(i, half, sem):
            return remote(landed(i, half), landed(i, half), sem, sibling)

        def passed(i):
            return remote(landed(i, c), landed(i, c), nf + i, sibling)

        def own(i):
            a, p = plan_own[i]
            src0, n, _ = specs[a].pieces[p]
            return remote(ins[a].at[:, pl.ds(src0, n), :], block(a, p, j, slice(None)), 2 * nf + i, sibling)

        def zero(i):
            a, z, l = plan_zero[i]
            r0, n = specs[a].zero_rows[z]
            return remote(zrefs[a].at[pl.ds(0, n), :], outs[a].at[l, pl.ds(r0, n), :],
                          2 * nf + len(plan_own) + i, sibling)

        fixed = [own(i) for i in range(len(plan_own))] + [zero(i) for i in range(len(plan_zero))]
        return c, fixed, first, arrival, passed

    def start(ins, outs, send_sems, recv_sems):
        _, fixed, first, _, _ = copies(ins, outs, send_sems, recv_sems)
        for cp in fixed:
            cp.start()
        for i in range(nf):
            first(i).start()

    def finish(ins, outs, send_sems, recv_sems):
        c, fixed, first, arrival, passed = copies(ins, outs, send_sems, recv_sems)
        for i in range(nf):
            arrival(i, c, i).wait_recv()
            passed(i).start()
        for i in range(nf):
            arrival(i, 1 - c, nf + i).wait_recv()
        for cp in fixed:
            cp.wait()
        for i in range(nf):
            first(i).wait_send()
            passed(i).wait_send()

    out_shape = tuple(jax.ShapeDtypeStruct((n_layers[a], spec.full_rows, spec.cols), BF16)
                      for a, spec in enumerate(specs))
    args = tuple(shards) + tuple(zeros[a] for a in zlist)
    return CommScript(args, out_shape, n_sems, start, finish)


def _start_all_wait_all(args, out_shape, n_sems, make_copies):
    def start(ins, outs, send_sems, recv_sems):
        for cp in make_copies(ins, outs, send_sems, recv_sems):
            cp.start()

    def finish(ins, outs, send_sems, recv_sems):
        for cp in make_copies(ins, outs, send_sems, recv_sems):
            cp.wait()

    return CommScript(tuple(args), tuple(out_shape), n_sems, start, finish)


def _exchange_script(specs, grads):
    na = len(grads)

    def make_copies(ins, outs, send_sems, recv_sems):
        x, y, c = _mesh_pos()
        return [pltpu.make_async_remote_copy(
            src_ref=ins[a].at[:, _half_cols(specs[a], 1 - c)], dst_ref=outs[a], send_sem=send_sems.at[a],
            recv_sem=recv_sems.at[a], device_id=(x, y, 1 - c), device_id_type=MESH) for a in range(na)]

    out_shape = [jax.ShapeDtypeStruct((s.full_rows, s.cols // 2), F32) for s in specs]
    return _start_all_wait_all(grads, out_shape, na, make_copies)


def _scatter_script(specs, parts):
    na = len(parts)
    plan = [(a, p, k) for a in range(na) for p in range(len(specs[a].pieces)) for k in range(3)]

    def make_copies(ins, outs, send_sems, recv_sems):
        x, y, c = _mesh_pos()
        chips = _other_chips(x, y)
        copies = []
        for i, (a, p, k) in enumerate(plan):
            src0, n, dst = specs[a].pieces[p]
            pk = 2 * chips[k][0] + chips[k][1]
            copies.append(pltpu.make_async_remote_copy(
                src_ref=ins[a].at[_rows(dst(pk), n), :], dst_ref=outs[a].at[k, pl.ds(src0, n), :],
                send_sem=send_sems.at[i], recv_sem=recv_sems.at[i], device_id=(*chips[k], c), device_id_type=MESH))
        return copies

    out_shape = [jax.ShapeDtypeStruct((3, s.own_rows, s.cols // 2), BF16) for s in specs]
    return _start_all_wait_all(parts, out_shape, len(plan), make_copies)


def _chip_sum(spec, part, recv, *, name):
    ch = spec.cols // 2
    npieces = len(spec.pieces)

    def kern(recv_ref, part_ref, o_ref, own_ref, sems):
        j = 2 * lax.axis_index("x") + lax.axis_index("y")
        copies = []
        for p, (src0, n, dst) in enumerate(spec.pieces):
            copies.append(pltpu.make_async_copy(part_ref.at[_rows(dst(j), n), :], own_ref.at[pl.ds(src0, n), :],
                                                sems.at[p]))
        for cp in copies:
            cp.start()
        for cp in copies:
            cp.wait()
        o_ref[...] = ((own_ref[...].astype(F32) + recv_ref[0].astype(F32)) + recv_ref[1].astype(F32)) \
            + recv_ref[2].astype(F32)

    vm = pl.BlockSpec(memory_space=pltpu.VMEM)
    return _pcall(kern, name=name, out_shape=jax.ShapeDtypeStruct((spec.own_rows, ch), F32),
                  in_specs=[vm, HBM_SPEC], out_specs=vm,
                  scratch=[pltpu.VMEM((spec.own_rows, ch), BF16), pltpu.SemaphoreType.DMA((npieces,))],
                  vmem_mb=48)(recv, part)


def _sibling_script(sums):
    na = len(sums)

    def make_copies(ins, outs, send_sems, recv_sems):
        x, y, c = _mesh_pos()
        return [pltpu.make_async_remote_copy(
            src_ref=ins[a], dst_ref=outs[a], send_sem=send_sems.at[a], recv_sem=recv_sems.at[a],
            device_id=(x, y, 1 - c), device_id_type=MESH) for a in range(na)]

    out_shape = [jax.ShapeDtypeStruct(t.shape, t.dtype) for t in sums]
    return _start_all_wait_all(sums, out_shape, na, make_copies)


class _SemWindow:
    def __init__(self, sems, offset):
        self._sems, self._offset = sems, offset

    @property
    def at(self):
        return self

    def __getitem__(self, i):
        return self._sems.at[i + self._offset]


def _merge_scripts(*scripts):
    a_off, o_off, s_off = [0], [0], [0]
    for s in scripts:
        a_off.append(a_off[-1] + len(s.args))
        o_off.append(o_off[-1] + len(s.out_shape))
        s_off.append(s_off[-1] + s.n_sems)

    def phase(which):
        def run(ins, outs, send_sems, recv_sems):
            for n, s in enumerate(scripts):
                getattr(s, which)(ins[a_off[n]:a_off[n + 1]], outs[o_off[n]:o_off[n + 1]],
                                  _SemWindow(send_sems, s_off[n]), _SemWindow(recv_sems, s_off[n]))
        return run

    return CommScript(sum((tuple(s.args) for s in scripts), ()), sum((tuple(s.out_shape) for s in scripts), ()),
                      s_off[-1], phase("start"), phase("finish"))


class _GradReducer:
    def __init__(self, layer, names, grads, c_idx):
        self.specs = tuple(WEIGHT_ROWS[nm] for nm in names)
        self.grads, self.c_idx = tuple(grads), c_idx
        self.names = [f"{nm}{layer}" for nm in names]

    def exchange(self):
        return _exchange_script(self.specs, self.grads)

    def scatter(self, theirs):
        self.parts = tuple(_pair_sum(g, th, self.c_idx, name=f"pair_sum_{nm}")
                           for g, th, nm in zip(self.grads, theirs, self.names))
        return _scatter_script(self.specs, self.parts)

    def sibling(self, recv):
        self.sums = tuple(_chip_sum(s, p, r, name=f"chip_sum_{nm}")
                          for s, p, r, nm in zip(self.specs, self.parts, recv, self.names))
        return _sibling_script(self.sums)

    def done(self, others):
        return list(zip(self.sums, others))


def _allgather_small_script(block):
    m_per, n = block.shape

    def copies(ins, outs, send_sems, recv_sems):
        (x_ref,), (out_ref,) = ins, outs
        x, y, c = _mesh_pos()
        me, sibling = (x, y, c), (x, y, 1 - c)
        chips = _other_chips(x, y)

        def rows(px, py, pc):
            return out_ref.at[4 * px + 2 * py + pc]

        def copy(k, blk, to, src=None):
            return pltpu.make_async_remote_copy(
                src_ref=rows(*blk) if src is None else src, dst_ref=rows(*blk), send_sem=send_sems.at[k],
                recv_sem=recv_sems.at[k], device_id=to, device_id_type=MESH)

        first = [copy(0, me, sibling, src=x_ref)]
        first += [copy(1 + k, me, (*chip, c), src=x_ref) for k, chip in enumerate(chips)]
        passed = [copy(4 + k, (*chip, c), sibling) for k, chip in enumerate(chips)]
        landed = [copy(1 + k, (*chip, c), me) for k, chip in enumerate(chips)]
        from_sibling = [copy(0, sibling, me)] + [copy(4 + k, (*chip, 1 - c), me) for k, chip in enumerate(chips)]
        return first, passed, landed, from_sibling

    def start(ins, outs, send_sems, recv_sems):
        first, _, _, _ = copies(ins, outs, send_sems, recv_sems)
        for cp in first:
            cp.start()

    def finish(ins, outs, send_sems, recv_sems):
        first, passed, landed, from_sibling = copies(ins, outs, send_sems, recv_sems)
        for k in range(3):
            landed[k].wait_recv()
            passed[k].start()
        for cp in from_sibling:
            cp.wait_recv()
        for cp in first + passed:
            cp.wait_send()

    return CommScript((block,), (jax.ShapeDtypeStruct((N_DEV, m_per, n), block.dtype),), 7, start, finish)


def _rope_tables(positions):
    half = ROPE // 2
    inv_freq = ROPE_THETA ** (-jnp.arange(half, dtype=F32) / half)
    ang = positions.astype(F32)[:, None] * inv_freq
    cos, sin = jnp.cos(ang), jnp.sin(ang)
    S = positions.shape[0]
    cos_t = jnp.concatenate([cos, cos, jnp.ones((S, 64), F32)], axis=1)
    sin_t = jnp.concatenate([-sin, sin, jnp.zeros((S, 64), F32)], axis=1)
    return cos_t, sin_t


def _decode_conv(bits):
    rows = bits.reshape(DEPTH, N_CHIPS, 16, 256)[:, :, :3, :]
    conv = lax.bitcast_convert_type(rows.reshape(DEPTH, N_CHIPS, 3, 128, 2), F32)
    return jnp.transpose(conv, (0, 2, 1, 3)).reshape(DEPTH, 3, 512)


def _local_step(x, positions, target, emb_g, emb_b, w_in_t0, rest0, weights1, q_g, kv_g, w_pool, pool_scale,
                b_out, ln_g, ln_b, c_idx=None):
    cos_t, sin_t = _rope_tables(positions)
    if isinstance(w_in_t0, CommScript):
        (h, hb), (landed,) = _ln_fwd(x, emb_g, emb_b, name="emb_ln", comm=w_in_t0)
        w_in_t0 = landed[0]
    else:
        h, hb = _ln_fwd(x, emb_g, emb_b, name="emb_ln")
    weights = [None, weights1]
    saved = []
    for l in range(DEPTH):
        if l == 0 and isinstance(rest0, CommScript):
            proj, landed = _matmul(hb, w_in_t0, "nt", name="in_proj0", tm=1024, tn=1024, tk=2048, vmem_mb=56,
                                   comm=rest0)
            weights[0] = (w_in_t0,) + tuple(a[0] for a in landed[:3])
            conv_w = _decode_conv(landed[3])
        else:
            if l == 0:
                weights[0] = (w_in_t0,) + tuple(rest0[:3])
                conv_w = rest0[3]
            proj = _matmul(hb, weights[l][0], "nt", name=f"in_proj{l}", tm=1024, tn=1024, tk=2048, vmem_mb=56)
        w_in_t, w_out, w_uq_t, w_ukv_t = weights[l]
        qc, kc, v, vt, qn, kvn = _mla_qkv(proj, cos_t, sin_t, q_g[l], kv_g[l], w_uq_t, w_ukv_t, name=f"mla_qkv{l}")
        nxt = weights[l + 1] if l + 1 < DEPTH else None
        if isinstance(nxt, CommScript):
            (o, lse2), landed = _flash_fwd(qc, kc, vt, name=f"flash_fwd{l}", comm=nxt)
            weights[l + 1] = tuple(a[0] for a in landed)
        else:
            o, lse2 = _flash_fwd(qc, kc, vt, name=f"flash_fwd{l}")
        mix = _mixer_fwd(proj, o, w_pool[l], pool_scale[l], conv_w[l], name=f"mixer_fwd{l}")
        if l == DEPTH - 1:
            r = _outproj_residual(mix, w_out, h, b_out[l], name=f"out_proj{l}")
            saved.append((hb, proj, qc, kc, v, qn, kvn, o, lse2, mix, r))
        else:
            h_next, hb_next, r = _outproj_ln(mix, w_out, h, b_out[l], ln_g[l], ln_b[l], name=f"out_proj_ln{l}")
            saved.append((hb, proj, qc, kc, v, qn, kvn, o, lse2, mix, r))
            h, hb = h_next, hb_next

    small = [None] * DEPTH
    big = [None] * DEPTH
    above = scatter_above = None
    for l in reversed(range(DEPTH)):
        w_in_t, w_out, w_uq_t, w_ukv_t = weights[l]
        hb_in, proj, qc, kc, v, qn, kvn, o, lse2, mix, r = saved[l]
        if l == DEPTH - 1:
            loss_acc, dr, drb, d_ln_g, d_ln_b, d_b_out = _loss_ln_bwd(target, r, ln_g[l], ln_b[l], name="loss_ln_bwd")
        else:
            dr, drb, d_ln_g, d_ln_b, d_b_out = _ln_bwd(dh, r, ln_g[l], name=f"ln_bwd{l}")
        dmix = _matmul(drb, w_out, "nt", name=f"dmix{l}", tm=1024, tn=1024, tk=2048, vmem_mb=56)
        d_w_out = _matmul(mix, drb, "tn", name=f"dw_out{l}", tm=1024, tn=1024, tk=2048, vmem_mb=56)
        d_mix, do, d_w_pool, d_ps, d_conv = _mixer_bwd(dmix, proj, o, w_pool[l], pool_scale[l], conv_w[l],
                                                       name=f"mixer_bwd{l}")
        delta = _attn_delta(o, do, name=f"attn_delta{l}")
        if above is not None:
            (dqb, dkvb, dkr), recv = _flash_bwd(qc, kc, v, do, lse2, delta, cos_t, sin_t, name=f"flash_bwd{l}",
                                                comm=scatter_above)
            sibling_above = above.sibling(recv)
        else:
            dqb, dkvb, dkr = _flash_bwd(qc, kc, v, do, lse2, delta, cos_t, sin_t, name=f"flash_bwd{l}")
        d_mla, d_qg, d_kvg = _mla_qkv_bwd(dqb, dkvb, dkr, proj, cos_t, sin_t, q_g[l], kv_g[l], w_uq_t, w_ukv_t,
                                          name=f"mla_qkv_bwd{l}")
        d_w_uq_t = _matmul(dqb, qn, "tn", name=f"dw_uq{l}", tm=2048, tn=512, tk=2048, vmem_mb=56)
        d_w_ukv_t = _matmul(dkvb, kvn, "tn", name=f"dw_ukv{l}", tm=2048, tn=256, tk=2048, vmem_mb=56)
        small[l] = dict(q_g=d_qg[0], kv_g=d_kvg[0], w_pool=d_w_pool, pool_scale=d_ps[0], conv_w=d_conv,
                        b_out=d_b_out[0], ln_g=d_ln_g[0], ln_b=d_ln_b[0])
        rest = (d_w_out, d_w_uq_t, d_w_ukv_t)
        if c_idx is None:
            d_w_in_t = _dproj_t_times_h(d_mla, d_mix, hb_in, name=f"dw_in{l}")
            dh = _dproj_times_w(d_mla, d_mix, w_in_t, dr, ALPHA, name=f"dh{l}")
            big[l] = (d_w_in_t,) + rest
        elif l > 0:
            d_w_in_t = _dproj_t_times_h(d_mla, d_mix, hb_in, name=f"dw_in{l}")
            above = _GradReducer(l, SHARDED_NAMES, (d_w_in_t,) + rest, c_idx)
            dh, theirs = _dproj_times_w(d_mla, d_mix, w_in_t, dr, ALPHA, name=f"dh{l}", comm=above.exchange())
            scatter_above = above.scatter(theirs)
        else:
            red_rest = _GradReducer(l, SHARDED_NAMES[1:], rest, c_idx)
            d_w_in_t, landed = _dproj_t_times_h(d_mla, d_mix, hb_in, name=f"dw_in{l}",
                                                comm=_merge_scripts(sibling_above, red_rest.exchange()))
            big[l + 1] = above.done(landed[:len(SHARDED)])
            red_in = _GradReducer(l, SHARDED_NAMES[:1], (d_w_in_t,), c_idx)
            landed = _run_comm(_merge_scripts(red_in.exchange(), red_rest.scatter(landed[len(SHARDED):])),
                               name="exchange_w_in0")
            sibling_rest = red_rest.sibling(landed[1:])
            dh, landed = _dproj_times_w(d_mla, d_mix, w_in_t, dr, ALPHA, name=f"dh{l}",
                                        comm=_merge_scripts(red_in.scatter(landed[:1]), sibling_rest))
            recv_in, others_rest = landed[:1], landed[1:]
    grad_x, _, d_emb_g, d_emb_b, _ = _ln_bwd(dh, x, emb_g, name="emb_ln_bwd", bf16_copy=False)
    if c_idx is not None:
        others_in = _run_comm(red_in.sibling(recv_in), name="send_to_sibling0")
        big[0] = red_in.done(others_in) + red_rest.done(others_rest)
    return loss_acc[0, 0], grad_x, d_emb_g, d_emb_b, small, big


SMALL_ORDER = ("emb_ln_g", "emb_ln_b", "q_norm_g", "kv_norm_g", "w_pool", "pool_scale", "b_out", "ln_g", "ln_b")
SMALL_LAYER_KEYS = ("q_g", "kv_g", "w_pool", "pool_scale", "b_out", "ln_g", "ln_b", "conv_w")


def _pack_small(arrs, extra_rows):
    flat = jnp.concatenate([a.reshape(-1) for a in arrs])
    rows = flat.shape[0] // LANE
    total = -(-(rows + extra_rows) // 8) * 8
    return jnp.pad(flat, (0, total * LANE - flat.shape[0])).reshape(total, LANE)


def kernel(x, positions, emb_ln_g, emb_ln_b, w_in, q_norm_g, kv_norm_g, w_uq, w_ukv, w_pool, pool_scale, conv_w, w_out, b_out, ln_g, ln_b, loss_target, m_emb_ln_g, m_emb_ln_b, m_w_in, m_q_norm_g, m_kv_norm_g, m_w_uq, m_w_ukv, m_w_pool, m_pool_scale, m_conv_w, m_w_out, m_b_out, m_ln_g, m_ln_b, v_emb_ln_g, v_emb_ln_b, v_w_in, v_q_norm_g, v_kv_norm_g, v_w_uq, v_w_ukv, v_w_pool, v_pool_scale, v_conv_w, v_w_out, v_b_out, v_ln_g, v_ln_b):
    xi, yi, ci = lax.axis_index("x"), lax.axis_index("y"), lax.axis_index("c")
    chip = 2 * xi + yi
    c_idx = ci.reshape(1).astype(jnp.int32)

    def t(a):
        return jnp.swapaxes(a, 1, 2)

    conv_bits = lax.bitcast_convert_type(conv_w.reshape(DEPTH, 3 * 128), BF16).reshape(DEPTH, 3, 256)
    conv_bits = jnp.pad(conv_bits, ((0, 0), (0, 13), (0, 0)))
    own = (t(w_in).astype(BF16), w_out.astype(BF16), t(w_uq).astype(BF16), t(w_ukv).astype(BF16))
    zeros = (jnp.zeros((GAP, D_MODEL), BF16), None, jnp.zeros((64, Q_LORA), BF16), None)
    gather_in0 = _allgather_script((W_IN,), own[:1], zeros[:1], (0,))
    gather0 = _allgather_script(SHARDED[1:] + (W_CONV,), own[1:] + (conv_bits,), zeros[1:] + (None,),
                                (0, 0, 0, None))
    gather1 = _allgather_script(SHARDED, own, zeros, (1, 1, 1, 1))

    loss_part, grad_x, d_emb_g, d_emb_b, grads, reduced = _local_step(
        x[0], positions[0], loss_target[0], emb_ln_g, emb_ln_b, gather_in0, gather0, gather1, q_norm_g, kv_norm_g,
        w_pool, pool_scale, b_out, ln_g, ln_b, c_idx)

    def rows(a):
        return a.reshape(1, -1) if a.ndim == 1 else a

    small_wmv = [tuple(rows(a) for a in wmv) for wmv in (
        (emb_ln_g, m_emb_ln_g, v_emb_ln_g), (emb_ln_b, m_emb_ln_b, v_emb_ln_b),
        (q_norm_g, m_q_norm_g, v_q_norm_g), (kv_norm_g, m_kv_norm_g, v_kv_norm_g), (w_pool, m_w_pool, v_w_pool),
        (pool_scale, m_pool_scale, v_pool_scale), (b_out, m_b_out, v_b_out), (ln_g, m_ln_g, v_ln_g),
        (ln_b, m_ln_b, v_ln_b))]
    packed_g = _pack_small(
        [d_emb_g, d_emb_b] + [jnp.stack([grads[l][key] for l in range(DEPTH)]) for key in SMALL_LAYER_KEYS]
        + [jnp.pad(loss_part.reshape(1), (0, LANE - 1))], 0)
    (gathered,) = _run_comm(_allgather_small_script(packed_g), name="allgather_small")
    g_tot, small_upd = _small_sum_adamw(gathered, packed_g, small_wmv, name="small_sum_adamw")
    off = sum(w.size for w, _, _ in small_wmv)
    flat_tot = g_tot.reshape(-1)

    def halves(a):
        return [reduced[l][a] for l in range(DEPTH)]

    upd = {}
    upd["w_in"] = tuple(t(o) for o in _adamw_halves(t(w_in), t(m_w_in), t(v_w_in), halves(0), c_idx,
                                                    name="adamw_w_in"))
    conv_tot = flat_tot[off:off + DEPTH * 3 * 512].reshape(DEPTH, 3, 512)
    loss = flat_tot[off + DEPTH * 3 * 512]
    g_conv = lax.dynamic_slice_in_dim(conv_tot, chip * 128, 128, axis=2)

    def whole(a):
        return jnp.stack([jnp.where(ci == 0, jnp.concatenate([mine, oth], axis=1),
                                    jnp.concatenate([oth, mine], axis=1)) for mine, oth in halves(a)])

    upd["w_out"] = _adamw_halves(w_out, m_w_out, v_w_out, halves(1), c_idx, name="adamw_w_out")
    g_uq, g_ukv = t(whole(2)), t(whole(3))
    upd["w_uq"] = (g_uq,) + _adamw(w_uq, g_uq, m_w_uq, v_w_uq, name="adamw_w_uq")
    upd["w_ukv"] = (g_ukv,) + _adamw(w_ukv, g_ukv, m_w_ukv, v_w_ukv, name="adamw_w_ukv")
    upd["conv_w"] = (g_conv,) + _adamw(conv_w, g_conv, m_conv_w, v_conv_w, name="adamw_conv_w")
    for nm, res in zip(SMALL_ORDER, small_upd):
        upd[nm] = tuple(a.reshape(-1) for a in res) if nm in ("emb_ln_g", "emb_ln_b") else res

    order = ("emb_ln_g", "emb_ln_b", "w_in", "q_norm_g", "kv_norm_g", "w_uq", "w_ukv", "w_pool", "pool_scale",
             "conv_w", "w_out", "b_out", "ln_g", "ln_b")
    outs = [loss, grad_x[None]]
    for field in range(4):
        outs += [upd[nm][field] for nm in order]
    return tuple(outs)
```

```python
import collections

import jax
import jax.numpy as jnp
from jax import lax
from jax.experimental import pallas as pl
from jax.experimental.pallas import tpu as pltpu

F32 = jnp.float32
BF16 = jnp.bfloat16
MESH = pl.DeviceIdType.MESH

D_MODEL = 2048
DEPTH = 2
N_HEADS = 8
NOPE = 128
ROPE = 64
Q_LORA = 512
KV_LORA = 256
D_MLA = 1024
POOL_WINDOWS = (2, 4, 8, 16)
D_IN_PROJ = 4928
LN_EPS = 1e-5
RMS_EPS = 1e-6
ROPE_THETA = 10000.0
ALPHA = (2 * DEPTH) ** 0.25
SCALE = (NOPE + ROPE) ** -0.5
LOG2E = 1.4426950408889634
SCALE_LOG2E = SCALE * LOG2E
ADAM_LR = 0.001
ADAM_B1 = 0.9
ADAM_B2 = 0.999
ADAM_EPS = 1e-08
ADAM_WD = 0.01
ADAM_STEP = 10

NP = 5120
GAP_AT = 832
GAP = NP - D_IN_PROJ
W_MLA = 1024
W_MIX = NP - W_MLA
HALO = 16
LANE = 128
N_CHIPS = 4
N_DEV = 8
TQ = 512
FWD_GROUP = 4
BWD_GROUP = 3

NN = (((1,), (0,)), ((), ()))
NT = (((1,), (1,)), ((), ()))
TN = (((0,), (0,)), ((), ()))


CommScript = collections.namedtuple("CommScript", "args out_shape n_sems start finish")
HBM_SPEC = pl.BlockSpec(memory_space=pl.ANY)


def _pcall(kern, *, name, out_shape, grid=None, in_specs=None, out_specs=None, scratch=(), dims=None,
           vmem_mb=None, comm=None):
    cp = {}
    if dims is not None:
        cp["dimension_semantics"] = dims if comm is None else ("arbitrary",) * len(dims)
    if vmem_mb is not None:
        cp["vmem_limit_bytes"] = vmem_mb << 20
    if comm is None:
        args = dict(name=name, out_shape=out_shape, scratch_shapes=list(scratch),
                    compiler_params=pltpu.CompilerParams(**cp))
        if grid is not None:
            args["grid"] = grid
        if in_specs is not None:
            args["in_specs"] = in_specs
        if out_specs is not None:
            args["out_specs"] = out_specs
        return pl.pallas_call(kern, **args)

    single = not isinstance(out_shape, (tuple, list))
    own_out = (out_shape,) if single else tuple(out_shape)
    own_out_specs = (out_specs,) if single else tuple(out_specs)
    n_in, n_out, n_scr = len(in_specs), len(own_out), len(scratch)
    na, no = len(comm.args), len(comm.out_shape)

    def at(end):
        cond = None
        for d, n in enumerate(grid):
            here = pl.program_id(d) == (n - 1 if end else 0)
            cond = here if cond is None else jnp.logical_and(cond, here)
        return cond

    def wrapped(*refs):
        own_in, c_in = refs[:n_in], refs[n_in:n_in + na]
        o0 = n_in + na
        own_o, c_out = refs[o0:o0 + n_out], refs[o0 + n_out:o0 + n_out + no]
        s0 = o0 + n_out + no
        own_s, (send_sems, recv_sems) = refs[s0:s0 + n_scr], refs[s0 + n_scr:]

        @pl.when(at(False))
        def _():
            comm.start(c_in, c_out, send_sems, recv_sems)

        kern(*own_in, *own_o, *own_s)

        @pl.when(at(True))
        def _():
            comm.finish(c_in, c_out, send_sems, recv_sems)

    call = pl.pallas_call(
        wrapped, name=name, out_shape=own_out + tuple(comm.out_shape), grid=grid,
        in_specs=list(in_specs) + [HBM_SPEC] * na, out_specs=own_out_specs + (HBM_SPEC,) * no,
        scratch_shapes=list(scratch) + [pltpu.SemaphoreType.DMA((comm.n_sems,)),
                                        pltpu.SemaphoreType.DMA((comm.n_sems,))],
        compiler_params=pltpu.CompilerParams(**cp))

    def run(*args):
        res = call(*args, *comm.args)
        own = res[0] if single else tuple(res[:n_out])
        return own, tuple(res[n_out:])

    return run


def _run_comm(script, *, name):
    na, no = len(script.args), len(script.out_shape)

    def body(*refs):
        ins, outs = refs[:na], refs[na:na + no]
        send_sems, recv_sems = refs[na + no:]
        script.start(ins, outs, send_sems, recv_sems)
        script.finish(ins, outs, send_sems, recv_sems)

    return pl.pallas_call(
        body, name=name, out_shape=tuple(script.out_shape), in_specs=[HBM_SPEC] * na, out_specs=(HBM_SPEC,) * no,
        scratch_shapes=[pltpu.SemaphoreType.DMA((script.n_sems,)), pltpu.SemaphoreType.DMA((script.n_sems,))])(
            *script.args)


def _sigmoid(g):
    return 1.0 / (1.0 + jnp.exp(-g))


def _silu_and_grad(g):
    sig = _sigmoid(g)
    return g * sig, sig * (1.0 + g * (1.0 - sig))


def _matmul(a, b, mode, *, name, tm, tn, tk, out_dtype=F32, vmem_mb=48, comm=None):
    if mode == "nn":
        (M, K), N = a.shape, b.shape[1]
    elif mode == "nt":
        (M, K), N = a.shape, b.shape[0]
    else:
        (K, M), N = a.shape, b.shape[1]
    tm, tn, tk = min(tm, M), min(tn, N), min(tk, K)
    assert M % tm == 0 and N % tn == 0 and K % tk == 0, (name, M, N, K)
    nk = K // tk
    dn = {"nn": NN, "nt": NT, "tn": TN}[mode]
    if mode == "tn":
        a_spec = pl.BlockSpec((tk, tm), lambda i, j, k: (k, i))
    else:
        a_spec = pl.BlockSpec((tm, tk), lambda i, j, k: (i, k))
    if mode == "nt":
        b_spec = pl.BlockSpec((tn, tk), lambda i, j, k: (j, k))
    else:
        b_spec = pl.BlockSpec((tk, tn), lambda i, j, k: (k, j))
    o_spec = pl.BlockSpec((tm, tn), lambda i, j, k: (i, j))

    def kern(a_ref, b_ref, o_ref, *rest):
        part = lax.dot_general(a_ref[...].astype(BF16), b_ref[...].astype(BF16), dn,
                               preferred_element_type=F32)
        if nk == 1:
            o_ref[...] = part.astype(out_dtype)
        else:
            acc_ref = rest[0]
            k = pl.program_id(2)

            @pl.when(k == 0)
            def _():
                acc_ref[...] = part

            @pl.when(k > 0)
            def _():
                acc_ref[...] += part

            @pl.when(k == nk - 1)
            def _():
                o_ref[...] = acc_ref[...].astype(out_dtype)

    scratch = [pltpu.VMEM((tm, tn), F32)] if nk > 1 else []
    return _pcall(kern, name=name, out_shape=jax.ShapeDtypeStruct((M, N), out_dtype),
                  grid=(M // tm, N // tn, nk), in_specs=[a_spec, b_spec], out_specs=o_spec, scratch=scratch,
                  dims=("parallel", "parallel", "arbitrary"), vmem_mb=vmem_mb, comm=comm)(a, b)


def _dproj_times_w(d_mla, d_mix, wt, add, add_scale, *, name, comm=None):
    S = d_mla.shape[0]
    Dm = wt.shape[1]
    tm, tn, tk = min(1024, S), 1024, 2048
    nk = 1 + W_MIX // tk

    def kern(a1_ref, a2_ref, b1_ref, b2_ref, add_ref, o_ref, acc_ref):
        k = pl.program_id(2)

        @pl.when(k == 0)
        def _():
            acc_ref[...] = jnp.dot(a1_ref[...], b1_ref[...], preferred_element_type=F32)

        @pl.when(k > 0)
        def _():
            acc_ref[...] += jnp.dot(a2_ref[...], b2_ref[...], preferred_element_type=F32)

        @pl.when(k == nk - 1)
        def _():
            o_ref[...] = add_scale * add_ref[...] + acc_ref[...]

    o_spec = pl.BlockSpec((tm, tn), lambda i, j, k: (i, j))
    b2_spec = pl.BlockSpec((pl.Element(tk), pl.Element(tn)),
                           lambda i, j, k: (pl.multiple_of(W_MLA + tk * jnp.maximum(k - 1, 0), W_MLA),
                                            pl.multiple_of(j * tn, tn)))
    return _pcall(kern, name=name, out_shape=jax.ShapeDtypeStruct((S, Dm), F32), grid=(S // tm, Dm // tn, nk),
                  in_specs=[pl.BlockSpec((tm, W_MLA), lambda i, j, k: (i, 0)),
                            pl.BlockSpec((tm, tk), lambda i, j, k: (i, jnp.maximum(k - 1, 0))),
                            pl.BlockSpec((W_MLA, tn), lambda i, j, k: (0, j)), b2_spec, o_spec],
                  out_specs=o_spec, scratch=[pltpu.VMEM((tm, tn), F32)],
                  dims=("parallel", "parallel", "arbitrary"), vmem_mb=56, comm=comm)(d_mla, d_mix, wt, wt, add)


def _dproj_t_times_h(d_mla, d_mix, h, *, name, comm=None):
    S, Dm = h.shape
    tm, tn, tk = W_MLA, 1024, min(2048, S)
    nk = S // tk

    def kern(a1_ref, a2_ref, b_ref, o_ref, acc_ref):
        i = pl.program_id(0)
        k = pl.program_id(2)
        b = b_ref[...].astype(BF16)

        def accumulate(part):
            @pl.when(k == 0)
            def _():
                acc_ref[...] = part

            @pl.when(k > 0)
            def _():
                acc_ref[...] += part

        @pl.when(i == 0)
        def _():
            accumulate(lax.dot_general(a1_ref[...], b, TN, preferred_element_type=F32))

        @pl.when(i > 0)
        def _():
            accumulate(lax.dot_general(a2_ref[...], b, TN, preferred_element_type=F32))

        @pl.when(k == nk - 1)
        def _():
            o_ref[...] = acc_ref[...]

    return _pcall(kern, name=name, out_shape=jax.ShapeDtypeStruct((NP, Dm), F32), grid=(NP // tm, Dm // tn, nk),
                  in_specs=[pl.BlockSpec((tk, tm), lambda i, j, k: (jnp.where(i == 0, k, nk - 1), 0)),
                            pl.BlockSpec((tk, tm), lambda i, j, k: (jnp.where(i == 0, 0, k), jnp.maximum(i - 1, 0))),
                            pl.BlockSpec((tk, tn), lambda i, j, k: (k, j))],
                  out_specs=pl.BlockSpec((tm, tn), lambda i, j, k: (i, j)), scratch=[pltpu.VMEM((tm, tn), F32)],
                  dims=("parallel", "parallel", "arbitrary"), vmem_mb=48, comm=comm)(d_mla, d_mix, h)


def _ln_fwd(x, g, b, *, name, comm=None):
    S, Dm = x.shape
    tm = min(512, S)

    def kern(x_ref, g_ref, b_ref, y_ref, yb_ref):
        xf = x_ref[...]
        mu = jnp.mean(xf, axis=-1, keepdims=True)
        xc = xf - mu
        var = jnp.mean(xc * xc, axis=-1, keepdims=True)
        y = xc * lax.rsqrt(var + LN_EPS) * g_ref[...] + b_ref[...]
        y_ref[...] = y
        yb_ref[...] = y.astype(BF16)

    row = pl.BlockSpec((tm, Dm), lambda i: (i, 0))
    vec = pl.BlockSpec((1, Dm), lambda i: (0, 0))
    return _pcall(kern, name=name,
                  out_shape=(jax.ShapeDtypeStruct((S, Dm), F32), jax.ShapeDtypeStruct((S, Dm), BF16)),
                  grid=(S // tm,), in_specs=[row, vec, vec], out_specs=(row, row), dims=("parallel",), vmem_mb=48,
                  comm=comm)(
                      x, g.reshape(1, Dm), b.reshape(1, Dm))


def _ln_bwd(dy, r, g, *, name, bf16_copy=True):
    S, Dm = r.shape
    tm = min(512, S)

    def kern(dy_ref, r_ref, g_ref, dr_ref, *rest):
        drb_ref = rest[0] if bf16_copy else None
        dg_ref, db_ref, ds_ref = rest[-3:]

        @pl.when(pl.program_id(0) == 0)
        def _():
            dg_ref[...] = jnp.zeros_like(dg_ref)
            db_ref[...] = jnp.zeros_like(db_ref)
            ds_ref[...] = jnp.zeros_like(ds_ref)

        rf = r_ref[...]
        dyf = dy_ref[...]
        mu = jnp.mean(rf, axis=-1, keepdims=True)
        xc = rf - mu
        var = jnp.mean(xc * xc, axis=-1, keepdims=True)
        rstd = lax.rsqrt(var + LN_EPS)
        xhat = xc * rstd
        dxh = dyf * g_ref[...]
        c1 = jnp.mean(dxh, axis=-1, keepdims=True)
        c2 = jnp.mean(dxh * xhat, axis=-1, keepdims=True)
        dr = rstd * (dxh - c1 - xhat * c2)
        dr_ref[...] = dr
        if bf16_copy:
            drb_ref[...] = dr.astype(BF16)
        dg_ref[...] += jnp.sum(dyf * xhat, axis=0, keepdims=True)
        db_ref[...] += jnp.sum(dyf, axis=0, keepdims=True)
        ds_ref[...] += jnp.sum(dr, axis=0, keepdims=True)

    row = pl.BlockSpec((tm, Dm), lambda i: (i, 0))
    vec = pl.BlockSpec((1, Dm), lambda i: (0, 0))
    vshape = jax.ShapeDtypeStruct((1, Dm), F32)
    copies = ((jax.ShapeDtypeStruct((S, Dm), BF16),), (row,)) if bf16_copy else ((), ())
    res = _pcall(kern, name=name,
                 out_shape=(jax.ShapeDtypeStruct((S, Dm), F32),) + copies[0] + (vshape, vshape, vshape),
                 grid=(S // tm,), in_specs=[row, row, vec], out_specs=(row,) + copies[1] + (vec, vec, vec),
                 dims=("arbitrary",), vmem_mb=48)(dy, r, g.reshape(1, Dm))
    return res if bf16_copy else (res[0], None) + tuple(res[1:])


def _loss_ln_bwd(target, r, g, b, *, name):
    S, Dm = r.shape
    tm = min(512, S)

    def kern(t_ref, r_ref, g_ref, b_ref, l_ref, dr_ref, drb_ref, dg_ref, db_ref, ds_ref):
        @pl.when(pl.program_id(0) == 0)
        def _():
            l_ref[...] = jnp.zeros_like(l_ref)
            dg_ref[...] = jnp.zeros_like(dg_ref)
            db_ref[...] = jnp.zeros_like(db_ref)
            ds_ref[...] = jnp.zeros_like(ds_ref)

        rf = r_ref[...]
        mu = jnp.mean(rf, axis=-1, keepdims=True)
        xc = rf - mu
        var = jnp.mean(xc * xc, axis=-1, keepdims=True)
        rstd = lax.rsqrt(var + LN_EPS)
        xhat = xc * rstd
        e = (xhat * g_ref[...] + b_ref[...]) - t_ref[...]
        dyf = e / float(Dm)
        per_row = jnp.mean(e * e, axis=-1, keepdims=True)
        l_ref[...] += 0.5 * jnp.sum(per_row, axis=0, keepdims=True)
        dxh = dyf * g_ref[...]
        c1 = jnp.mean(dxh, axis=-1, keepdims=True)
        c2 = jnp.mean(dxh * xhat, axis=-1, keepdims=True)
        dr = rstd * (dxh - c1 - xhat * c2)
        dr_ref[...] = dr
        drb_ref[...] = dr.astype(BF16)
        dg_ref[...] += jnp.sum(dyf * xhat, axis=0, keepdims=True)
        db_ref[...] += jnp.sum(dyf, axis=0, keepdims=True)
        ds_ref[...] += jnp.sum(dr, axis=0, keepdims=True)

    row = pl.BlockSpec((tm, Dm), lambda i: (i, 0))
    vec = pl.BlockSpec((1, Dm), lambda i: (0, 0))
    acc = pl.BlockSpec((8, LANE), lambda i: (0, 0))
    vshape = jax.ShapeDtypeStruct((1, Dm), F32)
    return _pcall(kern, name=name,
                  out_shape=(jax.ShapeDtypeStruct((8, LANE), F32), jax.ShapeDtypeStruct((S, Dm), F32),
                             jax.ShapeDtypeStruct((S, Dm), BF16), vshape, vshape, vshape),
                  grid=(S // tm,), in_specs=[row, row, vec, vec], out_specs=(acc, row, row, vec, vec, vec),
                  dims=("arbitrary",), vmem_mb=56)(target, r, g.reshape(1, Dm), b.reshape(1, Dm))


def _rot_sum(t):
    return pltpu.roll(t, 32, 1) + pltpu.roll(t, 96, 1)


def _mla_qkv(proj, cos_t, sin_t, qg, kvg, wuq_t, wukv_t, *, name):
    S = proj.shape[0]
    tm = min(256, S)

    def kern(ql_ref, kvl_ref, kr_ref, cos_ref, sin_ref, qg_ref, kvg_ref, wuq_ref, wukv_ref,
             qc_ref, kc_ref, v_ref, vt_ref, qn_ref, kvn_ref):
        cosv = cos_ref[...]
        sinv = sin_ref[...]

        def rope(t):
            return t * cosv + _rot_sum(t) * sinv

        ql = ql_ref[...]
        qn = (ql * lax.rsqrt(jnp.mean(ql * ql, axis=-1, keepdims=True) + RMS_EPS) * qg_ref[...]).astype(BF16)
        kvl = kvl_ref[...]
        kvn = (kvl * lax.rsqrt(jnp.mean(kvl * kvl, axis=-1, keepdims=True) + RMS_EPS) * kvg_ref[...]).astype(BF16)
        qn_ref[...] = qn
        kvn_ref[...] = kvn
        q = lax.dot_general(qn, wuq_ref[...], NT, preferred_element_type=F32)
        kv = lax.dot_general(kvn, wukv_ref[...], NT, preferred_element_type=F32)
        kr = rope(kr_ref[...]).astype(BF16)
        for h in range(N_HEADS):
            c0 = 256 * h
            qc_ref[:, c0:c0 + 128] = q[:, c0:c0 + 128].astype(BF16)
            qc_ref[:, c0 + 128:c0 + 256] = rope(q[:, c0 + 128:c0 + 256]).astype(BF16)
            kc_ref[:, c0:c0 + 128] = kv[:, c0:c0 + 128].astype(BF16)
            kc_ref[:, c0 + 128:c0 + 256] = kr
            vh = kv[:, c0 + 128:c0 + 256]
            v_ref[:, 128 * h:128 * h + 128] = vh.astype(BF16)
            vt_ref[h] = jnp.transpose(vh).astype(BF16)

    def row(w, blk):
        return pl.BlockSpec((tm, w), lambda i: (i, blk))

    def full(shape):
        return pl.BlockSpec(shape, lambda i: (0,) * len(shape))

    t = min(TQ, S)
    per = t // tm
    vt_spec = pl.BlockSpec((N_HEADS, None, 128, tm), lambda i: (0, i // per, 0, i % per))
    outs = (jax.ShapeDtypeStruct((S, 2048), BF16), jax.ShapeDtypeStruct((S, 2048), BF16),
            jax.ShapeDtypeStruct((S, 1024), BF16), jax.ShapeDtypeStruct((N_HEADS, S // t, 128, t), BF16),
            jax.ShapeDtypeStruct((S, Q_LORA), BF16), jax.ShapeDtypeStruct((S, KV_LORA), BF16))
    return _pcall(kern, name=name, out_shape=outs, grid=(S // tm,),
                  in_specs=[row(512, 0), row(256, 2), row(128, 6), row(128, 0), row(128, 0),
                            full((1, Q_LORA)), full((1, KV_LORA)), full((2048, Q_LORA)), full((2048, KV_LORA))],
                  out_specs=(row(2048, 0), row(2048, 0), row(1024, 0), vt_spec, row(512, 0), row(256, 0)),
                  dims=("parallel",), vmem_mb=48)(
                      proj, proj, proj, cos_t, sin_t, qg.reshape(1, -1), kvg.reshape(1, -1), wuq_t, wukv_t)


def _mla_qkv_bwd(dqb, dkvb, dkr_heads, proj, cos_t, sin_t, qg, kvg, wuq_t, wukv_t, *, name):
    S = proj.shape[0]
    tm = min(256, S)

    def kern(dqb_ref, dkvb_ref, dkrh_ref, ql_ref, kvl_ref, cos_ref, sin_ref, qg_ref, kvg_ref, wuq_ref, wukv_ref,
             dml_ref, dqg_ref, dkvg_ref):
        @pl.when(pl.program_id(0) == 0)
        def _():
            dqg_ref[...] = jnp.zeros_like(dqg_ref)
            dkvg_ref[...] = jnp.zeros_like(dkvg_ref)

        cosv = cos_ref[...]
        sinv = sin_ref[...]

        def unrope(t):
            return t * cosv - _rot_sum(t) * sinv

        dkr = dkrh_ref[:, 0:128]
        for h in range(1, N_HEADS):
            dkr = dkr + dkrh_ref[:, 128 * h:128 * h + 128]

        def rms_bwd(x, g, dy):
            n = x.shape[-1]
            rs = lax.rsqrt(jnp.mean(x * x, axis=-1, keepdims=True) + RMS_EPS)
            dyg = dy * g
            dx = rs * dyg - x * (rs * rs * rs) * (jnp.sum(dyg * x, axis=-1, keepdims=True) / n)
            return dx, jnp.sum(dy * (x * rs), axis=0, keepdims=True)

        dqn = jnp.dot(dqb_ref[...], wuq_ref[...], preferred_element_type=F32)
        dql, dqg = rms_bwd(ql_ref[...], qg_ref[...], dqn)
        dqg_ref[...] += dqg
        dkvn = jnp.dot(dkvb_ref[...], wukv_ref[...], preferred_element_type=F32)
        dkvl, dkvg = rms_bwd(kvl_ref[...], kvg_ref[...], dkvn)
        dkvg_ref[...] += dkvg
        dml_ref[:, 0:512] = dql.astype(BF16)
        dml_ref[:, 512:768] = dkvl.astype(BF16)
        dml_ref[:, 768:896] = unrope(dkr).astype(BF16)
        dml_ref[:, 896:1024] = jnp.zeros((tm, 128), BF16)

    def row(w, blk):
        return pl.BlockSpec((tm, w), lambda i: (i, blk))

    def full(shape):
        return pl.BlockSpec(shape, lambda i: (0,) * len(shape))

    outs = (jax.ShapeDtypeStruct((S, W_MLA), BF16), jax.ShapeDtypeStruct((1, Q_LORA), F32),
            jax.ShapeDtypeStruct((1, KV_LORA), F32))
    return _pcall(kern, name=name, out_shape=outs, grid=(S // tm,),
                  in_specs=[row(2048, 0), row(2048, 0), row(1024, 0), row(512, 0), row(256, 2),
                            row(128, 0), row(128, 0), full((1, Q_LORA)), full((1, KV_LORA)),
                            full((2048, Q_LORA)), full((2048, KV_LORA))],
                  out_specs=(row(W_MLA, 0), full((1, Q_LORA)), full((1, KV_LORA))),
                  dims=("arbitrary",), vmem_mb=56)(
                      dqb, dkvb, dkr_heads, proj, proj, cos_t, sin_t, qg.reshape(1, -1), kvg.reshape(1, -1),
                      wuq_t, wukv_t)


def _flash_fwd(qc, kc, vt, *, name, comm=None):
    S = qc.shape[0]
    t = min(TQ, S)
    n = S // t

    def kern(q_ref, k_ref, vt_ref, o_ref, lse_ref, m_s, l_s, acc_s):
        qi = pl.program_id(1)
        m_s[...] = jnp.full_like(m_s, -jnp.inf)
        l_s[...] = jnp.zeros_like(l_s)
        acc_s[...] = jnp.zeros_like(acc_s)

        half = t // 2

        def scores(kb, q_lo=0, q_n=t, k_n=t):
            k0 = pl.multiple_of(kb * t, t)
            return lax.dot_general(k_ref[pl.ds(k0, k_n), :], q_ref[q_lo:q_lo + q_n, :], NT,
                                   preferred_element_type=F32)

        def update(kb, st, q_lo=0, diagonal=False):
            k_n, q_n = st.shape
            if diagonal:
                krow = lax.broadcasted_iota(jnp.int32, (k_n, q_n), 0)
                qcol = lax.broadcasted_iota(jnp.int32, (k_n, q_n), 1) + q_lo
                st = jnp.where(krow <= qcol, st, -jnp.inf)
            lanes = slice(q_lo, q_lo + q_n)
            m_prev = m_s[:, lanes]
            m_new = jnp.maximum(m_prev, jnp.max(st, axis=0, keepdims=True))
            a = jnp.exp2((m_prev - m_new) * SCALE_LOG2E)
            pt = jnp.exp2((st - m_new) * SCALE_LOG2E)
            l_s[:, lanes] = a * l_s[:, lanes] + jnp.sum(pt, axis=0, keepdims=True)
            acc_s[:, lanes] = a * acc_s[:, lanes] + jnp.dot(vt_ref[kb, :, 0:k_n], pt.astype(BF16),
                                                            preferred_element_type=F32)
            m_s[:, lanes] = m_new

        def group(kb, count, last_diagonal):
            whole = count - 1 if last_diagonal else count
            sts = [scores(kb + g) for g in range(whole)]
            if last_diagonal:
                kd = kb + count - 1
                s_lo, s_hi = scores(kd, 0, half, half), scores(kd, half, half, t)
            for g in range(whole):
                update(kb + g, sts[g])
            if last_diagonal:
                update(kd, s_lo, 0, True)
                update(kd, s_hi, half, True)

        def body(i, carry):
            group(FWD_GROUP * i, FWD_GROUP, False)
            return carry

        full = qi // FWD_GROUP
        lax.fori_loop(0, full, body, 0)
        for rem in range(FWD_GROUP):
            @pl.when(qi - FWD_GROUP * full == rem)
            def _():
                group(qi - rem, rem + 1, True)
        o_ref[...] = jnp.transpose(acc_s[...] / l_s[...])
        lse_ref[pl.ds(qi, 1), :] = m_s[...] * SCALE_LOG2E + jnp.log2(l_s[...])

    q_spec = pl.BlockSpec((t, 256), lambda h, qi: (qi, h))
    k_spec = pl.BlockSpec((S, 256), lambda h, qi: (0, h))
    vt_spec = pl.BlockSpec((None, n, 128, t), lambda h, qi: (h, 0, 0, 0))
    o_spec = pl.BlockSpec((t, 128), lambda h, qi: (qi, h))
    lse_spec = pl.BlockSpec((None, n, t), lambda h, qi: (h, 0, 0))
    return _pcall(kern, name=name,
                  out_shape=(jax.ShapeDtypeStruct((S, D_MLA), F32), jax.ShapeDtypeStruct((N_HEADS, n, t), F32)),
                  grid=(N_HEADS, n), in_specs=[q_spec, k_spec, vt_spec], out_specs=(o_spec, lse_spec),
                  scratch=[pltpu.VMEM((1, t), F32), pltpu.VMEM((1, t), F32), pltpu.VMEM((128, t), F32)],
                  dims=("parallel", "arbitrary"), vmem_mb=48, comm=comm)(qc, kc, vt)


def _flash_bwd(qc, kc, v, do, lse2, delta, cos_t, sin_t, *, name, comm=None):
    S = qc.shape[0]
    t = min(TQ, S)
    n = S // t

    def kern(q_ref, k_ref, v_ref, do_ref, lse_ref, dl_ref, cos_ref, sin_ref, dqb_ref, dkvb_ref, dkr_ref,
             dq_ref, dk_ref, dv_ref):
        ki = pl.program_id(1)

        @pl.when(ki == 0)
        def _():
            dq_ref[...] = jnp.zeros_like(dq_ref)

        dk_ref[...] = jnp.zeros_like(dk_ref)
        dv_ref[...] = jnp.zeros_like(dv_ref)

        half = t // 2

        def step(qb, q_lo=0, q_n=t, k_n=t, diagonal=False):
            q0 = pl.multiple_of(qb * t + q_lo, half)
            lanes = slice(q_lo, q_lo + q_n)
            kt = k_ref[0:k_n, :]
            qblk = q_ref[pl.ds(q0, q_n), :]
            dob = do_ref[pl.ds(q0, q_n), :].astype(BF16)
            st = lax.dot_general(kt, qblk, NT, preferred_element_type=F32)
            pt = jnp.exp2(st * SCALE_LOG2E - lse_ref[pl.ds(qb, 1), lanes])
            if diagonal:
                krow = lax.broadcasted_iota(jnp.int32, (k_n, q_n), 0)
                qcol = lax.broadcasted_iota(jnp.int32, (k_n, q_n), 1) + q_lo
                pt = jnp.where(krow <= qcol, pt, 0.0)
            dv_ref[0:k_n, :] += jnp.dot(pt.astype(BF16), dob, preferred_element_type=F32)
            dpt = lax.dot_general(v_ref[0:k_n, :], dob, NT, preferred_element_type=F32)
            dst = (pt * (dpt - dl_ref[pl.ds(qb, 1), lanes]) * SCALE).astype(BF16)
            dk_ref[0:k_n, :] += jnp.dot(dst, qblk, preferred_element_type=F32)
            dq_ref[pl.ds(q0, q_n), :] += lax.dot_general(dst, kt, TN, preferred_element_type=F32)

        step(ki, 0, half, half, True)
        step(ki, half, half, t, True)
        rest = n - 1 - ki
        full = rest // BWD_GROUP

        def body(i, carry):
            for g in range(BWD_GROUP):
                step(ki + 1 + BWD_GROUP * i + g)
            return carry

        lax.fori_loop(0, full, body, 0)
        for rem in range(1, BWD_GROUP):
            @pl.when(rest - BWD_GROUP * full == rem)
            def _():
                for g in range(rem):
                    step(n - rem + g)

        dkvb_ref[:, 0:128] = dk_ref[:, 0:128].astype(BF16)
        dkvb_ref[:, 128:256] = dv_ref[...].astype(BF16)
        dkr_ref[...] = dk_ref[:, 128:256]

        @pl.when(ki == n - 1)
        def _():
            dqb_ref[:, 0:128] = dq_ref[:, 0:128].astype(BF16)
            dqr = dq_ref[:, 128:256]
            dqb_ref[:, 128:256] = (dqr * cos_ref[...] - _rot_sum(dqr) * sin_ref[...]).astype(BF16)

    def whole(w):
        return pl.BlockSpec((S, w), lambda h, ki: (0, h))

    def krow(w):
        return pl.BlockSpec((t, w), lambda h, ki: (ki, h))

    stat = pl.BlockSpec((None, n, t), lambda h, ki: (h, 0, 0))
    table = pl.BlockSpec((S, 128), lambda h, ki: (0, 0))
    return _pcall(kern, name=name,
                  out_shape=(jax.ShapeDtypeStruct((S, 2048), BF16), jax.ShapeDtypeStruct((S, 2048), BF16),
                             jax.ShapeDtypeStruct((S, D_MLA), F32)),
                  grid=(N_HEADS, n),
                  in_specs=[whole(256), krow(256), krow(128), whole(128), stat, stat, table, table],
                  out_specs=(whole(256), krow(256), krow(128)),
                  scratch=[pltpu.VMEM((S, 256), F32), pltpu.VMEM((t, 256), F32), pltpu.VMEM((t, 128), F32)],
                  dims=("parallel", "arbitrary"), vmem_mb=56, comm=comm)(qc, kc, v, do, lse2, delta, cos_t, sin_t)


def _mixer_specs(S, tm):
    hb = tm // HALO
    last_hb = S // HALO - 1

    def main(w, blk):
        return pl.BlockSpec((tm, w), lambda i: (i, blk))

    def prev(w, blk):
        return pl.BlockSpec((HALO, w), lambda i: (jnp.maximum(i * hb - 1, 0), blk))

    def nxt(w, blk):
        return pl.BlockSpec((HALO, w), lambda i: (jnp.minimum((i + 1) * hb, last_hb), blk))

    def full(shape):
        return pl.BlockSpec(shape, lambda i: (0,) * len(shape))

    return main, prev, nxt, full


def _fill_halo(i, xp, xu, hp_ref, hch_ref, hcc_ref, pin_ref, ch_ref, cc_ref, tm):
    first = i == 0
    xp[0:HALO, :] = jnp.where(first, 0.0, hp_ref[...])
    xp[HALO:HALO + tm, :] = pin_ref[...]
    xu[0:HALO, :] = jnp.where(first, 0.0, hch_ref[...] * hcc_ref[...])
    xu[HALO:HALO + tm, :] = cc_ref[...] * ch_ref[...]


def _pooled(xp, g, t1, tm):
    w = POOL_WINDOWS[g]
    lanes = slice(128 * g, 128 * g + 128)
    x0 = xp[HALO:HALO + tm, lanes]
    acc = x0
    for k in range(1, w):
        acc = acc + xp[HALO - k:HALO - k + tm, lanes]
    return acc / jnp.minimum(t1, float(w)) - x0


def _conv_fwd(xu, cw_ref, tm):
    return (cw_ref[0:1, :] * xu[HALO - 2:HALO - 2 + tm, :] + cw_ref[1:2, :] * xu[HALO - 1:HALO - 1 + tm, :]
            + cw_ref[2:3, :] * xu[HALO:HALO + tm, :])


def _mixer_fwd(proj, o, wpool, ps, convw, *, name):
    S = proj.shape[0]
    tm = min(256, S)
    main, prev, _, full = _mixer_specs(S, tm)

    def kern(gm_ref, pin_ref, gp_ref, ch_ref, cb_ref, cc_ref, gc_ref, hp_ref, hch_ref, hcc_ref,
             o_ref, wp_ref, ps_ref, cw_ref, mix_ref, xp, xu):
        i = pl.program_id(0)
        _fill_halo(i, xp, xu, hp_ref, hch_ref, hcc_ref, pin_ref, ch_ref, cc_ref, tm)
        t1 = (i * tm + lax.broadcasted_iota(jnp.int32, (tm, 1), 0) + 1).astype(F32)
        for g in range(4):
            lanes = slice(128 * g, 128 * g + 128)
            pooled = _pooled(xp, g, t1, tm)
            z = jnp.dot(pooled.astype(BF16), wp_ref[g].astype(BF16), preferred_element_type=F32)
            gp = gp_ref[:, lanes]
            y = z * ps_ref[:, lanes] * (gp * _sigmoid(gp))
            mix_ref[:, 1024 + 128 * g:1024 + 128 * g + 128] = y.astype(BF16)
        gc = gc_ref[...]
        mix_ref[:, 1536:2048] = (cb_ref[...] * _conv_fwd(xu, cw_ref, tm) * (gc * _sigmoid(gc))).astype(BF16)
        gm = gm_ref[...]
        mix_ref[:, 0:1024] = (o_ref[...] * (gm * _sigmoid(gm))).astype(BF16)

    return _pcall(kern, name=name, out_shape=jax.ShapeDtypeStruct((S, 2048), BF16), grid=(S // tm,),
                  in_specs=[main(1024, 1), main(512, 4), main(512, 5), main(512, 6), main(512, 7), main(512, 8),
                            main(512, 9), prev(512, 4), prev(512, 6), prev(512, 8),
                            main(1024, 0), full((4, 128, 128)), full((1, 512)), full((3, 512))],
                  out_specs=main(2048, 0),
                  scratch=[pltpu.VMEM((tm + HALO, 512), F32), pltpu.VMEM((tm + HALO, 512), F32)],
                  dims=("parallel",), vmem_mb=48)(
                      proj, proj, proj, proj, proj, proj, proj, proj, proj, proj, o, wpool, ps.reshape(1, 512), convw)


def _mixer_bwd(dmix, proj, o, wpool, ps, convw, *, name):
    S = proj.shape[0]
    tm = min(256, S)
    n = S // tm
    t = min(TQ, S)
    per = t // tm
    main, prev, nxt, full = _mixer_specs(S, tm)

    def kern(dm_ref, dmn_ref, gm_ref, pin_ref, gp_ref, ch_ref, cb_ref, cc_ref, gc_ref,
             hp_ref, hch_ref, hcc_ref, gpn_ref, cbn_ref, gcn_ref, o_ref, wp_ref, ps_ref, cw_ref,
             d_ref, do_ref, dl_ref, dwp_ref, dps_ref, dcw_ref, xp, xu, ee, ed):
        i = pl.program_id(0)
        last = i == n - 1

        @pl.when(i == 0)
        def _():
            dwp_ref[...] = jnp.zeros_like(dwp_ref)
            dps_ref[...] = jnp.zeros_like(dps_ref)
            dcw_ref[...] = jnp.zeros_like(dcw_ref)

        _fill_halo(i, xp, xu, hp_ref, hch_ref, hcc_ref, pin_ref, ch_ref, cc_ref, tm)
        t1 = (i * tm + lax.broadcasted_iota(jnp.int32, (tm, 1), 0) + 1).astype(F32)
        t1n = ((i + 1) * tm + lax.broadcasted_iota(jnp.int32, (HALO, 1), 0) + 1).astype(F32)
        c_pin, c_gp, c_ch, c_cb, c_cc, c_gc = 1024, 1536, 2048, 2560, 3072, 3584

        for g in range(4):
            w = float(POOL_WINDOWS[g])
            lanes = slice(128 * g, 128 * g + 128)
            pooled = _pooled(xp, g, t1, tm)
            pb = pooled.astype(BF16)
            wp = wp_ref[g].astype(BF16)
            z = jnp.dot(pb, wp, preferred_element_type=F32)
            psl = ps_ref[:, lanes]
            sg, dsg = _silu_and_grad(gp_ref[:, lanes])
            dmp = dm_ref[:, 1024 + 128 * g:1024 + 128 * g + 128]
            dyp = dmp * sg
            d_ref[:, c_gp + 128 * g:c_gp + 128 * g + 128] = (dmp * (z * psl) * dsg).astype(BF16)
            dps_ref[:, lanes] += jnp.sum(dyp * z, axis=0, keepdims=True)
            dz = (dyp * psl).astype(BF16)
            dwp_ref[g] += lax.dot_general(pb, dz, TN, preferred_element_type=F32)
            dpl = lax.dot_general(dz, wp, NT, preferred_element_type=F32)
            ee[0:tm, lanes] = dpl / jnp.minimum(t1, w)
            gpn = gpn_ref[:, lanes]
            dzn = (dmn_ref[:, lanes] * (gpn * _sigmoid(gpn)) * psl).astype(BF16)
            dpn = lax.dot_general(dzn, wp, NT, preferred_element_type=F32)
            ee[tm:tm + HALO, lanes] = jnp.where(last, 0.0, dpn / jnp.minimum(t1n, w))
            acc = ee[0:tm, lanes]
            for k in range(1, POOL_WINDOWS[g]):
                acc = acc + ee[k:k + tm, lanes]
            d_ref[:, c_pin + 128 * g:c_pin + 128 * g + 128] = (acc - dpl).astype(BF16)

        yc = _conv_fwd(xu, cw_ref, tm)
        sgc, dsgc = _silu_and_grad(gc_ref[...])
        cb = cb_ref[...]
        dmc = dm_ref[:, 1536:2048]
        d_ref[:, c_gc:c_gc + 512] = (dmc * cb * yc * dsgc).astype(BF16)
        d_ref[:, c_cb:c_cb + 512] = (dmc * yc * sgc).astype(BF16)
        dyc = dmc * cb * sgc
        ed[0:tm, :] = dyc
        gcn = gcn_ref[...]
        ed[tm:tm + HALO, :] = jnp.where(last, 0.0, dmn_ref[:, 512:1024] * cbn_ref[...] * (gcn * _sigmoid(gcn)))
        dcw_ref[0:1, :] += jnp.sum(dyc * xu[HALO - 2:HALO - 2 + tm, :], axis=0, keepdims=True)
        dcw_ref[1:2, :] += jnp.sum(dyc * xu[HALO - 1:HALO - 1 + tm, :], axis=0, keepdims=True)
        dcw_ref[2:3, :] += jnp.sum(dyc * xu[HALO:HALO + tm, :], axis=0, keepdims=True)
        du = cw_ref[2:3, :] * dyc + cw_ref[1:2, :] * ed[1:1 + tm, :] + cw_ref[0:1, :] * ed[2:2 + tm, :]
        d_ref[:, c_cc:c_cc + 512] = (du * ch_ref[...]).astype(BF16)
        d_ref[:, c_ch:c_ch + 512] = (du * cc_ref[...]).astype(BF16)

        sgm, dsgm = _silu_and_grad(gm_ref[...])
        dmm = dm_ref[:, 0:1024]
        ov = o_ref[...]
        dov = dmm * sgm
        do_ref[...] = dov
        d_ref[:, 0:1024] = (dmm * ov * dsgm).astype(BF16)
        lane = lax.broadcasted_iota(jnp.int32, (tm, LANE), 1)
        dmat = jnp.zeros((tm, LANE), F32)
        for h in range(N_HEADS):
            hs = slice(128 * h, 128 * h + 128)
            dmat = jnp.where(lane == h, jnp.sum(dov[:, hs] * ov[:, hs], axis=1, keepdims=True), dmat)
        dmat_t = jnp.transpose(dmat)
        for part in range(per):
            @pl.when(i % per == part)
            def _():
                for h in range(N_HEADS):
                    dl_ref[h, pl.ds(i // per, 1), part * tm:(part + 1) * tm] = dmat_t[h:h + 1, :]

    outs = (jax.ShapeDtypeStruct((S, W_MIX), BF16), jax.ShapeDtypeStruct((S, 1024), F32),
            jax.ShapeDtypeStruct((N_HEADS, S // t, t), F32),
            jax.ShapeDtypeStruct((4, 128, 128), F32), jax.ShapeDtypeStruct((1, 512), F32),
            jax.ShapeDtypeStruct((3, 512), F32))
    scr = [pltpu.VMEM((tm + HALO, 512), F32) for _ in range(4)]
    return _pcall(kern, name=name, out_shape=outs, grid=(n,),
                  in_specs=[main(2048, 0), nxt(1024, 1),
                            main(1024, 1), main(512, 4), main(512, 5), main(512, 6), main(512, 7), main(512, 8),
                            main(512, 9), prev(512, 4), prev(512, 6), prev(512, 8),
                            nxt(512, 5), nxt(512, 7), nxt(512, 9),
                            main(1024, 0), full((4, 128, 128)), full((1, 512)), full((3, 512))],
                  out_specs=(main(W_MIX, 0), main(1024, 0), full((N_HEADS, S // t, t)), full((4, 128, 128)),
                             full((1, 512)), full((3, 512))),
                  scratch=scr, dims=("arbitrary",), vmem_mb=56)(
                      dmix, dmix, proj, proj, proj, proj, proj, proj, proj, proj, proj, proj, proj, proj, proj,
                      o, wpool, ps.reshape(1, 512), convw)


def _outproj_residual(mix, wout, h, bout, *, name):
    S, Dm = h.shape
    tm = min(512, S)

    def kern(mix_ref, w_ref, h_ref, bo_ref, r_ref):
        out = jnp.dot(mix_ref[...], w_ref[...], preferred_element_type=F32) + bo_ref[...]
        r_ref[...] = ALPHA * h_ref[...] + out

    row = pl.BlockSpec((tm, Dm), lambda i: (i, 0))
    vec = pl.BlockSpec((1, Dm), lambda i: (0, 0))
    wsp = pl.BlockSpec((Dm, Dm), lambda i: (0, 0), pipeline_mode=pl.Buffered(1))
    return _pcall(kern, name=name, out_shape=jax.ShapeDtypeStruct((S, Dm), F32), grid=(S // tm,),
                  in_specs=[row, wsp, row, vec], out_specs=row, dims=("parallel",), vmem_mb=56)(
                      mix, wout, h, bout.reshape(1, Dm))


def _outproj_ln(mix, wout, h, bout, g, b, *, name):
    S, Dm = h.shape
    tm = min(512, S)

    def kern(mix_ref, w_ref, h_ref, bo_ref, g_ref, b_ref, y_ref, yb_ref, r_ref):
        out = jnp.dot(mix_ref[...], w_ref[...], preferred_element_type=F32) + bo_ref[...]
        r = ALPHA * h_ref[...] + out
        r_ref[...] = r
        mu = jnp.mean(r, axis=-1, keepdims=True)
        xc = r - mu
        var = jnp.mean(xc * xc, axis=-1, keepdims=True)
        y = xc * lax.rsqrt(var + LN_EPS) * g_ref[...] + b_ref[...]
        y_ref[...] = y
        yb_ref[...] = y.astype(BF16)

    row = pl.BlockSpec((tm, Dm), lambda i: (i, 0))
    vec = pl.BlockSpec((1, Dm), lambda i: (0, 0))
    wsp = pl.BlockSpec((Dm, Dm), lambda i: (0, 0), pipeline_mode=pl.Buffered(1))
    sds = jax.ShapeDtypeStruct((S, Dm), F32)
    return _pcall(kern, name=name, out_shape=(sds, jax.ShapeDtypeStruct((S, Dm), BF16), sds), grid=(S // tm,),
                  in_specs=[row, wsp, row, vec, vec, vec], out_specs=(row, row, row), dims=("parallel",),
                  vmem_mb=56)(
                      mix, wout, h, bout.reshape(1, Dm), g.reshape(1, Dm), b.reshape(1, Dm))


def _adamw_math(w, g, m, v):
    m = ADAM_B1 * m + (1.0 - ADAM_B1) * g
    v = ADAM_B2 * v + (1.0 - ADAM_B2) * (g * g)
    m_hat = m / (1.0 - ADAM_B1 ** ADAM_STEP)
    v_hat = v / (1.0 - ADAM_B2 ** ADAM_STEP)
    delta = -ADAM_LR * (m_hat / (jnp.sqrt(v_hat) + ADAM_EPS) + ADAM_WD * w)
    return delta, m, v


def _row_tile(R, C):
    best = None
    for cand in range(8, R, 8):
        if R % cand == 0 and cand * C <= 256 * 1024:
            best = cand
    return best if best is not None else R


def _adamw(w, g, m, v, *, name):
    shape = w.shape
    C = shape[-1]
    R = 1
    for s in shape[:-1]:
        R *= s
    tr = _row_tile(R, C)

    def kern(w_ref, g_ref, m_ref, v_ref, d_ref, mo_ref, vo_ref):
        d, mn, vn = _adamw_math(w_ref[...], g_ref[...], m_ref[...], v_ref[...])
        d_ref[...] = d
        mo_ref[...] = mn
        vo_ref[...] = vn

    blk = pl.BlockSpec((tr, C), lambda i: (i, 0))
    sds = jax.ShapeDtypeStruct((R, C), F32)
    outs = _pcall(kern, name=name, out_shape=(sds, sds, sds), grid=(R // tr,), in_specs=[blk] * 4,
                  out_specs=(blk, blk, blk), dims=("parallel",), vmem_mb=48)(
                      w.reshape(R, C), g.reshape(R, C), m.reshape(R, C), v.reshape(R, C))
    return tuple(t.reshape(shape) for t in outs)


def _adamw_halves(w, m, v, halves, c_idx, *, name, comm=None):
    _, R, C = w.shape
    ch = C // 2
    tr = _row_tile(R, ch)
    nb = R // tr

    def kern(c_ref, w_ref, a0_ref, b0_ref, a1_ref, b1_ref, m_ref, v_ref, g_ref, d_ref, mo_ref, vo_ref):
        layer = pl.program_id(0) // nb
        mine = pl.program_id(1) == c_ref[0]
        g = jnp.where(layer == 0, jnp.where(mine, a0_ref[...], b0_ref[...]),
                      jnp.where(mine, a1_ref[...], b1_ref[...]))
        g_ref[...] = g
        d, mn, vn = _adamw_math(w_ref[...], g, m_ref[...], v_ref[...])
        d_ref[...] = d
        mo_ref[...] = mn
        vo_ref[...] = vn

    full = pl.BlockSpec((tr, ch), lambda i, hc: (i, hc))
    half = pl.BlockSpec((tr, ch), lambda i, hc: (i % nb, 0))
    sds = jax.ShapeDtypeStruct((2 * R, C), F32)
    (a0, b0), (a1, b1) = halves
    res = _pcall(kern, name=name, out_shape=(sds,) * 4, grid=(2 * nb, 2),
                 in_specs=[pl.BlockSpec(memory_space=pltpu.SMEM), full, half, half, half, half, full, full],
                 out_specs=(full,) * 4, dims=("parallel", "parallel"), vmem_mb=48, comm=comm)(
                     c_idx, w.reshape(2 * R, C), a0, b0, a1, b1, m.reshape(2 * R, C), v.reshape(2 * R, C))
    outs, landed = res if comm is not None else (res, None)
    outs = tuple(t.reshape(2, R, C) for t in outs)
    return outs if comm is None else (outs, landed)


def _packed_pieces(shape):
    if len(shape) == 4:
        return [((l * shape[1] + g) * 128, 128, (l, g)) for l in range(shape[0]) for g in range(shape[1])]
    per_row = shape[1] // LANE
    return [(a * per_row + j, 1, (slice(a, a + 1), slice(LANE * j, LANE * (j + 1))))
            for a in range(shape[0]) for j in range(per_row)]


def _small_sum_adamw(gathered, own, weights, *, name):
    R = gathered.shape[1]
    nw = len(weights)
    shapes = [w.shape for w, _, _ in weights]
    first_row, r0 = [], 0
    for shp in shapes:
        first_row.append(r0)
        n = 1
        for s in shp:
            n *= s
        r0 += n // LANE

    def kern(ga_ref, own_ref, *refs):
        ins, gsum_ref, outs = refs[:3 * nw], refs[3 * nw], refs[3 * nw + 1:]
        me = 4 * lax.axis_index("x") + 2 * lax.axis_index("y") + lax.axis_index("c")

        def block(k):
            other = ga_ref[jnp.where(me == k, (k + 1) % N_DEV, k)]
            return jnp.where(me == k, own_ref[...], other)

        g = block(0)
        for k in range(1, N_DEV):
            g = g + block(k)
        gsum_ref[...] = g
        for p, shp in enumerate(shapes):
            w_ref, m_ref, v_ref = ins[3 * p:3 * p + 3]
            g_out, d_out, m_out, v_out = outs[4 * p:4 * p + 4]
            for row, rows, idx in _packed_pieces(shp):
                gp = gsum_ref[first_row[p] + row:first_row[p] + row + rows, :]
                d, mn, vn = _adamw_math(w_ref[idx], gp, m_ref[idx], v_ref[idx])
                g_out[idx] = gp
                d_out[idx] = d
                m_out[idx] = mn
                v_out[idx] = vn

    out_shape = [jax.ShapeDtypeStruct((R, LANE), F32)]
    for shp in shapes:
        out_shape += [jax.ShapeDtypeStruct(shp, F32)] * 4
    flat = [a for wmv in weights for a in wmv]
    res = _pcall(kern, name=name, out_shape=tuple(out_shape), vmem_mb=48)(gathered, own, *flat)
    return res[0], [tuple(res[1 + 4 * p:5 + 4 * p]) for p in range(nw)]


def _pair_sum(g, theirs, c_idx, *, name):
    R, C = g.shape
    ch = C // 2
    tr = _row_tile(R, ch)

    def kern(c_ref, a_ref, b_ref, o_ref):
        o_ref[...] = (a_ref[...] + b_ref[...]).astype(BF16)

    gs = pltpu.PrefetchScalarGridSpec(
        num_scalar_prefetch=1, grid=(R // tr,),
        in_specs=[pl.BlockSpec((tr, ch), lambda i, c: (i, c[0])), pl.BlockSpec((tr, ch), lambda i, c: (i, 0))],
        out_specs=pl.BlockSpec((tr, ch), lambda i, c: (i, 0)))
    return pl.pallas_call(kern, name=name, out_shape=jax.ShapeDtypeStruct((R, ch), BF16), grid_spec=gs,
                          compiler_params=pltpu.CompilerParams(dimension_semantics=("parallel",),
                                                               vmem_limit_bytes=48 << 20))(c_idx, g, theirs)


WeightRows = collections.namedtuple("WeightRows", "full_rows own_rows cols pieces zero_rows")


def _w_in_piece_a(j):
    return jnp.where(j == 0, 0, 1232 * j + GAP)


def _w_in_piece_b(j):
    return jnp.where(j == 0, GAP_AT + GAP, 1232 * j + GAP_AT + GAP)


W_IN = WeightRows(NP, 1232, D_MODEL, ((0, GAP_AT, _w_in_piece_a), (GAP_AT, 1232 - GAP_AT, _w_in_piece_b)),
                  ((GAP_AT, GAP),))
W_OUT = WeightRows(2048, 512, D_MODEL, ((0, 512, lambda j: 512 * j),), ())
W_UQ = WeightRows(2048, 384, Q_LORA, ((0, 192, lambda j: 512 * j), (192, 192, lambda j: 512 * j + 256)),
                  tuple((256 * h + 192, 64) for h in range(N_HEADS)))
W_UKV = WeightRows(2048, 512, KV_LORA, ((0, 512, lambda j: 512 * j),), ())
W_CONV = WeightRows(64, 16, 256, ((0, 16, lambda j: 16 * j),), ())
SHARDED = (W_IN, W_OUT, W_UQ, W_UKV)
SHARDED_NAMES = ("w_in", "w_out", "w_uq", "w_ukv")
WEIGHT_ROWS = dict(zip(SHARDED_NAMES, SHARDED))


def _mesh_pos():
    x, y, c = lax.axis_index("x"), lax.axis_index("y"), lax.axis_index("c")
    return x, y, c


def _other_chips(x, y):
    return [(1 - x, y), (x, 1 - y), (1 - x, 1 - y)]


def _rows(start, n):
    return pl.ds(pl.multiple_of(start, 16), n)


def _half_cols(spec, c):
    ch = spec.cols // 2
    return pl.ds(pl.multiple_of(c * ch, LANE), ch)


def _allgather_script(specs, shards, zeros, layers):
    na = len(specs)
    zlist = [a for a in range(na) if zeros[a] is not None]
    n_layers = [shards[a].shape[0] if layers[a] is None else 1 for a in range(na)]
    plan_first, plan_own, plan_zero = [], [], []
    for a, spec in enumerate(specs):
        for p in range(len(spec.pieces)):
            plan_own.append((a, p))
            for k in range(3):
                plan_first.append((a, p, k))
        for z in range(len(spec.zero_rows)):
            for l in range(n_layers[a]):
                plan_zero.append((a, z, l))
    nf = len(plan_first)
    n_sems = 2 * nf + len(plan_own) + len(plan_zero)

    def copies(ins_all, outs, send_sems, recv_sems):
        ins = [ins_all[a] if layers[a] is None else ins_all[a].at[pl.ds(layers[a], 1)] for a in range(na)]
        zrefs = dict(zip(zlist, ins_all[na:]))
        x, y, c = _mesh_pos()
        j = 2 * x + y
        chips = _other_chips(x, y)
        sibling = (x, y, 1 - c)

        def remote(src, dst, sem, to):
            return pltpu.make_async_remote_copy(src_ref=src, dst_ref=dst, send_sem=send_sems.at[sem],
                                                recv_sem=recv_sems.at[sem], device_id=to, device_id_type=MESH)

        def block(a, p, chip, cols):
            _, n, dst = specs[a].pieces[p]
            return outs[a].at[:, _rows(dst(chip), n), cols]

        def first(i):
            a, p, k = plan_first[i]
            src0, n, _ = specs[a].pieces[p]
            cols = _half_cols(specs[a], c)
            return remote(ins[a].at[:, pl.ds(src0, n), cols], block(a, p, j, cols), i, (*chips[k], c))

        def landed(i, half):
            a, p, k = plan_first[i]
            return block(a, p, 2 * chips[k][0] + chips[k][1], _half_cols(specs[a], half))

        def arrival(i, half, sem):
            return remote(landed(i, half), landed(i, half), sem, sibling)

        def passed(i):
            return remote(landed(i, c), landed(i, c), nf + i, sibling)

        def own(i):
            a, p = plan_own[i]
            src0, n, _ = specs[a].pieces[p]
            return remote(ins[a].at[:, pl.ds(src0, n), :], block(a, p, j, slice(None)), 2 * nf + i, sibling)

        def zero(i):
            a, z, l = plan_zero[i]
            r0, n = specs[a].zero_rows[z]
            return remote(zrefs[a].at[pl.ds(0, n), :], outs[a].at[l, pl.ds(r0, n), :],
                          2 * nf + len(plan_own) + i, sibling)

        fixed = [own(i) for i in range(len(plan_own))] + [zero(i) for i in range(len(plan_zero))]
        return c, fixed, first, arrival, passed

    def start(ins, outs, send_sems, recv_sems):
        _, fixed, first, _, _ = copies(ins, outs, send_sems, recv_sems)
        for cp in fixed:
            cp.start()
        for i in range(nf):
            first(i).start()

    def finish(ins, outs, send_sems, recv_sems):
        c, fixed, first, arrival, passed = copies(ins, outs, send_sems, recv_sems)
        for i in range(nf):
            arrival(i, c, i).wait_recv()
            passed(i).start()
        for i in range(nf):
            arrival(i, 1 - c, nf + i).wait_recv()
        for cp in fixed:
            cp.wait()
        for i in range(nf):
            first(i).wait_send()
            passed(i).wait_send()

    out_shape = tuple(jax.ShapeDtypeStruct((n_layers[a], spec.full_rows, spec.cols), BF16)
                      for a, spec in enumerate(specs))
    args = tuple(shards) + tuple(zeros[a] for a in zlist)
    return CommScript(args, out_shape, n_sems, start, finish)


def _start_all_wait_all(args, out_shape, n_sems, make_copies):
    def start(ins, outs, send_sems, recv_sems):
        for cp in make_copies(ins, outs, send_sems, recv_sems):
            cp.start()

    def finish(ins, outs, send_sems, recv_sems):
        for cp in make_copies(ins, outs, send_sems, recv_sems):
            cp.wait()

    return CommScript(tuple(args), tuple(out_shape), n_sems, start, finish)


def _exchange_script(specs, grads):
    na = len(grads)

    def make_copies(ins, outs, send_sems, recv_sems):
        x, y, c = _mesh_pos()
        return [pltpu.make_async_remote_copy(
            src_ref=ins[a].at[:, _half_cols(specs[a], 1 - c)], dst_ref=outs[a], send_sem=send_sems.at[a],
            recv_sem=recv_sems.at[a], device_id=(x, y, 1 - c), device_id_type=MESH) for a in range(na)]

    out_shape = [jax.ShapeDtypeStruct((s.full_rows, s.cols // 2), F32) for s in specs]
    return _start_all_wait_all(grads, out_shape, na, make_copies)


def _scatter_script(specs, parts):
    na = len(parts)
    plan = [(a, p, k) for a in range(na) for p in range(len(specs[a].pieces)) for k in range(3)]

    def make_copies(ins, outs, send_sems, recv_sems):
        x, y, c = _mesh_pos()
        chips = _other_chips(x, y)
        copies = []
        for i, (a, p, k) in enumerate(plan):
            src0, n, dst = specs[a].pieces[p]
            pk = 2 * chips[k][0] + chips[k][1]
            copies.append(pltpu.make_async_remote_copy(
                src_ref=ins[a].at[_rows(dst(pk), n), :], dst_ref=outs[a].at[k, pl.ds(src0, n), :],
                send_sem=send_sems.at[i], recv_sem=recv_sems.at[i], device_id=(*chips[k], c), device_id_type=MESH))
        return copies

    out_shape = [jax.ShapeDtypeStruct((3, s.own_rows, s.cols // 2), BF16) for s in specs]
    return _start_all_wait_all(parts, out_shape, len(plan), make_copies)


def _chip_sum(spec, part, recv, *, name):
    ch = spec.cols // 2
    npieces = len(spec.pieces)

    def kern(recv_ref, part_ref, o_ref, own_ref, sems):
        j = 2 * lax.axis_index("x") + lax.axis_index("y")
        copies = []
        for p, (src0, n, dst) in enumerate(spec.pieces):
            copies.append(pltpu.make_async_copy(part_ref.at[_rows(dst(j), n), :], own_ref.at[pl.ds(src0, n), :],
                                                sems.at[p]))
        for cp in copies:
            cp.start()
        for cp in copies:
            cp.wait()
        o_ref[...] = ((own_ref[...].astype(F32) + recv_ref[0].astype(F32)) + recv_ref[1].astype(F32)) \
            + recv_ref[2].astype(F32)

    vm = pl.BlockSpec(memory_space=pltpu.VMEM)
    return _pcall(kern, name=name, out_shape=jax.ShapeDtypeStruct((spec.own_rows, ch), F32),
                  in_specs=[vm, HBM_SPEC], out_specs=vm,
                  scratch=[pltpu.VMEM((spec.own_rows, ch), BF16), pltpu.SemaphoreType.DMA((npieces,))],
                  vmem_mb=48)(recv, part)


def _sibling_script(sums):
    na = len(sums)

    def make_copies(ins, outs, send_sems, recv_sems):
        x, y, c = _mesh_pos()
        return [pltpu.make_async_remote_copy(
            src_ref=ins[a], dst_ref=outs[a], send_sem=send_sems.at[a], recv_sem=recv_sems.at[a],
            device_id=(x, y, 1 - c), device_id_type=MESH) for a in range(na)]

    out_shape = [jax.ShapeDtypeStruct(t.shape, t.dtype) for t in sums]
    return _start_all_wait_all(sums, out_shape, na, make_copies)


class _SemWindow:
    def __init__(self, sems, offset):
        self._sems, self._offset = sems, offset

    @property
    def at(self):
        return self

    def __getitem__(self, i):
        return self._sems.at[i + self._offset]


def _merge_scripts(*scripts):
    a_off, o_off, s_off = [0], [0], [0]
    for s in scripts:
        a_off.append(a_off[-1] + len(s.args))
        o_off.append(o_off[-1] + len(s.out_shape))
        s_off.append(s_off[-1] + s.n_sems)

    def phase(which):
        def run(ins, outs, send_sems, recv_sems):
            for n, s in enumerate(scripts):
                getattr(s, which)(ins[a_off[n]:a_off[n + 1]], outs[o_off[n]:o_off[n + 1]],
                                  _SemWindow(send_sems, s_off[n]), _SemWindow(recv_sems, s_off[n]))
        return run

    return CommScript(sum((tuple(s.args) for s in scripts), ()), sum((tuple(s.out_shape) for s in scripts), ()),
                      s_off[-1], phase("start"), phase("finish"))


class _GradReducer:
    def __init__(self, layer, names, grads, c_idx):
        self.specs = tuple(WEIGHT_ROWS[nm] for nm in names)
        self.grads, self.c_idx = tuple(grads), c_idx
        self.names = [f"{nm}{layer}" for nm in names]

    def exchange(self):
        return _exchange_script(self.specs, self.grads)

    def scatter(self, theirs):
        self.parts = tuple(_pair_sum(g, th, self.c_idx, name=f"pair_sum_{nm}")
                           for g, th, nm in zip(self.grads, theirs, self.names))
        return _scatter_script(self.specs, self.parts)

    def sibling(self, recv):
        self.sums = tuple(_chip_sum(s, p, r, name=f"chip_sum_{nm}")
                          for s, p, r, nm in zip(self.specs, self.parts, recv, self.names))
        return _sibling_script(self.sums)

    def done(self, others):
        return list(zip(self.sums, others))


def _allgather_small_script(block):
    m_per, n = block.shape

    def copies(ins, outs, send_sems, recv_sems):
        (x_ref,), (out_ref,) = ins, outs
        x, y, c = _mesh_pos()
        me, sibling = (x, y, c), (x, y, 1 - c)
        chips = _other_chips(x, y)

        def rows(px, py, pc):
            return out_ref.at[4 * px + 2 * py + pc]

        def copy(k, blk, to, src=None):
            return pltpu.make_async_remote_copy(
                src_ref=rows(*blk) if src is None else src, dst_ref=rows(*blk), send_sem=send_sems.at[k],
                recv_sem=recv_sems.at[k], device_id=to, device_id_type=MESH)

        first = [copy(0, me, sibling, src=x_ref)]
        first += [copy(1 + k, me, (*chip, c), src=x_ref) for k, chip in enumerate(chips)]
        passed = [copy(4 + k, (*chip, c), sibling) for k, chip in enumerate(chips)]
        landed = [copy(1 + k, (*chip, c), me) for k, chip in enumerate(chips)]
        from_sibling = [copy(0, sibling, me)] + [copy(4 + k, (*chip, 1 - c), me) for k, chip in enumerate(chips)]
        return first, passed, landed, from_sibling

    def start(ins, outs, send_sems, recv_sems):
        first, _, _, _ = copies(ins, outs, send_sems, recv_sems)
        for cp in first:
            cp.start()

    def finish(ins, outs, send_sems, recv_sems):
        first, passed, landed, from_sibling = copies(ins, outs, send_sems, recv_sems)
        for k in range(3):
            landed[k].wait_recv()
            passed[k].start()
        for cp in from_sibling:
            cp.wait_recv()
        for cp in first + passed:
            cp.wait_send()

    return CommScript((block,), (jax.ShapeDtypeStruct((N_DEV, m_per, n), block.dtype),), 7, start, finish)


def _rope_tables(positions):
    half = ROPE // 2
    inv_freq = ROPE_THETA ** (-jnp.arange(half, dtype=F32) / half)
    ang = positions.astype(F32)[:, None] * inv_freq
    cos, sin = jnp.cos(ang), jnp.sin(ang)
    S = positions.shape[0]
    cos_t = jnp.concatenate([cos, cos, jnp.ones((S, 64), F32)], axis=1)
    sin_t = jnp.concatenate([-sin, sin, jnp.zeros((S, 64), F32)], axis=1)
    return cos_t, sin_t


def _decode_conv(bits):
    rows = bits.reshape(DEPTH, N_CHIPS, 16, 256)[:, :, :3, :]
    conv = lax.bitcast_convert_type(rows.reshape(DEPTH, N_CHIPS, 3, 128, 2), F32)
    return jnp.transpose(conv, (0, 2, 1, 3)).reshape(DEPTH, 3, 512)


def _local_step(x, positions, target, emb_g, emb_b, w_in_t0, rest0, weights1, q_g, kv_g, w_pool, pool_scale,
                b_out, ln_g, ln_b, c_idx=None):
    cos_t, sin_t = _rope_tables(positions)
    if isinstance(w_in_t0, CommScript):
        (h, hb), (landed,) = _ln_fwd(x, emb_g, emb_b, name="emb_ln", comm=w_in_t0)
        w_in_t0 = landed[0]
    else:
        h, hb = _ln_fwd(x, emb_g, emb_b, name="emb_ln")
    weights = [None, weights1]
    saved = []
    for l in range(DEPTH):
        if l == 0 and isinstance(rest0, CommScript):
            proj, landed = _matmul(hb, w_in_t0, "nt", name="in_proj0", tm=1024, tn=1024, tk=2048, vmem_mb=56,
                                   comm=rest0)
            weights[0] = (w_in_t0,) + tuple(a[0] for a in landed[:3])
            conv_w = _decode_conv(landed[3])
        else:
            if l == 0:
                weights[0] = (w_in_t0,) + tuple(rest0[:3])
                conv_w = rest0[3]
            proj = _matmul(hb, weights[l][0], "nt", name=f"in_proj{l}", tm=1024, tn=1024, tk=2048, vmem_mb=56)
        w_in_t, w_out, w_uq_t, w_ukv_t = weights[l]
        qc, kc, v, vt, qn, kvn = _mla_qkv(proj, cos_t, sin_t, q_g[l], kv_g[l], w_uq_t, w_ukv_t, name=f"mla_qkv{l}")
        nxt = weights[l + 1] if l + 1 < DEPTH else None
        if isinstance(nxt, CommScript):
            (o, lse2), landed = _flash_fwd(qc, kc, vt, name=f"flash_fwd{l}", comm=nxt)
            weights[l + 1] = tuple(a[0] for a in landed)
        else:
            o, lse2 = _flash_fwd(qc, kc, vt, name=f"flash_fwd{l}")
        mix = _mixer_fwd(proj, o, w_pool[l], pool_scale[l], conv_w[l], name=f"mixer_fwd{l}")
        if l == DEPTH - 1:
            r = _outproj_residual(mix, w_out, h, b_out[l], name=f"out_proj{l}")
            saved.append((hb, proj, qc, kc, v, qn, kvn, o, lse2, mix, r))
        else:
            h_next, hb_next, r = _outproj_ln(mix, w_out, h, b_out[l], ln_g[l], ln_b[l], name=f"out_proj_ln{l}")
            saved.append((hb, proj, qc, kc, v, qn, kvn, o, lse2, mix, r))
            h, hb = h_next, hb_next

    small = [None] * DEPTH
    big = [None] * DEPTH
    above = scatter_above = None
    for l in reversed(range(DEPTH)):
        w_in_t, w_out, w_uq_t, w_ukv_t = weights[l]
        hb_in, proj, qc, kc, v, qn, kvn, o, lse2, mix, r = saved[l]
        if l == DEPTH - 1:
            loss_acc, dr, drb, d_ln_g, d_ln_b, d_b_out = _loss_ln_bwd(target, r, ln_g[l], ln_b[l], name="loss_ln_bwd")
        else:
            dr, drb, d_ln_g, d_ln_b, d_b_out = _ln_bwd(dh, r, ln_g[l], name=f"ln_bwd{l}")
        dmix = _matmul(drb, w_out, "nt", name=f"dmix{l}", tm=1024, tn=1024, tk=2048, vmem_mb=56)
        d_w_out = _matmul(mix, drb, "tn", name=f"dw_out{l}", tm=1024, tn=1024, tk=2048, vmem_mb=56)
        d_mix, do, delta, d_w_pool, d_ps, d_conv = _mixer_bwd(dmix, proj, o, w_pool[l], pool_scale[l], conv_w[l],
                                                              name=f"mixer_bwd{l}")
        if above is not None:
            (dqb, dkvb, dkr), recv = _flash_bwd(qc, kc, v, do, lse2, delta, cos_t, sin_t, name=f"flash_bwd{l}",
                                                comm=scatter_above)
            sibling_above = above.sibling(recv)
        else:
            dqb, dkvb, dkr = _flash_bwd(qc, kc, v, do, lse2, delta, cos_t, sin_t, name=f"flash_bwd{l}")
        d_mla, d_qg, d_kvg = _mla_qkv_bwd(dqb, dkvb, dkr, proj, cos_t, sin_t, q_g[l], kv_g[l], w_uq_t, w_ukv_t,
                                          name=f"mla_qkv_bwd{l}")
        d_w_uq_t = _matmul(dqb, qn, "tn", name=f"dw_uq{l}", tm=2048, tn=512, tk=2048, vmem_mb=56)
        d_w_ukv_t = _matmul(dkvb, kvn, "tn", name=f"dw_ukv{l}", tm=2048, tn=256, tk=2048, vmem_mb=56)
        small[l] = dict(q_g=d_qg[0], kv_g=d_kvg[0], w_pool=d_w_pool, pool_scale=d_ps[0], conv_w=d_conv,
                        b_out=d_b_out[0], ln_g=d_ln_g[0], ln_b=d_ln_b[0])
        rest = (d_w_out, d_w_uq_t, d_w_ukv_t)
        if c_idx is None:
            d_w_in_t = _dproj_t_times_h(d_mla, d_mix, hb_in, name=f"dw_in{l}")
            dh = _dproj_times_w(d_mla, d_mix, w_in_t, dr, ALPHA, name=f"dh{l}")
            big[l] = (d_w_in_t,) + rest
        elif l > 0:
            d_w_in_t = _dproj_t_times_h(d_mla, d_mix, hb_in, name=f"dw_in{l}")
            above = _GradReducer(l, SHARDED_NAMES, (d_w_in_t,) + rest, c_idx)
            dh, theirs = _dproj_times_w(d_mla, d_mix, w_in_t, dr, ALPHA, name=f"dh{l}", comm=above.exchange())
            scatter_above = above.scatter(theirs)
        else:
            red_rest = _GradReducer(l, SHARDED_NAMES[1:], rest, c_idx)
            d_w_in_t, landed = _dproj_t_times_h(d_mla, d_mix, hb_in, name=f"dw_in{l}",
                                                comm=_merge_scripts(sibling_above, red_rest.exchange()))
            big[l + 1] = above.done(landed[:len(SHARDED)])
            red_in = _GradReducer(l, SHARDED_NAMES[:1], (d_w_in_t,), c_idx)
            landed = _run_comm(_merge_scripts(red_in.exchange(), red_rest.scatter(landed[len(SHARDED):])),
                               name="exchange_w_in0")
            sibling_rest = red_rest.sibling(landed[1:])
            dh, landed = _dproj_times_w(d_mla, d_mix, w_in_t, dr, ALPHA, name=f"dh{l}",
                                        comm=_merge_scripts(red_in.scatter(landed[:1]), sibling_rest))
            recv_in, others_rest = landed[:1], landed[1:]
    grad_x, _, d_emb_g, d_emb_b, _ = _ln_bwd(dh, x, emb_g, name="emb_ln_bwd", bf16_copy=False)
    if c_idx is not None:
        others_in = _run_comm(red_in.sibling(recv_in), name="send_to_sibling0")
        big[0] = red_in.done(others_in) + red_rest.done(others_rest)
    return loss_acc[0, 0], grad_x, d_emb_g, d_emb_b, small, big


SMALL_ORDER = ("emb_ln_g", "emb_ln_b", "q_norm_g", "kv_norm_g", "w_pool", "pool_scale", "b_out", "ln_g", "ln_b")
SMALL_LAYER_KEYS = ("q_g", "kv_g", "w_pool", "pool_scale", "b_out", "ln_g", "ln_b", "conv_w")


def _pack_small(arrs, extra_rows):
    flat = jnp.concatenate([a.reshape(-1) for a in arrs])
    rows = flat.shape[0] // LANE
    total = -(-(rows + extra_rows) // 8) * 8
    return jnp.pad(flat, (0, total * LANE - flat.shape[0])).reshape(total, LANE)


def kernel(x, positions, emb_ln_g, emb_ln_b, w_in, q_norm_g, kv_norm_g, w_uq, w_ukv, w_pool, pool_scale, conv_w, w_out, b_out, ln_g, ln_b, loss_target, m_emb_ln_g, m_emb_ln_b, m_w_in, m_q_norm_g, m_kv_norm_g, m_w_uq, m_w_ukv, m_w_pool, m_pool_scale, m_conv_w, m_w_out, m_b_out, m_ln_g, m_ln_b, v_emb_ln_g, v_emb_ln_b, v_w_in, v_q_norm_g, v_kv_norm_g, v_w_uq, v_w_ukv, v_w_pool, v_pool_scale, v_conv_w, v_w_out, v_b_out, v_ln_g, v_ln_b):
    xi, yi, ci = lax.axis_index("x"), lax.axis_index("y"), lax.axis_index("c")
    chip = 2 * xi + yi
    c_idx = ci.reshape(1).astype(jnp.int32)

    def t(a):
        return jnp.swapaxes(a, 1, 2)

    conv_bits = lax.bitcast_convert_type(conv_w.reshape(DEPTH, 3 * 128), BF16).reshape(DEPTH, 3, 256)
    conv_bits = jnp.pad(conv_bits, ((0, 0), (0, 13), (0, 0)))
    own = (t(w_in).astype(BF16), w_out.astype(BF16), t(w_uq).astype(BF16), t(w_ukv).astype(BF16))
    zeros = (jnp.zeros((GAP, D_MODEL), BF16), None, jnp.zeros((64, Q_LORA), BF16), None)
    gather_in0 = _allgather_script((W_IN,), own[:1], zeros[:1], (0,))
    gather0 = _allgather_script(SHARDED[1:] + (W_CONV,), own[1:] + (conv_bits,), zeros[1:] + (None,),
                                (0, 0, 0, None))
    gather1 = _allgather_script(SHARDED, own, zeros, (1, 1, 1, 1))

    loss_part, grad_x, d_emb_g, d_emb_b, grads, reduced = _local_step(
        x[0], positions[0], loss_target[0], emb_ln_g, emb_ln_b, gather_in0, gather0, gather1, q_norm_g, kv_norm_g,
        w_pool, pool_scale, b_out, ln_g, ln_b, c_idx)

    def rows(a):
        return a.reshape(1, -1) if a.ndim == 1 else a

    small_wmv = [tuple(rows(a) for a in wmv) for wmv in (
        (emb_ln_g, m_emb_ln_g, v_emb_ln_g), (emb_ln_b, m_emb_ln_b, v_emb_ln_b),
        (q_norm_g, m_q_norm_g, v_q_norm_g), (kv_norm_g, m_kv_norm_g, v_kv_norm_g), (w_pool, m_w_pool, v_w_pool),
        (pool_scale, m_pool_scale, v_pool_scale), (b_out, m_b_out, v_b_out), (ln_g, m_ln_g, v_ln_g),
        (ln_b, m_ln_b, v_ln_b))]
    packed_g = _pack_small(
        [d_emb_g, d_emb_b] + [jnp.stack([grads[l][key] for l in range(DEPTH)]) for key in SMALL_LAYER_KEYS]
        + [jnp.pad(loss_part.reshape(1), (0, LANE - 1))], 0)
    (gathered,) = _run_comm(_allgather_small_script(packed_g), name="allgather_small")
    g_tot, small_upd = _small_sum_adamw(gathered, packed_g, small_wmv, name="small_sum_adamw")
    off = sum(w.size for w, _, _ in small_wmv)
    flat_tot = g_tot.reshape(-1)

    def halves(a):
        return [reduced[l][a] for l in range(DEPTH)]

    upd = {}
    upd["w_in"] = tuple(t(o) for o in _adamw_halves(t(w_in), t(m_w_in), t(v_w_in), halves(0), c_idx,
                                                    name="adamw_w_in"))
    conv_tot = flat_tot[off:off + DEPTH * 3 * 512].reshape(DEPTH, 3, 512)
    loss = flat_tot[off + DEPTH * 3 * 512]
    g_conv = lax.dynamic_slice_in_dim(conv_tot, chip * 128, 128, axis=2)

    def whole(a):
        return jnp.stack([jnp.where(ci == 0, jnp.concatenate([mine, oth], axis=1),
                                    jnp.concatenate([oth, mine], axis=1)) for mine, oth in halves(a)])

    upd["w_out"] = _adamw_halves(w_out, m_w_out, v_w_out, halves(1), c_idx, name="adamw_w_out")
    g_uq, g_ukv = t(whole(2)), t(whole(3))
    upd["w_uq"] = (g_uq,) + _adamw(w_uq, g_uq, m_w_uq, v_w_uq, name="adamw_w_uq")
    upd["w_ukv"] = (g_ukv,) + _adamw(w_ukv, g_ukv, m_w_ukv, v_w_ukv, name="adamw_w_ukv")
    upd["conv_w"] = (g_conv,) + _adamw(conv_w, g_conv, m_conv_w, v_conv_w, name="adamw_conv_w")
    for nm, res in zip(SMALL_ORDER, small_upd):
        upd[nm] = tuple(a.reshape(-1) for a in res) if nm in ("emb_ln_g", "emb_ln_b") else res

    order = ("emb_ln_g", "emb_ln_b", "w_in", "q_norm_g", "kv_norm_g", "w_uq", "w_ukv", "w_pool", "pool_scale",
             "conv_w", "w_out", "b_out", "ln_g", "ln_b")
    outs = [loss, grad_x[None]]
    for field in range(4):
        outs += [upd[nm][field] for nm in order]
    return tuple(outs)
```

```python
import collections

import jax
import jax.numpy as jnp
from jax import lax
from jax.experimental import pallas as pl
from jax.experimental.pallas import tpu as pltpu

F32 = jnp.float32
BF16 = jnp.bfloat16
MESH = pl.DeviceIdType.MESH

D_MODEL = 2048
DEPTH = 2
N_HEADS = 8
NOPE = 128
ROPE = 64
Q_LORA = 512
KV_LORA = 256
D_MLA = 1024
POOL_WINDOWS = (2, 4, 8, 16)
D_IN_PROJ = 4928
LN_EPS = 1e-5
RMS_EPS = 1e-6
ROPE_THETA = 10000.0
ALPHA = (2 * DEPTH) ** 0.25
SCALE = (NOPE + ROPE) ** -0.5
LOG2E = 1.4426950408889634
SCALE_LOG2E = SCALE * LOG2E
ADAM_LR = 0.001
ADAM_B1 = 0.9
ADAM_B2 = 0.999
ADAM_EPS = 1e-08
ADAM_WD = 0.01
ADAM_STEP = 10

NP = 5120
GAP_AT = 832
GAP = NP - D_IN_PROJ
W_MLA = 1024
W_MIX = NP - W_MLA
HALO = 16
LANE = 128
N_CHIPS = 4
N_DEV = 8
TQ = 512
FWD_GROUP = 4
BWD_GROUP = 3

NN = (((1,), (0,)), ((), ()))
NT = (((1,), (1,)), ((), ()))
TN = (((0,), (0,)), ((), ()))


CommScript = collections.namedtuple("CommScript", "args out_shape n_sems start finish")
HBM_SPEC = pl.BlockSpec(memory_space=pl.ANY)


def _pcall(kern, *, name, out_shape, grid=None, in_specs=None, out_specs=None, scratch=(), dims=None,
           vmem_mb=None, comm=None):
    cp = {}
    if dims is not None:
        cp["dimension_semantics"] = dims if comm is None else ("arbitrary",) * len(dims)
    if vmem_mb is not None:
        cp["vmem_limit_bytes"] = vmem_mb << 20
    if comm is None:
        args = dict(name=name, out_shape=out_shape, scratch_shapes=list(scratch),
                    compiler_params=pltpu.CompilerParams(**cp))
        if grid is not None:
            args["grid"] = grid
        if in_specs is not None:
            args["in_specs"] = in_specs
        if out_specs is not None:
            args["out_specs"] = out_specs
        return pl.pallas_call(kern, **args)

    single = not isinstance(out_shape, (tuple, list))
    own_out = (out_shape,) if single else tuple(out_shape)
    own_out_specs = (out_specs,) if single else tuple(out_specs)
    n_in, n_out, n_scr = len(in_specs), len(own_out), len(scratch)
    na, no = len(comm.args), len(comm.out_shape)

    def at(end):
        cond = None
        for d, n in enumerate(grid):
            here = pl.program_id(d) == (n - 1 if end else 0)
            cond = here if cond is None else jnp.logical_and(cond, here)
        return cond

    def wrapped(*refs):
        own_in, c_in = refs[:n_in], refs[n_in:n_in + na]
        o0 = n_in + na
        own_o, c_out = refs[o0:o0 + n_out], refs[o0 + n_out:o0 + n_out + no]
        s0 = o0 + n_out + no
        own_s, (send_sems, recv_sems) = refs[s0:s0 + n_scr], refs[s0 + n_scr:]

        @pl.when(at(False))
        def _():
            comm.start(c_in, c_out, send_sems, recv_sems)

        kern(*own_in, *own_o, *own_s)

        @pl.when(at(True))
        def _():
            comm.finish(c_in, c_out, send_sems, recv_sems)

    call = pl.pallas_call(
        wrapped, name=name, out_shape=own_out + tuple(comm.out_shape), grid=grid,
        in_specs=list(in_specs) + [HBM_SPEC] * na, out_specs=own_out_specs + (HBM_SPEC,) * no,
        scratch_shapes=list(scratch) + [pltpu.SemaphoreType.DMA((comm.n_sems,)),
                                        pltpu.SemaphoreType.DMA((comm.n_sems,))],
        compiler_params=pltpu.CompilerParams(**cp))

    def run(*args):
        res = call(*args, *comm.args)
        own = res[0] if single else tuple(res[:n_out])
        return own, tuple(res[n_out:])

    return run


def _run_comm(script, *, name):
    na, no = len(script.args), len(script.out_shape)

    def body(*refs):
        ins, outs = refs[:na], refs[na:na + no]
        send_sems, recv_sems = refs[na + no:]
        script.start(ins, outs, send_sems, recv_sems)
        script.finish(ins, outs, send_sems, recv_sems)

    return pl.pallas_call(
        body, name=name, out_shape=tuple(script.out_shape), in_specs=[HBM_SPEC] * na, out_specs=(HBM_SPEC,) * no,
        scratch_shapes=[pltpu.SemaphoreType.DMA((script.n_sems,)), pltpu.SemaphoreType.DMA((script.n_sems,))])(
            *script.args)


def _sigmoid(g):
    return 1.0 / (1.0 + jnp.exp(-g))


def _silu_and_grad(g):
    sig = _sigmoid(g)
    return g * sig, sig * (1.0 + g * (1.0 - sig))


def _matmul(a, b, mode, *, name, tm, tn, tk, out_dtype=F32, vmem_mb=48, comm=None):
    if mode == "nn":
        (M, K), N = a.shape, b.shape[1]
    elif mode == "nt":
        (M, K), N = a.shape, b.shape[0]
    else:
        (K, M), N = a.shape, b.shape[1]
    tm, tn, tk = min(tm, M), min(tn, N), min(tk, K)
    assert M % tm == 0 and N % tn == 0 and K % tk == 0, (name, M, N, K)
    nk = K // tk
    dn = {"nn": NN, "nt": NT, "tn": TN}[mode]
    if mode == "tn":
        a_spec = pl.BlockSpec((tk, tm), lambda i, j, k: (k, i))
    else:
        a_spec = pl.BlockSpec((tm, tk), lambda i, j, k: (i, k))
    if mode == "nt":
        b_spec = pl.BlockSpec((tn, tk), lambda i, j, k: (j, k))
    else:
        b_spec = pl.BlockSpec((tk, tn), lambda i, j, k: (k, j))
    o_spec = pl.BlockSpec((tm, tn), lambda i, j, k: (i, j))

    def kern(a_ref, b_ref, o_ref, *rest):
        part = lax.dot_general(a_ref[...].astype(BF16), b_ref[...].astype(BF16), dn,
                               preferred_element_type=F32)
        if nk == 1:
            o_ref[...] = part.astype(out_dtype)
        else:
            acc_ref = rest[0]
            k = pl.program_id(2)

            @pl.when(k == 0)
            def _():
                acc_ref[...] = part

            @pl.when(k > 0)
            def _():
                acc_ref[...] += part

            @pl.when(k == nk - 1)
            def _():
                o_ref[...] = acc_ref[...].astype(out_dtype)

    scratch = [pltpu.VMEM((tm, tn), F32)] if nk > 1 else []
    return _pcall(kern, name=name, out_shape=jax.ShapeDtypeStruct((M, N), out_dtype),
                  grid=(M // tm, N // tn, nk), in_specs=[a_spec, b_spec], out_specs=o_spec, scratch=scratch,
                  dims=("parallel", "parallel", "arbitrary"), vmem_mb=vmem_mb, comm=comm)(a, b)


def _dproj_times_w(d_mla, d_mix, wt, add, add_scale, *, name, comm=None):
    S = d_mla.shape[0]
    Dm = wt.shape[1]
    tm, tn, tk = min(1024, S), 1024, 2048
    nk = 1 + W_MIX // tk

    def kern(a1_ref, a2_ref, b1_ref, b2_ref, add_ref, o_ref, acc_ref):
        k = pl.program_id(2)

        @pl.when(k == 0)
        def _():
            acc_ref[...] = jnp.dot(a1_ref[...], b1_ref[...], preferred_element_type=F32)

        @pl.when(k > 0)
        def _():
            acc_ref[...] += jnp.dot(a2_ref[...], b2_ref[...], preferred_element_type=F32)

        @pl.when(k == nk - 1)
        def _():
            o_ref[...] = add_scale * add_ref[...] + acc_ref[...]

    o_spec = pl.BlockSpec((tm, tn), lambda i, j, k: (i, j))
    b2_spec = pl.BlockSpec((pl.Element(tk), pl.Element(tn)),
                           lambda i, j, k: (pl.multiple_of(W_MLA + tk * jnp.maximum(k - 1, 0), W_MLA),
                                            pl.multiple_of(j * tn, tn)))
    return _pcall(kern, name=name, out_shape=jax.ShapeDtypeStruct((S, Dm), F32), grid=(S // tm, Dm // tn, nk),
                  in_specs=[pl.BlockSpec((tm, W_MLA), lambda i, j, k: (i, 0)),
                            pl.BlockSpec((tm, tk), lambda i, j, k: (i, jnp.maximum(k - 1, 0))),
                            pl.BlockSpec((W_MLA, tn), lambda i, j, k: (0, j)), b2_spec, o_spec],
                  out_specs=o_spec, scratch=[pltpu.VMEM((tm, tn), F32)],
                  dims=("parallel", "parallel", "arbitrary"), vmem_mb=56, comm=comm)(d_mla, d_mix, wt, wt, add)


def _dproj_t_times_h(d_mla, d_mix, h, *, name, comm=None):
    S, Dm = h.shape
    tm, tn, tk = W_MLA, 1024, min(2048, S)
    nk = S // tk

    def kern(a1_ref, a2_ref, b_ref, o_ref, acc_ref):
        i = pl.program_id(0)
        k = pl.program_id(2)
        b = b_ref[...].astype(BF16)

        def accumulate(part):
            @pl.when(k == 0)
            def _():
                acc_ref[...] = part

            @pl.when(k > 0)
            def _():
                acc_ref[...] += part

        @pl.when(i == 0)
        def _():
            accumulate(lax.dot_general(a1_ref[...], b, TN, preferred_element_type=F32))

        @pl.when(i > 0)
        def _():
            accumulate(lax.dot_general(a2_ref[...], b, TN, preferred_element_type=F32))

        @pl.when(k == nk - 1)
        def _():
            o_ref[...] = acc_ref[...]

    return _pcall(kern, name=name, out_shape=jax.ShapeDtypeStruct((NP, Dm), F32), grid=(NP // tm, Dm // tn, nk),
                  in_specs=[pl.BlockSpec((tk, tm), lambda i, j, k: (jnp.where(i == 0, k, nk - 1), 0)),
                            pl.BlockSpec((tk, tm), lambda i, j, k: (jnp.where(i == 0, 0, k), jnp.maximum(i - 1, 0))),
                            pl.BlockSpec((tk, tn), lambda i, j, k: (k, j))],
                  out_specs=pl.BlockSpec((tm, tn), lambda i, j, k: (i, j)), scratch=[pltpu.VMEM((tm, tn), F32)],
                  dims=("parallel", "parallel", "arbitrary"), vmem_mb=48, comm=comm)(d_mla, d_mix, h)


def _ln_fwd(x, g, b, *, name, comm=None):
    S, Dm = x.shape
    tm = min(512, S)

    def kern(x_ref, g_ref, b_ref, y_ref, yb_ref):
        xf = x_ref[...]
        mu = jnp.mean(xf, axis=-1, keepdims=True)
        xc = xf - mu
        var = jnp.mean(xc * xc, axis=-1, keepdims=True)
        y = xc * lax.rsqrt(var + LN_EPS) * g_ref[...] + b_ref[...]
        y_ref[...] = y
        yb_ref[...] = y.astype(BF16)

    row = pl.BlockSpec((tm, Dm), lambda i: (i, 0))
    vec = pl.BlockSpec((1, Dm), lambda i: (0, 0))
    return _pcall(kern, name=name,
                  out_shape=(jax.ShapeDtypeStruct((S, Dm), F32), jax.ShapeDtypeStruct((S, Dm), BF16)),
                  grid=(S // tm,), in_specs=[row, vec, vec], out_specs=(row, row), dims=("parallel",), vmem_mb=48,
                  comm=comm)(
                      x, g.reshape(1, Dm), b.reshape(1, Dm))


def _ln_bwd(dy, r, g, *, name, bf16_copy=True):
    S, Dm = r.shape
    tm = min(512, S)

    def kern(dy_ref, r_ref, g_ref, dr_ref, *rest):
        drb_ref = rest[0] if bf16_copy else None
        dg_ref, db_ref, ds_ref = rest[-3:]

        @pl.when(pl.program_id(0) == 0)
        def _():
            dg_ref[...] = jnp.zeros_like(dg_ref)
            db_ref[...] = jnp.zeros_like(db_ref)
            ds_ref[...] = jnp.zeros_like(ds_ref)

        rf = r_ref[...]
        dyf = dy_ref[...]
        mu = jnp.mean(rf, axis=-1, keepdims=True)
        xc = rf - mu
        var = jnp.mean(xc * xc, axis=-1, keepdims=True)
        rstd = lax.rsqrt(var + LN_EPS)
        xhat = xc * rstd
        dxh = dyf * g_ref[...]
        c1 = jnp.mean(dxh, axis=-1, keepdims=True)
        c2 = jnp.mean(dxh * xhat, axis=-1, keepdims=True)
        dr = rstd * (dxh - c1 - xhat * c2)
        dr_ref[...] = dr
        if bf16_copy:
            drb_ref[...] = dr.astype(BF16)
        dg_ref[...] += jnp.sum(dyf * xhat, axis=0, keepdims=True)
        db_ref[...] += jnp.sum(dyf, axis=0, keepdims=True)
        ds_ref[...] += jnp.sum(dr, axis=0, keepdims=True)

    row = pl.BlockSpec((tm, Dm), lambda i: (i, 0))
    vec = pl.BlockSpec((1, Dm), lambda i: (0, 0))
    vshape = jax.ShapeDtypeStruct((1, Dm), F32)
    copies = ((jax.ShapeDtypeStruct((S, Dm), BF16),), (row,)) if bf16_copy else ((), ())
    res = _pcall(kern, name=name,
                 out_shape=(jax.ShapeDtypeStruct((S, Dm), F32),) + copies[0] + (vshape, vshape, vshape),
                 grid=(S // tm,), in_specs=[row, row, vec], out_specs=(row,) + copies[1] + (vec, vec, vec),
                 dims=("arbitrary",), vmem_mb=48)(dy, r, g.reshape(1, Dm))
    return res if bf16_copy else (res[0], None) + tuple(res[1:])


def _loss_ln_bwd(target, r, g, b, *, name):
    S, Dm = r.shape
    tm = min(512, S)

    def kern(t_ref, r_ref, g_ref, b_ref, l_ref, dr_ref, drb_ref, dg_ref, db_ref, ds_ref):
        @pl.when(pl.program_id(0) == 0)
        def _():
            l_ref[...] = jnp.zeros_like(l_ref)
            dg_ref[...] = jnp.zeros_like(dg_ref)
            db_ref[...] = jnp.zeros_like(db_ref)
            ds_ref[...] = jnp.zeros_like(ds_ref)

        rf = r_ref[...]
        mu = jnp.mean(rf, axis=-1, keepdims=True)
        xc = rf - mu
        var = jnp.mean(xc * xc, axis=-1, keepdims=True)
        rstd = lax.rsqrt(var + LN_EPS)
        xhat = xc * rstd
        e = (xhat * g_ref[...] + b_ref[...]) - t_ref[...]
        dyf = e / float(Dm)
        per_row = jnp.mean(e * e, axis=-1, keepdims=True)
        l_ref[...] += 0.5 * jnp.sum(per_row, axis=0, keepdims=True)
        dxh = dyf * g_ref[...]
        c1 = jnp.mean(dxh, axis=-1, keepdims=True)
        c2 = jnp.mean(dxh * xhat, axis=-1, keepdims=True)
        dr = rstd * (dxh - c1 - xhat * c2)
        dr_ref[...] = dr
        drb_ref[...] = dr.astype(BF16)
        dg_ref[...] += jnp.sum(dyf * xhat, axis=0, keepdims=True)
        db_ref[...] += jnp.sum(dyf, axis=0, keepdims=True)
        ds_ref[...] += jnp.sum(dr, axis=0, keepdims=True)

    row = pl.BlockSpec((tm, Dm), lambda i: (i, 0))
    vec = pl.BlockSpec((1, Dm), lambda i: (0, 0))
    acc = pl.BlockSpec((8, LANE), lambda i: (0, 0))
    vshape = jax.ShapeDtypeStruct((1, Dm), F32)
    return _pcall(kern, name=name,
                  out_shape=(jax.ShapeDtypeStruct((8, LANE), F32), jax.ShapeDtypeStruct((S, Dm), F32),
                             jax.ShapeDtypeStruct((S, Dm), BF16), vshape, vshape, vshape),
                  grid=(S // tm,), in_specs=[row, row, vec, vec], out_specs=(acc, row, row, vec, vec, vec),
                  dims=("arbitrary",), vmem_mb=56)(target, r, g.reshape(1, Dm), b.reshape(1, Dm))


def _rot_sum(t):
    return pltpu.roll(t, 32, 1) + pltpu.roll(t, 96, 1)


def _mla_qkv(proj, cos_t, sin_t, qg, kvg, wuq_t, wukv_t, *, name):
    S = proj.shape[0]
    tm = min(256, S)

    def kern(ql_ref, kvl_ref, kr_ref, cos_ref, sin_ref, qg_ref, kvg_ref, wuq_ref, wukv_ref,
             qc_ref, kc_ref, v_ref, vt_ref, qn_ref, kvn_ref):
        cosv = cos_ref[...]
        sinv = sin_ref[...]

        def rope(t):
            return t * cosv + _rot_sum(t) * sinv

        ql = ql_ref[...]
        qn = (ql * lax.rsqrt(jnp.mean(ql * ql, axis=-1, keepdims=True) + RMS_EPS) * qg_ref[...]).astype(BF16)
        kvl = kvl_ref[...]
        kvn = (kvl * lax.rsqrt(jnp.mean(kvl * kvl, axis=-1, keepdims=True) + RMS_EPS) * kvg_ref[...]).astype(BF16)
        qn_ref[...] = qn
        kvn_ref[...] = kvn
        q = lax.dot_general(qn, wuq_ref[...], NT, preferred_element_type=F32)
        kv = lax.dot_general(kvn, wukv_ref[...], NT, preferred_element_type=F32)
        kr = rope(kr_ref[...]).astype(BF16)
        for h in range(N_HEADS):
            c0 = 256 * h
            qc_ref[:, c0:c0 + 128] = q[:, c0:c0 + 128].astype(BF16)
            qc_ref[:, c0 + 128:c0 + 256] = rope(q[:, c0 + 128:c0 + 256]).astype(BF16)
            kc_ref[:, c0:c0 + 128] = kv[:, c0:c0 + 128].astype(BF16)
            kc_ref[:, c0 + 128:c0 + 256] = kr
            vh = kv[:, c0 + 128:c0 + 256]
            v_ref[:, 128 * h:128 * h + 128] = vh.astype(BF16)
            vt_ref[h] = jnp.transpose(vh).astype(BF16)

    def row(w, blk):
        return pl.BlockSpec((tm, w), lambda i: (i, blk))

    def full(shape):
        return pl.BlockSpec(shape, lambda i: (0,) * len(shape))

    t = min(TQ, S)
    per = t // tm
    vt_spec = pl.BlockSpec((N_HEADS, None, 128, tm), lambda i: (0, i // per, 0, i % per))
    outs = (jax.ShapeDtypeStruct((S, 2048), BF16), jax.ShapeDtypeStruct((S, 2048), BF16),
            jax.ShapeDtypeStruct((S, 1024), BF16), jax.ShapeDtypeStruct((N_HEADS, S // t, 128, t), BF16),
            jax.ShapeDtypeStruct((S, Q_LORA), BF16), jax.ShapeDtypeStruct((S, KV_LORA), BF16))
    return _pcall(kern, name=name, out_shape=outs, grid=(S // tm,),
                  in_specs=[row(512, 0), row(256, 2), row(128, 6), row(128, 0), row(128, 0),
                            full((1, Q_LORA)), full((1, KV_LORA)), full((2048, Q_LORA)), full((2048, KV_LORA))],
                  out_specs=(row(2048, 0), row(2048, 0), row(1024, 0), vt_spec, row(512, 0), row(256, 0)),
                  dims=("parallel",), vmem_mb=48)(
                      proj, proj, proj, cos_t, sin_t, qg.reshape(1, -1), kvg.reshape(1, -1), wuq_t, wukv_t)


def _mla_qkv_bwd(dqb, dkvb, dkr_heads, qn, kvn, proj, cos_t, sin_t, qg, kvg, wuq_t, wukv_t, *, name):
    S = proj.shape[0]
    tm = min(512, S)

    def kern(dqb_ref, dkvb_ref, dkrh_ref, qn_ref, kvn_ref, ql_ref, kvl_ref, cos_ref, sin_ref, qg_ref, kvg_ref,
             wuq_ref, wukv_ref, dml_ref, dqg_ref, dkvg_ref, dwuq_ref, dwukv_ref):
        @pl.when(pl.program_id(0) == 0)
        def _():
            dqg_ref[...] = jnp.zeros_like(dqg_ref)
            dkvg_ref[...] = jnp.zeros_like(dkvg_ref)
            dwuq_ref[...] = jnp.zeros_like(dwuq_ref)
            dwukv_ref[...] = jnp.zeros_like(dwukv_ref)

        dwuq_ref[...] += lax.dot_general(dqb_ref[...], qn_ref[...], TN, preferred_element_type=F32)
        dwukv_ref[...] += lax.dot_general(dkvb_ref[...], kvn_ref[...], TN, preferred_element_type=F32)

        cosv = cos_ref[...]
        sinv = sin_ref[...]

        def unrope(t):
            return t * cosv - _rot_sum(t) * sinv

        dkr = dkrh_ref[:, 0:128]
        for h in range(1, N_HEADS):
            dkr = dkr + dkrh_ref[:, 128 * h:128 * h + 128]

        def rms_bwd(x, g, dy):
            n = x.shape[-1]
            rs = lax.rsqrt(jnp.mean(x * x, axis=-1, keepdims=True) + RMS_EPS)
            dyg = dy * g
            dx = rs * dyg - x * (rs * rs * rs) * (jnp.sum(dyg * x, axis=-1, keepdims=True) / n)
            return dx, jnp.sum(dy * (x * rs), axis=0, keepdims=True)

        dqn = jnp.dot(dqb_ref[...], wuq_ref[...], preferred_element_type=F32)
        dql, dqg = rms_bwd(ql_ref[...], qg_ref[...], dqn)
        dqg_ref[...] += dqg
        dkvn = jnp.dot(dkvb_ref[...], wukv_ref[...], preferred_element_type=F32)
        dkvl, dkvg = rms_bwd(kvl_ref[...], kvg_ref[...], dkvn)
        dkvg_ref[...] += dkvg
        dml_ref[:, 0:512] = dql.astype(BF16)
        dml_ref[:, 512:768] = dkvl.astype(BF16)
        dml_ref[:, 768:896] = unrope(dkr).astype(BF16)
        dml_ref[:, 896:1024] = jnp.zeros((tm, 128), BF16)

    def row(w, blk):
        return pl.BlockSpec((tm, w), lambda i: (i, blk))

    def full(shape):
        return pl.BlockSpec(shape, lambda i: (0,) * len(shape))

    outs = (jax.ShapeDtypeStruct((S, W_MLA), BF16), jax.ShapeDtypeStruct((1, Q_LORA), F32),
            jax.ShapeDtypeStruct((1, KV_LORA), F32), jax.ShapeDtypeStruct((2048, Q_LORA), F32),
            jax.ShapeDtypeStruct((2048, KV_LORA), F32))
    return _pcall(kern, name=name, out_shape=outs, grid=(S // tm,),
                  in_specs=[row(2048, 0), row(2048, 0), row(1024, 0), row(512, 0), row(256, 0), row(512, 0),
                            row(256, 2), row(128, 0), row(128, 0), full((1, Q_LORA)), full((1, KV_LORA)),
                            full((2048, Q_LORA)), full((2048, KV_LORA))],
                  out_specs=(row(W_MLA, 0), full((1, Q_LORA)), full((1, KV_LORA)), full((2048, Q_LORA)),
                             full((2048, KV_LORA))),
                  dims=("arbitrary",), vmem_mb=56)(
                      dqb, dkvb, dkr_heads, qn, kvn, proj, proj, cos_t, sin_t, qg.reshape(1, -1), kvg.reshape(1, -1),
                      wuq_t, wukv_t)


def _flash_fwd(qc, kc, vt, *, name, comm=None):
    S = qc.shape[0]
    t = min(TQ, S)
    n = S // t

    def kern(q_ref, k_ref, vt_ref, o_ref, lse_ref, m_s, l_s, acc_s):
        qi = pl.program_id(1)
        m_s[...] = jnp.full_like(m_s, -jnp.inf)
        l_s[...] = jnp.zeros_like(l_s)
        acc_s[...] = jnp.zeros_like(acc_s)

        half = t // 2

        def scores(kb, q_lo=0, q_n=t, k_n=t):
            k0 = pl.multiple_of(kb * t, t)
            return lax.dot_general(k_ref[pl.ds(k0, k_n), :], q_ref[q_lo:q_lo + q_n, :], NT,
                                   preferred_element_type=F32)

        def update(kb, st, q_lo=0, diagonal=False):
            k_n, q_n = st.shape
            if diagonal:
                krow = lax.broadcasted_iota(jnp.int32, (k_n, q_n), 0)
                qcol = lax.broadcasted_iota(jnp.int32, (k_n, q_n), 1) + q_lo
                st = jnp.where(krow <= qcol, st, -jnp.inf)
            lanes = slice(q_lo, q_lo + q_n)
            m_prev = m_s[:, lanes]
            m_new = jnp.maximum(m_prev, jnp.max(st, axis=0, keepdims=True))
            a = jnp.exp2((m_prev - m_new) * SCALE_LOG2E)
            pt = jnp.exp2((st - m_new) * SCALE_LOG2E)
            l_s[:, lanes] = a * l_s[:, lanes] + jnp.sum(pt, axis=0, keepdims=True)
            acc_s[:, lanes] = a * acc_s[:, lanes] + jnp.dot(vt_ref[kb, :, 0:k_n], pt.astype(BF16),
                                                            preferred_element_type=F32)
            m_s[:, lanes] = m_new

        def group(kb, count, last_diagonal):
            whole = count - 1 if last_diagonal else count
            sts = [scores(kb + g) for g in range(whole)]
            if last_diagonal:
                kd = kb + count - 1
                s_lo, s_hi = scores(kd, 0, half, half), scores(kd, half, half, t)
            for g in range(whole):
                update(kb + g, sts[g])
            if last_diagonal:
                update(kd, s_lo, 0, True)
                update(kd, s_hi, half, True)

        def body(i, carry):
            group(FWD_GROUP * i, FWD_GROUP, False)
            return carry

        full = qi // FWD_GROUP
        lax.fori_loop(0, full, body, 0)
        for rem in range(FWD_GROUP):
            @pl.when(qi - FWD_GROUP * full == rem)
            def _():
                group(qi - rem, rem + 1, True)
        o_ref[...] = jnp.transpose(acc_s[...] / l_s[...])
        lse_ref[pl.ds(qi, 1), :] = m_s[...] * SCALE_LOG2E + jnp.log2(l_s[...])

    q_spec = pl.BlockSpec((t, 256), lambda h, qi: (qi, h))
    k_spec = pl.BlockSpec((S, 256), lambda h, qi: (0, h))
    vt_spec = pl.BlockSpec((None, n, 128, t), lambda h, qi: (h, 0, 0, 0))
    o_spec = pl.BlockSpec((t, 128), lambda h, qi: (qi, h))
    lse_spec = pl.BlockSpec((None, n, t), lambda h, qi: (h, 0, 0))
    return _pcall(kern, name=name,
                  out_shape=(jax.ShapeDtypeStruct((S, D_MLA), F32), jax.ShapeDtypeStruct((N_HEADS, n, t), F32)),
                  grid=(N_HEADS, n), in_specs=[q_spec, k_spec, vt_spec], out_specs=(o_spec, lse_spec),
                  scratch=[pltpu.VMEM((1, t), F32), pltpu.VMEM((1, t), F32), pltpu.VMEM((128, t), F32)],
                  dims=("parallel", "arbitrary"), vmem_mb=48, comm=comm)(qc, kc, vt)


def _flash_bwd(qc, kc, v, do, lse2, delta, cos_t, sin_t, *, name, comm=None):
    S = qc.shape[0]
    t = min(TQ, S)
    n = S // t

    def kern(q_ref, k_ref, v_ref, do_ref, lse_ref, dl_ref, cos_ref, sin_ref, dqb_ref, dkvb_ref, dkr_ref,
             dq_ref, dk_ref, dv_ref):
        ki = pl.program_id(1)

        @pl.when(ki == 0)
        def _():
            dq_ref[...] = jnp.zeros_like(dq_ref)

        dk_ref[...] = jnp.zeros_like(dk_ref)
        dv_ref[...] = jnp.zeros_like(dv_ref)

        half = t // 2

        def step(qb, q_lo=0, q_n=t, k_n=t, diagonal=False):
            q0 = pl.multiple_of(qb * t + q_lo, half)
            lanes = slice(q_lo, q_lo + q_n)
            kt = k_ref[0:k_n, :]
            qblk = q_ref[pl.ds(q0, q_n), :]
            dob = do_ref[pl.ds(q0, q_n), :].astype(BF16)
            st = lax.dot_general(kt, qblk, NT, preferred_element_type=F32)
            pt = jnp.exp2(st * SCALE_LOG2E - lse_ref[pl.ds(qb, 1), lanes])
            if diagonal:
                krow = lax.broadcasted_iota(jnp.int32, (k_n, q_n), 0)
                qcol = lax.broadcasted_iota(jnp.int32, (k_n, q_n), 1) + q_lo
                pt = jnp.where(krow <= qcol, pt, 0.0)
            dv_ref[0:k_n, :] += jnp.dot(pt.astype(BF16), dob, preferred_element_type=F32)
            dpt = lax.dot_general(v_ref[0:k_n, :], dob, NT, preferred_element_type=F32)
            dst = (pt * (dpt - dl_ref[pl.ds(qb, 1), lanes]) * SCALE).astype(BF16)
            dk_ref[0:k_n, :] += jnp.dot(dst, qblk, preferred_element_type=F32)
            dq_ref[pl.ds(q0, q_n), :] += lax.dot_general(dst, kt, TN, preferred_element_type=F32)

        step(ki, 0, half, half, True)
        step(ki, half, half, t, True)
        rest = n - 1 - ki
        full = rest // BWD_GROUP

        def body(i, carry):
            for g in range(BWD_GROUP):
                step(ki + 1 + BWD_GROUP * i + g)
            return carry

        lax.fori_loop(0, full, body, 0)
        for rem in range(1, BWD_GROUP):
            @pl.when(rest - BWD_GROUP * full == rem)
            def _():
                for g in range(rem):
                    step(n - rem + g)

        dkvb_ref[:, 0:128] = dk_ref[:, 0:128].astype(BF16)
        dkvb_ref[:, 128:256] = dv_ref[...].astype(BF16)
        dkr_ref[...] = dk_ref[:, 128:256]

        @pl.when(ki == n - 1)
        def _():
            dqb_ref[:, 0:128] = dq_ref[:, 0:128].astype(BF16)
            dqr = dq_ref[:, 128:256]
            dqb_ref[:, 128:256] = (dqr * cos_ref[...] - _rot_sum(dqr) * sin_ref[...]).astype(BF16)

    def whole(w):
        return pl.BlockSpec((S, w), lambda h, ki: (0, h))

    def krow(w):
        return pl.BlockSpec((t, w), lambda h, ki: (ki, h))

    stat = pl.BlockSpec((None, n, t), lambda h, ki: (h, 0, 0))
    table = pl.BlockSpec((S, 128), lambda h, ki: (0, 0))
    return _pcall(kern, name=name,
                  out_shape=(jax.ShapeDtypeStruct((S, 2048), BF16), jax.ShapeDtypeStruct((S, 2048), BF16),
                             jax.ShapeDtypeStruct((S, D_MLA), F32)),
                  grid=(N_HEADS, n),
                  in_specs=[whole(256), krow(256), krow(128), whole(128), stat, stat, table, table],
                  out_specs=(whole(256), krow(256), krow(128)),
                  scratch=[pltpu.VMEM((S, 256), F32), pltpu.VMEM((t, 256), F32), pltpu.VMEM((t, 128), F32)],
                  dims=("parallel", "arbitrary"), vmem_mb=56, comm=comm)(qc, kc, v, do, lse2, delta, cos_t, sin_t)


def _mixer_specs(S, tm):
    hb = tm // HALO
    last_hb = S // HALO - 1

    def main(w, blk):
        return pl.BlockSpec((tm, w), lambda i: (i, blk))

    def prev(w, blk):
        return pl.BlockSpec((HALO, w), lambda i: (jnp.maximum(i * hb - 1, 0), blk))

    def nxt(w, blk):
        return pl.BlockSpec((HALO, w), lambda i: (jnp.minimum((i + 1) * hb, last_hb), blk))

    def full(shape):
        return pl.BlockSpec(shape, lambda i: (0,) * len(shape))

    return main, prev, nxt, full


def _fill_halo(i, xp, xu, hp_ref, hch_ref, hcc_ref, pin_ref, ch_ref, cc_ref, tm):
    first = i == 0
    xp[0:HALO, :] = jnp.where(first, 0.0, hp_ref[...])
    xp[HALO:HALO + tm, :] = pin_ref[...]
    xu[0:HALO, :] = jnp.where(first, 0.0, hch_ref[...] * hcc_ref[...])
    xu[HALO:HALO + tm, :] = cc_ref[...] * ch_ref[...]


def _pooled(xp, g, t1, tm):
    w = POOL_WINDOWS[g]
    lanes = slice(128 * g, 128 * g + 128)
    x0 = xp[HALO:HALO + tm, lanes]
    acc = x0
    for k in range(1, w):
        acc = acc + xp[HALO - k:HALO - k + tm, lanes]
    return acc / jnp.minimum(t1, float(w)) - x0


def _conv_fwd(xu, cw_ref, tm):
    return (cw_ref[0:1, :] * xu[HALO - 2:HALO - 2 + tm, :] + cw_ref[1:2, :] * xu[HALO - 1:HALO - 1 + tm, :]
            + cw_ref[2:3, :] * xu[HALO:HALO + tm, :])


def _mixer_fwd(proj, o, wpool, ps, convw, *, name):
    S = proj.shape[0]
    tm = min(256, S)
    main, prev, _, full = _mixer_specs(S, tm)

    def kern(gm_ref, pin_ref, gp_ref, ch_ref, cb_ref, cc_ref, gc_ref, hp_ref, hch_ref, hcc_ref,
             o_ref, wp_ref, ps_ref, cw_ref, mix_ref, xp, xu):
        i = pl.program_id(0)
        _fill_halo(i, xp, xu, hp_ref, hch_ref, hcc_ref, pin_ref, ch_ref, cc_ref, tm)
        t1 = (i * tm + lax.broadcasted_iota(jnp.int32, (tm, 1), 0) + 1).astype(F32)
        for g in range(4):
            lanes = slice(128 * g, 128 * g + 128)
            pooled = _pooled(xp, g, t1, tm)
            z = jnp.dot(pooled.astype(BF16), wp_ref[g].astype(BF16), preferred_element_type=F32)
            gp = gp_ref[:, lanes]
            y = z * ps_ref[:, lanes] * (gp * _sigmoid(gp))
            mix_ref[:, 1024 + 128 * g:1024 + 128 * g + 128] = y.astype(BF16)
        gc = gc_ref[...]
        mix_ref[:, 1536:2048] = (cb_ref[...] * _conv_fwd(xu, cw_ref, tm) * (gc * _sigmoid(gc))).astype(BF16)
        gm = gm_ref[...]
        mix_ref[:, 0:1024] = (o_ref[...] * (gm * _sigmoid(gm))).astype(BF16)

    return _pcall(kern, name=name, out_shape=jax.ShapeDtypeStruct((S, 2048), BF16), grid=(S // tm,),
                  in_specs=[main(1024, 1), main(512, 4), main(512, 5), main(512, 6), main(512, 7), main(512, 8),
                            main(512, 9), prev(512, 4), prev(512, 6), prev(512, 8),
                            main(1024, 0), full((4, 128, 128)), full((1, 512)), full((3, 512))],
                  out_specs=main(2048, 0),
                  scratch=[pltpu.VMEM((tm + HALO, 512), F32), pltpu.VMEM((tm + HALO, 512), F32)],
                  dims=("parallel",), vmem_mb=48)(
                      proj, proj, proj, proj, proj, proj, proj, proj, proj, proj, o, wpool, ps.reshape(1, 512), convw)


def _mixer_bwd(dmix, proj, o, wpool, ps, convw, *, name):
    S = proj.shape[0]
    tm = min(256, S)
    n = S // tm
    t = min(TQ, S)
    per = t // tm
    main, prev, nxt, full = _mixer_specs(S, tm)

    def kern(dm_ref, dmn_ref, gm_ref, pin_ref, gp_ref, ch_ref, cb_ref, cc_ref, gc_ref,
             hp_ref, hch_ref, hcc_ref, gpn_ref, cbn_ref, gcn_ref, o_ref, wp_ref, ps_ref, cw_ref,
             d_ref, do_ref, dl_ref, dwp_ref, dps_ref, dcw_ref, xp, xu, ee, ed):
        i = pl.program_id(0)
        last = i == n - 1

        @pl.when(i == 0)
        def _():
            dwp_ref[...] = jnp.zeros_like(dwp_ref)
            dps_ref[...] = jnp.zeros_like(dps_ref)
            dcw_ref[...] = jnp.zeros_like(dcw_ref)

        _fill_halo(i, xp, xu, hp_ref, hch_ref, hcc_ref, pin_ref, ch_ref, cc_ref, tm)
        t1 = (i * tm + lax.broadcasted_iota(jnp.int32, (tm, 1), 0) + 1).astype(F32)
        t1n = ((i + 1) * tm + lax.broadcasted_iota(jnp.int32, (HALO, 1), 0) + 1).astype(F32)
        c_pin, c_gp, c_ch, c_cb, c_cc, c_gc = 1024, 1536, 2048, 2560, 3072, 3584

        for g in range(4):
            w = float(POOL_WINDOWS[g])
            lanes = slice(128 * g, 128 * g + 128)
            pooled = _pooled(xp, g, t1, tm)
            pb = pooled.astype(BF16)
            wp = wp_ref[g].astype(BF16)
            z = jnp.dot(pb, wp, preferred_element_type=F32)
            psl = ps_ref[:, lanes]
            sg, dsg = _silu_and_grad(gp_ref[:, lanes])
            dmp = dm_ref[:, 1024 + 128 * g:1024 + 128 * g + 128]
            dyp = dmp * sg
            d_ref[:, c_gp + 128 * g:c_gp + 128 * g + 128] = (dmp * (z * psl) * dsg).astype(BF16)
            dps_ref[:, lanes] += jnp.sum(dyp * z, axis=0, keepdims=True)
            dz = (dyp * psl).astype(BF16)
            dwp_ref[g] += lax.dot_general(pb, dz, TN, preferred_element_type=F32)
            dpl = lax.dot_general(dz, wp, NT, preferred_element_type=F32)
            ee[0:tm, lanes] = dpl / jnp.minimum(t1, w)
            gpn = gpn_ref[:, lanes]
            dzn = (dmn_ref[:, lanes] * (gpn * _sigmoid(gpn)) * psl).astype(BF16)
            dpn = lax.dot_general(dzn, wp, NT, preferred_element_type=F32)
            ee[tm:tm + HALO, lanes] = jnp.where(last, 0.0, dpn / jnp.minimum(t1n, w))
            acc = ee[0:tm, lanes]
            for k in range(1, POOL_WINDOWS[g]):
                acc = acc + ee[k:k + tm, lanes]
            d_ref[:, c_pin + 128 * g:c_pin + 128 * g + 128] = (acc - dpl).astype(BF16)

        yc = _conv_fwd(xu, cw_ref, tm)
        sgc, dsgc = _silu_and_grad(gc_ref[...])
        cb = cb_ref[...]
        dmc = dm_ref[:, 1536:2048]
        d_ref[:, c_gc:c_gc + 512] = (dmc * cb * yc * dsgc).astype(BF16)
        d_ref[:, c_cb:c_cb + 512] = (dmc * yc * sgc).astype(BF16)
        dyc = dmc * cb * sgc
        ed[0:tm, :] = dyc
        gcn = gcn_ref[...]
        ed[tm:tm + HALO, :] = jnp.where(last, 0.0, dmn_ref[:, 512:1024] * cbn_ref[...] * (gcn * _sigmoid(gcn)))
        dcw_ref[0:1, :] += jnp.sum(dyc * xu[HALO - 2:HALO - 2 + tm, :], axis=0, keepdims=True)
        dcw_ref[1:2, :] += jnp.sum(dyc * xu[HALO - 1:HALO - 1 + tm, :], axis=0, keepdims=True)
        dcw_ref[2:3, :] += jnp.sum(dyc * xu[HALO:HALO + tm, :], axis=0, keepdims=True)
        du = cw_ref[2:3, :] * dyc + cw_ref[1:2, :] * ed[1:1 + tm, :] + cw_ref[0:1, :] * ed[2:2 + tm, :]
        d_ref[:, c_cc:c_cc + 512] = (du * ch_ref[...]).astype(BF16)
        d_ref[:, c_ch:c_ch + 512] = (du * cc_ref[...]).astype(BF16)

        sgm, dsgm = _silu_and_grad(gm_ref[...])
        dmm = dm_ref[:, 0:1024]
        ov = o_ref[...]
        dov = dmm * sgm
        do_ref[...] = dov
        d_ref[:, 0:1024] = (dmm * ov * dsgm).astype(BF16)
        lane = lax.broadcasted_iota(jnp.int32, (tm, LANE), 1)
        dmat = jnp.zeros((tm, LANE), F32)
        for h in range(N_HEADS):
            hs = slice(128 * h, 128 * h + 128)
            dmat = jnp.where(lane == h, jnp.sum(dov[:, hs] * ov[:, hs], axis=1, keepdims=True), dmat)
        dmat_t = jnp.transpose(dmat)
        for part in range(per):
            @pl.when(i % per == part)
            def _():
                for h in range(N_HEADS):
                    dl_ref[h, pl.ds(i // per, 1), part * tm:(part + 1) * tm] = dmat_t[h:h + 1, :]

    outs = (jax.ShapeDtypeStruct((S, W_MIX), BF16), jax.ShapeDtypeStruct((S, 1024), F32),
            jax.ShapeDtypeStruct((N_HEADS, S // t, t), F32),
            jax.ShapeDtypeStruct((4, 128, 128), F32), jax.ShapeDtypeStruct((1, 512), F32),
            jax.ShapeDtypeStruct((3, 512), F32))
    scr = [pltpu.VMEM((tm + HALO, 512), F32) for _ in range(4)]
    return _pcall(kern, name=name, out_shape=outs, grid=(n,),
                  in_specs=[main(2048, 0), nxt(1024, 1),
                            main(1024, 1), main(512, 4), main(512, 5), main(512, 6), main(512, 7), main(512, 8),
                            main(512, 9), prev(512, 4), prev(512, 6), prev(512, 8),
                            nxt(512, 5), nxt(512, 7), nxt(512, 9),
                            main(1024, 0), full((4, 128, 128)), full((1, 512)), full((3, 512))],
                  out_specs=(main(W_MIX, 0), main(1024, 0), full((N_HEADS, S // t, t)), full((4, 128, 128)),
                             full((1, 512)), full((3, 512))),
                  scratch=scr, dims=("arbitrary",), vmem_mb=56)(
                      dmix, dmix, proj, proj, proj, proj, proj, proj, proj, proj, proj, proj, proj, proj, proj,
                      o, wpool, ps.reshape(1, 512), convw)


def _outproj_residual(mix, wout, h, bout, *, name):
    S, Dm = h.shape
    tm = min(512, S)

    def kern(mix_ref, w_ref, h_ref, bo_ref, r_ref):
        out = jnp.dot(mix_ref[...], w_ref[...], preferred_element_type=F32) + bo_ref[...]
        r_ref[...] = ALPHA * h_ref[...] + out

    row = pl.BlockSpec((tm, Dm), lambda i: (i, 0))
    vec = pl.BlockSpec((1, Dm), lambda i: (0, 0))
    wsp = pl.BlockSpec((Dm, Dm), lambda i: (0, 0), pipeline_mode=pl.Buffered(1))
    return _pcall(kern, name=name, out_shape=jax.ShapeDtypeStruct((S, Dm), F32), grid=(S // tm,),
                  in_specs=[row, wsp, row, vec], out_specs=row, dims=("parallel",), vmem_mb=56)(
                      mix, wout, h, bout.reshape(1, Dm))


def _outproj_ln(mix, wout, h, bout, g, b, *, name):
    S, Dm = h.shape
    tm = min(512, S)

    def kern(mix_ref, w_ref, h_ref, bo_ref, g_ref, b_ref, y_ref, yb_ref, r_ref):
        out = jnp.dot(mix_ref[...], w_ref[...], preferred_element_type=F32) + bo_ref[...]
        r = ALPHA * h_ref[...] + out
        r_ref[...] = r
        mu = jnp.mean(r, axis=-1, keepdims=True)
        xc = r - mu
        var = jnp.mean(xc * xc, axis=-1, keepdims=True)
        y = xc * lax.rsqrt(var + LN_EPS) * g_ref[...] + b_ref[...]
        y_ref[...] = y
        yb_ref[...] = y.astype(BF16)

    row = pl.BlockSpec((tm, Dm), lambda i: (i, 0))
    vec = pl.BlockSpec((1, Dm), lambda i: (0, 0))
    wsp = pl.BlockSpec((Dm, Dm), lambda i: (0, 0), pipeline_mode=pl.Buffered(1))
    sds = jax.ShapeDtypeStruct((S, Dm), F32)
    return _pcall(kern, name=name, out_shape=(sds, jax.ShapeDtypeStruct((S, Dm), BF16), sds), grid=(S // tm,),
                  in_specs=[row, wsp, row, vec, vec, vec], out_specs=(row, row, row), dims=("parallel",),
                  vmem_mb=56)(
                      mix, wout, h, bout.reshape(1, Dm), g.reshape(1, Dm), b.reshape(1, Dm))


def _adamw_math(w, g, m, v):
    m = ADAM_B1 * m + (1.0 - ADAM_B1) * g
    v = ADAM_B2 * v + (1.0 - ADAM_B2) * (g * g)
    m_hat = m / (1.0 - ADAM_B1 ** ADAM_STEP)
    v_hat = v / (1.0 - ADAM_B2 ** ADAM_STEP)
    delta = -ADAM_LR * (m_hat / (jnp.sqrt(v_hat) + ADAM_EPS) + ADAM_WD * w)
    return delta, m, v


def _row_tile(R, C):
    best = None
    for cand in range(8, R, 8):
        if R % cand == 0 and cand * C <= 256 * 1024:
            best = cand
    return best if best is not None else R


def _adamw(w, g, m, v, *, name):
    shape = w.shape
    C = shape[-1]
    R = 1
    for s in shape[:-1]:
        R *= s
    tr = _row_tile(R, C)

    def kern(w_ref, g_ref, m_ref, v_ref, d_ref, mo_ref, vo_ref):
        d, mn, vn = _adamw_math(w_ref[...], g_ref[...], m_ref[...], v_ref[...])
        d_ref[...] = d
        mo_ref[...] = mn
        vo_ref[...] = vn

    blk = pl.BlockSpec((tr, C), lambda i: (i, 0))
    sds = jax.ShapeDtypeStruct((R, C), F32)
    outs = _pcall(kern, name=name, out_shape=(sds, sds, sds), grid=(R // tr,), in_specs=[blk] * 4,
                  out_specs=(blk, blk, blk), dims=("parallel",), vmem_mb=48)(
                      w.reshape(R, C), g.reshape(R, C), m.reshape(R, C), v.reshape(R, C))
    return tuple(t.reshape(shape) for t in outs)


def _adamw_halves(w, m, v, halves, c_idx, *, name, comm=None):
    _, R, C = w.shape
    ch = C // 2
    tr = _row_tile(R, ch)
    nb = R // tr

    def kern(c_ref, w_ref, a0_ref, b0_ref, a1_ref, b1_ref, m_ref, v_ref, g_ref, d_ref, mo_ref, vo_ref):
        layer = pl.program_id(0) // nb
        mine = pl.program_id(1) == c_ref[0]
        g = jnp.where(layer == 0, jnp.where(mine, a0_ref[...], b0_ref[...]),
                      jnp.where(mine, a1_ref[...], b1_ref[...]))
        g_ref[...] = g
        d, mn, vn = _adamw_math(w_ref[...], g, m_ref[...], v_ref[...])
        d_ref[...] = d
        mo_ref[...] = mn
        vo_ref[...] = vn

    full = pl.BlockSpec((tr, ch), lambda i, hc: (i, hc))
    half = pl.BlockSpec((tr, ch), lambda i, hc: (i % nb, 0))
    sds = jax.ShapeDtypeStruct((2 * R, C), F32)
    (a0, b0), (a1, b1) = halves
    res = _pcall(kern, name=name, out_shape=(sds,) * 4, grid=(2 * nb, 2),
                 in_specs=[pl.BlockSpec(memory_space=pltpu.SMEM), full, half, half, half, half, full, full],
                 out_specs=(full,) * 4, dims=("parallel", "parallel"), vmem_mb=48, comm=comm)(
                     c_idx, w.reshape(2 * R, C), a0, b0, a1, b1, m.reshape(2 * R, C), v.reshape(2 * R, C))
    outs, landed = res if comm is not None else (res, None)
    outs = tuple(t.reshape(2, R, C) for t in outs)
    return outs if comm is None else (outs, landed)


def _packed_pieces(shape):
    if len(shape) == 4:
        return [((l * shape[1] + g) * 128, 128, (l, g)) for l in range(shape[0]) for g in range(shape[1])]
    per_row = shape[1] // LANE
    return [(a * per_row + j, 1, (slice(a, a + 1), slice(LANE * j, LANE * (j + 1))))
            for a in range(shape[0]) for j in range(per_row)]


def _small_sum_adamw(gathered, own, weights, *, name):
    R = gathered.shape[1]
    nw = len(weights)
    shapes = [w.shape for w, _, _ in weights]
    first_row, r0 = [], 0
    for shp in shapes:
        first_row.append(r0)
        n = 1
        for s in shp:
            n *= s
        r0 += n // LANE

    def kern(ga_ref, own_ref, *refs):
        ins, gsum_ref, outs = refs[:3 * nw], refs[3 * nw], refs[3 * nw + 1:]
        me = 4 * lax.axis_index("x") + 2 * lax.axis_index("y") + lax.axis_index("c")

        def block(k):
            other = ga_ref[jnp.where(me == k, (k + 1) % N_DEV, k)]
            return jnp.where(me == k, own_ref[...], other)

        g = block(0)
        for k in range(1, N_DEV):
            g = g + block(k)
        gsum_ref[...] = g
        for p, shp in enumerate(shapes):
            w_ref, m_ref, v_ref = ins[3 * p:3 * p + 3]
            g_out, d_out, m_out, v_out = outs[4 * p:4 * p + 4]
            for row, rows, idx in _packed_pieces(shp):
                gp = gsum_ref[first_row[p] + row:first_row[p] + row + rows, :]
                d, mn, vn = _adamw_math(w_ref[idx], gp, m_ref[idx], v_ref[idx])
                g_out[idx] = gp
                d_out[idx] = d
                m_out[idx] = mn
                v_out[idx] = vn

    out_shape = [jax.ShapeDtypeStruct((R, LANE), F32)]
    for shp in shapes:
        out_shape += [jax.ShapeDtypeStruct(shp, F32)] * 4
    flat = [a for wmv in weights for a in wmv]
    res = _pcall(kern, name=name, out_shape=tuple(out_shape), vmem_mb=48)(gathered, own, *flat)
    return res[0], [tuple(res[1 + 4 * p:5 + 4 * p]) for p in range(nw)]


def _pair_sum(g, theirs, c_idx, *, name):
    R, C = g.shape
    ch = C // 2
    tr = _row_tile(R, ch)

    def kern(c_ref, a_ref, b_ref, o_ref):
        o_ref[...] = (a_ref[...] + b_ref[...]).astype(BF16)

    gs = pltpu.PrefetchScalarGridSpec(
        num_scalar_prefetch=1, grid=(R // tr,),
        in_specs=[pl.BlockSpec((tr, ch), lambda i, c: (i, c[0])), pl.BlockSpec((tr, ch), lambda i, c: (i, 0))],
        out_specs=pl.BlockSpec((tr, ch), lambda i, c: (i, 0)))
    return pl.pallas_call(kern, name=name, out_shape=jax.ShapeDtypeStruct((R, ch), BF16), grid_spec=gs,
                          compiler_params=pltpu.CompilerParams(dimension_semantics=("parallel",),
                                                               vmem_limit_bytes=48 << 20))(c_idx, g, theirs)


WeightRows = collections.namedtuple("WeightRows", "full_rows own_rows cols pieces zero_rows")


def _w_in_piece_a(j):
    return jnp.where(j == 0, 0, 1232 * j + GAP)


def _w_in_piece_b(j):
    return jnp.where(j == 0, GAP_AT + GAP, 1232 * j + GAP_AT + GAP)


W_IN = WeightRows(NP, 1232, D_MODEL, ((0, GAP_AT, _w_in_piece_a), (GAP_AT, 1232 - GAP_AT, _w_in_piece_b)),
                  ((GAP_AT, GAP),))
W_OUT = WeightRows(2048, 512, D_MODEL, ((0, 512, lambda j: 512 * j),), ())
W_UQ = WeightRows(2048, 384, Q_LORA, ((0, 192, lambda j: 512 * j), (192, 192, lambda j: 512 * j + 256)),
                  tuple((256 * h + 192, 64) for h in range(N_HEADS)))
W_UKV = WeightRows(2048, 512, KV_LORA, ((0, 512, lambda j: 512 * j),), ())
W_CONV = WeightRows(64, 16, 256, ((0, 16, lambda j: 16 * j),), ())
SHARDED = (W_IN, W_OUT, W_UQ, W_UKV)
SHARDED_NAMES = ("w_in", "w_out", "w_uq", "w_ukv")
WEIGHT_ROWS = dict(zip(SHARDED_NAMES, SHARDED))


def _mesh_pos():
    x, y, c = lax.axis_index("x"), lax.axis_index("y"), lax.axis_index("c")
    return x, y, c


def _other_chips(x, y):
    return [(1 - x, y), (x, 1 - y), (1 - x, 1 - y)]


def _rows(start, n):
    return pl.ds(pl.multiple_of(start, 16), n)


def _half_cols(spec, c):
    ch = spec.cols // 2
    return pl.ds(pl.multiple_of(c * ch, LANE), ch)


def _allgather_script(specs, shards, zeros, layers):
    na = len(specs)
    zlist = [a for a in range(na) if zeros[a] is not None]
    n_layers = [shards[a].shape[0] if layers[a] is None else 1 for a in range(na)]
    plan_first, plan_own, plan_zero = [], [], []
    for a, spec in enumerate(specs):
        for p in range(len(spec.pieces)):
            plan_own.append((a, p))
            for k in range(3):
                plan_first.append((a, p, k))
        for z in range(len(spec.zero_rows)):
            for l in range(n_layers[a]):
                plan_zero.append((a, z, l))
    nf = len(plan_first)
    n_sems = 2 * nf + len(plan_own) + len(plan_zero)

    def copies(ins_all, outs, send_sems, recv_sems):
        ins = [ins_all[a] if layers[a] is None else ins_all[a].at[pl.ds(layers[a], 1)] for a in range(na)]
        zrefs = dict(zip(zlist, ins_all[na:]))
        x, y, c = _mesh_pos()
        j = 2 * x + y
        chips = _other_chips(x, y)
        sibling = (x, y, 1 - c)

        def remote(src, dst, sem, to):
            return pltpu.make_async_remote_copy(src_ref=src, dst_ref=dst, send_sem=send_sems.at[sem],
                                                recv_sem=recv_sems.at[sem], device_id=to, device_id_type=MESH)

        def block(a, p, chip, cols):
            _, n, dst = specs[a].pieces[p]
            return outs[a].at[:, _rows(dst(chip), n), cols]

        def first(i):
            a, p, k = plan_first[i]
            src0, n, _ = specs[a].pieces[p]
            cols = _half_cols(specs[a], c)
            return remote(ins[a].at[:, pl.ds(src0, n), cols], block(a, p, j, cols), i, (*chips[k], c))

        def landed(i, half):
            a, p, k = plan_first[i]
            return block(a, p, 2 * chips[k][0] + chips[k][1], _half_cols(specs[a], half))

        def arrival(i, half, sem):
            return remote(landed(i, half), landed(i, half), sem, sibling)

        def passed(i):
            return remote(landed(i, c), landed(i, c), nf + i, sibling)

        def own(i):
            a, p = plan_own[i]
            src0, n, _ = specs[a].pieces[p]
            return remote(ins[a].at[:, pl.ds(src0, n), :], block(a, p, j, slice(None)), 2 * nf + i, sibling)

        def zero(i):
            a, z, l = plan_zero[i]
            r0, n = specs[a].zero_rows[z]
            return remote(zrefs[a].at[pl.ds(0, n), :], outs[a].at[l, pl.ds(r0, n), :],
                          2 * nf + len(plan_own) + i, sibling)

        fixed = [own(i) for i in range(len(plan_own))] + [zero(i) for i in range(len(plan_zero))]
        return c, fixed, first, arrival, passed

    def start(ins, outs, send_sems, recv_sems):
        _, fixed, first, _, _ = copies(ins, outs, send_sems, recv_sems)
        for cp in fixed:
            cp.start()
        for i in range(nf):
            first(i).start()

    def finish(ins, outs, send_sems, recv_sems):
        c, fixed, first, arrival, passed = copies(ins, outs, send_sems, recv_sems)
        for i in range(nf):
            arrival(i, c, i).wait_recv()
            passed(i).start()
        for i in range(nf):
            arrival(i, 1 - c, nf + i).wait_recv()
        for cp in fixed:
            cp.wait()
        for i in range(nf):
            first(i).wait_send()
            passed(i).wait_send()

    out_shape = tuple(jax.ShapeDtypeStruct((n_layers[a], spec.full_rows, spec.cols), BF16)
                      for a, spec in enumerate(specs))
    args = tuple(shards) + tuple(zeros[a] for a in zlist)
    return CommScript(args, out_shape, n_sems, start, finish)


def _start_all_wait_all(args, out_shape, n_sems, make_copies):
    def start(ins, outs, send_sems, recv_sems):
        for cp in make_copies(ins, outs, send_sems, recv_sems):
            cp.start()

    def finish(ins, outs, send_sems, recv_sems):
        for cp in make_copies(ins, outs, send_sems, recv_sems):
            cp.wait()

    return CommScript(tuple(args), tuple(out_shape), n_sems, start, finish)


def _exchange_script(specs, grads):
    na = len(grads)

    def make_copies(ins, outs, send_sems, recv_sems):
        x, y, c = _mesh_pos()
        return [pltpu.make_async_remote_copy(
            src_ref=ins[a].at[:, _half_cols(specs[a], 1 - c)], dst_ref=outs[a], send_sem=send_sems.at[a],
            recv_sem=recv_sems.at[a], device_id=(x, y, 1 - c), device_id_type=MESH) for a in range(na)]

    out_shape = [jax.ShapeDtypeStruct((s.full_rows, s.cols // 2), F32) for s in specs]
    return _start_all_wait_all(grads, out_shape, na, make_copies)


def _scatter_script(specs, parts):
    na = len(parts)
    plan = [(a, p, k) for a in range(na) for p in range(len(specs[a].pieces)) for k in range(3)]

    def make_copies(ins, outs, send_sems, recv_sems):
        x, y, c = _mesh_pos()
        chips = _other_chips(x, y)
        copies = []
        for i, (a, p, k) in enumerate(plan):
            src0, n, dst = specs[a].pieces[p]
            pk = 2 * chips[k][0] + chips[k][1]
            copies.append(pltpu.make_async_remote_copy(
                src_ref=ins[a].at[_rows(dst(pk), n), :], dst_ref=outs[a].at[k, pl.ds(src0, n), :],
                send_sem=send_sems.at[i], recv_sem=recv_sems.at[i], device_id=(*chips[k], c), device_id_type=MESH))
        return copies

    out_shape = [jax.ShapeDtypeStruct((3, s.own_rows, s.cols // 2), BF16) for s in specs]
    return _start_all_wait_all(parts, out_shape, len(plan), make_copies)


def _chip_sum(spec, part, recv, *, name):
    ch = spec.cols // 2
    npieces = len(spec.pieces)

    def kern(recv_ref, part_ref, o_ref, own_ref, sems):
        j = 2 * lax.axis_index("x") + lax.axis_index("y")
        copies = []
        for p, (src0, n, dst) in enumerate(spec.pieces):
            copies.append(pltpu.make_async_copy(part_ref.at[_rows(dst(j), n), :], own_ref.at[pl.ds(src0, n), :],
                                                sems.at[p]))
        for cp in copies:
            cp.start()
        for cp in copies:
            cp.wait()
        o_ref[...] = ((own_ref[...].astype(F32) + recv_ref[0].astype(F32)) + recv_ref[1].astype(F32)) \
            + recv_ref[2].astype(F32)

    vm = pl.BlockSpec(memory_space=pltpu.VMEM)
    return _pcall(kern, name=name, out_shape=jax.ShapeDtypeStruct((spec.own_rows, ch), F32),
                  in_specs=[vm, HBM_SPEC], out_specs=vm,
                  scratch=[pltpu.VMEM((spec.own_rows, ch), BF16), pltpu.SemaphoreType.DMA((npieces,))],
                  vmem_mb=48)(recv, part)


def _sibling_script(sums):
    na = len(sums)

    def make_copies(ins, outs, send_sems, recv_sems):
        x, y, c = _mesh_pos()
        return [pltpu.make_async_remote_copy(
            src_ref=ins[a], dst_ref=outs[a], send_sem=send_sems.at[a], recv_sem=recv_sems.at[a],
            device_id=(x, y, 1 - c), device_id_type=MESH) for a in range(na)]

    out_shape = [jax.ShapeDtypeStruct(t.shape, t.dtype) for t in sums]
    return _start_all_wait_all(sums, out_shape, na, make_copies)


class _SemWindow:
    def __init__(self, sems, offset):
        self._sems, self._offset = sems, offset

    @property
    def at(self):
        return self

    def __getitem__(self, i):
        return self._sems.at[i + self._offset]


def _merge_scripts(*scripts):
    a_off, o_off, s_off = [0], [0], [0]
    for s in scripts:
        a_off.append(a_off[-1] + len(s.args))
        o_off.append(o_off[-1] + len(s.out_shape))
        s_off.append(s_off[-1] + s.n_sems)

    def phase(which):
        def run(ins, outs, send_sems, recv_sems):
            for n, s in enumerate(scripts):
                getattr(s, which)(ins[a_off[n]:a_off[n + 1]], outs[o_off[n]:o_off[n + 1]],
                                  _SemWindow(send_sems, s_off[n]), _SemWindow(recv_sems, s_off[n]))
        return run

    return CommScript(sum((tuple(s.args) for s in scripts), ()), sum((tuple(s.out_shape) for s in scripts), ()),
                      s_off[-1], phase("start"), phase("finish"))


class _GradReducer:
    def __init__(self, layer, names, grads, c_idx):
        self.specs = tuple(WEIGHT_ROWS[nm] for nm in names)
        self.grads, self.c_idx = tuple(grads), c_idx
        self.names = [f"{nm}{layer}" for nm in names]

    def exchange(self):
        return _exchange_script(self.specs, self.grads)

    def scatter(self, theirs):
        self.parts = tuple(_pair_sum(g, th, self.c_idx, name=f"pair_sum_{nm}")
                           for g, th, nm in zip(self.grads, theirs, self.names))
        return _scatter_script(self.specs, self.parts)

    def sibling(self, recv):
        self.sums = tuple(_chip_sum(s, p, r, name=f"chip_sum_{nm}")
                          for s, p, r, nm in zip(self.specs, self.parts, recv, self.names))
        return _sibling_script(self.sums)

    def done(self, others):
        return list(zip(self.sums, others))


def _allgather_small_script(block):
    m_per, n = block.shape

    def copies(ins, outs, send_sems, recv_sems):
        (x_ref,), (out_ref,) = ins, outs
        x, y, c = _mesh_pos()
        me, sibling = (x, y, c), (x, y, 1 - c)
        chips = _other_chips(x, y)

        def rows(px, py, pc):
            return out_ref.at[4 * px + 2 * py + pc]

        def copy(k, blk, to, src=None):
            return pltpu.make_async_remote_copy(
                src_ref=rows(*blk) if src is None else src, dst_ref=rows(*blk), send_sem=send_sems.at[k],
                recv_sem=recv_sems.at[k], device_id=to, device_id_type=MESH)

        first = [copy(0, me, sibling, src=x_ref)]
        first += [copy(1 + k, me, (*chip, c), src=x_ref) for k, chip in enumerate(chips)]
        passed = [copy(4 + k, (*chip, c), sibling) for k, chip in enumerate(chips)]
        landed = [copy(1 + k, (*chip, c), me) for k, chip in enumerate(chips)]
        from_sibling = [copy(0, sibling, me)] + [copy(4 + k, (*chip, 1 - c), me) for k, chip in enumerate(chips)]
        return first, passed, landed, from_sibling

    def start(ins, outs, send_sems, recv_sems):
        first, _, _, _ = copies(ins, outs, send_sems, recv_sems)
        for cp in first:
            cp.start()

    def finish(ins, outs, send_sems, recv_sems):
        first, passed, landed, from_sibling = copies(ins, outs, send_sems, recv_sems)
        for k in range(3):
            landed[k].wait_recv()
            passed[k].start()
        for cp in from_sibling:
            cp.wait_recv()
        for cp in first + passed:
            cp.wait_send()

    return CommScript((block,), (jax.ShapeDtypeStruct((N_DEV, m_per, n), block.dtype),), 7, start, finish)


def _rope_tables(positions):
    half = ROPE // 2
    inv_freq = ROPE_THETA ** (-jnp.arange(half, dtype=F32) / half)
    ang = positions.astype(F32)[:, None] * inv_freq
    cos, sin = jnp.cos(ang), jnp.sin(ang)
    S = positions.shape[0]
    cos_t = jnp.concatenate([cos, cos, jnp.ones((S, 64), F32)], axis=1)
    sin_t = jnp.concatenate([-sin, sin, jnp.zeros((S, 64), F32)], axis=1)
    return cos_t, sin_t


def _decode_conv(bits):
    rows = bits.reshape(DEPTH, N_CHIPS, 16, 256)[:, :, :3, :]
    conv = lax.bitcast_convert_type(rows.reshape(DEPTH, N_CHIPS, 3, 128, 2), F32)
    return jnp.transpose(conv, (0, 2, 1, 3)).reshape(DEPTH, 3, 512)


def _local_step(x, positions, target, emb_g, emb_b, w_in_t0, rest0, weights1, q_g, kv_g, w_pool, pool_scale,
                b_out, ln_g, ln_b, c_idx=None):
    cos_t, sin_t = _rope_tables(positions)
    if isinstance(w_in_t0, CommScript):
        (h, hb), (landed,) = _ln_fwd(x, emb_g, emb_b, name="emb_ln", comm=w_in_t0)
        w_in_t0 = landed[0]
    else:
        h, hb = _ln_fwd(x, emb_g, emb_b, name="emb_ln")
    weights = [None, weights1]
    saved = []
    for l in range(DEPTH):
        if l == 0 and isinstance(rest0, CommScript):
            proj, landed = _matmul(hb, w_in_t0, "nt", name="in_proj0", tm=1024, tn=1024, tk=2048, vmem_mb=56,
                                   comm=rest0)
            weights[0] = (w_in_t0,) + tuple(a[0] for a in landed[:3])
            conv_w = _decode_conv(landed[3])
        else:
            if l == 0:
                weights[0] = (w_in_t0,) + tuple(rest0[:3])
                conv_w = rest0[3]
            proj = _matmul(hb, weights[l][0], "nt", name=f"in_proj{l}", tm=1024, tn=1024, tk=2048, vmem_mb=56)
        w_in_t, w_out, w_uq_t, w_ukv_t = weights[l]
        qc, kc, v, vt, qn, kvn = _mla_qkv(proj, cos_t, sin_t, q_g[l], kv_g[l], w_uq_t, w_ukv_t, name=f"mla_qkv{l}")
        nxt = weights[l + 1] if l + 1 < DEPTH else None
        if isinstance(nxt, CommScript):
            (o, lse2), landed = _flash_fwd(qc, kc, vt, name=f"flash_fwd{l}", comm=nxt)
            weights[l + 1] = tuple(a[0] for a in landed)
        else:
            o, lse2 = _flash_fwd(qc, kc, vt, name=f"flash_fwd{l}")
        mix = _mixer_fwd(proj, o, w_pool[l], pool_scale[l], conv_w[l], name=f"mixer_fwd{l}")
        if l == DEPTH - 1:
            r = _outproj_residual(mix, w_out, h, b_out[l], name=f"out_proj{l}")
            saved.append((hb, proj, qc, kc, v, qn, kvn, o, lse2, mix, r))
        else:
            h_next, hb_next, r = _outproj_ln(mix, w_out, h, b_out[l], ln_g[l], ln_b[l], name=f"out_proj_ln{l}")
            saved.append((hb, proj, qc, kc, v, qn, kvn, o, lse2, mix, r))
            h, hb = h_next, hb_next

    small = [None] * DEPTH
    big = [None] * DEPTH
    above = scatter_above = None
    for l in reversed(range(DEPTH)):
        w_in_t, w_out, w_uq_t, w_ukv_t = weights[l]
        hb_in, proj, qc, kc, v, qn, kvn, o, lse2, mix, r = saved[l]
        if l == DEPTH - 1:
            loss_acc, dr, drb, d_ln_g, d_ln_b, d_b_out = _loss_ln_bwd(target, r, ln_g[l], ln_b[l], name="loss_ln_bwd")
        else:
            dr, drb, d_ln_g, d_ln_b, d_b_out = _ln_bwd(dh, r, ln_g[l], name=f"ln_bwd{l}")
        dmix = _matmul(drb, w_out, "nt", name=f"dmix{l}", tm=1024, tn=1024, tk=2048, vmem_mb=56)
        d_w_out = _matmul(mix, drb, "tn", name=f"dw_out{l}", tm=1024, tn=1024, tk=2048, vmem_mb=56)
        d_mix, do, delta, d_w_pool, d_ps, d_conv = _mixer_bwd(dmix, proj, o, w_pool[l], pool_scale[l], conv_w[l],
                                                              name=f"mixer_bwd{l}")
        if above is not None:
            (dqb, dkvb, dkr), recv = _flash_bwd(qc, kc, v, do, lse2, delta, cos_t, sin_t, name=f"flash_bwd{l}",
                                                comm=scatter_above)
            sibling_above = above.sibling(recv)
        else:
            dqb, dkvb, dkr = _flash_bwd(qc, kc, v, do, lse2, delta, cos_t, sin_t, name=f"flash_bwd{l}")
        d_mla, d_qg, d_kvg, d_w_uq_t, d_w_ukv_t = _mla_qkv_bwd(
            dqb, dkvb, dkr, qn, kvn, proj, cos_t, sin_t, q_g[l], kv_g[l], w_uq_t, w_ukv_t, name=f"mla_qkv_bwd{l}")
        small[l] = dict(q_g=d_qg[0], kv_g=d_kvg[0], w_pool=d_w_pool, pool_scale=d_ps[0], conv_w=d_conv,
                        b_out=d_b_out[0], ln_g=d_ln_g[0], ln_b=d_ln_b[0])
        rest = (d_w_out, d_w_uq_t, d_w_ukv_t)
        if c_idx is None:
            d_w_in_t = _dproj_t_times_h(d_mla, d_mix, hb_in, name=f"dw_in{l}")
            dh = _dproj_times_w(d_mla, d_mix, w_in_t, dr, ALPHA, name=f"dh{l}")
            big[l] = (d_w_in_t,) + rest
        elif l > 0:
            d_w_in_t = _dproj_t_times_h(d_mla, d_mix, hb_in, name=f"dw_in{l}")
            above = _GradReducer(l, SHARDED_NAMES, (d_w_in_t,) + rest, c_idx)
            dh, theirs = _dproj_times_w(d_mla, d_mix, w_in_t, dr, ALPHA, name=f"dh{l}", comm=above.exchange())
            scatter_above = above.scatter(theirs)
        else:
            red_rest = _GradReducer(l, SHARDED_NAMES[1:], rest, c_idx)
            d_w_in_t, landed = _dproj_t_times_h(d_mla, d_mix, hb_in, name=f"dw_in{l}",
                                                comm=_merge_scripts(sibling_above, red_rest.exchange()))
            big[l + 1] = above.done(landed[:len(SHARDED)])
            red_in = _GradReducer(l, SHARDED_NAMES[:1], (d_w_in_t,), c_idx)
            landed = _run_comm(_merge_scripts(red_in.exchange(), red_rest.scatter(landed[len(SHARDED):])),
                               name="exchange_w_in0")
            sibling_rest = red_rest.sibling(landed[1:])
            dh, landed = _dproj_times_w(d_mla, d_mix, w_in_t, dr, ALPHA, name=f"dh{l}",
                                        comm=_merge_scripts(red_in.scatter(landed[:1]), sibling_rest))
            recv_in, others_rest = landed[:1], landed[1:]
    grad_x, _, d_emb_g, d_emb_b, _ = _ln_bwd(dh, x, emb_g, name="emb_ln_bwd", bf16_copy=False)
    if c_idx is not None:
        others_in = _run_comm(red_in.sibling(recv_in), name="send_to_sibling0")
        big[0] = red_in.done(others_in) + red_rest.done(others_rest)
    return loss_acc[0, 0], grad_x, d_emb_g, d_emb_b, small, big


SMALL_ORDER = ("emb_ln_g", "emb_ln_b", "q_norm_g", "kv_norm_g", "w_pool", "pool_scale", "b_out", "ln_g", "ln_b")
SMALL_LAYER_KEYS = ("q_g", "kv_g", "w_pool", "pool_scale", "b_out", "ln_g", "ln_b", "conv_w")


def _pack_small(arrs, extra_rows):
    flat = jnp.concatenate([a.reshape(-1) for a in arrs])
    rows = flat.shape[0] // LANE
    total = -(-(rows + extra_rows) // 8) * 8
    return jnp.pad(flat, (0, total * LANE - flat.shape[0])).reshape(total, LANE)


def kernel(x, positions, emb_ln_g, emb_ln_b, w_in, q_norm_g, kv_norm_g, w_uq, w_ukv, w_pool, pool_scale, conv_w, w_out, b_out, ln_g, ln_b, loss_target, m_emb_ln_g, m_emb_ln_b, m_w_in, m_q_norm_g, m_kv_norm_g, m_w_uq, m_w_ukv, m_w_pool, m_pool_scale, m_conv_w, m_w_out, m_b_out, m_ln_g, m_ln_b, v_emb_ln_g, v_emb_ln_b, v_w_in, v_q_norm_g, v_kv_norm_g, v_w_uq, v_w_ukv, v_w_pool, v_pool_scale, v_conv_w, v_w_out, v_b_out, v_ln_g, v_ln_b):
    xi, yi, ci = lax.axis_index("x"), lax.axis_index("y"), lax.axis_index("c")
    chip = 2 * xi + yi
    c_idx = ci.reshape(1).astype(jnp.int32)

    def t(a):
        return jnp.swapaxes(a, 1, 2)

    conv_bits = lax.bitcast_convert_type(conv_w.reshape(DEPTH, 3 * 128), BF16).reshape(DEPTH, 3, 256)
    conv_bits = jnp.pad(conv_bits, ((0, 0), (0, 13), (0, 0)))
    own = (t(w_in).astype(BF16), w_out.astype(BF16), t(w_uq).astype(BF16), t(w_ukv).astype(BF16))
    zeros = (jnp.zeros((GAP, D_MODEL), BF16), None, jnp.zeros((64, Q_LORA), BF16), None)
    gather_in0 = _allgather_script((W_IN,), own[:1], zeros[:1], (0,))
    gather0 = _allgather_script(SHARDED[1:] + (W_CONV,), own[1:] + (conv_bits,), zeros[1:] + (None,),
                                (0, 0, 0, None))
    gather1 = _allgather_script(SHARDED, own, zeros, (1, 1, 1, 1))

    loss_part, grad_x, d_emb_g, d_emb_b, grads, reduced = _local_step(
        x[0], positions[0], loss_target[0], emb_ln_g, emb_ln_b, gather_in0, gather0, gather1, q_norm_g, kv_norm_g,
        w_pool, pool_scale, b_out, ln_g, ln_b, c_idx)

    def rows(a):
        return a.reshape(1, -1) if a.ndim == 1 else a

    small_wmv = [tuple(rows(a) for a in wmv) for wmv in (
        (emb_ln_g, m_emb_ln_g, v_emb_ln_g), (emb_ln_b, m_emb_ln_b, v_emb_ln_b),
        (q_norm_g, m_q_norm_g, v_q_norm_g), (kv_norm_g, m_kv_norm_g, v_kv_norm_g), (w_pool, m_w_pool, v_w_pool),
        (pool_scale, m_pool_scale, v_pool_scale), (b_out, m_b_out, v_b_out), (ln_g, m_ln_g, v_ln_g),
        (ln_b, m_ln_b, v_ln_b))]
    packed_g = _pack_small(
        [d_emb_g, d_emb_b] + [jnp.stack([grads[l][key] for l in range(DEPTH)]) for key in SMALL_LAYER_KEYS]
        + [jnp.pad(loss_part.reshape(1), (0, LANE - 1))], 0)
    (gathered,) = _run_comm(_allgather_small_script(packed_g), name="allgather_small")
    g_tot, small_upd = _small_sum_adamw(gathered, packed_g, small_wmv, name="small_sum_adamw")
    off = sum(w.size for w, _, _ in small_wmv)
    flat_tot = g_tot.reshape(-1)

    def halves(a):
        return [reduced[l][a] for l in range(DEPTH)]

    upd = {}
    upd["w_in"] = tuple(t(o) for o in _adamw_halves(t(w_in), t(m_w_in), t(v_w_in), halves(0), c_idx,
                                                    name="adamw_w_in"))
    conv_tot = flat_tot[off:off + DEPTH * 3 * 512].reshape(DEPTH, 3, 512)
    loss = flat_tot[off + DEPTH * 3 * 512]
    g_conv = lax.dynamic_slice_in_dim(conv_tot, chip * 128, 128, axis=2)

    def whole(a):
        return jnp.stack([jnp.where(ci == 0, jnp.concatenate([mine, oth], axis=1),
                                    jnp.concatenate([oth, mine], axis=1)) for mine, oth in halves(a)])

    upd["w_out"] = _adamw_halves(w_out, m_w_out, v_w_out, halves(1), c_idx, name="adamw_w_out")
    g_uq, g_ukv = t(whole(2)), t(whole(3))
    upd["w_uq"] = (g_uq,) + _adamw(w_uq, g_uq, m_w_uq, v_w_uq, name="adamw_w_uq")
    upd["w_ukv"] = (g_ukv,) + _adamw(w_ukv, g_ukv, m_w_ukv, v_w_ukv, name="adamw_w_ukv")
    upd["conv_w"] = (g_conv,) + _adamw(conv_w, g_conv, m_conv_w, v_conv_w, name="adamw_conv_w")
    for nm, res in zip(SMALL_ORDER, small_upd):
        upd[nm] = tuple(a.reshape(-1) for a in res) if nm in ("emb_ln_g", "emb_ln_b") else res

    order = ("emb_ln_g", "emb_ln_b", "w_in", "q_norm_g", "kv_norm_g", "w_uq", "w_ukv", "w_pool", "pool_scale",
             "conv_w", "w_out", "b_out", "ln_g", "ln_b")
    outs = [loss, grad_x[None]]
    for field in range(4):
        outs += [upd[nm][field] for nm in order]
    return tuple(outs)
```

```python
import collections

import jax
import jax.numpy as jnp
from jax import lax
from jax.experimental import pallas as pl
from jax.experimental.pallas import tpu as pltpu

F32 = jnp.float32
BF16 = jnp.bfloat16
MESH = pl.DeviceIdType.MESH

D_MODEL = 2048
DEPTH = 2
N_HEADS = 8
NOPE = 128
ROPE = 64
Q_LORA = 512
KV_LORA = 256
D_MLA = 1024
POOL_WINDOWS = (2, 4, 8, 16)
D_IN_PROJ = 4928
LN_EPS = 1e-5
RMS_EPS = 1e-6
ROPE_THETA = 10000.0
ALPHA = (2 * DEPTH) ** 0.25
SCALE = (NOPE + ROPE) ** -0.5
LOG2E = 1.4426950408889634
SCALE_LOG2E = SCALE * LOG2E
ADAM_LR = 0.001
ADAM_B1 = 0.9
ADAM_B2 = 0.999
ADAM_EPS = 1e-08
ADAM_WD = 0.01
ADAM_STEP = 10

NP = 5120
GAP_AT = 832
GAP = NP - D_IN_PROJ
W_MLA = 1024
W_MIX = NP - W_MLA
HALO = 16
LANE = 128
N_CHIPS = 4
N_DEV = 8
TQ = 512
FWD_GROUP = 4
BWD_GROUP = 3

NN = (((1,), (0,)), ((), ()))
NT = (((1,), (1,)), ((), ()))
TN = (((0,), (0,)), ((), ()))


CommScript = collections.namedtuple("CommScript", "args out_shape n_sems start finish")
HBM_SPEC = pl.BlockSpec(memory_space=pl.ANY)


def _pcall(kern, *, name, out_shape, grid=None, in_specs=None, out_specs=None, scratch=(), dims=None,
           vmem_mb=None, comm=None):
    cp = {}
    if dims is not None:
        cp["dimension_semantics"] = dims if comm is None else ("arbitrary",) * len(dims)
    if vmem_mb is not None:
        cp["vmem_limit_bytes"] = vmem_mb << 20
    if comm is None:
        args = dict(name=name, out_shape=out_shape, scratch_shapes=list(scratch),
                    compiler_params=pltpu.CompilerParams(**cp))
        if grid is not None:
            args["grid"] = grid
        if in_specs is not None:
            args["in_specs"] = in_specs
        if out_specs is not None:
            args["out_specs"] = out_specs
        return pl.pallas_call(kern, **args)

    single = not isinstance(out_shape, (tuple, list))
    own_out = (out_shape,) if single else tuple(out_shape)
    own_out_specs = (out_specs,) if single else tuple(out_specs)
    n_in, n_out, n_scr = len(in_specs), len(own_out), len(scratch)
    na, no = len(comm.args), len(comm.out_shape)

    def at(end):
        cond = None
        for d, n in enumerate(grid):
            here = pl.program_id(d) == (n - 1 if end else 0)
            cond = here if cond is None else jnp.logical_and(cond, here)
        return cond

    def wrapped(*refs):
        own_in, c_in = refs[:n_in], refs[n_in:n_in + na]
        o0 = n_in + na
        own_o, c_out = refs[o0:o0 + n_out], refs[o0 + n_out:o0 + n_out + no]
        s0 = o0 + n_out + no
        own_s, (send_sems, recv_sems) = refs[s0:s0 + n_scr], refs[s0 + n_scr:]

        @pl.when(at(False))
        def _():
            comm.start(c_in, c_out, send_sems, recv_sems)

        kern(*own_in, *own_o, *own_s)

        @pl.when(at(True))
        def _():
            comm.finish(c_in, c_out, send_sems, recv_sems)

    call = pl.pallas_call(
        wrapped, name=name, out_shape=own_out + tuple(comm.out_shape), grid=grid,
        in_specs=list(in_specs) + [HBM_SPEC] * na, out_specs=own_out_specs + (HBM_SPEC,) * no,
        scratch_shapes=list(scratch) + [pltpu.SemaphoreType.DMA((comm.n_sems,)),
                                        pltpu.SemaphoreType.DMA((comm.n_sems,))],
        compiler_params=pltpu.CompilerParams(**cp))

    def run(*args):
        res = call(*args, *comm.args)
        own = res[0] if single else tuple(res[:n_out])
        return own, tuple(res[n_out:])

    return run


def _run_comm(script, *, name):
    na, no = len(script.args), len(script.out_shape)

    def body(*refs):
        ins, outs = refs[:na], refs[na:na + no]
        send_sems, recv_sems = refs[na + no:]
        script.start(ins, outs, send_sems, recv_sems)
        script.finish(ins, outs, send_sems, recv_sems)

    return pl.pallas_call(
        body, name=name, out_shape=tuple(script.out_shape), in_specs=[HBM_SPEC] * na, out_specs=(HBM_SPEC,) * no,
        scratch_shapes=[pltpu.SemaphoreType.DMA((script.n_sems,)), pltpu.SemaphoreType.DMA((script.n_sems,))])(
            *script.args)


def _sigmoid(g):
    return 1.0 / (1.0 + jnp.exp(-g))


def _silu_and_grad(g):
    sig = _sigmoid(g)
    return g * sig, sig * (1.0 + g * (1.0 - sig))


def _matmul(a, b, mode, *, name, tm, tn, tk, out_dtype=F32, vmem_mb=48, comm=None):
    if mode == "nn":
        (M, K), N = a.shape, b.shape[1]
    elif mode == "nt":
        (M, K), N = a.shape, b.shape[0]
    else:
        (K, M), N = a.shape, b.shape[1]
    tm, tn, tk = min(tm, M), min(tn, N), min(tk, K)
    assert M % tm == 0 and N % tn == 0 and K % tk == 0, (name, M, N, K)
    nk = K // tk
    dn = {"nn": NN, "nt": NT, "tn": TN}[mode]
    if mode == "tn":
        a_spec = pl.BlockSpec((tk, tm), lambda i, j, k: (k, i))
    else:
        a_spec = pl.BlockSpec((tm, tk), lambda i, j, k: (i, k))
    if mode == "nt":
        b_spec = pl.BlockSpec((tn, tk), lambda i, j, k: (j, k))
    else:
        b_spec = pl.BlockSpec((tk, tn), lambda i, j, k: (k, j))
    o_spec = pl.BlockSpec((tm, tn), lambda i, j, k: (i, j))

    def kern(a_ref, b_ref, o_ref, *rest):
        part = lax.dot_general(a_ref[...].astype(BF16), b_ref[...].astype(BF16), dn,
                               preferred_element_type=F32)
        if nk == 1:
            o_ref[...] = part.astype(out_dtype)
        else:
            acc_ref = rest[0]
            k = pl.program_id(2)

            @pl.when(k == 0)
            def _():
                acc_ref[...] = part

            @pl.when(k > 0)
            def _():
                acc_ref[...] += part

            @pl.when(k == nk - 1)
            def _():
                o_ref[...] = acc_ref[...].astype(out_dtype)

    scratch = [pltpu.VMEM((tm, tn), F32)] if nk > 1 else []
    return _pcall(kern, name=name, out_shape=jax.ShapeDtypeStruct((M, N), out_dtype),
                  grid=(M // tm, N // tn, nk), in_specs=[a_spec, b_spec], out_specs=o_spec, scratch=scratch,
                  dims=("parallel", "parallel", "arbitrary"), vmem_mb=vmem_mb, comm=comm)(a, b)


def _dproj_times_w(d_mla, d_mix, wt, add, add_scale, *, name, comm=None):
    S = d_mla.shape[0]
    Dm = wt.shape[1]
    tm, tn, tk = min(1024, S), 1024, 2048
    nk = 1 + W_MIX // tk

    def kern(a1_ref, a2_ref, b1_ref, b2_ref, add_ref, o_ref, acc_ref):
        k = pl.program_id(2)

        @pl.when(k == 0)
        def _():
            acc_ref[...] = jnp.dot(a1_ref[...], b1_ref[...], preferred_element_type=F32)

        @pl.when(k > 0)
        def _():
            acc_ref[...] += jnp.dot(a2_ref[...], b2_ref[...], preferred_element_type=F32)

        @pl.when(k == nk - 1)
        def _():
            o_ref[...] = add_scale * add_ref[...] + acc_ref[...]

    o_spec = pl.BlockSpec((tm, tn), lambda i, j, k: (i, j))
    b2_spec = pl.BlockSpec((pl.Element(tk), pl.Element(tn)),
                           lambda i, j, k: (pl.multiple_of(W_MLA + tk * jnp.maximum(k - 1, 0), W_MLA),
                                            pl.multiple_of(j * tn, tn)))
    return _pcall(kern, name=name, out_shape=jax.ShapeDtypeStruct((S, Dm), F32), grid=(S // tm, Dm // tn, nk),
                  in_specs=[pl.BlockSpec((tm, W_MLA), lambda i, j, k: (i, 0)),
                            pl.BlockSpec((tm, tk), lambda i, j, k: (i, jnp.maximum(k - 1, 0))),
                            pl.BlockSpec((W_MLA, tn), lambda i, j, k: (0, j)), b2_spec, o_spec],
                  out_specs=o_spec, scratch=[pltpu.VMEM((tm, tn), F32)],
                  dims=("parallel", "parallel", "arbitrary"), vmem_mb=56, comm=comm)(d_mla, d_mix, wt, wt, add)


def _dproj_t_times_h(d_mla, d_mix, h, *, name, comm=None):
    S, Dm = h.shape
    tm, tn, tk = W_MLA, 1024, min(2048, S)
    nk = S // tk

    def kern(a1_ref, a2_ref, b_ref, o_ref, acc_ref):
        i = pl.program_id(0)
        k = pl.program_id(2)
        b = b_ref[...].astype(BF16)

        def accumulate(part):
            @pl.when(k == 0)
            def _():
                acc_ref[...] = part

            @pl.when(k > 0)
            def _():
                acc_ref[...] += part

        @pl.when(i == 0)
        def _():
            accumulate(lax.dot_general(a1_ref[...], b, TN, preferred_element_type=F32))

        @pl.when(i > 0)
        def _():
            accumulate(lax.dot_general(a2_ref[...], b, TN, preferred_element_type=F32))

        @pl.when(k == nk - 1)
        def _():
            o_ref[...] = acc_ref[...]

    return _pcall(kern, name=name, out_shape=jax.ShapeDtypeStruct((NP, Dm), F32), grid=(NP // tm, Dm // tn, nk),
                  in_specs=[pl.BlockSpec((tk, tm), lambda i, j, k: (jnp.where(i == 0, k, nk - 1), 0)),
                            pl.BlockSpec((tk, tm), lambda i, j, k: (jnp.where(i == 0, 0, k), jnp.maximum(i - 1, 0))),
                            pl.BlockSpec((tk, tn), lambda i, j, k: (k, j))],
                  out_specs=pl.BlockSpec((tm, tn), lambda i, j, k: (i, j)), scratch=[pltpu.VMEM((tm, tn), F32)],
                  dims=("parallel", "parallel", "arbitrary"), vmem_mb=48, comm=comm)(d_mla, d_mix, h)


def _ln_fwd(x, g, b, *, name, comm=None):
    S, Dm = x.shape
    tm = min(512, S)

    def kern(x_ref, g_ref, b_ref, y_ref, yb_ref):
        xf = x_ref[...]
        mu = jnp.mean(xf, axis=-1, keepdims=True)
        xc = xf - mu
        var = jnp.mean(xc * xc, axis=-1, keepdims=True)
        y = xc * lax.rsqrt(var + LN_EPS) * g_ref[...] + b_ref[...]
        y_ref[...] = y
        yb_ref[...] = y.astype(BF16)

    row = pl.BlockSpec((tm, Dm), lambda i: (i, 0))
    vec = pl.BlockSpec((1, Dm), lambda i: (0, 0))
    return _pcall(kern, name=name,
                  out_shape=(jax.ShapeDtypeStruct((S, Dm), F32), jax.ShapeDtypeStruct((S, Dm), BF16)),
                  grid=(S // tm,), in_specs=[row, vec, vec], out_specs=(row, row), dims=("parallel",), vmem_mb=48,
                  comm=comm)(
                      x, g.reshape(1, Dm), b.reshape(1, Dm))


def _ln_bwd(dy, r, g, *, name, bf16_copy=True):
    S, Dm = r.shape
    tm = min(512, S)

    def kern(dy_ref, r_ref, g_ref, dr_ref, *rest):
        drb_ref = rest[0] if bf16_copy else None
        dg_ref, db_ref, ds_ref = rest[-3:]

        @pl.when(pl.program_id(0) == 0)
        def _():
            dg_ref[...] = jnp.zeros_like(dg_ref)
            db_ref[...] = jnp.zeros_like(db_ref)
            ds_ref[...] = jnp.zeros_like(ds_ref)

        rf = r_ref[...]
        dyf = dy_ref[...]
        mu = jnp.mean(rf, axis=-1, keepdims=True)
        xc = rf - mu
        var = jnp.mean(xc * xc, axis=-1, keepdims=True)
        rstd = lax.rsqrt(var + LN_EPS)
        xhat = xc * rstd
        dxh = dyf * g_ref[...]
        c1 = jnp.mean(dxh, axis=-1, keepdims=True)
        c2 = jnp.mean(dxh * xhat, axis=-1, keepdims=True)
        dr = rstd * (dxh - c1 - xhat * c2)
        dr_ref[...] = dr
        if bf16_copy:
            drb_ref[...] = dr.astype(BF16)
        dg_ref[...] += jnp.sum(dyf * xhat, axis=0, keepdims=True)
        db_ref[...] += jnp.sum(dyf, axis=0, keepdims=True)
        ds_ref[...] += jnp.sum(dr, axis=0, keepdims=True)

    row = pl.BlockSpec((tm, Dm), lambda i: (i, 0))
    vec = pl.BlockSpec((1, Dm), lambda i: (0, 0))
    vshape = jax.ShapeDtypeStruct((1, Dm), F32)
    copies = ((jax.ShapeDtypeStruct((S, Dm), BF16),), (row,)) if bf16_copy else ((), ())
    res = _pcall(kern, name=name,
                 out_shape=(jax.ShapeDtypeStruct((S, Dm), F32),) + copies[0] + (vshape, vshape, vshape),
                 grid=(S // tm,), in_specs=[row, row, vec], out_specs=(row,) + copies[1] + (vec, vec, vec),
                 dims=("arbitrary",), vmem_mb=48)(dy, r, g.reshape(1, Dm))
    return res if bf16_copy else (res[0], None) + tuple(res[1:])


def _loss_ln_bwd(target, r, g, b, *, name):
    S, Dm = r.shape
    tm = min(512, S)

    def kern(t_ref, r_ref, g_ref, b_ref, l_ref, dr_ref, drb_ref, dg_ref, db_ref, ds_ref):
        @pl.when(pl.program_id(0) == 0)
        def _():
            l_ref[...] = jnp.zeros_like(l_ref)
            dg_ref[...] = jnp.zeros_like(dg_ref)
            db_ref[...] = jnp.zeros_like(db_ref)
            ds_ref[...] = jnp.zeros_like(ds_ref)

        rf = r_ref[...]
        mu = jnp.mean(rf, axis=-1, keepdims=True)
        xc = rf - mu
        var = jnp.mean(xc * xc, axis=-1, keepdims=True)
        rstd = lax.rsqrt(var + LN_EPS)
        xhat = xc * rstd
        e = (xhat * g_ref[...] + b_ref[...]) - t_ref[...]
        dyf = e / float(Dm)
        per_row = jnp.mean(e * e, axis=-1, keepdims=True)
        l_ref[...] += 0.5 * jnp.sum(per_row, axis=0, keepdims=True)
        dxh = dyf * g_ref[...]
        c1 = jnp.mean(dxh, axis=-1, keepdims=True)
        c2 = jnp.mean(dxh * xhat, axis=-1, keepdims=True)
        dr = rstd * (dxh - c1 - xhat * c2)
        dr_ref[...] = dr
        drb_ref[...] = dr.astype(BF16)
        dg_ref[...] += jnp.sum(dyf * xhat, axis=0, keepdims=True)
        db_ref[...] += jnp.sum(dyf, axis=0, keepdims=True)
        ds_ref[...] += jnp.sum(dr, axis=0, keepdims=True)

    row = pl.BlockSpec((tm, Dm), lambda i: (i, 0))
    vec = pl.BlockSpec((1, Dm), lambda i: (0, 0))
    acc = pl.BlockSpec((8, LANE), lambda i: (0, 0))
    vshape = jax.ShapeDtypeStruct((1, Dm), F32)
    return _pcall(kern, name=name,
                  out_shape=(jax.ShapeDtypeStruct((8, LANE), F32), jax.ShapeDtypeStruct((S, Dm), F32),
                             jax.ShapeDtypeStruct((S, Dm), BF16), vshape, vshape, vshape),
                  grid=(S // tm,), in_specs=[row, row, vec, vec], out_specs=(acc, row, row, vec, vec, vec),
                  dims=("arbitrary",), vmem_mb=56)(target, r, g.reshape(1, Dm), b.reshape(1, Dm))


def _rot_sum(t):
    return pltpu.roll(t, 32, 1) + pltpu.roll(t, 96, 1)


def _mla_qkv(proj, cos_t, sin_t, qg, kvg, wuq_t, wukv_t, *, name):
    S = proj.shape[0]
    tm = min(256, S)

    def kern(ql_ref, kvl_ref, kr_ref, cos_ref, sin_ref, qg_ref, kvg_ref, wuq_ref, wukv_ref,
             qc_ref, kc_ref, v_ref, vt_ref, qn_ref, kvn_ref):
        cosv = cos_ref[...]
        sinv = sin_ref[...]

        def rope(t):
            return t * cosv + _rot_sum(t) * sinv

        ql = ql_ref[...]
        qn = (ql * lax.rsqrt(jnp.mean(ql * ql, axis=-1, keepdims=True) + RMS_EPS) * qg_ref[...]).astype(BF16)
        kvl = kvl_ref[...]
        kvn = (kvl * lax.rsqrt(jnp.mean(kvl * kvl, axis=-1, keepdims=True) + RMS_EPS) * kvg_ref[...]).astype(BF16)
        qn_ref[...] = qn
        kvn_ref[...] = kvn
        q = lax.dot_general(qn, wuq_ref[...], NT, preferred_element_type=F32)
        kv = lax.dot_general(kvn, wukv_ref[...], NT, preferred_element_type=F32)
        kr = rope(kr_ref[...]).astype(BF16)
        for h in range(N_HEADS):
            c0 = 256 * h
            qc_ref[:, c0:c0 + 128] = q[:, c0:c0 + 128].astype(BF16)
            qc_ref[:, c0 + 128:c0 + 256] = rope(q[:, c0 + 128:c0 + 256]).astype(BF16)
            kc_ref[:, c0:c0 + 128] = kv[:, c0:c0 + 128].astype(BF16)
            kc_ref[:, c0 + 128:c0 + 256] = kr
            vh = kv[:, c0 + 128:c0 + 256]
            v_ref[:, 128 * h:128 * h + 128] = vh.astype(BF16)
            vt_ref[h] = jnp.transpose(vh).astype(BF16)

    def row(w, blk):
        return pl.BlockSpec((tm, w), lambda i: (i, blk))

    def full(shape):
        return pl.BlockSpec(shape, lambda i: (0,) * len(shape))

    t = min(TQ, S)
    per = t // tm
    vt_spec = pl.BlockSpec((N_HEADS, None, 128, tm), lambda i: (0, i // per, 0, i % per))
    outs = (jax.ShapeDtypeStruct((S, 2048), BF16), jax.ShapeDtypeStruct((S, 2048), BF16),
            jax.ShapeDtypeStruct((S, 1024), BF16), jax.ShapeDtypeStruct((N_HEADS, S // t, 128, t), BF16),
            jax.ShapeDtypeStruct((S, Q_LORA), BF16), jax.ShapeDtypeStruct((S, KV_LORA), BF16))
    return _pcall(kern, name=name, out_shape=outs, grid=(S // tm,),
                  in_specs=[row(512, 0), row(256, 2), row(128, 6), row(128, 0), row(128, 0),
                            full((1, Q_LORA)), full((1, KV_LORA)), full((2048, Q_LORA)), full((2048, KV_LORA))],
                  out_specs=(row(2048, 0), row(2048, 0), row(1024, 0), vt_spec, row(512, 0), row(256, 0)),
                  dims=("parallel",), vmem_mb=48)(
                      proj, proj, proj, cos_t, sin_t, qg.reshape(1, -1), kvg.reshape(1, -1), wuq_t, wukv_t)


def _mla_qkv_bwd(dqb, dkvb, dkr_heads, qn, kvn, proj, cos_t, sin_t, qg, kvg, wuq_t, wukv_t, *, name):
    S = proj.shape[0]
    tm = min(512, S)

    def kern(dqb_ref, dkvb_ref, dkrh_ref, qn_ref, kvn_ref, ql_ref, kvl_ref, cos_ref, sin_ref, qg_ref, kvg_ref,
             wuq_ref, wukv_ref, dml_ref, dqg_ref, dkvg_ref, dwuq_ref, dwukv_ref):
        @pl.when(pl.program_id(0) == 0)
        def _():
            dqg_ref[...] = jnp.zeros_like(dqg_ref)
            dkvg_ref[...] = jnp.zeros_like(dkvg_ref)
            dwuq_ref[...] = jnp.zeros_like(dwuq_ref)
            dwukv_ref[...] = jnp.zeros_like(dwukv_ref)

        dwuq_ref[...] += lax.dot_general(dqb_ref[...], qn_ref[...], TN, preferred_element_type=F32)
        dwukv_ref[...] += lax.dot_general(dkvb_ref[...], kvn_ref[...], TN, preferred_element_type=F32)

        cosv = cos_ref[...]
        sinv = sin_ref[...]

        def unrope(t):
            return t * cosv - _rot_sum(t) * sinv

        dkr = dkrh_ref[:, 0:128]
        for h in range(1, N_HEADS):
            dkr = dkr + dkrh_ref[:, 128 * h:128 * h + 128]

        def rms_bwd(x, g, dy):
            n = x.shape[-1]
            rs = lax.rsqrt(jnp.mean(x * x, axis=-1, keepdims=True) + RMS_EPS)
            dyg = dy * g
            dx = rs * dyg - x * (rs * rs * rs) * (jnp.sum(dyg * x, axis=-1, keepdims=True) / n)
            return dx, jnp.sum(dy * (x * rs), axis=0, keepdims=True)

        dqn = jnp.dot(dqb_ref[...], wuq_ref[...], preferred_element_type=F32)
        dql, dqg = rms_bwd(ql_ref[...], qg_ref[...], dqn)
        dqg_ref[...] += dqg
        dkvn = jnp.dot(dkvb_ref[...], wukv_ref[...], preferred_element_type=F32)
        dkvl, dkvg = rms_bwd(kvl_ref[...], kvg_ref[...], dkvn)
        dkvg_ref[...] += dkvg
        dml_ref[:, 0:512] = dql.astype(BF16)
        dml_ref[:, 512:768] = dkvl.astype(BF16)
        dml_ref[:, 768:896] = unrope(dkr).astype(BF16)
        dml_ref[:, 896:1024] = jnp.zeros((tm, 128), BF16)

    def row(w, blk):
        return pl.BlockSpec((tm, w), lambda i: (i, blk))

    def full(shape):
        return pl.BlockSpec(shape, lambda i: (0,) * len(shape))

    outs = (jax.ShapeDtypeStruct((S, W_MLA), BF16), jax.ShapeDtypeStruct((1, Q_LORA), F32),
            jax.ShapeDtypeStruct((1, KV_LORA), F32), jax.ShapeDtypeStruct((2048, Q_LORA), F32),
            jax.ShapeDtypeStruct((2048, KV_LORA), F32))
    return _pcall(kern, name=name, out_shape=outs, grid=(S // tm,),
                  in_specs=[row(2048, 0), row(2048, 0), row(1024, 0), row(512, 0), row(256, 0), row(512, 0),
                            row(256, 2), row(128, 0), row(128, 0), full((1, Q_LORA)), full((1, KV_LORA)),
                            full((2048, Q_LORA)), full((2048, KV_LORA))],
                  out_specs=(row(W_MLA, 0), full((1, Q_LORA)), full((1, KV_LORA)), full((2048, Q_LORA)),
                             full((2048, KV_LORA))),
                  dims=("arbitrary",), vmem_mb=56)(
                      dqb, dkvb, dkr_heads, qn, kvn, proj, proj, cos_t, sin_t, qg.reshape(1, -1), kvg.reshape(1, -1),
                      wuq_t, wukv_t)


def _flash_fwd(qc, kc, vt, *, name, comm=None):
    S = qc.shape[0]
    t = min(TQ, S)
    n = S // t

    def kern(q_ref, k_ref, vt_ref, o_ref, lse_ref, m_s, l_s, acc_s):
        qi = pl.program_id(1)
        m_s[...] = jnp.full_like(m_s, -jnp.inf)
        l_s[...] = jnp.zeros_like(l_s)
        acc_s[...] = jnp.zeros_like(acc_s)

        half = t // 2

        def scores(kb, q_lo=0, q_n=t, k_n=t):
            k0 = pl.multiple_of(kb * t, t)
            return lax.dot_general(k_ref[pl.ds(k0, k_n), :], q_ref[q_lo:q_lo + q_n, :], NT,
                                   preferred_element_type=F32)

        def update(kb, st, q_lo=0, diagonal=False):
            k_n, q_n = st.shape
            if diagonal:
                krow = lax.broadcasted_iota(jnp.int32, (k_n, q_n), 0)
                qcol = lax.broadcasted_iota(jnp.int32, (k_n, q_n), 1) + q_lo
                st = jnp.where(krow <= qcol, st, -jnp.inf)
            lanes = slice(q_lo, q_lo + q_n)
            m_prev = m_s[:, lanes]
            m_new = jnp.maximum(m_prev, jnp.max(st, axis=0, keepdims=True))
            a = jnp.exp2((m_prev - m_new) * SCALE_LOG2E)
            pt = jnp.exp2((st - m_new) * SCALE_LOG2E)
            l_s[:, lanes] = a * l_s[:, lanes] + jnp.sum(pt, axis=0, keepdims=True)
            acc_s[:, lanes] = a * acc_s[:, lanes] + jnp.dot(vt_ref[kb, :, 0:k_n], pt.astype(BF16),
                                                            preferred_element_type=F32)
            m_s[:, lanes] = m_new

        def group(kb, count, last_diagonal):
            whole = count - 1 if last_diagonal else count
            sts = [scores(kb + g) for g in range(whole)]
            if last_diagonal:
                kd = kb + count - 1
                s_lo, s_hi = scores(kd, 0, half, half), scores(kd, half, half, t)
            for g in range(whole):
                update(kb + g, sts[g])
            if last_diagonal:
                update(kd, s_lo, 0, True)
                update(kd, s_hi, half, True)

        def body(i, carry):
            group(FWD_GROUP * i, FWD_GROUP, False)
            return carry

        full = qi // FWD_GROUP
        lax.fori_loop(0, full, body, 0)
        for rem in range(FWD_GROUP):
            @pl.when(qi - FWD_GROUP * full == rem)
            def _():
                group(qi - rem, rem + 1, True)
        o_ref[...] = jnp.transpose(acc_s[...] / l_s[...])
        lse_ref[pl.ds(qi, 1), :] = m_s[...] * SCALE_LOG2E + jnp.log2(l_s[...])

    q_spec = pl.BlockSpec((t, 256), lambda h, qi: (qi, h))
    k_spec = pl.BlockSpec((S, 256), lambda h, qi: (0, h))
    vt_spec = pl.BlockSpec((None, n, 128, t), lambda h, qi: (h, 0, 0, 0))
    o_spec = pl.BlockSpec((t, 128), lambda h, qi: (qi, h))
    lse_spec = pl.BlockSpec((None, n, t), lambda h, qi: (h, 0, 0))
    return _pcall(kern, name=name,
                  out_shape=(jax.ShapeDtypeStruct((S, D_MLA), F32), jax.ShapeDtypeStruct((N_HEADS, n, t), F32)),
                  grid=(N_HEADS, n), in_specs=[q_spec, k_spec, vt_spec], out_specs=(o_spec, lse_spec),
                  scratch=[pltpu.VMEM((1, t), F32), pltpu.VMEM((1, t), F32), pltpu.VMEM((128, t), F32)],
                  dims=("parallel", "arbitrary"), vmem_mb=48, comm=comm)(qc, kc, vt)


def _flash_bwd(qc, kc, v, do, lse2, delta, cos_t, sin_t, *, name, comm=None):
    S = qc.shape[0]
    t = min(TQ, S)
    n = S // t

    def kern(q_ref, k_ref, v_ref, do_ref, lse_ref, dl_ref, cos_ref, sin_ref, dqb_ref, dkvb_ref, dkr_ref,
             dq_ref, dk_ref, dv_ref):
        ki = pl.program_id(1)

        @pl.when(ki == 0)
        def _():
            dq_ref[...] = jnp.zeros_like(dq_ref)

        dk_ref[...] = jnp.zeros_like(dk_ref)
        dv_ref[...] = jnp.zeros_like(dv_ref)

        half = t // 2

        def step(qb, q_lo=0, q_n=t, k_n=t, diagonal=False):
            q0 = pl.multiple_of(qb * t + q_lo, half)
            lanes = slice(q_lo, q_lo + q_n)
            kt = k_ref[0:k_n, :]
            qblk = q_ref[pl.ds(q0, q_n), :]
            dob = do_ref[pl.ds(q0, q_n), :].astype(BF16)
            st = lax.dot_general(kt, qblk, NT, preferred_element_type=F32)
            pt = jnp.exp2(st * SCALE_LOG2E - lse_ref[pl.ds(qb, 1), lanes])
            if diagonal:
                krow = lax.broadcasted_iota(jnp.int32, (k_n, q_n), 0)
                qcol = lax.broadcasted_iota(jnp.int32, (k_n, q_n), 1) + q_lo
                pt = jnp.where(krow <= qcol, pt, 0.0)
            dv_ref[0:k_n, :] += jnp.dot(pt.astype(BF16), dob, preferred_element_type=F32)
            dpt = lax.dot_general(v_ref[0:k_n, :], dob, NT, preferred_element_type=F32)
            dst = (pt * (dpt - dl_ref[pl.ds(qb, 1), lanes]) * SCALE).astype(BF16)
            dk_ref[0:k_n, :] += jnp.dot(dst, qblk, preferred_element_type=F32)
            dq_ref[pl.ds(q0, q_n), :] += lax.dot_general(dst, kt, TN, preferred_element_type=F32)

        step(ki, 0, half, half, True)
        step(ki, half, half, t, True)
        rest = n - 1 - ki
        full = rest // BWD_GROUP

        def body(i, carry):
            for g in range(BWD_GROUP):
                step(ki + 1 + BWD_GROUP * i + g)
            return carry

        lax.fori_loop(0, full, body, 0)
        for rem in range(1, BWD_GROUP):
            @pl.when(rest - BWD_GROUP * full == rem)
            def _():
                for g in range(rem):
                    step(n - rem + g)

        dkvb_ref[:, 0:128] = dk_ref[:, 0:128].astype(BF16)
        dkvb_ref[:, 128:256] = dv_ref[...].astype(BF16)
        dkr_ref[...] = dk_ref[:, 128:256]

        @pl.when(ki == n - 1)
        def _():
            dqb_ref[:, 0:128] = dq_ref[:, 0:128].astype(BF16)
            dqr = dq_ref[:, 128:256]
            dqb_ref[:, 128:256] = (dqr * cos_ref[...] - _rot_sum(dqr) * sin_ref[...]).astype(BF16)

    def whole(w):
        return pl.BlockSpec((S, w), lambda h, ki: (0, h))

    def krow(w):
        return pl.BlockSpec((t, w), lambda h, ki: (ki, h))

    stat = pl.BlockSpec((None, n, t), lambda h, ki: (h, 0, 0))
    table = pl.BlockSpec((S, 128), lambda h, ki: (0, 0))
    return _pcall(kern, name=name,
                  out_shape=(jax.ShapeDtypeStruct((S, 2048), BF16), jax.ShapeDtypeStruct((S, 2048), BF16),
                             jax.ShapeDtypeStruct((S, D_MLA), F32)),
                  grid=(N_HEADS, n),
                  in_specs=[whole(256), krow(256), krow(128), whole(128), stat, stat, table, table],
                  out_specs=(whole(256), krow(256), krow(128)),
                  scratch=[pltpu.VMEM((S, 256), F32), pltpu.VMEM((t, 256), F32), pltpu.VMEM((t, 128), F32)],
                  dims=("parallel", "arbitrary"), vmem_mb=56, comm=comm)(qc, kc, v, do, lse2, delta, cos_t, sin_t)


def _mixer_specs(S, tm):
    hb = tm // HALO
    last_hb = S // HALO - 1

    def main(w, blk):
        return pl.BlockSpec((tm, w), lambda i: (i, blk))

    def prev(w, blk):
        return pl.BlockSpec((HALO, w), lambda i: (jnp.maximum(i * hb - 1, 0), blk))

    def nxt(w, blk):
        return pl.BlockSpec((HALO, w), lambda i: (jnp.minimum((i + 1) * hb, last_hb), blk))

    def full(shape):
        return pl.BlockSpec(shape, lambda i: (0,) * len(shape))

    return main, prev, nxt, full


def _fill_halo(i, xp, xu, hp_ref, hch_ref, hcc_ref, pin_ref, ch_ref, cc_ref, tm):
    first = i == 0
    xp[0:HALO, :] = jnp.where(first, 0.0, hp_ref[...])
    xp[HALO:HALO + tm, :] = pin_ref[...]
    xu[0:HALO, :] = jnp.where(first, 0.0, hch_ref[...] * hcc_ref[...])
    xu[HALO:HALO + tm, :] = cc_ref[...] * ch_ref[...]


def _pooled(xp, g, t1, tm):
    w = POOL_WINDOWS[g]
    lanes = slice(128 * g, 128 * g + 128)
    x0 = xp[HALO:HALO + tm, lanes]
    acc = x0
    for k in range(1, w):
        acc = acc + xp[HALO - k:HALO - k + tm, lanes]
    return acc / jnp.minimum(t1, float(w)) - x0


def _conv_fwd(xu, cw_ref, tm):
    return (cw_ref[0:1, :] * xu[HALO - 2:HALO - 2 + tm, :] + cw_ref[1:2, :] * xu[HALO - 1:HALO - 1 + tm, :]
            + cw_ref[2:3, :] * xu[HALO:HALO + tm, :])


def _mixer_fwd(proj, o, wpool, ps, convw, *, name):
    S = proj.shape[0]
    tm = min(256, S)
    main, prev, _, full = _mixer_specs(S, tm)

    def kern(gm_ref, pin_ref, gp_ref, ch_ref, cb_ref, cc_ref, gc_ref, hp_ref, hch_ref, hcc_ref,
             o_ref, wp_ref, ps_ref, cw_ref, mix_ref, xp, xu):
        i = pl.program_id(0)
        _fill_halo(i, xp, xu, hp_ref, hch_ref, hcc_ref, pin_ref, ch_ref, cc_ref, tm)
        t1 = (i * tm + lax.broadcasted_iota(jnp.int32, (tm, 1), 0) + 1).astype(F32)
        for g in range(4):
            lanes = slice(128 * g, 128 * g + 128)
            pooled = _pooled(xp, g, t1, tm)
            z = jnp.dot(pooled.astype(BF16), wp_ref[g].astype(BF16), preferred_element_type=F32)
            gp = gp_ref[:, lanes]
            y = z * ps_ref[:, lanes] * (gp * _sigmoid(gp))
            mix_ref[:, 1024 + 128 * g:1024 + 128 * g + 128] = y.astype(BF16)
        gc = gc_ref[...]
        mix_ref[:, 1536:2048] = (cb_ref[...] * _conv_fwd(xu, cw_ref, tm) * (gc * _sigmoid(gc))).astype(BF16)
        gm = gm_ref[...]
        mix_ref[:, 0:1024] = (o_ref[...] * (gm * _sigmoid(gm))).astype(BF16)

    return _pcall(kern, name=name, out_shape=jax.ShapeDtypeStruct((S, 2048), BF16), grid=(S // tm,),
                  in_specs=[main(1024, 1), main(512, 4), main(512, 5), main(512, 6), main(512, 7), main(512, 8),
                            main(512, 9), prev(512, 4), prev(512, 6), prev(512, 8),
                            main(1024, 0), full((4, 128, 128)), full((1, 512)), full((3, 512))],
                  out_specs=main(2048, 0),
                  scratch=[pltpu.VMEM((tm + HALO, 512), F32), pltpu.VMEM((tm + HALO, 512), F32)],
                  dims=("parallel",), vmem_mb=48)(
                      proj, proj, proj, proj, proj, proj, proj, proj, proj, proj, o, wpool, ps.reshape(1, 512), convw)


def _mixer_bwd(dmix, proj, o, wpool, ps, convw, *, name):
    S = proj.shape[0]
    tm = min(256, S)
    n = S // tm
    t = min(TQ, S)
    per = t // tm
    main, prev, nxt, full = _mixer_specs(S, tm)

    def kern(dm_ref, dmn_ref, gm_ref, pin_ref, gp_ref, ch_ref, cb_ref, cc_ref, gc_ref,
             hp_ref, hch_ref, hcc_ref, gpn_ref, cbn_ref, gcn_ref, o_ref, wp_ref, ps_ref, cw_ref,
             d_ref, do_ref, dl_ref, dwp_ref, dps_ref, dcw_ref, xp, xu, ee, ed):
        i = pl.program_id(0)
        last = i == n - 1

        @pl.when(i == 0)
        def _():
            dwp_ref[...] = jnp.zeros_like(dwp_ref)
            dps_ref[...] = jnp.zeros_like(dps_ref)
            dcw_ref[...] = jnp.zeros_like(dcw_ref)

        _fill_halo(i, xp, xu, hp_ref, hch_ref, hcc_ref, pin_ref, ch_ref, cc_ref, tm)
        t1 = (i * tm + lax.broadcasted_iota(jnp.int32, (tm, 1), 0) + 1).astype(F32)
        t1n = ((i + 1) * tm + lax.broadcasted_iota(jnp.int32, (HALO, 1), 0) + 1).astype(F32)
        c_pin, c_gp, c_ch, c_cb, c_cc, c_gc = 1024, 1536, 2048, 2560, 3072, 3584

        for g in range(4):
            w = float(POOL_WINDOWS[g])
            lanes = slice(128 * g, 128 * g + 128)
            pooled = _pooled(xp, g, t1, tm)
            pb = pooled.astype(BF16)
            wp = wp_ref[g].astype(BF16)
            z = jnp.dot(pb, wp, preferred_element_type=F32)
            psl = ps_ref[:, lanes]
            sg, dsg = _silu_and_grad(gp_ref[:, lanes])
            dmp = dm_ref[:, 1024 + 128 * g:1024 + 128 * g + 128]
            dyp = dmp * sg
            d_ref[:, c_gp + 128 * g:c_gp + 128 * g + 128] = (dmp * (z * psl) * dsg).astype(BF16)
            dps_ref[:, lanes] += jnp.sum(dyp * z, axis=0, keepdims=True)
            dz = (dyp * psl).astype(BF16)
            dwp_ref[g] += lax.dot_general(pb, dz, TN, preferred_element_type=F32)
            dpl = lax.dot_general(dz, wp, NT, preferred_element_type=F32)
            ee[0:tm, lanes] = dpl / jnp.minimum(t1, w)
            gpn = gpn_ref[:, lanes]
            dzn = (dmn_ref[:, lanes] * (gpn * _sigmoid(gpn)) * psl).astype(BF16)
            dpn = lax.dot_general(dzn, wp, NT, preferred_element_type=F32)
            ee[tm:tm + HALO, lanes] = jnp.where(last, 0.0, dpn / jnp.minimum(t1n, w))
            acc = ee[0:tm, lanes]
            for k in range(1, POOL_WINDOWS[g]):
                acc = acc + ee[k:k + tm, lanes]
            d_ref[:, c_pin + 128 * g:c_pin + 128 * g + 128] = (acc - dpl).astype(BF16)

        yc = _conv_fwd(xu, cw_ref, tm)
        sgc, dsgc = _silu_and_grad(gc_ref[...])
        cb = cb_ref[...]
        dmc = dm_ref[:, 1536:2048]
        d_ref[:, c_gc:c_gc + 512] = (dmc * cb * yc * dsgc).astype(BF16)
        d_ref[:, c_cb:c_cb + 512] = (dmc * yc * sgc).astype(BF16)
        dyc = dmc * cb * sgc
        ed[0:tm, :] = dyc
        gcn = gcn_ref[...]
        ed[tm:tm + HALO, :] = jnp.where(last, 0.0, dmn_ref[:, 512:1024] * cbn_ref[...] * (gcn * _sigmoid(gcn)))
        dcw_ref[0:1, :] += jnp.sum(dyc * xu[HALO - 2:HALO - 2 + tm, :], axis=0, keepdims=True)
        dcw_ref[1:2, :] += jnp.sum(dyc * xu[HALO - 1:HALO - 1 + tm, :], axis=0, keepdims=True)
        dcw_ref[2:3, :] += jnp.sum(dyc * xu[HALO:HALO + tm, :], axis=0, keepdims=True)
        du = cw_ref[2:3, :] * dyc + cw_ref[1:2, :] * ed[1:1 + tm, :] + cw_ref[0:1, :] * ed[2:2 + tm, :]
        d_ref[:, c_cc:c_cc + 512] = (du * ch_ref[...]).astype(BF16)
        d_ref[:, c_ch:c_ch + 512] = (du * cc_ref[...]).astype(BF16)

        sgm, dsgm = _silu_and_grad(gm_ref[...])
        dmm = dm_ref[:, 0:1024]
        ov = o_ref[...]
        dov = dmm * sgm
        do_ref[...] = dov
        d_ref[:, 0:1024] = (dmm * ov * dsgm).astype(BF16)
        lane = lax.broadcasted_iota(jnp.int32, (tm, LANE), 1)
        dmat = jnp.zeros((tm, LANE), F32)
        for h in range(N_HEADS):
            hs = slice(128 * h, 128 * h + 128)
            dmat = jnp.where(lane == h, jnp.sum(dov[:, hs] * ov[:, hs], axis=1, keepdims=True), dmat)
        dmat_t = jnp.transpose(dmat)
        for part in range(per):
            @pl.when(i % per == part)
            def _():
                for h in range(N_HEADS):
                    dl_ref[h, pl.ds(i // per, 1), part * tm:(part + 1) * tm] = dmat_t[h:h + 1, :]

    outs = (jax.ShapeDtypeStruct((S, W_MIX), BF16), jax.ShapeDtypeStruct((S, 1024), F32),
            jax.ShapeDtypeStruct((N_HEADS, S // t, t), F32),
            jax.ShapeDtypeStruct((4, 128, 128), F32), jax.ShapeDtypeStruct((1, 512), F32),
            jax.ShapeDtypeStruct((3, 512), F32))
    scr = [pltpu.VMEM((tm + HALO, 512), F32) for _ in range(4)]
    return _pcall(kern, name=name, out_shape=outs, grid=(n,),
                  in_specs=[main(2048, 0), nxt(1024, 1),
                            main(1024, 1), main(512, 4), main(512, 5), main(512, 6), main(512, 7), main(512, 8),
                            main(512, 9), prev(512, 4), prev(512, 6), prev(512, 8),
                            nxt(512, 5), nxt(512, 7), nxt(512, 9),
                            main(1024, 0), full((4, 128, 128)), full((1, 512)), full((3, 512))],
                  out_specs=(main(W_MIX, 0), main(1024, 0), full((N_HEADS, S // t, t)), full((4, 128, 128)),
                             full((1, 512)), full((3, 512))),
                  scratch=scr, dims=("arbitrary",), vmem_mb=56)(
                      dmix, dmix, proj, proj, proj, proj, proj, proj, proj, proj, proj, proj, proj, proj, proj,
                      o, wpool, ps.reshape(1, 512), convw)


def _outproj_residual(mix, wout, h, bout, *, name):
    S, Dm = h.shape
    tm = min(512, S)

    def kern(mix_ref, w_ref, h_ref, bo_ref, r_ref):
        out = jnp.dot(mix_ref[...], w_ref[...], preferred_element_type=F32) + bo_ref[...]
        r_ref[...] = ALPHA * h_ref[...] + out

    row = pl.BlockSpec((tm, Dm), lambda i: (i, 0))
    vec = pl.BlockSpec((1, Dm), lambda i: (0, 0))
    wsp = pl.BlockSpec((Dm, Dm), lambda i: (0, 0), pipeline_mode=pl.Buffered(1))
    return _pcall(kern, name=name, out_shape=jax.ShapeDtypeStruct((S, Dm), F32), grid=(S // tm,),
                  in_specs=[row, wsp, row, vec], out_specs=row, dims=("parallel",), vmem_mb=56)(
                      mix, wout, h, bout.reshape(1, Dm))


def _outproj_ln(mix, wout, h, bout, g, b, *, name):
    S, Dm = h.shape
    tm = min(512, S)

    def kern(mix_ref, w_ref, h_ref, bo_ref, g_ref, b_ref, y_ref, yb_ref, r_ref):
        out = jnp.dot(mix_ref[...], w_ref[...], preferred_element_type=F32) + bo_ref[...]
        r = ALPHA * h_ref[...] + out
        r_ref[...] = r
        mu = jnp.mean(r, axis=-1, keepdims=True)
        xc = r - mu
        var = jnp.mean(xc * xc, axis=-1, keepdims=True)
        y = xc * lax.rsqrt(var + LN_EPS) * g_ref[...] + b_ref[...]
        y_ref[...] = y
        yb_ref[...] = y.astype(BF16)

    row = pl.BlockSpec((tm, Dm), lambda i: (i, 0))
    vec = pl.BlockSpec((1, Dm), lambda i: (0, 0))
    wsp = pl.BlockSpec((Dm, Dm), lambda i: (0, 0), pipeline_mode=pl.Buffered(1))
    sds = jax.ShapeDtypeStruct((S, Dm), F32)
    return _pcall(kern, name=name, out_shape=(sds, jax.ShapeDtypeStruct((S, Dm), BF16), sds), grid=(S // tm,),
                  in_specs=[row, wsp, row, vec, vec, vec], out_specs=(row, row, row), dims=("parallel",),
                  vmem_mb=56)(
                      mix, wout, h, bout.reshape(1, Dm), g.reshape(1, Dm), b.reshape(1, Dm))


def _adamw_math(w, g, m, v):
    m = ADAM_B1 * m + (1.0 - ADAM_B1) * g
    v = ADAM_B2 * v + (1.0 - ADAM_B2) * (g * g)
    m_hat = m / (1.0 - ADAM_B1 ** ADAM_STEP)
    v_hat = v / (1.0 - ADAM_B2 ** ADAM_STEP)
    delta = -ADAM_LR * (m_hat / (jnp.sqrt(v_hat) + ADAM_EPS) + ADAM_WD * w)
    return delta, m, v


def _row_tile(R, C):
    best = None
    for cand in range(8, R, 8):
        if R % cand == 0 and cand * C <= 256 * 1024:
            best = cand
    return best if best is not None else R


def _adamw(w, g, m, v, *, name):
    shape = w.shape
    C = shape[-1]
    R = 1
    for s in shape[:-1]:
        R *= s
    tr = _row_tile(R, C)

    def kern(w_ref, g_ref, m_ref, v_ref, d_ref, mo_ref, vo_ref):
        d, mn, vn = _adamw_math(w_ref[...], g_ref[...], m_ref[...], v_ref[...])
        d_ref[...] = d
        mo_ref[...] = mn
        vo_ref[...] = vn

    blk = pl.BlockSpec((tr, C), lambda i: (i, 0))
    sds = jax.ShapeDtypeStruct((R, C), F32)
    outs = _pcall(kern, name=name, out_shape=(sds, sds, sds), grid=(R // tr,), in_specs=[blk] * 4,
                  out_specs=(blk, blk, blk), dims=("parallel",), vmem_mb=48)(
                      w.reshape(R, C), g.reshape(R, C), m.reshape(R, C), v.reshape(R, C))
    return tuple(t.reshape(shape) for t in outs)


def _adamw_halves(w, m, v, halves, c_idx, *, name, comm=None):
    _, R, C = w.shape
    ch = C // 2
    tr = _row_tile(R, ch)
    nb = R // tr

    def kern(c_ref, w_ref, a0_ref, b0_ref, a1_ref, b1_ref, m_ref, v_ref, g_ref, d_ref, mo_ref, vo_ref):
        layer = pl.program_id(0) // nb
        mine = pl.program_id(1) == c_ref[0]
        g = jnp.where(layer == 0, jnp.where(mine, a0_ref[...], b0_ref[...]),
                      jnp.where(mine, a1_ref[...], b1_ref[...]))
        g_ref[...] = g
        d, mn, vn = _adamw_math(w_ref[...], g, m_ref[...], v_ref[...])
        d_ref[...] = d
        mo_ref[...] = mn
        vo_ref[...] = vn

    full = pl.BlockSpec((tr, ch), lambda i, hc: (i, hc))
    half = pl.BlockSpec((tr, ch), lambda i, hc: (i % nb, 0))
    sds = jax.ShapeDtypeStruct((2 * R, C), F32)
    (a0, b0), (a1, b1) = halves
    res = _pcall(kern, name=name, out_shape=(sds,) * 4, grid=(2 * nb, 2),
                 in_specs=[pl.BlockSpec(memory_space=pltpu.SMEM), full, half, half, half, half, full, full],
                 out_specs=(full,) * 4, dims=("parallel", "parallel"), vmem_mb=48, comm=comm)(
                     c_idx, w.reshape(2 * R, C), a0, b0, a1, b1, m.reshape(2 * R, C), v.reshape(2 * R, C))
    outs, landed = res if comm is not None else (res, None)
    outs = tuple(t.reshape(2, R, C) for t in outs)
    return outs if comm is None else (outs, landed)


def _packed_pieces(shape):
    if len(shape) == 4:
        return [((l * shape[1] + g) * 128, 128, (l, g)) for l in range(shape[0]) for g in range(shape[1])]
    per_row = shape[1] // LANE
    return [(a * per_row + j, 1, (slice(a, a + 1), slice(LANE * j, LANE * (j + 1))))
            for a in range(shape[0]) for j in range(per_row)]


def _small_sum_adamw(gathered, own, weights, *, name):
    R = gathered.shape[1]
    nw = len(weights)
    shapes = [w.shape for w, _, _ in weights]
    first_row, r0 = [], 0
    for shp in shapes:
        first_row.append(r0)
        n = 1
        for s in shp:
            n *= s
        r0 += n // LANE

    def kern(ga_ref, own_ref, *refs):
        ins, gsum_ref, outs = refs[:3 * nw], refs[3 * nw], refs[3 * nw + 1:]
        me = 4 * lax.axis_index("x") + 2 * lax.axis_index("y") + lax.axis_index("c")

        def block(k):
            other = ga_ref[jnp.where(me == k, (k + 1) % N_DEV, k)]
            return jnp.where(me == k, own_ref[...], other)

        g = block(0)
        for k in range(1, N_DEV):
            g = g + block(k)
        gsum_ref[...] = g
        for p, shp in enumerate(shapes):
            w_ref, m_ref, v_ref = ins[3 * p:3 * p + 3]
            g_out, d_out, m_out, v_out = outs[4 * p:4 * p + 4]
            for row, rows, idx in _packed_pieces(shp):
                gp = gsum_ref[first_row[p] + row:first_row[p] + row + rows, :]
                d, mn, vn = _adamw_math(w_ref[idx], gp, m_ref[idx], v_ref[idx])
                g_out[idx] = gp
                d_out[idx] = d
                m_out[idx] = mn
                v_out[idx] = vn

    out_shape = [jax.ShapeDtypeStruct((R, LANE), F32)]
    for shp in shapes:
        out_shape += [jax.ShapeDtypeStruct(shp, F32)] * 4
    flat = [a for wmv in weights for a in wmv]
    res = _pcall(kern, name=name, out_shape=tuple(out_shape), vmem_mb=48)(gathered, own, *flat)
    return res[0], [tuple(res[1 + 4 * p:5 + 4 * p]) for p in range(nw)]


def _pair_sums(grads, theirs, c_idx, *, name):
    n = len(grads)
    steps = 8
    tiles = [(g.shape[0] // steps, g.shape[1] // 2) for g in grads]

    def kern(c_ref, *refs):
        for a in range(n):
            refs[2 * n + a][...] = (refs[a][...] + refs[n + a][...]).astype(BF16)

    gs = pltpu.PrefetchScalarGridSpec(
        num_scalar_prefetch=1, grid=(steps,),
        in_specs=[pl.BlockSpec(tl, lambda i, c: (i, c[0])) for tl in tiles]
        + [pl.BlockSpec(tl, lambda i, c: (i, 0)) for tl in tiles],
        out_specs=tuple(pl.BlockSpec(tl, lambda i, c: (i, 0)) for tl in tiles))
    out_shape = tuple(jax.ShapeDtypeStruct((g.shape[0], g.shape[1] // 2), BF16) for g in grads)
    return pl.pallas_call(kern, name=name, out_shape=out_shape, grid_spec=gs,
                          compiler_params=pltpu.CompilerParams(dimension_semantics=("parallel",),
                                                               vmem_limit_bytes=48 << 20))(c_idx, *grads, *theirs)


WeightRows = collections.namedtuple("WeightRows", "full_rows own_rows cols pieces zero_rows")


def _w_in_piece_a(j):
    return jnp.where(j == 0, 0, 1232 * j + GAP)


def _w_in_piece_b(j):
    return jnp.where(j == 0, GAP_AT + GAP, 1232 * j + GAP_AT + GAP)


W_IN = WeightRows(NP, 1232, D_MODEL, ((0, GAP_AT, _w_in_piece_a), (GAP_AT, 1232 - GAP_AT, _w_in_piece_b)),
                  ((GAP_AT, GAP),))
W_OUT = WeightRows(2048, 512, D_MODEL, ((0, 512, lambda j: 512 * j),), ())
W_UQ = WeightRows(2048, 384, Q_LORA, ((0, 192, lambda j: 512 * j), (192, 192, lambda j: 512 * j + 256)),
                  tuple((256 * h + 192, 64) for h in range(N_HEADS)))
W_UKV = WeightRows(2048, 512, KV_LORA, ((0, 512, lambda j: 512 * j),), ())
W_CONV = WeightRows(64, 16, 256, ((0, 16, lambda j: 16 * j),), ())
SHARDED = (W_IN, W_OUT, W_UQ, W_UKV)
SHARDED_NAMES = ("w_in", "w_out", "w_uq", "w_ukv")
WEIGHT_ROWS = dict(zip(SHARDED_NAMES, SHARDED))


def _mesh_pos():
    x, y, c = lax.axis_index("x"), lax.axis_index("y"), lax.axis_index("c")
    return x, y, c


def _other_chips(x, y):
    return [(1 - x, y), (x, 1 - y), (1 - x, 1 - y)]


def _rows(start, n):
    return pl.ds(pl.multiple_of(start, 16), n)


def _half_cols(spec, c):
    ch = spec.cols // 2
    return pl.ds(pl.multiple_of(c * ch, LANE), ch)


def _allgather_script(specs, shards, zeros, layers):
    na = len(specs)
    zlist = [a for a in range(na) if zeros[a] is not None]
    n_layers = [shards[a].shape[0] if layers[a] is None else 1 for a in range(na)]
    plan_first, plan_own, plan_zero = [], [], []
    for a, spec in enumerate(specs):
        for p in range(len(spec.pieces)):
            plan_own.append((a, p))
            for k in range(3):
                plan_first.append((a, p, k))
        for z in range(len(spec.zero_rows)):
            for l in range(n_layers[a]):
                plan_zero.append((a, z, l))
    nf = len(plan_first)
    n_sems = 2 * nf + len(plan_own) + len(plan_zero)

    def copies(ins_all, outs, send_sems, recv_sems):
        ins = [ins_all[a] if layers[a] is None else ins_all[a].at[pl.ds(layers[a], 1)] for a in range(na)]
        zrefs = dict(zip(zlist, ins_all[na:]))
        x, y, c = _mesh_pos()
        j = 2 * x + y
        chips = _other_chips(x, y)
        sibling = (x, y, 1 - c)

        def remote(src, dst, sem, to):
            return pltpu.make_async_remote_copy(src_ref=src, dst_ref=dst, send_sem=send_sems.at[sem],
                                                recv_sem=recv_sems.at[sem], device_id=to, device_id_type=MESH)

        def block(a, p, chip, cols):
            _, n, dst = specs[a].pieces[p]
            return outs[a].at[:, _rows(dst(chip), n), cols]

        def first(i):
            a, p, k = plan_first[i]
            src0, n, _ = specs[a].pieces[p]
            cols = _half_cols(specs[a], c)
            return remote(ins[a].at[:, pl.ds(src0, n), cols], block(a, p, j, cols), i, (*chips[k], c))

        def landed(i, half):
            a, p, k = plan_first[i]
            return block(a, p, 2 * chips[k][0] + chips[k][1], _half_cols(specs[a], half))

        def arrival(i, half, sem):
            return remote(landed(i, half), landed(i, half), sem, sibling)

        def passed(i):
            return remote(landed(i, c), landed(i, c), nf + i, sibling)

        def own(i):
            a, p = plan_own[i]
            src0, n, _ = specs[a].pieces[p]
            return remote(ins[a].at[:, pl.ds(src0, n), :], block(a, p, j, slice(None)), 2 * nf + i, sibling)

        def zero(i):
            a, z, l = plan_zero[i]
            r0, n = specs[a].zero_rows[z]
            return remote(zrefs[a].at[pl.ds(0, n), :], outs[a].at[l, pl.ds(r0, n), :],
                          2 * nf + len(plan_own) + i, sibling)

        fixed = [own(i) for i in range(len(plan_own))] + [zero(i) for i in range(len(plan_zero))]
        return c, fixed, first, arrival, passed

    def start(ins, outs, send_sems, recv_sems):
        _, fixed, first, _, _ = copies(ins, outs, send_sems, recv_sems)
        for cp in fixed:
            cp.start()
        for i in range(nf):
            first(i).start()

    def finish(ins, outs, send_sems, recv_sems):
        c, fixed, first, arrival, passed = copies(ins, outs, send_sems, recv_sems)
        for i in range(nf):
            arrival(i, c, i).wait_recv()
            passed(i).start()
        for i in range(nf):
            arrival(i, 1 - c, nf + i).wait_recv()
        for cp in fixed:
            cp.wait()
        for i in range(nf):
            first(i).wait_send()
            passed(i).wait_send()

    out_shape = tuple(jax.ShapeDtypeStruct((n_layers[a], spec.full_rows, spec.cols), BF16)
                      for a, spec in enumerate(specs))
    args = tuple(shards) + tuple(zeros[a] for a in zlist)
    return CommScript(args, out_shape, n_sems, start, finish)


def _start_all_wait_all(args, out_shape, n_sems, make_copies):
    def start(ins, outs, send_sems, recv_sems):
        for cp in make_copies(ins, outs, send_sems, recv_sems):
            cp.start()

    def finish(ins, outs, send_sems, recv_sems):
        for cp in make_copies(ins, outs, send_sems, recv_sems):
            cp.wait()

    return CommScript(tuple(args), tuple(out_shape), n_sems, start, finish)


def _exchange_script(specs, grads):
    na = len(grads)

    def make_copies(ins, outs, send_sems, recv_sems):
        x, y, c = _mesh_pos()
        return [pltpu.make_async_remote_copy(
            src_ref=ins[a].at[:, _half_cols(specs[a], 1 - c)], dst_ref=outs[a], send_sem=send_sems.at[a],
            recv_sem=recv_sems.at[a], device_id=(x, y, 1 - c), device_id_type=MESH) for a in range(na)]

    out_shape = [jax.ShapeDtypeStruct((s.full_rows, s.cols // 2), F32) for s in specs]
    return _start_all_wait_all(grads, out_shape, na, make_copies)


def _scatter_script(specs, parts):
    na = len(parts)
    plan = [(a, p, k) for a in range(na) for p in range(len(specs[a].pieces)) for k in range(3)]

    def make_copies(ins, outs, send_sems, recv_sems):
        x, y, c = _mesh_pos()
        chips = _other_chips(x, y)
        copies = []
        for i, (a, p, k) in enumerate(plan):
            src0, n, dst = specs[a].pieces[p]
            pk = 2 * chips[k][0] + chips[k][1]
            copies.append(pltpu.make_async_remote_copy(
                src_ref=ins[a].at[_rows(dst(pk), n), :], dst_ref=outs[a].at[k, pl.ds(src0, n), :],
                send_sem=send_sems.at[i], recv_sem=recv_sems.at[i], device_id=(*chips[k], c), device_id_type=MESH))
        return copies

    out_shape = [jax.ShapeDtypeStruct((3, s.own_rows, s.cols // 2), BF16) for s in specs]
    return _start_all_wait_all(parts, out_shape, len(plan), make_copies)


def _chip_sums(specs, parts, recvs, *, name):
    n = len(specs)
    plan = [(a, p) for a in range(n) for p in range(len(specs[a].pieces))]

    def kern(*refs):
        recv_refs, part_refs, o_refs = refs[:n], refs[n:2 * n], refs[2 * n:3 * n]
        own_refs, sems = refs[3 * n:4 * n], refs[4 * n]
        j = 2 * lax.axis_index("x") + lax.axis_index("y")
        copies = []
        for i, (a, p) in enumerate(plan):
            src0, rows, dst = specs[a].pieces[p]
            copies.append(pltpu.make_async_copy(part_refs[a].at[_rows(dst(j), rows), :],
                                                own_refs[a].at[pl.ds(src0, rows), :], sems.at[i]))
        for cp in copies:
            cp.start()
        for cp in copies:
            cp.wait()
        for a in range(n):
            r = recv_refs[a]
            o_refs[a][...] = ((own_refs[a][...].astype(F32) + r[0].astype(F32)) + r[1].astype(F32)) \
                + r[2].astype(F32)

    vm = pl.BlockSpec(memory_space=pltpu.VMEM)
    shapes = [(s.own_rows, s.cols // 2) for s in specs]
    return _pcall(kern, name=name, out_shape=tuple(jax.ShapeDtypeStruct(shp, F32) for shp in shapes),
                  in_specs=[vm] * n + [HBM_SPEC] * n, out_specs=(vm,) * n,
                  scratch=[pltpu.VMEM(shp, BF16) for shp in shapes] + [pltpu.SemaphoreType.DMA((len(plan),))],
                  vmem_mb=56)(*recvs, *parts)


def _sibling_script(sums):
    na = len(sums)

    def make_copies(ins, outs, send_sems, recv_sems):
        x, y, c = _mesh_pos()
        return [pltpu.make_async_remote_copy(
            src_ref=ins[a], dst_ref=outs[a], send_sem=send_sems.at[a], recv_sem=recv_sems.at[a],
            device_id=(x, y, 1 - c), device_id_type=MESH) for a in range(na)]

    out_shape = [jax.ShapeDtypeStruct(t.shape, t.dtype) for t in sums]
    return _start_all_wait_all(sums, out_shape, na, make_copies)


class _SemWindow:
    def __init__(self, sems, offset):
        self._sems, self._offset = sems, offset

    @property
    def at(self):
        return self

    def __getitem__(self, i):
        return self._sems.at[i + self._offset]


def _merge_scripts(*scripts):
    a_off, o_off, s_off = [0], [0], [0]
    for s in scripts:
        a_off.append(a_off[-1] + len(s.args))
        o_off.append(o_off[-1] + len(s.out_shape))
        s_off.append(s_off[-1] + s.n_sems)

    def phase(which):
        def run(ins, outs, send_sems, recv_sems):
            for n, s in enumerate(scripts):
                getattr(s, which)(ins[a_off[n]:a_off[n + 1]], outs[o_off[n]:o_off[n + 1]],
                                  _SemWindow(send_sems, s_off[n]), _SemWindow(recv_sems, s_off[n]))
        return run

    return CommScript(sum((tuple(s.args) for s in scripts), ()), sum((tuple(s.out_shape) for s in scripts), ()),
                      s_off[-1], phase("start"), phase("finish"))


class _GradReducer:
    def __init__(self, layer, names, grads, c_idx):
        self.specs = tuple(WEIGHT_ROWS[nm] for nm in names)
        self.grads, self.c_idx = tuple(grads), c_idx
        self.names = [f"{nm}{layer}" for nm in names]

    def exchange(self):
        return _exchange_script(self.specs, self.grads)

    def scatter(self, theirs):
        self.parts = tuple(_pair_sums(self.grads, tuple(theirs), self.c_idx, name=f"pair_sums_{self.names[0]}"))
        return _scatter_script(self.specs, self.parts)

    def sibling(self, recv):
        self.sums = tuple(_chip_sums(self.specs, self.parts, tuple(recv), name=f"chip_sums_{self.names[0]}"))
        return _sibling_script(self.sums)

    def done(self, others):
        return list(zip(self.sums, others))


def _allgather_small_script(block):
    m_per, n = block.shape

    def copies(ins, outs, send_sems, recv_sems):
        (x_ref,), (out_ref,) = ins, outs
        x, y, c = _mesh_pos()
        me, sibling = (x, y, c), (x, y, 1 - c)
        chips = _other_chips(x, y)

        def rows(px, py, pc):
            return out_ref.at[4 * px + 2 * py + pc]

        def copy(k, blk, to, src=None):
            return pltpu.make_async_remote_copy(
                src_ref=rows(*blk) if src is None else src, dst_ref=rows(*blk), send_sem=send_sems.at[k],
                recv_sem=recv_sems.at[k], device_id=to, device_id_type=MESH)

        first = [copy(0, me, sibling, src=x_ref)]
        first += [copy(1 + k, me, (*chip, c), src=x_ref) for k, chip in enumerate(chips)]
        passed = [copy(4 + k, (*chip, c), sibling) for k, chip in enumerate(chips)]
        landed = [copy(1 + k, (*chip, c), me) for k, chip in enumerate(chips)]
        from_sibling = [copy(0, sibling, me)] + [copy(4 + k, (*chip, 1 - c), me) for k, chip in enumerate(chips)]
        return first, passed, landed, from_sibling

    def start(ins, outs, send_sems, recv_sems):
        first, _, _, _ = copies(ins, outs, send_sems, recv_sems)
        for cp in first:
            cp.start()

    def finish(ins, outs, send_sems, recv_sems):
        first, passed, landed, from_sibling = copies(ins, outs, send_sems, recv_sems)
        for k in range(3):
            landed[k].wait_recv()
            passed[k].start()
        for cp in from_sibling:
            cp.wait_recv()
        for cp in first + passed:
            cp.wait_send()

    return CommScript((block,), (jax.ShapeDtypeStruct((N_DEV, m_per, n), block.dtype),), 7, start, finish)


def _rope_tables(positions):
    half = ROPE // 2
    inv_freq = ROPE_THETA ** (-jnp.arange(half, dtype=F32) / half)
    ang = positions.astype(F32)[:, None] * inv_freq
    cos, sin = jnp.cos(ang), jnp.sin(ang)
    S = positions.shape[0]
    cos_t = jnp.concatenate([cos, cos, jnp.ones((S, 64), F32)], axis=1)
    sin_t = jnp.concatenate([-sin, sin, jnp.zeros((S, 64), F32)], axis=1)
    return cos_t, sin_t


def _decode_conv(bits):
    rows = bits.reshape(DEPTH, N_CHIPS, 16, 256)[:, :, :3, :]
    conv = lax.bitcast_convert_type(rows.reshape(DEPTH, N_CHIPS, 3, 128, 2), F32)
    return jnp.transpose(conv, (0, 2, 1, 3)).reshape(DEPTH, 3, 512)


def _local_step(x, positions, target, emb_g, emb_b, w_in_t0, rest0, weights1, q_g, kv_g, w_pool, pool_scale,
                b_out, ln_g, ln_b, c_idx=None):
    cos_t, sin_t = _rope_tables(positions)
    if isinstance(w_in_t0, CommScript):
        (h, hb), (landed,) = _ln_fwd(x, emb_g, emb_b, name="emb_ln", comm=w_in_t0)
        w_in_t0 = landed[0]
    else:
        h, hb = _ln_fwd(x, emb_g, emb_b, name="emb_ln")
    weights = [None, weights1]
    saved = []
    for l in range(DEPTH):
        if l == 0 and isinstance(rest0, CommScript):
            proj, landed = _matmul(hb, w_in_t0, "nt", name="in_proj0", tm=1024, tn=1024, tk=2048, vmem_mb=56,
                                   comm=rest0)
            weights[0] = (w_in_t0,) + tuple(a[0] for a in landed[:3])
            conv_w = _decode_conv(landed[3])
        else:
            if l == 0:
                weights[0] = (w_in_t0,) + tuple(rest0[:3])
                conv_w = rest0[3]
            proj = _matmul(hb, weights[l][0], "nt", name=f"in_proj{l}", tm=1024, tn=1024, tk=2048, vmem_mb=56)
        w_in_t, w_out, w_uq_t, w_ukv_t = weights[l]
        qc, kc, v, vt, qn, kvn = _mla_qkv(proj, cos_t, sin_t, q_g[l], kv_g[l], w_uq_t, w_ukv_t, name=f"mla_qkv{l}")
        nxt = weights[l + 1] if l + 1 < DEPTH else None
        if isinstance(nxt, CommScript):
            (o, lse2), landed = _flash_fwd(qc, kc, vt, name=f"flash_fwd{l}", comm=nxt)
            weights[l + 1] = tuple(a[0] for a in landed)
        else:
            o, lse2 = _flash_fwd(qc, kc, vt, name=f"flash_fwd{l}")
        mix = _mixer_fwd(proj, o, w_pool[l], pool_scale[l], conv_w[l], name=f"mixer_fwd{l}")
        if l == DEPTH - 1:
            r = _outproj_residual(mix, w_out, h, b_out[l], name=f"out_proj{l}")
            saved.append((hb, proj, qc, kc, v, qn, kvn, o, lse2, mix, r))
        else:
            h_next, hb_next, r = _outproj_ln(mix, w_out, h, b_out[l], ln_g[l], ln_b[l], name=f"out_proj_ln{l}")
            saved.append((hb, proj, qc, kc, v, qn, kvn, o, lse2, mix, r))
            h, hb = h_next, hb_next

    small = [None] * DEPTH
    big = [None] * DEPTH
    above = scatter_above = None
    for l in reversed(range(DEPTH)):
        w_in_t, w_out, w_uq_t, w_ukv_t = weights[l]
        hb_in, proj, qc, kc, v, qn, kvn, o, lse2, mix, r = saved[l]
        if l == DEPTH - 1:
            loss_acc, dr, drb, d_ln_g, d_ln_b, d_b_out = _loss_ln_bwd(target, r, ln_g[l], ln_b[l], name="loss_ln_bwd")
        else:
            dr, drb, d_ln_g, d_ln_b, d_b_out = _ln_bwd(dh, r, ln_g[l], name=f"ln_bwd{l}")
        dmix = _matmul(drb, w_out, "nt", name=f"dmix{l}", tm=1024, tn=1024, tk=2048, vmem_mb=56)
        d_w_out = _matmul(mix, drb, "tn", name=f"dw_out{l}", tm=1024, tn=1024, tk=2048, vmem_mb=56)
        d_mix, do, delta, d_w_pool, d_ps, d_conv = _mixer_bwd(dmix, proj, o, w_pool[l], pool_scale[l], conv_w[l],
                                                              name=f"mixer_bwd{l}")
        if above is not None:
            (dqb, dkvb, dkr), recv = _flash_bwd(qc, kc, v, do, lse2, delta, cos_t, sin_t, name=f"flash_bwd{l}",
                                                comm=scatter_above)
            sibling_above = above.sibling(recv)
        else:
            dqb, dkvb, dkr = _flash_bwd(qc, kc, v, do, lse2, delta, cos_t, sin_t, name=f"flash_bwd{l}")
        d_mla, d_qg, d_kvg, d_w_uq_t, d_w_ukv_t = _mla_qkv_bwd(
            dqb, dkvb, dkr, qn, kvn, proj, cos_t, sin_t, q_g[l], kv_g[l], w_uq_t, w_ukv_t, name=f"mla_qkv_bwd{l}")
        small[l] = dict(q_g=d_qg[0], kv_g=d_kvg[0], w_pool=d_w_pool, pool_scale=d_ps[0], conv_w=d_conv,
                        b_out=d_b_out[0], ln_g=d_ln_g[0], ln_b=d_ln_b[0])
        rest = (d_w_out, d_w_uq_t, d_w_ukv_t)
        if c_idx is None:
            d_w_in_t = _dproj_t_times_h(d_mla, d_mix, hb_in, name=f"dw_in{l}")
            dh = _dproj_times_w(d_mla, d_mix, w_in_t, dr, ALPHA, name=f"dh{l}")
            big[l] = (d_w_in_t,) + rest
        elif l > 0:
            d_w_in_t = _dproj_t_times_h(d_mla, d_mix, hb_in, name=f"dw_in{l}")
            above = _GradReducer(l, SHARDED_NAMES, (d_w_in_t,) + rest, c_idx)
            dh, theirs = _dproj_times_w(d_mla, d_mix, w_in_t, dr, ALPHA, name=f"dh{l}", comm=above.exchange())
            scatter_above = above.scatter(theirs)
        else:
            red_rest = _GradReducer(l, SHARDED_NAMES[1:], rest, c_idx)
            d_w_in_t, landed = _dproj_t_times_h(d_mla, d_mix, hb_in, name=f"dw_in{l}",
                                                comm=_merge_scripts(sibling_above, red_rest.exchange()))
            big[l + 1] = above.done(landed[:len(SHARDED)])
            red_in = _GradReducer(l, SHARDED_NAMES[:1], (d_w_in_t,), c_idx)
            landed = _run_comm(_merge_scripts(red_in.exchange(), red_rest.scatter(landed[len(SHARDED):])),
                               name="exchange_w_in0")
            sibling_rest = red_rest.sibling(landed[1:])
            dh, landed = _dproj_times_w(d_mla, d_mix, w_in_t, dr, ALPHA, name=f"dh{l}",
                                        comm=_merge_scripts(red_in.scatter(landed[:1]), sibling_rest))
            recv_in, others_rest = landed[:1], landed[1:]
    grad_x, _, d_emb_g, d_emb_b, _ = _ln_bwd(dh, x, emb_g, name="emb_ln_bwd", bf16_copy=False)
    if c_idx is not None:
        others_in = _run_comm(red_in.sibling(recv_in), name="send_to_sibling0")
        big[0] = red_in.done(others_in) + red_rest.done(others_rest)
    return loss_acc[0, 0], grad_x, d_emb_g, d_emb_b, small, big


SMALL_ORDER = ("emb_ln_g", "emb_ln_b", "q_norm_g", "kv_norm_g", "w_pool", "pool_scale", "b_out", "ln_g", "ln_b")
SMALL_LAYER_KEYS = ("q_g", "kv_g", "w_pool", "pool_scale", "b_out", "ln_g", "ln_b", "conv_w")


def _pack_small(arrs, extra_rows):
    flat = jnp.concatenate([a.reshape(-1) for a in arrs])
    rows = flat.shape[0] // LANE
    total = -(-(rows + extra_rows) // 8) * 8
    return jnp.pad(flat, (0, total * LANE - flat.shape[0])).reshape(total, LANE)


def kernel(x, positions, emb_ln_g, emb_ln_b, w_in, q_norm_g, kv_norm_g, w_uq, w_ukv, w_pool, pool_scale, conv_w, w_out, b_out, ln_g, ln_b, loss_target, m_emb_ln_g, m_emb_ln_b, m_w_in, m_q_norm_g, m_kv_norm_g, m_w_uq, m_w_ukv, m_w_pool, m_pool_scale, m_conv_w, m_w_out, m_b_out, m_ln_g, m_ln_b, v_emb_ln_g, v_emb_ln_b, v_w_in, v_q_norm_g, v_kv_norm_g, v_w_uq, v_w_ukv, v_w_pool, v_pool_scale, v_conv_w, v_w_out, v_b_out, v_ln_g, v_ln_b):
    xi, yi, ci = lax.axis_index("x"), lax.axis_index("y"), lax.axis_index("c")
    chip = 2 * xi + yi
    c_idx = ci.reshape(1).astype(jnp.int32)

    def t(a):
        return jnp.swapaxes(a, 1, 2)

    conv_bits = lax.bitcast_convert_type(conv_w.reshape(DEPTH, 3 * 128), BF16).reshape(DEPTH, 3, 256)
    conv_bits = jnp.pad(conv_bits, ((0, 0), (0, 13), (0, 0)))
    own = (t(w_in).astype(BF16), w_out.astype(BF16), t(w_uq).astype(BF16), t(w_ukv).astype(BF16))
    zeros = (jnp.zeros((GAP, D_MODEL), BF16), None, jnp.zeros((64, Q_LORA), BF16), None)
    gather_in0 = _allgather_script((W_IN,), own[:1], zeros[:1], (0,))
    gather0 = _allgather_script(SHARDED[1:] + (W_CONV,), own[1:] + (conv_bits,), zeros[1:] + (None,),
                                (0, 0, 0, None))
    gather1 = _allgather_script(SHARDED, own, zeros, (1, 1, 1, 1))

    loss_part, grad_x, d_emb_g, d_emb_b, grads, reduced = _local_step(
        x[0], positions[0], loss_target[0], emb_ln_g, emb_ln_b, gather_in0, gather0, gather1, q_norm_g, kv_norm_g,
        w_pool, pool_scale, b_out, ln_g, ln_b, c_idx)

    def rows(a):
        return a.reshape(1, -1) if a.ndim == 1 else a

    small_wmv = [tuple(rows(a) for a in wmv) for wmv in (
        (emb_ln_g, m_emb_ln_g, v_emb_ln_g), (emb_ln_b, m_emb_ln_b, v_emb_ln_b),
        (q_norm_g, m_q_norm_g, v_q_norm_g), (kv_norm_g, m_kv_norm_g, v_kv_norm_g), (w_pool, m_w_pool, v_w_pool),
        (pool_scale, m_pool_scale, v_pool_scale), (b_out, m_b_out, v_b_out), (ln_g, m_ln_g, v_ln_g),
        (ln_b, m_ln_b, v_ln_b))]
    packed_g = _pack_small(
        [d_emb_g, d_emb_b] + [jnp.stack([grads[l][key] for l in range(DEPTH)]) for key in SMALL_LAYER_KEYS]
        + [jnp.pad(loss_part.reshape(1), (0, LANE - 1))], 0)
    (gathered,) = _run_comm(_allgather_small_script(packed_g), name="allgather_small")
    g_tot, small_upd = _small_sum_adamw(gathered, packed_g, small_wmv, name="small_sum_adamw")
    off = sum(w.size for w, _, _ in small_wmv)
    flat_tot = g_tot.reshape(-1)

    def halves(a):
        return [reduced[l][a] for l in range(DEPTH)]

    upd = {}
    upd["w_in"] = tuple(t(o) for o in _adamw_halves(t(w_in), t(m_w_in), t(v_w_in), halves(0), c_idx,
                                                    name="adamw_w_in"))
    conv_tot = flat_tot[off:off + DEPTH * 3 * 512].reshape(DEPTH, 3, 512)
    loss = flat_tot[off + DEPTH * 3 * 512]
    g_conv = lax.dynamic_slice_in_dim(conv_tot, chip * 128, 128, axis=2)

    def whole(a):
        return jnp.stack([jnp.where(ci == 0, jnp.concatenate([mine, oth], axis=1),
                                    jnp.concatenate([oth, mine], axis=1)) for mine, oth in halves(a)])

    upd["w_out"] = _adamw_halves(w_out, m_w_out, v_w_out, halves(1), c_idx, name="adamw_w_out")
    g_uq, g_ukv = t(whole(2)), t(whole(3))
    upd["w_uq"] = (g_uq,) + _adamw(w_uq, g_uq, m_w_uq, v_w_uq, name="adamw_w_uq")
    upd["w_ukv"] = (g_ukv,) + _adamw(w_ukv, g_ukv, m_w_ukv, v_w_ukv, name="adamw_w_ukv")
    upd["conv_w"] = (g_conv,) + _adamw(conv_w, g_conv, m_conv_w, v_conv_w, name="adamw_conv_w")
    for nm, res in zip(SMALL_ORDER, small_upd):
        upd[nm] = tuple(a.reshape(-1) for a in res) if nm in ("emb_ln_g", "emb_ln_b") else res

    order = ("emb_ln_g", "emb_ln_b", "w_in", "q_norm_g", "kv_norm_g", "w_uq", "w_ukv", "w_pool", "pool_scale",
             "conv_w", "w_out", "b_out", "ln_g", "ln_b")
    outs = [loss, grad_x[None]]
    for field in range(4):
        outs += [upd[nm][field] for nm in order]
    return tuple(outs)
```

```python
import collections

import jax
import jax.numpy as jnp
from jax import lax
from jax.experimental import pallas as pl
from jax.experimental.pallas import tpu as pltpu

F32 = jnp.float32
BF16 = jnp.bfloat16
MESH = pl.DeviceIdType.MESH

D_MODEL = 2048
DEPTH = 2
N_HEADS = 8
NOPE = 128
ROPE = 64
Q_LORA = 512
KV_LORA = 256
D_MLA = 1024
POOL_WINDOWS = (2, 4, 8, 16)
D_IN_PROJ = 4928
LN_EPS = 1e-5
RMS_EPS = 1e-6
ROPE_THETA = 10000.0
ALPHA = (2 * DEPTH) ** 0.25
SCALE = (NOPE + ROPE) ** -0.5
LOG2E = 1.4426950408889634
SCALE_LOG2E = SCALE * LOG2E
ADAM_LR = 0.001
ADAM_B1 = 0.9
ADAM_B2 = 0.999
ADAM_EPS = 1e-08
ADAM_WD = 0.01
ADAM_STEP = 10

NP = 5120
GAP_AT = 832
GAP = NP - D_IN_PROJ
W_MLA = 1024
W_MIX = NP - W_MLA
HALO = 16
LANE = 128
N_CHIPS = 4
N_DEV = 8
TQ = 512
FWD_GROUP = 4
BWD_GROUP = 3

NN = (((1,), (0,)), ((), ()))
NT = (((1,), (1,)), ((), ()))
TN = (((0,), (0,)), ((), ()))


CommScript = collections.namedtuple("CommScript", "args out_shape n_sems start finish")
HBM_SPEC = pl.BlockSpec(memory_space=pl.ANY)


def _pcall(kern, *, name, out_shape, grid=None, in_specs=None, out_specs=None, scratch=(), dims=None,
           vmem_mb=None, comm=None):
    cp = {}
    if dims is not None:
        cp["dimension_semantics"] = dims if comm is None else ("arbitrary",) * len(dims)
    if vmem_mb is not None:
        cp["vmem_limit_bytes"] = vmem_mb << 20
    if comm is None:
        args = dict(name=name, out_shape=out_shape, scratch_shapes=list(scratch),
                    compiler_params=pltpu.CompilerParams(**cp))
        if grid is not None:
            args["grid"] = grid
        if in_specs is not None:
            args["in_specs"] = in_specs
        if out_specs is not None:
            args["out_specs"] = out_specs
        return pl.pallas_call(kern, **args)

    single = not isinstance(out_shape, (tuple, list))
    own_out = (out_shape,) if single else tuple(out_shape)
    own_out_specs = (out_specs,) if single else tuple(out_specs)
    n_in, n_out, n_scr = len(in_specs), len(own_out), len(scratch)
    na, no = len(comm.args), len(comm.out_shape)

    def at(end):
        cond = None
        for d, n in enumerate(grid):
            here = pl.program_id(d) == (n - 1 if end else 0)
            cond = here if cond is None else jnp.logical_and(cond, here)
        return cond

    def wrapped(*refs):
        own_in, c_in = refs[:n_in], refs[n_in:n_in + na]
        o0 = n_in + na
        own_o, c_out = refs[o0:o0 + n_out], refs[o0 + n_out:o0 + n_out + no]
        s0 = o0 + n_out + no
        own_s, (send_sems, recv_sems) = refs[s0:s0 + n_scr], refs[s0 + n_scr:]

        @pl.when(at(False))
        def _():
            comm.start(c_in, c_out, send_sems, recv_sems)

        kern(*own_in, *own_o, *own_s)

        @pl.when(at(True))
        def _():
            comm.finish(c_in, c_out, send_sems, recv_sems)

    call = pl.pallas_call(
        wrapped, name=name, out_shape=own_out + tuple(comm.out_shape), grid=grid,
        in_specs=list(in_specs) + [HBM_SPEC] * na, out_specs=own_out_specs + (HBM_SPEC,) * no,
        scratch_shapes=list(scratch) + [pltpu.SemaphoreType.DMA((comm.n_sems,)),
                                        pltpu.SemaphoreType.DMA((comm.n_sems,))],
        compiler_params=pltpu.CompilerParams(**cp))

    def run(*args):
        res = call(*args, *comm.args)
        own = res[0] if single else tuple(res[:n_out])
        return own, tuple(res[n_out:])

    return run


def _run_comm(script, *, name):
    na, no = len(script.args), len(script.out_shape)

    def body(*refs):
        ins, outs = refs[:na], refs[na:na + no]
        send_sems, recv_sems = refs[na + no:]
        script.start(ins, outs, send_sems, recv_sems)
        script.finish(ins, outs, send_sems, recv_sems)

    return pl.pallas_call(
        body, name=name, out_shape=tuple(script.out_shape), in_specs=[HBM_SPEC] * na, out_specs=(HBM_SPEC,) * no,
        scratch_shapes=[pltpu.SemaphoreType.DMA((script.n_sems,)), pltpu.SemaphoreType.DMA((script.n_sems,))])(
            *script.args)


def _sigmoid(g):
    return 1.0 / (1.0 + jnp.exp(-g))


def _silu_and_grad(g):
    sig = _sigmoid(g)
    return g * sig, sig * (1.0 + g * (1.0 - sig))


def _matmul(a, b, mode, *, name, tm, tn, tk, out_dtype=F32, vmem_mb=48, comm=None):
    if mode == "nn":
        (M, K), N = a.shape, b.shape[1]
    elif mode == "nt":
        (M, K), N = a.shape, b.shape[0]
    else:
        (K, M), N = a.shape, b.shape[1]
    tm, tn, tk = min(tm, M), min(tn, N), min(tk, K)
    assert M % tm == 0 and N % tn == 0 and K % tk == 0, (name, M, N, K)
    nk = K // tk
    dn = {"nn": NN, "nt": NT, "tn": TN}[mode]
    if mode == "tn":
        a_spec = pl.BlockSpec((tk, tm), lambda i, j, k: (k, i))
    else:
        a_spec = pl.BlockSpec((tm, tk), lambda i, j, k: (i, k))
    if mode == "nt":
        b_spec = pl.BlockSpec((tn, tk), lambda i, j, k: (j, k))
    else:
        b_spec = pl.BlockSpec((tk, tn), lambda i, j, k: (k, j))
    o_spec = pl.BlockSpec((tm, tn), lambda i, j, k: (i, j))

    def kern(a_ref, b_ref, o_ref, *rest):
        part = lax.dot_general(a_ref[...].astype(BF16), b_ref[...].astype(BF16), dn,
                               preferred_element_type=F32)
        if nk == 1:
            o_ref[...] = part.astype(out_dtype)
        else:
            acc_ref = rest[0]
            k = pl.program_id(2)

            @pl.when(k == 0)
            def _():
                acc_ref[...] = part

            @pl.when(k > 0)
            def _():
                acc_ref[...] += part

            @pl.when(k == nk - 1)
            def _():
                o_ref[...] = acc_ref[...].astype(out_dtype)

    scratch = [pltpu.VMEM((tm, tn), F32)] if nk > 1 else []
    return _pcall(kern, name=name, out_shape=jax.ShapeDtypeStruct((M, N), out_dtype),
                  grid=(M // tm, N // tn, nk), in_specs=[a_spec, b_spec], out_specs=o_spec, scratch=scratch,
                  dims=("parallel", "parallel", "arbitrary"), vmem_mb=vmem_mb, comm=comm)(a, b)


def _dproj_times_w(d_mla, d_mix, wt, add, add_scale, *, name, comm=None):
    S = d_mla.shape[0]
    Dm = wt.shape[1]
    tm, tn, tk = min(1024, S), 1024, 2048
    nk = 1 + W_MIX // tk

    def kern(a1_ref, a2_ref, b1_ref, b2_ref, add_ref, o_ref, acc_ref):
        k = pl.program_id(2)

        @pl.when(k == 0)
        def _():
            acc_ref[...] = jnp.dot(a1_ref[...], b1_ref[...], preferred_element_type=F32)

        @pl.when(k > 0)
        def _():
            acc_ref[...] += jnp.dot(a2_ref[...], b2_ref[...], preferred_element_type=F32)

        @pl.when(k == nk - 1)
        def _():
            o_ref[...] = add_scale * add_ref[...] + acc_ref[...]

    o_spec = pl.BlockSpec((tm, tn), lambda i, j, k: (i, j))
    b2_spec = pl.BlockSpec((pl.Element(tk), pl.Element(tn)),
                           lambda i, j, k: (pl.multiple_of(W_MLA + tk * jnp.maximum(k - 1, 0), W_MLA),
                                            pl.multiple_of(j * tn, tn)))
    return _pcall(kern, name=name, out_shape=jax.ShapeDtypeStruct((S, Dm), F32), grid=(S // tm, Dm // tn, nk),
                  in_specs=[pl.BlockSpec((tm, W_MLA), lambda i, j, k: (i, 0)),
                            pl.BlockSpec((tm, tk), lambda i, j, k: (i, jnp.maximum(k - 1, 0))),
                            pl.BlockSpec((W_MLA, tn), lambda i, j, k: (0, j)), b2_spec, o_spec],
                  out_specs=o_spec, scratch=[pltpu.VMEM((tm, tn), F32)],
                  dims=("parallel", "parallel", "arbitrary"), vmem_mb=56, comm=comm)(d_mla, d_mix, wt, wt, add)


def _dproj_t_times_h(d_mla, d_mix, h, *, name, comm=None):
    S, Dm = h.shape
    tm, tn, tk = W_MLA, 1024, min(2048, S)
    nk = S // tk

    def kern(a1_ref, a2_ref, b_ref, o_ref, acc_ref):
        i = pl.program_id(0)
        k = pl.program_id(2)
        b = b_ref[...].astype(BF16)

        def accumulate(part):
            @pl.when(k == 0)
            def _():
                acc_ref[...] = part

            @pl.when(k > 0)
            def _():
                acc_ref[...] += part

        @pl.when(i == 0)
        def _():
            accumulate(lax.dot_general(a1_ref[...], b, TN, preferred_element_type=F32))

        @pl.when(i > 0)
        def _():
            accumulate(lax.dot_general(a2_ref[...], b, TN, preferred_element_type=F32))

        @pl.when(k == nk - 1)
        def _():
            o_ref[...] = acc_ref[...]

    return _pcall(kern, name=name, out_shape=jax.ShapeDtypeStruct((NP, Dm), F32), grid=(NP // tm, Dm // tn, nk),
                  in_specs=[pl.BlockSpec((tk, tm), lambda i, j, k: (jnp.where(i == 0, k, nk - 1), 0)),
                            pl.BlockSpec((tk, tm), lambda i, j, k: (jnp.where(i == 0, 0, k), jnp.maximum(i - 1, 0))),
                            pl.BlockSpec((tk, tn), lambda i, j, k: (k, j))],
                  out_specs=pl.BlockSpec((tm, tn), lambda i, j, k: (i, j)), scratch=[pltpu.VMEM((tm, tn), F32)],
                  dims=("parallel", "parallel", "arbitrary"), vmem_mb=48, comm=comm)(d_mla, d_mix, h)


def _ln_fwd(x, g, b, *, name, comm=None):
    S, Dm = x.shape
    tm = min(512, S)

    def kern(x_ref, g_ref, b_ref, y_ref, yb_ref):
        xf = x_ref[...]
        mu = jnp.mean(xf, axis=-1, keepdims=True)
        xc = xf - mu
        var = jnp.mean(xc * xc, axis=-1, keepdims=True)
        y = xc * lax.rsqrt(var + LN_EPS) * g_ref[...] + b_ref[...]
        y_ref[...] = y
        yb_ref[...] = y.astype(BF16)

    row = pl.BlockSpec((tm, Dm), lambda i: (i, 0))
    vec = pl.BlockSpec((1, Dm), lambda i: (0, 0))
    return _pcall(kern, name=name,
                  out_shape=(jax.ShapeDtypeStruct((S, Dm), F32), jax.ShapeDtypeStruct((S, Dm), BF16)),
                  grid=(S // tm,), in_specs=[row, vec, vec], out_specs=(row, row), dims=("parallel",), vmem_mb=48,
                  comm=comm)(
                      x, g.reshape(1, Dm), b.reshape(1, Dm))


def _ln_bwd(dy, r, g, *, name, bf16_copy=True):
    S, Dm = r.shape
    tm = min(512, S)

    def kern(dy_ref, r_ref, g_ref, dr_ref, *rest):
        drb_ref = rest[0] if bf16_copy else None
        dg_ref, db_ref, ds_ref = rest[-3:]

        @pl.when(pl.program_id(0) == 0)
        def _():
            dg_ref[...] = jnp.zeros_like(dg_ref)
            db_ref[...] = jnp.zeros_like(db_ref)
            ds_ref[...] = jnp.zeros_like(ds_ref)

        rf = r_ref[...]
        dyf = dy_ref[...]
        mu = jnp.mean(rf, axis=-1, keepdims=True)
        xc = rf - mu
        var = jnp.mean(xc * xc, axis=-1, keepdims=True)
        rstd = lax.rsqrt(var + LN_EPS)
        xhat = xc * rstd
        dxh = dyf * g_ref[...]
        c1 = jnp.mean(dxh, axis=-1, keepdims=True)
        c2 = jnp.mean(dxh * xhat, axis=-1, keepdims=True)
        dr = rstd * (dxh - c1 - xhat * c2)
        dr_ref[...] = dr
        if bf16_copy:
            drb_ref[...] = dr.astype(BF16)
        dg_ref[...] += jnp.sum(dyf * xhat, axis=0, keepdims=True)
        db_ref[...] += jnp.sum(dyf, axis=0, keepdims=True)
        ds_ref[...] += jnp.sum(dr, axis=0, keepdims=True)

    row = pl.BlockSpec((tm, Dm), lambda i: (i, 0))
    vec = pl.BlockSpec((1, Dm), lambda i: (0, 0))
    vshape = jax.ShapeDtypeStruct((1, Dm), F32)
    copies = ((jax.ShapeDtypeStruct((S, Dm), BF16),), (row,)) if bf16_copy else ((), ())
    res = _pcall(kern, name=name,
                 out_shape=(jax.ShapeDtypeStruct((S, Dm), F32),) + copies[0] + (vshape, vshape, vshape),
                 grid=(S // tm,), in_specs=[row, row, vec], out_specs=(row,) + copies[1] + (vec, vec, vec),
                 dims=("arbitrary",), vmem_mb=48)(dy, r, g.reshape(1, Dm))
    return res if bf16_copy else (res[0], None) + tuple(res[1:])


def _loss_ln_bwd(target, r, g, b, *, name):
    S, Dm = r.shape
    tm = min(512, S)

    def kern(t_ref, r_ref, g_ref, b_ref, l_ref, dr_ref, drb_ref, dg_ref, db_ref, ds_ref):
        @pl.when(pl.program_id(0) == 0)
        def _():
            l_ref[...] = jnp.zeros_like(l_ref)
            dg_ref[...] = jnp.zeros_like(dg_ref)
            db_ref[...] = jnp.zeros_like(db_ref)
            ds_ref[...] = jnp.zeros_like(ds_ref)

        rf = r_ref[...]
        mu = jnp.mean(rf, axis=-1, keepdims=True)
        xc = rf - mu
        var = jnp.mean(xc * xc, axis=-1, keepdims=True)
        rstd = lax.rsqrt(var + LN_EPS)
        xhat = xc * rstd
        e = (xhat * g_ref[...] + b_ref[...]) - t_ref[...]
        dyf = e / float(Dm)
        per_row = jnp.mean(e * e, axis=-1, keepdims=True)
        l_ref[...] += 0.5 * jnp.sum(per_row, axis=0, keepdims=True)
        dxh = dyf * g_ref[...]
        c1 = jnp.mean(dxh, axis=-1, keepdims=True)
        c2 = jnp.mean(dxh * xhat, axis=-1, keepdims=True)
        dr = rstd * (dxh - c1 - xhat * c2)
        dr_ref[...] = dr
        drb_ref[...] = dr.astype(BF16)
        dg_ref[...] += jnp.sum(dyf * xhat, axis=0, keepdims=True)
        db_ref[...] += jnp.sum(dyf, axis=0, keepdims=True)
        ds_ref[...] += jnp.sum(dr, axis=0, keepdims=True)

    row = pl.BlockSpec((tm, Dm), lambda i: (i, 0))
    vec = pl.BlockSpec((1, Dm), lambda i: (0, 0))
    acc = pl.BlockSpec((8, LANE), lambda i: (0, 0))
    vshape = jax.ShapeDtypeStruct((1, Dm), F32)
    return _pcall(kern, name=name,
                  out_shape=(jax.ShapeDtypeStruct((8, LANE), F32), jax.ShapeDtypeStruct((S, Dm), F32),
                             jax.ShapeDtypeStruct((S, Dm), BF16), vshape, vshape, vshape),
                  grid=(S // tm,), in_specs=[row, row, vec, vec], out_specs=(acc, row, row, vec, vec, vec),
                  dims=("arbitrary",), vmem_mb=56)(target, r, g.reshape(1, Dm), b.reshape(1, Dm))


def _rot_sum(t):
    return pltpu.roll(t, 32, 1) + pltpu.roll(t, 96, 1)


def _mla_qkv(proj, cos_t, sin_t, qg, kvg, wuq_t, wukv_t, *, name):
    S = proj.shape[0]
    tm = min(512, S)

    def kern(ql_ref, kvl_ref, kr_ref, cos_ref, sin_ref, qg_ref, kvg_ref, wuq_ref, wukv_ref,
             qc_ref, kc_ref, v_ref, vt_ref, qn_ref, kvn_ref):
        cosv = cos_ref[...]
        sinv = sin_ref[...]

        def rope(t):
            return t * cosv + _rot_sum(t) * sinv

        ql = ql_ref[...]
        qn = (ql * lax.rsqrt(jnp.mean(ql * ql, axis=-1, keepdims=True) + RMS_EPS) * qg_ref[...]).astype(BF16)
        kvl = kvl_ref[...]
        kvn = (kvl * lax.rsqrt(jnp.mean(kvl * kvl, axis=-1, keepdims=True) + RMS_EPS) * kvg_ref[...]).astype(BF16)
        qn_ref[...] = qn
        kvn_ref[...] = kvn
        q = lax.dot_general(qn, wuq_ref[...], NT, preferred_element_type=F32)
        kv = lax.dot_general(kvn, wukv_ref[...], NT, preferred_element_type=F32)
        kr = rope(kr_ref[...]).astype(BF16)
        for h in range(N_HEADS):
            c0 = 256 * h
            qc_ref[:, c0:c0 + 128] = q[:, c0:c0 + 128].astype(BF16)
            qc_ref[:, c0 + 128:c0 + 256] = rope(q[:, c0 + 128:c0 + 256]).astype(BF16)
            kc_ref[:, c0:c0 + 128] = kv[:, c0:c0 + 128].astype(BF16)
            kc_ref[:, c0 + 128:c0 + 256] = kr
            vh = kv[:, c0 + 128:c0 + 256]
            v_ref[:, 128 * h:128 * h + 128] = vh.astype(BF16)
            vt_ref[h] = jnp.transpose(vh).astype(BF16)

    def row(w, blk):
        return pl.BlockSpec((tm, w), lambda i: (i, blk))

    def full(shape):
        return pl.BlockSpec(shape, lambda i: (0,) * len(shape))

    t = min(TQ, S)
    per = t // tm
    vt_spec = pl.BlockSpec((N_HEADS, None, 128, tm), lambda i: (0, i // per, 0, i % per))
    outs = (jax.ShapeDtypeStruct((S, 2048), BF16), jax.ShapeDtypeStruct((S, 2048), BF16),
            jax.ShapeDtypeStruct((S, 1024), BF16), jax.ShapeDtypeStruct((N_HEADS, S // t, 128, t), BF16),
            jax.ShapeDtypeStruct((S, Q_LORA), BF16), jax.ShapeDtypeStruct((S, KV_LORA), BF16))
    return _pcall(kern, name=name, out_shape=outs, grid=(S // tm,),
                  in_specs=[row(512, 0), row(256, 2), row(128, 6), row(128, 0), row(128, 0),
                            full((1, Q_LORA)), full((1, KV_LORA)), full((2048, Q_LORA)), full((2048, KV_LORA))],
                  out_specs=(row(2048, 0), row(2048, 0), row(1024, 0), vt_spec, row(512, 0), row(256, 0)),
                  dims=("parallel",), vmem_mb=48)(
                      proj, proj, proj, cos_t, sin_t, qg.reshape(1, -1), kvg.reshape(1, -1), wuq_t, wukv_t)


def _mla_qkv_bwd(dqb, dkvb, dkr_heads, qn, kvn, proj, cos_t, sin_t, qg, kvg, wuq_t, wukv_t, *, name):
    S = proj.shape[0]
    tm = min(512, S)

    def kern(dqb_ref, dkvb_ref, dkrh_ref, qn_ref, kvn_ref, ql_ref, kvl_ref, cos_ref, sin_ref, qg_ref, kvg_ref,
             wuq_ref, wukv_ref, dml_ref, dqg_ref, dkvg_ref, dwuq_ref, dwukv_ref):
        @pl.when(pl.program_id(0) == 0)
        def _():
            dqg_ref[...] = jnp.zeros_like(dqg_ref)
            dkvg_ref[...] = jnp.zeros_like(dkvg_ref)
            dwuq_ref[...] = jnp.zeros_like(dwuq_ref)
            dwukv_ref[...] = jnp.zeros_like(dwukv_ref)

        dwuq_ref[...] += lax.dot_general(dqb_ref[...], qn_ref[...], TN, preferred_element_type=F32)
        dwukv_ref[...] += lax.dot_general(dkvb_ref[...], kvn_ref[...], TN, preferred_element_type=F32)

        cosv = cos_ref[...]
        sinv = sin_ref[...]

        def unrope(t):
            return t * cosv - _rot_sum(t) * sinv

        dkr = dkrh_ref[:, 0:128]
        for h in range(1, N_HEADS):
            dkr = dkr + dkrh_ref[:, 128 * h:128 * h + 128]

        def rms_bwd(x, g, dy):
            n = x.shape[-1]
            rs = lax.rsqrt(jnp.mean(x * x, axis=-1, keepdims=True) + RMS_EPS)
            dyg = dy * g
            dx = rs * dyg - x * (rs * rs * rs) * (jnp.sum(dyg * x, axis=-1, keepdims=True) / n)
            return dx, jnp.sum(dy * (x * rs), axis=0, keepdims=True)

        dqn = jnp.dot(dqb_ref[...], wuq_ref[...], preferred_element_type=F32)
        dql, dqg = rms_bwd(ql_ref[...], qg_ref[...], dqn)
        dqg_ref[...] += dqg
        dkvn = jnp.dot(dkvb_ref[...], wukv_ref[...], preferred_element_type=F32)
        dkvl, dkvg = rms_bwd(kvl_ref[...], kvg_ref[...], dkvn)
        dkvg_ref[...] += dkvg
        dml_ref[:, 0:512] = dql.astype(BF16)
        dml_ref[:, 512:768] = dkvl.astype(BF16)
        dml_ref[:, 768:896] = unrope(dkr).astype(BF16)
        dml_ref[:, 896:1024] = jnp.zeros((tm, 128), BF16)

    def row(w, blk):
        return pl.BlockSpec((tm, w), lambda i: (i, blk))

    def full(shape):
        return pl.BlockSpec(shape, lambda i: (0,) * len(shape))

    outs = (jax.ShapeDtypeStruct((S, W_MLA), BF16), jax.ShapeDtypeStruct((1, Q_LORA), F32),
            jax.ShapeDtypeStruct((1, KV_LORA), F32), jax.ShapeDtypeStruct((2048, Q_LORA), F32),
            jax.ShapeDtypeStruct((2048, KV_LORA), F32))
    return _pcall(kern, name=name, out_shape=outs, grid=(S // tm,),
                  in_specs=[row(2048, 0), row(2048, 0), row(1024, 0), row(512, 0), row(256, 0), row(512, 0),
                            row(256, 2), row(128, 0), row(128, 0), full((1, Q_LORA)), full((1, KV_LORA)),
                            full((2048, Q_LORA)), full((2048, KV_LORA))],
                  out_specs=(row(W_MLA, 0), full((1, Q_LORA)), full((1, KV_LORA)), full((2048, Q_LORA)),
                             full((2048, KV_LORA))),
                  dims=("arbitrary",), vmem_mb=56)(
                      dqb, dkvb, dkr_heads, qn, kvn, proj, proj, cos_t, sin_t, qg.reshape(1, -1), kvg.reshape(1, -1),
                      wuq_t, wukv_t)


def _flash_fwd(qc, kc, vt, *, name, comm=None):
    S = qc.shape[0]
    t = min(TQ, S)
    n = S // t

    def kern(q_ref, k_ref, vt_ref, o_ref, lse_ref, m_s, l_s, acc_s):
        qi = pl.program_id(1)
        m_s[...] = jnp.full_like(m_s, -jnp.inf)
        l_s[...] = jnp.zeros_like(l_s)
        acc_s[...] = jnp.zeros_like(acc_s)

        half = t // 2

        def scores(kb, q_lo=0, q_n=t, k_n=t):
            k0 = pl.multiple_of(kb * t, t)
            return lax.dot_general(k_ref[pl.ds(k0, k_n), :], q_ref[q_lo:q_lo + q_n, :], NT,
                                   preferred_element_type=F32)

        def update(kb, st, q_lo=0, diagonal=False):
            k_n, q_n = st.shape
            if diagonal:
                krow = lax.broadcasted_iota(jnp.int32, (k_n, q_n), 0)
                qcol = lax.broadcasted_iota(jnp.int32, (k_n, q_n), 1) + q_lo
                st = jnp.where(krow <= qcol, st, -jnp.inf)
            lanes = slice(q_lo, q_lo + q_n)
            m_prev = m_s[:, lanes]
            m_new = jnp.maximum(m_prev, jnp.max(st, axis=0, keepdims=True))
            a = jnp.exp2((m_prev - m_new) * SCALE_LOG2E)
            pt = jnp.exp2((st - m_new) * SCALE_LOG2E)
            l_s[:, lanes] = a * l_s[:, lanes] + jnp.sum(pt, axis=0, keepdims=True)
            acc_s[:, lanes] = a * acc_s[:, lanes] + jnp.dot(vt_ref[kb, :, 0:k_n], pt.astype(BF16),
                                                            preferred_element_type=F32)
            m_s[:, lanes] = m_new

        def group(kb, count, last_diagonal):
            whole = count - 1 if last_diagonal else count
            sts = [scores(kb + g) for g in range(whole)]
            if last_diagonal:
                kd = kb + count - 1
                s_lo, s_hi = scores(kd, 0, half, half), scores(kd, half, half, t)
            for g in range(whole):
                update(kb + g, sts[g])
            if last_diagonal:
                update(kd, s_lo, 0, True)
                update(kd, s_hi, half, True)

        def body(i, carry):
            group(FWD_GROUP * i, FWD_GROUP, False)
            return carry

        full = qi // FWD_GROUP
        lax.fori_loop(0, full, body, 0)
        for rem in range(FWD_GROUP):
            @pl.when(qi - FWD_GROUP * full == rem)
            def _():
                group(qi - rem, rem + 1, True)
        o_ref[...] = jnp.transpose(acc_s[...] / l_s[...])
        lse_ref[pl.ds(qi, 1), :] = m_s[...] * SCALE_LOG2E + jnp.log2(l_s[...])

    q_spec = pl.BlockSpec((t, 256), lambda h, qi: (qi, h))
    k_spec = pl.BlockSpec((S, 256), lambda h, qi: (0, h))
    vt_spec = pl.BlockSpec((None, n, 128, t), lambda h, qi: (h, 0, 0, 0))
    o_spec = pl.BlockSpec((t, 128), lambda h, qi: (qi, h))
    lse_spec = pl.BlockSpec((None, n, t), lambda h, qi: (h, 0, 0))
    return _pcall(kern, name=name,
                  out_shape=(jax.ShapeDtypeStruct((S, D_MLA), F32), jax.ShapeDtypeStruct((N_HEADS, n, t), F32)),
                  grid=(N_HEADS, n), in_specs=[q_spec, k_spec, vt_spec], out_specs=(o_spec, lse_spec),
                  scratch=[pltpu.VMEM((1, t), F32), pltpu.VMEM((1, t), F32), pltpu.VMEM((128, t), F32)],
                  dims=("parallel", "arbitrary"), vmem_mb=48, comm=comm)(qc, kc, vt)


def _flash_bwd(qc, kc, v, do, lse2, delta, cos_t, sin_t, *, name, comm=None):
    S = qc.shape[0]
    t = min(TQ, S)
    n = S // t

    def kern(q_ref, k_ref, v_ref, do_ref, lse_ref, dl_ref, cos_ref, sin_ref, dqb_ref, dkvb_ref, dkr_ref,
             dq_ref, dk_ref, dv_ref):
        ki = pl.program_id(1)

        @pl.when(ki == 0)
        def _():
            dq_ref[...] = jnp.zeros_like(dq_ref)

        dk_ref[...] = jnp.zeros_like(dk_ref)
        dv_ref[...] = jnp.zeros_like(dv_ref)

        half = t // 2

        def step(qb, q_lo=0, q_n=t, k_n=t, diagonal=False):
            q0 = pl.multiple_of(qb * t + q_lo, half)
            lanes = slice(q_lo, q_lo + q_n)
            kt = k_ref[0:k_n, :]
            qblk = q_ref[pl.ds(q0, q_n), :]
            dob = do_ref[pl.ds(q0, q_n), :].astype(BF16)
            st = lax.dot_general(kt, qblk, NT, preferred_element_type=F32)
            pt = jnp.exp2(st * SCALE_LOG2E - lse_ref[pl.ds(qb, 1), lanes])
            if diagonal:
                krow = lax.broadcasted_iota(jnp.int32, (k_n, q_n), 0)
                qcol = lax.broadcasted_iota(jnp.int32, (k_n, q_n), 1) + q_lo
                pt = jnp.where(krow <= qcol, pt, 0.0)
            dv_ref[0:k_n, :] += jnp.dot(pt.astype(BF16), dob, preferred_element_type=F32)
            dpt = lax.dot_general(v_ref[0:k_n, :], dob, NT, preferred_element_type=F32)
            dst = (pt * (dpt - dl_ref[pl.ds(qb, 1), lanes]) * SCALE).astype(BF16)
            dk_ref[0:k_n, :] += jnp.dot(dst, qblk, preferred_element_type=F32)
            dq_ref[pl.ds(q0, q_n), :] += lax.dot_general(dst, kt, TN, preferred_element_type=F32)

        step(ki, 0, half, half, True)
        step(ki, half, half, t, True)
        rest = n - 1 - ki
        full = rest // BWD_GROUP

        def body(i, carry):
            for g in range(BWD_GROUP):
                step(ki + 1 + BWD_GROUP * i + g)
            return carry

        lax.fori_loop(0, full, body, 0)
        for rem in range(1, BWD_GROUP):
            @pl.when(rest - BWD_GROUP * full == rem)
            def _():
                for g in range(rem):
                    step(n - rem + g)

        dkvb_ref[:, 0:128] = dk_ref[:, 0:128].astype(BF16)
        dkvb_ref[:, 128:256] = dv_ref[...].astype(BF16)
        dkr_ref[...] = dk_ref[:, 128:256]

        @pl.when(ki == n - 1)
        def _():
            dqb_ref[:, 0:128] = dq_ref[:, 0:128].astype(BF16)
            dqr = dq_ref[:, 128:256]
            dqb_ref[:, 128:256] = (dqr * cos_ref[...] - _rot_sum(dqr) * sin_ref[...]).astype(BF16)

    def whole(w):
        return pl.BlockSpec((S, w), lambda h, ki: (0, h))

    def krow(w):
        return pl.BlockSpec((t, w), lambda h, ki: (ki, h))

    stat = pl.BlockSpec((None, n, t), lambda h, ki: (h, 0, 0))
    table = pl.BlockSpec((S, 128), lambda h, ki: (0, 0))
    return _pcall(kern, name=name,
                  out_shape=(jax.ShapeDtypeStruct((S, 2048), BF16), jax.ShapeDtypeStruct((S, 2048), BF16),
                             jax.ShapeDtypeStruct((S, D_MLA), F32)),
                  grid=(N_HEADS, n),
                  in_specs=[whole(256), krow(256), krow(128), whole(128), stat, stat, table, table],
                  out_specs=(whole(256), krow(256), krow(128)),
                  scratch=[pltpu.VMEM((S, 256), F32), pltpu.VMEM((t, 256), F32), pltpu.VMEM((t, 128), F32)],
                  dims=("parallel", "arbitrary"), vmem_mb=56, comm=comm)(qc, kc, v, do, lse2, delta, cos_t, sin_t)


def _mixer_specs(S, tm):
    hb = tm // HALO
    last_hb = S // HALO - 1

    def main(w, blk):
        return pl.BlockSpec((tm, w), lambda i: (i, blk))

    def prev(w, blk):
        return pl.BlockSpec((HALO, w), lambda i: (jnp.maximum(i * hb - 1, 0), blk))

    def nxt(w, blk):
        return pl.BlockSpec((HALO, w), lambda i: (jnp.minimum((i + 1) * hb, last_hb), blk))

    def full(shape):
        return pl.BlockSpec(shape, lambda i: (0,) * len(shape))

    return main, prev, nxt, full


def _fill_halo(i, xp, xu, hp_ref, hch_ref, hcc_ref, pin_ref, ch_ref, cc_ref, tm):
    first = i == 0
    xp[0:HALO, :] = jnp.where(first, 0.0, hp_ref[...])
    xp[HALO:HALO + tm, :] = pin_ref[...]
    xu[0:HALO, :] = jnp.where(first, 0.0, hch_ref[...] * hcc_ref[...])
    xu[HALO:HALO + tm, :] = cc_ref[...] * ch_ref[...]


def _pooled(xp, g, t1, tm):
    w = POOL_WINDOWS[g]
    lanes = slice(128 * g, 128 * g + 128)
    x0 = xp[HALO:HALO + tm, lanes]
    acc = x0
    for k in range(1, w):
        acc = acc + xp[HALO - k:HALO - k + tm, lanes]
    return acc / jnp.minimum(t1, float(w)) - x0


def _conv_fwd(xu, cw_ref, tm):
    return (cw_ref[0:1, :] * xu[HALO - 2:HALO - 2 + tm, :] + cw_ref[1:2, :] * xu[HALO - 1:HALO - 1 + tm, :]
            + cw_ref[2:3, :] * xu[HALO:HALO + tm, :])


def _mixer_fwd(proj, o, wpool, ps, convw, *, name):
    S = proj.shape[0]
    tm = min(512, S)
    main, prev, _, full = _mixer_specs(S, tm)

    def kern(gm_ref, pin_ref, gp_ref, ch_ref, cb_ref, cc_ref, gc_ref, hp_ref, hch_ref, hcc_ref,
             o_ref, wp_ref, ps_ref, cw_ref, mix_ref, xp, xu):
        i = pl.program_id(0)
        _fill_halo(i, xp, xu, hp_ref, hch_ref, hcc_ref, pin_ref, ch_ref, cc_ref, tm)
        t1 = (i * tm + lax.broadcasted_iota(jnp.int32, (tm, 1), 0) + 1).astype(F32)
        for g in range(4):
            lanes = slice(128 * g, 128 * g + 128)
            pooled = _pooled(xp, g, t1, tm)
            z = jnp.dot(pooled.astype(BF16), wp_ref[g].astype(BF16), preferred_element_type=F32)
            gp = gp_ref[:, lanes]
            y = z * ps_ref[:, lanes] * (gp * _sigmoid(gp))
            mix_ref[:, 1024 + 128 * g:1024 + 128 * g + 128] = y.astype(BF16)
        gc = gc_ref[...]
        mix_ref[:, 1536:2048] = (cb_ref[...] * _conv_fwd(xu, cw_ref, tm) * (gc * _sigmoid(gc))).astype(BF16)
        gm = gm_ref[...]
        mix_ref[:, 0:1024] = (o_ref[...] * (gm * _sigmoid(gm))).astype(BF16)

    return _pcall(kern, name=name, out_shape=jax.ShapeDtypeStruct((S, 2048), BF16), grid=(S // tm,),
                  in_specs=[main(1024, 1), main(512, 4), main(512, 5), main(512, 6), main(512, 7), main(512, 8),
                            main(512, 9), prev(512, 4), prev(512, 6), prev(512, 8),
                            main(1024, 0), full((4, 128, 128)), full((1, 512)), full((3, 512))],
                  out_specs=main(2048, 0),
                  scratch=[pltpu.VMEM((tm + HALO, 512), F32), pltpu.VMEM((tm + HALO, 512), F32)],
                  dims=("parallel",), vmem_mb=48)(
                      proj, proj, proj, proj, proj, proj, proj, proj, proj, proj, o, wpool, ps.reshape(1, 512), convw)


def _mixer_bwd(dmix, proj, o, wpool, ps, convw, *, name):
    S = proj.shape[0]
    tm = min(256, S)
    n = S // tm
    t = min(TQ, S)
    per = t // tm
    main, prev, nxt, full = _mixer_specs(S, tm)

    def kern(dm_ref, dmn_ref, gm_ref, pin_ref, gp_ref, ch_ref, cb_ref, cc_ref, gc_ref,
             hp_ref, hch_ref, hcc_ref, gpn_ref, cbn_ref, gcn_ref, o_ref, wp_ref, ps_ref, cw_ref,
             d_ref, do_ref, dl_ref, dwp_ref, dps_ref, dcw_ref, xp, xu, ee, ed):
        i = pl.program_id(0)
        last = i == n - 1

        @pl.when(i == 0)
        def _():
            dwp_ref[...] = jnp.zeros_like(dwp_ref)
            dps_ref[...] = jnp.zeros_like(dps_ref)
            dcw_ref[...] = jnp.zeros_like(dcw_ref)

        _fill_halo(i, xp, xu, hp_ref, hch_ref, hcc_ref, pin_ref, ch_ref, cc_ref, tm)
        t1 = (i * tm + lax.broadcasted_iota(jnp.int32, (tm, 1), 0) + 1).astype(F32)
        t1n = ((i + 1) * tm + lax.broadcasted_iota(jnp.int32, (HALO, 1), 0) + 1).astype(F32)
        c_pin, c_gp, c_ch, c_cb, c_cc, c_gc = 1024, 1536, 2048, 2560, 3072, 3584

        for g in range(4):
            w = float(POOL_WINDOWS[g])
            lanes = slice(128 * g, 128 * g + 128)
            pooled = _pooled(xp, g, t1, tm)
            pb = pooled.astype(BF16)
            wp = wp_ref[g].astype(BF16)
            z = jnp.dot(pb, wp, preferred_element_type=F32)
            psl = ps_ref[:, lanes]
            sg, dsg = _silu_and_grad(gp_ref[:, lanes])
            dmp = dm_ref[:, 1024 + 128 * g:1024 + 128 * g + 128]
            dyp = dmp * sg
            d_ref[:, c_gp + 128 * g:c_gp + 128 * g + 128] = (dmp * (z * psl) * dsg).astype(BF16)
            dps_ref[:, lanes] += jnp.sum(dyp * z, axis=0, keepdims=True)
            dz = (dyp * psl).astype(BF16)
            dwp_ref[g] += lax.dot_general(pb, dz, TN, preferred_element_type=F32)
            dpl = lax.dot_general(dz, wp, NT, preferred_element_type=F32)
            ee[0:tm, lanes] = dpl / jnp.minimum(t1, w)
            gpn = gpn_ref[:, lanes]
            dzn = (dmn_ref[:, lanes] * (gpn * _sigmoid(gpn)) * psl).astype(BF16)
            dpn = lax.dot_general(dzn, wp, NT, preferred_element_type=F32)
            ee[tm:tm + HALO, lanes] = jnp.where(last, 0.0, dpn / jnp.minimum(t1n, w))
            acc = ee[0:tm, lanes]
            for k in range(1, POOL_WINDOWS[g]):
                acc = acc + ee[k:k + tm, lanes]
            d_ref[:, c_pin + 128 * g:c_pin + 128 * g + 128] = (acc - dpl).astype(BF16)

        yc = _conv_fwd(xu, cw_ref, tm)
        sgc, dsgc = _silu_and_grad(gc_ref[...])
        cb = cb_ref[...]
        dmc = dm_ref[:, 1536:2048]
        d_ref[:, c_gc:c_gc + 512] = (dmc * cb * yc * dsgc).astype(BF16)
        d_ref[:, c_cb:c_cb + 512] = (dmc * yc * sgc).astype(BF16)
        dyc = dmc * cb * sgc
        ed[0:tm, :] = dyc
        gcn = gcn_ref[...]
        ed[tm:tm + HALO, :] = jnp.where(last, 0.0, dmn_ref[:, 512:1024] * cbn_ref[...] * (gcn * _sigmoid(gcn)))
        dcw_ref[0:1, :] += jnp.sum(dyc * xu[HALO - 2:HALO - 2 + tm, :], axis=0, keepdims=True)
        dcw_ref[1:2, :] += jnp.sum(dyc * xu[HALO - 1:HALO - 1 + tm, :], axis=0, keepdims=True)
        dcw_ref[2:3, :] += jnp.sum(dyc * xu[HALO:HALO + tm, :], axis=0, keepdims=True)
        du = cw_ref[2:3, :] * dyc + cw_ref[1:2, :] * ed[1:1 + tm, :] + cw_ref[0:1, :] * ed[2:2 + tm, :]
        d_ref[:, c_cc:c_cc + 512] = (du * ch_ref[...]).astype(BF16)
        d_ref[:, c_ch:c_ch + 512] = (du * cc_ref[...]).astype(BF16)

        sgm, dsgm = _silu_and_grad(gm_ref[...])
        dmm = dm_ref[:, 0:1024]
        ov = o_ref[...]
        dov = dmm * sgm
        do_ref[...] = dov
        d_ref[:, 0:1024] = (dmm * ov * dsgm).astype(BF16)
        lane = lax.broadcasted_iota(jnp.int32, (tm, LANE), 1)
        dmat = jnp.zeros((tm, LANE), F32)
        for h in range(N_HEADS):
            hs = slice(128 * h, 128 * h + 128)
            dmat = jnp.where(lane == h, jnp.sum(dov[:, hs] * ov[:, hs], axis=1, keepdims=True), dmat)
        dmat_t = jnp.transpose(dmat)
        for part in range(per):
            @pl.when(i % per == part)
            def _():
                for h in range(N_HEADS):
                    dl_ref[h, pl.ds(i // per, 1), part * tm:(part + 1) * tm] = dmat_t[h:h + 1, :]

    outs = (jax.ShapeDtypeStruct((S, W_MIX), BF16), jax.ShapeDtypeStruct((S, 1024), F32),
            jax.ShapeDtypeStruct((N_HEADS, S // t, t), F32),
            jax.ShapeDtypeStruct((4, 128, 128), F32), jax.ShapeDtypeStruct((1, 512), F32),
            jax.ShapeDtypeStruct((3, 512), F32))
    scr = [pltpu.VMEM((tm + HALO, 512), F32) for _ in range(4)]
    return _pcall(kern, name=name, out_shape=outs, grid=(n,),
                  in_specs=[main(2048, 0), nxt(1024, 1),
                            main(1024, 1), main(512, 4), main(512, 5), main(512, 6), main(512, 7), main(512, 8),
                            main(512, 9), prev(512, 4), prev(512, 6), prev(512, 8),
                            nxt(512, 5), nxt(512, 7), nxt(512, 9),
                            main(1024, 0), full((4, 128, 128)), full((1, 512)), full((3, 512))],
                  out_specs=(main(W_MIX, 0), main(1024, 0), full((N_HEADS, S // t, t)), full((4, 128, 128)),
                             full((1, 512)), full((3, 512))),
                  scratch=scr, dims=("arbitrary",), vmem_mb=56)(
                      dmix, dmix, proj, proj, proj, proj, proj, proj, proj, proj, proj, proj, proj, proj, proj,
                      o, wpool, ps.reshape(1, 512), convw)


def _outproj_residual(mix, wout, h, bout, *, name):
    S, Dm = h.shape
    tm = min(512, S)

    def kern(mix_ref, w_ref, h_ref, bo_ref, r_ref):
        out = jnp.dot(mix_ref[...], w_ref[...], preferred_element_type=F32) + bo_ref[...]
        r_ref[...] = ALPHA * h_ref[...] + out

    row = pl.BlockSpec((tm, Dm), lambda i: (i, 0))
    vec = pl.BlockSpec((1, Dm), lambda i: (0, 0))
    wsp = pl.BlockSpec((Dm, Dm), lambda i: (0, 0), pipeline_mode=pl.Buffered(1))
    return _pcall(kern, name=name, out_shape=jax.ShapeDtypeStruct((S, Dm), F32), grid=(S // tm,),
                  in_specs=[row, wsp, row, vec], out_specs=row, dims=("parallel",), vmem_mb=56)(
                      mix, wout, h, bout.reshape(1, Dm))


def _outproj_ln(mix, wout, h, bout, g, b, *, name):
    S, Dm = h.shape
    tm = min(512, S)

    def kern(mix_ref, w_ref, h_ref, bo_ref, g_ref, b_ref, y_ref, yb_ref, r_ref):
        out = jnp.dot(mix_ref[...], w_ref[...], preferred_element_type=F32) + bo_ref[...]
        r = ALPHA * h_ref[...] + out
        r_ref[...] = r
        mu = jnp.mean(r, axis=-1, keepdims=True)
        xc = r - mu
        var = jnp.mean(xc * xc, axis=-1, keepdims=True)
        y = xc * lax.rsqrt(var + LN_EPS) * g_ref[...] + b_ref[...]
        y_ref[...] = y
        yb_ref[...] = y.astype(BF16)

    row = pl.BlockSpec((tm, Dm), lambda i: (i, 0))
    vec = pl.BlockSpec((1, Dm), lambda i: (0, 0))
    wsp = pl.BlockSpec((Dm, Dm), lambda i: (0, 0), pipeline_mode=pl.Buffered(1))
    sds = jax.ShapeDtypeStruct((S, Dm), F32)
    return _pcall(kern, name=name, out_shape=(sds, jax.ShapeDtypeStruct((S, Dm), BF16), sds), grid=(S // tm,),
                  in_specs=[row, wsp, row, vec, vec, vec], out_specs=(row, row, row), dims=("parallel",),
                  vmem_mb=56)(
                      mix, wout, h, bout.reshape(1, Dm), g.reshape(1, Dm), b.reshape(1, Dm))


def _adamw_math(w, g, m, v):
    m = ADAM_B1 * m + (1.0 - ADAM_B1) * g
    v = ADAM_B2 * v + (1.0 - ADAM_B2) * (g * g)
    m_hat = m / (1.0 - ADAM_B1 ** ADAM_STEP)
    v_hat = v / (1.0 - ADAM_B2 ** ADAM_STEP)
    delta = -ADAM_LR * (m_hat / (jnp.sqrt(v_hat) + ADAM_EPS) + ADAM_WD * w)
    return delta, m, v


def _row_tile(R, C):
    best = None
    for cand in range(8, R, 8):
        if R % cand == 0 and cand * C <= 256 * 1024:
            best = cand
    return best if best is not None else R


def _adamw(w, g, m, v, *, name):
    shape = w.shape
    C = shape[-1]
    R = 1
    for s in shape[:-1]:
        R *= s
    tr = _row_tile(R, C)

    def kern(w_ref, g_ref, m_ref, v_ref, d_ref, mo_ref, vo_ref):
        d, mn, vn = _adamw_math(w_ref[...], g_ref[...], m_ref[...], v_ref[...])
        d_ref[...] = d
        mo_ref[...] = mn
        vo_ref[...] = vn

    blk = pl.BlockSpec((tr, C), lambda i: (i, 0))
    sds = jax.ShapeDtypeStruct((R, C), F32)
    outs = _pcall(kern, name=name, out_shape=(sds, sds, sds), grid=(R // tr,), in_specs=[blk] * 4,
                  out_specs=(blk, blk, blk), dims=("parallel",), vmem_mb=48)(
                      w.reshape(R, C), g.reshape(R, C), m.reshape(R, C), v.reshape(R, C))
    return tuple(t.reshape(shape) for t in outs)


def _adamw_halves(w, m, v, halves, c_idx, *, name, comm=None):
    _, R, C = w.shape
    ch = C // 2
    tr = _row_tile(R, ch)
    nb = R // tr

    def kern(c_ref, w_ref, a0_ref, b0_ref, a1_ref, b1_ref, m_ref, v_ref, g_ref, d_ref, mo_ref, vo_ref):
        layer = pl.program_id(0) // nb
        mine = pl.program_id(1) == c_ref[0]
        g = jnp.where(layer == 0, jnp.where(mine, a0_ref[...], b0_ref[...]),
                      jnp.where(mine, a1_ref[...], b1_ref[...]))
        g_ref[...] = g
        d, mn, vn = _adamw_math(w_ref[...], g, m_ref[...], v_ref[...])
        d_ref[...] = d
        mo_ref[...] = mn
        vo_ref[...] = vn

    full = pl.BlockSpec((tr, ch), lambda i, hc: (i, hc))
    half = pl.BlockSpec((tr, ch), lambda i, hc: (i % nb, 0))
    sds = jax.ShapeDtypeStruct((2 * R, C), F32)
    (a0, b0), (a1, b1) = halves
    res = _pcall(kern, name=name, out_shape=(sds,) * 4, grid=(2 * nb, 2),
                 in_specs=[pl.BlockSpec(memory_space=pltpu.SMEM), full, half, half, half, half, full, full],
                 out_specs=(full,) * 4, dims=("parallel", "parallel"), vmem_mb=48, comm=comm)(
                     c_idx, w.reshape(2 * R, C), a0, b0, a1, b1, m.reshape(2 * R, C), v.reshape(2 * R, C))
    outs, landed = res if comm is not None else (res, None)
    outs = tuple(t.reshape(2, R, C) for t in outs)
    return outs if comm is None else (outs, landed)


def _packed_pieces(shape):
    if len(shape) == 4:
        return [((l * shape[1] + g) * 128, 128, (l, g)) for l in range(shape[0]) for g in range(shape[1])]
    per_row = shape[1] // LANE
    return [(a * per_row + j, 1, (slice(a, a + 1), slice(LANE * j, LANE * (j + 1))))
            for a in range(shape[0]) for j in range(per_row)]


def _small_sum_adamw(gathered, own, weights, *, name):
    R = gathered.shape[1]
    nw = len(weights)
    shapes = [w.shape for w, _, _ in weights]
    first_row, r0 = [], 0
    for shp in shapes:
        first_row.append(r0)
        n = 1
        for s in shp:
            n *= s
        r0 += n // LANE

    def kern(ga_ref, own_ref, *refs):
        ins, gsum_ref, outs = refs[:3 * nw], refs[3 * nw], refs[3 * nw + 1:]
        me = 4 * lax.axis_index("x") + 2 * lax.axis_index("y") + lax.axis_index("c")

        def block(k):
            other = ga_ref[jnp.where(me == k, (k + 1) % N_DEV, k)]
            return jnp.where(me == k, own_ref[...], other)

        g = block(0)
        for k in range(1, N_DEV):
            g = g + block(k)
        gsum_ref[...] = g
        for p, shp in enumerate(shapes):
            w_ref, m_ref, v_ref = ins[3 * p:3 * p + 3]
            g_out, d_out, m_out, v_out = outs[4 * p:4 * p + 4]
            for row, rows, idx in _packed_pieces(shp):
                gp = gsum_ref[first_row[p] + row:first_row[p] + row + rows, :]
                d, mn, vn = _adamw_math(w_ref[idx], gp, m_ref[idx], v_ref[idx])
                g_out[idx] = gp
                d_out[idx] = d
                m_out[idx] = mn
                v_out[idx] = vn

    out_shape = [jax.ShapeDtypeStruct((R, LANE), F32)]
    for shp in shapes:
        out_shape += [jax.ShapeDtypeStruct(shp, F32)] * 4
    flat = [a for wmv in weights for a in wmv]
    res = _pcall(kern, name=name, out_shape=tuple(out_shape), vmem_mb=48)(gathered, own, *flat)
    return res[0], [tuple(res[1 + 4 * p:5 + 4 * p]) for p in range(nw)]


def _pair_sums(grads, theirs, c_idx, *, name):
    n = len(grads)
    steps = 8
    tiles = [(g.shape[0] // steps, g.shape[1] // 2) for g in grads]

    def kern(c_ref, *refs):
        for a in range(n):
            refs[2 * n + a][...] = (refs[a][...] + refs[n + a][...]).astype(BF16)

    gs = pltpu.PrefetchScalarGridSpec(
        num_scalar_prefetch=1, grid=(steps,),
        in_specs=[pl.BlockSpec(tl, lambda i, c: (i, c[0])) for tl in tiles]
        + [pl.BlockSpec(tl, lambda i, c: (i, 0)) for tl in tiles],
        out_specs=tuple(pl.BlockSpec(tl, lambda i, c: (i, 0)) for tl in tiles))
    out_shape = tuple(jax.ShapeDtypeStruct((g.shape[0], g.shape[1] // 2), BF16) for g in grads)
    return pl.pallas_call(kern, name=name, out_shape=out_shape, grid_spec=gs,
                          compiler_params=pltpu.CompilerParams(dimension_semantics=("parallel",),
                                                               vmem_limit_bytes=48 << 20))(c_idx, *grads, *theirs)


WeightRows = collections.namedtuple("WeightRows", "full_rows own_rows cols pieces zero_rows")


def _w_in_piece_a(j):
    return jnp.where(j == 0, 0, 1232 * j + GAP)


def _w_in_piece_b(j):
    return jnp.where(j == 0, GAP_AT + GAP, 1232 * j + GAP_AT + GAP)


W_IN = WeightRows(NP, 1232, D_MODEL, ((0, GAP_AT, _w_in_piece_a), (GAP_AT, 1232 - GAP_AT, _w_in_piece_b)),
                  ((GAP_AT, GAP),))
W_OUT = WeightRows(2048, 512, D_MODEL, ((0, 512, lambda j: 512 * j),), ())
W_UQ = WeightRows(2048, 384, Q_LORA, ((0, 192, lambda j: 512 * j), (192, 192, lambda j: 512 * j + 256)),
                  tuple((256 * h + 192, 64) for h in range(N_HEADS)))
W_UKV = WeightRows(2048, 512, KV_LORA, ((0, 512, lambda j: 512 * j),), ())
W_CONV = WeightRows(64, 16, 256, ((0, 16, lambda j: 16 * j),), ())
SHARDED = (W_IN, W_OUT, W_UQ, W_UKV)
SHARDED_NAMES = ("w_in", "w_out", "w_uq", "w_ukv")
WEIGHT_ROWS = dict(zip(SHARDED_NAMES, SHARDED))


def _mesh_pos():
    x, y, c = lax.axis_index("x"), lax.axis_index("y"), lax.axis_index("c")
    return x, y, c


def _other_chips(x, y):
    return [(1 - x, y), (x, 1 - y), (1 - x, 1 - y)]


def _rows(start, n):
    return pl.ds(pl.multiple_of(start, 16), n)


def _half_cols(spec, c):
    ch = spec.cols // 2
    return pl.ds(pl.multiple_of(c * ch, LANE), ch)


def _allgather_script(specs, shards, zeros, layers):
    na = len(specs)
    zlist = [a for a in range(na) if zeros[a] is not None]
    n_layers = [shards[a].shape[0] if layers[a] is None else 1 for a in range(na)]
    plan_first, plan_own, plan_zero = [], [], []
    for a, spec in enumerate(specs):
        for p in range(len(spec.pieces)):
            plan_own.append((a, p))
            for k in range(3):
                plan_first.append((a, p, k))
        for z in range(len(spec.zero_rows)):
            for l in range(n_layers[a]):
                plan_zero.append((a, z, l))
    nf = len(plan_first)
    n_sems = 2 * nf + len(plan_own) + len(plan_zero)

    def copies(ins_all, outs, send_sems, recv_sems):
        ins = [ins_all[a] if layers[a] is None else ins_all[a].at[pl.ds(layers[a], 1)] for a in range(na)]
        zrefs = dict(zip(zlist, ins_all[na:]))
        x, y, c = _mesh_pos()
        j = 2 * x + y
        chips = _other_chips(x, y)
        sibling = (x, y, 1 - c)

        def remote(src, dst, sem, to):
            return pltpu.make_async_remote_copy(src_ref=src, dst_ref=dst, send_sem=send_sems.at[sem],
                                                recv_sem=recv_sems.at[sem], device_id=to, device_id_type=MESH)

        def block(a, p, chip, cols):
            _, n, dst = specs[a].pieces[p]
            return outs[a].at[:, _rows(dst(chip), n), cols]

        def first(i):
            a, p, k = plan_first[i]
            src0, n, _ = specs[a].pieces[p]
            cols = _half_cols(specs[a], c)
            return remote(ins[a].at[:, pl.ds(src0, n), cols], block(a, p, j, cols), i, (*chips[k], c))

        def landed(i, half):
            a, p, k = plan_first[i]
            return block(a, p, 2 * chips[k][0] + chips[k][1], _half_cols(specs[a], half))

        def arrival(i, half, sem):
            return remote(landed(i, half), landed(i, half), sem, sibling)

        def passed(i):
            return remote(landed(i, c), landed(i, c), nf + i, sibling)

        def own(i):
            a, p = plan_own[i]
            src0, n, _ = specs[a].pieces[p]
            return remote(ins[a].at[:, pl.ds(src0, n), :], block(a, p, j, slice(None)), 2 * nf + i, sibling)

        def zero(i):
            a, z, l = plan_zero[i]
            r0, n = specs[a].zero_rows[z]
            return remote(zrefs[a].at[pl.ds(0, n), :], outs[a].at[l, pl.ds(r0, n), :],
                          2 * nf + len(plan_own) + i, sibling)

        fixed = [own(i) for i in range(len(plan_own))] + [zero(i) for i in range(len(plan_zero))]
        return c, fixed, first, arrival, passed

    def start(ins, outs, send_sems, recv_sems):
        _, fixed, first, _, _ = copies(ins, outs, send_sems, recv_sems)
        for cp in fixed:
            cp.start()
        for i in range(nf):
            first(i).start()

    def finish(ins, outs, send_sems, recv_sems):
        c, fixed, first, arrival, passed = copies(ins, outs, send_sems, recv_sems)
        for i in range(nf):
            arrival(i, c, i).wait_recv()
            passed(i).start()
        for i in range(nf):
            arrival(i, 1 - c, nf + i).wait_recv()
        for cp in fixed:
            cp.wait()
        for i in range(nf):
            first(i).wait_send()
            passed(i).wait_send()

    out_shape = tuple(jax.ShapeDtypeStruct((n_layers[a], spec.full_rows, spec.cols), BF16)
                      for a, spec in enumerate(specs))
    args = tuple(shards) + tuple(zeros[a] for a in zlist)
    return CommScript(args, out_shape, n_sems, start, finish)


def _start_all_wait_all(args, out_shape, n_sems, make_copies):
    def start(ins, outs, send_sems, recv_sems):
        for cp in make_copies(ins, outs, send_sems, recv_sems):
            cp.start()

    def finish(ins, outs, send_sems, recv_sems):
        for cp in make_copies(ins, outs, send_sems, recv_sems):
            cp.wait()

    return CommScript(tuple(args), tuple(out_shape), n_sems, start, finish)


def _exchange_script(specs, grads):
    na = len(grads)

    def make_copies(ins, outs, send_sems, recv_sems):
        x, y, c = _mesh_pos()
        return [pltpu.make_async_remote_copy(
            src_ref=ins[a].at[:, _half_cols(specs[a], 1 - c)], dst_ref=outs[a], send_sem=send_sems.at[a],
            recv_sem=recv_sems.at[a], device_id=(x, y, 1 - c), device_id_type=MESH) for a in range(na)]

    out_shape = [jax.ShapeDtypeStruct((s.full_rows, s.cols // 2), F32) for s in specs]
    return _start_all_wait_all(grads, out_shape, na, make_copies)


def _scatter_script(specs, parts):
    na = len(parts)
    plan = [(a, p, k) for a in range(na) for p in range(len(specs[a].pieces)) for k in range(3)]

    def make_copies(ins, outs, send_sems, recv_sems):
        x, y, c = _mesh_pos()
        chips = _other_chips(x, y)
        copies = []
        for i, (a, p, k) in enumerate(plan):
            src0, n, dst = specs[a].pieces[p]
            pk = 2 * chips[k][0] + chips[k][1]
            copies.append(pltpu.make_async_remote_copy(
                src_ref=ins[a].at[_rows(dst(pk), n), :], dst_ref=outs[a].at[k, pl.ds(src0, n), :],
                send_sem=send_sems.at[i], recv_sem=recv_sems.at[i], device_id=(*chips[k], c), device_id_type=MESH))
        return copies

    out_shape = [jax.ShapeDtypeStruct((3, s.own_rows, s.cols // 2), BF16) for s in specs]
    return _start_all_wait_all(parts, out_shape, len(plan), make_copies)


def _chip_sums(specs, parts, recvs, *, name):
    n = len(specs)
    plan = [(a, p) for a in range(n) for p in range(len(specs[a].pieces))]

    def kern(*refs):
        recv_refs, part_refs, o_refs = refs[:n], refs[n:2 * n], refs[2 * n:3 * n]
        own_refs, sems = refs[3 * n:4 * n], refs[4 * n]
        j = 2 * lax.axis_index("x") + lax.axis_index("y")
        copies = []
        for i, (a, p) in enumerate(plan):
            src0, rows, dst = specs[a].pieces[p]
            copies.append(pltpu.make_async_copy(part_refs[a].at[_rows(dst(j), rows), :],
                                                own_refs[a].at[pl.ds(src0, rows), :], sems.at[i]))
        for cp in copies:
            cp.start()
        for cp in copies:
            cp.wait()
        for a in range(n):
            r = recv_refs[a]
            o_refs[a][...] = ((own_refs[a][...].astype(F32) + r[0].astype(F32)) + r[1].astype(F32)) \
                + r[2].astype(F32)

    vm = pl.BlockSpec(memory_space=pltpu.VMEM)
    shapes = [(s.own_rows, s.cols // 2) for s in specs]
    return _pcall(kern, name=name, out_shape=tuple(jax.ShapeDtypeStruct(shp, F32) for shp in shapes),
                  in_specs=[vm] * n + [HBM_SPEC] * n, out_specs=(vm,) * n,
                  scratch=[pltpu.VMEM(shp, BF16) for shp in shapes] + [pltpu.SemaphoreType.DMA((len(plan),))],
                  vmem_mb=56)(*recvs, *parts)


def _sibling_script(sums):
    na = len(sums)

    def make_copies(ins, outs, send_sems, recv_sems):
        x, y, c = _mesh_pos()
        return [pltpu.make_async_remote_copy(
            src_ref=ins[a], dst_ref=outs[a], send_sem=send_sems.at[a], recv_sem=recv_sems.at[a],
            device_id=(x, y, 1 - c), device_id_type=MESH) for a in range(na)]

    out_shape = [jax.ShapeDtypeStruct(t.shape, t.dtype) for t in sums]
    return _start_all_wait_all(sums, out_shape, na, make_copies)


class _SemWindow:
    def __init__(self, sems, offset):
        self._sems, self._offset = sems, offset

    @property
    def at(self):
        return self

    def __getitem__(self, i):
        return self._sems.at[i + self._offset]


def _merge_scripts(*scripts):
    a_off, o_off, s_off = [0], [0], [0]
    for s in scripts:
        a_off.append(a_off[-1] + len(s.args))
        o_off.append(o_off[-1] + len(s.out_shape))
        s_off.append(s_off[-1] + s.n_sems)

    def phase(which):
        def run(ins, outs, send_sems, recv_sems):
            for n, s in enumerate(scripts):
                getattr(s, which)(ins[a_off[n]:a_off[n + 1]], outs[o_off[n]:o_off[n + 1]],
                                  _SemWindow(send_sems, s_off[n]), _SemWindow(recv_sems, s_off[n]))
        return run

    return CommScript(sum((tuple(s.args) for s in scripts), ()), sum((tuple(s.out_shape) for s in scripts), ()),
                      s_off[-1], phase("start"), phase("finish"))


class _GradReducer:
    def __init__(self, layer, names, grads, c_idx):
        self.specs = tuple(WEIGHT_ROWS[nm] for nm in names)
        self.grads, self.c_idx = tuple(grads), c_idx
        self.names = [f"{nm}{layer}" for nm in names]

    def exchange(self):
        return _exchange_script(self.specs, self.grads)

    def scatter(self, theirs):
        self.parts = tuple(_pair_sums(self.grads, tuple(theirs), self.c_idx, name=f"pair_sums_{self.names[0]}"))
        return _scatter_script(self.specs, self.parts)

    def sibling(self, recv):
        self.sums = tuple(_chip_sums(self.specs, self.parts, tuple(recv), name=f"chip_sums_{self.names[0]}"))
        return _sibling_script(self.sums)

    def done(self, others):
        return list(zip(self.sums, others))


def _allgather_small_script(block):
    m_per, n = block.shape

    def copies(ins, outs, send_sems, recv_sems):
        (x_ref,), (out_ref,) = ins, outs
        x, y, c = _mesh_pos()
        me, sibling = (x, y, c), (x, y, 1 - c)
        chips = _other_chips(x, y)

        def rows(px, py, pc):
            return out_ref.at[4 * px + 2 * py + pc]

        def copy(k, blk, to, src=None):
            return pltpu.make_async_remote_copy(
                src_ref=rows(*blk) if src is None else src, dst_ref=rows(*blk), send_sem=send_sems.at[k],
                recv_sem=recv_sems.at[k], device_id=to, device_id_type=MESH)

        first = [copy(0, me, sibling, src=x_ref)]
        first += [copy(1 + k, me, (*chip, c), src=x_ref) for k, chip in enumerate(chips)]
        passed = [copy(4 + k, (*chip, c), sibling) for k, chip in enumerate(chips)]
        landed = [copy(1 + k, (*chip, c), me) for k, chip in enumerate(chips)]
        from_sibling = [copy(0, sibling, me)] + [copy(4 + k, (*chip, 1 - c), me) for k, chip in enumerate(chips)]
        return first, passed, landed, from_sibling

    def start(ins, outs, send_sems, recv_sems):
        first, _, _, _ = copies(ins, outs, send_sems, recv_sems)
        for cp in first:
            cp.start()

    def finish(ins, outs, send_sems, recv_sems):
        first, passed, landed, from_sibling = copies(ins, outs, send_sems, recv_sems)
        for k in range(3):
            landed[k].wait_recv()
            passed[k].start()
        for cp in from_sibling:
            cp.wait_recv()
        for cp in first + passed:
            cp.wait_send()

    return CommScript((block,), (jax.ShapeDtypeStruct((N_DEV, m_per, n), block.dtype),), 7, start, finish)


def _rope_tables(positions):
    half = ROPE // 2
    inv_freq = ROPE_THETA ** (-jnp.arange(half, dtype=F32) / half)
    ang = positions.astype(F32)[:, None] * inv_freq
    cos, sin = jnp.cos(ang), jnp.sin(ang)
    S = positions.shape[0]
    cos_t = jnp.concatenate([cos, cos, jnp.ones((S, 64), F32)], axis=1)
    sin_t = jnp.concatenate([-sin, sin, jnp.zeros((S, 64), F32)], axis=1)
    return cos_t, sin_t


def _decode_conv(bits):
    rows = bits.reshape(DEPTH, N_CHIPS, 16, 256)[:, :, :3, :]
    conv = lax.bitcast_convert_type(rows.reshape(DEPTH, N_CHIPS, 3, 128, 2), F32)
    return jnp.transpose(conv, (0, 2, 1, 3)).reshape(DEPTH, 3, 512)


def _local_step(x, positions, target, emb_g, emb_b, w_in_t0, rest0, weights1, q_g, kv_g, w_pool, pool_scale,
                b_out, ln_g, ln_b, c_idx=None):
    cos_t, sin_t = _rope_tables(positions)
    if isinstance(w_in_t0, CommScript):
        (h, hb), (landed,) = _ln_fwd(x, emb_g, emb_b, name="emb_ln", comm=w_in_t0)
        w_in_t0 = landed[0]
    else:
        h, hb = _ln_fwd(x, emb_g, emb_b, name="emb_ln")
    weights = [None, weights1]
    saved = []
    for l in range(DEPTH):
        if l == 0 and isinstance(rest0, CommScript):
            proj, landed = _matmul(hb, w_in_t0, "nt", name="in_proj0", tm=1024, tn=1024, tk=2048, vmem_mb=56,
                                   comm=rest0)
            weights[0] = (w_in_t0,) + tuple(a[0] for a in landed[:3])
            conv_w = _decode_conv(landed[3])
        else:
            if l == 0:
                weights[0] = (w_in_t0,) + tuple(rest0[:3])
                conv_w = rest0[3]
            proj = _matmul(hb, weights[l][0], "nt", name=f"in_proj{l}", tm=1024, tn=1024, tk=2048, vmem_mb=56)
        w_in_t, w_out, w_uq_t, w_ukv_t = weights[l]
        qc, kc, v, vt, qn, kvn = _mla_qkv(proj, cos_t, sin_t, q_g[l], kv_g[l], w_uq_t, w_ukv_t, name=f"mla_qkv{l}")
        nxt = weights[l + 1] if l + 1 < DEPTH else None
        if isinstance(nxt, CommScript):
            (o, lse2), landed = _flash_fwd(qc, kc, vt, name=f"flash_fwd{l}", comm=nxt)
            weights[l + 1] = tuple(a[0] for a in landed)
        else:
            o, lse2 = _flash_fwd(qc, kc, vt, name=f"flash_fwd{l}")
        mix = _mixer_fwd(proj, o, w_pool[l], pool_scale[l], conv_w[l], name=f"mixer_fwd{l}")
        if l == DEPTH - 1:
            r = _outproj_residual(mix, w_out, h, b_out[l], name=f"out_proj{l}")
            saved.append((hb, proj, qc, kc, v, qn, kvn, o, lse2, mix, r))
        else:
            h_next, hb_next, r = _outproj_ln(mix, w_out, h, b_out[l], ln_g[l], ln_b[l], name=f"out_proj_ln{l}")
            saved.append((hb, proj, qc, kc, v, qn, kvn, o, lse2, mix, r))
            h, hb = h_next, hb_next

    small = [None] * DEPTH
    big = [None] * DEPTH
    above = scatter_above = None
    for l in reversed(range(DEPTH)):
        w_in_t, w_out, w_uq_t, w_ukv_t = weights[l]
        hb_in, proj, qc, kc, v, qn, kvn, o, lse2, mix, r = saved[l]
        if l == DEPTH - 1:
            loss_acc, dr, drb, d_ln_g, d_ln_b, d_b_out = _loss_ln_bwd(target, r, ln_g[l], ln_b[l], name="loss_ln_bwd")
        else:
            dr, drb, d_ln_g, d_ln_b, d_b_out = _ln_bwd(dh, r, ln_g[l], name=f"ln_bwd{l}")
        dmix = _matmul(drb, w_out, "nt", name=f"dmix{l}", tm=1024, tn=1024, tk=2048, vmem_mb=56)
        d_w_out = _matmul(mix, drb, "tn", name=f"dw_out{l}", tm=1024, tn=1024, tk=2048, vmem_mb=56)
        d_mix, do, delta, d_w_pool, d_ps, d_conv = _mixer_bwd(dmix, proj, o, w_pool[l], pool_scale[l], conv_w[l],
                                                              name=f"mixer_bwd{l}")
        if above is not None:
            (dqb, dkvb, dkr), recv = _flash_bwd(qc, kc, v, do, lse2, delta, cos_t, sin_t, name=f"flash_bwd{l}",
                                                comm=scatter_above)
            sibling_above = above.sibling(recv)
        else:
            dqb, dkvb, dkr = _flash_bwd(qc, kc, v, do, lse2, delta, cos_t, sin_t, name=f"flash_bwd{l}")
        d_mla, d_qg, d_kvg, d_w_uq_t, d_w_ukv_t = _mla_qkv_bwd(
            dqb, dkvb, dkr, qn, kvn, proj, cos_t, sin_t, q_g[l], kv_g[l], w_uq_t, w_ukv_t, name=f"mla_qkv_bwd{l}")
        small[l] = dict(q_g=d_qg[0], kv_g=d_kvg[0], w_pool=d_w_pool, pool_scale=d_ps[0], conv_w=d_conv,
                        b_out=d_b_out[0], ln_g=d_ln_g[0], ln_b=d_ln_b[0])
        rest = (d_w_out, d_w_uq_t, d_w_ukv_t)
        if c_idx is None:
            d_w_in_t = _dproj_t_times_h(d_mla, d_mix, hb_in, name=f"dw_in{l}")
            dh = _dproj_times_w(d_mla, d_mix, w_in_t, dr, ALPHA, name=f"dh{l}")
            big[l] = (d_w_in_t,) + rest
        elif l > 0:
            d_w_in_t = _dproj_t_times_h(d_mla, d_mix, hb_in, name=f"dw_in{l}")
            above = _GradReducer(l, SHARDED_NAMES, (d_w_in_t,) + rest, c_idx)
            dh, theirs = _dproj_times_w(d_mla, d_mix, w_in_t, dr, ALPHA, name=f"dh{l}", comm=above.exchange())
            scatter_above = above.scatter(theirs)
        else:
            red_rest = _GradReducer(l, SHARDED_NAMES[1:], rest, c_idx)
            d_w_in_t, landed = _dproj_t_times_h(d_mla, d_mix, hb_in, name=f"dw_in{l}",
                                                comm=_merge_scripts(sibling_above, red_rest.exchange()))
            big[l + 1] = above.done(landed[:len(SHARDED)])
            red_in = _GradReducer(l, SHARDED_NAMES[:1], (d_w_in_t,), c_idx)
            landed = _run_comm(_merge_scripts(red_in.exchange(), red_rest.scatter(landed[len(SHARDED):])),
                               name="exchange_w_in0")
            sibling_rest = red_rest.sibling(landed[1:])
            dh, landed = _dproj_times_w(d_mla, d_mix, w_in_t, dr, ALPHA, name=f"dh{l}",
                                        comm=_merge_scripts(red_in.scatter(landed[:1]), sibling_rest))
            recv_in, others_rest = landed[:1], landed[1:]
    grad_x, _, d_emb_g, d_emb_b, _ = _ln_bwd(dh, x, emb_g, name="emb_ln_bwd", bf16_copy=False)
    if c_idx is not None:
        others_in = _run_comm(red_in.sibling(recv_in), name="send_to_sibling0")
        big[0] = red_in.done(others_in) + red_rest.done(others_rest)
    return loss_acc[0, 0], grad_x, d_emb_g, d_emb_b, small, big


SMALL_ORDER = ("emb_ln_g", "emb_ln_b", "q_norm_g", "kv_norm_g", "w_pool", "pool_scale", "b_out", "ln_g", "ln_b")
SMALL_LAYER_KEYS = ("q_g", "kv_g", "w_pool", "pool_scale", "b_out", "ln_g", "ln_b", "conv_w")


def _pack_small(arrs, extra_rows):
    flat = jnp.concatenate([a.reshape(-1) for a in arrs])
    rows = flat.shape[0] // LANE
    total = -(-(rows + extra_rows) // 8) * 8
    return jnp.pad(flat, (0, total * LANE - flat.shape[0])).reshape(total, LANE)


def kernel(x, positions, emb_ln_g, emb_ln_b, w_in, q_norm_g, kv_norm_g, w_uq, w_ukv, w_pool, pool_scale, conv_w, w_out, b_out, ln_g, ln_b, loss_target, m_emb_ln_g, m_emb_ln_b, m_w_in, m_q_norm_g, m_kv_norm_g, m_w_uq, m_w_ukv, m_w_pool, m_pool_scale, m_conv_w, m_w_out, m_b_out, m_ln_g, m_ln_b, v_emb_ln_g, v_emb_ln_b, v_w_in, v_q_norm_g, v_kv_norm_g, v_w_uq, v_w_ukv, v_w_pool, v_pool_scale, v_conv_w, v_w_out, v_b_out, v_ln_g, v_ln_b):
    xi, yi, ci = lax.axis_index("x"), lax.axis_index("y"), lax.axis_index("c")
    chip = 2 * xi + yi
    c_idx = ci.reshape(1).astype(jnp.int32)

    def t(a):
        return jnp.swapaxes(a, 1, 2)

    conv_bits = lax.bitcast_convert_type(conv_w.reshape(DEPTH, 3 * 128), BF16).reshape(DEPTH, 3, 256)
    conv_bits = jnp.pad(conv_bits, ((0, 0), (0, 13), (0, 0)))
    own = (t(w_in).astype(BF16), w_out.astype(BF16), t(w_uq).astype(BF16), t(w_ukv).astype(BF16))
    zeros = (jnp.zeros((GAP, D_MODEL), BF16), None, jnp.zeros((64, Q_LORA), BF16), None)
    gather_in0 = _allgather_script((W_IN,), own[:1], zeros[:1], (0,))
    gather0 = _allgather_script(SHARDED[1:] + (W_CONV,), own[1:] + (conv_bits,), zeros[1:] + (None,),
                                (0, 0, 0, None))
    gather1 = _allgather_script(SHARDED, own, zeros, (1, 1, 1, 1))

    loss_part, grad_x, d_emb_g, d_emb_b, grads, reduced = _local_step(
        x[0], positions[0], loss_target[0], emb_ln_g, emb_ln_b, gather_in0, gather0, gather1, q_norm_g, kv_norm_g,
        w_pool, pool_scale, b_out, ln_g, ln_b, c_idx)

    def rows(a):
        return a.reshape(1, -1) if a.ndim == 1 else a

    small_wmv = [tuple(rows(a) for a in wmv) for wmv in (
        (emb_ln_g, m_emb_ln_g, v_emb_ln_g), (emb_ln_b, m_emb_ln_b, v_emb_ln_b),
        (q_norm_g, m_q_norm_g, v_q_norm_g), (kv_norm_g, m_kv_norm_g, v_kv_norm_g), (w_pool, m_w_pool, v_w_pool),
        (pool_scale, m_pool_scale, v_pool_scale), (b_out, m_b_out, v_b_out), (ln_g, m_ln_g, v_ln_g),
        (ln_b, m_ln_b, v_ln_b))]
    packed_g = _pack_small(
        [d_emb_g, d_emb_b] + [jnp.stack([grads[l][key] for l in range(DEPTH)]) for key in SMALL_LAYER_KEYS]
        + [jnp.pad(loss_part.reshape(1), (0, LANE - 1))], 0)
    (gathered,) = _run_comm(_allgather_small_script(packed_g), name="allgather_small")
    g_tot, small_upd = _small_sum_adamw(gathered, packed_g, small_wmv, name="small_sum_adamw")
    off = sum(w.size for w, _, _ in small_wmv)
    flat_tot = g_tot.reshape(-1)

    def halves(a):
        return [reduced[l][a] for l in range(DEPTH)]

    upd = {}
    upd["w_in"] = tuple(t(o) for o in _adamw_halves(t(w_in), t(m_w_in), t(v_w_in), halves(0), c_idx,
                                                    name="adamw_w_in"))
    conv_tot = flat_tot[off:off + DEPTH * 3 * 512].reshape(DEPTH, 3, 512)
    loss = flat_tot[off + DEPTH * 3 * 512]
    g_conv = lax.dynamic_slice_in_dim(conv_tot, chip * 128, 128, axis=2)

    def whole(a):
        return jnp.stack([jnp.where(ci == 0, jnp.concatenate([mine, oth], axis=1),
                                    jnp.concatenate([oth, mine], axis=1)) for mine, oth in halves(a)])

    upd["w_out"] = _adamw_halves(w_out, m_w_out, v_w_out, halves(1), c_idx, name="adamw_w_out")
    g_uq, g_ukv = t(whole(2)), t(whole(3))
    upd["w_uq"] = (g_uq,) + _adamw(w_uq, g_uq, m_w_uq, v_w_uq, name="adamw_w_uq")
    upd["w_ukv"] = (g_ukv,) + _adamw(w_ukv, g_ukv, m_w_ukv, v_w_ukv, name="adamw_w_ukv")
    upd["conv_w"] = (g_conv,) + _adamw(conv_w, g_conv, m_conv_w, v_conv_w, name="adamw_conv_w")
    for nm, res in zip(SMALL_ORDER, small_upd):
        upd[nm] = tuple(a.reshape(-1) for a in res) if nm in ("emb_ln_g", "emb_ln_b") else res

    order = ("emb_ln_g", "emb_ln_b", "w_in", "q_norm_g", "kv_norm_g", "w_uq", "w_ukv", "w_pool", "pool_scale",
             "conv_w", "w_out", "b_out", "ln_g", "ln_b")
    outs = [loss, grad_x[None]]
    for field in range(4):
        outs += [upd[nm][field] for nm in order]
    return tuple(outs)
```

```python
import collections

import jax
import jax.numpy as jnp
from jax import lax
from jax.experimental import pallas as pl
from jax.experimental.pallas import tpu as pltpu

F32 = jnp.float32
BF16 = jnp.bfloat16
MESH = pl.DeviceIdType.MESH

D_MODEL = 2048
DEPTH = 2
N_HEADS = 8
NOPE = 128
ROPE = 64
Q_LORA = 512
KV_LORA = 256
D_MLA = 1024
POOL_WINDOWS = (2, 4, 8, 16)
D_IN_PROJ = 4928
LN_EPS = 1e-5
RMS_EPS = 1e-6
ROPE_THETA = 10000.0
ALPHA = (2 * DEPTH) ** 0.25
SCALE = (NOPE + ROPE) ** -0.5
LOG2E = 1.4426950408889634
SCALE_LOG2E = SCALE * LOG2E
ADAM_LR = 0.001
ADAM_B1 = 0.9
ADAM_B2 = 0.999
ADAM_EPS = 1e-08
ADAM_WD = 0.01
ADAM_STEP = 10

NP = 5120
GAP_AT = 832
GAP = NP - D_IN_PROJ
W_MLA = 1024
W_MIX = NP - W_MLA
HALO = 16
LANE = 128
N_CHIPS = 4
N_DEV = 8
TQ = 512
FWD_GROUP = 4
BWD_GROUP = 3

NN = (((1,), (0,)), ((), ()))
NT = (((1,), (1,)), ((), ()))
TN = (((0,), (0,)), ((), ()))


CommScript = collections.namedtuple("CommScript", "args out_shape n_sems start finish")
HBM_SPEC = pl.BlockSpec(memory_space=pl.ANY)


def _pcall(kern, *, name, out_shape, grid=None, in_specs=None, out_specs=None, scratch=(), dims=None,
           vmem_mb=None, comm=None):
    cp = {}
    if dims is not None:
        cp["dimension_semantics"] = dims if comm is None else ("arbitrary",) * len(dims)
    if vmem_mb is not None:
        cp["vmem_limit_bytes"] = vmem_mb << 20
    if comm is None:
        args = dict(name=name, out_shape=out_shape, scratch_shapes=list(scratch),
                    compiler_params=pltpu.CompilerParams(**cp))
        if grid is not None:
            args["grid"] = grid
        if in_specs is not None:
            args["in_specs"] = in_specs
        if out_specs is not None:
            args["out_specs"] = out_specs
        return pl.pallas_call(kern, **args)

    single = not isinstance(out_shape, (tuple, list))
    own_out = (out_shape,) if single else tuple(out_shape)
    own_out_specs = (out_specs,) if single else tuple(out_specs)
    n_in, n_out, n_scr = len(in_specs), len(own_out), len(scratch)
    na, no = len(comm.args), len(comm.out_shape)

    def at(end):
        cond = None
        for d, n in enumerate(grid):
            here = pl.program_id(d) == (n - 1 if end else 0)
            cond = here if cond is None else jnp.logical_and(cond, here)
        return cond

    def wrapped(*refs):
        own_in, c_in = refs[:n_in], refs[n_in:n_in + na]
        o0 = n_in + na
        own_o, c_out = refs[o0:o0 + n_out], refs[o0 + n_out:o0 + n_out + no]
        s0 = o0 + n_out + no
        own_s, (send_sems, recv_sems) = refs[s0:s0 + n_scr], refs[s0 + n_scr:]

        @pl.when(at(False))
        def _():
            comm.start(c_in, c_out, send_sems, recv_sems)

        kern(*own_in, *own_o, *own_s)

        @pl.when(at(True))
        def _():
            comm.finish(c_in, c_out, send_sems, recv_sems)

    call = pl.pallas_call(
        wrapped, name=name, out_shape=own_out + tuple(comm.out_shape), grid=grid,
        in_specs=list(in_specs) + [HBM_SPEC] * na, out_specs=own_out_specs + (HBM_SPEC,) * no,
        scratch_shapes=list(scratch) + [pltpu.SemaphoreType.DMA((comm.n_sems,)),
                                        pltpu.SemaphoreType.DMA((comm.n_sems,))],
        compiler_params=pltpu.CompilerParams(**cp))

    def run(*args):
        res = call(*args, *comm.args)
        own = res[0] if single else tuple(res[:n_out])
        return own, tuple(res[n_out:])

    return run


def _run_comm(script, *, name):
    na, no = len(script.args), len(script.out_shape)

    def body(*refs):
        ins, outs = refs[:na], refs[na:na + no]
        send_sems, recv_sems = refs[na + no:]
        script.start(ins, outs, send_sems, recv_sems)
        script.finish(ins, outs, send_sems, recv_sems)

    return pl.pallas_call(
        body, name=name, out_shape=tuple(script.out_shape), in_specs=[HBM_SPEC] * na, out_specs=(HBM_SPEC,) * no,
        scratch_shapes=[pltpu.SemaphoreType.DMA((script.n_sems,)), pltpu.SemaphoreType.DMA((script.n_sems,))])(
            *script.args)


def _sigmoid(g):
    return 1.0 / (1.0 + jnp.exp(-g))


def _silu_and_grad(g):
    sig = _sigmoid(g)
    return g * sig, sig * (1.0 + g * (1.0 - sig))


def _matmul(a, b, mode, *, name, tm, tn, tk, out_dtype=F32, vmem_mb=48, comm=None):
    if mode == "nn":
        (M, K), N = a.shape, b.shape[1]
    elif mode == "nt":
        (M, K), N = a.shape, b.shape[0]
    else:
        (K, M), N = a.shape, b.shape[1]
    tm, tn, tk = min(tm, M), min(tn, N), min(tk, K)
    assert M % tm == 0 and N % tn == 0 and K % tk == 0, (name, M, N, K)
    nk = K // tk
    dn = {"nn": NN, "nt": NT, "tn": TN}[mode]
    if mode == "tn":
        a_spec = pl.BlockSpec((tk, tm), lambda i, j, k: (k, i))
    else:
        a_spec = pl.BlockSpec((tm, tk), lambda i, j, k: (i, k))
    if mode == "nt":
        b_spec = pl.BlockSpec((tn, tk), lambda i, j, k: (j, k))
    else:
        b_spec = pl.BlockSpec((tk, tn), lambda i, j, k: (k, j))
    o_spec = pl.BlockSpec((tm, tn), lambda i, j, k: (i, j))

    def kern(a_ref, b_ref, o_ref, *rest):
        part = lax.dot_general(a_ref[...].astype(BF16), b_ref[...].astype(BF16), dn,
                               preferred_element_type=F32)
        if nk == 1:
            o_ref[...] = part.astype(out_dtype)
        else:
            acc_ref = rest[0]
            k = pl.program_id(2)

            @pl.when(k == 0)
            def _():
                acc_ref[...] = part

            @pl.when(k > 0)
            def _():
                acc_ref[...] += part

            @pl.when(k == nk - 1)
            def _():
                o_ref[...] = acc_ref[...].astype(out_dtype)

    scratch = [pltpu.VMEM((tm, tn), F32)] if nk > 1 else []
    return _pcall(kern, name=name, out_shape=jax.ShapeDtypeStruct((M, N), out_dtype),
                  grid=(M // tm, N // tn, nk), in_specs=[a_spec, b_spec], out_specs=o_spec, scratch=scratch,
                  dims=("parallel", "parallel", "arbitrary"), vmem_mb=vmem_mb, comm=comm)(a, b)


def _dproj_times_w(d_mla, d_mix, wt, add, add_scale, *, name, comm=None):
    S = d_mla.shape[0]
    Dm = wt.shape[1]
    tm, tn, tk = min(1024, S), 1024, 2048
    nk = 1 + W_MIX // tk

    def kern(a1_ref, a2_ref, b1_ref, b2_ref, add_ref, o_ref, acc_ref):
        k = pl.program_id(2)

        @pl.when(k == 0)
        def _():
            acc_ref[...] = jnp.dot(a1_ref[...], b1_ref[...], preferred_element_type=F32)

        @pl.when(k > 0)
        def _():
            acc_ref[...] += jnp.dot(a2_ref[...], b2_ref[...], preferred_element_type=F32)

        @pl.when(k == nk - 1)
        def _():
            o_ref[...] = add_scale * add_ref[...] + acc_ref[...]

    o_spec = pl.BlockSpec((tm, tn), lambda i, j, k: (i, j))
    b2_spec = pl.BlockSpec((pl.Element(tk), pl.Element(tn)),
                           lambda i, j, k: (pl.multiple_of(W_MLA + tk * jnp.maximum(k - 1, 0), W_MLA),
                                            pl.multiple_of(j * tn, tn)))
    return _pcall(kern, name=name, out_shape=jax.ShapeDtypeStruct((S, Dm), F32), grid=(S // tm, Dm // tn, nk),
                  in_specs=[pl.BlockSpec((tm, W_MLA), lambda i, j, k: (i, 0)),
                            pl.BlockSpec((tm, tk), lambda i, j, k: (i, jnp.maximum(k - 1, 0))),
                            pl.BlockSpec((W_MLA, tn), lambda i, j, k: (0, j)), b2_spec, o_spec],
                  out_specs=o_spec, scratch=[pltpu.VMEM((tm, tn), F32)],
                  dims=("parallel", "parallel", "arbitrary"), vmem_mb=56, comm=comm)(d_mla, d_mix, wt, wt, add)


def _dproj_t_times_h(d_mla, d_mix, h, *, name, comm=None):
    S, Dm = h.shape
    tm, tn, tk = W_MLA, 1024, min(2048, S)
    nk = S // tk

    def kern(a1_ref, a2_ref, b_ref, o_ref, acc_ref):
        i = pl.program_id(0)
        k = pl.program_id(2)
        b = b_ref[...].astype(BF16)

        def accumulate(part):
            @pl.when(k == 0)
            def _():
                acc_ref[...] = part

            @pl.when(k > 0)
            def _():
                acc_ref[...] += part

        @pl.when(i == 0)
        def _():
            accumulate(lax.dot_general(a1_ref[...], b, TN, preferred_element_type=F32))

        @pl.when(i > 0)
        def _():
            accumulate(lax.dot_general(a2_ref[...], b, TN, preferred_element_type=F32))

        @pl.when(k == nk - 1)
        def _():
            o_ref[...] = acc_ref[...]

    return _pcall(kern, name=name, out_shape=jax.ShapeDtypeStruct((NP, Dm), F32), grid=(NP // tm, Dm // tn, nk),
                  in_specs=[pl.BlockSpec((tk, tm), lambda i, j, k: (jnp.where(i == 0, k, nk - 1), 0)),
                            pl.BlockSpec((tk, tm), lambda i, j, k: (jnp.where(i == 0, 0, k), jnp.maximum(i - 1, 0))),
                            pl.BlockSpec((tk, tn), lambda i, j, k: (k, j))],
                  out_specs=pl.BlockSpec((tm, tn), lambda i, j, k: (i, j)), scratch=[pltpu.VMEM((tm, tn), F32)],
                  dims=("parallel", "parallel", "arbitrary"), vmem_mb=48, comm=comm)(d_mla, d_mix, h)


def _ln_fwd(x, g, b, *, name, comm=None):
    S, Dm = x.shape
    tm = min(512, S)

    def kern(x_ref, g_ref, b_ref, y_ref, yb_ref):
        xf = x_ref[...]
        mu = jnp.mean(xf, axis=-1, keepdims=True)
        xc = xf - mu
        var = jnp.mean(xc * xc, axis=-1, keepdims=True)
        y = xc * lax.rsqrt(var + LN_EPS) * g_ref[...] + b_ref[...]
        y_ref[...] = y
        yb_ref[...] = y.astype(BF16)

    row = pl.BlockSpec((tm, Dm), lambda i: (i, 0))
    vec = pl.BlockSpec((1, Dm), lambda i: (0, 0))
    return _pcall(kern, name=name,
                  out_shape=(jax.ShapeDtypeStruct((S, Dm), F32), jax.ShapeDtypeStruct((S, Dm), BF16)),
                  grid=(S // tm,), in_specs=[row, vec, vec], out_specs=(row, row), dims=("parallel",), vmem_mb=48,
                  comm=comm)(
                      x, g.reshape(1, Dm), b.reshape(1, Dm))


def _ln_bwd(dy, r, g, *, name, bf16_copy=True):
    S, Dm = r.shape
    tm = min(512, S)

    def kern(dy_ref, r_ref, g_ref, dr_ref, *rest):
        drb_ref = rest[0] if bf16_copy else None
        dg_ref, db_ref, ds_ref = rest[-3:]

        @pl.when(pl.program_id(0) == 0)
        def _():
            dg_ref[...] = jnp.zeros_like(dg_ref)
            db_ref[...] = jnp.zeros_like(db_ref)
            ds_ref[...] = jnp.zeros_like(ds_ref)

        rf = r_ref[...]
        dyf = dy_ref[...]
        mu = jnp.mean(rf, axis=-1, keepdims=True)
        xc = rf - mu
        var = jnp.mean(xc * xc, axis=-1, keepdims=True)
        rstd = lax.rsqrt(var + LN_EPS)
        xhat = xc * rstd
        dxh = dyf * g_ref[...]
        c1 = jnp.mean(dxh, axis=-1, keepdims=True)
        c2 = jnp.mean(dxh * xhat, axis=-1, keepdims=True)
        dr = rstd * (dxh - c1 - xhat * c2)
        dr_ref[...] = dr
        if bf16_copy:
            drb_ref[...] = dr.astype(BF16)
        dg_ref[...] += jnp.sum(dyf * xhat, axis=0, keepdims=True)
        db_ref[...] += jnp.sum(dyf, axis=0, keepdims=True)
        ds_ref[...] += jnp.sum(dr, axis=0, keepdims=True)

    row = pl.BlockSpec((tm, Dm), lambda i: (i, 0))
    vec = pl.BlockSpec((1, Dm), lambda i: (0, 0))
    vshape = jax.ShapeDtypeStruct((1, Dm), F32)
    copies = ((jax.ShapeDtypeStruct((S, Dm), BF16),), (row,)) if bf16_copy else ((), ())
    res = _pcall(kern, name=name,
                 out_shape=(jax.ShapeDtypeStruct((S, Dm), F32),) + copies[0] + (vshape, vshape, vshape),
                 grid=(S // tm,), in_specs=[row, row, vec], out_specs=(row,) + copies[1] + (vec, vec, vec),
                 dims=("arbitrary",), vmem_mb=48)(dy, r, g.reshape(1, Dm))
    return res if bf16_copy else (res[0], None) + tuple(res[1:])


def _loss_ln_bwd(target, r, g, b, *, name):
    S, Dm = r.shape
    tm = min(512, S)

    def kern(t_ref, r_ref, g_ref, b_ref, l_ref, dr_ref, drb_ref, dg_ref, db_ref, ds_ref):
        @pl.when(pl.program_id(0) == 0)
        def _():
            l_ref[...] = jnp.zeros_like(l_ref)
            dg_ref[...] = jnp.zeros_like(dg_ref)
            db_ref[...] = jnp.zeros_like(db_ref)
            ds_ref[...] = jnp.zeros_like(ds_ref)

        rf = r_ref[...]
        mu = jnp.mean(rf, axis=-1, keepdims=True)
        xc = rf - mu
        var = jnp.mean(xc * xc, axis=-1, keepdims=True)
        rstd = lax.rsqrt(var + LN_EPS)
        xhat = xc * rstd
        e = (xhat * g_ref[...] + b_ref[...]) - t_ref[...]
        dyf = e / float(Dm)
        per_row = jnp.mean(e * e, axis=-1, keepdims=True)
        l_ref[...] += 0.5 * jnp.sum(per_row, axis=0, keepdims=True)
        dxh = dyf * g_ref[...]
        c1 = jnp.mean(dxh, axis=-1, keepdims=True)
        c2 = jnp.mean(dxh * xhat, axis=-1, keepdims=True)
        dr = rstd * (dxh - c1 - xhat * c2)
        dr_ref[...] = dr
        drb_ref[...] = dr.astype(BF16)
        dg_ref[...] += jnp.sum(dyf * xhat, axis=0, keepdims=True)
        db_ref[...] += jnp.sum(dyf, axis=0, keepdims=True)
        ds_ref[...] += jnp.sum(dr, axis=0, keepdims=True)

    row = pl.BlockSpec((tm, Dm), lambda i: (i, 0))
    vec = pl.BlockSpec((1, Dm), lambda i: (0, 0))
    acc = pl.BlockSpec((8, LANE), lambda i: (0, 0))
    vshape = jax.ShapeDtypeStruct((1, Dm), F32)
    return _pcall(kern, name=name,
                  out_shape=(jax.ShapeDtypeStruct((8, LANE), F32), jax.ShapeDtypeStruct((S, Dm), F32),
                             jax.ShapeDtypeStruct((S, Dm), BF16), vshape, vshape, vshape),
                  grid=(S // tm,), in_specs=[row, row, vec, vec], out_specs=(acc, row, row, vec, vec, vec),
                  dims=("arbitrary",), vmem_mb=56)(target, r, g.reshape(1, Dm), b.reshape(1, Dm))


def _rot_sum(t):
    return pltpu.roll(t, 32, 1) + pltpu.roll(t, 96, 1)


def _mla_qkv(proj, cos_t, sin_t, qg, kvg, wuq_t, wukv_t, *, name):
    S = proj.shape[0]
    tm = min(512, S)

    def kern(ql_ref, kvl_ref, kr_ref, cos_ref, sin_ref, qg_ref, kvg_ref, wuq_ref, wukv_ref,
             qc_ref, kc_ref, v_ref, vt_ref, qn_ref, kvn_ref):
        cosv = cos_ref[...]
        sinv = sin_ref[...]

        def rope(t):
            return t * cosv + _rot_sum(t) * sinv

        ql = ql_ref[...]
        qn = (ql * lax.rsqrt(jnp.mean(ql * ql, axis=-1, keepdims=True) + RMS_EPS) * qg_ref[...]).astype(BF16)
        kvl = kvl_ref[...]
        kvn = (kvl * lax.rsqrt(jnp.mean(kvl * kvl, axis=-1, keepdims=True) + RMS_EPS) * kvg_ref[...]).astype(BF16)
        qn_ref[...] = qn
        kvn_ref[...] = kvn
        q = lax.dot_general(qn, wuq_ref[...], NT, preferred_element_type=F32)
        kv = lax.dot_general(kvn, wukv_ref[...], NT, preferred_element_type=F32)
        kr = rope(kr_ref[...]).astype(BF16)
        for h in range(N_HEADS):
            c0 = 256 * h
            qc_ref[:, c0:c0 + 128] = q[:, c0:c0 + 128].astype(BF16)
            qc_ref[:, c0 + 128:c0 + 256] = rope(q[:, c0 + 128:c0 + 256]).astype(BF16)
            kc_ref[:, c0:c0 + 128] = kv[:, c0:c0 + 128].astype(BF16)
            kc_ref[:, c0 + 128:c0 + 256] = kr
            vh = kv[:, c0 + 128:c0 + 256]
            v_ref[:, 128 * h:128 * h + 128] = vh.astype(BF16)
            vt_ref[h] = jnp.transpose(vh).astype(BF16)

    def row(w, blk):
        return pl.BlockSpec((tm, w), lambda i: (i, blk))

    def full(shape):
        return pl.BlockSpec(shape, lambda i: (0,) * len(shape))

    t = min(TQ, S)
    per = t // tm
    vt_spec = pl.BlockSpec((N_HEADS, None, 128, tm), lambda i: (0, i // per, 0, i % per))
    outs = (jax.ShapeDtypeStruct((S, 2048), BF16), jax.ShapeDtypeStruct((S, 2048), BF16),
            jax.ShapeDtypeStruct((S, 1024), BF16), jax.ShapeDtypeStruct((N_HEADS, S // t, 128, t), BF16),
            jax.ShapeDtypeStruct((S, Q_LORA), BF16), jax.ShapeDtypeStruct((S, KV_LORA), BF16))
    return _pcall(kern, name=name, out_shape=outs, grid=(S // tm,),
                  in_specs=[row(512, 0), row(256, 2), row(128, 6), row(128, 0), row(128, 0),
                            full((1, Q_LORA)), full((1, KV_LORA)), full((2048, Q_LORA)), full((2048, KV_LORA))],
                  out_specs=(row(2048, 0), row(2048, 0), row(1024, 0), vt_spec, row(512, 0), row(256, 0)),
                  dims=("parallel",), vmem_mb=48)(
                      proj, proj, proj, cos_t, sin_t, qg.reshape(1, -1), kvg.reshape(1, -1), wuq_t, wukv_t)


def _mla_qkv_bwd(dqb, dkvb, dkr_heads, qn, kvn, proj, cos_t, sin_t, qg, kvg, wuq_t, wukv_t, *, name):
    S = proj.shape[0]
    tm = min(512, S)

    def kern(dqb_ref, dkvb_ref, dkrh_ref, qn_ref, kvn_ref, ql_ref, kvl_ref, cos_ref, sin_ref, qg_ref, kvg_ref,
             wuq_ref, wukv_ref, dml_ref, dqg_ref, dkvg_ref, dwuq_ref, dwukv_ref):
        @pl.when(pl.program_id(0) == 0)
        def _():
            dqg_ref[...] = jnp.zeros_like(dqg_ref)
            dkvg_ref[...] = jnp.zeros_like(dkvg_ref)
            dwuq_ref[...] = jnp.zeros_like(dwuq_ref)
            dwukv_ref[...] = jnp.zeros_like(dwukv_ref)

        dwuq_ref[...] += lax.dot_general(dqb_ref[...], qn_ref[...], TN, preferred_element_type=F32)
        dwukv_ref[...] += lax.dot_general(dkvb_ref[...], kvn_ref[...], TN, preferred_element_type=F32)

        cosv = cos_ref[...]
        sinv = sin_ref[...]

        def unrope(t):
            return t * cosv - _rot_sum(t) * sinv

        dkr = dkrh_ref[:, 0:128]
        for h in range(1, N_HEADS):
            dkr = dkr + dkrh_ref[:, 128 * h:128 * h + 128]

        def rms_bwd(x, g, dy):
            n = x.shape[-1]
            rs = lax.rsqrt(jnp.mean(x * x, axis=-1, keepdims=True) + RMS_EPS)
            dyg = dy * g
            dx = rs * dyg - x * (rs * rs * rs) * (jnp.sum(dyg * x, axis=-1, keepdims=True) / n)
            return dx, jnp.sum(dy * (x * rs), axis=0, keepdims=True)

        dqn = jnp.dot(dqb_ref[...], wuq_ref[...], preferred_element_type=F32)
        dql, dqg = rms_bwd(ql_ref[...], qg_ref[...], dqn)
        dqg_ref[...] += dqg
        dkvn = jnp.dot(dkvb_ref[...], wukv_ref[...], preferred_element_type=F32)
        dkvl, dkvg = rms_bwd(kvl_ref[...], kvg_ref[...], dkvn)
        dkvg_ref[...] += dkvg
        dml_ref[:, 0:512] = dql.astype(BF16)
        dml_ref[:, 512:768] = dkvl.astype(BF16)
        dml_ref[:, 768:896] = unrope(dkr).astype(BF16)
        dml_ref[:, 896:1024] = jnp.zeros((tm, 128), BF16)

    def row(w, blk):
        return pl.BlockSpec((tm, w), lambda i: (i, blk))

    def full(shape):
        return pl.BlockSpec(shape, lambda i: (0,) * len(shape))

    outs = (jax.ShapeDtypeStruct((S, W_MLA), BF16), jax.ShapeDtypeStruct((1, Q_LORA), F32),
            jax.ShapeDtypeStruct((1, KV_LORA), F32), jax.ShapeDtypeStruct((2048, Q_LORA), F32),
            jax.ShapeDtypeStruct((2048, KV_LORA), F32))
    return _pcall(kern, name=name, out_shape=outs, grid=(S // tm,),
                  in_specs=[row(2048, 0), row(2048, 0), row(1024, 0), row(512, 0), row(256, 0), row(512, 0),
                            row(256, 2), row(128, 0), row(128, 0), full((1, Q_LORA)), full((1, KV_LORA)),
                            full((2048, Q_LORA)), full((2048, KV_LORA))],
                  out_specs=(row(W_MLA, 0), full((1, Q_LORA)), full((1, KV_LORA)), full((2048, Q_LORA)),
                             full((2048, KV_LORA))),
                  dims=("arbitrary",), vmem_mb=56)(
                      dqb, dkvb, dkr_heads, qn, kvn, proj, proj, cos_t, sin_t, qg.reshape(1, -1), kvg.reshape(1, -1),
                      wuq_t, wukv_t)


def _flash_fwd(qc, kc, vt, *, name, comm=None):
    S = qc.shape[0]
    t = min(TQ, S)
    n = S // t

    def kern(q_ref, k_ref, vt_ref, o_ref, lse_ref, m_s, l_s, acc_s):
        qi = pl.program_id(1)
        m_s[...] = jnp.full_like(m_s, -jnp.inf)
        l_s[...] = jnp.zeros_like(l_s)
        acc_s[...] = jnp.zeros_like(acc_s)

        half = t // 2

        def scores(kb, q_lo=0, q_n=t, k_n=t):
            k0 = pl.multiple_of(kb * t, t)
            return lax.dot_general(k_ref[pl.ds(k0, k_n), :], q_ref[q_lo:q_lo + q_n, :], NT,
                                   preferred_element_type=F32)

        def update(kb, st, q_lo=0, diagonal=False):
            k_n, q_n = st.shape
            if diagonal:
                krow = lax.broadcasted_iota(jnp.int32, (k_n, q_n), 0)
                qcol = lax.broadcasted_iota(jnp.int32, (k_n, q_n), 1) + q_lo
                st = jnp.where(krow <= qcol, st, -jnp.inf)
            lanes = slice(q_lo, q_lo + q_n)
            m_prev = m_s[:, lanes]
            m_new = jnp.maximum(m_prev, jnp.max(st, axis=0, keepdims=True))
            a = jnp.exp2((m_prev - m_new) * SCALE_LOG2E)
            pt = jnp.exp2((st - m_new) * SCALE_LOG2E)
            l_s[:, lanes] = a * l_s[:, lanes] + jnp.sum(pt, axis=0, keepdims=True)
            acc_s[:, lanes] = a * acc_s[:, lanes] + jnp.dot(vt_ref[kb, :, 0:k_n], pt.astype(BF16),
                                                            preferred_element_type=F32)
            m_s[:, lanes] = m_new

        def group(kb, count, last_diagonal):
            whole = count - 1 if last_diagonal else count
            sts = [scores(kb + g) for g in range(whole)]
            if last_diagonal:
                kd = kb + count - 1
                s_lo, s_hi = scores(kd, 0, half, half), scores(kd, half, half, t)
            for g in range(whole):
                update(kb + g, sts[g])
            if last_diagonal:
                update(kd, s_lo, 0, True)
                update(kd, s_hi, half, True)

        def body(i, carry):
            group(FWD_GROUP * i, FWD_GROUP, False)
            return carry

        full = qi // FWD_GROUP
        lax.fori_loop(0, full, body, 0)
        for rem in range(FWD_GROUP):
            @pl.when(qi - FWD_GROUP * full == rem)
            def _():
                group(qi - rem, rem + 1, True)
        o_ref[...] = jnp.transpose(acc_s[...] / l_s[...])
        lse_ref[pl.ds(qi, 1), :] = m_s[...] * SCALE_LOG2E + jnp.log2(l_s[...])

    q_spec = pl.BlockSpec((t, 256), lambda h, qi: (qi, h))
    k_spec = pl.BlockSpec((S, 256), lambda h, qi: (0, h))
    vt_spec = pl.BlockSpec((None, n, 128, t), lambda h, qi: (h, 0, 0, 0))
    o_spec = pl.BlockSpec((t, 128), lambda h, qi: (qi, h))
    lse_spec = pl.BlockSpec((None, n, t), lambda h, qi: (h, 0, 0))
    return _pcall(kern, name=name,
                  out_shape=(jax.ShapeDtypeStruct((S, D_MLA), F32), jax.ShapeDtypeStruct((N_HEADS, n, t), F32)),
                  grid=(N_HEADS, n), in_specs=[q_spec, k_spec, vt_spec], out_specs=(o_spec, lse_spec),
                  scratch=[pltpu.VMEM((1, t), F32), pltpu.VMEM((1, t), F32), pltpu.VMEM((128, t), F32)],
                  dims=("parallel", "arbitrary"), vmem_mb=48, comm=comm)(qc, kc, vt)


def _flash_bwd(qc, kc, v, do, lse2, delta, cos_t, sin_t, *, name, comm=None):
    S = qc.shape[0]
    t = min(TQ, S)
    n = S // t

    def kern(q_ref, k_ref, v_ref, do_ref, lse_ref, dl_ref, cos_ref, sin_ref, dqb_ref, dkvb_ref, dkr_ref,
             dq_ref, dk_ref, dv_ref):
        ki = pl.program_id(1)

        @pl.when(ki == 0)
        def _():
            dq_ref[...] = jnp.zeros_like(dq_ref)

        dk_ref[...] = jnp.zeros_like(dk_ref)
        dv_ref[...] = jnp.zeros_like(dv_ref)

        half = t // 2

        def step(qb, q_lo=0, q_n=t, k_n=t, diagonal=False):
            q0 = pl.multiple_of(qb * t + q_lo, half)
            lanes = slice(q_lo, q_lo + q_n)
            kt = k_ref[0:k_n, :]
            qblk = q_ref[pl.ds(q0, q_n), :]
            dob = do_ref[pl.ds(q0, q_n), :].astype(BF16)
            st = lax.dot_general(kt, qblk, NT, preferred_element_type=F32)
            pt = jnp.exp2(st * SCALE_LOG2E - lse_ref[pl.ds(qb, 1), lanes])
            if diagonal:
                krow = lax.broadcasted_iota(jnp.int32, (k_n, q_n), 0)
                qcol = lax.broadcasted_iota(jnp.int32, (k_n, q_n), 1) + q_lo
                pt = jnp.where(krow <= qcol, pt, 0.0)
            dv_ref[0:k_n, :] += jnp.dot(pt.astype(BF16), dob, preferred_element_type=F32)
            dpt = lax.dot_general(v_ref[0:k_n, :], dob, NT, preferred_element_type=F32)
            dst = (pt * (dpt - dl_ref[pl.ds(qb, 1), lanes]) * SCALE).astype(BF16)
            dk_ref[0:k_n, :] += jnp.dot(dst, qblk, preferred_element_type=F32)
            dq_ref[pl.ds(q0, q_n), :] += lax.dot_general(dst, kt, TN, preferred_element_type=F32)

        step(ki, 0, half, half, True)
        step(ki, half, half, t, True)
        rest = n - 1 - ki
        full = rest // BWD_GROUP

        def body(i, carry):
            for g in range(BWD_GROUP):
                step(ki + 1 + BWD_GROUP * i + g)
            return carry

        lax.fori_loop(0, full, body, 0)
        for rem in range(1, BWD_GROUP):
            @pl.when(rest - BWD_GROUP * full == rem)
            def _():
                for g in range(rem):
                    step(n - rem + g)

        dkvb_ref[:, 0:128] = dk_ref[:, 0:128].astype(BF16)
        dkvb_ref[:, 128:256] = dv_ref[...].astype(BF16)
        dkr_ref[...] = dk_ref[:, 128:256]

        @pl.when(ki == n - 1)
        def _():
            dqb_ref[:, 0:128] = dq_ref[:, 0:128].astype(BF16)
            dqr = dq_ref[:, 128:256]
            dqb_ref[:, 128:256] = (dqr * cos_ref[...] - _rot_sum(dqr) * sin_ref[...]).astype(BF16)

    def whole(w):
        return pl.BlockSpec((S, w), lambda h, ki: (0, h))

    def krow(w):
        return pl.BlockSpec((t, w), lambda h, ki: (ki, h))

    stat = pl.BlockSpec((None, n, t), lambda h, ki: (h, 0, 0))
    table = pl.BlockSpec((S, 128), lambda h, ki: (0, 0))
    return _pcall(kern, name=name,
                  out_shape=(jax.ShapeDtypeStruct((S, 2048), BF16), jax.ShapeDtypeStruct((S, 2048), BF16),
                             jax.ShapeDtypeStruct((S, D_MLA), F32)),
                  grid=(N_HEADS, n),
                  in_specs=[whole(256), krow(256), krow(128), whole(128), stat, stat, table, table],
                  out_specs=(whole(256), krow(256), krow(128)),
                  scratch=[pltpu.VMEM((S, 256), F32), pltpu.VMEM((t, 256), F32), pltpu.VMEM((t, 128), F32)],
                  dims=("parallel", "arbitrary"), vmem_mb=56, comm=comm)(qc, kc, v, do, lse2, delta, cos_t, sin_t)


def _mixer_specs(S, tm):
    hb = tm // HALO
    last_hb = S // HALO - 1

    def main(w, blk):
        return pl.BlockSpec((tm, w), lambda i: (i, blk))

    def prev(w, blk):
        return pl.BlockSpec((HALO, w), lambda i: (jnp.maximum(i * hb - 1, 0), blk))

    def nxt(w, blk):
        return pl.BlockSpec((HALO, w), lambda i: (jnp.minimum((i + 1) * hb, last_hb), blk))

    def full(shape):
        return pl.BlockSpec(shape, lambda i: (0,) * len(shape))

    return main, prev, nxt, full


def _fill_halo(i, xp, xu, hp_ref, hch_ref, hcc_ref, pin_ref, ch_ref, cc_ref, tm):
    first = i == 0
    xp[0:HALO, :] = jnp.where(first, 0.0, hp_ref[...])
    xp[HALO:HALO + tm, :] = pin_ref[...]
    xu[0:HALO, :] = jnp.where(first, 0.0, hch_ref[...] * hcc_ref[...])
    xu[HALO:HALO + tm, :] = cc_ref[...] * ch_ref[...]


def _pooled(xp, g, t1, tm):
    w = POOL_WINDOWS[g]
    lanes = slice(128 * g, 128 * g + 128)
    x0 = xp[HALO:HALO + tm, lanes]
    acc = x0
    for k in range(1, w):
        acc = acc + xp[HALO - k:HALO - k + tm, lanes]
    return acc / jnp.minimum(t1, float(w)) - x0


def _conv_fwd(xu, cw_ref, tm):
    return (cw_ref[0:1, :] * xu[HALO - 2:HALO - 2 + tm, :] + cw_ref[1:2, :] * xu[HALO - 1:HALO - 1 + tm, :]
            + cw_ref[2:3, :] * xu[HALO:HALO + tm, :])


def _mixer_fwd(proj, o, wpool, ps, convw, *, name):
    S = proj.shape[0]
    tm = min(512, S)
    main, prev, _, full = _mixer_specs(S, tm)

    def kern(gm_ref, pin_ref, gp_ref, ch_ref, cb_ref, cc_ref, gc_ref, hp_ref, hch_ref, hcc_ref,
             o_ref, wp_ref, ps_ref, cw_ref, mix_ref, xp, xu):
        i = pl.program_id(0)
        _fill_halo(i, xp, xu, hp_ref, hch_ref, hcc_ref, pin_ref, ch_ref, cc_ref, tm)
        t1 = (i * tm + lax.broadcasted_iota(jnp.int32, (tm, 1), 0) + 1).astype(F32)
        for g in range(4):
            lanes = slice(128 * g, 128 * g + 128)
            pooled = _pooled(xp, g, t1, tm)
            z = jnp.dot(pooled.astype(BF16), wp_ref[g].astype(BF16), preferred_element_type=F32)
            gp = gp_ref[:, lanes]
            y = z * ps_ref[:, lanes] * (gp * _sigmoid(gp))
            mix_ref[:, 1024 + 128 * g:1024 + 128 * g + 128] = y.astype(BF16)
        gc = gc_ref[...]
        mix_ref[:, 1536:2048] = (cb_ref[...] * _conv_fwd(xu, cw_ref, tm) * (gc * _sigmoid(gc))).astype(BF16)
        gm = gm_ref[...]
        mix_ref[:, 0:1024] = (o_ref[...] * (gm * _sigmoid(gm))).astype(BF16)

    return _pcall(kern, name=name, out_shape=jax.ShapeDtypeStruct((S, 2048), BF16), grid=(S // tm,),
                  in_specs=[main(1024, 1), main(512, 4), main(512, 5), main(512, 6), main(512, 7), main(512, 8),
                            main(512, 9), prev(512, 4), prev(512, 6), prev(512, 8),
                            main(1024, 0), full((4, 128, 128)), full((1, 512)), full((3, 512))],
                  out_specs=main(2048, 0),
                  scratch=[pltpu.VMEM((tm + HALO, 512), F32), pltpu.VMEM((tm + HALO, 512), F32)],
                  dims=("parallel",), vmem_mb=48)(
                      proj, proj, proj, proj, proj, proj, proj, proj, proj, proj, o, wpool, ps.reshape(1, 512), convw)


def _mixer_bwd(dmix, proj, o, wpool, ps, convw, *, name):
    S = proj.shape[0]
    tm = min(512, S)
    n = S // tm
    t = min(TQ, S)
    per = t // tm
    main, prev, nxt, full = _mixer_specs(S, tm)

    def kern(dm_ref, dmn_ref, gm_ref, pin_ref, gp_ref, ch_ref, cb_ref, cc_ref, gc_ref,
             hp_ref, hch_ref, hcc_ref, gpn_ref, cbn_ref, gcn_ref, o_ref, wp_ref, ps_ref, cw_ref,
             d_ref, do_ref, dl_ref, dwp_ref, dps_ref, dcw_ref, xp, xu, ee, ed):
        i = pl.program_id(0)
        last = i == n - 1

        @pl.when(i == 0)
        def _():
            dwp_ref[...] = jnp.zeros_like(dwp_ref)
            dps_ref[...] = jnp.zeros_like(dps_ref)
            dcw_ref[...] = jnp.zeros_like(dcw_ref)

        _fill_halo(i, xp, xu, hp_ref, hch_ref, hcc_ref, pin_ref, ch_ref, cc_ref, tm)
        t1 = (i * tm + lax.broadcasted_iota(jnp.int32, (tm, 1), 0) + 1).astype(F32)
        t1n = ((i + 1) * tm + lax.broadcasted_iota(jnp.int32, (HALO, 1), 0) + 1).astype(F32)
        c_pin, c_gp, c_ch, c_cb, c_cc, c_gc = 1024, 1536, 2048, 2560, 3072, 3584

        for g in range(4):
            w = float(POOL_WINDOWS[g])
            lanes = slice(128 * g, 128 * g + 128)
            pooled = _pooled(xp, g, t1, tm)
            pb = pooled.astype(BF16)
            wp = wp_ref[g].astype(BF16)
            z = jnp.dot(pb, wp, preferred_element_type=F32)
            psl = ps_ref[:, lanes]
            sg, dsg = _silu_and_grad(gp_ref[:, lanes])
            dmp = dm_ref[:, 1024 + 128 * g:1024 + 128 * g + 128]
            dyp = dmp * sg
            d_ref[:, c_gp + 128 * g:c_gp + 128 * g + 128] = (dmp * (z * psl) * dsg).astype(BF16)
            dps_ref[:, lanes] += jnp.sum(dyp * z, axis=0, keepdims=True)
            dz = (dyp * psl).astype(BF16)
            dwp_ref[g] += lax.dot_general(pb, dz, TN, preferred_element_type=F32)
            dpl = lax.dot_general(dz, wp, NT, preferred_element_type=F32)
            ee[0:tm, lanes] = dpl / jnp.minimum(t1, w)
            gpn = gpn_ref[:, lanes]
            dzn = (dmn_ref[:, lanes] * (gpn * _sigmoid(gpn)) * psl).astype(BF16)
            dpn = lax.dot_general(dzn, wp, NT, preferred_element_type=F32)
            ee[tm:tm + HALO, lanes] = jnp.where(last, 0.0, dpn / jnp.minimum(t1n, w))
            acc = ee[0:tm, lanes]
            for k in range(1, POOL_WINDOWS[g]):
                acc = acc + ee[k:k + tm, lanes]
            d_ref[:, c_pin + 128 * g:c_pin + 128 * g + 128] = (acc - dpl).astype(BF16)

        yc = _conv_fwd(xu, cw_ref, tm)
        sgc, dsgc = _silu_and_grad(gc_ref[...])
        cb = cb_ref[...]
        dmc = dm_ref[:, 1536:2048]
        d_ref[:, c_gc:c_gc + 512] = (dmc * cb * yc * dsgc).astype(BF16)
        d_ref[:, c_cb:c_cb + 512] = (dmc * yc * sgc).astype(BF16)
        dyc = dmc * cb * sgc
        ed[0:tm, :] = dyc
        gcn = gcn_ref[...]
        ed[tm:tm + HALO, :] = jnp.where(last, 0.0, dmn_ref[:, 512:1024] * cbn_ref[...] * (gcn * _sigmoid(gcn)))
        dcw_ref[0:1, :] += jnp.sum(dyc * xu[HALO - 2:HALO - 2 + tm, :], axis=0, keepdims=True)
        dcw_ref[1:2, :] += jnp.sum(dyc * xu[HALO - 1:HALO - 1 + tm, :], axis=0, keepdims=True)
        dcw_ref[2:3, :] += jnp.sum(dyc * xu[HALO:HALO + tm, :], axis=0, keepdims=True)
        du = cw_ref[2:3, :] * dyc + cw_ref[1:2, :] * ed[1:1 + tm, :] + cw_ref[0:1, :] * ed[2:2 + tm, :]
        d_ref[:, c_cc:c_cc + 512] = (du * ch_ref[...]).astype(BF16)
        d_ref[:, c_ch:c_ch + 512] = (du * cc_ref[...]).astype(BF16)

        sgm, dsgm = _silu_and_grad(gm_ref[...])
        dmm = dm_ref[:, 0:1024]
        ov = o_ref[...]
        dov = dmm * sgm
        do_ref[...] = dov
        d_ref[:, 0:1024] = (dmm * ov * dsgm).astype(BF16)
        lane = lax.broadcasted_iota(jnp.int32, (tm, LANE), 1)
        dmat = jnp.zeros((tm, LANE), F32)
        for h in range(N_HEADS):
            hs = slice(128 * h, 128 * h + 128)
            dmat = jnp.where(lane == h, jnp.sum(dov[:, hs] * ov[:, hs], axis=1, keepdims=True), dmat)
        dmat_t = jnp.transpose(dmat)
        for part in range(per):
            @pl.when(i % per == part)
            def _():
                for h in range(N_HEADS):
                    dl_ref[h, pl.ds(i // per, 1), part * tm:(part + 1) * tm] = dmat_t[h:h + 1, :]

    outs = (jax.ShapeDtypeStruct((S, W_MIX), BF16), jax.ShapeDtypeStruct((S, 1024), F32),
            jax.ShapeDtypeStruct((N_HEADS, S // t, t), F32),
            jax.ShapeDtypeStruct((4, 128, 128), F32), jax.ShapeDtypeStruct((1, 512), F32),
            jax.ShapeDtypeStruct((3, 512), F32))
    scr = [pltpu.VMEM((tm + HALO, 512), F32) for _ in range(4)]
    return _pcall(kern, name=name, out_shape=outs, grid=(n,),
                  in_specs=[main(2048, 0), nxt(1024, 1),
                            main(1024, 1), main(512, 4), main(512, 5), main(512, 6), main(512, 7), main(512, 8),
                            main(512, 9), prev(512, 4), prev(512, 6), prev(512, 8),
                            nxt(512, 5), nxt(512, 7), nxt(512, 9),
                            main(1024, 0), full((4, 128, 128)), full((1, 512)), full((3, 512))],
                  out_specs=(main(W_MIX, 0), main(1024, 0), full((N_HEADS, S // t, t)), full((4, 128, 128)),
                             full((1, 512)), full((3, 512))),
                  scratch=scr, dims=("arbitrary",), vmem_mb=60)(
                      dmix, dmix, proj, proj, proj, proj, proj, proj, proj, proj, proj, proj, proj, proj, proj,
                      o, wpool, ps.reshape(1, 512), convw)


def _outproj_residual(mix, wout, h, bout, *, name):
    S, Dm = h.shape
    tm = min(512, S)

    def kern(mix_ref, w_ref, h_ref, bo_ref, r_ref):
        out = jnp.dot(mix_ref[...], w_ref[...], preferred_element_type=F32) + bo_ref[...]
        r_ref[...] = ALPHA * h_ref[...] + out

    row = pl.BlockSpec((tm, Dm), lambda i: (i, 0))
    vec = pl.BlockSpec((1, Dm), lambda i: (0, 0))
    wsp = pl.BlockSpec((Dm, Dm), lambda i: (0, 0), pipeline_mode=pl.Buffered(1))
    return _pcall(kern, name=name, out_shape=jax.ShapeDtypeStruct((S, Dm), F32), grid=(S // tm,),
                  in_specs=[row, wsp, row, vec], out_specs=row, dims=("parallel",), vmem_mb=56)(
                      mix, wout, h, bout.reshape(1, Dm))


def _outproj_ln(mix, wout, h, bout, g, b, *, name):
    S, Dm = h.shape
    tm = min(512, S)

    def kern(mix_ref, w_ref, h_ref, bo_ref, g_ref, b_ref, y_ref, yb_ref, r_ref):
        out = jnp.dot(mix_ref[...], w_ref[...], preferred_element_type=F32) + bo_ref[...]
        r = ALPHA * h_ref[...] + out
        r_ref[...] = r
        mu = jnp.mean(r, axis=-1, keepdims=True)
        xc = r - mu
        var = jnp.mean(xc * xc, axis=-1, keepdims=True)
        y = xc * lax.rsqrt(var + LN_EPS) * g_ref[...] + b_ref[...]
        y_ref[...] = y
        yb_ref[...] = y.astype(BF16)

    row = pl.BlockSpec((tm, Dm), lambda i: (i, 0))
    vec = pl.BlockSpec((1, Dm), lambda i: (0, 0))
    wsp = pl.BlockSpec((Dm, Dm), lambda i: (0, 0), pipeline_mode=pl.Buffered(1))
    sds = jax.ShapeDtypeStruct((S, Dm), F32)
    return _pcall(kern, name=name, out_shape=(sds, jax.ShapeDtypeStruct((S, Dm), BF16), sds), grid=(S // tm,),
                  in_specs=[row, wsp, row, vec, vec, vec], out_specs=(row, row, row), dims=("parallel",),
                  vmem_mb=56)(
                      mix, wout, h, bout.reshape(1, Dm), g.reshape(1, Dm), b.reshape(1, Dm))


def _adamw_math(w, g, m, v):
    m = ADAM_B1 * m + (1.0 - ADAM_B1) * g
    v = ADAM_B2 * v + (1.0 - ADAM_B2) * (g * g)
    m_hat = m / (1.0 - ADAM_B1 ** ADAM_STEP)
    v_hat = v / (1.0 - ADAM_B2 ** ADAM_STEP)
    delta = -ADAM_LR * (m_hat / (jnp.sqrt(v_hat) + ADAM_EPS) + ADAM_WD * w)
    return delta, m, v


def _row_tile(R, C):
    best = None
    for cand in range(8, R, 8):
        if R % cand == 0 and cand * C <= 256 * 1024:
            best = cand
    return best if best is not None else R


def _adamw(w, g, m, v, *, name):
    shape = w.shape
    C = shape[-1]
    R = 1
    for s in shape[:-1]:
        R *= s
    tr = _row_tile(R, C)

    def kern(w_ref, g_ref, m_ref, v_ref, d_ref, mo_ref, vo_ref):
        d, mn, vn = _adamw_math(w_ref[...], g_ref[...], m_ref[...], v_ref[...])
        d_ref[...] = d
        mo_ref[...] = mn
        vo_ref[...] = vn

    blk = pl.BlockSpec((tr, C), lambda i: (i, 0))
    sds = jax.ShapeDtypeStruct((R, C), F32)
    outs = _pcall(kern, name=name, out_shape=(sds, sds, sds), grid=(R // tr,), in_specs=[blk] * 4,
                  out_specs=(blk, blk, blk), dims=("parallel",), vmem_mb=48)(
                      w.reshape(R, C), g.reshape(R, C), m.reshape(R, C), v.reshape(R, C))
    return tuple(t.reshape(shape) for t in outs)


def _adamw_halves(w, m, v, halves, c_idx, *, name, comm=None):
    _, R, C = w.shape
    ch = C // 2
    tr = _row_tile(R, ch)
    nb = R // tr

    def kern(c_ref, w_ref, a0_ref, b0_ref, a1_ref, b1_ref, m_ref, v_ref, g_ref, d_ref, mo_ref, vo_ref):
        layer = pl.program_id(0) // nb
        mine = pl.program_id(1) == c_ref[0]
        g = jnp.where(layer == 0, jnp.where(mine, a0_ref[...], b0_ref[...]),
                      jnp.where(mine, a1_ref[...], b1_ref[...]))
        g_ref[...] = g
        d, mn, vn = _adamw_math(w_ref[...], g, m_ref[...], v_ref[...])
        d_ref[...] = d
        mo_ref[...] = mn
        vo_ref[...] = vn

    full = pl.BlockSpec((tr, ch), lambda i, hc: (i, hc))
    half = pl.BlockSpec((tr, ch), lambda i, hc: (i % nb, 0))
    sds = jax.ShapeDtypeStruct((2 * R, C), F32)
    (a0, b0), (a1, b1) = halves
    res = _pcall(kern, name=name, out_shape=(sds,) * 4, grid=(2 * nb, 2),
                 in_specs=[pl.BlockSpec(memory_space=pltpu.SMEM), full, half, half, half, half, full, full],
                 out_specs=(full,) * 4, dims=("parallel", "parallel"), vmem_mb=48, comm=comm)(
                     c_idx, w.reshape(2 * R, C), a0, b0, a1, b1, m.reshape(2 * R, C), v.reshape(2 * R, C))
    outs, landed = res if comm is not None else (res, None)
    outs = tuple(t.reshape(2, R, C) for t in outs)
    return outs if comm is None else (outs, landed)


def _packed_pieces(shape):
    if len(shape) == 4:
        return [((l * shape[1] + g) * 128, 128, (l, g)) for l in range(shape[0]) for g in range(shape[1])]
    per_row = shape[1] // LANE
    return [(a * per_row + j, 1, (slice(a, a + 1), slice(LANE * j, LANE * (j + 1))))
            for a in range(shape[0]) for j in range(per_row)]


def _small_sum_adamw(gathered, own, weights, *, name):
    R = gathered.shape[1]
    nw = len(weights)
    shapes = [w.shape for w, _, _ in weights]
    first_row, r0 = [], 0
    for shp in shapes:
        first_row.append(r0)
        n = 1
        for s in shp:
            n *= s
        r0 += n // LANE

    def kern(ga_ref, own_ref, *refs):
        ins, gsum_ref, outs = refs[:3 * nw], refs[3 * nw], refs[3 * nw + 1:]
        me = 4 * lax.axis_index("x") + 2 * lax.axis_index("y") + lax.axis_index("c")

        def block(k):
            other = ga_ref[jnp.where(me == k, (k + 1) % N_DEV, k)]
            return jnp.where(me == k, own_ref[...], other)

        g = block(0)
        for k in range(1, N_DEV):
            g = g + block(k)
        gsum_ref[...] = g
        for p, shp in enumerate(shapes):
            w_ref, m_ref, v_ref = ins[3 * p:3 * p + 3]
            g_out, d_out, m_out, v_out = outs[4 * p:4 * p + 4]
            for row, rows, idx in _packed_pieces(shp):
                gp = gsum_ref[first_row[p] + row:first_row[p] + row + rows, :]
                d, mn, vn = _adamw_math(w_ref[idx], gp, m_ref[idx], v_ref[idx])
                g_out[idx] = gp
                d_out[idx] = d
                m_out[idx] = mn
                v_out[idx] = vn

    out_shape = [jax.ShapeDtypeStruct((R, LANE), F32)]
    for shp in shapes:
        out_shape += [jax.ShapeDtypeStruct(shp, F32)] * 4
    flat = [a for wmv in weights for a in wmv]
    res = _pcall(kern, name=name, out_shape=tuple(out_shape), vmem_mb=48)(gathered, own, *flat)
    return res[0], [tuple(res[1 + 4 * p:5 + 4 * p]) for p in range(nw)]


def _pair_sums(grads, theirs, c_idx, *, name):
    n = len(grads)
    steps = 8
    tiles = [(g.shape[0] // steps, g.shape[1] // 2) for g in grads]

    def kern(c_ref, *refs):
        for a in range(n):
            refs[2 * n + a][...] = (refs[a][...] + refs[n + a][...]).astype(BF16)

    gs = pltpu.PrefetchScalarGridSpec(
        num_scalar_prefetch=1, grid=(steps,),
        in_specs=[pl.BlockSpec(tl, lambda i, c: (i, c[0])) for tl in tiles]
        + [pl.BlockSpec(tl, lambda i, c: (i, 0)) for tl in tiles],
        out_specs=tuple(pl.BlockSpec(tl, lambda i, c: (i, 0)) for tl in tiles))
    out_shape = tuple(jax.ShapeDtypeStruct((g.shape[0], g.shape[1] // 2), BF16) for g in grads)
    return pl.pallas_call(kern, name=name, out_shape=out_shape, grid_spec=gs,
                          compiler_params=pltpu.CompilerParams(dimension_semantics=("parallel",),
                                                               vmem_limit_bytes=48 << 20))(c_idx, *grads, *theirs)


WeightRows = collections.namedtuple("WeightRows", "full_rows own_rows cols pieces zero_rows")


def _w_in_piece_a(j):
    return jnp.where(j == 0, 0, 1232 * j + GAP)


def _w_in_piece_b(j):
    return jnp.where(j == 0, GAP_AT + GAP, 1232 * j + GAP_AT + GAP)


W_IN = WeightRows(NP, 1232, D_MODEL, ((0, GAP_AT, _w_in_piece_a), (GAP_AT, 1232 - GAP_AT, _w_in_piece_b)),
                  ((GAP_AT, GAP),))
W_OUT = WeightRows(2048, 512, D_MODEL, ((0, 512, lambda j: 512 * j),), ())
W_UQ = WeightRows(2048, 384, Q_LORA, ((0, 192, lambda j: 512 * j), (192, 192, lambda j: 512 * j + 256)),
                  tuple((256 * h + 192, 64) for h in range(N_HEADS)))
W_UKV = WeightRows(2048, 512, KV_LORA, ((0, 512, lambda j: 512 * j),), ())
W_CONV = WeightRows(64, 16, 256, ((0, 16, lambda j: 16 * j),), ())
SHARDED = (W_IN, W_OUT, W_UQ, W_UKV)
SHARDED_NAMES = ("w_in", "w_out", "w_uq", "w_ukv")
WEIGHT_ROWS = dict(zip(SHARDED_NAMES, SHARDED))


def _mesh_pos():
    x, y, c = lax.axis_index("x"), lax.axis_index("y"), lax.axis_index("c")
    return x, y, c


def _other_chips(x, y):
    return [(1 - x, y), (x, 1 - y), (1 - x, 1 - y)]


def _rows(start, n):
    return pl.ds(pl.multiple_of(start, 16), n)


def _half_cols(spec, c):
    ch = spec.cols // 2
    return pl.ds(pl.multiple_of(c * ch, LANE), ch)


def _allgather_script(specs, shards, zeros, layers):
    na = len(specs)
    zlist = [a for a in range(na) if zeros[a] is not None]
    n_layers = [shards[a].shape[0] if layers[a] is None else 1 for a in range(na)]
    plan_first, plan_own, plan_zero = [], [], []
    for a, spec in enumerate(specs):
        for p in range(len(spec.pieces)):
            plan_own.append((a, p))
            for k in range(3):
                plan_first.append((a, p, k))
        for z in range(len(spec.zero_rows)):
            for l in range(n_layers[a]):
                plan_zero.append((a, z, l))
    nf = len(plan_first)
    n_sems = 2 * nf + len(plan_own) + len(plan_zero)

    def copies(ins_all, outs, send_sems, recv_sems):
        ins = [ins_all[a] if layers[a] is None else ins_all[a].at[pl.ds(layers[a], 1)] for a in range(na)]
        zrefs = dict(zip(zlist, ins_all[na:]))
        x, y, c = _mesh_pos()
        j = 2 * x + y
        chips = _other_chips(x, y)
        sibling = (x, y, 1 - c)

        def remote(src, dst, sem, to):
            return pltpu.make_async_remote_copy(src_ref=src, dst_ref=dst, send_sem=send_sems.at[sem],
                                                recv_sem=recv_sems.at[sem], device_id=to, device_id_type=MESH)

        def block(a, p, chip, cols):
            _, n, dst = specs[a].pieces[p]
            return outs[a].at[:, _rows(dst(chip), n), cols]

        def first(i):
            a, p, k = plan_first[i]
            src0, n, _ = specs[a].pieces[p]
            cols = _half_cols(specs[a], c)
            return remote(ins[a].at[:, pl.ds(src0, n), cols], block(a, p, j, cols), i, (*chips[k], c))

        def landed(i, half):
            a, p, k = plan_first[i]
            return block(a, p, 2 * chips[k][0] + chips[k][1], _half_cols(specs[a], half))

        def arrival(i, half, sem):
            return remote(landed(i, half), landed(i, half), sem, sibling)

        def passed(i):
            return remote(landed(i, c), landed(i, c), nf + i, sibling)

        def own(i):
            a, p = plan_own[i]
            src0, n, _ = specs[a].pieces[p]
            return remote(ins[a].at[:, pl.ds(src0, n), :], block(a, p, j, slice(None)), 2 * nf + i, sibling)

        def zero(i):
            a, z, l = plan_zero[i]
            r0, n = specs[a].zero_rows[z]
            return remote(zrefs[a].at[pl.ds(0, n), :], outs[a].at[l, pl.ds(r0, n), :],
                          2 * nf + len(plan_own) + i, sibling)

        fixed = [own(i) for i in range(len(plan_own))] + [zero(i) for i in range(len(plan_zero))]
        return c, fixed, first, arrival, passed

    def start(ins, outs, send_sems, recv_sems):
        _, fixed, first, _, _ = copies(ins, outs, send_sems, recv_sems)
        for cp in fixed:
            cp.start()
        for i in range(nf):
            first(i).start()

    def finish(ins, outs, send_sems, recv_sems):
        c, fixed, first, arrival, passed = copies(ins, outs, send_sems, recv_sems)
        for i in range(nf):
            arrival(i, c, i).wait_recv()
            passed(i).start()
        for i in range(nf):
            arrival(i, 1 - c, nf + i).wait_recv()
        for cp in fixed:
            cp.wait()
        for i in range(nf):
            first(i).wait_send()
            passed(i).wait_send()

    out_shape = tuple(jax.ShapeDtypeStruct((n_layers[a], spec.full_rows, spec.cols), BF16)
                      for a, spec in enumerate(specs))
    args = tuple(shards) + tuple(zeros[a] for a in zlist)
    return CommScript(args, out_shape, n_sems, start, finish)


def _start_all_wait_all(args, out_shape, n_sems, make_copies):
    def start(ins, outs, send_sems, recv_sems):
        for cp in make_copies(ins, outs, send_sems, recv_sems):
            cp.start()

    def finish(ins, outs, send_sems, recv_sems):
        for cp in make_copies(ins, outs, send_sems, recv_sems):
            cp.wait()

    return CommScript(tuple(args), tuple(out_shape), n_sems, start, finish)


def _exchange_script(specs, grads):
    na = len(grads)

    def make_copies(ins, outs, send_sems, recv_sems):
        x, y, c = _mesh_pos()
        return [pltpu.make_async_remote_copy(
            src_ref=ins[a].at[:, _half_cols(specs[a], 1 - c)], dst_ref=outs[a], send_sem=send_sems.at[a],
            recv_sem=recv_sems.at[a], device_id=(x, y, 1 - c), device_id_type=MESH) for a in range(na)]

    out_shape = [jax.ShapeDtypeStruct((s.full_rows, s.cols // 2), F32) for s in specs]
    return _start_all_wait_all(grads, out_shape, na, make_copies)


def _scatter_script(specs, parts):
    na = len(parts)
    plan = [(a, p, k) for a in range(na) for p in range(len(specs[a].pieces)) for k in range(3)]

    def make_copies(ins, outs, send_sems, recv_sems):
        x, y, c = _mesh_pos()
        chips = _other_chips(x, y)
        copies = []
        for i, (a, p, k) in enumerate(plan):
            src0, n, dst = specs[a].pieces[p]
            pk = 2 * chips[k][0] + chips[k][1]
            copies.append(pltpu.make_async_remote_copy(
                src_ref=ins[a].at[_rows(dst(pk), n), :], dst_ref=outs[a].at[k, pl.ds(src0, n), :],
                send_sem=send_sems.at[i], recv_sem=recv_sems.at[i], device_id=(*chips[k], c), device_id_type=MESH))
        return copies

    out_shape = [jax.ShapeDtypeStruct((3, s.own_rows, s.cols // 2), BF16) for s in specs]
    return _start_all_wait_all(parts, out_shape, len(plan), make_copies)


def _chip_sums(specs, parts, recvs, *, name):
    n = len(specs)
    plan = [(a, p) for a in range(n) for p in range(len(specs[a].pieces))]

    def kern(*refs):
        recv_refs, part_refs, o_refs = refs[:n], refs[n:2 * n], refs[2 * n:3 * n]
        own_refs, sems = refs[3 * n:4 * n], refs[4 * n]
        j = 2 * lax.axis_index("x") + lax.axis_index("y")
        copies = []
        for i, (a, p) in enumerate(plan):
            src0, rows, dst = specs[a].pieces[p]
            copies.append(pltpu.make_async_copy(part_refs[a].at[_rows(dst(j), rows), :],
                                                own_refs[a].at[pl.ds(src0, rows), :], sems.at[i]))
        for cp in copies:
            cp.start()
        for cp in copies:
            cp.wait()
        for a in range(n):
            r = recv_refs[a]
            o_refs[a][...] = ((own_refs[a][...].astype(F32) + r[0].astype(F32)) + r[1].astype(F32)) \
                + r[2].astype(F32)

    vm = pl.BlockSpec(memory_space=pltpu.VMEM)
    shapes = [(s.own_rows, s.cols // 2) for s in specs]
    return _pcall(kern, name=name, out_shape=tuple(jax.ShapeDtypeStruct(shp, F32) for shp in shapes),
                  in_specs=[vm] * n + [HBM_SPEC] * n, out_specs=(vm,) * n,
                  scratch=[pltpu.VMEM(shp, BF16) for shp in shapes] + [pltpu.SemaphoreType.DMA((len(plan),))],
                  vmem_mb=56)(*recvs, *parts)


def _sibling_script(sums):
    na = len(sums)

    def make_copies(ins, outs, send_sems, recv_sems):
        x, y, c = _mesh_pos()
        return [pltpu.make_async_remote_copy(
            src_ref=ins[a], dst_ref=outs[a], send_sem=send_sems.at[a], recv_sem=recv_sems.at[a],
            device_id=(x, y, 1 - c), device_id_type=MESH) for a in range(na)]

    out_shape = [jax.ShapeDtypeStruct(t.shape, t.dtype) for t in sums]
    return _start_all_wait_all(sums, out_shape, na, make_copies)


class _SemWindow:
    def __init__(self, sems, offset):
        self._sems, self._offset = sems, offset

    @property
    def at(self):
        return self

    def __getitem__(self, i):
        return self._sems.at[i + self._offset]


def _merge_scripts(*scripts):
    a_off, o_off, s_off = [0], [0], [0]
    for s in scripts:
        a_off.append(a_off[-1] + len(s.args))
        o_off.append(o_off[-1] + len(s.out_shape))
        s_off.append(s_off[-1] + s.n_sems)

    def phase(which):
        def run(ins, outs, send_sems, recv_sems):
            for n, s in enumerate(scripts):
                getattr(s, which)(ins[a_off[n]:a_off[n + 1]], outs[o_off[n]:o_off[n + 1]],
                                  _SemWindow(send_sems, s_off[n]), _SemWindow(recv_sems, s_off[n]))
        return run

    return CommScript(sum((tuple(s.args) for s in scripts), ()), sum((tuple(s.out_shape) for s in scripts), ()),
                      s_off[-1], phase("start"), phase("finish"))


class _GradReducer:
    def __init__(self, layer, names, grads, c_idx):
        self.specs = tuple(WEIGHT_ROWS[nm] for nm in names)
        self.grads, self.c_idx = tuple(grads), c_idx
        self.names = [f"{nm}{layer}" for nm in names]

    def exchange(self):
        return _exchange_script(self.specs, self.grads)

    def scatter(self, theirs):
        self.parts = tuple(_pair_sums(self.grads, tuple(theirs), self.c_idx, name=f"pair_sums_{self.names[0]}"))
        return _scatter_script(self.specs, self.parts)

    def sibling(self, recv):
        self.sums = tuple(_chip_sums(self.specs, self.parts, tuple(recv), name=f"chip_sums_{self.names[0]}"))
        return _sibling_script(self.sums)

    def done(self, others):
        return list(zip(self.sums, others))


def _allgather_small_script(block):
    m_per, n = block.shape

    def copies(ins, outs, send_sems, recv_sems):
        (x_ref,), (out_ref,) = ins, outs
        x, y, c = _mesh_pos()
        me, sibling = (x, y, c), (x, y, 1 - c)
        chips = _other_chips(x, y)

        def rows(px, py, pc):
            return out_ref.at[4 * px + 2 * py + pc]

        def copy(k, blk, to, src=None):
            return pltpu.make_async_remote_copy(
                src_ref=rows(*blk) if src is None else src, dst_ref=rows(*blk), send_sem=send_sems.at[k],
                recv_sem=recv_sems.at[k], device_id=to, device_id_type=MESH)

        first = [copy(0, me, sibling, src=x_ref)]
        first += [copy(1 + k, me, (*chip, c), src=x_ref) for k, chip in enumerate(chips)]
        passed = [copy(4 + k, (*chip, c), sibling) for k, chip in enumerate(chips)]
        landed = [copy(1 + k, (*chip, c), me) for k, chip in enumerate(chips)]
        from_sibling = [copy(0, sibling, me)] + [copy(4 + k, (*chip, 1 - c), me) for k, chip in enumerate(chips)]
        return first, passed, landed, from_sibling

    def start(ins, outs, send_sems, recv_sems):
        first, _, _, _ = copies(ins, outs, send_sems, recv_sems)
        for cp in first:
            cp.start()

    def finish(ins, outs, send_sems, recv_sems):
        first, passed, landed, from_sibling = copies(ins, outs, send_sems, recv_sems)
        for k in range(3):
            landed[k].wait_recv()
            passed[k].start()
        for cp in from_sibling:
            cp.wait_recv()
        for cp in first + passed:
            cp.wait_send()

    return CommScript((block,), (jax.ShapeDtypeStruct((N_DEV, m_per, n), block.dtype),), 7, start, finish)


def _rope_tables(positions):
    half = ROPE // 2
    inv_freq = ROPE_THETA ** (-jnp.arange(half, dtype=F32) / half)
    ang = positions.astype(F32)[:, None] * inv_freq
    cos, sin = jnp.cos(ang), jnp.sin(ang)
    S = positions.shape[0]
    cos_t = jnp.concatenate([cos, cos, jnp.ones((S, 64), F32)], axis=1)
    sin_t = jnp.concatenate([-sin, sin, jnp.zeros((S, 64), F32)], axis=1)
    return cos_t, sin_t


def _decode_conv(bits):
    rows = bits.reshape(DEPTH, N_CHIPS, 16, 256)[:, :, :3, :]
    conv = lax.bitcast_convert_type(rows.reshape(DEPTH, N_CHIPS, 3, 128, 2), F32)
    return jnp.transpose(conv, (0, 2, 1, 3)).reshape(DEPTH, 3, 512)


def _local_step(x, positions, target, emb_g, emb_b, w_in_t0, rest0, weights1, q_g, kv_g, w_pool, pool_scale,
                b_out, ln_g, ln_b, c_idx=None):
    cos_t, sin_t = _rope_tables(positions)
    if isinstance(w_in_t0, CommScript):
        (h, hb), (landed,) = _ln_fwd(x, emb_g, emb_b, name="emb_ln", comm=w_in_t0)
        w_in_t0 = landed[0]
    else:
        h, hb = _ln_fwd(x, emb_g, emb_b, name="emb_ln")
    weights = [None, weights1]
    saved = []
    for l in range(DEPTH):
        if l == 0 and isinstance(rest0, CommScript):
            proj, landed = _matmul(hb, w_in_t0, "nt", name="in_proj0", tm=1024, tn=1024, tk=2048, vmem_mb=56,
                                   comm=rest0)
            weights[0] = (w_in_t0,) + tuple(a[0] for a in landed[:3])
            conv_w = _decode_conv(landed[3])
        else:
            if l == 0:
                weights[0] = (w_in_t0,) + tuple(rest0[:3])
                conv_w = rest0[3]
            proj = _matmul(hb, weights[l][0], "nt", name=f"in_proj{l}", tm=1024, tn=1024, tk=2048, vmem_mb=56)
        w_in_t, w_out, w_uq_t, w_ukv_t = weights[l]
        qc, kc, v, vt, qn, kvn = _mla_qkv(proj, cos_t, sin_t, q_g[l], kv_g[l], w_uq_t, w_ukv_t, name=f"mla_qkv{l}")
        nxt = weights[l + 1] if l + 1 < DEPTH else None
        if isinstance(nxt, CommScript):
            (o, lse2), landed = _flash_fwd(qc, kc, vt, name=f"flash_fwd{l}", comm=nxt)
            weights[l + 1] = tuple(a[0] for a in landed)
        else:
            o, lse2 = _flash_fwd(qc, kc, vt, name=f"flash_fwd{l}")
        mix = _mixer_fwd(proj, o, w_pool[l], pool_scale[l], conv_w[l], name=f"mixer_fwd{l}")
        if l == DEPTH - 1:
            r = _outproj_residual(mix, w_out, h, b_out[l], name=f"out_proj{l}")
            saved.append((hb, proj, qc, kc, v, qn, kvn, o, lse2, mix, r))
        else:
            h_next, hb_next, r = _outproj_ln(mix, w_out, h, b_out[l], ln_g[l], ln_b[l], name=f"out_proj_ln{l}")
            saved.append((hb, proj, qc, kc, v, qn, kvn, o, lse2, mix, r))
            h, hb = h_next, hb_next

    small = [None] * DEPTH
    big = [None] * DEPTH
    above = scatter_above = None
    for l in reversed(range(DEPTH)):
        w_in_t, w_out, w_uq_t, w_ukv_t = weights[l]
        hb_in, proj, qc, kc, v, qn, kvn, o, lse2, mix, r = saved[l]
        if l == DEPTH - 1:
            loss_acc, dr, drb, d_ln_g, d_ln_b, d_b_out = _loss_ln_bwd(target, r, ln_g[l], ln_b[l], name="loss_ln_bwd")
        else:
            dr, drb, d_ln_g, d_ln_b, d_b_out = _ln_bwd(dh, r, ln_g[l], name=f"ln_bwd{l}")
        dmix = _matmul(drb, w_out, "nt", name=f"dmix{l}", tm=1024, tn=1024, tk=2048, vmem_mb=56)
        d_w_out = _matmul(mix, drb, "tn", name=f"dw_out{l}", tm=1024, tn=1024, tk=2048, vmem_mb=56)
        d_mix, do, delta, d_w_pool, d_ps, d_conv = _mixer_bwd(dmix, proj, o, w_pool[l], pool_scale[l], conv_w[l],
                                                              name=f"mixer_bwd{l}")
        if above is not None:
            (dqb, dkvb, dkr), recv = _flash_bwd(qc, kc, v, do, lse2, delta, cos_t, sin_t, name=f"flash_bwd{l}",
                                                comm=scatter_above)
            sibling_above = above.sibling(recv)
        else:
            dqb, dkvb, dkr = _flash_bwd(qc, kc, v, do, lse2, delta, cos_t, sin_t, name=f"flash_bwd{l}")
        d_mla, d_qg, d_kvg, d_w_uq_t, d_w_ukv_t = _mla_qkv_bwd(
            dqb, dkvb, dkr, qn, kvn, proj, cos_t, sin_t, q_g[l], kv_g[l], w_uq_t, w_ukv_t, name=f"mla_qkv_bwd{l}")
        small[l] = dict(q_g=d_qg[0], kv_g=d_kvg[0], w_pool=d_w_pool, pool_scale=d_ps[0], conv_w=d_conv,
                        b_out=d_b_out[0], ln_g=d_ln_g[0], ln_b=d_ln_b[0])
        rest = (d_w_out, d_w_uq_t, d_w_ukv_t)
        if c_idx is None:
            d_w_in_t = _dproj_t_times_h(d_mla, d_mix, hb_in, name=f"dw_in{l}")
            dh = _dproj_times_w(d_mla, d_mix, w_in_t, dr, ALPHA, name=f"dh{l}")
            big[l] = (d_w_in_t,) + rest
        elif l > 0:
            d_w_in_t = _dproj_t_times_h(d_mla, d_mix, hb_in, name=f"dw_in{l}")
            above = _GradReducer(l, SHARDED_NAMES, (d_w_in_t,) + rest, c_idx)
            dh, theirs = _dproj_times_w(d_mla, d_mix, w_in_t, dr, ALPHA, name=f"dh{l}", comm=above.exchange())
            scatter_above = above.scatter(theirs)
        else:
            red_rest = _GradReducer(l, SHARDED_NAMES[1:], rest, c_idx)
            d_w_in_t, landed = _dproj_t_times_h(d_mla, d_mix, hb_in, name=f"dw_in{l}",
                                                comm=_merge_scripts(sibling_above, red_rest.exchange()))
            big[l + 1] = above.done(landed[:len(SHARDED)])
            red_in = _GradReducer(l, SHARDED_NAMES[:1], (d_w_in_t,), c_idx)
            landed = _run_comm(_merge_scripts(red_in.exchange(), red_rest.scatter(landed[len(SHARDED):])),
                               name="exchange_w_in0")
            sibling_rest = red_rest.sibling(landed[1:])
            dh, landed = _dproj_times_w(d_mla, d_mix, w_in_t, dr, ALPHA, name=f"dh{l}",
                                        comm=_merge_scripts(red_in.scatter(landed[:1]), sibling_rest))
            recv_in, others_rest = landed[:1], landed[1:]
    grad_x, _, d_emb_g, d_emb_b, _ = _ln_bwd(dh, x, emb_g, name="emb_ln_bwd", bf16_copy=False)
    if c_idx is not None:
        others_in = _run_comm(red_in.sibling(recv_in), name="send_to_sibling0")
        big[0] = red_in.done(others_in) + red_rest.done(others_rest)
    return loss_acc[0, 0], grad_x, d_emb_g, d_emb_b, small, big


SMALL_ORDER = ("emb_ln_g", "emb_ln_b", "q_norm_g", "kv_norm_g", "w_pool", "pool_scale", "b_out", "ln_g", "ln_b")
SMALL_LAYER_KEYS = ("q_g", "kv_g", "w_pool", "pool_scale", "b_out", "ln_g", "ln_b", "conv_w")


def _pack_small(arrs, extra_rows):
    flat = jnp.concatenate([a.reshape(-1) for a in arrs])
    rows = flat.shape[0] // LANE
    total = -(-(rows + extra_rows) // 8) * 8
    return jnp.pad(flat, (0, total * LANE - flat.shape[0])).reshape(total, LANE)


def kernel(x, positions, emb_ln_g, emb_ln_b, w_in, q_norm_g, kv_norm_g, w_uq, w_ukv, w_pool, pool_scale, conv_w, w_out, b_out, ln_g, ln_b, loss_target, m_emb_ln_g, m_emb_ln_b, m_w_in, m_q_norm_g, m_kv_norm_g, m_w_uq, m_w_ukv, m_w_pool, m_pool_scale, m_conv_w, m_w_out, m_b_out, m_ln_g, m_ln_b, v_emb_ln_g, v_emb_ln_b, v_w_in, v_q_norm_g, v_kv_norm_g, v_w_uq, v_w_ukv, v_w_pool, v_pool_scale, v_conv_w, v_w_out, v_b_out, v_ln_g, v_ln_b):
    xi, yi, ci = lax.axis_index("x"), lax.axis_index("y"), lax.axis_index("c")
    chip = 2 * xi + yi
    c_idx = ci.reshape(1).astype(jnp.int32)

    def t(a):
        return jnp.swapaxes(a, 1, 2)

    conv_bits = lax.bitcast_convert_type(conv_w.reshape(DEPTH, 3 * 128), BF16).reshape(DEPTH, 3, 256)
    conv_bits = jnp.pad(conv_bits, ((0, 0), (0, 13), (0, 0)))
    own = (t(w_in).astype(BF16), w_out.astype(BF16), t(w_uq).astype(BF16), t(w_ukv).astype(BF16))
    zeros = (jnp.zeros((GAP, D_MODEL), BF16), None, jnp.zeros((64, Q_LORA), BF16), None)
    gather_in0 = _allgather_script((W_IN,), own[:1], zeros[:1], (0,))
    gather0 = _allgather_script(SHARDED[1:] + (W_CONV,), own[1:] + (conv_bits,), zeros[1:] + (None,),
                                (0, 0, 0, None))
    gather1 = _allgather_script(SHARDED, own, zeros, (1, 1, 1, 1))

    loss_part, grad_x, d_emb_g, d_emb_b, grads, reduced = _local_step(
        x[0], positions[0], loss_target[0], emb_ln_g, emb_ln_b, gather_in0, gather0, gather1, q_norm_g, kv_norm_g,
        w_pool, pool_scale, b_out, ln_g, ln_b, c_idx)

    def rows(a):
        return a.reshape(1, -1) if a.ndim == 1 else a

    small_wmv = [tuple(rows(a) for a in wmv) for wmv in (
        (emb_ln_g, m_emb_ln_g, v_emb_ln_g), (emb_ln_b, m_emb_ln_b, v_emb_ln_b),
        (q_norm_g, m_q_norm_g, v_q_norm_g), (kv_norm_g, m_kv_norm_g, v_kv_norm_g), (w_pool, m_w_pool, v_w_pool),
        (pool_scale, m_pool_scale, v_pool_scale), (b_out, m_b_out, v_b_out), (ln_g, m_ln_g, v_ln_g),
        (ln_b, m_ln_b, v_ln_b))]
    packed_g = _pack_small(
        [d_emb_g, d_emb_b] + [jnp.stack([grads[l][key] for l in range(DEPTH)]) for key in SMALL_LAYER_KEYS]
        + [jnp.pad(loss_part.reshape(1), (0, LANE - 1))], 0)
    (gathered,) = _run_comm(_allgather_small_script(packed_g), name="allgather_small")
    g_tot, small_upd = _small_sum_adamw(gathered, packed_g, small_wmv, name="small_sum_adamw")
    off = sum(w.size for w, _, _ in small_wmv)
    flat_tot = g_tot.reshape(-1)

    def halves(a):
        return [reduced[l][a] for l in range(DEPTH)]

    upd = {}
    upd["w_in"] = tuple(t(o) for o in _adamw_halves(t(w_in), t(m_w_in), t(v_w_in), halves(0), c_idx,
                                                    name="adamw_w_in"))
    conv_tot = flat_tot[off:off + DEPTH * 3 * 512].reshape(DEPTH, 3, 512)
    loss = flat_tot[off + DEPTH * 3 * 512]
    g_conv = lax.dynamic_slice_in_dim(conv_tot, chip * 128, 128, axis=2)

    def whole(a):
        return jnp.stack([jnp.where(ci == 0, jnp.concatenate([mine, oth], axis=1),
                                    jnp.concatenate([oth, mine], axis=1)) for mine, oth in halves(a)])

    upd["w_out"] = _adamw_halves(w_out, m_w_out, v_w_out, halves(1), c_idx, name="adamw_w_out")
    g_uq, g_ukv = t(whole(2)), t(whole(3))
    upd["w_uq"] = (g_uq,) + _adamw(w_uq, g_uq, m_w_uq, v_w_uq, name="adamw_w_uq")
    upd["w_ukv"] = (g_ukv,) + _adamw(w_ukv, g_ukv, m_w_ukv, v_w_ukv, name="adamw_w_ukv")
    upd["conv_w"] = (g_conv,) + _adamw(conv_w, g_conv, m_conv_w, v_conv_w, name="adamw_conv_w")
    for nm, res in zip(SMALL_ORDER, small_upd):
        upd[nm] = tuple(a.reshape(-1) for a in res) if nm in ("emb_ln_g", "emb_ln_b") else res

    order = ("emb_ln_g", "emb_ln_b", "w_in", "q_norm_g", "kv_norm_g", "w_uq", "w_ukv", "w_pool", "pool_scale",
             "conv_w", "w_out", "b_out", "ln_g", "ln_b")
    outs = [loss, grad_x[None]]
    for field in range(4):
        outs += [upd[nm][field] for nm in order]
    return tuple(outs)
```

```python
import collections

import jax
import jax.numpy as jnp
from jax import lax
from jax.experimental import pallas as pl
from jax.experimental.pallas import tpu as pltpu

F32 = jnp.float32
BF16 = jnp.bfloat16
MESH = pl.DeviceIdType.MESH

D_MODEL = 2048
DEPTH = 2
N_HEADS = 8
NOPE = 128
ROPE = 64
Q_LORA = 512
KV_LORA = 256
D_MLA = 1024
POOL_WINDOWS = (2, 4, 8, 16)
D_IN_PROJ = 4928
LN_EPS = 1e-5
RMS_EPS = 1e-6
ROPE_THETA = 10000.0
ALPHA = (2 * DEPTH) ** 0.25
SCALE = (NOPE + ROPE) ** -0.5
LOG2E = 1.4426950408889634
SCALE_LOG2E = SCALE * LOG2E
ADAM_LR = 0.001
ADAM_B1 = 0.9
ADAM_B2 = 0.999
ADAM_EPS = 1e-08
ADAM_WD = 0.01
ADAM_STEP = 10

NP = 5120
GAP_AT = 832
GAP = NP - D_IN_PROJ
W_MLA = 1024
W_MIX = NP - W_MLA
HALO = 16
LANE = 128
N_CHIPS = 4
N_DEV = 8
TQ = 512
FWD_GROUP = 4
BWD_GROUP = 3

NN = (((1,), (0,)), ((), ()))
NT = (((1,), (1,)), ((), ()))
TN = (((0,), (0,)), ((), ()))


CommScript = collections.namedtuple("CommScript", "args out_shape n_sems start finish")
HBM_SPEC = pl.BlockSpec(memory_space=pl.ANY)


def _pcall(kern, *, name, out_shape, grid=None, in_specs=None, out_specs=None, scratch=(), dims=None,
           vmem_mb=None, comm=None):
    cp = {}
    if dims is not None:
        cp["dimension_semantics"] = dims if comm is None else ("arbitrary",) * len(dims)
    if vmem_mb is not None:
        cp["vmem_limit_bytes"] = vmem_mb << 20
    if comm is None:
        args = dict(name=name, out_shape=out_shape, scratch_shapes=list(scratch),
                    compiler_params=pltpu.CompilerParams(**cp))
        if grid is not None:
            args["grid"] = grid
        if in_specs is not None:
            args["in_specs"] = in_specs
        if out_specs is not None:
            args["out_specs"] = out_specs
        return pl.pallas_call(kern, **args)

    single = not isinstance(out_shape, (tuple, list))
    own_out = (out_shape,) if single else tuple(out_shape)
    own_out_specs = (out_specs,) if single else tuple(out_specs)
    n_in, n_out, n_scr = len(in_specs), len(own_out), len(scratch)
    na, no = len(comm.args), len(comm.out_shape)

    def at(end):
        cond = None
        for d, n in enumerate(grid):
            here = pl.program_id(d) == (n - 1 if end else 0)
            cond = here if cond is None else jnp.logical_and(cond, here)
        return cond

    def wrapped(*refs):
        own_in, c_in = refs[:n_in], refs[n_in:n_in + na]
        o0 = n_in + na
        own_o, c_out = refs[o0:o0 + n_out], refs[o0 + n_out:o0 + n_out + no]
        s0 = o0 + n_out + no
        own_s, (send_sems, recv_sems) = refs[s0:s0 + n_scr], refs[s0 + n_scr:]

        @pl.when(at(False))
        def _():
            comm.start(c_in, c_out, send_sems, recv_sems)

        kern(*own_in, *own_o, *own_s)

        @pl.when(at(True))
        def _():
            comm.finish(c_in, c_out, send_sems, recv_sems)

    call = pl.pallas_call(
        wrapped, name=name, out_shape=own_out + tuple(comm.out_shape), grid=grid,
        in_specs=list(in_specs) + [HBM_SPEC] * na, out_specs=own_out_specs + (HBM_SPEC,) * no,
        scratch_shapes=list(scratch) + [pltpu.SemaphoreType.DMA((comm.n_sems,)),
                                        pltpu.SemaphoreType.DMA((comm.n_sems,))],
        compiler_params=pltpu.CompilerParams(**cp))

    def run(*args):
        res = call(*args, *comm.args)
        own = res[0] if single else tuple(res[:n_out])
        return own, tuple(res[n_out:])

    return run


def _run_comm(script, *, name):
    na, no = len(script.args), len(script.out_shape)

    def body(*refs):
        ins, outs = refs[:na], refs[na:na + no]
        send_sems, recv_sems = refs[na + no:]
        script.start(ins, outs, send_sems, recv_sems)
        script.finish(ins, outs, send_sems, recv_sems)

    return pl.pallas_call(
        body, name=name, out_shape=tuple(script.out_shape), in_specs=[HBM_SPEC] * na, out_specs=(HBM_SPEC,) * no,
        scratch_shapes=[pltpu.SemaphoreType.DMA((script.n_sems,)), pltpu.SemaphoreType.DMA((script.n_sems,))])(
            *script.args)


def _sigmoid(g):
    return 1.0 / (1.0 + jnp.exp(-g))


def _silu_and_grad(g):
    sig = _sigmoid(g)
    return g * sig, sig * (1.0 + g * (1.0 - sig))


def _matmul(a, b, mode, *, name, tm, tn, tk, out_dtype=F32, vmem_mb=48, comm=None):
    if mode == "nn":
        (M, K), N = a.shape, b.shape[1]
    elif mode == "nt":
        (M, K), N = a.shape, b.shape[0]
    else:
        (K, M), N = a.shape, b.shape[1]
    tm, tn, tk = min(tm, M), min(tn, N), min(tk, K)
    assert M % tm == 0 and N % tn == 0 and K % tk == 0, (name, M, N, K)
    nk = K // tk
    dn = {"nn": NN, "nt": NT, "tn": TN}[mode]
    if mode == "tn":
        a_spec = pl.BlockSpec((tk, tm), lambda i, j, k: (k, i))
    else:
        a_spec = pl.BlockSpec((tm, tk), lambda i, j, k: (i, k))
    if mode == "nt":
        b_spec = pl.BlockSpec((tn, tk), lambda i, j, k: (j, k))
    else:
        b_spec = pl.BlockSpec((tk, tn), lambda i, j, k: (k, j))
    o_spec = pl.BlockSpec((tm, tn), lambda i, j, k: (i, j))

    def kern(a_ref, b_ref, o_ref, *rest):
        part = lax.dot_general(a_ref[...].astype(BF16), b_ref[...].astype(BF16), dn,
                               preferred_element_type=F32)
        if nk == 1:
            o_ref[...] = part.astype(out_dtype)
        else:
            acc_ref = rest[0]
            k = pl.program_id(2)

            @pl.when(k == 0)
            def _():
                acc_ref[...] = part

            @pl.when(k > 0)
            def _():
                acc_ref[...] += part

            @pl.when(k == nk - 1)
            def _():
                o_ref[...] = acc_ref[...].astype(out_dtype)

    scratch = [pltpu.VMEM((tm, tn), F32)] if nk > 1 else []
    return _pcall(kern, name=name, out_shape=jax.ShapeDtypeStruct((M, N), out_dtype),
                  grid=(M // tm, N // tn, nk), in_specs=[a_spec, b_spec], out_specs=o_spec, scratch=scratch,
                  dims=("parallel", "parallel", "arbitrary"), vmem_mb=vmem_mb, comm=comm)(a, b)


def _dproj_times_w(d_mla, d_mix, wt, add, add_scale, *, name, comm=None):
    S = d_mla.shape[0]
    Dm = wt.shape[1]
    tm, tn, tk = min(1024, S), 1024, 2048
    nk = 1 + W_MIX // tk

    def kern(a1_ref, a2_ref, b1_ref, b2_ref, add_ref, o_ref, acc_ref):
        k = pl.program_id(2)

        @pl.when(k == 0)
        def _():
            acc_ref[...] = jnp.dot(a1_ref[...], b1_ref[...], preferred_element_type=F32)

        @pl.when(k > 0)
        def _():
            acc_ref[...] += jnp.dot(a2_ref[...], b2_ref[...], preferred_element_type=F32)

        @pl.when(k == nk - 1)
        def _():
            o_ref[...] = add_scale * add_ref[...] + acc_ref[...]

    o_spec = pl.BlockSpec((tm, tn), lambda i, j, k: (i, j))
    b2_spec = pl.BlockSpec((pl.Element(tk), pl.Element(tn)),
                           lambda i, j, k: (pl.multiple_of(W_MLA + tk * jnp.maximum(k - 1, 0), W_MLA),
                                            pl.multiple_of(j * tn, tn)))
    return _pcall(kern, name=name, out_shape=jax.ShapeDtypeStruct((S, Dm), F32), grid=(S // tm, Dm // tn, nk),
                  in_specs=[pl.BlockSpec((tm, W_MLA), lambda i, j, k: (i, 0)),
                            pl.BlockSpec((tm, tk), lambda i, j, k: (i, jnp.maximum(k - 1, 0))),
                            pl.BlockSpec((W_MLA, tn), lambda i, j, k: (0, j)), b2_spec, o_spec],
                  out_specs=o_spec, scratch=[pltpu.VMEM((tm, tn), F32)],
                  dims=("parallel", "parallel", "arbitrary"), vmem_mb=56, comm=comm)(d_mla, d_mix, wt, wt, add)


def _dproj_t_times_h(d_mla, d_mix, h, *, name, comm=None):
    S, Dm = h.shape
    tm, tn, tk = W_MLA, 1024, min(2048, S)
    nk = S // tk

    def kern(a1_ref, a2_ref, b_ref, o_ref, acc_ref):
        i = pl.program_id(0)
        k = pl.program_id(2)
        b = b_ref[...].astype(BF16)

        def accumulate(part):
            @pl.when(k == 0)
            def _():
                acc_ref[...] = part

            @pl.when(k > 0)
            def _():
                acc_ref[...] += part

        @pl.when(i == 0)
        def _():
            accumulate(lax.dot_general(a1_ref[...], b, TN, preferred_element_type=F32))

        @pl.when(i > 0)
        def _():
            accumulate(lax.dot_general(a2_ref[...], b, TN, preferred_element_type=F32))

        @pl.when(k == nk - 1)
        def _():
            o_ref[...] = acc_ref[...]

    return _pcall(kern, name=name, out_shape=jax.ShapeDtypeStruct((NP, Dm), F32), grid=(NP // tm, Dm // tn, nk),
                  in_specs=[pl.BlockSpec((tk, tm), lambda i, j, k: (jnp.where(i == 0, k, nk - 1), 0)),
                            pl.BlockSpec((tk, tm), lambda i, j, k: (jnp.where(i == 0, 0, k), jnp.maximum(i - 1, 0))),
                            pl.BlockSpec((tk, tn), lambda i, j, k: (k, j))],
                  out_specs=pl.BlockSpec((tm, tn), lambda i, j, k: (i, j)), scratch=[pltpu.VMEM((tm, tn), F32)],
                  dims=("parallel", "parallel", "arbitrary"), vmem_mb=48, comm=comm)(d_mla, d_mix, h)


def _ln_fwd(x, g, b, *, name, comm=None):
    S, Dm = x.shape
    tm = min(512, S)

    def kern(x_ref, g_ref, b_ref, y_ref, yb_ref):
        xf = x_ref[...]
        mu = jnp.mean(xf, axis=-1, keepdims=True)
        xc = xf - mu
        var = jnp.mean(xc * xc, axis=-1, keepdims=True)
        y = xc * lax.rsqrt(var + LN_EPS) * g_ref[...] + b_ref[...]
        y_ref[...] = y
        yb_ref[...] = y.astype(BF16)

    row = pl.BlockSpec((tm, Dm), lambda i: (i, 0))
    vec = pl.BlockSpec((1, Dm), lambda i: (0, 0))
    return _pcall(kern, name=name,
                  out_shape=(jax.ShapeDtypeStruct((S, Dm), F32), jax.ShapeDtypeStruct((S, Dm), BF16)),
                  grid=(S // tm,), in_specs=[row, vec, vec], out_specs=(row, row), dims=("parallel",), vmem_mb=48,
                  comm=comm)(
                      x, g.reshape(1, Dm), b.reshape(1, Dm))


def _ln_bwd(dy, r, g, *, name, bf16_copy=True):
    S, Dm = r.shape
    tm = min(512, S)

    def kern(dy_ref, r_ref, g_ref, dr_ref, *rest):
        drb_ref = rest[0] if bf16_copy else None
        dg_ref, db_ref, ds_ref = rest[-3:]

        @pl.when(pl.program_id(0) == 0)
        def _():
            dg_ref[...] = jnp.zeros_like(dg_ref)
            db_ref[...] = jnp.zeros_like(db_ref)
            ds_ref[...] = jnp.zeros_like(ds_ref)

        rf = r_ref[...]
        dyf = dy_ref[...]
        mu = jnp.mean(rf, axis=-1, keepdims=True)
        xc = rf - mu
        var = jnp.mean(xc * xc, axis=-1, keepdims=True)
        rstd = lax.rsqrt(var + LN_EPS)
        xhat = xc * rstd
        dxh = dyf * g_ref[...]
        c1 = jnp.mean(dxh, axis=-1, keepdims=True)
        c2 = jnp.mean(dxh * xhat, axis=-1, keepdims=True)
        dr = rstd * (dxh - c1 - xhat * c2)
        dr_ref[...] = dr
        if bf16_copy:
            drb_ref[...] = dr.astype(BF16)
        dg_ref[...] += jnp.sum(dyf * xhat, axis=0, keepdims=True)
        db_ref[...] += jnp.sum(dyf, axis=0, keepdims=True)
        ds_ref[...] += jnp.sum(dr, axis=0, keepdims=True)

    row = pl.BlockSpec((tm, Dm), lambda i: (i, 0))
    vec = pl.BlockSpec((1, Dm), lambda i: (0, 0))
    vshape = jax.ShapeDtypeStruct((1, Dm), F32)
    copies = ((jax.ShapeDtypeStruct((S, Dm), BF16),), (row,)) if bf16_copy else ((), ())
    res = _pcall(kern, name=name,
                 out_shape=(jax.ShapeDtypeStruct((S, Dm), F32),) + copies[0] + (vshape, vshape, vshape),
                 grid=(S // tm,), in_specs=[row, row, vec], out_specs=(row,) + copies[1] + (vec, vec, vec),
                 dims=("arbitrary",), vmem_mb=48)(dy, r, g.reshape(1, Dm))
    return res if bf16_copy else (res[0], None) + tuple(res[1:])


def _loss_ln_bwd(target, r, g, b, *, name):
    S, Dm = r.shape
    tm = min(512, S)

    def kern(t_ref, r_ref, g_ref, b_ref, l_ref, dr_ref, drb_ref, dg_ref, db_ref, ds_ref):
        @pl.when(pl.program_id(0) == 0)
        def _():
            l_ref[...] = jnp.zeros_like(l_ref)
            dg_ref[...] = jnp.zeros_like(dg_ref)
            db_ref[...] = jnp.zeros_like(db_ref)
            ds_ref[...] = jnp.zeros_like(ds_ref)

        rf = r_ref[...]
        mu = jnp.mean(rf, axis=-1, keepdims=True)
        xc = rf - mu
        var = jnp.mean(xc * xc, axis=-1, keepdims=True)
        rstd = lax.rsqrt(var + LN_EPS)
        xhat = xc * rstd
        e = (xhat * g_ref[...] + b_ref[...]) - t_ref[...]
        dyf = e / float(Dm)
        per_row = jnp.mean(e * e, axis=-1, keepdims=True)
        l_ref[...] += 0.5 * jnp.sum(per_row, axis=0, keepdims=True)
        dxh = dyf * g_ref[...]
        c1 = jnp.mean(dxh, axis=-1, keepdims=True)
        c2 = jnp.mean(dxh * xhat, axis=-1, keepdims=True)
        dr = rstd * (dxh - c1 - xhat * c2)
        dr_ref[...] = dr
        drb_ref[...] = dr.astype(BF16)
        dg_ref[...] += jnp.sum(dyf * xhat, axis=0, keepdims=True)
        db_ref[...] += jnp.sum(dyf, axis=0, keepdims=True)
        ds_ref[...] += jnp.sum(dr, axis=0, keepdims=True)

    row = pl.BlockSpec((tm, Dm), lambda i: (i, 0))
    vec = pl.BlockSpec((1, Dm), lambda i: (0, 0))
    acc = pl.BlockSpec((8, LANE), lambda i: (0, 0))
    vshape = jax.ShapeDtypeStruct((1, Dm), F32)
    return _pcall(kern, name=name,
                  out_shape=(jax.ShapeDtypeStruct((8, LANE), F32), jax.ShapeDtypeStruct((S, Dm), F32),
                             jax.ShapeDtypeStruct((S, Dm), BF16), vshape, vshape, vshape),
                  grid=(S // tm,), in_specs=[row, row, vec, vec], out_specs=(acc, row, row, vec, vec, vec),
                  dims=("arbitrary",), vmem_mb=56)(target, r, g.reshape(1, Dm), b.reshape(1, Dm))


def _rot_sum(t):
    return pltpu.roll(t, 32, 1) + pltpu.roll(t, 96, 1)


def _mla_qkv(proj, cos_t, sin_t, qg, kvg, wuq_t, wukv_t, *, name):
    S = proj.shape[0]
    tm = min(512, S)

    def kern(ql_ref, kvl_ref, kr_ref, cos_ref, sin_ref, qg_ref, kvg_ref, wuq_ref, wukv_ref,
             qc_ref, kc_ref, v_ref, vt_ref, qn_ref, kvn_ref):
        cosv = cos_ref[...]
        sinv = sin_ref[...]

        def rope(t):
            return t * cosv + _rot_sum(t) * sinv

        ql = ql_ref[...]
        qn = (ql * lax.rsqrt(jnp.mean(ql * ql, axis=-1, keepdims=True) + RMS_EPS) * qg_ref[...]).astype(BF16)
        kvl = kvl_ref[...]
        kvn = (kvl * lax.rsqrt(jnp.mean(kvl * kvl, axis=-1, keepdims=True) + RMS_EPS) * kvg_ref[...]).astype(BF16)
        qn_ref[...] = qn
        kvn_ref[...] = kvn
        q = lax.dot_general(qn, wuq_ref[...], NT, preferred_element_type=F32)
        kv = lax.dot_general(kvn, wukv_ref[...], NT, preferred_element_type=F32)
        kr = rope(kr_ref[...]).astype(BF16)
        for h in range(N_HEADS):
            c0 = 256 * h
            qc_ref[:, c0:c0 + 128] = q[:, c0:c0 + 128].astype(BF16)
            qc_ref[:, c0 + 128:c0 + 256] = rope(q[:, c0 + 128:c0 + 256]).astype(BF16)
            kc_ref[:, c0:c0 + 128] = kv[:, c0:c0 + 128].astype(BF16)
            kc_ref[:, c0 + 128:c0 + 256] = kr
            vh = kv[:, c0 + 128:c0 + 256]
            v_ref[:, 128 * h:128 * h + 128] = vh.astype(BF16)
            vt_ref[h] = jnp.transpose(vh).astype(BF16)

    def row(w, blk):
        return pl.BlockSpec((tm, w), lambda i: (i, blk))

    def full(shape):
        return pl.BlockSpec(shape, lambda i: (0,) * len(shape))

    t = min(TQ, S)
    per = t // tm
    vt_spec = pl.BlockSpec((N_HEADS, None, 128, tm), lambda i: (0, i // per, 0, i % per))
    outs = (jax.ShapeDtypeStruct((S, 2048), BF16), jax.ShapeDtypeStruct((S, 2048), BF16),
            jax.ShapeDtypeStruct((S, 1024), BF16), jax.ShapeDtypeStruct((N_HEADS, S // t, 128, t), BF16),
            jax.ShapeDtypeStruct((S, Q_LORA), BF16), jax.ShapeDtypeStruct((S, KV_LORA), BF16))
    return _pcall(kern, name=name, out_shape=outs, grid=(S // tm,),
                  in_specs=[row(512, 0), row(256, 2), row(128, 6), row(128, 0), row(128, 0),
                            full((1, Q_LORA)), full((1, KV_LORA)), full((2048, Q_LORA)), full((2048, KV_LORA))],
                  out_specs=(row(2048, 0), row(2048, 0), row(1024, 0), vt_spec, row(512, 0), row(256, 0)),
                  dims=("parallel",), vmem_mb=48)(
                      proj, proj, proj, cos_t, sin_t, qg.reshape(1, -1), kvg.reshape(1, -1), wuq_t, wukv_t)


def _mla_qkv_bwd(dqb, dkvb, dkr_heads, qn, kvn, proj, cos_t, sin_t, qg, kvg, wuq_t, wukv_t, *, name):
    S = proj.shape[0]
    tm = min(512, S)

    def kern(dqb_ref, dkvb_ref, dkrh_ref, qn_ref, kvn_ref, ql_ref, kvl_ref, cos_ref, sin_ref, qg_ref, kvg_ref,
             wuq_ref, wukv_ref, dml_ref, dqg_ref, dkvg_ref, dwuq_ref, dwukv_ref):
        @pl.when(pl.program_id(0) == 0)
        def _():
            dqg_ref[...] = jnp.zeros_like(dqg_ref)
            dkvg_ref[...] = jnp.zeros_like(dkvg_ref)
            dwuq_ref[...] = jnp.zeros_like(dwuq_ref)
            dwukv_ref[...] = jnp.zeros_like(dwukv_ref)

        dwuq_ref[...] += lax.dot_general(dqb_ref[...], qn_ref[...], TN, preferred_element_type=F32)
        dwukv_ref[...] += lax.dot_general(dkvb_ref[...], kvn_ref[...], TN, preferred_element_type=F32)

        cosv = cos_ref[...]
        sinv = sin_ref[...]

        def unrope(t):
            return t * cosv - _rot_sum(t) * sinv

        dkr = dkrh_ref[:, 0:128]
        for h in range(1, N_HEADS):
            dkr = dkr + dkrh_ref[:, 128 * h:128 * h + 128]

        def rms_bwd(x, g, dy):
            n = x.shape[-1]
            rs = lax.rsqrt(jnp.mean(x * x, axis=-1, keepdims=True) + RMS_EPS)
            dyg = dy * g
            dx = rs * dyg - x * (rs * rs * rs) * (jnp.sum(dyg * x, axis=-1, keepdims=True) / n)
            return dx, jnp.sum(dy * (x * rs), axis=0, keepdims=True)

        dqn = jnp.dot(dqb_ref[...], wuq_ref[...], preferred_element_type=F32)
        dql, dqg = rms_bwd(ql_ref[...], qg_ref[...], dqn)
        dqg_ref[...] += dqg
        dkvn = jnp.dot(dkvb_ref[...], wukv_ref[...], preferred_element_type=F32)
        dkvl, dkvg = rms_bwd(kvl_ref[...], kvg_ref[...], dkvn)
        dkvg_ref[...] += dkvg
        dml_ref[:, 0:512] = dql.astype(BF16)
        dml_ref[:, 512:768] = dkvl.astype(BF16)
        dml_ref[:, 768:896] = unrope(dkr).astype(BF16)
        dml_ref[:, 896:1024] = jnp.zeros((tm, 128), BF16)

    def row(w, blk):
        return pl.BlockSpec((tm, w), lambda i: (i, blk))

    def full(shape):
        return pl.BlockSpec(shape, lambda i: (0,) * len(shape))

    outs = (jax.ShapeDtypeStruct((S, W_MLA), BF16), jax.ShapeDtypeStruct((1, Q_LORA), F32),
            jax.ShapeDtypeStruct((1, KV_LORA), F32), jax.ShapeDtypeStruct((2048, Q_LORA), F32),
            jax.ShapeDtypeStruct((2048, KV_LORA), F32))
    return _pcall(kern, name=name, out_shape=outs, grid=(S // tm,),
                  in_specs=[row(2048, 0), row(2048, 0), row(1024, 0), row(512, 0), row(256, 0), row(512, 0),
                            row(256, 2), row(128, 0), row(128, 0), full((1, Q_LORA)), full((1, KV_LORA)),
                            full((2048, Q_LORA)), full((2048, KV_LORA))],
                  out_specs=(row(W_MLA, 0), full((1, Q_LORA)), full((1, KV_LORA)), full((2048, Q_LORA)),
                             full((2048, KV_LORA))),
                  dims=("arbitrary",), vmem_mb=56)(
                      dqb, dkvb, dkr_heads, qn, kvn, proj, proj, cos_t, sin_t, qg.reshape(1, -1), kvg.reshape(1, -1),
                      wuq_t, wukv_t)


def _flash_fwd(qc, kc, vt, *, name, comm=None):
    S = qc.shape[0]
    t = min(TQ, S)
    n = S // t

    def kern(q_ref, k_ref, vt_ref, o_ref, lse_ref, m_s, l_s, acc_s):
        qi = pl.program_id(1)
        m_s[...] = jnp.full_like(m_s, -jnp.inf)
        l_s[...] = jnp.zeros_like(l_s)
        acc_s[...] = jnp.zeros_like(acc_s)

        half = t // 2

        def scores(kb, q_lo=0, q_n=t, k_n=t):
            k0 = pl.multiple_of(kb * t, t)
            return lax.dot_general(k_ref[pl.ds(k0, k_n), :], q_ref[q_lo:q_lo + q_n, :], NT,
                                   preferred_element_type=F32)

        def update(kb, st, q_lo=0, diagonal=False):
            k_n, q_n = st.shape
            if diagonal:
                krow = lax.broadcasted_iota(jnp.int32, (k_n, q_n), 0)
                qcol = lax.broadcasted_iota(jnp.int32, (k_n, q_n), 1) + q_lo
                st = jnp.where(krow <= qcol, st, -jnp.inf)
            lanes = slice(q_lo, q_lo + q_n)
            m_prev = m_s[:, lanes]
            m_new = jnp.maximum(m_prev, jnp.max(st, axis=0, keepdims=True))
            a = jnp.exp2((m_prev - m_new) * SCALE_LOG2E)
            pt = jnp.exp2((st - m_new) * SCALE_LOG2E)
            l_s[:, lanes] = a * l_s[:, lanes] + jnp.sum(pt, axis=0, keepdims=True)
            acc_s[:, lanes] = a * acc_s[:, lanes] + jnp.dot(vt_ref[kb, :, 0:k_n], pt.astype(BF16),
                                                            preferred_element_type=F32)
            m_s[:, lanes] = m_new

        def group(kb, count, last_diagonal):
            whole = count - 1 if last_diagonal else count
            sts = [scores(kb + g) for g in range(whole)]
            if last_diagonal:
                kd = kb + count - 1
                s_lo, s_hi = scores(kd, 0, half, half), scores(kd, half, half, t)
            for g in range(whole):
                update(kb + g, sts[g])
            if last_diagonal:
                update(kd, s_lo, 0, True)
                update(kd, s_hi, half, True)

        def body(i, carry):
            group(FWD_GROUP * i, FWD_GROUP, False)
            return carry

        full = qi // FWD_GROUP
        lax.fori_loop(0, full, body, 0)
        for rem in range(FWD_GROUP):
            @pl.when(qi - FWD_GROUP * full == rem)
            def _():
                group(qi - rem, rem + 1, True)
        o_ref[...] = jnp.transpose(acc_s[...] / l_s[...])
        lse_ref[pl.ds(qi, 1), :] = m_s[...] * SCALE_LOG2E + jnp.log2(l_s[...])

    q_spec = pl.BlockSpec((t, 256), lambda h, qi: (qi, h))
    k_spec = pl.BlockSpec((S, 256), lambda h, qi: (0, h))
    vt_spec = pl.BlockSpec((None, n, 128, t), lambda h, qi: (h, 0, 0, 0))
    o_spec = pl.BlockSpec((t, 128), lambda h, qi: (qi, h))
    lse_spec = pl.BlockSpec((None, n, t), lambda h, qi: (h, 0, 0))
    return _pcall(kern, name=name,
                  out_shape=(jax.ShapeDtypeStruct((S, D_MLA), F32), jax.ShapeDtypeStruct((N_HEADS, n, t), F32)),
                  grid=(N_HEADS, n), in_specs=[q_spec, k_spec, vt_spec], out_specs=(o_spec, lse_spec),
                  scratch=[pltpu.VMEM((1, t), F32), pltpu.VMEM((1, t), F32), pltpu.VMEM((128, t), F32)],
                  dims=("parallel", "arbitrary"), vmem_mb=48, comm=comm)(qc, kc, vt)


def _flash_bwd(qc, kc, v, do, lse2, delta, cos_t, sin_t, *, name, comm=None):
    S = qc.shape[0]
    t = min(TQ, S)
    n = S // t

    def kern(q_ref, k_ref, v_ref, do_ref, lse_ref, dl_ref, cos_ref, sin_ref, dqb_ref, dkvb_ref, dkr_ref,
             dq_ref, dk_ref, dv_ref):
        ki = pl.program_id(1)

        @pl.when(ki == 0)
        def _():
            dq_ref[...] = jnp.zeros_like(dq_ref)

        half = t // 2

        def step(qb, q_lo=0, q_n=t, k_n=t, diagonal=False, first=False):
            q0 = pl.multiple_of(qb * t + q_lo, half)
            lanes = slice(q_lo, q_lo + q_n)
            kt = k_ref[0:k_n, :]
            qblk = q_ref[pl.ds(q0, q_n), :]
            dob = do_ref[pl.ds(q0, q_n), :].astype(BF16)
            st = lax.dot_general(kt, qblk, NT, preferred_element_type=F32)
            pt = jnp.exp2(st * SCALE_LOG2E - lse_ref[pl.ds(qb, 1), lanes])
            if diagonal:
                krow = lax.broadcasted_iota(jnp.int32, (k_n, q_n), 0)
                qcol = lax.broadcasted_iota(jnp.int32, (k_n, q_n), 1) + q_lo
                pt = jnp.where(krow <= qcol, pt, 0.0)
            dv_part = jnp.dot(pt.astype(BF16), dob, preferred_element_type=F32)
            dpt = lax.dot_general(v_ref[0:k_n, :], dob, NT, preferred_element_type=F32)
            dst = (pt * (dpt - dl_ref[pl.ds(qb, 1), lanes]) * SCALE).astype(BF16)
            dk_part = jnp.dot(dst, qblk, preferred_element_type=F32)
            if first:
                dv_ref[...] = dv_part
                dk_ref[...] = dk_part
            else:
                dv_ref[0:k_n, :] += dv_part
                dk_ref[0:k_n, :] += dk_part
            dq_ref[pl.ds(q0, q_n), :] += lax.dot_general(dst, kt, TN, preferred_element_type=F32)

        step(ki, half, half, t, True, first=True)
        step(ki, 0, half, half, True)
        rest = n - 1 - ki
        full = rest // BWD_GROUP

        def body(i, carry):
            for g in range(BWD_GROUP):
                step(ki + 1 + BWD_GROUP * i + g)
            return carry

        lax.fori_loop(0, full, body, 0)
        for rem in range(1, BWD_GROUP):
            @pl.when(rest - BWD_GROUP * full == rem)
            def _():
                for g in range(rem):
                    step(n - rem + g)

        dkvb_ref[:, 0:128] = dk_ref[:, 0:128].astype(BF16)
        dkvb_ref[:, 128:256] = dv_ref[...].astype(BF16)
        dkr_ref[...] = dk_ref[:, 128:256]

        @pl.when(ki == n - 1)
        def _():
            dqb_ref[:, 0:128] = dq_ref[:, 0:128].astype(BF16)
            dqr = dq_ref[:, 128:256]
            dqb_ref[:, 128:256] = (dqr * cos_ref[...] - _rot_sum(dqr) * sin_ref[...]).astype(BF16)

    def whole(w):
        return pl.BlockSpec((S, w), lambda h, ki: (0, h))

    def krow(w):
        return pl.BlockSpec((t, w), lambda h, ki: (ki, h))

    stat = pl.BlockSpec((None, n, t), lambda h, ki: (h, 0, 0))
    table = pl.BlockSpec((S, 128), lambda h, ki: (0, 0))
    return _pcall(kern, name=name,
                  out_shape=(jax.ShapeDtypeStruct((S, 2048), BF16), jax.ShapeDtypeStruct((S, 2048), BF16),
                             jax.ShapeDtypeStruct((S, D_MLA), F32)),
                  grid=(N_HEADS, n),
                  in_specs=[whole(256), krow(256), krow(128), whole(128), stat, stat, table, table],
                  out_specs=(whole(256), krow(256), krow(128)),
                  scratch=[pltpu.VMEM((S, 256), F32), pltpu.VMEM((t, 256), F32), pltpu.VMEM((t, 128), F32)],
                  dims=("parallel", "arbitrary"), vmem_mb=56, comm=comm)(qc, kc, v, do, lse2, delta, cos_t, sin_t)


def _mixer_specs(S, tm):
    hb = tm // HALO
    last_hb = S // HALO - 1

    def main(w, blk):
        return pl.BlockSpec((tm, w), lambda i: (i, blk))

    def prev(w, blk):
        return pl.BlockSpec((HALO, w), lambda i: (jnp.maximum(i * hb - 1, 0), blk))

    def nxt(w, blk):
        return pl.BlockSpec((HALO, w), lambda i: (jnp.minimum((i + 1) * hb, last_hb), blk))

    def full(shape):
        return pl.BlockSpec(shape, lambda i: (0,) * len(shape))

    return main, prev, nxt, full


def _fill_halo(i, xp, xu, hp_ref, hch_ref, hcc_ref, pin_ref, ch_ref, cc_ref, tm):
    first = i == 0
    xp[0:HALO, :] = jnp.where(first, 0.0, hp_ref[...])
    xp[HALO:HALO + tm, :] = pin_ref[...]
    xu[0:HALO, :] = jnp.where(first, 0.0, hch_ref[...] * hcc_ref[...])
    xu[HALO:HALO + tm, :] = cc_ref[...] * ch_ref[...]


def _pooled(xp, g, t1, tm):
    w = POOL_WINDOWS[g]
    lanes = slice(128 * g, 128 * g + 128)
    x0 = xp[HALO:HALO + tm, lanes]
    acc = x0
    for k in range(1, w):
        acc = acc + xp[HALO - k:HALO - k + tm, lanes]
    return acc / jnp.minimum(t1, float(w)) - x0


def _conv_fwd(xu, cw_ref, tm):
    return (cw_ref[0:1, :] * xu[HALO - 2:HALO - 2 + tm, :] + cw_ref[1:2, :] * xu[HALO - 1:HALO - 1 + tm, :]
            + cw_ref[2:3, :] * xu[HALO:HALO + tm, :])


def _mixer_fwd(proj, o, wpool, ps, convw, *, name):
    S = proj.shape[0]
    tm = min(512, S)
    main, prev, _, full = _mixer_specs(S, tm)

    def kern(gm_ref, pin_ref, gp_ref, ch_ref, cb_ref, cc_ref, gc_ref, hp_ref, hch_ref, hcc_ref,
             o_ref, wp_ref, ps_ref, cw_ref, mix_ref, xp, xu):
        i = pl.program_id(0)
        _fill_halo(i, xp, xu, hp_ref, hch_ref, hcc_ref, pin_ref, ch_ref, cc_ref, tm)
        t1 = (i * tm + lax.broadcasted_iota(jnp.int32, (tm, 1), 0) + 1).astype(F32)
        for g in range(4):
            lanes = slice(128 * g, 128 * g + 128)
            pooled = _pooled(xp, g, t1, tm)
            z = jnp.dot(pooled.astype(BF16), wp_ref[g].astype(BF16), preferred_element_type=F32)
            gp = gp_ref[:, lanes]
            y = z * ps_ref[:, lanes] * (gp * _sigmoid(gp))
            mix_ref[:, 1024 + 128 * g:1024 + 128 * g + 128] = y.astype(BF16)
        gc = gc_ref[...]
        mix_ref[:, 1536:2048] = (cb_ref[...] * _conv_fwd(xu, cw_ref, tm) * (gc * _sigmoid(gc))).astype(BF16)
        gm = gm_ref[...]
        mix_ref[:, 0:1024] = (o_ref[...] * (gm * _sigmoid(gm))).astype(BF16)

    return _pcall(kern, name=name, out_shape=jax.ShapeDtypeStruct((S, 2048), BF16), grid=(S // tm,),
                  in_specs=[main(1024, 1), main(512, 4), main(512, 5), main(512, 6), main(512, 7), main(512, 8),
                            main(512, 9), prev(512, 4), prev(512, 6), prev(512, 8),
                            main(1024, 0), full((4, 128, 128)), full((1, 512)), full((3, 512))],
                  out_specs=main(2048, 0),
                  scratch=[pltpu.VMEM((tm + HALO, 512), F32), pltpu.VMEM((tm + HALO, 512), F32)],
                  dims=("parallel",), vmem_mb=48)(
                      proj, proj, proj, proj, proj, proj, proj, proj, proj, proj, o, wpool, ps.reshape(1, 512), convw)


def _mixer_bwd(dmix, proj, o, wpool, ps, convw, *, name):
    S = proj.shape[0]
    tm = min(512, S)
    n = S // tm
    t = min(TQ, S)
    per = t // tm
    main, prev, nxt, full = _mixer_specs(S, tm)

    def kern(dm_ref, dmn_ref, gm_ref, pin_ref, gp_ref, ch_ref, cb_ref, cc_ref, gc_ref,
             hp_ref, hch_ref, hcc_ref, gpn_ref, cbn_ref, gcn_ref, o_ref, wp_ref, ps_ref, cw_ref,
             d_ref, do_ref, dl_ref, dwp_ref, dps_ref, dcw_ref, xp, xu, ee, ed):
        i = pl.program_id(0)
        last = i == n - 1

        @pl.when(i == 0)
        def _():
            dwp_ref[...] = jnp.zeros_like(dwp_ref)
            dps_ref[...] = jnp.zeros_like(dps_ref)
            dcw_ref[...] = jnp.zeros_like(dcw_ref)

        _fill_halo(i, xp, xu, hp_ref, hch_ref, hcc_ref, pin_ref, ch_ref, cc_ref, tm)
        t1 = (i * tm + lax.broadcasted_iota(jnp.int32, (tm, 1), 0) + 1).astype(F32)
        t1n = ((i + 1) * tm + lax.broadcasted_iota(jnp.int32, (HALO, 1), 0) + 1).astype(F32)
        c_pin, c_gp, c_ch, c_cb, c_cc, c_gc = 1024, 1536, 2048, 2560, 3072, 3584

        for g in range(4):
            w = float(POOL_WINDOWS[g])
            lanes = slice(128 * g, 128 * g + 128)
            pooled = _pooled(xp, g, t1, tm)
            pb = pooled.astype(BF16)
            wp = wp_ref[g].astype(BF16)
            z = jnp.dot(pb, wp, preferred_element_type=F32)
            psl = ps_ref[:, lanes]
            sg, dsg = _silu_and_grad(gp_ref[:, lanes])
            dmp = dm_ref[:, 1024 + 128 * g:1024 + 128 * g + 128]
            dyp = dmp * sg
            d_ref[:, c_gp + 128 * g:c_gp + 128 * g + 128] = (dmp * (z * psl) * dsg).astype(BF16)
            dps_ref[:, lanes] += jnp.sum(dyp * z, axis=0, keepdims=True)
            dz = (dyp * psl).astype(BF16)
            dwp_ref[g] += lax.dot_general(pb, dz, TN, preferred_element_type=F32)
            dpl = lax.dot_general(dz, wp, NT, preferred_element_type=F32)
            ee[0:tm, lanes] = dpl / jnp.minimum(t1, w)
            gpn = gpn_ref[:, lanes]
            dzn = (dmn_ref[:, lanes] * (gpn * _sigmoid(gpn)) * psl).astype(BF16)
            dpn = lax.dot_general(dzn, wp, NT, preferred_element_type=F32)
            ee[tm:tm + HALO, lanes] = jnp.where(last, 0.0, dpn / jnp.minimum(t1n, w))
            acc = ee[0:tm, lanes]
            for k in range(1, POOL_WINDOWS[g]):
                acc = acc + ee[k:k + tm, lanes]
            d_ref[:, c_pin + 128 * g:c_pin + 128 * g + 128] = (acc - dpl).astype(BF16)

        yc = _conv_fwd(xu, cw_ref, tm)
        sgc, dsgc = _silu_and_grad(gc_ref[...])
        cb = cb_ref[...]
        dmc = dm_ref[:, 1536:2048]
        d_ref[:, c_gc:c_gc + 512] = (dmc * cb * yc * dsgc).astype(BF16)
        d_ref[:, c_cb:c_cb + 512] = (dmc * yc * sgc).astype(BF16)
        dyc = dmc * cb * sgc
        ed[0:tm, :] = dyc
        gcn = gcn_ref[...]
        ed[tm:tm + HALO, :] = jnp.where(last, 0.0, dmn_ref[:, 512:1024] * cbn_ref[...] * (gcn * _sigmoid(gcn)))
        dcw_ref[0:1, :] += jnp.sum(dyc * xu[HALO - 2:HALO - 2 + tm, :], axis=0, keepdims=True)
        dcw_ref[1:2, :] += jnp.sum(dyc * xu[HALO - 1:HALO - 1 + tm, :], axis=0, keepdims=True)
        dcw_ref[2:3, :] += jnp.sum(dyc * xu[HALO:HALO + tm, :], axis=0, keepdims=True)
        du = cw_ref[2:3, :] * dyc + cw_ref[1:2, :] * ed[1:1 + tm, :] + cw_ref[0:1, :] * ed[2:2 + tm, :]
        d_ref[:, c_cc:c_cc + 512] = (du * ch_ref[...]).astype(BF16)
        d_ref[:, c_ch:c_ch + 512] = (du * cc_ref[...]).astype(BF16)

        sgm, dsgm = _silu_and_grad(gm_ref[...])
        dmm = dm_ref[:, 0:1024]
        ov = o_ref[...]
        dov = dmm * sgm
        do_ref[...] = dov
        d_ref[:, 0:1024] = (dmm * ov * dsgm).astype(BF16)
        lane = lax.broadcasted_iota(jnp.int32, (tm, LANE), 1)
        dmat = jnp.zeros((tm, LANE), F32)
        for h in range(N_HEADS):
            hs = slice(128 * h, 128 * h + 128)
            dmat = jnp.where(lane == h, jnp.sum(dov[:, hs] * ov[:, hs], axis=1, keepdims=True), dmat)
        dmat_t = jnp.transpose(dmat)
        for part in range(per):
            @pl.when(i % per == part)
            def _():
                for h in range(N_HEADS):
                    dl_ref[h, pl.ds(i // per, 1), part * tm:(part + 1) * tm] = dmat_t[h:h + 1, :]

    outs = (jax.ShapeDtypeStruct((S, W_MIX), BF16), jax.ShapeDtypeStruct((S, 1024), F32),
            jax.ShapeDtypeStruct((N_HEADS, S // t, t), F32),
            jax.ShapeDtypeStruct((4, 128, 128), F32), jax.ShapeDtypeStruct((1, 512), F32),
            jax.ShapeDtypeStruct((3, 512), F32))
    scr = [pltpu.VMEM((tm + HALO, 512), F32) for _ in range(4)]
    return _pcall(kern, name=name, out_shape=outs, grid=(n,),
                  in_specs=[main(2048, 0), nxt(1024, 1),
                            main(1024, 1), main(512, 4), main(512, 5), main(512, 6), main(512, 7), main(512, 8),
                            main(512, 9), prev(512, 4), prev(512, 6), prev(512, 8),
                            nxt(512, 5), nxt(512, 7), nxt(512, 9),
                            main(1024, 0), full((4, 128, 128)), full((1, 512)), full((3, 512))],
                  out_specs=(main(W_MIX, 0), main(1024, 0), full((N_HEADS, S // t, t)), full((4, 128, 128)),
                             full((1, 512)), full((3, 512))),
                  scratch=scr, dims=("arbitrary",), vmem_mb=60)(
                      dmix, dmix, proj, proj, proj, proj, proj, proj, proj, proj, proj, proj, proj, proj, proj,
                      o, wpool, ps.reshape(1, 512), convw)


def _outproj_residual(mix, wout, h, bout, *, name):
    S, Dm = h.shape
    tm = min(512, S)

    def kern(mix_ref, w_ref, h_ref, bo_ref, r_ref):
        out = jnp.dot(mix_ref[...], w_ref[...], preferred_element_type=F32) + bo_ref[...]
        r_ref[...] = ALPHA * h_ref[...] + out

    row = pl.BlockSpec((tm, Dm), lambda i: (i, 0))
    vec = pl.BlockSpec((1, Dm), lambda i: (0, 0))
    wsp = pl.BlockSpec((Dm, Dm), lambda i: (0, 0), pipeline_mode=pl.Buffered(1))
    return _pcall(kern, name=name, out_shape=jax.ShapeDtypeStruct((S, Dm), F32), grid=(S // tm,),
                  in_specs=[row, wsp, row, vec], out_specs=row, dims=("parallel",), vmem_mb=56)(
                      mix, wout, h, bout.reshape(1, Dm))


def _outproj_ln(mix, wout, h, bout, g, b, *, name):
    S, Dm = h.shape
    tm = min(512, S)

    def kern(mix_ref, w_ref, h_ref, bo_ref, g_ref, b_ref, y_ref, yb_ref, r_ref):
        out = jnp.dot(mix_ref[...], w_ref[...], preferred_element_type=F32) + bo_ref[...]
        r = ALPHA * h_ref[...] + out
        r_ref[...] = r
        mu = jnp.mean(r, axis=-1, keepdims=True)
        xc = r - mu
        var = jnp.mean(xc * xc, axis=-1, keepdims=True)
        y = xc * lax.rsqrt(var + LN_EPS) * g_ref[...] + b_ref[...]
        y_ref[...] = y
        yb_ref[...] = y.astype(BF16)

    row = pl.BlockSpec((tm, Dm), lambda i: (i, 0))
    vec = pl.BlockSpec((1, Dm), lambda i: (0, 0))
    wsp = pl.BlockSpec((Dm, Dm), lambda i: (0, 0), pipeline_mode=pl.Buffered(1))
    sds = jax.ShapeDtypeStruct((S, Dm), F32)
    return _pcall(kern, name=name, out_shape=(sds, jax.ShapeDtypeStruct((S, Dm), BF16), sds), grid=(S // tm,),
                  in_specs=[row, wsp, row, vec, vec, vec], out_specs=(row, row, row), dims=("parallel",),
                  vmem_mb=56)(
                      mix, wout, h, bout.reshape(1, Dm), g.reshape(1, Dm), b.reshape(1, Dm))


def _adamw_math(w, g, m, v):
    m = ADAM_B1 * m + (1.0 - ADAM_B1) * g
    v = ADAM_B2 * v + (1.0 - ADAM_B2) * (g * g)
    m_hat = m / (1.0 - ADAM_B1 ** ADAM_STEP)
    v_hat = v / (1.0 - ADAM_B2 ** ADAM_STEP)
    delta = -ADAM_LR * (m_hat / (jnp.sqrt(v_hat) + ADAM_EPS) + ADAM_WD * w)
    return delta, m, v


def _row_tile(R, C):
    best = None
    for cand in range(8, R, 8):
        if R % cand == 0 and cand * C <= 256 * 1024:
            best = cand
    return best if best is not None else R


def _adamw(w, g, m, v, *, name):
    shape = w.shape
    C = shape[-1]
    R = 1
    for s in shape[:-1]:
        R *= s
    tr = _row_tile(R, C)

    def kern(w_ref, g_ref, m_ref, v_ref, d_ref, mo_ref, vo_ref):
        d, mn, vn = _adamw_math(w_ref[...], g_ref[...], m_ref[...], v_ref[...])
        d_ref[...] = d
        mo_ref[...] = mn
        vo_ref[...] = vn

    blk = pl.BlockSpec((tr, C), lambda i: (i, 0))
    sds = jax.ShapeDtypeStruct((R, C), F32)
    outs = _pcall(kern, name=name, out_shape=(sds, sds, sds), grid=(R // tr,), in_specs=[blk] * 4,
                  out_specs=(blk, blk, blk), dims=("parallel",), vmem_mb=48)(
                      w.reshape(R, C), g.reshape(R, C), m.reshape(R, C), v.reshape(R, C))
    return tuple(t.reshape(shape) for t in outs)


def _adamw_halves(w, m, v, halves, c_idx, *, name, comm=None):
    _, R, C = w.shape
    ch = C // 2
    tr = _row_tile(R, ch)
    nb = R // tr

    def kern(c_ref, w_ref, a0_ref, b0_ref, a1_ref, b1_ref, m_ref, v_ref, g_ref, d_ref, mo_ref, vo_ref):
        layer = pl.program_id(0) // nb
        mine = pl.program_id(1) == c_ref[0]
        g = jnp.where(layer == 0, jnp.where(mine, a0_ref[...], b0_ref[...]),
                      jnp.where(mine, a1_ref[...], b1_ref[...]))
        g_ref[...] = g
        d, mn, vn = _adamw_math(w_ref[...], g, m_ref[...], v_ref[...])
        d_ref[...] = d
        mo_ref[...] = mn
        vo_ref[...] = vn

    full = pl.BlockSpec((tr, ch), lambda i, hc: (i, hc))
    half = pl.BlockSpec((tr, ch), lambda i, hc: (i % nb, 0))
    sds = jax.ShapeDtypeStruct((2 * R, C), F32)
    (a0, b0), (a1, b1) = halves
    res = _pcall(kern, name=name, out_shape=(sds,) * 4, grid=(2 * nb, 2),
                 in_specs=[pl.BlockSpec(memory_space=pltpu.SMEM), full, half, half, half, half, full, full],
                 out_specs=(full,) * 4, dims=("parallel", "parallel"), vmem_mb=48, comm=comm)(
                     c_idx, w.reshape(2 * R, C), a0, b0, a1, b1, m.reshape(2 * R, C), v.reshape(2 * R, C))
    outs, landed = res if comm is not None else (res, None)
    outs = tuple(t.reshape(2, R, C) for t in outs)
    return outs if comm is None else (outs, landed)


def _packed_pieces(shape):
    if len(shape) == 4:
        return [((l * shape[1] + g) * 128, 128, (l, g)) for l in range(shape[0]) for g in range(shape[1])]
    per_row = shape[1] // LANE
    return [(a * per_row + j, 1, (slice(a, a + 1), slice(LANE * j, LANE * (j + 1))))
            for a in range(shape[0]) for j in range(per_row)]


def _small_sum_adamw(gathered, own, weights, *, name):
    R = gathered.shape[1]
    nw = len(weights)
    shapes = [w.shape for w, _, _ in weights]
    first_row, r0 = [], 0
    for shp in shapes:
        first_row.append(r0)
        n = 1
        for s in shp:
            n *= s
        r0 += n // LANE

    def kern(ga_ref, own_ref, *refs):
        ins, gsum_ref, outs = refs[:3 * nw], refs[3 * nw], refs[3 * nw + 1:]
        me = 4 * lax.axis_index("x") + 2 * lax.axis_index("y") + lax.axis_index("c")

        def block(k):
            other = ga_ref[jnp.where(me == k, (k + 1) % N_DEV, k)]
            return jnp.where(me == k, own_ref[...], other)

        g = block(0)
        for k in range(1, N_DEV):
            g = g + block(k)
        gsum_ref[...] = g
        for p, shp in enumerate(shapes):
            w_ref, m_ref, v_ref = ins[3 * p:3 * p + 3]
            g_out, d_out, m_out, v_out = outs[4 * p:4 * p + 4]
            for row, rows, idx in _packed_pieces(shp):
                gp = gsum_ref[first_row[p] + row:first_row[p] + row + rows, :]
                d, mn, vn = _adamw_math(w_ref[idx], gp, m_ref[idx], v_ref[idx])
                g_out[idx] = gp
                d_out[idx] = d
                m_out[idx] = mn
                v_out[idx] = vn

    out_shape = [jax.ShapeDtypeStruct((R, LANE), F32)]
    for shp in shapes:
        out_shape += [jax.ShapeDtypeStruct(shp, F32)] * 4
    flat = [a for wmv in weights for a in wmv]
    res = _pcall(kern, name=name, out_shape=tuple(out_shape), vmem_mb=48)(gathered, own, *flat)
    return res[0], [tuple(res[1 + 4 * p:5 + 4 * p]) for p in range(nw)]


def _pair_sums(grads, theirs, c_idx, *, name):
    n = len(grads)
    steps = 8
    tiles = [(g.shape[0] // steps, g.shape[1] // 2) for g in grads]

    def kern(c_ref, *refs):
        for a in range(n):
            refs[2 * n + a][...] = (refs[a][...] + refs[n + a][...]).astype(BF16)

    gs = pltpu.PrefetchScalarGridSpec(
        num_scalar_prefetch=1, grid=(steps,),
        in_specs=[pl.BlockSpec(tl, lambda i, c: (i, c[0])) for tl in tiles]
        + [pl.BlockSpec(tl, lambda i, c: (i, 0)) for tl in tiles],
        out_specs=tuple(pl.BlockSpec(tl, lambda i, c: (i, 0)) for tl in tiles))
    out_shape = tuple(jax.ShapeDtypeStruct((g.shape[0], g.shape[1] // 2), BF16) for g in grads)
    return pl.pallas_call(kern, name=name, out_shape=out_shape, grid_spec=gs,
                          compiler_params=pltpu.CompilerParams(dimension_semantics=("parallel",),
                                                               vmem_limit_bytes=48 << 20))(c_idx, *grads, *theirs)


WeightRows = collections.namedtuple("WeightRows", "full_rows own_rows cols pieces zero_rows")


def _w_in_piece_a(j):
    return jnp.where(j == 0, 0, 1232 * j + GAP)


def _w_in_piece_b(j):
    return jnp.where(j == 0, GAP_AT + GAP, 1232 * j + GAP_AT + GAP)


W_IN = WeightRows(NP, 1232, D_MODEL, ((0, GAP_AT, _w_in_piece_a), (GAP_AT, 1232 - GAP_AT, _w_in_piece_b)),
                  ((GAP_AT, GAP),))
W_OUT = WeightRows(2048, 512, D_MODEL, ((0, 512, lambda j: 512 * j),), ())
W_UQ = WeightRows(2048, 384, Q_LORA, ((0, 192, lambda j: 512 * j), (192, 192, lambda j: 512 * j + 256)),
                  tuple((256 * h + 192, 64) for h in range(N_HEADS)))
W_UKV = WeightRows(2048, 512, KV_LORA, ((0, 512, lambda j: 512 * j),), ())
W_CONV = WeightRows(64, 16, 256, ((0, 16, lambda j: 16 * j),), ())
SHARDED = (W_IN, W_OUT, W_UQ, W_UKV)
SHARDED_NAMES = ("w_in", "w_out", "w_uq", "w_ukv")
WEIGHT_ROWS = dict(zip(SHARDED_NAMES, SHARDED))


def _mesh_pos():
    x, y, c = lax.axis_index("x"), lax.axis_index("y"), lax.axis_index("c")
    return x, y, c


def _other_chips(x, y):
    return [(1 - x, y), (x, 1 - y), (1 - x, 1 - y)]


def _rows(start, n):
    return pl.ds(pl.multiple_of(start, 16), n)


def _half_cols(spec, c):
    ch = spec.cols // 2
    return pl.ds(pl.multiple_of(c * ch, LANE), ch)


def _allgather_script(specs, shards, zeros, layers):
    na = len(specs)
    zlist = [a for a in range(na) if zeros[a] is not None]
    n_layers = [shards[a].shape[0] if layers[a] is None else 1 for a in range(na)]
    plan_first, plan_own, plan_zero = [], [], []
    for a, spec in enumerate(specs):
        for p in range(len(spec.pieces)):
            plan_own.append((a, p))
            for k in range(3):
                plan_first.append((a, p, k))
        for z in range(len(spec.zero_rows)):
            for l in range(n_layers[a]):
                plan_zero.append((a, z, l))
    nf = len(plan_first)
    n_sems = 2 * nf + len(plan_own) + len(plan_zero)

    def copies(ins_all, outs, send_sems, recv_sems):
        ins = [ins_all[a] if layers[a] is None else ins_all[a].at[pl.ds(layers[a], 1)] for a in range(na)]
        zrefs = dict(zip(zlist, ins_all[na:]))
        x, y, c = _mesh_pos()
        j = 2 * x + y
        chips = _other_chips(x, y)
        sibling = (x, y, 1 - c)

        def remote(src, dst, sem, to):
            return pltpu.make_async_remote_copy(src_ref=src, dst_ref=dst, send_sem=send_sems.at[sem],
                                                recv_sem=recv_sems.at[sem], device_id=to, device_id_type=MESH)

        def block(a, p, chip, cols):
            _, n, dst = specs[a].pieces[p]
            return outs[a].at[:, _rows(dst(chip), n), cols]

        def first(i):
            a, p, k = plan_first[i]
            src0, n, _ = specs[a].pieces[p]
            cols = _half_cols(specs[a], c)
            return remote(ins[a].at[:, pl.ds(src0, n), cols], block(a, p, j, cols), i, (*chips[k], c))

        def landed(i, half):
            a, p, k = plan_first[i]
            return block(a, p, 2 * chips[k][0] + chips[k][1], _half_cols(specs[a], half))

        def arrival(i, half, sem):
            return remote(landed(i, half), landed(i, half), sem, sibling)

        def passed(i):
            return remote(landed(i, c), landed(i, c), nf + i, sibling)

        def own(i):
            a, p = plan_own[i]
            src0, n, _ = specs[a].pieces[p]
            return remote(ins[a].at[:, pl.ds(src0, n), :], block(a, p, j, slice(None)), 2 * nf + i, sibling)

        def zero(i):
            a, z, l = plan_zero[i]
            r0, n = specs[a].zero_rows[z]
            return remote(zrefs[a].at[pl.ds(0, n), :], outs[a].at[l, pl.ds(r0, n), :],
                          2 * nf + len(plan_own) + i, sibling)

        fixed = [own(i) for i in range(len(plan_own))] + [zero(i) for i in range(len(plan_zero))]
        return c, fixed, first, arrival, passed

    def start(ins, outs, send_sems, recv_sems):
        _, fixed, first, _, _ = copies(ins, outs, send_sems, recv_sems)
        for cp in fixed:
            cp.start()
        for i in range(nf):
            first(i).start()

    def finish(ins, outs, send_sems, recv_sems):
        c, fixed, first, arrival, passed = copies(ins, outs, send_sems, recv_sems)
        for i in range(nf):
            arrival(i, c, i).wait_recv()
            passed(i).start()
        for i in range(nf):
            arrival(i, 1 - c, nf + i).wait_recv()
        for cp in fixed:
            cp.wait()
        for i in range(nf):
            first(i).wait_send()
            passed(i).wait_send()

    out_shape = tuple(jax.ShapeDtypeStruct((n_layers[a], spec.full_rows, spec.cols), BF16)
                      for a, spec in enumerate(specs))
    args = tuple(shards) + tuple(zeros[a] for a in zlist)
    return CommScript(args, out_shape, n_sems, start, finish)


def _start_all_wait_all(args, out_shape, n_sems, make_copies):
    def start(ins, outs, send_sems, recv_sems):
        for cp in make_copies(ins, outs, send_sems, recv_sems):
            cp.start()

    def finish(ins, outs, send_sems, recv_sems):
        for cp in make_copies(ins, outs, send_sems, recv_sems):
            cp.wait()

    return CommScript(tuple(args), tuple(out_shape), n_sems, start, finish)


def _exchange_script(specs, grads):
    na = len(grads)

    def make_copies(ins, outs, send_sems, recv_sems):
        x, y, c = _mesh_pos()
        return [pltpu.make_async_remote_copy(
            src_ref=ins[a].at[:, _half_cols(specs[a], 1 - c)], dst_ref=outs[a], send_sem=send_sems.at[a],
            recv_sem=recv_sems.at[a], device_id=(x, y, 1 - c), device_id_type=MESH) for a in range(na)]

    out_shape = [jax.ShapeDtypeStruct((s.full_rows, s.cols // 2), F32) for s in specs]
    return _start_all_wait_all(grads, out_shape, na, make_copies)


def _scatter_script(specs, parts):
    na = len(parts)
    plan = [(a, p, k) for a in range(na) for p in range(len(specs[a].pieces)) for k in range(3)]

    def make_copies(ins, outs, send_sems, recv_sems):
        x, y, c = _mesh_pos()
        chips = _other_chips(x, y)
        copies = []
        for i, (a, p, k) in enumerate(plan):
            src0, n, dst = specs[a].pieces[p]
            pk = 2 * chips[k][0] + chips[k][1]
            copies.append(pltpu.make_async_remote_copy(
                src_ref=ins[a].at[_rows(dst(pk), n), :], dst_ref=outs[a].at[k, pl.ds(src0, n), :],
                send_sem=send_sems.at[i], recv_sem=recv_sems.at[i], device_id=(*chips[k], c), device_id_type=MESH))
        return copies

    out_shape = [jax.ShapeDtypeStruct((3, s.own_rows, s.cols // 2), BF16) for s in specs]
    return _start_all_wait_all(parts, out_shape, len(plan), make_copies)


def _chip_sums(specs, parts, recvs, *, name):
    n = len(specs)
    plan = [(a, p) for a in range(n) for p in range(len(specs[a].pieces))]

    def kern(*refs):
        recv_refs, part_refs, o_refs = refs[:n], refs[n:2 * n], refs[2 * n:3 * n]
        own_refs, sems = refs[3 * n:4 * n], refs[4 * n]
        j = 2 * lax.axis_index("x") + lax.axis_index("y")
        copies = []
        for i, (a, p) in enumerate(plan):
            src0, rows, dst = specs[a].pieces[p]
            copies.append(pltpu.make_async_copy(part_refs[a].at[_rows(dst(j), rows), :],
                                                own_refs[a].at[pl.ds(src0, rows), :], sems.at[i]))
        for cp in copies:
            cp.start()
        for cp in copies:
            cp.wait()
        for a in range(n):
            r = recv_refs[a]
            o_refs[a][...] = ((own_refs[a][...].astype(F32) + r[0].astype(F32)) + r[1].astype(F32)) \
                + r[2].astype(F32)

    vm = pl.BlockSpec(memory_space=pltpu.VMEM)
    shapes = [(s.own_rows, s.cols // 2) for s in specs]
    return _pcall(kern, name=name, out_shape=tuple(jax.ShapeDtypeStruct(shp, F32) for shp in shapes),
                  in_specs=[vm] * n + [HBM_SPEC] * n, out_specs=(vm,) * n,
                  scratch=[pltpu.VMEM(shp, BF16) for shp in shapes] + [pltpu.SemaphoreType.DMA((len(plan),))],
                  vmem_mb=56)(*recvs, *parts)


def _sibling_script(sums):
    na = len(sums)

    def make_copies(ins, outs, send_sems, recv_sems):
        x, y, c = _mesh_pos()
        return [pltpu.make_async_remote_copy(
            src_ref=ins[a], dst_ref=outs[a], send_sem=send_sems.at[a], recv_sem=recv_sems.at[a],
            device_id=(x, y, 1 - c), device_id_type=MESH) for a in range(na)]

    out_shape = [jax.ShapeDtypeStruct(t.shape, t.dtype) for t in sums]
    return _start_all_wait_all(sums, out_shape, na, make_copies)


class _SemWindow:
    def __init__(self, sems, offset):
        self._sems, self._offset = sems, offset

    @property
    def at(self):
        return self

    def __getitem__(self, i):
        return self._sems.at[i + self._offset]


def _merge_scripts(*scripts):
    a_off, o_off, s_off = [0], [0], [0]
    for s in scripts:
        a_off.append(a_off[-1] + len(s.args))
        o_off.append(o_off[-1] + len(s.out_shape))
        s_off.append(s_off[-1] + s.n_sems)

    def phase(which):
        def run(ins, outs, send_sems, recv_sems):
            for n, s in enumerate(scripts):
                getattr(s, which)(ins[a_off[n]:a_off[n + 1]], outs[o_off[n]:o_off[n + 1]],
                                  _SemWindow(send_sems, s_off[n]), _SemWindow(recv_sems, s_off[n]))
        return run

    return CommScript(sum((tuple(s.args) for s in scripts), ()), sum((tuple(s.out_shape) for s in scripts), ()),
                      s_off[-1], phase("start"), phase("finish"))


class _GradReducer:
    def __init__(self, layer, names, grads, c_idx):
        self.specs = tuple(WEIGHT_ROWS[nm] for nm in names)
        self.grads, self.c_idx = tuple(grads), c_idx
        self.names = [f"{nm}{layer}" for nm in names]

    def exchange(self):
        return _exchange_script(self.specs, self.grads)

    def scatter(self, theirs):
        self.parts = tuple(_pair_sums(self.grads, tuple(theirs), self.c_idx, name=f"pair_sums_{self.names[0]}"))
        return _scatter_script(self.specs, self.parts)

    def sibling(self, recv):
        self.sums = tuple(_chip_sums(self.specs, self.parts, tuple(recv), name=f"chip_sums_{self.names[0]}"))
        return _sibling_script(self.sums)

    def done(self, others):
        return list(zip(self.sums, others))


def _allgather_small_script(block):
    m_per, n = block.shape

    def copies(ins, outs, send_sems, recv_sems):
        (x_ref,), (out_ref,) = ins, outs
        x, y, c = _mesh_pos()
        me, sibling = (x, y, c), (x, y, 1 - c)
        chips = _other_chips(x, y)

        def rows(px, py, pc):
            return out_ref.at[4 * px + 2 * py + pc]

        def copy(k, blk, to, src=None):
            return pltpu.make_async_remote_copy(
                src_ref=rows(*blk) if src is None else src, dst_ref=rows(*blk), send_sem=send_sems.at[k],
                recv_sem=recv_sems.at[k], device_id=to, device_id_type=MESH)

        first = [copy(0, me, sibling, src=x_ref)]
        first += [copy(1 + k, me, (*chip, c), src=x_ref) for k, chip in enumerate(chips)]
        passed = [copy(4 + k, (*chip, c), sibling) for k, chip in enumerate(chips)]
        landed = [copy(1 + k, (*chip, c), me) for k, chip in enumerate(chips)]
        from_sibling = [copy(0, sibling, me)] + [copy(4 + k, (*chip, 1 - c), me) for k, chip in enumerate(chips)]
        return first, passed, landed, from_sibling

    def start(ins, outs, send_sems, recv_sems):
        first, _, _, _ = copies(ins, outs, send_sems, recv_sems)
        for cp in first:
            cp.start()

    def finish(ins, outs, send_sems, recv_sems):
        first, passed, landed, from_sibling = copies(ins, outs, send_sems, recv_sems)
        for k in range(3):
            landed[k].wait_recv()
            passed[k].start()
        for cp in from_sibling:
            cp.wait_recv()
        for cp in first + passed:
            cp.wait_send()

    return CommScript((block,), (jax.ShapeDtypeStruct((N_DEV, m_per, n), block.dtype),), 7, start, finish)


def _rope_tables(positions):
    half = ROPE // 2
    inv_freq = ROPE_THETA ** (-jnp.arange(half, dtype=F32) / half)
    ang = positions.astype(F32)[:, None] * inv_freq
    cos, sin = jnp.cos(ang), jnp.sin(ang)
    S = positions.shape[0]
    cos_t = jnp.concatenate([cos, cos, jnp.ones((S, 64), F32)], axis=1)
    sin_t = jnp.concatenate([-sin, sin, jnp.zeros((S, 64), F32)], axis=1)
    return cos_t, sin_t


def _decode_conv(bits):
    rows = bits.reshape(DEPTH, N_CHIPS, 16, 256)[:, :, :3, :]
    conv = lax.bitcast_convert_type(rows.reshape(DEPTH, N_CHIPS, 3, 128, 2), F32)
    return jnp.transpose(conv, (0, 2, 1, 3)).reshape(DEPTH, 3, 512)


def _local_step(x, positions, target, emb_g, emb_b, w_in_t0, rest0, weights1, q_g, kv_g, w_pool, pool_scale,
                b_out, ln_g, ln_b, c_idx=None):
    cos_t, sin_t = _rope_tables(positions)
    if isinstance(w_in_t0, CommScript):
        (h, hb), (landed,) = _ln_fwd(x, emb_g, emb_b, name="emb_ln", comm=w_in_t0)
        w_in_t0 = landed[0]
    else:
        h, hb = _ln_fwd(x, emb_g, emb_b, name="emb_ln")
    weights = [None, weights1]
    saved = []
    for l in range(DEPTH):
        if l == 0 and isinstance(rest0, CommScript):
            proj, landed = _matmul(hb, w_in_t0, "nt", name="in_proj0", tm=1024, tn=1024, tk=2048, vmem_mb=56,
                                   comm=rest0)
            weights[0] = (w_in_t0,) + tuple(a[0] for a in landed[:3])
            conv_w = _decode_conv(landed[3])
        else:
            if l == 0:
                weights[0] = (w_in_t0,) + tuple(rest0[:3])
                conv_w = rest0[3]
            proj = _matmul(hb, weights[l][0], "nt", name=f"in_proj{l}", tm=1024, tn=1024, tk=2048, vmem_mb=56)
        w_in_t, w_out, w_uq_t, w_ukv_t = weights[l]
        qc, kc, v, vt, qn, kvn = _mla_qkv(proj, cos_t, sin_t, q_g[l], kv_g[l], w_uq_t, w_ukv_t, name=f"mla_qkv{l}")
        nxt = weights[l + 1] if l + 1 < DEPTH else None
        if isinstance(nxt, CommScript):
            (o, lse2), landed = _flash_fwd(qc, kc, vt, name=f"flash_fwd{l}", comm=nxt)
            weights[l + 1] = tuple(a[0] for a in landed)
        else:
            o, lse2 = _flash_fwd(qc, kc, vt, name=f"flash_fwd{l}")
        mix = _mixer_fwd(proj, o, w_pool[l], pool_scale[l], conv_w[l], name=f"mixer_fwd{l}")
        if l == DEPTH - 1:
            r = _outproj_residual(mix, w_out, h, b_out[l], name=f"out_proj{l}")
            saved.append((hb, proj, qc, kc, v, qn, kvn, o, lse2, mix, r))
        else:
            h_next, hb_next, r = _outproj_ln(mix, w_out, h, b_out[l], ln_g[l], ln_b[l], name=f"out_proj_ln{l}")
            saved.append((hb, proj, qc, kc, v, qn, kvn, o, lse2, mix, r))
            h, hb = h_next, hb_next

    small = [None] * DEPTH
    big = [None] * DEPTH
    above = scatter_above = None
    for l in reversed(range(DEPTH)):
        w_in_t, w_out, w_uq_t, w_ukv_t = weights[l]
        hb_in, proj, qc, kc, v, qn, kvn, o, lse2, mix, r = saved[l]
        if l == DEPTH - 1:
            loss_acc, dr, drb, d_ln_g, d_ln_b, d_b_out = _loss_ln_bwd(target, r, ln_g[l], ln_b[l], name="loss_ln_bwd")
        else:
            dr, drb, d_ln_g, d_ln_b, d_b_out = _ln_bwd(dh, r, ln_g[l], name=f"ln_bwd{l}")
        dmix = _matmul(drb, w_out, "nt", name=f"dmix{l}", tm=1024, tn=1024, tk=2048, vmem_mb=56)
        d_w_out = _matmul(mix, drb, "tn", name=f"dw_out{l}", tm=1024, tn=1024, tk=2048, vmem_mb=56)
        d_mix, do, delta, d_w_pool, d_ps, d_conv = _mixer_bwd(dmix, proj, o, w_pool[l], pool_scale[l], conv_w[l],
                                                              name=f"mixer_bwd{l}")
        if above is not None:
            (dqb, dkvb, dkr), recv = _flash_bwd(qc, kc, v, do, lse2, delta, cos_t, sin_t, name=f"flash_bwd{l}",
                                                comm=scatter_above)
            sibling_above = above.sibling(recv)
        else:
            dqb, dkvb, dkr = _flash_bwd(qc, kc, v, do, lse2, delta, cos_t, sin_t, name=f"flash_bwd{l}")
        d_mla, d_qg, d_kvg, d_w_uq_t, d_w_ukv_t = _mla_qkv_bwd(
            dqb, dkvb, dkr, qn, kvn, proj, cos_t, sin_t, q_g[l], kv_g[l], w_uq_t, w_ukv_t, name=f"mla_qkv_bwd{l}")
        small[l] = dict(q_g=d_qg[0], kv_g=d_kvg[0], w_pool=d_w_pool, pool_scale=d_ps[0], conv_w=d_conv,
                        b_out=d_b_out[0], ln_g=d_ln_g[0], ln_b=d_ln_b[0])
        rest = (d_w_out, d_w_uq_t, d_w_ukv_t)
        if c_idx is None:
            d_w_in_t = _dproj_t_times_h(d_mla, d_mix, hb_in, name=f"dw_in{l}")
            dh = _dproj_times_w(d_mla, d_mix, w_in_t, dr, ALPHA, name=f"dh{l}")
            big[l] = (d_w_in_t,) + rest
        elif l > 0:
            d_w_in_t = _dproj_t_times_h(d_mla, d_mix, hb_in, name=f"dw_in{l}")
            above = _GradReducer(l, SHARDED_NAMES, (d_w_in_t,) + rest, c_idx)
            dh, theirs = _dproj_times_w(d_mla, d_mix, w_in_t, dr, ALPHA, name=f"dh{l}", comm=above.exchange())
            scatter_above = above.scatter(theirs)
        else:
            red_rest = _GradReducer(l, SHARDED_NAMES[1:], rest, c_idx)
            d_w_in_t, landed = _dproj_t_times_h(d_mla, d_mix, hb_in, name=f"dw_in{l}",
                                                comm=_merge_scripts(sibling_above, red_rest.exchange()))
            big[l + 1] = above.done(landed[:len(SHARDED)])
            red_in = _GradReducer(l, SHARDED_NAMES[:1], (d_w_in_t,), c_idx)
            landed = _run_comm(_merge_scripts(red_in.exchange(), red_rest.scatter(landed[len(SHARDED):])),
                               name="exchange_w_in0")
            sibling_rest = red_rest.sibling(landed[1:])
            dh, landed = _dproj_times_w(d_mla, d_mix, w_in_t, dr, ALPHA, name=f"dh{l}",
                                        comm=_merge_scripts(red_in.scatter(landed[:1]), sibling_rest))
            recv_in, others_rest = landed[:1], landed[1:]
    grad_x, _, d_emb_g, d_emb_b, _ = _ln_bwd(dh, x, emb_g, name="emb_ln_bwd", bf16_copy=False)
    if c_idx is not None:
        others_in = _run_comm(red_in.sibling(recv_in), name="send_to_sibling0")
        big[0] = red_in.done(others_in) + red_rest.done(others_rest)
    return loss_acc[0, 0], grad_x, d_emb_g, d_emb_b, small, big


SMALL_ORDER = ("emb_ln_g", "emb_ln_b", "q_norm_g", "kv_norm_g", "w_pool", "pool_scale", "b_out", "ln_g", "ln_b")
SMALL_LAYER_KEYS = ("q_g", "kv_g", "w_pool", "pool_scale", "b_out", "ln_g", "ln_b", "conv_w")


def _pack_small(arrs, extra_rows):
    flat = jnp.concatenate([a.reshape(-1) for a in arrs])
    rows = flat.shape[0] // LANE
    total = -(-(rows + extra_rows) // 8) * 8
    return jnp.pad(flat, (0, total * LANE - flat.shape[0])).reshape(total, LANE)


def kernel(x, positions, emb_ln_g, emb_ln_b, w_in, q_norm_g, kv_norm_g, w_uq, w_ukv, w_pool, pool_scale, conv_w, w_out, b_out, ln_g, ln_b, loss_target, m_emb_ln_g, m_emb_ln_b, m_w_in, m_q_norm_g, m_kv_norm_g, m_w_uq, m_w_ukv, m_w_pool, m_pool_scale, m_conv_w, m_w_out, m_b_out, m_ln_g, m_ln_b, v_emb_ln_g, v_emb_ln_b, v_w_in, v_q_norm_g, v_kv_norm_g, v_w_uq, v_w_ukv, v_w_pool, v_pool_scale, v_conv_w, v_w_out, v_b_out, v_ln_g, v_ln_b):
    xi, yi, ci = lax.axis_index("x"), lax.axis_index("y"), lax.axis_index("c")
    chip = 2 * xi + yi
    c_idx = ci.reshape(1).astype(jnp.int32)

    def t(a):
        return jnp.swapaxes(a, 1, 2)

    conv_bits = lax.bitcast_convert_type(conv_w.reshape(DEPTH, 3 * 128), BF16).reshape(DEPTH, 3, 256)
    conv_bits = jnp.pad(conv_bits, ((0, 0), (0, 13), (0, 0)))
    own = (t(w_in).astype(BF16), w_out.astype(BF16), t(w_uq).astype(BF16), t(w_ukv).astype(BF16))
    zeros = (jnp.zeros((GAP, D_MODEL), BF16), None, jnp.zeros((64, Q_LORA), BF16), None)
    gather_in0 = _allgather_script((W_IN,), own[:1], zeros[:1], (0,))
    gather0 = _allgather_script(SHARDED[1:] + (W_CONV,), own[1:] + (conv_bits,), zeros[1:] + (None,),
                                (0, 0, 0, None))
    gather1 = _allgather_script(SHARDED, own, zeros, (1, 1, 1, 1))

    loss_part, grad_x, d_emb_g, d_emb_b, grads, reduced = _local_step(
        x[0], positions[0], loss_target[0], emb_ln_g, emb_ln_b, gather_in0, gather0, gather1, q_norm_g, kv_norm_g,
        w_pool, pool_scale, b_out, ln_g, ln_b, c_idx)

    def rows(a):
        return a.reshape(1, -1) if a.ndim == 1 else a

    small_wmv = [tuple(rows(a) for a in wmv) for wmv in (
        (emb_ln_g, m_emb_ln_g, v_emb_ln_g), (emb_ln_b, m_emb_ln_b, v_emb_ln_b),
        (q_norm_g, m_q_norm_g, v_q_norm_g), (kv_norm_g, m_kv_norm_g, v_kv_norm_g), (w_pool, m_w_pool, v_w_pool),
        (pool_scale, m_pool_scale, v_pool_scale), (b_out, m_b_out, v_b_out), (ln_g, m_ln_g, v_ln_g),
        (ln_b, m_ln_b, v_ln_b))]
    packed_g = _pack_small(
        [d_emb_g, d_emb_b] + [jnp.stack([grads[l][key] for l in range(DEPTH)]) for key in SMALL_LAYER_KEYS]
        + [jnp.pad(loss_part.reshape(1), (0, LANE - 1))], 0)
    (gathered,) = _run_comm(_allgather_small_script(packed_g), name="allgather_small")
    g_tot, small_upd = _small_sum_adamw(gathered, packed_g, small_wmv, name="small_sum_adamw")
    off = sum(w.size for w, _, _ in small_wmv)
    flat_tot = g_tot.reshape(-1)

    def halves(a):
        return [reduced[l][a] for l in range(DEPTH)]

    upd = {}
    upd["w_in"] = tuple(t(o) for o in _adamw_halves(t(w_in), t(m_w_in), t(v_w_in), halves(0), c_idx,
                                                    name="adamw_w_in"))
    conv_tot = flat_tot[off:off + DEPTH * 3 * 512].reshape(DEPTH, 3, 512)
    loss = flat_tot[off + DEPTH * 3 * 512]
    g_conv = lax.dynamic_slice_in_dim(conv_tot, chip * 128, 128, axis=2)

    def whole(a):
        return jnp.stack([jnp.where(ci == 0, jnp.concatenate([mine, oth], axis=1),
                                    jnp.concatenate([oth, mine], axis=1)) for mine, oth in halves(a)])

    upd["w_out"] = _adamw_halves(w_out, m_w_out, v_w_out, halves(1), c_idx, name="adamw_w_out")
    g_uq, g_ukv = t(whole(2)), t(whole(3))
    upd["w_uq"] = (g_uq,) + _adamw(w_uq, g_uq, m_w_uq, v_w_uq, name="adamw_w_uq")
    upd["w_ukv"] = (g_ukv,) + _adamw(w_ukv, g_ukv, m_w_ukv, v_w_ukv, name="adamw_w_ukv")
    upd["conv_w"] = (g_conv,) + _adamw(conv_w, g_conv, m_conv_w, v_conv_w, name="adamw_conv_w")
    for nm, res in zip(SMALL_ORDER, small_upd):
        upd[nm] = tuple(a.reshape(-1) for a in res) if nm in ("emb_ln_g", "emb_ln_b") else res

    order = ("emb_ln_g", "emb_ln_b", "w_in", "q_norm_g", "kv_norm_g", "w_uq", "w_ukv", "w_pool", "pool_scale",
             "conv_w", "w_out", "b_out", "ln_g", "ln_b")
    outs = [loss, grad_x[None]]
    for field in range(4):
        outs += [upd[nm][field] for nm in order]
    return tuple(outs)
```
